```python
import jax, jax.numpy as jnp
from jax import lax
import numpy as np

D_MODEL = 1024
BATCH = 8
SEQ = 2048
DEPTH = 4

EXPAND = 2
E_INNER = EXPAND * D_MODEL
HEAD_DIM = 128
E_A = E_INNER // 2
E_B = E_INNER - E_A
H_A = E_A // HEAD_DIM
H_B = E_B // HEAD_DIM
CONV_WIDTH = 3
CHUNK = 128
AB_SPLITS = (E_A, E_A, E_A, E_A, E_B, E_B, E_B)
AB_IN = sum(AB_SPLITS)
E_C = E_INNER
POOL_WINDOWS = (2, 4, 8, 16)
N_POOL_GROUPS = len(POOL_WINDOWS)
G_C = E_C // N_POOL_GROUPS
N_EVEN = (DEPTH + 1) // 2
N_ODD = DEPTH // 2
EPS = 1e-6

kernel_name = "hybrid_shortconv_sgu_pool_adaln_trunk"


def rmsnorm(x, g):
    x32 = x.astype(jnp.float32)
    y = x32 * lax.rsqrt(jnp.mean(x32 * x32, axis=-1, keepdims=True) + EPS)
    return (y * g.astype(jnp.float32)).astype(x.dtype)


def modulate(h, shift, scale):
    return h * (1 + scale[:, None, :]) + shift[:, None, :]


def causal_short_conv(x, w):
    S = x.shape[1]
    xp = jnp.pad(x, ((0, 0), (CONV_WIDTH - 1, 0), (0, 0)))
    y = xp[:, 0:S] * w[0]
    for k in range(1, CONV_WIDTH):
        y = y + xp[:, k:k + S] * w[k]
    return y


def chunked_sgu(u, v, ln_g, ln_b, w_s, b_s):
    Bn, S, _ = v.shape
    n_chunks = S // CHUNK
    v32 = v.astype(jnp.float32).reshape(Bn, S, H_B, HEAD_DIM)
    mu = jnp.mean(v32, axis=-1, keepdims=True)
    var = jnp.mean(jnp.square(v32 - mu), axis=-1, keepdims=True)
    vn = ((v32 - mu) * lax.rsqrt(var + EPS)).reshape(Bn, S, E_B)
    vn = (vn * ln_g.astype(jnp.float32) + ln_b.astype(jnp.float32)).astype(v.dtype)
    vn = vn.reshape(Bn, n_chunks, CHUNK, H_B, HEAD_DIM)
    causal = jnp.tril(jnp.ones((CHUNK, CHUNK), dtype=bool))
    w_masked = jnp.where(causal[None], w_s, jnp.zeros_like(w_s))
    mixed = jnp.einsum('hts,bnshd->bnthd', w_masked, vn) + b_s.T[None, None, :, :, None]
    return u * mixed.reshape(Bn, S, E_B)


def multiscale_pool(p):
    S = p.shape[1]
    p32 = p.astype(jnp.float32)
    cs = jnp.cumsum(p32, axis=1)
    outs = []
    for gi, win in enumerate(POOL_WINDOWS):
        sl = slice(gi * G_C, (gi + 1) * G_C)
        csg = cs[..., sl]
        prev = jnp.pad(csg, ((0, 0), (win, 0), (0, 0)))[:, :S]
        cnt = jnp.minimum(jnp.arange(1, S + 1), win).astype(jnp.float32)[None, :, None]
        outs.append((csg - prev) / cnt - p32[..., sl])
    return jnp.stack(outs, axis=2).astype(p.dtype)


def even_mixer(h, w_in, conv_w, ln_g, ln_b, w_s, b_s, w_out):
    proj = h @ w_in
    idx = [int(i) for i in np.cumsum(AB_SPLITS)[:-1]]
    a_h, a_b, a_c, a_z, b_u, b_v, b_z = jnp.split(proj, idx, axis=-1)
    y_a = a_b * causal_short_conv(a_c * a_h, conv_w)
    y_a = y_a * jax.nn.silu(a_z)
    y_b = chunked_sgu(b_u, b_v, ln_g, ln_b, w_s, b_s)
    y_b = y_b * jax.nn.silu(b_z)
    return jnp.concatenate([y_a, y_b], axis=-1) @ w_out


def odd_mixer(h, w_in, pool_w, pool_scale, w_out):
    Bn, S, _ = h.shape
    proj = h @ w_in
    p, z = jnp.split(proj, 2, axis=-1)
    pooled = multiscale_pool(p)
    y = jnp.einsum('bsgi,gio->bsgo', pooled, pool_w).reshape(Bn, S, E_C)
    y = y * pool_scale * jax.nn.silu(z)
    return y @ w_out


def _fwd_setup_inputs(seed: int = 0) -> dict:
    key = jax.random.key(seed)
    ks = jax.random.split(key, 20)
    nrm = jax.random.normal
    f32 = jnp.float32
    return {
        "x": nrm(ks[0], (BATCH, SEQ, D_MODEL), f32),
        "c": nrm(ks[1], (BATCH, D_MODEL), f32),
        "norm_g": 1.0 + 0.1 * nrm(ks[2], (DEPTH, D_MODEL), f32),
        "ada_w": nrm(ks[3], (DEPTH, D_MODEL, 3 * D_MODEL), f32) * D_MODEL ** -0.5,
        "ada_b": 0.01 * nrm(ks[4], (DEPTH, 3 * D_MODEL), f32),
        "ab_w_in": nrm(ks[5], (N_EVEN, D_MODEL, AB_IN), f32) * D_MODEL ** -0.5,
        "ab_conv_w": nrm(ks[6], (N_EVEN, CONV_WIDTH, E_A), f32) * CONV_WIDTH ** -0.5,
        "ab_ln_g": 1.0 + 0.1 * nrm(ks[7], (N_EVEN, E_B), f32),
        "ab_ln_b": 0.02 * nrm(ks[8], (N_EVEN, E_B), f32),
        "ab_sgu_w": nrm(ks[9], (N_EVEN, H_B, CHUNK, CHUNK), f32) * CHUNK ** -0.5,
        "ab_sgu_b": 1.0 + 0.1 * nrm(ks[10], (N_EVEN, H_B, CHUNK), f32),
        "ab_w_out": nrm(ks[11], (N_EVEN, E_A + E_B, D_MODEL), f32) * (E_A + E_B) ** -0.5,
        "c_w_in": nrm(ks[12], (N_ODD, D_MODEL, 2 * E_C), f32) * D_MODEL ** -0.5,
        "c_pool_w": nrm(ks[13], (N_ODD, N_POOL_GROUPS, G_C, G_C), f32) * G_C ** -0.5,
        "c_pool_scale": 1.0 + 0.1 * nrm(ks[14], (N_ODD, E_C), f32),
        "c_w_out": nrm(ks[15], (N_ODD, E_C, D_MODEL), f32) * E_C ** -0.5,
        "final_g": 1.0 + 0.1 * nrm(ks[16], (D_MODEL,), f32),
    }


def _fwd_reference(x, c, norm_g, ada_w, ada_b, ab_w_in, ab_conv_w, ab_ln_g, ab_ln_b,
              ab_sgu_w, ab_sgu_b, ab_w_out, c_w_in, c_pool_w, c_pool_scale,
              c_w_out, final_g):
    c_act = jax.nn.silu(c)
    for i in range(DEPTH):
        mod = c_act @ ada_w[i] + ada_b[i]
        shift, scale, gate = jnp.split(mod, 3, axis=-1)
        h = modulate(rmsnorm(x, norm_g[i]), shift, scale)
        j = i // 2
        if i % 2 == 0:
            out = even_mixer(h, ab_w_in[j], ab_conv_w[j], ab_ln_g[j], ab_ln_b[j],
                             ab_sgu_w[j], ab_sgu_b[j], ab_w_out[j])
        else:
            out = odd_mixer(h, c_w_in[j], c_pool_w[j], c_pool_scale[j], c_w_out[j])
        x = x + gate[:, None, :] * out
    return rmsnorm(x, final_g)


import jax as _jax
import jax.numpy as _jnp

TWIN_FORMAT = 'train_step'
FWD_PARAMS = ['x', 'c', 'norm_g', 'ada_w', 'ada_b', 'ab_w_in', 'ab_conv_w', 'ab_ln_g', 'ab_ln_b', 'ab_sgu_w', 'ab_sgu_b', 'ab_w_out', 'c_w_in', 'c_pool_w', 'c_pool_scale', 'c_w_out', 'final_g']
TWIN_WEIGHTS = ['norm_g', 'ada_w', 'ada_b', 'ab_w_in', 'ab_conv_w', 'ab_ln_g', 'ab_ln_b', 'ab_sgu_w', 'ab_sgu_b', 'ab_w_out', 'c_w_in', 'c_pool_w', 'c_pool_scale', 'c_w_out', 'final_g']
TWIN_DIFF_INPUT = 'x'
TWIN_INPUTS = ['x', 'c', 'norm_g', 'ada_w', 'ada_b', 'ab_w_in', 'ab_conv_w', 'ab_ln_g', 'ab_ln_b', 'ab_sgu_w', 'ab_sgu_b', 'ab_w_out', 'c_w_in', 'c_pool_w', 'c_pool_scale', 'c_w_out', 'final_g', 'loss_target', 'm_norm_g', 'm_ada_w', 'm_ada_b', 'm_ab_w_in', 'm_ab_conv_w', 'm_ab_ln_g', 'm_ab_ln_b', 'm_ab_sgu_w', 'm_ab_sgu_b', 'm_ab_w_out', 'm_c_w_in', 'm_c_pool_w', 'm_c_pool_scale', 'm_c_w_out', 'm_final_g', 'v_norm_g', 'v_ada_w', 'v_ada_b', 'v_ab_w_in', 'v_ab_conv_w', 'v_ab_ln_g', 'v_ab_ln_b', 'v_ab_sgu_w', 'v_ab_sgu_b', 'v_ab_w_out', 'v_c_w_in', 'v_c_pool_w', 'v_c_pool_scale', 'v_c_w_out', 'v_final_g']
TWIN_OUTPUTS = ['loss', 'grad_x', 'grad_norm_g', 'grad_ada_w', 'grad_ada_b', 'grad_ab_w_in', 'grad_ab_conv_w', 'grad_ab_ln_g', 'grad_ab_ln_b', 'grad_ab_sgu_w', 'grad_ab_sgu_b', 'grad_ab_w_out', 'grad_c_w_in', 'grad_c_pool_w', 'grad_c_pool_scale', 'grad_c_w_out', 'grad_final_g', 'delta_norm_g', 'delta_ada_w', 'delta_ada_b', 'delta_ab_w_in', 'delta_ab_conv_w', 'delta_ab_ln_g', 'delta_ab_ln_b', 'delta_ab_sgu_w', 'delta_ab_sgu_b', 'delta_ab_w_out', 'delta_c_w_in', 'delta_c_pool_w', 'delta_c_pool_scale', 'delta_c_w_out', 'delta_final_g', 'new_m_norm_g', 'new_m_ada_w', 'new_m_ada_b', 'new_m_ab_w_in', 'new_m_ab_conv_w', 'new_m_ab_ln_g', 'new_m_ab_ln_b', 'new_m_ab_sgu_w', 'new_m_ab_sgu_b', 'new_m_ab_w_out', 'new_m_c_w_in', 'new_m_c_pool_w', 'new_m_c_pool_scale', 'new_m_c_w_out', 'new_m_final_g', 'new_v_norm_g', 'new_v_ada_w', 'new_v_ada_b', 'new_v_ab_w_in', 'new_v_ab_conv_w', 'new_v_ab_ln_g', 'new_v_ab_ln_b', 'new_v_ab_sgu_w', 'new_v_ab_sgu_b', 'new_v_ab_w_out', 'new_v_c_w_in', 'new_v_c_pool_w', 'new_v_c_pool_scale', 'new_v_c_w_out', 'new_v_final_g']
TWIN_LEAF_KINDS = {'loss': 'loss', 'grad_x': 'grad_x', 'grad_norm_g': 'grad_w', 'grad_ada_w': 'grad_w', 'grad_ada_b': 'grad_w', 'grad_ab_w_in': 'grad_w', 'grad_ab_conv_w': 'grad_w', 'grad_ab_ln_g': 'grad_w', 'grad_ab_ln_b': 'grad_w', 'grad_ab_sgu_w': 'grad_w', 'grad_ab_sgu_b': 'grad_w', 'grad_ab_w_out': 'grad_w', 'grad_c_w_in': 'grad_w', 'grad_c_pool_w': 'grad_w', 'grad_c_pool_scale': 'grad_w', 'grad_c_w_out': 'grad_w', 'grad_final_g': 'grad_w', 'delta_norm_g': 'delta_w', 'delta_ada_w': 'delta_w', 'delta_ada_b': 'delta_w', 'delta_ab_w_in': 'delta_w', 'delta_ab_conv_w': 'delta_w', 'delta_ab_ln_g': 'delta_w', 'delta_ab_ln_b': 'delta_w', 'delta_ab_sgu_w': 'delta_w', 'delta_ab_sgu_b': 'delta_w', 'delta_ab_w_out': 'delta_w', 'delta_c_w_in': 'delta_w', 'delta_c_pool_w': 'delta_w', 'delta_c_pool_scale': 'delta_w', 'delta_c_w_out': 'delta_w', 'delta_final_g': 'delta_w', 'new_m_norm_g': 'new_m', 'new_m_ada_w': 'new_m', 'new_m_ada_b': 'new_m', 'new_m_ab_w_in': 'new_m', 'new_m_ab_conv_w': 'new_m', 'new_m_ab_ln_g': 'new_m', 'new_m_ab_ln_b': 'new_m', 'new_m_ab_sgu_w': 'new_m', 'new_m_ab_sgu_b': 'new_m', 'new_m_ab_w_out': 'new_m', 'new_m_c_w_in': 'new_m', 'new_m_c_pool_w': 'new_m', 'new_m_c_pool_scale': 'new_m', 'new_m_c_w_out': 'new_m', 'new_m_final_g': 'new_m', 'new_v_norm_g': 'new_v', 'new_v_ada_w': 'new_v', 'new_v_ada_b': 'new_v', 'new_v_ab_w_in': 'new_v', 'new_v_ab_conv_w': 'new_v', 'new_v_ab_ln_g': 'new_v', 'new_v_ab_ln_b': 'new_v', 'new_v_ab_sgu_w': 'new_v', 'new_v_ab_sgu_b': 'new_v', 'new_v_ab_w_out': 'new_v', 'new_v_c_w_in': 'new_v', 'new_v_c_pool_w': 'new_v', 'new_v_c_pool_scale': 'new_v', 'new_v_c_w_out': 'new_v', 'new_v_final_g': 'new_v'}


def _forward(args):
    return _fwd_reference(*[args[k] for k in FWD_PARAMS])


def _output_shape():
    out = _jax.eval_shape(lambda: _forward(_fwd_setup_inputs(0)))
    return out.shape, out.dtype

N_MICROBATCH = 1
ADAM_LR = 0.001
ADAM_B1 = 0.9
ADAM_B2 = 0.999
ADAM_EPS = 1e-08
ADAM_WD = 0.01
ADAM_STEP = 10
PER_EXAMPLE_BATCH_AXIS = {'x': 0, 'c': 0, 'loss_target': 0}
SHARED_INPUTS = []
_WEIGHT_DTYPES = {'norm_g': _jnp.float32, 'ada_w': _jnp.float32, 'ada_b': _jnp.float32, 'ab_w_in': _jnp.float32, 'ab_conv_w': _jnp.float32, 'ab_ln_g': _jnp.float32, 'ab_ln_b': _jnp.float32, 'ab_sgu_w': _jnp.float32, 'ab_sgu_b': _jnp.float32, 'ab_w_out': _jnp.float32, 'c_w_in': _jnp.float32, 'c_pool_w': _jnp.float32, 'c_pool_scale': _jnp.float32, 'c_w_out': _jnp.float32, 'final_g': _jnp.float32}
MOMENT_SCALE = {'norm_g': 1.318681e-01, 'ada_w': 1.713417e-01, 'ada_b': 3.076071e-01, 'ab_w_in': 7.428219e-02, 'ab_conv_w': 8.461273e-02, 'ab_ln_g': 3.387663e-02, 'ab_ln_b': 3.426154e-02, 'ab_sgu_w': 3.391855e-02, 'ab_sgu_b': 4.873149e-02, 'ab_w_out': 1.071268e-01, 'c_w_in': 3.485294e-02, 'c_pool_w': 3.435359e-02, 'c_pool_scale': 3.448956e-02, 'c_w_out': 4.877239e-02, 'final_g': 1.675542e+01}


def _to_microbatches(a, axis):
    t = _jnp.moveaxis(a, axis, 0)
    t = t.reshape((N_MICROBATCH, t.shape[0] // N_MICROBATCH) + t.shape[1:])
    return _jnp.moveaxis(t, 1, axis + 1)


def setup_inputs(seed: int = 0) -> dict:
    inp = _fwd_setup_inputs(seed)
    key = _jax.random.fold_in(_jax.random.key(seed), 7919)
    shape, _ = _output_shape()
    out = dict(inp)
    out["loss_target"] = _jax.random.normal(_jax.random.fold_in(key, 0), shape, _jnp.float32)
    for i, name in enumerate(TWIN_WEIGHTS):
        w = inp[name].astype(_jnp.float32)
        if MOMENT_SCALE is None:
            s = _jnp.sqrt(_jnp.mean(_jnp.square(w)) + 1e-30)
        else:
            s = MOMENT_SCALE[name]
        km, kv = _jax.random.split(_jax.random.fold_in(key, i + 1))
        out[name] = w
        out["m_" + name] = s * _jax.random.normal(km, w.shape, _jnp.float32)
        out["v_" + name] = (s * s) * _jax.random.uniform(kv, w.shape, _jnp.float32, 0.5, 1.5)
    if N_MICROBATCH > 1:
        for name, axis in PER_EXAMPLE_BATCH_AXIS.items():
            out[name] = _to_microbatches(out[name], axis)
    return {'x': out['x'], 'c': out['c'], 'norm_g': out['norm_g'], 'ada_w': out['ada_w'], 'ada_b': out['ada_b'], 'ab_w_in': out['ab_w_in'], 'ab_conv_w': out['ab_conv_w'], 'ab_ln_g': out['ab_ln_g'], 'ab_ln_b': out['ab_ln_b'], 'ab_sgu_w': out['ab_sgu_w'], 'ab_sgu_b': out['ab_sgu_b'], 'ab_w_out': out['ab_w_out'], 'c_w_in': out['c_w_in'], 'c_pool_w': out['c_pool_w'], 'c_pool_scale': out['c_pool_scale'], 'c_w_out': out['c_w_out'], 'final_g': out['final_g'], 'loss_target': out['loss_target'], 'm_norm_g': out['m_norm_g'], 'm_ada_w': out['m_ada_w'], 'm_ada_b': out['m_ada_b'], 'm_ab_w_in': out['m_ab_w_in'], 'm_ab_conv_w': out['m_ab_conv_w'], 'm_ab_ln_g': out['m_ab_ln_g'], 'm_ab_ln_b': out['m_ab_ln_b'], 'm_ab_sgu_w': out['m_ab_sgu_w'], 'm_ab_sgu_b': out['m_ab_sgu_b'], 'm_ab_w_out': out['m_ab_w_out'], 'm_c_w_in': out['m_c_w_in'], 'm_c_pool_w': out['m_c_pool_w'], 'm_c_pool_scale': out['m_c_pool_scale'], 'm_c_w_out': out['m_c_w_out'], 'm_final_g': out['m_final_g'], 'v_norm_g': out['v_norm_g'], 'v_ada_w': out['v_ada_w'], 'v_ada_b': out['v_ada_b'], 'v_ab_w_in': out['v_ab_w_in'], 'v_ab_conv_w': out['v_ab_conv_w'], 'v_ab_ln_g': out['v_ab_ln_g'], 'v_ab_ln_b': out['v_ab_ln_b'], 'v_ab_sgu_w': out['v_ab_sgu_w'], 'v_ab_sgu_b': out['v_ab_sgu_b'], 'v_ab_w_out': out['v_ab_w_out'], 'v_c_w_in': out['v_c_w_in'], 'v_c_pool_w': out['v_c_pool_w'], 'v_c_pool_scale': out['v_c_pool_scale'], 'v_c_w_out': out['v_c_w_out'], 'v_final_g': out['v_final_g']}


def _loss(weights, diff, rest, loss_target):
    with _jax.named_scope("forward"):
        args = {**rest, TWIN_DIFF_INPUT: diff, **{k: w.astype(_WEIGHT_DTYPES[k]) for k, w in weights.items()}}
        y = _forward(args)
    with _jax.named_scope("loss_head"):
        err = _jnp.square(y.astype(_jnp.float32) - loss_target)
        return 0.5 * _jnp.sum(_jnp.mean(err, axis=-1)) if err.ndim else 0.5 * err


def _adamw(w, g, m, v):
    m = ADAM_B1 * m + (1.0 - ADAM_B1) * g
    v = ADAM_B2 * v + (1.0 - ADAM_B2) * _jnp.square(g)
    m_hat = m / (1.0 - ADAM_B1 ** ADAM_STEP)
    v_hat = v / (1.0 - ADAM_B2 ** ADAM_STEP)
    delta = -ADAM_LR * (m_hat / (_jnp.sqrt(v_hat) + ADAM_EPS) + ADAM_WD * w)
    return delta, m, v


def reference(x, c, norm_g, ada_w, ada_b, ab_w_in, ab_conv_w, ab_ln_g, ab_ln_b, ab_sgu_w, ab_sgu_b, ab_w_out, c_w_in, c_pool_w, c_pool_scale, c_w_out, final_g, loss_target, m_norm_g, m_ada_w, m_ada_b, m_ab_w_in, m_ab_conv_w, m_ab_ln_g, m_ab_ln_b, m_ab_sgu_w, m_ab_sgu_b, m_ab_w_out, m_c_w_in, m_c_pool_w, m_c_pool_scale, m_c_w_out, m_final_g, v_norm_g, v_ada_w, v_ada_b, v_ab_w_in, v_ab_conv_w, v_ab_ln_g, v_ab_ln_b, v_ab_sgu_w, v_ab_sgu_b, v_ab_w_out, v_c_w_in, v_c_pool_w, v_c_pool_scale, v_c_w_out, v_final_g):
    given = dict(x=x, c=c, norm_g=norm_g, ada_w=ada_w, ada_b=ada_b, ab_w_in=ab_w_in, ab_conv_w=ab_conv_w, ab_ln_g=ab_ln_g, ab_ln_b=ab_ln_b, ab_sgu_w=ab_sgu_w, ab_sgu_b=ab_sgu_b, ab_w_out=ab_w_out, c_w_in=c_w_in, c_pool_w=c_pool_w, c_pool_scale=c_pool_scale, c_w_out=c_w_out, final_g=final_g, loss_target=loss_target, m_norm_g=m_norm_g, m_ada_w=m_ada_w, m_ada_b=m_ada_b, m_ab_w_in=m_ab_w_in, m_ab_conv_w=m_ab_conv_w, m_ab_ln_g=m_ab_ln_g, m_ab_ln_b=m_ab_ln_b, m_ab_sgu_w=m_ab_sgu_w, m_ab_sgu_b=m_ab_sgu_b, m_ab_w_out=m_ab_w_out, m_c_w_in=m_c_w_in, m_c_pool_w=m_c_pool_w, m_c_pool_scale=m_c_pool_scale, m_c_w_out=m_c_w_out, m_final_g=m_final_g, v_norm_g=v_norm_g, v_ada_w=v_ada_w, v_ada_b=v_ada_b, v_ab_w_in=v_ab_w_in, v_ab_conv_w=v_ab_conv_w, v_ab_ln_g=v_ab_ln_g, v_ab_ln_b=v_ab_ln_b, v_ab_sgu_w=v_ab_sgu_w, v_ab_sgu_b=v_ab_sgu_b, v_ab_w_out=v_ab_w_out, v_c_w_in=v_c_w_in, v_c_pool_w=v_c_pool_w, v_c_pool_scale=v_c_pool_scale, v_c_w_out=v_c_w_out, v_final_g=v_final_g)
    weights = {n: given[n] for n in TWIN_WEIGHTS}
    shared = {n: given[n] for n in SHARED_INPUTS}
    per_example = {n: given[n] for n in ['x', 'c']}
    grad_fn = _jax.value_and_grad(_loss, argnums=(0, 1))

    def one_microbatch(ex, loss_target):
        ex = dict(ex)
        diff = ex.pop(TWIN_DIFF_INPUT)
        return grad_fn(weights, diff, {**shared, **ex}, loss_target)

    if N_MICROBATCH == 1:
        loss, (grad_w, grad_x) = one_microbatch(per_example, given["loss_target"])
    else:
        def body(carry, xs):
            loss_sum, grad_sum = carry
            l_k, (gw_k, gx_k) = one_microbatch(xs[0], xs[1])
            with _jax.named_scope("update"):
                return (loss_sum + l_k, _jax.tree.map(_jnp.add, grad_sum, gw_k)), gx_k

        init = (_jnp.zeros((), _jnp.float32), _jax.tree.map(_jnp.zeros_like, weights))
        (loss, grad_w), grad_x = _jax.lax.scan(body, init, (per_example, given["loss_target"]))
    with _jax.named_scope("update"):
        delta_w, new_m, new_v = {}, {}, {}
        for n in TWIN_WEIGHTS:
            delta_w[n], new_m[n], new_v[n] = _adamw(weights[n], grad_w[n], given["m_" + n], given["v_" + n])
    return (loss, grad_x, *[grad_w[n] for n in TWIN_WEIGHTS], *[delta_w[n] for n in TWIN_WEIGHTS],
            *[new_m[n] for n in TWIN_WEIGHTS], *[new_v[n] for n in TWIN_WEIGHTS])
```

```python
import functools

import jax
import jax.numpy as jnp
from jax import lax
from jax.experimental import pallas as pl
from jax.experimental.pallas import tpu as pltpu

f32, bf16 = jnp.float32, jnp.bfloat16

D = 1024
HEAD = 128
NH = 8
WINDOWS = (2, 4, 8, 16)
GC = 512
EPS = 1e-6
HALO_CONV = 8
HALO_POOL = 16
CHUNK_ROWS = 256
N_DEV = 8
LANES = 128

ADAM_LR, ADAM_B1, ADAM_B2, ADAM_EPS, ADAM_WD, ADAM_STEP = 0.001, 0.9, 0.999, 1e-08, 0.01, 10

MESH = pl.DeviceIdType.MESH
ANY = pl.BlockSpec(memory_space=pl.ANY)
VMEM = pl.BlockSpec(memory_space=pltpu.VMEM)
MIB = 2 ** 20


def _pcall(body, *, name, out_shape, grid=None, in_specs=None, out_specs=None, scratch=(), vmem_mb=None):
    kw = {}
    if grid is not None:
        kw["grid"] = grid
    if in_specs is not None:
        kw["in_specs"] = in_specs
    if out_specs is not None:
        kw["out_specs"] = out_specs
    if scratch:
        kw["scratch_shapes"] = list(scratch)
    params = pltpu.CompilerParams(vmem_limit_bytes=None if vmem_mb is None else vmem_mb * MIB)
    return pl.pallas_call(body, name=name, out_shape=out_shape, compiler_params=params, **kw)


def _sds(shape, dtype):
    return jax.ShapeDtypeStruct(tuple(shape), dtype)


def _silu(z):
    return z * jax.nn.sigmoid(z)


def _silu_and_grad(z):
    s = jax.nn.sigmoid(z)
    return z * s, s * (1.0 + z * (1.0 - s))


def _place():
    return lax.axis_index("x"), lax.axis_index("y"), lax.axis_index("c")


def _gather8(blk, name):
    def body(x_ref, o_ref, ssem, rsem):
        x, y, c = _place()
        me = 4 * x + 2 * y + c
        o_ref[me] = x_ref[...]
        sends = []
        for k in range(1, N_DEV):
            px = 1 - x if k & 4 else x
            py = 1 - y if k & 2 else y
            pc = 1 - c if k & 1 else c
            cp = pltpu.make_async_remote_copy(src_ref=x_ref, dst_ref=o_ref.at[me], send_sem=ssem.at[k - 1],
                                              recv_sem=rsem.at[k - 1], device_id=(px, py, pc), device_id_type=MESH)
            cp.start()
            sends.append((cp, 4 * px + 2 * py + pc))
        for k, (cp, peer) in enumerate(sends):
            pltpu.make_async_remote_copy(src_ref=x_ref, dst_ref=o_ref.at[peer], send_sem=ssem.at[k],
                                         recv_sem=rsem.at[k], device_id=(x, y, c), device_id_type=MESH).wait_recv()
        for cp, _ in sends:
            cp.wait_send()

    return _pcall(body, name=name, out_shape=_sds((N_DEV,) + blk.shape, blk.dtype), in_specs=[VMEM], out_specs=VMEM,
                  scratch=[pltpu.SemaphoreType.DMA((N_DEV - 1,)), pltpu.SemaphoreType.DMA((N_DEV - 1,))])(blk)


def _allreduce8(buf, name):
    rows = buf.shape[0]
    rb = rows // N_DEV
    assert rb * N_DEV == rows and rb % 8 == 0

    def body(x_ref, o_ref, stage, ssem, rsem):
        x, y, c = _place()
        me = 4 * x + 2 * y + c
        peers = []
        for k in range(1, N_DEV):
            px = 1 - x if k & 4 else x
            py = 1 - y if k & 2 else y
            pc = 1 - c if k & 1 else c
            peers.append(((px, py, pc), 4 * px + 2 * py + pc))

        def blk(ref, idx):
            return ref.at[pl.ds(pl.multiple_of(idx * rb, 8), rb), :]

        def copy(phase, k, src, dst, dev):
            return pltpu.make_async_remote_copy(src_ref=src, dst_ref=dst, send_sem=ssem.at[phase, k],
                                                recv_sem=rsem.at[phase, k], device_id=dev, device_id_type=MESH)

        stage[me] = blk(x_ref, me)[...]
        scatter = [copy(0, k, blk(x_ref, pidx), stage.at[me], dev) for k, (dev, pidx) in enumerate(peers)]
        for cp in scatter:
            cp.start()
        for k, (dev, pidx) in enumerate(peers):
            copy(0, k, blk(x_ref, pidx), stage.at[pidx], dev).wait_recv()
        total = stage[0]
        for j in range(1, N_DEV):
            total = total + stage[j]
        blk(o_ref, me)[...] = total
        gather = [copy(1, k, blk(o_ref, me), blk(o_ref, me), dev) for k, (dev, pidx) in enumerate(peers)]
        for cp in gather:
            cp.start()
        for k, (dev, pidx) in enumerate(peers):
            copy(1, k, blk(o_ref, pidx), blk(o_ref, pidx), dev).wait_recv()
        for cp in scatter + gather:
            cp.wait_send()

    return _pcall(body, name=name, out_shape=_sds(buf.shape, f32), in_specs=[VMEM], out_specs=VMEM,
                  scratch=[pltpu.VMEM((N_DEV, rb, LANES), f32), pltpu.SemaphoreType.DMA((2, N_DEV - 1)),
                           pltpu.SemaphoreType.DMA((2, N_DEV - 1))])(buf)


def _other_chips(x, y):
    return [((1 - x, y), 2 * (1 - x) + y), ((x, 1 - y), 2 * x + (1 - y)), ((1 - x, 1 - y), 2 * (1 - x) + (1 - y))]


def _allgather_weights(halves, name):
    n = len(halves)

    def body(*refs):
        w, o = refs[:n], refs[n:2 * n]
        ssem, rsem, lsem = refs[2 * n:]
        x, y, c = _place()
        s_me = 2 * x + y
        chips = _other_chips(x, y)

        def copy(t, k, s, half, src, dev):
            return pltpu.make_async_remote_copy(src_ref=src, dst_ref=o[t].at[s, half], send_sem=ssem.at[t, k],
                                                recv_sem=rsem.at[t, k], device_id=dev, device_id_type=MESH)

        local = [pltpu.make_async_copy(w[t], o[t].at[s_me], lsem.at[t]) for t in range(n)]
        for cp in local:
            cp.start()
        sends = []
        for t in range(n):
            for j, ((px, py), _) in enumerate(chips):
                sends.append(copy(t, j, s_me, c, w[t].at[c], (px, py, c)))
                sends[-1].start()
        for j, (_, s_p) in enumerate(chips):
            for t in range(n):
                copy(t, j, s_p, c, w[t].at[c], (x, y, c)).wait_recv()
                sends.append(copy(t, 3 + j, s_p, c, o[t].at[s_p, c], (x, y, 1 - c)))
                sends[-1].start()
        for j, (_, s_p) in enumerate(chips):
            for t in range(n):
                copy(t, 3 + j, s_p, 1 - c, w[t].at[c], (x, y, c)).wait_recv()
        for cp in sends:
            cp.wait_send()
        for cp in local:
            cp.wait()

    out_shape = [_sds((4,) + h.shape, bf16) for h in halves]
    return _pcall(body, name=name, out_shape=out_shape, in_specs=[ANY] * n, out_specs=[ANY] * n,
                  scratch=[pltpu.SemaphoreType.DMA((n, 6)), pltpu.SemaphoreType.DMA((n, 6)),
                           pltpu.SemaphoreType.DMA((n,))])(*halves)


def _rs_pair_exchange(grads, name):
    n = len(grads)

    def body(*refs):
        g, mine, theirs = refs[:n], refs[n:2 * n], refs[2 * n:3 * n]
        ssem, rsem, lsem = refs[3 * n:]
        x, y, c = _place()
        local = [pltpu.make_async_copy(g[t].at[:, c], mine[t], lsem.at[t]) for t in range(n)]
        sends = [pltpu.make_async_remote_copy(src_ref=g[t].at[:, 1 - c], dst_ref=theirs[t], send_sem=ssem.at[t],
                                              recv_sem=rsem.at[t], device_id=(x, y, 1 - c), device_id_type=MESH)
                 for t in range(n)]
        for cp in local + sends:
            cp.start()
        for cp in sends:
            cp.wait()
        for cp in local:
            cp.wait()

    half = [_sds((4,) + gr.shape[2:], bf16) for gr in grads]
    outs = _pcall(body, name=name, out_shape=half + half, in_specs=[ANY] * n, out_specs=[ANY] * (2 * n),
                  scratch=[pltpu.SemaphoreType.DMA((n,)), pltpu.SemaphoreType.DMA((n,)),
                           pltpu.SemaphoreType.DMA((n,))])(*grads)
    return outs[:n], outs[n:]


def _rs_chip_exchange(pairs, name):
    n = len(pairs)

    def body(*refs):
        p, q = refs[:n], refs[n:2 * n]
        ssem, rsem, lsem = refs[2 * n:]
        x, y, c = _place()
        s_me = 2 * x + y
        chips = _other_chips(x, y)
        local = [pltpu.make_async_copy(p[t].at[s_me], q[t].at[s_me], lsem.at[t]) for t in range(n)]
        for cp in local:
            cp.start()
        sends = []
        for t in range(n):
            for j, ((px, py), s_p) in enumerate(chips):
                sends.append(pltpu.make_async_remote_copy(src_ref=p[t].at[s_p], dst_ref=q[t].at[s_me],
                                                          send_sem=ssem.at[t, j], recv_sem=rsem.at[t, j],
                                                          device_id=(px, py, c), device_id_type=MESH))
                sends[-1].start()
        for t in range(n):
            for j, (_, s_p) in enumerate(chips):
                pltpu.make_async_remote_copy(src_ref=p[t].at[s_p], dst_ref=q[t].at[s_p], send_sem=ssem.at[t, j],
                                             recv_sem=rsem.at[t, j], device_id=(x, y, c), device_id_type=MESH).wait_recv()
        for cp in sends:
            cp.wait_send()
        for cp in local:
            cp.wait()

    return _pcall(body, name=name, out_shape=[_sds(p.shape, bf16) for p in pairs], in_specs=[ANY] * n,
                  out_specs=[ANY] * n,
                  scratch=[pltpu.SemaphoreType.DMA((n, 3)), pltpu.SemaphoreType.DMA((n, 3)),
                           pltpu.SemaphoreType.DMA((n,))])(*pairs)


def _rs_half_exchange(groups, name):
    flat = [h for grp in groups for h in grp]
    where = [(k, l) for k, grp in enumerate(groups) for l in range(len(grp))]
    n, ng = len(flat), len(groups)

    def body(*refs):
        h, o = refs[:n], refs[n:n + ng]
        ssem, rsem, lsem = refs[n + ng:]
        x, y, c = _place()
        local = [pltpu.make_async_copy(h[t], o[k].at[l, c], lsem.at[t]) for t, (k, l) in enumerate(where)]
        sends = [pltpu.make_async_remote_copy(src_ref=h[t], dst_ref=o[k].at[l, c], send_sem=ssem.at[t],
                                              recv_sem=rsem.at[t], device_id=(x, y, 1 - c), device_id_type=MESH)
                 for t, (k, l) in enumerate(where)]
        for cp in local + sends:
            cp.start()
        for t, (k, l) in enumerate(where):
            pltpu.make_async_remote_copy(src_ref=h[t], dst_ref=o[k].at[l, 1 - c], send_sem=ssem.at[t],
                                         recv_sem=rsem.at[t], device_id=(x, y, c), device_id_type=MESH).wait_recv()
        for cp in sends:
            cp.wait_send()
        for cp in local:
            cp.wait()

    out_shape = [_sds((len(grp), 2) + grp[0].shape, f32) for grp in groups]
    return _pcall(body, name=name, out_shape=out_shape, in_specs=[ANY] * n, out_specs=[ANY] * ng,
                  scratch=[pltpu.SemaphoreType.DMA((n,)), pltpu.SemaphoreType.DMA((n,)),
                           pltpu.SemaphoreType.DMA((n,))])(*flat)


def _row_spec(tm, cols):
    return pl.BlockSpec((tm, cols), lambda i: (i, 0))


def _vec_spec(cols, rows=1):
    return pl.BlockSpec((rows, cols), lambda i: (0, 0))


def _hnorm(x, g, shift, scale):
    T, tm = x.shape[0], 256

    def body(x_ref, g_ref, sh_ref, sc_ref, h_ref):
        xv = x_ref[...]
        r = lax.rsqrt(jnp.mean(xv * xv, axis=-1, keepdims=True) + EPS)
        a = (xv * r) * g_ref[...]
        h_ref[...] = (a * (1.0 + sc_ref[...]) + sh_ref[...]).astype(bf16)

    return _pcall(body, name="hnorm", out_shape=_sds((T, D), bf16), grid=(T // tm,),
                  in_specs=[_row_spec(tm, D), _vec_spec(D), _vec_spec(D), _vec_spec(D)],
                  out_specs=_row_spec(tm, D))(x, g, shift, scale)


def _out_proj(y2, wo, x, gate):
    T, tm = x.shape[0], 512

    def body(y_ref, w_ref, x_ref, g_ref, xo_ref, o_ref):
        o = jnp.dot(y_ref[0], w_ref[0], preferred_element_type=f32)
        o = o + jnp.dot(y_ref[1], w_ref[1], preferred_element_type=f32)
        o_ref[...] = o
        xo_ref[...] = x_ref[...] + g_ref[...] * o

    return _pcall(body, name="out_proj", out_shape=[_sds((T, D), f32), _sds((T, D), f32)], grid=(T // tm,),
                  in_specs=[pl.BlockSpec((2, tm, D), lambda i: (0, i, 0)), pl.BlockSpec((2, D, D), lambda i: (0, 0, 0)),
                            _row_spec(tm, D), _vec_spec(D)],
                  out_specs=[_row_spec(tm, D), _row_spec(tm, D)], vmem_mb=40)(y2, wo, x, gate)


def _loss_bwd(x, target, g):
    T, tm = x.shape[0], 256

    def body(x_ref, t_ref, g_ref, dx_ref, loss_ref, dg_ref):
        @pl.when(pl.program_id(0) == 0)
        def _():
            loss_ref[...] = jnp.zeros_like(loss_ref)
            dg_ref[...] = jnp.zeros_like(dg_ref)

        xv, gv = x_ref[...], g_ref[...]
        r = lax.rsqrt(jnp.mean(xv * xv, axis=-1, keepdims=True) + EPS)
        xn = xv * r
        err = xn * gv - t_ref[...]
        dy = err * (1.0 / D)
        dxn = dy * gv
        dx_ref[...] = r * (dxn - xn * jnp.mean(dxn * xn, axis=-1, keepdims=True))
        dg_ref[...] += jnp.sum(dy * xn, axis=0, keepdims=True)
        loss_ref[...] += (0.5 / D) * jnp.sum(jnp.sum(err * err, axis=1, keepdims=True), axis=0, keepdims=True)

    return _pcall(body, name="loss_bwd", out_shape=[_sds((T, D), f32), _sds((1, 1), f32), _sds((1, D), f32)],
                  grid=(T // tm,), in_specs=[_row_spec(tm, D), _row_spec(tm, D), _vec_spec(D)],
                  out_specs=[_row_spec(tm, D), pl.BlockSpec((1, 1), lambda i: (0, 0)), _vec_spec(D)])(x, target, g)


def _gate_bwd(gin, o, gate):
    T, tm = gin.shape[0], 512

    def body(gin_ref, o_ref, gate_ref, dob_ref, dgate_ref):
        @pl.when(pl.program_id(0) == 0)
        def _():
            dgate_ref[...] = jnp.zeros_like(dgate_ref)

        gv = gin_ref[...]
        dob_ref[...] = (gv * gate_ref[...]).astype(bf16)
        dgate_ref[...] += jnp.sum(gv * o_ref[...], axis=0, keepdims=True)

    return _pcall(body, name="gate_bwd", out_shape=[_sds((T, D), bf16), _sds((1, D), f32)], grid=(T // tm,),
                  in_specs=[_row_spec(tm, D), _row_spec(tm, D), _vec_spec(D)],
                  out_specs=[_row_spec(tm, D), _vec_spec(D)])(gin, o, gate)


def _norm_bwd(x, dh, gin, g, scale):
    T, tm = x.shape[0], 256

    def body(x_ref, dh_ref, gin_ref, g_ref, sc_ref, dx_ref, st_ref):
        @pl.when(pl.program_id(0) == 0)
        def _():
            st_ref[...] = jnp.zeros_like(st_ref)

        xv, gv, dhv = x_ref[...], g_ref[...], dh_ref[...]
        r = lax.rsqrt(jnp.mean(xv * xv, axis=-1, keepdims=True) + EPS)
        xn = xv * r
        da = dhv * (1.0 + sc_ref[...])
        dxn = da * gv
        dx_ref[...] = gin_ref[...] + r * (dxn - xn * jnp.mean(dxn * xn, axis=-1, keepdims=True))
        st_ref[0:1, :] += jnp.sum(dhv, axis=0, keepdims=True)
        st_ref[1:2, :] += jnp.sum(dhv * (xn * gv), axis=0, keepdims=True)
        st_ref[2:3, :] += jnp.sum(da * xn, axis=0, keepdims=True)

    return _pcall(body, name="norm_bwd", out_shape=[_sds((T, D), f32), _sds((8, D), f32)], grid=(T // tm,),
                  in_specs=[_row_spec(tm, D), _row_spec(tm, D), _row_spec(tm, D), _vec_spec(D), _vec_spec(D)],
                  out_specs=[_row_spec(tm, D), _vec_spec(D, 8)])(x, dh, gin, g, scale)


def _rs_add(mine, theirs):
    _, rows, cols = mine.shape
    tr = 256
    spec = pl.BlockSpec((None, tr, cols), lambda s, i: (s, i, 0))

    def body(a_ref, b_ref, o_ref):
        o_ref[...] = (a_ref[...].astype(f32) + b_ref[...].astype(f32)).astype(bf16)

    return _pcall(body, name="rs_add", out_shape=_sds(mine.shape, bf16), grid=(4, rows // tr),
                  in_specs=[spec, spec], out_specs=spec)(mine, theirs)


def _rs_sum(q):
    _, rows, cols = q.shape
    tr = 256

    def body(q_ref, o_ref):
        o_ref[...] = ((q_ref[0].astype(f32) + q_ref[1].astype(f32)) + q_ref[2].astype(f32)) + q_ref[3].astype(f32)

    return _pcall(body, name="rs_sum", out_shape=_sds((rows, cols), f32), grid=(rows // tr,),
                  in_specs=[pl.BlockSpec((4, tr, cols), lambda i: (0, i, 0))],
                  out_specs=pl.BlockSpec((tr, cols), lambda i: (i, 0)))(q)


def _adamw_math(w, g, m, v):
    m = ADAM_B1 * m + (1.0 - ADAM_B1) * g
    v = ADAM_B2 * v + (1.0 - ADAM_B2) * jnp.square(g)
    m_hat = m / (1.0 - ADAM_B1 ** ADAM_STEP)
    v_hat = v / (1.0 - ADAM_B2 ** ADAM_STEP)
    delta = -ADAM_LR * (m_hat / (jnp.sqrt(v_hat) + ADAM_EPS) + ADAM_WD * w)
    return delta, m, v


def _adamw(w, g, m, v):
    layers, rows, cols = w.shape
    tr = 128
    spec = pl.BlockSpec((None, tr, cols), lambda l, i: (l, i, 0))

    def body(w_ref, g_ref, m_ref, v_ref, d_ref, mo_ref, vo_ref):
        d_ref[...], mo_ref[...], vo_ref[...] = _adamw_math(w_ref[...], g_ref[...], m_ref[...], v_ref[...])

    return _pcall(body, name="adamw", out_shape=[_sds(w.shape, f32)] * 3, grid=(layers, rows // tr),
                  in_specs=[spec] * 4, out_specs=[spec] * 3)(w, g, m, v)


def _adamw_small(items):
    n = len(items)

    def body(*refs):
        ins, outs = refs[:4 * n], refs[4 * n:]
        for t in range(n):
            w_ref, g_ref, m_ref, v_ref = ins[4 * t:4 * t + 4]
            if len(g_ref.shape) == len(w_ref.shape) + 1:
                g = g_ref[0]
                for b in range(1, g_ref.shape[0]):
                    g = g + g_ref[b]
            else:
                g = g_ref[...]
            d, m, v = _adamw_math(w_ref[...], g, m_ref[...], v_ref[...])
            outs[4 * t][...], outs[4 * t + 1][...], outs[4 * t + 2][...], outs[4 * t + 3][...] = g, d, m, v

    out_shape = [_sds(w.shape, f32) for (w, _, _, _) in items for _ in range(4)]
    flat = [a for it in items for a in it]
    res = _pcall(body, name="adamw_small", out_shape=out_shape, in_specs=[VMEM] * (4 * n),
                 out_specs=[VMEM] * (4 * n))(*flat)
    return [tuple(res[4 * t:4 * t + 4]) for t in range(n)]


NN = ((1,), (0,))
NT = ((1,), (1,))
TN = ((0,), (0,))


def _mm(name, a, b, *, grid, a_spec, b_spec, out_shape, out_spec, dims, acc_k=False, vmem_mb=48):
    def body(a_ref, b_ref, o_ref):
        r = lax.dot_general(a_ref[...], b_ref[...], (dims, ((), ())), preferred_element_type=f32)
        if acc_k:
            @pl.when(pl.program_id(0) == 0)
            def _():
                o_ref[...] = r

            @pl.when(pl.program_id(0) > 0)
            def _():
                o_ref[...] += r
        else:
            o_ref[...] = r.astype(o_ref.dtype)

    return _pcall(body, name=name, out_shape=out_shape, grid=grid, in_specs=[a_spec, b_spec], out_specs=out_spec,
                  vmem_mb=vmem_mb)(a, b)


def _whole(shape):
    return pl.BlockSpec(shape, lambda j: (0,) * len(shape))


def _split_spec(rows, tile, per_split):
    return pl.BlockSpec((None, rows, tile), lambda j: (j // per_split, 0, j % per_split))


class _Proj:
    def __init__(self, n, splits, tile):
        self.n, self.splits, self.tile = n, splits, tile
        self.steps = n // tile
        self.w_per = n // 4 // tile
        self.a_per = n // splits // tile
        assert self.w_per * tile * 4 == n and self.a_per * tile * splits == n

    def fwd(self, hb, wg):
        T = hb.shape[0]
        return _mm("proj_fwd", hb, wg, grid=(self.steps,), a_spec=_whole((T, D)),
                   b_spec=_split_spec(D, self.tile, self.w_per),
                   out_shape=_sds((self.splits, T, self.n // self.splits), f32),
                   out_spec=_split_spec(T, self.tile, self.a_per), dims=NN)

    def dw(self, hb, dp):
        T = hb.shape[0]
        return _mm("proj_dw", hb, dp, grid=(self.steps,), a_spec=_whole((T, D)),
                   b_spec=_split_spec(T, self.tile, self.a_per), out_shape=_sds((4, D, self.n // 4), bf16),
                   out_spec=_split_spec(D, self.tile, self.w_per), dims=TN)

    def dh(self, dp, wg):
        T = dp.shape[1]
        return _mm("proj_dh", dp, wg, grid=(self.steps,), a_spec=_split_spec(T, self.tile, self.a_per),
                   b_spec=_split_spec(D, self.tile, self.w_per), out_shape=_sds((T, D), f32),
                   out_spec=_whole((T, D)), dims=NT, acc_k=True)


EVEN_PROJ = _Proj(7 * D, 7, 256)
ODD_PROJ = _Proj(4 * D, 2, 512)


def _dy_mm(dob, wo):
    T = dob.shape[0]
    return _mm("out_dy", dob, wo, grid=(4,), a_spec=_whole((T, D)),
               b_spec=pl.BlockSpec((None, 512, D), lambda j: (j, 0, 0)), out_shape=_sds((2, T, D), f32),
               out_spec=_split_spec(T, 512, 2), dims=NT)


def _dwo_mm(y2, dob):
    T = dob.shape[0]
    return _mm("out_dw", y2, dob, grid=(4,), a_spec=_split_spec(T, 512, 2), b_spec=_whole((T, D)),
               out_shape=_sds((4, 512, D), bf16), out_spec=pl.BlockSpec((None, 512, D), lambda j: (j, 0, 0)), dims=TN)


def _head_spec(lead, T):
    return pl.BlockSpec((lead, T, HEAD), lambda h: (0, 0, h))


def _head_vec(rows):
    return pl.BlockSpec((rows, HEAD), lambda h: (0, h))


_HEAD_MAT = pl.BlockSpec((None, HEAD, HEAD), lambda h: (h, 0, 0))


def _causal():
    return lax.broadcasted_iota(jnp.int32, (HEAD, HEAD), 0) >= lax.broadcasted_iota(jnp.int32, (HEAD, HEAD), 1)


def _layernorm_head(v):
    mu = jnp.mean(v, axis=-1, keepdims=True)
    d = v - mu
    rstd = lax.rsqrt(jnp.mean(d * d, axis=-1, keepdims=True) + EPS)
    return d * rstd, rstd


def _even_fwd(p7, conv_w, ln_g, ln_b, sgu_w, sgu_bias):
    T, C = p7.shape[1], CHUNK_ROWS

    def body(p_ref, cw_ref, lg_ref, lb_ref, w_ref, b_ref, y_ref):
        w0, w1, w2 = cw_ref[0:1, :], cw_ref[1:2, :], cw_ref[2:3, :]
        wm = jnp.where(_causal(), w_ref[...], 0.0).astype(bf16)
        bias, lg, lb = b_ref[...], lg_ref[...], lb_ref[...]

        def step(i, halo):
            rows = pl.ds(pl.multiple_of(i * C, C), C)
            tt = p_ref[2, rows, :] * p_ref[0, rows, :]
            ext = jnp.concatenate([halo, tt], axis=0)
            cv = w2 * tt + w1 * pltpu.roll(ext, 1, 0)[HALO_CONV:] + w0 * pltpu.roll(ext, 2, 0)[HALO_CONV:]
            y_ref[0, rows, :] = (p_ref[1, rows, :] * cv * _silu(p_ref[3, rows, :])).astype(bf16)
            vhat, _ = _layernorm_head(p_ref[5, rows, :])
            vn = (vhat * lg + lb).astype(bf16)
            mix = jnp.concatenate([jnp.dot(wm, vn[k * HEAD:(k + 1) * HEAD], preferred_element_type=f32) + bias
                                   for k in range(C // HEAD)], axis=0)
            y_ref[1, rows, :] = (p_ref[4, rows, :] * mix * _silu(p_ref[6, rows, :])).astype(bf16)
            return tt[C - HALO_CONV:]

        lax.fori_loop(0, T // C, step, jnp.zeros((HALO_CONV, HEAD), f32))

    return _pcall(body, name="even_fwd", out_shape=_sds((2, T, D), bf16), grid=(NH,),
                  in_specs=[_head_spec(7, T), _head_vec(3), _head_vec(1), _head_vec(1), _HEAD_MAT, _HEAD_MAT],
                  out_specs=_head_spec(2, T), vmem_mb=32)(p7, conv_w, ln_g, ln_b, sgu_w, sgu_bias)


def _even_bwd(p7, dy2, conv_w, ln_g, ln_b, sgu_w, sgu_bias):
    T, C = p7.shape[1], CHUNK_ROWS
    n_chunks = T // C

    def body(p_ref, dy_ref, cw_ref, lg_ref, lb_ref, w_ref, b_ref,
             dp_ref, dcw_ref, dlg_ref, dlb_ref, dw_ref, dms_ref, dcv_s):
        w0, w1, w2 = cw_ref[0:1, :], cw_ref[1:2, :], cw_ref[2:3, :]
        tri = _causal()
        wm = jnp.where(tri, w_ref[...], 0.0).astype(bf16)
        bias, lg, lb = b_ref[...], lg_ref[...], lb_ref[...]
        dw_ref[...] = jnp.zeros_like(dw_ref)
        dms_ref[...] = jnp.zeros_like(dms_ref)

        def fwd_step(i, carry):
            halo, a0, a1, a2, alg, alb = carry
            rows = pl.ds(pl.multiple_of(i * C, C), C)
            ah, ab, ac, az = p_ref[0, rows, :], p_ref[1, rows, :], p_ref[2, rows, :], p_ref[3, rows, :]
            dya = dy_ref[0, rows, :]
            tt = ac * ah
            ext = jnp.concatenate([halo, tt], axis=0)
            t1, t2 = pltpu.roll(ext, 1, 0)[HALO_CONV:], pltpu.roll(ext, 2, 0)[HALO_CONV:]
            cv = w2 * tt + w1 * t1 + w0 * t2
            sa, dsa = _silu_and_grad(az)
            g1 = dya * sa
            dp_ref[1, rows, :] = (g1 * cv).astype(bf16)
            dp_ref[3, rows, :] = (dya * ab * cv * dsa).astype(bf16)
            dcv = g1 * ab
            dcv_s[rows, :] = dcv
            a2 = a2 + jnp.sum(dcv * tt, axis=0, keepdims=True)
            a1 = a1 + jnp.sum(dcv * t1, axis=0, keepdims=True)
            a0 = a0 + jnp.sum(dcv * t2, axis=0, keepdims=True)

            u, zb, dyb = p_ref[4, rows, :], p_ref[6, rows, :], dy_ref[1, rows, :]
            vhat, rstd = _layernorm_head(p_ref[5, rows, :])
            vn = (vhat * lg + lb).astype(bf16)
            sb, dsb = _silu_and_grad(zb)
            mix = jnp.concatenate([jnp.dot(wm, vn[k * HEAD:(k + 1) * HEAD], preferred_element_type=f32) + bias
                                   for k in range(C // HEAD)], axis=0)
            dp_ref[4, rows, :] = (dyb * mix * sb).astype(bf16)
            dp_ref[6, rows, :] = (dyb * u * mix * dsb).astype(bf16)
            dmix = dyb * u * sb
            dvn_parts = []
            for k in range(C // HEAD):
                dm = dmix[k * HEAD:(k + 1) * HEAD]
                dmb = dm.astype(bf16)
                dvn_parts.append(lax.dot_general(wm, dmb, (TN, ((), ())), preferred_element_type=f32))
                dw_ref[...] += lax.dot_general(dmb, vn[k * HEAD:(k + 1) * HEAD], (NT, ((), ())),
                                               preferred_element_type=f32)
                dms_ref[...] += dm
            dvn = jnp.concatenate(dvn_parts, axis=0)
            alg = alg + jnp.sum(dvn * vhat, axis=0, keepdims=True)
            alb = alb + jnp.sum(dvn, axis=0, keepdims=True)
            dvh = dvn * lg
            dv = rstd * (dvh - jnp.mean(dvh, axis=-1, keepdims=True)
                         - vhat * jnp.mean(dvh * vhat, axis=-1, keepdims=True))
            dp_ref[5, rows, :] = dv.astype(bf16)
            return tt[C - HALO_CONV:], a0, a1, a2, alg, alb

        zrow = jnp.zeros((1, HEAD), f32)
        _, a0, a1, a2, alg, alb = lax.fori_loop(
            0, n_chunks, fwd_step, (jnp.zeros((HALO_CONV, HEAD), f32), zrow, zrow, zrow, zrow, zrow))
        dcw_ref[0:1, :], dcw_ref[1:2, :], dcw_ref[2:3, :] = a0, a1, a2
        dlg_ref[...], dlb_ref[...] = alg, alb
        dw_ref[...] = jnp.where(tri, dw_ref[...], 0.0)

        def bwd_step(k, halo):
            rows = pl.ds(pl.multiple_of((n_chunks - 1 - k) * C, C), C)
            dcv = dcv_s[rows, :]
            ext = jnp.concatenate([dcv, halo], axis=0)
            n1 = pltpu.roll(ext, C + HALO_CONV - 1, 0)[:C]
            n2 = pltpu.roll(ext, C + HALO_CONV - 2, 0)[:C]
            dtt = w2 * dcv + w1 * n1 + w0 * n2
            dp_ref[2, rows, :] = (dtt * p_ref[0, rows, :]).astype(bf16)
            dp_ref[0, rows, :] = (dtt * p_ref[2, rows, :]).astype(bf16)
            return dcv[:HALO_CONV]

        lax.fori_loop(0, n_chunks, bwd_step, jnp.zeros((HALO_CONV, HEAD), f32))

    out_shape = [_sds((7, T, D), bf16), _sds((3, D), f32), _sds((1, D), f32), _sds((1, D), f32),
                 _sds((NH, HEAD, HEAD), f32), _sds((NH, HEAD, HEAD), f32)]
    return _pcall(body, name="even_bwd", out_shape=out_shape, grid=(NH,),
                  in_specs=[_head_spec(7, T), _head_spec(2, T), _head_vec(3), _head_vec(1), _head_vec(1),
                            _HEAD_MAT, _HEAD_MAT],
                  out_specs=[_head_spec(7, T), _head_vec(3), _head_vec(1), _head_vec(1), _HEAD_MAT, _HEAD_MAT],
                  scratch=[pltpu.VMEM((T, HEAD), f32)], vmem_mb=48)(p7, dy2, conv_w, ln_g, ln_b, sgu_w, sgu_bias)


def _window_sum(ext, win, towards_past):
    n, k, s = ext.shape[0], 1, ext
    while k < win:
        s = s + pltpu.roll(s, k if towards_past else n - k, 0)
        k *= 2
    return s


def _pool_count(i, C, win):
    t = i * C + lax.broadcasted_iota(jnp.int32, (C, 1), 0)
    return jnp.minimum(t + 1, win).astype(f32)


def _group_specs(T):
    p_spec = pl.BlockSpec((None, T, GC), lambda g: (0, 0, g))
    z_spec = pl.BlockSpec((None, T, GC), lambda g: (1, 0, g))
    pw_spec = pl.BlockSpec((4, GC // 4, GC), lambda g: (0, g, 0))
    ps_spec = pl.BlockSpec((1, GC), lambda g: (0, g))
    y_spec = pl.BlockSpec((None, T, GC), lambda g: (g // 2, 0, g % 2))
    return p_spec, z_spec, pw_spec, ps_spec, y_spec


def _odd_fwd(p2, pool_wg, pool_scale):
    T, C = p2.shape[1], CHUNK_ROWS
    p_spec, z_spec, pw_spec, ps_spec, y_spec = _group_specs(T)

    def body(p_ref, z_ref, pw_ref, ps_ref, y_ref):
        pw, ps = pw_ref[...].reshape(GC, GC), ps_ref[...]

        def run(win):
            def step(i, halo):
                rows = pl.ds(pl.multiple_of(i * C, C), C)
                p = p_ref[rows, :]
                s = _window_sum(jnp.concatenate([halo, p], axis=0), win, True)[HALO_POOL:]
                pooled = s / _pool_count(i, C, win) - p
                ypre = jnp.dot(pooled.astype(bf16), pw, preferred_element_type=f32)
                y_ref[rows, :] = (ypre * ps * _silu(z_ref[rows, :])).astype(bf16)
                return p[C - HALO_POOL:]

            lax.fori_loop(0, T // C, step, jnp.zeros((HALO_POOL, GC), f32))

        for gi, win in enumerate(WINDOWS):
            pl.when(pl.program_id(0) == gi)(functools.partial(run, win))

    return _pcall(body, name="odd_fwd", out_shape=_sds((2, T, D), bf16), grid=(len(WINDOWS),),
                  in_specs=[p_spec, z_spec, pw_spec, ps_spec], out_specs=y_spec, vmem_mb=40)(p2, p2, pool_wg, pool_scale)


def _odd_bwd(p2, dy2, pool_wg, pool_scale):
    T, C = p2.shape[1], CHUNK_ROWS
    n_chunks = T // C
    p_spec, z_spec, pw_spec, ps_spec, y_spec = _group_specs(T)

    def body(p_ref, z_ref, dy_ref, pw_ref, ps_ref, dp_ref, dpw_ref, dps_ref, q_s, acc_s):
        pw, ps = pw_ref[...].reshape(GC, GC), ps_ref[...]

        def run(win):
            acc_s[...] = jnp.zeros_like(acc_s)

            def fwd_step(i, carry):
                halo, aps = carry
                rows = pl.ds(pl.multiple_of(i * C, C), C)
                p, z, dy = p_ref[rows, :], z_ref[rows, :], dy_ref[rows, :]
                cnt = _pool_count(i, C, win)
                s = _window_sum(jnp.concatenate([halo, p], axis=0), win, True)[HALO_POOL:]
                pb = (s / cnt - p).astype(bf16)
                ypre = jnp.dot(pb, pw, preferred_element_type=f32)
                sz, dsz = _silu_and_grad(z)
                aps = aps + jnp.sum(dy * ypre * sz, axis=0, keepdims=True)
                dp_ref[1, rows, :] = (dy * ypre * ps * dsz).astype(bf16)
                dyp = (dy * ps * sz).astype(bf16)
                acc_s[...] += lax.dot_general(pb, dyp, (TN, ((), ())), preferred_element_type=f32)
                dpool = lax.dot_general(dyp, pw, (NT, ((), ())), preferred_element_type=f32)
                q_s[rows, :] = dpool / cnt
                return p[C - HALO_POOL:], aps

            _, aps = lax.fori_loop(0, n_chunks, fwd_step, (jnp.zeros((HALO_POOL, GC), f32), jnp.zeros((1, GC), f32)))
            dps_ref[...] = aps
            dpw_ref[...] = acc_s[...].reshape(4, GC // 4, GC).astype(bf16)

            def bwd_step(k, halo):
                i = n_chunks - 1 - k
                rows = pl.ds(pl.multiple_of(i * C, C), C)
                q = q_s[rows, :]
                s = _window_sum(jnp.concatenate([q, halo], axis=0), win, False)[:C]
                dp_ref[0, rows, :] = (s - q * _pool_count(i, C, win)).astype(bf16)
                return q[:HALO_POOL]

            lax.fori_loop(0, n_chunks, bwd_step, jnp.zeros((HALO_POOL, GC), f32))

        for gi, win in enumerate(WINDOWS):
            pl.when(pl.program_id(0) == gi)(functools.partial(run, win))

    out_shape = [_sds((2, T, 2 * D), bf16), _sds((4, GC, GC), bf16), _sds((1, 2 * D), f32)]
    return _pcall(body, name="odd_bwd", out_shape=out_shape, grid=(len(WINDOWS),),
                  in_specs=[p_spec, z_spec, y_spec, pw_spec, ps_spec],
                  out_specs=[pl.BlockSpec((2, T, GC), lambda g: (0, 0, g)), pw_spec, ps_spec],
                  scratch=[pltpu.VMEM((T, GC), f32), pltpu.VMEM((GC, GC), f32)], vmem_mb=52)(
                      p2, p2, dy2, pool_wg, pool_scale)


def _ada_fwd(c_all, ada_w):
    cols = ada_w.shape[2]

    def body(c_ref, w_ref, o_ref):
        o_ref[...] = jnp.dot(_silu(c_ref[...]), w_ref[...], preferred_element_type=f32,
                             precision=lax.Precision.HIGHEST)

    return _pcall(body, name="ada_fwd", out_shape=_sds((4, N_DEV, cols), f32), grid=(4,),
                  in_specs=[pl.BlockSpec((N_DEV, D), lambda i: (0, 0)), pl.BlockSpec((None, D, cols), lambda i: (i, 0, 0))],
                  out_specs=pl.BlockSpec((None, N_DEV, cols), lambda i: (i, 0, 0)))(c_all, ada_w)


def _ada_bwd(c_all_t, dmod, w, m, v):
    cols, tr = w.shape[2], 256
    spec = pl.BlockSpec((None, tr, cols), lambda l, i: (l, i, 0))

    def body(c_ref, dm_ref, w_ref, m_ref, v_ref, g_ref, d_ref, mo_ref, vo_ref):
        sc = _silu(c_ref[...])
        g = sc[:, 0:1] * dm_ref[0:1, :]
        for b in range(1, N_DEV):
            g = g + sc[:, b:b + 1] * dm_ref[b:b + 1, :]
        g_ref[...] = g
        d_ref[...], mo_ref[...], vo_ref[...] = _adamw_math(w_ref[...], g, m_ref[...], v_ref[...])

    return _pcall(body, name="ada_bwd", out_shape=[_sds(w.shape, f32)] * 4, grid=(4, D // tr),
                  in_specs=[pl.BlockSpec((tr, N_DEV), lambda l, i: (i, 0)),
                            pl.BlockSpec((None, N_DEV, cols), lambda l, i: (l, 0, 0)), spec, spec, spec],
                  out_specs=[spec] * 4)(c_all_t, dmod, w, m, v)


def _halves(w2d):
    rows, cols = w2d.shape
    return w2d.astype(bf16).reshape(2, rows // 2, cols)


def _local_step(x, target, mods, norm_g, final_g, even_w, odd_w):
    depth = len(mods)
    saved = []
    for i in range(depth):
        shift, scale, gate = mods[i]
        hb = _hnorm(x, norm_g[i:i + 1], shift, scale)
        if i % 2 == 0:
            w_in, conv_w, ln_g, ln_b, sgu_w, sgu_b, w_out = even_w[i // 2]
            bias = jnp.broadcast_to(sgu_b[:, :, None], (NH, HEAD, HEAD))
            p = EVEN_PROJ.fwd(hb, w_in)
            y2 = _even_fwd(p, conv_w, ln_g, ln_b, sgu_w, bias)
        else:
            w_in, pool_w, pool_scale, w_out = odd_w[i // 2]
            p = ODD_PROJ.fwd(hb, w_in)
            y2 = _odd_fwd(p, pool_w, pool_scale)
        x_next, o = _out_proj(y2, w_out.reshape(2, D, D), x, gate)
        saved.append((x, hb, p, y2, o))
        x = x_next

    gin, loss, dfinal_g = _loss_bwd(x, target, final_g)
    layer_grads, dmod, dnorm_g = [None] * depth, [None] * depth, [None] * depth
    for i in reversed(range(depth)):
        shift, scale, gate = mods[i]
        x_in, hb, p, y2, o = saved[i]
        dob, dgate = _gate_bwd(gin, o, gate)
        if i % 2 == 0:
            w_in, conv_w, ln_g, ln_b, sgu_w, sgu_b, w_out = even_w[i // 2]
            bias = jnp.broadcast_to(sgu_b[:, :, None], (NH, HEAD, HEAD))
            dy2 = _dy_mm(dob, w_out)
            dp, dconv, dlg, dlb, dsw, dms = _even_bwd(p, dy2, conv_w, ln_g, ln_b, sgu_w, bias)
            proj = EVEN_PROJ
            grads = dict(conv_w=dconv, ln_g=dlg, ln_b=dlb, sgu_w=dsw, sgu_b=jnp.sum(dms, axis=-1))
        else:
            w_in, pool_w, pool_scale, w_out = odd_w[i // 2]
            dy2 = _dy_mm(dob, w_out)
            dp, dpw, dps = _odd_bwd(p, dy2, pool_w, pool_scale)
            proj = ODD_PROJ
            grads = dict(pool_w=dpw, pool_scale=dps)
        grads["w_out"] = _dwo_mm(y2, dob)
        grads["w_in"] = proj.dw(hb, dp)
        dh = proj.dh(dp, w_in)
        gin, stats = _norm_bwd(x_in, dh, gin, norm_g[i:i + 1], scale)
        layer_grads[i] = grads
        dmod[i] = jnp.concatenate([stats[0:2], dgate], axis=0)
        dnorm_g[i] = stats[2:3]
    return loss, gin, layer_grads, jnp.stack(dmod), jnp.concatenate(dnorm_g, axis=0), dfinal_g


def _pack_rows(parts):
    rows = [p.reshape(-1, LANES) for p in parts]
    total = sum(r.shape[0] for r in rows)
    padded = -(-total // (8 * N_DEV)) * (8 * N_DEV)
    if padded > total:
        rows.append(jnp.zeros((padded - total, LANES), f32))
    return jnp.concatenate(rows, axis=0)


def _unpack_rows(buf, shapes):
    out, r = [], 0
    for shp in shapes:
        n = 1
        for d in shp:
            n *= d
        out.append(buf[r:r + n // LANES].reshape(shp))
        r += n // LANES
    return out


def kernel(x, c, norm_g, ada_w, ada_b, ab_w_in, ab_conv_w, ab_ln_g, ab_ln_b, ab_sgu_w, ab_sgu_b, ab_w_out, c_w_in, c_pool_w, c_pool_scale, c_w_out, final_g, loss_target, m_norm_g, m_ada_w, m_ada_b, m_ab_w_in, m_ab_conv_w, m_ab_ln_g, m_ab_ln_b, m_ab_sgu_w, m_ab_sgu_b, m_ab_w_out, m_c_w_in, m_c_pool_w, m_c_pool_scale, m_c_w_out, m_final_g, v_norm_g, v_ada_w, v_ada_b, v_ab_w_in, v_ab_conv_w, v_ab_ln_g, v_ab_ln_b, v_ab_sgu_w, v_ab_sgu_b, v_ab_w_out, v_c_w_in, v_c_pool_w, v_c_pool_scale, v_c_w_out, v_final_g):
    ix, iy, ic = _place()
    chip, dev = 2 * ix + iy, 4 * ix + 2 * iy + ic
    n_even, n_odd = ab_w_in.shape[0], c_w_in.shape[0]
    depth = n_even + n_odd
    acols = ada_w.shape[2]

    c_all = _gather8(c, "gather_c").reshape(N_DEV, D)
    modp = _ada_fwd(c_all, ada_w)
    modg = _gather8(modp, "gather_mod")
    mod_rows = lax.dynamic_index_in_dim(modg[0::2], dev, axis=2, keepdims=False)
    mod = jnp.transpose(mod_rows, (1, 0, 2)).reshape(depth, 3 * D) + ada_b
    mods = [(mod[i:i + 1, 0:D], mod[i:i + 1, D:2 * D], mod[i:i + 1, 2 * D:3 * D]) for i in range(depth)]

    big = []
    for j in range(n_even):
        big += [_halves(ab_w_in[j]), _halves(ab_w_out[j])]
    for j in range(n_odd):
        big += [_halves(c_w_in[j]), _halves(c_pool_w[j].reshape(GC, GC)), _halves(c_w_out[j])]
    gathered = _allgather_weights(big, "allgather_weights")
    gathered = [g.reshape(4, 2 * g.shape[2], g.shape[3]) for g in gathered]

    def shard_cols(a, width):
        return lax.dynamic_slice_in_dim(a, chip * width, width, axis=a.ndim - 1)

    small_sharded = jnp.concatenate([ab_conv_w.reshape(1, -1), c_pool_scale.reshape(1, -1)], axis=1)
    small_all = _gather8(small_sharded, "gather_small")[0::2, 0]
    n_conv = ab_conv_w.size
    conv_all = small_all[:, :n_conv].reshape(4, n_even, 3, D // 4)
    conv_full = jnp.transpose(conv_all, (1, 2, 0, 3)).reshape(n_even, 3, D)
    scale_all = small_all[:, n_conv:].reshape(4, n_odd, 2 * D // 4)
    scale_full = jnp.transpose(scale_all, (1, 0, 2)).reshape(n_odd, 2 * D)

    even_w = [(gathered[2 * j], conv_full[j], ab_ln_g[j:j + 1], ab_ln_b[j:j + 1], ab_sgu_w[j], ab_sgu_b[j],
               gathered[2 * j + 1]) for j in range(n_even)]
    base = 2 * n_even
    odd_w = [(gathered[base + 3 * j], gathered[base + 3 * j + 1], scale_full[j:j + 1], gathered[base + 3 * j + 2])
             for j in range(n_odd)]

    loss, grad_x, layer_grads, dmod, dnorm_g, dfinal_g = _local_step(
        x[0], loss_target[0], mods, norm_g, final_g.reshape(1, D), even_w, odd_w)
    loss = lax.psum(loss[0, 0], ("x", "y", "c"))

    names = []
    for j in range(n_even):
        names += [(2 * j, "w_in"), (2 * j, "w_out")]
    for j in range(n_odd):
        names += [(2 * j + 1, "w_in"), (2 * j + 1, "pool_w"), (2 * j + 1, "w_out")]
    big_g = [layer_grads[i][k] for i, k in names]
    big_g = [g.reshape(4, 2, g.shape[1] // 2, g.shape[2]) for g in big_g]
    mine, theirs = _rs_pair_exchange(big_g, "rs_pair_exchange")
    pairs = [_rs_add(a, b) for a, b in zip(mine, theirs)]
    slots = _rs_chip_exchange(pairs, "rs_chip_exchange")
    halves = [_rs_sum(q) for q in slots]
    ev = [halves[2 * j:2 * j + 2] for j in range(n_even)]
    od = [halves[base + 3 * j:base + 3 * j + 3] for j in range(n_odd)]
    groups = [[e[0] for e in ev], [e[1] for e in ev], [o[0] for o in od], [o[1] for o in od], [o[2] for o in od]]
    g_ab_w_in, g_ab_w_out, g_c_w_in, g_c_pool_w, g_c_w_out = [
        g.reshape(g.shape[0], 2 * g.shape[2], g.shape[3]) for g in _rs_half_exchange(groups, "rs_half_exchange")]

    small_parts = [dnorm_g, dfinal_g,
                   jnp.stack([layer_grads[2 * j]["conv_w"] for j in range(n_even)]),
                   jnp.concatenate([layer_grads[2 * j]["ln_g"] for j in range(n_even)], axis=0),
                   jnp.concatenate([layer_grads[2 * j]["ln_b"] for j in range(n_even)], axis=0),
                   jnp.stack([layer_grads[2 * j]["sgu_w"] for j in range(n_even)]),
                   jnp.stack([layer_grads[2 * j]["sgu_b"] for j in range(n_even)]),
                   jnp.concatenate([layer_grads[2 * j + 1]["pool_scale"] for j in range(n_odd)], axis=0)]
    small_shapes = [p.shape for p in small_parts]
    reduced = _allreduce8(_pack_rows(small_parts), "allreduce_small")
    g_norm_g, g_final_g, g_conv_full, g_ln_g, g_ln_b, g_sgu_w, g_sgu_b, g_scale_full = _unpack_rows(reduced, small_shapes)
    g_conv = shard_cols(g_conv_full, D // 4)
    g_scale = shard_cols(g_scale_full, 2 * D // 4)
    dmod_all = _gather8(dmod.reshape(depth * 3 * D // LANES, LANES), "gather_dmod").reshape(N_DEV, depth, 3 * D)

    def two_d(a):
        return a.reshape(-1, a.shape[-1])

    small = [(norm_g, g_norm_g, m_norm_g, v_norm_g),
             (ada_b, dmod_all, m_ada_b, v_ada_b),
             (two_d(ab_conv_w), two_d(g_conv), two_d(m_ab_conv_w), two_d(v_ab_conv_w)),
             (ab_ln_g, g_ln_g, m_ab_ln_g, v_ab_ln_g),
             (ab_ln_b, g_ln_b, m_ab_ln_b, v_ab_ln_b),
             (two_d(ab_sgu_w), two_d(g_sgu_w), two_d(m_ab_sgu_w), two_d(v_ab_sgu_w)),
             (two_d(ab_sgu_b), two_d(g_sgu_b), two_d(m_ab_sgu_b), two_d(v_ab_sgu_b)),
             (c_pool_scale, g_scale, m_c_pool_scale, v_c_pool_scale),
             (final_g.reshape(1, D), g_final_g, m_final_g.reshape(1, D), v_final_g.reshape(1, D))]
    small_res = _adamw_small(small)
    small_shapes_out = [norm_g.shape, ada_b.shape, ab_conv_w.shape, ab_ln_g.shape, ab_ln_b.shape, ab_sgu_w.shape,
                        ab_sgu_b.shape, c_pool_scale.shape, final_g.shape]
    (r_norm_g, r_ada_b, r_conv, r_ln_g, r_ln_b, r_sgu_w, r_sgu_b, r_scale, r_final_g) = [
        tuple(a.reshape(shp) for a in res) for res, shp in zip(small_res, small_shapes_out)]

    dmod_cols = jnp.transpose(shard_cols(dmod_all, acols), (1, 0, 2))
    r_ada_w = _ada_bwd(c_all.T, dmod_cols, ada_w, m_ada_w, v_ada_w)

    def big_adamw(w, g, m, v):
        shp = w.shape
        w3, m3, v3 = (a.reshape(g.shape) for a in (w, m, v))
        d, mo, vo = _adamw(w3, g, m3, v3)
        return tuple(a.reshape(shp) for a in (g, d, mo, vo))

    r_ab_w_in = big_adamw(ab_w_in, g_ab_w_in, m_ab_w_in, v_ab_w_in)
    r_ab_w_out = big_adamw(ab_w_out, g_ab_w_out, m_ab_w_out, v_ab_w_out)
    r_c_w_in = big_adamw(c_w_in, g_c_w_in, m_c_w_in, v_c_w_in)
    r_c_pool_w = big_adamw(c_pool_w, g_c_pool_w, m_c_pool_w, v_c_pool_w)
    r_c_w_out = big_adamw(c_w_out, g_c_w_out, m_c_w_out, v_c_w_out)

    order = [r_norm_g, r_ada_w, r_ada_b, r_ab_w_in, r_conv, r_ln_g, r_ln_b, r_sgu_w, r_sgu_b, r_ab_w_out,
             r_c_w_in, r_c_pool_w, r_scale, r_c_w_out, r_final_g]
    outs = [loss, grad_x[None]]
    for field in range(4):
        outs += [r[field] for r in order]
    return tuple(outs)
```

```python
import functools

import jax
import jax.numpy as jnp
from jax import lax
from jax.experimental import pallas as pl
from jax.experimental.pallas import tpu as pltpu

f32, bf16 = jnp.float32, jnp.bfloat16

D = 1024
HEAD = 128
NH = 8
WINDOWS = (2, 4, 8, 16)
GC = 512
EPS = 1e-6
HALO_CONV = 8
HALO_POOL = 16
CHUNK_ROWS = 256
N_DEV = 8
LANES = 128

ADAM_LR, ADAM_B1, ADAM_B2, ADAM_EPS, ADAM_WD, ADAM_STEP = 0.001, 0.9, 0.999, 1e-08, 0.01, 10

MESH = pl.DeviceIdType.MESH
ANY = pl.BlockSpec(memory_space=pl.ANY)
VMEM = pl.BlockSpec(memory_space=pltpu.VMEM)
MIB = 2 ** 20


def _pcall(body, *, name, out_shape, grid=None, in_specs=None, out_specs=None, scratch=(), vmem_mb=None,
           aliases=None, prefetch=0):
    kw = {}
    if prefetch:
        kw["grid_spec"] = pltpu.PrefetchScalarGridSpec(num_scalar_prefetch=prefetch, grid=grid, in_specs=in_specs,
                                                       out_specs=out_specs, scratch_shapes=list(scratch))
    else:
        if grid is not None:
            kw["grid"] = grid
        if in_specs is not None:
            kw["in_specs"] = in_specs
        if out_specs is not None:
            kw["out_specs"] = out_specs
        if scratch:
            kw["scratch_shapes"] = list(scratch)
    if aliases:
        kw["input_output_aliases"] = aliases
    params = pltpu.CompilerParams(vmem_limit_bytes=None if vmem_mb is None else vmem_mb * MIB)
    return pl.pallas_call(body, name=name, out_shape=out_shape, compiler_params=params, **kw)


def _sds(shape, dtype):
    return jax.ShapeDtypeStruct(tuple(shape), dtype)


def _silu(z):
    return z * jax.nn.sigmoid(z)


def _silu_and_grad(z):
    s = jax.nn.sigmoid(z)
    return z * s, s * (1.0 + z * (1.0 - s))


def _place():
    return lax.axis_index("x"), lax.axis_index("y"), lax.axis_index("c")


def _gather8(blk, name):
    def body(x_ref, o_ref, ssem, rsem):
        x, y, c = _place()
        me = 4 * x + 2 * y + c
        o_ref[me] = x_ref[...]
        sends = []
        for k in range(1, N_DEV):
            px = 1 - x if k & 4 else x
            py = 1 - y if k & 2 else y
            pc = 1 - c if k & 1 else c
            cp = pltpu.make_async_remote_copy(src_ref=x_ref, dst_ref=o_ref.at[me], send_sem=ssem.at[k - 1],
                                              recv_sem=rsem.at[k - 1], device_id=(px, py, pc), device_id_type=MESH)
            cp.start()
            sends.append((cp, 4 * px + 2 * py + pc))
        for k, (cp, peer) in enumerate(sends):
            pltpu.make_async_remote_copy(src_ref=x_ref, dst_ref=o_ref.at[peer], send_sem=ssem.at[k],
                                         recv_sem=rsem.at[k], device_id=(x, y, c), device_id_type=MESH).wait_recv()
        for cp, _ in sends:
            cp.wait_send()

    return _pcall(body, name=name, out_shape=_sds((N_DEV,) + blk.shape, blk.dtype), in_specs=[VMEM], out_specs=VMEM,
                  scratch=[pltpu.SemaphoreType.DMA((N_DEV - 1,)), pltpu.SemaphoreType.DMA((N_DEV - 1,))])(blk)


def _allreduce8(buf, name):
    rows = buf.shape[0]
    rb = rows // N_DEV
    assert rb * N_DEV == rows and rb % 8 == 0

    def body(x_ref, o_ref, stage, ssem, rsem):
        x, y, c = _place()
        me = 4 * x + 2 * y + c
        peers = []
        for k in range(1, N_DEV):
            px = 1 - x if k & 4 else x
            py = 1 - y if k & 2 else y
            pc = 1 - c if k & 1 else c
            peers.append(((px, py, pc), 4 * px + 2 * py + pc))

        def blk(ref, idx):
            return ref.at[pl.ds(pl.multiple_of(idx * rb, 8), rb), :]

        def copy(phase, k, src, dst, dev):
            return pltpu.make_async_remote_copy(src_ref=src, dst_ref=dst, send_sem=ssem.at[phase, k],
                                                recv_sem=rsem.at[phase, k], device_id=dev, device_id_type=MESH)

        stage[me] = blk(x_ref, me)[...]
        scatter = [copy(0, k, blk(x_ref, pidx), stage.at[me], dev) for k, (dev, pidx) in enumerate(peers)]
        for cp in scatter:
            cp.start()
        for k, (dev, pidx) in enumerate(peers):
            copy(0, k, blk(x_ref, pidx), stage.at[pidx], dev).wait_recv()
        total = stage[0]
        for j in range(1, N_DEV):
            total = total + stage[j]
        blk(o_ref, me)[...] = total
        gather = [copy(1, k, blk(o_ref, me), blk(o_ref, me), dev) for k, (dev, pidx) in enumerate(peers)]
        for cp in gather:
            cp.start()
        for k, (dev, pidx) in enumerate(peers):
            copy(1, k, blk(o_ref, pidx), blk(o_ref, pidx), dev).wait_recv()
        for cp in scatter + gather:
            cp.wait_send()

    return _pcall(body, name=name, out_shape=_sds(buf.shape, f32), in_specs=[VMEM], out_specs=VMEM,
                  scratch=[pltpu.VMEM((N_DEV, rb, LANES), f32), pltpu.SemaphoreType.DMA((2, N_DEV - 1)),
                           pltpu.SemaphoreType.DMA((2, N_DEV - 1))])(buf)


def _other_chips(x, y):
    return [((1 - x, y), 2 * (1 - x) + y), ((x, 1 - y), 2 * x + (1 - y)), ((1 - x, 1 - y), 2 * (1 - x) + (1 - y))]


def _allgather_weights(placed, name):
    n = len(placed)

    def body(*refs):
        o = refs[n:2 * n]
        ssem, rsem = refs[2 * n:]
        x, y, c = _place()
        s_me = 2 * x + y
        chips = _other_chips(x, y)

        def copy(t, k, s, half, dev):
            return pltpu.make_async_remote_copy(src_ref=o[t].at[s, c], dst_ref=o[t].at[s, half], send_sem=ssem.at[t, k],
                                                recv_sem=rsem.at[t, k], device_id=dev, device_id_type=MESH)

        sends = []
        for t in range(n):
            for j, ((px, py), _) in enumerate(chips):
                sends.append(copy(t, j, s_me, c, (px, py, c)))
                sends[-1].start()
        for j, (_, s_p) in enumerate(chips):
            for t in range(n):
                copy(t, j, s_p, c, (x, y, c)).wait_recv()
                sends.append(copy(t, 3 + j, s_p, c, (x, y, 1 - c)))
                sends[-1].start()
        for j, (_, s_p) in enumerate(chips):
            for t in range(n):
                copy(t, 3 + j, s_p, 1 - c, (x, y, c)).wait_recv()
        for cp in sends:
            cp.wait_send()

    return _pcall(body, name=name, out_shape=[_sds(p.shape, bf16) for p in placed], in_specs=[ANY] * n,
                  out_specs=[ANY] * n, aliases={t: t for t in range(n)},
                  scratch=[pltpu.SemaphoreType.DMA((n, 6)), pltpu.SemaphoreType.DMA((n, 6))])(*placed)


def _rs_pair_exchange(grads, name):
    n = len(grads)

    def body(*refs):
        g, theirs = refs[:n], refs[n:2 * n]
        ssem, rsem = refs[2 * n:]
        x, y, c = _place()
        sends = [pltpu.make_async_remote_copy(src_ref=g[t].at[:, 1 - c], dst_ref=theirs[t], send_sem=ssem.at[t],
                                              recv_sem=rsem.at[t], device_id=(x, y, 1 - c), device_id_type=MESH)
                 for t in range(n)]
        for cp in sends:
            cp.start()
        for cp in sends:
            cp.wait()

    half = [_sds((4,) + gr.shape[2:], bf16) for gr in grads]
    return _pcall(body, name=name, out_shape=half, in_specs=[ANY] * n, out_specs=[ANY] * n,
                  scratch=[pltpu.SemaphoreType.DMA((n,)), pltpu.SemaphoreType.DMA((n,))])(*grads)


def _rs_chip_exchange(pairs, name):
    n = len(pairs)

    def body(*refs):
        p, q = refs[:n], refs[n:2 * n]
        ssem, rsem = refs[2 * n:]
        x, y, c = _place()
        chips = _other_chips(x, y)

        def copy(t, j, s, dev):
            return pltpu.make_async_remote_copy(src_ref=p[t].at[s], dst_ref=q[t].at[j], send_sem=ssem.at[t, j],
                                                recv_sem=rsem.at[t, j], device_id=dev, device_id_type=MESH)

        sends = [copy(t, j, s_p, (px, py, c)) for t in range(n) for j, ((px, py), s_p) in enumerate(chips)]
        for cp in sends:
            cp.start()
        for cp in sends:
            cp.wait()

    return _pcall(body, name=name, out_shape=[_sds((3,) + p.shape[1:], bf16) for p in pairs], in_specs=[ANY] * n,
                  out_specs=[ANY] * n,
                  scratch=[pltpu.SemaphoreType.DMA((n, 3)), pltpu.SemaphoreType.DMA((n, 3))])(*pairs)


def _rs_half_exchange(groups, name):
    ng = len(groups)
    where = [(k, l) for k, grp in enumerate(groups) for l in range(grp.shape[0])]

    def body(*refs):
        o = refs[ng:2 * ng]
        ssem, rsem = refs[2 * ng:]
        x, y, c = _place()

        def copy(t, k, l, half, dev):
            return pltpu.make_async_remote_copy(src_ref=o[k].at[l, c], dst_ref=o[k].at[l, half], send_sem=ssem.at[t],
                                                recv_sem=rsem.at[t], device_id=dev, device_id_type=MESH)

        sends = [copy(t, k, l, c, (x, y, 1 - c)) for t, (k, l) in enumerate(where)]
        for cp in sends:
            cp.start()
        for t, (k, l) in enumerate(where):
            copy(t, k, l, 1 - c, (x, y, c)).wait_recv()
        for cp in sends:
            cp.wait_send()

    nt = len(where)
    return _pcall(body, name=name, out_shape=[_sds(g.shape, f32) for g in groups], in_specs=[ANY] * ng,
                  out_specs=[ANY] * ng, aliases={k: k for k in range(ng)},
                  scratch=[pltpu.SemaphoreType.DMA((nt,)), pltpu.SemaphoreType.DMA((nt,))])(*groups)


def _row_spec(tm, cols):
    return pl.BlockSpec((tm, cols), lambda i: (i, 0))


def _vec_spec(cols, rows=1):
    return pl.BlockSpec((rows, cols), lambda i: (0, 0))


def _hnorm(x, g, shift, scale):
    T, tm = x.shape[0], 256

    def body(x_ref, g_ref, sh_ref, sc_ref, h_ref):
        xv = x_ref[...]
        r = lax.rsqrt(jnp.mean(xv * xv, axis=-1, keepdims=True) + EPS)
        a = (xv * r) * g_ref[...]
        h_ref[...] = (a * (1.0 + sc_ref[...]) + sh_ref[...]).astype(bf16)

    return _pcall(body, name="hnorm", out_shape=_sds((T, D), bf16), grid=(T // tm,),
                  in_specs=[_row_spec(tm, D), _vec_spec(D), _vec_spec(D), _vec_spec(D)],
                  out_specs=_row_spec(tm, D))(x, g, shift, scale)


def _out_proj(y2, wo, x, gate):
    T, tm = x.shape[0], 512

    def body(y_ref, w_ref, x_ref, g_ref, xo_ref, o_ref):
        o = jnp.dot(y_ref[0], w_ref[0], preferred_element_type=f32)
        o = o + jnp.dot(y_ref[1], w_ref[1], preferred_element_type=f32)
        o_ref[...] = o
        xo_ref[...] = x_ref[...] + g_ref[...] * o

    return _pcall(body, name="out_proj", out_shape=[_sds((T, D), f32), _sds((T, D), f32)], grid=(T // tm,),
                  in_specs=[pl.BlockSpec((2, tm, D), lambda i: (0, i, 0)), pl.BlockSpec((2, D, D), lambda i: (0, 0, 0)),
                            _row_spec(tm, D), _vec_spec(D)],
                  out_specs=[_row_spec(tm, D), _row_spec(tm, D)], vmem_mb=40)(y2, wo, x, gate)


def _loss_bwd(x, target, g):
    T, tm = x.shape[0], 256

    def body(x_ref, t_ref, g_ref, dx_ref, loss_ref, dg_ref):
        @pl.when(pl.program_id(0) == 0)
        def _():
            loss_ref[...] = jnp.zeros_like(loss_ref)
            dg_ref[...] = jnp.zeros_like(dg_ref)

        xv, gv = x_ref[...], g_ref[...]
        r = lax.rsqrt(jnp.mean(xv * xv, axis=-1, keepdims=True) + EPS)
        xn = xv * r
        err = xn * gv - t_ref[...]
        dy = err * (1.0 / D)
        dxn = dy * gv
        dx_ref[...] = r * (dxn - xn * jnp.mean(dxn * xn, axis=-1, keepdims=True))
        dg_ref[...] += jnp.sum(dy * xn, axis=0, keepdims=True)
        loss_ref[...] += (0.5 / D) * jnp.sum(jnp.sum(err * err, axis=1, keepdims=True), axis=0, keepdims=True)

    return _pcall(body, name="loss_bwd", out_shape=[_sds((T, D), f32), _sds((1, 1), f32), _sds((1, D), f32)],
                  grid=(T // tm,), in_specs=[_row_spec(tm, D), _row_spec(tm, D), _vec_spec(D)],
                  out_specs=[_row_spec(tm, D), pl.BlockSpec((1, 1), lambda i: (0, 0)), _vec_spec(D)])(x, target, g)


def _gate_bwd(gin, o, gate):
    T, tm = gin.shape[0], 512

    def body(gin_ref, o_ref, gate_ref, dob_ref, dgate_ref):
        @pl.when(pl.program_id(0) == 0)
        def _():
            dgate_ref[...] = jnp.zeros_like(dgate_ref)

        gv = gin_ref[...]
        dob_ref[...] = (gv * gate_ref[...]).astype(bf16)
        dgate_ref[...] += jnp.sum(gv * o_ref[...], axis=0, keepdims=True)

    return _pcall(body, name="gate_bwd", out_shape=[_sds((T, D), bf16), _sds((1, D), f32)], grid=(T // tm,),
                  in_specs=[_row_spec(tm, D), _row_spec(tm, D), _vec_spec(D)],
                  out_specs=[_row_spec(tm, D), _vec_spec(D)])(gin, o, gate)


def _norm_bwd(x, dh, gin, g, scale):
    T, tm = x.shape[0], 256

    def body(x_ref, dh_ref, gin_ref, g_ref, sc_ref, dx_ref, st_ref):
        @pl.when(pl.program_id(0) == 0)
        def _():
            st_ref[...] = jnp.zeros_like(st_ref)

        xv, gv, dhv = x_ref[...], g_ref[...], dh_ref[...]
        r = lax.rsqrt(jnp.mean(xv * xv, axis=-1, keepdims=True) + EPS)
        xn = xv * r
        da = dhv * (1.0 + sc_ref[...])
        dxn = da * gv
        dx_ref[...] = gin_ref[...] + r * (dxn - xn * jnp.mean(dxn * xn, axis=-1, keepdims=True))
        st_ref[0:1, :] += jnp.sum(dhv, axis=0, keepdims=True)
        st_ref[1:2, :] += jnp.sum(dhv * (xn * gv), axis=0, keepdims=True)
        st_ref[2:3, :] += jnp.sum(da * xn, axis=0, keepdims=True)

    return _pcall(body, name="norm_bwd", out_shape=[_sds((T, D), f32), _sds((8, D), f32)], grid=(T // tm,),
                  in_specs=[_row_spec(tm, D), _row_spec(tm, D), _row_spec(tm, D), _vec_spec(D), _vec_spec(D)],
                  out_specs=[_row_spec(tm, D), _vec_spec(D, 8)])(x, dh, gin, g, scale)


def _cast_place(place, w):
    rows, cols = w.shape
    tr = 256

    def body(place_ref, w_ref, o_ref):
        o_ref[...] = w_ref[...].astype(bf16)

    return _pcall(body, name="cast_place", out_shape=_sds((4, rows, cols), bf16), grid=(rows // tr,), prefetch=1,
                  in_specs=[pl.BlockSpec((tr, cols), lambda i, pr: (i, 0))],
                  out_specs=pl.BlockSpec((None, tr, cols), lambda i, pr: (pr[0], i, 0)))(place, w)


def _rs_add(place, grad, theirs):
    _, rows, cols = theirs.shape
    tr = 256
    spec = pl.BlockSpec((None, tr, cols), lambda s, i, pr: (s, i, 0))

    def body(place_ref, a_ref, b_ref, o_ref):
        o_ref[...] = (a_ref[...].astype(f32) + b_ref[...].astype(f32)).astype(bf16)

    return _pcall(body, name="rs_add", out_shape=_sds(theirs.shape, bf16), grid=(4, rows // tr), prefetch=1,
                  in_specs=[pl.BlockSpec((None, None, tr, cols), lambda s, i, pr: (s, pr[1], i, 0)), spec],
                  out_specs=spec)(place, grad, theirs)


def _rs_sum(place, pairs, slots, stacked, layer, layers):
    _, rows, cols = slots.shape
    tr = 256

    def body(place_ref, p_ref, q_ref, *rest):
        o_ref = rest[-1]
        o_ref[...] = ((p_ref[...].astype(f32) + q_ref[0].astype(f32)) + q_ref[1].astype(f32)) + q_ref[2].astype(f32)

    in_specs = [pl.BlockSpec((None, tr, cols), lambda i, pr: (pr[0], i, 0)),
                pl.BlockSpec((3, tr, cols), lambda i, pr: (0, i, 0))]
    args = [place, pairs, slots]
    if stacked is not None:
        in_specs.append(ANY)
        args.append(stacked)
    return _pcall(body, name="rs_sum", out_shape=_sds((layers, 2, rows, cols), f32), grid=(rows // tr,), prefetch=1,
                  in_specs=in_specs, aliases=None if stacked is None else {3: 0},
                  out_specs=pl.BlockSpec((None, None, tr, cols), lambda i, pr: (layer, pr[1], i, 0)))(*args)


def _adamw_math(w, g, m, v):
    m = ADAM_B1 * m + (1.0 - ADAM_B1) * g
    v = ADAM_B2 * v + (1.0 - ADAM_B2) * jnp.square(g)
    m_hat = m / (1.0 - ADAM_B1 ** ADAM_STEP)
    v_hat = v / (1.0 - ADAM_B2 ** ADAM_STEP)
    delta = -ADAM_LR * (m_hat / (jnp.sqrt(v_hat) + ADAM_EPS) + ADAM_WD * w)
    return delta, m, v


def _adamw(w, g, m, v):
    layers, rows, cols = w.shape
    tr = 128
    spec = pl.BlockSpec((None, tr, cols), lambda l, i: (l, i, 0))

    def body(w_ref, g_ref, m_ref, v_ref, d_ref, mo_ref, vo_ref):
        d_ref[...], mo_ref[...], vo_ref[...] = _adamw_math(w_ref[...], g_ref[...], m_ref[...], v_ref[...])

    return _pcall(body, name="adamw", out_shape=[_sds(w.shape, f32)] * 3, grid=(layers, rows // tr),
                  in_specs=[spec] * 4, out_specs=[spec] * 3)(w, g, m, v)


def _adamw_small(items):
    n = len(items)

    def body(*refs):
        ins, outs = refs[:4 * n], refs[4 * n:]
        for t in range(n):
            w_ref, g_ref, m_ref, v_ref = ins[4 * t:4 * t + 4]
            if len(g_ref.shape) == len(w_ref.shape) + 1:
                g = g_ref[0]
                for b in range(1, g_ref.shape[0]):
                    g = g + g_ref[b]
            else:
                g = g_ref[...]
            d, m, v = _adamw_math(w_ref[...], g, m_ref[...], v_ref[...])
            outs[4 * t][...], outs[4 * t + 1][...], outs[4 * t + 2][...], outs[4 * t + 3][...] = g, d, m, v

    out_shape = [_sds(w.shape, f32) for (w, _, _, _) in items for _ in range(4)]
    flat = [a for it in items for a in it]
    res = _pcall(body, name="adamw_small", out_shape=out_shape, in_specs=[VMEM] * (4 * n),
                 out_specs=[VMEM] * (4 * n))(*flat)
    return [tuple(res[4 * t:4 * t + 4]) for t in range(n)]


NN = ((1,), (0,))
NT = ((1,), (1,))
TN = ((0,), (0,))


def _mm(name, a, b, *, grid, a_spec, b_spec, out_shape, out_spec, dims, acc_k=False, vmem_mb=48):
    def body(a_ref, b_ref, o_ref):
        r = lax.dot_general(a_ref[...], b_ref[...], (dims, ((), ())), preferred_element_type=f32)
        if acc_k:
            @pl.when(pl.program_id(0) == 0)
            def _():
                o_ref[...] = r

            @pl.when(pl.program_id(0) > 0)
            def _():
                o_ref[...] += r
        else:
            o_ref[...] = r.astype(o_ref.dtype)

    return _pcall(body, name=name, out_shape=out_shape, grid=grid, in_specs=[a_spec, b_spec], out_specs=out_spec,
                  vmem_mb=vmem_mb)(a, b)


def _whole(shape):
    return pl.BlockSpec(shape, lambda j: (0,) * len(shape))


def _split_spec(rows, tile, per_split):
    return pl.BlockSpec((None, rows, tile), lambda j: (j // per_split, 0, j % per_split))


class _Proj:
    def __init__(self, n, splits, tile):
        self.n, self.splits, self.tile = n, splits, tile
        self.steps = n // tile
        self.w_per = n // 4 // tile
        self.a_per = n // splits // tile
        assert self.w_per * tile * 4 == n and self.a_per * tile * splits == n

    def fwd(self, hb, wg):
        T = hb.shape[0]
        return _mm("proj_fwd", hb, wg, grid=(self.steps,), a_spec=_whole((T, D)),
                   b_spec=_split_spec(D, self.tile, self.w_per),
                   out_shape=_sds((self.splits, T, self.n // self.splits), f32),
                   out_spec=_split_spec(T, self.tile, self.a_per), dims=NN)

    def dw(self, hb, dp):
        T = hb.shape[0]
        return _mm("proj_dw", hb, dp, grid=(self.steps,), a_spec=_whole((T, D)),
                   b_spec=_split_spec(T, self.tile, self.a_per), out_shape=_sds((4, D, self.n // 4), bf16),
                   out_spec=_split_spec(D, self.tile, self.w_per), dims=TN)

    def dh(self, dp, wg):
        T = dp.shape[1]
        return _mm("proj_dh", dp, wg, grid=(self.steps,), a_spec=_split_spec(T, self.tile, self.a_per),
                   b_spec=_split_spec(D, self.tile, self.w_per), out_shape=_sds((T, D), f32),
                   out_spec=_whole((T, D)), dims=NT, acc_k=True)


EVEN_PROJ = _Proj(7 * D, 7, 256)
ODD_PROJ = _Proj(4 * D, 2, 512)


def _dy_mm(dob, wo):
    T = dob.shape[0]
    return _mm("out_dy", dob, wo, grid=(4,), a_spec=_whole((T, D)),
               b_spec=pl.BlockSpec((None, 512, D), lambda j: (j, 0, 0)), out_shape=_sds((2, T, D), f32),
               out_spec=_split_spec(T, 512, 2), dims=NT)


def _dwo_mm(y2, dob):
    T = dob.shape[0]
    return _mm("out_dw", y2, dob, grid=(4,), a_spec=_split_spec(T, 512, 2), b_spec=_whole((T, D)),
               out_shape=_sds((4, 512, D), bf16), out_spec=pl.BlockSpec((None, 512, D), lambda j: (j, 0, 0)), dims=TN)


def _head_spec(lead, T):
    return pl.BlockSpec((lead, T, HEAD), lambda h: (0, 0, h))


def _head_vec(rows):
    return pl.BlockSpec((rows, HEAD), lambda h: (0, h))


_HEAD_MAT = pl.BlockSpec((None, HEAD, HEAD), lambda h: (h, 0, 0))


def _causal():
    return lax.broadcasted_iota(jnp.int32, (HEAD, HEAD), 0) >= lax.broadcasted_iota(jnp.int32, (HEAD, HEAD), 1)


def _layernorm_head(v):
    mu = jnp.mean(v, axis=-1, keepdims=True)
    d = v - mu
    rstd = lax.rsqrt(jnp.mean(d * d, axis=-1, keepdims=True) + EPS)
    return d * rstd, rstd


def _even_fwd(p7, conv_w, ln_g, ln_b, sgu_w, sgu_bias):
    T, C = p7.shape[1], CHUNK_ROWS

    def body(p_ref, cw_ref, lg_ref, lb_ref, w_ref, b_ref, y_ref):
        w0, w1, w2 = cw_ref[0:1, :], cw_ref[1:2, :], cw_ref[2:3, :]
        wm = jnp.where(_causal(), w_ref[...], 0.0).astype(bf16)
        bias, lg, lb = b_ref[...], lg_ref[...], lb_ref[...]

        def step(i, halo):
            rows = pl.ds(pl.multiple_of(i * C, C), C)
            tt = p_ref[2, rows, :] * p_ref[0, rows, :]
            ext = jnp.concatenate([halo, tt], axis=0)
            cv = w2 * tt + w1 * pltpu.roll(ext, 1, 0)[HALO_CONV:] + w0 * pltpu.roll(ext, 2, 0)[HALO_CONV:]
            y_ref[0, rows, :] = (p_ref[1, rows, :] * cv * _silu(p_ref[3, rows, :])).astype(bf16)
            vhat, _ = _layernorm_head(p_ref[5, rows, :])
            vn = (vhat * lg + lb).astype(bf16)
            mix = jnp.concatenate([jnp.dot(wm, vn[k * HEAD:(k + 1) * HEAD], preferred_element_type=f32) + bias
                                   for k in range(C // HEAD)], axis=0)
            y_ref[1, rows, :] = (p_ref[4, rows, :] * mix * _silu(p_ref[6, rows, :])).astype(bf16)
            return tt[C - HALO_CONV:]

        lax.fori_loop(0, T // C, step, jnp.zeros((HALO_CONV, HEAD), f32))

    return _pcall(body, name="even_fwd", out_shape=_sds((2, T, D), bf16), grid=(NH,),
                  in_specs=[_head_spec(7, T), _head_vec(3), _head_vec(1), _head_vec(1), _HEAD_MAT, _HEAD_MAT],
                  out_specs=_head_spec(2, T), vmem_mb=32)(p7, conv_w, ln_g, ln_b, sgu_w, sgu_bias)


def _even_bwd(p7, dy2, conv_w, ln_g, ln_b, sgu_w, sgu_bias):
    T, C = p7.shape[1], CHUNK_ROWS
    n_chunks = T // C

    def body(p_ref, dy_ref, cw_ref, lg_ref, lb_ref, w_ref, b_ref,
             dp_ref, dcw_ref, dlg_ref, dlb_ref, dw_ref, dms_ref, dcv_s):
        w0, w1, w2 = cw_ref[0:1, :], cw_ref[1:2, :], cw_ref[2:3, :]
        tri = _causal()
        wm = jnp.where(tri, w_ref[...], 0.0).astype(bf16)
        bias, lg, lb = b_ref[...], lg_ref[...], lb_ref[...]
        dw_ref[...] = jnp.zeros_like(dw_ref)
        dms_ref[...] = jnp.zeros_like(dms_ref)

        def fwd_step(i, carry):
            halo, a0, a1, a2, alg, alb = carry
            rows = pl.ds(pl.multiple_of(i * C, C), C)
            ah, ab, ac, az = p_ref[0, rows, :], p_ref[1, rows, :], p_ref[2, rows, :], p_ref[3, rows, :]
            dya = dy_ref[0, rows, :]
            tt = ac * ah
            ext = jnp.concatenate([halo, tt], axis=0)
            t1, t2 = pltpu.roll(ext, 1, 0)[HALO_CONV:], pltpu.roll(ext, 2, 0)[HALO_CONV:]
            cv = w2 * tt + w1 * t1 + w0 * t2
            sa, dsa = _silu_and_grad(az)
            g1 = dya * sa
            dp_ref[1, rows, :] = (g1 * cv).astype(bf16)
            dp_ref[3, rows, :] = (dya * ab * cv * dsa).astype(bf16)
            dcv = g1 * ab
            dcv_s[rows, :] = dcv
            a2 = a2 + jnp.sum(dcv * tt, axis=0, keepdims=True)
            a1 = a1 + jnp.sum(dcv * t1, axis=0, keepdims=True)
            a0 = a0 + jnp.sum(dcv * t2, axis=0, keepdims=True)

            u, zb, dyb = p_ref[4, rows, :], p_ref[6, rows, :], dy_ref[1, rows, :]
            vhat, rstd = _layernorm_head(p_ref[5, rows, :])
            vn = (vhat * lg + lb).astype(bf16)
            sb, dsb = _silu_and_grad(zb)
            mix = jnp.concatenate([jnp.dot(wm, vn[k * HEAD:(k + 1) * HEAD], preferred_element_type=f32) + bias
                                   for k in range(C // HEAD)], axis=0)
            dp_ref[4, rows, :] = (dyb * mix * sb).astype(bf16)
            dp_ref[6, rows, :] = (dyb * u * mix * dsb).astype(bf16)
            dmix = dyb * u * sb
            dvn_parts = []
            for k in range(C // HEAD):
                dm = dmix[k * HEAD:(k + 1) * HEAD]
                dmb = dm.astype(bf16)
                dvn_parts.append(lax.dot_general(wm, dmb, (TN, ((), ())), preferred_element_type=f32))
                dw_ref[...] += lax.dot_general(dmb, vn[k * HEAD:(k + 1) * HEAD], (NT, ((), ())),
                                               preferred_element_type=f32)
                dms_ref[...] += dm
            dvn = jnp.concatenate(dvn_parts, axis=0)
            alg = alg + jnp.sum(dvn * vhat, axis=0, keepdims=True)
            alb = alb + jnp.sum(dvn, axis=0, keepdims=True)
            dvh = dvn * lg
            dv = rstd * (dvh - jnp.mean(dvh, axis=-1, keepdims=True)
                         - vhat * jnp.mean(dvh * vhat, axis=-1, keepdims=True))
            dp_ref[5, rows, :] = dv.astype(bf16)
            return tt[C - HALO_CONV:], a0, a1, a2, alg, alb

        zrow = jnp.zeros((1, HEAD), f32)
        _, a0, a1, a2, alg, alb = lax.fori_loop(
            0, n_chunks, fwd_step, (jnp.zeros((HALO_CONV, HEAD), f32), zrow, zrow, zrow, zrow, zrow))
        dcw_ref[0:1, :], dcw_ref[1:2, :], dcw_ref[2:3, :] = a0, a1, a2
        dlg_ref[...], dlb_ref[...] = alg, alb
        dw_ref[...] = jnp.where(tri, dw_ref[...], 0.0)

        def bwd_step(k, halo):
            rows = pl.ds(pl.multiple_of((n_chunks - 1 - k) * C, C), C)
            dcv = dcv_s[rows, :]
            ext = jnp.concatenate([dcv, halo], axis=0)
            n1 = pltpu.roll(ext, C + HALO_CONV - 1, 0)[:C]
            n2 = pltpu.roll(ext, C + HALO_CONV - 2, 0)[:C]
            dtt = w2 * dcv + w1 * n1 + w0 * n2
            dp_ref[2, rows, :] = (dtt * p_ref[0, rows, :]).astype(bf16)
            dp_ref[0, rows, :] = (dtt * p_ref[2, rows, :]).astype(bf16)
            return dcv[:HALO_CONV]

        lax.fori_loop(0, n_chunks, bwd_step, jnp.zeros((HALO_CONV, HEAD), f32))

    out_shape = [_sds((7, T, D), bf16), _sds((3, D), f32), _sds((1, D), f32), _sds((1, D), f32),
                 _sds((NH, HEAD, HEAD), f32), _sds((NH, HEAD, HEAD), f32)]
    return _pcall(body, name="even_bwd", out_shape=out_shape, grid=(NH,),
                  in_specs=[_head_spec(7, T), _head_spec(2, T), _head_vec(3), _head_vec(1), _head_vec(1),
                            _HEAD_MAT, _HEAD_MAT],
                  out_specs=[_head_spec(7, T), _head_vec(3), _head_vec(1), _head_vec(1), _HEAD_MAT, _HEAD_MAT],
                  scratch=[pltpu.VMEM((T, HEAD), f32)], vmem_mb=48)(p7, dy2, conv_w, ln_g, ln_b, sgu_w, sgu_bias)


def _window_sum(ext, win, towards_past):
    n, k, s = ext.shape[0], 1, ext
    while k < win:
        s = s + pltpu.roll(s, k if towards_past else n - k, 0)
        k *= 2
    return s


def _pool_count(i, C, win):
    t = i * C + lax.broadcasted_iota(jnp.int32, (C, 1), 0)
    return jnp.minimum(t + 1, win).astype(f32)


def _group_specs(T):
    p_spec = pl.BlockSpec((None, T, GC), lambda g: (0, 0, g))
    z_spec = pl.BlockSpec((None, T, GC), lambda g: (1, 0, g))
    pw_spec = pl.BlockSpec((4, GC // 4, GC), lambda g: (0, g, 0))
    ps_spec = pl.BlockSpec((1, GC), lambda g: (0, g))
    y_spec = pl.BlockSpec((None, T, GC), lambda g: (g // 2, 0, g % 2))
    return p_spec, z_spec, pw_spec, ps_spec, y_spec


def _odd_fwd(p2, pool_wg, pool_scale):
    T, C = p2.shape[1], CHUNK_ROWS
    p_spec, z_spec, pw_spec, ps_spec, y_spec = _group_specs(T)

    def body(p_ref, z_ref, pw_ref, ps_ref, y_ref):
        pw, ps = pw_ref[...].reshape(GC, GC), ps_ref[...]

        def run(win):
            def step(i, halo):
                rows = pl.ds(pl.multiple_of(i * C, C), C)
                p = p_ref[rows, :]
                s = _window_sum(jnp.concatenate([halo, p], axis=0), win, True)[HALO_POOL:]
                pooled = s / _pool_count(i, C, win) - p
                ypre = jnp.dot(pooled.astype(bf16), pw, preferred_element_type=f32)
                y_ref[rows, :] = (ypre * ps * _silu(z_ref[rows, :])).astype(bf16)
                return p[C - HALO_POOL:]

            lax.fori_loop(0, T // C, step, jnp.zeros((HALO_POOL, GC), f32))

        for gi, win in enumerate(WINDOWS):
            pl.when(pl.program_id(0) == gi)(functools.partial(run, win))

    return _pcall(body, name="odd_fwd", out_shape=_sds((2, T, D), bf16), grid=(len(WINDOWS),),
                  in_specs=[p_spec, z_spec, pw_spec, ps_spec], out_specs=y_spec, vmem_mb=40)(p2, p2, pool_wg, pool_scale)


def _odd_bwd(p2, dy2, pool_wg, pool_scale):
    T, C = p2.shape[1], CHUNK_ROWS
    n_chunks = T // C
    p_spec, z_spec, pw_spec, ps_spec, y_spec = _group_specs(T)

    def body(p_ref, z_ref, dy_ref, pw_ref, ps_ref, dp_ref, dpw_ref, dps_ref, q_s, acc_s):
        pw, ps = pw_ref[...].reshape(GC, GC), ps_ref[...]

        def run(win):
            acc_s[...] = jnp.zeros_like(acc_s)

            def fwd_step(i, carry):
                halo, aps = carry
                rows = pl.ds(pl.multiple_of(i * C, C), C)
                p, z, dy = p_ref[rows, :], z_ref[rows, :], dy_ref[rows, :]
                cnt = _pool_count(i, C, win)
                s = _window_sum(jnp.concatenate([halo, p], axis=0), win, True)[HALO_POOL:]
                pb = (s / cnt - p).astype(bf16)
                ypre = jnp.dot(pb, pw, preferred_element_type=f32)
                sz, dsz = _silu_and_grad(z)
                aps = aps + jnp.sum(dy * ypre * sz, axis=0, keepdims=True)
                dp_ref[1, rows, :] = (dy * ypre * ps * dsz).astype(bf16)
                dyp = (dy * ps * sz).astype(bf16)
                acc_s[...] += lax.dot_general(pb, dyp, (TN, ((), ())), preferred_element_type=f32)
                dpool = lax.dot_general(dyp, pw, (NT, ((), ())), preferred_element_type=f32)
                q_s[rows, :] = dpool / cnt
                return p[C - HALO_POOL:], aps

            _, aps = lax.fori_loop(0, n_chunks, fwd_step, (jnp.zeros((HALO_POOL, GC), f32), jnp.zeros((1, GC), f32)))
            dps_ref[...] = aps
            dpw_ref[...] = acc_s[...].reshape(4, GC // 4, GC).astype(bf16)

            def bwd_step(k, halo):
                i = n_chunks - 1 - k
                rows = pl.ds(pl.multiple_of(i * C, C), C)
                q = q_s[rows, :]
                s = _window_sum(jnp.concatenate([q, halo], axis=0), win, False)[:C]
                dp_ref[0, rows, :] = (s - q * _pool_count(i, C, win)).astype(bf16)
                return q[:HALO_POOL]

            lax.fori_loop(0, n_chunks, bwd_step, jnp.zeros((HALO_POOL, GC), f32))

        for gi, win in enumerate(WINDOWS):
            pl.when(pl.program_id(0) == gi)(functools.partial(run, win))

    out_shape = [_sds((2, T, 2 * D), bf16), _sds((4, GC, GC), bf16), _sds((1, 2 * D), f32)]
    return _pcall(body, name="odd_bwd", out_shape=out_shape, grid=(len(WINDOWS),),
                  in_specs=[p_spec, z_spec, y_spec, pw_spec, ps_spec],
                  out_specs=[pl.BlockSpec((2, T, GC), lambda g: (0, 0, g)), pw_spec, ps_spec],
                  scratch=[pltpu.VMEM((T, GC), f32), pltpu.VMEM((GC, GC), f32)], vmem_mb=52)(
                      p2, p2, dy2, pool_wg, pool_scale)


def _ada_fwd(c_all, ada_w):
    cols = ada_w.shape[2]

    def body(c_ref, w_ref, o_ref):
        o_ref[...] = jnp.dot(_silu(c_ref[...]), w_ref[...], preferred_element_type=f32,
                             precision=lax.Precision.HIGHEST)

    return _pcall(body, name="ada_fwd", out_shape=_sds((4, N_DEV, cols), f32), grid=(4,),
                  in_specs=[pl.BlockSpec((N_DEV, D), lambda i: (0, 0)), pl.BlockSpec((None, D, cols), lambda i: (i, 0, 0))],
                  out_specs=pl.BlockSpec((None, N_DEV, cols), lambda i: (i, 0, 0)))(c_all, ada_w)


def _ada_bwd(c_all_t, dmod, w, m, v):
    cols, tr = w.shape[2], 256
    spec = pl.BlockSpec((None, tr, cols), lambda l, i: (l, i, 0))

    def body(c_ref, dm_ref, w_ref, m_ref, v_ref, g_ref, d_ref, mo_ref, vo_ref):
        sc = _silu(c_ref[...])
        g = sc[:, 0:1] * dm_ref[0:1, :]
        for b in range(1, N_DEV):
            g = g + sc[:, b:b + 1] * dm_ref[b:b + 1, :]
        g_ref[...] = g
        d_ref[...], mo_ref[...], vo_ref[...] = _adamw_math(w_ref[...], g, m_ref[...], v_ref[...])

    return _pcall(body, name="ada_bwd", out_shape=[_sds(w.shape, f32)] * 4, grid=(4, D // tr),
                  in_specs=[pl.BlockSpec((tr, N_DEV), lambda l, i: (i, 0)),
                            pl.BlockSpec((None, N_DEV, cols), lambda l, i: (l, 0, 0)), spec, spec, spec],
                  out_specs=[spec] * 4)(c_all_t, dmod, w, m, v)


def _local_step(x, target, mods, norm_g, final_g, even_w, odd_w):
    depth = len(mods)
    saved = []
    for i in range(depth):
        shift, scale, gate = mods[i]
        hb = _hnorm(x, norm_g[i:i + 1], shift, scale)
        if i % 2 == 0:
            w_in, conv_w, ln_g, ln_b, sgu_w, sgu_b, w_out = even_w[i // 2]
            bias = jnp.broadcast_to(sgu_b[:, :, None], (NH, HEAD, HEAD))
            p = EVEN_PROJ.fwd(hb, w_in)
            y2 = _even_fwd(p, conv_w, ln_g, ln_b, sgu_w, bias)
        else:
            w_in, pool_w, pool_scale, w_out = odd_w[i // 2]
            p = ODD_PROJ.fwd(hb, w_in)
            y2 = _odd_fwd(p, pool_w, pool_scale)
        x_next, o = _out_proj(y2, w_out.reshape(2, D, D), x, gate)
        saved.append((x, hb, p, y2, o))
        x = x_next

    gin, loss, dfinal_g = _loss_bwd(x, target, final_g)
    layer_grads, dmod, dnorm_g = [None] * depth, [None] * depth, [None] * depth
    for i in reversed(range(depth)):
        shift, scale, gate = mods[i]
        x_in, hb, p, y2, o = saved[i]
        dob, dgate = _gate_bwd(gin, o, gate)
        if i % 2 == 0:
            w_in, conv_w, ln_g, ln_b, sgu_w, sgu_b, w_out = even_w[i // 2]
            bias = jnp.broadcast_to(sgu_b[:, :, None], (NH, HEAD, HEAD))
            dy2 = _dy_mm(dob, w_out)
            dp, dconv, dlg, dlb, dsw, dms = _even_bwd(p, dy2, conv_w, ln_g, ln_b, sgu_w, bias)
            proj = EVEN_PROJ
            grads = dict(conv_w=dconv, ln_g=dlg, ln_b=dlb, sgu_w=dsw, sgu_b=jnp.sum(dms, axis=-1))
        else:
            w_in, pool_w, pool_scale, w_out = odd_w[i // 2]
            dy2 = _dy_mm(dob, w_out)
            dp, dpw, dps = _odd_bwd(p, dy2, pool_w, pool_scale)
            proj = ODD_PROJ
            grads = dict(pool_w=dpw, pool_scale=dps)
        grads["w_out"] = _dwo_mm(y2, dob)
        grads["w_in"] = proj.dw(hb, dp)
        dh = proj.dh(dp, w_in)
        gin, stats = _norm_bwd(x_in, dh, gin, norm_g[i:i + 1], scale)
        layer_grads[i] = grads
        dmod[i] = jnp.concatenate([stats[0:2], dgate], axis=0)
        dnorm_g[i] = stats[2:3]
    return loss, gin, layer_grads, jnp.stack(dmod), jnp.concatenate(dnorm_g, axis=0), dfinal_g


def _pack_rows(parts):
    rows = [p.reshape(-1, LANES) for p in parts]
    total = sum(r.shape[0] for r in rows)
    padded = -(-total // (8 * N_DEV)) * (8 * N_DEV)
    if padded > total:
        rows.append(jnp.zeros((padded - total, LANES), f32))
    return jnp.concatenate(rows, axis=0)


def _unpack_rows(buf, shapes):
    out, r = [], 0
    for shp in shapes:
        n = 1
        for d in shp:
            n *= d
        out.append(buf[r:r + n // LANES].reshape(shp))
        r += n // LANES
    return out


def kernel(x, c, norm_g, ada_w, ada_b, ab_w_in, ab_conv_w, ab_ln_g, ab_ln_b, ab_sgu_w, ab_sgu_b, ab_w_out, c_w_in, c_pool_w, c_pool_scale, c_w_out, final_g, loss_target, m_norm_g, m_ada_w, m_ada_b, m_ab_w_in, m_ab_conv_w, m_ab_ln_g, m_ab_ln_b, m_ab_sgu_w, m_ab_sgu_b, m_ab_w_out, m_c_w_in, m_c_pool_w, m_c_pool_scale, m_c_w_out, m_final_g, v_norm_g, v_ada_w, v_ada_b, v_ab_w_in, v_ab_conv_w, v_ab_ln_g, v_ab_ln_b, v_ab_sgu_w, v_ab_sgu_b, v_ab_w_out, v_c_w_in, v_c_pool_w, v_c_pool_scale, v_c_w_out, v_final_g):
    ix, iy, ic = _place()
    chip, dev = 2 * ix + iy, 4 * ix + 2 * iy + ic
    n_even, n_odd = ab_w_in.shape[0], c_w_in.shape[0]
    depth = n_even + n_odd
    acols = ada_w.shape[2]

    c_all = _gather8(c, "gather_c").reshape(N_DEV, D)
    modp = _ada_fwd(c_all, ada_w)
    modg = _gather8(modp, "gather_mod")
    mod_rows = lax.dynamic_index_in_dim(modg[0::2], dev, axis=2, keepdims=False)
    mod = jnp.transpose(mod_rows, (1, 0, 2)).reshape(depth, 3 * D) + ada_b
    mods = [(mod[i:i + 1, 0:D], mod[i:i + 1, D:2 * D], mod[i:i + 1, 2 * D:3 * D]) for i in range(depth)]

    place = jnp.stack([chip, ic]).astype(jnp.int32)
    big = []
    for j in range(n_even):
        big += [ab_w_in[j], ab_w_out[j]]
    for j in range(n_odd):
        big += [c_w_in[j], c_pool_w[j].reshape(GC, GC), c_w_out[j]]
    placed = [_cast_place(place, w) for w in big]
    placed = [p.reshape(4, 2, p.shape[1] // 2, p.shape[2]) for p in placed]
    gathered = _allgather_weights(placed, "allgather_weights")
    gathered = [g.reshape(4, 2 * g.shape[2], g.shape[3]) for g in gathered]

    def shard_cols(a, width):
        return lax.dynamic_slice_in_dim(a, chip * width, width, axis=a.ndim - 1)

    small_sharded = jnp.concatenate([ab_conv_w.reshape(1, -1), c_pool_scale.reshape(1, -1)], axis=1)
    small_all = _gather8(small_sharded, "gather_small")[0::2, 0]
    n_conv = ab_conv_w.size
    conv_all = small_all[:, :n_conv].reshape(4, n_even, 3, D // 4)
    conv_full = jnp.transpose(conv_all, (1, 2, 0, 3)).reshape(n_even, 3, D)
    scale_all = small_all[:, n_conv:].reshape(4, n_odd, 2 * D // 4)
    scale_full = jnp.transpose(scale_all, (1, 0, 2)).reshape(n_odd, 2 * D)

    even_w = [(gathered[2 * j], conv_full[j], ab_ln_g[j:j + 1], ab_ln_b[j:j + 1], ab_sgu_w[j], ab_sgu_b[j],
               gathered[2 * j + 1]) for j in range(n_even)]
    base = 2 * n_even
    odd_w = [(gathered[base + 3 * j], gathered[base + 3 * j + 1], scale_full[j:j + 1], gathered[base + 3 * j + 2])
             for j in range(n_odd)]

    loss, grad_x, layer_grads, dmod, dnorm_g, dfinal_g = _local_step(
        x[0], loss_target[0], mods, norm_g, final_g.reshape(1, D), even_w, odd_w)
    loss = lax.psum(loss[0, 0], ("x", "y", "c"))

    names = []
    for j in range(n_even):
        names += [(2 * j, "w_in"), (2 * j, "w_out")]
    for j in range(n_odd):
        names += [(2 * j + 1, "w_in"), (2 * j + 1, "pool_w"), (2 * j + 1, "w_out")]
    big_g = [layer_grads[i][k] for i, k in names]
    big_g = [g.reshape(4, 2, g.shape[1] // 2, g.shape[2]) for g in big_g]
    theirs = _rs_pair_exchange(big_g, "rs_pair_exchange")
    pairs = [_rs_add(place, a, b) for a, b in zip(big_g, theirs)]
    slots = _rs_chip_exchange(pairs, "rs_chip_exchange")
    members = [[2 * j for j in range(n_even)], [2 * j + 1 for j in range(n_even)],
               [base + 3 * j for j in range(n_odd)], [base + 3 * j + 1 for j in range(n_odd)],
               [base + 3 * j + 2 for j in range(n_odd)]]
    groups = []
    for tensors in members:
        stacked = None
        for layer, t in enumerate(tensors):
            stacked = _rs_sum(place, pairs[t], slots[t], stacked, layer, len(tensors))
        groups.append(stacked)
    g_ab_w_in, g_ab_w_out, g_c_w_in, g_c_pool_w, g_c_w_out = [
        g.reshape(g.shape[0], 2 * g.shape[2], g.shape[3]) for g in _rs_half_exchange(groups, "rs_half_exchange")]

    small_parts = [dnorm_g, dfinal_g,
                   jnp.stack([layer_grads[2 * j]["conv_w"] for j in range(n_even)]),
                   jnp.concatenate([layer_grads[2 * j]["ln_g"] for j in range(n_even)], axis=0),
                   jnp.concatenate([layer_grads[2 * j]["ln_b"] for j in range(n_even)], axis=0),
                   jnp.stack([layer_grads[2 * j]["sgu_w"] for j in range(n_even)]),
                   jnp.stack([layer_grads[2 * j]["sgu_b"] for j in range(n_even)]),
                   jnp.concatenate([layer_grads[2 * j + 1]["pool_scale"] for j in range(n_odd)], axis=0)]
    small_shapes = [p.shape for p in small_parts]
    reduced = _allreduce8(_pack_rows(small_parts), "allreduce_small")
    g_norm_g, g_final_g, g_conv_full, g_ln_g, g_ln_b, g_sgu_w, g_sgu_b, g_scale_full = _unpack_rows(reduced, small_shapes)
    g_conv = shard_cols(g_conv_full, D // 4)
    g_scale = shard_cols(g_scale_full, 2 * D // 4)
    dmod_all = _gather8(dmod.reshape(depth * 3 * D // LANES, LANES), "gather_dmod").reshape(N_DEV, depth, 3 * D)

    def two_d(a):
        return a.reshape(-1, a.shape[-1])

    small = [(norm_g, g_norm_g, m_norm_g, v_norm_g),
             (ada_b, dmod_all, m_ada_b, v_ada_b),
             (two_d(ab_conv_w), two_d(g_conv), two_d(m_ab_conv_w), two_d(v_ab_conv_w)),
             (ab_ln_g, g_ln_g, m_ab_ln_g, v_ab_ln_g),
             (ab_ln_b, g_ln_b, m_ab_ln_b, v_ab_ln_b),
             (two_d(ab_sgu_w), two_d(g_sgu_w), two_d(m_ab_sgu_w), two_d(v_ab_sgu_w)),
             (two_d(ab_sgu_b), two_d(g_sgu_b), two_d(m_ab_sgu_b), two_d(v_ab_sgu_b)),
             (c_pool_scale, g_scale, m_c_pool_scale, v_c_pool_scale),
             (final_g.reshape(1, D), g_final_g, m_final_g.reshape(1, D), v_final_g.reshape(1, D))]
    small_res = _adamw_small(small)
    small_shapes_out = [norm_g.shape, ada_b.shape, ab_conv_w.shape, ab_ln_g.shape, ab_ln_b.shape, ab_sgu_w.shape,
                        ab_sgu_b.shape, c_pool_scale.shape, final_g.shape]
    (r_norm_g, r_ada_b, r_conv, r_ln_g, r_ln_b, r_sgu_w, r_sgu_b, r_scale, r_final_g) = [
        tuple(a.reshape(shp) for a in res) for res, shp in zip(small_res, small_shapes_out)]

    dmod_cols = jnp.transpose(shard_cols(dmod_all, acols), (1, 0, 2))
    r_ada_w = _ada_bwd(c_all.T, dmod_cols, ada_w, m_ada_w, v_ada_w)

    def big_adamw(w, g, m, v):
        shp = w.shape
        w3, m3, v3 = (a.reshape(g.shape) for a in (w, m, v))
        d, mo, vo = _adamw(w3, g, m3, v3)
        return tuple(a.reshape(shp) for a in (g, d, mo, vo))

    r_ab_w_in = big_adamw(ab_w_in, g_ab_w_in, m_ab_w_in, v_ab_w_in)
    r_ab_w_out = big_adamw(ab_w_out, g_ab_w_out, m_ab_w_out, v_ab_w_out)
    r_c_w_in = big_adamw(c_w_in, g_c_w_in, m_c_w_in, v_c_w_in)
    r_c_pool_w = big_adamw(c_pool_w, g_c_pool_w, m_c_pool_w, v_c_pool_w)
    r_c_w_out = big_adamw(c_w_out, g_c_w_out, m_c_w_out, v_c_w_out)

    order = [r_norm_g, r_ada_w, r_ada_b, r_ab_w_in, r_conv, r_ln_g, r_ln_b, r_sgu_w, r_sgu_b, r_ab_w_out,
             r_c_w_in, r_c_pool_w, r_scale, r_c_w_out, r_final_g]
    outs = [loss, grad_x[None]]
    for field in range(4):
        outs += [r[field] for r in order]
    return tuple(outs)
```

```python
import functools

import jax
import jax.numpy as jnp
from jax import lax
from jax.experimental import pallas as pl
from jax.experimental.pallas import tpu as pltpu

f32, bf16 = jnp.float32, jnp.bfloat16

D = 1024
HEAD = 128
NH = 8
WINDOWS = (2, 4, 8, 16)
GC = 512
EPS = 1e-6
HALO_CONV = 8
HALO_POOL = 16
CHUNK_ROWS = 256
N_DEV = 8
LANES = 128

ADAM_LR, ADAM_B1, ADAM_B2, ADAM_EPS, ADAM_WD, ADAM_STEP = 0.001, 0.9, 0.999, 1e-08, 0.01, 10

MESH = pl.DeviceIdType.MESH
ANY = pl.BlockSpec(memory_space=pl.ANY)
VMEM = pl.BlockSpec(memory_space=pltpu.VMEM)
MIB = 2 ** 20


def _pcall(body, *, name, out_shape, grid=None, in_specs=None, out_specs=None, scratch=(), vmem_mb=None,
           aliases=None, prefetch=0):
    kw = {}
    if prefetch:
        kw["grid_spec"] = pltpu.PrefetchScalarGridSpec(num_scalar_prefetch=prefetch, grid=grid, in_specs=in_specs,
                                                       out_specs=out_specs, scratch_shapes=list(scratch))
    else:
        if grid is not None:
            kw["grid"] = grid
        if in_specs is not None:
            kw["in_specs"] = in_specs
        if out_specs is not None:
            kw["out_specs"] = out_specs
        if scratch:
            kw["scratch_shapes"] = list(scratch)
    if aliases:
        kw["input_output_aliases"] = aliases
    params = pltpu.CompilerParams(vmem_limit_bytes=None if vmem_mb is None else vmem_mb * MIB)
    return pl.pallas_call(body, name=name, out_shape=out_shape, compiler_params=params, **kw)


def _sds(shape, dtype):
    return jax.ShapeDtypeStruct(tuple(shape), dtype)


def _silu(z):
    return z * jax.nn.sigmoid(z)


def _silu_and_grad(z):
    s = jax.nn.sigmoid(z)
    return z * s, s * (1.0 + z * (1.0 - s))


def _place():
    return lax.axis_index("x"), lax.axis_index("y"), lax.axis_index("c")


def _gather8(blk, name):
    def body(x_ref, o_ref, ssem, rsem):
        x, y, c = _place()
        me = 4 * x + 2 * y + c
        o_ref[me] = x_ref[...]
        sends = []
        for k in range(1, N_DEV):
            px = 1 - x if k & 4 else x
            py = 1 - y if k & 2 else y
            pc = 1 - c if k & 1 else c
            cp = pltpu.make_async_remote_copy(src_ref=x_ref, dst_ref=o_ref.at[me], send_sem=ssem.at[k - 1],
                                              recv_sem=rsem.at[k - 1], device_id=(px, py, pc), device_id_type=MESH)
            cp.start()
            sends.append((cp, 4 * px + 2 * py + pc))
        for k, (cp, peer) in enumerate(sends):
            pltpu.make_async_remote_copy(src_ref=x_ref, dst_ref=o_ref.at[peer], send_sem=ssem.at[k],
                                         recv_sem=rsem.at[k], device_id=(x, y, c), device_id_type=MESH).wait_recv()
        for cp, _ in sends:
            cp.wait_send()

    return _pcall(body, name=name, out_shape=_sds((N_DEV,) + blk.shape, blk.dtype), in_specs=[VMEM], out_specs=VMEM,
                  scratch=[pltpu.SemaphoreType.DMA((N_DEV - 1,)), pltpu.SemaphoreType.DMA((N_DEV - 1,))])(blk)


def _allreduce8(buf, name):
    rows = buf.shape[0]
    rb = rows // N_DEV
    assert rb * N_DEV == rows and rb % 8 == 0

    def body(x_ref, o_ref, stage, ssem, rsem):
        x, y, c = _place()
        me = 4 * x + 2 * y + c
        peers = []
        for k in range(1, N_DEV):
            px = 1 - x if k & 4 else x
            py = 1 - y if k & 2 else y
            pc = 1 - c if k & 1 else c
            peers.append(((px, py, pc), 4 * px + 2 * py + pc))

        def blk(ref, idx):
            return ref.at[pl.ds(pl.multiple_of(idx * rb, 8), rb), :]

        def copy(phase, k, src, dst, dev):
            return pltpu.make_async_remote_copy(src_ref=src, dst_ref=dst, send_sem=ssem.at[phase, k],
                                                recv_sem=rsem.at[phase, k], device_id=dev, device_id_type=MESH)

        stage[me] = blk(x_ref, me)[...]
        scatter = [copy(0, k, blk(x_ref, pidx), stage.at[me], dev) for k, (dev, pidx) in enumerate(peers)]
        for cp in scatter:
            cp.start()
        for k, (dev, pidx) in enumerate(peers):
            copy(0, k, blk(x_ref, pidx), stage.at[pidx], dev).wait_recv()
        total = stage[0]
        for j in range(1, N_DEV):
            total = total + stage[j]
        blk(o_ref, me)[...] = total
        gather = [copy(1, k, blk(o_ref, me), blk(o_ref, me), dev) for k, (dev, pidx) in enumerate(peers)]
        for cp in gather:
            cp.start()
        for k, (dev, pidx) in enumerate(peers):
            copy(1, k, blk(o_ref, pidx), blk(o_ref, pidx), dev).wait_recv()
        for cp in scatter + gather:
            cp.wait_send()

    return _pcall(body, name=name, out_shape=_sds(buf.shape, f32), in_specs=[VMEM], out_specs=VMEM,
                  scratch=[pltpu.VMEM((N_DEV, rb, LANES), f32), pltpu.SemaphoreType.DMA((2, N_DEV - 1)),
                           pltpu.SemaphoreType.DMA((2, N_DEV - 1))])(buf)


def _other_chips(x, y):
    return [((1 - x, y), 2 * (1 - x) + y), ((x, 1 - y), 2 * x + (1 - y)), ((1 - x, 1 - y), 2 * (1 - x) + (1 - y))]


HBM = pl.BlockSpec(memory_space=pltpu.HBM)
SEM = pl.BlockSpec(memory_space=pltpu.SEMAPHORE)
EFFECT = pltpu.SideEffectType.DATAFLOW_SIDE_EFFECTING


def _in_hbm(a):
    return pltpu.with_memory_space_constraint(a, pltpu.HBM)


def _ag_start(layers, name):
    flat = [t for lay in layers for t in lay]
    n, nl = len(flat), len(layers)

    def body(*refs):
        src = refs[:n]
        sems = refs[n:n + 2 * nl]
        token = refs[-1]
        x, y, c = _place()
        s_me = 2 * x + y
        t = 0
        for i, lay in enumerate(layers):
            for k in range(len(lay)):
                for j, ((px, py), _) in enumerate(_other_chips(x, y)):
                    pltpu.make_async_remote_copy(src_ref=src[t].at[s_me, c], dst_ref=src[t].at[s_me, c],
                                                 send_sem=sems[2 * i].at[3 * k + j], recv_sem=sems[2 * i + 1].at[3 * k + j],
                                                 device_id=(px, py, c), device_id_type=MESH).start()
                t += 1
        token[...] = jnp.zeros_like(token)

    sem_shapes = [pltpu.SemaphoreType.DMA((3 * len(lay),)) for lay in layers for _ in range(2)]
    out_shape = sem_shapes + [pltpu.HBM(t.shape, t.dtype) for t in flat] + [_sds((8, LANES), f32)]
    outs = pl.pallas_call(
        body, name=name, out_shape=out_shape, in_specs=[HBM] * n, out_specs=[SEM] * (2 * nl) + [HBM] * n + [VMEM],
        input_output_aliases={t: 2 * nl + t for t in range(n)},
        compiler_params=pltpu.CompilerParams(has_side_effects=EFFECT))(*[_in_hbm(t) for t in flat])
    sems = [(outs[2 * i], outs[2 * i + 1]) for i in range(nl)]
    thru, t = [], 2 * nl
    for lay in layers:
        thru.append(list(outs[t:t + len(lay)]))
        t += len(lay)
    return sems, thru, outs[-1]


def _ag_wait(inflight, sems, after, name):
    n = len(inflight)

    def body(*refs):
        src, ssem, rsem = refs[:n], refs[n], refs[n + 1]
        x, y, c = _place()
        s_me = 2 * x + y
        for k in range(n):
            for j, (_, s_p) in enumerate(_other_chips(x, y)):
                cp = pltpu.make_async_remote_copy(src_ref=src[k].at[s_me, c], dst_ref=src[k].at[s_p, c],
                                                  send_sem=ssem.at[3 * k + j], recv_sem=rsem.at[3 * k + j],
                                                  device_id=(x, y, c), device_id_type=MESH)
                cp.wait_send()
                cp.wait_recv()

    return pl.pallas_call(
        body, name=name, out_shape=[pltpu.HBM(t.shape, t.dtype) for t in inflight],
        in_specs=[HBM] * n + [SEM, SEM, ANY], out_specs=[HBM] * n, input_output_aliases={t: t for t in range(n)},
        compiler_params=pltpu.CompilerParams(has_side_effects=EFFECT))(*inflight, sems[0], sems[1], after)


def _ag_forward(arrived, name):
    n = len(arrived)

    def body(*refs):
        o = refs[n:2 * n]
        ssem, rsem = refs[2 * n:]
        x, y, c = _place()

        def copy(t, j, s, half, dev):
            return pltpu.make_async_remote_copy(src_ref=o[t].at[s, c], dst_ref=o[t].at[s, half], send_sem=ssem.at[t, j],
                                                recv_sem=rsem.at[t, j], device_id=dev, device_id_type=MESH)

        chips = _other_chips(x, y)
        sends = [copy(t, j, s_p, c, (x, y, 1 - c)) for t in range(n) for j, (_, s_p) in enumerate(chips)]
        for cp in sends:
            cp.start()
        for t in range(n):
            for j, (_, s_p) in enumerate(chips):
                copy(t, j, s_p, 1 - c, (x, y, c)).wait_recv()
        for cp in sends:
            cp.wait_send()

    return _pcall(body, name=name, out_shape=[_sds(p.shape, bf16) for p in arrived], in_specs=[ANY] * n,
                  out_specs=[ANY] * n, aliases={t: t for t in range(n)},
                  scratch=[pltpu.SemaphoreType.DMA((n, 3)), pltpu.SemaphoreType.DMA((n, 3))])(*arrived)


def _rs_pair_exchange(grads, name):
    n = len(grads)

    def body(*refs):
        g, theirs = refs[:n], refs[n:2 * n]
        ssem, rsem = refs[2 * n:]
        x, y, c = _place()
        sends = [pltpu.make_async_remote_copy(src_ref=g[t].at[:, 1 - c], dst_ref=theirs[t], send_sem=ssem.at[t],
                                              recv_sem=rsem.at[t], device_id=(x, y, 1 - c), device_id_type=MESH)
                 for t in range(n)]
        for cp in sends:
            cp.start()
        for cp in sends:
            cp.wait()

    half = [_sds((4,) + gr.shape[2:], bf16) for gr in grads]
    return _pcall(body, name=name, out_shape=half, in_specs=[ANY] * n, out_specs=[ANY] * n,
                  scratch=[pltpu.SemaphoreType.DMA((n,)), pltpu.SemaphoreType.DMA((n,))])(*grads)


def _rs_chip_start(pairs, name):
    n = len(pairs)

    def body(*refs):
        p, q = refs[:n], refs[n:2 * n]
        ssem, rsem, token = refs[2 * n], refs[2 * n + 1], refs[-1]
        x, y, c = _place()
        for t in range(n):
            for j, ((px, py), s_p) in enumerate(_other_chips(x, y)):
                pltpu.make_async_remote_copy(src_ref=p[t].at[s_p], dst_ref=q[t].at[j], send_sem=ssem.at[3 * t + j],
                                             recv_sem=rsem.at[3 * t + j], device_id=(px, py, c), device_id_type=MESH).start()
        token[...] = jnp.zeros_like(token)

    lands = [lax.empty((3,) + p.shape[1:], bf16) for p in pairs]
    out_shape = ([pltpu.SemaphoreType.DMA((3 * n,))] * 2 + [pltpu.HBM(p.shape, bf16) for p in pairs]
                 + [pltpu.HBM(q.shape, bf16) for q in lands] + [_sds((8, LANES), f32)])
    outs = pl.pallas_call(
        body, name=name, out_shape=out_shape, in_specs=[HBM] * (2 * n), out_specs=[SEM, SEM] + [HBM] * (2 * n) + [VMEM],
        input_output_aliases={t: 2 + t for t in range(2 * n)},
        compiler_params=pltpu.CompilerParams(has_side_effects=EFFECT))(*[_in_hbm(a) for a in list(pairs) + lands])
    return (outs[0], outs[1]), list(outs[2:2 + n]), list(outs[2 + n:2 + 2 * n]), outs[-1]


def _rs_chip_wait(sems, pairs, lands, after, name):
    n = len(pairs)

    def body(*refs):
        p, q = refs[:n], refs[n:2 * n]
        ssem, rsem = refs[2 * n], refs[2 * n + 1]
        x, y, c = _place()
        for t in range(n):
            for j, (_, s_p) in enumerate(_other_chips(x, y)):
                cp = pltpu.make_async_remote_copy(src_ref=p[t].at[s_p], dst_ref=q[t].at[j], send_sem=ssem.at[3 * t + j],
                                                  recv_sem=rsem.at[3 * t + j], device_id=(x, y, c), device_id_type=MESH)
                cp.wait_send()
                cp.wait_recv()

    outs = pl.pallas_call(
        body, name=name, out_shape=[pltpu.HBM(a.shape, bf16) for a in list(pairs) + list(lands)],
        in_specs=[HBM] * (2 * n) + [SEM, SEM, ANY], out_specs=[HBM] * (2 * n),
        input_output_aliases={t: t for t in range(2 * n)},
        compiler_params=pltpu.CompilerParams(has_side_effects=EFFECT))(*pairs, *lands, sems[0], sems[1], after)
    return list(outs[:n]), list(outs[n:])


def _rs_half_exchange(halves, name):
    n = len(halves)

    def body(*refs):
        o = refs[n:2 * n]
        ssem, rsem = refs[2 * n:]
        x, y, c = _place()

        def copy(t, half, dev):
            return pltpu.make_async_remote_copy(src_ref=o[t].at[c], dst_ref=o[t].at[half], send_sem=ssem.at[t],
                                                recv_sem=rsem.at[t], device_id=dev, device_id_type=MESH)

        sends = [copy(t, c, (x, y, 1 - c)) for t in range(n)]
        for cp in sends:
            cp.start()
        for t in range(n):
            copy(t, 1 - c, (x, y, c)).wait_recv()
        for cp in sends:
            cp.wait_send()

    return _pcall(body, name=name, out_shape=[_sds(h.shape, f32) for h in halves], in_specs=[ANY] * n,
                  out_specs=[ANY] * n, aliases={t: t for t in range(n)},
                  scratch=[pltpu.SemaphoreType.DMA((n,)), pltpu.SemaphoreType.DMA((n,))])(*halves)


def _row_spec(tm, cols):
    return pl.BlockSpec((tm, cols), lambda i: (i, 0))


def _vec_spec(cols, rows=1):
    return pl.BlockSpec((rows, cols), lambda i: (0, 0))


def _hnorm(x, g, shift, scale):
    T, tm = x.shape[0], 256

    def body(x_ref, g_ref, sh_ref, sc_ref, h_ref):
        xv = x_ref[...]
        r = lax.rsqrt(jnp.mean(xv * xv, axis=-1, keepdims=True) + EPS)
        a = (xv * r) * g_ref[...]
        h_ref[...] = (a * (1.0 + sc_ref[...]) + sh_ref[...]).astype(bf16)

    return _pcall(body, name="hnorm", out_shape=_sds((T, D), bf16), grid=(T // tm,),
                  in_specs=[_row_spec(tm, D), _vec_spec(D), _vec_spec(D), _vec_spec(D)],
                  out_specs=_row_spec(tm, D))(x, g, shift, scale)


def _out_proj(y2, wo, x, gate):
    T, tm = x.shape[0], 512

    def body(y_ref, w_ref, x_ref, g_ref, xo_ref, o_ref):
        o = jnp.dot(y_ref[0], w_ref[0], preferred_element_type=f32)
        o = o + jnp.dot(y_ref[1], w_ref[1], preferred_element_type=f32)
        o_ref[...] = o
        xo_ref[...] = x_ref[...] + g_ref[...] * o

    return _pcall(body, name="out_proj", out_shape=[_sds((T, D), f32), _sds((T, D), f32)], grid=(T // tm,),
                  in_specs=[pl.BlockSpec((2, tm, D), lambda i: (0, i, 0)), pl.BlockSpec((2, D, D), lambda i: (0, 0, 0)),
                            _row_spec(tm, D), _vec_spec(D)],
                  out_specs=[_row_spec(tm, D), _row_spec(tm, D)], vmem_mb=40)(y2, wo, x, gate)


def _loss_bwd(x, target, g):
    T, tm = x.shape[0], 256

    def body(x_ref, t_ref, g_ref, dx_ref, loss_ref, dg_ref):
        @pl.when(pl.program_id(0) == 0)
        def _():
            loss_ref[...] = jnp.zeros_like(loss_ref)
            dg_ref[...] = jnp.zeros_like(dg_ref)

        xv, gv = x_ref[...], g_ref[...]
        r = lax.rsqrt(jnp.mean(xv * xv, axis=-1, keepdims=True) + EPS)
        xn = xv * r
        err = xn * gv - t_ref[...]
        dy = err * (1.0 / D)
        dxn = dy * gv
        dx_ref[...] = r * (dxn - xn * jnp.mean(dxn * xn, axis=-1, keepdims=True))
        dg_ref[...] += jnp.sum(dy * xn, axis=0, keepdims=True)
        loss_ref[...] += (0.5 / D) * jnp.sum(jnp.sum(err * err, axis=1, keepdims=True), axis=0, keepdims=True)

    return _pcall(body, name="loss_bwd", out_shape=[_sds((T, D), f32), _sds((1, 1), f32), _sds((1, D), f32)],
                  grid=(T // tm,), in_specs=[_row_spec(tm, D), _row_spec(tm, D), _vec_spec(D)],
                  out_specs=[_row_spec(tm, D), pl.BlockSpec((1, 1), lambda i: (0, 0)), _vec_spec(D)])(x, target, g)


def _gate_bwd(gin, o, gate):
    T, tm = gin.shape[0], 512

    def body(gin_ref, o_ref, gate_ref, dob_ref, dgate_ref):
        @pl.when(pl.program_id(0) == 0)
        def _():
            dgate_ref[...] = jnp.zeros_like(dgate_ref)

        gv = gin_ref[...]
        dob_ref[...] = (gv * gate_ref[...]).astype(bf16)
        dgate_ref[...] += jnp.sum(gv * o_ref[...], axis=0, keepdims=True)

    return _pcall(body, name="gate_bwd", out_shape=[_sds((T, D), bf16), _sds((1, D), f32)], grid=(T // tm,),
                  in_specs=[_row_spec(tm, D), _row_spec(tm, D), _vec_spec(D)],
                  out_specs=[_row_spec(tm, D), _vec_spec(D)])(gin, o, gate)


def _norm_bwd(x, dh, gin, g, scale):
    T, tm = x.shape[0], 256

    def body(x_ref, dh_ref, gin_ref, g_ref, sc_ref, dx_ref, st_ref):
        @pl.when(pl.program_id(0) == 0)
        def _():
            st_ref[...] = jnp.zeros_like(st_ref)

        xv, gv, dhv = x_ref[...], g_ref[...], dh_ref[...]
        r = lax.rsqrt(jnp.mean(xv * xv, axis=-1, keepdims=True) + EPS)
        xn = xv * r
        da = dhv * (1.0 + sc_ref[...])
        dxn = da * gv
        dx_ref[...] = gin_ref[...] + r * (dxn - xn * jnp.mean(dxn * xn, axis=-1, keepdims=True))
        st_ref[0:1, :] += jnp.sum(dhv, axis=0, keepdims=True)
        st_ref[1:2, :] += jnp.sum(dhv * (xn * gv), axis=0, keepdims=True)
        st_ref[2:3, :] += jnp.sum(da * xn, axis=0, keepdims=True)

    return _pcall(body, name="norm_bwd", out_shape=[_sds((T, D), f32), _sds((8, D), f32)], grid=(T // tm,),
                  in_specs=[_row_spec(tm, D), _row_spec(tm, D), _row_spec(tm, D), _vec_spec(D), _vec_spec(D)],
                  out_specs=[_row_spec(tm, D), _vec_spec(D, 8)])(x, dh, gin, g, scale)


def _cast_place(place, w):
    rows, cols = w.shape
    tr = 256

    def body(place_ref, w_ref, o_ref):
        o_ref[...] = w_ref[...].astype(bf16)

    return _pcall(body, name="cast_place", out_shape=_sds((4, rows, cols), bf16), grid=(rows // tr,), prefetch=1,
                  in_specs=[pl.BlockSpec((tr, cols), lambda i, pr: (i, 0))],
                  out_specs=pl.BlockSpec((None, tr, cols), lambda i, pr: (pr[0], i, 0)))(place, w)


def _rs_add(place, grad, theirs):
    _, rows, cols = theirs.shape
    tr = 256
    spec = pl.BlockSpec((None, tr, cols), lambda s, i, pr: (s, i, 0))

    def body(place_ref, a_ref, b_ref, o_ref):
        o_ref[...] = (a_ref[...].astype(f32) + b_ref[...].astype(f32)).astype(bf16)

    return _pcall(body, name="rs_add", out_shape=_sds(theirs.shape, bf16), grid=(4, rows // tr), prefetch=1,
                  in_specs=[pl.BlockSpec((None, None, tr, cols), lambda s, i, pr: (s, pr[1], i, 0)), spec],
                  out_specs=spec)(place, grad, theirs)


def _rs_sum(place, pairs, slots):
    _, rows, cols = slots.shape
    tr = 256

    def body(place_ref, p_ref, q_ref, o_ref):
        o_ref[...] = ((p_ref[...].astype(f32) + q_ref[0].astype(f32)) + q_ref[1].astype(f32)) + q_ref[2].astype(f32)

    return _pcall(body, name="rs_sum", out_shape=_sds((2, rows, cols), f32), grid=(rows // tr,), prefetch=1,
                  in_specs=[pl.BlockSpec((None, tr, cols), lambda i, pr: (pr[0], i, 0)),
                            pl.BlockSpec((3, tr, cols), lambda i, pr: (0, i, 0))],
                  out_specs=pl.BlockSpec((None, tr, cols), lambda i, pr: (pr[1], i, 0)))(place, pairs, slots)


def _adamw_math(w, g, m, v):
    m = ADAM_B1 * m + (1.0 - ADAM_B1) * g
    v = ADAM_B2 * v + (1.0 - ADAM_B2) * jnp.square(g)
    m_hat = m / (1.0 - ADAM_B1 ** ADAM_STEP)
    v_hat = v / (1.0 - ADAM_B2 ** ADAM_STEP)
    delta = -ADAM_LR * (m_hat / (jnp.sqrt(v_hat) + ADAM_EPS) + ADAM_WD * w)
    return delta, m, v


def _adamw_layer(layer, w, g, m, v, so_far):
    _, rows, cols = w.shape
    tr = 128
    spec = pl.BlockSpec((None, tr, cols), lambda i: (layer, i, 0))

    def body(w_ref, g_ref, m_ref, v_ref, *rest):
        go_ref, d_ref, mo_ref, vo_ref = rest[-4:]
        g = g_ref[...]
        go_ref[...] = g
        d_ref[...], mo_ref[...], vo_ref[...] = _adamw_math(w_ref[...], g, m_ref[...], v_ref[...])

    args, in_specs, aliases = [w, g, m, v], [spec, pl.BlockSpec((tr, cols), lambda i: (i, 0)), spec, spec], None
    if so_far is not None:
        args += list(so_far)
        in_specs += [ANY] * 4
        aliases = {4 + k: k for k in range(4)}
    return _pcall(body, name="adamw", out_shape=[_sds(w.shape, f32)] * 4, grid=(rows // tr,), in_specs=in_specs,
                  out_specs=[spec] * 4, aliases=aliases)(*args)


def _adamw_small(items):
    n = len(items)

    def body(*refs):
        ins, outs = refs[:4 * n], refs[4 * n:]
        for t in range(n):
            w_ref, g_ref, m_ref, v_ref = ins[4 * t:4 * t + 4]
            if len(g_ref.shape) == len(w_ref.shape) + 1:
                g = g_ref[0]
                for b in range(1, g_ref.shape[0]):
                    g = g + g_ref[b]
            else:
                g = g_ref[...]
            d, m, v = _adamw_math(w_ref[...], g, m_ref[...], v_ref[...])
            outs[4 * t][...], outs[4 * t + 1][...], outs[4 * t + 2][...], outs[4 * t + 3][...] = g, d, m, v

    out_shape = [_sds(w.shape, f32) for (w, _, _, _) in items for _ in range(4)]
    flat = [a for it in items for a in it]
    res = _pcall(body, name="adamw_small", out_shape=out_shape, in_specs=[VMEM] * (4 * n),
                 out_specs=[VMEM] * (4 * n))(*flat)
    return [tuple(res[4 * t:4 * t + 4]) for t in range(n)]


NN = ((1,), (0,))
NT = ((1,), (1,))
TN = ((0,), (0,))


def _mm(name, a, b, *, grid, a_spec, b_spec, out_shape, out_spec, dims, acc_k=False, vmem_mb=48):
    def body(a_ref, b_ref, o_ref):
        r = lax.dot_general(a_ref[...], b_ref[...], (dims, ((), ())), preferred_element_type=f32)
        if acc_k:
            @pl.when(pl.program_id(0) == 0)
            def _():
                o_ref[...] = r

            @pl.when(pl.program_id(0) > 0)
            def _():
                o_ref[...] += r
        else:
            o_ref[...] = r.astype(o_ref.dtype)

    return _pcall(body, name=name, out_shape=out_shape, grid=grid, in_specs=[a_spec, b_spec], out_specs=out_spec,
                  vmem_mb=vmem_mb)(a, b)


def _whole(shape):
    return pl.BlockSpec(shape, lambda j: (0,) * len(shape))


def _split_spec(rows, tile, per_split):
    return pl.BlockSpec((None, rows, tile), lambda j: (j // per_split, 0, j % per_split))


class _Proj:
    def __init__(self, n, splits, tile):
        self.n, self.splits, self.tile = n, splits, tile
        self.steps = n // tile
        self.w_per = n // 4 // tile
        self.a_per = n // splits // tile
        assert self.w_per * tile * 4 == n and self.a_per * tile * splits == n

    def fwd(self, hb, wg):
        T = hb.shape[0]
        return _mm("proj_fwd", hb, wg, grid=(self.steps,), a_spec=_whole((T, D)),
                   b_spec=_split_spec(D, self.tile, self.w_per),
                   out_shape=_sds((self.splits, T, self.n // self.splits), f32),
                   out_spec=_split_spec(T, self.tile, self.a_per), dims=NN)

    def dw(self, hb, dp):
        T = hb.shape[0]
        return _mm("proj_dw", hb, dp, grid=(self.steps,), a_spec=_whole((T, D)),
                   b_spec=_split_spec(T, self.tile, self.a_per), out_shape=_sds((4, D, self.n // 4), bf16),
                   out_spec=_split_spec(D, self.tile, self.w_per), dims=TN)

    def dh(self, dp, wg):
        T = dp.shape[1]
        return _mm("proj_dh", dp, wg, grid=(self.steps,), a_spec=_split_spec(T, self.tile, self.a_per),
                   b_spec=_split_spec(D, self.tile, self.w_per), out_shape=_sds((T, D), f32),
                   out_spec=_whole((T, D)), dims=NT, acc_k=True)


EVEN_PROJ = _Proj(7 * D, 7, 256)
ODD_PROJ = _Proj(4 * D, 2, 512)


def _dy_mm(dob, wo):
    T = dob.shape[0]
    return _mm("out_dy", dob, wo, grid=(4,), a_spec=_whole((T, D)),
               b_spec=pl.BlockSpec((None, 512, D), lambda j: (j, 0, 0)), out_shape=_sds((2, T, D), f32),
               out_spec=_split_spec(T, 512, 2), dims=NT)


def _dwo_mm(y2, dob):
    T = dob.shape[0]
    return _mm("out_dw", y2, dob, grid=(4,), a_spec=_split_spec(T, 512, 2), b_spec=_whole((T, D)),
               out_shape=_sds((4, 512, D), bf16), out_spec=pl.BlockSpec((None, 512, D), lambda j: (j, 0, 0)), dims=TN)


def _head_spec(lead, T):
    return pl.BlockSpec((lead, T, HEAD), lambda h: (0, 0, h))


def _head_vec(rows):
    return pl.BlockSpec((rows, HEAD), lambda h: (0, h))


_HEAD_MAT = pl.BlockSpec((None, HEAD, HEAD), lambda h: (h, 0, 0))


def _causal():
    return lax.broadcasted_iota(jnp.int32, (HEAD, HEAD), 0) >= lax.broadcasted_iota(jnp.int32, (HEAD, HEAD), 1)


def _layernorm_head(v):
    mu = jnp.mean(v, axis=-1, keepdims=True)
    d = v - mu
    rstd = lax.rsqrt(jnp.mean(d * d, axis=-1, keepdims=True) + EPS)
    return d * rstd, rstd


def _even_fwd(p7, conv_w, ln_g, ln_b, sgu_w, sgu_bias):
    T, C = p7.shape[1], CHUNK_ROWS

    def body(p_ref, cw_ref, lg_ref, lb_ref, w_ref, b_ref, y_ref):
        w0, w1, w2 = cw_ref[0:1, :], cw_ref[1:2, :], cw_ref[2:3, :]
        wm = jnp.where(_causal(), w_ref[...], 0.0).astype(bf16)
        bias, lg, lb = b_ref[...], lg_ref[...], lb_ref[...]

        def step(i, halo):
            rows = pl.ds(pl.multiple_of(i * C, C), C)
            tt = p_ref[2, rows, :] * p_ref[0, rows, :]
            ext = jnp.concatenate([halo, tt], axis=0)
            cv = w2 * tt + w1 * pltpu.roll(ext, 1, 0)[HALO_CONV:] + w0 * pltpu.roll(ext, 2, 0)[HALO_CONV:]
            y_ref[0, rows, :] = (p_ref[1, rows, :] * cv * _silu(p_ref[3, rows, :])).astype(bf16)
            vhat, _ = _layernorm_head(p_ref[5, rows, :])
            vn = (vhat * lg + lb).astype(bf16)
            mix = jnp.concatenate([jnp.dot(wm, vn[k * HEAD:(k + 1) * HEAD], preferred_element_type=f32) + bias
                                   for k in range(C // HEAD)], axis=0)
            y_ref[1, rows, :] = (p_ref[4, rows, :] * mix * _silu(p_ref[6, rows, :])).astype(bf16)
            return tt[C - HALO_CONV:]

        lax.fori_loop(0, T // C, step, jnp.zeros((HALO_CONV, HEAD), f32))

    return _pcall(body, name="even_fwd", out_shape=_sds((2, T, D), bf16), grid=(NH,),
                  in_specs=[_head_spec(7, T), _head_vec(3), _head_vec(1), _head_vec(1), _HEAD_MAT, _HEAD_MAT],
                  out_specs=_head_spec(2, T), vmem_mb=32)(p7, conv_w, ln_g, ln_b, sgu_w, sgu_bias)


def _even_bwd(p7, dy2, conv_w, ln_g, ln_b, sgu_w, sgu_bias):
    T, C = p7.shape[1], CHUNK_ROWS
    n_chunks = T // C

    def body(p_ref, dy_ref, cw_ref, lg_ref, lb_ref, w_ref, b_ref,
             dp_ref, dcw_ref, dlg_ref, dlb_ref, dw_ref, dms_ref, dcv_s):
        w0, w1, w2 = cw_ref[0:1, :], cw_ref[1:2, :], cw_ref[2:3, :]
        tri = _causal()
        wm = jnp.where(tri, w_ref[...], 0.0).astype(bf16)
        bias, lg, lb = b_ref[...], lg_ref[...], lb_ref[...]
        dw_ref[...] = jnp.zeros_like(dw_ref)
        dms_ref[...] = jnp.zeros_like(dms_ref)

        def fwd_step(i, carry):
            halo, a0, a1, a2, alg, alb = carry
            rows = pl.ds(pl.multiple_of(i * C, C), C)
            ah, ab, ac, az = p_ref[0, rows, :], p_ref[1, rows, :], p_ref[2, rows, :], p_ref[3, rows, :]
            dya = dy_ref[0, rows, :]
            tt = ac * ah
            ext = jnp.concatenate([halo, tt], axis=0)
            t1, t2 = pltpu.roll(ext, 1, 0)[HALO_CONV:], pltpu.roll(ext, 2, 0)[HALO_CONV:]
            cv = w2 * tt + w1 * t1 + w0 * t2
            sa, dsa = _silu_and_grad(az)
            g1 = dya * sa
            dp_ref[1, rows, :] = (g1 * cv).astype(bf16)
            dp_ref[3, rows, :] = (dya * ab * cv * dsa).astype(bf16)
            dcv = g1 * ab
            dcv_s[rows, :] = dcv
            a2 = a2 + jnp.sum(dcv * tt, axis=0, keepdims=True)
            a1 = a1 + jnp.sum(dcv * t1, axis=0, keepdims=True)
            a0 = a0 + jnp.sum(dcv * t2, axis=0, keepdims=True)

            u, zb, dyb = p_ref[4, rows, :], p_ref[6, rows, :], dy_ref[1, rows, :]
            vhat, rstd = _layernorm_head(p_ref[5, rows, :])
            vn = (vhat * lg + lb).astype(bf16)
            sb, dsb = _silu_and_grad(zb)
            mix = jnp.concatenate([jnp.dot(wm, vn[k * HEAD:(k + 1) * HEAD], preferred_element_type=f32) + bias
                                   for k in range(C // HEAD)], axis=0)
            dp_ref[4, rows, :] = (dyb * mix * sb).astype(bf16)
            dp_ref[6, rows, :] = (dyb * u * mix * dsb).astype(bf16)
            dmix = dyb * u * sb
            dvn_parts = []
            for k in range(C // HEAD):
                dm = dmix[k * HEAD:(k + 1) * HEAD]
                dmb = dm.astype(bf16)
                dvn_parts.append(lax.dot_general(wm, dmb, (TN, ((), ())), preferred_element_type=f32))
                dw_ref[...] += lax.dot_general(dmb, vn[k * HEAD:(k + 1) * HEAD], (NT, ((), ())),
                                               preferred_element_type=f32)
                dms_ref[...] += dm
            dvn = jnp.concatenate(dvn_parts, axis=0)
            alg = alg + jnp.sum(dvn * vhat, axis=0, keepdims=True)
            alb = alb + jnp.sum(dvn, axis=0, keepdims=True)
            dvh = dvn * lg
            dv = rstd * (dvh - jnp.mean(dvh, axis=-1, keepdims=True)
                         - vhat * jnp.mean(dvh * vhat, axis=-1, keepdims=True))
            dp_ref[5, rows, :] = dv.astype(bf16)
            return tt[C - HALO_CONV:], a0, a1, a2, alg, alb

        zrow = jnp.zeros((1, HEAD), f32)
        _, a0, a1, a2, alg, alb = lax.fori_loop(
            0, n_chunks, fwd_step, (jnp.zeros((HALO_CONV, HEAD), f32), zrow, zrow, zrow, zrow, zrow))
        dcw_ref[0:1, :], dcw_ref[1:2, :], dcw_ref[2:3, :] = a0, a1, a2
        dlg_ref[...], dlb_ref[...] = alg, alb
        dw_ref[...] = jnp.where(tri, dw_ref[...], 0.0)

        def bwd_step(k, halo):
            rows = pl.ds(pl.multiple_of((n_chunks - 1 - k) * C, C), C)
            dcv = dcv_s[rows, :]
            ext = jnp.concatenate([dcv, halo], axis=0)
            n1 = pltpu.roll(ext, C + HALO_CONV - 1, 0)[:C]
            n2 = pltpu.roll(ext, C + HALO_CONV - 2, 0)[:C]
            dtt = w2 * dcv + w1 * n1 + w0 * n2
            dp_ref[2, rows, :] = (dtt * p_ref[0, rows, :]).astype(bf16)
            dp_ref[0, rows, :] = (dtt * p_ref[2, rows, :]).astype(bf16)
            return dcv[:HALO_CONV]

        lax.fori_loop(0, n_chunks, bwd_step, jnp.zeros((HALO_CONV, HEAD), f32))

    out_shape = [_sds((7, T, D), bf16), _sds((3, D), f32), _sds((1, D), f32), _sds((1, D), f32),
                 _sds((NH, HEAD, HEAD), f32), _sds((NH, HEAD, HEAD), f32)]
    return _pcall(body, name="even_bwd", out_shape=out_shape, grid=(NH,),
                  in_specs=[_head_spec(7, T), _head_spec(2, T), _head_vec(3), _head_vec(1), _head_vec(1),
                            _HEAD_MAT, _HEAD_MAT],
                  out_specs=[_head_spec(7, T), _head_vec(3), _head_vec(1), _head_vec(1), _HEAD_MAT, _HEAD_MAT],
                  scratch=[pltpu.VMEM((T, HEAD), f32)], vmem_mb=48)(p7, dy2, conv_w, ln_g, ln_b, sgu_w, sgu_bias)


def _window_sum(ext, win, towards_past):
    n, k, s = ext.shape[0], 1, ext
    while k < win:
        s = s + pltpu.roll(s, k if towards_past else n - k, 0)
        k *= 2
    return s


def _pool_count(i, C, win):
    t = i * C + lax.broadcasted_iota(jnp.int32, (C, 1), 0)
    return jnp.minimum(t + 1, win).astype(f32)


def _group_specs(T):
    p_spec = pl.BlockSpec((None, T, GC), lambda g: (0, 0, g))
    z_spec = pl.BlockSpec((None, T, GC), lambda g: (1, 0, g))
    pw_spec = pl.BlockSpec((4, GC // 4, GC), lambda g: (0, g, 0))
    ps_spec = pl.BlockSpec((1, GC), lambda g: (0, g))
    y_spec = pl.BlockSpec((None, T, GC), lambda g: (g // 2, 0, g % 2))
    return p_spec, z_spec, pw_spec, ps_spec, y_spec


def _odd_fwd(p2, pool_wg, pool_scale):
    T, C = p2.shape[1], CHUNK_ROWS
    p_spec, z_spec, pw_spec, ps_spec, y_spec = _group_specs(T)

    def body(p_ref, z_ref, pw_ref, ps_ref, y_ref):
        pw, ps = pw_ref[...].reshape(GC, GC), ps_ref[...]

        def run(win):
            def step(i, halo):
                rows = pl.ds(pl.multiple_of(i * C, C), C)
                p = p_ref[rows, :]
                s = _window_sum(jnp.concatenate([halo, p], axis=0), win, True)[HALO_POOL:]
                pooled = s / _pool_count(i, C, win) - p
                ypre = jnp.dot(pooled.astype(bf16), pw, preferred_element_type=f32)
                y_ref[rows, :] = (ypre * ps * _silu(z_ref[rows, :])).astype(bf16)
                return p[C - HALO_POOL:]

            lax.fori_loop(0, T // C, step, jnp.zeros((HALO_POOL, GC), f32))

        for gi, win in enumerate(WINDOWS):
            pl.when(pl.program_id(0) == gi)(functools.partial(run, win))

    return _pcall(body, name="odd_fwd", out_shape=_sds((2, T, D), bf16), grid=(len(WINDOWS),),
                  in_specs=[p_spec, z_spec, pw_spec, ps_spec], out_specs=y_spec, vmem_mb=40)(p2, p2, pool_wg, pool_scale)


def _odd_bwd(p2, dy2, pool_wg, pool_scale):
    T, C = p2.shape[1], CHUNK_ROWS
    n_chunks = T // C
    p_spec, z_spec, pw_spec, ps_spec, y_spec = _group_specs(T)

    def body(p_ref, z_ref, dy_ref, pw_ref, ps_ref, dp_ref, dpw_ref, dps_ref, q_s, acc_s):
        pw, ps = pw_ref[...].reshape(GC, GC), ps_ref[...]

        def run(win):
            acc_s[...] = jnp.zeros_like(acc_s)

            def fwd_step(i, carry):
                halo, aps = carry
                rows = pl.ds(pl.multiple_of(i * C, C), C)
                p, z, dy = p_ref[rows, :], z_ref[rows, :], dy_ref[rows, :]
                cnt = _pool_count(i, C, win)
                s = _window_sum(jnp.concatenate([halo, p], axis=0), win, True)[HALO_POOL:]
                pb = (s / cnt - p).astype(bf16)
                ypre = jnp.dot(pb, pw, preferred_element_type=f32)
                sz, dsz = _silu_and_grad(z)
                aps = aps + jnp.sum(dy * ypre * sz, axis=0, keepdims=True)
                dp_ref[1, rows, :] = (dy * ypre * ps * dsz).astype(bf16)
                dyp = (dy * ps * sz).astype(bf16)
                acc_s[...] += lax.dot_general(pb, dyp, (TN, ((), ())), preferred_element_type=f32)
                dpool = lax.dot_general(dyp, pw, (NT, ((), ())), preferred_element_type=f32)
                q_s[rows, :] = dpool / cnt
                return p[C - HALO_POOL:], aps

            _, aps = lax.fori_loop(0, n_chunks, fwd_step, (jnp.zeros((HALO_POOL, GC), f32), jnp.zeros((1, GC), f32)))
            dps_ref[...] = aps
            dpw_ref[...] = acc_s[...].reshape(4, GC // 4, GC).astype(bf16)

            def bwd_step(k, halo):
                i = n_chunks - 1 - k
                rows = pl.ds(pl.multiple_of(i * C, C), C)
                q = q_s[rows, :]
                s = _window_sum(jnp.concatenate([q, halo], axis=0), win, False)[:C]
                dp_ref[0, rows, :] = (s - q * _pool_count(i, C, win)).astype(bf16)
                return q[:HALO_POOL]

            lax.fori_loop(0, n_chunks, bwd_step, jnp.zeros((HALO_POOL, GC), f32))

        for gi, win in enumerate(WINDOWS):
            pl.when(pl.program_id(0) == gi)(functools.partial(run, win))

    out_shape = [_sds((2, T, 2 * D), bf16), _sds((4, GC, GC), bf16), _sds((1, 2 * D), f32)]
    return _pcall(body, name="odd_bwd", out_shape=out_shape, grid=(len(WINDOWS),),
                  in_specs=[p_spec, z_spec, y_spec, pw_spec, ps_spec],
                  out_specs=[pl.BlockSpec((2, T, GC), lambda g: (0, 0, g)), pw_spec, ps_spec],
                  scratch=[pltpu.VMEM((T, GC), f32), pltpu.VMEM((GC, GC), f32)], vmem_mb=52)(
                      p2, p2, dy2, pool_wg, pool_scale)


def _ada_fwd(c_all, ada_w):
    cols = ada_w.shape[2]

    def body(c_ref, w_ref, o_ref):
        o_ref[...] = jnp.dot(_silu(c_ref[...]), w_ref[...], preferred_element_type=f32,
                             precision=lax.Precision.HIGHEST)

    return _pcall(body, name="ada_fwd", out_shape=_sds((4, N_DEV, cols), f32), grid=(4,),
                  in_specs=[pl.BlockSpec((N_DEV, D), lambda i: (0, 0)), pl.BlockSpec((None, D, cols), lambda i: (i, 0, 0))],
                  out_specs=pl.BlockSpec((None, N_DEV, cols), lambda i: (i, 0, 0)))(c_all, ada_w)


def _ada_bwd(c_all_t, dmod, w, m, v):
    cols, tr = w.shape[2], 256
    spec = pl.BlockSpec((None, tr, cols), lambda l, i: (l, i, 0))

    def body(c_ref, dm_ref, w_ref, m_ref, v_ref, g_ref, d_ref, mo_ref, vo_ref):
        sc = _silu(c_ref[...])
        g = sc[:, 0:1] * dm_ref[0:1, :]
        for b in range(1, N_DEV):
            g = g + sc[:, b:b + 1] * dm_ref[b:b + 1, :]
        g_ref[...] = g
        d_ref[...], mo_ref[...], vo_ref[...] = _adamw_math(w_ref[...], g, m_ref[...], v_ref[...])

    return _pcall(body, name="ada_bwd", out_shape=[_sds(w.shape, f32)] * 4, grid=(4, D // tr),
                  in_specs=[pl.BlockSpec((tr, N_DEV), lambda l, i: (i, 0)),
                            pl.BlockSpec((None, N_DEV, cols), lambda l, i: (l, 0, 0)), spec, spec, spec],
                  out_specs=[spec] * 4)(c_all_t, dmod, w, m, v)


def _layer_fwd(even, x, mod, g, w):
    shift, scale, gate = mod
    hb = _hnorm(x, g, shift, scale)
    if even:
        w_in, w_out, conv_w, ln_g, ln_b, sgu_w, sgu_b = w
        bias = jnp.broadcast_to(sgu_b[:, :, None], (NH, HEAD, HEAD))
        p = EVEN_PROJ.fwd(hb, w_in)
        y2 = _even_fwd(p, conv_w, ln_g, ln_b, sgu_w, bias)
    else:
        w_in, pool_w, w_out, pool_scale = w
        p = ODD_PROJ.fwd(hb, w_in)
        y2 = _odd_fwd(p, pool_w, pool_scale)
    x_next, o = _out_proj(y2, w_out.reshape(2, D, D), x, gate)
    return x_next, (x, hb, p, y2, o)


def _layer_bwd(even, gin, saved, mod, g, w):
    shift, scale, gate = mod
    x_in, hb, p, y2, o = saved
    dob, dgate = _gate_bwd(gin, o, gate)
    if even:
        w_in, w_out, conv_w, ln_g, ln_b, sgu_w, sgu_b = w
        bias = jnp.broadcast_to(sgu_b[:, :, None], (NH, HEAD, HEAD))
        dy2 = _dy_mm(dob, w_out)
        dp, dconv, dlg, dlb, dsw, dms = _even_bwd(p, dy2, conv_w, ln_g, ln_b, sgu_w, bias)
        proj = EVEN_PROJ
        small = dict(conv_w=dconv, ln_g=dlg, ln_b=dlb, sgu_w=dsw, sgu_b=jnp.sum(dms, axis=-1))
        big = [proj.dw(hb, dp), _dwo_mm(y2, dob)]
    else:
        w_in, pool_w, w_out, pool_scale = w
        dy2 = _dy_mm(dob, w_out)
        dp, dpw, dps = _odd_bwd(p, dy2, pool_w, pool_scale)
        proj = ODD_PROJ
        small = dict(pool_scale=dps)
        big = [proj.dw(hb, dp), dpw, _dwo_mm(y2, dob)]
    dh = proj.dh(dp, w_in)
    gx, stats = _norm_bwd(x_in, dh, gin, g, scale)
    return gx, big, small, jnp.concatenate([stats[0:2], dgate], axis=0), stats[2:3]


def _pack_rows(parts):
    rows = [p.reshape(-1, LANES) for p in parts]
    total = sum(r.shape[0] for r in rows)
    padded = -(-total // (8 * N_DEV)) * (8 * N_DEV)
    if padded > total:
        rows.append(jnp.zeros((padded - total, LANES), f32))
    return jnp.concatenate(rows, axis=0)


def _unpack_rows(buf, shapes):
    out, r = [], 0
    for shp in shapes:
        n = 1
        for d in shp:
            n *= d
        out.append(buf[r:r + n // LANES].reshape(shp))
        r += n // LANES
    return out


def kernel(x, c, norm_g, ada_w, ada_b, ab_w_in, ab_conv_w, ab_ln_g, ab_ln_b, ab_sgu_w, ab_sgu_b, ab_w_out, c_w_in, c_pool_w, c_pool_scale, c_w_out, final_g, loss_target, m_norm_g, m_ada_w, m_ada_b, m_ab_w_in, m_ab_conv_w, m_ab_ln_g, m_ab_ln_b, m_ab_sgu_w, m_ab_sgu_b, m_ab_w_out, m_c_w_in, m_c_pool_w, m_c_pool_scale, m_c_w_out, m_final_g, v_norm_g, v_ada_w, v_ada_b, v_ab_w_in, v_ab_conv_w, v_ab_ln_g, v_ab_ln_b, v_ab_sgu_w, v_ab_sgu_b, v_ab_w_out, v_c_w_in, v_c_pool_w, v_c_pool_scale, v_c_w_out, v_final_g):
    ix, iy, ic = _place()
    chip, dev = 2 * ix + iy, 4 * ix + 2 * iy + ic
    n_even, n_odd = ab_w_in.shape[0], c_w_in.shape[0]
    depth = n_even + n_odd
    acols = ada_w.shape[2]

    place = jnp.stack([chip, ic]).astype(jnp.int32)
    even_names, odd_names = ["ab_w_in", "ab_w_out"], ["c_w_in", "c_pool_w", "c_w_out"]
    params = {"ab_w_in": (ab_w_in, m_ab_w_in, v_ab_w_in), "ab_w_out": (ab_w_out, m_ab_w_out, v_ab_w_out),
              "c_w_in": (c_w_in, m_c_w_in, v_c_w_in), "c_w_out": (c_w_out, m_c_w_out, v_c_w_out),
              "c_pool_w": tuple(a.reshape(n_odd, GC, GC) for a in (c_pool_w, m_c_pool_w, v_c_pool_w))}

    placed = []
    for i in range(depth):
        local = [params[nm][0][i // 2] for nm in (even_names if i % 2 == 0 else odd_names)]
        placed.append([_cast_place(place, w).reshape(4, 2, w.shape[0] // 2, w.shape[1]) for w in local])
    ag_sems, inflight, tok = _ag_start(placed, "ag_start")
    c = c + tok[0:1, 0:1]

    c_all = _gather8(c, "gather_c").reshape(N_DEV, D)
    modp = _ada_fwd(c_all, ada_w)
    modg = _gather8(modp, "gather_mod")
    mod_rows = lax.dynamic_index_in_dim(modg[0::2], dev, axis=2, keepdims=False)
    mod = jnp.transpose(mod_rows, (1, 0, 2)).reshape(depth, 3 * D) + ada_b
    mods = [(mod[i:i + 1, 0:D], mod[i:i + 1, D:2 * D], mod[i:i + 1, 2 * D:3 * D]) for i in range(depth)]

    def shard_cols(a, width):
        return lax.dynamic_slice_in_dim(a, chip * width, width, axis=a.ndim - 1)

    small_sharded = jnp.concatenate([ab_conv_w.reshape(1, -1), c_pool_scale.reshape(1, -1)], axis=1)
    small_all = _gather8(small_sharded, "gather_small")[0::2, 0]
    n_conv = ab_conv_w.size
    conv_all = small_all[:, :n_conv].reshape(4, n_even, 3, D // 4)
    conv_full = jnp.transpose(conv_all, (1, 2, 0, 3)).reshape(n_even, 3, D)
    scale_all = small_all[:, n_conv:].reshape(4, n_odd, 2 * D // 4)
    scale_full = jnp.transpose(scale_all, (1, 0, 2)).reshape(n_odd, 2 * D)

    x_cur, after, saved, weights = x[0], scale_full, [], []
    for i in range(depth):
        j = i // 2
        arrived = _ag_wait(inflight[i], ag_sems[i], after, f"ag_wait_{i}")
        full = [g.reshape(4, 2 * g.shape[2], g.shape[3]) for g in _ag_forward(arrived, "ag_forward")]
        if i % 2 == 0:
            w = (full[0], full[1], conv_full[j], ab_ln_g[j:j + 1], ab_ln_b[j:j + 1], ab_sgu_w[j], ab_sgu_b[j])
        else:
            w = (full[0], full[1], full[2], scale_full[j:j + 1])
        weights.append(w)
        x_cur, sv = _layer_fwd(i % 2 == 0, x_cur, mods[i], norm_g[i:i + 1], w)
        saved.append(sv)
        after = x_cur
    gin, loss, dfinal_g = _loss_bwd(x_cur, loss_target[0], final_g.reshape(1, D))
    loss = lax.psum(loss[0, 0], ("x", "y", "c"))

    stacked = {}

    def finish(i, sems, pairs, lands, after):
        pairs, slots = _rs_chip_wait(sems, pairs, lands, after, f"rs_chip_wait_{i}")
        halves = [_rs_sum(place, p, q) for p, q in zip(pairs, slots)]
        for nm, g in zip(even_names if i % 2 == 0 else odd_names, _rs_half_exchange(halves, "rs_half_exchange")):
            w, m, v = params[nm]
            stacked[nm] = _adamw_layer(i // 2, w, g.reshape(w.shape[1], w.shape[2]), m, v, stacked.get(nm))

    small_g, dmod, dnorm_g, pending, tok = [None] * depth, [None] * depth, [None] * depth, None, None
    for i in reversed(range(depth)):
        shift, scale, gate = mods[i]
        if tok is not None:
            gate = gate + tok[0:1, 0:1]
        gin, big_g, small_g[i], dmod[i], dnorm_g[i] = _layer_bwd(i % 2 == 0, gin, saved[i], (shift, scale, gate),
                                                                  norm_g[i:i + 1], weights[i])
        big_g = [g.reshape(4, 2, g.shape[1] // 2, g.shape[2]) for g in big_g]
        theirs = _rs_pair_exchange(big_g, "rs_pair_exchange")
        pairs = [_rs_add(place, a, b) for a, b in zip(big_g, theirs)]
        sems, pairs, lands, tok = _rs_chip_start(pairs, f"rs_chip_start_{i}")
        if pending is not None:
            finish(*pending, tok)
        pending = (i, sems, pairs, lands)
    grad_x = gin
    dmod, dnorm_g = jnp.stack(dmod), jnp.concatenate(dnorm_g, axis=0)

    small_parts = [dnorm_g + tok[0:1, 0:1], dfinal_g,
                   jnp.stack([small_g[2 * j]["conv_w"] for j in range(n_even)]),
                   jnp.concatenate([small_g[2 * j]["ln_g"] for j in range(n_even)], axis=0),
                   jnp.concatenate([small_g[2 * j]["ln_b"] for j in range(n_even)], axis=0),
                   jnp.stack([small_g[2 * j]["sgu_w"] for j in range(n_even)]),
                   jnp.stack([small_g[2 * j]["sgu_b"] for j in range(n_even)]),
                   jnp.concatenate([small_g[2 * j + 1]["pool_scale"] for j in range(n_odd)], axis=0)]
    small_shapes = [p.shape for p in small_parts]
    reduced = _allreduce8(_pack_rows(small_parts), "allreduce_small")
    g_norm_g, g_final_g, g_conv_full, g_ln_g, g_ln_b, g_sgu_w, g_sgu_b, g_scale_full = _unpack_rows(reduced, small_shapes)
    g_conv = shard_cols(g_conv_full, D // 4)
    g_scale = shard_cols(g_scale_full, 2 * D // 4)
    dmod_all = _gather8(dmod.reshape(depth * 3 * D // LANES, LANES), "gather_dmod").reshape(N_DEV, depth, 3 * D)

    def two_d(a):
        return a.reshape(-1, a.shape[-1])

    small = [(norm_g, g_norm_g, m_norm_g, v_norm_g),
             (ada_b, dmod_all, m_ada_b, v_ada_b),
             (two_d(ab_conv_w), two_d(g_conv), two_d(m_ab_conv_w), two_d(v_ab_conv_w)),
             (ab_ln_g, g_ln_g, m_ab_ln_g, v_ab_ln_g),
             (ab_ln_b, g_ln_b, m_ab_ln_b, v_ab_ln_b),
             (two_d(ab_sgu_w), two_d(g_sgu_w), two_d(m_ab_sgu_w), two_d(v_ab_sgu_w)),
             (two_d(ab_sgu_b), two_d(g_sgu_b), two_d(m_ab_sgu_b), two_d(v_ab_sgu_b)),
             (c_pool_scale, g_scale, m_c_pool_scale, v_c_pool_scale),
             (final_g.reshape(1, D), g_final_g, m_final_g.reshape(1, D), v_final_g.reshape(1, D))]
    small_res = _adamw_small(small)
    small_shapes_out = [norm_g.shape, ada_b.shape, ab_conv_w.shape, ab_ln_g.shape, ab_ln_b.shape, ab_sgu_w.shape,
                        ab_sgu_b.shape, c_pool_scale.shape, final_g.shape]
    (r_norm_g, r_ada_b, r_conv, r_ln_g, r_ln_b, r_sgu_w, r_sgu_b, r_scale, r_final_g) = [
        tuple(a.reshape(shp) for a in res) for res, shp in zip(small_res, small_shapes_out)]

    dmod_cols = jnp.transpose(shard_cols(dmod_all, acols), (1, 0, 2))
    r_ada_w = _ada_bwd(c_all.T, dmod_cols, ada_w, m_ada_w, v_ada_w)

    finish(*pending, r_ada_w[1])
    r_ab_w_in, r_ab_w_out, r_c_w_in, r_c_w_out = (stacked[nm] for nm in ("ab_w_in", "ab_w_out", "c_w_in", "c_w_out"))
    r_c_pool_w = tuple(a.reshape(c_pool_w.shape) for a in stacked["c_pool_w"])

    order = [r_norm_g, r_ada_w, r_ada_b, r_ab_w_in, r_conv, r_ln_g, r_ln_b, r_sgu_w, r_sgu_b, r_ab_w_out,
             r_c_w_in, r_c_pool_w, r_scale, r_c_w_out, r_final_g]
    outs = [loss, grad_x[None]]
    for field in range(4):
        outs += [r[field] for r in order]
    return tuple(outs)
```

```python
import functools

import jax
import jax.numpy as jnp
from jax import lax
from jax.experimental import pallas as pl
from jax.experimental.pallas import tpu as pltpu

f32, bf16 = jnp.float32, jnp.bfloat16

D = 1024
HEAD = 128
NH = 8
WINDOWS = (2, 4, 8, 16)
GC = 512
EPS = 1e-6
HALO_CONV = 8
HALO_POOL = 16
CHUNK_ROWS = 256
DH_WIDE = 1024
N_DEV = 8
LANES = 128

ADAM_LR, ADAM_B1, ADAM_B2, ADAM_EPS, ADAM_WD, ADAM_STEP = 0.001, 0.9, 0.999, 1e-08, 0.01, 10

MESH = pl.DeviceIdType.MESH
ANY = pl.BlockSpec(memory_space=pl.ANY)
VMEM = pl.BlockSpec(memory_space=pltpu.VMEM)
MIB = 2 ** 20


def _pcall(body, *, name, out_shape, grid=None, in_specs=None, out_specs=None, scratch=(), vmem_mb=None,
           aliases=None, prefetch=0):
    kw = {}
    if prefetch:
        kw["grid_spec"] = pltpu.PrefetchScalarGridSpec(num_scalar_prefetch=prefetch, grid=grid, in_specs=in_specs,
                                                       out_specs=out_specs, scratch_shapes=list(scratch))
    else:
        if grid is not None:
            kw["grid"] = grid
        if in_specs is not None:
            kw["in_specs"] = in_specs
        if out_specs is not None:
            kw["out_specs"] = out_specs
        if scratch:
            kw["scratch_shapes"] = list(scratch)
    if aliases:
        kw["input_output_aliases"] = aliases
    params = pltpu.CompilerParams(vmem_limit_bytes=None if vmem_mb is None else vmem_mb * MIB)
    return pl.pallas_call(body, name=name, out_shape=out_shape, compiler_params=params, **kw)


def _sds(shape, dtype):
    return jax.ShapeDtypeStruct(tuple(shape), dtype)


def _silu(z):
    return z * jax.nn.sigmoid(z)


def _silu_and_grad(z):
    s = jax.nn.sigmoid(z)
    return z * s, s * (1.0 + z * (1.0 - s))


def _place():
    return lax.axis_index("x"), lax.axis_index("y"), lax.axis_index("c")


def _gather8(blk, name):
    def body(x_ref, o_ref, ssem, rsem):
        x, y, c = _place()
        me = 4 * x + 2 * y + c
        o_ref[me] = x_ref[...]
        sends = []
        for k in range(1, N_DEV):
            px = 1 - x if k & 4 else x
            py = 1 - y if k & 2 else y
            pc = 1 - c if k & 1 else c
            cp = pltpu.make_async_remote_copy(src_ref=x_ref, dst_ref=o_ref.at[me], send_sem=ssem.at[k - 1],
                                              recv_sem=rsem.at[k - 1], device_id=(px, py, pc), device_id_type=MESH)
            cp.start()
            sends.append((cp, 4 * px + 2 * py + pc))
        for k, (cp, peer) in enumerate(sends):
            pltpu.make_async_remote_copy(src_ref=x_ref, dst_ref=o_ref.at[peer], send_sem=ssem.at[k],
                                         recv_sem=rsem.at[k], device_id=(x, y, c), device_id_type=MESH).wait_recv()
        for cp, _ in sends:
            cp.wait_send()

    return _pcall(body, name=name, out_shape=_sds((N_DEV,) + blk.shape, blk.dtype), in_specs=[VMEM], out_specs=VMEM,
                  scratch=[pltpu.SemaphoreType.DMA((N_DEV - 1,)), pltpu.SemaphoreType.DMA((N_DEV - 1,))])(blk)


def _allreduce8(buf, name):
    rows = buf.shape[0]
    rb = rows // N_DEV
    assert rb * N_DEV == rows and rb % 8 == 0

    def body(x_ref, o_ref, stage, ssem, rsem):
        x, y, c = _place()
        me = 4 * x + 2 * y + c
        peers = []
        for k in range(1, N_DEV):
            px = 1 - x if k & 4 else x
            py = 1 - y if k & 2 else y
            pc = 1 - c if k & 1 else c
            peers.append(((px, py, pc), 4 * px + 2 * py + pc))

        def blk(ref, idx):
            return ref.at[pl.ds(pl.multiple_of(idx * rb, 8), rb), :]

        def copy(phase, k, src, dst, dev):
            return pltpu.make_async_remote_copy(src_ref=src, dst_ref=dst, send_sem=ssem.at[phase, k],
                                                recv_sem=rsem.at[phase, k], device_id=dev, device_id_type=MESH)

        stage[me] = blk(x_ref, me)[...]
        scatter = [copy(0, k, blk(x_ref, pidx), stage.at[me], dev) for k, (dev, pidx) in enumerate(peers)]
        for cp in scatter:
            cp.start()
        for k, (dev, pidx) in enumerate(peers):
            copy(0, k, blk(x_ref, pidx), stage.at[pidx], dev).wait_recv()
        total = stage[0]
        for j in range(1, N_DEV):
            total = total + stage[j]
        blk(o_ref, me)[...] = total
        gather = [copy(1, k, blk(o_ref, me), blk(o_ref, me), dev) for k, (dev, pidx) in enumerate(peers)]
        for cp in gather:
            cp.start()
        for k, (dev, pidx) in enumerate(peers):
            copy(1, k, blk(o_ref, pidx), blk(o_ref, pidx), dev).wait_recv()
        for cp in scatter + gather:
            cp.wait_send()

    return _pcall(body, name=name, out_shape=_sds(buf.shape, f32), in_specs=[VMEM], out_specs=VMEM,
                  scratch=[pltpu.VMEM((N_DEV, rb, LANES), f32), pltpu.SemaphoreType.DMA((2, N_DEV - 1)),
                           pltpu.SemaphoreType.DMA((2, N_DEV - 1))])(buf)


def _other_chips(x, y):
    return [((1 - x, y), 2 * (1 - x) + y), ((x, 1 - y), 2 * x + (1 - y)), ((1 - x, 1 - y), 2 * (1 - x) + (1 - y))]


HBM = pl.BlockSpec(memory_space=pltpu.HBM)
SEM = pl.BlockSpec(memory_space=pltpu.SEMAPHORE)
EFFECT = pltpu.SideEffectType.DATAFLOW_SIDE_EFFECTING


def _in_hbm(a):
    return pltpu.with_memory_space_constraint(a, pltpu.HBM)


def _ag_start(layers, after, name):
    flat = [t for lay in layers for t in lay]
    n, nl = len(flat), len(layers)

    def body(*refs):
        src = refs[:n]
        sems = refs[n + 1:n + 1 + 2 * nl]
        token = refs[-1]
        x, y, c = _place()
        s_me = 2 * x + y
        t = 0
        for i, lay in enumerate(layers):
            for k in range(len(lay)):
                for j, ((px, py), _) in enumerate(_other_chips(x, y)):
                    pltpu.make_async_remote_copy(src_ref=src[t].at[s_me, c], dst_ref=src[t].at[s_me, c],
                                                 send_sem=sems[2 * i].at[3 * k + j], recv_sem=sems[2 * i + 1].at[3 * k + j],
                                                 device_id=(px, py, c), device_id_type=MESH).start()
                t += 1
        token[...] = jnp.zeros_like(token)

    sem_shapes = [pltpu.SemaphoreType.DMA((3 * len(lay),)) for lay in layers for _ in range(2)]
    out_shape = sem_shapes + [pltpu.HBM(t.shape, t.dtype) for t in flat] + [_sds((8, LANES), f32)]
    outs = pl.pallas_call(
        body, name=name, out_shape=out_shape, in_specs=[HBM] * n + [ANY],
        out_specs=[SEM] * (2 * nl) + [HBM] * n + [VMEM], input_output_aliases={t: 2 * nl + t for t in range(n)},
        compiler_params=pltpu.CompilerParams(has_side_effects=EFFECT))(*[_in_hbm(t) for t in flat], after)
    sems = [(outs[2 * i], outs[2 * i + 1]) for i in range(nl)]
    thru, t = [], 2 * nl
    for lay in layers:
        thru.append(list(outs[t:t + len(lay)]))
        t += len(lay)
    return sems, thru, outs[-1]


def _ag_wait(inflight, sems, after, name):
    n = len(inflight)

    def body(*refs):
        src, ssem, rsem = refs[:n], refs[n], refs[n + 1]
        x, y, c = _place()
        s_me = 2 * x + y
        for k in range(n):
            for j, (_, s_p) in enumerate(_other_chips(x, y)):
                cp = pltpu.make_async_remote_copy(src_ref=src[k].at[s_me, c], dst_ref=src[k].at[s_p, c],
                                                  send_sem=ssem.at[3 * k + j], recv_sem=rsem.at[3 * k + j],
                                                  device_id=(x, y, c), device_id_type=MESH)
                cp.wait_send()
                cp.wait_recv()

    return pl.pallas_call(
        body, name=name, out_shape=[pltpu.HBM(t.shape, t.dtype) for t in inflight],
        in_specs=[HBM] * n + [SEM, SEM, ANY], out_specs=[HBM] * n, input_output_aliases={t: t for t in range(n)},
        compiler_params=pltpu.CompilerParams(has_side_effects=EFFECT))(*inflight, sems[0], sems[1], after)


def _ag_forward(arrived, name):
    n = len(arrived)

    def body(*refs):
        o = refs[n:2 * n]
        ssem, rsem = refs[2 * n:]
        x, y, c = _place()

        def copy(t, j, s, half, dev):
            return pltpu.make_async_remote_copy(src_ref=o[t].at[s, c], dst_ref=o[t].at[s, half], send_sem=ssem.at[t, j],
                                                recv_sem=rsem.at[t, j], device_id=dev, device_id_type=MESH)

        chips = _other_chips(x, y)
        sends = [copy(t, j, s_p, c, (x, y, 1 - c)) for t in range(n) for j, (_, s_p) in enumerate(chips)]
        for cp in sends:
            cp.start()
        for t in range(n):
            for j, (_, s_p) in enumerate(chips):
                copy(t, j, s_p, 1 - c, (x, y, c)).wait_recv()
        for cp in sends:
            cp.wait_send()

    return _pcall(body, name=name, out_shape=[_sds(p.shape, bf16) for p in arrived], in_specs=[ANY] * n,
                  out_specs=[ANY] * n, aliases={t: t for t in range(n)},
                  scratch=[pltpu.SemaphoreType.DMA((n, 3)), pltpu.SemaphoreType.DMA((n, 3))])(*arrived)


def _rs_pair_exchange(grads, name):
    n = len(grads)

    def body(*refs):
        g, theirs = refs[:n], refs[n:2 * n]
        ssem, rsem = refs[2 * n:]
        x, y, c = _place()
        sends = [pltpu.make_async_remote_copy(src_ref=g[t].at[:, 1 - c], dst_ref=theirs[t], send_sem=ssem.at[t],
                                              recv_sem=rsem.at[t], device_id=(x, y, 1 - c), device_id_type=MESH)
                 for t in range(n)]
        for cp in sends:
            cp.start()
        for cp in sends:
            cp.wait()

    half = [_sds((4,) + gr.shape[2:], bf16) for gr in grads]
    return _pcall(body, name=name, out_shape=half, in_specs=[ANY] * n, out_specs=[ANY] * n,
                  scratch=[pltpu.SemaphoreType.DMA((n,)), pltpu.SemaphoreType.DMA((n,))])(*grads)


def _rs_chip_start(pairs, name):
    n = len(pairs)

    def body(*refs):
        p, q = refs[:n], refs[n:2 * n]
        ssem, rsem, token = refs[2 * n], refs[2 * n + 1], refs[-1]
        x, y, c = _place()
        for t in range(n):
            for j, ((px, py), s_p) in enumerate(_other_chips(x, y)):
                pltpu.make_async_remote_copy(src_ref=p[t].at[s_p], dst_ref=q[t].at[j], send_sem=ssem.at[3 * t + j],
                                             recv_sem=rsem.at[3 * t + j], device_id=(px, py, c), device_id_type=MESH).start()
        token[...] = jnp.zeros_like(token)

    lands = [lax.empty((3,) + p.shape[1:], bf16) for p in pairs]
    out_shape = ([pltpu.SemaphoreType.DMA((3 * n,))] * 2 + [pltpu.HBM(p.shape, bf16) for p in pairs]
                 + [pltpu.HBM(q.shape, bf16) for q in lands] + [_sds((8, LANES), f32)])
    outs = pl.pallas_call(
        body, name=name, out_shape=out_shape, in_specs=[HBM] * (2 * n), out_specs=[SEM, SEM] + [HBM] * (2 * n) + [VMEM],
        input_output_aliases={t: 2 + t for t in range(2 * n)},
        compiler_params=pltpu.CompilerParams(has_side_effects=EFFECT))(*[_in_hbm(a) for a in list(pairs) + lands])
    return (outs[0], outs[1]), list(outs[2:2 + n]), list(outs[2 + n:2 + 2 * n]), outs[-1]


def _rs_chip_wait(sems, pairs, lands, after, name):
    n = len(pairs)

    def body(*refs):
        p, q = refs[:n], refs[n:2 * n]
        ssem, rsem = refs[2 * n], refs[2 * n + 1]
        x, y, c = _place()
        for t in range(n):
            for j, (_, s_p) in enumerate(_other_chips(x, y)):
                cp = pltpu.make_async_remote_copy(src_ref=p[t].at[s_p], dst_ref=q[t].at[j], send_sem=ssem.at[3 * t + j],
                                                  recv_sem=rsem.at[3 * t + j], device_id=(x, y, c), device_id_type=MESH)
                cp.wait_send()
                cp.wait_recv()

    outs = pl.pallas_call(
        body, name=name, out_shape=[pltpu.HBM(a.shape, bf16) for a in list(pairs) + list(lands)],
        in_specs=[HBM] * (2 * n) + [SEM, SEM, ANY], out_specs=[HBM] * (2 * n),
        input_output_aliases={t: t for t in range(2 * n)},
        compiler_params=pltpu.CompilerParams(has_side_effects=EFFECT))(*pairs, *lands, sems[0], sems[1], after)
    return list(outs[:n]), list(outs[n:])


def _rs_half_exchange(halves, name):
    n = len(halves)

    def body(*refs):
        o = refs[n:2 * n]
        ssem, rsem = refs[2 * n:]
        x, y, c = _place()

        def copy(t, half, dev):
            return pltpu.make_async_remote_copy(src_ref=o[t].at[c], dst_ref=o[t].at[half], send_sem=ssem.at[t],
                                                recv_sem=rsem.at[t], device_id=dev, device_id_type=MESH)

        sends = [copy(t, c, (x, y, 1 - c)) for t in range(n)]
        for cp in sends:
            cp.start()
        for t in range(n):
            copy(t, 1 - c, (x, y, c)).wait_recv()
        for cp in sends:
            cp.wait_send()

    return _pcall(body, name=name, out_shape=[_sds(h.shape, f32) for h in halves], in_specs=[ANY] * n,
                  out_specs=[ANY] * n, aliases={t: t for t in range(n)},
                  scratch=[pltpu.SemaphoreType.DMA((n,)), pltpu.SemaphoreType.DMA((n,))])(*halves)


def _row_spec(tm, cols):
    return pl.BlockSpec((tm, cols), lambda i: (i, 0))


def _vec_spec(cols, rows=1):
    return pl.BlockSpec((rows, cols), lambda i: (0, 0))


def _hnorm(x, g, shift, scale):
    T, tm = x.shape[0], 256

    def body(x_ref, g_ref, sh_ref, sc_ref, h_ref):
        xv = x_ref[...]
        r = lax.rsqrt(jnp.mean(xv * xv, axis=-1, keepdims=True) + EPS)
        a = (xv * r) * g_ref[...]
        h_ref[...] = (a * (1.0 + sc_ref[...]) + sh_ref[...]).astype(bf16)

    return _pcall(body, name="hnorm", out_shape=_sds((T, D), bf16), grid=(T // tm,),
                  in_specs=[_row_spec(tm, D), _vec_spec(D), _vec_spec(D), _vec_spec(D)],
                  out_specs=_row_spec(tm, D))(x, g, shift, scale)


def _out_proj(y2, wo, x, gate):
    T, tm = x.shape[0], 512

    def body(y_ref, w_ref, x_ref, g_ref, xo_ref, o_ref):
        o = jnp.dot(y_ref[0], w_ref[0], preferred_element_type=f32)
        o = o + jnp.dot(y_ref[1], w_ref[1], preferred_element_type=f32)
        o_ref[...] = o
        xo_ref[...] = x_ref[...] + g_ref[...] * o

    return _pcall(body, name="out_proj", out_shape=[_sds((T, D), f32), _sds((T, D), f32)], grid=(T // tm,),
                  in_specs=[pl.BlockSpec((2, tm, D), lambda i: (0, i, 0)), pl.BlockSpec((2, D, D), lambda i: (0, 0, 0)),
                            _row_spec(tm, D), _vec_spec(D)],
                  out_specs=[_row_spec(tm, D), _row_spec(tm, D)], vmem_mb=40)(y2, wo, x, gate)


def _loss_bwd(x, target, g):
    T, tm = x.shape[0], 256

    def body(x_ref, t_ref, g_ref, dx_ref, loss_ref, dg_ref):
        @pl.when(pl.program_id(0) == 0)
        def _():
            loss_ref[...] = jnp.zeros_like(loss_ref)
            dg_ref[...] = jnp.zeros_like(dg_ref)

        xv, gv = x_ref[...], g_ref[...]
        r = lax.rsqrt(jnp.mean(xv * xv, axis=-1, keepdims=True) + EPS)
        xn = xv * r
        err = xn * gv - t_ref[...]
        dy = err * (1.0 / D)
        dxn = dy * gv
        dx_ref[...] = r * (dxn - xn * jnp.mean(dxn * xn, axis=-1, keepdims=True))
        dg_ref[...] += jnp.sum(dy * xn, axis=0, keepdims=True)
        loss_ref[...] += (0.5 / D) * jnp.sum(jnp.sum(err * err, axis=1, keepdims=True), axis=0, keepdims=True)

    return _pcall(body, name="loss_bwd", out_shape=[_sds((T, D), f32), _sds((1, 1), f32), _sds((1, D), f32)],
                  grid=(T // tm,), in_specs=[_row_spec(tm, D), _row_spec(tm, D), _vec_spec(D)],
                  out_specs=[_row_spec(tm, D), pl.BlockSpec((1, 1), lambda i: (0, 0)), _vec_spec(D)])(x, target, g)


def _gate_bwd(gin, o, gate):
    T, tm = gin.shape[0], 512

    def body(gin_ref, o_ref, gate_ref, dob_ref, dgate_ref):
        @pl.when(pl.program_id(0) == 0)
        def _():
            dgate_ref[...] = jnp.zeros_like(dgate_ref)

        gv = gin_ref[...]
        dob_ref[...] = (gv * gate_ref[...]).astype(bf16)
        dgate_ref[...] += jnp.sum(gv * o_ref[...], axis=0, keepdims=True)

    return _pcall(body, name="gate_bwd", out_shape=[_sds((T, D), bf16), _sds((1, D), f32)], grid=(T // tm,),
                  in_specs=[_row_spec(tm, D), _row_spec(tm, D), _vec_spec(D)],
                  out_specs=[_row_spec(tm, D), _vec_spec(D)])(gin, o, gate)


def _norm_bwd(x, dh, gin, g, scale):
    T, tm = x.shape[0], 256

    def body(x_ref, dh_ref, gin_ref, g_ref, sc_ref, dx_ref, st_ref):
        @pl.when(pl.program_id(0) == 0)
        def _():
            st_ref[...] = jnp.zeros_like(st_ref)

        xv, gv, dhv = x_ref[...], g_ref[...], dh_ref[...]
        r = lax.rsqrt(jnp.mean(xv * xv, axis=-1, keepdims=True) + EPS)
        xn = xv * r
        da = dhv * (1.0 + sc_ref[...])
        dxn = da * gv
        dx_ref[...] = gin_ref[...] + r * (dxn - xn * jnp.mean(dxn * xn, axis=-1, keepdims=True))
        st_ref[0:1, :] += jnp.sum(dhv, axis=0, keepdims=True)
        st_ref[1:2, :] += jnp.sum(dhv * (xn * gv), axis=0, keepdims=True)
        st_ref[2:3, :] += jnp.sum(da * xn, axis=0, keepdims=True)

    return _pcall(body, name="norm_bwd", out_shape=[_sds((T, D), f32), _sds((8, D), f32)], grid=(T // tm,),
                  in_specs=[_row_spec(tm, D), _row_spec(tm, D), _row_spec(tm, D), _vec_spec(D), _vec_spec(D)],
                  out_specs=[_row_spec(tm, D), _vec_spec(D, 8)])(x, dh, gin, g, scale)


def _cast_place(place, w, layer):
    _, rows, cols = w.shape
    tr = 256

    def body(place_ref, w_ref, o_ref):
        o_ref[...] = w_ref[...].astype(bf16)

    return _pcall(body, name="cast_place", out_shape=_sds((4, rows, cols), bf16), grid=(rows // tr,), prefetch=1,
                  in_specs=[pl.BlockSpec((None, tr, cols), lambda i, pr: (layer, i, 0))],
                  out_specs=pl.BlockSpec((None, tr, cols), lambda i, pr: (pr[0], i, 0)))(place, w)


def _rs_add(place, grad, theirs):
    _, rows, cols = theirs.shape
    tr = 256
    spec = pl.BlockSpec((None, tr, cols), lambda s, i, pr: (s, i, 0))

    def body(place_ref, a_ref, b_ref, o_ref):
        o_ref[...] = (a_ref[...].astype(f32) + b_ref[...].astype(f32)).astype(bf16)

    return _pcall(body, name="rs_add", out_shape=_sds(theirs.shape, bf16), grid=(4, rows // tr), prefetch=1,
                  in_specs=[pl.BlockSpec((None, None, tr, cols), lambda s, i, pr: (s, pr[1], i, 0)), spec],
                  out_specs=spec)(place, grad, theirs)


def _rs_sum(place, pairs, slots):
    _, rows, cols = slots.shape
    tr = 256

    def body(place_ref, p_ref, q_ref, o_ref):
        o_ref[...] = ((p_ref[...].astype(f32) + q_ref[0].astype(f32)) + q_ref[1].astype(f32)) + q_ref[2].astype(f32)

    return _pcall(body, name="rs_sum", out_shape=_sds((2, rows, cols), f32), grid=(rows // tr,), prefetch=1,
                  in_specs=[pl.BlockSpec((None, tr, cols), lambda i, pr: (pr[0], i, 0)),
                            pl.BlockSpec((3, tr, cols), lambda i, pr: (0, i, 0))],
                  out_specs=pl.BlockSpec((None, tr, cols), lambda i, pr: (pr[1], i, 0)))(place, pairs, slots)


def _adamw_math(w, g, m, v):
    m = ADAM_B1 * m + (1.0 - ADAM_B1) * g
    v = ADAM_B2 * v + (1.0 - ADAM_B2) * jnp.square(g)
    m_hat = m / (1.0 - ADAM_B1 ** ADAM_STEP)
    v_hat = v / (1.0 - ADAM_B2 ** ADAM_STEP)
    delta = -ADAM_LR * (m_hat / (jnp.sqrt(v_hat) + ADAM_EPS) + ADAM_WD * w)
    return delta, m, v


def _adamw_layer(layer, w, g, m, v, so_far):
    _, rows, cols = w.shape
    tr = 128
    spec = pl.BlockSpec((None, tr, cols), lambda i: (layer, i, 0))

    def body(w_ref, g_ref, m_ref, v_ref, *rest):
        go_ref, d_ref, mo_ref, vo_ref = rest[-4:]
        g = g_ref[...]
        go_ref[...] = g
        d_ref[...], mo_ref[...], vo_ref[...] = _adamw_math(w_ref[...], g, m_ref[...], v_ref[...])

    args, in_specs, aliases = [w, g, m, v], [spec, pl.BlockSpec((tr, cols), lambda i: (i, 0)), spec, spec], None
    if so_far is not None:
        args += list(so_far)
        in_specs += [ANY] * 4
        aliases = {4 + k: k for k in range(4)}
    return _pcall(body, name="adamw", out_shape=[_sds(w.shape, f32)] * 4, grid=(rows // tr,), in_specs=in_specs,
                  out_specs=[spec] * 4, aliases=aliases)(*args)


def _adamw_small(items):
    n = len(items)

    def body(*refs):
        ins, outs = refs[:4 * n], refs[4 * n:]
        for t in range(n):
            w_ref, g_ref, m_ref, v_ref = ins[4 * t:4 * t + 4]
            if len(g_ref.shape) == len(w_ref.shape) + 1:
                g = g_ref[0]
                for b in range(1, g_ref.shape[0]):
                    g = g + g_ref[b]
            else:
                g = g_ref[...]
            d, m, v = _adamw_math(w_ref[...], g, m_ref[...], v_ref[...])
            outs[4 * t][...], outs[4 * t + 1][...], outs[4 * t + 2][...], outs[4 * t + 3][...] = g, d, m, v

    out_shape = [_sds(w.shape, f32) for (w, _, _, _) in items for _ in range(4)]
    flat = [a for it in items for a in it]
    res = _pcall(body, name="adamw_small", out_shape=out_shape, in_specs=[VMEM] * (4 * n),
                 out_specs=[VMEM] * (4 * n))(*flat)
    return [tuple(res[4 * t:4 * t + 4]) for t in range(n)]


NN = ((1,), (0,))
NT = ((1,), (1,))
TN = ((0,), (0,))


def _mm(name, a, b, *, grid, a_spec, b_spec, out_shape, out_spec, dims, vmem_mb=48):
    def body(a_ref, b_ref, o_ref):
        r = lax.dot_general(a_ref[...], b_ref[...], (dims, ((), ())), preferred_element_type=f32)
        o_ref[...] = r.astype(o_ref.dtype)

    return _pcall(body, name=name, out_shape=out_shape, grid=grid, in_specs=[a_spec, b_spec], out_specs=out_spec,
                  vmem_mb=vmem_mb)(a, b)


def _whole(shape):
    return pl.BlockSpec(shape, lambda j: (0,) * len(shape))


def _split_spec(rows, tile, per_split):
    return pl.BlockSpec((None, rows, tile), lambda j: (j // per_split, 0, j % per_split))


class _Proj:
    def __init__(self, n, splits, tile):
        self.n, self.splits, self.tile = n, splits, tile
        self.steps = n // tile
        self.w_per = n // 4 // tile
        self.a_per = n // splits // tile
        assert self.w_per * tile * 4 == n and self.a_per * tile * splits == n

    def fwd(self, hb, wg):
        T = hb.shape[0]
        return _mm("proj_fwd", hb, wg, grid=(self.steps,), a_spec=_whole((T, D)),
                   b_spec=_split_spec(D, self.tile, self.w_per),
                   out_shape=_sds((self.splits, T, self.n // self.splits), f32),
                   out_spec=_split_spec(T, self.tile, self.a_per), dims=NN)

    def dw(self, hb, dp):
        T = hb.shape[0]
        return _mm("proj_dw", hb, dp, grid=(self.steps,), a_spec=_whole((T, D)),
                   b_spec=_split_spec(T, self.tile, self.a_per), out_shape=_sds((4, D, self.n // 4), bf16),
                   out_spec=_split_spec(D, self.tile, self.w_per), dims=TN)

    def dh(self, dp, wg):
        T = dp.shape[1]
        sub, tile, w_per = DH_WIDE // self.tile, self.tile, self.w_per
        a_per = self.n // self.splits // DH_WIDE
        assert sub * tile == DH_WIDE and a_per * DH_WIDE * self.splits == self.n

        def w_tile(q):
            return pl.BlockSpec((None, D, tile), lambda k: ((sub * k + q) // w_per, 0, (sub * k + q) % w_per))

        def body(a_ref, *rest):
            o_ref = rest[sub]
            w = jnp.concatenate([rest[q][...] for q in range(sub)], axis=1)
            r = lax.dot_general(a_ref[...], w, (NT, ((), ())), preferred_element_type=f32)

            @pl.when(pl.program_id(0) == 0)
            def _():
                o_ref[...] = r

            @pl.when(pl.program_id(0) > 0)
            def _():
                o_ref[...] += r

        return _pcall(body, name="proj_dh", out_shape=_sds((T, D), f32), grid=(self.n // DH_WIDE,),
                      in_specs=[pl.BlockSpec((None, T, DH_WIDE), lambda k: (k // a_per, 0, k % a_per))]
                      + [w_tile(q) for q in range(sub)],
                      out_specs=_whole((T, D)), vmem_mb=48)(dp, *([wg] * sub))


EVEN_PROJ = _Proj(7 * D, 7, 256)
ODD_PROJ = _Proj(4 * D, 2, 512)


def _dy_mm(dob, wo):
    T = dob.shape[0]
    return _mm("out_dy", dob, wo, grid=(4,), a_spec=_whole((T, D)),
               b_spec=pl.BlockSpec((None, 512, D), lambda j: (j, 0, 0)), out_shape=_sds((2, T, D), f32),
               out_spec=_split_spec(T, 512, 2), dims=NT)


def _dwo_mm(y2, dob):
    T = dob.shape[0]
    return _mm("out_dw", y2, dob, grid=(4,), a_spec=_split_spec(T, 512, 2), b_spec=_whole((T, D)),
               out_shape=_sds((4, 512, D), bf16), out_spec=pl.BlockSpec((None, 512, D), lambda j: (j, 0, 0)), dims=TN)


def _head_spec(lead, T):
    return pl.BlockSpec((lead, T, HEAD), lambda h: (0, 0, h))


def _head_vec(rows):
    return pl.BlockSpec((rows, HEAD), lambda h: (0, h))


_HEAD_MAT = pl.BlockSpec((None, HEAD, HEAD), lambda h: (h, 0, 0))


def _causal():
    return lax.broadcasted_iota(jnp.int32, (HEAD, HEAD), 0) >= lax.broadcasted_iota(jnp.int32, (HEAD, HEAD), 1)


def _layernorm_head(v):
    mu = jnp.mean(v, axis=-1, keepdims=True)
    d = v - mu
    rstd = lax.rsqrt(jnp.mean(d * d, axis=-1, keepdims=True) + EPS)
    return d * rstd, rstd


def _even_fwd(p7, conv_w, ln_g, ln_b, sgu_w, sgu_bias):
    T, C = p7.shape[1], CHUNK_ROWS

    def body(p_ref, cw_ref, lg_ref, lb_ref, w_ref, b_ref, y_ref):
        w0, w1, w2 = cw_ref[0:1, :], cw_ref[1:2, :], cw_ref[2:3, :]
        wm = jnp.where(_causal(), w_ref[...], 0.0).astype(bf16)
        bias, lg, lb = b_ref[...], lg_ref[...], lb_ref[...]

        def step(i, halo):
            rows = pl.ds(pl.multiple_of(i * C, C), C)
            tt = p_ref[2, rows, :] * p_ref[0, rows, :]
            ext = jnp.concatenate([halo, tt], axis=0)
            cv = w2 * tt + w1 * pltpu.roll(ext, 1, 0)[HALO_CONV:] + w0 * pltpu.roll(ext, 2, 0)[HALO_CONV:]
            y_ref[0, rows, :] = (p_ref[1, rows, :] * cv * _silu(p_ref[3, rows, :])).astype(bf16)
            vhat, _ = _layernorm_head(p_ref[5, rows, :])
            vn = (vhat * lg + lb).astype(bf16)
            mix = jnp.concatenate([jnp.dot(wm, vn[k * HEAD:(k + 1) * HEAD], preferred_element_type=f32) + bias
                                   for k in range(C // HEAD)], axis=0)
            y_ref[1, rows, :] = (p_ref[4, rows, :] * mix * _silu(p_ref[6, rows, :])).astype(bf16)
            return tt[C - HALO_CONV:]

        lax.fori_loop(0, T // C, step, jnp.zeros((HALO_CONV, HEAD), f32))

    return _pcall(body, name="even_fwd", out_shape=_sds((2, T, D), bf16), grid=(NH,),
                  in_specs=[_head_spec(7, T), _head_vec(3), _head_vec(1), _head_vec(1), _HEAD_MAT, _HEAD_MAT],
                  out_specs=_head_spec(2, T), vmem_mb=32)(p7, conv_w, ln_g, ln_b, sgu_w, sgu_bias)


def _even_bwd(p7, dy2, conv_w, ln_g, ln_b, sgu_w, sgu_bias):
    T, C = p7.shape[1], CHUNK_ROWS
    n_chunks = T // C

    def body(p_ref, dy_ref, cw_ref, lg_ref, lb_ref, w_ref, b_ref,
             dp_ref, dcw_ref, dlg_ref, dlb_ref, dw_ref, dms_ref, dcv_s):
        w0, w1, w2 = cw_ref[0:1, :], cw_ref[1:2, :], cw_ref[2:3, :]
        tri = _causal()
        wm = jnp.where(tri, w_ref[...], 0.0).astype(bf16)
        bias, lg, lb = b_ref[...], lg_ref[...], lb_ref[...]
        dw_ref[...] = jnp.zeros_like(dw_ref)
        dms_ref[...] = jnp.zeros_like(dms_ref)

        def fwd_step(i, carry):
            halo, a0, a1, a2, alg, alb = carry
            rows = pl.ds(pl.multiple_of(i * C, C), C)
            ah, ab, ac, az = p_ref[0, rows, :], p_ref[1, rows, :], p_ref[2, rows, :], p_ref[3, rows, :]
            dya = dy_ref[0, rows, :]
            tt = ac * ah
            ext = jnp.concatenate([halo, tt], axis=0)
            t1, t2 = pltpu.roll(ext, 1, 0)[HALO_CONV:], pltpu.roll(ext, 2, 0)[HALO_CONV:]
            cv = w2 * tt + w1 * t1 + w0 * t2
            sa, dsa = _silu_and_grad(az)
            g1 = dya * sa
            dp_ref[1, rows, :] = (g1 * cv).astype(bf16)
            dp_ref[3, rows, :] = (dya * ab * cv * dsa).astype(bf16)
            dcv = g1 * ab
            dcv_s[rows, :] = dcv
            a2 = a2 + jnp.sum(dcv * tt, axis=0, keepdims=True)
            a1 = a1 + jnp.sum(dcv * t1, axis=0, keepdims=True)
            a0 = a0 + jnp.sum(dcv * t2, axis=0, keepdims=True)

            u, zb, dyb = p_ref[4, rows, :], p_ref[6, rows, :], dy_ref[1, rows, :]
            vhat, rstd = _layernorm_head(p_ref[5, rows, :])
            vn = (vhat * lg + lb).astype(bf16)
            sb, dsb = _silu_and_grad(zb)
            mix = jnp.concatenate([jnp.dot(wm, vn[k * HEAD:(k + 1) * HEAD], preferred_element_type=f32) + bias
                                   for k in range(C // HEAD)], axis=0)
            dp_ref[4, rows, :] = (dyb * mix * sb).astype(bf16)
            dp_ref[6, rows, :] = (dyb * u * mix * dsb).astype(bf16)
            dmix = dyb * u * sb
            dvn_parts = []
            for k in range(C // HEAD):
                dm = dmix[k * HEAD:(k + 1) * HEAD]
                dmb = dm.astype(bf16)
                dvn_parts.append(lax.dot_general(wm, dmb, (TN, ((), ())), preferred_element_type=f32))
                dw_ref[...] += lax.dot_general(dmb, vn[k * HEAD:(k + 1) * HEAD], (NT, ((), ())),
                                               preferred_element_type=f32)
                dms_ref[...] += dm
            dvn = jnp.concatenate(dvn_parts, axis=0)
            alg = alg + jnp.sum(dvn * vhat, axis=0, keepdims=True)
            alb = alb + jnp.sum(dvn, axis=0, keepdims=True)
            dvh = dvn * lg
            dv = rstd * (dvh - jnp.mean(dvh, axis=-1, keepdims=True)
                         - vhat * jnp.mean(dvh * vhat, axis=-1, keepdims=True))
            dp_ref[5, rows, :] = dv.astype(bf16)
            return tt[C - HALO_CONV:], a0, a1, a2, alg, alb

        zrow = jnp.zeros((1, HEAD), f32)
        _, a0, a1, a2, alg, alb = lax.fori_loop(
            0, n_chunks, fwd_step, (jnp.zeros((HALO_CONV, HEAD), f32), zrow, zrow, zrow, zrow, zrow))
        dcw_ref[0:1, :], dcw_ref[1:2, :], dcw_ref[2:3, :] = a0, a1, a2
        dlg_ref[...], dlb_ref[...] = alg, alb
        dw_ref[...] = jnp.where(tri, dw_ref[...], 0.0)

        def bwd_step(k, halo):
            rows = pl.ds(pl.multiple_of((n_chunks - 1 - k) * C, C), C)
            dcv = dcv_s[rows, :]
            ext = jnp.concatenate([dcv, halo], axis=0)
            n1 = pltpu.roll(ext, C + HALO_CONV - 1, 0)[:C]
            n2 = pltpu.roll(ext, C + HALO_CONV - 2, 0)[:C]
            dtt = w2 * dcv + w1 * n1 + w0 * n2
            dp_ref[2, rows, :] = (dtt * p_ref[0, rows, :]).astype(bf16)
            dp_ref[0, rows, :] = (dtt * p_ref[2, rows, :]).astype(bf16)
            return dcv[:HALO_CONV]

        lax.fori_loop(0, n_chunks, bwd_step, jnp.zeros((HALO_CONV, HEAD), f32))

    out_shape = [_sds((7, T, D), bf16), _sds((3, D), f32), _sds((1, D), f32), _sds((1, D), f32),
                 _sds((NH, HEAD, HEAD), f32), _sds((NH, HEAD, HEAD), f32)]
    return _pcall(body, name="even_bwd", out_shape=out_shape, grid=(NH,),
                  in_specs=[_head_spec(7, T), _head_spec(2, T), _head_vec(3), _head_vec(1), _head_vec(1),
                            _HEAD_MAT, _HEAD_MAT],
                  out_specs=[_head_spec(7, T), _head_vec(3), _head_vec(1), _head_vec(1), _HEAD_MAT, _HEAD_MAT],
                  scratch=[pltpu.VMEM((T, HEAD), f32)], vmem_mb=48)(p7, dy2, conv_w, ln_g, ln_b, sgu_w, sgu_bias)


def _window_sum(ext, win, towards_past):
    n, k, s = ext.shape[0], 1, ext
    while k < win:
        s = s + pltpu.roll(s, k if towards_past else n - k, 0)
        k *= 2
    return s


def _pool_count(i, C, win):
    t = i * C + lax.broadcasted_iota(jnp.int32, (C, 1), 0)
    return jnp.minimum(t + 1, win).astype(f32)


def _group_specs(T):
    p_spec = pl.BlockSpec((None, T, GC), lambda g: (0, 0, g))
    z_spec = pl.BlockSpec((None, T, GC), lambda g: (1, 0, g))
    pw_spec = pl.BlockSpec((4, GC // 4, GC), lambda g: (0, g, 0))
    ps_spec = pl.BlockSpec((1, GC), lambda g: (0, g))
    y_spec = pl.BlockSpec((None, T, GC), lambda g: (g // 2, 0, g % 2))
    return p_spec, z_spec, pw_spec, ps_spec, y_spec


def _odd_fwd(p2, pool_wg, pool_scale):
    T, C = p2.shape[1], CHUNK_ROWS
    p_spec, z_spec, pw_spec, ps_spec, y_spec = _group_specs(T)

    def body(p_ref, z_ref, pw_ref, ps_ref, y_ref):
        pw, ps = pw_ref[...].reshape(GC, GC), ps_ref[...]

        def run(win):
            def step(i, halo):
                rows = pl.ds(pl.multiple_of(i * C, C), C)
                p = p_ref[rows, :]
                s = _window_sum(jnp.concatenate([halo, p], axis=0), win, True)[HALO_POOL:]
                pooled = s / _pool_count(i, C, win) - p
                ypre = jnp.dot(pooled.astype(bf16), pw, preferred_element_type=f32)
                y_ref[rows, :] = (ypre * ps * _silu(z_ref[rows, :])).astype(bf16)
                return p[C - HALO_POOL:]

            lax.fori_loop(0, T // C, step, jnp.zeros((HALO_POOL, GC), f32))

        for gi, win in enumerate(WINDOWS):
            pl.when(pl.program_id(0) == gi)(functools.partial(run, win))

    return _pcall(body, name="odd_fwd", out_shape=_sds((2, T, D), bf16), grid=(len(WINDOWS),),
                  in_specs=[p_spec, z_spec, pw_spec, ps_spec], out_specs=y_spec, vmem_mb=40)(p2, p2, pool_wg, pool_scale)


def _odd_bwd(p2, dy2, pool_wg, pool_scale):
    T, C = p2.shape[1], CHUNK_ROWS
    n_chunks = T // C
    p_spec, z_spec, pw_spec, ps_spec, y_spec = _group_specs(T)

    def body(p_ref, z_ref, dy_ref, pw_ref, ps_ref, dp_ref, dpw_ref, dps_ref, q_s, acc_s):
        pw, ps = pw_ref[...].reshape(GC, GC), ps_ref[...]

        def run(win):
            acc_s[...] = jnp.zeros_like(acc_s)

            def fwd_step(i, carry):
                halo, aps = carry
                rows = pl.ds(pl.multiple_of(i * C, C), C)
                p, z, dy = p_ref[rows, :], z_ref[rows, :], dy_ref[rows, :]
                cnt = _pool_count(i, C, win)
                s = _window_sum(jnp.concatenate([halo, p], axis=0), win, True)[HALO_POOL:]
                pb = (s / cnt - p).astype(bf16)
                ypre = jnp.dot(pb, pw, preferred_element_type=f32)
                sz, dsz = _silu_and_grad(z)
                aps = aps + jnp.sum(dy * ypre * sz, axis=0, keepdims=True)
                dp_ref[1, rows, :] = (dy * ypre * ps * dsz).astype(bf16)
                dyp = (dy * ps * sz).astype(bf16)
                acc_s[...] += lax.dot_general(pb, dyp, (TN, ((), ())), preferred_element_type=f32)
                dpool = lax.dot_general(dyp, pw, (NT, ((), ())), preferred_element_type=f32)
                q_s[rows, :] = dpool / cnt
                return p[C - HALO_POOL:], aps

            _, aps = lax.fori_loop(0, n_chunks, fwd_step, (jnp.zeros((HALO_POOL, GC), f32), jnp.zeros((1, GC), f32)))
            dps_ref[...] = aps
            dpw_ref[...] = acc_s[...].reshape(4, GC // 4, GC).astype(bf16)

            def bwd_step(k, halo):
                i = n_chunks - 1 - k
                rows = pl.ds(pl.multiple_of(i * C, C), C)
                q = q_s[rows, :]
                s = _window_sum(jnp.concatenate([q, halo], axis=0), win, False)[:C]
                dp_ref[0, rows, :] = (s - q * _pool_count(i, C, win)).astype(bf16)
                return q[:HALO_POOL]

            lax.fori_loop(0, n_chunks, bwd_step, jnp.zeros((HALO_POOL, GC), f32))

        for gi, win in enumerate(WINDOWS):
            pl.when(pl.program_id(0) == gi)(functools.partial(run, win))

    out_shape = [_sds((2, T, 2 * D), bf16), _sds((4, GC, GC), bf16), _sds((1, 2 * D), f32)]
    return _pcall(body, name="odd_bwd", out_shape=out_shape, grid=(len(WINDOWS),),
                  in_specs=[p_spec, z_spec, y_spec, pw_spec, ps_spec],
                  out_specs=[pl.BlockSpec((2, T, GC), lambda g: (0, 0, g)), pw_spec, ps_spec],
                  scratch=[pltpu.VMEM((T, GC), f32), pltpu.VMEM((GC, GC), f32)], vmem_mb=52)(
                      p2, p2, dy2, pool_wg, pool_scale)


def _ada_fwd(c_all, ada_w):
    cols = ada_w.shape[2]

    def body(c_ref, w_ref, o_ref):
        o_ref[...] = jnp.dot(_silu(c_ref[...]), w_ref[...], preferred_element_type=f32,
                             precision=lax.Precision.HIGHEST)

    return _pcall(body, name="ada_fwd", out_shape=_sds((4, N_DEV, cols), f32), grid=(4,),
                  in_specs=[pl.BlockSpec((N_DEV, D), lambda i: (0, 0)), pl.BlockSpec((None, D, cols), lambda i: (i, 0, 0))],
                  out_specs=pl.BlockSpec((None, N_DEV, cols), lambda i: (i, 0, 0)))(c_all, ada_w)


def _ada_bwd(c_all_t, dmod, w, m, v):
    cols, tr = w.shape[2], 256
    spec = pl.BlockSpec((None, tr, cols), lambda l, i: (l, i, 0))

    def body(c_ref, dm_ref, w_ref, m_ref, v_ref, g_ref, d_ref, mo_ref, vo_ref):
        sc = _silu(c_ref[...])
        g = sc[:, 0:1] * dm_ref[0:1, :]
        for b in range(1, N_DEV):
            g = g + sc[:, b:b + 1] * dm_ref[b:b + 1, :]
        g_ref[...] = g
        d_ref[...], mo_ref[...], vo_ref[...] = _adamw_math(w_ref[...], g, m_ref[...], v_ref[...])

    return _pcall(body, name="ada_bwd", out_shape=[_sds(w.shape, f32)] * 4, grid=(4, D // tr),
                  in_specs=[pl.BlockSpec((tr, N_DEV), lambda l, i: (i, 0)),
                            pl.BlockSpec((None, N_DEV, cols), lambda l, i: (l, 0, 0)), spec, spec, spec],
                  out_specs=[spec] * 4)(c_all_t, dmod, w, m, v)


def _layer_fwd(even, x, mod, g, w):
    shift, scale, gate = mod
    hb = _hnorm(x, g, shift, scale)
    if even:
        w_in, w_out, conv_w, ln_g, ln_b, sgu_w, sgu_b = w
        bias = jnp.broadcast_to(sgu_b[:, :, None], (NH, HEAD, HEAD))
        p = EVEN_PROJ.fwd(hb, w_in)
        y2 = _even_fwd(p, conv_w, ln_g, ln_b, sgu_w, bias)
    else:
        w_in, pool_w, w_out, pool_scale = w
        p = ODD_PROJ.fwd(hb, w_in)
        y2 = _odd_fwd(p, pool_w, pool_scale)
    x_next, o = _out_proj(y2, w_out.reshape(2, D, D), x, gate)
    return x_next, (x, hb, p, y2, o)


def _layer_bwd(even, gin, saved, mod, g, w):
    shift, scale, gate = mod
    x_in, hb, p, y2, o = saved
    dob, dgate = _gate_bwd(gin, o, gate)
    if even:
        w_in, w_out, conv_w, ln_g, ln_b, sgu_w, sgu_b = w
        bias = jnp.broadcast_to(sgu_b[:, :, None], (NH, HEAD, HEAD))
        dy2 = _dy_mm(dob, w_out)
        dp, dconv, dlg, dlb, dsw, dms = _even_bwd(p, dy2, conv_w, ln_g, ln_b, sgu_w, bias)
        proj = EVEN_PROJ
        small = dict(conv_w=dconv, ln_g=dlg, ln_b=dlb, sgu_w=dsw, sgu_b=jnp.sum(dms, axis=-1))
        big = [proj.dw(hb, dp), _dwo_mm(y2, dob)]
    else:
        w_in, pool_w, w_out, pool_scale = w
        dy2 = _dy_mm(dob, w_out)
        dp, dpw, dps = _odd_bwd(p, dy2, pool_w, pool_scale)
        proj = ODD_PROJ
        small = dict(pool_scale=dps)
        big = [proj.dw(hb, dp), dpw, _dwo_mm(y2, dob)]
    dh = proj.dh(dp, w_in)
    gx, stats = _norm_bwd(x_in, dh, gin, g, scale)
    return gx, big, small, jnp.concatenate([stats[0:2], dgate], axis=0), stats[2:3]


def _pack_rows(parts):
    rows = [p.reshape(-1, LANES) for p in parts]
    total = sum(r.shape[0] for r in rows)
    padded = -(-total // (8 * N_DEV)) * (8 * N_DEV)
    if padded > total:
        rows.append(jnp.zeros((padded - total, LANES), f32))
    return jnp.concatenate(rows, axis=0)


def _unpack_rows(buf, shapes):
    out, r = [], 0
    for shp in shapes:
        n = 1
        for d in shp:
            n *= d
        out.append(buf[r:r + n // LANES].reshape(shp))
        r += n // LANES
    return out


def kernel(x, c, norm_g, ada_w, ada_b, ab_w_in, ab_conv_w, ab_ln_g, ab_ln_b, ab_sgu_w, ab_sgu_b, ab_w_out, c_w_in, c_pool_w, c_pool_scale, c_w_out, final_g, loss_target, m_norm_g, m_ada_w, m_ada_b, m_ab_w_in, m_ab_conv_w, m_ab_ln_g, m_ab_ln_b, m_ab_sgu_w, m_ab_sgu_b, m_ab_w_out, m_c_w_in, m_c_pool_w, m_c_pool_scale, m_c_w_out, m_final_g, v_norm_g, v_ada_w, v_ada_b, v_ab_w_in, v_ab_conv_w, v_ab_ln_g, v_ab_ln_b, v_ab_sgu_w, v_ab_sgu_b, v_ab_w_out, v_c_w_in, v_c_pool_w, v_c_pool_scale, v_c_w_out, v_final_g):
    ix, iy, ic = _place()
    chip, dev = 2 * ix + iy, 4 * ix + 2 * iy + ic
    n_even, n_odd = ab_w_in.shape[0], c_w_in.shape[0]
    depth = n_even + n_odd
    acols = ada_w.shape[2]

    place = jnp.stack([chip, ic]).astype(jnp.int32)
    even_names, odd_names = ["ab_w_in", "ab_w_out"], ["c_w_in", "c_pool_w", "c_w_out"]
    params = {"ab_w_in": (ab_w_in, m_ab_w_in, v_ab_w_in), "ab_w_out": (ab_w_out, m_ab_w_out, v_ab_w_out),
              "c_w_in": (c_w_in, m_c_w_in, v_c_w_in), "c_w_out": (c_w_out, m_c_w_out, v_c_w_out),
              "c_pool_w": tuple(a.reshape(n_odd, GC, GC) for a in (c_pool_w, m_c_pool_w, v_c_pool_w))}

    c_all = _gather8(c, "gather_c").reshape(N_DEV, D)
    modp = _ada_fwd(c_all, ada_w)
    modg = _gather8(modp, "gather_mod")
    mod_rows = lax.dynamic_index_in_dim(modg[0::2], dev, axis=2, keepdims=False)
    mod = jnp.transpose(mod_rows, (1, 0, 2)).reshape(depth, 3 * D) + ada_b
    mods = [(mod[i:i + 1, 0:D], mod[i:i + 1, D:2 * D], mod[i:i + 1, 2 * D:3 * D]) for i in range(depth)]

    def shard_cols(a, width):
        return lax.dynamic_slice_in_dim(a, chip * width, width, axis=a.ndim - 1)

    small_sharded = jnp.concatenate([ab_conv_w.reshape(1, -1), c_pool_scale.reshape(1, -1)], axis=1)
    small_all = _gather8(small_sharded, "gather_small")[0::2, 0]
    n_conv = ab_conv_w.size
    conv_all = small_all[:, :n_conv].reshape(4, n_even, 3, D // 4)
    conv_full = jnp.transpose(conv_all, (1, 2, 0, 3)).reshape(n_even, 3, D)
    scale_all = small_all[:, n_conv:].reshape(4, n_odd, 2 * D // 4)
    scale_full = jnp.transpose(scale_all, (1, 0, 2)).reshape(n_odd, 2 * D)

    def placed(i):
        ws = [params[nm][0] for nm in (even_names if i % 2 == 0 else odd_names)]
        return [_cast_place(place, w, i // 2).reshape(4, 2, w.shape[1] // 2, w.shape[2]) for w in ws]

    sems_0, inflight_0, tok = _ag_start([placed(0)], scale_full, "ag_start_0")
    sems_r, inflight_r, tok = _ag_start([placed(i) for i in range(1, depth)], tok, "ag_start_rest")
    ag_sems, inflight = sems_0 + sems_r, inflight_0 + inflight_r

    x_cur, after, saved, weights = x[0], tok, [], []
    for i in range(depth):
        j = i // 2
        arrived = _ag_wait(inflight[i], ag_sems[i], after, f"ag_wait_{i}")
        full = [g.reshape(4, 2 * g.shape[2], g.shape[3]) for g in _ag_forward(arrived, "ag_forward")]
        if i % 2 == 0:
            w = (full[0], full[1], conv_full[j], ab_ln_g[j:j + 1], ab_ln_b[j:j + 1], ab_sgu_w[j], ab_sgu_b[j])
        else:
            w = (full[0], full[1], full[2], scale_full[j:j + 1])
        weights.append(w)
        x_cur, sv = _layer_fwd(i % 2 == 0, x_cur, mods[i], norm_g[i:i + 1], w)
        saved.append(sv)
        after = x_cur
    gin, loss, dfinal_g = _loss_bwd(x_cur, loss_target[0], final_g.reshape(1, D))
    loss = lax.psum(loss[0, 0], ("x", "y", "c"))

    stacked = {}

    def finish(i, sems, pairs, lands, after):
        pairs, slots = _rs_chip_wait(sems, pairs, lands, after, f"rs_chip_wait_{i}")
        halves = [_rs_sum(place, p, q) for p, q in zip(pairs, slots)]
        for nm, g in zip(even_names if i % 2 == 0 else odd_names, _rs_half_exchange(halves, "rs_half_exchange")):
            w, m, v = params[nm]
            stacked[nm] = _adamw_layer(i // 2, w, g.reshape(w.shape[1], w.shape[2]), m, v, stacked.get(nm))

    small_g, dmod, dnorm_g, pending, tok = [None] * depth, [None] * depth, [None] * depth, None, None
    for i in reversed(range(depth)):
        shift, scale, gate = mods[i]
        if tok is not None:
            gate = gate + tok[0:1, 0:1]
        gin, big_g, small_g[i], dmod[i], dnorm_g[i] = _layer_bwd(i % 2 == 0, gin, saved[i], (shift, scale, gate),
                                                                  norm_g[i:i + 1], weights[i])
        big_g = [g.reshape(4, 2, g.shape[1] // 2, g.shape[2]) for g in big_g]
        theirs = _rs_pair_exchange(big_g, "rs_pair_exchange")
        pairs = [_rs_add(place, a, b) for a, b in zip(big_g, theirs)]
        sems, pairs, lands, tok = _rs_chip_start(pairs, f"rs_chip_start_{i}")
        if pending is not None:
            finish(*pending, tok)
        pending = (i, sems, pairs, lands)
    grad_x = gin
    dmod, dnorm_g = jnp.stack(dmod), jnp.concatenate(dnorm_g, axis=0)

    small_parts = [dnorm_g + tok[0:1, 0:1], dfinal_g,
                   jnp.stack([small_g[2 * j]["conv_w"] for j in range(n_even)]),
                   jnp.concatenate([small_g[2 * j]["ln_g"] for j in range(n_even)], axis=0),
                   jnp.concatenate([small_g[2 * j]["ln_b"] for j in range(n_even)], axis=0),
                   jnp.stack([small_g[2 * j]["sgu_w"] for j in range(n_even)]),
                   jnp.stack([small_g[2 * j]["sgu_b"] for j in range(n_even)]),
                   jnp.concatenate([small_g[2 * j + 1]["pool_scale"] for j in range(n_odd)], axis=0)]
    small_shapes = [p.shape for p in small_parts]
    reduced = _allreduce8(_pack_rows(small_parts), "allreduce_small")
    g_norm_g, g_final_g, g_conv_full, g_ln_g, g_ln_b, g_sgu_w, g_sgu_b, g_scale_full = _unpack_rows(reduced, small_shapes)
    g_conv = shard_cols(g_conv_full, D // 4)
    g_scale = shard_cols(g_scale_full, 2 * D // 4)
    dmod_all = _gather8(dmod.reshape(depth * 3 * D // LANES, LANES), "gather_dmod").reshape(N_DEV, depth, 3 * D)

    def two_d(a):
        return a.reshape(-1, a.shape[-1])

    small = [(norm_g, g_norm_g, m_norm_g, v_norm_g),
             (ada_b, dmod_all, m_ada_b, v_ada_b),
             (two_d(ab_conv_w), two_d(g_conv), two_d(m_ab_conv_w), two_d(v_ab_conv_w)),
             (ab_ln_g, g_ln_g, m_ab_ln_g, v_ab_ln_g),
             (ab_ln_b, g_ln_b, m_ab_ln_b, v_ab_ln_b),
             (two_d(ab_sgu_w), two_d(g_sgu_w), two_d(m_ab_sgu_w), two_d(v_ab_sgu_w)),
             (two_d(ab_sgu_b), two_d(g_sgu_b), two_d(m_ab_sgu_b), two_d(v_ab_sgu_b)),
             (c_pool_scale, g_scale, m_c_pool_scale, v_c_pool_scale),
             (final_g.reshape(1, D), g_final_g, m_final_g.reshape(1, D), v_final_g.reshape(1, D))]
    small_res = _adamw_small(small)
    small_shapes_out = [norm_g.shape, ada_b.shape, ab_conv_w.shape, ab_ln_g.shape, ab_ln_b.shape, ab_sgu_w.shape,
                        ab_sgu_b.shape, c_pool_scale.shape, final_g.shape]
    (r_norm_g, r_ada_b, r_conv, r_ln_g, r_ln_b, r_sgu_w, r_sgu_b, r_scale, r_final_g) = [
        tuple(a.reshape(shp) for a in res) for res, shp in zip(small_res, small_shapes_out)]

    dmod_cols = jnp.transpose(shard_cols(dmod_all, acols), (1, 0, 2))
    r_ada_w = _ada_bwd(c_all.T, dmod_cols, ada_w, m_ada_w, v_ada_w)

    finish(*pending, r_ada_w[1])
    r_ab_w_in, r_ab_w_out, r_c_w_in, r_c_w_out = (stacked[nm] for nm in ("ab_w_in", "ab_w_out", "c_w_in", "c_w_out"))
    r_c_pool_w = tuple(a.reshape(c_pool_w.shape) for a in stacked["c_pool_w"])

    order = [r_norm_g, r_ada_w, r_ada_b, r_ab_w_in, r_conv, r_ln_g, r_ln_b, r_sgu_w, r_sgu_b, r_ab_w_out,
             r_c_w_in, r_c_pool_w, r_scale, r_c_w_out, r_final_g]
    outs = [loss, grad_x[None]]
    for field in range(4):
        outs += [r[field] for r in order]
    return tuple(outs)
```

```python
import functools

import jax
import jax.numpy as jnp
from jax import lax
from jax.experimental import pallas as pl
from jax.experimental.pallas import tpu as pltpu

f32, bf16 = jnp.float32, jnp.bfloat16

D = 1024
HEAD = 128
NH = 8
WINDOWS = (2, 4, 8, 16)
GC = 512
EPS = 1e-6
HALO_CONV = 8
HALO_POOL = 16
CHUNK_ROWS = 512
DH_WIDE = 1024
N_DEV = 8
LANES = 128

ADAM_LR, ADAM_B1, ADAM_B2, ADAM_EPS, ADAM_WD, ADAM_STEP = 0.001, 0.9, 0.999, 1e-08, 0.01, 10

MESH = pl.DeviceIdType.MESH
ANY = pl.BlockSpec(memory_space=pl.ANY)
VMEM = pl.BlockSpec(memory_space=pltpu.VMEM)
MIB = 2 ** 20


def _pcall(body, *, name, out_shape, grid=None, in_specs=None, out_specs=None, scratch=(), vmem_mb=None,
           aliases=None, prefetch=0):
    kw = {}
    if prefetch:
        kw["grid_spec"] = pltpu.PrefetchScalarGridSpec(num_scalar_prefetch=prefetch, grid=grid, in_specs=in_specs,
                                                       out_specs=out_specs, scratch_shapes=list(scratch))
    else:
        if grid is not None:
            kw["grid"] = grid
        if in_specs is not None:
            kw["in_specs"] = in_specs
        if out_specs is not None:
            kw["out_specs"] = out_specs
        if scratch:
            kw["scratch_shapes"] = list(scratch)
    if aliases:
        kw["input_output_aliases"] = aliases
    params = pltpu.CompilerParams(vmem_limit_bytes=None if vmem_mb is None else vmem_mb * MIB)
    return pl.pallas_call(body, name=name, out_shape=out_shape, compiler_params=params, **kw)


def _sds(shape, dtype):
    return jax.ShapeDtypeStruct(tuple(shape), dtype)


def _sigmoid(z):
    return pl.reciprocal(1.0 + jnp.exp(-z), approx=True)


def _silu(z):
    return z * _sigmoid(z)


def _silu_and_grad(z):
    s = _sigmoid(z)
    return z * s, s * (1.0 + z * (1.0 - s))


def _place():
    return lax.axis_index("x"), lax.axis_index("y"), lax.axis_index("c")


def _gather8(blk, name):
    def body(x_ref, o_ref, ssem, rsem):
        x, y, c = _place()
        me = 4 * x + 2 * y + c
        o_ref[me] = x_ref[...]
        sends = []
        for k in range(1, N_DEV):
            px = 1 - x if k & 4 else x
            py = 1 - y if k & 2 else y
            pc = 1 - c if k & 1 else c
            cp = pltpu.make_async_remote_copy(src_ref=x_ref, dst_ref=o_ref.at[me], send_sem=ssem.at[k - 1],
                                              recv_sem=rsem.at[k - 1], device_id=(px, py, pc), device_id_type=MESH)
            cp.start()
            sends.append((cp, 4 * px + 2 * py + pc))
        for k, (cp, peer) in enumerate(sends):
            pltpu.make_async_remote_copy(src_ref=x_ref, dst_ref=o_ref.at[peer], send_sem=ssem.at[k],
                                         recv_sem=rsem.at[k], device_id=(x, y, c), device_id_type=MESH).wait_recv()
        for cp, _ in sends:
            cp.wait_send()

    return _pcall(body, name=name, out_shape=_sds((N_DEV,) + blk.shape, blk.dtype), in_specs=[VMEM], out_specs=VMEM,
                  scratch=[pltpu.SemaphoreType.DMA((N_DEV - 1,)), pltpu.SemaphoreType.DMA((N_DEV - 1,))])(blk)


def _allreduce8(buf, name):
    rows = buf.shape[0]
    rb = rows // N_DEV
    assert rb * N_DEV == rows and rb % 8 == 0

    def body(x_ref, o_ref, stage, ssem, rsem):
        x, y, c = _place()
        me = 4 * x + 2 * y + c
        peers = []
        for k in range(1, N_DEV):
            px = 1 - x if k & 4 else x
            py = 1 - y if k & 2 else y
            pc = 1 - c if k & 1 else c
            peers.append(((px, py, pc), 4 * px + 2 * py + pc))

        def blk(ref, idx):
            return ref.at[pl.ds(pl.multiple_of(idx * rb, 8), rb), :]

        def copy(phase, k, src, dst, dev):
            return pltpu.make_async_remote_copy(src_ref=src, dst_ref=dst, send_sem=ssem.at[phase, k],
                                                recv_sem=rsem.at[phase, k], device_id=dev, device_id_type=MESH)

        stage[me] = blk(x_ref, me)[...]
        scatter = [copy(0, k, blk(x_ref, pidx), stage.at[me], dev) for k, (dev, pidx) in enumerate(peers)]
        for cp in scatter:
            cp.start()
        for k, (dev, pidx) in enumerate(peers):
            copy(0, k, blk(x_ref, pidx), stage.at[pidx], dev).wait_recv()
        total = stage[0]
        for j in range(1, N_DEV):
            total = total + stage[j]
        blk(o_ref, me)[...] = total
        gather = [copy(1, k, blk(o_ref, me), blk(o_ref, me), dev) for k, (dev, pidx) in enumerate(peers)]
        for cp in gather:
            cp.start()
        for k, (dev, pidx) in enumerate(peers):
            copy(1, k, blk(o_ref, pidx), blk(o_ref, pidx), dev).wait_recv()
        for cp in scatter + gather:
            cp.wait_send()

    return _pcall(body, name=name, out_shape=_sds(buf.shape, f32), in_specs=[VMEM], out_specs=VMEM,
                  scratch=[pltpu.VMEM((N_DEV, rb, LANES), f32), pltpu.SemaphoreType.DMA((2, N_DEV - 1)),
                           pltpu.SemaphoreType.DMA((2, N_DEV - 1))])(buf)


def _other_chips(x, y):
    return [((1 - x, y), 2 * (1 - x) + y), ((x, 1 - y), 2 * x + (1 - y)), ((1 - x, 1 - y), 2 * (1 - x) + (1 - y))]


HBM = pl.BlockSpec(memory_space=pltpu.HBM)
SEM = pl.BlockSpec(memory_space=pltpu.SEMAPHORE)
EFFECT = pltpu.SideEffectType.DATAFLOW_SIDE_EFFECTING


def _in_hbm(a):
    return pltpu.with_memory_space_constraint(a, pltpu.HBM)


def _ag_start(layers, after, name):
    flat = [t for lay in layers for t in lay]
    n, nl = len(flat), len(layers)

    def body(*refs):
        src = refs[:n]
        sems = refs[n + 1:n + 1 + 2 * nl]
        token = refs[-1]
        x, y, c = _place()
        s_me = 2 * x + y
        t = 0
        for i, lay in enumerate(layers):
            for k in range(len(lay)):
                for j, ((px, py), _) in enumerate(_other_chips(x, y)):
                    pltpu.make_async_remote_copy(src_ref=src[t].at[s_me, c], dst_ref=src[t].at[s_me, c],
                                                 send_sem=sems[2 * i].at[3 * k + j], recv_sem=sems[2 * i + 1].at[3 * k + j],
                                                 device_id=(px, py, c), device_id_type=MESH).start()
                t += 1
        token[...] = jnp.zeros_like(token)

    sem_shapes = [pltpu.SemaphoreType.DMA((3 * len(lay),)) for lay in layers for _ in range(2)]
    out_shape = sem_shapes + [pltpu.HBM(t.shape, t.dtype) for t in flat] + [_sds((8, LANES), f32)]
    outs = pl.pallas_call(
        body, name=name, out_shape=out_shape, in_specs=[HBM] * n + [ANY],
        out_specs=[SEM] * (2 * nl) + [HBM] * n + [VMEM], input_output_aliases={t: 2 * nl + t for t in range(n)},
        compiler_params=pltpu.CompilerParams(has_side_effects=EFFECT))(*[_in_hbm(t) for t in flat], after)
    sems = [(outs[2 * i], outs[2 * i + 1]) for i in range(nl)]
    thru, t = [], 2 * nl
    for lay in layers:
        thru.append(list(outs[t:t + len(lay)]))
        t += len(lay)
    return sems, thru, outs[-1]


def _ag_wait(inflight, sems, after, name):
    n = len(inflight)

    def body(*refs):
        src, ssem, rsem = refs[:n], refs[n], refs[n + 1]
        x, y, c = _place()
        s_me = 2 * x + y
        for k in range(n):
            for j, (_, s_p) in enumerate(_other_chips(x, y)):
                cp = pltpu.make_async_remote_copy(src_ref=src[k].at[s_me, c], dst_ref=src[k].at[s_p, c],
                                                  send_sem=ssem.at[3 * k + j], recv_sem=rsem.at[3 * k + j],
                                                  device_id=(x, y, c), device_id_type=MESH)
                cp.wait_send()
                cp.wait_recv()

    return pl.pallas_call(
        body, name=name, out_shape=[pltpu.HBM(t.shape, t.dtype) for t in inflight],
        in_specs=[HBM] * n + [SEM, SEM, ANY], out_specs=[HBM] * n, input_output_aliases={t: t for t in range(n)},
        compiler_params=pltpu.CompilerParams(has_side_effects=EFFECT))(*inflight, sems[0], sems[1], after)


def _ag_forward(arrived, name):
    n = len(arrived)

    def body(*refs):
        o = refs[n:2 * n]
        ssem, rsem = refs[2 * n:]
        x, y, c = _place()

        def copy(t, j, s, half, dev):
            return pltpu.make_async_remote_copy(src_ref=o[t].at[s, c], dst_ref=o[t].at[s, half], send_sem=ssem.at[t, j],
                                                recv_sem=rsem.at[t, j], device_id=dev, device_id_type=MESH)

        chips = _other_chips(x, y)
        sends = [copy(t, j, s_p, c, (x, y, 1 - c)) for t in range(n) for j, (_, s_p) in enumerate(chips)]
        for cp in sends:
            cp.start()
        for t in range(n):
            for j, (_, s_p) in enumerate(chips):
                copy(t, j, s_p, 1 - c, (x, y, c)).wait_recv()
        for cp in sends:
            cp.wait_send()

    return _pcall(body, name=name, out_shape=[_sds(p.shape, bf16) for p in arrived], in_specs=[ANY] * n,
                  out_specs=[ANY] * n, aliases={t: t for t in range(n)},
                  scratch=[pltpu.SemaphoreType.DMA((n, 3)), pltpu.SemaphoreType.DMA((n, 3))])(*arrived)


def _rs_pair_start(grads, name):
    n = len(grads)

    def body(*refs):
        g, theirs = refs[:n], refs[n:2 * n]
        ssem, rsem, token = refs[2 * n], refs[2 * n + 1], refs[-1]
        x, y, c = _place()
        for t in range(n):
            pltpu.make_async_remote_copy(src_ref=g[t].at[:, 1 - c], dst_ref=theirs[t], send_sem=ssem.at[t],
                                         recv_sem=rsem.at[t], device_id=(x, y, 1 - c), device_id_type=MESH).start()
        token[...] = jnp.zeros_like(token)

    lands = [lax.empty((4,) + g.shape[2:], bf16) for g in grads]
    out_shape = ([pltpu.SemaphoreType.DMA((n,))] * 2 + [pltpu.HBM(g.shape, bf16) for g in grads]
                 + [pltpu.HBM(q.shape, bf16) for q in lands] + [_sds((8, LANES), f32)])
    outs = pl.pallas_call(
        body, name=name, out_shape=out_shape, in_specs=[HBM] * (2 * n), out_specs=[SEM, SEM] + [HBM] * (2 * n) + [VMEM],
        input_output_aliases={t: 2 + t for t in range(2 * n)},
        compiler_params=pltpu.CompilerParams(has_side_effects=EFFECT))(*[_in_hbm(a) for a in list(grads) + lands])
    return (outs[0], outs[1]), list(outs[2:2 + n]), list(outs[2 + n:2 + 2 * n]), outs[-1]


def _rs_pair_wait(sems, grads, lands, after, name):
    n = len(grads)

    def body(*refs):
        g, theirs = refs[:n], refs[n:2 * n]
        ssem, rsem = refs[2 * n], refs[2 * n + 1]
        x, y, c = _place()
        for t in range(n):
            cp = pltpu.make_async_remote_copy(src_ref=g[t].at[:, 1 - c], dst_ref=theirs[t], send_sem=ssem.at[t],
                                              recv_sem=rsem.at[t], device_id=(x, y, c), device_id_type=MESH)
            cp.wait_send()
            cp.wait_recv()

    outs = pl.pallas_call(
        body, name=name, out_shape=[pltpu.HBM(a.shape, bf16) for a in list(grads) + list(lands)],
        in_specs=[HBM] * (2 * n) + [SEM, SEM, ANY], out_specs=[HBM] * (2 * n),
        input_output_aliases={t: t for t in range(2 * n)},
        compiler_params=pltpu.CompilerParams(has_side_effects=EFFECT))(*grads, *lands, sems[0], sems[1], after)
    return list(outs[:n]), list(outs[n:])


def _rs_chip_start(pairs, name):
    n = len(pairs)

    def body(*refs):
        p, q = refs[:n], refs[n:2 * n]
        ssem, rsem, token = refs[2 * n], refs[2 * n + 1], refs[-1]
        x, y, c = _place()
        for t in range(n):
            for j, ((px, py), s_p) in enumerate(_other_chips(x, y)):
                pltpu.make_async_remote_copy(src_ref=p[t].at[s_p], dst_ref=q[t].at[j], send_sem=ssem.at[3 * t + j],
                                             recv_sem=rsem.at[3 * t + j], device_id=(px, py, c), device_id_type=MESH).start()
        token[...] = jnp.zeros_like(token)

    lands = [lax.empty((3,) + p.shape[1:], bf16) for p in pairs]
    out_shape = ([pltpu.SemaphoreType.DMA((3 * n,))] * 2 + [pltpu.HBM(p.shape, bf16) for p in pairs]
                 + [pltpu.HBM(q.shape, bf16) for q in lands] + [_sds((8, LANES), f32)])
    outs = pl.pallas_call(
        body, name=name, out_shape=out_shape, in_specs=[HBM] * (2 * n), out_specs=[SEM, SEM] + [HBM] * (2 * n) + [VMEM],
        input_output_aliases={t: 2 + t for t in range(2 * n)},
        compiler_params=pltpu.CompilerParams(has_side_effects=EFFECT))(*[_in_hbm(a) for a in list(pairs) + lands])
    return (outs[0], outs[1]), list(outs[2:2 + n]), list(outs[2 + n:2 + 2 * n]), outs[-1]


def _rs_chip_wait(sems, pairs, lands, after, name):
    n = len(pairs)

    def body(*refs):
        p, q = refs[:n], refs[n:2 * n]
        ssem, rsem = refs[2 * n], refs[2 * n + 1]
        x, y, c = _place()
        for t in range(n):
            for j, (_, s_p) in enumerate(_other_chips(x, y)):
                cp = pltpu.make_async_remote_copy(src_ref=p[t].at[s_p], dst_ref=q[t].at[j], send_sem=ssem.at[3 * t + j],
                                                  recv_sem=rsem.at[3 * t + j], device_id=(x, y, c), device_id_type=MESH)
                cp.wait_send()
                cp.wait_recv()

    outs = pl.pallas_call(
        body, name=name, out_shape=[pltpu.HBM(a.shape, bf16) for a in list(pairs) + list(lands)],
        in_specs=[HBM] * (2 * n) + [SEM, SEM, ANY], out_specs=[HBM] * (2 * n),
        input_output_aliases={t: t for t in range(2 * n)},
        compiler_params=pltpu.CompilerParams(has_side_effects=EFFECT))(*pairs, *lands, sems[0], sems[1], after)
    return list(outs[:n]), list(outs[n:])


def _rs_half_exchange(halves, name):
    n = len(halves)

    def body(*refs):
        o = refs[n:2 * n]
        ssem, rsem = refs[2 * n:]
        x, y, c = _place()

        def copy(t, half, dev):
            return pltpu.make_async_remote_copy(src_ref=o[t].at[c], dst_ref=o[t].at[half], send_sem=ssem.at[t],
                                                recv_sem=rsem.at[t], device_id=dev, device_id_type=MESH)

        sends = [copy(t, c, (x, y, 1 - c)) for t in range(n)]
        for cp in sends:
            cp.start()
        for t in range(n):
            copy(t, 1 - c, (x, y, c)).wait_recv()
        for cp in sends:
            cp.wait_send()

    return _pcall(body, name=name, out_shape=[_sds(h.shape, f32) for h in halves], in_specs=[ANY] * n,
                  out_specs=[ANY] * n, aliases={t: t for t in range(n)},
                  scratch=[pltpu.SemaphoreType.DMA((n,)), pltpu.SemaphoreType.DMA((n,))])(*halves)


def _row_spec(tm, cols):
    return pl.BlockSpec((tm, cols), lambda i: (i, 0))


def _vec_spec(cols, rows=1):
    return pl.BlockSpec((rows, cols), lambda i: (0, 0))


def _hnorm(x, g, shift, scale):
    T, tm = x.shape[0], 256

    def body(x_ref, g_ref, sh_ref, sc_ref, h_ref):
        xv = x_ref[...]
        r = lax.rsqrt(jnp.mean(xv * xv, axis=-1, keepdims=True) + EPS)
        a = (xv * r) * g_ref[...]
        h_ref[...] = (a * (1.0 + sc_ref[...]) + sh_ref[...]).astype(bf16)

    return _pcall(body, name="hnorm", out_shape=_sds((T, D), bf16), grid=(T // tm,),
                  in_specs=[_row_spec(tm, D), _vec_spec(D), _vec_spec(D), _vec_spec(D)],
                  out_specs=_row_spec(tm, D))(x, g, shift, scale)


def _out_proj(y2, wo, x, gate):
    T, tm = x.shape[0], 512

    def body(y_ref, w_ref, x_ref, g_ref, xo_ref, o_ref):
        o = jnp.dot(y_ref[0], w_ref[0], preferred_element_type=f32)
        o = o + jnp.dot(y_ref[1], w_ref[1], preferred_element_type=f32)
        o_ref[...] = o
        xo_ref[...] = x_ref[...] + g_ref[...] * o

    return _pcall(body, name="out_proj", out_shape=[_sds((T, D), f32), _sds((T, D), f32)], grid=(T // tm,),
                  in_specs=[pl.BlockSpec((2, tm, D), lambda i: (0, i, 0)), pl.BlockSpec((2, D, D), lambda i: (0, 0, 0)),
                            _row_spec(tm, D), _vec_spec(D)],
                  out_specs=[_row_spec(tm, D), _row_spec(tm, D)], vmem_mb=40)(y2, wo, x, gate)


def _loss_bwd(x, target, g):
    T, tm = x.shape[0], 256

    def body(x_ref, t_ref, g_ref, dx_ref, loss_ref, dg_ref):
        @pl.when(pl.program_id(0) == 0)
        def _():
            loss_ref[...] = jnp.zeros_like(loss_ref)
            dg_ref[...] = jnp.zeros_like(dg_ref)

        xv, gv = x_ref[...], g_ref[...]
        r = lax.rsqrt(jnp.mean(xv * xv, axis=-1, keepdims=True) + EPS)
        xn = xv * r
        err = xn * gv - t_ref[...]
        dy = err * (1.0 / D)
        dxn = dy * gv
        dx_ref[...] = r * (dxn - xn * jnp.mean(dxn * xn, axis=-1, keepdims=True))
        dg_ref[...] += jnp.sum(dy * xn, axis=0, keepdims=True)
        loss_ref[...] += (0.5 / D) * jnp.sum(jnp.sum(err * err, axis=1, keepdims=True), axis=0, keepdims=True)

    return _pcall(body, name="loss_bwd", out_shape=[_sds((T, D), f32), _sds((1, 1), f32), _sds((1, D), f32)],
                  grid=(T // tm,), in_specs=[_row_spec(tm, D), _row_spec(tm, D), _vec_spec(D)],
                  out_specs=[_row_spec(tm, D), pl.BlockSpec((1, 1), lambda i: (0, 0)), _vec_spec(D)])(x, target, g)


def _gate_bwd(gin, o, gate):
    T, tm = gin.shape[0], 512

    def body(gin_ref, o_ref, gate_ref, dob_ref, dgate_ref):
        @pl.when(pl.program_id(0) == 0)
        def _():
            dgate_ref[...] = jnp.zeros_like(dgate_ref)

        gv = gin_ref[...]
        dob_ref[...] = (gv * gate_ref[...]).astype(bf16)
        dgate_ref[...] += jnp.sum(gv * o_ref[...], axis=0, keepdims=True)

    return _pcall(body, name="gate_bwd", out_shape=[_sds((T, D), bf16), _sds((1, D), f32)], grid=(T // tm,),
                  in_specs=[_row_spec(tm, D), _row_spec(tm, D), _vec_spec(D)],
                  out_specs=[_row_spec(tm, D), _vec_spec(D)])(gin, o, gate)


def _norm_bwd(x, dh, gin, g, scale):
    T, tm = x.shape[0], 256

    def body(x_ref, dh_ref, gin_ref, g_ref, sc_ref, dx_ref, st_ref):
        @pl.when(pl.program_id(0) == 0)
        def _():
            st_ref[...] = jnp.zeros_like(st_ref)

        xv, gv, dhv = x_ref[...], g_ref[...], dh_ref[...]
        r = lax.rsqrt(jnp.mean(xv * xv, axis=-1, keepdims=True) + EPS)
        xn = xv * r
        da = dhv * (1.0 + sc_ref[...])
        dxn = da * gv
        dx_ref[...] = gin_ref[...] + r * (dxn - xn * jnp.mean(dxn * xn, axis=-1, keepdims=True))
        st_ref[0:1, :] += jnp.sum(dhv, axis=0, keepdims=True)
        st_ref[1:2, :] += jnp.sum(dhv * (xn * gv), axis=0, keepdims=True)
        st_ref[2:3, :] += jnp.sum(da * xn, axis=0, keepdims=True)

    return _pcall(body, name="norm_bwd", out_shape=[_sds((T, D), f32), _sds((8, D), f32)], grid=(T // tm,),
                  in_specs=[_row_spec(tm, D), _row_spec(tm, D), _row_spec(tm, D), _vec_spec(D), _vec_spec(D)],
                  out_specs=[_row_spec(tm, D), _vec_spec(D, 8)])(x, dh, gin, g, scale)


def _cast_place(place, w, layer):
    _, rows, cols = w.shape
    tr = 256

    def body(place_ref, w_ref, o_ref):
        o_ref[...] = w_ref[...].astype(bf16)

    return _pcall(body, name="cast_place", out_shape=_sds((4, rows, cols), bf16), grid=(rows // tr,), prefetch=1,
                  in_specs=[pl.BlockSpec((None, tr, cols), lambda i, pr: (layer, i, 0))],
                  out_specs=pl.BlockSpec((None, tr, cols), lambda i, pr: (pr[0], i, 0)))(place, w)


def _rs_add(place, grad, theirs):
    _, rows, cols = theirs.shape
    tr = 256
    spec = pl.BlockSpec((None, tr, cols), lambda s, i, pr: (s, i, 0))

    def body(place_ref, a_ref, b_ref, o_ref):
        o_ref[...] = (a_ref[...].astype(f32) + b_ref[...].astype(f32)).astype(bf16)

    return _pcall(body, name="rs_add", out_shape=_sds(theirs.shape, bf16), grid=(4, rows // tr), prefetch=1,
                  in_specs=[pl.BlockSpec((None, None, tr, cols), lambda s, i, pr: (s, pr[1], i, 0)), spec],
                  out_specs=spec)(place, grad, theirs)


def _rs_sum(place, pairs, slots):
    _, rows, cols = slots.shape
    tr = 256

    def body(place_ref, p_ref, q_ref, o_ref):
        o_ref[...] = ((p_ref[...].astype(f32) + q_ref[0].astype(f32)) + q_ref[1].astype(f32)) + q_ref[2].astype(f32)

    return _pcall(body, name="rs_sum", out_shape=_sds((2, rows, cols), f32), grid=(rows // tr,), prefetch=1,
                  in_specs=[pl.BlockSpec((None, tr, cols), lambda i, pr: (pr[0], i, 0)),
                            pl.BlockSpec((3, tr, cols), lambda i, pr: (0, i, 0))],
                  out_specs=pl.BlockSpec((None, tr, cols), lambda i, pr: (pr[1], i, 0)))(place, pairs, slots)


def _adamw_math(w, g, m, v):
    m = ADAM_B1 * m + (1.0 - ADAM_B1) * g
    v = ADAM_B2 * v + (1.0 - ADAM_B2) * jnp.square(g)
    m_hat = m / (1.0 - ADAM_B1 ** ADAM_STEP)
    v_hat = v / (1.0 - ADAM_B2 ** ADAM_STEP)
    delta = -ADAM_LR * (m_hat / (jnp.sqrt(v_hat) + ADAM_EPS) + ADAM_WD * w)
    return delta, m, v


def _adamw_layer(layer, w, g, m, v, so_far):
    _, rows, cols = w.shape
    tr = 128
    spec = pl.BlockSpec((None, tr, cols), lambda i: (layer, i, 0))

    def body(w_ref, g_ref, m_ref, v_ref, *rest):
        go_ref, d_ref, mo_ref, vo_ref = rest[-4:]
        g = g_ref[...]
        go_ref[...] = g
        d_ref[...], mo_ref[...], vo_ref[...] = _adamw_math(w_ref[...], g, m_ref[...], v_ref[...])

    args, in_specs, aliases = [w, g, m, v], [spec, pl.BlockSpec((tr, cols), lambda i: (i, 0)), spec, spec], None
    if so_far is not None:
        args += list(so_far)
        in_specs += [ANY] * 4
        aliases = {4 + k: k for k in range(4)}
    return _pcall(body, name="adamw", out_shape=[_sds(w.shape, f32)] * 4, grid=(rows // tr,), in_specs=in_specs,
                  out_specs=[spec] * 4, aliases=aliases)(*args)


def _adamw_small(items):
    n = len(items)

    def body(*refs):
        ins, outs = refs[:4 * n], refs[4 * n:]
        for t in range(n):
            w_ref, g_ref, m_ref, v_ref = ins[4 * t:4 * t + 4]
            if len(g_ref.shape) == len(w_ref.shape) + 1:
                g = g_ref[0]
                for b in range(1, g_ref.shape[0]):
                    g = g + g_ref[b]
            else:
                g = g_ref[...]
            d, m, v = _adamw_math(w_ref[...], g, m_ref[...], v_ref[...])
            outs[4 * t][...], outs[4 * t + 1][...], outs[4 * t + 2][...], outs[4 * t + 3][...] = g, d, m, v

    out_shape = [_sds(w.shape, f32) for (w, _, _, _) in items for _ in range(4)]
    flat = [a for it in items for a in it]
    res = _pcall(body, name="adamw_small", out_shape=out_shape, in_specs=[VMEM] * (4 * n),
                 out_specs=[VMEM] * (4 * n))(*flat)
    return [tuple(res[4 * t:4 * t + 4]) for t in range(n)]


NN = ((1,), (0,))
NT = ((1,), (1,))
TN = ((0,), (0,))


def _mm(name, a, b, *, grid, a_spec, b_spec, out_shape, out_spec, dims, vmem_mb=48):
    def body(a_ref, b_ref, o_ref):
        r = lax.dot_general(a_ref[...], b_ref[...], (dims, ((), ())), preferred_element_type=f32)
        o_ref[...] = r.astype(o_ref.dtype)

    return _pcall(body, name=name, out_shape=out_shape, grid=grid, in_specs=[a_spec, b_spec], out_specs=out_spec,
                  vmem_mb=vmem_mb)(a, b)


def _whole(shape):
    return pl.BlockSpec(shape, lambda j: (0,) * len(shape))


def _split_spec(rows, tile, per_split):
    return pl.BlockSpec((None, rows, tile), lambda j: (j // per_split, 0, j % per_split))


class _Proj:
    def __init__(self, n, splits, tile):
        self.n, self.splits, self.tile = n, splits, tile
        self.steps = n // tile
        self.w_per = n // 4 // tile
        self.a_per = n // splits // tile
        assert self.w_per * tile * 4 == n and self.a_per * tile * splits == n

    def fwd(self, hb, wg):
        T = hb.shape[0]
        return _mm("proj_fwd", hb, wg, grid=(self.steps,), a_spec=_whole((T, D)),
                   b_spec=_split_spec(D, self.tile, self.w_per),
                   out_shape=_sds((self.splits, T, self.n // self.splits), f32),
                   out_spec=_split_spec(T, self.tile, self.a_per), dims=NN)

    def dw(self, hb, dp):
        T = hb.shape[0]
        return _mm("proj_dw", hb, dp, grid=(self.steps,), a_spec=_whole((T, D)),
                   b_spec=_split_spec(T, self.tile, self.a_per), out_shape=_sds((4, D, self.n // 4), bf16),
                   out_spec=_split_spec(D, self.tile, self.w_per), dims=TN)

    def dh(self, dp, wg):
        T = dp.shape[1]
        sub, tile, w_per = DH_WIDE // self.tile, self.tile, self.w_per
        a_per = self.n // self.splits // DH_WIDE
        assert sub * tile == DH_WIDE and a_per * DH_WIDE * self.splits == self.n

        def w_tile(q):
            return pl.BlockSpec((None, D, tile), lambda k: ((sub * k + q) // w_per, 0, (sub * k + q) % w_per))

        def body(a_ref, *rest):
            o_ref = rest[sub]
            w = jnp.concatenate([rest[q][...] for q in range(sub)], axis=1)
            r = lax.dot_general(a_ref[...], w, (NT, ((), ())), preferred_element_type=f32)

            @pl.when(pl.program_id(0) == 0)
            def _():
                o_ref[...] = r

            @pl.when(pl.program_id(0) > 0)
            def _():
                o_ref[...] += r

        return _pcall(body, name="proj_dh", out_shape=_sds((T, D), f32), grid=(self.n // DH_WIDE,),
                      in_specs=[pl.BlockSpec((None, T, DH_WIDE), lambda k: (k // a_per, 0, k % a_per))]
                      + [w_tile(q) for q in range(sub)],
                      out_specs=_whole((T, D)), vmem_mb=48)(dp, *([wg] * sub))


EVEN_PROJ = _Proj(7 * D, 7, 256)
ODD_PROJ = _Proj(4 * D, 2, 512)


def _dy_mm(dob, wo):
    T = dob.shape[0]
    return _mm("out_dy", dob, wo, grid=(4,), a_spec=_whole((T, D)),
               b_spec=pl.BlockSpec((None, 512, D), lambda j: (j, 0, 0)), out_shape=_sds((2, T, D), f32),
               out_spec=_split_spec(T, 512, 2), dims=NT)


def _dwo_mm(y2, dob):
    T = dob.shape[0]
    return _mm("out_dw", y2, dob, grid=(4,), a_spec=_split_spec(T, 512, 2), b_spec=_whole((T, D)),
               out_shape=_sds((4, 512, D), bf16), out_spec=pl.BlockSpec((None, 512, D), lambda j: (j, 0, 0)), dims=TN)


def _head_spec(lead, T):
    return pl.BlockSpec((lead, T, HEAD), lambda h: (0, 0, h))


def _head_vec(rows):
    return pl.BlockSpec((rows, HEAD), lambda h: (0, h))


_HEAD_MAT = pl.BlockSpec((None, HEAD, HEAD), lambda h: (h, 0, 0))


def _causal():
    return lax.broadcasted_iota(jnp.int32, (HEAD, HEAD), 0) >= lax.broadcasted_iota(jnp.int32, (HEAD, HEAD), 1)


def _layernorm_head(v):
    mu = jnp.mean(v, axis=-1, keepdims=True)
    d = v - mu
    rstd = lax.rsqrt(jnp.mean(d * d, axis=-1, keepdims=True) + EPS)
    return d * rstd, rstd


def _even_fwd(p7, conv_w, ln_g, ln_b, sgu_w, sgu_bias):
    T, C = p7.shape[1], CHUNK_ROWS

    def body(p_ref, cw_ref, lg_ref, lb_ref, w_ref, b_ref, y_ref):
        w0, w1, w2 = cw_ref[0:1, :], cw_ref[1:2, :], cw_ref[2:3, :]
        wm = jnp.where(_causal(), w_ref[...], 0.0).astype(bf16)
        bias, lg, lb = b_ref[...], lg_ref[...], lb_ref[...]

        def step(i, halo):
            rows = pl.ds(pl.multiple_of(i * C, C), C)
            tt = p_ref[2, rows, :] * p_ref[0, rows, :]
            ext = jnp.concatenate([halo, tt], axis=0)
            cv = w2 * tt + w1 * pltpu.roll(ext, 1, 0)[HALO_CONV:] + w0 * pltpu.roll(ext, 2, 0)[HALO_CONV:]
            y_ref[0, rows, :] = (p_ref[1, rows, :] * cv * _silu(p_ref[3, rows, :])).astype(bf16)
            vhat, _ = _layernorm_head(p_ref[5, rows, :])
            vn = (vhat * lg + lb).astype(bf16)
            mix = jnp.concatenate([jnp.dot(wm, vn[k * HEAD:(k + 1) * HEAD], preferred_element_type=f32) + bias
                                   for k in range(C // HEAD)], axis=0)
            y_ref[1, rows, :] = (p_ref[4, rows, :] * mix * _silu(p_ref[6, rows, :])).astype(bf16)
            return tt[C - HALO_CONV:]

        lax.fori_loop(0, T // C, step, jnp.zeros((HALO_CONV, HEAD), f32))

    return _pcall(body, name="even_fwd", out_shape=_sds((2, T, D), bf16), grid=(NH,),
                  in_specs=[_head_spec(7, T), _head_vec(3), _head_vec(1), _head_vec(1), _HEAD_MAT, _HEAD_MAT],
                  out_specs=_head_spec(2, T), vmem_mb=32)(p7, conv_w, ln_g, ln_b, sgu_w, sgu_bias)


def _even_bwd(p7, dy2, conv_w, ln_g, ln_b, sgu_w, sgu_bias):
    T, C = p7.shape[1], CHUNK_ROWS
    n_chunks = T // C

    def body(p_ref, dy_ref, cw_ref, lg_ref, lb_ref, w_ref, b_ref,
             dp_ref, dcw_ref, dlg_ref, dlb_ref, dw_ref, dms_ref, dcv_s):
        w0, w1, w2 = cw_ref[0:1, :], cw_ref[1:2, :], cw_ref[2:3, :]
        tri = _causal()
        wm = jnp.where(tri, w_ref[...], 0.0).astype(bf16)
        bias, lg, lb = b_ref[...], lg_ref[...], lb_ref[...]
        dw_ref[...] = jnp.zeros_like(dw_ref)
        dms_ref[...] = jnp.zeros_like(dms_ref)

        def fwd_step(i, carry):
            halo, a0, a1, a2, alg, alb = carry
            rows = pl.ds(pl.multiple_of(i * C, C), C)
            ah, ab, ac, az = p_ref[0, rows, :], p_ref[1, rows, :], p_ref[2, rows, :], p_ref[3, rows, :]
            dya = dy_ref[0, rows, :]
            tt = ac * ah
            ext = jnp.concatenate([halo, tt], axis=0)
            t1, t2 = pltpu.roll(ext, 1, 0)[HALO_CONV:], pltpu.roll(ext, 2, 0)[HALO_CONV:]
            cv = w2 * tt + w1 * t1 + w0 * t2
            sa, dsa = _silu_and_grad(az)
            g1 = dya * sa
            dp_ref[1, rows, :] = (g1 * cv).astype(bf16)
            dp_ref[3, rows, :] = (dya * ab * cv * dsa).astype(bf16)
            dcv = g1 * ab
            dcv_s[rows, :] = dcv
            a2 = a2 + jnp.sum(dcv * tt, axis=0, keepdims=True)
            a1 = a1 + jnp.sum(dcv * t1, axis=0, keepdims=True)
            a0 = a0 + jnp.sum(dcv * t2, axis=0, keepdims=True)

            u, zb, dyb = p_ref[4, rows, :], p_ref[6, rows, :], dy_ref[1, rows, :]
            vhat, rstd = _layernorm_head(p_ref[5, rows, :])
            vn = (vhat * lg + lb).astype(bf16)
            sb, dsb = _silu_and_grad(zb)
            mix = jnp.concatenate([jnp.dot(wm, vn[k * HEAD:(k + 1) * HEAD], preferred_element_type=f32) + bias
                                   for k in range(C // HEAD)], axis=0)
            dp_ref[4, rows, :] = (dyb * mix * sb).astype(bf16)
            dp_ref[6, rows, :] = (dyb * u * mix * dsb).astype(bf16)
            dmix = dyb * u * sb
            dvn_parts = []
            for k in range(C // HEAD):
                dm = dmix[k * HEAD:(k + 1) * HEAD]
                dmb = dm.astype(bf16)
                dvn_parts.append(lax.dot_general(wm, dmb, (TN, ((), ())), preferred_element_type=f32))
                dw_ref[...] += lax.dot_general(dmb, vn[k * HEAD:(k + 1) * HEAD], (NT, ((), ())),
                                               preferred_element_type=f32)
                dms_ref[...] += dm
            dvn = jnp.concatenate(dvn_parts, axis=0)
            alg = alg + jnp.sum(dvn * vhat, axis=0, keepdims=True)
            alb = alb + jnp.sum(dvn, axis=0, keepdims=True)
            dvh = dvn * lg
            dv = rstd * (dvh - jnp.mean(dvh, axis=-1, keepdims=True)
                         - vhat * jnp.mean(dvh * vhat, axis=-1, keepdims=True))
            dp_ref[5, rows, :] = dv.astype(bf16)
            return tt[C - HALO_CONV:], a0, a1, a2, alg, alb

        zrow = jnp.zeros((1, HEAD), f32)
        _, a0, a1, a2, alg, alb = lax.fori_loop(
            0, n_chunks, fwd_step, (jnp.zeros((HALO_CONV, HEAD), f32), zrow, zrow, zrow, zrow, zrow))
        dcw_ref[0:1, :], dcw_ref[1:2, :], dcw_ref[2:3, :] = a0, a1, a2
        dlg_ref[...], dlb_ref[...] = alg, alb
        dw_ref[...] = jnp.where(tri, dw_ref[...], 0.0)

        def bwd_step(k, halo):
            rows = pl.ds(pl.multiple_of((n_chunks - 1 - k) * C, C), C)
            dcv = dcv_s[rows, :]
            ext = jnp.concatenate([dcv, halo], axis=0)
            n1 = pltpu.roll(ext, C + HALO_CONV - 1, 0)[:C]
            n2 = pltpu.roll(ext, C + HALO_CONV - 2, 0)[:C]
            dtt = w2 * dcv + w1 * n1 + w0 * n2
            dp_ref[2, rows, :] = (dtt * p_ref[0, rows, :]).astype(bf16)
            dp_ref[0, rows, :] = (dtt * p_ref[2, rows, :]).astype(bf16)
            return dcv[:HALO_CONV]

        lax.fori_loop(0, n_chunks, bwd_step, jnp.zeros((HALO_CONV, HEAD), f32))

    out_shape = [_sds((7, T, D), bf16), _sds((3, D), f32), _sds((1, D), f32), _sds((1, D), f32),
                 _sds((NH, HEAD, HEAD), f32), _sds((NH, HEAD, HEAD), f32)]
    return _pcall(body, name="even_bwd", out_shape=out_shape, grid=(NH,),
                  in_specs=[_head_spec(7, T), _head_spec(2, T), _head_vec(3), _head_vec(1), _head_vec(1),
                            _HEAD_MAT, _HEAD_MAT],
                  out_specs=[_head_spec(7, T), _head_vec(3), _head_vec(1), _head_vec(1), _HEAD_MAT, _HEAD_MAT],
                  scratch=[pltpu.VMEM((T, HEAD), f32)], vmem_mb=48)(p7, dy2, conv_w, ln_g, ln_b, sgu_w, sgu_bias)


def _window_sum(ext, win, towards_past):
    n, k, s = ext.shape[0], 1, ext
    while k < win:
        s = s + pltpu.roll(s, k if towards_past else n - k, 0)
        k *= 2
    return s


def _pool_count(i, C, win):
    t = i * C + lax.broadcasted_iota(jnp.int32, (C, 1), 0)
    cnt = jnp.minimum(t + 1, win).astype(f32)
    return cnt, 1.0 / cnt


def _group_specs(T):
    p_spec = pl.BlockSpec((None, T, GC), lambda g: (0, 0, g))
    z_spec = pl.BlockSpec((None, T, GC), lambda g: (1, 0, g))
    pw_spec = pl.BlockSpec((4, GC // 4, GC), lambda g: (0, g, 0))
    ps_spec = pl.BlockSpec((1, GC), lambda g: (0, g))
    y_spec = pl.BlockSpec((None, T, GC), lambda g: (g // 2, 0, g % 2))
    return p_spec, z_spec, pw_spec, ps_spec, y_spec


def _odd_fwd(p2, pool_wg, pool_scale):
    T, C = p2.shape[1], CHUNK_ROWS
    p_spec, z_spec, pw_spec, ps_spec, y_spec = _group_specs(T)

    def body(p_ref, z_ref, pw_ref, ps_ref, y_ref):
        pw, ps = pw_ref[...].reshape(GC, GC), ps_ref[...]

        def run(win):
            def step(i, halo):
                rows = pl.ds(pl.multiple_of(i * C, C), C)
                p = p_ref[rows, :]
                s = _window_sum(jnp.concatenate([halo, p], axis=0), win, True)[HALO_POOL:]
                pooled = s * _pool_count(i, C, win)[1] - p
                ypre = jnp.dot(pooled.astype(bf16), pw, preferred_element_type=f32)
                y_ref[rows, :] = (ypre * ps * _silu(z_ref[rows, :])).astype(bf16)
                return p[C - HALO_POOL:]

            lax.fori_loop(0, T // C, step, jnp.zeros((HALO_POOL, GC), f32))

        for gi, win in enumerate(WINDOWS):
            pl.when(pl.program_id(0) == gi)(functools.partial(run, win))

    return _pcall(body, name="odd_fwd", out_shape=_sds((2, T, D), bf16), grid=(len(WINDOWS),),
                  in_specs=[p_spec, z_spec, pw_spec, ps_spec], out_specs=y_spec, vmem_mb=40)(p2, p2, pool_wg, pool_scale)


def _odd_bwd(p2, dy2, pool_wg, pool_scale):
    T, C = p2.shape[1], CHUNK_ROWS
    n_chunks = T // C
    p_spec, z_spec, pw_spec, ps_spec, y_spec = _group_specs(T)

    def body(p_ref, z_ref, dy_ref, pw_ref, ps_ref, dp_ref, dpw_ref, dps_ref, q_s, acc_s):
        pw, ps = pw_ref[...].reshape(GC, GC), ps_ref[...]

        def run(win):
            acc_s[...] = jnp.zeros_like(acc_s)

            def fwd_step(i, carry):
                halo, aps = carry
                rows = pl.ds(pl.multiple_of(i * C, C), C)
                p, z, dy = p_ref[rows, :], z_ref[rows, :], dy_ref[rows, :]
                _, inv_cnt = _pool_count(i, C, win)
                s = _window_sum(jnp.concatenate([halo, p], axis=0), win, True)[HALO_POOL:]
                pb = (s * inv_cnt - p).astype(bf16)
                ypre = jnp.dot(pb, pw, preferred_element_type=f32)
                sz, dsz = _silu_and_grad(z)
                aps = aps + jnp.sum(dy * ypre * sz, axis=0, keepdims=True)
                dp_ref[1, rows, :] = (dy * ypre * ps * dsz).astype(bf16)
                dyp = (dy * ps * sz).astype(bf16)
                acc_s[...] += lax.dot_general(pb, dyp, (TN, ((), ())), preferred_element_type=f32)
                dpool = lax.dot_general(dyp, pw, (NT, ((), ())), preferred_element_type=f32)
                q_s[rows, :] = dpool * inv_cnt
                return p[C - HALO_POOL:], aps

            _, aps = lax.fori_loop(0, n_chunks, fwd_step, (jnp.zeros((HALO_POOL, GC), f32), jnp.zeros((1, GC), f32)))
            dps_ref[...] = aps
            dpw_ref[...] = acc_s[...].reshape(4, GC // 4, GC).astype(bf16)

            def bwd_step(k, halo):
                i = n_chunks - 1 - k
                rows = pl.ds(pl.multiple_of(i * C, C), C)
                q = q_s[rows, :]
                s = _window_sum(jnp.concatenate([q, halo], axis=0), win, False)[:C]
                dp_ref[0, rows, :] = (s - q * _pool_count(i, C, win)[0]).astype(bf16)
                return q[:HALO_POOL]

            lax.fori_loop(0, n_chunks, bwd_step, jnp.zeros((HALO_POOL, GC), f32))

        for gi, win in enumerate(WINDOWS):
            pl.when(pl.program_id(0) == gi)(functools.partial(run, win))

    out_shape = [_sds((2, T, 2 * D), bf16), _sds((4, GC, GC), bf16), _sds((1, 2 * D), f32)]
    return _pcall(body, name="odd_bwd", out_shape=out_shape, grid=(len(WINDOWS),),
                  in_specs=[p_spec, z_spec, y_spec, pw_spec, ps_spec],
                  out_specs=[pl.BlockSpec((2, T, GC), lambda g: (0, 0, g)), pw_spec, ps_spec],
                  scratch=[pltpu.VMEM((T, GC), f32), pltpu.VMEM((GC, GC), f32)], vmem_mb=52)(
                      p2, p2, dy2, pool_wg, pool_scale)


def _ada_fwd(c_all, ada_w):
    cols = ada_w.shape[2]

    def body(c_ref, w_ref, o_ref):
        o_ref[...] = jnp.dot(_silu(c_ref[...]), w_ref[...], preferred_element_type=f32,
                             precision=lax.Precision.HIGHEST)

    return _pcall(body, name="ada_fwd", out_shape=_sds((4, N_DEV, cols), f32), grid=(4,),
                  in_specs=[pl.BlockSpec((N_DEV, D), lambda i: (0, 0)), pl.BlockSpec((None, D, cols), lambda i: (i, 0, 0))],
                  out_specs=pl.BlockSpec((None, N_DEV, cols), lambda i: (i, 0, 0)))(c_all, ada_w)


def _ada_bwd(c_all_t, dmod, w, m, v):
    cols, tr = w.shape[2], 256
    spec = pl.BlockSpec((None, tr, cols), lambda l, i: (l, i, 0))

    def body(c_ref, dm_ref, w_ref, m_ref, v_ref, g_ref, d_ref, mo_ref, vo_ref):
        sc = _silu(c_ref[...])
        g = sc[:, 0:1] * dm_ref[0:1, :]
        for b in range(1, N_DEV):
            g = g + sc[:, b:b + 1] * dm_ref[b:b + 1, :]
        g_ref[...] = g
        d_ref[...], mo_ref[...], vo_ref[...] = _adamw_math(w_ref[...], g, m_ref[...], v_ref[...])

    return _pcall(body, name="ada_bwd", out_shape=[_sds(w.shape, f32)] * 4, grid=(4, D // tr),
                  in_specs=[pl.BlockSpec((tr, N_DEV), lambda l, i: (i, 0)),
                            pl.BlockSpec((None, N_DEV, cols), lambda l, i: (l, 0, 0)), spec, spec, spec],
                  out_specs=[spec] * 4)(c_all_t, dmod, w, m, v)


def _layer_fwd(even, x, mod, g, w):
    shift, scale, gate = mod
    hb = _hnorm(x, g, shift, scale)
    if even:
        w_in, w_out, conv_w, ln_g, ln_b, sgu_w, sgu_b = w
        bias = jnp.broadcast_to(sgu_b[:, :, None], (NH, HEAD, HEAD))
        p = EVEN_PROJ.fwd(hb, w_in)
        y2 = _even_fwd(p, conv_w, ln_g, ln_b, sgu_w, bias)
    else:
        w_in, pool_w, w_out, pool_scale = w
        p = ODD_PROJ.fwd(hb, w_in)
        y2 = _odd_fwd(p, pool_w, pool_scale)
    x_next, o = _out_proj(y2, w_out.reshape(2, D, D), x, gate)
    return x_next, (x, hb, p, y2, o)


def _layer_bwd(even, gin, saved, mod, g, w, send=None):
    shift, scale, gate = mod
    x_in, hb, p, y2, o = saved
    dob, dgate = _gate_bwd(gin, o, gate)
    if even:
        w_in, w_out, conv_w, ln_g, ln_b, sgu_w, sgu_b = w
        bias = jnp.broadcast_to(sgu_b[:, :, None], (NH, HEAD, HEAD))
        dy2 = _dy_mm(dob, w_out)
        dp, dconv, dlg, dlb, dsw, dms = _even_bwd(p, dy2, conv_w, ln_g, ln_b, sgu_w, bias)
        proj = EVEN_PROJ
        small = dict(conv_w=dconv, ln_g=dlg, ln_b=dlb, sgu_w=dsw, sgu_b=jnp.sum(dms, axis=-1))
        big = [proj.dw(hb, dp), _dwo_mm(y2, dob)]
    else:
        w_in, pool_w, w_out, pool_scale = w
        dy2 = _dy_mm(dob, w_out)
        dp, dpw, dps = _odd_bwd(p, dy2, pool_w, pool_scale)
        proj = ODD_PROJ
        small = dict(pool_scale=dps)
        big = [proj.dw(hb, dp), dpw, _dwo_mm(y2, dob)]
    if send is not None:
        big, tok = send(big)
        scale = scale + tok[0:1, 0:1]
    dh = proj.dh(dp, w_in)
    gx, stats = _norm_bwd(x_in, dh, gin, g, scale)
    return gx, big, small, jnp.concatenate([stats[0:2], dgate], axis=0), stats[2:3]


def _pack_rows(parts):
    rows = [p.reshape(-1, LANES) for p in parts]
    total = sum(r.shape[0] for r in rows)
    padded = -(-total // (8 * N_DEV)) * (8 * N_DEV)
    if padded > total:
        rows.append(jnp.zeros((padded - total, LANES), f32))
    return jnp.concatenate(rows, axis=0)


def _unpack_rows(buf, shapes):
    out, r = [], 0
    for shp in shapes:
        n = 1
        for d in shp:
            n *= d
        out.append(buf[r:r + n // LANES].reshape(shp))
        r += n // LANES
    return out


def kernel(x, c, norm_g, ada_w, ada_b, ab_w_in, ab_conv_w, ab_ln_g, ab_ln_b, ab_sgu_w, ab_sgu_b, ab_w_out, c_w_in, c_pool_w, c_pool_scale, c_w_out, final_g, loss_target, m_norm_g, m_ada_w, m_ada_b, m_ab_w_in, m_ab_conv_w, m_ab_ln_g, m_ab_ln_b, m_ab_sgu_w, m_ab_sgu_b, m_ab_w_out, m_c_w_in, m_c_pool_w, m_c_pool_scale, m_c_w_out, m_final_g, v_norm_g, v_ada_w, v_ada_b, v_ab_w_in, v_ab_conv_w, v_ab_ln_g, v_ab_ln_b, v_ab_sgu_w, v_ab_sgu_b, v_ab_w_out, v_c_w_in, v_c_pool_w, v_c_pool_scale, v_c_w_out, v_final_g):
    ix, iy, ic = _place()
    chip, dev = 2 * ix + iy, 4 * ix + 2 * iy + ic
    n_even, n_odd = ab_w_in.shape[0], c_w_in.shape[0]
    depth = n_even + n_odd
    acols = ada_w.shape[2]

    place = jnp.stack([chip, ic]).astype(jnp.int32)
    even_names, odd_names = ["ab_w_in", "ab_w_out"], ["c_w_in", "c_pool_w", "c_w_out"]
    params = {"ab_w_in": (ab_w_in, m_ab_w_in, v_ab_w_in), "ab_w_out": (ab_w_out, m_ab_w_out, v_ab_w_out),
              "c_w_in": (c_w_in, m_c_w_in, v_c_w_in), "c_w_out": (c_w_out, m_c_w_out, v_c_w_out),
              "c_pool_w": tuple(a.reshape(n_odd, GC, GC) for a in (c_pool_w, m_c_pool_w, v_c_pool_w))}

    c_all = _gather8(c, "gather_c").reshape(N_DEV, D)
    modp = _ada_fwd(c_all, ada_w)
    modg = _gather8(modp, "gather_mod")
    mod_rows = lax.dynamic_index_in_dim(modg[0::2], dev, axis=2, keepdims=False)
    mod = jnp.transpose(mod_rows, (1, 0, 2)).reshape(depth, 3 * D) + ada_b
    mods = [(mod[i:i + 1, 0:D], mod[i:i + 1, D:2 * D], mod[i:i + 1, 2 * D:3 * D]) for i in range(depth)]

    def shard_cols(a, width):
        return lax.dynamic_slice_in_dim(a, chip * width, width, axis=a.ndim - 1)

    small_sharded = jnp.concatenate([ab_conv_w.reshape(1, -1), c_pool_scale.reshape(1, -1)], axis=1)
    small_all = _gather8(small_sharded, "gather_small")[0::2, 0]
    n_conv = ab_conv_w.size
    conv_all = small_all[:, :n_conv].reshape(4, n_even, 3, D // 4)
    conv_full = jnp.transpose(conv_all, (1, 2, 0, 3)).reshape(n_even, 3, D)
    scale_all = small_all[:, n_conv:].reshape(4, n_odd, 2 * D // 4)
    scale_full = jnp.transpose(scale_all, (1, 0, 2)).reshape(n_odd, 2 * D)

    def placed(i):
        ws = [params[nm][0] for nm in (even_names if i % 2 == 0 else odd_names)]
        return [_cast_place(place, w, i // 2).reshape(4, 2, w.shape[1] // 2, w.shape[2]) for w in ws]

    gathers_done = mod[0:1, 0:LANES] + scale_full[0:1, 0:LANES]
    sems_0, inflight_0, tok = _ag_start([placed(0)], gathers_done, "ag_start_0")
    sems_r, inflight_r, tok = _ag_start([placed(i) for i in range(1, depth)], tok, "ag_start_rest")
    ag_sems, inflight = sems_0 + sems_r, inflight_0 + inflight_r

    x_cur, after, saved, weights = x[0], tok, [], []
    for i in range(depth):
        j = i // 2
        arrived = _ag_wait(inflight[i], ag_sems[i], after, f"ag_wait_{i}")
        full = [g.reshape(4, 2 * g.shape[2], g.shape[3]) for g in _ag_forward(arrived, "ag_forward")]
        if i % 2 == 0:
            w = (full[0], full[1], conv_full[j], ab_ln_g[j:j + 1], ab_ln_b[j:j + 1], ab_sgu_w[j], ab_sgu_b[j])
        else:
            w = (full[0], full[1], full[2], scale_full[j:j + 1])
        weights.append(w)
        x_cur, sv = _layer_fwd(i % 2 == 0, x_cur, mods[i], norm_g[i:i + 1], w)
        saved.append(sv)
        after = x_cur
    gin, loss, dfinal_g = _loss_bwd(x_cur, loss_target[0], final_g.reshape(1, D))

    stacked = {}

    def finish(i, sems, pairs, lands, after):
        pairs, slots = _rs_chip_wait(sems, pairs, lands, after, f"rs_chip_wait_{i}")
        halves = [_rs_sum(place, p, q) for p, q in zip(pairs, slots)]
        for nm, g in zip(even_names if i % 2 == 0 else odd_names, _rs_half_exchange(halves, "rs_half_exchange")):
            w, m, v = params[nm]
            stacked[nm] = _adamw_layer(i // 2, w, g.reshape(w.shape[1], w.shape[2]), m, v, stacked.get(nm))

    small_g, dmod, dnorm_g, pending, tok = [None] * depth, [None] * depth, [None] * depth, None, None
    for i in reversed(range(depth)):
        shift, scale, gate = mods[i]
        if tok is not None:
            gate = gate + tok[0:1, 0:1]
        def send(big_g, i=i):
            big_g = [g.reshape(4, 2, g.shape[1] // 2, g.shape[2]) for g in big_g]
            sems, big_g, lands, tok = _rs_pair_start(big_g, f"rs_pair_start_{i}")
            return (sems, big_g, lands), tok

        gin, sent, small_g[i], dmod[i], dnorm_g[i] = _layer_bwd(i % 2 == 0, gin, saved[i], (shift, scale, gate),
                                                                 norm_g[i:i + 1], weights[i], send)
        big_g, theirs = _rs_pair_wait(*sent, gin, f"rs_pair_wait_{i}")
        pairs = [_rs_add(place, a, b) for a, b in zip(big_g, theirs)]
        sems, pairs, lands, tok = _rs_chip_start(pairs, f"rs_chip_start_{i}")
        if pending is not None:
            finish(*pending, tok)
        pending = (i, sems, pairs, lands)
    grad_x = gin
    dmod, dnorm_g = jnp.stack(dmod), jnp.concatenate(dnorm_g, axis=0)

    small_parts = [dnorm_g + tok[0:1, 0:1], dfinal_g,
                   jnp.stack([small_g[2 * j]["conv_w"] for j in range(n_even)]),
                   jnp.concatenate([small_g[2 * j]["ln_g"] for j in range(n_even)], axis=0),
                   jnp.concatenate([small_g[2 * j]["ln_b"] for j in range(n_even)], axis=0),
                   jnp.stack([small_g[2 * j]["sgu_w"] for j in range(n_even)]),
                   jnp.stack([small_g[2 * j]["sgu_b"] for j in range(n_even)]),
                   jnp.concatenate([small_g[2 * j + 1]["pool_scale"] for j in range(n_odd)], axis=0),
                   jnp.pad(loss, ((0, 7), (0, LANES - 1)))]
    small_shapes = [p.shape for p in small_parts]
    reduced = _allreduce8(_pack_rows(small_parts), "allreduce_small")
    (g_norm_g, g_final_g, g_conv_full, g_ln_g, g_ln_b, g_sgu_w, g_sgu_b, g_scale_full,
     loss_row) = _unpack_rows(reduced, small_shapes)
    loss = loss_row[0, 0]
    g_conv = shard_cols(g_conv_full, D // 4)
    g_scale = shard_cols(g_scale_full, 2 * D // 4)
    dmod_all = _gather8(dmod.reshape(depth * 3 * D // LANES, LANES), "gather_dmod").reshape(N_DEV, depth, 3 * D)

    def two_d(a):
        return a.reshape(-1, a.shape[-1])

    small = [(norm_g, g_norm_g, m_norm_g, v_norm_g),
             (ada_b, dmod_all, m_ada_b, v_ada_b),
             (two_d(ab_conv_w), two_d(g_conv), two_d(m_ab_conv_w), two_d(v_ab_conv_w)),
             (ab_ln_g, g_ln_g, m_ab_ln_g, v_ab_ln_g),
             (ab_ln_b, g_ln_b, m_ab_ln_b, v_ab_ln_b),
             (two_d(ab_sgu_w), two_d(g_sgu_w), two_d(m_ab_sgu_w), two_d(v_ab_sgu_w)),
             (two_d(ab_sgu_b), two_d(g_sgu_b), two_d(m_ab_sgu_b), two_d(v_ab_sgu_b)),
             (c_pool_scale, g_scale, m_c_pool_scale, v_c_pool_scale),
             (final_g.reshape(1, D), g_final_g, m_final_g.reshape(1, D), v_final_g.reshape(1, D))]
    small_res = _adamw_small(small)
    small_shapes_out = [norm_g.shape, ada_b.shape, ab_conv_w.shape, ab_ln_g.shape, ab_ln_b.shape, ab_sgu_w.shape,
                        ab_sgu_b.shape, c_pool_scale.shape, final_g.shape]
    (r_norm_g, r_ada_b, r_conv, r_ln_g, r_ln_b, r_sgu_w, r_sgu_b, r_scale, r_final_g) = [
        tuple(a.reshape(shp) for a in res) for res, shp in zip(small_res, small_shapes_out)]

    dmod_cols = jnp.transpose(shard_cols(dmod_all, acols), (1, 0, 2))
    r_ada_w = _ada_bwd(c_all.T, dmod_cols, ada_w, m_ada_w, v_ada_w)

    finish(*pending, r_ada_w[1])
    r_ab_w_in, r_ab_w_out, r_c_w_in, r_c_w_out = (stacked[nm] for nm in ("ab_w_in", "ab_w_out", "c_w_in", "c_w_out"))
    r_c_pool_w = tuple(a.reshape(c_pool_w.shape) for a in stacked["c_pool_w"])

    order = [r_norm_g, r_ada_w, r_ada_b, r_ab_w_in, r_conv, r_ln_g, r_ln_b, r_sgu_w, r_sgu_b, r_ab_w_out,
             r_c_w_in, r_c_pool_w, r_scale, r_c_w_out, r_final_g]
    outs = [loss, grad_x[None]]
    for field in range(4):
        outs += [r[field] for r in order]
    return tuple(outs)
```

```python
import functools

import jax
import jax.numpy as jnp
from jax import lax
from jax.experimental import pallas as pl
from jax.experimental.pallas import tpu as pltpu

f32, bf16 = jnp.float32, jnp.bfloat16

D = 1024
HEAD = 128
NH = 8
WINDOWS = (2, 4, 8, 16)
GC = 512
EPS = 1e-6
HALO_CONV = 8
HALO_POOL = 16
CHUNK_ROWS = 512
DH_WIDE = 1024
N_DEV = 8
LANES = 128

ADAM_LR, ADAM_B1, ADAM_B2, ADAM_EPS, ADAM_WD, ADAM_STEP = 0.001, 0.9, 0.999, 1e-08, 0.01, 10

MESH = pl.DeviceIdType.MESH
ANY = pl.BlockSpec(memory_space=pl.ANY)
VMEM = pl.BlockSpec(memory_space=pltpu.VMEM)
MIB = 2 ** 20


def _pcall(body, *, name, out_shape, grid=None, in_specs=None, out_specs=None, scratch=(), vmem_mb=None,
           aliases=None, prefetch=0):
    kw = {}
    if prefetch:
        kw["grid_spec"] = pltpu.PrefetchScalarGridSpec(num_scalar_prefetch=prefetch, grid=grid, in_specs=in_specs,
                                                       out_specs=out_specs, scratch_shapes=list(scratch))
    else:
        if grid is not None:
            kw["grid"] = grid
        if in_specs is not None:
            kw["in_specs"] = in_specs
        if out_specs is not None:
            kw["out_specs"] = out_specs
        if scratch:
            kw["scratch_shapes"] = list(scratch)
    if aliases:
        kw["input_output_aliases"] = aliases
    params = pltpu.CompilerParams(vmem_limit_bytes=None if vmem_mb is None else vmem_mb * MIB)
    return pl.pallas_call(body, name=name, out_shape=out_shape, compiler_params=params, **kw)


def _sds(shape, dtype):
    return jax.ShapeDtypeStruct(tuple(shape), dtype)


def _sigmoid(z):
    return pl.reciprocal(1.0 + jnp.exp(-z), approx=True)


def _silu(z):
    return z * _sigmoid(z)


def _silu_and_grad(z):
    s = _sigmoid(z)
    return z * s, s * (1.0 + z * (1.0 - s))


def _place():
    return lax.axis_index("x"), lax.axis_index("y"), lax.axis_index("c")


def _gather8(blk, name):
    def body(x_ref, o_ref, ssem, rsem):
        x, y, c = _place()
        me = 4 * x + 2 * y + c
        o_ref[me] = x_ref[...]
        sends = []
        for k in range(1, N_DEV):
            px = 1 - x if k & 4 else x
            py = 1 - y if k & 2 else y
            pc = 1 - c if k & 1 else c
            cp = pltpu.make_async_remote_copy(src_ref=x_ref, dst_ref=o_ref.at[me], send_sem=ssem.at[k - 1],
                                              recv_sem=rsem.at[k - 1], device_id=(px, py, pc), device_id_type=MESH)
            cp.start()
            sends.append((cp, 4 * px + 2 * py + pc))
        for k, (cp, peer) in enumerate(sends):
            pltpu.make_async_remote_copy(src_ref=x_ref, dst_ref=o_ref.at[peer], send_sem=ssem.at[k],
                                         recv_sem=rsem.at[k], device_id=(x, y, c), device_id_type=MESH).wait_recv()
        for cp, _ in sends:
            cp.wait_send()

    return _pcall(body, name=name, out_shape=_sds((N_DEV,) + blk.shape, blk.dtype), in_specs=[VMEM], out_specs=VMEM,
                  scratch=[pltpu.SemaphoreType.DMA((N_DEV - 1,)), pltpu.SemaphoreType.DMA((N_DEV - 1,))])(blk)


def _allreduce8(buf, name):
    rows = buf.shape[0]
    rb = rows // N_DEV
    assert rb * N_DEV == rows and rb % 8 == 0

    def body(x_ref, o_ref, stage, ssem, rsem):
        x, y, c = _place()
        me = 4 * x + 2 * y + c
        peers = []
        for k in range(1, N_DEV):
            px = 1 - x if k & 4 else x
            py = 1 - y if k & 2 else y
            pc = 1 - c if k & 1 else c
            peers.append(((px, py, pc), 4 * px + 2 * py + pc))

        def blk(ref, idx):
            return ref.at[pl.ds(pl.multiple_of(idx * rb, 8), rb), :]

        def copy(phase, k, src, dst, dev):
            return pltpu.make_async_remote_copy(src_ref=src, dst_ref=dst, send_sem=ssem.at[phase, k],
                                                recv_sem=rsem.at[phase, k], device_id=dev, device_id_type=MESH)

        stage[me] = blk(x_ref, me)[...]
        scatter = [copy(0, k, blk(x_ref, pidx), stage.at[me], dev) for k, (dev, pidx) in enumerate(peers)]
        for cp in scatter:
            cp.start()
        for k, (dev, pidx) in enumerate(peers):
            copy(0, k, blk(x_ref, pidx), stage.at[pidx], dev).wait_recv()
        total = stage[0]
        for j in range(1, N_DEV):
            total = total + stage[j]
        blk(o_ref, me)[...] = total
        gather = [copy(1, k, blk(o_ref, me), blk(o_ref, me), dev) for k, (dev, pidx) in enumerate(peers)]
        for cp in gather:
            cp.start()
        for k, (dev, pidx) in enumerate(peers):
            copy(1, k, blk(o_ref, pidx), blk(o_ref, pidx), dev).wait_recv()
        for cp in scatter + gather:
            cp.wait_send()

    return _pcall(body, name=name, out_shape=_sds(buf.shape, f32), in_specs=[VMEM], out_specs=VMEM,
                  scratch=[pltpu.VMEM((N_DEV, rb, LANES), f32), pltpu.SemaphoreType.DMA((2, N_DEV - 1)),
                           pltpu.SemaphoreType.DMA((2, N_DEV - 1))])(buf)


def _other_chips(x, y):
    return [((1 - x, y), 2 * (1 - x) + y), ((x, 1 - y), 2 * x + (1 - y)), ((1 - x, 1 - y), 2 * (1 - x) + (1 - y))]


HBM = pl.BlockSpec(memory_space=pltpu.HBM)
SEM = pl.BlockSpec(memory_space=pltpu.SEMAPHORE)
EFFECT = pltpu.SideEffectType.DATAFLOW_SIDE_EFFECTING


def _in_hbm(a):
    return pltpu.with_memory_space_constraint(a, pltpu.HBM)


def _ag_start(layers, after, name):
    flat = [t for lay in layers for t in lay]
    n, nl = len(flat), len(layers)

    def body(*refs):
        src = refs[:n]
        sems = refs[n + 1:n + 1 + 2 * nl]
        token = refs[-1]
        x, y, c = _place()
        s_me = 2 * x + y
        t = 0
        for i, lay in enumerate(layers):
            for k in range(len(lay)):
                for j, ((px, py), _) in enumerate(_other_chips(x, y)):
                    pltpu.make_async_remote_copy(src_ref=src[t].at[s_me, c], dst_ref=src[t].at[s_me, c],
                                                 send_sem=sems[2 * i].at[3 * k + j], recv_sem=sems[2 * i + 1].at[3 * k + j],
                                                 device_id=(px, py, c), device_id_type=MESH).start()
                t += 1
        token[...] = jnp.zeros_like(token)

    sem_shapes = [pltpu.SemaphoreType.DMA((3 * len(lay),)) for lay in layers for _ in range(2)]
    out_shape = sem_shapes + [pltpu.HBM(t.shape, t.dtype) for t in flat] + [_sds((8, LANES), f32)]
    outs = pl.pallas_call(
        body, name=name, out_shape=out_shape, in_specs=[HBM] * n + [ANY],
        out_specs=[SEM] * (2 * nl) + [HBM] * n + [VMEM], input_output_aliases={t: 2 * nl + t for t in range(n)},
        compiler_params=pltpu.CompilerParams(has_side_effects=EFFECT))(*[_in_hbm(t) for t in flat], after)
    sems = [(outs[2 * i], outs[2 * i + 1]) for i in range(nl)]
    thru, t = [], 2 * nl
    for lay in layers:
        thru.append(list(outs[t:t + len(lay)]))
        t += len(lay)
    return sems, thru, outs[-1]


def _ag_wait(inflight, sems, after, name):
    n = len(inflight)

    def body(*refs):
        src, ssem, rsem = refs[:n], refs[n], refs[n + 1]
        x, y, c = _place()
        s_me = 2 * x + y
        for k in range(n):
            for j, (_, s_p) in enumerate(_other_chips(x, y)):
                cp = pltpu.make_async_remote_copy(src_ref=src[k].at[s_me, c], dst_ref=src[k].at[s_p, c],
                                                  send_sem=ssem.at[3 * k + j], recv_sem=rsem.at[3 * k + j],
                                                  device_id=(x, y, c), device_id_type=MESH)
                cp.wait_send()
                cp.wait_recv()

    return pl.pallas_call(
        body, name=name, out_shape=[pltpu.HBM(t.shape, t.dtype) for t in inflight],
        in_specs=[HBM] * n + [SEM, SEM, ANY], out_specs=[HBM] * n, input_output_aliases={t: t for t in range(n)},
        compiler_params=pltpu.CompilerParams(has_side_effects=EFFECT))(*inflight, sems[0], sems[1], after)


def _ag_forward(arrived, name):
    n = len(arrived)

    def body(*refs):
        o = refs[n:2 * n]
        ssem, rsem = refs[2 * n:]
        x, y, c = _place()

        def copy(t, j, s, half, dev):
            return pltpu.make_async_remote_copy(src_ref=o[t].at[s, c], dst_ref=o[t].at[s, half], send_sem=ssem.at[t, j],
                                                recv_sem=rsem.at[t, j], device_id=dev, device_id_type=MESH)

        chips = _other_chips(x, y)
        sends = [copy(t, j, s_p, c, (x, y, 1 - c)) for t in range(n) for j, (_, s_p) in enumerate(chips)]
        for cp in sends:
            cp.start()
        for t in range(n):
            for j, (_, s_p) in enumerate(chips):
                copy(t, j, s_p, 1 - c, (x, y, c)).wait_recv()
        for cp in sends:
            cp.wait_send()

    return _pcall(body, name=name, out_shape=[_sds(p.shape, bf16) for p in arrived], in_specs=[ANY] * n,
                  out_specs=[ANY] * n, aliases={t: t for t in range(n)},
                  scratch=[pltpu.SemaphoreType.DMA((n, 3)), pltpu.SemaphoreType.DMA((n, 3))])(*arrived)


def _rs_pair_start(grads, name):
    n = len(grads)

    def body(*refs):
        g, theirs = refs[:n], refs[n:2 * n]
        ssem, rsem, token = refs[2 * n], refs[2 * n + 1], refs[-1]
        x, y, c = _place()
        for t in range(n):
            pltpu.make_async_remote_copy(src_ref=g[t].at[:, 1 - c], dst_ref=theirs[t], send_sem=ssem.at[t],
                                         recv_sem=rsem.at[t], device_id=(x, y, 1 - c), device_id_type=MESH).start()
        token[...] = jnp.zeros_like(token)

    lands = [lax.empty((4,) + g.shape[2:], bf16) for g in grads]
    out_shape = ([pltpu.SemaphoreType.DMA((n,))] * 2 + [pltpu.HBM(g.shape, bf16) for g in grads]
                 + [pltpu.HBM(q.shape, bf16) for q in lands] + [_sds((8, LANES), f32)])
    outs = pl.pallas_call(
        body, name=name, out_shape=out_shape, in_specs=[HBM] * (2 * n), out_specs=[SEM, SEM] + [HBM] * (2 * n) + [VMEM],
        input_output_aliases={t: 2 + t for t in range(2 * n)},
        compiler_params=pltpu.CompilerParams(has_side_effects=EFFECT))(*[_in_hbm(a) for a in list(grads) + lands])
    return (outs[0], outs[1]), list(outs[2:2 + n]), list(outs[2 + n:2 + 2 * n]), outs[-1]


def _rs_pair_wait(sems, grads, lands, after, name):
    n = len(grads)

    def body(*refs):
        g, theirs = refs[:n], refs[n:2 * n]
        ssem, rsem = refs[2 * n], refs[2 * n + 1]
        x, y, c = _place()
        for t in range(n):
            cp = pltpu.make_async_remote_copy(src_ref=g[t].at[:, 1 - c], dst_ref=theirs[t], send_sem=ssem.at[t],
                                              recv_sem=rsem.at[t], device_id=(x, y, c), device_id_type=MESH)
            cp.wait_send()
            cp.wait_recv()

    outs = pl.pallas_call(
        body, name=name, out_shape=[pltpu.HBM(a.shape, bf16) for a in list(grads) + list(lands)],
        in_specs=[HBM] * (2 * n) + [SEM, SEM, ANY], out_specs=[HBM] * (2 * n),
        input_output_aliases={t: t for t in range(2 * n)},
        compiler_params=pltpu.CompilerParams(has_side_effects=EFFECT))(*grads, *lands, sems[0], sems[1], after)
    return list(outs[:n]), list(outs[n:])


def _rs_chip_start(pairs, name):
    n = len(pairs)

    def body(*refs):
        p, q = refs[:n], refs[n:2 * n]
        ssem, rsem, token = refs[2 * n], refs[2 * n + 1], refs[-1]
        x, y, c = _place()
        for t in range(n):
            for j, ((px, py), s_p) in enumerate(_other_chips(x, y)):
                pltpu.make_async_remote_copy(src_ref=p[t].at[s_p], dst_ref=q[t].at[j], send_sem=ssem.at[3 * t + j],
                                             recv_sem=rsem.at[3 * t + j], device_id=(px, py, c), device_id_type=MESH).start()
        token[...] = jnp.zeros_like(token)

    lands = [lax.empty((3,) + p.shape[1:], bf16) for p in pairs]
    out_shape = ([pltpu.SemaphoreType.DMA((3 * n,))] * 2 + [pltpu.HBM(p.shape, bf16) for p in pairs]
                 + [pltpu.HBM(q.shape, bf16) for q in lands] + [_sds((8, LANES), f32)])
    outs = pl.pallas_call(
        body, name=name, out_shape=out_shape, in_specs=[HBM] * (2 * n), out_specs=[SEM, SEM] + [HBM] * (2 * n) + [VMEM],
        input_output_aliases={t: 2 + t for t in range(2 * n)},
        compiler_params=pltpu.CompilerParams(has_side_effects=EFFECT))(*[_in_hbm(a) for a in list(pairs) + lands])
    return (outs[0], outs[1]), list(outs[2:2 + n]), list(outs[2 + n:2 + 2 * n]), outs[-1]


def _rs_chip_wait(sems, pairs, lands, after, name):
    n = len(pairs)

    def body(*refs):
        p, q = refs[:n], refs[n:2 * n]
        ssem, rsem = refs[2 * n], refs[2 * n + 1]
        x, y, c = _place()
        for t in range(n):
            for j, (_, s_p) in enumerate(_other_chips(x, y)):
                cp = pltpu.make_async_remote_copy(src_ref=p[t].at[s_p], dst_ref=q[t].at[j], send_sem=ssem.at[3 * t + j],
                                                  recv_sem=rsem.at[3 * t + j], device_id=(x, y, c), device_id_type=MESH)
                cp.wait_send()
                cp.wait_recv()

    outs = pl.pallas_call(
        body, name=name, out_shape=[pltpu.HBM(a.shape, bf16) for a in list(pairs) + list(lands)],
        in_specs=[HBM] * (2 * n) + [SEM, SEM, ANY], out_specs=[HBM] * (2 * n),
        input_output_aliases={t: t for t in range(2 * n)},
        compiler_params=pltpu.CompilerParams(has_side_effects=EFFECT))(*pairs, *lands, sems[0], sems[1], after)
    return list(outs[:n]), list(outs[n:])


def _rs_half_exchange(halves, name):
    n = len(halves)

    def body(*refs):
        o = refs[n:2 * n]
        ssem, rsem = refs[2 * n:]
        x, y, c = _place()

        def copy(t, half, dev):
            return pltpu.make_async_remote_copy(src_ref=o[t].at[c], dst_ref=o[t].at[half], send_sem=ssem.at[t],
                                                recv_sem=rsem.at[t], device_id=dev, device_id_type=MESH)

        sends = [copy(t, c, (x, y, 1 - c)) for t in range(n)]
        for cp in sends:
            cp.start()
        for t in range(n):
            copy(t, 1 - c, (x, y, c)).wait_recv()
        for cp in sends:
            cp.wait_send()

    return _pcall(body, name=name, out_shape=[_sds(h.shape, f32) for h in halves], in_specs=[ANY] * n,
                  out_specs=[ANY] * n, aliases={t: t for t in range(n)},
                  scratch=[pltpu.SemaphoreType.DMA((n,)), pltpu.SemaphoreType.DMA((n,))])(*halves)


def _row_spec(tm, cols):
    return pl.BlockSpec((tm, cols), lambda i: (i, 0))


def _vec_spec(cols, rows=1):
    return pl.BlockSpec((rows, cols), lambda i: (0, 0))


def _modulated_norm(xv, g, shift, scale):
    r = lax.rsqrt(jnp.mean(xv * xv, axis=-1, keepdims=True) + EPS)
    return (((xv * r) * g) * (1.0 + scale) + shift).astype(bf16)


def _hnorm(x, g, shift, scale):
    T, tm = x.shape[0], 256

    def body(x_ref, g_ref, sh_ref, sc_ref, h_ref):
        h_ref[...] = _modulated_norm(x_ref[...], g_ref[...], sh_ref[...], sc_ref[...])

    return _pcall(body, name="hnorm", out_shape=_sds((T, D), bf16), grid=(T // tm,),
                  in_specs=[_row_spec(tm, D), _vec_spec(D), _vec_spec(D), _vec_spec(D)],
                  out_specs=_row_spec(tm, D))(x, g, shift, scale)


def _out_proj(y2, wo, x, gate, nxt=None):
    T, tm = x.shape[0], 512

    def body(y_ref, w_ref, x_ref, g_ref, *rest):
        o = jnp.dot(y_ref[0], w_ref[0], preferred_element_type=f32)
        o = o + jnp.dot(y_ref[1], w_ref[1], preferred_element_type=f32)
        xo = x_ref[...] + g_ref[...] * o
        if nxt is None:
            xo_ref, o_ref = rest
        else:
            ng_ref, nsh_ref, nsc_ref, xo_ref, o_ref, h_ref = rest
            h_ref[...] = _modulated_norm(xo, ng_ref[...], nsh_ref[...], nsc_ref[...])
        o_ref[...] = o
        xo_ref[...] = xo

    extra = [] if nxt is None else list(nxt)
    n_out = 2 if nxt is None else 3
    return _pcall(body, name="out_proj", out_shape=[_sds((T, D), f32), _sds((T, D), f32), _sds((T, D), bf16)][:n_out],
                  grid=(T // tm,),
                  in_specs=[pl.BlockSpec((2, tm, D), lambda i: (0, i, 0)), pl.BlockSpec((2, D, D), lambda i: (0, 0, 0)),
                            _row_spec(tm, D), _vec_spec(D)] + [_vec_spec(D)] * len(extra),
                  out_specs=[_row_spec(tm, D)] * n_out, vmem_mb=48)(y2, wo, x, gate, *extra)


def _gate_bwd_tile(dx, o_ref, gate_ref, dob_ref, dgate_ref):
    dob_ref[...] = (dx * gate_ref[...]).astype(bf16)
    dgate_ref[...] += jnp.sum(dx * o_ref[...], axis=0, keepdims=True)


def _loss_bwd(x, target, g, o, gate):
    T, tm = x.shape[0], 256

    def body(x_ref, t_ref, g_ref, o_ref, gate_ref, dx_ref, loss_ref, dg_ref, dob_ref, dgate_ref):
        @pl.when(pl.program_id(0) == 0)
        def _():
            loss_ref[...] = jnp.zeros_like(loss_ref)
            dg_ref[...] = jnp.zeros_like(dg_ref)
            dgate_ref[...] = jnp.zeros_like(dgate_ref)

        xv, gv = x_ref[...], g_ref[...]
        r = lax.rsqrt(jnp.mean(xv * xv, axis=-1, keepdims=True) + EPS)
        xn = xv * r
        err = xn * gv - t_ref[...]
        dy = err * (1.0 / D)
        dxn = dy * gv
        dx = r * (dxn - xn * jnp.mean(dxn * xn, axis=-1, keepdims=True))
        dx_ref[...] = dx
        dg_ref[...] += jnp.sum(dy * xn, axis=0, keepdims=True)
        loss_ref[...] += (0.5 / D) * jnp.sum(jnp.sum(err * err, axis=1, keepdims=True), axis=0, keepdims=True)
        _gate_bwd_tile(dx, o_ref, gate_ref, dob_ref, dgate_ref)

    return _pcall(body, name="loss_bwd",
                  out_shape=[_sds((T, D), f32), _sds((1, 1), f32), _sds((1, D), f32), _sds((T, D), bf16), _sds((1, D), f32)],
                  grid=(T // tm,),
                  in_specs=[_row_spec(tm, D), _row_spec(tm, D), _vec_spec(D), _row_spec(tm, D), _vec_spec(D)],
                  out_specs=[_row_spec(tm, D), pl.BlockSpec((1, 1), lambda i: (0, 0)), _vec_spec(D), _row_spec(tm, D),
                             _vec_spec(D)])(x, target, g, o, gate)


def _norm_bwd(x, dh, gin, g, scale, below=None):
    T, tm = x.shape[0], 256

    def body(x_ref, dh_ref, gin_ref, g_ref, sc_ref, *rest):
        if below is None:
            dx_ref, st_ref = rest
        else:
            o_ref, gate_ref, dx_ref, st_ref, dob_ref, dgate_ref = rest

        @pl.when(pl.program_id(0) == 0)
        def _():
            st_ref[...] = jnp.zeros_like(st_ref)
            if below is not None:
                dgate_ref[...] = jnp.zeros_like(dgate_ref)

        xv, gv, dhv = x_ref[...], g_ref[...], dh_ref[...]
        r = lax.rsqrt(jnp.mean(xv * xv, axis=-1, keepdims=True) + EPS)
        xn = xv * r
        da = dhv * (1.0 + sc_ref[...])
        dxn = da * gv
        dx = gin_ref[...] + r * (dxn - xn * jnp.mean(dxn * xn, axis=-1, keepdims=True))
        dx_ref[...] = dx
        st_ref[0:1, :] += jnp.sum(dhv, axis=0, keepdims=True)
        st_ref[1:2, :] += jnp.sum(dhv * (xn * gv), axis=0, keepdims=True)
        st_ref[2:3, :] += jnp.sum(da * xn, axis=0, keepdims=True)
        if below is not None:
            _gate_bwd_tile(dx, o_ref, gate_ref, dob_ref, dgate_ref)

    out_shape = [_sds((T, D), f32), _sds((8, D), f32)]
    in_specs = [_row_spec(tm, D), _row_spec(tm, D), _row_spec(tm, D), _vec_spec(D), _vec_spec(D)]
    out_specs = [_row_spec(tm, D), _vec_spec(D, 8)]
    args = [x, dh, gin, g, scale]
    if below is not None:
        out_shape += [_sds((T, D), bf16), _sds((1, D), f32)]
        in_specs += [_row_spec(tm, D), _vec_spec(D)]
        out_specs += [_row_spec(tm, D), _vec_spec(D)]
        args += list(below)
    return _pcall(body, name="norm_bwd", out_shape=out_shape, grid=(T // tm,), in_specs=in_specs,
                  out_specs=out_specs)(*args)


def _cast_place(place, w, layer):
    _, rows, cols = w.shape
    tr = 256

    def body(place_ref, w_ref, o_ref):
        o_ref[...] = w_ref[...].astype(bf16)

    return _pcall(body, name="cast_place", out_shape=_sds((4, rows, cols), bf16), grid=(rows // tr,), prefetch=1,
                  in_specs=[pl.BlockSpec((None, tr, cols), lambda i, pr: (layer, i, 0))],
                  out_specs=pl.BlockSpec((None, tr, cols), lambda i, pr: (pr[0], i, 0)))(place, w)


def _rs_add(place, grad, theirs):
    _, rows, cols = theirs.shape
    tr = min(rows, 512)
    spec = pl.BlockSpec((None, tr, cols), lambda s, i, pr: (s, i, 0))

    def body(place_ref, a_ref, b_ref, o_ref):
        o_ref[...] = (a_ref[...].astype(f32) + b_ref[...].astype(f32)).astype(bf16)

    return _pcall(body, name="rs_add", out_shape=_sds(theirs.shape, bf16), grid=(4, rows // tr), prefetch=1,
                  in_specs=[pl.BlockSpec((None, None, tr, cols), lambda s, i, pr: (s, pr[1], i, 0)), spec],
                  out_specs=spec)(place, grad, theirs)


def _rs_sum(place, pairs, slots):
    _, rows, cols = slots.shape
    tr = min(rows, 512)

    def body(place_ref, p_ref, q_ref, o_ref):
        o_ref[...] = ((p_ref[...].astype(f32) + q_ref[0].astype(f32)) + q_ref[1].astype(f32)) + q_ref[2].astype(f32)

    return _pcall(body, name="rs_sum", out_shape=_sds((2, rows, cols), f32), grid=(rows // tr,), prefetch=1,
                  in_specs=[pl.BlockSpec((None, tr, cols), lambda i, pr: (pr[0], i, 0)),
                            pl.BlockSpec((3, tr, cols), lambda i, pr: (0, i, 0))],
                  out_specs=pl.BlockSpec((None, tr, cols), lambda i, pr: (pr[1], i, 0)))(place, pairs, slots)


def _adamw_math(w, g, m, v):
    m = ADAM_B1 * m + (1.0 - ADAM_B1) * g
    v = ADAM_B2 * v + (1.0 - ADAM_B2) * jnp.square(g)
    m_hat = m / (1.0 - ADAM_B1 ** ADAM_STEP)
    v_hat = v / (1.0 - ADAM_B2 ** ADAM_STEP)
    delta = -ADAM_LR * (m_hat / (jnp.sqrt(v_hat) + ADAM_EPS) + ADAM_WD * w)
    return delta, m, v


def _adamw_layer(layer, w, g, m, v, so_far):
    _, rows, cols = w.shape
    tr = 256
    spec = pl.BlockSpec((None, tr, cols), lambda i: (layer, i, 0))

    def body(w_ref, g_ref, m_ref, v_ref, *rest):
        go_ref, d_ref, mo_ref, vo_ref = rest[-4:]
        g = g_ref[...]
        go_ref[...] = g
        d_ref[...], mo_ref[...], vo_ref[...] = _adamw_math(w_ref[...], g, m_ref[...], v_ref[...])

    args, in_specs, aliases = [w, g, m, v], [spec, pl.BlockSpec((tr, cols), lambda i: (i, 0)), spec, spec], None
    if so_far is not None:
        args += list(so_far)
        in_specs += [ANY] * 4
        aliases = {4 + k: k for k in range(4)}
    return _pcall(body, name="adamw", out_shape=[_sds(w.shape, f32)] * 4, grid=(rows // tr,), in_specs=in_specs,
                  out_specs=[spec] * 4, aliases=aliases, vmem_mb=48)(*args)


def _adamw_small(items):
    n = len(items)

    def body(*refs):
        ins, outs = refs[:4 * n], refs[4 * n:]
        for t in range(n):
            w_ref, g_ref, m_ref, v_ref = ins[4 * t:4 * t + 4]
            if len(g_ref.shape) == len(w_ref.shape) + 1:
                g = g_ref[0]
                for b in range(1, g_ref.shape[0]):
                    g = g + g_ref[b]
            else:
                g = g_ref[...]
            d, m, v = _adamw_math(w_ref[...], g, m_ref[...], v_ref[...])
            outs[4 * t][...], outs[4 * t + 1][...], outs[4 * t + 2][...], outs[4 * t + 3][...] = g, d, m, v

    out_shape = [_sds(w.shape, f32) for (w, _, _, _) in items for _ in range(4)]
    flat = [a for it in items for a in it]
    res = _pcall(body, name="adamw_small", out_shape=out_shape, in_specs=[VMEM] * (4 * n),
                 out_specs=[VMEM] * (4 * n))(*flat)
    return [tuple(res[4 * t:4 * t + 4]) for t in range(n)]


NN = ((1,), (0,))
NT = ((1,), (1,))
TN = ((0,), (0,))


def _mm(name, a, b, *, grid, a_spec, b_spec, out_shape, out_spec, dims, vmem_mb=48):
    def body(a_ref, b_ref, o_ref):
        r = lax.dot_general(a_ref[...], b_ref[...], (dims, ((), ())), preferred_element_type=f32)
        o_ref[...] = r.astype(o_ref.dtype)

    return _pcall(body, name=name, out_shape=out_shape, grid=grid, in_specs=[a_spec, b_spec], out_specs=out_spec,
                  vmem_mb=vmem_mb)(a, b)


def _whole(shape):
    return pl.BlockSpec(shape, lambda j: (0,) * len(shape))


def _split_spec(rows, tile, per_split):
    return pl.BlockSpec((None, rows, tile), lambda j: (j // per_split, 0, j % per_split))


class _Proj:
    def __init__(self, n, splits, tile):
        self.n, self.splits, self.tile = n, splits, tile
        self.steps = n // tile
        self.w_per = n // 4 // tile
        self.a_per = n // splits // tile
        assert self.w_per * tile * 4 == n and self.a_per * tile * splits == n

    def fwd(self, hb, wg):
        T = hb.shape[0]
        return _mm("proj_fwd", hb, wg, grid=(self.steps,), a_spec=_whole((T, D)),
                   b_spec=_split_spec(D, self.tile, self.w_per),
                   out_shape=_sds((self.splits, T, self.n // self.splits), f32),
                   out_spec=_split_spec(T, self.tile, self.a_per), dims=NN)

    def dw(self, hb, dp):
        T = hb.shape[0]
        return _mm("proj_dw", hb, dp, grid=(self.steps,), a_spec=_whole((T, D)),
                   b_spec=_split_spec(T, self.tile, self.a_per), out_shape=_sds((4, D, self.n // 4), bf16),
                   out_spec=_split_spec(D, self.tile, self.w_per), dims=TN)

    def dh(self, dp, wg):
        T = dp.shape[1]
        sub, tile, w_per = DH_WIDE // self.tile, self.tile, self.w_per
        a_per = self.n // self.splits // DH_WIDE
        assert sub * tile == DH_WIDE and a_per * DH_WIDE * self.splits == self.n

        def w_tile(q):
            return pl.BlockSpec((None, D, tile), lambda k: ((sub * k + q) // w_per, 0, (sub * k + q) % w_per))

        def body(a_ref, *rest):
            o_ref = rest[sub]
            w = jnp.concatenate([rest[q][...] for q in range(sub)], axis=1)
            r = lax.dot_general(a_ref[...], w, (NT, ((), ())), preferred_element_type=f32)

            @pl.when(pl.program_id(0) == 0)
            def _():
                o_ref[...] = r

            @pl.when(pl.program_id(0) > 0)
            def _():
                o_ref[...] += r

        return _pcall(body, name="proj_dh", out_shape=_sds((T, D), f32), grid=(self.n // DH_WIDE,),
                      in_specs=[pl.BlockSpec((None, T, DH_WIDE), lambda k: (k // a_per, 0, k % a_per))]
                      + [w_tile(q) for q in range(sub)],
                      out_specs=_whole((T, D)), vmem_mb=48)(dp, *([wg] * sub))


EVEN_PROJ = _Proj(7 * D, 7, 256)
ODD_PROJ = _Proj(4 * D, 2, 512)


def _dy_mm(dob, wo):
    T = dob.shape[0]
    return _mm("out_dy", dob, wo, grid=(4,), a_spec=_whole((T, D)),
               b_spec=pl.BlockSpec((None, 512, D), lambda j: (j, 0, 0)), out_shape=_sds((2, T, D), f32),
               out_spec=_split_spec(T, 512, 2), dims=NT)


def _dwo_mm(y2, dob):
    T = dob.shape[0]
    return _mm("out_dw", y2, dob, grid=(4,), a_spec=_split_spec(T, 512, 2), b_spec=_whole((T, D)),
               out_shape=_sds((4, 512, D), bf16), out_spec=pl.BlockSpec((None, 512, D), lambda j: (j, 0, 0)), dims=TN)


def _head_spec(lead, T):
    return pl.BlockSpec((lead, T, HEAD), lambda h: (0, 0, h))


def _head_vec(rows):
    return pl.BlockSpec((rows, HEAD), lambda h: (0, h))


_HEAD_MAT = pl.BlockSpec((None, HEAD, HEAD), lambda h: (h, 0, 0))


def _causal():
    return lax.broadcasted_iota(jnp.int32, (HEAD, HEAD), 0) >= lax.broadcasted_iota(jnp.int32, (HEAD, HEAD), 1)


def _layernorm_head(v):
    mu = jnp.mean(v, axis=-1, keepdims=True)
    d = v - mu
    rstd = lax.rsqrt(jnp.mean(d * d, axis=-1, keepdims=True) + EPS)
    return d * rstd, rstd


def _even_fwd(p7, conv_w, ln_g, ln_b, sgu_w, sgu_bias):
    T, C = p7.shape[1], CHUNK_ROWS

    def body(p_ref, cw_ref, lg_ref, lb_ref, w_ref, b_ref, y_ref):
        w0, w1, w2 = cw_ref[0:1, :], cw_ref[1:2, :], cw_ref[2:3, :]
        wm = jnp.where(_causal(), w_ref[...], 0.0).astype(bf16)
        bias, lg, lb = b_ref[...], lg_ref[...], lb_ref[...]

        def step(i, halo):
            rows = pl.ds(pl.multiple_of(i * C, C), C)
            tt = p_ref[2, rows, :] * p_ref[0, rows, :]
            ext = jnp.concatenate([halo, tt], axis=0)
            cv = w2 * tt + w1 * pltpu.roll(ext, 1, 0)[HALO_CONV:] + w0 * pltpu.roll(ext, 2, 0)[HALO_CONV:]
            y_ref[0, rows, :] = (p_ref[1, rows, :] * cv * _silu(p_ref[3, rows, :])).astype(bf16)
            vhat, _ = _layernorm_head(p_ref[5, rows, :])
            vn = (vhat * lg + lb).astype(bf16)
            mix = jnp.concatenate([jnp.dot(wm, vn[k * HEAD:(k + 1) * HEAD], preferred_element_type=f32) + bias
                                   for k in range(C // HEAD)], axis=0)
            y_ref[1, rows, :] = (p_ref[4, rows, :] * mix * _silu(p_ref[6, rows, :])).astype(bf16)
            return tt[C - HALO_CONV:]

        lax.fori_loop(0, T // C, step, jnp.zeros((HALO_CONV, HEAD), f32))

    return _pcall(body, name="even_fwd", out_shape=_sds((2, T, D), bf16), grid=(NH,),
                  in_specs=[_head_spec(7, T), _head_vec(3), _head_vec(1), _head_vec(1), _HEAD_MAT, _HEAD_MAT],
                  out_specs=_head_spec(2, T), vmem_mb=32)(p7, conv_w, ln_g, ln_b, sgu_w, sgu_bias)


def _even_bwd(p7, dy2, conv_w, ln_g, ln_b, sgu_w, sgu_bias):
    T, C = p7.shape[1], CHUNK_ROWS
    n_chunks = T // C

    def body(p_ref, dy_ref, cw_ref, lg_ref, lb_ref, w_ref, b_ref,
             dp_ref, dcw_ref, dlg_ref, dlb_ref, dw_ref, dms_ref, dcv_s):
        w0, w1, w2 = cw_ref[0:1, :], cw_ref[1:2, :], cw_ref[2:3, :]
        tri = _causal()
        wm = jnp.where(tri, w_ref[...], 0.0).astype(bf16)
        bias, lg, lb = b_ref[...], lg_ref[...], lb_ref[...]
        dw_ref[...] = jnp.zeros_like(dw_ref)
        dms_ref[...] = jnp.zeros_like(dms_ref)

        def fwd_step(i, carry):
            halo, a0, a1, a2, alg, alb = carry
            rows = pl.ds(pl.multiple_of(i * C, C), C)
            ah, ab, ac, az = p_ref[0, rows, :], p_ref[1, rows, :], p_ref[2, rows, :], p_ref[3, rows, :]
            dya = dy_ref[0, rows, :]
            tt = ac * ah
            ext = jnp.concatenate([halo, tt], axis=0)
            t1, t2 = pltpu.roll(ext, 1, 0)[HALO_CONV:], pltpu.roll(ext, 2, 0)[HALO_CONV:]
            cv = w2 * tt + w1 * t1 + w0 * t2
            sa, dsa = _silu_and_grad(az)
            g1 = dya * sa
            dp_ref[1, rows, :] = (g1 * cv).astype(bf16)
            dp_ref[3, rows, :] = (dya * ab * cv * dsa).astype(bf16)
            dcv = g1 * ab
            dcv_s[rows, :] = dcv
            a2 = a2 + jnp.sum(dcv * tt, axis=0, keepdims=True)
            a1 = a1 + jnp.sum(dcv * t1, axis=0, keepdims=True)
            a0 = a0 + jnp.sum(dcv * t2, axis=0, keepdims=True)

            u, zb, dyb = p_ref[4, rows, :], p_ref[6, rows, :], dy_ref[1, rows, :]
            vhat, rstd = _layernorm_head(p_ref[5, rows, :])
            vn = (vhat * lg + lb).astype(bf16)
            sb, dsb = _silu_and_grad(zb)
            mix = jnp.concatenate([jnp.dot(wm, vn[k * HEAD:(k + 1) * HEAD], preferred_element_type=f32) + bias
                                   for k in range(C // HEAD)], axis=0)
            dp_ref[4, rows, :] = (dyb * mix * sb).astype(bf16)
            dp_ref[6, rows, :] = (dyb * u * mix * dsb).astype(bf16)
            dmix = dyb * u * sb
            dvn_parts = []
            for k in range(C // HEAD):
                dm = dmix[k * HEAD:(k + 1) * HEAD]
                dmb = dm.astype(bf16)
                dvn_parts.append(lax.dot_general(wm, dmb, (TN, ((), ())), preferred_element_type=f32))
                dw_ref[...] += lax.dot_general(dmb, vn[k * HEAD:(k + 1) * HEAD], (NT, ((), ())),
                                               preferred_element_type=f32)
                dms_ref[...] += dm
            dvn = jnp.concatenate(dvn_parts, axis=0)
            alg = alg + jnp.sum(dvn * vhat, axis=0, keepdims=True)
            alb = alb + jnp.sum(dvn, axis=0, keepdims=True)
            dvh = dvn * lg
            dv = rstd * (dvh - jnp.mean(dvh, axis=-1, keepdims=True)
                         - vhat * jnp.mean(dvh * vhat, axis=-1, keepdims=True))
            dp_ref[5, rows, :] = dv.astype(bf16)
            return tt[C - HALO_CONV:], a0, a1, a2, alg, alb

        zrow = jnp.zeros((1, HEAD), f32)
        _, a0, a1, a2, alg, alb = lax.fori_loop(
            0, n_chunks, fwd_step, (jnp.zeros((HALO_CONV, HEAD), f32), zrow, zrow, zrow, zrow, zrow))
        dcw_ref[0:1, :], dcw_ref[1:2, :], dcw_ref[2:3, :] = a0, a1, a2
        dlg_ref[...], dlb_ref[...] = alg, alb
        dw_ref[...] = jnp.where(tri, dw_ref[...], 0.0)

        def bwd_step(k, halo):
            rows = pl.ds(pl.multiple_of((n_chunks - 1 - k) * C, C), C)
            dcv = dcv_s[rows, :]
            ext = jnp.concatenate([dcv, halo], axis=0)
            n1 = pltpu.roll(ext, C + HALO_CONV - 1, 0)[:C]
            n2 = pltpu.roll(ext, C + HALO_CONV - 2, 0)[:C]
            dtt = w2 * dcv + w1 * n1 + w0 * n2
            dp_ref[2, rows, :] = (dtt * p_ref[0, rows, :]).astype(bf16)
            dp_ref[0, rows, :] = (dtt * p_ref[2, rows, :]).astype(bf16)
            return dcv[:HALO_CONV]

        lax.fori_loop(0, n_chunks, bwd_step, jnp.zeros((HALO_CONV, HEAD), f32))

    out_shape = [_sds((7, T, D), bf16), _sds((3, D), f32), _sds((1, D), f32), _sds((1, D), f32),
                 _sds((NH, HEAD, HEAD), f32), _sds((NH, HEAD, HEAD), f32)]
    return _pcall(body, name="even_bwd", out_shape=out_shape, grid=(NH,),
                  in_specs=[_head_spec(7, T), _head_spec(2, T), _head_vec(3), _head_vec(1), _head_vec(1),
                            _HEAD_MAT, _HEAD_MAT],
                  out_specs=[_head_spec(7, T), _head_vec(3), _head_vec(1), _head_vec(1), _HEAD_MAT, _HEAD_MAT],
                  scratch=[pltpu.VMEM((T, HEAD), f32)], vmem_mb=48)(p7, dy2, conv_w, ln_g, ln_b, sgu_w, sgu_bias)


def _window_sum(ext, win, towards_past):
    n, k, s = ext.shape[0], 1, ext
    while k < win:
        s = s + pltpu.roll(s, k if towards_past else n - k, 0)
        k *= 2
    return s


def _pool_count(i, C, win):
    t = i * C + lax.broadcasted_iota(jnp.int32, (C, 1), 0)
    cnt = jnp.minimum(t + 1, win).astype(f32)
    return cnt, 1.0 / cnt


def _group_specs(T):
    p_spec = pl.BlockSpec((None, T, GC), lambda g: (0, 0, g))
    z_spec = pl.BlockSpec((None, T, GC), lambda g: (1, 0, g))
    pw_spec = pl.BlockSpec((4, GC // 4, GC), lambda g: (0, g, 0))
    ps_spec = pl.BlockSpec((1, GC), lambda g: (0, g))
    y_spec = pl.BlockSpec((None, T, GC), lambda g: (g // 2, 0, g % 2))
    return p_spec, z_spec, pw_spec, ps_spec, y_spec


def _odd_fwd(p2, pool_wg, pool_scale):
    T, C = p2.shape[1], CHUNK_ROWS
    p_spec, z_spec, pw_spec, ps_spec, y_spec = _group_specs(T)

    def body(p_ref, z_ref, pw_ref, ps_ref, y_ref):
        pw, ps = pw_ref[...].reshape(GC, GC), ps_ref[...]

        def run(win):
            def step(i, halo):
                rows = pl.ds(pl.multiple_of(i * C, C), C)
                p = p_ref[rows, :]
                s = _window_sum(jnp.concatenate([halo, p], axis=0), win, True)[HALO_POOL:]
                pooled = s * _pool_count(i, C, win)[1] - p
                ypre = jnp.dot(pooled.astype(bf16), pw, preferred_element_type=f32)
                y_ref[rows, :] = (ypre * ps * _silu(z_ref[rows, :])).astype(bf16)
                return p[C - HALO_POOL:]

            lax.fori_loop(0, T // C, step, jnp.zeros((HALO_POOL, GC), f32))

        for gi, win in enumerate(WINDOWS):
            pl.when(pl.program_id(0) == gi)(functools.partial(run, win))

    return _pcall(body, name="odd_fwd", out_shape=_sds((2, T, D), bf16), grid=(len(WINDOWS),),
                  in_specs=[p_spec, z_spec, pw_spec, ps_spec], out_specs=y_spec, vmem_mb=40)(p2, p2, pool_wg, pool_scale)


def _odd_bwd(p2, dy2, pool_wg, pool_scale):
    T, C = p2.shape[1], CHUNK_ROWS
    n_chunks = T // C
    p_spec, z_spec, pw_spec, ps_spec, y_spec = _group_specs(T)

    def body(p_ref, z_ref, dy_ref, pw_ref, ps_ref, dp_ref, dpw_ref, dps_ref, q_s, acc_s):
        pw, ps = pw_ref[...].reshape(GC, GC), ps_ref[...]

        def run(win):
            acc_s[...] = jnp.zeros_like(acc_s)

            def fwd_step(i, carry):
                halo, aps = carry
                rows = pl.ds(pl.multiple_of(i * C, C), C)
                p, z, dy = p_ref[rows, :], z_ref[rows, :], dy_ref[rows, :]
                _, inv_cnt = _pool_count(i, C, win)
                s = _window_sum(jnp.concatenate([halo, p], axis=0), win, True)[HALO_POOL:]
                pb = (s * inv_cnt - p).astype(bf16)
                ypre = jnp.dot(pb, pw, preferred_element_type=f32)
                sz, dsz = _silu_and_grad(z)
                aps = aps + jnp.sum(dy * ypre * sz, axis=0, keepdims=True)
                dp_ref[1, rows, :] = (dy * ypre * ps * dsz).astype(bf16)
                dyp = (dy * ps * sz).astype(bf16)
                acc_s[...] += lax.dot_general(pb, dyp, (TN, ((), ())), preferred_element_type=f32)
                dpool = lax.dot_general(dyp, pw, (NT, ((), ())), preferred_element_type=f32)
                q_s[rows, :] = dpool * inv_cnt
                return p[C - HALO_POOL:], aps

            _, aps = lax.fori_loop(0, n_chunks, fwd_step, (jnp.zeros((HALO_POOL, GC), f32), jnp.zeros((1, GC), f32)))
            dps_ref[...] = aps
            dpw_ref[...] = acc_s[...].reshape(4, GC // 4, GC).astype(bf16)

            def bwd_step(k, halo):
                i = n_chunks - 1 - k
                rows = pl.ds(pl.multiple_of(i * C, C), C)
                q = q_s[rows, :]
                s = _window_sum(jnp.concatenate([q, halo], axis=0), win, False)[:C]
                dp_ref[0, rows, :] = (s - q * _pool_count(i, C, win)[0]).astype(bf16)
                return q[:HALO_POOL]

            lax.fori_loop(0, n_chunks, bwd_step, jnp.zeros((HALO_POOL, GC), f32))

        for gi, win in enumerate(WINDOWS):
            pl.when(pl.program_id(0) == gi)(functools.partial(run, win))

    out_shape = [_sds((2, T, 2 * D), bf16), _sds((4, GC, GC), bf16), _sds((1, 2 * D), f32)]
    return _pcall(body, name="odd_bwd", out_shape=out_shape, grid=(len(WINDOWS),),
                  in_specs=[p_spec, z_spec, y_spec, pw_spec, ps_spec],
                  out_specs=[pl.BlockSpec((2, T, GC), lambda g: (0, 0, g)), pw_spec, ps_spec],
                  scratch=[pltpu.VMEM((T, GC), f32), pltpu.VMEM((GC, GC), f32)], vmem_mb=52)(
                      p2, p2, dy2, pool_wg, pool_scale)


def _ada_fwd(c_all, ada_w):
    cols = ada_w.shape[2]

    def body(c_ref, w_ref, o_ref):
        o_ref[...] = jnp.dot(_silu(c_ref[...]), w_ref[...], preferred_element_type=f32,
                             precision=lax.Precision.HIGHEST)

    return _pcall(body, name="ada_fwd", out_shape=_sds((4, N_DEV, cols), f32), grid=(4,),
                  in_specs=[pl.BlockSpec((N_DEV, D), lambda i: (0, 0)), pl.BlockSpec((None, D, cols), lambda i: (i, 0, 0))],
                  out_specs=pl.BlockSpec((None, N_DEV, cols), lambda i: (i, 0, 0)))(c_all, ada_w)


def _ada_bwd(c_all_t, dmod, w, m, v):
    cols, tr = w.shape[2], 256
    spec = pl.BlockSpec((None, tr, cols), lambda l, i: (l, i, 0))

    def body(c_ref, dm_ref, w_ref, m_ref, v_ref, g_ref, d_ref, mo_ref, vo_ref):
        sc = _silu(c_ref[...])
        g = sc[:, 0:1] * dm_ref[0:1, :]
        for b in range(1, N_DEV):
            g = g + sc[:, b:b + 1] * dm_ref[b:b + 1, :]
        g_ref[...] = g
        d_ref[...], mo_ref[...], vo_ref[...] = _adamw_math(w_ref[...], g, m_ref[...], v_ref[...])

    return _pcall(body, name="ada_bwd", out_shape=[_sds(w.shape, f32)] * 4, grid=(4, D // tr),
                  in_specs=[pl.BlockSpec((tr, N_DEV), lambda l, i: (i, 0)),
                            pl.BlockSpec((None, N_DEV, cols), lambda l, i: (l, 0, 0)), spec, spec, spec],
                  out_specs=[spec] * 4)(c_all_t, dmod, w, m, v)


def _layer_fwd(even, x, hb, gate, w, nxt):
    if even:
        w_in, w_out, conv_w, ln_g, ln_b, sgu_w, sgu_b = w
        bias = jnp.broadcast_to(sgu_b[:, :, None], (NH, HEAD, HEAD))
        p = EVEN_PROJ.fwd(hb, w_in)
        y2 = _even_fwd(p, conv_w, ln_g, ln_b, sgu_w, bias)
    else:
        w_in, pool_w, w_out, pool_scale = w
        p = ODD_PROJ.fwd(hb, w_in)
        y2 = _odd_fwd(p, pool_w, pool_scale)
    outs = _out_proj(y2, w_out.reshape(2, D, D), x, gate, nxt)
    return outs[0], (None if nxt is None else outs[2]), (x, hb, p, y2, outs[1])


def _layer_bwd(even, gin, dob, dgate, saved, scale, g, w, below=None, send=None):
    x_in, hb, p, y2, o = saved
    if even:
        w_in, w_out, conv_w, ln_g, ln_b, sgu_w, sgu_b = w
        bias = jnp.broadcast_to(sgu_b[:, :, None], (NH, HEAD, HEAD))
        dy2 = _dy_mm(dob, w_out)
        dp, dconv, dlg, dlb, dsw, dms = _even_bwd(p, dy2, conv_w, ln_g, ln_b, sgu_w, bias)
        proj = EVEN_PROJ
        small = dict(conv_w=dconv, ln_g=dlg, ln_b=dlb, sgu_w=dsw, sgu_b=jnp.sum(dms, axis=-1))
        big = [proj.dw(hb, dp), _dwo_mm(y2, dob)]
    else:
        w_in, pool_w, w_out, pool_scale = w
        dy2 = _dy_mm(dob, w_out)
        dp, dpw, dps = _odd_bwd(p, dy2, pool_w, pool_scale)
        proj = ODD_PROJ
        small = dict(pool_scale=dps)
        big = [proj.dw(hb, dp), dpw, _dwo_mm(y2, dob)]
    if send is not None:
        big, tok = send(big)
        scale = scale + tok[0:1, 0:1]
    dh = proj.dh(dp, w_in)
    res = _norm_bwd(x_in, dh, gin, g, scale, below)
    stats = res[1]
    return (res[0], (None if below is None else (res[2], res[3])), big, small,
            jnp.concatenate([stats[0:2], dgate], axis=0), stats[2:3])


def _pack_rows(parts):
    rows = [p.reshape(-1, LANES) for p in parts]
    total = sum(r.shape[0] for r in rows)
    padded = -(-total // (8 * N_DEV)) * (8 * N_DEV)
    if padded > total:
        rows.append(jnp.zeros((padded - total, LANES), f32))
    return jnp.concatenate(rows, axis=0)


def _unpack_rows(buf, shapes):
    out, r = [], 0
    for shp in shapes:
        n = 1
        for d in shp:
            n *= d
        out.append(buf[r:r + n // LANES].reshape(shp))
        r += n // LANES
    return out


def kernel(x, c, norm_g, ada_w, ada_b, ab_w_in, ab_conv_w, ab_ln_g, ab_ln_b, ab_sgu_w, ab_sgu_b, ab_w_out, c_w_in, c_pool_w, c_pool_scale, c_w_out, final_g, loss_target, m_norm_g, m_ada_w, m_ada_b, m_ab_w_in, m_ab_conv_w, m_ab_ln_g, m_ab_ln_b, m_ab_sgu_w, m_ab_sgu_b, m_ab_w_out, m_c_w_in, m_c_pool_w, m_c_pool_scale, m_c_w_out, m_final_g, v_norm_g, v_ada_w, v_ada_b, v_ab_w_in, v_ab_conv_w, v_ab_ln_g, v_ab_ln_b, v_ab_sgu_w, v_ab_sgu_b, v_ab_w_out, v_c_w_in, v_c_pool_w, v_c_pool_scale, v_c_w_out, v_final_g):
    ix, iy, ic = _place()
    chip, dev = 2 * ix + iy, 4 * ix + 2 * iy + ic
    n_even, n_odd = ab_w_in.shape[0], c_w_in.shape[0]
    depth = n_even + n_odd
    acols = ada_w.shape[2]

    place = jnp.stack([chip, ic]).astype(jnp.int32)
    even_names, odd_names = ["ab_w_in", "ab_w_out"], ["c_w_in", "c_pool_w", "c_w_out"]
    params = {"ab_w_in": (ab_w_in, m_ab_w_in, v_ab_w_in), "ab_w_out": (ab_w_out, m_ab_w_out, v_ab_w_out),
              "c_w_in": (c_w_in, m_c_w_in, v_c_w_in), "c_w_out": (c_w_out, m_c_w_out, v_c_w_out),
              "c_pool_w": tuple(a.reshape(n_odd, GC, GC) for a in (c_pool_w, m_c_pool_w, v_c_pool_w))}

    c_all = _gather8(c, "gather_c").reshape(N_DEV, D)
    modp = _ada_fwd(c_all, ada_w)
    modg = _gather8(modp, "gather_mod")
    mod_rows = lax.dynamic_index_in_dim(modg[0::2], dev, axis=2, keepdims=False)
    mod = jnp.transpose(mod_rows, (1, 0, 2)).reshape(depth, 3 * D) + ada_b
    mods = [(mod[i:i + 1, 0:D], mod[i:i + 1, D:2 * D], mod[i:i + 1, 2 * D:3 * D]) for i in range(depth)]

    def shard_cols(a, width):
        return lax.dynamic_slice_in_dim(a, chip * width, width, axis=a.ndim - 1)

    small_sharded = jnp.concatenate([ab_conv_w.reshape(1, -1), c_pool_scale.reshape(1, -1)], axis=1)
    small_all = _gather8(small_sharded, "gather_small")[0::2, 0]
    n_conv = ab_conv_w.size
    conv_all = small_all[:, :n_conv].reshape(4, n_even, 3, D // 4)
    conv_full = jnp.transpose(conv_all, (1, 2, 0, 3)).reshape(n_even, 3, D)
    scale_all = small_all[:, n_conv:].reshape(4, n_odd, 2 * D // 4)
    scale_full = jnp.transpose(scale_all, (1, 0, 2)).reshape(n_odd, 2 * D)

    def placed(i):
        ws = [params[nm][0] for nm in (even_names if i % 2 == 0 else odd_names)]
        return [_cast_place(place, w, i // 2).reshape(4, 2, w.shape[1] // 2, w.shape[2]) for w in ws]

    gathers_done = mod[0:1, 0:LANES] + scale_full[0:1, 0:LANES]
    sems_0, inflight_0, tok = _ag_start([placed(0)], gathers_done, "ag_start_0")
    sems_r, inflight_r, tok = _ag_start([placed(i) for i in range(1, depth)], tok, "ag_start_rest")
    ag_sems, inflight = sems_0 + sems_r, inflight_0 + inflight_r

    x_cur, after, saved, weights = x[0], tok, [], []
    hb = _hnorm(x_cur, norm_g[0:1], mods[0][0], mods[0][1])
    for i in range(depth):
        j = i // 2
        arrived = _ag_wait(inflight[i], ag_sems[i], after, f"ag_wait_{i}")
        full = [g.reshape(4, 2 * g.shape[2], g.shape[3]) for g in _ag_forward(arrived, "ag_forward")]
        if i % 2 == 0:
            w = (full[0], full[1], conv_full[j], ab_ln_g[j:j + 1], ab_ln_b[j:j + 1], ab_sgu_w[j], ab_sgu_b[j])
        else:
            w = (full[0], full[1], full[2], scale_full[j:j + 1])
        weights.append(w)
        nxt = (norm_g[i + 1:i + 2], mods[i + 1][0], mods[i + 1][1]) if i + 1 < depth else None
        x_cur, hb, sv = _layer_fwd(i % 2 == 0, x_cur, hb, mods[i][2], w, nxt)
        saved.append(sv)
        after = x_cur
    gin, loss, dfinal_g, dob, dgate = _loss_bwd(x_cur, loss_target[0], final_g.reshape(1, D), saved[-1][4],
                                                mods[-1][2])

    stacked = {}

    def finish(i, sems, pairs, lands, after):
        pairs, slots = _rs_chip_wait(sems, pairs, lands, after, f"rs_chip_wait_{i}")
        halves = [_rs_sum(place, p, q) for p, q in zip(pairs, slots)]
        for nm, g in zip(even_names if i % 2 == 0 else odd_names, _rs_half_exchange(halves, "rs_half_exchange")):
            w, m, v = params[nm]
            stacked[nm] = _adamw_layer(i // 2, w, g.reshape(w.shape[1], w.shape[2]), m, v, stacked.get(nm))

    small_g, dmod, dnorm_g, pending, tok = [None] * depth, [None] * depth, [None] * depth, None, None
    for i in reversed(range(depth)):
        w = weights[i]
        if tok is not None:
            w = w[:2] + (w[2] + tok[0:1, 0:1],) + w[3:] if i % 2 == 0 else w[:3] + (w[3] + tok[0:1, 0:1],)
        below = (saved[i - 1][4], mods[i - 1][2]) if i > 0 else None

        def send(big_g, i=i):
            big_g = [g.reshape(4, 2, g.shape[1] // 2, g.shape[2]) for g in big_g]
            sems, big_g, lands, tok = _rs_pair_start(big_g, f"rs_pair_start_{i}")
            return (sems, big_g, lands), tok

        gin, gate_bwd, sent, small_g[i], dmod[i], dnorm_g[i] = _layer_bwd(
            i % 2 == 0, gin, dob, dgate, saved[i], mods[i][1], norm_g[i:i + 1], w, below, send)
        if below is not None:
            dob, dgate = gate_bwd
        big_g, theirs = _rs_pair_wait(*sent, gin, f"rs_pair_wait_{i}")
        pairs = [_rs_add(place, a, b) for a, b in zip(big_g, theirs)]
        sems, pairs, lands, tok = _rs_chip_start(pairs, f"rs_chip_start_{i}")
        if pending is not None:
            finish(*pending, tok)
        pending = (i, sems, pairs, lands)
    grad_x = gin
    dmod, dnorm_g = jnp.stack(dmod), jnp.concatenate(dnorm_g, axis=0)

    small_parts = [dnorm_g + tok[0:1, 0:1], dfinal_g,
                   jnp.stack([small_g[2 * j]["conv_w"] for j in range(n_even)]),
                   jnp.concatenate([small_g[2 * j]["ln_g"] for j in range(n_even)], axis=0),
                   jnp.concatenate([small_g[2 * j]["ln_b"] for j in range(n_even)], axis=0),
                   jnp.stack([small_g[2 * j]["sgu_w"] for j in range(n_even)]),
                   jnp.stack([small_g[2 * j]["sgu_b"] for j in range(n_even)]),
                   jnp.concatenate([small_g[2 * j + 1]["pool_scale"] for j in range(n_odd)], axis=0),
                   jnp.pad(loss, ((0, 7), (0, LANES - 1)))]
    small_shapes = [p.shape for p in small_parts]
    reduced = _allreduce8(_pack_rows(small_parts), "allreduce_small")
    (g_norm_g, g_final_g, g_conv_full, g_ln_g, g_ln_b, g_sgu_w, g_sgu_b, g_scale_full,
     loss_row) = _unpack_rows(reduced, small_shapes)
    loss = loss_row[0, 0]
    g_conv = shard_cols(g_conv_full, D // 4)
    g_scale = shard_cols(g_scale_full, 2 * D // 4)
    dmod_all = _gather8(dmod.reshape(depth * 3 * D // LANES, LANES), "gather_dmod").reshape(N_DEV, depth, 3 * D)

    def two_d(a):
        return a.reshape(-1, a.shape[-1])

    small = [(norm_g, g_norm_g, m_norm_g, v_norm_g),
             (ada_b, dmod_all, m_ada_b, v_ada_b),
             (two_d(ab_conv_w), two_d(g_conv), two_d(m_ab_conv_w), two_d(v_ab_conv_w)),
             (ab_ln_g, g_ln_g, m_ab_ln_g, v_ab_ln_g),
             (ab_ln_b, g_ln_b, m_ab_ln_b, v_ab_ln_b),
             (two_d(ab_sgu_w), two_d(g_sgu_w), two_d(m_ab_sgu_w), two_d(v_ab_sgu_w)),
             (two_d(ab_sgu_b), two_d(g_sgu_b), two_d(m_ab_sgu_b), two_d(v_ab_sgu_b)),
             (c_pool_scale, g_scale, m_c_pool_scale, v_c_pool_scale),
             (final_g.reshape(1, D), g_final_g, m_final_g.reshape(1, D), v_final_g.reshape(1, D))]
    small_res = _adamw_small(small)
    small_shapes_out = [norm_g.shape, ada_b.shape, ab_conv_w.shape, ab_ln_g.shape, ab_ln_b.shape, ab_sgu_w.shape,
                        ab_sgu_b.shape, c_pool_scale.shape, final_g.shape]
    (r_norm_g, r_ada_b, r_conv, r_ln_g, r_ln_b, r_sgu_w, r_sgu_b, r_scale, r_final_g) = [
        tuple(a.reshape(shp) for a in res) for res, shp in zip(small_res, small_shapes_out)]

    dmod_cols = jnp.transpose(shard_cols(dmod_all, acols), (1, 0, 2))
    r_ada_w = _ada_bwd(c_all.T, dmod_cols, ada_w, m_ada_w, v_ada_w)

    finish(*pending, r_ada_w[1])
    r_ab_w_in, r_ab_w_out, r_c_w_in, r_c_w_out = (stacked[nm] for nm in ("ab_w_in", "ab_w_out", "c_w_in", "c_w_out"))
    r_c_pool_w = tuple(a.reshape(c_pool_w.shape) for a in stacked["c_pool_w"])

    order = [r_norm_g, r_ada_w, r_ada_b, r_ab_w_in, r_conv, r_ln_g, r_ln_b, r_sgu_w, r_sgu_b, r_ab_w_out,
             r_c_w_in, r_c_pool_w, r_scale, r_c_w_out, r_final_g]
    outs = [loss, grad_x[None]]
    for field in range(4):
        outs += [r[field] for r in order]
    return tuple(outs)
```

```python
import functools

import jax
import jax.numpy as jnp
from jax import lax
from jax.experimental import pallas as pl
from jax.experimental.pallas import tpu as pltpu

f32, bf16 = jnp.float32, jnp.bfloat16

D = 1024
HEAD = 128
NH = 8
WINDOWS = (2, 4, 8, 16)
GC = 512
EPS = 1e-6
HALO_CONV = 8
HALO_POOL = 16
CHUNK_ROWS = 512
DH_WIDE = 1024
N_DEV = 8
LANES = 128

ADAM_LR, ADAM_B1, ADAM_B2, ADAM_EPS, ADAM_WD, ADAM_STEP = 0.001, 0.9, 0.999, 1e-08, 0.01, 10

MESH = pl.DeviceIdType.MESH
ANY = pl.BlockSpec(memory_space=pl.ANY)
VMEM = pl.BlockSpec(memory_space=pltpu.VMEM)
MIB = 2 ** 20


def _pcall(body, *, name, out_shape, grid=None, in_specs=None, out_specs=None, scratch=(), vmem_mb=None,
           aliases=None, prefetch=0):
    kw = {}
    if prefetch:
        kw["grid_spec"] = pltpu.PrefetchScalarGridSpec(num_scalar_prefetch=prefetch, grid=grid, in_specs=in_specs,
                                                       out_specs=out_specs, scratch_shapes=list(scratch))
    else:
        if grid is not None:
            kw["grid"] = grid
        if in_specs is not None:
            kw["in_specs"] = in_specs
        if out_specs is not None:
            kw["out_specs"] = out_specs
        if scratch:
            kw["scratch_shapes"] = list(scratch)
    if aliases:
        kw["input_output_aliases"] = aliases
    params = pltpu.CompilerParams(vmem_limit_bytes=None if vmem_mb is None else vmem_mb * MIB)
    return pl.pallas_call(body, name=name, out_shape=out_shape, compiler_params=params, **kw)


def _sds(shape, dtype):
    return jax.ShapeDtypeStruct(tuple(shape), dtype)


def _sigmoid(z):
    return pl.reciprocal(1.0 + jnp.exp(-z), approx=True)


def _silu(z):
    return z * _sigmoid(z)


def _silu_and_grad(z):
    s = _sigmoid(z)
    return z * s, s * (1.0 + z * (1.0 - s))


def _place():
    return lax.axis_index("x"), lax.axis_index("y"), lax.axis_index("c")


def _gather8(blk, name):
    def body(x_ref, o_ref, ssem, rsem):
        x, y, c = _place()
        me = 4 * x + 2 * y + c
        o_ref[me] = x_ref[...]
        sends = []
        for k in range(1, N_DEV):
            px = 1 - x if k & 4 else x
            py = 1 - y if k & 2 else y
            pc = 1 - c if k & 1 else c
            cp = pltpu.make_async_remote_copy(src_ref=x_ref, dst_ref=o_ref.at[me], send_sem=ssem.at[k - 1],
                                              recv_sem=rsem.at[k - 1], device_id=(px, py, pc), device_id_type=MESH)
            cp.start()
            sends.append((cp, 4 * px + 2 * py + pc))
        for k, (cp, peer) in enumerate(sends):
            pltpu.make_async_remote_copy(src_ref=x_ref, dst_ref=o_ref.at[peer], send_sem=ssem.at[k],
                                         recv_sem=rsem.at[k], device_id=(x, y, c), device_id_type=MESH).wait_recv()
        for cp, _ in sends:
            cp.wait_send()

    return _pcall(body, name=name, out_shape=_sds((N_DEV,) + blk.shape, blk.dtype), in_specs=[VMEM], out_specs=VMEM,
                  scratch=[pltpu.SemaphoreType.DMA((N_DEV - 1,)), pltpu.SemaphoreType.DMA((N_DEV - 1,))])(blk)


def _allreduce8(buf, name):
    rows = buf.shape[0]
    rb = rows // N_DEV
    assert rb * N_DEV == rows and rb % 8 == 0

    def body(x_ref, o_ref, stage, ssem, rsem):
        x, y, c = _place()
        me = 4 * x + 2 * y + c
        peers = []
        for k in range(1, N_DEV):
            px = 1 - x if k & 4 else x
            py = 1 - y if k & 2 else y
            pc = 1 - c if k & 1 else c
            peers.append(((px, py, pc), 4 * px + 2 * py + pc))

        def blk(ref, idx):
            return ref.at[pl.ds(pl.multiple_of(idx * rb, 8), rb), :]

        def copy(phase, k, src, dst, dev):
            return pltpu.make_async_remote_copy(src_ref=src, dst_ref=dst, send_sem=ssem.at[phase, k],
                                                recv_sem=rsem.at[phase, k], device_id=dev, device_id_type=MESH)

        stage[me] = blk(x_ref, me)[...]
        scatter = [copy(0, k, blk(x_ref, pidx), stage.at[me], dev) for k, (dev, pidx) in enumerate(peers)]
        for cp in scatter:
            cp.start()
        for k, (dev, pidx) in enumerate(peers):
            copy(0, k, blk(x_ref, pidx), stage.at[pidx], dev).wait_recv()
        total = stage[0]
        for j in range(1, N_DEV):
            total = total + stage[j]
        blk(o_ref, me)[...] = total
        gather = [copy(1, k, blk(o_ref, me), blk(o_ref, me), dev) for k, (dev, pidx) in enumerate(peers)]
        for cp in gather:
            cp.start()
        for k, (dev, pidx) in enumerate(peers):
            copy(1, k, blk(o_ref, pidx), blk(o_ref, pidx), dev).wait_recv()
        for cp in scatter + gather:
            cp.wait_send()

    return _pcall(body, name=name, out_shape=_sds(buf.shape, f32), in_specs=[VMEM], out_specs=VMEM,
                  scratch=[pltpu.VMEM((N_DEV, rb, LANES), f32), pltpu.SemaphoreType.DMA((2, N_DEV - 1)),
                           pltpu.SemaphoreType.DMA((2, N_DEV - 1))])(buf)


def _other_chips(x, y):
    return [((1 - x, y), 2 * (1 - x) + y), ((x, 1 - y), 2 * x + (1 - y)), ((1 - x, 1 - y), 2 * (1 - x) + (1 - y))]


HBM = pl.BlockSpec(memory_space=pltpu.HBM)
SEM = pl.BlockSpec(memory_space=pltpu.SEMAPHORE)
EFFECT = pltpu.SideEffectType.DATAFLOW_SIDE_EFFECTING


def _in_hbm(a):
    return pltpu.with_memory_space_constraint(a, pltpu.HBM)


def _ag_start(layers, after, name):
    flat = [t for lay in layers for t in lay]
    n, nl = len(flat), len(layers)

    def body(*refs):
        src = refs[:n]
        sems = refs[n + 1:n + 1 + 2 * nl]
        token = refs[-1]
        x, y, c = _place()
        s_me = 2 * x + y
        t = 0
        for i, lay in enumerate(layers):
            for k in range(len(lay)):
                for j, ((px, py), _) in enumerate(_other_chips(x, y)):
                    pltpu.make_async_remote_copy(src_ref=src[t].at[s_me, c], dst_ref=src[t].at[s_me, c],
                                                 send_sem=sems[2 * i].at[3 * k + j], recv_sem=sems[2 * i + 1].at[3 * k + j],
                                                 device_id=(px, py, c), device_id_type=MESH).start()
                t += 1
        token[...] = jnp.zeros_like(token)

    sem_shapes = [pltpu.SemaphoreType.DMA((3 * len(lay),)) for lay in layers for _ in range(2)]
    out_shape = sem_shapes + [pltpu.HBM(t.shape, t.dtype) for t in flat] + [_sds((8, LANES), f32)]
    outs = pl.pallas_call(
        body, name=name, out_shape=out_shape, in_specs=[HBM] * n + [ANY],
        out_specs=[SEM] * (2 * nl) + [HBM] * n + [VMEM], input_output_aliases={t: 2 * nl + t for t in range(n)},
        compiler_params=pltpu.CompilerParams(has_side_effects=EFFECT))(*[_in_hbm(t) for t in flat], after)
    sems = [(outs[2 * i], outs[2 * i + 1]) for i in range(nl)]
    thru, t = [], 2 * nl
    for lay in layers:
        thru.append(list(outs[t:t + len(lay)]))
        t += len(lay)
    return sems, thru, outs[-1]


def _ag_wait(inflight, sems, after, name):
    n = len(inflight)

    def body(*refs):
        src, ssem, rsem = refs[:n], refs[n], refs[n + 1]
        x, y, c = _place()
        s_me = 2 * x + y
        for k in range(n):
            for j, (_, s_p) in enumerate(_other_chips(x, y)):
                cp = pltpu.make_async_remote_copy(src_ref=src[k].at[s_me, c], dst_ref=src[k].at[s_p, c],
                                                  send_sem=ssem.at[3 * k + j], recv_sem=rsem.at[3 * k + j],
                                                  device_id=(x, y, c), device_id_type=MESH)
                cp.wait_send()
                cp.wait_recv()

    return pl.pallas_call(
        body, name=name, out_shape=[pltpu.HBM(t.shape, t.dtype) for t in inflight],
        in_specs=[HBM] * n + [SEM, SEM, ANY], out_specs=[HBM] * n, input_output_aliases={t: t for t in range(n)},
        compiler_params=pltpu.CompilerParams(has_side_effects=EFFECT))(*inflight, sems[0], sems[1], after)


def _ag_forward(arrived, name):
    n = len(arrived)

    def body(*refs):
        o = refs[n:2 * n]
        ssem, rsem = refs[2 * n:]
        x, y, c = _place()

        def copy(t, j, s, half, dev):
            return pltpu.make_async_remote_copy(src_ref=o[t].at[s, c], dst_ref=o[t].at[s, half], send_sem=ssem.at[t, j],
                                                recv_sem=rsem.at[t, j], device_id=dev, device_id_type=MESH)

        chips = _other_chips(x, y)
        sends = [copy(t, j, s_p, c, (x, y, 1 - c)) for t in range(n) for j, (_, s_p) in enumerate(chips)]
        for cp in sends:
            cp.start()
        for t in range(n):
            for j, (_, s_p) in enumerate(chips):
                copy(t, j, s_p, 1 - c, (x, y, c)).wait_recv()
        for cp in sends:
            cp.wait_send()

    return _pcall(body, name=name, out_shape=[_sds(p.shape, bf16) for p in arrived], in_specs=[ANY] * n,
                  out_specs=[ANY] * n, aliases={t: t for t in range(n)},
                  scratch=[pltpu.SemaphoreType.DMA((n, 3)), pltpu.SemaphoreType.DMA((n, 3))])(*arrived)


def _agf_start(arrived, name):
    n = len(arrived)

    def body(*refs):
        o = refs[:n]
        ssem, rsem, token = refs[n], refs[n + 1], refs[-1]
        x, y, c = _place()
        for t in range(n):
            for j, (_, s_p) in enumerate(_other_chips(x, y)):
                pltpu.make_async_remote_copy(src_ref=o[t].at[s_p, c], dst_ref=o[t].at[s_p, c],
                                             send_sem=ssem.at[3 * t + j], recv_sem=rsem.at[3 * t + j],
                                             device_id=(x, y, 1 - c), device_id_type=MESH).start()
        token[...] = jnp.zeros_like(token)

    out_shape = ([pltpu.SemaphoreType.DMA((3 * n,))] * 2 + [pltpu.HBM(a.shape, bf16) for a in arrived]
                 + [_sds((8, LANES), f32)])
    outs = pl.pallas_call(
        body, name=name, out_shape=out_shape, in_specs=[HBM] * n, out_specs=[SEM, SEM] + [HBM] * n + [VMEM],
        input_output_aliases={t: 2 + t for t in range(n)},
        compiler_params=pltpu.CompilerParams(has_side_effects=EFFECT))(*[_in_hbm(a) for a in arrived])
    return (outs[0], outs[1]), list(outs[2:2 + n]), outs[-1]


def _agf_wait(sems, inflight, after, name):
    n = len(inflight)

    def body(*refs):
        o, ssem, rsem = refs[:n], refs[n], refs[n + 1]
        x, y, c = _place()
        for t in range(n):
            for j, (_, s_p) in enumerate(_other_chips(x, y)):
                cp = pltpu.make_async_remote_copy(src_ref=o[t].at[s_p, c], dst_ref=o[t].at[s_p, 1 - c],
                                                  send_sem=ssem.at[3 * t + j], recv_sem=rsem.at[3 * t + j],
                                                  device_id=(x, y, c), device_id_type=MESH)
                cp.wait_send()
                cp.wait_recv()

    return pl.pallas_call(
        body, name=name, out_shape=[pltpu.HBM(a.shape, bf16) for a in inflight],
        in_specs=[HBM] * n + [SEM, SEM, ANY], out_specs=[HBM] * n, input_output_aliases={t: t for t in range(n)},
        compiler_params=pltpu.CompilerParams(has_side_effects=EFFECT))(*inflight, sems[0], sems[1], after)


def _rs_pair_start(grads, name):
    n = len(grads)

    def body(*refs):
        g, theirs = refs[:n], refs[n:2 * n]
        ssem, rsem, token = refs[2 * n], refs[2 * n + 1], refs[-1]
        x, y, c = _place()
        for t in range(n):
            pltpu.make_async_remote_copy(src_ref=g[t].at[:, 1 - c], dst_ref=theirs[t], send_sem=ssem.at[t],
                                         recv_sem=rsem.at[t], device_id=(x, y, 1 - c), device_id_type=MESH).start()
        token[...] = jnp.zeros_like(token)

    lands = [lax.empty((4,) + g.shape[2:], bf16) for g in grads]
    out_shape = ([pltpu.SemaphoreType.DMA((n,))] * 2 + [pltpu.HBM(g.shape, bf16) for g in grads]
                 + [pltpu.HBM(q.shape, bf16) for q in lands] + [_sds((8, LANES), f32)])
    outs = pl.pallas_call(
        body, name=name, out_shape=out_shape, in_specs=[HBM] * (2 * n), out_specs=[SEM, SEM] + [HBM] * (2 * n) + [VMEM],
        input_output_aliases={t: 2 + t for t in range(2 * n)},
        compiler_params=pltpu.CompilerParams(has_side_effects=EFFECT))(*[_in_hbm(a) for a in list(grads) + lands])
    return (outs[0], outs[1]), list(outs[2:2 + n]), list(outs[2 + n:2 + 2 * n]), outs[-1]


def _rs_pair_wait(sems, grads, lands, after, name):
    n = len(grads)

    def body(*refs):
        g, theirs = refs[:n], refs[n:2 * n]
        ssem, rsem = refs[2 * n], refs[2 * n + 1]
        x, y, c = _place()
        for t in range(n):
            cp = pltpu.make_async_remote_copy(src_ref=g[t].at[:, 1 - c], dst_ref=theirs[t], send_sem=ssem.at[t],
                                              recv_sem=rsem.at[t], device_id=(x, y, c), device_id_type=MESH)
            cp.wait_send()
            cp.wait_recv()

    outs = pl.pallas_call(
        body, name=name, out_shape=[pltpu.HBM(a.shape, bf16) for a in list(grads) + list(lands)],
        in_specs=[HBM] * (2 * n) + [SEM, SEM, ANY], out_specs=[HBM] * (2 * n),
        input_output_aliases={t: t for t in range(2 * n)},
        compiler_params=pltpu.CompilerParams(has_side_effects=EFFECT))(*grads, *lands, sems[0], sems[1], after)
    return list(outs[:n]), list(outs[n:])


def _rs_chip_start(pairs, name):
    n = len(pairs)

    def body(*refs):
        p, q = refs[:n], refs[n:2 * n]
        ssem, rsem, token = refs[2 * n], refs[2 * n + 1], refs[-1]
        x, y, c = _place()
        for t in range(n):
            for j, ((px, py), s_p) in enumerate(_other_chips(x, y)):
                pltpu.make_async_remote_copy(src_ref=p[t].at[s_p], dst_ref=q[t].at[j], send_sem=ssem.at[3 * t + j],
                                             recv_sem=rsem.at[3 * t + j], device_id=(px, py, c), device_id_type=MESH).start()
        token[...] = jnp.zeros_like(token)

    lands = [lax.empty((3,) + p.shape[1:], bf16) for p in pairs]
    out_shape = ([pltpu.SemaphoreType.DMA((3 * n,))] * 2 + [pltpu.HBM(p.shape, bf16) for p in pairs]
                 + [pltpu.HBM(q.shape, bf16) for q in lands] + [_sds((8, LANES), f32)])
    outs = pl.pallas_call(
        body, name=name, out_shape=out_shape, in_specs=[HBM] * (2 * n), out_specs=[SEM, SEM] + [HBM] * (2 * n) + [VMEM],
        input_output_aliases={t: 2 + t for t in range(2 * n)},
        compiler_params=pltpu.CompilerParams(has_side_effects=EFFECT))(*[_in_hbm(a) for a in list(pairs) + lands])
    return (outs[0], outs[1]), list(outs[2:2 + n]), list(outs[2 + n:2 + 2 * n]), outs[-1]


def _rs_chip_wait(sems, pairs, lands, after, name):
    n = len(pairs)

    def body(*refs):
        p, q = refs[:n], refs[n:2 * n]
        ssem, rsem = refs[2 * n], refs[2 * n + 1]
        x, y, c = _place()
        for t in range(n):
            for j, (_, s_p) in enumerate(_other_chips(x, y)):
                cp = pltpu.make_async_remote_copy(src_ref=p[t].at[s_p], dst_ref=q[t].at[j], send_sem=ssem.at[3 * t + j],
                                                  recv_sem=rsem.at[3 * t + j], device_id=(x, y, c), device_id_type=MESH)
                cp.wait_send()
                cp.wait_recv()

    outs = pl.pallas_call(
        body, name=name, out_shape=[pltpu.HBM(a.shape, bf16) for a in list(pairs) + list(lands)],
        in_specs=[HBM] * (2 * n) + [SEM, SEM, ANY], out_specs=[HBM] * (2 * n),
        input_output_aliases={t: t for t in range(2 * n)},
        compiler_params=pltpu.CompilerParams(has_side_effects=EFFECT))(*pairs, *lands, sems[0], sems[1], after)
    return list(outs[:n]), list(outs[n:])


def _rs_half_exchange(halves, name):
    n = len(halves)

    def body(*refs):
        o = refs[n:2 * n]
        ssem, rsem = refs[2 * n:]
        x, y, c = _place()

        def copy(t, half, dev):
            return pltpu.make_async_remote_copy(src_ref=o[t].at[c], dst_ref=o[t].at[half], send_sem=ssem.at[t],
                                                recv_sem=rsem.at[t], device_id=dev, device_id_type=MESH)

        sends = [copy(t, c, (x, y, 1 - c)) for t in range(n)]
        for cp in sends:
            cp.start()
        for t in range(n):
            copy(t, 1 - c, (x, y, c)).wait_recv()
        for cp in sends:
            cp.wait_send()

    return _pcall(body, name=name, out_shape=[_sds(h.shape, f32) for h in halves], in_specs=[ANY] * n,
                  out_specs=[ANY] * n, aliases={t: t for t in range(n)},
                  scratch=[pltpu.SemaphoreType.DMA((n,)), pltpu.SemaphoreType.DMA((n,))])(*halves)


def _row_spec(tm, cols):
    return pl.BlockSpec((tm, cols), lambda i: (i, 0))


def _vec_spec(cols, rows=1):
    return pl.BlockSpec((rows, cols), lambda i: (0, 0))


def _modulated_norm(xv, g, shift, scale):
    r = lax.rsqrt(jnp.mean(xv * xv, axis=-1, keepdims=True) + EPS)
    return (((xv * r) * g) * (1.0 + scale) + shift).astype(bf16)


def _hnorm(x, g, shift, scale):
    T, tm = x.shape[0], 256

    def body(x_ref, g_ref, sh_ref, sc_ref, h_ref):
        h_ref[...] = _modulated_norm(x_ref[...], g_ref[...], sh_ref[...], sc_ref[...])

    return _pcall(body, name="hnorm", out_shape=_sds((T, D), bf16), grid=(T // tm,),
                  in_specs=[_row_spec(tm, D), _vec_spec(D), _vec_spec(D), _vec_spec(D)],
                  out_specs=_row_spec(tm, D))(x, g, shift, scale)


def _out_proj(y2, wo, x, gate, nxt=None):
    T, tm = x.shape[0], 512

    def body(y_ref, w_ref, x_ref, g_ref, *rest):
        o = jnp.dot(y_ref[0], w_ref[0], preferred_element_type=f32)
        o = o + jnp.dot(y_ref[1], w_ref[1], preferred_element_type=f32)
        xo = x_ref[...] + g_ref[...] * o
        if nxt is None:
            xo_ref, o_ref = rest
        else:
            ng_ref, nsh_ref, nsc_ref, xo_ref, o_ref, h_ref = rest
            h_ref[...] = _modulated_norm(xo, ng_ref[...], nsh_ref[...], nsc_ref[...])
        o_ref[...] = o
        xo_ref[...] = xo

    extra = [] if nxt is None else list(nxt)
    n_out = 2 if nxt is None else 3
    return _pcall(body, name="out_proj", out_shape=[_sds((T, D), f32), _sds((T, D), f32), _sds((T, D), bf16)][:n_out],
                  grid=(T // tm,),
                  in_specs=[pl.BlockSpec((2, tm, D), lambda i: (0, i, 0)), pl.BlockSpec((2, D, D), lambda i: (0, 0, 0)),
                            _row_spec(tm, D), _vec_spec(D)] + [_vec_spec(D)] * len(extra),
                  out_specs=[_row_spec(tm, D)] * n_out, vmem_mb=48)(y2, wo, x, gate, *extra)


def _gate_bwd_tile(dx, o_ref, gate_ref, dob_ref, dgate_ref):
    dob_ref[...] = (dx * gate_ref[...]).astype(bf16)
    dgate_ref[...] += jnp.sum(dx * o_ref[...], axis=0, keepdims=True)


def _loss_bwd(x, target, g, o, gate):
    T, tm = x.shape[0], 256

    def body(x_ref, t_ref, g_ref, o_ref, gate_ref, dx_ref, loss_ref, dg_ref, dob_ref, dgate_ref):
        @pl.when(pl.program_id(0) == 0)
        def _():
            loss_ref[...] = jnp.zeros_like(loss_ref)
            dg_ref[...] = jnp.zeros_like(dg_ref)
            dgate_ref[...] = jnp.zeros_like(dgate_ref)

        xv, gv = x_ref[...], g_ref[...]
        r = lax.rsqrt(jnp.mean(xv * xv, axis=-1, keepdims=True) + EPS)
        xn = xv * r
        err = xn * gv - t_ref[...]
        dy = err * (1.0 / D)
        dxn = dy * gv
        dx = r * (dxn - xn * jnp.mean(dxn * xn, axis=-1, keepdims=True))
        dx_ref[...] = dx
        dg_ref[...] += jnp.sum(dy * xn, axis=0, keepdims=True)
        loss_ref[...] += (0.5 / D) * jnp.sum(jnp.sum(err * err, axis=1, keepdims=True), axis=0, keepdims=True)
        _gate_bwd_tile(dx, o_ref, gate_ref, dob_ref, dgate_ref)

    return _pcall(body, name="loss_bwd",
                  out_shape=[_sds((T, D), f32), _sds((1, 1), f32), _sds((1, D), f32), _sds((T, D), bf16), _sds((1, D), f32)],
                  grid=(T // tm,),
                  in_specs=[_row_spec(tm, D), _row_spec(tm, D), _vec_spec(D), _row_spec(tm, D), _vec_spec(D)],
                  out_specs=[_row_spec(tm, D), pl.BlockSpec((1, 1), lambda i: (0, 0)), _vec_spec(D), _row_spec(tm, D),
                             _vec_spec(D)])(x, target, g, o, gate)


def _norm_bwd(x, dh, gin, g, scale, below=None):
    T, tm = x.shape[0], 256

    def body(x_ref, dh_ref, gin_ref, g_ref, sc_ref, *rest):
        if below is None:
            dx_ref, st_ref = rest
        else:
            o_ref, gate_ref, dx_ref, st_ref, dob_ref, dgate_ref = rest

        @pl.when(pl.program_id(0) == 0)
        def _():
            st_ref[...] = jnp.zeros_like(st_ref)
            if below is not None:
                dgate_ref[...] = jnp.zeros_like(dgate_ref)

        xv, gv, dhv = x_ref[...], g_ref[...], dh_ref[...]
        r = lax.rsqrt(jnp.mean(xv * xv, axis=-1, keepdims=True) + EPS)
        xn = xv * r
        da = dhv * (1.0 + sc_ref[...])
        dxn = da * gv
        dx = gin_ref[...] + r * (dxn - xn * jnp.mean(dxn * xn, axis=-1, keepdims=True))
        dx_ref[...] = dx
        st_ref[0:1, :] += jnp.sum(dhv, axis=0, keepdims=True)
        st_ref[1:2, :] += jnp.sum(dhv * (xn * gv), axis=0, keepdims=True)
        st_ref[2:3, :] += jnp.sum(da * xn, axis=0, keepdims=True)
        if below is not None:
            _gate_bwd_tile(dx, o_ref, gate_ref, dob_ref, dgate_ref)

    out_shape = [_sds((T, D), f32), _sds((8, D), f32)]
    in_specs = [_row_spec(tm, D), _row_spec(tm, D), _row_spec(tm, D), _vec_spec(D), _vec_spec(D)]
    out_specs = [_row_spec(tm, D), _vec_spec(D, 8)]
    args = [x, dh, gin, g, scale]
    if below is not None:
        out_shape += [_sds((T, D), bf16), _sds((1, D), f32)]
        in_specs += [_row_spec(tm, D), _vec_spec(D)]
        out_specs += [_row_spec(tm, D), _vec_spec(D)]
        args += list(below)
    return _pcall(body, name="norm_bwd", out_shape=out_shape, grid=(T // tm,), in_specs=in_specs,
                  out_specs=out_specs)(*args)


def _cast_place(place, w, layer, after=None):
    _, rows, cols = w.shape
    tr = 256

    def body(place_ref, w_ref, *rest):
        rest[-1][...] = w_ref[...].astype(bf16)

    extra = [] if after is None else [after]
    return _pcall(body, name="cast_place", out_shape=_sds((4, rows, cols), bf16), grid=(rows // tr,), prefetch=1,
                  in_specs=[pl.BlockSpec((None, tr, cols), lambda i, pr: (layer, i, 0))] + [ANY] * len(extra),
                  out_specs=pl.BlockSpec((None, tr, cols), lambda i, pr: (pr[0], i, 0)))(place, w, *extra)


def _rs_add(place, grad, theirs):
    _, rows, cols = theirs.shape
    tr = min(rows, 512)
    spec = pl.BlockSpec((None, tr, cols), lambda s, i, pr: (s, i, 0))

    def body(place_ref, a_ref, b_ref, o_ref):
        o_ref[...] = (a_ref[...].astype(f32) + b_ref[...].astype(f32)).astype(bf16)

    return _pcall(body, name="rs_add", out_shape=_sds(theirs.shape, bf16), grid=(4, rows // tr), prefetch=1,
                  in_specs=[pl.BlockSpec((None, None, tr, cols), lambda s, i, pr: (s, pr[1], i, 0)), spec],
                  out_specs=spec)(place, grad, theirs)


def _rs_sum(place, pairs, slots):
    _, rows, cols = slots.shape
    tr = min(rows, 512)

    def body(place_ref, p_ref, q_ref, o_ref):
        o_ref[...] = ((p_ref[...].astype(f32) + q_ref[0].astype(f32)) + q_ref[1].astype(f32)) + q_ref[2].astype(f32)

    return _pcall(body, name="rs_sum", out_shape=_sds((2, rows, cols), f32), grid=(rows // tr,), prefetch=1,
                  in_specs=[pl.BlockSpec((None, tr, cols), lambda i, pr: (pr[0], i, 0)),
                            pl.BlockSpec((3, tr, cols), lambda i, pr: (0, i, 0))],
                  out_specs=pl.BlockSpec((None, tr, cols), lambda i, pr: (pr[1], i, 0)))(place, pairs, slots)


def _adamw_math(w, g, m, v):
    m = ADAM_B1 * m + (1.0 - ADAM_B1) * g
    v = ADAM_B2 * v + (1.0 - ADAM_B2) * jnp.square(g)
    m_hat = m / (1.0 - ADAM_B1 ** ADAM_STEP)
    v_hat = v / (1.0 - ADAM_B2 ** ADAM_STEP)
    delta = -ADAM_LR * (m_hat / (jnp.sqrt(v_hat) + ADAM_EPS) + ADAM_WD * w)
    return delta, m, v


def _adamw_layer(layer, w, g, m, v, so_far):
    _, rows, cols = w.shape
    tr = 256
    spec = pl.BlockSpec((None, tr, cols), lambda i: (layer, i, 0))

    def body(w_ref, g_ref, m_ref, v_ref, *rest):
        go_ref, d_ref, mo_ref, vo_ref = rest[-4:]
        g = g_ref[...]
        go_ref[...] = g
        d_ref[...], mo_ref[...], vo_ref[...] = _adamw_math(w_ref[...], g, m_ref[...], v_ref[...])

    args, in_specs, aliases = [w, g, m, v], [spec, pl.BlockSpec((tr, cols), lambda i: (i, 0)), spec, spec], None
    if so_far is not None:
        args += list(so_far)
        in_specs += [ANY] * 4
        aliases = {4 + k: k for k in range(4)}
    return _pcall(body, name="adamw", out_shape=[_sds(w.shape, f32)] * 4, grid=(rows // tr,), in_specs=in_specs,
                  out_specs=[spec] * 4, aliases=aliases, vmem_mb=48)(*args)


def _adamw_small(items):
    n = len(items)

    def body(*refs):
        ins, outs = refs[:4 * n], refs[4 * n:]
        for t in range(n):
            w_ref, g_ref, m_ref, v_ref = ins[4 * t:4 * t + 4]
            if len(g_ref.shape) == len(w_ref.shape) + 1:
                g = g_ref[0]
                for b in range(1, g_ref.shape[0]):
                    g = g + g_ref[b]
            else:
                g = g_ref[...]
            d, m, v = _adamw_math(w_ref[...], g, m_ref[...], v_ref[...])
            outs[4 * t][...], outs[4 * t + 1][...], outs[4 * t + 2][...], outs[4 * t + 3][...] = g, d, m, v

    out_shape = [_sds(w.shape, f32) for (w, _, _, _) in items for _ in range(4)]
    flat = [a for it in items for a in it]
    res = _pcall(body, name="adamw_small", out_shape=out_shape, in_specs=[VMEM] * (4 * n),
                 out_specs=[VMEM] * (4 * n))(*flat)
    return [tuple(res[4 * t:4 * t + 4]) for t in range(n)]


NN = ((1,), (0,))
NT = ((1,), (1,))
TN = ((0,), (0,))


def _mm(name, a, b, *, grid, a_spec, b_spec, out_shape, out_spec, dims, vmem_mb=48):
    def body(a_ref, b_ref, o_ref):
        r = lax.dot_general(a_ref[...], b_ref[...], (dims, ((), ())), preferred_element_type=f32)
        o_ref[...] = r.astype(o_ref.dtype)

    return _pcall(body, name=name, out_shape=out_shape, grid=grid, in_specs=[a_spec, b_spec], out_specs=out_spec,
                  vmem_mb=vmem_mb)(a, b)


def _whole(shape):
    return pl.BlockSpec(shape, lambda j: (0,) * len(shape))


def _split_spec(rows, tile, per_split):
    return pl.BlockSpec((None, rows, tile), lambda j: (j // per_split, 0, j % per_split))


class _Proj:
    def __init__(self, n, splits, tile):
        self.n, self.splits, self.tile = n, splits, tile
        self.steps = n // tile
        self.w_per = n // 4 // tile
        self.a_per = n // splits // tile
        assert self.w_per * tile * 4 == n and self.a_per * tile * splits == n

    def fwd(self, hb, wg):
        T = hb.shape[0]
        return _mm("proj_fwd", hb, wg, grid=(self.steps,), a_spec=_whole((T, D)),
                   b_spec=_split_spec(D, self.tile, self.w_per),
                   out_shape=_sds((self.splits, T, self.n // self.splits), bf16),
                   out_spec=_split_spec(T, self.tile, self.a_per), dims=NN)

    def dw(self, hb, dp):
        T = hb.shape[0]
        return _mm("proj_dw", hb, dp, grid=(self.steps,), a_spec=_whole((T, D)),
                   b_spec=_split_spec(T, self.tile, self.a_per), out_shape=_sds((4, D, self.n // 4), bf16),
                   out_spec=_split_spec(D, self.tile, self.w_per), dims=TN)

    def dh(self, dp, wg):
        T = dp.shape[1]
        sub, tile, w_per = DH_WIDE // self.tile, self.tile, self.w_per
        a_per = self.n // self.splits // DH_WIDE
        assert sub * tile == DH_WIDE and a_per * DH_WIDE * self.splits == self.n

        def w_tile(q):
            return pl.BlockSpec((None, D, tile), lambda k: ((sub * k + q) // w_per, 0, (sub * k + q) % w_per))

        def body(a_ref, *rest):
            o_ref = rest[sub]
            w = jnp.concatenate([rest[q][...] for q in range(sub)], axis=1)
            r = lax.dot_general(a_ref[...], w, (NT, ((), ())), preferred_element_type=f32)

            @pl.when(pl.program_id(0) == 0)
            def _():
                o_ref[...] = r

            @pl.when(pl.program_id(0) > 0)
            def _():
                o_ref[...] += r

        return _pcall(body, name="proj_dh", out_shape=_sds((T, D), f32), grid=(self.n // DH_WIDE,),
                      in_specs=[pl.BlockSpec((None, T, DH_WIDE), lambda k: (k // a_per, 0, k % a_per))]
                      + [w_tile(q) for q in range(sub)],
                      out_specs=_whole((T, D)), vmem_mb=48)(dp, *([wg] * sub))


EVEN_PROJ = _Proj(7 * D, 7, 256)
ODD_PROJ = _Proj(4 * D, 2, 512)


def _dy_mm(dob, wo):
    T = dob.shape[0]
    return _mm("out_dy", dob, wo, grid=(4,), a_spec=_whole((T, D)),
               b_spec=pl.BlockSpec((None, 512, D), lambda j: (j, 0, 0)), out_shape=_sds((2, T, D), f32),
               out_spec=_split_spec(T, 512, 2), dims=NT)


def _dwo_mm(y2, dob):
    T = dob.shape[0]
    return _mm("out_dw", y2, dob, grid=(4,), a_spec=_split_spec(T, 512, 2), b_spec=_whole((T, D)),
               out_shape=_sds((4, 512, D), bf16), out_spec=pl.BlockSpec((None, 512, D), lambda j: (j, 0, 0)), dims=TN)


def _head_spec(lead, T):
    return pl.BlockSpec((lead, T, HEAD), lambda h: (0, 0, h))


def _head_vec(rows):
    return pl.BlockSpec((rows, HEAD), lambda h: (0, h))


_HEAD_MAT = pl.BlockSpec((None, HEAD, HEAD), lambda h: (h, 0, 0))


def _causal():
    return lax.broadcasted_iota(jnp.int32, (HEAD, HEAD), 0) >= lax.broadcasted_iota(jnp.int32, (HEAD, HEAD), 1)


def _layernorm_head(v):
    mu = jnp.mean(v, axis=-1, keepdims=True)
    d = v - mu
    rstd = lax.rsqrt(jnp.mean(d * d, axis=-1, keepdims=True) + EPS)
    return d * rstd, rstd


def _even_fwd(p7, conv_w, ln_g, ln_b, sgu_w, sgu_bias):
    T, C = p7.shape[1], CHUNK_ROWS

    def body(p_ref, cw_ref, lg_ref, lb_ref, w_ref, b_ref, y_ref):
        w0, w1, w2 = cw_ref[0:1, :], cw_ref[1:2, :], cw_ref[2:3, :]
        wm = jnp.where(_causal(), w_ref[...], 0.0).astype(bf16)
        bias, lg, lb = b_ref[...], lg_ref[...], lb_ref[...]

        def step(i, halo):
            rows = pl.ds(pl.multiple_of(i * C, C), C)
            ah, ab, ac, az, u, v, zb = (p_ref[k, rows, :].astype(f32) for k in range(7))
            tt = ac * ah
            ext = jnp.concatenate([halo, tt], axis=0)
            cv = w2 * tt + w1 * pltpu.roll(ext, 1, 0)[HALO_CONV:] + w0 * pltpu.roll(ext, 2, 0)[HALO_CONV:]
            y_ref[0, rows, :] = (ab * cv * _silu(az)).astype(bf16)
            vhat, _ = _layernorm_head(v)
            vn = (vhat * lg + lb).astype(bf16)
            mix = jnp.concatenate([jnp.dot(wm, vn[k * HEAD:(k + 1) * HEAD], preferred_element_type=f32) + bias
                                   for k in range(C // HEAD)], axis=0)
            y_ref[1, rows, :] = (u * mix * _silu(zb)).astype(bf16)
            return tt[C - HALO_CONV:]

        lax.fori_loop(0, T // C, step, jnp.zeros((HALO_CONV, HEAD), f32))

    return _pcall(body, name="even_fwd", out_shape=_sds((2, T, D), bf16), grid=(NH,),
                  in_specs=[_head_spec(7, T), _head_vec(3), _head_vec(1), _head_vec(1), _HEAD_MAT, _HEAD_MAT],
                  out_specs=_head_spec(2, T), vmem_mb=32)(p7, conv_w, ln_g, ln_b, sgu_w, sgu_bias)


def _even_bwd(p7, dy2, conv_w, ln_g, ln_b, sgu_w, sgu_bias):
    T, C = p7.shape[1], CHUNK_ROWS
    n_chunks = T // C

    def body(p_ref, dy_ref, cw_ref, lg_ref, lb_ref, w_ref, b_ref,
             dp_ref, dcw_ref, dlg_ref, dlb_ref, dw_ref, dms_ref, dcv_s):
        w0, w1, w2 = cw_ref[0:1, :], cw_ref[1:2, :], cw_ref[2:3, :]
        tri = _causal()
        wm = jnp.where(tri, w_ref[...], 0.0).astype(bf16)
        bias, lg, lb = b_ref[...], lg_ref[...], lb_ref[...]
        dw_ref[...] = jnp.zeros_like(dw_ref)
        dms_ref[...] = jnp.zeros_like(dms_ref)

        def fwd_step(i, carry):
            halo, a0, a1, a2, alg, alb = carry
            rows = pl.ds(pl.multiple_of(i * C, C), C)
            ah, ab, ac, az = (p_ref[k, rows, :].astype(f32) for k in range(4))
            dya = dy_ref[0, rows, :]
            tt = ac * ah
            ext = jnp.concatenate([halo, tt], axis=0)
            t1, t2 = pltpu.roll(ext, 1, 0)[HALO_CONV:], pltpu.roll(ext, 2, 0)[HALO_CONV:]
            cv = w2 * tt + w1 * t1 + w0 * t2
            sa, dsa = _silu_and_grad(az)
            g1 = dya * sa
            dp_ref[1, rows, :] = (g1 * cv).astype(bf16)
            dp_ref[3, rows, :] = (dya * ab * cv * dsa).astype(bf16)
            dcv = g1 * ab
            dcv_s[rows, :] = dcv
            a2 = a2 + jnp.sum(dcv * tt, axis=0, keepdims=True)
            a1 = a1 + jnp.sum(dcv * t1, axis=0, keepdims=True)
            a0 = a0 + jnp.sum(dcv * t2, axis=0, keepdims=True)

            u, zb, dyb = p_ref[4, rows, :].astype(f32), p_ref[6, rows, :].astype(f32), dy_ref[1, rows, :]
            vhat, rstd = _layernorm_head(p_ref[5, rows, :].astype(f32))
            vn = (vhat * lg + lb).astype(bf16)
            sb, dsb = _silu_and_grad(zb)
            mix = jnp.concatenate([jnp.dot(wm, vn[k * HEAD:(k + 1) * HEAD], preferred_element_type=f32) + bias
                                   for k in range(C // HEAD)], axis=0)
            dp_ref[4, rows, :] = (dyb * mix * sb).astype(bf16)
            dp_ref[6, rows, :] = (dyb * u * mix * dsb).astype(bf16)
            dmix = dyb * u * sb
            dvn_parts = []
            for k in range(C // HEAD):
                dm = dmix[k * HEAD:(k + 1) * HEAD]
                dmb = dm.astype(bf16)
                dvn_parts.append(lax.dot_general(wm, dmb, (TN, ((), ())), preferred_element_type=f32))
                dw_ref[...] += lax.dot_general(dmb, vn[k * HEAD:(k + 1) * HEAD], (NT, ((), ())),
                                               preferred_element_type=f32)
                dms_ref[...] += dm
            dvn = jnp.concatenate(dvn_parts, axis=0)
            alg = alg + jnp.sum(dvn * vhat, axis=0, keepdims=True)
            alb = alb + jnp.sum(dvn, axis=0, keepdims=True)
            dvh = dvn * lg
            dv = rstd * (dvh - jnp.mean(dvh, axis=-1, keepdims=True)
                         - vhat * jnp.mean(dvh * vhat, axis=-1, keepdims=True))
            dp_ref[5, rows, :] = dv.astype(bf16)
            return tt[C - HALO_CONV:], a0, a1, a2, alg, alb

        zrow = jnp.zeros((1, HEAD), f32)
        _, a0, a1, a2, alg, alb = lax.fori_loop(
            0, n_chunks, fwd_step, (jnp.zeros((HALO_CONV, HEAD), f32), zrow, zrow, zrow, zrow, zrow))
        dcw_ref[0:1, :], dcw_ref[1:2, :], dcw_ref[2:3, :] = a0, a1, a2
        dlg_ref[...], dlb_ref[...] = alg, alb
        dw_ref[...] = jnp.where(tri, dw_ref[...], 0.0)

        def bwd_step(k, halo):
            rows = pl.ds(pl.multiple_of((n_chunks - 1 - k) * C, C), C)
            dcv = dcv_s[rows, :]
            ext = jnp.concatenate([dcv, halo], axis=0)
            n1 = pltpu.roll(ext, C + HALO_CONV - 1, 0)[:C]
            n2 = pltpu.roll(ext, C + HALO_CONV - 2, 0)[:C]
            dtt = w2 * dcv + w1 * n1 + w0 * n2
            dp_ref[2, rows, :] = (dtt * p_ref[0, rows, :].astype(f32)).astype(bf16)
            dp_ref[0, rows, :] = (dtt * p_ref[2, rows, :].astype(f32)).astype(bf16)
            return dcv[:HALO_CONV]

        lax.fori_loop(0, n_chunks, bwd_step, jnp.zeros((HALO_CONV, HEAD), f32))

    out_shape = [_sds((7, T, D), bf16), _sds((3, D), f32), _sds((1, D), f32), _sds((1, D), f32),
                 _sds((NH, HEAD, HEAD), f32), _sds((NH, HEAD, HEAD), f32)]
    return _pcall(body, name="even_bwd", out_shape=out_shape, grid=(NH,),
                  in_specs=[_head_spec(7, T), _head_spec(2, T), _head_vec(3), _head_vec(1), _head_vec(1),
                            _HEAD_MAT, _HEAD_MAT],
                  out_specs=[_head_spec(7, T), _head_vec(3), _head_vec(1), _head_vec(1), _HEAD_MAT, _HEAD_MAT],
                  scratch=[pltpu.VMEM((T, HEAD), f32)], vmem_mb=48)(p7, dy2, conv_w, ln_g, ln_b, sgu_w, sgu_bias)


def _window_sum(ext, win, towards_past):
    n, k, s = ext.shape[0], 1, ext
    while k < win:
        s = s + pltpu.roll(s, k if towards_past else n - k, 0)
        k *= 2
    return s


def _pool_count(i, C, win):
    t = i * C + lax.broadcasted_iota(jnp.int32, (C, 1), 0)
    cnt = jnp.minimum(t + 1, win).astype(f32)
    return cnt, 1.0 / cnt


def _group_specs(T):
    p_spec = pl.BlockSpec((None, T, GC), lambda g: (0, 0, g))
    z_spec = pl.BlockSpec((None, T, GC), lambda g: (1, 0, g))
    pw_spec = pl.BlockSpec((4, GC // 4, GC), lambda g: (0, g, 0))
    ps_spec = pl.BlockSpec((1, GC), lambda g: (0, g))
    y_spec = pl.BlockSpec((None, T, GC), lambda g: (g // 2, 0, g % 2))
    return p_spec, z_spec, pw_spec, ps_spec, y_spec


def _odd_fwd(p2, pool_wg, pool_scale):
    T, C = p2.shape[1], CHUNK_ROWS
    p_spec, z_spec, pw_spec, ps_spec, y_spec = _group_specs(T)

    def body(p_ref, z_ref, pw_ref, ps_ref, y_ref):
        pw, ps = pw_ref[...].reshape(GC, GC), ps_ref[...]

        def run(win):
            def step(i, halo):
                rows = pl.ds(pl.multiple_of(i * C, C), C)
                p = p_ref[rows, :].astype(f32)
                s = _window_sum(jnp.concatenate([halo, p], axis=0), win, True)[HALO_POOL:]
                pooled = s * _pool_count(i, C, win)[1] - p
                ypre = jnp.dot(pooled.astype(bf16), pw, preferred_element_type=f32)
                y_ref[rows, :] = (ypre * ps * _silu(z_ref[rows, :].astype(f32))).astype(bf16)
                return p[C - HALO_POOL:]

            lax.fori_loop(0, T // C, step, jnp.zeros((HALO_POOL, GC), f32))

        for gi, win in enumerate(WINDOWS):
            pl.when(pl.program_id(0) == gi)(functools.partial(run, win))

    return _pcall(body, name="odd_fwd", out_shape=_sds((2, T, D), bf16), grid=(len(WINDOWS),),
                  in_specs=[p_spec, z_spec, pw_spec, ps_spec], out_specs=y_spec, vmem_mb=40)(p2, p2, pool_wg, pool_scale)


def _odd_bwd(p2, dy2, pool_wg, pool_scale):
    T, C = p2.shape[1], CHUNK_ROWS
    n_chunks = T // C
    p_spec, z_spec, pw_spec, ps_spec, y_spec = _group_specs(T)

    def body(p_ref, z_ref, dy_ref, pw_ref, ps_ref, dp_ref, dpw_ref, dps_ref, q_s, acc_s):
        pw, ps = pw_ref[...].reshape(GC, GC), ps_ref[...]

        def run(win):
            acc_s[...] = jnp.zeros_like(acc_s)

            def fwd_step(i, carry):
                halo, aps = carry
                rows = pl.ds(pl.multiple_of(i * C, C), C)
                p, z, dy = p_ref[rows, :].astype(f32), z_ref[rows, :].astype(f32), dy_ref[rows, :]
                _, inv_cnt = _pool_count(i, C, win)
                s = _window_sum(jnp.concatenate([halo, p], axis=0), win, True)[HALO_POOL:]
                pb = (s * inv_cnt - p).astype(bf16)
                ypre = jnp.dot(pb, pw, preferred_element_type=f32)
                sz, dsz = _silu_and_grad(z)
                aps = aps + jnp.sum(dy * ypre * sz, axis=0, keepdims=True)
                dp_ref[1, rows, :] = (dy * ypre * ps * dsz).astype(bf16)
                dyp = (dy * ps * sz).astype(bf16)
                acc_s[...] += lax.dot_general(pb, dyp, (TN, ((), ())), preferred_element_type=f32)
                dpool = lax.dot_general(dyp, pw, (NT, ((), ())), preferred_element_type=f32)
                q_s[rows, :] = dpool * inv_cnt
                return p[C - HALO_POOL:], aps

            _, aps = lax.fori_loop(0, n_chunks, fwd_step, (jnp.zeros((HALO_POOL, GC), f32), jnp.zeros((1, GC), f32)))
            dps_ref[...] = aps
            dpw_ref[...] = acc_s[...].reshape(4, GC // 4, GC).astype(bf16)

            def bwd_step(k, halo):
                i = n_chunks - 1 - k
                rows = pl.ds(pl.multiple_of(i * C, C), C)
                q = q_s[rows, :]
                s = _window_sum(jnp.concatenate([q, halo], axis=0), win, False)[:C]
                dp_ref[0, rows, :] = (s - q * _pool_count(i, C, win)[0]).astype(bf16)
                return q[:HALO_POOL]

            lax.fori_loop(0, n_chunks, bwd_step, jnp.zeros((HALO_POOL, GC), f32))

        for gi, win in enumerate(WINDOWS):
            pl.when(pl.program_id(0) == gi)(functools.partial(run, win))

    out_shape = [_sds((2, T, 2 * D), bf16), _sds((4, GC, GC), bf16), _sds((1, 2 * D), f32)]
    return _pcall(body, name="odd_bwd", out_shape=out_shape, grid=(len(WINDOWS),),
                  in_specs=[p_spec, z_spec, y_spec, pw_spec, ps_spec],
                  out_specs=[pl.BlockSpec((2, T, GC), lambda g: (0, 0, g)), pw_spec, ps_spec],
                  scratch=[pltpu.VMEM((T, GC), f32), pltpu.VMEM((GC, GC), f32)], vmem_mb=52)(
                      p2, p2, dy2, pool_wg, pool_scale)


def _ada_fwd(c_all, ada_w):
    cols = ada_w.shape[2]

    def body(c_ref, w_ref, o_ref):
        o_ref[...] = jnp.dot(_silu(c_ref[...]), w_ref[...], preferred_element_type=f32,
                             precision=lax.Precision.HIGHEST)

    return _pcall(body, name="ada_fwd", out_shape=_sds((4, N_DEV, cols), f32), grid=(4,),
                  in_specs=[pl.BlockSpec((N_DEV, D), lambda i: (0, 0)), pl.BlockSpec((None, D, cols), lambda i: (i, 0, 0))],
                  out_specs=pl.BlockSpec((None, N_DEV, cols), lambda i: (i, 0, 0)))(c_all, ada_w)


def _ada_bwd(c_all_t, dmod, w, m, v):
    cols, tr = w.shape[2], 256
    spec = pl.BlockSpec((None, tr, cols), lambda l, i: (l, i, 0))

    def body(c_ref, dm_ref, w_ref, m_ref, v_ref, g_ref, d_ref, mo_ref, vo_ref):
        sc = _silu(c_ref[...])
        g = sc[:, 0:1] * dm_ref[0:1, :]
        for b in range(1, N_DEV):
            g = g + sc[:, b:b + 1] * dm_ref[b:b + 1, :]
        g_ref[...] = g
        d_ref[...], mo_ref[...], vo_ref[...] = _adamw_math(w_ref[...], g, m_ref[...], v_ref[...])

    return _pcall(body, name="ada_bwd", out_shape=[_sds(w.shape, f32)] * 4, grid=(4, D // tr),
                  in_specs=[pl.BlockSpec((tr, N_DEV), lambda l, i: (i, 0)),
                            pl.BlockSpec((None, N_DEV, cols), lambda l, i: (l, 0, 0)), spec, spec, spec],
                  out_specs=[spec] * 4)(c_all_t, dmod, w, m, v)


def _layer_fwd(even, x, hb, gate, w, nxt, before_out=None):
    if even:
        w_in, w_out, conv_w, ln_g, ln_b, sgu_w, sgu_b = w
        bias = jnp.broadcast_to(sgu_b[:, :, None], (NH, HEAD, HEAD))
        p = EVEN_PROJ.fwd(hb, w_in)
        y2 = _even_fwd(p, conv_w, ln_g, ln_b, sgu_w, bias)
    else:
        w_in, pool_w, w_out, pool_scale = w
        p = ODD_PROJ.fwd(hb, w_in)
        y2 = _odd_fwd(p, pool_w, pool_scale)
    if before_out is not None:
        late_w_out, tok = before_out(y2)
        if late_w_out is not None:
            w_out = late_w_out
            w = (w_in, w_out) + tuple(w[2:]) if even else (w_in, pool_w, w_out, pool_scale)
        if tok is not None:
            gate = gate + tok[0:1, 0:1]
    outs = _out_proj(y2, w_out.reshape(2, D, D), x, gate, nxt)
    return outs[0], (None if nxt is None else outs[2]), (x, hb, p, y2, outs[1]), w


def _layer_bwd(even, gin, dob, dgate, saved, scale, g, w, below=None, send=None):
    x_in, hb, p, y2, o = saved
    if even:
        w_in, w_out, conv_w, ln_g, ln_b, sgu_w, sgu_b = w
        bias = jnp.broadcast_to(sgu_b[:, :, None], (NH, HEAD, HEAD))
        dy2 = _dy_mm(dob, w_out)
        dp, dconv, dlg, dlb, dsw, dms = _even_bwd(p, dy2, conv_w, ln_g, ln_b, sgu_w, bias)
        proj = EVEN_PROJ
        small = dict(conv_w=dconv, ln_g=dlg, ln_b=dlb, sgu_w=dsw, sgu_b=jnp.sum(dms, axis=-1))
        big = [proj.dw(hb, dp), _dwo_mm(y2, dob)]
    else:
        w_in, pool_w, w_out, pool_scale = w
        dy2 = _dy_mm(dob, w_out)
        dp, dpw, dps = _odd_bwd(p, dy2, pool_w, pool_scale)
        proj = ODD_PROJ
        small = dict(pool_scale=dps)
        big = [proj.dw(hb, dp), dpw, _dwo_mm(y2, dob)]
    if send is not None:
        big, tok = send(big)
        scale = scale + tok[0:1, 0:1]
    dh = proj.dh(dp, w_in)
    res = _norm_bwd(x_in, dh, gin, g, scale, below)
    stats = res[1]
    return (res[0], (None if below is None else (res[2], res[3])), big, small,
            jnp.concatenate([stats[0:2], dgate], axis=0), stats[2:3])


def _pack_rows(parts):
    rows = [p.reshape(-1, LANES) for p in parts]
    total = sum(r.shape[0] for r in rows)
    padded = -(-total // (8 * N_DEV)) * (8 * N_DEV)
    if padded > total:
        rows.append(jnp.zeros((padded - total, LANES), f32))
    return jnp.concatenate(rows, axis=0)


def _unpack_rows(buf, shapes):
    out, r = [], 0
    for shp in shapes:
        n = 1
        for d in shp:
            n *= d
        out.append(buf[r:r + n // LANES].reshape(shp))
        r += n // LANES
    return out


def kernel(x, c, norm_g, ada_w, ada_b, ab_w_in, ab_conv_w, ab_ln_g, ab_ln_b, ab_sgu_w, ab_sgu_b, ab_w_out, c_w_in, c_pool_w, c_pool_scale, c_w_out, final_g, loss_target, m_norm_g, m_ada_w, m_ada_b, m_ab_w_in, m_ab_conv_w, m_ab_ln_g, m_ab_ln_b, m_ab_sgu_w, m_ab_sgu_b, m_ab_w_out, m_c_w_in, m_c_pool_w, m_c_pool_scale, m_c_w_out, m_final_g, v_norm_g, v_ada_w, v_ada_b, v_ab_w_in, v_ab_conv_w, v_ab_ln_g, v_ab_ln_b, v_ab_sgu_w, v_ab_sgu_b, v_ab_w_out, v_c_w_in, v_c_pool_w, v_c_pool_scale, v_c_w_out, v_final_g):
    ix, iy, ic = _place()
    chip, dev = 2 * ix + iy, 4 * ix + 2 * iy + ic
    n_even, n_odd = ab_w_in.shape[0], c_w_in.shape[0]
    depth = n_even + n_odd
    acols = ada_w.shape[2]

    place = jnp.stack([chip, ic]).astype(jnp.int32)
    even_names, odd_names = ["ab_w_in", "ab_w_out"], ["c_w_in", "c_pool_w", "c_w_out"]
    params = {"ab_w_in": (ab_w_in, m_ab_w_in, v_ab_w_in), "ab_w_out": (ab_w_out, m_ab_w_out, v_ab_w_out),
              "c_w_in": (c_w_in, m_c_w_in, v_c_w_in), "c_w_out": (c_w_out, m_c_w_out, v_c_w_out),
              "c_pool_w": tuple(a.reshape(n_odd, GC, GC) for a in (c_pool_w, m_c_pool_w, v_c_pool_w))}

    c_all = _gather8(c, "gather_c").reshape(N_DEV, D)
    modp = _ada_fwd(c_all, ada_w)
    modg = _gather8(modp, "gather_mod")
    mod_rows = lax.dynamic_index_in_dim(modg[0::2], dev, axis=2, keepdims=False)
    mod = jnp.transpose(mod_rows, (1, 0, 2)).reshape(depth, 3 * D) + ada_b
    mods = [(mod[i:i + 1, 0:D], mod[i:i + 1, D:2 * D], mod[i:i + 1, 2 * D:3 * D]) for i in range(depth)]

    def shard_cols(a, width):
        return lax.dynamic_slice_in_dim(a, chip * width, width, axis=a.ndim - 1)

    small_sharded = jnp.concatenate([ab_conv_w.reshape(1, -1), c_pool_scale.reshape(1, -1)], axis=1)
    small_all = _gather8(small_sharded, "gather_small")[0::2, 0]
    n_conv = ab_conv_w.size
    conv_all = small_all[:, :n_conv].reshape(4, n_even, 3, D // 4)
    conv_full = jnp.transpose(conv_all, (1, 2, 0, 3)).reshape(n_even, 3, D)
    scale_all = small_all[:, n_conv:].reshape(4, n_odd, 2 * D // 4)
    scale_full = jnp.transpose(scale_all, (1, 0, 2)).reshape(n_odd, 2 * D)

    def placed(nm, layer, after=None):
        w = params[nm][0]
        return _cast_place(place, w, layer, after).reshape(4, 2, w.shape[1] // 2, w.shape[2])

    def whole(arrays):
        return [g.reshape(4, 2 * g.shape[2], g.shape[3]) for g in arrays]

    gathers_done = mod[0:1, 0:LANES] + scale_full[0:1, 0:LANES]
    sems_a, in_a, tok = _ag_start([[placed(even_names[0], 0)]], gathers_done, "ag_start_0a")
    sems_b, in_b, tok = _ag_start([[placed(even_names[1], 0, tok)]], tok, "ag_start_0b")
    rest = [[placed(nm, i // 2, tok) for nm in (even_names if i % 2 == 0 else odd_names)] for i in range(1, depth)]
    sems_r, in_r, tok = _ag_start(rest, tok, "ag_start_rest")

    x_cur, saved, weights, handoff = x[0], [], [], {}
    hb = _hnorm(x_cur, norm_g[0:1], mods[0][0] + tok[0:1, 0:1], mods[0][1])
    for i in range(depth):
        j = i // 2
        if i == 0:
            full = whole(_ag_forward(_ag_wait(in_a[0], sems_a[0], hb, "ag_wait_0a"), "ag_forward")) + [None]
        else:
            full = whole(_agf_wait(*handoff.pop(i), x_cur, f"agf_wait_{i}"))
        if i % 2 == 0:
            w = (full[0], full[1], conv_full[j], ab_ln_g[j:j + 1], ab_ln_b[j:j + 1], ab_sgu_w[j], ab_sgu_b[j])
        else:
            w = (full[0], full[1], full[2], scale_full[j:j + 1])

        def before_out(y2, i=i):
            w_out, tok = None, None
            if i == 0:
                w_out = whole(_ag_forward(_ag_wait(in_b[0], sems_b[0], y2, "ag_wait_0b"), "ag_forward"))[0]
            if i + 1 < depth:
                arrived = _ag_wait(in_r[i], sems_r[i], y2, f"ag_wait_{i + 1}")
                sems_f, inflight, tok = _agf_start(arrived, f"agf_start_{i + 1}")
                handoff[i + 1] = (sems_f, inflight)
            return w_out, tok

        nxt = (norm_g[i + 1:i + 2], mods[i + 1][0], mods[i + 1][1]) if i + 1 < depth else None
        x_cur, hb, sv, w = _layer_fwd(i % 2 == 0, x_cur, hb, mods[i][2], w, nxt, before_out)
        weights.append(w)
        saved.append(sv)
    gin, loss, dfinal_g, dob, dgate = _loss_bwd(x_cur, loss_target[0], final_g.reshape(1, D), saved[-1][4],
                                                mods[-1][2])

    stacked = {}

    def finish(i, sems, pairs, lands, after):
        pairs, slots = _rs_chip_wait(sems, pairs, lands, after, f"rs_chip_wait_{i}")
        halves = [_rs_sum(place, p, q) for p, q in zip(pairs, slots)]
        for nm, g in zip(even_names if i % 2 == 0 else odd_names, _rs_half_exchange(halves, "rs_half_exchange")):
            w, m, v = params[nm]
            stacked[nm] = _adamw_layer(i // 2, w, g.reshape(w.shape[1], w.shape[2]), m, v, stacked.get(nm))

    small_g, dmod, dnorm_g, pending, tok = [None] * depth, [None] * depth, [None] * depth, None, None
    for i in reversed(range(depth)):
        w = weights[i]
        if tok is not None:
            w = w[:2] + (w[2] + tok[0:1, 0:1],) + w[3:] if i % 2 == 0 else w[:3] + (w[3] + tok[0:1, 0:1],)
        below = (saved[i - 1][4], mods[i - 1][2]) if i > 0 else None

        def send(big_g, i=i):
            big_g = [g.reshape(4, 2, g.shape[1] // 2, g.shape[2]) for g in big_g]
            sems, big_g, lands, tok = _rs_pair_start(big_g, f"rs_pair_start_{i}")
            return (sems, big_g, lands), tok

        gin, gate_bwd, sent, small_g[i], dmod[i], dnorm_g[i] = _layer_bwd(
            i % 2 == 0, gin, dob, dgate, saved[i], mods[i][1], norm_g[i:i + 1], w, below, send)
        if below is not None:
            dob, dgate = gate_bwd
        big_g, theirs = _rs_pair_wait(*sent, gin, f"rs_pair_wait_{i}")
        pairs = [_rs_add(place, a, b) for a, b in zip(big_g, theirs)]
        sems, pairs, lands, tok = _rs_chip_start(pairs, f"rs_chip_start_{i}")
        if pending is not None:
            finish(*pending, tok)
        pending = (i, sems, pairs, lands)
    grad_x = gin
    dmod, dnorm_g = jnp.stack(dmod), jnp.concatenate(dnorm_g, axis=0)

    small_parts = [dnorm_g + tok[0:1, 0:1], dfinal_g,
                   jnp.stack([small_g[2 * j]["conv_w"] for j in range(n_even)]),
                   jnp.concatenate([small_g[2 * j]["ln_g"] for j in range(n_even)], axis=0),
                   jnp.concatenate([small_g[2 * j]["ln_b"] for j in range(n_even)], axis=0),
                   jnp.stack([small_g[2 * j]["sgu_w"] for j in range(n_even)]),
                   jnp.stack([small_g[2 * j]["sgu_b"] for j in range(n_even)]),
                   jnp.concatenate([small_g[2 * j + 1]["pool_scale"] for j in range(n_odd)], axis=0),
                   jnp.pad(loss, ((0, 7), (0, LANES - 1)))]
    small_shapes = [p.shape for p in small_parts]
    reduced = _allreduce8(_pack_rows(small_parts), "allreduce_small")
    (g_norm_g, g_final_g, g_conv_full, g_ln_g, g_ln_b, g_sgu_w, g_sgu_b, g_scale_full,
     loss_row) = _unpack_rows(reduced, small_shapes)
    loss = loss_row[0, 0]
    g_conv = shard_cols(g_conv_full, D // 4)
    g_scale = shard_cols(g_scale_full, 2 * D // 4)
    dmod_all = _gather8(dmod.reshape(depth * 3 * D // LANES, LANES), "gather_dmod").reshape(N_DEV, depth, 3 * D)

    def two_d(a):
        return a.reshape(-1, a.shape[-1])

    small = [(norm_g, g_norm_g, m_norm_g, v_norm_g),
             (ada_b, dmod_all, m_ada_b, v_ada_b),
             (two_d(ab_conv_w), two_d(g_conv), two_d(m_ab_conv_w), two_d(v_ab_conv_w)),
             (ab_ln_g, g_ln_g, m_ab_ln_g, v_ab_ln_g),
             (ab_ln_b, g_ln_b, m_ab_ln_b, v_ab_ln_b),
             (two_d(ab_sgu_w), two_d(g_sgu_w), two_d(m_ab_sgu_w), two_d(v_ab_sgu_w)),
             (two_d(ab_sgu_b), two_d(g_sgu_b), two_d(m_ab_sgu_b), two_d(v_ab_sgu_b)),
             (c_pool_scale, g_scale, m_c_pool_scale, v_c_pool_scale),
             (final_g.reshape(1, D), g_final_g, m_final_g.reshape(1, D), v_final_g.reshape(1, D))]
    small_res = _adamw_small(small)
    small_shapes_out = [norm_g.shape, ada_b.shape, ab_conv_w.shape, ab_ln_g.shape, ab_ln_b.shape, ab_sgu_w.shape,
                        ab_sgu_b.shape, c_pool_scale.shape, final_g.shape]
    (r_norm_g, r_ada_b, r_conv, r_ln_g, r_ln_b, r_sgu_w, r_sgu_b, r_scale, r_final_g) = [
        tuple(a.reshape(shp) for a in res) for res, shp in zip(small_res, small_shapes_out)]

    dmod_cols = jnp.transpose(shard_cols(dmod_all, acols), (1, 0, 2))
    r_ada_w = _ada_bwd(c_all.T, dmod_cols, ada_w, m_ada_w, v_ada_w)

    finish(*pending, r_ada_w[1])
    r_ab_w_in, r_ab_w_out, r_c_w_in, r_c_w_out = (stacked[nm] for nm in ("ab_w_in", "ab_w_out", "c_w_in", "c_w_out"))
    r_c_pool_w = tuple(a.reshape(c_pool_w.shape) for a in stacked["c_pool_w"])

    order = [r_norm_g, r_ada_w, r_ada_b, r_ab_w_in, r_conv, r_ln_g, r_ln_b, r_sgu_w, r_sgu_b, r_ab_w_out,
             r_c_w_in, r_c_pool_w, r_scale, r_c_w_out, r_final_g]
    outs = [loss, grad_x[None]]
    for field in range(4):
        outs += [r[field] for r in order]
    return tuple(outs)
```

```python
import functools

import jax
import jax.numpy as jnp
from jax import lax
from jax.experimental import pallas as pl
from jax.experimental.pallas import tpu as pltpu

f32, bf16 = jnp.float32, jnp.bfloat16

D = 1024
HEAD = 128
NH = 8
WINDOWS = (2, 4, 8, 16)
GC = 512
EPS = 1e-6
HALO_CONV = 8
HALO_POOL = 16
CHUNK_ROWS = 512
DH_WIDE = 1024
FWD_TILES = 2
N_DEV = 8
LANES = 128

ADAM_LR, ADAM_B1, ADAM_B2, ADAM_EPS, ADAM_WD, ADAM_STEP = 0.001, 0.9, 0.999, 1e-08, 0.01, 10

MESH = pl.DeviceIdType.MESH
ANY = pl.BlockSpec(memory_space=pl.ANY)
VMEM = pl.BlockSpec(memory_space=pltpu.VMEM)
MIB = 2 ** 20


def _pcall(body, *, name, out_shape, grid=None, in_specs=None, out_specs=None, scratch=(), vmem_mb=None,
           aliases=None, prefetch=0):
    kw = {}
    if prefetch:
        kw["grid_spec"] = pltpu.PrefetchScalarGridSpec(num_scalar_prefetch=prefetch, grid=grid, in_specs=in_specs,
                                                       out_specs=out_specs, scratch_shapes=list(scratch))
    else:
        if grid is not None:
            kw["grid"] = grid
        if in_specs is not None:
            kw["in_specs"] = in_specs
        if out_specs is not None:
            kw["out_specs"] = out_specs
        if scratch:
            kw["scratch_shapes"] = list(scratch)
    if aliases:
        kw["input_output_aliases"] = aliases
    params = pltpu.CompilerParams(vmem_limit_bytes=None if vmem_mb is None else vmem_mb * MIB)
    return pl.pallas_call(body, name=name, out_shape=out_shape, compiler_params=params, **kw)


def _sds(shape, dtype):
    return jax.ShapeDtypeStruct(tuple(shape), dtype)


def _sigmoid(z):
    return pl.reciprocal(1.0 + jnp.exp(-z), approx=True)


def _silu(z):
    return z * _sigmoid(z)


def _silu_and_grad(z):
    s = _sigmoid(z)
    return z * s, s * (1.0 + z * (1.0 - s))


def _place():
    return lax.axis_index("x"), lax.axis_index("y"), lax.axis_index("c")


def _gather8(blk, name):
    def body(x_ref, o_ref, ssem, rsem):
        x, y, c = _place()
        me = 4 * x + 2 * y + c
        o_ref[me] = x_ref[...]
        sends = []
        for k in range(1, N_DEV):
            px = 1 - x if k & 4 else x
            py = 1 - y if k & 2 else y
            pc = 1 - c if k & 1 else c
            cp = pltpu.make_async_remote_copy(src_ref=x_ref, dst_ref=o_ref.at[me], send_sem=ssem.at[k - 1],
                                              recv_sem=rsem.at[k - 1], device_id=(px, py, pc), device_id_type=MESH)
            cp.start()
            sends.append((cp, 4 * px + 2 * py + pc))
        for k, (cp, peer) in enumerate(sends):
            pltpu.make_async_remote_copy(src_ref=x_ref, dst_ref=o_ref.at[peer], send_sem=ssem.at[k],
                                         recv_sem=rsem.at[k], device_id=(x, y, c), device_id_type=MESH).wait_recv()
        for cp, _ in sends:
            cp.wait_send()

    return _pcall(body, name=name, out_shape=_sds((N_DEV,) + blk.shape, blk.dtype), in_specs=[VMEM], out_specs=VMEM,
                  scratch=[pltpu.SemaphoreType.DMA((N_DEV - 1,)), pltpu.SemaphoreType.DMA((N_DEV - 1,))])(blk)


def _allreduce8(buf, name):
    rows = buf.shape[0]
    rb = rows // N_DEV
    assert rb * N_DEV == rows and rb % 8 == 0

    def body(x_ref, o_ref, stage, ssem, rsem):
        x, y, c = _place()
        me = 4 * x + 2 * y + c
        peers = []
        for k in range(1, N_DEV):
            px = 1 - x if k & 4 else x
            py = 1 - y if k & 2 else y
            pc = 1 - c if k & 1 else c
            peers.append(((px, py, pc), 4 * px + 2 * py + pc))

        def blk(ref, idx):
            return ref.at[pl.ds(pl.multiple_of(idx * rb, 8), rb), :]

        def copy(phase, k, src, dst, dev):
            return pltpu.make_async_remote_copy(src_ref=src, dst_ref=dst, send_sem=ssem.at[phase, k],
                                                recv_sem=rsem.at[phase, k], device_id=dev, device_id_type=MESH)

        stage[me] = blk(x_ref, me)[...]
        scatter = [copy(0, k, blk(x_ref, pidx), stage.at[me], dev) for k, (dev, pidx) in enumerate(peers)]
        for cp in scatter:
            cp.start()
        for k, (dev, pidx) in enumerate(peers):
            copy(0, k, blk(x_ref, pidx), stage.at[pidx], dev).wait_recv()
        total = stage[0]
        for j in range(1, N_DEV):
            total = total + stage[j]
        blk(o_ref, me)[...] = total
        gather = [copy(1, k, blk(o_ref, me), blk(o_ref, me), dev) for k, (dev, pidx) in enumerate(peers)]
        for cp in gather:
            cp.start()
        for k, (dev, pidx) in enumerate(peers):
            copy(1, k, blk(o_ref, pidx), blk(o_ref, pidx), dev).wait_recv()
        for cp in scatter + gather:
            cp.wait_send()

    return _pcall(body, name=name, out_shape=_sds(buf.shape, f32), in_specs=[VMEM], out_specs=VMEM,
                  scratch=[pltpu.VMEM((N_DEV, rb, LANES), f32), pltpu.SemaphoreType.DMA((2, N_DEV - 1)),
                           pltpu.SemaphoreType.DMA((2, N_DEV - 1))])(buf)


def _other_chips(x, y):
    return [((1 - x, y), 2 * (1 - x) + y), ((x, 1 - y), 2 * x + (1 - y)), ((1 - x, 1 - y), 2 * (1 - x) + (1 - y))]


HBM = pl.BlockSpec(memory_space=pltpu.HBM)
SEM = pl.BlockSpec(memory_space=pltpu.SEMAPHORE)
EFFECT = pltpu.SideEffectType.DATAFLOW_SIDE_EFFECTING


def _in_hbm(a):
    return pltpu.with_memory_space_constraint(a, pltpu.HBM)


def _ag_start(layers, after, name):
    flat = [t for lay in layers for t in lay]
    n, nl = len(flat), len(layers)

    def body(*refs):
        src = refs[:n]
        sems = refs[n + 1:n + 1 + 2 * nl]
        token = refs[-1]
        x, y, c = _place()
        s_me = 2 * x + y
        t = 0
        for i, lay in enumerate(layers):
            for k in range(len(lay)):
                for j, ((px, py), _) in enumerate(_other_chips(x, y)):
                    pltpu.make_async_remote_copy(src_ref=src[t].at[s_me, c], dst_ref=src[t].at[s_me, c],
                                                 send_sem=sems[2 * i].at[3 * k + j], recv_sem=sems[2 * i + 1].at[3 * k + j],
                                                 device_id=(px, py, c), device_id_type=MESH).start()
                t += 1
        token[...] = jnp.zeros_like(token)

    sem_shapes = [pltpu.SemaphoreType.DMA((3 * len(lay),)) for lay in layers for _ in range(2)]
    out_shape = sem_shapes + [pltpu.HBM(t.shape, t.dtype) for t in flat] + [_sds((8, LANES), f32)]
    outs = pl.pallas_call(
        body, name=name, out_shape=out_shape, in_specs=[HBM] * n + [ANY],
        out_specs=[SEM] * (2 * nl) + [HBM] * n + [VMEM], input_output_aliases={t: 2 * nl + t for t in range(n)},
        compiler_params=pltpu.CompilerParams(has_side_effects=EFFECT))(*[_in_hbm(t) for t in flat], after)
    sems = [(outs[2 * i], outs[2 * i + 1]) for i in range(nl)]
    thru, t = [], 2 * nl
    for lay in layers:
        thru.append(list(outs[t:t + len(lay)]))
        t += len(lay)
    return sems, thru, outs[-1]


def _ag_wait(inflight, sems, after, name):
    n = len(inflight)

    def body(*refs):
        src, ssem, rsem = refs[:n], refs[n], refs[n + 1]
        x, y, c = _place()
        s_me = 2 * x + y
        for k in range(n):
            for j, (_, s_p) in enumerate(_other_chips(x, y)):
                cp = pltpu.make_async_remote_copy(src_ref=src[k].at[s_me, c], dst_ref=src[k].at[s_p, c],
                                                  send_sem=ssem.at[3 * k + j], recv_sem=rsem.at[3 * k + j],
                                                  device_id=(x, y, c), device_id_type=MESH)
                cp.wait_send()
                cp.wait_recv()

    return pl.pallas_call(
        body, name=name, out_shape=[pltpu.HBM(t.shape, t.dtype) for t in inflight],
        in_specs=[HBM] * n + [SEM, SEM, ANY], out_specs=[HBM] * n, input_output_aliases={t: t for t in range(n)},
        compiler_params=pltpu.CompilerParams(has_side_effects=EFFECT))(*inflight, sems[0], sems[1], after)


def _ag_forward(arrived, name):
    n = len(arrived)

    def body(*refs):
        o = refs[n:2 * n]
        ssem, rsem = refs[2 * n:]
        x, y, c = _place()

        def copy(t, j, s, half, dev):
            return pltpu.make_async_remote_copy(src_ref=o[t].at[s, c], dst_ref=o[t].at[s, half], send_sem=ssem.at[t, j],
                                                recv_sem=rsem.at[t, j], device_id=dev, device_id_type=MESH)

        chips = _other_chips(x, y)
        sends = [copy(t, j, s_p, c, (x, y, 1 - c)) for t in range(n) for j, (_, s_p) in enumerate(chips)]
        for cp in sends:
            cp.start()
        for t in range(n):
            for j, (_, s_p) in enumerate(chips):
                copy(t, j, s_p, 1 - c, (x, y, c)).wait_recv()
        for cp in sends:
            cp.wait_send()

    return _pcall(body, name=name, out_shape=[_sds(p.shape, bf16) for p in arrived], in_specs=[ANY] * n,
                  out_specs=[ANY] * n, aliases={t: t for t in range(n)},
                  scratch=[pltpu.SemaphoreType.DMA((n, 3)), pltpu.SemaphoreType.DMA((n, 3))])(*arrived)


def _agf_start(arrived, name):
    n = len(arrived)

    def body(*refs):
        o = refs[:n]
        ssem, rsem, token = refs[n], refs[n + 1], refs[-1]
        x, y, c = _place()
        for t in range(n):
            for j, (_, s_p) in enumerate(_other_chips(x, y)):
                pltpu.make_async_remote_copy(src_ref=o[t].at[s_p, c], dst_ref=o[t].at[s_p, c],
                                             send_sem=ssem.at[3 * t + j], recv_sem=rsem.at[3 * t + j],
                                             device_id=(x, y, 1 - c), device_id_type=MESH).start()
        token[...] = jnp.zeros_like(token)

    out_shape = ([pltpu.SemaphoreType.DMA((3 * n,))] * 2 + [pltpu.HBM(a.shape, bf16) for a in arrived]
                 + [_sds((8, LANES), f32)])
    outs = pl.pallas_call(
        body, name=name, out_shape=out_shape, in_specs=[HBM] * n, out_specs=[SEM, SEM] + [HBM] * n + [VMEM],
        input_output_aliases={t: 2 + t for t in range(n)},
        compiler_params=pltpu.CompilerParams(has_side_effects=EFFECT))(*[_in_hbm(a) for a in arrived])
    return (outs[0], outs[1]), list(outs[2:2 + n]), outs[-1]


def _agf_wait(sems, inflight, after, name):
    n = len(inflight)

    def body(*refs):
        o, ssem, rsem = refs[:n], refs[n], refs[n + 1]
        x, y, c = _place()
        for t in range(n):
            for j, (_, s_p) in enumerate(_other_chips(x, y)):
                cp = pltpu.make_async_remote_copy(src_ref=o[t].at[s_p, c], dst_ref=o[t].at[s_p, 1 - c],
                                                  send_sem=ssem.at[3 * t + j], recv_sem=rsem.at[3 * t + j],
                                                  device_id=(x, y, c), device_id_type=MESH)
                cp.wait_send()
                cp.wait_recv()

    return pl.pallas_call(
        body, name=name, out_shape=[pltpu.HBM(a.shape, bf16) for a in inflight],
        in_specs=[HBM] * n + [SEM, SEM, ANY], out_specs=[HBM] * n, input_output_aliases={t: t for t in range(n)},
        compiler_params=pltpu.CompilerParams(has_side_effects=EFFECT))(*inflight, sems[0], sems[1], after)


def _rs_pair_start(grads, name):
    n = len(grads)

    def body(*refs):
        g, theirs = refs[:n], refs[n:2 * n]
        ssem, rsem, token = refs[2 * n], refs[2 * n + 1], refs[-1]
        x, y, c = _place()
        for t in range(n):
            pltpu.make_async_remote_copy(src_ref=g[t].at[:, 1 - c], dst_ref=theirs[t], send_sem=ssem.at[t],
                                         recv_sem=rsem.at[t], device_id=(x, y, 1 - c), device_id_type=MESH).start()
        token[...] = jnp.zeros_like(token)

    lands = [lax.empty((4,) + g.shape[2:], bf16) for g in grads]
    out_shape = ([pltpu.SemaphoreType.DMA((n,))] * 2 + [pltpu.HBM(g.shape, bf16) for g in grads]
                 + [pltpu.HBM(q.shape, bf16) for q in lands] + [_sds((8, LANES), f32)])
    outs = pl.pallas_call(
        body, name=name, out_shape=out_shape, in_specs=[HBM] * (2 * n), out_specs=[SEM, SEM] + [HBM] * (2 * n) + [VMEM],
        input_output_aliases={t: 2 + t for t in range(2 * n)},
        compiler_params=pltpu.CompilerParams(has_side_effects=EFFECT))(*[_in_hbm(a) for a in list(grads) + lands])
    return (outs[0], outs[1]), list(outs[2:2 + n]), list(outs[2 + n:2 + 2 * n]), outs[-1]


def _rs_pair_wait(sems, grads, lands, after, name):
    n = len(grads)

    def body(*refs):
        g, theirs = refs[:n], refs[n:2 * n]
        ssem, rsem = refs[2 * n], refs[2 * n + 1]
        x, y, c = _place()
        for t in range(n):
            cp = pltpu.make_async_remote_copy(src_ref=g[t].at[:, 1 - c], dst_ref=theirs[t], send_sem=ssem.at[t],
                                              recv_sem=rsem.at[t], device_id=(x, y, c), device_id_type=MESH)
            cp.wait_send()
            cp.wait_recv()

    outs = pl.pallas_call(
        body, name=name, out_shape=[pltpu.HBM(a.shape, bf16) for a in list(grads) + list(lands)],
        in_specs=[HBM] * (2 * n) + [SEM, SEM, ANY], out_specs=[HBM] * (2 * n),
        input_output_aliases={t: t for t in range(2 * n)},
        compiler_params=pltpu.CompilerParams(has_side_effects=EFFECT))(*grads, *lands, sems[0], sems[1], after)
    return list(outs[:n]), list(outs[n:])


def _rs_chip_start(pairs, name):
    n = len(pairs)

    def body(*refs):
        p, q = refs[:n], refs[n:2 * n]
        ssem, rsem, token = refs[2 * n], refs[2 * n + 1], refs[-1]
        x, y, c = _place()
        for t in range(n):
            for j, ((px, py), s_p) in enumerate(_other_chips(x, y)):
                pltpu.make_async_remote_copy(src_ref=p[t].at[s_p], dst_ref=q[t].at[j], send_sem=ssem.at[3 * t + j],
                                             recv_sem=rsem.at[3 * t + j], device_id=(px, py, c), device_id_type=MESH).start()
        token[...] = jnp.zeros_like(token)

    lands = [lax.empty((3,) + p.shape[1:], bf16) for p in pairs]
    out_shape = ([pltpu.SemaphoreType.DMA((3 * n,))] * 2 + [pltpu.HBM(p.shape, bf16) for p in pairs]
                 + [pltpu.HBM(q.shape, bf16) for q in lands] + [_sds((8, LANES), f32)])
    outs = pl.pallas_call(
        body, name=name, out_shape=out_shape, in_specs=[HBM] * (2 * n), out_specs=[SEM, SEM] + [HBM] * (2 * n) + [VMEM],
        input_output_aliases={t: 2 + t for t in range(2 * n)},
        compiler_params=pltpu.CompilerParams(has_side_effects=EFFECT))(*[_in_hbm(a) for a in list(pairs) + lands])
    return (outs[0], outs[1]), list(outs[2:2 + n]), list(outs[2 + n:2 + 2 * n]), outs[-1]


def _rs_chip_wait(sems, pairs, lands, after, name):
    n = len(pairs)

    def body(*refs):
        p, q = refs[:n], refs[n:2 * n]
        ssem, rsem = refs[2 * n], refs[2 * n + 1]
        x, y, c = _place()
        for t in range(n):
            for j, (_, s_p) in enumerate(_other_chips(x, y)):
                cp = pltpu.make_async_remote_copy(src_ref=p[t].at[s_p], dst_ref=q[t].at[j], send_sem=ssem.at[3 * t + j],
                                                  recv_sem=rsem.at[3 * t + j], device_id=(x, y, c), device_id_type=MESH)
                cp.wait_send()
                cp.wait_recv()

    outs = pl.pallas_call(
        body, name=name, out_shape=[pltpu.HBM(a.shape, bf16) for a in list(pairs) + list(lands)],
        in_specs=[HBM] * (2 * n) + [SEM, SEM, ANY], out_specs=[HBM] * (2 * n),
        input_output_aliases={t: t for t in range(2 * n)},
        compiler_params=pltpu.CompilerParams(has_side_effects=EFFECT))(*pairs, *lands, sems[0], sems[1], after)
    return list(outs[:n]), list(outs[n:])


def _rs_half_exchange(halves, name):
    n = len(halves)

    def body(*refs):
        o = refs[n:2 * n]
        ssem, rsem = refs[2 * n:]
        x, y, c = _place()

        def copy(t, half, dev):
            return pltpu.make_async_remote_copy(src_ref=o[t].at[c], dst_ref=o[t].at[half], send_sem=ssem.at[t],
                                                recv_sem=rsem.at[t], device_id=dev, device_id_type=MESH)

        sends = [copy(t, c, (x, y, 1 - c)) for t in range(n)]
        for cp in sends:
            cp.start()
        for t in range(n):
            copy(t, 1 - c, (x, y, c)).wait_recv()
        for cp in sends:
            cp.wait_send()

    return _pcall(body, name=name, out_shape=[_sds(h.shape, h.dtype) for h in halves], in_specs=[ANY] * n,
                  out_specs=[ANY] * n, aliases={t: t for t in range(n)},
                  scratch=[pltpu.SemaphoreType.DMA((n,)), pltpu.SemaphoreType.DMA((n,))])(*halves)


def _row_spec(tm, cols):
    return pl.BlockSpec((tm, cols), lambda i: (i, 0))


def _vec_spec(cols, rows=1):
    return pl.BlockSpec((rows, cols), lambda i: (0, 0))


def _modulated_norm(xv, g, shift, scale):
    r = lax.rsqrt(jnp.mean(xv * xv, axis=-1, keepdims=True) + EPS)
    return (((xv * r) * g) * (1.0 + scale) + shift).astype(bf16)


def _hnorm(x, g, shift, scale):
    T, tm = x.shape[0], 256

    def body(x_ref, g_ref, sh_ref, sc_ref, h_ref):
        h_ref[...] = _modulated_norm(x_ref[...], g_ref[...], sh_ref[...], sc_ref[...])

    return _pcall(body, name="hnorm", out_shape=_sds((T, D), bf16), grid=(T // tm,),
                  in_specs=[_row_spec(tm, D), _vec_spec(D), _vec_spec(D), _vec_spec(D)],
                  out_specs=_row_spec(tm, D))(x, g, shift, scale)


def _out_proj(y2, wo, x, gate, nxt=None):
    T, tm = x.shape[0], 512

    def body(y_ref, w_ref, x_ref, g_ref, *rest):
        o = jnp.dot(y_ref[0], w_ref[0], preferred_element_type=f32)
        o = o + jnp.dot(y_ref[1], w_ref[1], preferred_element_type=f32)
        xo = x_ref[...] + g_ref[...] * o
        if nxt is None:
            xo_ref, o_ref = rest
        else:
            ng_ref, nsh_ref, nsc_ref, xo_ref, o_ref, h_ref = rest
            h_ref[...] = _modulated_norm(xo, ng_ref[...], nsh_ref[...], nsc_ref[...])
        o_ref[...] = o
        xo_ref[...] = xo

    extra = [] if nxt is None else list(nxt)
    n_out = 2 if nxt is None else 3
    return _pcall(body, name="out_proj", out_shape=[_sds((T, D), f32), _sds((T, D), f32), _sds((T, D), bf16)][:n_out],
                  grid=(T // tm,),
                  in_specs=[pl.BlockSpec((2, tm, D), lambda i: (0, i, 0)), pl.BlockSpec((2, D, D), lambda i: (0, 0, 0)),
                            _row_spec(tm, D), _vec_spec(D)] + [_vec_spec(D)] * len(extra),
                  out_specs=[_row_spec(tm, D)] * n_out, vmem_mb=48)(y2, wo, x, gate, *extra)


def _gate_bwd_tile(dx, o_ref, gate_ref, dob_ref, dgate_ref):
    dob_ref[...] = (dx * gate_ref[...]).astype(bf16)
    dgate_ref[...] += jnp.sum(dx * o_ref[...], axis=0, keepdims=True)


def _loss_bwd(x, target, g, o, gate):
    T, tm = x.shape[0], 256

    def body(x_ref, t_ref, g_ref, o_ref, gate_ref, dx_ref, loss_ref, dg_ref, dob_ref, dgate_ref):
        @pl.when(pl.program_id(0) == 0)
        def _():
            loss_ref[...] = jnp.zeros_like(loss_ref)
            dg_ref[...] = jnp.zeros_like(dg_ref)
            dgate_ref[...] = jnp.zeros_like(dgate_ref)

        xv, gv = x_ref[...], g_ref[...]
        r = lax.rsqrt(jnp.mean(xv * xv, axis=-1, keepdims=True) + EPS)
        xn = xv * r
        err = xn * gv - t_ref[...]
        dy = err * (1.0 / D)
        dxn = dy * gv
        dx = r * (dxn - xn * jnp.mean(dxn * xn, axis=-1, keepdims=True))
        dx_ref[...] = dx
        dg_ref[...] += jnp.sum(dy * xn, axis=0, keepdims=True)
        loss_ref[...] += (0.5 / D) * jnp.sum(jnp.sum(err * err, axis=1, keepdims=True), axis=0, keepdims=True)
        _gate_bwd_tile(dx, o_ref, gate_ref, dob_ref, dgate_ref)

    return _pcall(body, name="loss_bwd",
                  out_shape=[_sds((T, D), f32), _sds((1, 1), f32), _sds((1, D), f32), _sds((T, D), bf16), _sds((1, D), f32)],
                  grid=(T // tm,),
                  in_specs=[_row_spec(tm, D), _row_spec(tm, D), _vec_spec(D), _row_spec(tm, D), _vec_spec(D)],
                  out_specs=[_row_spec(tm, D), pl.BlockSpec((1, 1), lambda i: (0, 0)), _vec_spec(D), _row_spec(tm, D),
                             _vec_spec(D)])(x, target, g, o, gate)


def _norm_bwd(x, dh, gin, g, scale, below=None):
    T, tm = x.shape[0], 256

    def body(x_ref, dh_ref, gin_ref, g_ref, sc_ref, *rest):
        if below is None:
            dx_ref, st_ref = rest
        else:
            o_ref, gate_ref, dx_ref, st_ref, dob_ref, dgate_ref = rest

        @pl.when(pl.program_id(0) == 0)
        def _():
            st_ref[...] = jnp.zeros_like(st_ref)
            if below is not None:
                dgate_ref[...] = jnp.zeros_like(dgate_ref)

        xv, gv, dhv = x_ref[...], g_ref[...], dh_ref[...]
        r = lax.rsqrt(jnp.mean(xv * xv, axis=-1, keepdims=True) + EPS)
        xn = xv * r
        da = dhv * (1.0 + sc_ref[...])
        dxn = da * gv
        dx = gin_ref[...] + r * (dxn - xn * jnp.mean(dxn * xn, axis=-1, keepdims=True))
        dx_ref[...] = dx
        st_ref[0:1, :] += jnp.sum(dhv, axis=0, keepdims=True)
        st_ref[1:2, :] += jnp.sum(dhv * (xn * gv), axis=0, keepdims=True)
        st_ref[2:3, :] += jnp.sum(da * xn, axis=0, keepdims=True)
        if below is not None:
            _gate_bwd_tile(dx, o_ref, gate_ref, dob_ref, dgate_ref)

    out_shape = [_sds((T, D), f32), _sds((8, D), f32)]
    in_specs = [_row_spec(tm, D), _row_spec(tm, D), _row_spec(tm, D), _vec_spec(D), _vec_spec(D)]
    out_specs = [_row_spec(tm, D), _vec_spec(D, 8)]
    args = [x, dh, gin, g, scale]
    if below is not None:
        out_shape += [_sds((T, D), bf16), _sds((1, D), f32)]
        in_specs += [_row_spec(tm, D), _vec_spec(D)]
        out_specs += [_row_spec(tm, D), _vec_spec(D)]
        args += list(below)
    return _pcall(body, name="norm_bwd", out_shape=out_shape, grid=(T // tm,), in_specs=in_specs,
                  out_specs=out_specs)(*args)


def _cast_place(place, w, layer, after=None):
    _, rows, cols = w.shape
    tr = 256

    def body(place_ref, w_ref, *rest):
        rest[-1][...] = w_ref[...].astype(bf16)

    extra = [] if after is None else [after]
    return _pcall(body, name="cast_place", out_shape=_sds((4, rows, cols), bf16), grid=(rows // tr,), prefetch=1,
                  in_specs=[pl.BlockSpec((None, tr, cols), lambda i, pr: (layer, i, 0))] + [ANY] * len(extra),
                  out_specs=pl.BlockSpec((None, tr, cols), lambda i, pr: (pr[0], i, 0)))(place, w, *extra)


def _rs_add(place, grad, theirs):
    _, rows, cols = theirs.shape
    tr = min(rows, 512)
    spec = pl.BlockSpec((None, tr, cols), lambda s, i, pr: (s, i, 0))

    def body(place_ref, a_ref, b_ref, o_ref):
        o_ref[...] = (a_ref[...].astype(f32) + b_ref[...].astype(f32)).astype(bf16)

    return _pcall(body, name="rs_add", out_shape=_sds(theirs.shape, bf16), grid=(4, rows // tr), prefetch=1,
                  in_specs=[pl.BlockSpec((None, None, tr, cols), lambda s, i, pr: (s, pr[1], i, 0)), spec],
                  out_specs=spec)(place, grad, theirs)


def _rs_sum(place, pairs, slots):
    _, rows, cols = slots.shape
    tr = min(rows, 512)

    def body(place_ref, p_ref, q_ref, o_ref):
        total = ((p_ref[...].astype(f32) + q_ref[0].astype(f32)) + q_ref[1].astype(f32)) + q_ref[2].astype(f32)
        o_ref[...] = total.astype(bf16)

    return _pcall(body, name="rs_sum", out_shape=_sds((2, rows, cols), bf16), grid=(rows // tr,), prefetch=1,
                  in_specs=[pl.BlockSpec((None, tr, cols), lambda i, pr: (pr[0], i, 0)),
                            pl.BlockSpec((3, tr, cols), lambda i, pr: (0, i, 0))],
                  out_specs=pl.BlockSpec((None, tr, cols), lambda i, pr: (pr[1], i, 0)))(place, pairs, slots)


def _adamw_math(w, g, m, v):
    m = ADAM_B1 * m + (1.0 - ADAM_B1) * g
    v = ADAM_B2 * v + (1.0 - ADAM_B2) * jnp.square(g)
    m_hat = m / (1.0 - ADAM_B1 ** ADAM_STEP)
    v_hat = v / (1.0 - ADAM_B2 ** ADAM_STEP)
    delta = -ADAM_LR * (m_hat / (jnp.sqrt(v_hat) + ADAM_EPS) + ADAM_WD * w)
    return delta, m, v


def _adamw_layer(layer, w, g, m, v, so_far):
    _, rows, cols = w.shape
    tr = 256
    spec = pl.BlockSpec((None, tr, cols), lambda i: (layer, i, 0))

    def body(w_ref, g_ref, m_ref, v_ref, *rest):
        go_ref, d_ref, mo_ref, vo_ref = rest[-4:]
        g = g_ref[...].astype(f32)
        go_ref[...] = g
        d_ref[...], mo_ref[...], vo_ref[...] = _adamw_math(w_ref[...], g, m_ref[...], v_ref[...])

    args, in_specs, aliases = [w, g, m, v], [spec, pl.BlockSpec((tr, cols), lambda i: (i, 0)), spec, spec], None
    if so_far is not None:
        args += list(so_far)
        in_specs += [ANY] * 4
        aliases = {4 + k: k for k in range(4)}
    return _pcall(body, name="adamw", out_shape=[_sds(w.shape, f32)] * 4, grid=(rows // tr,), in_specs=in_specs,
                  out_specs=[spec] * 4, aliases=aliases, vmem_mb=48)(*args)


def _adamw_small(items):
    n = len(items)

    def body(*refs):
        ins, outs = refs[:4 * n], refs[4 * n:]
        for t in range(n):
            w_ref, g_ref, m_ref, v_ref = ins[4 * t:4 * t + 4]
            if len(g_ref.shape) == len(w_ref.shape) + 1:
                g = g_ref[0]
                for b in range(1, g_ref.shape[0]):
                    g = g + g_ref[b]
            else:
                g = g_ref[...]
            d, m, v = _adamw_math(w_ref[...], g, m_ref[...], v_ref[...])
            outs[4 * t][...], outs[4 * t + 1][...], outs[4 * t + 2][...], outs[4 * t + 3][...] = g, d, m, v

    out_shape = [_sds(w.shape, f32) for (w, _, _, _) in items for _ in range(4)]
    flat = [a for it in items for a in it]
    res = _pcall(body, name="adamw_small", out_shape=out_shape, in_specs=[VMEM] * (4 * n),
                 out_specs=[VMEM] * (4 * n))(*flat)
    return [tuple(res[4 * t:4 * t + 4]) for t in range(n)]


NN = ((1,), (0,))
NT = ((1,), (1,))
TN = ((0,), (0,))


def _mm(name, a, b, *, grid, a_spec, b_spec, out_shape, out_spec, dims, vmem_mb=48):
    def body(a_ref, b_ref, o_ref):
        r = lax.dot_general(a_ref[...], b_ref[...], (dims, ((), ())), preferred_element_type=f32)
        o_ref[...] = r.astype(o_ref.dtype)

    return _pcall(body, name=name, out_shape=out_shape, grid=grid, in_specs=[a_spec, b_spec], out_specs=out_spec,
                  vmem_mb=vmem_mb)(a, b)


def _whole(shape):
    return pl.BlockSpec(shape, lambda j: (0,) * len(shape))


def _split_spec(rows, tile, per_split):
    return pl.BlockSpec((None, rows, tile), lambda j: (j // per_split, 0, j % per_split))


class _Proj:
    def __init__(self, n, splits, tile):
        self.n, self.splits, self.tile = n, splits, tile
        self.steps = n // tile
        self.w_per = n // 4 // tile
        self.a_per = n // splits // tile
        assert self.w_per * tile * 4 == n and self.a_per * tile * splits == n

    def fwd(self, hb, wg):
        T = hb.shape[0]
        sub, tile, w_per = FWD_TILES, self.tile, self.w_per
        wide = sub * tile
        a_per = self.n // self.splits // wide
        assert a_per * wide * self.splits == self.n

        def w_tile(q):
            return pl.BlockSpec((None, D, tile), lambda j: ((sub * j + q) // w_per, 0, (sub * j + q) % w_per))

        def body(a_ref, *rest):
            w = jnp.concatenate([rest[q][...] for q in range(sub)], axis=1)
            rest[sub][...] = jnp.dot(a_ref[...], w, preferred_element_type=f32).astype(bf16)

        return _pcall(body, name="proj_fwd", out_shape=_sds((self.splits, T, self.n // self.splits), bf16),
                      grid=(self.n // wide,), in_specs=[_whole((T, D))] + [w_tile(q) for q in range(sub)],
                      out_specs=pl.BlockSpec((None, T, wide), lambda j: (j // a_per, 0, j % a_per)),
                      vmem_mb=48)(hb, *([wg] * sub))

    def dw(self, hb, dp):
        T = hb.shape[0]
        return _mm("proj_dw", hb, dp, grid=(self.steps,), a_spec=_whole((T, D)),
                   b_spec=_split_spec(T, self.tile, self.a_per), out_shape=_sds((4, D, self.n // 4), bf16),
                   out_spec=_split_spec(D, self.tile, self.w_per), dims=TN)

    def dh(self, dp, wg):
        T = dp.shape[1]
        sub, tile, w_per = DH_WIDE // self.tile, self.tile, self.w_per
        a_per = self.n // self.splits // DH_WIDE
        assert sub * tile == DH_WIDE and a_per * DH_WIDE * self.splits == self.n

        def w_tile(q):
            return pl.BlockSpec((None, D, tile), lambda k: ((sub * k + q) // w_per, 0, (sub * k + q) % w_per))

        def body(a_ref, *rest):
            o_ref = rest[sub]
            w = jnp.concatenate([rest[q][...] for q in range(sub)], axis=1)
            r = lax.dot_general(a_ref[...], w, (NT, ((), ())), preferred_element_type=f32)

            @pl.when(pl.program_id(0) == 0)
            def _():
                o_ref[...] = r

            @pl.when(pl.program_id(0) > 0)
            def _():
                o_ref[...] += r

        return _pcall(body, name="proj_dh", out_shape=_sds((T, D), f32), grid=(self.n // DH_WIDE,),
                      in_specs=[pl.BlockSpec((None, T, DH_WIDE), lambda k: (k // a_per, 0, k % a_per))]
                      + [w_tile(q) for q in range(sub)],
                      out_specs=_whole((T, D)), vmem_mb=48)(dp, *([wg] * sub))


EVEN_PROJ = _Proj(7 * D, 7, 256)
ODD_PROJ = _Proj(4 * D, 2, 512)


def _dy_mm(dob, wo):
    T = dob.shape[0]
    return _mm("out_dy", dob, wo, grid=(4,), a_spec=_whole((T, D)),
               b_spec=pl.BlockSpec((None, 512, D), lambda j: (j, 0, 0)), out_shape=_sds((2, T, D), f32),
               out_spec=_split_spec(T, 512, 2), dims=NT)


def _dwo_mm(y2, dob):
    T = dob.shape[0]
    return _mm("out_dw", y2, dob, grid=(4,), a_spec=_split_spec(T, 512, 2), b_spec=_whole((T, D)),
               out_shape=_sds((4, 512, D), bf16), out_spec=pl.BlockSpec((None, 512, D), lambda j: (j, 0, 0)), dims=TN)


def _head_spec(lead, T):
    return pl.BlockSpec((lead, T, HEAD), lambda h: (0, 0, h))


def _head_vec(rows):
    return pl.BlockSpec((rows, HEAD), lambda h: (0, h))


_HEAD_MAT = pl.BlockSpec((None, HEAD, HEAD), lambda h: (h, 0, 0))


def _causal():
    return lax.broadcasted_iota(jnp.int32, (HEAD, HEAD), 0) >= lax.broadcasted_iota(jnp.int32, (HEAD, HEAD), 1)


def _layernorm_head(v):
    mu = jnp.mean(v, axis=-1, keepdims=True)
    d = v - mu
    rstd = lax.rsqrt(jnp.mean(d * d, axis=-1, keepdims=True) + EPS)
    return d * rstd, rstd


def _even_fwd(p7, conv_w, ln_g, ln_b, sgu_w, sgu_bias):
    T, C = p7.shape[1], CHUNK_ROWS

    def body(p_ref, cw_ref, lg_ref, lb_ref, w_ref, b_ref, y_ref):
        w0, w1, w2 = cw_ref[0:1, :], cw_ref[1:2, :], cw_ref[2:3, :]
        wm = jnp.where(_causal(), w_ref[...], 0.0).astype(bf16)
        bias, lg, lb = b_ref[...], lg_ref[...], lb_ref[...]

        def step(i, halo):
            rows = pl.ds(pl.multiple_of(i * C, C), C)
            ah, ab, ac, az, u, v, zb = (p_ref[k, rows, :].astype(f32) for k in range(7))
            tt = ac * ah
            ext = jnp.concatenate([halo, tt], axis=0)
            cv = w2 * tt + w1 * pltpu.roll(ext, 1, 0)[HALO_CONV:] + w0 * pltpu.roll(ext, 2, 0)[HALO_CONV:]
            y_ref[0, rows, :] = (ab * cv * _silu(az)).astype(bf16)
            vhat, _ = _layernorm_head(v)
            vn = (vhat * lg + lb).astype(bf16)
            mix = jnp.concatenate([jnp.dot(wm, vn[k * HEAD:(k + 1) * HEAD], preferred_element_type=f32) + bias
                                   for k in range(C // HEAD)], axis=0)
            y_ref[1, rows, :] = (u * mix * _silu(zb)).astype(bf16)
            return tt[C - HALO_CONV:]

        lax.fori_loop(0, T // C, step, jnp.zeros((HALO_CONV, HEAD), f32))

    return _pcall(body, name="even_fwd", out_shape=_sds((2, T, D), bf16), grid=(NH,),
                  in_specs=[_head_spec(7, T), _head_vec(3), _head_vec(1), _head_vec(1), _HEAD_MAT, _HEAD_MAT],
                  out_specs=_head_spec(2, T), vmem_mb=32)(p7, conv_w, ln_g, ln_b, sgu_w, sgu_bias)


def _even_bwd(p7, dy2, conv_w, ln_g, ln_b, sgu_w, sgu_bias):
    T, C = p7.shape[1], CHUNK_ROWS
    n_chunks = T // C

    def body(p_ref, dy_ref, cw_ref, lg_ref, lb_ref, w_ref, b_ref,
             dp_ref, dcw_ref, dlg_ref, dlb_ref, dw_ref, dms_ref, dcv_s):
        w0, w1, w2 = cw_ref[0:1, :], cw_ref[1:2, :], cw_ref[2:3, :]
        tri = _causal()
        wm = jnp.where(tri, w_ref[...], 0.0).astype(bf16)
        bias, lg, lb = b_ref[...], lg_ref[...], lb_ref[...]
        dw_ref[...] = jnp.zeros_like(dw_ref)
        dms_ref[...] = jnp.zeros_like(dms_ref)

        def fwd_step(i, carry):
            halo, a0, a1, a2, alg, alb = carry
            rows = pl.ds(pl.multiple_of(i * C, C), C)
            ah, ab, ac, az = (p_ref[k, rows, :].astype(f32) for k in range(4))
            dya = dy_ref[0, rows, :]
            tt = ac * ah
            ext = jnp.concatenate([halo, tt], axis=0)
            t1, t2 = pltpu.roll(ext, 1, 0)[HALO_CONV:], pltpu.roll(ext, 2, 0)[HALO_CONV:]
            cv = w2 * tt + w1 * t1 + w0 * t2
            sa, dsa = _silu_and_grad(az)
            g1 = dya * sa
            dp_ref[1, rows, :] = (g1 * cv).astype(bf16)
            dp_ref[3, rows, :] = (dya * ab * cv * dsa).astype(bf16)
            dcv = g1 * ab
            dcv_s[rows, :] = dcv
            a2 = a2 + jnp.sum(dcv * tt, axis=0, keepdims=True)
            a1 = a1 + jnp.sum(dcv * t1, axis=0, keepdims=True)
            a0 = a0 + jnp.sum(dcv * t2, axis=0, keepdims=True)

            u, zb, dyb = p_ref[4, rows, :].astype(f32), p_ref[6, rows, :].astype(f32), dy_ref[1, rows, :]
            vhat, rstd = _layernorm_head(p_ref[5, rows, :].astype(f32))
            vn = (vhat * lg + lb).astype(bf16)
            sb, dsb = _silu_and_grad(zb)
            mix = jnp.concatenate([jnp.dot(wm, vn[k * HEAD:(k + 1) * HEAD], preferred_element_type=f32) + bias
                                   for k in range(C // HEAD)], axis=0)
            dp_ref[4, rows, :] = (dyb * mix * sb).astype(bf16)
            dp_ref[6, rows, :] = (dyb * u * mix * dsb).astype(bf16)
            dmix = dyb * u * sb
            dvn_parts = []
            for k in range(C // HEAD):
                dm = dmix[k * HEAD:(k + 1) * HEAD]
                dmb = dm.astype(bf16)
                dvn_parts.append(lax.dot_general(wm, dmb, (TN, ((), ())), preferred_element_type=f32))
                dw_ref[...] += lax.dot_general(dmb, vn[k * HEAD:(k + 1) * HEAD], (NT, ((), ())),
                                               preferred_element_type=f32)
                dms_ref[...] += dm
            dvn = jnp.concatenate(dvn_parts, axis=0)
            alg = alg + jnp.sum(dvn * vhat, axis=0, keepdims=True)
            alb = alb + jnp.sum(dvn, axis=0, keepdims=True)
            dvh = dvn * lg
            dv = rstd * (dvh - jnp.mean(dvh, axis=-1, keepdims=True)
                         - vhat * jnp.mean(dvh * vhat, axis=-1, keepdims=True))
            dp_ref[5, rows, :] = dv.astype(bf16)
            return tt[C - HALO_CONV:], a0, a1, a2, alg, alb

        zrow = jnp.zeros((1, HEAD), f32)
        _, a0, a1, a2, alg, alb = lax.fori_loop(
            0, n_chunks, fwd_step, (jnp.zeros((HALO_CONV, HEAD), f32), zrow, zrow, zrow, zrow, zrow))
        dcw_ref[0:1, :], dcw_ref[1:2, :], dcw_ref[2:3, :] = a0, a1, a2
        dlg_ref[...], dlb_ref[...] = alg, alb
        dw_ref[...] = jnp.where(tri, dw_ref[...], 0.0)

        def bwd_step(k, halo):
            rows = pl.ds(pl.multiple_of((n_chunks - 1 - k) * C, C), C)
            dcv = dcv_s[rows, :]
            ext = jnp.concatenate([dcv, halo], axis=0)
            n1 = pltpu.roll(ext, C + HALO_CONV - 1, 0)[:C]
            n2 = pltpu.roll(ext, C + HALO_CONV - 2, 0)[:C]
            dtt = w2 * dcv + w1 * n1 + w0 * n2
            dp_ref[2, rows, :] = (dtt * p_ref[0, rows, :].astype(f32)).astype(bf16)
            dp_ref[0, rows, :] = (dtt * p_ref[2, rows, :].astype(f32)).astype(bf16)
            return dcv[:HALO_CONV]

        lax.fori_loop(0, n_chunks, bwd_step, jnp.zeros((HALO_CONV, HEAD), f32))

    out_shape = [_sds((7, T, D), bf16), _sds((3, D), f32), _sds((1, D), f32), _sds((1, D), f32),
                 _sds((NH, HEAD, HEAD), f32), _sds((NH, HEAD, HEAD), f32)]
    return _pcall(body, name="even_bwd", out_shape=out_shape, grid=(NH,),
                  in_specs=[_head_spec(7, T), _head_spec(2, T), _head_vec(3), _head_vec(1), _head_vec(1),
                            _HEAD_MAT, _HEAD_MAT],
                  out_specs=[_head_spec(7, T), _head_vec(3), _head_vec(1), _head_vec(1), _HEAD_MAT, _HEAD_MAT],
                  scratch=[pltpu.VMEM((T, HEAD), f32)], vmem_mb=48)(p7, dy2, conv_w, ln_g, ln_b, sgu_w, sgu_bias)


def _window_sum(ext, win, towards_past):
    n, k, s = ext.shape[0], 1, ext
    while k < win:
        s = s + pltpu.roll(s, k if towards_past else n - k, 0)
        k *= 2
    return s


def _pool_count(i, C, win):
    t = i * C + lax.broadcasted_iota(jnp.int32, (C, 1), 0)
    cnt = jnp.minimum(t + 1, win).astype(f32)
    return cnt, 1.0 / cnt


def _group_specs(T):
    p_spec = pl.BlockSpec((None, T, GC), lambda g: (0, 0, g))
    z_spec = pl.BlockSpec((None, T, GC), lambda g: (1, 0, g))
    pw_spec = pl.BlockSpec((4, GC // 4, GC), lambda g: (0, g, 0))
    ps_spec = pl.BlockSpec((1, GC), lambda g: (0, g))
    y_spec = pl.BlockSpec((None, T, GC), lambda g: (g // 2, 0, g % 2))
    return p_spec, z_spec, pw_spec, ps_spec, y_spec


def _odd_fwd(p2, pool_wg, pool_scale):
    T, C = p2.shape[1], CHUNK_ROWS
    p_spec, z_spec, pw_spec, ps_spec, y_spec = _group_specs(T)

    def body(p_ref, z_ref, pw_ref, ps_ref, y_ref):
        pw, ps = pw_ref[...].reshape(GC, GC), ps_ref[...]

        def run(win):
            def step(i, halo):
                rows = pl.ds(pl.multiple_of(i * C, C), C)
                p = p_ref[rows, :].astype(f32)
                s = _window_sum(jnp.concatenate([halo, p], axis=0), win, True)[HALO_POOL:]
                pooled = s * _pool_count(i, C, win)[1] - p
                ypre = jnp.dot(pooled.astype(bf16), pw, preferred_element_type=f32)
                y_ref[rows, :] = (ypre * ps * _silu(z_ref[rows, :].astype(f32))).astype(bf16)
                return p[C - HALO_POOL:]

            lax.fori_loop(0, T // C, step, jnp.zeros((HALO_POOL, GC), f32))

        for gi, win in enumerate(WINDOWS):
            pl.when(pl.program_id(0) == gi)(functools.partial(run, win))

    return _pcall(body, name="odd_fwd", out_shape=_sds((2, T, D), bf16), grid=(len(WINDOWS),),
                  in_specs=[p_spec, z_spec, pw_spec, ps_spec], out_specs=y_spec, vmem_mb=40)(p2, p2, pool_wg, pool_scale)


def _odd_bwd(p2, dy2, pool_wg, pool_scale):
    T, C = p2.shape[1], CHUNK_ROWS
    n_chunks = T // C
    p_spec, z_spec, pw_spec, ps_spec, y_spec = _group_specs(T)

    def body(p_ref, z_ref, dy_ref, pw_ref, ps_ref, dp_ref, dpw_ref, dps_ref, q_s, acc_s):
        pw, ps = pw_ref[...].reshape(GC, GC), ps_ref[...]

        def run(win):
            acc_s[...] = jnp.zeros_like(acc_s)

            def fwd_step(i, carry):
                halo, aps = carry
                rows = pl.ds(pl.multiple_of(i * C, C), C)
                p, z, dy = p_ref[rows, :].astype(f32), z_ref[rows, :].astype(f32), dy_ref[rows, :]
                _, inv_cnt = _pool_count(i, C, win)
                s = _window_sum(jnp.concatenate([halo, p], axis=0), win, True)[HALO_POOL:]
                pb = (s * inv_cnt - p).astype(bf16)
                ypre = jnp.dot(pb, pw, preferred_element_type=f32)
                sz, dsz = _silu_and_grad(z)
                aps = aps + jnp.sum(dy * ypre * sz, axis=0, keepdims=True)
                dp_ref[1, rows, :] = (dy * ypre * ps * dsz).astype(bf16)
                dyp = (dy * ps * sz).astype(bf16)
                acc_s[...] += lax.dot_general(pb, dyp, (TN, ((), ())), preferred_element_type=f32)
                dpool = lax.dot_general(dyp, pw, (NT, ((), ())), preferred_element_type=f32)
                q_s[rows, :] = dpool * inv_cnt
                return p[C - HALO_POOL:], aps

            _, aps = lax.fori_loop(0, n_chunks, fwd_step, (jnp.zeros((HALO_POOL, GC), f32), jnp.zeros((1, GC), f32)))
            dps_ref[...] = aps
            dpw_ref[...] = acc_s[...].reshape(4, GC // 4, GC).astype(bf16)

            def bwd_step(k, halo):
                i = n_chunks - 1 - k
                rows = pl.ds(pl.multiple_of(i * C, C), C)
                q = q_s[rows, :]
                s = _window_sum(jnp.concatenate([q, halo], axis=0), win, False)[:C]
                dp_ref[0, rows, :] = (s - q * _pool_count(i, C, win)[0]).astype(bf16)
                return q[:HALO_POOL]

            lax.fori_loop(0, n_chunks, bwd_step, jnp.zeros((HALO_POOL, GC), f32))

        for gi, win in enumerate(WINDOWS):
            pl.when(pl.program_id(0) == gi)(functools.partial(run, win))

    out_shape = [_sds((2, T, 2 * D), bf16), _sds((4, GC, GC), bf16), _sds((1, 2 * D), f32)]
    return _pcall(body, name="odd_bwd", out_shape=out_shape, grid=(len(WINDOWS),),
                  in_specs=[p_spec, z_spec, y_spec, pw_spec, ps_spec],
                  out_specs=[pl.BlockSpec((2, T, GC), lambda g: (0, 0, g)), pw_spec, ps_spec],
                  scratch=[pltpu.VMEM((T, GC), f32), pltpu.VMEM((GC, GC), f32)], vmem_mb=52)(
                      p2, p2, dy2, pool_wg, pool_scale)


def _ada_fwd(c_all, ada_w):
    cols = ada_w.shape[2]

    def body(c_ref, w_ref, o_ref):
        o_ref[...] = jnp.dot(_silu(c_ref[...]), w_ref[...], preferred_element_type=f32,
                             precision=lax.Precision.HIGHEST)

    return _pcall(body, name="ada_fwd", out_shape=_sds((4, N_DEV, cols), f32), grid=(4,),
                  in_specs=[pl.BlockSpec((N_DEV, D), lambda i: (0, 0)), pl.BlockSpec((None, D, cols), lambda i: (i, 0, 0))],
                  out_specs=pl.BlockSpec((None, N_DEV, cols), lambda i: (i, 0, 0)))(c_all, ada_w)


def _ada_bwd(c_all_t, dmod, w, m, v):
    cols, tr = w.shape[2], 256
    spec = pl.BlockSpec((None, tr, cols), lambda l, i: (l, i, 0))

    def body(c_ref, dm_ref, w_ref, m_ref, v_ref, g_ref, d_ref, mo_ref, vo_ref):
        sc = _silu(c_ref[...])
        g = sc[:, 0:1] * dm_ref[0:1, :]
        for b in range(1, N_DEV):
            g = g + sc[:, b:b + 1] * dm_ref[b:b + 1, :]
        g_ref[...] = g
        d_ref[...], mo_ref[...], vo_ref[...] = _adamw_math(w_ref[...], g, m_ref[...], v_ref[...])

    return _pcall(body, name="ada_bwd", out_shape=[_sds(w.shape, f32)] * 4, grid=(4, D // tr),
                  in_specs=[pl.BlockSpec((tr, N_DEV), lambda l, i: (i, 0)),
                            pl.BlockSpec((None, N_DEV, cols), lambda l, i: (l, 0, 0)), spec, spec, spec],
                  out_specs=[spec] * 4)(c_all_t, dmod, w, m, v)


def _layer_fwd(even, x, hb, gate, w, nxt, before_out=None):
    if even:
        w_in, w_out, conv_w, ln_g, ln_b, sgu_w, sgu_b = w
        bias = jnp.broadcast_to(sgu_b[:, :, None], (NH, HEAD, HEAD))
        p = EVEN_PROJ.fwd(hb, w_in)
        y2 = _even_fwd(p, conv_w, ln_g, ln_b, sgu_w, bias)
    else:
        w_in, pool_w, w_out, pool_scale = w
        p = ODD_PROJ.fwd(hb, w_in)
        y2 = _odd_fwd(p, pool_w, pool_scale)
    if before_out is not None:
        late_w_out, tok = before_out(y2)
        if late_w_out is not None:
            w_out = late_w_out
            w = (w_in, w_out) + tuple(w[2:]) if even else (w_in, pool_w, w_out, pool_scale)
        if tok is not None:
            gate = gate + tok[0:1, 0:1]
    outs = _out_proj(y2, w_out.reshape(2, D, D), x, gate, nxt)
    return outs[0], (None if nxt is None else outs[2]), (x, hb, p, y2, outs[1]), w


def _layer_bwd(even, gin, dob, dgate, saved, scale, g, w, below=None, send=None):
    x_in, hb, p, y2, o = saved
    if even:
        w_in, w_out, conv_w, ln_g, ln_b, sgu_w, sgu_b = w
        bias = jnp.broadcast_to(sgu_b[:, :, None], (NH, HEAD, HEAD))
        dy2 = _dy_mm(dob, w_out)
        dp, dconv, dlg, dlb, dsw, dms = _even_bwd(p, dy2, conv_w, ln_g, ln_b, sgu_w, bias)
        proj = EVEN_PROJ
        small = dict(conv_w=dconv, ln_g=dlg, ln_b=dlb, sgu_w=dsw, sgu_b=jnp.sum(dms, axis=-1))
        big = [proj.dw(hb, dp), _dwo_mm(y2, dob)]
    else:
        w_in, pool_w, w_out, pool_scale = w
        dy2 = _dy_mm(dob, w_out)
        dp, dpw, dps = _odd_bwd(p, dy2, pool_w, pool_scale)
        proj = ODD_PROJ
        small = dict(pool_scale=dps)
        big = [proj.dw(hb, dp), dpw, _dwo_mm(y2, dob)]
    if send is not None:
        big, tok = send(big)
        scale = scale + tok[0:1, 0:1]
    dh = proj.dh(dp, w_in)
    res = _norm_bwd(x_in, dh, gin, g, scale, below)
    stats = res[1]
    return (res[0], (None if below is None else (res[2], res[3])), big, small,
            jnp.concatenate([stats[0:2], dgate], axis=0), stats[2:3])


def _pack_rows(parts):
    rows = [p.reshape(-1, LANES) for p in parts]
    total = sum(r.shape[0] for r in rows)
    padded = -(-total // (8 * N_DEV)) * (8 * N_DEV)
    if padded > total:
        rows.append(jnp.zeros((padded - total, LANES), f32))
    return jnp.concatenate(rows, axis=0)


def _unpack_rows(buf, shapes):
    out, r = [], 0
    for shp in shapes:
        n = 1
        for d in shp:
            n *= d
        out.append(buf[r:r + n // LANES].reshape(shp))
        r += n // LANES
    return out


def kernel(x, c, norm_g, ada_w, ada_b, ab_w_in, ab_conv_w, ab_ln_g, ab_ln_b, ab_sgu_w, ab_sgu_b, ab_w_out, c_w_in, c_pool_w, c_pool_scale, c_w_out, final_g, loss_target, m_norm_g, m_ada_w, m_ada_b, m_ab_w_in, m_ab_conv_w, m_ab_ln_g, m_ab_ln_b, m_ab_sgu_w, m_ab_sgu_b, m_ab_w_out, m_c_w_in, m_c_pool_w, m_c_pool_scale, m_c_w_out, m_final_g, v_norm_g, v_ada_w, v_ada_b, v_ab_w_in, v_ab_conv_w, v_ab_ln_g, v_ab_ln_b, v_ab_sgu_w, v_ab_sgu_b, v_ab_w_out, v_c_w_in, v_c_pool_w, v_c_pool_scale, v_c_w_out, v_final_g):
    ix, iy, ic = _place()
    chip, dev = 2 * ix + iy, 4 * ix + 2 * iy + ic
    n_even, n_odd = ab_w_in.shape[0], c_w_in.shape[0]
    depth = n_even + n_odd
    acols = ada_w.shape[2]

    place = jnp.stack([chip, ic]).astype(jnp.int32)
    even_names, odd_names = ["ab_w_in", "ab_w_out"], ["c_w_in", "c_pool_w", "c_w_out"]
    params = {"ab_w_in": (ab_w_in, m_ab_w_in, v_ab_w_in), "ab_w_out": (ab_w_out, m_ab_w_out, v_ab_w_out),
              "c_w_in": (c_w_in, m_c_w_in, v_c_w_in), "c_w_out": (c_w_out, m_c_w_out, v_c_w_out),
              "c_pool_w": tuple(a.reshape(n_odd, GC, GC) for a in (c_pool_w, m_c_pool_w, v_c_pool_w))}

    first = _gather8(jnp.concatenate([c, ab_conv_w.reshape(1, -1), c_pool_scale.reshape(1, -1)], axis=1), "gather_c")
    c_all, small_all = first[:, 0, :D], first[0::2, 0, D:]
    modp = _ada_fwd(c_all, ada_w)
    modg = _gather8(modp, "gather_mod")
    mod_rows = lax.dynamic_index_in_dim(modg[0::2], dev, axis=2, keepdims=False)
    mod = jnp.transpose(mod_rows, (1, 0, 2)).reshape(depth, 3 * D) + ada_b
    mods = [(mod[i:i + 1, 0:D], mod[i:i + 1, D:2 * D], mod[i:i + 1, 2 * D:3 * D]) for i in range(depth)]

    def shard_cols(a, width):
        return lax.dynamic_slice_in_dim(a, chip * width, width, axis=a.ndim - 1)

    n_conv = ab_conv_w.size
    conv_all = small_all[:, :n_conv].reshape(4, n_even, 3, D // 4)
    conv_full = jnp.transpose(conv_all, (1, 2, 0, 3)).reshape(n_even, 3, D)
    scale_all = small_all[:, n_conv:].reshape(4, n_odd, 2 * D // 4)
    scale_full = jnp.transpose(scale_all, (1, 0, 2)).reshape(n_odd, 2 * D)

    def placed(nm, layer, after=None):
        w = params[nm][0]
        return _cast_place(place, w, layer, after).reshape(4, 2, w.shape[1] // 2, w.shape[2])

    def whole(arrays):
        return [g.reshape(4, 2 * g.shape[2], g.shape[3]) for g in arrays]

    gathers_done = mod[0:1, 0:LANES] + scale_full[0:1, 0:LANES]
    sems_a, in_a, tok = _ag_start([[placed(even_names[0], 0)]], gathers_done, "ag_start_0a")
    sems_b, in_b, tok = _ag_start([[placed(even_names[1], 0, tok)]], tok, "ag_start_0b")
    rest = [[placed(nm, i // 2, tok) for nm in (even_names if i % 2 == 0 else odd_names)] for i in range(1, depth)]
    sems_r, in_r, tok = _ag_start(rest, tok, "ag_start_rest")

    x_cur, saved, weights, handoff = x[0], [], [], {}
    hb = _hnorm(x_cur, norm_g[0:1], mods[0][0] + tok[0:1, 0:1], mods[0][1])
    for i in range(depth):
        j = i // 2
        if i == 0:
            full = whole(_ag_forward(_ag_wait(in_a[0], sems_a[0], hb, "ag_wait_0a"), "ag_forward")) + [None]
        else:
            full = whole(_agf_wait(*handoff.pop(i), x_cur, f"agf_wait_{i}"))
        if i % 2 == 0:
            w = (full[0], full[1], conv_full[j], ab_ln_g[j:j + 1], ab_ln_b[j:j + 1], ab_sgu_w[j], ab_sgu_b[j])
        else:
            w = (full[0], full[1], full[2], scale_full[j:j + 1])

        def before_out(y2, i=i):
            w_out, tok = None, None
            if i == 0:
                w_out = whole(_ag_forward(_ag_wait(in_b[0], sems_b[0], y2, "ag_wait_0b"), "ag_forward"))[0]
            if i + 1 < depth:
                arrived = _ag_wait(in_r[i], sems_r[i], y2, f"ag_wait_{i + 1}")
                sems_f, inflight, tok = _agf_start(arrived, f"agf_start_{i + 1}")
                handoff[i + 1] = (sems_f, inflight)
            return w_out, tok

        nxt = (norm_g[i + 1:i + 2], mods[i + 1][0], mods[i + 1][1]) if i + 1 < depth else None
        x_cur, hb, sv, w = _layer_fwd(i % 2 == 0, x_cur, hb, mods[i][2], w, nxt, before_out)
        weights.append(w)
        saved.append(sv)
    gin, loss, dfinal_g, dob, dgate = _loss_bwd(x_cur, loss_target[0], final_g.reshape(1, D), saved[-1][4],
                                                mods[-1][2])

    stacked = {}

    def finish(i, sems, pairs, lands, after):
        pairs, slots = _rs_chip_wait(sems, pairs, lands, after, f"rs_chip_wait_{i}")
        halves = [_rs_sum(place, p, q) for p, q in zip(pairs, slots)]
        for nm, g in zip(even_names if i % 2 == 0 else odd_names, _rs_half_exchange(halves, "rs_half_exchange")):
            w, m, v = params[nm]
            stacked[nm] = _adamw_layer(i // 2, w, g.reshape(w.shape[1], w.shape[2]), m, v, stacked.get(nm))

    small_g, dmod, dnorm_g, pending, tok = [None] * depth, [None] * depth, [None] * depth, None, None
    for i in reversed(range(depth)):
        w = weights[i]
        if tok is not None:
            w = w[:2] + (w[2] + tok[0:1, 0:1],) + w[3:] if i % 2 == 0 else w[:3] + (w[3] + tok[0:1, 0:1],)
        below = (saved[i - 1][4], mods[i - 1][2]) if i > 0 else None

        def send(big_g, i=i):
            big_g = [g.reshape(4, 2, g.shape[1] // 2, g.shape[2]) for g in big_g]
            sems, big_g, lands, tok = _rs_pair_start(big_g, f"rs_pair_start_{i}")
            return (sems, big_g, lands), tok

        gin, gate_bwd, sent, small_g[i], dmod[i], dnorm_g[i] = _layer_bwd(
            i % 2 == 0, gin, dob, dgate, saved[i], mods[i][1], norm_g[i:i + 1], w, below, send)
        if below is not None:
            dob, dgate = gate_bwd
        big_g, theirs = _rs_pair_wait(*sent, gin, f"rs_pair_wait_{i}")
        pairs = [_rs_add(place, a, b) for a, b in zip(big_g, theirs)]
        sems, pairs, lands, tok = _rs_chip_start(pairs, f"rs_chip_start_{i}")
        if pending is not None:
            finish(*pending, tok)
        pending = (i, sems, pairs, lands)
    grad_x = gin
    dmod, dnorm_g = jnp.stack(dmod), jnp.concatenate(dnorm_g, axis=0)

    small_parts = [dnorm_g + tok[0:1, 0:1], dfinal_g,
                   jnp.stack([small_g[2 * j]["conv_w"] for j in range(n_even)]),
                   jnp.concatenate([small_g[2 * j]["ln_g"] for j in range(n_even)], axis=0),
                   jnp.concatenate([small_g[2 * j]["ln_b"] for j in range(n_even)], axis=0),
                   jnp.stack([small_g[2 * j]["sgu_w"] for j in range(n_even)]),
                   jnp.stack([small_g[2 * j]["sgu_b"] for j in range(n_even)]),
                   jnp.concatenate([small_g[2 * j + 1]["pool_scale"] for j in range(n_odd)], axis=0),
                   jnp.pad(loss, ((0, 7), (0, LANES - 1)))]
    small_shapes = [p.shape for p in small_parts]
    reduced = _allreduce8(_pack_rows(small_parts), "allreduce_small")
    (g_norm_g, g_final_g, g_conv_full, g_ln_g, g_ln_b, g_sgu_w, g_sgu_b, g_scale_full,
     loss_row) = _unpack_rows(reduced, small_shapes)
    loss = loss_row[0, 0]
    g_conv = shard_cols(g_conv_full, D // 4)
    g_scale = shard_cols(g_scale_full, 2 * D // 4)
    dmod_all = _gather8(dmod.reshape(depth * 3 * D // LANES, LANES), "gather_dmod").reshape(N_DEV, depth, 3 * D)

    def two_d(a):
        return a.reshape(-1, a.shape[-1])

    small = [(norm_g, g_norm_g, m_norm_g, v_norm_g),
             (ada_b, dmod_all, m_ada_b, v_ada_b),
             (two_d(ab_conv_w), two_d(g_conv), two_d(m_ab_conv_w), two_d(v_ab_conv_w)),
             (ab_ln_g, g_ln_g, m_ab_ln_g, v_ab_ln_g),
             (ab_ln_b, g_ln_b, m_ab_ln_b, v_ab_ln_b),
             (two_d(ab_sgu_w), two_d(g_sgu_w), two_d(m_ab_sgu_w), two_d(v_ab_sgu_w)),
             (two_d(ab_sgu_b), two_d(g_sgu_b), two_d(m_ab_sgu_b), two_d(v_ab_sgu_b)),
             (c_pool_scale, g_scale, m_c_pool_scale, v_c_pool_scale),
             (final_g.reshape(1, D), g_final_g, m_final_g.reshape(1, D), v_final_g.reshape(1, D))]
    small_res = _adamw_small(small)
    small_shapes_out = [norm_g.shape, ada_b.shape, ab_conv_w.shape, ab_ln_g.shape, ab_ln_b.shape, ab_sgu_w.shape,
                        ab_sgu_b.shape, c_pool_scale.shape, final_g.shape]
    (r_norm_g, r_ada_b, r_conv, r_ln_g, r_ln_b, r_sgu_w, r_sgu_b, r_scale, r_final_g) = [
        tuple(a.reshape(shp) for a in res) for res, shp in zip(small_res, small_shapes_out)]

    dmod_cols = jnp.transpose(shard_cols(dmod_all, acols), (1, 0, 2))
    r_ada_w = _ada_bwd(c_all.T, dmod_cols, ada_w, m_ada_w, v_ada_w)

    finish(*pending, r_ada_w[1])
    r_ab_w_in, r_ab_w_out, r_c_w_in, r_c_w_out = (stacked[nm] for nm in ("ab_w_in", "ab_w_out", "c_w_in", "c_w_out"))
    r_c_pool_w = tuple(a.reshape(c_pool_w.shape) for a in stacked["c_pool_w"])

    order = [r_norm_g, r_ada_w, r_ada_b, r_ab_w_in, r_conv, r_ln_g, r_ln_b, r_sgu_w, r_sgu_b, r_ab_w_out,
             r_c_w_in, r_c_pool_w, r_scale, r_c_w_out, r_final_g]
    outs = [loss, grad_x[None]]
    for field in range(4):
        outs += [r[field] for r in order]
    return tuple(outs)
```

```python
import functools

import jax
import jax.numpy as jnp
from jax import lax
from jax.experimental import pallas as pl
from jax.experimental.pallas import tpu as pltpu

f32, bf16 = jnp.float32, jnp.bfloat16

D = 1024
HEAD = 128
NH = 8
WINDOWS = (2, 4, 8, 16)
GC = 512
EPS = 1e-6
HALO_CONV = 8
HALO_POOL = 16
CHUNK_ROWS = 512
DH_WIDE = 1024
FWD_TILES = 2
N_DEV = 8
LANES = 128

ADAM_LR, ADAM_B1, ADAM_B2, ADAM_EPS, ADAM_WD, ADAM_STEP = 0.001, 0.9, 0.999, 1e-08, 0.01, 10

MESH = pl.DeviceIdType.MESH
ANY = pl.BlockSpec(memory_space=pl.ANY)
VMEM = pl.BlockSpec(memory_space=pltpu.VMEM)
MIB = 2 ** 20


def _pcall(body, *, name, out_shape, grid=None, in_specs=None, out_specs=None, scratch=(), vmem_mb=None,
           aliases=None, prefetch=0):
    kw = {}
    if prefetch:
        kw["grid_spec"] = pltpu.PrefetchScalarGridSpec(num_scalar_prefetch=prefetch, grid=grid, in_specs=in_specs,
                                                       out_specs=out_specs, scratch_shapes=list(scratch))
    else:
        if grid is not None:
            kw["grid"] = grid
        if in_specs is not None:
            kw["in_specs"] = in_specs
        if out_specs is not None:
            kw["out_specs"] = out_specs
        if scratch:
            kw["scratch_shapes"] = list(scratch)
    if aliases:
        kw["input_output_aliases"] = aliases
    params = pltpu.CompilerParams(vmem_limit_bytes=None if vmem_mb is None else vmem_mb * MIB)
    return pl.pallas_call(body, name=name, out_shape=out_shape, compiler_params=params, **kw)


def _sds(shape, dtype):
    return jax.ShapeDtypeStruct(tuple(shape), dtype)


def _sigmoid(z):
    return pl.reciprocal(1.0 + jnp.exp(-z), approx=True)


def _silu(z):
    return z * _sigmoid(z)


def _silu_and_grad(z):
    s = _sigmoid(z)
    return z * s, s * (1.0 + z * (1.0 - s))


def _place():
    return lax.axis_index("x"), lax.axis_index("y"), lax.axis_index("c")


def _gather8(blk, name):
    def body(x_ref, o_ref, ssem, rsem):
        x, y, c = _place()
        me = 4 * x + 2 * y + c
        o_ref[me] = x_ref[...]
        sends = []
        for k in range(1, N_DEV):
            px = 1 - x if k & 4 else x
            py = 1 - y if k & 2 else y
            pc = 1 - c if k & 1 else c
            cp = pltpu.make_async_remote_copy(src_ref=x_ref, dst_ref=o_ref.at[me], send_sem=ssem.at[k - 1],
                                              recv_sem=rsem.at[k - 1], device_id=(px, py, pc), device_id_type=MESH)
            cp.start()
            sends.append((cp, 4 * px + 2 * py + pc))
        for k, (cp, peer) in enumerate(sends):
            pltpu.make_async_remote_copy(src_ref=x_ref, dst_ref=o_ref.at[peer], send_sem=ssem.at[k],
                                         recv_sem=rsem.at[k], device_id=(x, y, c), device_id_type=MESH).wait_recv()
        for cp, _ in sends:
            cp.wait_send()

    return _pcall(body, name=name, out_shape=_sds((N_DEV,) + blk.shape, blk.dtype), in_specs=[VMEM], out_specs=VMEM,
                  scratch=[pltpu.SemaphoreType.DMA((N_DEV - 1,)), pltpu.SemaphoreType.DMA((N_DEV - 1,))])(blk)


def _allreduce8(bufs, name):
    n = len(bufs)
    rbs = [b.shape[0] // N_DEV for b in bufs]
    assert all(rb * N_DEV == b.shape[0] and rb % 8 == 0 for rb, b in zip(rbs, bufs))

    def body(*refs):
        xs, outs, stages = refs[:n], refs[n:2 * n], refs[2 * n:3 * n]
        ssem, rsem = refs[3 * n:]
        x, y, c = _place()
        me = 4 * x + 2 * y + c
        peers = []
        for k in range(1, N_DEV):
            px = 1 - x if k & 4 else x
            py = 1 - y if k & 2 else y
            pc = 1 - c if k & 1 else c
            peers.append(((px, py, pc), 4 * px + 2 * py + pc))

        def blk(t, ref, idx):
            return ref.at[pl.ds(pl.multiple_of(idx * rbs[t], 8), rbs[t]), :]

        def copy(t, phase, k, src, dst, dev):
            return pltpu.make_async_remote_copy(src_ref=src, dst_ref=dst, send_sem=ssem.at[t, phase, k],
                                                recv_sem=rsem.at[t, phase, k], device_id=dev, device_id_type=MESH)

        scatter = [copy(t, 0, k, blk(t, xs[t], pidx), stages[t].at[me], dev)
                   for t in range(n) for k, (dev, pidx) in enumerate(peers)]
        for cp in scatter:
            cp.start()
        gather = []
        for t in range(n):
            stages[t][me] = blk(t, xs[t], me)[...]
            for k, (dev, pidx) in enumerate(peers):
                copy(t, 0, k, blk(t, xs[t], pidx), stages[t].at[pidx], dev).wait_recv()
            total = stages[t][0]
            for j in range(1, N_DEV):
                total = total + stages[t][j]
            blk(t, outs[t], me)[...] = total
            sends = [copy(t, 1, k, blk(t, outs[t], me), blk(t, outs[t], me), dev) for k, (dev, pidx) in enumerate(peers)]
            for cp in sends:
                cp.start()
            gather += sends
        for t in range(n):
            for k, (dev, pidx) in enumerate(peers):
                copy(t, 1, k, blk(t, outs[t], pidx), blk(t, outs[t], pidx), dev).wait_recv()
        for cp in scatter + gather:
            cp.wait_send()

    return _pcall(body, name=name, out_shape=[_sds(b.shape, f32) for b in bufs], in_specs=[VMEM] * n, out_specs=[VMEM] * n,
                  scratch=[pltpu.VMEM((N_DEV, rb, LANES), f32) for rb in rbs]
                  + [pltpu.SemaphoreType.DMA((n, 2, N_DEV - 1)), pltpu.SemaphoreType.DMA((n, 2, N_DEV - 1))])(*bufs)


def _other_chips(x, y):
    return [((1 - x, y), 2 * (1 - x) + y), ((x, 1 - y), 2 * x + (1 - y)), ((1 - x, 1 - y), 2 * (1 - x) + (1 - y))]


HBM = pl.BlockSpec(memory_space=pltpu.HBM)
SEM = pl.BlockSpec(memory_space=pltpu.SEMAPHORE)
EFFECT = pltpu.SideEffectType.DATAFLOW_SIDE_EFFECTING


def _in_hbm(a):
    return pltpu.with_memory_space_constraint(a, pltpu.HBM)


def _ag_start(layers, after, name):
    flat = [t for lay in layers for t in lay]
    n, nl = len(flat), len(layers)

    def body(*refs):
        src = refs[:n]
        sems = refs[n + 1:n + 1 + 2 * nl]
        token = refs[-1]
        x, y, c = _place()
        s_me = 2 * x + y
        t = 0
        for i, lay in enumerate(layers):
            for k in range(len(lay)):
                for j, ((px, py), _) in enumerate(_other_chips(x, y)):
                    pltpu.make_async_remote_copy(src_ref=src[t].at[s_me, c], dst_ref=src[t].at[s_me, c],
                                                 send_sem=sems[2 * i].at[3 * k + j], recv_sem=sems[2 * i + 1].at[3 * k + j],
                                                 device_id=(px, py, c), device_id_type=MESH).start()
                t += 1
        token[...] = jnp.zeros_like(token)

    sem_shapes = [pltpu.SemaphoreType.DMA((3 * len(lay),)) for lay in layers for _ in range(2)]
    out_shape = sem_shapes + [pltpu.HBM(t.shape, t.dtype) for t in flat] + [_sds((8, LANES), f32)]
    outs = pl.pallas_call(
        body, name=name, out_shape=out_shape, in_specs=[HBM] * n + [ANY],
        out_specs=[SEM] * (2 * nl) + [HBM] * n + [VMEM], input_output_aliases={t: 2 * nl + t for t in range(n)},
        compiler_params=pltpu.CompilerParams(has_side_effects=EFFECT))(*[_in_hbm(t) for t in flat], after)
    sems = [(outs[2 * i], outs[2 * i + 1]) for i in range(nl)]
    thru, t = [], 2 * nl
    for lay in layers:
        thru.append(list(outs[t:t + len(lay)]))
        t += len(lay)
    return sems, thru, outs[-1]


def _ag_wait(inflight, sems, after, name):
    n = len(inflight)

    def body(*refs):
        src, ssem, rsem = refs[:n], refs[n], refs[n + 1]
        x, y, c = _place()
        s_me = 2 * x + y
        for k in range(n):
            for j, (_, s_p) in enumerate(_other_chips(x, y)):
                cp = pltpu.make_async_remote_copy(src_ref=src[k].at[s_me, c], dst_ref=src[k].at[s_p, c],
                                                  send_sem=ssem.at[3 * k + j], recv_sem=rsem.at[3 * k + j],
                                                  device_id=(x, y, c), device_id_type=MESH)
                cp.wait_send()
                cp.wait_recv()

    return pl.pallas_call(
        body, name=name, out_shape=[pltpu.HBM(t.shape, t.dtype) for t in inflight],
        in_specs=[HBM] * n + [SEM, SEM, ANY], out_specs=[HBM] * n, input_output_aliases={t: t for t in range(n)},
        compiler_params=pltpu.CompilerParams(has_side_effects=EFFECT))(*inflight, sems[0], sems[1], after)


def _ag_forward(arrived, name):
    n = len(arrived)

    def body(*refs):
        o = refs[n:2 * n]
        ssem, rsem = refs[2 * n:]
        x, y, c = _place()

        def copy(t, j, s, half, dev):
            return pltpu.make_async_remote_copy(src_ref=o[t].at[s, c], dst_ref=o[t].at[s, half], send_sem=ssem.at[t, j],
                                                recv_sem=rsem.at[t, j], device_id=dev, device_id_type=MESH)

        chips = _other_chips(x, y)
        sends = [copy(t, j, s_p, c, (x, y, 1 - c)) for t in range(n) for j, (_, s_p) in enumerate(chips)]
        for cp in sends:
            cp.start()
        for t in range(n):
            for j, (_, s_p) in enumerate(chips):
                copy(t, j, s_p, 1 - c, (x, y, c)).wait_recv()
        for cp in sends:
            cp.wait_send()

    return _pcall(body, name=name, out_shape=[_sds(p.shape, bf16) for p in arrived], in_specs=[ANY] * n,
                  out_specs=[ANY] * n, aliases={t: t for t in range(n)},
                  scratch=[pltpu.SemaphoreType.DMA((n, 3)), pltpu.SemaphoreType.DMA((n, 3))])(*arrived)


def _agf_start(arrived, name):
    n = len(arrived)

    def body(*refs):
        o = refs[:n]
        ssem, rsem, token = refs[n], refs[n + 1], refs[-1]
        x, y, c = _place()
        for t in range(n):
            for j, (_, s_p) in enumerate(_other_chips(x, y)):
                pltpu.make_async_remote_copy(src_ref=o[t].at[s_p, c], dst_ref=o[t].at[s_p, c],
                                             send_sem=ssem.at[3 * t + j], recv_sem=rsem.at[3 * t + j],
                                             device_id=(x, y, 1 - c), device_id_type=MESH).start()
        token[...] = jnp.zeros_like(token)

    out_shape = ([pltpu.SemaphoreType.DMA((3 * n,))] * 2 + [pltpu.HBM(a.shape, bf16) for a in arrived]
                 + [_sds((8, LANES), f32)])
    outs = pl.pallas_call(
        body, name=name, out_shape=out_shape, in_specs=[HBM] * n, out_specs=[SEM, SEM] + [HBM] * n + [VMEM],
        input_output_aliases={t: 2 + t for t in range(n)},
        compiler_params=pltpu.CompilerParams(has_side_effects=EFFECT))(*[_in_hbm(a) for a in arrived])
    return (outs[0], outs[1]), list(outs[2:2 + n]), outs[-1]


def _agf_wait(sems, inflight, after, name):
    n = len(inflight)

    def body(*refs):
        o, ssem, rsem = refs[:n], refs[n], refs[n + 1]
        x, y, c = _place()
        for t in range(n):
            for j, (_, s_p) in enumerate(_other_chips(x, y)):
                cp = pltpu.make_async_remote_copy(src_ref=o[t].at[s_p, c], dst_ref=o[t].at[s_p, 1 - c],
                                                  send_sem=ssem.at[3 * t + j], recv_sem=rsem.at[3 * t + j],
                                                  device_id=(x, y, c), device_id_type=MESH)
                cp.wait_send()
                cp.wait_recv()

    return pl.pallas_call(
        body, name=name, out_shape=[pltpu.HBM(a.shape, bf16) for a in inflight],
        in_specs=[HBM] * n + [SEM, SEM, ANY], out_specs=[HBM] * n, input_output_aliases={t: t for t in range(n)},
        compiler_params=pltpu.CompilerParams(has_side_effects=EFFECT))(*inflight, sems[0], sems[1], after)


def _rs_pair_start(grads, name):
    n = len(grads)

    def body(*refs):
        g, theirs = refs[:n], refs[n:2 * n]
        ssem, rsem, token = refs[2 * n], refs[2 * n + 1], refs[-1]
        x, y, c = _place()
        for t in range(n):
            pltpu.make_async_remote_copy(src_ref=g[t].at[:, 1 - c], dst_ref=theirs[t], send_sem=ssem.at[t],
                                         recv_sem=rsem.at[t], device_id=(x, y, 1 - c), device_id_type=MESH).start()
        token[...] = jnp.zeros_like(token)

    lands = [lax.empty((4,) + g.shape[2:], bf16) for g in grads]
    out_shape = ([pltpu.SemaphoreType.DMA((n,))] * 2 + [pltpu.HBM(g.shape, bf16) for g in grads]
                 + [pltpu.HBM(q.shape, bf16) for q in lands] + [_sds((8, LANES), f32)])
    outs = pl.pallas_call(
        body, name=name, out_shape=out_shape, in_specs=[HBM] * (2 * n), out_specs=[SEM, SEM] + [HBM] * (2 * n) + [VMEM],
        input_output_aliases={t: 2 + t for t in range(2 * n)},
        compiler_params=pltpu.CompilerParams(has_side_effects=EFFECT))(*[_in_hbm(a) for a in list(grads) + lands])
    return (outs[0], outs[1]), list(outs[2:2 + n]), list(outs[2 + n:2 + 2 * n]), outs[-1]


def _rs_pair_wait(sems, grads, lands, after, name):
    n = len(grads)

    def body(*refs):
        g, theirs = refs[:n], refs[n:2 * n]
        ssem, rsem = refs[2 * n], refs[2 * n + 1]
        x, y, c = _place()
        for t in range(n):
            cp = pltpu.make_async_remote_copy(src_ref=g[t].at[:, 1 - c], dst_ref=theirs[t], send_sem=ssem.at[t],
                                              recv_sem=rsem.at[t], device_id=(x, y, c), device_id_type=MESH)
            cp.wait_send()
            cp.wait_recv()

    outs = pl.pallas_call(
        body, name=name, out_shape=[pltpu.HBM(a.shape, bf16) for a in list(grads) + list(lands)],
        in_specs=[HBM] * (2 * n) + [SEM, SEM, ANY], out_specs=[HBM] * (2 * n),
        input_output_aliases={t: t for t in range(2 * n)},
        compiler_params=pltpu.CompilerParams(has_side_effects=EFFECT))(*grads, *lands, sems[0], sems[1], after)
    return list(outs[:n]), list(outs[n:])


def _rs_chip_start(pairs, name):
    n = len(pairs)

    def body(*refs):
        p, q = refs[:n], refs[n:2 * n]
        ssem, rsem, token = refs[2 * n], refs[2 * n + 1], refs[-1]
        x, y, c = _place()
        for t in range(n):
            for j, ((px, py), s_p) in enumerate(_other_chips(x, y)):
                pltpu.make_async_remote_copy(src_ref=p[t].at[s_p], dst_ref=q[t].at[j], send_sem=ssem.at[3 * t + j],
                                             recv_sem=rsem.at[3 * t + j], device_id=(px, py, c), device_id_type=MESH).start()
        token[...] = jnp.zeros_like(token)

    lands = [lax.empty((3,) + p.shape[1:], bf16) for p in pairs]
    out_shape = ([pltpu.SemaphoreType.DMA((3 * n,))] * 2 + [pltpu.HBM(p.shape, bf16) for p in pairs]
                 + [pltpu.HBM(q.shape, bf16) for q in lands] + [_sds((8, LANES), f32)])
    outs = pl.pallas_call(
        body, name=name, out_shape=out_shape, in_specs=[HBM] * (2 * n), out_specs=[SEM, SEM] + [HBM] * (2 * n) + [VMEM],
        input_output_aliases={t: 2 + t for t in range(2 * n)},
        compiler_params=pltpu.CompilerParams(has_side_effects=EFFECT))(*[_in_hbm(a) for a in list(pairs) + lands])
    return (outs[0], outs[1]), list(outs[2:2 + n]), list(outs[2 + n:2 + 2 * n]), outs[-1]


def _rs_chip_wait(sems, pairs, lands, after, name):
    n = len(pairs)

    def body(*refs):
        p, q = refs[:n], refs[n:2 * n]
        ssem, rsem = refs[2 * n], refs[2 * n + 1]
        x, y, c = _place()
        for t in range(n):
            for j, (_, s_p) in enumerate(_other_chips(x, y)):
                cp = pltpu.make_async_remote_copy(src_ref=p[t].at[s_p], dst_ref=q[t].at[j], send_sem=ssem.at[3 * t + j],
                                                  recv_sem=rsem.at[3 * t + j], device_id=(x, y, c), device_id_type=MESH)
                cp.wait_send()
                cp.wait_recv()

    outs = pl.pallas_call(
        body, name=name, out_shape=[pltpu.HBM(a.shape, bf16) for a in list(pairs) + list(lands)],
        in_specs=[HBM] * (2 * n) + [SEM, SEM, ANY], out_specs=[HBM] * (2 * n),
        input_output_aliases={t: t for t in range(2 * n)},
        compiler_params=pltpu.CompilerParams(has_side_effects=EFFECT))(*pairs, *lands, sems[0], sems[1], after)
    return list(outs[:n]), list(outs[n:])


def _rs_half_exchange(halves, name):
    n = len(halves)

    def body(*refs):
        o = refs[n:2 * n]
        ssem, rsem = refs[2 * n:]
        x, y, c = _place()

        def copy(t, half, dev):
            return pltpu.make_async_remote_copy(src_ref=o[t].at[c], dst_ref=o[t].at[half], send_sem=ssem.at[t],
                                                recv_sem=rsem.at[t], device_id=dev, device_id_type=MESH)

        sends = [copy(t, c, (x, y, 1 - c)) for t in range(n)]
        for cp in sends:
            cp.start()
        for t in range(n):
            copy(t, 1 - c, (x, y, c)).wait_recv()
        for cp in sends:
            cp.wait_send()

    return _pcall(body, name=name, out_shape=[_sds(h.shape, h.dtype) for h in halves], in_specs=[ANY] * n,
                  out_specs=[ANY] * n, aliases={t: t for t in range(n)},
                  scratch=[pltpu.SemaphoreType.DMA((n,)), pltpu.SemaphoreType.DMA((n,))])(*halves)


def _row_spec(tm, cols):
    return pl.BlockSpec((tm, cols), lambda i: (i, 0))


def _vec_spec(cols, rows=1):
    return pl.BlockSpec((rows, cols), lambda i: (0, 0))


def _modulated_norm(xv, g, shift, scale):
    r = lax.rsqrt(jnp.mean(xv * xv, axis=-1, keepdims=True) + EPS)
    return (((xv * r) * g) * (1.0 + scale) + shift).astype(bf16)


def _hnorm(x, g, shift, scale):
    T, tm = x.shape[0], 256

    def body(x_ref, g_ref, sh_ref, sc_ref, h_ref):
        h_ref[...] = _modulated_norm(x_ref[...], g_ref[...], sh_ref[...], sc_ref[...])

    return _pcall(body, name="hnorm", out_shape=_sds((T, D), bf16), grid=(T // tm,),
                  in_specs=[_row_spec(tm, D), _vec_spec(D), _vec_spec(D), _vec_spec(D)],
                  out_specs=_row_spec(tm, D))(x, g, shift, scale)


def _out_proj(y2, wo, x, gate, nxt=None):
    T, tm = x.shape[0], 512

    def body(y_ref, w_ref, x_ref, g_ref, *rest):
        o = jnp.dot(y_ref[0], w_ref[0], preferred_element_type=f32)
        o = o + jnp.dot(y_ref[1], w_ref[1], preferred_element_type=f32)
        xo = x_ref[...] + g_ref[...] * o
        if nxt is None:
            xo_ref, o_ref = rest
        else:
            ng_ref, nsh_ref, nsc_ref, xo_ref, o_ref, h_ref = rest
            h_ref[...] = _modulated_norm(xo, ng_ref[...], nsh_ref[...], nsc_ref[...])
        o_ref[...] = o
        xo_ref[...] = xo

    extra = [] if nxt is None else list(nxt)
    n_out = 2 if nxt is None else 3
    return _pcall(body, name="out_proj", out_shape=[_sds((T, D), f32), _sds((T, D), f32), _sds((T, D), bf16)][:n_out],
                  grid=(T // tm,),
                  in_specs=[pl.BlockSpec((2, tm, D), lambda i: (0, i, 0)), pl.BlockSpec((2, D, D), lambda i: (0, 0, 0)),
                            _row_spec(tm, D), _vec_spec(D)] + [_vec_spec(D)] * len(extra),
                  out_specs=[_row_spec(tm, D)] * n_out, vmem_mb=48)(y2, wo, x, gate, *extra)


def _gate_bwd_tile(dx, o_ref, gate_ref, dob_ref, dgate_ref):
    dob_ref[...] = (dx * gate_ref[...]).astype(bf16)
    dgate_ref[...] += jnp.sum(dx * o_ref[...], axis=0, keepdims=True)


def _loss_bwd(x, target, g, o, gate):
    T, tm = x.shape[0], 256

    def body(x_ref, t_ref, g_ref, o_ref, gate_ref, dx_ref, loss_ref, dg_ref, dob_ref, dgate_ref):
        @pl.when(pl.program_id(0) == 0)
        def _():
            loss_ref[...] = jnp.zeros_like(loss_ref)
            dg_ref[...] = jnp.zeros_like(dg_ref)
            dgate_ref[...] = jnp.zeros_like(dgate_ref)

        xv, gv = x_ref[...], g_ref[...]
        r = lax.rsqrt(jnp.mean(xv * xv, axis=-1, keepdims=True) + EPS)
        xn = xv * r
        err = xn * gv - t_ref[...]
        dy = err * (1.0 / D)
        dxn = dy * gv
        dx = r * (dxn - xn * jnp.mean(dxn * xn, axis=-1, keepdims=True))
        dx_ref[...] = dx
        dg_ref[...] += jnp.sum(dy * xn, axis=0, keepdims=True)
        loss_ref[...] += (0.5 / D) * jnp.sum(jnp.sum(err * err, axis=1, keepdims=True), axis=0, keepdims=True)
        _gate_bwd_tile(dx, o_ref, gate_ref, dob_ref, dgate_ref)

    return _pcall(body, name="loss_bwd",
                  out_shape=[_sds((T, D), f32), _sds((1, 1), f32), _sds((1, D), f32), _sds((T, D), bf16), _sds((1, D), f32)],
                  grid=(T // tm,),
                  in_specs=[_row_spec(tm, D), _row_spec(tm, D), _vec_spec(D), _row_spec(tm, D), _vec_spec(D)],
                  out_specs=[_row_spec(tm, D), pl.BlockSpec((1, 1), lambda i: (0, 0)), _vec_spec(D), _row_spec(tm, D),
                             _vec_spec(D)])(x, target, g, o, gate)


def _norm_bwd(x, dh, gin, g, scale, below=None):
    T, tm = x.shape[0], 256

    def body(x_ref, dh_ref, gin_ref, g_ref, sc_ref, *rest):
        if below is None:
            dx_ref, st_ref = rest
        else:
            o_ref, gate_ref, dx_ref, st_ref, dob_ref, dgate_ref = rest

        @pl.when(pl.program_id(0) == 0)
        def _():
            st_ref[...] = jnp.zeros_like(st_ref)
            if below is not None:
                dgate_ref[...] = jnp.zeros_like(dgate_ref)

        xv, gv, dhv = x_ref[...], g_ref[...], dh_ref[...]
        r = lax.rsqrt(jnp.mean(xv * xv, axis=-1, keepdims=True) + EPS)
        xn = xv * r
        da = dhv * (1.0 + sc_ref[...])
        dxn = da * gv
        dx = gin_ref[...] + r * (dxn - xn * jnp.mean(dxn * xn, axis=-1, keepdims=True))
        dx_ref[...] = dx
        st_ref[0:1, :] += jnp.sum(dhv, axis=0, keepdims=True)
        st_ref[1:2, :] += jnp.sum(dhv * (xn * gv), axis=0, keepdims=True)
        st_ref[2:3, :] += jnp.sum(da * xn, axis=0, keepdims=True)
        if below is not None:
            _gate_bwd_tile(dx, o_ref, gate_ref, dob_ref, dgate_ref)

    out_shape = [_sds((T, D), f32), _sds((8, D), f32)]
    in_specs = [_row_spec(tm, D), _row_spec(tm, D), _row_spec(tm, D), _vec_spec(D), _vec_spec(D)]
    out_specs = [_row_spec(tm, D), _vec_spec(D, 8)]
    args = [x, dh, gin, g, scale]
    if below is not None:
        out_shape += [_sds((T, D), bf16), _sds((1, D), f32)]
        in_specs += [_row_spec(tm, D), _vec_spec(D)]
        out_specs += [_row_spec(tm, D), _vec_spec(D)]
        args += list(below)
    return _pcall(body, name="norm_bwd", out_shape=out_shape, grid=(T // tm,), in_specs=in_specs,
                  out_specs=out_specs)(*args)


STEPS = 4


def _cast_place(place, ws, layer, after=None):
    n = len(ws)

    def body(place_ref, *refs):
        for t in range(n):
            refs[-n + t][...] = refs[t][...].astype(bf16)

    def tile(w):
        return w.shape[1] // STEPS, w.shape[2]

    extra = [] if after is None else [after]
    return _pcall(body, name="cast_place", out_shape=[_sds((4,) + w.shape[1:], bf16) for w in ws], grid=(STEPS,),
                  prefetch=1,
                  in_specs=[pl.BlockSpec((None,) + tile(w), lambda i, pr: (layer, i, 0)) for w in ws] + [ANY] * len(extra),
                  out_specs=[pl.BlockSpec((None,) + tile(w), lambda i, pr: (pr[0], i, 0)) for w in ws],
                  vmem_mb=40)(place, *ws, *extra)


def _rs_add(place, grads, theirs):
    n = len(grads)

    def body(place_ref, *refs):
        for t in range(n):
            refs[2 * n + t][...] = (refs[t][...].astype(f32) + refs[n + t][...].astype(f32)).astype(bf16)

    mine = [pl.BlockSpec((None, None) + g.shape[2:], lambda s, pr: (s, pr[1], 0, 0)) for g in grads]
    shard = [pl.BlockSpec((None,) + q.shape[1:], lambda s, pr: (s, 0, 0)) for q in theirs]
    return _pcall(body, name="rs_add", out_shape=[_sds(q.shape, bf16) for q in theirs], grid=(4,), prefetch=1,
                  in_specs=mine + shard, out_specs=shard, vmem_mb=40)(place, *grads, *theirs)


def _rs_sum(place, pairs, slots):
    n, steps = len(pairs), 2

    def body(place_ref, *refs):
        for t in range(n):
            p_ref, q_ref = refs[t], refs[n + t]
            total = ((p_ref[...].astype(f32) + q_ref[0].astype(f32)) + q_ref[1].astype(f32)) + q_ref[2].astype(f32)
            refs[2 * n + t][...] = total.astype(bf16)

    def tile(q):
        return q.shape[1] // steps, q.shape[2]

    return _pcall(body, name="rs_sum", out_shape=[_sds((2,) + q.shape[1:], bf16) for q in slots], grid=(steps,),
                  prefetch=1,
                  in_specs=[pl.BlockSpec((None,) + tile(q), lambda i, pr: (pr[0], i, 0)) for q in slots]
                  + [pl.BlockSpec((3,) + tile(q), lambda i, pr: (0, i, 0)) for q in slots],
                  out_specs=[pl.BlockSpec((None,) + tile(q), lambda i, pr: (pr[1], i, 0)) for q in slots],
                  vmem_mb=40)(place, *pairs, *slots)


def _adamw_math(w, g, m, v):
    m = ADAM_B1 * m + (1.0 - ADAM_B1) * g
    v = ADAM_B2 * v + (1.0 - ADAM_B2) * jnp.square(g)
    m_hat = m / (1.0 - ADAM_B1 ** ADAM_STEP)
    v_hat = v / (1.0 - ADAM_B2 ** ADAM_STEP)
    delta = -ADAM_LR * (m_hat / (jnp.sqrt(v_hat) + ADAM_EPS) + ADAM_WD * w)
    return delta, m, v


def _adamw_layer(layer, items):
    n = len(items)

    def body(*refs):
        outs = refs[-4 * n:]
        for t in range(n):
            w_ref, g_ref, m_ref, v_ref = refs[4 * t:4 * t + 4]
            g = g_ref[...].astype(f32)
            outs[4 * t][...] = g
            outs[4 * t + 1][...], outs[4 * t + 2][...], outs[4 * t + 3][...] = _adamw_math(
                w_ref[...], g, m_ref[...], v_ref[...])

    args, in_specs, out_specs, out_shape = [], [], [], []
    for w, g, m, v, _ in items:
        tr, cols = w.shape[1] // STEPS, w.shape[2]
        spec = pl.BlockSpec((None, tr, cols), lambda i: (layer, i, 0))
        args += [w, g, m, v]
        in_specs += [spec, pl.BlockSpec((tr, cols), lambda i: (i, 0)), spec, spec]
        out_specs += [spec] * 4
        out_shape += [_sds(w.shape, f32)] * 4
    aliases = {}
    for t, it in enumerate(items):
        if it[4] is not None:
            for k in range(4):
                aliases[len(args)] = 4 * t + k
                args.append(it[4][k])
                in_specs.append(ANY)
    res = _pcall(body, name="adamw", out_shape=out_shape, grid=(STEPS,), in_specs=in_specs, out_specs=out_specs,
                 aliases=aliases, vmem_mb=48)(*args)
    return [tuple(res[4 * t:4 * t + 4]) for t in range(n)]


def _adamw_small(items):
    n = len(items)

    def body(*refs):
        ins, outs = refs[:4 * n], refs[4 * n:]
        for t in range(n):
            w_ref, g_ref, m_ref, v_ref = ins[4 * t:4 * t + 4]
            if len(g_ref.shape) == len(w_ref.shape) + 1:
                g = g_ref[0]
                for b in range(1, g_ref.shape[0]):
                    g = g + g_ref[b]
            else:
                g = g_ref[...]
            d, m, v = _adamw_math(w_ref[...], g, m_ref[...], v_ref[...])
            outs[4 * t][...], outs[4 * t + 1][...], outs[4 * t + 2][...], outs[4 * t + 3][...] = g, d, m, v

    out_shape = [_sds(w.shape, f32) for (w, _, _, _) in items for _ in range(4)]
    flat = [a for it in items for a in it]
    res = _pcall(body, name="adamw_small", out_shape=out_shape, in_specs=[VMEM] * (4 * n),
                 out_specs=[VMEM] * (4 * n))(*flat)
    return [tuple(res[4 * t:4 * t + 4]) for t in range(n)]


NN = ((1,), (0,))
NT = ((1,), (1,))
TN = ((0,), (0,))


def _mm(name, a, b, *, grid, a_spec, b_spec, out_shape, out_spec, dims, vmem_mb=48):
    def body(a_ref, b_ref, o_ref):
        r = lax.dot_general(a_ref[...], b_ref[...], (dims, ((), ())), preferred_element_type=f32)
        o_ref[...] = r.astype(o_ref.dtype)

    return _pcall(body, name=name, out_shape=out_shape, grid=grid, in_specs=[a_spec, b_spec], out_specs=out_spec,
                  vmem_mb=vmem_mb)(a, b)


def _whole(shape):
    return pl.BlockSpec(shape, lambda j: (0,) * len(shape))


def _split_spec(rows, tile, per_split):
    return pl.BlockSpec((None, rows, tile), lambda j: (j // per_split, 0, j % per_split))


class _Proj:
    def __init__(self, n, splits, tile):
        self.n, self.splits, self.tile = n, splits, tile
        self.steps = n // tile
        self.w_per = n // 4 // tile
        self.a_per = n // splits // tile
        assert self.w_per * tile * 4 == n and self.a_per * tile * splits == n

    def fwd(self, hb, wg):
        T = hb.shape[0]
        sub, tile, w_per = FWD_TILES, self.tile, self.w_per
        wide = sub * tile
        a_per = self.n // self.splits // wide
        assert a_per * wide * self.splits == self.n

        def w_tile(q):
            return pl.BlockSpec((None, D, tile), lambda j: ((sub * j + q) // w_per, 0, (sub * j + q) % w_per))

        def body(a_ref, *rest):
            w = jnp.concatenate([rest[q][...] for q in range(sub)], axis=1)
            rest[sub][...] = jnp.dot(a_ref[...], w, preferred_element_type=f32).astype(bf16)

        return _pcall(body, name="proj_fwd", out_shape=_sds((self.splits, T, self.n // self.splits), bf16),
                      grid=(self.n // wide,), in_specs=[_whole((T, D))] + [w_tile(q) for q in range(sub)],
                      out_specs=pl.BlockSpec((None, T, wide), lambda j: (j // a_per, 0, j % a_per)),
                      vmem_mb=48)(hb, *([wg] * sub))

    def dw(self, hb, dp):
        T = hb.shape[0]
        return _mm("proj_dw", hb, dp, grid=(self.steps,), a_spec=_whole((T, D)),
                   b_spec=_split_spec(T, self.tile, self.a_per), out_shape=_sds((4, D, self.n // 4), bf16),
                   out_spec=_split_spec(D, self.tile, self.w_per), dims=TN)

    def dh(self, dp, wg):
        T = dp.shape[1]
        sub, tile, w_per = DH_WIDE // self.tile, self.tile, self.w_per
        a_per = self.n // self.splits // DH_WIDE
        assert sub * tile == DH_WIDE and a_per * DH_WIDE * self.splits == self.n

        def w_tile(q):
            return pl.BlockSpec((None, D, tile), lambda k: ((sub * k + q) // w_per, 0, (sub * k + q) % w_per))

        def body(a_ref, *rest):
            o_ref = rest[sub]
            w = jnp.concatenate([rest[q][...] for q in range(sub)], axis=1)
            r = lax.dot_general(a_ref[...], w, (NT, ((), ())), preferred_element_type=f32)

            @pl.when(pl.program_id(0) == 0)
            def _():
                o_ref[...] = r

            @pl.when(pl.program_id(0) > 0)
            def _():
                o_ref[...] += r

        return _pcall(body, name="proj_dh", out_shape=_sds((T, D), f32), grid=(self.n // DH_WIDE,),
                      in_specs=[pl.BlockSpec((None, T, DH_WIDE), lambda k: (k // a_per, 0, k % a_per))]
                      + [w_tile(q) for q in range(sub)],
                      out_specs=_whole((T, D)), vmem_mb=48)(dp, *([wg] * sub))


EVEN_PROJ = _Proj(7 * D, 7, 256)
ODD_PROJ = _Proj(4 * D, 2, 512)


def _dy_mm(dob, wo):
    T = dob.shape[0]
    return _mm("out_dy", dob, wo, grid=(4,), a_spec=_whole((T, D)),
               b_spec=pl.BlockSpec((None, 512, D), lambda j: (j, 0, 0)), out_shape=_sds((2, T, D), f32),
               out_spec=_split_spec(T, 512, 2), dims=NT)


def _dwo_mm(y2, dob):
    T = dob.shape[0]
    return _mm("out_dw", y2, dob, grid=(4,), a_spec=_split_spec(T, 512, 2), b_spec=_whole((T, D)),
               out_shape=_sds((4, 512, D), bf16), out_spec=pl.BlockSpec((None, 512, D), lambda j: (j, 0, 0)), dims=TN)


def _head_spec(lead, T):
    return pl.BlockSpec((lead, T, HEAD), lambda h: (0, 0, h))


def _head_vec(rows):
    return pl.BlockSpec((rows, HEAD), lambda h: (0, h))


_HEAD_MAT = pl.BlockSpec((None, HEAD, HEAD), lambda h: (h, 0, 0))


def _causal():
    return lax.broadcasted_iota(jnp.int32, (HEAD, HEAD), 0) >= lax.broadcasted_iota(jnp.int32, (HEAD, HEAD), 1)


def _layernorm_head(v):
    mu = jnp.mean(v, axis=-1, keepdims=True)
    d = v - mu
    rstd = lax.rsqrt(jnp.mean(d * d, axis=-1, keepdims=True) + EPS)
    return d * rstd, rstd


def _even_fwd(p7, conv_w, ln_g, ln_b, sgu_w, sgu_bias):
    T, C = p7.shape[1], CHUNK_ROWS

    def body(p_ref, cw_ref, lg_ref, lb_ref, w_ref, b_ref, y_ref):
        w0, w1, w2 = cw_ref[0:1, :], cw_ref[1:2, :], cw_ref[2:3, :]
        wm = jnp.where(_causal(), w_ref[...], 0.0).astype(bf16)
        bias, lg, lb = b_ref[...], lg_ref[...], lb_ref[...]

        def step(i, halo):
            rows = pl.ds(pl.multiple_of(i * C, C), C)
            ah, ab, ac, az, u, v, zb = (p_ref[k, rows, :].astype(f32) for k in range(7))
            tt = ac * ah
            ext = jnp.concatenate([halo, tt], axis=0)
            cv = w2 * tt + w1 * pltpu.roll(ext, 1, 0)[HALO_CONV:] + w0 * pltpu.roll(ext, 2, 0)[HALO_CONV:]
            y_ref[0, rows, :] = (ab * cv * _silu(az)).astype(bf16)
            vhat, _ = _layernorm_head(v)
            vn = (vhat * lg + lb).astype(bf16)
            mix = jnp.concatenate([jnp.dot(wm, vn[k * HEAD:(k + 1) * HEAD], preferred_element_type=f32) + bias
                                   for k in range(C // HEAD)], axis=0)
            y_ref[1, rows, :] = (u * mix * _silu(zb)).astype(bf16)
            return tt[C - HALO_CONV:]

        lax.fori_loop(0, T // C, step, jnp.zeros((HALO_CONV, HEAD), f32))

    return _pcall(body, name="even_fwd", out_shape=_sds((2, T, D), bf16), grid=(NH,),
                  in_specs=[_head_spec(7, T), _head_vec(3), _head_vec(1), _head_vec(1), _HEAD_MAT, _HEAD_MAT],
                  out_specs=_head_spec(2, T), vmem_mb=32)(p7, conv_w, ln_g, ln_b, sgu_w, sgu_bias)


def _even_bwd(p7, dy2, conv_w, ln_g, ln_b, sgu_w, sgu_bias):
    T, C = p7.shape[1], CHUNK_ROWS
    n_chunks = T // C

    def body(p_ref, dy_ref, cw_ref, lg_ref, lb_ref, w_ref, b_ref,
             dp_ref, dcw_ref, dlg_ref, dlb_ref, dw_ref, dms_ref, dcv_s):
        w0, w1, w2 = cw_ref[0:1, :], cw_ref[1:2, :], cw_ref[2:3, :]
        tri = _causal()
        wm = jnp.where(tri, w_ref[...], 0.0).astype(bf16)
        bias, lg, lb = b_ref[...], lg_ref[...], lb_ref[...]
        dw_ref[...] = jnp.zeros_like(dw_ref)
        dms_ref[...] = jnp.zeros_like(dms_ref)

        def fwd_step(i, carry):
            halo, a0, a1, a2, alg, alb = carry
            rows = pl.ds(pl.multiple_of(i * C, C), C)
            ah, ab, ac, az = (p_ref[k, rows, :].astype(f32) for k in range(4))
            dya = dy_ref[0, rows, :]
            tt = ac * ah
            ext = jnp.concatenate([halo, tt], axis=0)
            t1, t2 = pltpu.roll(ext, 1, 0)[HALO_CONV:], pltpu.roll(ext, 2, 0)[HALO_CONV:]
            cv = w2 * tt + w1 * t1 + w0 * t2
            sa, dsa = _silu_and_grad(az)
            g1 = dya * sa
            dp_ref[1, rows, :] = (g1 * cv).astype(bf16)
            dp_ref[3, rows, :] = (dya * ab * cv * dsa).astype(bf16)
            dcv = g1 * ab
            dcv_s[rows, :] = dcv
            a2 = a2 + jnp.sum(dcv * tt, axis=0, keepdims=True)
            a1 = a1 + jnp.sum(dcv * t1, axis=0, keepdims=True)
            a0 = a0 + jnp.sum(dcv * t2, axis=0, keepdims=True)

            u, zb, dyb = p_ref[4, rows, :].astype(f32), p_ref[6, rows, :].astype(f32), dy_ref[1, rows, :]
            vhat, rstd = _layernorm_head(p_ref[5, rows, :].astype(f32))
            vn = (vhat * lg + lb).astype(bf16)
            sb, dsb = _silu_and_grad(zb)
            mix = jnp.concatenate([jnp.dot(wm, vn[k * HEAD:(k + 1) * HEAD], preferred_element_type=f32) + bias
                                   for k in range(C // HEAD)], axis=0)
            dp_ref[4, rows, :] = (dyb * mix * sb).astype(bf16)
            dp_ref[6, rows, :] = (dyb * u * mix * dsb).astype(bf16)
            dmix = dyb * u * sb
            dvn_parts = []
            for k in range(C // HEAD):
                dm = dmix[k * HEAD:(k + 1) * HEAD]
                dmb = dm.astype(bf16)
                dvn_parts.append(lax.dot_general(wm, dmb, (TN, ((), ())), preferred_element_type=f32))
                dw_ref[...] += lax.dot_general(dmb, vn[k * HEAD:(k + 1) * HEAD], (NT, ((), ())),
                                               preferred_element_type=f32)
                dms_ref[...] += dm
            dvn = jnp.concatenate(dvn_parts, axis=0)
            alg = alg + jnp.sum(dvn * vhat, axis=0, keepdims=True)
            alb = alb + jnp.sum(dvn, axis=0, keepdims=True)
            dvh = dvn * lg
            dv = rstd * (dvh - jnp.mean(dvh, axis=-1, keepdims=True)
                         - vhat * jnp.mean(dvh * vhat, axis=-1, keepdims=True))
            dp_ref[5, rows, :] = dv.astype(bf16)
            return tt[C - HALO_CONV:], a0, a1, a2, alg, alb

        zrow = jnp.zeros((1, HEAD), f32)
        _, a0, a1, a2, alg, alb = lax.fori_loop(
            0, n_chunks, fwd_step, (jnp.zeros((HALO_CONV, HEAD), f32), zrow, zrow, zrow, zrow, zrow))
        dcw_ref[0:1, :], dcw_ref[1:2, :], dcw_ref[2:3, :] = a0, a1, a2
        dlg_ref[...], dlb_ref[...] = alg, alb
        dw_ref[...] = jnp.where(tri, dw_ref[...], 0.0)

        def bwd_step(k, halo):
            rows = pl.ds(pl.multiple_of((n_chunks - 1 - k) * C, C), C)
            dcv = dcv_s[rows, :]
            ext = jnp.concatenate([dcv, halo], axis=0)
            n1 = pltpu.roll(ext, C + HALO_CONV - 1, 0)[:C]
            n2 = pltpu.roll(ext, C + HALO_CONV - 2, 0)[:C]
            dtt = w2 * dcv + w1 * n1 + w0 * n2
            dp_ref[2, rows, :] = (dtt * p_ref[0, rows, :].astype(f32)).astype(bf16)
            dp_ref[0, rows, :] = (dtt * p_ref[2, rows, :].astype(f32)).astype(bf16)
            return dcv[:HALO_CONV]

        lax.fori_loop(0, n_chunks, bwd_step, jnp.zeros((HALO_CONV, HEAD), f32))

    out_shape = [_sds((7, T, D), bf16), _sds((3, D), f32), _sds((1, D), f32), _sds((1, D), f32),
                 _sds((NH, HEAD, HEAD), f32), _sds((NH, HEAD, HEAD), f32)]
    return _pcall(body, name="even_bwd", out_shape=out_shape, grid=(NH,),
                  in_specs=[_head_spec(7, T), _head_spec(2, T), _head_vec(3), _head_vec(1), _head_vec(1),
                            _HEAD_MAT, _HEAD_MAT],
                  out_specs=[_head_spec(7, T), _head_vec(3), _head_vec(1), _head_vec(1), _HEAD_MAT, _HEAD_MAT],
                  scratch=[pltpu.VMEM((T, HEAD), f32)], vmem_mb=48)(p7, dy2, conv_w, ln_g, ln_b, sgu_w, sgu_bias)


def _window_sum(ext, win, towards_past):
    n, k, s = ext.shape[0], 1, ext
    while k < win:
        s = s + pltpu.roll(s, k if towards_past else n - k, 0)
        k *= 2
    return s


def _pool_count(i, C, win):
    t = i * C + lax.broadcasted_iota(jnp.int32, (C, 1), 0)
    cnt = jnp.minimum(t + 1, win).astype(f32)
    return cnt, 1.0 / cnt


def _group_specs(T):
    p_spec = pl.BlockSpec((None, T, GC), lambda g: (0, 0, g))
    z_spec = pl.BlockSpec((None, T, GC), lambda g: (1, 0, g))
    pw_spec = pl.BlockSpec((4, GC // 4, GC), lambda g: (0, g, 0))
    ps_spec = pl.BlockSpec((1, GC), lambda g: (0, g))
    y_spec = pl.BlockSpec((None, T, GC), lambda g: (g // 2, 0, g % 2))
    return p_spec, z_spec, pw_spec, ps_spec, y_spec


def _odd_fwd(p2, pool_wg, pool_scale):
    T, C = p2.shape[1], CHUNK_ROWS
    p_spec, z_spec, pw_spec, ps_spec, y_spec = _group_specs(T)

    def body(p_ref, z_ref, pw_ref, ps_ref, y_ref):
        pw, ps = pw_ref[...].reshape(GC, GC), ps_ref[...]

        def run(win):
            def step(i, halo):
                rows = pl.ds(pl.multiple_of(i * C, C), C)
                p = p_ref[rows, :].astype(f32)
                s = _window_sum(jnp.concatenate([halo, p], axis=0), win, True)[HALO_POOL:]
                pooled = s * _pool_count(i, C, win)[1] - p
                ypre = jnp.dot(pooled.astype(bf16), pw, preferred_element_type=f32)
                y_ref[rows, :] = (ypre * ps * _silu(z_ref[rows, :].astype(f32))).astype(bf16)
                return p[C - HALO_POOL:]

            lax.fori_loop(0, T // C, step, jnp.zeros((HALO_POOL, GC), f32))

        for gi, win in enumerate(WINDOWS):
            pl.when(pl.program_id(0) == gi)(functools.partial(run, win))

    return _pcall(body, name="odd_fwd", out_shape=_sds((2, T, D), bf16), grid=(len(WINDOWS),),
                  in_specs=[p_spec, z_spec, pw_spec, ps_spec], out_specs=y_spec, vmem_mb=40)(p2, p2, pool_wg, pool_scale)


def _odd_bwd(p2, dy2, pool_wg, pool_scale):
    T, C = p2.shape[1], CHUNK_ROWS
    n_chunks = T // C
    p_spec, z_spec, pw_spec, ps_spec, y_spec = _group_specs(T)

    def body(p_ref, z_ref, dy_ref, pw_ref, ps_ref, dp_ref, dpw_ref, dps_ref, q_s, acc_s):
        pw, ps = pw_ref[...].reshape(GC, GC), ps_ref[...]

        def run(win):
            acc_s[...] = jnp.zeros_like(acc_s)

            def fwd_step(i, carry):
                halo, aps = carry
                rows = pl.ds(pl.multiple_of(i * C, C), C)
                p, z, dy = p_ref[rows, :].astype(f32), z_ref[rows, :].astype(f32), dy_ref[rows, :]
                _, inv_cnt = _pool_count(i, C, win)
                s = _window_sum(jnp.concatenate([halo, p], axis=0), win, True)[HALO_POOL:]
                pb = (s * inv_cnt - p).astype(bf16)
                ypre = jnp.dot(pb, pw, preferred_element_type=f32)
                sz, dsz = _silu_and_grad(z)
                aps = aps + jnp.sum(dy * ypre * sz, axis=0, keepdims=True)
                dp_ref[1, rows, :] = (dy * ypre * ps * dsz).astype(bf16)
                dyp = (dy * ps * sz).astype(bf16)
                acc_s[...] += lax.dot_general(pb, dyp, (TN, ((), ())), preferred_element_type=f32)
                dpool = lax.dot_general(dyp, pw, (NT, ((), ())), preferred_element_type=f32)
                q_s[rows, :] = dpool * inv_cnt
                return p[C - HALO_POOL:], aps

            _, aps = lax.fori_loop(0, n_chunks, fwd_step, (jnp.zeros((HALO_POOL, GC), f32), jnp.zeros((1, GC), f32)))
            dps_ref[...] = aps
            dpw_ref[...] = acc_s[...].reshape(4, GC // 4, GC).astype(bf16)

            def bwd_step(k, halo):
                i = n_chunks - 1 - k
                rows = pl.ds(pl.multiple_of(i * C, C), C)
                q = q_s[rows, :]
                s = _window_sum(jnp.concatenate([q, halo], axis=0), win, False)[:C]
                dp_ref[0, rows, :] = (s - q * _pool_count(i, C, win)[0]).astype(bf16)
                return q[:HALO_POOL]

            lax.fori_loop(0, n_chunks, bwd_step, jnp.zeros((HALO_POOL, GC), f32))

        for gi, win in enumerate(WINDOWS):
            pl.when(pl.program_id(0) == gi)(functools.partial(run, win))

    out_shape = [_sds((2, T, 2 * D), bf16), _sds((4, GC, GC), bf16), _sds((1, 2 * D), f32)]
    return _pcall(body, name="odd_bwd", out_shape=out_shape, grid=(len(WINDOWS),),
                  in_specs=[p_spec, z_spec, y_spec, pw_spec, ps_spec],
                  out_specs=[pl.BlockSpec((2, T, GC), lambda g: (0, 0, g)), pw_spec, ps_spec],
                  scratch=[pltpu.VMEM((T, GC), f32), pltpu.VMEM((GC, GC), f32)], vmem_mb=52)(
                      p2, p2, dy2, pool_wg, pool_scale)


def _ada_fwd(c_all, ada_w):
    cols = ada_w.shape[2]

    def body(c_ref, w_ref, o_ref):
        o_ref[...] = jnp.dot(_silu(c_ref[...]), w_ref[...], preferred_element_type=f32,
                             precision=lax.Precision.HIGHEST)

    return _pcall(body, name="ada_fwd", out_shape=_sds((4, N_DEV, cols), f32), grid=(4,),
                  in_specs=[pl.BlockSpec((N_DEV, D), lambda i: (0, 0)), pl.BlockSpec((None, D, cols), lambda i: (i, 0, 0))],
                  out_specs=pl.BlockSpec((None, N_DEV, cols), lambda i: (i, 0, 0)))(c_all, ada_w)


def _ada_bwd(c_all_t, dmod, w, m, v):
    cols, tr = w.shape[2], 256
    spec = pl.BlockSpec((None, tr, cols), lambda l, i: (l, i, 0))

    def body(c_ref, dm_ref, w_ref, m_ref, v_ref, g_ref, d_ref, mo_ref, vo_ref):
        sc = _silu(c_ref[...])
        g = sc[:, 0:1] * dm_ref[0:1, :]
        for b in range(1, N_DEV):
            g = g + sc[:, b:b + 1] * dm_ref[b:b + 1, :]
        g_ref[...] = g
        d_ref[...], mo_ref[...], vo_ref[...] = _adamw_math(w_ref[...], g, m_ref[...], v_ref[...])

    return _pcall(body, name="ada_bwd", out_shape=[_sds(w.shape, f32)] * 4, grid=(4, D // tr),
                  in_specs=[pl.BlockSpec((tr, N_DEV), lambda l, i: (i, 0)),
                            pl.BlockSpec((None, N_DEV, cols), lambda l, i: (l, 0, 0)), spec, spec, spec],
                  out_specs=[spec] * 4)(c_all_t, dmod, w, m, v)


def _layer_fwd(even, x, hb, gate, w, nxt, before_out=None):
    if even:
        w_in, w_out, conv_w, ln_g, ln_b, sgu_w, sgu_b = w
        bias = jnp.broadcast_to(sgu_b[:, :, None], (NH, HEAD, HEAD))
        p = EVEN_PROJ.fwd(hb, w_in)
        y2 = _even_fwd(p, conv_w, ln_g, ln_b, sgu_w, bias)
    else:
        w_in, pool_w, w_out, pool_scale = w
        p = ODD_PROJ.fwd(hb, w_in)
        y2 = _odd_fwd(p, pool_w, pool_scale)
    if before_out is not None:
        late_w_out, tok = before_out(y2)
        if late_w_out is not None:
            w_out = late_w_out
            w = (w_in, w_out) + tuple(w[2:]) if even else (w_in, pool_w, w_out, pool_scale)
        if tok is not None:
            gate = gate + tok[0:1, 0:1]
    outs = _out_proj(y2, w_out.reshape(2, D, D), x, gate, nxt)
    return outs[0], (None if nxt is None else outs[2]), (x, hb, p, y2, outs[1]), w


def _layer_bwd(even, gin, dob, dgate, saved, scale, g, w, below=None, send=None):
    x_in, hb, p, y2, o = saved
    if even:
        w_in, w_out, conv_w, ln_g, ln_b, sgu_w, sgu_b = w
        bias = jnp.broadcast_to(sgu_b[:, :, None], (NH, HEAD, HEAD))
        dy2 = _dy_mm(dob, w_out)
        dp, dconv, dlg, dlb, dsw, dms = _even_bwd(p, dy2, conv_w, ln_g, ln_b, sgu_w, bias)
        proj = EVEN_PROJ
        small = dict(conv_w=dconv, ln_g=dlg, ln_b=dlb, sgu_w=dsw, sgu_b=jnp.sum(dms, axis=-1))
        big = [proj.dw(hb, dp), _dwo_mm(y2, dob)]
    else:
        w_in, pool_w, w_out, pool_scale = w
        dy2 = _dy_mm(dob, w_out)
        dp, dpw, dps = _odd_bwd(p, dy2, pool_w, pool_scale)
        proj = ODD_PROJ
        small = dict(pool_scale=dps)
        big = [proj.dw(hb, dp), dpw, _dwo_mm(y2, dob)]
    if send is not None:
        big, tok = send(big)
        scale = scale + tok[0:1, 0:1]
    dh = proj.dh(dp, w_in)
    res = _norm_bwd(x_in, dh, gin, g, scale, below)
    stats = res[1]
    return (res[0], (None if below is None else (res[2], res[3])), big, small,
            jnp.concatenate([stats[0:2], dgate], axis=0), stats[2:3])


def _pack_rows(parts):
    rows = [p.reshape(-1, LANES) for p in parts]
    total = sum(r.shape[0] for r in rows)
    padded = -(-total // (8 * N_DEV)) * (8 * N_DEV)
    if padded > total:
        rows.append(jnp.zeros((padded - total, LANES), f32))
    return jnp.concatenate(rows, axis=0)


def _unpack_rows(buf, shapes):
    out, r = [], 0
    for shp in shapes:
        n = 1
        for d in shp:
            n *= d
        out.append(buf[r:r + n // LANES].reshape(shp))
        r += n // LANES
    return out


def kernel(x, c, norm_g, ada_w, ada_b, ab_w_in, ab_conv_w, ab_ln_g, ab_ln_b, ab_sgu_w, ab_sgu_b, ab_w_out, c_w_in, c_pool_w, c_pool_scale, c_w_out, final_g, loss_target, m_norm_g, m_ada_w, m_ada_b, m_ab_w_in, m_ab_conv_w, m_ab_ln_g, m_ab_ln_b, m_ab_sgu_w, m_ab_sgu_b, m_ab_w_out, m_c_w_in, m_c_pool_w, m_c_pool_scale, m_c_w_out, m_final_g, v_norm_g, v_ada_w, v_ada_b, v_ab_w_in, v_ab_conv_w, v_ab_ln_g, v_ab_ln_b, v_ab_sgu_w, v_ab_sgu_b, v_ab_w_out, v_c_w_in, v_c_pool_w, v_c_pool_scale, v_c_w_out, v_final_g):
    ix, iy, ic = _place()
    chip, dev = 2 * ix + iy, 4 * ix + 2 * iy + ic
    n_even, n_odd = ab_w_in.shape[0], c_w_in.shape[0]
    depth = n_even + n_odd
    acols = ada_w.shape[2]

    place = jnp.stack([chip, ic]).astype(jnp.int32)
    even_names, odd_names = ["ab_w_in", "ab_w_out"], ["c_w_in", "c_pool_w", "c_w_out"]
    params = {"ab_w_in": (ab_w_in, m_ab_w_in, v_ab_w_in), "ab_w_out": (ab_w_out, m_ab_w_out, v_ab_w_out),
              "c_w_in": (c_w_in, m_c_w_in, v_c_w_in), "c_w_out": (c_w_out, m_c_w_out, v_c_w_out),
              "c_pool_w": tuple(a.reshape(n_odd, GC, GC) for a in (c_pool_w, m_c_pool_w, v_c_pool_w))}

    first = _gather8(jnp.concatenate([c, ab_conv_w.reshape(1, -1), c_pool_scale.reshape(1, -1)], axis=1), "gather_c")
    c_all, small_all = first[:, 0, :D], first[0::2, 0, D:]
    modp = _ada_fwd(c_all, ada_w)
    modg = _gather8(modp, "gather_mod")
    mod_rows = lax.dynamic_index_in_dim(modg[0::2], dev, axis=2, keepdims=False)
    mod = jnp.transpose(mod_rows, (1, 0, 2)).reshape(depth, 3 * D) + ada_b
    mods = [(mod[i:i + 1, 0:D], mod[i:i + 1, D:2 * D], mod[i:i + 1, 2 * D:3 * D]) for i in range(depth)]

    def shard_cols(a, width):
        return lax.dynamic_slice_in_dim(a, chip * width, width, axis=a.ndim - 1)

    n_conv = ab_conv_w.size
    conv_all = small_all[:, :n_conv].reshape(4, n_even, 3, D // 4)
    conv_full = jnp.transpose(conv_all, (1, 2, 0, 3)).reshape(n_even, 3, D)
    scale_all = small_all[:, n_conv:].reshape(4, n_odd, 2 * D // 4)
    scale_full = jnp.transpose(scale_all, (1, 0, 2)).reshape(n_odd, 2 * D)

    def placed(names, layer, after=None):
        ws = [params[nm][0] for nm in names]
        return [p.reshape(4, 2, p.shape[1] // 2, p.shape[2]) for p in _cast_place(place, ws, layer, after)]

    def whole(arrays):
        return [g.reshape(4, 2 * g.shape[2], g.shape[3]) for g in arrays]

    gathers_done = mod[0:1, 0:LANES] + scale_full[0:1, 0:LANES]
    sems_a, in_a, tok = _ag_start([placed(even_names[:1], 0)], gathers_done, "ag_start_0a")
    sems_b, in_b, tok = _ag_start([placed(even_names[1:], 0, tok)], tok, "ag_start_0b")
    rest = [placed(even_names if i % 2 == 0 else odd_names, i // 2, tok) for i in range(1, depth)]
    sems_r, in_r, tok = _ag_start(rest, tok, "ag_start_rest")

    x_cur, saved, weights, handoff = x[0], [], [], {}
    hb = _hnorm(x_cur, norm_g[0:1], mods[0][0] + tok[0:1, 0:1], mods[0][1])
    for i in range(depth):
        j = i // 2
        if i == 0:
            full = whole(_ag_forward(_ag_wait(in_a[0], sems_a[0], hb, "ag_wait_0a"), "ag_forward")) + [None]
        else:
            full = whole(_agf_wait(*handoff.pop(i), x_cur, f"agf_wait_{i}"))
        if i % 2 == 0:
            w = (full[0], full[1], conv_full[j], ab_ln_g[j:j + 1], ab_ln_b[j:j + 1], ab_sgu_w[j], ab_sgu_b[j])
        else:
            w = (full[0], full[1], full[2], scale_full[j:j + 1])

        def before_out(y2, i=i):
            w_out, tok = None, None
            if i == 0:
                w_out = whole(_ag_forward(_ag_wait(in_b[0], sems_b[0], y2, "ag_wait_0b"), "ag_forward"))[0]
            if i + 1 < depth:
                arrived = _ag_wait(in_r[i], sems_r[i], y2, f"ag_wait_{i + 1}")
                sems_f, inflight, tok = _agf_start(arrived, f"agf_start_{i + 1}")
                handoff[i + 1] = (sems_f, inflight)
            return w_out, tok

        nxt = (norm_g[i + 1:i + 2], mods[i + 1][0], mods[i + 1][1]) if i + 1 < depth else None
        x_cur, hb, sv, w = _layer_fwd(i % 2 == 0, x_cur, hb, mods[i][2], w, nxt, before_out)
        weights.append(w)
        saved.append(sv)
    gin, loss, dfinal_g, dob, dgate = _loss_bwd(x_cur, loss_target[0], final_g.reshape(1, D), saved[-1][4],
                                                mods[-1][2])

    stacked = {}

    def finish(i, sems, pairs, lands, after):
        pairs, slots = _rs_chip_wait(sems, pairs, lands, after, f"rs_chip_wait_{i}")
        names = even_names if i % 2 == 0 else odd_names
        grads = _rs_half_exchange(_rs_sum(place, pairs, slots), "rs_half_exchange")
        items = [(params[nm][0], g.reshape(params[nm][0].shape[1:]), params[nm][1], params[nm][2], stacked.get(nm))
                 for nm, g in zip(names, grads)]
        for nm, res in zip(names, _adamw_layer(i // 2, items)):
            stacked[nm] = res

    small_g, dmod, dnorm_g, pending, tok = [None] * depth, [None] * depth, [None] * depth, None, None
    for i in reversed(range(depth)):
        w = weights[i]
        if tok is not None:
            w = w[:2] + (w[2] + tok[0:1, 0:1],) + w[3:] if i % 2 == 0 else w[:3] + (w[3] + tok[0:1, 0:1],)
        below = (saved[i - 1][4], mods[i - 1][2]) if i > 0 else None

        def send(big_g, i=i):
            big_g = [g.reshape(4, 2, g.shape[1] // 2, g.shape[2]) for g in big_g]
            sems, big_g, lands, tok = _rs_pair_start(big_g, f"rs_pair_start_{i}")
            return (sems, big_g, lands), tok

        gin, gate_bwd, sent, small_g[i], dmod[i], dnorm_g[i] = _layer_bwd(
            i % 2 == 0, gin, dob, dgate, saved[i], mods[i][1], norm_g[i:i + 1], w, below, send)
        if below is not None:
            dob, dgate = gate_bwd
        big_g, theirs = _rs_pair_wait(*sent, gin, f"rs_pair_wait_{i}")
        pairs = _rs_add(place, big_g, theirs)
        sems, pairs, lands, tok = _rs_chip_start(pairs, f"rs_chip_start_{i}")
        if pending is not None:
            finish(*pending, tok)
        pending = (i, sems, pairs, lands)
    grad_x = gin
    dmod, dnorm_g = jnp.stack(dmod), jnp.concatenate(dnorm_g, axis=0)

    small_parts = [dnorm_g + tok[0:1, 0:1], dfinal_g,
                   jnp.stack([small_g[2 * j]["conv_w"] for j in range(n_even)]),
                   jnp.concatenate([small_g[2 * j]["ln_g"] for j in range(n_even)], axis=0),
                   jnp.concatenate([small_g[2 * j]["ln_b"] for j in range(n_even)], axis=0),
                   jnp.stack([small_g[2 * j]["sgu_b"] for j in range(n_even)]),
                   jnp.concatenate([small_g[2 * j + 1]["pool_scale"] for j in range(n_odd)], axis=0),
                   jnp.pad(loss, ((0, 7), (0, LANES - 1)))]
    small_shapes = [p.shape for p in small_parts]
    sgu_parts = [small_g[2 * j]["sgu_w"].reshape(NH * HEAD, HEAD) for j in range(n_even)]
    reduced = _allreduce8([_pack_rows(small_parts)] + sgu_parts, "allreduce_small")
    g_norm_g, g_final_g, g_conv_full, g_ln_g, g_ln_b, g_sgu_b, g_scale_full, loss_row = _unpack_rows(reduced[0],
                                                                                                     small_shapes)
    g_sgu_w = jnp.stack(reduced[1:])
    loss = loss_row[0, 0]
    g_conv = shard_cols(g_conv_full, D // 4)
    g_scale = shard_cols(g_scale_full, 2 * D // 4)
    dmod_all = _gather8(dmod.reshape(depth * 3 * D // LANES, LANES), "gather_dmod").reshape(N_DEV, depth, 3 * D)

    def two_d(a):
        return a.reshape(-1, a.shape[-1])

    small = [(norm_g, g_norm_g, m_norm_g, v_norm_g),
             (ada_b, dmod_all, m_ada_b, v_ada_b),
             (two_d(ab_conv_w), two_d(g_conv), two_d(m_ab_conv_w), two_d(v_ab_conv_w)),
             (ab_ln_g, g_ln_g, m_ab_ln_g, v_ab_ln_g),
             (ab_ln_b, g_ln_b, m_ab_ln_b, v_ab_ln_b),
             (two_d(ab_sgu_w), two_d(g_sgu_w), two_d(m_ab_sgu_w), two_d(v_ab_sgu_w)),
             (two_d(ab_sgu_b), two_d(g_sgu_b), two_d(m_ab_sgu_b), two_d(v_ab_sgu_b)),
             (c_pool_scale, g_scale, m_c_pool_scale, v_c_pool_scale),
             (final_g.reshape(1, D), g_final_g, m_final_g.reshape(1, D), v_final_g.reshape(1, D))]
    small_res = _adamw_small(small)
    small_shapes_out = [norm_g.shape, ada_b.shape, ab_conv_w.shape, ab_ln_g.shape, ab_ln_b.shape, ab_sgu_w.shape,
                        ab_sgu_b.shape, c_pool_scale.shape, final_g.shape]
    (r_norm_g, r_ada_b, r_conv, r_ln_g, r_ln_b, r_sgu_w, r_sgu_b, r_scale, r_final_g) = [
        tuple(a.reshape(shp) for a in res) for res, shp in zip(small_res, small_shapes_out)]

    dmod_cols = jnp.transpose(shard_cols(dmod_all, acols), (1, 0, 2))
    r_ada_w = _ada_bwd(c_all.T, dmod_cols, ada_w, m_ada_w, v_ada_w)

    finish(*pending, r_ada_w[1])
    r_ab_w_in, r_ab_w_out, r_c_w_in, r_c_w_out = (stacked[nm] for nm in ("ab_w_in", "ab_w_out", "c_w_in", "c_w_out"))
    r_c_pool_w = tuple(a.reshape(c_pool_w.shape) for a in stacked["c_pool_w"])

    order = [r_norm_g, r_ada_w, r_ada_b, r_ab_w_in, r_conv, r_ln_g, r_ln_b, r_sgu_w, r_sgu_b, r_ab_w_out,
             r_c_w_in, r_c_pool_w, r_scale, r_c_w_out, r_final_g]
    outs = [loss, grad_x[None]]
    for field in range(4):
        outs += [r[field] for r in order]
    return tuple(outs)
```

```python
import functools

import jax
import jax.numpy as jnp
from jax import lax
from jax.experimental import pallas as pl
from jax.experimental.pallas import tpu as pltpu

f32, bf16 = jnp.float32, jnp.bfloat16

D = 1024
HEAD = 128
NH = 8
WINDOWS = (2, 4, 8, 16)
GC = 512
EPS = 1e-6
HALO_CONV = 8
HALO_POOL = 16
CHUNK_ROWS = 512
DH_WIDE = 1024
FWD_TILES = 2
N_DEV = 8
LANES = 128

ADAM_LR, ADAM_B1, ADAM_B2, ADAM_EPS, ADAM_WD, ADAM_STEP = 0.001, 0.9, 0.999, 1e-08, 0.01, 10

MESH = pl.DeviceIdType.MESH
ANY = pl.BlockSpec(memory_space=pl.ANY)
VMEM = pl.BlockSpec(memory_space=pltpu.VMEM)
MIB = 2 ** 20


def _pcall(body, *, name, out_shape, grid=None, in_specs=None, out_specs=None, scratch=(), vmem_mb=None,
           aliases=None, prefetch=0):
    kw = {}
    if prefetch:
        kw["grid_spec"] = pltpu.PrefetchScalarGridSpec(num_scalar_prefetch=prefetch, grid=grid, in_specs=in_specs,
                                                       out_specs=out_specs, scratch_shapes=list(scratch))
    else:
        if grid is not None:
            kw["grid"] = grid
        if in_specs is not None:
            kw["in_specs"] = in_specs
        if out_specs is not None:
            kw["out_specs"] = out_specs
        if scratch:
            kw["scratch_shapes"] = list(scratch)
    if aliases:
        kw["input_output_aliases"] = aliases
    params = pltpu.CompilerParams(vmem_limit_bytes=None if vmem_mb is None else vmem_mb * MIB)
    return pl.pallas_call(body, name=name, out_shape=out_shape, compiler_params=params, **kw)


def _sds(shape, dtype):
    return jax.ShapeDtypeStruct(tuple(shape), dtype)


def _sigmoid(z):
    return pl.reciprocal(1.0 + jnp.exp(-z), approx=True)


def _silu(z):
    return z * _sigmoid(z)


def _silu_and_grad(z):
    s = _sigmoid(z)
    return z * s, s * (1.0 + z * (1.0 - s))


def _place():
    return lax.axis_index("x"), lax.axis_index("y"), lax.axis_index("c")


def _gather8(blk, name):
    def body(x_ref, o_ref, ssem, rsem):
        x, y, c = _place()
        me = 4 * x + 2 * y + c
        o_ref[me] = x_ref[...]
        sends = []
        for k in range(1, N_DEV):
            px = 1 - x if k & 4 else x
            py = 1 - y if k & 2 else y
            pc = 1 - c if k & 1 else c
            cp = pltpu.make_async_remote_copy(src_ref=x_ref, dst_ref=o_ref.at[me], send_sem=ssem.at[k - 1],
                                              recv_sem=rsem.at[k - 1], device_id=(px, py, pc), device_id_type=MESH)
            cp.start()
            sends.append((cp, 4 * px + 2 * py + pc))
        for k, (cp, peer) in enumerate(sends):
            pltpu.make_async_remote_copy(src_ref=x_ref, dst_ref=o_ref.at[peer], send_sem=ssem.at[k],
                                         recv_sem=rsem.at[k], device_id=(x, y, c), device_id_type=MESH).wait_recv()
        for cp, _ in sends:
            cp.wait_send()

    return _pcall(body, name=name, out_shape=_sds((N_DEV,) + blk.shape, blk.dtype), in_specs=[VMEM], out_specs=VMEM,
                  scratch=[pltpu.SemaphoreType.DMA((N_DEV - 1,)), pltpu.SemaphoreType.DMA((N_DEV - 1,))])(blk)


def _allreduce8(bufs, name):
    n = len(bufs)
    rbs = [b.shape[0] // N_DEV for b in bufs]
    assert all(rb * N_DEV == b.shape[0] and rb % 8 == 0 for rb, b in zip(rbs, bufs))

    def body(*refs):
        xs, outs, stages = refs[:n], refs[n:2 * n], refs[2 * n:3 * n]
        ssem, rsem = refs[3 * n:]
        x, y, c = _place()
        me = 4 * x + 2 * y + c
        peers = []
        for k in range(1, N_DEV):
            px = 1 - x if k & 4 else x
            py = 1 - y if k & 2 else y
            pc = 1 - c if k & 1 else c
            peers.append(((px, py, pc), 4 * px + 2 * py + pc))

        def blk(t, ref, idx):
            return ref.at[pl.ds(pl.multiple_of(idx * rbs[t], 8), rbs[t]), :]

        def copy(t, phase, k, src, dst, dev):
            return pltpu.make_async_remote_copy(src_ref=src, dst_ref=dst, send_sem=ssem.at[t, phase, k],
                                                recv_sem=rsem.at[t, phase, k], device_id=dev, device_id_type=MESH)

        scatter = [copy(t, 0, k, blk(t, xs[t], pidx), stages[t].at[me], dev)
                   for t in range(n) for k, (dev, pidx) in enumerate(peers)]
        for cp in scatter:
            cp.start()
        gather = []
        for t in range(n):
            stages[t][me] = blk(t, xs[t], me)[...]
            for k, (dev, pidx) in enumerate(peers):
                copy(t, 0, k, blk(t, xs[t], pidx), stages[t].at[pidx], dev).wait_recv()
            total = stages[t][0]
            for j in range(1, N_DEV):
                total = total + stages[t][j]
            blk(t, outs[t], me)[...] = total
            sends = [copy(t, 1, k, blk(t, outs[t], me), blk(t, outs[t], me), dev) for k, (dev, pidx) in enumerate(peers)]
            for cp in sends:
                cp.start()
            gather += sends
        for t in range(n):
            for k, (dev, pidx) in enumerate(peers):
                copy(t, 1, k, blk(t, outs[t], pidx), blk(t, outs[t], pidx), dev).wait_recv()
        for cp in scatter + gather:
            cp.wait_send()

    return _pcall(body, name=name, out_shape=[_sds(b.shape, f32) for b in bufs], in_specs=[VMEM] * n, out_specs=[VMEM] * n,
                  scratch=[pltpu.VMEM((N_DEV, rb, LANES), f32) for rb in rbs]
                  + [pltpu.SemaphoreType.DMA((n, 2, N_DEV - 1)), pltpu.SemaphoreType.DMA((n, 2, N_DEV - 1))])(*bufs)


def _other_chips(x, y):
    return [((1 - x, y), 2 * (1 - x) + y), ((x, 1 - y), 2 * x + (1 - y)), ((1 - x, 1 - y), 2 * (1 - x) + (1 - y))]


HBM = pl.BlockSpec(memory_space=pltpu.HBM)
SEM = pl.BlockSpec(memory_space=pltpu.SEMAPHORE)
EFFECT = pltpu.SideEffectType.DATAFLOW_SIDE_EFFECTING


def _in_hbm(a):
    return pltpu.with_memory_space_constraint(a, pltpu.HBM)


def _ag_start(layers, after, name):
    flat = [t for lay in layers for t in lay]
    n, nl = len(flat), len(layers)

    def body(*refs):
        src = refs[:n]
        sems = refs[n + 1:n + 1 + 2 * nl]
        token = refs[-1]
        x, y, c = _place()
        s_me = 2 * x + y
        t = 0
        for i, lay in enumerate(layers):
            for k in range(len(lay)):
                for j, ((px, py), _) in enumerate(_other_chips(x, y)):
                    pltpu.make_async_remote_copy(src_ref=src[t].at[s_me, c], dst_ref=src[t].at[s_me, c],
                                                 send_sem=sems[2 * i].at[3 * k + j], recv_sem=sems[2 * i + 1].at[3 * k + j],
                                                 device_id=(px, py, c), device_id_type=MESH).start()
                t += 1
        token[...] = jnp.zeros_like(token)

    sem_shapes = [pltpu.SemaphoreType.DMA((3 * len(lay),)) for lay in layers for _ in range(2)]
    out_shape = sem_shapes + [pltpu.HBM(t.shape, t.dtype) for t in flat] + [_sds((8, LANES), f32)]
    outs = pl.pallas_call(
        body, name=name, out_shape=out_shape, in_specs=[HBM] * n + [ANY],
        out_specs=[SEM] * (2 * nl) + [HBM] * n + [VMEM], input_output_aliases={t: 2 * nl + t for t in range(n)},
        compiler_params=pltpu.CompilerParams(has_side_effects=EFFECT))(*[_in_hbm(t) for t in flat], after)
    sems = [(outs[2 * i], outs[2 * i + 1]) for i in range(nl)]
    thru, t = [], 2 * nl
    for lay in layers:
        thru.append(list(outs[t:t + len(lay)]))
        t += len(lay)
    return sems, thru, outs[-1]


def _ag_wait(inflight, sems, after, name):
    n = len(inflight)

    def body(*refs):
        src, ssem, rsem = refs[:n], refs[n], refs[n + 1]
        x, y, c = _place()
        s_me = 2 * x + y
        for k in range(n):
            for j, (_, s_p) in enumerate(_other_chips(x, y)):
                cp = pltpu.make_async_remote_copy(src_ref=src[k].at[s_me, c], dst_ref=src[k].at[s_p, c],
                                                  send_sem=ssem.at[3 * k + j], recv_sem=rsem.at[3 * k + j],
                                                  device_id=(x, y, c), device_id_type=MESH)
                cp.wait_send()
                cp.wait_recv()

    return pl.pallas_call(
        body, name=name, out_shape=[pltpu.HBM(t.shape, t.dtype) for t in inflight],
        in_specs=[HBM] * n + [SEM, SEM, ANY], out_specs=[HBM] * n, input_output_aliases={t: t for t in range(n)},
        compiler_params=pltpu.CompilerParams(has_side_effects=EFFECT))(*inflight, sems[0], sems[1], after)


def _ag_forward(arrived, name):
    n = len(arrived)

    def body(*refs):
        o = refs[n:2 * n]
        ssem, rsem = refs[2 * n:]
        x, y, c = _place()

        def copy(t, j, s, half, dev):
            return pltpu.make_async_remote_copy(src_ref=o[t].at[s, c], dst_ref=o[t].at[s, half], send_sem=ssem.at[t, j],
                                                recv_sem=rsem.at[t, j], device_id=dev, device_id_type=MESH)

        chips = _other_chips(x, y)
        sends = [copy(t, j, s_p, c, (x, y, 1 - c)) for t in range(n) for j, (_, s_p) in enumerate(chips)]
        for cp in sends:
            cp.start()
        for t in range(n):
            for j, (_, s_p) in enumerate(chips):
                copy(t, j, s_p, 1 - c, (x, y, c)).wait_recv()
        for cp in sends:
            cp.wait_send()

    return _pcall(body, name=name, out_shape=[_sds(p.shape, bf16) for p in arrived], in_specs=[ANY] * n,
                  out_specs=[ANY] * n, aliases={t: t for t in range(n)},
                  scratch=[pltpu.SemaphoreType.DMA((n, 3)), pltpu.SemaphoreType.DMA((n, 3))])(*arrived)


def _agf_start(arrived, name):
    n = len(arrived)

    def body(*refs):
        o = refs[:n]
        ssem, rsem, token = refs[n], refs[n + 1], refs[-1]
        x, y, c = _place()
        for t in range(n):
            for j, (_, s_p) in enumerate(_other_chips(x, y)):
                pltpu.make_async_remote_copy(src_ref=o[t].at[s_p, c], dst_ref=o[t].at[s_p, c],
                                             send_sem=ssem.at[3 * t + j], recv_sem=rsem.at[3 * t + j],
                                             device_id=(x, y, 1 - c), device_id_type=MESH).start()
        token[...] = jnp.zeros_like(token)

    out_shape = ([pltpu.SemaphoreType.DMA((3 * n,))] * 2 + [pltpu.HBM(a.shape, bf16) for a in arrived]
                 + [_sds((8, LANES), f32)])
    outs = pl.pallas_call(
        body, name=name, out_shape=out_shape, in_specs=[HBM] * n, out_specs=[SEM, SEM] + [HBM] * n + [VMEM],
        input_output_aliases={t: 2 + t for t in range(n)},
        compiler_params=pltpu.CompilerParams(has_side_effects=EFFECT))(*[_in_hbm(a) for a in arrived])
    return (outs[0], outs[1]), list(outs[2:2 + n]), outs[-1]


def _agf_wait(sems, inflight, after, name):
    n = len(inflight)

    def body(*refs):
        o, ssem, rsem = refs[:n], refs[n], refs[n + 1]
        x, y, c = _place()
        for t in range(n):
            for j, (_, s_p) in enumerate(_other_chips(x, y)):
                cp = pltpu.make_async_remote_copy(src_ref=o[t].at[s_p, c], dst_ref=o[t].at[s_p, 1 - c],
                                                  send_sem=ssem.at[3 * t + j], recv_sem=rsem.at[3 * t + j],
                                                  device_id=(x, y, c), device_id_type=MESH)
                cp.wait_send()
                cp.wait_recv()

    return pl.pallas_call(
        body, name=name, out_shape=[pltpu.HBM(a.shape, bf16) for a in inflight],
        in_specs=[HBM] * n + [SEM, SEM, ANY], out_specs=[HBM] * n, input_output_aliases={t: t for t in range(n)},
        compiler_params=pltpu.CompilerParams(has_side_effects=EFFECT))(*inflight, sems[0], sems[1], after)


def _rs_pair_start(grads, name):
    n = len(grads)

    def body(*refs):
        g, theirs = refs[:n], refs[n:2 * n]
        ssem, rsem, token = refs[2 * n], refs[2 * n + 1], refs[-1]
        x, y, c = _place()
        for t in range(n):
            pltpu.make_async_remote_copy(src_ref=g[t].at[:, 1 - c], dst_ref=theirs[t], send_sem=ssem.at[t],
                                         recv_sem=rsem.at[t], device_id=(x, y, 1 - c), device_id_type=MESH).start()
        token[...] = jnp.zeros_like(token)

    lands = [lax.empty((4,) + g.shape[2:], bf16) for g in grads]
    out_shape = ([pltpu.SemaphoreType.DMA((n,))] * 2 + [pltpu.HBM(g.shape, bf16) for g in grads]
                 + [pltpu.HBM(q.shape, bf16) for q in lands] + [_sds((8, LANES), f32)])
    outs = pl.pallas_call(
        body, name=name, out_shape=out_shape, in_specs=[HBM] * (2 * n), out_specs=[SEM, SEM] + [HBM] * (2 * n) + [VMEM],
        input_output_aliases={t: 2 + t for t in range(2 * n)},
        compiler_params=pltpu.CompilerParams(has_side_effects=EFFECT))(*[_in_hbm(a) for a in list(grads) + lands])
    return (outs[0], outs[1]), list(outs[2:2 + n]), list(outs[2 + n:2 + 2 * n]), outs[-1]


def _rs_pair_wait(sems, grads, lands, after, name):
    n = len(grads)

    def body(*refs):
        g, theirs = refs[:n], refs[n:2 * n]
        ssem, rsem = refs[2 * n], refs[2 * n + 1]
        x, y, c = _place()
        for t in range(n):
            cp = pltpu.make_async_remote_copy(src_ref=g[t].at[:, 1 - c], dst_ref=theirs[t], send_sem=ssem.at[t],
                                              recv_sem=rsem.at[t], device_id=(x, y, c), device_id_type=MESH)
            cp.wait_send()
            cp.wait_recv()

    outs = pl.pallas_call(
        body, name=name, out_shape=[pltpu.HBM(a.shape, bf16) for a in list(grads) + list(lands)],
        in_specs=[HBM] * (2 * n) + [SEM, SEM, ANY], out_specs=[HBM] * (2 * n),
        input_output_aliases={t: t for t in range(2 * n)},
        compiler_params=pltpu.CompilerParams(has_side_effects=EFFECT))(*grads, *lands, sems[0], sems[1], after)
    return list(outs[:n]), list(outs[n:])


def _rs_chip_start(pairs, name):
    n = len(pairs)

    def body(*refs):
        p, q = refs[:n], refs[n:2 * n]
        ssem, rsem, token = refs[2 * n], refs[2 * n + 1], refs[-1]
        x, y, c = _place()
        for t in range(n):
            for j, ((px, py), s_p) in enumerate(_other_chips(x, y)):
                pltpu.make_async_remote_copy(src_ref=p[t].at[s_p], dst_ref=q[t].at[j], send_sem=ssem.at[3 * t + j],
                                             recv_sem=rsem.at[3 * t + j], device_id=(px, py, c), device_id_type=MESH).start()
        token[...] = jnp.zeros_like(token)

    lands = [lax.empty((3,) + p.shape[1:], bf16) for p in pairs]
    out_shape = ([pltpu.SemaphoreType.DMA((3 * n,))] * 2 + [pltpu.HBM(p.shape, bf16) for p in pairs]
                 + [pltpu.HBM(q.shape, bf16) for q in lands] + [_sds((8, LANES), f32)])
    outs = pl.pallas_call(
        body, name=name, out_shape=out_shape, in_specs=[HBM] * (2 * n), out_specs=[SEM, SEM] + [HBM] * (2 * n) + [VMEM],
        input_output_aliases={t: 2 + t for t in range(2 * n)},
        compiler_params=pltpu.CompilerParams(has_side_effects=EFFECT))(*[_in_hbm(a) for a in list(pairs) + lands])
    return (outs[0], outs[1]), list(outs[2:2 + n]), list(outs[2 + n:2 + 2 * n]), outs[-1]


def _rs_chip_wait(sems, pairs, lands, after, name):
    n = len(pairs)

    def body(*refs):
        p, q = refs[:n], refs[n:2 * n]
        ssem, rsem = refs[2 * n], refs[2 * n + 1]
        x, y, c = _place()
        for t in range(n):
            for j, (_, s_p) in enumerate(_other_chips(x, y)):
                cp = pltpu.make_async_remote_copy(src_ref=p[t].at[s_p], dst_ref=q[t].at[j], send_sem=ssem.at[3 * t + j],
                                                  recv_sem=rsem.at[3 * t + j], device_id=(x, y, c), device_id_type=MESH)
                cp.wait_send()
                cp.wait_recv()

    outs = pl.pallas_call(
        body, name=name, out_shape=[pltpu.HBM(a.shape, bf16) for a in list(pairs) + list(lands)],
        in_specs=[HBM] * (2 * n) + [SEM, SEM, ANY], out_specs=[HBM] * (2 * n),
        input_output_aliases={t: t for t in range(2 * n)},
        compiler_params=pltpu.CompilerParams(has_side_effects=EFFECT))(*pairs, *lands, sems[0], sems[1], after)
    return list(outs[:n]), list(outs[n:])


def _rs_half_exchange(halves, name):
    n = len(halves)

    def body(*refs):
        o = refs[n:2 * n]
        ssem, rsem = refs[2 * n:]
        x, y, c = _place()

        def copy(t, half, dev):
            return pltpu.make_async_remote_copy(src_ref=o[t].at[c], dst_ref=o[t].at[half], send_sem=ssem.at[t],
                                                recv_sem=rsem.at[t], device_id=dev, device_id_type=MESH)

        sends = [copy(t, c, (x, y, 1 - c)) for t in range(n)]
        for cp in sends:
            cp.start()
        for t in range(n):
            copy(t, 1 - c, (x, y, c)).wait_recv()
        for cp in sends:
            cp.wait_send()

    return _pcall(body, name=name, out_shape=[_sds(h.shape, h.dtype) for h in halves], in_specs=[ANY] * n,
                  out_specs=[ANY] * n, aliases={t: t for t in range(n)},
                  scratch=[pltpu.SemaphoreType.DMA((n,)), pltpu.SemaphoreType.DMA((n,))])(*halves)


def _row_spec(tm, cols):
    return pl.BlockSpec((tm, cols), lambda i: (i, 0))


def _vec_spec(cols, rows=1):
    return pl.BlockSpec((rows, cols), lambda i: (0, 0))


def _modulated_norm(xv, g, shift, scale):
    r = lax.rsqrt(jnp.mean(xv * xv, axis=-1, keepdims=True) + EPS)
    return (((xv * r) * g) * (1.0 + scale) + shift).astype(bf16)


def _hnorm(x, g, shift, scale):
    T, tm = x.shape[0], 256

    def body(x_ref, g_ref, sh_ref, sc_ref, h_ref):
        h_ref[...] = _modulated_norm(x_ref[...], g_ref[...], sh_ref[...], sc_ref[...])

    return _pcall(body, name="hnorm", out_shape=_sds((T, D), bf16), grid=(T // tm,),
                  in_specs=[_row_spec(tm, D), _vec_spec(D), _vec_spec(D), _vec_spec(D)],
                  out_specs=_row_spec(tm, D))(x, g, shift, scale)


def _out_proj(y2, wo, x, gate, nxt=None):
    T, tm = x.shape[0], 512

    def body(y_ref, w_ref, x_ref, g_ref, *rest):
        o = jnp.dot(y_ref[0], w_ref[0], preferred_element_type=f32)
        o = o + jnp.dot(y_ref[1], w_ref[1], preferred_element_type=f32)
        xo = x_ref[...] + g_ref[...] * o
        if nxt is None:
            xo_ref, o_ref = rest
        else:
            ng_ref, nsh_ref, nsc_ref, xo_ref, o_ref, h_ref = rest
            h_ref[...] = _modulated_norm(xo, ng_ref[...], nsh_ref[...], nsc_ref[...])
        o_ref[...] = o
        xo_ref[...] = xo

    extra = [] if nxt is None else list(nxt)
    n_out = 2 if nxt is None else 3
    return _pcall(body, name="out_proj", out_shape=[_sds((T, D), f32), _sds((T, D), f32), _sds((T, D), bf16)][:n_out],
                  grid=(T // tm,),
                  in_specs=[pl.BlockSpec((2, tm, D), lambda i: (0, i, 0)), pl.BlockSpec((2, D, D), lambda i: (0, 0, 0)),
                            _row_spec(tm, D), _vec_spec(D)] + [_vec_spec(D)] * len(extra),
                  out_specs=[_row_spec(tm, D)] * n_out, vmem_mb=40)(y2, wo, x, gate, *extra)


def _gate_bwd_tile(dx, o_ref, gate_ref, dob_ref, dgate_ref):
    dob_ref[...] = (dx * gate_ref[...]).astype(bf16)
    dgate_ref[...] += jnp.sum(dx * o_ref[...], axis=0, keepdims=True)


def _loss_bwd(x, target, g, o, gate):
    T, tm = x.shape[0], 256

    def body(x_ref, t_ref, g_ref, o_ref, gate_ref, dx_ref, loss_ref, dg_ref, dob_ref, dgate_ref):
        @pl.when(pl.program_id(0) == 0)
        def _():
            loss_ref[...] = jnp.zeros_like(loss_ref)
            dg_ref[...] = jnp.zeros_like(dg_ref)
            dgate_ref[...] = jnp.zeros_like(dgate_ref)

        xv, gv = x_ref[...], g_ref[...]
        r = lax.rsqrt(jnp.mean(xv * xv, axis=-1, keepdims=True) + EPS)
        xn = xv * r
        err = xn * gv - t_ref[...]
        dy = err * (1.0 / D)
        dxn = dy * gv
        dx = r * (dxn - xn * jnp.mean(dxn * xn, axis=-1, keepdims=True))
        dx_ref[...] = dx
        dg_ref[...] += jnp.sum(dy * xn, axis=0, keepdims=True)
        loss_ref[...] += (0.5 / D) * jnp.sum(jnp.sum(err * err, axis=1, keepdims=True), axis=0, keepdims=True)
        _gate_bwd_tile(dx, o_ref, gate_ref, dob_ref, dgate_ref)

    return _pcall(body, name="loss_bwd",
                  out_shape=[_sds((T, D), f32), _sds((1, 1), f32), _sds((1, D), f32), _sds((T, D), bf16), _sds((1, D), f32)],
                  grid=(T // tm,),
                  in_specs=[_row_spec(tm, D), _row_spec(tm, D), _vec_spec(D), _row_spec(tm, D), _vec_spec(D)],
                  out_specs=[_row_spec(tm, D), pl.BlockSpec((1, 1), lambda i: (0, 0)), _vec_spec(D), _row_spec(tm, D),
                             _vec_spec(D)])(x, target, g, o, gate)


def _norm_bwd(x, dh, gin, g, scale, below=None):
    T, tm = x.shape[0], 256

    def body(x_ref, dh_ref, gin_ref, g_ref, sc_ref, *rest):
        if below is None:
            dx_ref, st_ref = rest
        else:
            o_ref, gate_ref, dx_ref, st_ref, dob_ref, dgate_ref = rest

        @pl.when(pl.program_id(0) == 0)
        def _():
            st_ref[...] = jnp.zeros_like(st_ref)
            if below is not None:
                dgate_ref[...] = jnp.zeros_like(dgate_ref)

        xv, gv, dhv = x_ref[...], g_ref[...], dh_ref[...]
        r = lax.rsqrt(jnp.mean(xv * xv, axis=-1, keepdims=True) + EPS)
        xn = xv * r
        da = dhv * (1.0 + sc_ref[...])
        dxn = da * gv
        dx = gin_ref[...] + r * (dxn - xn * jnp.mean(dxn * xn, axis=-1, keepdims=True))
        dx_ref[...] = dx
        st_ref[0:1, :] += jnp.sum(dhv, axis=0, keepdims=True)
        st_ref[1:2, :] += jnp.sum(dhv * (xn * gv), axis=0, keepdims=True)
        st_ref[2:3, :] += jnp.sum(da * xn, axis=0, keepdims=True)
        if below is not None:
            _gate_bwd_tile(dx, o_ref, gate_ref, dob_ref, dgate_ref)

    out_shape = [_sds((T, D), f32), _sds((8, D), f32)]
    in_specs = [_row_spec(tm, D), _row_spec(tm, D), _row_spec(tm, D), _vec_spec(D), _vec_spec(D)]
    out_specs = [_row_spec(tm, D), _vec_spec(D, 8)]
    args = [x, dh, gin, g, scale]
    if below is not None:
        out_shape += [_sds((T, D), bf16), _sds((1, D), f32)]
        in_specs += [_row_spec(tm, D), _vec_spec(D)]
        out_specs += [_row_spec(tm, D), _vec_spec(D)]
        args += list(below)
    return _pcall(body, name="norm_bwd", out_shape=out_shape, grid=(T // tm,), in_specs=in_specs,
                  out_specs=out_specs)(*args)


STEPS = 4
ADAMW_STEPS = 8


def _cast_place(place, ws, layer, after=None):
    n = len(ws)

    def body(place_ref, *refs):
        for t in range(n):
            refs[-n + t][...] = refs[t][...].astype(bf16)

    def tile(w):
        return w.shape[1] // STEPS, w.shape[2]

    extra = [] if after is None else [after]
    return _pcall(body, name="cast_place", out_shape=[_sds((4,) + w.shape[1:], bf16) for w in ws], grid=(STEPS,),
                  prefetch=1,
                  in_specs=[pl.BlockSpec((None,) + tile(w), lambda i, pr: (layer, i, 0)) for w in ws] + [ANY] * len(extra),
                  out_specs=[pl.BlockSpec((None,) + tile(w), lambda i, pr: (pr[0], i, 0)) for w in ws])(
                      place, *ws, *extra)


def _rs_add(place, grads, theirs):
    n = len(grads)

    def body(place_ref, *refs):
        for t in range(n):
            refs[2 * n + t][...] = (refs[t][...].astype(f32) + refs[n + t][...].astype(f32)).astype(bf16)

    mine = [pl.BlockSpec((None, None) + g.shape[2:], lambda s, pr: (s, pr[1], 0, 0)) for g in grads]
    shard = [pl.BlockSpec((None,) + q.shape[1:], lambda s, pr: (s, 0, 0)) for q in theirs]
    return _pcall(body, name="rs_add", out_shape=[_sds(q.shape, bf16) for q in theirs], grid=(4,), prefetch=1,
                  in_specs=mine + shard, out_specs=shard)(place, *grads, *theirs)


def _rs_sum(place, pairs, slots):
    n, steps = len(pairs), 2

    def body(place_ref, *refs):
        for t in range(n):
            p_ref, q_ref = refs[t], refs[n + t]
            total = ((p_ref[...].astype(f32) + q_ref[0].astype(f32)) + q_ref[1].astype(f32)) + q_ref[2].astype(f32)
            refs[2 * n + t][...] = total.astype(bf16)

    def tile(q):
        return q.shape[1] // steps, q.shape[2]

    return _pcall(body, name="rs_sum", out_shape=[_sds((2,) + q.shape[1:], bf16) for q in slots], grid=(steps,),
                  prefetch=1,
                  in_specs=[pl.BlockSpec((None,) + tile(q), lambda i, pr: (pr[0], i, 0)) for q in slots]
                  + [pl.BlockSpec((3,) + tile(q), lambda i, pr: (0, i, 0)) for q in slots],
                  out_specs=[pl.BlockSpec((None,) + tile(q), lambda i, pr: (pr[1], i, 0)) for q in slots])(
                      place, *pairs, *slots)


def _adamw_math(w, g, m, v):
    m = ADAM_B1 * m + (1.0 - ADAM_B1) * g
    v = ADAM_B2 * v + (1.0 - ADAM_B2) * jnp.square(g)
    m_hat = m / (1.0 - ADAM_B1 ** ADAM_STEP)
    v_hat = v / (1.0 - ADAM_B2 ** ADAM_STEP)
    delta = -ADAM_LR * (m_hat / (jnp.sqrt(v_hat) + ADAM_EPS) + ADAM_WD * w)
    return delta, m, v


def _adamw_layer(layer, items):
    n = len(items)

    def body(*refs):
        outs = refs[-4 * n:]
        for t in range(n):
            w_ref, g_ref, m_ref, v_ref = refs[4 * t:4 * t + 4]
            g = g_ref[...].astype(f32)
            outs[4 * t][...] = g
            outs[4 * t + 1][...], outs[4 * t + 2][...], outs[4 * t + 3][...] = _adamw_math(
                w_ref[...], g, m_ref[...], v_ref[...])

    args, in_specs, out_specs, out_shape = [], [], [], []
    for w, g, m, v, _ in items:
        tr, cols = w.shape[1] // ADAMW_STEPS, w.shape[2]
        spec = pl.BlockSpec((None, tr, cols), lambda i: (layer, i, 0))
        args += [w, g, m, v]
        in_specs += [spec, pl.BlockSpec((tr, cols), lambda i: (i, 0)), spec, spec]
        out_specs += [spec] * 4
        out_shape += [_sds(w.shape, f32)] * 4
    aliases = {}
    for t, it in enumerate(items):
        if it[4] is not None:
            for k in range(4):
                aliases[len(args)] = 4 * t + k
                args.append(it[4][k])
                in_specs.append(ANY)
    res = _pcall(body, name="adamw", out_shape=out_shape, grid=(ADAMW_STEPS,), in_specs=in_specs, out_specs=out_specs,
                 aliases=aliases)(*args)
    return [tuple(res[4 * t:4 * t + 4]) for t in range(n)]


def _adamw_small(items):
    n = len(items)

    def body(*refs):
        ins, outs = refs[:4 * n], refs[4 * n:]
        for t in range(n):
            w_ref, g_ref, m_ref, v_ref = ins[4 * t:4 * t + 4]
            if len(g_ref.shape) == len(w_ref.shape) + 1:
                g = g_ref[0]
                for b in range(1, g_ref.shape[0]):
                    g = g + g_ref[b]
            else:
                g = g_ref[...]
            d, m, v = _adamw_math(w_ref[...], g, m_ref[...], v_ref[...])
            outs[4 * t][...], outs[4 * t + 1][...], outs[4 * t + 2][...], outs[4 * t + 3][...] = g, d, m, v

    out_shape = [_sds(w.shape, f32) for (w, _, _, _) in items for _ in range(4)]
    flat = [a for it in items for a in it]
    res = _pcall(body, name="adamw_small", out_shape=out_shape, in_specs=[VMEM] * (4 * n),
                 out_specs=[VMEM] * (4 * n))(*flat)
    return [tuple(res[4 * t:4 * t + 4]) for t in range(n)]


NN = ((1,), (0,))
NT = ((1,), (1,))
TN = ((0,), (0,))


def _mm(name, a, b, *, grid, a_spec, b_spec, out_shape, out_spec, dims, vmem_mb=None):
    def body(a_ref, b_ref, o_ref):
        r = lax.dot_general(a_ref[...], b_ref[...], (dims, ((), ())), preferred_element_type=f32)
        o_ref[...] = r.astype(o_ref.dtype)

    return _pcall(body, name=name, out_shape=out_shape, grid=grid, in_specs=[a_spec, b_spec], out_specs=out_spec,
                  vmem_mb=vmem_mb)(a, b)


def _whole(shape):
    return pl.BlockSpec(shape, lambda j: (0,) * len(shape))


def _split_spec(rows, tile, per_split):
    return pl.BlockSpec((None, rows, tile), lambda j: (j // per_split, 0, j % per_split))


class _Proj:
    def __init__(self, n, splits, tile):
        self.n, self.splits, self.tile = n, splits, tile
        self.steps = n // tile
        self.w_per = n // 4 // tile
        self.a_per = n // splits // tile
        assert self.w_per * tile * 4 == n and self.a_per * tile * splits == n

    def fwd(self, hb, wg):
        T = hb.shape[0]
        sub, tile, w_per = FWD_TILES, self.tile, self.w_per
        wide = sub * tile
        a_per = self.n // self.splits // wide
        assert a_per * wide * self.splits == self.n

        def w_tile(q):
            return pl.BlockSpec((None, D, tile), lambda j: ((sub * j + q) // w_per, 0, (sub * j + q) % w_per))

        def body(a_ref, *rest):
            w = jnp.concatenate([rest[q][...] for q in range(sub)], axis=1)
            rest[sub][...] = jnp.dot(a_ref[...], w, preferred_element_type=f32).astype(bf16)

        return _pcall(body, name="proj_fwd", out_shape=_sds((self.splits, T, self.n // self.splits), bf16),
                      grid=(self.n // wide,), in_specs=[_whole((T, D))] + [w_tile(q) for q in range(sub)],
                      out_specs=pl.BlockSpec((None, T, wide), lambda j: (j // a_per, 0, j % a_per)),
                      vmem_mb=40 if wide > 512 else None)(hb, *([wg] * sub))

    def dw(self, hb, dp):
        T = hb.shape[0]
        return _mm("proj_dw", hb, dp, grid=(self.steps,), a_spec=_whole((T, D)),
                   b_spec=_split_spec(T, self.tile, self.a_per), out_shape=_sds((4, D, self.n // 4), bf16),
                   out_spec=_split_spec(D, self.tile, self.w_per), dims=TN)

    def dh(self, dp, wg):
        T = dp.shape[1]
        sub, tile, w_per = DH_WIDE // self.tile, self.tile, self.w_per
        a_per = self.n // self.splits // DH_WIDE
        assert sub * tile == DH_WIDE and a_per * DH_WIDE * self.splits == self.n

        def w_tile(q):
            return pl.BlockSpec((None, D, tile), lambda k: ((sub * k + q) // w_per, 0, (sub * k + q) % w_per))

        def body(a_ref, *rest):
            o_ref = rest[sub]
            w = jnp.concatenate([rest[q][...] for q in range(sub)], axis=1)
            r = lax.dot_general(a_ref[...], w, (NT, ((), ())), preferred_element_type=f32)

            @pl.when(pl.program_id(0) == 0)
            def _():
                o_ref[...] = r

            @pl.when(pl.program_id(0) > 0)
            def _():
                o_ref[...] += r

        return _pcall(body, name="proj_dh", out_shape=_sds((T, D), f32), grid=(self.n // DH_WIDE,),
                      in_specs=[pl.BlockSpec((None, T, DH_WIDE), lambda k: (k // a_per, 0, k % a_per))]
                      + [w_tile(q) for q in range(sub)],
                      out_specs=_whole((T, D)), vmem_mb=40)(dp, *([wg] * sub))


EVEN_PROJ = _Proj(7 * D, 7, 256)
ODD_PROJ = _Proj(4 * D, 2, 512)


def _dy_mm(dob, wo):
    T = dob.shape[0]
    return _mm("out_dy", dob, wo, grid=(4,), a_spec=_whole((T, D)),
               b_spec=pl.BlockSpec((None, 512, D), lambda j: (j, 0, 0)), out_shape=_sds((2, T, D), f32),
               out_spec=_split_spec(T, 512, 2), dims=NT)


def _dwo_mm(y2, dob):
    T = dob.shape[0]
    return _mm("out_dw", y2, dob, grid=(4,), a_spec=_split_spec(T, 512, 2), b_spec=_whole((T, D)),
               out_shape=_sds((4, 512, D), bf16), out_spec=pl.BlockSpec((None, 512, D), lambda j: (j, 0, 0)), dims=TN)


def _head_spec(lead, T):
    return pl.BlockSpec((lead, T, HEAD), lambda h: (0, 0, h))


def _head_vec(rows):
    return pl.BlockSpec((rows, HEAD), lambda h: (0, h))


_HEAD_MAT = pl.BlockSpec((None, HEAD, HEAD), lambda h: (h, 0, 0))


def _causal():
    return lax.broadcasted_iota(jnp.int32, (HEAD, HEAD), 0) >= lax.broadcasted_iota(jnp.int32, (HEAD, HEAD), 1)


def _layernorm_head(v):
    mu = jnp.mean(v, axis=-1, keepdims=True)
    d = v - mu
    rstd = lax.rsqrt(jnp.mean(d * d, axis=-1, keepdims=True) + EPS)
    return d * rstd, rstd


def _even_fwd(p7, conv_w, ln_g, ln_b, sgu_w, sgu_bias):
    T, C = p7.shape[1], CHUNK_ROWS

    def body(p_ref, cw_ref, lg_ref, lb_ref, w_ref, b_ref, y_ref):
        w0, w1, w2 = cw_ref[0:1, :], cw_ref[1:2, :], cw_ref[2:3, :]
        wm = jnp.where(_causal(), w_ref[...], 0.0).astype(bf16)
        bias, lg, lb = b_ref[...], lg_ref[...], lb_ref[...]

        def step(i, halo):
            rows = pl.ds(pl.multiple_of(i * C, C), C)
            ah, ab, ac, az, u, v, zb = (p_ref[k, rows, :].astype(f32) for k in range(7))
            tt = ac * ah
            ext = jnp.concatenate([halo, tt], axis=0)
            cv = w2 * tt + w1 * pltpu.roll(ext, 1, 0)[HALO_CONV:] + w0 * pltpu.roll(ext, 2, 0)[HALO_CONV:]
            y_ref[0, rows, :] = (ab * cv * _silu(az)).astype(bf16)
            vhat, _ = _layernorm_head(v)
            vn = (vhat * lg + lb).astype(bf16)
            mix = jnp.concatenate([jnp.dot(wm, vn[k * HEAD:(k + 1) * HEAD], preferred_element_type=f32) + bias
                                   for k in range(C // HEAD)], axis=0)
            y_ref[1, rows, :] = (u * mix * _silu(zb)).astype(bf16)
            return tt[C - HALO_CONV:]

        lax.fori_loop(0, T // C, step, jnp.zeros((HALO_CONV, HEAD), f32))

    return _pcall(body, name="even_fwd", out_shape=_sds((2, T, D), bf16), grid=(NH,),
                  in_specs=[_head_spec(7, T), _head_vec(3), _head_vec(1), _head_vec(1), _HEAD_MAT, _HEAD_MAT],
                  out_specs=_head_spec(2, T))(p7, conv_w, ln_g, ln_b, sgu_w, sgu_bias)


def _even_bwd(p7, dy2, conv_w, ln_g, ln_b, sgu_w, sgu_bias):
    T, C = p7.shape[1], CHUNK_ROWS
    n_chunks = T // C

    def body(p_ref, dy_ref, cw_ref, lg_ref, lb_ref, w_ref, b_ref,
             dp_ref, dcw_ref, dlg_ref, dlb_ref, dw_ref, dms_ref, dcv_s):
        w0, w1, w2 = cw_ref[0:1, :], cw_ref[1:2, :], cw_ref[2:3, :]
        tri = _causal()
        wm = jnp.where(tri, w_ref[...], 0.0).astype(bf16)
        bias, lg, lb = b_ref[...], lg_ref[...], lb_ref[...]
        dw_ref[...] = jnp.zeros_like(dw_ref)
        dms_ref[...] = jnp.zeros_like(dms_ref)

        def fwd_step(i, carry):
            halo, a0, a1, a2, alg, alb = carry
            rows = pl.ds(pl.multiple_of(i * C, C), C)
            ah, ab, ac, az = (p_ref[k, rows, :].astype(f32) for k in range(4))
            dya = dy_ref[0, rows, :]
            tt = ac * ah
            ext = jnp.concatenate([halo, tt], axis=0)
            t1, t2 = pltpu.roll(ext, 1, 0)[HALO_CONV:], pltpu.roll(ext, 2, 0)[HALO_CONV:]
            cv = w2 * tt + w1 * t1 + w0 * t2
            sa, dsa = _silu_and_grad(az)
            g1 = dya * sa
            dp_ref[1, rows, :] = (g1 * cv).astype(bf16)
            dp_ref[3, rows, :] = (dya * ab * cv * dsa).astype(bf16)
            dcv = g1 * ab
            dcv_s[rows, :] = dcv
            a2 = a2 + jnp.sum(dcv * tt, axis=0, keepdims=True)
            a1 = a1 + jnp.sum(dcv * t1, axis=0, keepdims=True)
            a0 = a0 + jnp.sum(dcv * t2, axis=0, keepdims=True)

            u, zb, dyb = p_ref[4, rows, :].astype(f32), p_ref[6, rows, :].astype(f32), dy_ref[1, rows, :]
            vhat, rstd = _layernorm_head(p_ref[5, rows, :].astype(f32))
            vn = (vhat * lg + lb).astype(bf16)
            sb, dsb = _silu_and_grad(zb)
            mix = jnp.concatenate([jnp.dot(wm, vn[k * HEAD:(k + 1) * HEAD], preferred_element_type=f32) + bias
                                   for k in range(C // HEAD)], axis=0)
            dp_ref[4, rows, :] = (dyb * mix * sb).astype(bf16)
            dp_ref[6, rows, :] = (dyb * u * mix * dsb).astype(bf16)
            dmix = dyb * u * sb
            dvn_parts = []
            for k in range(C // HEAD):
                dm = dmix[k * HEAD:(k + 1) * HEAD]
                dmb = dm.astype(bf16)
                dvn_parts.append(lax.dot_general(wm, dmb, (TN, ((), ())), preferred_element_type=f32))
                dw_ref[...] += lax.dot_general(dmb, vn[k * HEAD:(k + 1) * HEAD], (NT, ((), ())),
                                               preferred_element_type=f32)
                dms_ref[...] += dm
            dvn = jnp.concatenate(dvn_parts, axis=0)
            alg = alg + jnp.sum(dvn * vhat, axis=0, keepdims=True)
            alb = alb + jnp.sum(dvn, axis=0, keepdims=True)
            dvh = dvn * lg
            dv = rstd * (dvh - jnp.mean(dvh, axis=-1, keepdims=True)
                         - vhat * jnp.mean(dvh * vhat, axis=-1, keepdims=True))
            dp_ref[5, rows, :] = dv.astype(bf16)
            return tt[C - HALO_CONV:], a0, a1, a2, alg, alb

        zrow = jnp.zeros((1, HEAD), f32)
        _, a0, a1, a2, alg, alb = lax.fori_loop(
            0, n_chunks, fwd_step, (jnp.zeros((HALO_CONV, HEAD), f32), zrow, zrow, zrow, zrow, zrow))
        dcw_ref[0:1, :], dcw_ref[1:2, :], dcw_ref[2:3, :] = a0, a1, a2
        dlg_ref[...], dlb_ref[...] = alg, alb
        dw_ref[...] = jnp.where(tri, dw_ref[...], 0.0)

        def bwd_step(k, halo):
            rows = pl.ds(pl.multiple_of((n_chunks - 1 - k) * C, C), C)
            dcv = dcv_s[rows, :]
            ext = jnp.concatenate([dcv, halo], axis=0)
            n1 = pltpu.roll(ext, C + HALO_CONV - 1, 0)[:C]
            n2 = pltpu.roll(ext, C + HALO_CONV - 2, 0)[:C]
            dtt = w2 * dcv + w1 * n1 + w0 * n2
            dp_ref[2, rows, :] = (dtt * p_ref[0, rows, :].astype(f32)).astype(bf16)
            dp_ref[0, rows, :] = (dtt * p_ref[2, rows, :].astype(f32)).astype(bf16)
            return dcv[:HALO_CONV]

        lax.fori_loop(0, n_chunks, bwd_step, jnp.zeros((HALO_CONV, HEAD), f32))

    out_shape = [_sds((7, T, D), bf16), _sds((3, D), f32), _sds((1, D), f32), _sds((1, D), f32),
                 _sds((NH, HEAD, HEAD), f32), _sds((NH, HEAD, HEAD), f32)]
    return _pcall(body, name="even_bwd", out_shape=out_shape, grid=(NH,),
                  in_specs=[_head_spec(7, T), _head_spec(2, T), _head_vec(3), _head_vec(1), _head_vec(1),
                            _HEAD_MAT, _HEAD_MAT],
                  out_specs=[_head_spec(7, T), _head_vec(3), _head_vec(1), _head_vec(1), _HEAD_MAT, _HEAD_MAT],
                  scratch=[pltpu.VMEM((T, HEAD), f32)])(p7, dy2, conv_w, ln_g, ln_b, sgu_w, sgu_bias)


def _window_sum(ext, win, towards_past):
    n, k, s = ext.shape[0], 1, ext
    while k < win:
        s = s + pltpu.roll(s, k if towards_past else n - k, 0)
        k *= 2
    return s


def _pool_count(i, C, win):
    t = i * C + lax.broadcasted_iota(jnp.int32, (C, 1), 0)
    cnt = jnp.minimum(t + 1, win).astype(f32)
    return cnt, 1.0 / cnt


def _group_specs(T):
    p_spec = pl.BlockSpec((None, T, GC), lambda g: (0, 0, g))
    z_spec = pl.BlockSpec((None, T, GC), lambda g: (1, 0, g))
    pw_spec = pl.BlockSpec((4, GC // 4, GC), lambda g: (0, g, 0))
    ps_spec = pl.BlockSpec((1, GC), lambda g: (0, g))
    y_spec = pl.BlockSpec((None, T, GC), lambda g: (g // 2, 0, g % 2))
    return p_spec, z_spec, pw_spec, ps_spec, y_spec


def _odd_fwd(p2, pool_wg, pool_scale):
    T, C = p2.shape[1], CHUNK_ROWS
    p_spec, z_spec, pw_spec, ps_spec, y_spec = _group_specs(T)

    def body(p_ref, z_ref, pw_ref, ps_ref, y_ref):
        pw, ps = pw_ref[...].reshape(GC, GC), ps_ref[...]

        def run(win):
            def step(i, halo):
                rows = pl.ds(pl.multiple_of(i * C, C), C)
                p = p_ref[rows, :].astype(f32)
                s = _window_sum(jnp.concatenate([halo, p], axis=0), win, True)[HALO_POOL:]
                pooled = s * _pool_count(i, C, win)[1] - p
                ypre = jnp.dot(pooled.astype(bf16), pw, preferred_element_type=f32)
                y_ref[rows, :] = (ypre * ps * _silu(z_ref[rows, :].astype(f32))).astype(bf16)
                return p[C - HALO_POOL:]

            lax.fori_loop(0, T // C, step, jnp.zeros((HALO_POOL, GC), f32))

        for gi, win in enumerate(WINDOWS):
            pl.when(pl.program_id(0) == gi)(functools.partial(run, win))

    return _pcall(body, name="odd_fwd", out_shape=_sds((2, T, D), bf16), grid=(len(WINDOWS),),
                  in_specs=[p_spec, z_spec, pw_spec, ps_spec], out_specs=y_spec)(p2, p2, pool_wg, pool_scale)


def _odd_bwd(p2, dy2, pool_wg, pool_scale):
    T, C = p2.shape[1], CHUNK_ROWS
    n_chunks = T // C
    p_spec, z_spec, pw_spec, ps_spec, y_spec = _group_specs(T)

    def body(p_ref, z_ref, dy_ref, pw_ref, ps_ref, dp_ref, dpw_ref, dps_ref, q_s, acc_s):
        pw, ps = pw_ref[...].reshape(GC, GC), ps_ref[...]

        def run(win):
            acc_s[...] = jnp.zeros_like(acc_s)

            def fwd_step(i, carry):
                halo, aps = carry
                rows = pl.ds(pl.multiple_of(i * C, C), C)
                p, z, dy = p_ref[rows, :].astype(f32), z_ref[rows, :].astype(f32), dy_ref[rows, :]
                _, inv_cnt = _pool_count(i, C, win)
                s = _window_sum(jnp.concatenate([halo, p], axis=0), win, True)[HALO_POOL:]
                pb = (s * inv_cnt - p).astype(bf16)
                ypre = jnp.dot(pb, pw, preferred_element_type=f32)
                sz, dsz = _silu_and_grad(z)
                aps = aps + jnp.sum(dy * ypre * sz, axis=0, keepdims=True)
                dp_ref[1, rows, :] = (dy * ypre * ps * dsz).astype(bf16)
                dyp = (dy * ps * sz).astype(bf16)
                acc_s[...] += lax.dot_general(pb, dyp, (TN, ((), ())), preferred_element_type=f32)
                dpool = lax.dot_general(dyp, pw, (NT, ((), ())), preferred_element_type=f32)
                q_s[rows, :] = dpool * inv_cnt
                return p[C - HALO_POOL:], aps

            _, aps = lax.fori_loop(0, n_chunks, fwd_step, (jnp.zeros((HALO_POOL, GC), f32), jnp.zeros((1, GC), f32)))
            dps_ref[...] = aps
            dpw_ref[...] = acc_s[...].reshape(4, GC // 4, GC).astype(bf16)

            def bwd_step(k, halo):
                i = n_chunks - 1 - k
                rows = pl.ds(pl.multiple_of(i * C, C), C)
                q = q_s[rows, :]
                s = _window_sum(jnp.concatenate([q, halo], axis=0), win, False)[:C]
                dp_ref[0, rows, :] = (s - q * _pool_count(i, C, win)[0]).astype(bf16)
                return q[:HALO_POOL]

            lax.fori_loop(0, n_chunks, bwd_step, jnp.zeros((HALO_POOL, GC), f32))

        for gi, win in enumerate(WINDOWS):
            pl.when(pl.program_id(0) == gi)(functools.partial(run, win))

    out_shape = [_sds((2, T, 2 * D), bf16), _sds((4, GC, GC), bf16), _sds((1, 2 * D), f32)]
    return _pcall(body, name="odd_bwd", out_shape=out_shape, grid=(len(WINDOWS),),
                  in_specs=[p_spec, z_spec, y_spec, pw_spec, ps_spec],
                  out_specs=[pl.BlockSpec((2, T, GC), lambda g: (0, 0, g)), pw_spec, ps_spec],
                  scratch=[pltpu.VMEM((T, GC), f32), pltpu.VMEM((GC, GC), f32)], vmem_mb=44)(
                      p2, p2, dy2, pool_wg, pool_scale)


def _ada_fwd(c_all, ada_w):
    cols = ada_w.shape[2]

    def body(c_ref, w_ref, o_ref):
        o_ref[...] = jnp.dot(_silu(c_ref[...]), w_ref[...], preferred_element_type=f32,
                             precision=lax.Precision.HIGHEST)

    return _pcall(body, name="ada_fwd", out_shape=_sds((4, N_DEV, cols), f32), grid=(4,),
                  in_specs=[pl.BlockSpec((N_DEV, D), lambda i: (0, 0)), pl.BlockSpec((None, D, cols), lambda i: (i, 0, 0))],
                  out_specs=pl.BlockSpec((None, N_DEV, cols), lambda i: (i, 0, 0)))(c_all, ada_w)


def _ada_bwd(c_all_t, dmod, w, m, v):
    cols, tr = w.shape[2], 256
    spec = pl.BlockSpec((None, tr, cols), lambda l, i: (l, i, 0))

    def body(c_ref, dm_ref, w_ref, m_ref, v_ref, g_ref, d_ref, mo_ref, vo_ref):
        sc = _silu(c_ref[...])
        g = sc[:, 0:1] * dm_ref[0:1, :]
        for b in range(1, N_DEV):
            g = g + sc[:, b:b + 1] * dm_ref[b:b + 1, :]
        g_ref[...] = g
        d_ref[...], mo_ref[...], vo_ref[...] = _adamw_math(w_ref[...], g, m_ref[...], v_ref[...])

    return _pcall(body, name="ada_bwd", out_shape=[_sds(w.shape, f32)] * 4, grid=(4, D // tr),
                  in_specs=[pl.BlockSpec((tr, N_DEV), lambda l, i: (i, 0)),
                            pl.BlockSpec((None, N_DEV, cols), lambda l, i: (l, 0, 0)), spec, spec, spec],
                  out_specs=[spec] * 4)(c_all_t, dmod, w, m, v)


def _layer_fwd(even, x, hb, gate, w, nxt, before_out=None):
    if even:
        w_in, w_out, conv_w, ln_g, ln_b, sgu_w, sgu_b = w
        bias = jnp.broadcast_to(sgu_b[:, :, None], (NH, HEAD, HEAD))
        p = EVEN_PROJ.fwd(hb, w_in)
        y2 = _even_fwd(p, conv_w, ln_g, ln_b, sgu_w, bias)
    else:
        w_in, pool_w, w_out, pool_scale = w
        p = ODD_PROJ.fwd(hb, w_in)
        y2 = _odd_fwd(p, pool_w, pool_scale)
    if before_out is not None:
        late_w_out, tok = before_out(y2)
        if late_w_out is not None:
            w_out = late_w_out
            w = (w_in, w_out) + tuple(w[2:]) if even else (w_in, pool_w, w_out, pool_scale)
        if tok is not None:
            gate = gate + tok[0:1, 0:1]
    outs = _out_proj(y2, w_out.reshape(2, D, D), x, gate, nxt)
    return outs[0], (None if nxt is None else outs[2]), (x, hb, p, y2, outs[1]), w


def _layer_bwd(even, gin, dob, dgate, saved, scale, g, w, below=None, send=None):
    x_in, hb, p, y2, o = saved
    if even:
        w_in, w_out, conv_w, ln_g, ln_b, sgu_w, sgu_b = w
        bias = jnp.broadcast_to(sgu_b[:, :, None], (NH, HEAD, HEAD))
        dy2 = _dy_mm(dob, w_out)
        dp, dconv, dlg, dlb, dsw, dms = _even_bwd(p, dy2, conv_w, ln_g, ln_b, sgu_w, bias)
        proj = EVEN_PROJ
        small = dict(conv_w=dconv, ln_g=dlg, ln_b=dlb, sgu_w=dsw, sgu_b=jnp.sum(dms, axis=-1))
        big = [proj.dw(hb, dp), _dwo_mm(y2, dob)]
    else:
        w_in, pool_w, w_out, pool_scale = w
        dy2 = _dy_mm(dob, w_out)
        dp, dpw, dps = _odd_bwd(p, dy2, pool_w, pool_scale)
        proj = ODD_PROJ
        small = dict(pool_scale=dps)
        big = [proj.dw(hb, dp), dpw, _dwo_mm(y2, dob)]
    if send is not None:
        big, tok = send(big)
        scale = scale + tok[0:1, 0:1]
    dh = proj.dh(dp, w_in)
    res = _norm_bwd(x_in, dh, gin, g, scale, below)
    stats = res[1]
    return (res[0], (None if below is None else (res[2], res[3])), big, small,
            jnp.concatenate([stats[0:2], dgate], axis=0), stats[2:3])


def _pack_rows(parts):
    rows = [p.reshape(-1, LANES) for p in parts]
    total = sum(r.shape[0] for r in rows)
    padded = -(-total // (8 * N_DEV)) * (8 * N_DEV)
    if padded > total:
        rows.append(jnp.zeros((padded - total, LANES), f32))
    return jnp.concatenate(rows, axis=0)


def _unpack_rows(buf, shapes):
    out, r = [], 0
    for shp in shapes:
        n = 1
        for d in shp:
            n *= d
        out.append(buf[r:r + n // LANES].reshape(shp))
        r += n // LANES
    return out


def kernel(x, c, norm_g, ada_w, ada_b, ab_w_in, ab_conv_w, ab_ln_g, ab_ln_b, ab_sgu_w, ab_sgu_b, ab_w_out, c_w_in, c_pool_w, c_pool_scale, c_w_out, final_g, loss_target, m_norm_g, m_ada_w, m_ada_b, m_ab_w_in, m_ab_conv_w, m_ab_ln_g, m_ab_ln_b, m_ab_sgu_w, m_ab_sgu_b, m_ab_w_out, m_c_w_in, m_c_pool_w, m_c_pool_scale, m_c_w_out, m_final_g, v_norm_g, v_ada_w, v_ada_b, v_ab_w_in, v_ab_conv_w, v_ab_ln_g, v_ab_ln_b, v_ab_sgu_w, v_ab_sgu_b, v_ab_w_out, v_c_w_in, v_c_pool_w, v_c_pool_scale, v_c_w_out, v_final_g):
    ix, iy, ic = _place()
    chip, dev = 2 * ix + iy, 4 * ix + 2 * iy + ic
    n_even, n_odd = ab_w_in.shape[0], c_w_in.shape[0]
    depth = n_even + n_odd
    acols = ada_w.shape[2]

    place = jnp.stack([chip, ic]).astype(jnp.int32)
    even_names, odd_names = ["ab_w_in", "ab_w_out"], ["c_w_in", "c_pool_w", "c_w_out"]
    params = {"ab_w_in": (ab_w_in, m_ab_w_in, v_ab_w_in), "ab_w_out": (ab_w_out, m_ab_w_out, v_ab_w_out),
              "c_w_in": (c_w_in, m_c_w_in, v_c_w_in), "c_w_out": (c_w_out, m_c_w_out, v_c_w_out),
              "c_pool_w": tuple(a.reshape(n_odd, GC, GC) for a in (c_pool_w, m_c_pool_w, v_c_pool_w))}

    first = _gather8(jnp.concatenate([c, ab_conv_w.reshape(1, -1), c_pool_scale.reshape(1, -1)], axis=1), "gather_c")
    c_all, small_all = first[:, 0, :D], first[0::2, 0, D:]
    modp = _ada_fwd(c_all, ada_w)
    modg = _gather8(modp, "gather_mod")
    mod_rows = lax.dynamic_index_in_dim(modg[0::2], dev, axis=2, keepdims=False)
    mod = jnp.transpose(mod_rows, (1, 0, 2)).reshape(depth, 3 * D) + ada_b
    mods = [(mod[i:i + 1, 0:D], mod[i:i + 1, D:2 * D], mod[i:i + 1, 2 * D:3 * D]) for i in range(depth)]

    def shard_cols(a, width):
        return lax.dynamic_slice_in_dim(a, chip * width, width, axis=a.ndim - 1)

    n_conv = ab_conv_w.size
    conv_all = small_all[:, :n_conv].reshape(4, n_even, 3, D // 4)
    conv_full = jnp.transpose(conv_all, (1, 2, 0, 3)).reshape(n_even, 3, D)
    scale_all = small_all[:, n_conv:].reshape(4, n_odd, 2 * D // 4)
    scale_full = jnp.transpose(scale_all, (1, 0, 2)).reshape(n_odd, 2 * D)

    def placed(names, layer, after=None):
        ws = [params[nm][0] for nm in names]
        return [p.reshape(4, 2, p.shape[1] // 2, p.shape[2]) for p in _cast_place(place, ws, layer, after)]

    def whole(arrays):
        return [g.reshape(4, 2 * g.shape[2], g.shape[3]) for g in arrays]

    gathers_done = mod[0:1, 0:LANES] + scale_full[0:1, 0:LANES]
    sems_a, in_a, tok = _ag_start([placed(even_names[:1], 0)], gathers_done, "ag_start_0a")
    sems_b, in_b, tok = _ag_start([placed(even_names[1:], 0, tok)], tok, "ag_start_0b")
    rest = [placed(even_names if i % 2 == 0 else odd_names, i // 2, tok) for i in range(1, depth)]
    sems_r, in_r, tok = _ag_start(rest, tok, "ag_start_rest")

    x_cur, saved, weights, handoff = x[0], [], [], {}
    hb = _hnorm(x_cur, norm_g[0:1], mods[0][0] + tok[0:1, 0:1], mods[0][1])
    for i in range(depth):
        j = i // 2
        if i == 0:
            full = whole(_ag_forward(_ag_wait(in_a[0], sems_a[0], hb, "ag_wait_0a"), "ag_forward")) + [None]
        else:
            full = whole(_agf_wait(*handoff.pop(i), x_cur, f"agf_wait_{i}"))
        if i % 2 == 0:
            w = (full[0], full[1], conv_full[j], ab_ln_g[j:j + 1], ab_ln_b[j:j + 1], ab_sgu_w[j], ab_sgu_b[j])
        else:
            w = (full[0], full[1], full[2], scale_full[j:j + 1])

        def before_out(y2, i=i):
            w_out, tok = None, None
            if i == 0:
                w_out = whole(_ag_forward(_ag_wait(in_b[0], sems_b[0], y2, "ag_wait_0b"), "ag_forward"))[0]
            if i + 1 < depth:
                arrived = _ag_wait(in_r[i], sems_r[i], y2, f"ag_wait_{i + 1}")
                sems_f, inflight, tok = _agf_start(arrived, f"agf_start_{i + 1}")
                handoff[i + 1] = (sems_f, inflight)
            return w_out, tok

        nxt = (norm_g[i + 1:i + 2], mods[i + 1][0], mods[i + 1][1]) if i + 1 < depth else None
        x_cur, hb, sv, w = _layer_fwd(i % 2 == 0, x_cur, hb, mods[i][2], w, nxt, before_out)
        weights.append(w)
        saved.append(sv)
    gin, loss, dfinal_g, dob, dgate = _loss_bwd(x_cur, loss_target[0], final_g.reshape(1, D), saved[-1][4],
                                                mods[-1][2])

    stacked = {}

    def finish(i, sems, pairs, lands, after):
        pairs, slots = _rs_chip_wait(sems, pairs, lands, after, f"rs_chip_wait_{i}")
        names = even_names if i % 2 == 0 else odd_names
        grads = _rs_half_exchange(_rs_sum(place, pairs, slots), "rs_half_exchange")
        items = [(params[nm][0], g.reshape(params[nm][0].shape[1:]), params[nm][1], params[nm][2], stacked.get(nm))
                 for nm, g in zip(names, grads)]
        for nm, res in zip(names, _adamw_layer(i // 2, items)):
            stacked[nm] = res

    small_g, dmod, dnorm_g, pending, tok = [None] * depth, [None] * depth, [None] * depth, None, None
    for i in reversed(range(depth)):
        w = weights[i]
        if tok is not None:
            w = w[:2] + (w[2] + tok[0:1, 0:1],) + w[3:] if i % 2 == 0 else w[:3] + (w[3] + tok[0:1, 0:1],)
        below = (saved[i - 1][4], mods[i - 1][2]) if i > 0 else None

        def send(big_g, i=i):
            big_g = [g.reshape(4, 2, g.shape[1] // 2, g.shape[2]) for g in big_g]
            sems, big_g, lands, tok = _rs_pair_start(big_g, f"rs_pair_start_{i}")
            return (sems, big_g, lands), tok

        gin, gate_bwd, sent, small_g[i], dmod[i], dnorm_g[i] = _layer_bwd(
            i % 2 == 0, gin, dob, dgate, saved[i], mods[i][1], norm_g[i:i + 1], w, below, send)
        if below is not None:
            dob, dgate = gate_bwd
        big_g, theirs = _rs_pair_wait(*sent, gin, f"rs_pair_wait_{i}")
        pairs = _rs_add(place, big_g, theirs)
        sems, pairs, lands, tok = _rs_chip_start(pairs, f"rs_chip_start_{i}")
        if pending is not None:
            finish(*pending, tok)
        pending = (i, sems, pairs, lands)
    grad_x = gin
    dmod, dnorm_g = jnp.stack(dmod), jnp.concatenate(dnorm_g, axis=0)

    small_parts = [dnorm_g + tok[0:1, 0:1], dfinal_g,
                   jnp.stack([small_g[2 * j]["conv_w"] for j in range(n_even)]),
                   jnp.concatenate([small_g[2 * j]["ln_g"] for j in range(n_even)], axis=0),
                   jnp.concatenate([small_g[2 * j]["ln_b"] for j in range(n_even)], axis=0),
                   jnp.stack([small_g[2 * j]["sgu_b"] for j in range(n_even)]),
                   jnp.concatenate([small_g[2 * j + 1]["pool_scale"] for j in range(n_odd)], axis=0),
                   jnp.pad(loss, ((0, 7), (0, LANES - 1)))]
    small_shapes = [p.shape for p in small_parts]
    sgu_parts = [small_g[2 * j]["sgu_w"].reshape(NH * HEAD, HEAD) for j in range(n_even)]
    reduced = _allreduce8([_pack_rows(small_parts)] + sgu_parts, "allreduce_small")
    g_norm_g, g_final_g, g_conv_full, g_ln_g, g_ln_b, g_sgu_b, g_scale_full, loss_row = _unpack_rows(reduced[0],
                                                                                                     small_shapes)
    g_sgu_w = jnp.stack(reduced[1:])
    loss = loss_row[0, 0]
    g_conv = shard_cols(g_conv_full, D // 4)
    g_scale = shard_cols(g_scale_full, 2 * D // 4)
    dmod_all = _gather8(dmod.reshape(depth * 3 * D // LANES, LANES), "gather_dmod").reshape(N_DEV, depth, 3 * D)

    def two_d(a):
        return a.reshape(-1, a.shape[-1])

    small = [(norm_g, g_norm_g, m_norm_g, v_norm_g),
             (ada_b, dmod_all, m_ada_b, v_ada_b),
             (two_d(ab_conv_w), two_d(g_conv), two_d(m_ab_conv_w), two_d(v_ab_conv_w)),
             (ab_ln_g, g_ln_g, m_ab_ln_g, v_ab_ln_g),
             (ab_ln_b, g_ln_b, m_ab_ln_b, v_ab_ln_b),
             (two_d(ab_sgu_w), two_d(g_sgu_w), two_d(m_ab_sgu_w), two_d(v_ab_sgu_w)),
             (two_d(ab_sgu_b), two_d(g_sgu_b), two_d(m_ab_sgu_b), two_d(v_ab_sgu_b)),
             (c_pool_scale, g_scale, m_c_pool_scale, v_c_pool_scale),
             (final_g.reshape(1, D), g_final_g, m_final_g.reshape(1, D), v_final_g.reshape(1, D))]
    small_res = _adamw_small(small)
    small_shapes_out = [norm_g.shape, ada_b.shape, ab_conv_w.shape, ab_ln_g.shape, ab_ln_b.shape, ab_sgu_w.shape,
                        ab_sgu_b.shape, c_pool_scale.shape, final_g.shape]
    (r_norm_g, r_ada_b, r_conv, r_ln_g, r_ln_b, r_sgu_w, r_sgu_b, r_scale, r_final_g) = [
        tuple(a.reshape(shp) for a in res) for res, shp in zip(small_res, small_shapes_out)]

    dmod_cols = jnp.transpose(shard_cols(dmod_all, acols), (1, 0, 2))
    r_ada_w = _ada_bwd(c_all.T, dmod_cols, ada_w, m_ada_w, v_ada_w)

    finish(*pending, r_ada_w[1])
    r_ab_w_in, r_ab_w_out, r_c_w_in, r_c_w_out = (stacked[nm] for nm in ("ab_w_in", "ab_w_out", "c_w_in", "c_w_out"))
    r_c_pool_w = tuple(a.reshape(c_pool_w.shape) for a in stacked["c_pool_w"])

    order = [r_norm_g, r_ada_w, r_ada_b, r_ab_w_in, r_conv, r_ln_g, r_ln_b, r_sgu_w, r_sgu_b, r_ab_w_out,
             r_c_w_in, r_c_pool_w, r_scale, r_c_w_out, r_final_g]
    outs = [loss, grad_x[None]]
    for field in range(4):
        outs += [r[field] for r in order]
    return tuple(outs)
```

```python
import functools

import jax
import jax.numpy as jnp
from jax import lax
from jax.experimental import pallas as pl
from jax.experimental.pallas import tpu as pltpu

f32, bf16 = jnp.float32, jnp.bfloat16

D = 1024
HEAD = 128
NH = 8
WINDOWS = (2, 4, 8, 16)
GC = 512
EPS = 1e-6
HALO_CONV = 8
HALO_POOL = 16
CHUNK_ROWS = 512
DH_WIDE = 1024
FWD_TILES = 2
N_DEV = 8
LANES = 128

ADAM_LR, ADAM_B1, ADAM_B2, ADAM_EPS, ADAM_WD, ADAM_STEP = 0.001, 0.9, 0.999, 1e-08, 0.01, 10

MESH = pl.DeviceIdType.MESH
ANY = pl.BlockSpec(memory_space=pl.ANY)
VMEM = pl.BlockSpec(memory_space=pltpu.VMEM)
MIB = 2 ** 20


def _pcall(body, *, name, out_shape, grid=None, in_specs=None, out_specs=None, scratch=(), vmem_mb=None,
           aliases=None, prefetch=0):
    kw = {}
    if prefetch:
        kw["grid_spec"] = pltpu.PrefetchScalarGridSpec(num_scalar_prefetch=prefetch, grid=grid, in_specs=in_specs,
                                                       out_specs=out_specs, scratch_shapes=list(scratch))
    else:
        if grid is not None:
            kw["grid"] = grid
        if in_specs is not None:
            kw["in_specs"] = in_specs
        if out_specs is not None:
            kw["out_specs"] = out_specs
        if scratch:
            kw["scratch_shapes"] = list(scratch)
    if aliases:
        kw["input_output_aliases"] = aliases
    params = pltpu.CompilerParams(vmem_limit_bytes=None if vmem_mb is None else vmem_mb * MIB)
    return pl.pallas_call(body, name=name, out_shape=out_shape, compiler_params=params, **kw)


def _sds(shape, dtype):
    return jax.ShapeDtypeStruct(tuple(shape), dtype)


def _sigmoid(z):
    return pl.reciprocal(1.0 + jnp.exp(-z), approx=True)


def _silu(z):
    return z * _sigmoid(z)


def _silu_and_grad(z):
    s = _sigmoid(z)
    return z * s, s * (1.0 + z * (1.0 - s))


def _place():
    return lax.axis_index("x"), lax.axis_index("y"), lax.axis_index("c")


def _gather8(blk, name):
    def body(x_ref, o_ref, ssem, rsem):
        x, y, c = _place()
        me = 4 * x + 2 * y + c
        o_ref[me] = x_ref[...]
        sends = []
        for k in range(1, N_DEV):
            px = 1 - x if k & 4 else x
            py = 1 - y if k & 2 else y
            pc = 1 - c if k & 1 else c
            cp = pltpu.make_async_remote_copy(src_ref=x_ref, dst_ref=o_ref.at[me], send_sem=ssem.at[k - 1],
                                              recv_sem=rsem.at[k - 1], device_id=(px, py, pc), device_id_type=MESH)
            cp.start()
            sends.append((cp, 4 * px + 2 * py + pc))
        for k, (cp, peer) in enumerate(sends):
            pltpu.make_async_remote_copy(src_ref=x_ref, dst_ref=o_ref.at[peer], send_sem=ssem.at[k],
                                         recv_sem=rsem.at[k], device_id=(x, y, c), device_id_type=MESH).wait_recv()
        for cp, _ in sends:
            cp.wait_send()

    return _pcall(body, name=name, out_shape=_sds((N_DEV,) + blk.shape, blk.dtype), in_specs=[VMEM], out_specs=VMEM,
                  scratch=[pltpu.SemaphoreType.DMA((N_DEV - 1,)), pltpu.SemaphoreType.DMA((N_DEV - 1,))])(blk)


def _allreduce8(bufs, name, after=None):
    n, n_after = len(bufs), 0 if after is None else 1
    rbs = [b.shape[0] // N_DEV for b in bufs]
    assert all(rb * N_DEV == b.shape[0] and rb % 8 == 0 for rb, b in zip(rbs, bufs))

    def body(*refs):
        refs = refs[:n] + refs[n + n_after:]
        xs, outs, stages = refs[:n], refs[n:2 * n], refs[2 * n:3 * n]
        ssem, rsem = refs[3 * n:]
        x, y, c = _place()
        me = 4 * x + 2 * y + c
        peers = []
        for k in range(1, N_DEV):
            px = 1 - x if k & 4 else x
            py = 1 - y if k & 2 else y
            pc = 1 - c if k & 1 else c
            peers.append(((px, py, pc), 4 * px + 2 * py + pc))

        def blk(t, ref, idx):
            return ref.at[pl.ds(pl.multiple_of(idx * rbs[t], 8), rbs[t]), :]

        def copy(t, phase, k, src, dst, dev):
            return pltpu.make_async_remote_copy(src_ref=src, dst_ref=dst, send_sem=ssem.at[t, phase, k],
                                                recv_sem=rsem.at[t, phase, k], device_id=dev, device_id_type=MESH)

        scatter = [copy(t, 0, k, blk(t, xs[t], pidx), stages[t].at[me], dev)
                   for t in range(n) for k, (dev, pidx) in enumerate(peers)]
        for cp in scatter:
            cp.start()
        gather = []
        for t in range(n):
            stages[t][me] = blk(t, xs[t], me)[...]
            for k, (dev, pidx) in enumerate(peers):
                copy(t, 0, k, blk(t, xs[t], pidx), stages[t].at[pidx], dev).wait_recv()
            total = stages[t][0]
            for j in range(1, N_DEV):
                total = total + stages[t][j]
            blk(t, outs[t], me)[...] = total
            sends = [copy(t, 1, k, blk(t, outs[t], me), blk(t, outs[t], me), dev) for k, (dev, pidx) in enumerate(peers)]
            for cp in sends:
                cp.start()
            gather += sends
        for t in range(n):
            for k, (dev, pidx) in enumerate(peers):
                copy(t, 1, k, blk(t, outs[t], pidx), blk(t, outs[t], pidx), dev).wait_recv()
        for cp in scatter + gather:
            cp.wait_send()

    return _pcall(body, name=name, out_shape=[_sds(b.shape, f32) for b in bufs], in_specs=[VMEM] * n + [ANY] * n_after,
                  out_specs=[VMEM] * n,
                  scratch=[pltpu.VMEM((N_DEV, rb, LANES), f32) for rb in rbs]
                  + [pltpu.SemaphoreType.DMA((n, 2, N_DEV - 1)), pltpu.SemaphoreType.DMA((n, 2, N_DEV - 1))])(
                      *bufs, *([] if after is None else [after]))


def _other_chips(x, y):
    return [((1 - x, y), 2 * (1 - x) + y), ((x, 1 - y), 2 * x + (1 - y)), ((1 - x, 1 - y), 2 * (1 - x) + (1 - y))]


HBM = pl.BlockSpec(memory_space=pltpu.HBM)
SEM = pl.BlockSpec(memory_space=pltpu.SEMAPHORE)
EFFECT = pltpu.SideEffectType.DATAFLOW_SIDE_EFFECTING


def _in_hbm(a):
    return pltpu.with_memory_space_constraint(a, pltpu.HBM)


def _ag_start(layers, after, name):
    flat = [t for lay in layers for t in lay]
    n, nl = len(flat), len(layers)

    def body(*refs):
        src = refs[:n]
        sems = refs[n + 1:n + 1 + 2 * nl]
        token = refs[-1]
        x, y, c = _place()
        s_me = 2 * x + y
        t = 0
        for i, lay in enumerate(layers):
            for k in range(len(lay)):
                for j, ((px, py), _) in enumerate(_other_chips(x, y)):
                    pltpu.make_async_remote_copy(src_ref=src[t].at[s_me, c], dst_ref=src[t].at[s_me, c],
                                                 send_sem=sems[2 * i].at[3 * k + j], recv_sem=sems[2 * i + 1].at[3 * k + j],
                                                 device_id=(px, py, c), device_id_type=MESH).start()
                t += 1
        token[...] = jnp.zeros_like(token)

    sem_shapes = [pltpu.SemaphoreType.DMA((3 * len(lay),)) for lay in layers for _ in range(2)]
    out_shape = sem_shapes + [pltpu.HBM(t.shape, t.dtype) for t in flat] + [_sds((8, LANES), f32)]
    outs = pl.pallas_call(
        body, name=name, out_shape=out_shape, in_specs=[HBM] * n + [ANY],
        out_specs=[SEM] * (2 * nl) + [HBM] * n + [VMEM], input_output_aliases={t: 2 * nl + t for t in range(n)},
        compiler_params=pltpu.CompilerParams(has_side_effects=EFFECT))(*[_in_hbm(t) for t in flat], after)
    sems = [(outs[2 * i], outs[2 * i + 1]) for i in range(nl)]
    thru, t = [], 2 * nl
    for lay in layers:
        thru.append(list(outs[t:t + len(lay)]))
        t += len(lay)
    return sems, thru, outs[-1]


def _ag_wait(inflight, sems, after, name):
    n = len(inflight)

    def body(*refs):
        src, ssem, rsem = refs[:n], refs[n], refs[n + 1]
        x, y, c = _place()
        s_me = 2 * x + y
        for k in range(n):
            for j, (_, s_p) in enumerate(_other_chips(x, y)):
                cp = pltpu.make_async_remote_copy(src_ref=src[k].at[s_me, c], dst_ref=src[k].at[s_p, c],
                                                  send_sem=ssem.at[3 * k + j], recv_sem=rsem.at[3 * k + j],
                                                  device_id=(x, y, c), device_id_type=MESH)
                cp.wait_send()
                cp.wait_recv()

    return pl.pallas_call(
        body, name=name, out_shape=[pltpu.HBM(t.shape, t.dtype) for t in inflight],
        in_specs=[HBM] * n + [SEM, SEM, ANY], out_specs=[HBM] * n, input_output_aliases={t: t for t in range(n)},
        compiler_params=pltpu.CompilerParams(has_side_effects=EFFECT))(*inflight, sems[0], sems[1], after)


def _ag_forward(arrived, name):
    n = len(arrived)

    def body(*refs):
        o = refs[n:2 * n]
        ssem, rsem = refs[2 * n:]
        x, y, c = _place()

        def copy(t, j, s, half, dev):
            return pltpu.make_async_remote_copy(src_ref=o[t].at[s, c], dst_ref=o[t].at[s, half], send_sem=ssem.at[t, j],
                                                recv_sem=rsem.at[t, j], device_id=dev, device_id_type=MESH)

        chips = _other_chips(x, y)
        sends = [copy(t, j, s_p, c, (x, y, 1 - c)) for t in range(n) for j, (_, s_p) in enumerate(chips)]
        for cp in sends:
            cp.start()
        for t in range(n):
            for j, (_, s_p) in enumerate(chips):
                copy(t, j, s_p, 1 - c, (x, y, c)).wait_recv()
        for cp in sends:
            cp.wait_send()

    return _pcall(body, name=name, out_shape=[_sds(p.shape, bf16) for p in arrived], in_specs=[ANY] * n,
                  out_specs=[ANY] * n, aliases={t: t for t in range(n)},
                  scratch=[pltpu.SemaphoreType.DMA((n, 3)), pltpu.SemaphoreType.DMA((n, 3))])(*arrived)


def _agf_start(arrived, name):
    n = len(arrived)

    def body(*refs):
        o = refs[:n]
        ssem, rsem, token = refs[n], refs[n + 1], refs[-1]
        x, y, c = _place()
        for t in range(n):
            for j, (_, s_p) in enumerate(_other_chips(x, y)):
                pltpu.make_async_remote_copy(src_ref=o[t].at[s_p, c], dst_ref=o[t].at[s_p, c],
                                             send_sem=ssem.at[3 * t + j], recv_sem=rsem.at[3 * t + j],
                                             device_id=(x, y, 1 - c), device_id_type=MESH).start()
        token[...] = jnp.zeros_like(token)

    out_shape = ([pltpu.SemaphoreType.DMA((3 * n,))] * 2 + [pltpu.HBM(a.shape, bf16) for a in arrived]
                 + [_sds((8, LANES), f32)])
    outs = pl.pallas_call(
        body, name=name, out_shape=out_shape, in_specs=[HBM] * n, out_specs=[SEM, SEM] + [HBM] * n + [VMEM],
        input_output_aliases={t: 2 + t for t in range(n)},
        compiler_params=pltpu.CompilerParams(has_side_effects=EFFECT))(*[_in_hbm(a) for a in arrived])
    return (outs[0], outs[1]), list(outs[2:2 + n]), outs[-1]


def _agf_wait(sems, inflight, after, name):
    n = len(inflight)

    def body(*refs):
        o, ssem, rsem = refs[:n], refs[n], refs[n + 1]
        x, y, c = _place()
        for t in range(n):
            for j, (_, s_p) in enumerate(_other_chips(x, y)):
                cp = pltpu.make_async_remote_copy(src_ref=o[t].at[s_p, c], dst_ref=o[t].at[s_p, 1 - c],
                                                  send_sem=ssem.at[3 * t + j], recv_sem=rsem.at[3 * t + j],
                                                  device_id=(x, y, c), device_id_type=MESH)
                cp.wait_send()
                cp.wait_recv()

    return pl.pallas_call(
        body, name=name, out_shape=[pltpu.HBM(a.shape, bf16) for a in inflight],
        in_specs=[HBM] * n + [SEM, SEM, ANY], out_specs=[HBM] * n, input_output_aliases={t: t for t in range(n)},
        compiler_params=pltpu.CompilerParams(has_side_effects=EFFECT))(*inflight, sems[0], sems[1], after)


def _rs_pair_start(grads, name):
    n = len(grads)

    def body(*refs):
        g, theirs = refs[:n], refs[n:2 * n]
        ssem, rsem, token = refs[2 * n], refs[2 * n + 1], refs[-1]
        x, y, c = _place()
        for t in range(n):
            pltpu.make_async_remote_copy(src_ref=g[t].at[:, 1 - c], dst_ref=theirs[t], send_sem=ssem.at[t],
                                         recv_sem=rsem.at[t], device_id=(x, y, 1 - c), device_id_type=MESH).start()
        token[...] = jnp.zeros_like(token)

    lands = [lax.empty((4,) + g.shape[2:], bf16) for g in grads]
    out_shape = ([pltpu.SemaphoreType.DMA((n,))] * 2 + [pltpu.HBM(g.shape, bf16) for g in grads]
                 + [pltpu.HBM(q.shape, bf16) for q in lands] + [_sds((8, LANES), f32)])
    outs = pl.pallas_call(
        body, name=name, out_shape=out_shape, in_specs=[HBM] * (2 * n), out_specs=[SEM, SEM] + [HBM] * (2 * n) + [VMEM],
        input_output_aliases={t: 2 + t for t in range(2 * n)},
        compiler_params=pltpu.CompilerParams(has_side_effects=EFFECT))(*[_in_hbm(a) for a in list(grads) + lands])
    return (outs[0], outs[1]), list(outs[2:2 + n]), list(outs[2 + n:2 + 2 * n]), outs[-1]


def _rs_pair_wait(sems, grads, lands, after, name):
    n = len(grads)

    def body(*refs):
        g, theirs = refs[:n], refs[n:2 * n]
        ssem, rsem = refs[2 * n], refs[2 * n + 1]
        x, y, c = _place()
        for t in range(n):
            cp = pltpu.make_async_remote_copy(src_ref=g[t].at[:, 1 - c], dst_ref=theirs[t], send_sem=ssem.at[t],
                                              recv_sem=rsem.at[t], device_id=(x, y, c), device_id_type=MESH)
            cp.wait_send()
            cp.wait_recv()

    outs = pl.pallas_call(
        body, name=name, out_shape=[pltpu.HBM(a.shape, bf16) for a in list(grads) + list(lands)],
        in_specs=[HBM] * (2 * n) + [SEM, SEM, ANY], out_specs=[HBM] * (2 * n),
        input_output_aliases={t: t for t in range(2 * n)},
        compiler_params=pltpu.CompilerParams(has_side_effects=EFFECT))(*grads, *lands, sems[0], sems[1], after)
    return list(outs[:n]), list(outs[n:])


def _rs_chip_start(pairs, name):
    n = len(pairs)

    def body(*refs):
        p, q = refs[:n], refs[n:2 * n]
        ssem, rsem, token = refs[2 * n], refs[2 * n + 1], refs[-1]
        x, y, c = _place()
        for t in range(n):
            for j, ((px, py), s_p) in enumerate(_other_chips(x, y)):
                pltpu.make_async_remote_copy(src_ref=p[t].at[s_p], dst_ref=q[t].at[j], send_sem=ssem.at[3 * t + j],
                                             recv_sem=rsem.at[3 * t + j], device_id=(px, py, c), device_id_type=MESH).start()
        token[...] = jnp.zeros_like(token)

    lands = [lax.empty((3,) + p.shape[1:], bf16) for p in pairs]
    out_shape = ([pltpu.SemaphoreType.DMA((3 * n,))] * 2 + [pltpu.HBM(p.shape, bf16) for p in pairs]
                 + [pltpu.HBM(q.shape, bf16) for q in lands] + [_sds((8, LANES), f32)])
    outs = pl.pallas_call(
        body, name=name, out_shape=out_shape, in_specs=[HBM] * (2 * n), out_specs=[SEM, SEM] + [HBM] * (2 * n) + [VMEM],
        input_output_aliases={t: 2 + t for t in range(2 * n)},
        compiler_params=pltpu.CompilerParams(has_side_effects=EFFECT))(*[_in_hbm(a) for a in list(pairs) + lands])
    return (outs[0], outs[1]), list(outs[2:2 + n]), list(outs[2 + n:2 + 2 * n]), outs[-1]


def _rs_chip_wait(sems, pairs, lands, after, name):
    n = len(pairs)

    def body(*refs):
        p, q = refs[:n], refs[n:2 * n]
        ssem, rsem = refs[2 * n], refs[2 * n + 1]
        x, y, c = _place()
        for t in range(n):
            for j, (_, s_p) in enumerate(_other_chips(x, y)):
                cp = pltpu.make_async_remote_copy(src_ref=p[t].at[s_p], dst_ref=q[t].at[j], send_sem=ssem.at[3 * t + j],
                                                  recv_sem=rsem.at[3 * t + j], device_id=(x, y, c), device_id_type=MESH)
                cp.wait_send()
                cp.wait_recv()

    outs = pl.pallas_call(
        body, name=name, out_shape=[pltpu.HBM(a.shape, bf16) for a in list(pairs) + list(lands)],
        in_specs=[HBM] * (2 * n) + [SEM, SEM, ANY], out_specs=[HBM] * (2 * n),
        input_output_aliases={t: t for t in range(2 * n)},
        compiler_params=pltpu.CompilerParams(has_side_effects=EFFECT))(*pairs, *lands, sems[0], sems[1], after)
    return list(outs[:n]), list(outs[n:])


def _rs_half_exchange(halves, name):
    n = len(halves)

    def body(*refs):
        o = refs[n:2 * n]
        ssem, rsem = refs[2 * n:]
        x, y, c = _place()

        def copy(t, half, dev):
            return pltpu.make_async_remote_copy(src_ref=o[t].at[c], dst_ref=o[t].at[half], send_sem=ssem.at[t],
                                                recv_sem=rsem.at[t], device_id=dev, device_id_type=MESH)

        sends = [copy(t, c, (x, y, 1 - c)) for t in range(n)]
        for cp in sends:
            cp.start()
        for t in range(n):
            copy(t, 1 - c, (x, y, c)).wait_recv()
        for cp in sends:
            cp.wait_send()

    return _pcall(body, name=name, out_shape=[_sds(h.shape, h.dtype) for h in halves], in_specs=[ANY] * n,
                  out_specs=[ANY] * n, aliases={t: t for t in range(n)},
                  scratch=[pltpu.SemaphoreType.DMA((n,)), pltpu.SemaphoreType.DMA((n,))])(*halves)


def _row_spec(tm, cols):
    return pl.BlockSpec((tm, cols), lambda i: (i, 0))


def _vec_spec(cols, rows=1):
    return pl.BlockSpec((rows, cols), lambda i: (0, 0))


def _modulated_norm(xv, g, shift, scale):
    r = lax.rsqrt(jnp.mean(xv * xv, axis=-1, keepdims=True) + EPS)
    return (((xv * r) * g) * (1.0 + scale) + shift).astype(bf16)


def _hnorm(x, g, shift, scale):
    T, tm = x.shape[0], 256

    def body(x_ref, g_ref, sh_ref, sc_ref, h_ref):
        h_ref[...] = _modulated_norm(x_ref[...], g_ref[...], sh_ref[...], sc_ref[...])

    return _pcall(body, name="hnorm", out_shape=_sds((T, D), bf16), grid=(T // tm,),
                  in_specs=[_row_spec(tm, D), _vec_spec(D), _vec_spec(D), _vec_spec(D)],
                  out_specs=_row_spec(tm, D))(x, g, shift, scale)


def _out_proj(y2, wo, x, gate, nxt=None):
    T, tm = x.shape[0], 512

    def body(y_ref, w_ref, x_ref, g_ref, *rest):
        o = jnp.dot(y_ref[0], w_ref[0], preferred_element_type=f32)
        o = o + jnp.dot(y_ref[1], w_ref[1], preferred_element_type=f32)
        xo = x_ref[...] + g_ref[...] * o
        if nxt is None:
            xo_ref, o_ref = rest
        else:
            ng_ref, nsh_ref, nsc_ref, xo_ref, o_ref, h_ref = rest
            h_ref[...] = _modulated_norm(xo, ng_ref[...], nsh_ref[...], nsc_ref[...])
        o_ref[...] = o
        xo_ref[...] = xo

    extra = [] if nxt is None else list(nxt)
    n_out = 2 if nxt is None else 3
    return _pcall(body, name="out_proj", out_shape=[_sds((T, D), f32), _sds((T, D), f32), _sds((T, D), bf16)][:n_out],
                  grid=(T // tm,),
                  in_specs=[pl.BlockSpec((2, tm, D), lambda i: (0, i, 0)), pl.BlockSpec((2, D, D), lambda i: (0, 0, 0)),
                            _row_spec(tm, D), _vec_spec(D)] + [_vec_spec(D)] * len(extra),
                  out_specs=[_row_spec(tm, D)] * n_out, vmem_mb=40)(y2, wo, x, gate, *extra)


def _gate_bwd_tile(dx, o_ref, gate_ref, dob_ref, dgate_ref):
    dob_ref[...] = (dx * gate_ref[...]).astype(bf16)
    dgate_ref[...] += jnp.sum(dx * o_ref[...], axis=0, keepdims=True)


def _loss_bwd(x, target, g, o, gate):
    T, tm = x.shape[0], 256

    def body(x_ref, t_ref, g_ref, o_ref, gate_ref, dx_ref, loss_ref, dg_ref, dob_ref, dgate_ref):
        @pl.when(pl.program_id(0) == 0)
        def _():
            loss_ref[...] = jnp.zeros_like(loss_ref)
            dg_ref[...] = jnp.zeros_like(dg_ref)
            dgate_ref[...] = jnp.zeros_like(dgate_ref)

        xv, gv = x_ref[...], g_ref[...]
        r = lax.rsqrt(jnp.mean(xv * xv, axis=-1, keepdims=True) + EPS)
        xn = xv * r
        err = xn * gv - t_ref[...]
        dy = err * (1.0 / D)
        dxn = dy * gv
        dx = r * (dxn - xn * jnp.mean(dxn * xn, axis=-1, keepdims=True))
        dx_ref[...] = dx
        dg_ref[...] += jnp.sum(dy * xn, axis=0, keepdims=True)
        loss_ref[...] += (0.5 / D) * jnp.sum(jnp.sum(err * err, axis=1, keepdims=True), axis=0, keepdims=True)
        _gate_bwd_tile(dx, o_ref, gate_ref, dob_ref, dgate_ref)

    return _pcall(body, name="loss_bwd",
                  out_shape=[_sds((T, D), f32), _sds((1, 1), f32), _sds((1, D), f32), _sds((T, D), bf16), _sds((1, D), f32)],
                  grid=(T // tm,),
                  in_specs=[_row_spec(tm, D), _row_spec(tm, D), _vec_spec(D), _row_spec(tm, D), _vec_spec(D)],
                  out_specs=[_row_spec(tm, D), pl.BlockSpec((1, 1), lambda i: (0, 0)), _vec_spec(D), _row_spec(tm, D),
                             _vec_spec(D)])(x, target, g, o, gate)


def _norm_bwd(x, dh, gin, g, scale, below=None):
    T, tm = x.shape[0], 256

    def body(x_ref, dh_ref, gin_ref, g_ref, sc_ref, *rest):
        if below is None:
            dx_ref, st_ref = rest
        else:
            o_ref, gate_ref, dx_ref, st_ref, dob_ref, dgate_ref = rest

        @pl.when(pl.program_id(0) == 0)
        def _():
            st_ref[...] = jnp.zeros_like(st_ref)
            if below is not None:
                dgate_ref[...] = jnp.zeros_like(dgate_ref)

        xv, gv, dhv = x_ref[...], g_ref[...], dh_ref[...]
        r = lax.rsqrt(jnp.mean(xv * xv, axis=-1, keepdims=True) + EPS)
        xn = xv * r
        da = dhv * (1.0 + sc_ref[...])
        dxn = da * gv
        dx = gin_ref[...] + r * (dxn - xn * jnp.mean(dxn * xn, axis=-1, keepdims=True))
        dx_ref[...] = dx
        st_ref[0:1, :] += jnp.sum(dhv, axis=0, keepdims=True)
        st_ref[1:2, :] += jnp.sum(dhv * (xn * gv), axis=0, keepdims=True)
        st_ref[2:3, :] += jnp.sum(da * xn, axis=0, keepdims=True)
        if below is not None:
            _gate_bwd_tile(dx, o_ref, gate_ref, dob_ref, dgate_ref)

    out_shape = [_sds((T, D), f32), _sds((8, D), f32)]
    in_specs = [_row_spec(tm, D), _row_spec(tm, D), _row_spec(tm, D), _vec_spec(D), _vec_spec(D)]
    out_specs = [_row_spec(tm, D), _vec_spec(D, 8)]
    args = [x, dh, gin, g, scale]
    if below is not None:
        out_shape += [_sds((T, D), bf16), _sds((1, D), f32)]
        in_specs += [_row_spec(tm, D), _vec_spec(D)]
        out_specs += [_row_spec(tm, D), _vec_spec(D)]
        args += list(below)
    return _pcall(body, name="norm_bwd", out_shape=out_shape, grid=(T // tm,), in_specs=in_specs,
                  out_specs=out_specs)(*args)


STEPS = 4
ADAMW_STEPS = 8


def _cast_place(place, ws, layer, after=None):
    n = len(ws)

    def body(place_ref, *refs):
        for t in range(n):
            refs[-n + t][...] = refs[t][...].astype(bf16)

    def tile(w):
        return w.shape[1] // STEPS, w.shape[2]

    extra = [] if after is None else [after]
    return _pcall(body, name="cast_place", out_shape=[_sds((4,) + w.shape[1:], bf16) for w in ws], grid=(STEPS,),
                  prefetch=1,
                  in_specs=[pl.BlockSpec((None,) + tile(w), lambda i, pr: (layer, i, 0)) for w in ws] + [ANY] * len(extra),
                  out_specs=[pl.BlockSpec((None,) + tile(w), lambda i, pr: (pr[0], i, 0)) for w in ws])(
                      place, *ws, *extra)


def _rs_add(place, grads, theirs):
    n = len(grads)

    def body(place_ref, *refs):
        for t in range(n):
            refs[2 * n + t][...] = (refs[t][...].astype(f32) + refs[n + t][...].astype(f32)).astype(bf16)

    mine = [pl.BlockSpec((None, None) + g.shape[2:], lambda s, pr: (s, pr[1], 0, 0)) for g in grads]
    shard = [pl.BlockSpec((None,) + q.shape[1:], lambda s, pr: (s, 0, 0)) for q in theirs]
    return _pcall(body, name="rs_add", out_shape=[_sds(q.shape, bf16) for q in theirs], grid=(4,), prefetch=1,
                  in_specs=mine + shard, out_specs=shard)(place, *grads, *theirs)


def _rs_sum(place, pairs, slots):
    n, steps = len(pairs), 2

    def body(place_ref, *refs):
        for t in range(n):
            p_ref, q_ref = refs[t], refs[n + t]
            total = ((p_ref[...].astype(f32) + q_ref[0].astype(f32)) + q_ref[1].astype(f32)) + q_ref[2].astype(f32)
            refs[2 * n + t][...] = total.astype(bf16)

    def tile(q):
        return q.shape[1] // steps, q.shape[2]

    return _pcall(body, name="rs_sum", out_shape=[_sds((2,) + q.shape[1:], bf16) for q in slots], grid=(steps,),
                  prefetch=1,
                  in_specs=[pl.BlockSpec((None,) + tile(q), lambda i, pr: (pr[0], i, 0)) for q in slots]
                  + [pl.BlockSpec((3,) + tile(q), lambda i, pr: (0, i, 0)) for q in slots],
                  out_specs=[pl.BlockSpec((None,) + tile(q), lambda i, pr: (pr[1], i, 0)) for q in slots])(
                      place, *pairs, *slots)


def _adamw_math(w, g, m, v):
    m = ADAM_B1 * m + (1.0 - ADAM_B1) * g
    v = ADAM_B2 * v + (1.0 - ADAM_B2) * jnp.square(g)
    m_hat = m / (1.0 - ADAM_B1 ** ADAM_STEP)
    v_hat = v / (1.0 - ADAM_B2 ** ADAM_STEP)
    delta = -ADAM_LR * (m_hat / (jnp.sqrt(v_hat) + ADAM_EPS) + ADAM_WD * w)
    return delta, m, v


def _adamw_layer(layer, items):
    n = len(items)

    def body(*refs):
        outs = refs[-4 * n:]
        for t in range(n):
            w_ref, g_ref, m_ref, v_ref = refs[4 * t:4 * t + 4]
            g = g_ref[...].astype(f32)
            outs[4 * t][...] = g
            outs[4 * t + 1][...], outs[4 * t + 2][...], outs[4 * t + 3][...] = _adamw_math(
                w_ref[...], g, m_ref[...], v_ref[...])

    args, in_specs, out_specs, out_shape = [], [], [], []
    for w, g, m, v, _ in items:
        tr, cols = w.shape[1] // ADAMW_STEPS, w.shape[2]
        spec = pl.BlockSpec((None, tr, cols), lambda i: (layer, i, 0))
        args += [w, g, m, v]
        in_specs += [spec, pl.BlockSpec((tr, cols), lambda i: (i, 0)), spec, spec]
        out_specs += [spec] * 4
        out_shape += [_sds(w.shape, f32)] * 4
    aliases = {}
    for t, it in enumerate(items):
        if it[4] is not None:
            for k in range(4):
                aliases[len(args)] = 4 * t + k
                args.append(it[4][k])
                in_specs.append(ANY)
    res = _pcall(body, name="adamw", out_shape=out_shape, grid=(ADAMW_STEPS,), in_specs=in_specs, out_specs=out_specs,
                 aliases=aliases)(*args)
    return [tuple(res[4 * t:4 * t + 4]) for t in range(n)]


def _adamw_small(items):
    n = len(items)

    def body(*refs):
        ins, outs = refs[:4 * n], refs[4 * n:]
        for t in range(n):
            w_ref, g_ref, m_ref, v_ref = ins[4 * t:4 * t + 4]
            if len(g_ref.shape) == len(w_ref.shape) + 1:
                g = g_ref[0]
                for b in range(1, g_ref.shape[0]):
                    g = g + g_ref[b]
            else:
                g = g_ref[...]
            d, m, v = _adamw_math(w_ref[...], g, m_ref[...], v_ref[...])
            outs[4 * t][...], outs[4 * t + 1][...], outs[4 * t + 2][...], outs[4 * t + 3][...] = g, d, m, v

    out_shape = [_sds(w.shape, f32) for (w, _, _, _) in items for _ in range(4)]
    flat = [a for it in items for a in it]
    res = _pcall(body, name="adamw_small", out_shape=out_shape, in_specs=[VMEM] * (4 * n),
                 out_specs=[VMEM] * (4 * n))(*flat)
    return [tuple(res[4 * t:4 * t + 4]) for t in range(n)]


NN = ((1,), (0,))
NT = ((1,), (1,))
TN = ((0,), (0,))


def _mm(name, a, b, *, grid, a_spec, b_spec, out_shape, out_spec, dims, vmem_mb=None):
    def body(a_ref, b_ref, o_ref):
        r = lax.dot_general(a_ref[...], b_ref[...], (dims, ((), ())), preferred_element_type=f32)
        o_ref[...] = r.astype(o_ref.dtype)

    return _pcall(body, name=name, out_shape=out_shape, grid=grid, in_specs=[a_spec, b_spec], out_specs=out_spec,
                  vmem_mb=vmem_mb)(a, b)


def _whole(shape):
    return pl.BlockSpec(shape, lambda j: (0,) * len(shape))


def _split_spec(rows, tile, per_split):
    return pl.BlockSpec((None, rows, tile), lambda j: (j // per_split, 0, j % per_split))


class _Proj:
    def __init__(self, n, splits, tile):
        self.n, self.splits, self.tile = n, splits, tile
        self.steps = n // tile
        self.w_per = n // 4 // tile
        self.a_per = n // splits // tile
        assert self.w_per * tile * 4 == n and self.a_per * tile * splits == n

    def fwd(self, hb, wg):
        T = hb.shape[0]
        sub, tile, w_per = FWD_TILES, self.tile, self.w_per
        wide = sub * tile
        a_per = self.n // self.splits // wide
        assert a_per * wide * self.splits == self.n

        def w_tile(q):
            return pl.BlockSpec((None, D, tile), lambda j: ((sub * j + q) // w_per, 0, (sub * j + q) % w_per))

        def body(a_ref, *rest):
            w = jnp.concatenate([rest[q][...] for q in range(sub)], axis=1)
            rest[sub][...] = jnp.dot(a_ref[...], w, preferred_element_type=f32).astype(bf16)

        return _pcall(body, name="proj_fwd", out_shape=_sds((self.splits, T, self.n // self.splits), bf16),
                      grid=(self.n // wide,), in_specs=[_whole((T, D))] + [w_tile(q) for q in range(sub)],
                      out_specs=pl.BlockSpec((None, T, wide), lambda j: (j // a_per, 0, j % a_per)),
                      vmem_mb=40 if wide > 512 else None)(hb, *([wg] * sub))

    def dw(self, hb, dp):
        T = hb.shape[0]
        return _mm("proj_dw", hb, dp, grid=(self.steps,), a_spec=_whole((T, D)),
                   b_spec=_split_spec(T, self.tile, self.a_per), out_shape=_sds((4, D, self.n // 4), bf16),
                   out_spec=_split_spec(D, self.tile, self.w_per), dims=TN)

    def dh(self, dp, wg):
        T = dp.shape[1]
        sub, tile, w_per = DH_WIDE // self.tile, self.tile, self.w_per
        a_per = self.n // self.splits // DH_WIDE
        assert sub * tile == DH_WIDE and a_per * DH_WIDE * self.splits == self.n

        def w_tile(q):
            return pl.BlockSpec((None, D, tile), lambda k: ((sub * k + q) // w_per, 0, (sub * k + q) % w_per))

        def body(a_ref, *rest):
            o_ref = rest[sub]
            w = jnp.concatenate([rest[q][...] for q in range(sub)], axis=1)
            r = lax.dot_general(a_ref[...], w, (NT, ((), ())), preferred_element_type=f32)

            @pl.when(pl.program_id(0) == 0)
            def _():
                o_ref[...] = r

            @pl.when(pl.program_id(0) > 0)
            def _():
                o_ref[...] += r

        return _pcall(body, name="proj_dh", out_shape=_sds((T, D), f32), grid=(self.n // DH_WIDE,),
                      in_specs=[pl.BlockSpec((None, T, DH_WIDE), lambda k: (k // a_per, 0, k % a_per))]
                      + [w_tile(q) for q in range(sub)],
                      out_specs=_whole((T, D)), vmem_mb=40)(dp, *([wg] * sub))


EVEN_PROJ = _Proj(7 * D, 7, 256)
ODD_PROJ = _Proj(4 * D, 2, 512)


def _dy_mm(dob, wo):
    T = dob.shape[0]
    return _mm("out_dy", dob, wo, grid=(4,), a_spec=_whole((T, D)),
               b_spec=pl.BlockSpec((None, 512, D), lambda j: (j, 0, 0)), out_shape=_sds((2, T, D), f32),
               out_spec=_split_spec(T, 512, 2), dims=NT)


def _dwo_mm(y2, dob):
    T = dob.shape[0]
    return _mm("out_dw", y2, dob, grid=(4,), a_spec=_split_spec(T, 512, 2), b_spec=_whole((T, D)),
               out_shape=_sds((4, 512, D), bf16), out_spec=pl.BlockSpec((None, 512, D), lambda j: (j, 0, 0)), dims=TN)


def _head_spec(lead, T):
    return pl.BlockSpec((lead, T, HEAD), lambda h: (0, 0, h))


def _head_vec(rows):
    return pl.BlockSpec((rows, HEAD), lambda h: (0, h))


_HEAD_MAT = pl.BlockSpec((None, HEAD, HEAD), lambda h: (h, 0, 0))


def _causal():
    return lax.broadcasted_iota(jnp.int32, (HEAD, HEAD), 0) >= lax.broadcasted_iota(jnp.int32, (HEAD, HEAD), 1)


def _layernorm_head(v):
    mu = jnp.mean(v, axis=-1, keepdims=True)
    d = v - mu
    rstd = lax.rsqrt(jnp.mean(d * d, axis=-1, keepdims=True) + EPS)
    return d * rstd, rstd


def _even_fwd(p7, conv_w, ln_g, ln_b, sgu_w, sgu_bias):
    T, C = p7.shape[1], CHUNK_ROWS

    def body(p_ref, cw_ref, lg_ref, lb_ref, w_ref, b_ref, y_ref):
        w0, w1, w2 = cw_ref[0:1, :], cw_ref[1:2, :], cw_ref[2:3, :]
        wm = jnp.where(_causal(), w_ref[...], 0.0).astype(bf16)
        bias, lg, lb = b_ref[...], lg_ref[...], lb_ref[...]

        def step(i, halo):
            rows = pl.ds(pl.multiple_of(i * C, C), C)
            ah, ab, ac, az, u, v, zb = (p_ref[k, rows, :].astype(f32) for k in range(7))
            tt = ac * ah
            ext = jnp.concatenate([halo, tt], axis=0)
            cv = w2 * tt + w1 * pltpu.roll(ext, 1, 0)[HALO_CONV:] + w0 * pltpu.roll(ext, 2, 0)[HALO_CONV:]
            y_ref[0, rows, :] = (ab * cv * _silu(az)).astype(bf16)
            vhat, _ = _layernorm_head(v)
            vn = (vhat * lg + lb).astype(bf16)
            mix = jnp.concatenate([jnp.dot(wm, vn[k * HEAD:(k + 1) * HEAD], preferred_element_type=f32) + bias
                                   for k in range(C // HEAD)], axis=0)
            y_ref[1, rows, :] = (u * mix * _silu(zb)).astype(bf16)
            return tt[C - HALO_CONV:]

        lax.fori_loop(0, T // C, step, jnp.zeros((HALO_CONV, HEAD), f32))

    return _pcall(body, name="even_fwd", out_shape=_sds((2, T, D), bf16), grid=(NH,),
                  in_specs=[_head_spec(7, T), _head_vec(3), _head_vec(1), _head_vec(1), _HEAD_MAT, _HEAD_MAT],
                  out_specs=_head_spec(2, T))(p7, conv_w, ln_g, ln_b, sgu_w, sgu_bias)


def _even_bwd(p7, dy2, conv_w, ln_g, ln_b, sgu_w, sgu_bias):
    T, C = p7.shape[1], CHUNK_ROWS
    n_chunks = T // C

    def body(p_ref, dy_ref, cw_ref, lg_ref, lb_ref, w_ref, b_ref,
             dp_ref, dcw_ref, dlg_ref, dlb_ref, dw_ref, dms_ref, dcv_s):
        w0, w1, w2 = cw_ref[0:1, :], cw_ref[1:2, :], cw_ref[2:3, :]
        tri = _causal()
        wm = jnp.where(tri, w_ref[...], 0.0).astype(bf16)
        bias, lg, lb = b_ref[...], lg_ref[...], lb_ref[...]
        dw_ref[...] = jnp.zeros_like(dw_ref)
        dms_ref[...] = jnp.zeros_like(dms_ref)

        def fwd_step(i, carry):
            halo, a0, a1, a2, alg, alb = carry
            rows = pl.ds(pl.multiple_of(i * C, C), C)
            ah, ab, ac, az = (p_ref[k, rows, :].astype(f32) for k in range(4))
            dya = dy_ref[0, rows, :]
            tt = ac * ah
            ext = jnp.concatenate([halo, tt], axis=0)
            t1, t2 = pltpu.roll(ext, 1, 0)[HALO_CONV:], pltpu.roll(ext, 2, 0)[HALO_CONV:]
            cv = w2 * tt + w1 * t1 + w0 * t2
            sa, dsa = _silu_and_grad(az)
            g1 = dya * sa
            dp_ref[1, rows, :] = (g1 * cv).astype(bf16)
            dp_ref[3, rows, :] = (dya * ab * cv * dsa).astype(bf16)
            dcv = g1 * ab
            dcv_s[rows, :] = dcv
            a2 = a2 + jnp.sum(dcv * tt, axis=0, keepdims=True)
            a1 = a1 + jnp.sum(dcv * t1, axis=0, keepdims=True)
            a0 = a0 + jnp.sum(dcv * t2, axis=0, keepdims=True)

            u, zb, dyb = p_ref[4, rows, :].astype(f32), p_ref[6, rows, :].astype(f32), dy_ref[1, rows, :]
            vhat, rstd = _layernorm_head(p_ref[5, rows, :].astype(f32))
            vn = (vhat * lg + lb).astype(bf16)
            sb, dsb = _silu_and_grad(zb)
            mix = jnp.concatenate([jnp.dot(wm, vn[k * HEAD:(k + 1) * HEAD], preferred_element_type=f32) + bias
                                   for k in range(C // HEAD)], axis=0)
            dp_ref[4, rows, :] = (dyb * mix * sb).astype(bf16)
            dp_ref[6, rows, :] = (dyb * u * mix * dsb).astype(bf16)
            dmix = dyb * u * sb
            dvn_parts = []
            for k in range(C // HEAD):
                dm = dmix[k * HEAD:(k + 1) * HEAD]
                dmb = dm.astype(bf16)
                dvn_parts.append(lax.dot_general(wm, dmb, (TN, ((), ())), preferred_element_type=f32))
                dw_ref[...] += lax.dot_general(dmb, vn[k * HEAD:(k + 1) * HEAD], (NT, ((), ())),
                                               preferred_element_type=f32)
                dms_ref[...] += dm
            dvn = jnp.concatenate(dvn_parts, axis=0)
            alg = alg + jnp.sum(dvn * vhat, axis=0, keepdims=True)
            alb = alb + jnp.sum(dvn, axis=0, keepdims=True)
            dvh = dvn * lg
            dv = rstd * (dvh - jnp.mean(dvh, axis=-1, keepdims=True)
                         - vhat * jnp.mean(dvh * vhat, axis=-1, keepdims=True))
            dp_ref[5, rows, :] = dv.astype(bf16)
            return tt[C - HALO_CONV:], a0, a1, a2, alg, alb

        zrow = jnp.zeros((1, HEAD), f32)
        _, a0, a1, a2, alg, alb = lax.fori_loop(
            0, n_chunks, fwd_step, (jnp.zeros((HALO_CONV, HEAD), f32), zrow, zrow, zrow, zrow, zrow))
        dcw_ref[0:1, :], dcw_ref[1:2, :], dcw_ref[2:3, :] = a0, a1, a2
        dlg_ref[...], dlb_ref[...] = alg, alb
        dw_ref[...] = jnp.where(tri, dw_ref[...], 0.0)

        def bwd_step(k, halo):
            rows = pl.ds(pl.multiple_of((n_chunks - 1 - k) * C, C), C)
            dcv = dcv_s[rows, :]
            ext = jnp.concatenate([dcv, halo], axis=0)
            n1 = pltpu.roll(ext, C + HALO_CONV - 1, 0)[:C]
            n2 = pltpu.roll(ext, C + HALO_CONV - 2, 0)[:C]
            dtt = w2 * dcv + w1 * n1 + w0 * n2
            dp_ref[2, rows, :] = (dtt * p_ref[0, rows, :].astype(f32)).astype(bf16)
            dp_ref[0, rows, :] = (dtt * p_ref[2, rows, :].astype(f32)).astype(bf16)
            return dcv[:HALO_CONV]

        lax.fori_loop(0, n_chunks, bwd_step, jnp.zeros((HALO_CONV, HEAD), f32))

    out_shape = [_sds((7, T, D), bf16), _sds((3, D), f32), _sds((1, D), f32), _sds((1, D), f32),
                 _sds((NH, HEAD, HEAD), f32), _sds((NH, HEAD, HEAD), f32)]
    return _pcall(body, name="even_bwd", out_shape=out_shape, grid=(NH,),
                  in_specs=[_head_spec(7, T), _head_spec(2, T), _head_vec(3), _head_vec(1), _head_vec(1),
                            _HEAD_MAT, _HEAD_MAT],
                  out_specs=[_head_spec(7, T), _head_vec(3), _head_vec(1), _head_vec(1), _HEAD_MAT, _HEAD_MAT],
                  scratch=[pltpu.VMEM((T, HEAD), f32)])(p7, dy2, conv_w, ln_g, ln_b, sgu_w, sgu_bias)


def _window_sum(ext, win, towards_past):
    n, k, s = ext.shape[0], 1, ext
    while k < win:
        s = s + pltpu.roll(s, k if towards_past else n - k, 0)
        k *= 2
    return s


def _pool_count(i, C, win):
    t = i * C + lax.broadcasted_iota(jnp.int32, (C, 1), 0)
    cnt = jnp.minimum(t + 1, win).astype(f32)
    return cnt, 1.0 / cnt


def _group_specs(T):
    p_spec = pl.BlockSpec((None, T, GC), lambda g: (0, 0, g))
    z_spec = pl.BlockSpec((None, T, GC), lambda g: (1, 0, g))
    pw_spec = pl.BlockSpec((4, GC // 4, GC), lambda g: (0, g, 0))
    ps_spec = pl.BlockSpec((1, GC), lambda g: (0, g))
    y_spec = pl.BlockSpec((None, T, GC), lambda g: (g // 2, 0, g % 2))
    return p_spec, z_spec, pw_spec, ps_spec, y_spec


def _odd_fwd(p2, pool_wg, pool_scale):
    T, C = p2.shape[1], CHUNK_ROWS
    p_spec, z_spec, pw_spec, ps_spec, y_spec = _group_specs(T)

    def body(p_ref, z_ref, pw_ref, ps_ref, y_ref):
        pw, ps = pw_ref[...].reshape(GC, GC), ps_ref[...]

        def run(win):
            def step(i, halo):
                rows = pl.ds(pl.multiple_of(i * C, C), C)
                p = p_ref[rows, :].astype(f32)
                s = _window_sum(jnp.concatenate([halo, p], axis=0), win, True)[HALO_POOL:]
                pooled = s * _pool_count(i, C, win)[1] - p
                ypre = jnp.dot(pooled.astype(bf16), pw, preferred_element_type=f32)
                y_ref[rows, :] = (ypre * ps * _silu(z_ref[rows, :].astype(f32))).astype(bf16)
                return p[C - HALO_POOL:]

            lax.fori_loop(0, T // C, step, jnp.zeros((HALO_POOL, GC), f32))

        for gi, win in enumerate(WINDOWS):
            pl.when(pl.program_id(0) == gi)(functools.partial(run, win))

    return _pcall(body, name="odd_fwd", out_shape=_sds((2, T, D), bf16), grid=(len(WINDOWS),),
                  in_specs=[p_spec, z_spec, pw_spec, ps_spec], out_specs=y_spec)(p2, p2, pool_wg, pool_scale)


def _odd_bwd(p2, dy2, pool_wg, pool_scale):
    T, C = p2.shape[1], CHUNK_ROWS
    n_chunks = T // C
    p_spec, z_spec, pw_spec, ps_spec, y_spec = _group_specs(T)

    def body(p_ref, z_ref, dy_ref, pw_ref, ps_ref, dp_ref, dpw_ref, dps_ref, q_s, acc_s):
        pw, ps = pw_ref[...].reshape(GC, GC), ps_ref[...]

        def run(win):
            acc_s[...] = jnp.zeros_like(acc_s)

            def fwd_step(i, carry):
                halo, aps = carry
                rows = pl.ds(pl.multiple_of(i * C, C), C)
                p, z, dy = p_ref[rows, :].astype(f32), z_ref[rows, :].astype(f32), dy_ref[rows, :]
                _, inv_cnt = _pool_count(i, C, win)
                s = _window_sum(jnp.concatenate([halo, p], axis=0), win, True)[HALO_POOL:]
                pb = (s * inv_cnt - p).astype(bf16)
                ypre = jnp.dot(pb, pw, preferred_element_type=f32)
                sz, dsz = _silu_and_grad(z)
                aps = aps + jnp.sum(dy * ypre * sz, axis=0, keepdims=True)
                dp_ref[1, rows, :] = (dy * ypre * ps * dsz).astype(bf16)
                dyp = (dy * ps * sz).astype(bf16)
                acc_s[...] += lax.dot_general(pb, dyp, (TN, ((), ())), preferred_element_type=f32)
                dpool = lax.dot_general(dyp, pw, (NT, ((), ())), preferred_element_type=f32)
                q_s[rows, :] = dpool * inv_cnt
                return p[C - HALO_POOL:], aps

            _, aps = lax.fori_loop(0, n_chunks, fwd_step, (jnp.zeros((HALO_POOL, GC), f32), jnp.zeros((1, GC), f32)))
            dps_ref[...] = aps
            dpw_ref[...] = acc_s[...].reshape(4, GC // 4, GC).astype(bf16)

            def bwd_step(k, halo):
                i = n_chunks - 1 - k
                rows = pl.ds(pl.multiple_of(i * C, C), C)
                q = q_s[rows, :]
                s = _window_sum(jnp.concatenate([q, halo], axis=0), win, False)[:C]
                dp_ref[0, rows, :] = (s - q * _pool_count(i, C, win)[0]).astype(bf16)
                return q[:HALO_POOL]

            lax.fori_loop(0, n_chunks, bwd_step, jnp.zeros((HALO_POOL, GC), f32))

        for gi, win in enumerate(WINDOWS):
            pl.when(pl.program_id(0) == gi)(functools.partial(run, win))

    out_shape = [_sds((2, T, 2 * D), bf16), _sds((4, GC, GC), bf16), _sds((1, 2 * D), f32)]
    return _pcall(body, name="odd_bwd", out_shape=out_shape, grid=(len(WINDOWS),),
                  in_specs=[p_spec, z_spec, y_spec, pw_spec, ps_spec],
                  out_specs=[pl.BlockSpec((2, T, GC), lambda g: (0, 0, g)), pw_spec, ps_spec],
                  scratch=[pltpu.VMEM((T, GC), f32), pltpu.VMEM((GC, GC), f32)], vmem_mb=44)(
                      p2, p2, dy2, pool_wg, pool_scale)


def _ada_fwd(c_all, ada_w):
    cols = ada_w.shape[2]

    def body(c_ref, w_ref, o_ref):
        o_ref[...] = jnp.dot(_silu(c_ref[...]), w_ref[...], preferred_element_type=f32,
                             precision=lax.Precision.HIGHEST)

    return _pcall(body, name="ada_fwd", out_shape=_sds((4, N_DEV, cols), f32), grid=(4,),
                  in_specs=[pl.BlockSpec((N_DEV, D), lambda i: (0, 0)), pl.BlockSpec((None, D, cols), lambda i: (i, 0, 0))],
                  out_specs=pl.BlockSpec((None, N_DEV, cols), lambda i: (i, 0, 0)))(c_all, ada_w)


def _ada_bwd(c_all_t, dmod, w, m, v):
    cols, tr = w.shape[2], 256
    spec = pl.BlockSpec((None, tr, cols), lambda l, i: (l, i, 0))

    def body(c_ref, dm_ref, w_ref, m_ref, v_ref, g_ref, d_ref, mo_ref, vo_ref):
        sc = _silu(c_ref[...])
        g = sc[:, 0:1] * dm_ref[0:1, :]
        for b in range(1, N_DEV):
            g = g + sc[:, b:b + 1] * dm_ref[b:b + 1, :]
        g_ref[...] = g
        d_ref[...], mo_ref[...], vo_ref[...] = _adamw_math(w_ref[...], g, m_ref[...], v_ref[...])

    return _pcall(body, name="ada_bwd", out_shape=[_sds(w.shape, f32)] * 4, grid=(4, D // tr),
                  in_specs=[pl.BlockSpec((tr, N_DEV), lambda l, i: (i, 0)),
                            pl.BlockSpec((None, N_DEV, cols), lambda l, i: (l, 0, 0)), spec, spec, spec],
                  out_specs=[spec] * 4)(c_all_t, dmod, w, m, v)


def _layer_fwd(even, x, hb, gate, w, nxt, before_out=None):
    if even:
        w_in, w_out, conv_w, ln_g, ln_b, sgu_w, sgu_b = w
        bias = jnp.broadcast_to(sgu_b[:, :, None], (NH, HEAD, HEAD))
        p = EVEN_PROJ.fwd(hb, w_in)
        y2 = _even_fwd(p, conv_w, ln_g, ln_b, sgu_w, bias)
    else:
        w_in, pool_w, w_out, pool_scale = w
        p = ODD_PROJ.fwd(hb, w_in)
        y2 = _odd_fwd(p, pool_w, pool_scale)
    if before_out is not None:
        late_w_out, tok = before_out(y2)
        if late_w_out is not None:
            w_out = late_w_out
            w = (w_in, w_out) + tuple(w[2:]) if even else (w_in, pool_w, w_out, pool_scale)
        if tok is not None:
            gate = gate + tok[0:1, 0:1]
    outs = _out_proj(y2, w_out.reshape(2, D, D), x, gate, nxt)
    return outs[0], (None if nxt is None else outs[2]), (x, hb, p, y2, outs[1]), w


def _layer_bwd(even, gin, dob, dgate, saved, scale, g, w, below=None, send=None):
    x_in, hb, p, y2, o = saved
    if even:
        w_in, w_out, conv_w, ln_g, ln_b, sgu_w, sgu_b = w
        bias = jnp.broadcast_to(sgu_b[:, :, None], (NH, HEAD, HEAD))
        dy2 = _dy_mm(dob, w_out)
        dp, dconv, dlg, dlb, dsw, dms = _even_bwd(p, dy2, conv_w, ln_g, ln_b, sgu_w, bias)
        proj = EVEN_PROJ
        small = dict(conv_w=dconv, ln_g=dlg, ln_b=dlb, sgu_w=dsw, sgu_b=jnp.sum(dms, axis=-1))
        big = [proj.dw(hb, dp), _dwo_mm(y2, dob)]
    else:
        w_in, pool_w, w_out, pool_scale = w
        dy2 = _dy_mm(dob, w_out)
        dp, dpw, dps = _odd_bwd(p, dy2, pool_w, pool_scale)
        proj = ODD_PROJ
        small = dict(pool_scale=dps)
        big = [proj.dw(hb, dp), dpw, _dwo_mm(y2, dob)]
    if send is not None:
        big, tok = send(big)
        scale = scale + tok[0:1, 0:1]
    dh = proj.dh(dp, w_in)
    res = _norm_bwd(x_in, dh, gin, g, scale, below)
    stats = res[1]
    return (res[0], (None if below is None else (res[2], res[3])), big, small,
            jnp.concatenate([stats[0:2], dgate], axis=0), stats[2:3])


def _pack_rows(parts):
    rows = [p.reshape(-1, LANES) for p in parts]
    total = sum(r.shape[0] for r in rows)
    padded = -(-total // (8 * N_DEV)) * (8 * N_DEV)
    if padded > total:
        rows.append(jnp.zeros((padded - total, LANES), f32))
    return jnp.concatenate(rows, axis=0)


def _unpack_rows(buf, shapes):
    out, r = [], 0
    for shp in shapes:
        n = 1
        for d in shp:
            n *= d
        out.append(buf[r:r + n // LANES].reshape(shp))
        r += n // LANES
    return out


def kernel(x, c, norm_g, ada_w, ada_b, ab_w_in, ab_conv_w, ab_ln_g, ab_ln_b, ab_sgu_w, ab_sgu_b, ab_w_out, c_w_in, c_pool_w, c_pool_scale, c_w_out, final_g, loss_target, m_norm_g, m_ada_w, m_ada_b, m_ab_w_in, m_ab_conv_w, m_ab_ln_g, m_ab_ln_b, m_ab_sgu_w, m_ab_sgu_b, m_ab_w_out, m_c_w_in, m_c_pool_w, m_c_pool_scale, m_c_w_out, m_final_g, v_norm_g, v_ada_w, v_ada_b, v_ab_w_in, v_ab_conv_w, v_ab_ln_g, v_ab_ln_b, v_ab_sgu_w, v_ab_sgu_b, v_ab_w_out, v_c_w_in, v_c_pool_w, v_c_pool_scale, v_c_w_out, v_final_g):
    ix, iy, ic = _place()
    chip, dev = 2 * ix + iy, 4 * ix + 2 * iy + ic
    n_even, n_odd = ab_w_in.shape[0], c_w_in.shape[0]
    depth = n_even + n_odd
    acols = ada_w.shape[2]

    place = jnp.stack([chip, ic]).astype(jnp.int32)
    even_names, odd_names = ["ab_w_in", "ab_w_out"], ["c_w_in", "c_pool_w", "c_w_out"]
    params = {"ab_w_in": (ab_w_in, m_ab_w_in, v_ab_w_in), "ab_w_out": (ab_w_out, m_ab_w_out, v_ab_w_out),
              "c_w_in": (c_w_in, m_c_w_in, v_c_w_in), "c_w_out": (c_w_out, m_c_w_out, v_c_w_out),
              "c_pool_w": tuple(a.reshape(n_odd, GC, GC) for a in (c_pool_w, m_c_pool_w, v_c_pool_w))}

    def placed(names, layer, after=None):
        ws = [params[nm][0] for nm in names]
        return [p.reshape(4, 2, p.shape[1] // 2, p.shape[2]) for p in _cast_place(place, ws, layer, after)]

    def whole(arrays):
        return [g.reshape(4, 2 * g.shape[2], g.shape[3]) for g in arrays]

    first = _gather8(jnp.concatenate([c, ab_conv_w.reshape(1, -1), c_pool_scale.reshape(1, -1)], axis=1), "gather_c")
    c_all, small_all = first[:, 0, :D], first[0::2, 0, D:]
    sems_a, in_a, tok = _ag_start([placed(even_names[:1], 0)], first[0:1, 0, 0:LANES], "ag_start_0a")
    modp = _ada_fwd(c_all, ada_w)
    modg = _gather8(modp + tok[0:1, 0:1], "gather_mod")
    mod_rows = lax.dynamic_index_in_dim(modg[0::2], dev, axis=2, keepdims=False)
    mod = jnp.transpose(mod_rows, (1, 0, 2)).reshape(depth, 3 * D) + ada_b
    mods = [(mod[i:i + 1, 0:D], mod[i:i + 1, D:2 * D], mod[i:i + 1, 2 * D:3 * D]) for i in range(depth)]

    def shard_cols(a, width):
        return lax.dynamic_slice_in_dim(a, chip * width, width, axis=a.ndim - 1)

    n_conv = ab_conv_w.size
    conv_all = small_all[:, :n_conv].reshape(4, n_even, 3, D // 4)
    conv_full = jnp.transpose(conv_all, (1, 2, 0, 3)).reshape(n_even, 3, D)
    scale_all = small_all[:, n_conv:].reshape(4, n_odd, 2 * D // 4)
    scale_full = jnp.transpose(scale_all, (1, 0, 2)).reshape(n_odd, 2 * D)

    gathers_done = mod[0:1, 0:LANES] + scale_full[0:1, 0:LANES]
    sems_b, in_b, tok = _ag_start([placed(even_names[1:], 0, tok)], gathers_done, "ag_start_0b")
    rest = [placed(even_names if i % 2 == 0 else odd_names, i // 2, tok) for i in range(1, depth)]
    sems_r, in_r, tok = _ag_start(rest, tok, "ag_start_rest")

    x_cur, saved, weights, handoff = x[0], [], [], {}
    hb = _hnorm(x_cur, norm_g[0:1], mods[0][0] + tok[0:1, 0:1], mods[0][1])
    for i in range(depth):
        j = i // 2
        if i == 0:
            full = whole(_ag_forward(_ag_wait(in_a[0], sems_a[0], hb, "ag_wait_0a"), "ag_forward")) + [None]
        else:
            full = whole(_agf_wait(*handoff.pop(i), x_cur, f"agf_wait_{i}"))
        if i % 2 == 0:
            w = (full[0], full[1], conv_full[j], ab_ln_g[j:j + 1], ab_ln_b[j:j + 1], ab_sgu_w[j], ab_sgu_b[j])
        else:
            w = (full[0], full[1], full[2], scale_full[j:j + 1])

        def before_out(y2, i=i):
            w_out, tok = None, None
            if i == 0:
                w_out = whole(_ag_forward(_ag_wait(in_b[0], sems_b[0], y2, "ag_wait_0b"), "ag_forward"))[0]
            if i + 1 < depth:
                arrived = _ag_wait(in_r[i], sems_r[i], y2, f"ag_wait_{i + 1}")
                sems_f, inflight, tok = _agf_start(arrived, f"agf_start_{i + 1}")
                handoff[i + 1] = (sems_f, inflight)
            return w_out, tok

        nxt = (norm_g[i + 1:i + 2], mods[i + 1][0], mods[i + 1][1]) if i + 1 < depth else None
        x_cur, hb, sv, w = _layer_fwd(i % 2 == 0, x_cur, hb, mods[i][2], w, nxt, before_out)
        weights.append(w)
        saved.append(sv)
    gin, loss, dfinal_g, dob, dgate = _loss_bwd(x_cur, loss_target[0], final_g.reshape(1, D), saved[-1][4],
                                                mods[-1][2])

    stacked = {}

    def finish(i, sems, pairs, lands, after):
        pairs, slots = _rs_chip_wait(sems, pairs, lands, after, f"rs_chip_wait_{i}")
        names = even_names if i % 2 == 0 else odd_names
        grads = _rs_half_exchange(_rs_sum(place, pairs, slots), "rs_half_exchange")
        items = [(params[nm][0], g.reshape(params[nm][0].shape[1:]), params[nm][1], params[nm][2], stacked.get(nm))
                 for nm, g in zip(names, grads)]
        for nm, res in zip(names, _adamw_layer(i // 2, items)):
            stacked[nm] = res

    small_g, dmod, dnorm_g, pending, tok = [None] * depth, [None] * depth, [None] * depth, None, None
    for i in reversed(range(depth)):
        w = weights[i]
        if tok is not None:
            w = w[:2] + (w[2] + tok[0:1, 0:1],) + w[3:] if i % 2 == 0 else w[:3] + (w[3] + tok[0:1, 0:1],)
        below = (saved[i - 1][4], mods[i - 1][2]) if i > 0 else None

        def send(big_g, i=i):
            big_g = [g.reshape(4, 2, g.shape[1] // 2, g.shape[2]) for g in big_g]
            sems, big_g, lands, tok = _rs_pair_start(big_g, f"rs_pair_start_{i}")
            return (sems, big_g, lands), tok

        gin, gate_bwd, sent, small_g[i], dmod[i], dnorm_g[i] = _layer_bwd(
            i % 2 == 0, gin, dob, dgate, saved[i], mods[i][1], norm_g[i:i + 1], w, below, send)
        if below is not None:
            dob, dgate = gate_bwd
        after = gin
        if i == 0:
            dmod_all = _gather8(jnp.stack(dmod).reshape(depth * 3 * D // LANES, LANES), "gather_dmod")
            after = dmod_all = dmod_all.reshape(N_DEV, depth, 3 * D)
        big_g, theirs = _rs_pair_wait(*sent, after, f"rs_pair_wait_{i}")
        pairs = _rs_add(place, big_g, theirs)
        sems, pairs, lands, tok = _rs_chip_start(pairs, f"rs_chip_start_{i}")
        if pending is not None:
            finish(*pending, tok)
        pending = (i, sems, pairs, lands)
    grad_x = gin
    dnorm_g = jnp.concatenate(dnorm_g, axis=0)

    dmod_cols = jnp.transpose(shard_cols(dmod_all, acols), (1, 0, 2))
    r_ada_w = _ada_bwd(c_all.T, dmod_cols, ada_w, m_ada_w, v_ada_w)
    finish(*pending, r_ada_w[1])
    r_ab_w_in, r_ab_w_out, r_c_w_in, r_c_w_out = (stacked[nm] for nm in ("ab_w_in", "ab_w_out", "c_w_in", "c_w_out"))
    r_c_pool_w = tuple(a.reshape(c_pool_w.shape) for a in stacked["c_pool_w"])

    small_parts = [dnorm_g, dfinal_g,
                   jnp.stack([small_g[2 * j]["conv_w"] for j in range(n_even)]),
                   jnp.concatenate([small_g[2 * j]["ln_g"] for j in range(n_even)], axis=0),
                   jnp.concatenate([small_g[2 * j]["ln_b"] for j in range(n_even)], axis=0),
                   jnp.stack([small_g[2 * j]["sgu_b"] for j in range(n_even)]),
                   jnp.concatenate([small_g[2 * j + 1]["pool_scale"] for j in range(n_odd)], axis=0),
                   jnp.pad(loss, ((0, 7), (0, LANES - 1)))]
    small_shapes = [p.shape for p in small_parts]
    sgu_parts = [small_g[2 * j]["sgu_w"].reshape(NH * HEAD, HEAD) for j in range(n_even)]
    reduced = _allreduce8([_pack_rows(small_parts)] + sgu_parts, "allreduce_small", r_ab_w_in[1])
    g_norm_g, g_final_g, g_conv_full, g_ln_g, g_ln_b, g_sgu_b, g_scale_full, loss_row = _unpack_rows(reduced[0],
                                                                                                     small_shapes)
    g_sgu_w = jnp.stack(reduced[1:])
    loss = loss_row[0, 0]
    g_conv = shard_cols(g_conv_full, D // 4)
    g_scale = shard_cols(g_scale_full, 2 * D // 4)

    def two_d(a):
        return a.reshape(-1, a.shape[-1])

    small = [(norm_g, g_norm_g, m_norm_g, v_norm_g),
             (ada_b, dmod_all, m_ada_b, v_ada_b),
             (two_d(ab_conv_w), two_d(g_conv), two_d(m_ab_conv_w), two_d(v_ab_conv_w)),
             (ab_ln_g, g_ln_g, m_ab_ln_g, v_ab_ln_g),
             (ab_ln_b, g_ln_b, m_ab_ln_b, v_ab_ln_b),
             (two_d(ab_sgu_w), two_d(g_sgu_w), two_d(m_ab_sgu_w), two_d(v_ab_sgu_w)),
             (two_d(ab_sgu_b), two_d(g_sgu_b), two_d(m_ab_sgu_b), two_d(v_ab_sgu_b)),
             (c_pool_scale, g_scale, m_c_pool_scale, v_c_pool_scale),
             (final_g.reshape(1, D), g_final_g, m_final_g.reshape(1, D), v_final_g.reshape(1, D))]
    small_res = _adamw_small(small)
    small_shapes_out = [norm_g.shape, ada_b.shape, ab_conv_w.shape, ab_ln_g.shape, ab_ln_b.shape, ab_sgu_w.shape,
                        ab_sgu_b.shape, c_pool_scale.shape, final_g.shape]
    (r_norm_g, r_ada_b, r_conv, r_ln_g, r_ln_b, r_sgu_w, r_sgu_b, r_scale, r_final_g) = [
        tuple(a.reshape(shp) for a in res) for res, shp in zip(small_res, small_shapes_out)]

    order = [r_norm_g, r_ada_w, r_ada_b, r_ab_w_in, r_conv, r_ln_g, r_ln_b, r_sgu_w, r_sgu_b, r_ab_w_out,
             r_c_w_in, r_c_pool_w, r_scale, r_c_w_out, r_final_g]
    outs = [loss, grad_x[None]]
    for field in range(4):
        outs += [r[field] for r in order]
    return tuple(outs)
```

```python
import functools

import jax
import jax.numpy as jnp
from jax import lax
from jax.experimental import pallas as pl
from jax.experimental.pallas import tpu as pltpu

f32, bf16 = jnp.float32, jnp.bfloat16

D = 1024
HEAD = 128
NH = 8
WINDOWS = (2, 4, 8, 16)
GC = 512
EPS = 1e-6
HALO_CONV = 8
HALO_POOL = 16
CHUNK_ROWS = 512
DH_WIDE = 1024
FWD_TILES = 2
N_DEV = 8
LANES = 128

ADAM_LR, ADAM_B1, ADAM_B2, ADAM_EPS, ADAM_WD, ADAM_STEP = 0.001, 0.9, 0.999, 1e-08, 0.01, 10

MESH = pl.DeviceIdType.MESH
ANY = pl.BlockSpec(memory_space=pl.ANY)
VMEM = pl.BlockSpec(memory_space=pltpu.VMEM)
MIB = 2 ** 20


def _pcall(body, *, name, out_shape, grid=None, in_specs=None, out_specs=None, scratch=(), vmem_mb=None,
           aliases=None, prefetch=0):
    kw = {}
    if prefetch:
        kw["grid_spec"] = pltpu.PrefetchScalarGridSpec(num_scalar_prefetch=prefetch, grid=grid, in_specs=in_specs,
                                                       out_specs=out_specs, scratch_shapes=list(scratch))
    else:
        if grid is not None:
            kw["grid"] = grid
        if in_specs is not None:
            kw["in_specs"] = in_specs
        if out_specs is not None:
            kw["out_specs"] = out_specs
        if scratch:
            kw["scratch_shapes"] = list(scratch)
    if aliases:
        kw["input_output_aliases"] = aliases
    params = pltpu.CompilerParams(vmem_limit_bytes=None if vmem_mb is None else vmem_mb * MIB)
    return pl.pallas_call(body, name=name, out_shape=out_shape, compiler_params=params, **kw)


def _sds(shape, dtype):
    return jax.ShapeDtypeStruct(tuple(shape), dtype)


def _sigmoid(z):
    return pl.reciprocal(1.0 + jnp.exp(-z), approx=True)


def _silu(z):
    return z * _sigmoid(z)


def _silu_and_grad(z):
    s = _sigmoid(z)
    return z * s, s * (1.0 + z * (1.0 - s))


def _place():
    return lax.axis_index("x"), lax.axis_index("y"), lax.axis_index("c")


def _gather8(blk, name, after=()):
    def body(x_ref, *rest):
        o_ref, ssem, rsem = rest[len(after):]
        x, y, c = _place()
        me = 4 * x + 2 * y + c
        o_ref[me] = x_ref[...]
        sends = []
        for k in range(1, N_DEV):
            px = 1 - x if k & 4 else x
            py = 1 - y if k & 2 else y
            pc = 1 - c if k & 1 else c
            cp = pltpu.make_async_remote_copy(src_ref=x_ref, dst_ref=o_ref.at[me], send_sem=ssem.at[k - 1],
                                              recv_sem=rsem.at[k - 1], device_id=(px, py, pc), device_id_type=MESH)
            cp.start()
            sends.append((cp, 4 * px + 2 * py + pc))
        for k, (cp, peer) in enumerate(sends):
            pltpu.make_async_remote_copy(src_ref=x_ref, dst_ref=o_ref.at[peer], send_sem=ssem.at[k],
                                         recv_sem=rsem.at[k], device_id=(x, y, c), device_id_type=MESH).wait_recv()
        for cp, _ in sends:
            cp.wait_send()

    return _pcall(body, name=name, out_shape=_sds((N_DEV,) + blk.shape, blk.dtype), in_specs=[VMEM] + [ANY] * len(after),
                  out_specs=VMEM,
                  scratch=[pltpu.SemaphoreType.DMA((N_DEV - 1,)), pltpu.SemaphoreType.DMA((N_DEV - 1,))])(blk, *after)


def _allreduce8(bufs, name, after=None):
    n, n_after = len(bufs), 0 if after is None else 1
    rbs = [b.shape[0] // N_DEV for b in bufs]
    assert all(rb * N_DEV == b.shape[0] and rb % 8 == 0 for rb, b in zip(rbs, bufs))

    def body(*refs):
        refs = refs[:n] + refs[n + n_after:]
        xs, outs, stages = refs[:n], refs[n:2 * n], refs[2 * n:3 * n]
        ssem, rsem = refs[3 * n:]
        x, y, c = _place()
        me = 4 * x + 2 * y + c
        peers = []
        for k in range(1, N_DEV):
            px = 1 - x if k & 4 else x
            py = 1 - y if k & 2 else y
            pc = 1 - c if k & 1 else c
            peers.append(((px, py, pc), 4 * px + 2 * py + pc))

        def blk(t, ref, idx):
            return ref.at[pl.ds(pl.multiple_of(idx * rbs[t], 8), rbs[t]), :]

        def copy(t, phase, k, src, dst, dev):
            return pltpu.make_async_remote_copy(src_ref=src, dst_ref=dst, send_sem=ssem.at[t, phase, k],
                                                recv_sem=rsem.at[t, phase, k], device_id=dev, device_id_type=MESH)

        scatter = [copy(t, 0, k, blk(t, xs[t], pidx), stages[t].at[me], dev)
                   for t in range(n) for k, (dev, pidx) in enumerate(peers)]
        for cp in scatter:
            cp.start()
        gather = []
        for t in range(n):
            stages[t][me] = blk(t, xs[t], me)[...]
            for k, (dev, pidx) in enumerate(peers):
                copy(t, 0, k, blk(t, xs[t], pidx), stages[t].at[pidx], dev).wait_recv()
            total = stages[t][0]
            for j in range(1, N_DEV):
                total = total + stages[t][j]
            blk(t, outs[t], me)[...] = total
            sends = [copy(t, 1, k, blk(t, outs[t], me), blk(t, outs[t], me), dev) for k, (dev, pidx) in enumerate(peers)]
            for cp in sends:
                cp.start()
            gather += sends
        for t in range(n):
            for k, (dev, pidx) in enumerate(peers):
                copy(t, 1, k, blk(t, outs[t], pidx), blk(t, outs[t], pidx), dev).wait_recv()
        for cp in scatter + gather:
            cp.wait_send()

    return _pcall(body, name=name, out_shape=[_sds(b.shape, f32) for b in bufs], in_specs=[VMEM] * n + [ANY] * n_after,
                  out_specs=[VMEM] * n,
                  scratch=[pltpu.VMEM((N_DEV, rb, LANES), f32) for rb in rbs]
                  + [pltpu.SemaphoreType.DMA((n, 2, N_DEV - 1)), pltpu.SemaphoreType.DMA((n, 2, N_DEV - 1))])(
                      *bufs, *([] if after is None else [after]))


def _other_chips(x, y):
    return [((1 - x, y), 2 * (1 - x) + y), ((x, 1 - y), 2 * x + (1 - y)), ((1 - x, 1 - y), 2 * (1 - x) + (1 - y))]


HBM = pl.BlockSpec(memory_space=pltpu.HBM)
SEM = pl.BlockSpec(memory_space=pltpu.SEMAPHORE)
EFFECT = pltpu.SideEffectType.DATAFLOW_SIDE_EFFECTING


def _in_hbm(a):
    return pltpu.with_memory_space_constraint(a, pltpu.HBM)


def _ag_start(layers, after, name):
    flat = [t for lay in layers for t in lay]
    n, nl = len(flat), len(layers)

    def body(*refs):
        src = refs[:n]
        sems = refs[n + 1:n + 1 + 2 * nl]
        token = refs[-1]
        x, y, c = _place()
        s_me = 2 * x + y
        t = 0
        for i, lay in enumerate(layers):
            for k in range(len(lay)):
                for j, ((px, py), _) in enumerate(_other_chips(x, y)):
                    pltpu.make_async_remote_copy(src_ref=src[t].at[s_me, c], dst_ref=src[t].at[s_me, c],
                                                 send_sem=sems[2 * i].at[3 * k + j], recv_sem=sems[2 * i + 1].at[3 * k + j],
                                                 device_id=(px, py, c), device_id_type=MESH).start()
                t += 1
        token[...] = jnp.zeros_like(token)

    sem_shapes = [pltpu.SemaphoreType.DMA((3 * len(lay),)) for lay in layers for _ in range(2)]
    out_shape = sem_shapes + [pltpu.HBM(t.shape, t.dtype) for t in flat] + [_sds((8, LANES), f32)]
    outs = pl.pallas_call(
        body, name=name, out_shape=out_shape, in_specs=[HBM] * n + [ANY],
        out_specs=[SEM] * (2 * nl) + [HBM] * n + [VMEM], input_output_aliases={t: 2 * nl + t for t in range(n)},
        compiler_params=pltpu.CompilerParams(has_side_effects=EFFECT))(*[_in_hbm(t) for t in flat], after)
    sems = [(outs[2 * i], outs[2 * i + 1]) for i in range(nl)]
    thru, t = [], 2 * nl
    for lay in layers:
        thru.append(list(outs[t:t + len(lay)]))
        t += len(lay)
    return sems, thru, outs[-1]


def _ag_wait(inflight, sems, after, name):
    n = len(inflight)

    def body(*refs):
        src, ssem, rsem = refs[:n], refs[n], refs[n + 1]
        x, y, c = _place()
        s_me = 2 * x + y
        for k in range(n):
            for j, (_, s_p) in enumerate(_other_chips(x, y)):
                cp = pltpu.make_async_remote_copy(src_ref=src[k].at[s_me, c], dst_ref=src[k].at[s_p, c],
                                                  send_sem=ssem.at[3 * k + j], recv_sem=rsem.at[3 * k + j],
                                                  device_id=(x, y, c), device_id_type=MESH)
                cp.wait_send()
                cp.wait_recv()

    return pl.pallas_call(
        body, name=name, out_shape=[pltpu.HBM(t.shape, t.dtype) for t in inflight],
        in_specs=[HBM] * n + [SEM, SEM, ANY], out_specs=[HBM] * n, input_output_aliases={t: t for t in range(n)},
        compiler_params=pltpu.CompilerParams(has_side_effects=EFFECT))(*inflight, sems[0], sems[1], after)


def _ag_forward(arrived, name):
    n = len(arrived)

    def body(*refs):
        o = refs[n:2 * n]
        ssem, rsem = refs[2 * n:]
        x, y, c = _place()

        def copy(t, j, s, half, dev):
            return pltpu.make_async_remote_copy(src_ref=o[t].at[s, c], dst_ref=o[t].at[s, half], send_sem=ssem.at[t, j],
                                                recv_sem=rsem.at[t, j], device_id=dev, device_id_type=MESH)

        chips = _other_chips(x, y)
        sends = [copy(t, j, s_p, c, (x, y, 1 - c)) for t in range(n) for j, (_, s_p) in enumerate(chips)]
        for cp in sends:
            cp.start()
        for t in range(n):
            for j, (_, s_p) in enumerate(chips):
                copy(t, j, s_p, 1 - c, (x, y, c)).wait_recv()
        for cp in sends:
            cp.wait_send()

    return _pcall(body, name=name, out_shape=[_sds(p.shape, bf16) for p in arrived], in_specs=[ANY] * n,
                  out_specs=[ANY] * n, aliases={t: t for t in range(n)},
                  scratch=[pltpu.SemaphoreType.DMA((n, 3)), pltpu.SemaphoreType.DMA((n, 3))])(*arrived)


def _agf_start(arrived, name):
    n = len(arrived)

    def body(*refs):
        o = refs[:n]
        ssem, rsem, token = refs[n], refs[n + 1], refs[-1]
        x, y, c = _place()
        for t in range(n):
            for j, (_, s_p) in enumerate(_other_chips(x, y)):
                pltpu.make_async_remote_copy(src_ref=o[t].at[s_p, c], dst_ref=o[t].at[s_p, c],
                                             send_sem=ssem.at[3 * t + j], recv_sem=rsem.at[3 * t + j],
                                             device_id=(x, y, 1 - c), device_id_type=MESH).start()
        token[...] = jnp.zeros_like(token)

    out_shape = ([pltpu.SemaphoreType.DMA((3 * n,))] * 2 + [pltpu.HBM(a.shape, bf16) for a in arrived]
                 + [_sds((8, LANES), f32)])
    outs = pl.pallas_call(
        body, name=name, out_shape=out_shape, in_specs=[HBM] * n, out_specs=[SEM, SEM] + [HBM] * n + [VMEM],
        input_output_aliases={t: 2 + t for t in range(n)},
        compiler_params=pltpu.CompilerParams(has_side_effects=EFFECT))(*[_in_hbm(a) for a in arrived])
    return (outs[0], outs[1]), list(outs[2:2 + n]), outs[-1]


def _agf_wait(sems, inflight, after, name):
    n = len(inflight)

    def body(*refs):
        o, ssem, rsem = refs[:n], refs[n], refs[n + 1]
        x, y, c = _place()
        for t in range(n):
            for j, (_, s_p) in enumerate(_other_chips(x, y)):
                cp = pltpu.make_async_remote_copy(src_ref=o[t].at[s_p, c], dst_ref=o[t].at[s_p, 1 - c],
                                                  send_sem=ssem.at[3 * t + j], recv_sem=rsem.at[3 * t + j],
                                                  device_id=(x, y, c), device_id_type=MESH)
                cp.wait_send()
                cp.wait_recv()

    return pl.pallas_call(
        body, name=name, out_shape=[pltpu.HBM(a.shape, bf16) for a in inflight],
        in_specs=[HBM] * n + [SEM, SEM, ANY], out_specs=[HBM] * n, input_output_aliases={t: t for t in range(n)},
        compiler_params=pltpu.CompilerParams(has_side_effects=EFFECT))(*inflight, sems[0], sems[1], after)


def _rs_pair_start(grads, name):
    n = len(grads)

    def body(*refs):
        g, theirs = refs[:n], refs[n:2 * n]
        ssem, rsem, token = refs[2 * n], refs[2 * n + 1], refs[-1]
        x, y, c = _place()
        for t in range(n):
            pltpu.make_async_remote_copy(src_ref=g[t].at[:, 1 - c], dst_ref=theirs[t], send_sem=ssem.at[t],
                                         recv_sem=rsem.at[t], device_id=(x, y, 1 - c), device_id_type=MESH).start()
        token[...] = jnp.zeros_like(token)

    lands = [lax.empty((4,) + g.shape[2:], bf16) for g in grads]
    out_shape = ([pltpu.SemaphoreType.DMA((n,))] * 2 + [pltpu.HBM(g.shape, bf16) for g in grads]
                 + [pltpu.HBM(q.shape, bf16) for q in lands] + [_sds((8, LANES), f32)])
    outs = pl.pallas_call(
        body, name=name, out_shape=out_shape, in_specs=[HBM] * (2 * n), out_specs=[SEM, SEM] + [HBM] * (2 * n) + [VMEM],
        input_output_aliases={t: 2 + t for t in range(2 * n)},
        compiler_params=pltpu.CompilerParams(has_side_effects=EFFECT))(*[_in_hbm(a) for a in list(grads) + lands])
    return (outs[0], outs[1]), list(outs[2:2 + n]), list(outs[2 + n:2 + 2 * n]), outs[-1]


def _rs_pair_wait(sems, grads, lands, after, name):
    n = len(grads)

    def body(*refs):
        g, theirs = refs[:n], refs[n:2 * n]
        ssem, rsem = refs[2 * n], refs[2 * n + 1]
        x, y, c = _place()
        for t in range(n):
            cp = pltpu.make_async_remote_copy(src_ref=g[t].at[:, 1 - c], dst_ref=theirs[t], send_sem=ssem.at[t],
                                              recv_sem=rsem.at[t], device_id=(x, y, c), device_id_type=MESH)
            cp.wait_send()
            cp.wait_recv()

    outs = pl.pallas_call(
        body, name=name, out_shape=[pltpu.HBM(a.shape, bf16) for a in list(grads) + list(lands)],
        in_specs=[HBM] * (2 * n) + [SEM, SEM, ANY], out_specs=[HBM] * (2 * n),
        input_output_aliases={t: t for t in range(2 * n)},
        compiler_params=pltpu.CompilerParams(has_side_effects=EFFECT))(*grads, *lands, sems[0], sems[1], after)
    return list(outs[:n]), list(outs[n:])


def _rs_chip_start(pairs, name):
    n = len(pairs)

    def body(*refs):
        p, q = refs[:n], refs[n:2 * n]
        ssem, rsem, token = refs[2 * n], refs[2 * n + 1], refs[-1]
        x, y, c = _place()
        for t in range(n):
            for j, ((px, py), s_p) in enumerate(_other_chips(x, y)):
                pltpu.make_async_remote_copy(src_ref=p[t].at[s_p], dst_ref=q[t].at[j], send_sem=ssem.at[3 * t + j],
                                             recv_sem=rsem.at[3 * t + j], device_id=(px, py, c), device_id_type=MESH).start()
        token[...] = jnp.zeros_like(token)

    lands = [lax.empty((3,) + p.shape[1:], bf16) for p in pairs]
    out_shape = ([pltpu.SemaphoreType.DMA((3 * n,))] * 2 + [pltpu.HBM(p.shape, bf16) for p in pairs]
                 + [pltpu.HBM(q.shape, bf16) for q in lands] + [_sds((8, LANES), f32)])
    outs = pl.pallas_call(
        body, name=name, out_shape=out_shape, in_specs=[HBM] * (2 * n), out_specs=[SEM, SEM] + [HBM] * (2 * n) + [VMEM],
        input_output_aliases={t: 2 + t for t in range(2 * n)},
        compiler_params=pltpu.CompilerParams(has_side_effects=EFFECT))(*[_in_hbm(a) for a in list(pairs) + lands])
    return (outs[0], outs[1]), list(outs[2:2 + n]), list(outs[2 + n:2 + 2 * n]), outs[-1]


def _rs_chip_wait(sems, pairs, lands, after, name):
    n = len(pairs)

    def body(*refs):
        p, q = refs[:n], refs[n:2 * n]
        ssem, rsem = refs[2 * n], refs[2 * n + 1]
        x, y, c = _place()
        for t in range(n):
            for j, (_, s_p) in enumerate(_other_chips(x, y)):
                cp = pltpu.make_async_remote_copy(src_ref=p[t].at[s_p], dst_ref=q[t].at[j], send_sem=ssem.at[3 * t + j],
                                                  recv_sem=rsem.at[3 * t + j], device_id=(x, y, c), device_id_type=MESH)
                cp.wait_send()
                cp.wait_recv()

    outs = pl.pallas_call(
        body, name=name, out_shape=[pltpu.HBM(a.shape, bf16) for a in list(pairs) + list(lands)],
        in_specs=[HBM] * (2 * n) + [SEM, SEM, ANY], out_specs=[HBM] * (2 * n),
        input_output_aliases={t: t for t in range(2 * n)},
        compiler_params=pltpu.CompilerParams(has_side_effects=EFFECT))(*pairs, *lands, sems[0], sems[1], after)
    return list(outs[:n]), list(outs[n:])


def _rs_half_exchange(halves, name):
    n = len(halves)

    def body(*refs):
        o = refs[n:2 * n]
        ssem, rsem = refs[2 * n:]
        x, y, c = _place()

        def copy(t, half, dev):
            return pltpu.make_async_remote_copy(src_ref=o[t].at[c], dst_ref=o[t].at[half], send_sem=ssem.at[t],
                                                recv_sem=rsem.at[t], device_id=dev, device_id_type=MESH)

        sends = [copy(t, c, (x, y, 1 - c)) for t in range(n)]
        for cp in sends:
            cp.start()
        for t in range(n):
            copy(t, 1 - c, (x, y, c)).wait_recv()
        for cp in sends:
            cp.wait_send()

    return _pcall(body, name=name, out_shape=[_sds(h.shape, h.dtype) for h in halves], in_specs=[ANY] * n,
                  out_specs=[ANY] * n, aliases={t: t for t in range(n)},
                  scratch=[pltpu.SemaphoreType.DMA((n,)), pltpu.SemaphoreType.DMA((n,))])(*halves)


def _row_spec(tm, cols):
    return pl.BlockSpec((tm, cols), lambda i: (i, 0))


def _vec_spec(cols, rows=1):
    return pl.BlockSpec((rows, cols), lambda i: (0, 0))


def _modulated_norm(xv, g, shift, scale):
    r = lax.rsqrt(jnp.mean(xv * xv, axis=-1, keepdims=True) + EPS)
    return (((xv * r) * g) * (1.0 + scale) + shift).astype(bf16)


def _hnorm(x, g, shift, scale):
    T, tm = x.shape[0], 256

    def body(x_ref, g_ref, sh_ref, sc_ref, h_ref):
        h_ref[...] = _modulated_norm(x_ref[...], g_ref[...], sh_ref[...], sc_ref[...])

    return _pcall(body, name="hnorm", out_shape=_sds((T, D), bf16), grid=(T // tm,),
                  in_specs=[_row_spec(tm, D), _vec_spec(D), _vec_spec(D), _vec_spec(D)],
                  out_specs=_row_spec(tm, D))(x, g, shift, scale)


def _out_proj(y2, wo, x, gate, nxt=None):
    T, tm = x.shape[0], 512

    def body(y_ref, w_ref, x_ref, g_ref, *rest):
        o = jnp.dot(y_ref[0], w_ref[0], preferred_element_type=f32)
        o = o + jnp.dot(y_ref[1], w_ref[1], preferred_element_type=f32)
        xo = x_ref[...] + g_ref[...] * o
        if nxt is None:
            xo_ref, o_ref = rest
        else:
            ng_ref, nsh_ref, nsc_ref, xo_ref, o_ref, h_ref = rest
            h_ref[...] = _modulated_norm(xo, ng_ref[...], nsh_ref[...], nsc_ref[...])
        o_ref[...] = o.astype(bf16)
        xo_ref[...] = xo

    extra = [] if nxt is None else list(nxt)
    n_out = 2 if nxt is None else 3
    return _pcall(body, name="out_proj", out_shape=[_sds((T, D), f32), _sds((T, D), bf16), _sds((T, D), bf16)][:n_out],
                  grid=(T // tm,),
                  in_specs=[pl.BlockSpec((2, tm, D), lambda i: (0, i, 0)), pl.BlockSpec((2, D, D), lambda i: (0, 0, 0)),
                            _row_spec(tm, D), _vec_spec(D)] + [_vec_spec(D)] * len(extra),
                  out_specs=[_row_spec(tm, D)] * n_out, vmem_mb=40)(y2, wo, x, gate, *extra)


def _gate_bwd_tile(dx, o_ref, gate_ref, dob_ref, dgate_ref):
    dob_ref[...] = (dx * gate_ref[...]).astype(bf16)
    dgate_ref[...] += jnp.sum(dx * o_ref[...].astype(f32), axis=0, keepdims=True)


def _loss_bwd(x, target, g, o, gate):
    T, tm = x.shape[0], 256

    def body(x_ref, t_ref, g_ref, o_ref, gate_ref, dx_ref, loss_ref, dg_ref, dob_ref, dgate_ref):
        @pl.when(pl.program_id(0) == 0)
        def _():
            loss_ref[...] = jnp.zeros_like(loss_ref)
            dg_ref[...] = jnp.zeros_like(dg_ref)
            dgate_ref[...] = jnp.zeros_like(dgate_ref)

        xv, gv = x_ref[...], g_ref[...]
        r = lax.rsqrt(jnp.mean(xv * xv, axis=-1, keepdims=True) + EPS)
        xn = xv * r
        err = xn * gv - t_ref[...]
        dy = err * (1.0 / D)
        dxn = dy * gv
        dx = r * (dxn - xn * jnp.mean(dxn * xn, axis=-1, keepdims=True))
        dx_ref[...] = dx
        dg_ref[...] += jnp.sum(dy * xn, axis=0, keepdims=True)
        loss_ref[...] += (0.5 / D) * jnp.sum(jnp.sum(err * err, axis=1, keepdims=True), axis=0, keepdims=True)
        _gate_bwd_tile(dx, o_ref, gate_ref, dob_ref, dgate_ref)

    return _pcall(body, name="loss_bwd",
                  out_shape=[_sds((T, D), f32), _sds((1, 1), f32), _sds((1, D), f32), _sds((T, D), bf16), _sds((1, D), f32)],
                  grid=(T // tm,),
                  in_specs=[_row_spec(tm, D), _row_spec(tm, D), _vec_spec(D), _row_spec(tm, D), _vec_spec(D)],
                  out_specs=[_row_spec(tm, D), pl.BlockSpec((1, 1), lambda i: (0, 0)), _vec_spec(D), _row_spec(tm, D),
                             _vec_spec(D)])(x, target, g, o, gate)


def _norm_bwd(x, dh, gin, g, scale, below=None):
    T, tm = x.shape[0], 256

    def body(x_ref, dh_ref, gin_ref, g_ref, sc_ref, *rest):
        if below is None:
            dx_ref, st_ref = rest
        else:
            o_ref, gate_ref, dx_ref, st_ref, dob_ref, dgate_ref = rest

        @pl.when(pl.program_id(0) == 0)
        def _():
            st_ref[...] = jnp.zeros_like(st_ref)
            if below is not None:
                dgate_ref[...] = jnp.zeros_like(dgate_ref)

        xv, gv, dhv = x_ref[...], g_ref[...], dh_ref[...]
        r = lax.rsqrt(jnp.mean(xv * xv, axis=-1, keepdims=True) + EPS)
        xn = xv * r
        da = dhv * (1.0 + sc_ref[...])
        dxn = da * gv
        dx = gin_ref[...] + r * (dxn - xn * jnp.mean(dxn * xn, axis=-1, keepdims=True))
        dx_ref[...] = dx
        st_ref[0:1, :] += jnp.sum(dhv, axis=0, keepdims=True)
        st_ref[1:2, :] += jnp.sum(dhv * (xn * gv), axis=0, keepdims=True)
        st_ref[2:3, :] += jnp.sum(da * xn, axis=0, keepdims=True)
        if below is not None:
            _gate_bwd_tile(dx, o_ref, gate_ref, dob_ref, dgate_ref)

    out_shape = [_sds((T, D), f32), _sds((8, D), f32)]
    in_specs = [_row_spec(tm, D), _row_spec(tm, D), _row_spec(tm, D), _vec_spec(D), _vec_spec(D)]
    out_specs = [_row_spec(tm, D), _vec_spec(D, 8)]
    args = [x, dh, gin, g, scale]
    if below is not None:
        out_shape += [_sds((T, D), bf16), _sds((1, D), f32)]
        in_specs += [_row_spec(tm, D), _vec_spec(D)]
        out_specs += [_row_spec(tm, D), _vec_spec(D)]
        args += list(below)
    return _pcall(body, name="norm_bwd", out_shape=out_shape, grid=(T // tm,), in_specs=in_specs,
                  out_specs=out_specs)(*args)


STEPS = 4
ADAMW_STEPS = 8


def _cast_place(place, ws, layer, after=None):
    n = len(ws)

    def body(place_ref, *refs):
        for t in range(n):
            refs[-n + t][...] = refs[t][...].astype(bf16)

    def tile(w):
        return w.shape[1] // STEPS, w.shape[2]

    extra = [] if after is None else [after]
    return _pcall(body, name="cast_place", out_shape=[_sds((4,) + w.shape[1:], bf16) for w in ws], grid=(STEPS,),
                  prefetch=1,
                  in_specs=[pl.BlockSpec((None,) + tile(w), lambda i, pr: (layer, i, 0)) for w in ws] + [ANY] * len(extra),
                  out_specs=[pl.BlockSpec((None,) + tile(w), lambda i, pr: (pr[0], i, 0)) for w in ws])(
                      place, *ws, *extra)


def _rs_add(place, grads, theirs):
    n = len(grads)

    def body(place_ref, *refs):
        for t in range(n):
            refs[2 * n + t][...] = (refs[t][...].astype(f32) + refs[n + t][...].astype(f32)).astype(bf16)

    mine = [pl.BlockSpec((None, None) + g.shape[2:], lambda s, pr: (s, pr[1], 0, 0)) for g in grads]
    shard = [pl.BlockSpec((None,) + q.shape[1:], lambda s, pr: (s, 0, 0)) for q in theirs]
    return _pcall(body, name="rs_add", out_shape=[_sds(q.shape, bf16) for q in theirs], grid=(4,), prefetch=1,
                  in_specs=mine + shard, out_specs=shard)(place, *grads, *theirs)


def _rs_sum(place, pairs, slots):
    n, steps = len(pairs), 2

    def body(place_ref, *refs):
        for t in range(n):
            p_ref, q_ref = refs[t], refs[n + t]
            total = ((p_ref[...].astype(f32) + q_ref[0].astype(f32)) + q_ref[1].astype(f32)) + q_ref[2].astype(f32)
            refs[2 * n + t][...] = total.astype(bf16)

    def tile(q):
        return q.shape[1] // steps, q.shape[2]

    return _pcall(body, name="rs_sum", out_shape=[_sds((2,) + q.shape[1:], bf16) for q in slots], grid=(steps,),
                  prefetch=1,
                  in_specs=[pl.BlockSpec((None,) + tile(q), lambda i, pr: (pr[0], i, 0)) for q in slots]
                  + [pl.BlockSpec((3,) + tile(q), lambda i, pr: (0, i, 0)) for q in slots],
                  out_specs=[pl.BlockSpec((None,) + tile(q), lambda i, pr: (pr[1], i, 0)) for q in slots])(
                      place, *pairs, *slots)


def _adamw_math(w, g, m, v):
    m = ADAM_B1 * m + (1.0 - ADAM_B1) * g
    v = ADAM_B2 * v + (1.0 - ADAM_B2) * jnp.square(g)
    m_hat = m / (1.0 - ADAM_B1 ** ADAM_STEP)
    v_hat = v / (1.0 - ADAM_B2 ** ADAM_STEP)
    delta = -ADAM_LR * (m_hat / (jnp.sqrt(v_hat) + ADAM_EPS) + ADAM_WD * w)
    return delta, m, v


def _adamw_layer(layer, items):
    n = len(items)

    def body(*refs):
        outs = refs[-4 * n:]
        for t in range(n):
            w_ref, g_ref, m_ref, v_ref = refs[4 * t:4 * t + 4]
            g = g_ref[...].astype(f32)
            outs[4 * t][...] = g
            outs[4 * t + 1][...], outs[4 * t + 2][...], outs[4 * t + 3][...] = _adamw_math(
                w_ref[...], g, m_ref[...], v_ref[...])

    args, in_specs, out_specs, out_shape = [], [], [], []
    for w, g, m, v, _ in items:
        tr, cols = w.shape[1] // ADAMW_STEPS, w.shape[2]
        spec = pl.BlockSpec((None, tr, cols), lambda i: (layer, i, 0))
        args += [w, g, m, v]
        in_specs += [spec, pl.BlockSpec((tr, cols), lambda i: (i, 0)), spec, spec]
        out_specs += [spec] * 4
        out_shape += [_sds(w.shape, f32)] * 4
    aliases = {}
    for t, it in enumerate(items):
        if it[4] is not None:
            for k in range(4):
                aliases[len(args)] = 4 * t + k
                args.append(it[4][k])
                in_specs.append(ANY)
    res = _pcall(body, name="adamw", out_shape=out_shape, grid=(ADAMW_STEPS,), in_specs=in_specs, out_specs=out_specs,
                 aliases=aliases)(*args)
    return [tuple(res[4 * t:4 * t + 4]) for t in range(n)]


def _adamw_small(items):
    n = len(items)

    def body(*refs):
        ins, outs = refs[:4 * n], refs[4 * n:]
        for t in range(n):
            w_ref, g_ref, m_ref, v_ref = ins[4 * t:4 * t + 4]
            if len(g_ref.shape) == len(w_ref.shape) + 1:
                g = g_ref[0]
                for b in range(1, g_ref.shape[0]):
                    g = g + g_ref[b]
            else:
                g = g_ref[...]
            d, m, v = _adamw_math(w_ref[...], g, m_ref[...], v_ref[...])
            outs[4 * t][...], outs[4 * t + 1][...], outs[4 * t + 2][...], outs[4 * t + 3][...] = g, d, m, v

    out_shape = [_sds(w.shape, f32) for (w, _, _, _) in items for _ in range(4)]
    flat = [a for it in items for a in it]
    res = _pcall(body, name="adamw_small", out_shape=out_shape, in_specs=[VMEM] * (4 * n),
                 out_specs=[VMEM] * (4 * n))(*flat)
    return [tuple(res[4 * t:4 * t + 4]) for t in range(n)]


NN = ((1,), (0,))
NT = ((1,), (1,))
TN = ((0,), (0,))


def _mm(name, a, b, *, grid, a_spec, b_spec, out_shape, out_spec, dims, vmem_mb=None):
    def body(a_ref, b_ref, o_ref):
        r = lax.dot_general(a_ref[...], b_ref[...], (dims, ((), ())), preferred_element_type=f32)
        o_ref[...] = r.astype(o_ref.dtype)

    return _pcall(body, name=name, out_shape=out_shape, grid=grid, in_specs=[a_spec, b_spec], out_specs=out_spec,
                  vmem_mb=vmem_mb)(a, b)


def _whole(shape):
    return pl.BlockSpec(shape, lambda j: (0,) * len(shape))


def _split_spec(rows, tile, per_split):
    return pl.BlockSpec((None, rows, tile), lambda j: (j // per_split, 0, j % per_split))


class _Proj:
    def __init__(self, n, splits, tile):
        self.n, self.splits, self.tile = n, splits, tile
        self.steps = n // tile
        self.w_per = n // 4 // tile
        self.a_per = n // splits // tile
        assert self.w_per * tile * 4 == n and self.a_per * tile * splits == n

    def fwd(self, hb, wg):
        T = hb.shape[0]
        sub, tile, w_per = FWD_TILES, self.tile, self.w_per
        wide = sub * tile
        a_per = self.n // self.splits // wide
        assert a_per * wide * self.splits == self.n

        def w_tile(q):
            return pl.BlockSpec((None, D, tile), lambda j: ((sub * j + q) // w_per, 0, (sub * j + q) % w_per))

        def body(a_ref, *rest):
            w = jnp.concatenate([rest[q][...] for q in range(sub)], axis=1)
            rest[sub][...] = jnp.dot(a_ref[...], w, preferred_element_type=f32).astype(bf16)

        return _pcall(body, name="proj_fwd", out_shape=_sds((self.splits, T, self.n // self.splits), bf16),
                      grid=(self.n // wide,), in_specs=[_whole((T, D))] + [w_tile(q) for q in range(sub)],
                      out_specs=pl.BlockSpec((None, T, wide), lambda j: (j // a_per, 0, j % a_per)),
                      vmem_mb=40 if wide > 512 else None)(hb, *([wg] * sub))

    def dw(self, hb, dp):
        T = hb.shape[0]
        return _mm("proj_dw", hb, dp, grid=(self.steps,), a_spec=_whole((T, D)),
                   b_spec=_split_spec(T, self.tile, self.a_per), out_shape=_sds((4, D, self.n // 4), bf16),
                   out_spec=_split_spec(D, self.tile, self.w_per), dims=TN)

    def dh(self, dp, wg):
        T = dp.shape[1]
        sub, tile, w_per = DH_WIDE // self.tile, self.tile, self.w_per
        a_per = self.n // self.splits // DH_WIDE
        assert sub * tile == DH_WIDE and a_per * DH_WIDE * self.splits == self.n

        def w_tile(q):
            return pl.BlockSpec((None, D, tile), lambda k: ((sub * k + q) // w_per, 0, (sub * k + q) % w_per))

        def body(a_ref, *rest):
            o_ref = rest[sub]
            w = jnp.concatenate([rest[q][...] for q in range(sub)], axis=1)
            r = lax.dot_general(a_ref[...], w, (NT, ((), ())), preferred_element_type=f32)

            @pl.when(pl.program_id(0) == 0)
            def _():
                o_ref[...] = r

            @pl.when(pl.program_id(0) > 0)
            def _():
                o_ref[...] += r

        return _pcall(body, name="proj_dh", out_shape=_sds((T, D), f32), grid=(self.n // DH_WIDE,),
                      in_specs=[pl.BlockSpec((None, T, DH_WIDE), lambda k: (k // a_per, 0, k % a_per))]
                      + [w_tile(q) for q in range(sub)],
                      out_specs=_whole((T, D)), vmem_mb=40)(dp, *([wg] * sub))


EVEN_PROJ = _Proj(7 * D, 7, 256)
ODD_PROJ = _Proj(4 * D, 2, 512)


def _dy_mm(dob, wo):
    T = dob.shape[0]
    return _mm("out_dy", dob, wo, grid=(4,), a_spec=_whole((T, D)),
               b_spec=pl.BlockSpec((None, 512, D), lambda j: (j, 0, 0)), out_shape=_sds((2, T, D), bf16),
               out_spec=_split_spec(T, 512, 2), dims=NT)


def _dwo_mm(y2, dob):
    T = dob.shape[0]
    return _mm("out_dw", y2, dob, grid=(4,), a_spec=_split_spec(T, 512, 2), b_spec=_whole((T, D)),
               out_shape=_sds((4, 512, D), bf16), out_spec=pl.BlockSpec((None, 512, D), lambda j: (j, 0, 0)), dims=TN)


def _head_spec(lead, T):
    return pl.BlockSpec((lead, T, HEAD), lambda h: (0, 0, h))


def _head_vec(rows):
    return pl.BlockSpec((rows, HEAD), lambda h: (0, h))


_HEAD_MAT = pl.BlockSpec((None, HEAD, HEAD), lambda h: (h, 0, 0))


def _causal():
    return lax.broadcasted_iota(jnp.int32, (HEAD, HEAD), 0) >= lax.broadcasted_iota(jnp.int32, (HEAD, HEAD), 1)


def _layernorm_head(v):
    mu = jnp.mean(v, axis=-1, keepdims=True)
    d = v - mu
    rstd = lax.rsqrt(jnp.mean(d * d, axis=-1, keepdims=True) + EPS)
    return d * rstd, rstd


def _even_fwd(p7, conv_w, ln_g, ln_b, sgu_w, sgu_bias):
    T, C = p7.shape[1], CHUNK_ROWS

    def body(p_ref, cw_ref, lg_ref, lb_ref, w_ref, b_ref, y_ref):
        w0, w1, w2 = cw_ref[0:1, :], cw_ref[1:2, :], cw_ref[2:3, :]
        wm = jnp.where(_causal(), w_ref[...], 0.0).astype(bf16)
        bias, lg, lb = b_ref[...], lg_ref[...], lb_ref[...]

        def step(i, halo):
            rows = pl.ds(pl.multiple_of(i * C, C), C)
            ah, ab, ac, az, u, v, zb = (p_ref[k, rows, :].astype(f32) for k in range(7))
            tt = ac * ah
            ext = jnp.concatenate([halo, tt], axis=0)
            cv = w2 * tt + w1 * pltpu.roll(ext, 1, 0)[HALO_CONV:] + w0 * pltpu.roll(ext, 2, 0)[HALO_CONV:]
            y_ref[0, rows, :] = (ab * cv * _silu(az)).astype(bf16)
            vhat, _ = _layernorm_head(v)
            vn = (vhat * lg + lb).astype(bf16)
            mix = jnp.concatenate([jnp.dot(wm, vn[k * HEAD:(k + 1) * HEAD], preferred_element_type=f32) + bias
                                   for k in range(C // HEAD)], axis=0)
            y_ref[1, rows, :] = (u * mix * _silu(zb)).astype(bf16)
            return tt[C - HALO_CONV:]

        lax.fori_loop(0, T // C, step, jnp.zeros((HALO_CONV, HEAD), f32))

    return _pcall(body, name="even_fwd", out_shape=_sds((2, T, D), bf16), grid=(NH,),
                  in_specs=[_head_spec(7, T), _head_vec(3), _head_vec(1), _head_vec(1), _HEAD_MAT, _HEAD_MAT],
                  out_specs=_head_spec(2, T))(p7, conv_w, ln_g, ln_b, sgu_w, sgu_bias)


def _even_bwd(p7, dy2, conv_w, ln_g, ln_b, sgu_w, sgu_bias):
    T, C = p7.shape[1], CHUNK_ROWS
    n_chunks = T // C

    def body(p_ref, dy_ref, cw_ref, lg_ref, lb_ref, w_ref, b_ref,
             dp_ref, dcw_ref, dlg_ref, dlb_ref, dw_ref, dms_ref, dcv_s):
        w0, w1, w2 = cw_ref[0:1, :], cw_ref[1:2, :], cw_ref[2:3, :]
        tri = _causal()
        wm = jnp.where(tri, w_ref[...], 0.0).astype(bf16)
        bias, lg, lb = b_ref[...], lg_ref[...], lb_ref[...]
        dw_ref[...] = jnp.zeros_like(dw_ref)
        dms_ref[...] = jnp.zeros_like(dms_ref)

        def fwd_step(i, carry):
            halo, a0, a1, a2, alg, alb = carry
            rows = pl.ds(pl.multiple_of(i * C, C), C)
            ah, ab, ac, az = (p_ref[k, rows, :].astype(f32) for k in range(4))
            dya = dy_ref[0, rows, :].astype(f32)
            tt = ac * ah
            ext = jnp.concatenate([halo, tt], axis=0)
            t1, t2 = pltpu.roll(ext, 1, 0)[HALO_CONV:], pltpu.roll(ext, 2, 0)[HALO_CONV:]
            cv = w2 * tt + w1 * t1 + w0 * t2
            sa, dsa = _silu_and_grad(az)
            g1 = dya * sa
            dp_ref[1, rows, :] = (g1 * cv).astype(bf16)
            dp_ref[3, rows, :] = (dya * ab * cv * dsa).astype(bf16)
            dcv = g1 * ab
            dcv_s[rows, :] = dcv
            a2 = a2 + jnp.sum(dcv * tt, axis=0, keepdims=True)
            a1 = a1 + jnp.sum(dcv * t1, axis=0, keepdims=True)
            a0 = a0 + jnp.sum(dcv * t2, axis=0, keepdims=True)

            u, zb, dyb = p_ref[4, rows, :].astype(f32), p_ref[6, rows, :].astype(f32), dy_ref[1, rows, :].astype(f32)
            vhat, rstd = _layernorm_head(p_ref[5, rows, :].astype(f32))
            vn = (vhat * lg + lb).astype(bf16)
            sb, dsb = _silu_and_grad(zb)
            mix = jnp.concatenate([jnp.dot(wm, vn[k * HEAD:(k + 1) * HEAD], preferred_element_type=f32) + bias
                                   for k in range(C // HEAD)], axis=0)
            dp_ref[4, rows, :] = (dyb * mix * sb).astype(bf16)
            dp_ref[6, rows, :] = (dyb * u * mix * dsb).astype(bf16)
            dmix = dyb * u * sb
            dvn_parts = []
            for k in range(C // HEAD):
                dm = dmix[k * HEAD:(k + 1) * HEAD]
                dmb = dm.astype(bf16)
                dvn_parts.append(lax.dot_general(wm, dmb, (TN, ((), ())), preferred_element_type=f32))
                dw_ref[...] += lax.dot_general(dmb, vn[k * HEAD:(k + 1) * HEAD], (NT, ((), ())),
                                               preferred_element_type=f32)
                dms_ref[...] += dm
            dvn = jnp.concatenate(dvn_parts, axis=0)
            alg = alg + jnp.sum(dvn * vhat, axis=0, keepdims=True)
            alb = alb + jnp.sum(dvn, axis=0, keepdims=True)
            dvh = dvn * lg
            dv = rstd * (dvh - jnp.mean(dvh, axis=-1, keepdims=True)
                         - vhat * jnp.mean(dvh * vhat, axis=-1, keepdims=True))
            dp_ref[5, rows, :] = dv.astype(bf16)
            return tt[C - HALO_CONV:], a0, a1, a2, alg, alb

        zrow = jnp.zeros((1, HEAD), f32)
        _, a0, a1, a2, alg, alb = lax.fori_loop(
            0, n_chunks, fwd_step, (jnp.zeros((HALO_CONV, HEAD), f32), zrow, zrow, zrow, zrow, zrow))
        dcw_ref[0:1, :], dcw_ref[1:2, :], dcw_ref[2:3, :] = a0, a1, a2
        dlg_ref[...], dlb_ref[...] = alg, alb
        dw_ref[...] = jnp.where(tri, dw_ref[...], 0.0)

        def bwd_step(k, halo):
            rows = pl.ds(pl.multiple_of((n_chunks - 1 - k) * C, C), C)
            dcv = dcv_s[rows, :]
            ext = jnp.concatenate([dcv, halo], axis=0)
            n1 = pltpu.roll(ext, C + HALO_CONV - 1, 0)[:C]
            n2 = pltpu.roll(ext, C + HALO_CONV - 2, 0)[:C]
            dtt = w2 * dcv + w1 * n1 + w0 * n2
            dp_ref[2, rows, :] = (dtt * p_ref[0, rows, :].astype(f32)).astype(bf16)
            dp_ref[0, rows, :] = (dtt * p_ref[2, rows, :].astype(f32)).astype(bf16)
            return dcv[:HALO_CONV]

        lax.fori_loop(0, n_chunks, bwd_step, jnp.zeros((HALO_CONV, HEAD), f32))

    out_shape = [_sds((7, T, D), bf16), _sds((3, D), f32), _sds((1, D), f32), _sds((1, D), f32),
                 _sds((NH, HEAD, HEAD), f32), _sds((NH, HEAD, HEAD), f32)]
    return _pcall(body, name="even_bwd", out_shape=out_shape, grid=(NH,),
                  in_specs=[_head_spec(7, T), _head_spec(2, T), _head_vec(3), _head_vec(1), _head_vec(1),
                            _HEAD_MAT, _HEAD_MAT],
                  out_specs=[_head_spec(7, T), _head_vec(3), _head_vec(1), _head_vec(1), _HEAD_MAT, _HEAD_MAT],
                  scratch=[pltpu.VMEM((T, HEAD), f32)])(p7, dy2, conv_w, ln_g, ln_b, sgu_w, sgu_bias)


def _window_sum(ext, win, towards_past):
    n, k, s = ext.shape[0], 1, ext
    while k < win:
        s = s + pltpu.roll(s, k if towards_past else n - k, 0)
        k *= 2
    return s


def _pool_count(i, C, win):
    t = i * C + lax.broadcasted_iota(jnp.int32, (C, 1), 0)
    cnt = jnp.minimum(t + 1, win).astype(f32)
    return cnt, 1.0 / cnt


def _group_specs(T):
    p_spec = pl.BlockSpec((None, T, GC), lambda g: (0, 0, g))
    z_spec = pl.BlockSpec((None, T, GC), lambda g: (1, 0, g))
    pw_spec = pl.BlockSpec((4, GC // 4, GC), lambda g: (0, g, 0))
    ps_spec = pl.BlockSpec((1, GC), lambda g: (0, g))
    y_spec = pl.BlockSpec((None, T, GC), lambda g: (g // 2, 0, g % 2))
    return p_spec, z_spec, pw_spec, ps_spec, y_spec


def _odd_fwd(p2, pool_wg, pool_scale):
    T, C = p2.shape[1], CHUNK_ROWS
    p_spec, z_spec, pw_spec, ps_spec, y_spec = _group_specs(T)

    def body(p_ref, z_ref, pw_ref, ps_ref, y_ref):
        pw, ps = pw_ref[...].reshape(GC, GC), ps_ref[...]

        def run(win):
            def step(i, halo):
                rows = pl.ds(pl.multiple_of(i * C, C), C)
                p = p_ref[rows, :].astype(f32)
                s = _window_sum(jnp.concatenate([halo, p], axis=0), win, True)[HALO_POOL:]
                pooled = s * _pool_count(i, C, win)[1] - p
                ypre = jnp.dot(pooled.astype(bf16), pw, preferred_element_type=f32)
                y_ref[rows, :] = (ypre * ps * _silu(z_ref[rows, :].astype(f32))).astype(bf16)
                return p[C - HALO_POOL:]

            lax.fori_loop(0, T // C, step, jnp.zeros((HALO_POOL, GC), f32))

        for gi, win in enumerate(WINDOWS):
            pl.when(pl.program_id(0) == gi)(functools.partial(run, win))

    return _pcall(body, name="odd_fwd", out_shape=_sds((2, T, D), bf16), grid=(len(WINDOWS),),
                  in_specs=[p_spec, z_spec, pw_spec, ps_spec], out_specs=y_spec)(p2, p2, pool_wg, pool_scale)


def _odd_bwd(p2, dy2, pool_wg, pool_scale):
    T, C = p2.shape[1], CHUNK_ROWS
    n_chunks = T // C
    p_spec, z_spec, pw_spec, ps_spec, y_spec = _group_specs(T)

    def body(p_ref, z_ref, dy_ref, pw_ref, ps_ref, dp_ref, dpw_ref, dps_ref, q_s, acc_s):
        pw, ps = pw_ref[...].reshape(GC, GC), ps_ref[...]

        def run(win):
            acc_s[...] = jnp.zeros_like(acc_s)

            def fwd_step(i, carry):
                halo, aps = carry
                rows = pl.ds(pl.multiple_of(i * C, C), C)
                p, z, dy = p_ref[rows, :].astype(f32), z_ref[rows, :].astype(f32), dy_ref[rows, :].astype(f32)
                _, inv_cnt = _pool_count(i, C, win)
                s = _window_sum(jnp.concatenate([halo, p], axis=0), win, True)[HALO_POOL:]
                pb = (s * inv_cnt - p).astype(bf16)
                ypre = jnp.dot(pb, pw, preferred_element_type=f32)
                sz, dsz = _silu_and_grad(z)
                aps = aps + jnp.sum(dy * ypre * sz, axis=0, keepdims=True)
                dp_ref[1, rows, :] = (dy * ypre * ps * dsz).astype(bf16)
                dyp = (dy * ps * sz).astype(bf16)
                acc_s[...] += lax.dot_general(pb, dyp, (TN, ((), ())), preferred_element_type=f32)
                dpool = lax.dot_general(dyp, pw, (NT, ((), ())), preferred_element_type=f32)
                q_s[rows, :] = dpool * inv_cnt
                return p[C - HALO_POOL:], aps

            _, aps = lax.fori_loop(0, n_chunks, fwd_step, (jnp.zeros((HALO_POOL, GC), f32), jnp.zeros((1, GC), f32)))
            dps_ref[...] = aps
            dpw_ref[...] = acc_s[...].reshape(4, GC // 4, GC).astype(bf16)

            def bwd_step(k, halo):
                i = n_chunks - 1 - k
                rows = pl.ds(pl.multiple_of(i * C, C), C)
                q = q_s[rows, :]
                s = _window_sum(jnp.concatenate([q, halo], axis=0), win, False)[:C]
                dp_ref[0, rows, :] = (s - q * _pool_count(i, C, win)[0]).astype(bf16)
                return q[:HALO_POOL]

            lax.fori_loop(0, n_chunks, bwd_step, jnp.zeros((HALO_POOL, GC), f32))

        for gi, win in enumerate(WINDOWS):
            pl.when(pl.program_id(0) == gi)(functools.partial(run, win))

    out_shape = [_sds((2, T, 2 * D), bf16), _sds((4, GC, GC), bf16), _sds((1, 2 * D), f32)]
    return _pcall(body, name="odd_bwd", out_shape=out_shape, grid=(len(WINDOWS),),
                  in_specs=[p_spec, z_spec, y_spec, pw_spec, ps_spec],
                  out_specs=[pl.BlockSpec((2, T, GC), lambda g: (0, 0, g)), pw_spec, ps_spec],
                  scratch=[pltpu.VMEM((T, GC), f32), pltpu.VMEM((GC, GC), f32)], vmem_mb=44)(
                      p2, p2, dy2, pool_wg, pool_scale)


def _ada_fwd(c_all, ada_w):
    cols = ada_w.shape[2]

    def body(c_ref, w_ref, o_ref):
        o_ref[...] = jnp.dot(_silu(c_ref[...]), w_ref[...], preferred_element_type=f32,
                             precision=lax.Precision.HIGHEST)

    return _pcall(body, name="ada_fwd", out_shape=_sds((4, N_DEV, cols), f32), grid=(4,),
                  in_specs=[pl.BlockSpec((N_DEV, D), lambda i: (0, 0)), pl.BlockSpec((None, D, cols), lambda i: (i, 0, 0))],
                  out_specs=pl.BlockSpec((None, N_DEV, cols), lambda i: (i, 0, 0)))(c_all, ada_w)


def _ada_bwd(c_all_t, dmod, w, m, v):
    cols, tr = w.shape[2], 256
    spec = pl.BlockSpec((None, tr, cols), lambda l, i: (l, i, 0))

    def body(c_ref, dm_ref, w_ref, m_ref, v_ref, g_ref, d_ref, mo_ref, vo_ref):
        sc = _silu(c_ref[...])
        g = sc[:, 0:1] * dm_ref[0:1, :]
        for b in range(1, N_DEV):
            g = g + sc[:, b:b + 1] * dm_ref[b:b + 1, :]
        g_ref[...] = g
        d_ref[...], mo_ref[...], vo_ref[...] = _adamw_math(w_ref[...], g, m_ref[...], v_ref[...])

    return _pcall(body, name="ada_bwd", out_shape=[_sds(w.shape, f32)] * 4, grid=(4, D // tr),
                  in_specs=[pl.BlockSpec((tr, N_DEV), lambda l, i: (i, 0)),
                            pl.BlockSpec((None, N_DEV, cols), lambda l, i: (l, 0, 0)), spec, spec, spec],
                  out_specs=[spec] * 4)(c_all_t, dmod, w, m, v)


def _layer_fwd(even, x, hb, gate, w, nxt, before_out=None):
    if even:
        w_in, w_out, conv_w, ln_g, ln_b, sgu_w, sgu_b = w
        bias = jnp.broadcast_to(sgu_b[:, :, None], (NH, HEAD, HEAD))
        p = EVEN_PROJ.fwd(hb, w_in)
        y2 = _even_fwd(p, conv_w, ln_g, ln_b, sgu_w, bias)
    else:
        w_in, pool_w, w_out, pool_scale = w
        p = ODD_PROJ.fwd(hb, w_in)
        y2 = _odd_fwd(p, pool_w, pool_scale)
    if before_out is not None:
        late_w_out, tok = before_out(y2)
        if late_w_out is not None:
            w_out = late_w_out
            w = (w_in, w_out) + tuple(w[2:]) if even else (w_in, pool_w, w_out, pool_scale)
        if tok is not None:
            gate = gate + tok[0:1, 0:1]
    outs = _out_proj(y2, w_out.reshape(2, D, D), x, gate, nxt)
    return outs[0], (None if nxt is None else outs[2]), (x, hb, p, y2, outs[1]), w


def _layer_bwd(even, gin, dob, dgate, saved, scale, g, w, below=None, send=None):
    x_in, hb, p, y2, o = saved
    if even:
        w_in, w_out, conv_w, ln_g, ln_b, sgu_w, sgu_b = w
        bias = jnp.broadcast_to(sgu_b[:, :, None], (NH, HEAD, HEAD))
        dy2 = _dy_mm(dob, w_out)
        dp, dconv, dlg, dlb, dsw, dms = _even_bwd(p, dy2, conv_w, ln_g, ln_b, sgu_w, bias)
        proj = EVEN_PROJ
        small = dict(conv_w=dconv, ln_g=dlg, ln_b=dlb, sgu_w=dsw, sgu_b=jnp.sum(dms, axis=-1))
        big = [proj.dw(hb, dp), _dwo_mm(y2, dob)]
    else:
        w_in, pool_w, w_out, pool_scale = w
        dy2 = _dy_mm(dob, w_out)
        dp, dpw, dps = _odd_bwd(p, dy2, pool_w, pool_scale)
        proj = ODD_PROJ
        small = dict(pool_scale=dps)
        big = [proj.dw(hb, dp), dpw, _dwo_mm(y2, dob)]
    if send is not None:
        big, tok = send(big)
        scale = scale + tok[0:1, 0:1]
    dh = proj.dh(dp, w_in)
    res = _norm_bwd(x_in, dh, gin, g, scale, below)
    stats = res[1]
    return (res[0], (None if below is None else (res[2], res[3])), big, small,
            jnp.concatenate([stats[0:2], dgate], axis=0), stats[2:3])


def _pack_rows(parts):
    rows = [p.reshape(-1, LANES) for p in parts]
    total = sum(r.shape[0] for r in rows)
    padded = -(-total // (8 * N_DEV)) * (8 * N_DEV)
    if padded > total:
        rows.append(jnp.zeros((padded - total, LANES), f32))
    return jnp.concatenate(rows, axis=0)


def _unpack_rows(buf, shapes):
    out, r = [], 0
    for shp in shapes:
        n = 1
        for d in shp:
            n *= d
        out.append(buf[r:r + n // LANES].reshape(shp))
        r += n // LANES
    return out


def kernel(x, c, norm_g, ada_w, ada_b, ab_w_in, ab_conv_w, ab_ln_g, ab_ln_b, ab_sgu_w, ab_sgu_b, ab_w_out, c_w_in, c_pool_w, c_pool_scale, c_w_out, final_g, loss_target, m_norm_g, m_ada_w, m_ada_b, m_ab_w_in, m_ab_conv_w, m_ab_ln_g, m_ab_ln_b, m_ab_sgu_w, m_ab_sgu_b, m_ab_w_out, m_c_w_in, m_c_pool_w, m_c_pool_scale, m_c_w_out, m_final_g, v_norm_g, v_ada_w, v_ada_b, v_ab_w_in, v_ab_conv_w, v_ab_ln_g, v_ab_ln_b, v_ab_sgu_w, v_ab_sgu_b, v_ab_w_out, v_c_w_in, v_c_pool_w, v_c_pool_scale, v_c_w_out, v_final_g):
    ix, iy, ic = _place()
    chip, dev = 2 * ix + iy, 4 * ix + 2 * iy + ic
    n_even, n_odd = ab_w_in.shape[0], c_w_in.shape[0]
    depth = n_even + n_odd
    acols = ada_w.shape[2]

    place = jnp.stack([chip, ic]).astype(jnp.int32)
    even_names, odd_names = ["ab_w_in", "ab_w_out"], ["c_w_in", "c_pool_w", "c_w_out"]
    params = {"ab_w_in": (ab_w_in, m_ab_w_in, v_ab_w_in), "ab_w_out": (ab_w_out, m_ab_w_out, v_ab_w_out),
              "c_w_in": (c_w_in, m_c_w_in, v_c_w_in), "c_w_out": (c_w_out, m_c_w_out, v_c_w_out),
              "c_pool_w": tuple(a.reshape(n_odd, GC, GC) for a in (c_pool_w, m_c_pool_w, v_c_pool_w))}

    def placed(names, layer, after=None):
        ws = [params[nm][0] for nm in names]
        return [p.reshape(4, 2, p.shape[1] // 2, p.shape[2]) for p in _cast_place(place, ws, layer, after)]

    def whole(arrays):
        return [g.reshape(4, 2 * g.shape[2], g.shape[3]) for g in arrays]

    first = _gather8(jnp.concatenate([c, ab_conv_w.reshape(1, -1), c_pool_scale.reshape(1, -1)], axis=1), "gather_c")
    c_all, small_all = first[:, 0, :D], first[0::2, 0, D:]
    sems_a, in_a, tok = _ag_start([placed(even_names[:1], 0)], first[0:1, 0, 0:LANES], "ag_start_0a")
    modp = _ada_fwd(c_all, ada_w)
    later = [placed(even_names[1:], 0, tok)]
    later += [placed(even_names if i % 2 == 0 else odd_names, i // 2, tok) for i in range(1, depth)]
    modg = _gather8(modp + tok[0:1, 0:1], "gather_mod", [lay[-1] for lay in later])
    mod_rows = lax.dynamic_index_in_dim(modg[0::2], dev, axis=2, keepdims=False)
    mod = jnp.transpose(mod_rows, (1, 0, 2)).reshape(depth, 3 * D) + ada_b
    mods = [(mod[i:i + 1, 0:D], mod[i:i + 1, D:2 * D], mod[i:i + 1, 2 * D:3 * D]) for i in range(depth)]

    def shard_cols(a, width):
        return lax.dynamic_slice_in_dim(a, chip * width, width, axis=a.ndim - 1)

    n_conv = ab_conv_w.size
    conv_all = small_all[:, :n_conv].reshape(4, n_even, 3, D // 4)
    conv_full = jnp.transpose(conv_all, (1, 2, 0, 3)).reshape(n_even, 3, D)
    scale_all = small_all[:, n_conv:].reshape(4, n_odd, 2 * D // 4)
    scale_full = jnp.transpose(scale_all, (1, 0, 2)).reshape(n_odd, 2 * D)

    gathers_done = mod[0:1, 0:LANES] + scale_full[0:1, 0:LANES]
    sems_b, in_b, tok = _ag_start(later[:1], gathers_done, "ag_start_0b")
    sems_r, in_r, tok = _ag_start(later[1:], tok, "ag_start_rest")

    x_cur, saved, weights, handoff = x[0], [], [], {}
    hb = _hnorm(x_cur, norm_g[0:1], mods[0][0] + tok[0:1, 0:1], mods[0][1])
    for i in range(depth):
        j = i // 2
        if i == 0:
            full = whole(_ag_forward(_ag_wait(in_a[0], sems_a[0], hb, "ag_wait_0a"), "ag_forward")) + [None]
        else:
            full = whole(_agf_wait(*handoff.pop(i), x_cur, f"agf_wait_{i}"))
        if i % 2 == 0:
            w = (full[0], full[1], conv_full[j], ab_ln_g[j:j + 1], ab_ln_b[j:j + 1], ab_sgu_w[j], ab_sgu_b[j])
        else:
            w = (full[0], full[1], full[2], scale_full[j:j + 1])

        def before_out(y2, i=i):
            w_out, tok = None, None
            if i == 0:
                w_out = whole(_ag_forward(_ag_wait(in_b[0], sems_b[0], y2, "ag_wait_0b"), "ag_forward"))[0]
            if i + 1 < depth:
                arrived = _ag_wait(in_r[i], sems_r[i], y2, f"ag_wait_{i + 1}")
                sems_f, inflight, tok = _agf_start(arrived, f"agf_start_{i + 1}")
                handoff[i + 1] = (sems_f, inflight)
            return w_out, tok

        nxt = (norm_g[i + 1:i + 2], mods[i + 1][0], mods[i + 1][1]) if i + 1 < depth else None
        x_cur, hb, sv, w = _layer_fwd(i % 2 == 0, x_cur, hb, mods[i][2], w, nxt, before_out)
        weights.append(w)
        saved.append(sv)
    gin, loss, dfinal_g, dob, dgate = _loss_bwd(x_cur, loss_target[0], final_g.reshape(1, D), saved[-1][4],
                                                mods[-1][2])

    stacked = {}

    def finish(i, sems, pairs, lands, after):
        pairs, slots = _rs_chip_wait(sems, pairs, lands, after, f"rs_chip_wait_{i}")
        names = even_names if i % 2 == 0 else odd_names
        grads = _rs_half_exchange(_rs_sum(place, pairs, slots), "rs_half_exchange")
        items = [(params[nm][0], g.reshape(params[nm][0].shape[1:]), params[nm][1], params[nm][2], stacked.get(nm))
                 for nm, g in zip(names, grads)]
        for nm, res in zip(names, _adamw_layer(i // 2, items)):
            stacked[nm] = res

    small_g, dmod, dnorm_g, pending, tok = [None] * depth, [None] * depth, [None] * depth, None, None
    for i in reversed(range(depth)):
        w = weights[i]
        if tok is not None:
            w = w[:2] + (w[2] + tok[0:1, 0:1],) + w[3:] if i % 2 == 0 else w[:3] + (w[3] + tok[0:1, 0:1],)
        below = (saved[i - 1][4], mods[i - 1][2]) if i > 0 else None

        def send(big_g, i=i):
            big_g = [g.reshape(4, 2, g.shape[1] // 2, g.shape[2]) for g in big_g]
            sems, big_g, lands, tok = _rs_pair_start(big_g, f"rs_pair_start_{i}")
            return (sems, big_g, lands), tok

        gin, gate_bwd, sent, small_g[i], dmod[i], dnorm_g[i] = _layer_bwd(
            i % 2 == 0, gin, dob, dgate, saved[i], mods[i][1], norm_g[i:i + 1], w, below, send)
        if below is not None:
            dob, dgate = gate_bwd
        after = gin
        if i == 0:
            dmod_all = _gather8(jnp.stack(dmod).reshape(depth * 3 * D // LANES, LANES), "gather_dmod")
            after = dmod_all = dmod_all.reshape(N_DEV, depth, 3 * D)
        big_g, theirs = _rs_pair_wait(*sent, after, f"rs_pair_wait_{i}")
        pairs = _rs_add(place, big_g, theirs)
        sems, pairs, lands, tok = _rs_chip_start(pairs, f"rs_chip_start_{i}")
        if pending is not None:
            finish(*pending, tok)
        pending = (i, sems, pairs, lands)
    grad_x = gin
    dnorm_g = jnp.concatenate(dnorm_g, axis=0)

    dmod_cols = jnp.transpose(shard_cols(dmod_all, acols), (1, 0, 2))
    r_ada_w = _ada_bwd(c_all.T, dmod_cols, ada_w, m_ada_w, v_ada_w)
    finish(*pending, r_ada_w[1])
    r_ab_w_in, r_ab_w_out, r_c_w_in, r_c_w_out = (stacked[nm] for nm in ("ab_w_in", "ab_w_out", "c_w_in", "c_w_out"))
    r_c_pool_w = tuple(a.reshape(c_pool_w.shape) for a in stacked["c_pool_w"])

    small_parts = [dnorm_g, dfinal_g,
                   jnp.stack([small_g[2 * j]["conv_w"] for j in range(n_even)]),
                   jnp.concatenate([small_g[2 * j]["ln_g"] for j in range(n_even)], axis=0),
                   jnp.concatenate([small_g[2 * j]["ln_b"] for j in range(n_even)], axis=0),
                   jnp.stack([small_g[2 * j]["sgu_b"] for j in range(n_even)]),
                   jnp.concatenate([small_g[2 * j + 1]["pool_scale"] for j in range(n_odd)], axis=0),
                   jnp.pad(loss, ((0, 7), (0, LANES - 1)))]
    small_shapes = [p.shape for p in small_parts]
    sgu_parts = [small_g[2 * j]["sgu_w"].reshape(NH * HEAD, HEAD) for j in range(n_even)]
    reduced = _allreduce8([_pack_rows(small_parts)] + sgu_parts, "allreduce_small", r_ab_w_in[1])
    g_norm_g, g_final_g, g_conv_full, g_ln_g, g_ln_b, g_sgu_b, g_scale_full, loss_row = _unpack_rows(reduced[0],
                                                                                                     small_shapes)
    g_sgu_w = jnp.stack(reduced[1:])
    loss = loss_row[0, 0]
    g_conv = shard_cols(g_conv_full, D // 4)
    g_scale = shard_cols(g_scale_full, 2 * D // 4)

    def two_d(a):
        return a.reshape(-1, a.shape[-1])

    small = [(norm_g, g_norm_g, m_norm_g, v_norm_g),
             (ada_b, dmod_all, m_ada_b, v_ada_b),
             (two_d(ab_conv_w), two_d(g_conv), two_d(m_ab_conv_w), two_d(v_ab_conv_w)),
             (ab_ln_g, g_ln_g, m_ab_ln_g, v_ab_ln_g),
             (ab_ln_b, g_ln_b, m_ab_ln_b, v_ab_ln_b),
             (two_d(ab_sgu_w), two_d(g_sgu_w), two_d(m_ab_sgu_w), two_d(v_ab_sgu_w)),
             (two_d(ab_sgu_b), two_d(g_sgu_b), two_d(m_ab_sgu_b), two_d(v_ab_sgu_b)),
             (c_pool_scale, g_scale, m_c_pool_scale, v_c_pool_scale),
             (final_g.reshape(1, D), g_final_g, m_final_g.reshape(1, D), v_final_g.reshape(1, D))]
    small_res = _adamw_small(small)
    small_shapes_out = [norm_g.shape, ada_b.shape, ab_conv_w.shape, ab_ln_g.shape, ab_ln_b.shape, ab_sgu_w.shape,
                        ab_sgu_b.shape, c_pool_scale.shape, final_g.shape]
    (r_norm_g, r_ada_b, r_conv, r_ln_g, r_ln_b, r_sgu_w, r_sgu_b, r_scale, r_final_g) = [
        tuple(a.reshape(shp) for a in res) for res, shp in zip(small_res, small_shapes_out)]

    order = [r_norm_g, r_ada_w, r_ada_b, r_ab_w_in, r_conv, r_ln_g, r_ln_b, r_sgu_w, r_sgu_b, r_ab_w_out,
             r_c_w_in, r_c_pool_w, r_scale, r_c_w_out, r_final_g]
    outs = [loss, grad_x[None]]
    for field in range(4):
        outs += [r[field] for r in order]
    return tuple(outs)
```

```python
import functools

import jax
import jax.numpy as jnp
from jax import lax
from jax.experimental import pallas as pl
from jax.experimental.pallas import tpu as pltpu

f32, bf16 = jnp.float32, jnp.bfloat16

D = 1024
HEAD = 128
NH = 8
WINDOWS = (2, 4, 8, 16)
GC = 512
EPS = 1e-6
HALO_CONV = 8
HALO_POOL = 16
CHUNK_ROWS = 512
DH_WIDE = 1024
FWD_TILES = 2
N_DEV = 8
LANES = 128

ADAM_LR, ADAM_B1, ADAM_B2, ADAM_EPS, ADAM_WD, ADAM_STEP = 0.001, 0.9, 0.999, 1e-08, 0.01, 10

MESH = pl.DeviceIdType.MESH
ANY = pl.BlockSpec(memory_space=pl.ANY)
VMEM = pl.BlockSpec(memory_space=pltpu.VMEM)
MIB = 2 ** 20


def _pcall(body, *, name, out_shape, grid=None, in_specs=None, out_specs=None, scratch=(), vmem_mb=None,
           aliases=None, prefetch=0):
    kw = {}
    if prefetch:
        kw["grid_spec"] = pltpu.PrefetchScalarGridSpec(num_scalar_prefetch=prefetch, grid=grid, in_specs=in_specs,
                                                       out_specs=out_specs, scratch_shapes=list(scratch))
    else:
        if grid is not None:
            kw["grid"] = grid
        if in_specs is not None:
            kw["in_specs"] = in_specs
        if out_specs is not None:
            kw["out_specs"] = out_specs
        if scratch:
            kw["scratch_shapes"] = list(scratch)
    if aliases:
        kw["input_output_aliases"] = aliases
    params = pltpu.CompilerParams(vmem_limit_bytes=None if vmem_mb is None else vmem_mb * MIB)
    return pl.pallas_call(body, name=name, out_shape=out_shape, compiler_params=params, **kw)


def _sds(shape, dtype):
    return jax.ShapeDtypeStruct(tuple(shape), dtype)


def _sigmoid(z):
    return pl.reciprocal(1.0 + jnp.exp(-z), approx=True)


def _silu(z):
    return z * _sigmoid(z)


def _silu_and_grad(z):
    s = _sigmoid(z)
    return z * s, s * (1.0 + z * (1.0 - s))


def _place():
    return lax.axis_index("x"), lax.axis_index("y"), lax.axis_index("c")


def _gather8(blk, name, after=()):
    def body(x_ref, *rest):
        o_ref, ssem, rsem = rest[len(after):]
        x, y, c = _place()
        me = 4 * x + 2 * y + c
        o_ref[me] = x_ref[...]
        sends = []
        for k in range(1, N_DEV):
            px = 1 - x if k & 4 else x
            py = 1 - y if k & 2 else y
            pc = 1 - c if k & 1 else c
            cp = pltpu.make_async_remote_copy(src_ref=x_ref, dst_ref=o_ref.at[me], send_sem=ssem.at[k - 1],
                                              recv_sem=rsem.at[k - 1], device_id=(px, py, pc), device_id_type=MESH)
            cp.start()
            sends.append((cp, 4 * px + 2 * py + pc))
        for k, (cp, peer) in enumerate(sends):
            pltpu.make_async_remote_copy(src_ref=x_ref, dst_ref=o_ref.at[peer], send_sem=ssem.at[k],
                                         recv_sem=rsem.at[k], device_id=(x, y, c), device_id_type=MESH).wait_recv()
        for cp, _ in sends:
            cp.wait_send()

    return _pcall(body, name=name, out_shape=_sds((N_DEV,) + blk.shape, blk.dtype), in_specs=[VMEM] + [ANY] * len(after),
                  out_specs=VMEM,
                  scratch=[pltpu.SemaphoreType.DMA((N_DEV - 1,)), pltpu.SemaphoreType.DMA((N_DEV - 1,))])(blk, *after)


def _allreduce8(bufs, name, after=None):
    n, n_after = len(bufs), 0 if after is None else 1
    rbs = [b.shape[0] // N_DEV for b in bufs]
    assert all(rb * N_DEV == b.shape[0] and rb % 8 == 0 for rb, b in zip(rbs, bufs))

    def body(*refs):
        refs = refs[:n] + refs[n + n_after:]
        xs, outs, stages = refs[:n], refs[n:2 * n], refs[2 * n:3 * n]
        ssem, rsem = refs[3 * n:]
        x, y, c = _place()
        me = 4 * x + 2 * y + c
        peers = []
        for k in range(1, N_DEV):
            px = 1 - x if k & 4 else x
            py = 1 - y if k & 2 else y
            pc = 1 - c if k & 1 else c
            peers.append(((px, py, pc), 4 * px + 2 * py + pc))

        def blk(t, ref, idx):
            return ref.at[pl.ds(pl.multiple_of(idx * rbs[t], 8), rbs[t]), :]

        def copy(t, phase, k, src, dst, dev):
            return pltpu.make_async_remote_copy(src_ref=src, dst_ref=dst, send_sem=ssem.at[t, phase, k],
                                                recv_sem=rsem.at[t, phase, k], device_id=dev, device_id_type=MESH)

        scatter = [copy(t, 0, k, blk(t, xs[t], pidx), stages[t].at[me], dev)
                   for t in range(n) for k, (dev, pidx) in enumerate(peers)]
        for cp in scatter:
            cp.start()
        gather = []
        for t in range(n):
            stages[t][me] = blk(t, xs[t], me)[...]
            for k, (dev, pidx) in enumerate(peers):
                copy(t, 0, k, blk(t, xs[t], pidx), stages[t].at[pidx], dev).wait_recv()
            total = stages[t][0]
            for j in range(1, N_DEV):
                total = total + stages[t][j]
            blk(t, outs[t], me)[...] = total
            sends = [copy(t, 1, k, blk(t, outs[t], me), blk(t, outs[t], me), dev) for k, (dev, pidx) in enumerate(peers)]
            for cp in sends:
                cp.start()
            gather += sends
        for t in range(n):
            for k, (dev, pidx) in enumerate(peers):
                copy(t, 1, k, blk(t, outs[t], pidx), blk(t, outs[t], pidx), dev).wait_recv()
        for cp in scatter + gather:
            cp.wait_send()

    return _pcall(body, name=name, out_shape=[_sds(b.shape, f32) for b in bufs], in_specs=[VMEM] * n + [ANY] * n_after,
                  out_specs=[VMEM] * n,
                  scratch=[pltpu.VMEM((N_DEV, rb, LANES), f32) for rb in rbs]
                  + [pltpu.SemaphoreType.DMA((n, 2, N_DEV - 1)), pltpu.SemaphoreType.DMA((n, 2, N_DEV - 1))])(
                      *bufs, *([] if after is None else [after]))


def _other_chips(x, y):
    return [((1 - x, y), 2 * (1 - x) + y), ((x, 1 - y), 2 * x + (1 - y)), ((1 - x, 1 - y), 2 * (1 - x) + (1 - y))]


HBM = pl.BlockSpec(memory_space=pltpu.HBM)
SEM = pl.BlockSpec(memory_space=pltpu.SEMAPHORE)
EFFECT = pltpu.SideEffectType.DATAFLOW_SIDE_EFFECTING


def _in_hbm(a):
    return pltpu.with_memory_space_constraint(a, pltpu.HBM)


def _ag_start(layers, after, name):
    flat = [t for lay in layers for t in lay]
    n, nl = len(flat), len(layers)

    def body(*refs):
        src = refs[:n]
        sems = refs[n + 1:n + 1 + 2 * nl]
        token = refs[-1]
        x, y, c = _place()
        s_me = 2 * x + y
        t = 0
        for i, lay in enumerate(layers):
            for k in range(len(lay)):
                for j, ((px, py), _) in enumerate(_other_chips(x, y)):
                    pltpu.make_async_remote_copy(src_ref=src[t].at[s_me, c], dst_ref=src[t].at[s_me, c],
                                                 send_sem=sems[2 * i].at[3 * k + j], recv_sem=sems[2 * i + 1].at[3 * k + j],
                                                 device_id=(px, py, c), device_id_type=MESH).start()
                t += 1
        token[...] = jnp.zeros_like(token)

    sem_shapes = [pltpu.SemaphoreType.DMA((3 * len(lay),)) for lay in layers for _ in range(2)]
    out_shape = sem_shapes + [pltpu.HBM(t.shape, t.dtype) for t in flat] + [_sds((8, LANES), f32)]
    outs = pl.pallas_call(
        body, name=name, out_shape=out_shape, in_specs=[HBM] * n + [ANY],
        out_specs=[SEM] * (2 * nl) + [HBM] * n + [VMEM], input_output_aliases={t: 2 * nl + t for t in range(n)},
        compiler_params=pltpu.CompilerParams(has_side_effects=EFFECT))(*[_in_hbm(t) for t in flat], after)
    sems = [(outs[2 * i], outs[2 * i + 1]) for i in range(nl)]
    thru, t = [], 2 * nl
    for lay in layers:
        thru.append(list(outs[t:t + len(lay)]))
        t += len(lay)
    return sems, thru, outs[-1]


def _ag_wait(inflight, sems, after, name):
    n = len(inflight)

    def body(*refs):
        src, ssem, rsem = refs[:n], refs[n], refs[n + 1]
        x, y, c = _place()
        s_me = 2 * x + y
        for k in range(n):
            for j, (_, s_p) in enumerate(_other_chips(x, y)):
                cp = pltpu.make_async_remote_copy(src_ref=src[k].at[s_me, c], dst_ref=src[k].at[s_p, c],
                                                  send_sem=ssem.at[3 * k + j], recv_sem=rsem.at[3 * k + j],
                                                  device_id=(x, y, c), device_id_type=MESH)
                cp.wait_send()
                cp.wait_recv()

    return pl.pallas_call(
        body, name=name, out_shape=[pltpu.HBM(t.shape, t.dtype) for t in inflight],
        in_specs=[HBM] * n + [SEM, SEM, ANY], out_specs=[HBM] * n, input_output_aliases={t: t for t in range(n)},
        compiler_params=pltpu.CompilerParams(has_side_effects=EFFECT))(*inflight, sems[0], sems[1], after)


def _ag_forward(arrived, name):
    n = len(arrived)

    def body(*refs):
        o = refs[n:2 * n]
        ssem, rsem = refs[2 * n:]
        x, y, c = _place()

        def copy(t, j, s, half, dev):
            return pltpu.make_async_remote_copy(src_ref=o[t].at[s, c], dst_ref=o[t].at[s, half], send_sem=ssem.at[t, j],
                                                recv_sem=rsem.at[t, j], device_id=dev, device_id_type=MESH)

        chips = _other_chips(x, y)
        sends = [copy(t, j, s_p, c, (x, y, 1 - c)) for t in range(n) for j, (_, s_p) in enumerate(chips)]
        for cp in sends:
            cp.start()
        for t in range(n):
            for j, (_, s_p) in enumerate(chips):
                copy(t, j, s_p, 1 - c, (x, y, c)).wait_recv()
        for cp in sends:
            cp.wait_send()

    return _pcall(body, name=name, out_shape=[_sds(p.shape, bf16) for p in arrived], in_specs=[ANY] * n,
                  out_specs=[ANY] * n, aliases={t: t for t in range(n)},
                  scratch=[pltpu.SemaphoreType.DMA((n, 3)), pltpu.SemaphoreType.DMA((n, 3))])(*arrived)


def _agf_start(arrived, name):
    n = len(arrived)

    def body(*refs):
        o = refs[:n]
        ssem, rsem, token = refs[n], refs[n + 1], refs[-1]
        x, y, c = _place()
        for t in range(n):
            for j, (_, s_p) in enumerate(_other_chips(x, y)):
                pltpu.make_async_remote_copy(src_ref=o[t].at[s_p, c], dst_ref=o[t].at[s_p, c],
                                             send_sem=ssem.at[3 * t + j], recv_sem=rsem.at[3 * t + j],
                                             device_id=(x, y, 1 - c), device_id_type=MESH).start()
        token[...] = jnp.zeros_like(token)

    out_shape = ([pltpu.SemaphoreType.DMA((3 * n,))] * 2 + [pltpu.HBM(a.shape, bf16) for a in arrived]
                 + [_sds((8, LANES), f32)])
    outs = pl.pallas_call(
        body, name=name, out_shape=out_shape, in_specs=[HBM] * n, out_specs=[SEM, SEM] + [HBM] * n + [VMEM],
        input_output_aliases={t: 2 + t for t in range(n)},
        compiler_params=pltpu.CompilerParams(has_side_effects=EFFECT))(*[_in_hbm(a) for a in arrived])
    return (outs[0], outs[1]), list(outs[2:2 + n]), outs[-1]


def _agf_wait(sems, inflight, after, name):
    n = len(inflight)

    def body(*refs):
        o, ssem, rsem = refs[:n], refs[n], refs[n + 1]
        x, y, c = _place()
        for t in range(n):
            for j, (_, s_p) in enumerate(_other_chips(x, y)):
                cp = pltpu.make_async_remote_copy(src_ref=o[t].at[s_p, c], dst_ref=o[t].at[s_p, 1 - c],
                                                  send_sem=ssem.at[3 * t + j], recv_sem=rsem.at[3 * t + j],
                                                  device_id=(x, y, c), device_id_type=MESH)
                cp.wait_send()
                cp.wait_recv()

    return pl.pallas_call(
        body, name=name, out_shape=[pltpu.HBM(a.shape, bf16) for a in inflight],
        in_specs=[HBM] * n + [SEM, SEM, ANY], out_specs=[HBM] * n, input_output_aliases={t: t for t in range(n)},
        compiler_params=pltpu.CompilerParams(has_side_effects=EFFECT))(*inflight, sems[0], sems[1], after)


def _rs_pair_start(grads, name):
    n = len(grads)

    def body(*refs):
        g, theirs = refs[:n], refs[n:2 * n]
        ssem, rsem, token = refs[2 * n], refs[2 * n + 1], refs[-1]
        x, y, c = _place()
        for t in range(n):
            pltpu.make_async_remote_copy(src_ref=g[t].at[:, 1 - c], dst_ref=theirs[t], send_sem=ssem.at[t],
                                         recv_sem=rsem.at[t], device_id=(x, y, 1 - c), device_id_type=MESH).start()
        token[...] = jnp.zeros_like(token)

    lands = [lax.empty((4,) + g.shape[2:], bf16) for g in grads]
    out_shape = ([pltpu.SemaphoreType.DMA((n,))] * 2 + [pltpu.HBM(g.shape, bf16) for g in grads]
                 + [pltpu.HBM(q.shape, bf16) for q in lands] + [_sds((8, LANES), f32)])
    outs = pl.pallas_call(
        body, name=name, out_shape=out_shape, in_specs=[HBM] * (2 * n), out_specs=[SEM, SEM] + [HBM] * (2 * n) + [VMEM],
        input_output_aliases={t: 2 + t for t in range(2 * n)},
        compiler_params=pltpu.CompilerParams(has_side_effects=EFFECT))(*[_in_hbm(a) for a in list(grads) + lands])
    return (outs[0], outs[1]), list(outs[2:2 + n]), list(outs[2 + n:2 + 2 * n]), outs[-1]


def _rs_pair_wait(sems, grads, lands, after, name):
    n = len(grads)

    def body(*refs):
        g, theirs = refs[:n], refs[n:2 * n]
        ssem, rsem = refs[2 * n], refs[2 * n + 1]
        x, y, c = _place()
        for t in range(n):
            cp = pltpu.make_async_remote_copy(src_ref=g[t].at[:, 1 - c], dst_ref=theirs[t], send_sem=ssem.at[t],
                                              recv_sem=rsem.at[t], device_id=(x, y, c), device_id_type=MESH)
            cp.wait_send()
            cp.wait_recv()

    outs = pl.pallas_call(
        body, name=name, out_shape=[pltpu.HBM(a.shape, bf16) for a in list(grads) + list(lands)],
        in_specs=[HBM] * (2 * n) + [SEM, SEM, ANY], out_specs=[HBM] * (2 * n),
        input_output_aliases={t: t for t in range(2 * n)},
        compiler_params=pltpu.CompilerParams(has_side_effects=EFFECT))(*grads, *lands, sems[0], sems[1], after)
    return list(outs[:n]), list(outs[n:])


def _rs_chip_start(pairs, name):
    n = len(pairs)

    def body(*refs):
        p, q = refs[:n], refs[n:2 * n]
        ssem, rsem, token = refs[2 * n], refs[2 * n + 1], refs[-1]
        x, y, c = _place()
        for t in range(n):
            for j, ((px, py), s_p) in enumerate(_other_chips(x, y)):
                pltpu.make_async_remote_copy(src_ref=p[t].at[s_p], dst_ref=q[t].at[j], send_sem=ssem.at[3 * t + j],
                                             recv_sem=rsem.at[3 * t + j], device_id=(px, py, c), device_id_type=MESH).start()
        token[...] = jnp.zeros_like(token)

    lands = [lax.empty((3,) + p.shape[1:], bf16) for p in pairs]
    out_shape = ([pltpu.SemaphoreType.DMA((3 * n,))] * 2 + [pltpu.HBM(p.shape, bf16) for p in pairs]
                 + [pltpu.HBM(q.shape, bf16) for q in lands] + [_sds((8, LANES), f32)])
    outs = pl.pallas_call(
        body, name=name, out_shape=out_shape, in_specs=[HBM] * (2 * n), out_specs=[SEM, SEM] + [HBM] * (2 * n) + [VMEM],
        input_output_aliases={t: 2 + t for t in range(2 * n)},
        compiler_params=pltpu.CompilerParams(has_side_effects=EFFECT))(*[_in_hbm(a) for a in list(pairs) + lands])
    return (outs[0], outs[1]), list(outs[2:2 + n]), list(outs[2 + n:2 + 2 * n]), outs[-1]


def _rs_chip_wait(sems, pairs, lands, after, name):
    n = len(pairs)

    def body(*refs):
        p, q = refs[:n], refs[n:2 * n]
        ssem, rsem = refs[2 * n], refs[2 * n + 1]
        x, y, c = _place()
        for t in range(n):
            for j, (_, s_p) in enumerate(_other_chips(x, y)):
                cp = pltpu.make_async_remote_copy(src_ref=p[t].at[s_p], dst_ref=q[t].at[j], send_sem=ssem.at[3 * t + j],
                                                  recv_sem=rsem.at[3 * t + j], device_id=(x, y, c), device_id_type=MESH)
                cp.wait_send()
                cp.wait_recv()

    outs = pl.pallas_call(
        body, name=name, out_shape=[pltpu.HBM(a.shape, bf16) for a in list(pairs) + list(lands)],
        in_specs=[HBM] * (2 * n) + [SEM, SEM, ANY], out_specs=[HBM] * (2 * n),
        input_output_aliases={t: t for t in range(2 * n)},
        compiler_params=pltpu.CompilerParams(has_side_effects=EFFECT))(*pairs, *lands, sems[0], sems[1], after)
    return list(outs[:n]), list(outs[n:])


def _rs_half_exchange(halves, name):
    n = len(halves)

    def body(*refs):
        o = refs[n:2 * n]
        ssem, rsem = refs[2 * n:]
        x, y, c = _place()

        def copy(t, half, dev):
            return pltpu.make_async_remote_copy(src_ref=o[t].at[c], dst_ref=o[t].at[half], send_sem=ssem.at[t],
                                                recv_sem=rsem.at[t], device_id=dev, device_id_type=MESH)

        sends = [copy(t, c, (x, y, 1 - c)) for t in range(n)]
        for cp in sends:
            cp.start()
        for t in range(n):
            copy(t, 1 - c, (x, y, c)).wait_recv()
        for cp in sends:
            cp.wait_send()

    return _pcall(body, name=name, out_shape=[_sds(h.shape, h.dtype) for h in halves], in_specs=[ANY] * n,
                  out_specs=[ANY] * n, aliases={t: t for t in range(n)},
                  scratch=[pltpu.SemaphoreType.DMA((n,)), pltpu.SemaphoreType.DMA((n,))])(*halves)


def _row_spec(tm, cols):
    return pl.BlockSpec((tm, cols), lambda i: (i, 0))


def _vec_spec(cols, rows=1):
    return pl.BlockSpec((rows, cols), lambda i: (0, 0))


def _modulated_norm(xv, g, shift, scale):
    r = lax.rsqrt(jnp.mean(xv * xv, axis=-1, keepdims=True) + EPS)
    return (((xv * r) * g) * (1.0 + scale) + shift).astype(bf16)


def _hnorm(x, g, shift, scale):
    T, tm = x.shape[0], 256

    def body(x_ref, g_ref, sh_ref, sc_ref, h_ref):
        h_ref[...] = _modulated_norm(x_ref[...], g_ref[...], sh_ref[...], sc_ref[...])

    return _pcall(body, name="hnorm", out_shape=_sds((T, D), bf16), grid=(T // tm,),
                  in_specs=[_row_spec(tm, D), _vec_spec(D), _vec_spec(D), _vec_spec(D)],
                  out_specs=_row_spec(tm, D))(x, g, shift, scale)


def _out_proj(y2, wo, x, gate, nxt=None):
    T, tm = x.shape[0], 512

    def body(y_ref, w_ref, x_ref, g_ref, *rest):
        o = jnp.dot(y_ref[0], w_ref[0], preferred_element_type=f32)
        o = o + jnp.dot(y_ref[1], w_ref[1], preferred_element_type=f32)
        xo = x_ref[...] + g_ref[...] * o
        if nxt is None:
            xo_ref, o_ref = rest
        else:
            ng_ref, nsh_ref, nsc_ref, xo_ref, o_ref, h_ref = rest
            h_ref[...] = _modulated_norm(xo, ng_ref[...], nsh_ref[...], nsc_ref[...])
        o_ref[...] = o.astype(bf16)
        xo_ref[...] = xo

    extra = [] if nxt is None else list(nxt)
    n_out = 2 if nxt is None else 3
    return _pcall(body, name="out_proj", out_shape=[_sds((T, D), f32), _sds((T, D), bf16), _sds((T, D), bf16)][:n_out],
                  grid=(T // tm,),
                  in_specs=[pl.BlockSpec((2, tm, D), lambda i: (0, i, 0)), pl.BlockSpec((2, D, D), lambda i: (0, 0, 0)),
                            _row_spec(tm, D), _vec_spec(D)] + [_vec_spec(D)] * len(extra),
                  out_specs=[_row_spec(tm, D)] * n_out, vmem_mb=40)(y2, wo, x, gate, *extra)


def _gate_bwd_tile(dx, o_ref, gate_ref, dob_ref, dgate_ref):
    dob_ref[...] = (dx * gate_ref[...]).astype(bf16)
    dgate_ref[...] += jnp.sum(dx * o_ref[...].astype(f32), axis=0, keepdims=True)


def _loss_bwd(x, target, g, o, gate):
    T, tm = x.shape[0], 512

    def body(x_ref, t_ref, g_ref, o_ref, gate_ref, dx_ref, loss_ref, dg_ref, dob_ref, dgate_ref):
        @pl.when(pl.program_id(0) == 0)
        def _():
            loss_ref[...] = jnp.zeros_like(loss_ref)
            dg_ref[...] = jnp.zeros_like(dg_ref)
            dgate_ref[...] = jnp.zeros_like(dgate_ref)

        xv, gv = x_ref[...], g_ref[...]
        r = lax.rsqrt(jnp.mean(xv * xv, axis=-1, keepdims=True) + EPS)
        xn = xv * r
        err = xn * gv - t_ref[...]
        dy = err * (1.0 / D)
        dxn = dy * gv
        dx = r * (dxn - xn * jnp.mean(dxn * xn, axis=-1, keepdims=True))
        dx_ref[...] = dx
        dg_ref[...] += jnp.sum(dy * xn, axis=0, keepdims=True)
        loss_ref[...] += (0.5 / D) * jnp.sum(jnp.sum(err * err, axis=1, keepdims=True), axis=0, keepdims=True)
        _gate_bwd_tile(dx, o_ref, gate_ref, dob_ref, dgate_ref)

    return _pcall(body, name="loss_bwd",
                  out_shape=[_sds((T, D), f32), _sds((1, 1), f32), _sds((1, D), f32), _sds((T, D), bf16), _sds((1, D), f32)],
                  grid=(T // tm,),
                  in_specs=[_row_spec(tm, D), _row_spec(tm, D), _vec_spec(D), _row_spec(tm, D), _vec_spec(D)],
                  out_specs=[_row_spec(tm, D), pl.BlockSpec((1, 1), lambda i: (0, 0)), _vec_spec(D), _row_spec(tm, D),
                             _vec_spec(D)])(x, target, g, o, gate)


def _norm_bwd(x, dh, gin, g, scale, below=None):
    T, tm = x.shape[0], 512

    def body(x_ref, dh_ref, gin_ref, g_ref, sc_ref, *rest):
        if below is None:
            dx_ref, st_ref = rest
        else:
            o_ref, gate_ref, dx_ref, st_ref, dob_ref, dgate_ref = rest

        @pl.when(pl.program_id(0) == 0)
        def _():
            st_ref[...] = jnp.zeros_like(st_ref)
            if below is not None:
                dgate_ref[...] = jnp.zeros_like(dgate_ref)

        xv, gv, dhv = x_ref[...], g_ref[...], dh_ref[...]
        r = lax.rsqrt(jnp.mean(xv * xv, axis=-1, keepdims=True) + EPS)
        xn = xv * r
        da = dhv * (1.0 + sc_ref[...])
        dxn = da * gv
        dx = gin_ref[...] + r * (dxn - xn * jnp.mean(dxn * xn, axis=-1, keepdims=True))
        dx_ref[...] = dx
        st_ref[0:1, :] += jnp.sum(dhv, axis=0, keepdims=True)
        st_ref[1:2, :] += jnp.sum(dhv * (xn * gv), axis=0, keepdims=True)
        st_ref[2:3, :] += jnp.sum(da * xn, axis=0, keepdims=True)
        if below is not None:
            _gate_bwd_tile(dx, o_ref, gate_ref, dob_ref, dgate_ref)

    out_shape = [_sds((T, D), f32), _sds((8, D), f32)]
    in_specs = [_row_spec(tm, D), _row_spec(tm, D), _row_spec(tm, D), _vec_spec(D), _vec_spec(D)]
    out_specs = [_row_spec(tm, D), _vec_spec(D, 8)]
    args = [x, dh, gin, g, scale]
    if below is not None:
        out_shape += [_sds((T, D), bf16), _sds((1, D), f32)]
        in_specs += [_row_spec(tm, D), _vec_spec(D)]
        out_specs += [_row_spec(tm, D), _vec_spec(D)]
        args += list(below)
    return _pcall(body, name="norm_bwd", out_shape=out_shape, grid=(T // tm,), in_specs=in_specs,
                  out_specs=out_specs)(*args)


STEPS = 4
ADAMW_STEPS = 8


def _cast_place(place, ws, layer, after=None):
    n = len(ws)

    def body(place_ref, *refs):
        for t in range(n):
            refs[-n + t][...] = refs[t][...].astype(bf16)

    def tile(w):
        return w.shape[1] // STEPS, w.shape[2]

    extra = [] if after is None else [after]
    return _pcall(body, name="cast_place", out_shape=[_sds((4,) + w.shape[1:], bf16) for w in ws], grid=(STEPS,),
                  prefetch=1,
                  in_specs=[pl.BlockSpec((None,) + tile(w), lambda i, pr: (layer, i, 0)) for w in ws] + [ANY] * len(extra),
                  out_specs=[pl.BlockSpec((None,) + tile(w), lambda i, pr: (pr[0], i, 0)) for w in ws])(
                      place, *ws, *extra)


def _rs_add(place, grads, theirs):
    n = len(grads)

    def body(place_ref, *refs):
        for t in range(n):
            refs[2 * n + t][...] = (refs[t][...].astype(f32) + refs[n + t][...].astype(f32)).astype(bf16)

    def tile(q):
        return q.shape[1] // 2, q.shape[2]

    mine = [pl.BlockSpec((None, None) + tile(q), lambda s, i, pr: (s, pr[1], i, 0)) for q in theirs]
    shard = [pl.BlockSpec((None,) + tile(q), lambda s, i, pr: (s, i, 0)) for q in theirs]
    return _pcall(body, name="rs_add", out_shape=[_sds(q.shape, bf16) for q in theirs], grid=(4, 2), prefetch=1,
                  in_specs=mine + shard, out_specs=shard)(place, *grads, *theirs)


def _rs_sum(place, pairs, slots):
    n, steps = len(pairs), 4

    def body(place_ref, *refs):
        for t in range(n):
            p_ref, q_ref = refs[t], refs[n + t]
            total = ((p_ref[...].astype(f32) + q_ref[0].astype(f32)) + q_ref[1].astype(f32)) + q_ref[2].astype(f32)
            refs[2 * n + t][...] = total.astype(bf16)

    def tile(q):
        return q.shape[1] // steps, q.shape[2]

    return _pcall(body, name="rs_sum", out_shape=[_sds((2,) + q.shape[1:], bf16) for q in slots], grid=(steps,),
                  prefetch=1,
                  in_specs=[pl.BlockSpec((None,) + tile(q), lambda i, pr: (pr[0], i, 0)) for q in slots]
                  + [pl.BlockSpec((3,) + tile(q), lambda i, pr: (0, i, 0)) for q in slots],
                  out_specs=[pl.BlockSpec((None,) + tile(q), lambda i, pr: (pr[1], i, 0)) for q in slots])(
                      place, *pairs, *slots)


def _adamw_math(w, g, m, v):
    m = ADAM_B1 * m + (1.0 - ADAM_B1) * g
    v = ADAM_B2 * v + (1.0 - ADAM_B2) * jnp.square(g)
    m_hat = m / (1.0 - ADAM_B1 ** ADAM_STEP)
    v_hat = v / (1.0 - ADAM_B2 ** ADAM_STEP)
    delta = -ADAM_LR * (m_hat / (jnp.sqrt(v_hat) + ADAM_EPS) + ADAM_WD * w)
    return delta, m, v


def _adamw_layer(layer, items):
    n = len(items)

    def body(*refs):
        outs = refs[-4 * n:]
        for t in range(n):
            w_ref, g_ref, m_ref, v_ref = refs[4 * t:4 * t + 4]
            g = g_ref[...].astype(f32)
            outs[4 * t][...] = g
            outs[4 * t + 1][...], outs[4 * t + 2][...], outs[4 * t + 3][...] = _adamw_math(
                w_ref[...], g, m_ref[...], v_ref[...])

    args, in_specs, out_specs, out_shape = [], [], [], []
    for w, g, m, v, _ in items:
        tr, cols = w.shape[1] // ADAMW_STEPS, w.shape[2]
        spec = pl.BlockSpec((None, tr, cols), lambda i: (layer, i, 0))
        args += [w, g, m, v]
        in_specs += [spec, pl.BlockSpec((tr, cols), lambda i: (i, 0)), spec, spec]
        out_specs += [spec] * 4
        out_shape += [_sds(w.shape, f32)] * 4
    aliases = {}
    for t, it in enumerate(items):
        if it[4] is not None:
            for k in range(4):
                aliases[len(args)] = 4 * t + k
                args.append(it[4][k])
                in_specs.append(ANY)
    res = _pcall(body, name="adamw", out_shape=out_shape, grid=(ADAMW_STEPS,), in_specs=in_specs, out_specs=out_specs,
                 aliases=aliases)(*args)
    return [tuple(res[4 * t:4 * t + 4]) for t in range(n)]


def _adamw_small(items):
    n = len(items)

    def body(*refs):
        ins, outs = refs[:4 * n], refs[4 * n:]
        for t in range(n):
            w_ref, g_ref, m_ref, v_ref = ins[4 * t:4 * t + 4]
            if len(g_ref.shape) == len(w_ref.shape) + 1:
                g = g_ref[0]
                for b in range(1, g_ref.shape[0]):
                    g = g + g_ref[b]
            else:
                g = g_ref[...]
            d, m, v = _adamw_math(w_ref[...], g, m_ref[...], v_ref[...])
            outs[4 * t][...], outs[4 * t + 1][...], outs[4 * t + 2][...], outs[4 * t + 3][...] = g, d, m, v

    out_shape = [_sds(w.shape, f32) for (w, _, _, _) in items for _ in range(4)]
    flat = [a for it in items for a in it]
    res = _pcall(body, name="adamw_small", out_shape=out_shape, in_specs=[VMEM] * (4 * n),
                 out_specs=[VMEM] * (4 * n))(*flat)
    return [tuple(res[4 * t:4 * t + 4]) for t in range(n)]


NN = ((1,), (0,))
NT = ((1,), (1,))
TN = ((0,), (0,))


def _mm(name, a, b, *, grid, a_spec, b_spec, out_shape, out_spec, dims, vmem_mb=None):
    def body(a_ref, b_ref, o_ref):
        r = lax.dot_general(a_ref[...], b_ref[...], (dims, ((), ())), preferred_element_type=f32)
        o_ref[...] = r.astype(o_ref.dtype)

    return _pcall(body, name=name, out_shape=out_shape, grid=grid, in_specs=[a_spec, b_spec], out_specs=out_spec,
                  vmem_mb=vmem_mb)(a, b)


def _whole(shape):
    return pl.BlockSpec(shape, lambda j: (0,) * len(shape))


def _split_spec(rows, tile, per_split):
    return pl.BlockSpec((None, rows, tile), lambda j: (j // per_split, 0, j % per_split))


class _Proj:
    def __init__(self, n, splits, tile):
        self.n, self.splits, self.tile = n, splits, tile
        self.steps = n // tile
        self.w_per = n // 4 // tile
        self.a_per = n // splits // tile
        assert self.w_per * tile * 4 == n and self.a_per * tile * splits == n

    def fwd(self, hb, wg):
        T = hb.shape[0]
        sub, tile, w_per = FWD_TILES, self.tile, self.w_per
        wide = sub * tile
        a_per = self.n // self.splits // wide
        assert a_per * wide * self.splits == self.n

        def w_tile(q):
            return pl.BlockSpec((None, D, tile), lambda j: ((sub * j + q) // w_per, 0, (sub * j + q) % w_per))

        def body(a_ref, *rest):
            w = jnp.concatenate([rest[q][...] for q in range(sub)], axis=1)
            rest[sub][...] = jnp.dot(a_ref[...], w, preferred_element_type=f32).astype(bf16)

        return _pcall(body, name="proj_fwd", out_shape=_sds((self.splits, T, self.n // self.splits), bf16),
                      grid=(self.n // wide,), in_specs=[_whole((T, D))] + [w_tile(q) for q in range(sub)],
                      out_specs=pl.BlockSpec((None, T, wide), lambda j: (j // a_per, 0, j % a_per)),
                      vmem_mb=40 if wide > 512 else None)(hb, *([wg] * sub))

    def dw(self, hb, dp):
        T = hb.shape[0]
        return _mm("proj_dw", hb, dp, grid=(self.steps,), a_spec=_whole((T, D)),
                   b_spec=_split_spec(T, self.tile, self.a_per), out_shape=_sds((4, D, self.n // 4), bf16),
                   out_spec=_split_spec(D, self.tile, self.w_per), dims=TN)

    def dh(self, dp, wg):
        T = dp.shape[1]
        sub, tile, w_per = DH_WIDE // self.tile, self.tile, self.w_per
        a_per = self.n // self.splits // DH_WIDE
        assert sub * tile == DH_WIDE and a_per * DH_WIDE * self.splits == self.n

        def w_tile(q):
            return pl.BlockSpec((None, D, tile), lambda k: ((sub * k + q) // w_per, 0, (sub * k + q) % w_per))

        def body(a_ref, *rest):
            o_ref = rest[sub]
            w = jnp.concatenate([rest[q][...] for q in range(sub)], axis=1)
            r = lax.dot_general(a_ref[...], w, (NT, ((), ())), preferred_element_type=f32)

            @pl.when(pl.program_id(0) == 0)
            def _():
                o_ref[...] = r

            @pl.when(pl.program_id(0) > 0)
            def _():
                o_ref[...] += r

        return _pcall(body, name="proj_dh", out_shape=_sds((T, D), f32), grid=(self.n // DH_WIDE,),
                      in_specs=[pl.BlockSpec((None, T, DH_WIDE), lambda k: (k // a_per, 0, k % a_per))]
                      + [w_tile(q) for q in range(sub)],
                      out_specs=_whole((T, D)), vmem_mb=40)(dp, *([wg] * sub))


EVEN_PROJ = _Proj(7 * D, 7, 256)
ODD_PROJ = _Proj(4 * D, 2, 512)


def _dy_mm(dob, wo):
    T = dob.shape[0]
    return _mm("out_dy", dob, wo, grid=(4,), a_spec=_whole((T, D)),
               b_spec=pl.BlockSpec((None, 512, D), lambda j: (j, 0, 0)), out_shape=_sds((2, T, D), bf16),
               out_spec=_split_spec(T, 512, 2), dims=NT)


def _dwo_mm(y2, dob):
    T = dob.shape[0]
    return _mm("out_dw", y2, dob, grid=(4,), a_spec=_split_spec(T, 512, 2), b_spec=_whole((T, D)),
               out_shape=_sds((4, 512, D), bf16), out_spec=pl.BlockSpec((None, 512, D), lambda j: (j, 0, 0)), dims=TN)


def _head_spec(lead, T):
    return pl.BlockSpec((lead, T, HEAD), lambda h: (0, 0, h))


def _head_vec(rows):
    return pl.BlockSpec((rows, HEAD), lambda h: (0, h))


_HEAD_MAT = pl.BlockSpec((None, HEAD, HEAD), lambda h: (h, 0, 0))


def _causal():
    return lax.broadcasted_iota(jnp.int32, (HEAD, HEAD), 0) >= lax.broadcasted_iota(jnp.int32, (HEAD, HEAD), 1)


def _layernorm_head(v):
    mu = jnp.mean(v, axis=-1, keepdims=True)
    d = v - mu
    rstd = lax.rsqrt(jnp.mean(d * d, axis=-1, keepdims=True) + EPS)
    return d * rstd, rstd


def _even_fwd(p7, conv_w, ln_g, ln_b, sgu_w, sgu_bias):
    T, C = p7.shape[1], CHUNK_ROWS

    def body(p_ref, cw_ref, lg_ref, lb_ref, w_ref, b_ref, y_ref):
        w0, w1, w2 = cw_ref[0:1, :], cw_ref[1:2, :], cw_ref[2:3, :]
        wm = jnp.where(_causal(), w_ref[...], 0.0).astype(bf16)
        bias, lg, lb = b_ref[...], lg_ref[...], lb_ref[...]

        def step(i, halo):
            rows = pl.ds(pl.multiple_of(i * C, C), C)
            ah, ab, ac, az, u, v, zb = (p_ref[k, rows, :].astype(f32) for k in range(7))
            tt = ac * ah
            ext = jnp.concatenate([halo, tt], axis=0)
            cv = w2 * tt + w1 * pltpu.roll(ext, 1, 0)[HALO_CONV:] + w0 * pltpu.roll(ext, 2, 0)[HALO_CONV:]
            y_ref[0, rows, :] = (ab * cv * _silu(az)).astype(bf16)
            vhat, _ = _layernorm_head(v)
            vn = (vhat * lg + lb).astype(bf16)
            mix = jnp.concatenate([jnp.dot(wm, vn[k * HEAD:(k + 1) * HEAD], preferred_element_type=f32) + bias
                                   for k in range(C // HEAD)], axis=0)
            y_ref[1, rows, :] = (u * mix * _silu(zb)).astype(bf16)
            return tt[C - HALO_CONV:]

        lax.fori_loop(0, T // C, step, jnp.zeros((HALO_CONV, HEAD), f32))

    return _pcall(body, name="even_fwd", out_shape=_sds((2, T, D), bf16), grid=(NH,),
                  in_specs=[_head_spec(7, T), _head_vec(3), _head_vec(1), _head_vec(1), _HEAD_MAT, _HEAD_MAT],
                  out_specs=_head_spec(2, T))(p7, conv_w, ln_g, ln_b, sgu_w, sgu_bias)


def _even_bwd(p7, dy2, conv_w, ln_g, ln_b, sgu_w, sgu_bias):
    T, C = p7.shape[1], CHUNK_ROWS
    n_chunks = T // C

    def body(p_ref, dy_ref, cw_ref, lg_ref, lb_ref, w_ref, b_ref,
             dp_ref, dcw_ref, dlg_ref, dlb_ref, dw_ref, dms_ref, dcv_s):
        w0, w1, w2 = cw_ref[0:1, :], cw_ref[1:2, :], cw_ref[2:3, :]
        tri = _causal()
        wm = jnp.where(tri, w_ref[...], 0.0).astype(bf16)
        bias, lg, lb = b_ref[...], lg_ref[...], lb_ref[...]
        dw_ref[...] = jnp.zeros_like(dw_ref)
        dms_ref[...] = jnp.zeros_like(dms_ref)

        def fwd_step(i, carry):
            halo, a0, a1, a2, alg, alb = carry
            rows = pl.ds(pl.multiple_of(i * C, C), C)
            ah, ab, ac, az = (p_ref[k, rows, :].astype(f32) for k in range(4))
            dya = dy_ref[0, rows, :].astype(f32)
            tt = ac * ah
            ext = jnp.concatenate([halo, tt], axis=0)
            t1, t2 = pltpu.roll(ext, 1, 0)[HALO_CONV:], pltpu.roll(ext, 2, 0)[HALO_CONV:]
            cv = w2 * tt + w1 * t1 + w0 * t2
            sa, dsa = _silu_and_grad(az)
            g1 = dya * sa
            dp_ref[1, rows, :] = (g1 * cv).astype(bf16)
            dp_ref[3, rows, :] = (dya * ab * cv * dsa).astype(bf16)
            dcv = g1 * ab
            dcv_s[rows, :] = dcv
            a2 = a2 + jnp.sum(dcv * tt, axis=0, keepdims=True)
            a1 = a1 + jnp.sum(dcv * t1, axis=0, keepdims=True)
            a0 = a0 + jnp.sum(dcv * t2, axis=0, keepdims=True)

            u, zb, dyb = p_ref[4, rows, :].astype(f32), p_ref[6, rows, :].astype(f32), dy_ref[1, rows, :].astype(f32)
            vhat, rstd = _layernorm_head(p_ref[5, rows, :].astype(f32))
            vn = (vhat * lg + lb).astype(bf16)
            sb, dsb = _silu_and_grad(zb)
            mix = jnp.concatenate([jnp.dot(wm, vn[k * HEAD:(k + 1) * HEAD], preferred_element_type=f32) + bias
                                   for k in range(C // HEAD)], axis=0)
            dp_ref[4, rows, :] = (dyb * mix * sb).astype(bf16)
            dp_ref[6, rows, :] = (dyb * u * mix * dsb).astype(bf16)
            dmix = dyb * u * sb
            dvn_parts = []
            for k in range(C // HEAD):
                dm = dmix[k * HEAD:(k + 1) * HEAD]
                dmb = dm.astype(bf16)
                dvn_parts.append(lax.dot_general(wm, dmb, (TN, ((), ())), preferred_element_type=f32))
                dw_ref[...] += lax.dot_general(dmb, vn[k * HEAD:(k + 1) * HEAD], (NT, ((), ())),
                                               preferred_element_type=f32)
                dms_ref[...] += dm
            dvn = jnp.concatenate(dvn_parts, axis=0)
            alg = alg + jnp.sum(dvn * vhat, axis=0, keepdims=True)
            alb = alb + jnp.sum(dvn, axis=0, keepdims=True)
            dvh = dvn * lg
            dv = rstd * (dvh - jnp.mean(dvh, axis=-1, keepdims=True)
                         - vhat * jnp.mean(dvh * vhat, axis=-1, keepdims=True))
            dp_ref[5, rows, :] = dv.astype(bf16)
            return tt[C - HALO_CONV:], a0, a1, a2, alg, alb

        zrow = jnp.zeros((1, HEAD), f32)
        _, a0, a1, a2, alg, alb = lax.fori_loop(
            0, n_chunks, fwd_step, (jnp.zeros((HALO_CONV, HEAD), f32), zrow, zrow, zrow, zrow, zrow))
        dcw_ref[0:1, :], dcw_ref[1:2, :], dcw_ref[2:3, :] = a0, a1, a2
        dlg_ref[...], dlb_ref[...] = alg, alb
        dw_ref[...] = jnp.where(tri, dw_ref[...], 0.0)

        def bwd_step(k, halo):
            rows = pl.ds(pl.multiple_of((n_chunks - 1 - k) * C, C), C)
            dcv = dcv_s[rows, :]
            ext = jnp.concatenate([dcv, halo], axis=0)
            n1 = pltpu.roll(ext, C + HALO_CONV - 1, 0)[:C]
            n2 = pltpu.roll(ext, C + HALO_CONV - 2, 0)[:C]
            dtt = w2 * dcv + w1 * n1 + w0 * n2
            dp_ref[2, rows, :] = (dtt * p_ref[0, rows, :].astype(f32)).astype(bf16)
            dp_ref[0, rows, :] = (dtt * p_ref[2, rows, :].astype(f32)).astype(bf16)
            return dcv[:HALO_CONV]

        lax.fori_loop(0, n_chunks, bwd_step, jnp.zeros((HALO_CONV, HEAD), f32))

    out_shape = [_sds((7, T, D), bf16), _sds((3, D), f32), _sds((1, D), f32), _sds((1, D), f32),
                 _sds((NH, HEAD, HEAD), f32), _sds((NH, HEAD, HEAD), f32)]
    return _pcall(body, name="even_bwd", out_shape=out_shape, grid=(NH,),
                  in_specs=[_head_spec(7, T), _head_spec(2, T), _head_vec(3), _head_vec(1), _head_vec(1),
                            _HEAD_MAT, _HEAD_MAT],
                  out_specs=[_head_spec(7, T), _head_vec(3), _head_vec(1), _head_vec(1), _HEAD_MAT, _HEAD_MAT],
                  scratch=[pltpu.VMEM((T, HEAD), f32)])(p7, dy2, conv_w, ln_g, ln_b, sgu_w, sgu_bias)


def _window_sum(ext, win, towards_past):
    n, k, s = ext.shape[0], 1, ext
    while k < win:
        s = s + pltpu.roll(s, k if towards_past else n - k, 0)
        k *= 2
    return s


def _pool_count(i, C, win):
    t = i * C + lax.broadcasted_iota(jnp.int32, (C, 1), 0)
    cnt = jnp.minimum(t + 1, win).astype(f32)
    return cnt, 1.0 / cnt


def _group_specs(T):
    p_spec = pl.BlockSpec((None, T, GC), lambda g: (0, 0, g))
    z_spec = pl.BlockSpec((None, T, GC), lambda g: (1, 0, g))
    pw_spec = pl.BlockSpec((4, GC // 4, GC), lambda g: (0, g, 0))
    ps_spec = pl.BlockSpec((1, GC), lambda g: (0, g))
    y_spec = pl.BlockSpec((None, T, GC), lambda g: (g // 2, 0, g % 2))
    return p_spec, z_spec, pw_spec, ps_spec, y_spec


def _odd_fwd(p2, pool_wg, pool_scale):
    T, C = p2.shape[1], CHUNK_ROWS
    p_spec, z_spec, pw_spec, ps_spec, y_spec = _group_specs(T)

    def body(p_ref, z_ref, pw_ref, ps_ref, y_ref):
        pw, ps = pw_ref[...].reshape(GC, GC), ps_ref[...]

        def run(win):
            def step(i, halo):
                rows = pl.ds(pl.multiple_of(i * C, C), C)
                p = p_ref[rows, :].astype(f32)
                s = _window_sum(jnp.concatenate([halo, p], axis=0), win, True)[HALO_POOL:]
                pooled = s * _pool_count(i, C, win)[1] - p
                ypre = jnp.dot(pooled.astype(bf16), pw, preferred_element_type=f32)
                y_ref[rows, :] = (ypre * ps * _silu(z_ref[rows, :].astype(f32))).astype(bf16)
                return p[C - HALO_POOL:]

            lax.fori_loop(0, T // C, step, jnp.zeros((HALO_POOL, GC), f32))

        for gi, win in enumerate(WINDOWS):
            pl.when(pl.program_id(0) == gi)(functools.partial(run, win))

    return _pcall(body, name="odd_fwd", out_shape=_sds((2, T, D), bf16), grid=(len(WINDOWS),),
                  in_specs=[p_spec, z_spec, pw_spec, ps_spec], out_specs=y_spec)(p2, p2, pool_wg, pool_scale)


def _odd_bwd(p2, dy2, pool_wg, pool_scale):
    T, C = p2.shape[1], CHUNK_ROWS
    n_chunks = T // C
    p_spec, z_spec, pw_spec, ps_spec, y_spec = _group_specs(T)

    def body(p_ref, z_ref, dy_ref, pw_ref, ps_ref, dp_ref, dpw_ref, dps_ref, q_s, acc_s):
        pw, ps = pw_ref[...].reshape(GC, GC), ps_ref[...]

        def run(win):
            acc_s[...] = jnp.zeros_like(acc_s)

            def fwd_step(i, carry):
                halo, aps = carry
                rows = pl.ds(pl.multiple_of(i * C, C), C)
                p, z, dy = p_ref[rows, :].astype(f32), z_ref[rows, :].astype(f32), dy_ref[rows, :].astype(f32)
                _, inv_cnt = _pool_count(i, C, win)
                s = _window_sum(jnp.concatenate([halo, p], axis=0), win, True)[HALO_POOL:]
                pb = (s * inv_cnt - p).astype(bf16)
                ypre = jnp.dot(pb, pw, preferred_element_type=f32)
                sz, dsz = _silu_and_grad(z)
                aps = aps + jnp.sum(dy * ypre * sz, axis=0, keepdims=True)
                dp_ref[1, rows, :] = (dy * ypre * ps * dsz).astype(bf16)
                dyp = (dy * ps * sz).astype(bf16)
                acc_s[...] += lax.dot_general(pb, dyp, (TN, ((), ())), preferred_element_type=f32)
                dpool = lax.dot_general(dyp, pw, (NT, ((), ())), preferred_element_type=f32)
                q_s[rows, :] = dpool * inv_cnt
                return p[C - HALO_POOL:], aps

            _, aps = lax.fori_loop(0, n_chunks, fwd_step, (jnp.zeros((HALO_POOL, GC), f32), jnp.zeros((1, GC), f32)))
            dps_ref[...] = aps
            dpw_ref[...] = acc_s[...].reshape(4, GC // 4, GC).astype(bf16)

            def bwd_step(k, halo):
                i = n_chunks - 1 - k
                rows = pl.ds(pl.multiple_of(i * C, C), C)
                q = q_s[rows, :]
                s = _window_sum(jnp.concatenate([q, halo], axis=0), win, False)[:C]
                dp_ref[0, rows, :] = (s - q * _pool_count(i, C, win)[0]).astype(bf16)
                return q[:HALO_POOL]

            lax.fori_loop(0, n_chunks, bwd_step, jnp.zeros((HALO_POOL, GC), f32))

        for gi, win in enumerate(WINDOWS):
            pl.when(pl.program_id(0) == gi)(functools.partial(run, win))

    out_shape = [_sds((2, T, 2 * D), bf16), _sds((4, GC, GC), bf16), _sds((1, 2 * D), f32)]
    return _pcall(body, name="odd_bwd", out_shape=out_shape, grid=(len(WINDOWS),),
                  in_specs=[p_spec, z_spec, y_spec, pw_spec, ps_spec],
                  out_specs=[pl.BlockSpec((2, T, GC), lambda g: (0, 0, g)), pw_spec, ps_spec],
                  scratch=[pltpu.VMEM((T, GC), f32), pltpu.VMEM((GC, GC), f32)], vmem_mb=44)(
                      p2, p2, dy2, pool_wg, pool_scale)


def _ada_fwd(c_all, ada_w):
    cols = ada_w.shape[2]

    def body(c_ref, w_ref, o_ref):
        o_ref[...] = jnp.dot(_silu(c_ref[...]), w_ref[...], preferred_element_type=f32,
                             precision=lax.Precision.HIGHEST)

    return _pcall(body, name="ada_fwd", out_shape=_sds((4, N_DEV, cols), f32), grid=(4,),
                  in_specs=[pl.BlockSpec((N_DEV, D), lambda i: (0, 0)), pl.BlockSpec((None, D, cols), lambda i: (i, 0, 0))],
                  out_specs=pl.BlockSpec((None, N_DEV, cols), lambda i: (i, 0, 0)))(c_all, ada_w)


def _ada_bwd(c_all_t, dmod, w, m, v):
    cols, tr = w.shape[2], 256
    spec = pl.BlockSpec((None, tr, cols), lambda l, i: (l, i, 0))

    def body(c_ref, dm_ref, w_ref, m_ref, v_ref, g_ref, d_ref, mo_ref, vo_ref):
        sc = _silu(c_ref[...])
        g = sc[:, 0:1] * dm_ref[0:1, :]
        for b in range(1, N_DEV):
            g = g + sc[:, b:b + 1] * dm_ref[b:b + 1, :]
        g_ref[...] = g
        d_ref[...], mo_ref[...], vo_ref[...] = _adamw_math(w_ref[...], g, m_ref[...], v_ref[...])

    return _pcall(body, name="ada_bwd", out_shape=[_sds(w.shape, f32)] * 4, grid=(4, D // tr),
                  in_specs=[pl.BlockSpec((tr, N_DEV), lambda l, i: (i, 0)),
                            pl.BlockSpec((None, N_DEV, cols), lambda l, i: (l, 0, 0)), spec, spec, spec],
                  out_specs=[spec] * 4)(c_all_t, dmod, w, m, v)


def _layer_fwd(even, x, hb, gate, w, nxt, before_out=None):
    if even:
        w_in, w_out, conv_w, ln_g, ln_b, sgu_w, sgu_b = w
        bias = jnp.broadcast_to(sgu_b[:, :, None], (NH, HEAD, HEAD))
        p = EVEN_PROJ.fwd(hb, w_in)
        y2 = _even_fwd(p, conv_w, ln_g, ln_b, sgu_w, bias)
    else:
        w_in, pool_w, w_out, pool_scale = w
        p = ODD_PROJ.fwd(hb, w_in)
        y2 = _odd_fwd(p, pool_w, pool_scale)
    if before_out is not None:
        late_w_out, tok = before_out(y2)
        if late_w_out is not None:
            w_out = late_w_out
            w = (w_in, w_out) + tuple(w[2:]) if even else (w_in, pool_w, w_out, pool_scale)
        if tok is not None:
            gate = gate + tok[0:1, 0:1]
    outs = _out_proj(y2, w_out.reshape(2, D, D), x, gate, nxt)
    return outs[0], (None if nxt is None else outs[2]), (x, hb, p, y2, outs[1]), w


def _layer_bwd(even, gin, dob, dgate, saved, scale, g, w, below=None, send=None):
    x_in, hb, p, y2, o = saved
    if even:
        w_in, w_out, conv_w, ln_g, ln_b, sgu_w, sgu_b = w
        bias = jnp.broadcast_to(sgu_b[:, :, None], (NH, HEAD, HEAD))
        dy2 = _dy_mm(dob, w_out)
        dp, dconv, dlg, dlb, dsw, dms = _even_bwd(p, dy2, conv_w, ln_g, ln_b, sgu_w, bias)
        proj = EVEN_PROJ
        small = dict(conv_w=dconv, ln_g=dlg, ln_b=dlb, sgu_w=dsw, sgu_b=jnp.sum(dms, axis=-1))
        big = [proj.dw(hb, dp), _dwo_mm(y2, dob)]
    else:
        w_in, pool_w, w_out, pool_scale = w
        dy2 = _dy_mm(dob, w_out)
        dp, dpw, dps = _odd_bwd(p, dy2, pool_w, pool_scale)
        proj = ODD_PROJ
        small = dict(pool_scale=dps)
        big = [proj.dw(hb, dp), dpw, _dwo_mm(y2, dob)]
    if send is not None:
        big, tok = send(big)
        scale = scale + tok[0:1, 0:1]
    dh = proj.dh(dp, w_in)
    res = _norm_bwd(x_in, dh, gin, g, scale, below)
    stats = res[1]
    return (res[0], (None if below is None else (res[2], res[3])), big, small,
            jnp.concatenate([stats[0:2], dgate], axis=0), stats[2:3])


def _pack_rows(parts):
    rows = [p.reshape(-1, LANES) for p in parts]
    total = sum(r.shape[0] for r in rows)
    padded = -(-total // (8 * N_DEV)) * (8 * N_DEV)
    if padded > total:
        rows.append(jnp.zeros((padded - total, LANES), f32))
    return jnp.concatenate(rows, axis=0)


def _unpack_rows(buf, shapes):
    out, r = [], 0
    for shp in shapes:
        n = 1
        for d in shp:
            n *= d
        out.append(buf[r:r + n // LANES].reshape(shp))
        r += n // LANES
    return out


def kernel(x, c, norm_g, ada_w, ada_b, ab_w_in, ab_conv_w, ab_ln_g, ab_ln_b, ab_sgu_w, ab_sgu_b, ab_w_out, c_w_in, c_pool_w, c_pool_scale, c_w_out, final_g, loss_target, m_norm_g, m_ada_w, m_ada_b, m_ab_w_in, m_ab_conv_w, m_ab_ln_g, m_ab_ln_b, m_ab_sgu_w, m_ab_sgu_b, m_ab_w_out, m_c_w_in, m_c_pool_w, m_c_pool_scale, m_c_w_out, m_final_g, v_norm_g, v_ada_w, v_ada_b, v_ab_w_in, v_ab_conv_w, v_ab_ln_g, v_ab_ln_b, v_ab_sgu_w, v_ab_sgu_b, v_ab_w_out, v_c_w_in, v_c_pool_w, v_c_pool_scale, v_c_w_out, v_final_g):
    ix, iy, ic = _place()
    chip, dev = 2 * ix + iy, 4 * ix + 2 * iy + ic
    n_even, n_odd = ab_w_in.shape[0], c_w_in.shape[0]
    depth = n_even + n_odd
    acols = ada_w.shape[2]

    place = jnp.stack([chip, ic]).astype(jnp.int32)
    even_names, odd_names = ["ab_w_in", "ab_w_out"], ["c_w_in", "c_pool_w", "c_w_out"]
    params = {"ab_w_in": (ab_w_in, m_ab_w_in, v_ab_w_in), "ab_w_out": (ab_w_out, m_ab_w_out, v_ab_w_out),
              "c_w_in": (c_w_in, m_c_w_in, v_c_w_in), "c_w_out": (c_w_out, m_c_w_out, v_c_w_out),
              "c_pool_w": tuple(a.reshape(n_odd, GC, GC) for a in (c_pool_w, m_c_pool_w, v_c_pool_w))}

    def placed(names, layer, after=None):
        ws = [params[nm][0] for nm in names]
        return [p.reshape(4, 2, p.shape[1] // 2, p.shape[2]) for p in _cast_place(place, ws, layer, after)]

    def whole(arrays):
        return [g.reshape(4, 2 * g.shape[2], g.shape[3]) for g in arrays]

    first = _gather8(jnp.concatenate([c, ab_conv_w.reshape(1, -1), c_pool_scale.reshape(1, -1)], axis=1), "gather_c")
    c_all, small_all = first[:, 0, :D], first[0::2, 0, D:]
    sems_a, in_a, tok = _ag_start([placed(even_names[:1], 0)], first[0:1, 0, 0:LANES], "ag_start_0a")
    modp = _ada_fwd(c_all, ada_w)
    later = [placed(even_names[1:], 0, tok)]
    later += [placed(even_names if i % 2 == 0 else odd_names, i // 2, tok) for i in range(1, depth)]
    modg = _gather8(modp + tok[0:1, 0:1], "gather_mod", [lay[-1] for lay in later])
    mod_rows = lax.dynamic_index_in_dim(modg[0::2], dev, axis=2, keepdims=False)
    mod = jnp.transpose(mod_rows, (1, 0, 2)).reshape(depth, 3 * D) + ada_b
    mods = [(mod[i:i + 1, 0:D], mod[i:i + 1, D:2 * D], mod[i:i + 1, 2 * D:3 * D]) for i in range(depth)]

    def shard_cols(a, width):
        return lax.dynamic_slice_in_dim(a, chip * width, width, axis=a.ndim - 1)

    n_conv = ab_conv_w.size
    conv_all = small_all[:, :n_conv].reshape(4, n_even, 3, D // 4)
    conv_full = jnp.transpose(conv_all, (1, 2, 0, 3)).reshape(n_even, 3, D)
    scale_all = small_all[:, n_conv:].reshape(4, n_odd, 2 * D // 4)
    scale_full = jnp.transpose(scale_all, (1, 0, 2)).reshape(n_odd, 2 * D)

    gathers_done = mod[0:1, 0:LANES] + scale_full[0:1, 0:LANES]
    sems_b, in_b, tok = _ag_start(later[:1], gathers_done, "ag_start_0b")
    sems_r, in_r, tok = _ag_start(later[1:], tok, "ag_start_rest")

    x_cur, saved, weights, handoff = x[0], [], [], {}
    sems_f, in_f, tok = _agf_start(_ag_wait(in_a[0], sems_a[0], tok, "ag_wait_0a"), "agf_start_0")
    hb = _hnorm(x_cur, norm_g[0:1], mods[0][0] + tok[0:1, 0:1], mods[0][1])
    for i in range(depth):
        j = i // 2
        if i == 0:
            full = whole(_agf_wait(sems_f, in_f, hb, "agf_wait_0")) + [None]
        else:
            full = whole(_agf_wait(*handoff.pop(i), x_cur, f"agf_wait_{i}"))
        if i % 2 == 0:
            w = (full[0], full[1], conv_full[j], ab_ln_g[j:j + 1], ab_ln_b[j:j + 1], ab_sgu_w[j], ab_sgu_b[j])
        else:
            w = (full[0], full[1], full[2], scale_full[j:j + 1])

        def before_out(y2, i=i):
            w_out, tok = None, None
            if i == 0:
                w_out = whole(_ag_forward(_ag_wait(in_b[0], sems_b[0], y2, "ag_wait_0b"), "ag_forward"))[0]
            if i + 1 < depth:
                arrived = _ag_wait(in_r[i], sems_r[i], y2, f"ag_wait_{i + 1}")
                sems_f, inflight, tok = _agf_start(arrived, f"agf_start_{i + 1}")
                handoff[i + 1] = (sems_f, inflight)
            return w_out, tok

        nxt = (norm_g[i + 1:i + 2], mods[i + 1][0], mods[i + 1][1]) if i + 1 < depth else None
        x_cur, hb, sv, w = _layer_fwd(i % 2 == 0, x_cur, hb, mods[i][2], w, nxt, before_out)
        weights.append(w)
        saved.append(sv)
    gin, loss, dfinal_g, dob, dgate = _loss_bwd(x_cur, loss_target[0], final_g.reshape(1, D), saved[-1][4],
                                                mods[-1][2])

    stacked = {}

    def finish(i, sems, pairs, lands, after):
        pairs, slots = _rs_chip_wait(sems, pairs, lands, after, f"rs_chip_wait_{i}")
        names = even_names if i % 2 == 0 else odd_names
        grads = _rs_half_exchange(_rs_sum(place, pairs, slots), "rs_half_exchange")
        items = [(params[nm][0], g.reshape(params[nm][0].shape[1:]), params[nm][1], params[nm][2], stacked.get(nm))
                 for nm, g in zip(names, grads)]
        for nm, res in zip(names, _adamw_layer(i // 2, items)):
            stacked[nm] = res

    small_g, dmod, dnorm_g, pending, tok = [None] * depth, [None] * depth, [None] * depth, None, None
    for i in reversed(range(depth)):
        w = weights[i]
        if tok is not None:
            w = w[:2] + (w[2] + tok[0:1, 0:1],) + w[3:] if i % 2 == 0 else w[:3] + (w[3] + tok[0:1, 0:1],)
        below = (saved[i - 1][4], mods[i - 1][2]) if i > 0 else None

        def send(big_g, i=i):
            big_g = [g.reshape(4, 2, g.shape[1] // 2, g.shape[2]) for g in big_g]
            sems, big_g, lands, tok = _rs_pair_start(big_g, f"rs_pair_start_{i}")
            return (sems, big_g, lands), tok

        gin, gate_bwd, sent, small_g[i], dmod[i], dnorm_g[i] = _layer_bwd(
            i % 2 == 0, gin, dob, dgate, saved[i], mods[i][1], norm_g[i:i + 1], w, below, send)
        if below is not None:
            dob, dgate = gate_bwd
        after = gin
        if i == 0:
            dmod_all = _gather8(jnp.stack(dmod).reshape(depth * 3 * D // LANES, LANES), "gather_dmod")
            after = dmod_all = dmod_all.reshape(N_DEV, depth, 3 * D)
        big_g, theirs = _rs_pair_wait(*sent, after, f"rs_pair_wait_{i}")
        pairs = _rs_add(place, big_g, theirs)
        sems, pairs, lands, tok = _rs_chip_start(pairs, f"rs_chip_start_{i}")
        if pending is not None:
            finish(*pending, tok)
        pending = (i, sems, pairs, lands)
    grad_x = gin
    dnorm_g = jnp.concatenate(dnorm_g, axis=0)

    dmod_cols = jnp.transpose(shard_cols(dmod_all, acols), (1, 0, 2))
    r_ada_w = _ada_bwd(c_all.T, dmod_cols, ada_w, m_ada_w, v_ada_w)
    finish(*pending, r_ada_w[1])
    r_ab_w_in, r_ab_w_out, r_c_w_in, r_c_w_out = (stacked[nm] for nm in ("ab_w_in", "ab_w_out", "c_w_in", "c_w_out"))
    r_c_pool_w = tuple(a.reshape(c_pool_w.shape) for a in stacked["c_pool_w"])

    small_parts = [dnorm_g, dfinal_g,
                   jnp.stack([small_g[2 * j]["conv_w"] for j in range(n_even)]),
                   jnp.concatenate([small_g[2 * j]["ln_g"] for j in range(n_even)], axis=0),
                   jnp.concatenate([small_g[2 * j]["ln_b"] for j in range(n_even)], axis=0),
                   jnp.stack([small_g[2 * j]["sgu_b"] for j in range(n_even)]),
                   jnp.concatenate([small_g[2 * j + 1]["pool_scale"] for j in range(n_odd)], axis=0),
                   jnp.pad(loss, ((0, 7), (0, LANES - 1)))]
    small_shapes = [p.shape for p in small_parts]
    sgu_parts = [small_g[2 * j]["sgu_w"].reshape(NH * HEAD, HEAD) for j in range(n_even)]
    reduced = _allreduce8([_pack_rows(small_parts)] + sgu_parts, "allreduce_small", r_ab_w_in[1])
    g_norm_g, g_final_g, g_conv_full, g_ln_g, g_ln_b, g_sgu_b, g_scale_full, loss_row = _unpack_rows(reduced[0],
                                                                                                     small_shapes)
    g_sgu_w = jnp.stack(reduced[1:])
    loss = loss_row[0, 0]
    g_conv = shard_cols(g_conv_full, D // 4)
    g_scale = shard_cols(g_scale_full, 2 * D // 4)

    def two_d(a):
        return a.reshape(-1, a.shape[-1])

    small = [(norm_g, g_norm_g, m_norm_g, v_norm_g),
             (ada_b, dmod_all, m_ada_b, v_ada_b),
             (two_d(ab_conv_w), two_d(g_conv), two_d(m_ab_conv_w), two_d(v_ab_conv_w)),
             (ab_ln_g, g_ln_g, m_ab_ln_g, v_ab_ln_g),
             (ab_ln_b, g_ln_b, m_ab_ln_b, v_ab_ln_b),
             (two_d(ab_sgu_w), two_d(g_sgu_w), two_d(m_ab_sgu_w), two_d(v_ab_sgu_w)),
             (two_d(ab_sgu_b), two_d(g_sgu_b), two_d(m_ab_sgu_b), two_d(v_ab_sgu_b)),
             (c_pool_scale, g_scale, m_c_pool_scale, v_c_pool_scale),
             (final_g.reshape(1, D), g_final_g, m_final_g.reshape(1, D), v_final_g.reshape(1, D))]
    small_res = _adamw_small(small)
    small_shapes_out = [norm_g.shape, ada_b.shape, ab_conv_w.shape, ab_ln_g.shape, ab_ln_b.shape, ab_sgu_w.shape,
                        ab_sgu_b.shape, c_pool_scale.shape, final_g.shape]
    (r_norm_g, r_ada_b, r_conv, r_ln_g, r_ln_b, r_sgu_w, r_sgu_b, r_scale, r_final_g) = [
        tuple(a.reshape(shp) for a in res) for res, shp in zip(small_res, small_shapes_out)]

    order = [r_norm_g, r_ada_w, r_ada_b, r_ab_w_in, r_conv, r_ln_g, r_ln_b, r_sgu_w, r_sgu_b, r_ab_w_out,
             r_c_w_in, r_c_pool_w, r_scale, r_c_w_out, r_final_g]
    outs = [loss, grad_x[None]]
    for field in range(4):
        outs += [r[field] for r in order]
    return tuple(outs)
```

```python
import functools

import jax
import jax.numpy as jnp
from jax import lax
from jax.experimental import pallas as pl
from jax.experimental.pallas import tpu as pltpu

f32, bf16 = jnp.float32, jnp.bfloat16

D = 1024
HEAD = 128
NH = 8
WINDOWS = (2, 4, 8, 16)
GC = 512
EPS = 1e-6
HALO_CONV = 8
HALO_POOL = 16
CHUNK_ROWS = 512
DH_WIDE = 1024
FWD_TILES = 2
N_DEV = 8
LANES = 128

ADAM_LR, ADAM_B1, ADAM_B2, ADAM_EPS, ADAM_WD, ADAM_STEP = 0.001, 0.9, 0.999, 1e-08, 0.01, 10

MESH = pl.DeviceIdType.MESH
ANY = pl.BlockSpec(memory_space=pl.ANY)
VMEM = pl.BlockSpec(memory_space=pltpu.VMEM)
MIB = 2 ** 20


def _pcall(body, *, name, out_shape, grid=None, in_specs=None, out_specs=None, scratch=(), vmem_mb=None,
           aliases=None, prefetch=0):
    kw = {}
    if prefetch:
        kw["grid_spec"] = pltpu.PrefetchScalarGridSpec(num_scalar_prefetch=prefetch, grid=grid, in_specs=in_specs,
                                                       out_specs=out_specs, scratch_shapes=list(scratch))
    else:
        if grid is not None:
            kw["grid"] = grid
        if in_specs is not None:
            kw["in_specs"] = in_specs
        if out_specs is not None:
            kw["out_specs"] = out_specs
        if scratch:
            kw["scratch_shapes"] = list(scratch)
    if aliases:
        kw["input_output_aliases"] = aliases
    params = pltpu.CompilerParams(vmem_limit_bytes=None if vmem_mb is None else vmem_mb * MIB)
    return pl.pallas_call(body, name=name, out_shape=out_shape, compiler_params=params, **kw)


def _sds(shape, dtype):
    return jax.ShapeDtypeStruct(tuple(shape), dtype)


def _sigmoid(z):
    return pl.reciprocal(1.0 + jnp.exp(-z), approx=True)


def _silu(z):
    return z * _sigmoid(z)


def _silu_and_grad(z):
    s = _sigmoid(z)
    return z * s, s * (1.0 + z * (1.0 - s))


def _place():
    return lax.axis_index("x"), lax.axis_index("y"), lax.axis_index("c")


def _gather8(blk, name, after=()):
    def body(x_ref, *rest):
        o_ref, ssem, rsem = rest[len(after):]
        x, y, c = _place()
        me = 4 * x + 2 * y + c
        o_ref[me] = x_ref[...]
        sends = []
        for k in range(1, N_DEV):
            px = 1 - x if k & 4 else x
            py = 1 - y if k & 2 else y
            pc = 1 - c if k & 1 else c
            cp = pltpu.make_async_remote_copy(src_ref=x_ref, dst_ref=o_ref.at[me], send_sem=ssem.at[k - 1],
                                              recv_sem=rsem.at[k - 1], device_id=(px, py, pc), device_id_type=MESH)
            cp.start()
            sends.append((cp, 4 * px + 2 * py + pc))
        for k, (cp, peer) in enumerate(sends):
            pltpu.make_async_remote_copy(src_ref=x_ref, dst_ref=o_ref.at[peer], send_sem=ssem.at[k],
                                         recv_sem=rsem.at[k], device_id=(x, y, c), device_id_type=MESH).wait_recv()
        for cp, _ in sends:
            cp.wait_send()

    return _pcall(body, name=name, out_shape=_sds((N_DEV,) + blk.shape, blk.dtype), in_specs=[VMEM] + [ANY] * len(after),
                  out_specs=VMEM,
                  scratch=[pltpu.SemaphoreType.DMA((N_DEV - 1,)), pltpu.SemaphoreType.DMA((N_DEV - 1,))])(blk, *after)


def _allreduce8(bufs, name, after=None):
    n, n_after = len(bufs), 0 if after is None else 1
    rbs = [b.shape[0] // N_DEV for b in bufs]
    assert all(rb * N_DEV == b.shape[0] and rb % 8 == 0 for rb, b in zip(rbs, bufs))

    def body(*refs):
        refs = refs[:n] + refs[n + n_after:]
        xs, outs, stages = refs[:n], refs[n:2 * n], refs[2 * n:3 * n]
        ssem, rsem = refs[3 * n:]
        x, y, c = _place()
        me = 4 * x + 2 * y + c
        peers = []
        for k in range(1, N_DEV):
            px = 1 - x if k & 4 else x
            py = 1 - y if k & 2 else y
            pc = 1 - c if k & 1 else c
            peers.append(((px, py, pc), 4 * px + 2 * py + pc))

        def blk(t, ref, idx):
            return ref.at[pl.ds(pl.multiple_of(idx * rbs[t], 8), rbs[t]), :]

        def copy(t, phase, k, src, dst, dev):
            return pltpu.make_async_remote_copy(src_ref=src, dst_ref=dst, send_sem=ssem.at[t, phase, k],
                                                recv_sem=rsem.at[t, phase, k], device_id=dev, device_id_type=MESH)

        scatter = [copy(t, 0, k, blk(t, xs[t], pidx), stages[t].at[me], dev)
                   for t in range(n) for k, (dev, pidx) in enumerate(peers)]
        for cp in scatter:
            cp.start()
        gather = []
        for t in range(n):
            stages[t][me] = blk(t, xs[t], me)[...]
            for k, (dev, pidx) in enumerate(peers):
                copy(t, 0, k, blk(t, xs[t], pidx), stages[t].at[pidx], dev).wait_recv()
            total = stages[t][0]
            for j in range(1, N_DEV):
                total = total + stages[t][j]
            blk(t, outs[t], me)[...] = total
            sends = [copy(t, 1, k, blk(t, outs[t], me), blk(t, outs[t], me), dev) for k, (dev, pidx) in enumerate(peers)]
            for cp in sends:
                cp.start()
            gather += sends
        for t in range(n):
            for k, (dev, pidx) in enumerate(peers):
                copy(t, 1, k, blk(t, outs[t], pidx), blk(t, outs[t], pidx), dev).wait_recv()
        for cp in scatter + gather:
            cp.wait_send()

    return _pcall(body, name=name, out_shape=[_sds(b.shape, f32) for b in bufs], in_specs=[VMEM] * n + [ANY] * n_after,
                  out_specs=[VMEM] * n,
                  scratch=[pltpu.VMEM((N_DEV, rb, LANES), f32) for rb in rbs]
                  + [pltpu.SemaphoreType.DMA((n, 2, N_DEV - 1)), pltpu.SemaphoreType.DMA((n, 2, N_DEV - 1))])(
                      *bufs, *([] if after is None else [after]))


def _other_chips(x, y):
    return [((1 - x, y), 2 * (1 - x) + y), ((x, 1 - y), 2 * x + (1 - y)), ((1 - x, 1 - y), 2 * (1 - x) + (1 - y))]


HBM = pl.BlockSpec(memory_space=pltpu.HBM)
SEM = pl.BlockSpec(memory_space=pltpu.SEMAPHORE)
EFFECT = pltpu.SideEffectType.DATAFLOW_SIDE_EFFECTING


def _in_hbm(a):
    return pltpu.with_memory_space_constraint(a, pltpu.HBM)


def _ag_start(layers, after, name):
    flat = [t for lay in layers for t in lay]
    n, nl = len(flat), len(layers)

    def body(*refs):
        src = refs[:n]
        sems = refs[n + 1:n + 1 + 2 * nl]
        token = refs[-1]
        x, y, c = _place()
        s_me = 2 * x + y
        t = 0
        for i, lay in enumerate(layers):
            for k in range(len(lay)):
                for j, ((px, py), _) in enumerate(_other_chips(x, y)):
                    pltpu.make_async_remote_copy(src_ref=src[t].at[s_me, c], dst_ref=src[t].at[s_me, c],
                                                 send_sem=sems[2 * i].at[3 * k + j], recv_sem=sems[2 * i + 1].at[3 * k + j],
                                                 device_id=(px, py, c), device_id_type=MESH).start()
                t += 1
        token[...] = jnp.zeros_like(token)

    sem_shapes = [pltpu.SemaphoreType.DMA((3 * len(lay),)) for lay in layers for _ in range(2)]
    out_shape = sem_shapes + [pltpu.HBM(t.shape, t.dtype) for t in flat] + [_sds((8, LANES), f32)]
    outs = pl.pallas_call(
        body, name=name, out_shape=out_shape, in_specs=[HBM] * n + [ANY],
        out_specs=[SEM] * (2 * nl) + [HBM] * n + [VMEM], input_output_aliases={t: 2 * nl + t for t in range(n)},
        compiler_params=pltpu.CompilerParams(has_side_effects=EFFECT))(*[_in_hbm(t) for t in flat], after)
    sems = [(outs[2 * i], outs[2 * i + 1]) for i in range(nl)]
    thru, t = [], 2 * nl
    for lay in layers:
        thru.append(list(outs[t:t + len(lay)]))
        t += len(lay)
    return sems, thru, outs[-1]


def _ag_wait(inflight, sems, after, name):
    n = len(inflight)

    def body(*refs):
        src, ssem, rsem = refs[:n], refs[n], refs[n + 1]
        x, y, c = _place()
        s_me = 2 * x + y
        for k in range(n):
            for j, (_, s_p) in enumerate(_other_chips(x, y)):
                cp = pltpu.make_async_remote_copy(src_ref=src[k].at[s_me, c], dst_ref=src[k].at[s_p, c],
                                                  send_sem=ssem.at[3 * k + j], recv_sem=rsem.at[3 * k + j],
                                                  device_id=(x, y, c), device_id_type=MESH)
                cp.wait_send()
                cp.wait_recv()

    return pl.pallas_call(
        body, name=name, out_shape=[pltpu.HBM(t.shape, t.dtype) for t in inflight],
        in_specs=[HBM] * n + [SEM, SEM, ANY], out_specs=[HBM] * n, input_output_aliases={t: t for t in range(n)},
        compiler_params=pltpu.CompilerParams(has_side_effects=EFFECT))(*inflight, sems[0], sems[1], after)


def _ag_forward(arrived, name):
    n = len(arrived)

    def body(*refs):
        o = refs[n:2 * n]
        ssem, rsem = refs[2 * n:]
        x, y, c = _place()

        def copy(t, j, s, half, dev):
            return pltpu.make_async_remote_copy(src_ref=o[t].at[s, c], dst_ref=o[t].at[s, half], send_sem=ssem.at[t, j],
                                                recv_sem=rsem.at[t, j], device_id=dev, device_id_type=MESH)

        chips = _other_chips(x, y)
        sends = [copy(t, j, s_p, c, (x, y, 1 - c)) for t in range(n) for j, (_, s_p) in enumerate(chips)]
        for cp in sends:
            cp.start()
        for t in range(n):
            for j, (_, s_p) in enumerate(chips):
                copy(t, j, s_p, 1 - c, (x, y, c)).wait_recv()
        for cp in sends:
            cp.wait_send()

    return _pcall(body, name=name, out_shape=[_sds(p.shape, bf16) for p in arrived], in_specs=[ANY] * n,
                  out_specs=[ANY] * n, aliases={t: t for t in range(n)},
                  scratch=[pltpu.SemaphoreType.DMA((n, 3)), pltpu.SemaphoreType.DMA((n, 3))])(*arrived)


def _agf_start(arrived, name):
    n = len(arrived)

    def body(*refs):
        o = refs[:n]
        ssem, rsem, token = refs[n], refs[n + 1], refs[-1]
        x, y, c = _place()
        for t in range(n):
            for j, (_, s_p) in enumerate(_other_chips(x, y)):
                pltpu.make_async_remote_copy(src_ref=o[t].at[s_p, c], dst_ref=o[t].at[s_p, c],
                                             send_sem=ssem.at[3 * t + j], recv_sem=rsem.at[3 * t + j],
                                             device_id=(x, y, 1 - c), device_id_type=MESH).start()
        token[...] = jnp.zeros_like(token)

    out_shape = ([pltpu.SemaphoreType.DMA((3 * n,))] * 2 + [pltpu.HBM(a.shape, bf16) for a in arrived]
                 + [_sds((8, LANES), f32)])
    outs = pl.pallas_call(
        body, name=name, out_shape=out_shape, in_specs=[HBM] * n, out_specs=[SEM, SEM] + [HBM] * n + [VMEM],
        input_output_aliases={t: 2 + t for t in range(n)},
        compiler_params=pltpu.CompilerParams(has_side_effects=EFFECT))(*[_in_hbm(a) for a in arrived])
    return (outs[0], outs[1]), list(outs[2:2 + n]), outs[-1]


def _agf_wait(sems, inflight, after, name):
    n = len(inflight)

    def body(*refs):
        o, ssem, rsem = refs[:n], refs[n], refs[n + 1]
        x, y, c = _place()
        for t in range(n):
            for j, (_, s_p) in enumerate(_other_chips(x, y)):
                cp = pltpu.make_async_remote_copy(src_ref=o[t].at[s_p, c], dst_ref=o[t].at[s_p, 1 - c],
                                                  send_sem=ssem.at[3 * t + j], recv_sem=rsem.at[3 * t + j],
                                                  device_id=(x, y, c), device_id_type=MESH)
                cp.wait_send()
                cp.wait_recv()

    return pl.pallas_call(
        body, name=name, out_shape=[pltpu.HBM(a.shape, bf16) for a in inflight],
        in_specs=[HBM] * n + [SEM, SEM, ANY], out_specs=[HBM] * n, input_output_aliases={t: t for t in range(n)},
        compiler_params=pltpu.CompilerParams(has_side_effects=EFFECT))(*inflight, sems[0], sems[1], after)


def _rs_pair_start(grads, name):
    n = len(grads)

    def body(*refs):
        g, theirs = refs[:n], refs[n:2 * n]
        ssem, rsem, token = refs[2 * n], refs[2 * n + 1], refs[-1]
        x, y, c = _place()
        for t in range(n):
            pltpu.make_async_remote_copy(src_ref=g[t].at[:, 1 - c], dst_ref=theirs[t], send_sem=ssem.at[t],
                                         recv_sem=rsem.at[t], device_id=(x, y, 1 - c), device_id_type=MESH).start()
        token[...] = jnp.zeros_like(token)

    lands = [lax.empty((4,) + g.shape[2:], bf16) for g in grads]
    out_shape = ([pltpu.SemaphoreType.DMA((n,))] * 2 + [pltpu.HBM(g.shape, bf16) for g in grads]
                 + [pltpu.HBM(q.shape, bf16) for q in lands] + [_sds((8, LANES), f32)])
    outs = pl.pallas_call(
        body, name=name, out_shape=out_shape, in_specs=[HBM] * (2 * n), out_specs=[SEM, SEM] + [HBM] * (2 * n) + [VMEM],
        input_output_aliases={t: 2 + t for t in range(2 * n)},
        compiler_params=pltpu.CompilerParams(has_side_effects=EFFECT))(*[_in_hbm(a) for a in list(grads) + lands])
    return (outs[0], outs[1]), list(outs[2:2 + n]), list(outs[2 + n:2 + 2 * n]), outs[-1]


def _rs_pair_wait(sems, grads, lands, after, name):
    n = len(grads)

    def body(*refs):
        g, theirs = refs[:n], refs[n:2 * n]
        ssem, rsem = refs[2 * n], refs[2 * n + 1]
        x, y, c = _place()
        for t in range(n):
            cp = pltpu.make_async_remote_copy(src_ref=g[t].at[:, 1 - c], dst_ref=theirs[t], send_sem=ssem.at[t],
                                              recv_sem=rsem.at[t], device_id=(x, y, c), device_id_type=MESH)
            cp.wait_send()
            cp.wait_recv()

    outs = pl.pallas_call(
        body, name=name, out_shape=[pltpu.HBM(a.shape, bf16) for a in list(grads) + list(lands)],
        in_specs=[HBM] * (2 * n) + [SEM, SEM, ANY], out_specs=[HBM] * (2 * n),
        input_output_aliases={t: t for t in range(2 * n)},
        compiler_params=pltpu.CompilerParams(has_side_effects=EFFECT))(*grads, *lands, sems[0], sems[1], after)
    return list(outs[:n]), list(outs[n:])


def _rs_chip_start(pairs, name):
    n = len(pairs)

    def body(*refs):
        p, q = refs[:n], refs[n:2 * n]
        ssem, rsem, token = refs[2 * n], refs[2 * n + 1], refs[-1]
        x, y, c = _place()
        for t in range(n):
            for j, ((px, py), s_p) in enumerate(_other_chips(x, y)):
                pltpu.make_async_remote_copy(src_ref=p[t].at[s_p], dst_ref=q[t].at[j], send_sem=ssem.at[3 * t + j],
                                             recv_sem=rsem.at[3 * t + j], device_id=(px, py, c), device_id_type=MESH).start()
        token[...] = jnp.zeros_like(token)

    lands = [lax.empty((3,) + p.shape[1:], bf16) for p in pairs]
    out_shape = ([pltpu.SemaphoreType.DMA((3 * n,))] * 2 + [pltpu.HBM(p.shape, bf16) for p in pairs]
                 + [pltpu.HBM(q.shape, bf16) for q in lands] + [_sds((8, LANES), f32)])
    outs = pl.pallas_call(
        body, name=name, out_shape=out_shape, in_specs=[HBM] * (2 * n), out_specs=[SEM, SEM] + [HBM] * (2 * n) + [VMEM],
        input_output_aliases={t: 2 + t for t in range(2 * n)},
        compiler_params=pltpu.CompilerParams(has_side_effects=EFFECT))(*[_in_hbm(a) for a in list(pairs) + lands])
    return (outs[0], outs[1]), list(outs[2:2 + n]), list(outs[2 + n:2 + 2 * n]), outs[-1]


def _rs_chip_wait(sems, pairs, lands, after, name):
    n = len(pairs)

    def body(*refs):
        p, q = refs[:n], refs[n:2 * n]
        ssem, rsem = refs[2 * n], refs[2 * n + 1]
        x, y, c = _place()
        for t in range(n):
            for j, (_, s_p) in enumerate(_other_chips(x, y)):
                cp = pltpu.make_async_remote_copy(src_ref=p[t].at[s_p], dst_ref=q[t].at[j], send_sem=ssem.at[3 * t + j],
                                                  recv_sem=rsem.at[3 * t + j], device_id=(x, y, c), device_id_type=MESH)
                cp.wait_send()
                cp.wait_recv()

    outs = pl.pallas_call(
        body, name=name, out_shape=[pltpu.HBM(a.shape, bf16) for a in list(pairs) + list(lands)],
        in_specs=[HBM] * (2 * n) + [SEM, SEM, ANY], out_specs=[HBM] * (2 * n),
        input_output_aliases={t: t for t in range(2 * n)},
        compiler_params=pltpu.CompilerParams(has_side_effects=EFFECT))(*pairs, *lands, sems[0], sems[1], after)
    return list(outs[:n]), list(outs[n:])


def _rs_half_start(halves, name):
    n = len(halves)

    def body(*refs):
        o = refs[:n]
        ssem, rsem, token = refs[n], refs[n + 1], refs[-1]
        x, y, c = _place()
        for t in range(n):
            pltpu.make_async_remote_copy(src_ref=o[t].at[c], dst_ref=o[t].at[c], send_sem=ssem.at[t],
                                         recv_sem=rsem.at[t], device_id=(x, y, 1 - c), device_id_type=MESH).start()
        token[...] = jnp.zeros_like(token)

    out_shape = ([pltpu.SemaphoreType.DMA((n,))] * 2 + [pltpu.HBM(h.shape, h.dtype) for h in halves]
                 + [_sds((8, LANES), f32)])
    outs = pl.pallas_call(
        body, name=name, out_shape=out_shape, in_specs=[HBM] * n, out_specs=[SEM, SEM] + [HBM] * n + [VMEM],
        input_output_aliases={t: 2 + t for t in range(n)},
        compiler_params=pltpu.CompilerParams(has_side_effects=EFFECT))(*[_in_hbm(h) for h in halves])
    return (outs[0], outs[1]), list(outs[2:2 + n]), outs[-1]


def _rs_half_wait(sems, inflight, after, name):
    n = len(inflight)

    def body(*refs):
        o, ssem, rsem = refs[:n], refs[n], refs[n + 1]
        x, y, c = _place()
        for t in range(n):
            cp = pltpu.make_async_remote_copy(src_ref=o[t].at[c], dst_ref=o[t].at[1 - c], send_sem=ssem.at[t],
                                              recv_sem=rsem.at[t], device_id=(x, y, c), device_id_type=MESH)
            cp.wait_send()
            cp.wait_recv()

    return pl.pallas_call(
        body, name=name, out_shape=[pltpu.HBM(h.shape, h.dtype) for h in inflight],
        in_specs=[HBM] * n + [SEM, SEM, ANY], out_specs=[HBM] * n, input_output_aliases={t: t for t in range(n)},
        compiler_params=pltpu.CompilerParams(has_side_effects=EFFECT))(*inflight, sems[0], sems[1], after)


def _row_spec(tm, cols):
    return pl.BlockSpec((tm, cols), lambda i: (i, 0))


def _vec_spec(cols, rows=1):
    return pl.BlockSpec((rows, cols), lambda i: (0, 0))


def _modulated_norm(xv, g, shift, scale):
    r = lax.rsqrt(jnp.mean(xv * xv, axis=-1, keepdims=True) + EPS)
    return (((xv * r) * g) * (1.0 + scale) + shift).astype(bf16)


def _hnorm(x, g, shift, scale):
    T, tm = x.shape[0], 256

    def body(x_ref, g_ref, sh_ref, sc_ref, h_ref):
        h_ref[...] = _modulated_norm(x_ref[...], g_ref[...], sh_ref[...], sc_ref[...])

    return _pcall(body, name="hnorm", out_shape=_sds((T, D), bf16), grid=(T // tm,),
                  in_specs=[_row_spec(tm, D), _vec_spec(D), _vec_spec(D), _vec_spec(D)],
                  out_specs=_row_spec(tm, D))(x, g, shift, scale)


def _out_proj(y2, wo, x, gate, nxt=None):
    T, tm = x.shape[0], 512

    def body(y_ref, w_ref, x_ref, g_ref, *rest):
        o = jnp.dot(y_ref[0], w_ref[0], preferred_element_type=f32)
        o = o + jnp.dot(y_ref[1], w_ref[1], preferred_element_type=f32)
        xo = x_ref[...] + g_ref[...] * o
        if nxt is None:
            xo_ref, o_ref = rest
        else:
            ng_ref, nsh_ref, nsc_ref, xo_ref, o_ref, h_ref = rest
            h_ref[...] = _modulated_norm(xo, ng_ref[...], nsh_ref[...], nsc_ref[...])
        o_ref[...] = o.astype(bf16)
        xo_ref[...] = xo

    extra = [] if nxt is None else list(nxt)
    n_out = 2 if nxt is None else 3
    return _pcall(body, name="out_proj", out_shape=[_sds((T, D), f32), _sds((T, D), bf16), _sds((T, D), bf16)][:n_out],
                  grid=(T // tm,),
                  in_specs=[pl.BlockSpec((2, tm, D), lambda i: (0, i, 0)), pl.BlockSpec((2, D, D), lambda i: (0, 0, 0)),
                            _row_spec(tm, D), _vec_spec(D)] + [_vec_spec(D)] * len(extra),
                  out_specs=[_row_spec(tm, D)] * n_out, vmem_mb=40)(y2, wo, x, gate, *extra)


def _gate_bwd_tile(dx, o_ref, gate_ref, dob_ref, dgate_ref):
    dob_ref[...] = (dx * gate_ref[...]).astype(bf16)
    dgate_ref[...] += jnp.sum(dx * o_ref[...].astype(f32), axis=0, keepdims=True)


def _loss_bwd(x, target, g, o, gate):
    T, tm = x.shape[0], 512

    def body(x_ref, t_ref, g_ref, o_ref, gate_ref, dx_ref, loss_ref, dg_ref, dob_ref, dgate_ref):
        @pl.when(pl.program_id(0) == 0)
        def _():
            loss_ref[...] = jnp.zeros_like(loss_ref)
            dg_ref[...] = jnp.zeros_like(dg_ref)
            dgate_ref[...] = jnp.zeros_like(dgate_ref)

        xv, gv = x_ref[...], g_ref[...]
        r = lax.rsqrt(jnp.mean(xv * xv, axis=-1, keepdims=True) + EPS)
        xn = xv * r
        err = xn * gv - t_ref[...]
        dy = err * (1.0 / D)
        dxn = dy * gv
        dx = r * (dxn - xn * jnp.mean(dxn * xn, axis=-1, keepdims=True))
        dx_ref[...] = dx
        dg_ref[...] += jnp.sum(dy * xn, axis=0, keepdims=True)
        loss_ref[...] += (0.5 / D) * jnp.sum(jnp.sum(err * err, axis=1, keepdims=True), axis=0, keepdims=True)
        _gate_bwd_tile(dx, o_ref, gate_ref, dob_ref, dgate_ref)

    return _pcall(body, name="loss_bwd",
                  out_shape=[_sds((T, D), f32), _sds((1, 1), f32), _sds((1, D), f32), _sds((T, D), bf16), _sds((1, D), f32)],
                  grid=(T // tm,),
                  in_specs=[_row_spec(tm, D), _row_spec(tm, D), _vec_spec(D), _row_spec(tm, D), _vec_spec(D)],
                  out_specs=[_row_spec(tm, D), pl.BlockSpec((1, 1), lambda i: (0, 0)), _vec_spec(D), _row_spec(tm, D),
                             _vec_spec(D)])(x, target, g, o, gate)


def _norm_bwd(x, dh, gin, g, scale, below=None):
    T, tm = x.shape[0], 512

    def body(x_ref, dh_ref, gin_ref, g_ref, sc_ref, *rest):
        if below is None:
            dx_ref, st_ref = rest
        else:
            o_ref, gate_ref, dx_ref, st_ref, dob_ref, dgate_ref = rest

        @pl.when(pl.program_id(0) == 0)
        def _():
            st_ref[...] = jnp.zeros_like(st_ref)
            if below is not None:
                dgate_ref[...] = jnp.zeros_like(dgate_ref)

        xv, gv, dhv = x_ref[...], g_ref[...], dh_ref[...]
        r = lax.rsqrt(jnp.mean(xv * xv, axis=-1, keepdims=True) + EPS)
        xn = xv * r
        da = dhv * (1.0 + sc_ref[...])
        dxn = da * gv
        dx = gin_ref[...] + r * (dxn - xn * jnp.mean(dxn * xn, axis=-1, keepdims=True))
        dx_ref[...] = dx
        st_ref[0:1, :] += jnp.sum(dhv, axis=0, keepdims=True)
        st_ref[1:2, :] += jnp.sum(dhv * (xn * gv), axis=0, keepdims=True)
        st_ref[2:3, :] += jnp.sum(da * xn, axis=0, keepdims=True)
        if below is not None:
            _gate_bwd_tile(dx, o_ref, gate_ref, dob_ref, dgate_ref)

    out_shape = [_sds((T, D), f32), _sds((8, D), f32)]
    in_specs = [_row_spec(tm, D), _row_spec(tm, D), _row_spec(tm, D), _vec_spec(D), _vec_spec(D)]
    out_specs = [_row_spec(tm, D), _vec_spec(D, 8)]
    args = [x, dh, gin, g, scale]
    if below is not None:
        out_shape += [_sds((T, D), bf16), _sds((1, D), f32)]
        in_specs += [_row_spec(tm, D), _vec_spec(D)]
        out_specs += [_row_spec(tm, D), _vec_spec(D)]
        args += list(below)
    return _pcall(body, name="norm_bwd", out_shape=out_shape, grid=(T // tm,), in_specs=in_specs,
                  out_specs=out_specs)(*args)


STEPS = 4
ADAMW_STEPS = 8


def _cast_place(place, ws, layer, after=None):
    n = len(ws)

    def body(place_ref, *refs):
        for t in range(n):
            refs[-n + t][...] = refs[t][...].astype(bf16)

    def tile(w):
        return w.shape[1] // STEPS, w.shape[2]

    extra = [] if after is None else [after]
    return _pcall(body, name="cast_place", out_shape=[_sds((4,) + w.shape[1:], bf16) for w in ws], grid=(STEPS,),
                  prefetch=1,
                  in_specs=[pl.BlockSpec((None,) + tile(w), lambda i, pr: (layer, i, 0)) for w in ws] + [ANY] * len(extra),
                  out_specs=[pl.BlockSpec((None,) + tile(w), lambda i, pr: (pr[0], i, 0)) for w in ws])(
                      place, *ws, *extra)


def _rs_add(place, grads, theirs):
    n = len(grads)

    def body(place_ref, *refs):
        for t in range(n):
            refs[2 * n + t][...] = (refs[t][...].astype(f32) + refs[n + t][...].astype(f32)).astype(bf16)

    def tile(q):
        return q.shape[1] // 2, q.shape[2]

    mine = [pl.BlockSpec((None, None) + tile(q), lambda s, i, pr: (s, pr[1], i, 0)) for q in theirs]
    shard = [pl.BlockSpec((None,) + tile(q), lambda s, i, pr: (s, i, 0)) for q in theirs]
    return _pcall(body, name="rs_add", out_shape=[_sds(q.shape, bf16) for q in theirs], grid=(4, 2), prefetch=1,
                  in_specs=mine + shard, out_specs=shard)(place, *grads, *theirs)


def _rs_sum(place, pairs, slots):
    n, steps = len(pairs), 4

    def body(place_ref, *refs):
        for t in range(n):
            p_ref, q_ref = refs[t], refs[n + t]
            total = ((p_ref[...].astype(f32) + q_ref[0].astype(f32)) + q_ref[1].astype(f32)) + q_ref[2].astype(f32)
            refs[2 * n + t][...] = total.astype(bf16)

    def tile(q):
        return q.shape[1] // steps, q.shape[2]

    return _pcall(body, name="rs_sum", out_shape=[_sds((2,) + q.shape[1:], bf16) for q in slots], grid=(steps,),
                  prefetch=1,
                  in_specs=[pl.BlockSpec((None,) + tile(q), lambda i, pr: (pr[0], i, 0)) for q in slots]
                  + [pl.BlockSpec((3,) + tile(q), lambda i, pr: (0, i, 0)) for q in slots],
                  out_specs=[pl.BlockSpec((None,) + tile(q), lambda i, pr: (pr[1], i, 0)) for q in slots])(
                      place, *pairs, *slots)


def _adamw_math(w, g, m, v):
    m = ADAM_B1 * m + (1.0 - ADAM_B1) * g
    v = ADAM_B2 * v + (1.0 - ADAM_B2) * jnp.square(g)
    m_hat = m / (1.0 - ADAM_B1 ** ADAM_STEP)
    v_hat = v / (1.0 - ADAM_B2 ** ADAM_STEP)
    delta = -ADAM_LR * (m_hat / (jnp.sqrt(v_hat) + ADAM_EPS) + ADAM_WD * w)
    return delta, m, v


def _adamw_layer(layer, items):
    n = len(items)

    def body(*refs):
        outs = refs[-4 * n:]
        for t in range(n):
            w_ref, g_ref, m_ref, v_ref = refs[4 * t:4 * t + 4]
            g = g_ref[...].astype(f32)
            outs[4 * t][...] = g
            outs[4 * t + 1][...], outs[4 * t + 2][...], outs[4 * t + 3][...] = _adamw_math(
                w_ref[...], g, m_ref[...], v_ref[...])

    args, in_specs, out_specs, out_shape = [], [], [], []
    for w, g, m, v, _ in items:
        tr, cols = w.shape[1] // ADAMW_STEPS, w.shape[2]
        spec = pl.BlockSpec((None, tr, cols), lambda i: (layer, i, 0))
        args += [w, g, m, v]
        in_specs += [spec, pl.BlockSpec((tr, cols), lambda i: (i, 0)), spec, spec]
        out_specs += [spec] * 4
        out_shape += [_sds(w.shape, f32)] * 4
    aliases = {}
    for t, it in enumerate(items):
        if it[4] is not None:
            for k in range(4):
                aliases[len(args)] = 4 * t + k
                args.append(it[4][k])
                in_specs.append(ANY)
    res = _pcall(body, name="adamw", out_shape=out_shape, grid=(ADAMW_STEPS,), in_specs=in_specs, out_specs=out_specs,
                 aliases=aliases)(*args)
    return [tuple(res[4 * t:4 * t + 4]) for t in range(n)]


def _adamw_small(items):
    n = len(items)

    def body(*refs):
        ins, outs = refs[:4 * n], refs[4 * n:]
        for t in range(n):
            w_ref, g_ref, m_ref, v_ref = ins[4 * t:4 * t + 4]
            if len(g_ref.shape) == len(w_ref.shape) + 1:
                g = g_ref[0]
                for b in range(1, g_ref.shape[0]):
                    g = g + g_ref[b]
            else:
                g = g_ref[...]
            d, m, v = _adamw_math(w_ref[...], g, m_ref[...], v_ref[...])
            outs[4 * t][...], outs[4 * t + 1][...], outs[4 * t + 2][...], outs[4 * t + 3][...] = g, d, m, v

    out_shape = [_sds(w.shape, f32) for (w, _, _, _) in items for _ in range(4)]
    flat = [a for it in items for a in it]
    res = _pcall(body, name="adamw_small", out_shape=out_shape, in_specs=[VMEM] * (4 * n),
                 out_specs=[VMEM] * (4 * n))(*flat)
    return [tuple(res[4 * t:4 * t + 4]) for t in range(n)]


NN = ((1,), (0,))
NT = ((1,), (1,))
TN = ((0,), (0,))


def _mm(name, a, b, *, grid, a_spec, b_spec, out_shape, out_spec, dims, vmem_mb=None):
    def body(a_ref, b_ref, o_ref):
        r = lax.dot_general(a_ref[...], b_ref[...], (dims, ((), ())), preferred_element_type=f32)
        o_ref[...] = r.astype(o_ref.dtype)

    return _pcall(body, name=name, out_shape=out_shape, grid=grid, in_specs=[a_spec, b_spec], out_specs=out_spec,
                  vmem_mb=vmem_mb)(a, b)


def _whole(shape):
    return pl.BlockSpec(shape, lambda j: (0,) * len(shape))


def _split_spec(rows, tile, per_split):
    return pl.BlockSpec((None, rows, tile), lambda j: (j // per_split, 0, j % per_split))


class _Proj:
    def __init__(self, n, splits, tile):
        self.n, self.splits, self.tile = n, splits, tile
        self.steps = n // tile
        self.w_per = n // 4 // tile
        self.a_per = n // splits // tile
        assert self.w_per * tile * 4 == n and self.a_per * tile * splits == n

    def fwd(self, hb, wg):
        T = hb.shape[0]
        sub, tile, w_per = FWD_TILES, self.tile, self.w_per
        wide = sub * tile
        a_per = self.n // self.splits // wide
        assert a_per * wide * self.splits == self.n

        def w_tile(q):
            return pl.BlockSpec((None, D, tile), lambda j: ((sub * j + q) // w_per, 0, (sub * j + q) % w_per))

        def body(a_ref, *rest):
            w = jnp.concatenate([rest[q][...] for q in range(sub)], axis=1)
            rest[sub][...] = jnp.dot(a_ref[...], w, preferred_element_type=f32).astype(bf16)

        return _pcall(body, name="proj_fwd", out_shape=_sds((self.splits, T, self.n // self.splits), bf16),
                      grid=(self.n // wide,), in_specs=[_whole((T, D))] + [w_tile(q) for q in range(sub)],
                      out_specs=pl.BlockSpec((None, T, wide), lambda j: (j // a_per, 0, j % a_per)),
                      vmem_mb=40 if wide > 512 else None)(hb, *([wg] * sub))

    def dw(self, hb, dp):
        T = hb.shape[0]
        return _mm("proj_dw", hb, dp, grid=(self.steps,), a_spec=_whole((T, D)),
                   b_spec=_split_spec(T, self.tile, self.a_per), out_shape=_sds((4, D, self.n // 4), bf16),
                   out_spec=_split_spec(D, self.tile, self.w_per), dims=TN)

    def dh(self, dp, wg):
        T = dp.shape[1]
        sub, tile, w_per = DH_WIDE // self.tile, self.tile, self.w_per
        a_per = self.n // self.splits // DH_WIDE
        assert sub * tile == DH_WIDE and a_per * DH_WIDE * self.splits == self.n

        def w_tile(q):
            return pl.BlockSpec((None, D, tile), lambda k: ((sub * k + q) // w_per, 0, (sub * k + q) % w_per))

        def body(a_ref, *rest):
            o_ref = rest[sub]
            w = jnp.concatenate([rest[q][...] for q in range(sub)], axis=1)
            r = lax.dot_general(a_ref[...], w, (NT, ((), ())), preferred_element_type=f32)

            @pl.when(pl.program_id(0) == 0)
            def _():
                o_ref[...] = r

            @pl.when(pl.program_id(0) > 0)
            def _():
                o_ref[...] += r

        return _pcall(body, name="proj_dh", out_shape=_sds((T, D), f32), grid=(self.n // DH_WIDE,),
                      in_specs=[pl.BlockSpec((None, T, DH_WIDE), lambda k: (k // a_per, 0, k % a_per))]
                      + [w_tile(q) for q in range(sub)],
                      out_specs=_whole((T, D)), vmem_mb=40)(dp, *([wg] * sub))


EVEN_PROJ = _Proj(7 * D, 7, 256)
ODD_PROJ = _Proj(4 * D, 2, 512)


def _dy_mm(dob, wo):
    T = dob.shape[0]
    return _mm("out_dy", dob, wo, grid=(4,), a_spec=_whole((T, D)),
               b_spec=pl.BlockSpec((None, 512, D), lambda j: (j, 0, 0)), out_shape=_sds((2, T, D), bf16),
               out_spec=_split_spec(T, 512, 2), dims=NT)


def _dwo_mm(y2, dob):
    T = dob.shape[0]
    return _mm("out_dw", y2, dob, grid=(4,), a_spec=_split_spec(T, 512, 2), b_spec=_whole((T, D)),
               out_shape=_sds((4, 512, D), bf16), out_spec=pl.BlockSpec((None, 512, D), lambda j: (j, 0, 0)), dims=TN)


def _head_spec(lead, T):
    return pl.BlockSpec((lead, T, HEAD), lambda h: (0, 0, h))


def _head_vec(rows):
    return pl.BlockSpec((rows, HEAD), lambda h: (0, h))


_HEAD_MAT = pl.BlockSpec((None, HEAD, HEAD), lambda h: (h, 0, 0))


def _causal():
    return lax.broadcasted_iota(jnp.int32, (HEAD, HEAD), 0) >= lax.broadcasted_iota(jnp.int32, (HEAD, HEAD), 1)


def _layernorm_head(v):
    mu = jnp.mean(v, axis=-1, keepdims=True)
    d = v - mu
    rstd = lax.rsqrt(jnp.mean(d * d, axis=-1, keepdims=True) + EPS)
    return d * rstd, rstd


def _even_fwd(p7, conv_w, ln_g, ln_b, sgu_w, sgu_bias):
    T, C = p7.shape[1], CHUNK_ROWS

    def body(p_ref, cw_ref, lg_ref, lb_ref, w_ref, b_ref, y_ref):
        w0, w1, w2 = cw_ref[0:1, :], cw_ref[1:2, :], cw_ref[2:3, :]
        wm = jnp.where(_causal(), w_ref[...], 0.0).astype(bf16)
        bias, lg, lb = b_ref[...], lg_ref[...], lb_ref[...]

        def step(i, halo):
            rows = pl.ds(pl.multiple_of(i * C, C), C)
            ah, ab, ac, az, u, v, zb = (p_ref[k, rows, :].astype(f32) for k in range(7))
            tt = ac * ah
            ext = jnp.concatenate([halo, tt], axis=0)
            cv = w2 * tt + w1 * pltpu.roll(ext, 1, 0)[HALO_CONV:] + w0 * pltpu.roll(ext, 2, 0)[HALO_CONV:]
            y_ref[0, rows, :] = (ab * cv * _silu(az)).astype(bf16)
            vhat, _ = _layernorm_head(v)
            vn = (vhat * lg + lb).astype(bf16)
            mix = jnp.concatenate([jnp.dot(wm, vn[k * HEAD:(k + 1) * HEAD], preferred_element_type=f32) + bias
                                   for k in range(C // HEAD)], axis=0)
            y_ref[1, rows, :] = (u * mix * _silu(zb)).astype(bf16)
            return tt[C - HALO_CONV:]

        lax.fori_loop(0, T // C, step, jnp.zeros((HALO_CONV, HEAD), f32))

    return _pcall(body, name="even_fwd", out_shape=_sds((2, T, D), bf16), grid=(NH,),
                  in_specs=[_head_spec(7, T), _head_vec(3), _head_vec(1), _head_vec(1), _HEAD_MAT, _HEAD_MAT],
                  out_specs=_head_spec(2, T))(p7, conv_w, ln_g, ln_b, sgu_w, sgu_bias)


def _even_bwd(p7, dy2, conv_w, ln_g, ln_b, sgu_w, sgu_bias):
    T, C = p7.shape[1], CHUNK_ROWS
    n_chunks = T // C

    def body(p_ref, dy_ref, cw_ref, lg_ref, lb_ref, w_ref, b_ref,
             dp_ref, dcw_ref, dlg_ref, dlb_ref, dw_ref, dms_ref, dcv_s):
        w0, w1, w2 = cw_ref[0:1, :], cw_ref[1:2, :], cw_ref[2:3, :]
        tri = _causal()
        wm = jnp.where(tri, w_ref[...], 0.0).astype(bf16)
        bias, lg, lb = b_ref[...], lg_ref[...], lb_ref[...]
        dw_ref[...] = jnp.zeros_like(dw_ref)
        dms_ref[...] = jnp.zeros_like(dms_ref)

        def fwd_step(i, carry):
            halo, a0, a1, a2, alg, alb = carry
            rows = pl.ds(pl.multiple_of(i * C, C), C)
            ah, ab, ac, az = (p_ref[k, rows, :].astype(f32) for k in range(4))
            dya = dy_ref[0, rows, :].astype(f32)
            tt = ac * ah
            ext = jnp.concatenate([halo, tt], axis=0)
            t1, t2 = pltpu.roll(ext, 1, 0)[HALO_CONV:], pltpu.roll(ext, 2, 0)[HALO_CONV:]
            cv = w2 * tt + w1 * t1 + w0 * t2
            sa, dsa = _silu_and_grad(az)
            g1 = dya * sa
            dp_ref[1, rows, :] = (g1 * cv).astype(bf16)
            dp_ref[3, rows, :] = (dya * ab * cv * dsa).astype(bf16)
            dcv = g1 * ab
            dcv_s[rows, :] = dcv
            a2 = a2 + jnp.sum(dcv * tt, axis=0, keepdims=True)
            a1 = a1 + jnp.sum(dcv * t1, axis=0, keepdims=True)
            a0 = a0 + jnp.sum(dcv * t2, axis=0, keepdims=True)

            u, zb, dyb = p_ref[4, rows, :].astype(f32), p_ref[6, rows, :].astype(f32), dy_ref[1, rows, :].astype(f32)
            vhat, rstd = _layernorm_head(p_ref[5, rows, :].astype(f32))
            vn = (vhat * lg + lb).astype(bf16)
            sb, dsb = _silu_and_grad(zb)
            mix = jnp.concatenate([jnp.dot(wm, vn[k * HEAD:(k + 1) * HEAD], preferred_element_type=f32) + bias
                                   for k in range(C // HEAD)], axis=0)
            dp_ref[4, rows, :] = (dyb * mix * sb).astype(bf16)
            dp_ref[6, rows, :] = (dyb * u * mix * dsb).astype(bf16)
            dmix = dyb * u * sb
            dvn_parts = []
            for k in range(C // HEAD):
                dm = dmix[k * HEAD:(k + 1) * HEAD]
                dmb = dm.astype(bf16)
                dvn_parts.append(lax.dot_general(wm, dmb, (TN, ((), ())), preferred_element_type=f32))
                dw_ref[...] += lax.dot_general(dmb, vn[k * HEAD:(k + 1) * HEAD], (NT, ((), ())),
                                               preferred_element_type=f32)
                dms_ref[...] += dm
            dvn = jnp.concatenate(dvn_parts, axis=0)
            alg = alg + jnp.sum(dvn * vhat, axis=0, keepdims=True)
            alb = alb + jnp.sum(dvn, axis=0, keepdims=True)
            dvh = dvn * lg
            dv = rstd * (dvh - jnp.mean(dvh, axis=-1, keepdims=True)
                         - vhat * jnp.mean(dvh * vhat, axis=-1, keepdims=True))
            dp_ref[5, rows, :] = dv.astype(bf16)
            return tt[C - HALO_CONV:], a0, a1, a2, alg, alb

        zrow = jnp.zeros((1, HEAD), f32)
        _, a0, a1, a2, alg, alb = lax.fori_loop(
            0, n_chunks, fwd_step, (jnp.zeros((HALO_CONV, HEAD), f32), zrow, zrow, zrow, zrow, zrow))
        dcw_ref[0:1, :], dcw_ref[1:2, :], dcw_ref[2:3, :] = a0, a1, a2
        dlg_ref[...], dlb_ref[...] = alg, alb
        dw_ref[...] = jnp.where(tri, dw_ref[...], 0.0)

        def bwd_step(k, halo):
            rows = pl.ds(pl.multiple_of((n_chunks - 1 - k) * C, C), C)
            dcv = dcv_s[rows, :]
            ext = jnp.concatenate([dcv, halo], axis=0)
            n1 = pltpu.roll(ext, C + HALO_CONV - 1, 0)[:C]
            n2 = pltpu.roll(ext, C + HALO_CONV - 2, 0)[:C]
            dtt = w2 * dcv + w1 * n1 + w0 * n2
            dp_ref[2, rows, :] = (dtt * p_ref[0, rows, :].astype(f32)).astype(bf16)
            dp_ref[0, rows, :] = (dtt * p_ref[2, rows, :].astype(f32)).astype(bf16)
            return dcv[:HALO_CONV]

        lax.fori_loop(0, n_chunks, bwd_step, jnp.zeros((HALO_CONV, HEAD), f32))

    out_shape = [_sds((7, T, D), bf16), _sds((3, D), f32), _sds((1, D), f32), _sds((1, D), f32),
                 _sds((NH, HEAD, HEAD), f32), _sds((NH, HEAD, HEAD), f32)]
    return _pcall(body, name="even_bwd", out_shape=out_shape, grid=(NH,),
                  in_specs=[_head_spec(7, T), _head_spec(2, T), _head_vec(3), _head_vec(1), _head_vec(1),
                            _HEAD_MAT, _HEAD_MAT],
                  out_specs=[_head_spec(7, T), _head_vec(3), _head_vec(1), _head_vec(1), _HEAD_MAT, _HEAD_MAT],
                  scratch=[pltpu.VMEM((T, HEAD), f32)])(p7, dy2, conv_w, ln_g, ln_b, sgu_w, sgu_bias)


def _window_sum(ext, win, towards_past):
    n, k, s = ext.shape[0], 1, ext
    while k < win:
        s = s + pltpu.roll(s, k if towards_past else n - k, 0)
        k *= 2
    return s


def _pool_count(i, C, win):
    t = i * C + lax.broadcasted_iota(jnp.int32, (C, 1), 0)
    cnt = jnp.minimum(t + 1, win).astype(f32)
    return cnt, 1.0 / cnt


def _group_specs(T):
    p_spec = pl.BlockSpec((None, T, GC), lambda g: (0, 0, g))
    z_spec = pl.BlockSpec((None, T, GC), lambda g: (1, 0, g))
    pw_spec = pl.BlockSpec((4, GC // 4, GC), lambda g: (0, g, 0))
    ps_spec = pl.BlockSpec((1, GC), lambda g: (0, g))
    y_spec = pl.BlockSpec((None, T, GC), lambda g: (g // 2, 0, g % 2))
    return p_spec, z_spec, pw_spec, ps_spec, y_spec


def _odd_fwd(p2, pool_wg, pool_scale):
    T, C = p2.shape[1], CHUNK_ROWS
    p_spec, z_spec, pw_spec, ps_spec, y_spec = _group_specs(T)

    def body(p_ref, z_ref, pw_ref, ps_ref, y_ref):
        pw, ps = pw_ref[...].reshape(GC, GC), ps_ref[...]

        def run(win):
            def step(i, halo):
                rows = pl.ds(pl.multiple_of(i * C, C), C)
                p = p_ref[rows, :].astype(f32)
                s = _window_sum(jnp.concatenate([halo, p], axis=0), win, True)[HALO_POOL:]
                pooled = s * _pool_count(i, C, win)[1] - p
                ypre = jnp.dot(pooled.astype(bf16), pw, preferred_element_type=f32)
                y_ref[rows, :] = (ypre * ps * _silu(z_ref[rows, :].astype(f32))).astype(bf16)
                return p[C - HALO_POOL:]

            lax.fori_loop(0, T // C, step, jnp.zeros((HALO_POOL, GC), f32))

        for gi, win in enumerate(WINDOWS):
            pl.when(pl.program_id(0) == gi)(functools.partial(run, win))

    return _pcall(body, name="odd_fwd", out_shape=_sds((2, T, D), bf16), grid=(len(WINDOWS),),
                  in_specs=[p_spec, z_spec, pw_spec, ps_spec], out_specs=y_spec)(p2, p2, pool_wg, pool_scale)


def _odd_bwd(p2, dy2, pool_wg, pool_scale):
    T, C = p2.shape[1], CHUNK_ROWS
    n_chunks = T // C
    p_spec, z_spec, pw_spec, ps_spec, y_spec = _group_specs(T)

    def body(p_ref, z_ref, dy_ref, pw_ref, ps_ref, dp_ref, dpw_ref, dps_ref, q_s, acc_s):
        pw, ps = pw_ref[...].reshape(GC, GC), ps_ref[...]

        def run(win):
            acc_s[...] = jnp.zeros_like(acc_s)

            def fwd_step(i, carry):
                halo, aps = carry
                rows = pl.ds(pl.multiple_of(i * C, C), C)
                p, z, dy = p_ref[rows, :].astype(f32), z_ref[rows, :].astype(f32), dy_ref[rows, :].astype(f32)
                _, inv_cnt = _pool_count(i, C, win)
                s = _window_sum(jnp.concatenate([halo, p], axis=0), win, True)[HALO_POOL:]
                pb = (s * inv_cnt - p).astype(bf16)
                ypre = jnp.dot(pb, pw, preferred_element_type=f32)
                sz, dsz = _silu_and_grad(z)
                aps = aps + jnp.sum(dy * ypre * sz, axis=0, keepdims=True)
                dp_ref[1, rows, :] = (dy * ypre * ps * dsz).astype(bf16)
                dyp = (dy * ps * sz).astype(bf16)
                acc_s[...] += lax.dot_general(pb, dyp, (TN, ((), ())), preferred_element_type=f32)
                dpool = lax.dot_general(dyp, pw, (NT, ((), ())), preferred_element_type=f32)
                q_s[rows, :] = dpool * inv_cnt
                return p[C - HALO_POOL:], aps

            _, aps = lax.fori_loop(0, n_chunks, fwd_step, (jnp.zeros((HALO_POOL, GC), f32), jnp.zeros((1, GC), f32)))
            dps_ref[...] = aps
            dpw_ref[...] = acc_s[...].reshape(4, GC // 4, GC).astype(bf16)

            def bwd_step(k, halo):
                i = n_chunks - 1 - k
                rows = pl.ds(pl.multiple_of(i * C, C), C)
                q = q_s[rows, :]
                s = _window_sum(jnp.concatenate([q, halo], axis=0), win, False)[:C]
                dp_ref[0, rows, :] = (s - q * _pool_count(i, C, win)[0]).astype(bf16)
                return q[:HALO_POOL]

            lax.fori_loop(0, n_chunks, bwd_step, jnp.zeros((HALO_POOL, GC), f32))

        for gi, win in enumerate(WINDOWS):
            pl.when(pl.program_id(0) == gi)(functools.partial(run, win))

    out_shape = [_sds((2, T, 2 * D), bf16), _sds((4, GC, GC), bf16), _sds((1, 2 * D), f32)]
    return _pcall(body, name="odd_bwd", out_shape=out_shape, grid=(len(WINDOWS),),
                  in_specs=[p_spec, z_spec, y_spec, pw_spec, ps_spec],
                  out_specs=[pl.BlockSpec((2, T, GC), lambda g: (0, 0, g)), pw_spec, ps_spec],
                  scratch=[pltpu.VMEM((T, GC), f32), pltpu.VMEM((GC, GC), f32)], vmem_mb=44)(
                      p2, p2, dy2, pool_wg, pool_scale)


def _ada_fwd(c_all, ada_w):
    cols = ada_w.shape[2]

    def body(c_ref, w_ref, o_ref):
        o_ref[...] = jnp.dot(_silu(c_ref[...]), w_ref[...], preferred_element_type=f32,
                             precision=lax.Precision.HIGHEST)

    return _pcall(body, name="ada_fwd", out_shape=_sds((4, N_DEV, cols), f32), grid=(4,),
                  in_specs=[pl.BlockSpec((N_DEV, D), lambda i: (0, 0)), pl.BlockSpec((None, D, cols), lambda i: (i, 0, 0))],
                  out_specs=pl.BlockSpec((None, N_DEV, cols), lambda i: (i, 0, 0)))(c_all, ada_w)


def _ada_bwd(c_all_t, dmod, w, m, v):
    cols, tr = w.shape[2], 256
    spec = pl.BlockSpec((None, tr, cols), lambda l, i: (l, i, 0))

    def body(c_ref, dm_ref, w_ref, m_ref, v_ref, g_ref, d_ref, mo_ref, vo_ref):
        sc = _silu(c_ref[...])
        g = sc[:, 0:1] * dm_ref[0:1, :]
        for b in range(1, N_DEV):
            g = g + sc[:, b:b + 1] * dm_ref[b:b + 1, :]
        g_ref[...] = g
        d_ref[...], mo_ref[...], vo_ref[...] = _adamw_math(w_ref[...], g, m_ref[...], v_ref[...])

    return _pcall(body, name="ada_bwd", out_shape=[_sds(w.shape, f32)] * 4, grid=(4, D // tr),
                  in_specs=[pl.BlockSpec((tr, N_DEV), lambda l, i: (i, 0)),
                            pl.BlockSpec((None, N_DEV, cols), lambda l, i: (l, 0, 0)), spec, spec, spec],
                  out_specs=[spec] * 4)(c_all_t, dmod, w, m, v)


def _layer_fwd(even, x, hb, gate, w, nxt, before_out=None):
    if even:
        w_in, w_out, conv_w, ln_g, ln_b, sgu_w, sgu_b = w
        bias = jnp.broadcast_to(sgu_b[:, :, None], (NH, HEAD, HEAD))
        p = EVEN_PROJ.fwd(hb, w_in)
        y2 = _even_fwd(p, conv_w, ln_g, ln_b, sgu_w, bias)
    else:
        w_in, pool_w, w_out, pool_scale = w
        p = ODD_PROJ.fwd(hb, w_in)
        y2 = _odd_fwd(p, pool_w, pool_scale)
    if before_out is not None:
        late_w_out, tok = before_out(y2)
        if late_w_out is not None:
            w_out = late_w_out
            w = (w_in, w_out) + tuple(w[2:]) if even else (w_in, pool_w, w_out, pool_scale)
        if tok is not None:
            gate = gate + tok[0:1, 0:1]
    outs = _out_proj(y2, w_out.reshape(2, D, D), x, gate, nxt)
    return outs[0], (None if nxt is None else outs[2]), (x, hb, p, y2, outs[1]), w


def _layer_bwd(even, gin, dob, dgate, saved, scale, g, w, below=None, send=None):
    x_in, hb, p, y2, o = saved
    if even:
        w_in, w_out, conv_w, ln_g, ln_b, sgu_w, sgu_b = w
        bias = jnp.broadcast_to(sgu_b[:, :, None], (NH, HEAD, HEAD))
        dy2 = _dy_mm(dob, w_out)
        dp, dconv, dlg, dlb, dsw, dms = _even_bwd(p, dy2, conv_w, ln_g, ln_b, sgu_w, bias)
        proj = EVEN_PROJ
        small = dict(conv_w=dconv, ln_g=dlg, ln_b=dlb, sgu_w=dsw, sgu_b=jnp.sum(dms, axis=-1))
        big = [proj.dw(hb, dp), _dwo_mm(y2, dob)]
    else:
        w_in, pool_w, w_out, pool_scale = w
        dy2 = _dy_mm(dob, w_out)
        dp, dpw, dps = _odd_bwd(p, dy2, pool_w, pool_scale)
        proj = ODD_PROJ
        small = dict(pool_scale=dps)
        big = [proj.dw(hb, dp), dpw, _dwo_mm(y2, dob)]
    if send is not None:
        big, tok = send(big)
        scale = scale + tok[0:1, 0:1]
    dh = proj.dh(dp, w_in)
    res = _norm_bwd(x_in, dh, gin, g, scale, below)
    stats = res[1]
    return (res[0], (None if below is None else (res[2], res[3])), big, small,
            jnp.concatenate([stats[0:2], dgate], axis=0), stats[2:3])


def _pack_rows(parts):
    rows = [p.reshape(-1, LANES) for p in parts]
    total = sum(r.shape[0] for r in rows)
    padded = -(-total // (8 * N_DEV)) * (8 * N_DEV)
    if padded > total:
        rows.append(jnp.zeros((padded - total, LANES), f32))
    return jnp.concatenate(rows, axis=0)


def _unpack_rows(buf, shapes):
    out, r = [], 0
    for shp in shapes:
        n = 1
        for d in shp:
            n *= d
        out.append(buf[r:r + n // LANES].reshape(shp))
        r += n // LANES
    return out


def kernel(x, c, norm_g, ada_w, ada_b, ab_w_in, ab_conv_w, ab_ln_g, ab_ln_b, ab_sgu_w, ab_sgu_b, ab_w_out, c_w_in, c_pool_w, c_pool_scale, c_w_out, final_g, loss_target, m_norm_g, m_ada_w, m_ada_b, m_ab_w_in, m_ab_conv_w, m_ab_ln_g, m_ab_ln_b, m_ab_sgu_w, m_ab_sgu_b, m_ab_w_out, m_c_w_in, m_c_pool_w, m_c_pool_scale, m_c_w_out, m_final_g, v_norm_g, v_ada_w, v_ada_b, v_ab_w_in, v_ab_conv_w, v_ab_ln_g, v_ab_ln_b, v_ab_sgu_w, v_ab_sgu_b, v_ab_w_out, v_c_w_in, v_c_pool_w, v_c_pool_scale, v_c_w_out, v_final_g):
    ix, iy, ic = _place()
    chip, dev = 2 * ix + iy, 4 * ix + 2 * iy + ic
    n_even, n_odd = ab_w_in.shape[0], c_w_in.shape[0]
    depth = n_even + n_odd
    acols = ada_w.shape[2]

    place = jnp.stack([chip, ic]).astype(jnp.int32)
    even_names, odd_names = ["ab_w_in", "ab_w_out"], ["c_w_in", "c_pool_w", "c_w_out"]
    params = {"ab_w_in": (ab_w_in, m_ab_w_in, v_ab_w_in), "ab_w_out": (ab_w_out, m_ab_w_out, v_ab_w_out),
              "c_w_in": (c_w_in, m_c_w_in, v_c_w_in), "c_w_out": (c_w_out, m_c_w_out, v_c_w_out),
              "c_pool_w": tuple(a.reshape(n_odd, GC, GC) for a in (c_pool_w, m_c_pool_w, v_c_pool_w))}

    def placed(names, layer, after=None):
        ws = [params[nm][0] for nm in names]
        return [p.reshape(4, 2, p.shape[1] // 2, p.shape[2]) for p in _cast_place(place, ws, layer, after)]

    def whole(arrays):
        return [g.reshape(4, 2 * g.shape[2], g.shape[3]) for g in arrays]

    first = _gather8(jnp.concatenate([c, ab_conv_w.reshape(1, -1), c_pool_scale.reshape(1, -1)], axis=1), "gather_c")
    c_all, small_all = first[:, 0, :D], first[0::2, 0, D:]
    sems_a, in_a, tok = _ag_start([placed(even_names[:1], 0)], first[0:1, 0, 0:LANES], "ag_start_0a")
    modp = _ada_fwd(c_all, ada_w)
    later = [placed(even_names[1:], 0, tok)]
    later += [placed(even_names if i % 2 == 0 else odd_names, i // 2, tok) for i in range(1, depth)]
    modg = _gather8(modp + tok[0:1, 0:1], "gather_mod", [lay[-1] for lay in later])
    mod_rows = lax.dynamic_index_in_dim(modg[0::2], dev, axis=2, keepdims=False)
    mod = jnp.transpose(mod_rows, (1, 0, 2)).reshape(depth, 3 * D) + ada_b
    mods = [(mod[i:i + 1, 0:D], mod[i:i + 1, D:2 * D], mod[i:i + 1, 2 * D:3 * D]) for i in range(depth)]

    def shard_cols(a, width):
        return lax.dynamic_slice_in_dim(a, chip * width, width, axis=a.ndim - 1)

    n_conv = ab_conv_w.size
    conv_all = small_all[:, :n_conv].reshape(4, n_even, 3, D // 4)
    conv_full = jnp.transpose(conv_all, (1, 2, 0, 3)).reshape(n_even, 3, D)
    scale_all = small_all[:, n_conv:].reshape(4, n_odd, 2 * D // 4)
    scale_full = jnp.transpose(scale_all, (1, 0, 2)).reshape(n_odd, 2 * D)

    gathers_done = mod[0:1, 0:LANES] + scale_full[0:1, 0:LANES]
    sems_b, in_b, tok = _ag_start(later[:1], gathers_done, "ag_start_0b")
    sems_r, in_r, tok = _ag_start(later[1:], tok, "ag_start_rest")

    x_cur, saved, weights, handoff = x[0], [], [], {}
    sems_f, in_f, tok = _agf_start(_ag_wait(in_a[0], sems_a[0], tok, "ag_wait_0a"), "agf_start_0")
    hb = _hnorm(x_cur, norm_g[0:1], mods[0][0] + tok[0:1, 0:1], mods[0][1])
    for i in range(depth):
        j = i // 2
        if i == 0:
            full = whole(_agf_wait(sems_f, in_f, hb, "agf_wait_0")) + [None]
        else:
            full = whole(_agf_wait(*handoff.pop(i), x_cur, f"agf_wait_{i}"))
        if i % 2 == 0:
            w = (full[0], full[1], conv_full[j], ab_ln_g[j:j + 1], ab_ln_b[j:j + 1], ab_sgu_w[j], ab_sgu_b[j])
        else:
            w = (full[0], full[1], full[2], scale_full[j:j + 1])

        def before_out(y2, i=i):
            w_out, tok = None, None
            if i == 0:
                w_out = whole(_ag_forward(_ag_wait(in_b[0], sems_b[0], y2, "ag_wait_0b"), "ag_forward"))[0]
            if i + 1 < depth:
                arrived = _ag_wait(in_r[i], sems_r[i], y2, f"ag_wait_{i + 1}")
                sems_f, inflight, tok = _agf_start(arrived, f"agf_start_{i + 1}")
                handoff[i + 1] = (sems_f, inflight)
            return w_out, tok

        nxt = (norm_g[i + 1:i + 2], mods[i + 1][0], mods[i + 1][1]) if i + 1 < depth else None
        x_cur, hb, sv, w = _layer_fwd(i % 2 == 0, x_cur, hb, mods[i][2], w, nxt, before_out)
        weights.append(w)
        saved.append(sv)
    gin, loss, dfinal_g, dob, dgate = _loss_bwd(x_cur, loss_target[0], final_g.reshape(1, D), saved[-1][4],
                                                mods[-1][2])

    stacked = {}

    def reduce_layer(i, sems, pairs, lands, after):
        pairs, slots = _rs_chip_wait(sems, pairs, lands, after, f"rs_chip_wait_{i}")
        half_sems, halves, _ = _rs_half_start(_rs_sum(place, pairs, slots), f"rs_half_start_{i}")
        return i, half_sems, halves

    def update_layer(i, half_sems, halves, after):
        names = even_names if i % 2 == 0 else odd_names
        grads = _rs_half_wait(half_sems, halves, after, f"rs_half_wait_{i}")
        items = [(params[nm][0], g.reshape(params[nm][0].shape[1:]), params[nm][1], params[nm][2], stacked.get(nm))
                 for nm, g in zip(names, grads)]
        for nm, res in zip(names, _adamw_layer(i // 2, items)):
            stacked[nm] = res

    small_g, dmod, dnorm_g, pending, tok = [None] * depth, [None] * depth, [None] * depth, None, None
    exchanging = []
    for i in reversed(range(depth)):
        w = weights[i]
        if tok is not None:
            w = w[:2] + (w[2] + tok[0:1, 0:1],) + w[3:] if i % 2 == 0 else w[:3] + (w[3] + tok[0:1, 0:1],)
        below = (saved[i - 1][4], mods[i - 1][2]) if i > 0 else None

        def send(big_g, i=i):
            if exchanging:
                update_layer(*exchanging.pop(), big_g[0])
            big_g = [g.reshape(4, 2, g.shape[1] // 2, g.shape[2]) for g in big_g]
            sems, big_g, lands, tok = _rs_pair_start(big_g, f"rs_pair_start_{i}")
            return (sems, big_g, lands), tok

        gin, gate_bwd, sent, small_g[i], dmod[i], dnorm_g[i] = _layer_bwd(
            i % 2 == 0, gin, dob, dgate, saved[i], mods[i][1], norm_g[i:i + 1], w, below, send)
        if below is not None:
            dob, dgate = gate_bwd
        after = gin
        if i == 0:
            dmod_all = _gather8(jnp.stack(dmod).reshape(depth * 3 * D // LANES, LANES), "gather_dmod")
            after = dmod_all = dmod_all.reshape(N_DEV, depth, 3 * D)
        big_g, theirs = _rs_pair_wait(*sent, after, f"rs_pair_wait_{i}")
        pairs = _rs_add(place, big_g, theirs)
        sems, pairs, lands, tok = _rs_chip_start(pairs, f"rs_chip_start_{i}")
        if pending is not None:
            exchanging.append(reduce_layer(*pending, tok))
        pending = (i, sems, pairs, lands)
    grad_x = gin
    dnorm_g = jnp.concatenate(dnorm_g, axis=0)

    dmod_cols = jnp.transpose(shard_cols(dmod_all, acols), (1, 0, 2))
    r_ada_w = _ada_bwd(c_all.T, dmod_cols, ada_w, m_ada_w, v_ada_w)
    update_layer(*exchanging.pop(), r_ada_w[1])
    last = reduce_layer(*pending, r_ada_w[1])

    small_parts = [dnorm_g, dfinal_g,
                   jnp.stack([small_g[2 * j]["conv_w"] for j in range(n_even)]),
                   jnp.concatenate([small_g[2 * j]["ln_g"] for j in range(n_even)], axis=0),
                   jnp.concatenate([small_g[2 * j]["ln_b"] for j in range(n_even)], axis=0),
                   jnp.stack([small_g[2 * j]["sgu_b"] for j in range(n_even)]),
                   jnp.concatenate([small_g[2 * j + 1]["pool_scale"] for j in range(n_odd)], axis=0),
                   jnp.pad(loss, ((0, 7), (0, LANES - 1)))]
    small_shapes = [p.shape for p in small_parts]
    sgu_parts = [small_g[2 * j]["sgu_w"].reshape(NH * HEAD, HEAD) for j in range(n_even)]
    reduced = _allreduce8([_pack_rows(small_parts)] + sgu_parts, "allreduce_small", last[2][0])
    update_layer(*last, reduced[0])
    r_ab_w_in, r_ab_w_out, r_c_w_in, r_c_w_out = (stacked[nm] for nm in ("ab_w_in", "ab_w_out", "c_w_in", "c_w_out"))
    r_c_pool_w = tuple(a.reshape(c_pool_w.shape) for a in stacked["c_pool_w"])
    g_norm_g, g_final_g, g_conv_full, g_ln_g, g_ln_b, g_sgu_b, g_scale_full, loss_row = _unpack_rows(reduced[0],
                                                                                                     small_shapes)
    g_sgu_w = jnp.stack(reduced[1:])
    loss = loss_row[0, 0]
    g_conv = shard_cols(g_conv_full, D // 4)
    g_scale = shard_cols(g_scale_full, 2 * D // 4)

    def two_d(a):
        return a.reshape(-1, a.shape[-1])

    small = [(norm_g, g_norm_g, m_norm_g, v_norm_g),
             (ada_b, dmod_all, m_ada_b, v_ada_b),
             (two_d(ab_conv_w), two_d(g_conv), two_d(m_ab_conv_w), two_d(v_ab_conv_w)),
             (ab_ln_g, g_ln_g, m_ab_ln_g, v_ab_ln_g),
             (ab_ln_b, g_ln_b, m_ab_ln_b, v_ab_ln_b),
             (two_d(ab_sgu_w), two_d(g_sgu_w), two_d(m_ab_sgu_w), two_d(v_ab_sgu_w)),
             (two_d(ab_sgu_b), two_d(g_sgu_b), two_d(m_ab_sgu_b), two_d(v_ab_sgu_b)),
             (c_pool_scale, g_scale, m_c_pool_scale, v_c_pool_scale),
             (final_g.reshape(1, D), g_final_g, m_final_g.reshape(1, D), v_final_g.reshape(1, D))]
    small_res = _adamw_small(small)
    small_shapes_out = [norm_g.shape, ada_b.shape, ab_conv_w.shape, ab_ln_g.shape, ab_ln_b.shape, ab_sgu_w.shape,
                        ab_sgu_b.shape, c_pool_scale.shape, final_g.shape]
    (r_norm_g, r_ada_b, r_conv, r_ln_g, r_ln_b, r_sgu_w, r_sgu_b, r_scale, r_final_g) = [
        tuple(a.reshape(shp) for a in res) for res, shp in zip(small_res, small_shapes_out)]

    order = [r_norm_g, r_ada_w, r_ada_b, r_ab_w_in, r_conv, r_ln_g, r_ln_b, r_sgu_w, r_sgu_b, r_ab_w_out,
             r_c_w_in, r_c_pool_w, r_scale, r_c_w_out, r_final_g]
    outs = [loss, grad_x[None]]
    for field in range(4):
        outs += [r[field] for r in order]
    return tuple(outs)
```

```python
import functools

import jax
import jax.numpy as jnp
from jax import lax
from jax.experimental import pallas as pl
from jax.experimental.pallas import tpu as pltpu

f32, bf16 = jnp.float32, jnp.bfloat16

D = 1024
HEAD = 128
NH = 8
WINDOWS = (2, 4, 8, 16)
GC = 512
EPS = 1e-6
HALO_CONV = 8
HALO_POOL = 16
CHUNK_ROWS = 512
DH_WIDE = 1024
FWD_TILES = 2
N_DEV = 8
LANES = 128

ADAM_LR, ADAM_B1, ADAM_B2, ADAM_EPS, ADAM_WD, ADAM_STEP = 0.001, 0.9, 0.999, 1e-08, 0.01, 10

MESH = pl.DeviceIdType.MESH
ANY = pl.BlockSpec(memory_space=pl.ANY)
VMEM = pl.BlockSpec(memory_space=pltpu.VMEM)
MIB = 2 ** 20


def _pcall(body, *, name, out_shape, grid=None, in_specs=None, out_specs=None, scratch=(), vmem_mb=None,
           aliases=None, prefetch=0):
    kw = {}
    if prefetch:
        kw["grid_spec"] = pltpu.PrefetchScalarGridSpec(num_scalar_prefetch=prefetch, grid=grid, in_specs=in_specs,
                                                       out_specs=out_specs, scratch_shapes=list(scratch))
    else:
        if grid is not None:
            kw["grid"] = grid
        if in_specs is not None:
            kw["in_specs"] = in_specs
        if out_specs is not None:
            kw["out_specs"] = out_specs
        if scratch:
            kw["scratch_shapes"] = list(scratch)
    if aliases:
        kw["input_output_aliases"] = aliases
    params = pltpu.CompilerParams(vmem_limit_bytes=None if vmem_mb is None else vmem_mb * MIB)
    return pl.pallas_call(body, name=name, out_shape=out_shape, compiler_params=params, **kw)


def _sds(shape, dtype):
    return jax.ShapeDtypeStruct(tuple(shape), dtype)


def _sigmoid(z):
    return pl.reciprocal(1.0 + jnp.exp(-z), approx=True)


def _silu(z):
    return z * _sigmoid(z)


def _silu_and_grad(z):
    s = _sigmoid(z)
    return z * s, s * (1.0 + z * (1.0 - s))


def _place():
    return lax.axis_index("x"), lax.axis_index("y"), lax.axis_index("c")


def _gather8(blk, name, after=()):
    def body(x_ref, *rest):
        o_ref, ssem, rsem = rest[len(after):]
        x, y, c = _place()
        me = 4 * x + 2 * y + c
        o_ref[me] = x_ref[...]
        sends = []
        for k in range(1, N_DEV):
            px = 1 - x if k & 4 else x
            py = 1 - y if k & 2 else y
            pc = 1 - c if k & 1 else c
            cp = pltpu.make_async_remote_copy(src_ref=x_ref, dst_ref=o_ref.at[me], send_sem=ssem.at[k - 1],
                                              recv_sem=rsem.at[k - 1], device_id=(px, py, pc), device_id_type=MESH)
            cp.start()
            sends.append((cp, 4 * px + 2 * py + pc))
        for k, (cp, peer) in enumerate(sends):
            pltpu.make_async_remote_copy(src_ref=x_ref, dst_ref=o_ref.at[peer], send_sem=ssem.at[k],
                                         recv_sem=rsem.at[k], device_id=(x, y, c), device_id_type=MESH).wait_recv()
        for cp, _ in sends:
            cp.wait_send()

    return _pcall(body, name=name, out_shape=_sds((N_DEV,) + blk.shape, blk.dtype), in_specs=[VMEM] + [ANY] * len(after),
                  out_specs=VMEM,
                  scratch=[pltpu.SemaphoreType.DMA((N_DEV - 1,)), pltpu.SemaphoreType.DMA((N_DEV - 1,))])(blk, *after)


def _allreduce8(bufs, name, after=None):
    n, n_after = len(bufs), 0 if after is None else 1
    rbs = [b.shape[0] // N_DEV for b in bufs]
    assert all(rb * N_DEV == b.shape[0] and rb % 8 == 0 for rb, b in zip(rbs, bufs))

    def body(*refs):
        refs = refs[:n] + refs[n + n_after:]
        xs, outs, stages = refs[:n], refs[n:2 * n], refs[2 * n:3 * n]
        ssem, rsem = refs[3 * n:]
        x, y, c = _place()
        me = 4 * x + 2 * y + c
        peers = []
        for k in range(1, N_DEV):
            px = 1 - x if k & 4 else x
            py = 1 - y if k & 2 else y
            pc = 1 - c if k & 1 else c
            peers.append(((px, py, pc), 4 * px + 2 * py + pc))

        def blk(t, ref, idx):
            return ref.at[pl.ds(pl.multiple_of(idx * rbs[t], 8), rbs[t]), :]

        def copy(t, phase, k, src, dst, dev):
            return pltpu.make_async_remote_copy(src_ref=src, dst_ref=dst, send_sem=ssem.at[t, phase, k],
                                                recv_sem=rsem.at[t, phase, k], device_id=dev, device_id_type=MESH)

        scatter = [copy(t, 0, k, blk(t, xs[t], pidx), stages[t].at[me], dev)
                   for t in range(n) for k, (dev, pidx) in enumerate(peers)]
        for cp in scatter:
            cp.start()
        gather = []
        for t in range(n):
            stages[t][me] = blk(t, xs[t], me)[...]
            for k, (dev, pidx) in enumerate(peers):
                copy(t, 0, k, blk(t, xs[t], pidx), stages[t].at[pidx], dev).wait_recv()
            total = stages[t][0]
            for j in range(1, N_DEV):
                total = total + stages[t][j]
            blk(t, outs[t], me)[...] = total
            sends = [copy(t, 1, k, blk(t, outs[t], me), blk(t, outs[t], me), dev) for k, (dev, pidx) in enumerate(peers)]
            for cp in sends:
                cp.start()
            gather += sends
        for t in range(n):
            for k, (dev, pidx) in enumerate(peers):
                copy(t, 1, k, blk(t, outs[t], pidx), blk(t, outs[t], pidx), dev).wait_recv()
        for cp in scatter + gather:
            cp.wait_send()

    return _pcall(body, name=name, out_shape=[_sds(b.shape, f32) for b in bufs], in_specs=[VMEM] * n + [ANY] * n_after,
                  out_specs=[VMEM] * n,
                  scratch=[pltpu.VMEM((N_DEV, rb, LANES), f32) for rb in rbs]
                  + [pltpu.SemaphoreType.DMA((n, 2, N_DEV - 1)), pltpu.SemaphoreType.DMA((n, 2, N_DEV - 1))])(
                      *bufs, *([] if after is None else [after]))


def _other_chips(x, y):
    return [((1 - x, y), 2 * (1 - x) + y), ((x, 1 - y), 2 * x + (1 - y)), ((1 - x, 1 - y), 2 * (1 - x) + (1 - y))]


HBM = pl.BlockSpec(memory_space=pltpu.HBM)
SEM = pl.BlockSpec(memory_space=pltpu.SEMAPHORE)
EFFECT = pltpu.SideEffectType.DATAFLOW_SIDE_EFFECTING


def _in_hbm(a):
    return pltpu.with_memory_space_constraint(a, pltpu.HBM)


def _ag_start(layers, after, name):
    flat = [t for lay in layers for t in lay]
    n, nl = len(flat), len(layers)

    def body(*refs):
        src = refs[:n]
        sems = refs[n + 1:n + 1 + 2 * nl]
        token = refs[-1]
        x, y, c = _place()
        s_me = 2 * x + y
        t = 0
        for i, lay in enumerate(layers):
            for k in range(len(lay)):
                for j, ((px, py), _) in enumerate(_other_chips(x, y)):
                    pltpu.make_async_remote_copy(src_ref=src[t].at[s_me, c], dst_ref=src[t].at[s_me, c],
                                                 send_sem=sems[2 * i].at[3 * k + j], recv_sem=sems[2 * i + 1].at[3 * k + j],
                                                 device_id=(px, py, c), device_id_type=MESH).start()
                t += 1
        token[...] = jnp.zeros_like(token)

    sem_shapes = [pltpu.SemaphoreType.DMA((3 * len(lay),)) for lay in layers for _ in range(2)]
    out_shape = sem_shapes + [pltpu.HBM(t.shape, t.dtype) for t in flat] + [_sds((8, LANES), f32)]
    outs = pl.pallas_call(
        body, name=name, out_shape=out_shape, in_specs=[HBM] * n + [ANY],
        out_specs=[SEM] * (2 * nl) + [HBM] * n + [VMEM], input_output_aliases={t: 2 * nl + t for t in range(n)},
        compiler_params=pltpu.CompilerParams(has_side_effects=EFFECT))(*[_in_hbm(t) for t in flat], after)
    sems = [(outs[2 * i], outs[2 * i + 1]) for i in range(nl)]
    thru, t = [], 2 * nl
    for lay in layers:
        thru.append(list(outs[t:t + len(lay)]))
        t += len(lay)
    return sems, thru, outs[-1]


def _ag_wait(inflight, sems, after, name):
    n = len(inflight)

    def body(*refs):
        src, ssem, rsem = refs[:n], refs[n], refs[n + 1]
        x, y, c = _place()
        s_me = 2 * x + y
        for k in range(n):
            for j, (_, s_p) in enumerate(_other_chips(x, y)):
                cp = pltpu.make_async_remote_copy(src_ref=src[k].at[s_me, c], dst_ref=src[k].at[s_p, c],
                                                  send_sem=ssem.at[3 * k + j], recv_sem=rsem.at[3 * k + j],
                                                  device_id=(x, y, c), device_id_type=MESH)
                cp.wait_send()
                cp.wait_recv()

    return pl.pallas_call(
        body, name=name, out_shape=[pltpu.HBM(t.shape, t.dtype) for t in inflight],
        in_specs=[HBM] * n + [SEM, SEM, ANY], out_specs=[HBM] * n, input_output_aliases={t: t for t in range(n)},
        compiler_params=pltpu.CompilerParams(has_side_effects=EFFECT))(*inflight, sems[0], sems[1], after)


def _ag_forward(arrived, name):
    n = len(arrived)

    def body(*refs):
        o = refs[n:2 * n]
        ssem, rsem = refs[2 * n:]
        x, y, c = _place()

        def copy(t, j, s, half, dev):
            return pltpu.make_async_remote_copy(src_ref=o[t].at[s, c], dst_ref=o[t].at[s, half], send_sem=ssem.at[t, j],
                                                recv_sem=rsem.at[t, j], device_id=dev, device_id_type=MESH)

        chips = _other_chips(x, y)
        sends = [copy(t, j, s_p, c, (x, y, 1 - c)) for t in range(n) for j, (_, s_p) in enumerate(chips)]
        for cp in sends:
            cp.start()
        for t in range(n):
            for j, (_, s_p) in enumerate(chips):
                copy(t, j, s_p, 1 - c, (x, y, c)).wait_recv()
        for cp in sends:
            cp.wait_send()

    return _pcall(body, name=name, out_shape=[_sds(p.shape, bf16) for p in arrived], in_specs=[ANY] * n,
                  out_specs=[ANY] * n, aliases={t: t for t in range(n)},
                  scratch=[pltpu.SemaphoreType.DMA((n, 3)), pltpu.SemaphoreType.DMA((n, 3))])(*arrived)


def _agf_start(arrived, name):
    n = len(arrived)

    def body(*refs):
        o = refs[:n]
        ssem, rsem, token = refs[n], refs[n + 1], refs[-1]
        x, y, c = _place()
        for t in range(n):
            for j, (_, s_p) in enumerate(_other_chips(x, y)):
                pltpu.make_async_remote_copy(src_ref=o[t].at[s_p, c], dst_ref=o[t].at[s_p, c],
                                             send_sem=ssem.at[3 * t + j], recv_sem=rsem.at[3 * t + j],
                                             device_id=(x, y, 1 - c), device_id_type=MESH).start()
        token[...] = jnp.zeros_like(token)

    out_shape = ([pltpu.SemaphoreType.DMA((3 * n,))] * 2 + [pltpu.HBM(a.shape, bf16) for a in arrived]
                 + [_sds((8, LANES), f32)])
    outs = pl.pallas_call(
        body, name=name, out_shape=out_shape, in_specs=[HBM] * n, out_specs=[SEM, SEM] + [HBM] * n + [VMEM],
        input_output_aliases={t: 2 + t for t in range(n)},
        compiler_params=pltpu.CompilerParams(has_side_effects=EFFECT))(*[_in_hbm(a) for a in arrived])
    return (outs[0], outs[1]), list(outs[2:2 + n]), outs[-1]


def _agf_wait(sems, inflight, after, name):
    n = len(inflight)

    def body(*refs):
        o, ssem, rsem = refs[:n], refs[n], refs[n + 1]
        x, y, c = _place()
        for t in range(n):
            for j, (_, s_p) in enumerate(_other_chips(x, y)):
                cp = pltpu.make_async_remote_copy(src_ref=o[t].at[s_p, c], dst_ref=o[t].at[s_p, 1 - c],
                                                  send_sem=ssem.at[3 * t + j], recv_sem=rsem.at[3 * t + j],
                                                  device_id=(x, y, c), device_id_type=MESH)
                cp.wait_send()
                cp.wait_recv()

    return pl.pallas_call(
        body, name=name, out_shape=[pltpu.HBM(a.shape, bf16) for a in inflight],
        in_specs=[HBM] * n + [SEM, SEM, ANY], out_specs=[HBM] * n, input_output_aliases={t: t for t in range(n)},
        compiler_params=pltpu.CompilerParams(has_side_effects=EFFECT))(*inflight, sems[0], sems[1], after)


def _rs_pair_start(grads, name):
    n = len(grads)

    def body(*refs):
        g, theirs = refs[:n], refs[n:2 * n]
        ssem, rsem, token = refs[2 * n], refs[2 * n + 1], refs[-1]
        x, y, c = _place()
        for t in range(n):
            pltpu.make_async_remote_copy(src_ref=g[t].at[:, 1 - c], dst_ref=theirs[t], send_sem=ssem.at[t],
                                         recv_sem=rsem.at[t], device_id=(x, y, 1 - c), device_id_type=MESH).start()
        token[...] = jnp.zeros_like(token)

    lands = [lax.empty((4,) + g.shape[2:], bf16) for g in grads]
    out_shape = ([pltpu.SemaphoreType.DMA((n,))] * 2 + [pltpu.HBM(g.shape, bf16) for g in grads]
                 + [pltpu.HBM(q.shape, bf16) for q in lands] + [_sds((8, LANES), f32)])
    outs = pl.pallas_call(
        body, name=name, out_shape=out_shape, in_specs=[HBM] * (2 * n), out_specs=[SEM, SEM] + [HBM] * (2 * n) + [VMEM],
        input_output_aliases={t: 2 + t for t in range(2 * n)},
        compiler_params=pltpu.CompilerParams(has_side_effects=EFFECT))(*[_in_hbm(a) for a in list(grads) + lands])
    return (outs[0], outs[1]), list(outs[2:2 + n]), list(outs[2 + n:2 + 2 * n]), outs[-1]


def _rs_pair_wait(sems, grads, lands, after, name):
    n = len(grads)

    def body(*refs):
        g, theirs = refs[:n], refs[n:2 * n]
        ssem, rsem = refs[2 * n], refs[2 * n + 1]
        x, y, c = _place()
        for t in range(n):
            cp = pltpu.make_async_remote_copy(src_ref=g[t].at[:, 1 - c], dst_ref=theirs[t], send_sem=ssem.at[t],
                                              recv_sem=rsem.at[t], device_id=(x, y, c), device_id_type=MESH)
            cp.wait_send()
            cp.wait_recv()

    outs = pl.pallas_call(
        body, name=name, out_shape=[pltpu.HBM(a.shape, bf16) for a in list(grads) + list(lands)],
        in_specs=[HBM] * (2 * n) + [SEM, SEM, ANY], out_specs=[HBM] * (2 * n),
        input_output_aliases={t: t for t in range(2 * n)},
        compiler_params=pltpu.CompilerParams(has_side_effects=EFFECT))(*grads, *lands, sems[0], sems[1], after)
    return list(outs[:n]), list(outs[n:])


def _rs_chip_start(pairs, name):
    n = len(pairs)

    def body(*refs):
        p, q = refs[:n], refs[n:2 * n]
        ssem, rsem, token = refs[2 * n], refs[2 * n + 1], refs[-1]
        x, y, c = _place()
        for t in range(n):
            for j, ((px, py), s_p) in enumerate(_other_chips(x, y)):
                pltpu.make_async_remote_copy(src_ref=p[t].at[s_p], dst_ref=q[t].at[j], send_sem=ssem.at[3 * t + j],
                                             recv_sem=rsem.at[3 * t + j], device_id=(px, py, c), device_id_type=MESH).start()
        token[...] = jnp.zeros_like(token)

    lands = [lax.empty((3,) + p.shape[1:], bf16) for p in pairs]
    out_shape = ([pltpu.SemaphoreType.DMA((3 * n,))] * 2 + [pltpu.HBM(p.shape, bf16) for p in pairs]
                 + [pltpu.HBM(q.shape, bf16) for q in lands] + [_sds((8, LANES), f32)])
    outs = pl.pallas_call(
        body, name=name, out_shape=out_shape, in_specs=[HBM] * (2 * n), out_specs=[SEM, SEM] + [HBM] * (2 * n) + [VMEM],
        input_output_aliases={t: 2 + t for t in range(2 * n)},
        compiler_params=pltpu.CompilerParams(has_side_effects=EFFECT))(*[_in_hbm(a) for a in list(pairs) + lands])
    return (outs[0], outs[1]), list(outs[2:2 + n]), list(outs[2 + n:2 + 2 * n]), outs[-1]


def _rs_chip_wait(sems, pairs, lands, after, name):
    n = len(pairs)

    def body(*refs):
        p, q = refs[:n], refs[n:2 * n]
        ssem, rsem = refs[2 * n], refs[2 * n + 1]
        x, y, c = _place()
        for t in range(n):
            for j, (_, s_p) in enumerate(_other_chips(x, y)):
                cp = pltpu.make_async_remote_copy(src_ref=p[t].at[s_p], dst_ref=q[t].at[j], send_sem=ssem.at[3 * t + j],
                                                  recv_sem=rsem.at[3 * t + j], device_id=(x, y, c), device_id_type=MESH)
                cp.wait_send()
                cp.wait_recv()

    outs = pl.pallas_call(
        body, name=name, out_shape=[pltpu.HBM(a.shape, bf16) for a in list(pairs) + list(lands)],
        in_specs=[HBM] * (2 * n) + [SEM, SEM, ANY], out_specs=[HBM] * (2 * n),
        input_output_aliases={t: t for t in range(2 * n)},
        compiler_params=pltpu.CompilerParams(has_side_effects=EFFECT))(*pairs, *lands, sems[0], sems[1], after)
    return list(outs[:n]), list(outs[n:])


def _rs_half_start(halves, name):
    n = len(halves)

    def body(*refs):
        o = refs[:n]
        ssem, rsem, token = refs[n], refs[n + 1], refs[-1]
        x, y, c = _place()
        for t in range(n):
            pltpu.make_async_remote_copy(src_ref=o[t].at[c], dst_ref=o[t].at[c], send_sem=ssem.at[t],
                                         recv_sem=rsem.at[t], device_id=(x, y, 1 - c), device_id_type=MESH).start()
        token[...] = jnp.zeros_like(token)

    out_shape = ([pltpu.SemaphoreType.DMA((n,))] * 2 + [pltpu.HBM(h.shape, h.dtype) for h in halves]
                 + [_sds((8, LANES), f32)])
    outs = pl.pallas_call(
        body, name=name, out_shape=out_shape, in_specs=[HBM] * n, out_specs=[SEM, SEM] + [HBM] * n + [VMEM],
        input_output_aliases={t: 2 + t for t in range(n)},
        compiler_params=pltpu.CompilerParams(has_side_effects=EFFECT))(*[_in_hbm(h) for h in halves])
    return (outs[0], outs[1]), list(outs[2:2 + n]), outs[-1]


def _rs_half_wait(sems, inflight, after, name):
    n = len(inflight)

    def body(*refs):
        o, ssem, rsem = refs[:n], refs[n], refs[n + 1]
        x, y, c = _place()
        for t in range(n):
            cp = pltpu.make_async_remote_copy(src_ref=o[t].at[c], dst_ref=o[t].at[1 - c], send_sem=ssem.at[t],
                                              recv_sem=rsem.at[t], device_id=(x, y, c), device_id_type=MESH)
            cp.wait_send()
            cp.wait_recv()

    return pl.pallas_call(
        body, name=name, out_shape=[pltpu.HBM(h.shape, h.dtype) for h in inflight],
        in_specs=[HBM] * n + [SEM, SEM, ANY], out_specs=[HBM] * n, input_output_aliases={t: t for t in range(n)},
        compiler_params=pltpu.CompilerParams(has_side_effects=EFFECT))(*inflight, sems[0], sems[1], after)


def _row_spec(tm, cols):
    return pl.BlockSpec((tm, cols), lambda i: (i, 0))


def _vec_spec(cols, rows=1):
    return pl.BlockSpec((rows, cols), lambda i: (0, 0))


def _modulated_norm(xv, g, shift, scale):
    r = lax.rsqrt(jnp.mean(xv * xv, axis=-1, keepdims=True) + EPS)
    return (((xv * r) * g) * (1.0 + scale) + shift).astype(bf16)


def _hnorm(x, g, shift, scale):
    T, tm = x.shape[0], 256

    def body(x_ref, g_ref, sh_ref, sc_ref, h_ref):
        h_ref[...] = _modulated_norm(x_ref[...], g_ref[...], sh_ref[...], sc_ref[...])

    return _pcall(body, name="hnorm", out_shape=_sds((T, D), bf16), grid=(T // tm,),
                  in_specs=[_row_spec(tm, D), _vec_spec(D), _vec_spec(D), _vec_spec(D)],
                  out_specs=_row_spec(tm, D))(x, g, shift, scale)


def _out_proj(y2, wo, x, gate, nxt=None):
    T, tm = x.shape[0], 512

    def body(y_ref, w_ref, x_ref, g_ref, *rest):
        o = jnp.dot(y_ref[0], w_ref[0], preferred_element_type=f32)
        o = o + jnp.dot(y_ref[1], w_ref[1], preferred_element_type=f32)
        xo = x_ref[...] + g_ref[...] * o
        if nxt is None:
            xo_ref, o_ref = rest
        else:
            ng_ref, nsh_ref, nsc_ref, xo_ref, o_ref, h_ref = rest
            h_ref[...] = _modulated_norm(xo, ng_ref[...], nsh_ref[...], nsc_ref[...])
        o_ref[...] = o.astype(bf16)
        xo_ref[...] = xo

    extra = [] if nxt is None else list(nxt)
    n_out = 2 if nxt is None else 3
    return _pcall(body, name="out_proj", out_shape=[_sds((T, D), f32), _sds((T, D), bf16), _sds((T, D), bf16)][:n_out],
                  grid=(T // tm,),
                  in_specs=[pl.BlockSpec((2, tm, D), lambda i: (0, i, 0)), pl.BlockSpec((2, D, D), lambda i: (0, 0, 0)),
                            _row_spec(tm, D), _vec_spec(D)] + [_vec_spec(D)] * len(extra),
                  out_specs=[_row_spec(tm, D)] * n_out, vmem_mb=40)(y2, wo, x, gate, *extra)


def _gate_bwd_tile(dx, o_ref, gate_ref, dob_ref, dgate_ref):
    dob_ref[...] = (dx * gate_ref[...]).astype(bf16)
    dgate_ref[...] += jnp.sum(dx * o_ref[...].astype(f32), axis=0, keepdims=True)


def _loss_bwd(x, target, g, o, gate):
    T, tm = x.shape[0], 512

    def body(x_ref, t_ref, g_ref, o_ref, gate_ref, dx_ref, loss_ref, dg_ref, dob_ref, dgate_ref):
        @pl.when(pl.program_id(0) == 0)
        def _():
            loss_ref[...] = jnp.zeros_like(loss_ref)
            dg_ref[...] = jnp.zeros_like(dg_ref)
            dgate_ref[...] = jnp.zeros_like(dgate_ref)

        xv, gv = x_ref[...], g_ref[...]
        r = lax.rsqrt(jnp.mean(xv * xv, axis=-1, keepdims=True) + EPS)
        xn = xv * r
        err = xn * gv - t_ref[...]
        dy = err * (1.0 / D)
        dxn = dy * gv
        dx = r * (dxn - xn * jnp.mean(dxn * xn, axis=-1, keepdims=True))
        dx_ref[...] = dx
        dg_ref[...] += jnp.sum(dy * xn, axis=0, keepdims=True)
        loss_ref[...] += (0.5 / D) * jnp.sum(jnp.sum(err * err, axis=1, keepdims=True), axis=0, keepdims=True)
        _gate_bwd_tile(dx, o_ref, gate_ref, dob_ref, dgate_ref)

    return _pcall(body, name="loss_bwd",
                  out_shape=[_sds((T, D), f32), _sds((1, 1), f32), _sds((1, D), f32), _sds((T, D), bf16), _sds((1, D), f32)],
                  grid=(T // tm,),
                  in_specs=[_row_spec(tm, D), _row_spec(tm, D), _vec_spec(D), _row_spec(tm, D), _vec_spec(D)],
                  out_specs=[_row_spec(tm, D), pl.BlockSpec((1, 1), lambda i: (0, 0)), _vec_spec(D), _row_spec(tm, D),
                             _vec_spec(D)])(x, target, g, o, gate)


def _norm_bwd(x, dh, gin, g, scale, below=None):
    T, tm = x.shape[0], 512

    def body(x_ref, dh_ref, gin_ref, g_ref, sc_ref, *rest):
        if below is None:
            dx_ref, st_ref = rest
        else:
            o_ref, gate_ref, dx_ref, st_ref, dob_ref, dgate_ref = rest

        @pl.when(pl.program_id(0) == 0)
        def _():
            st_ref[...] = jnp.zeros_like(st_ref)
            if below is not None:
                dgate_ref[...] = jnp.zeros_like(dgate_ref)

        xv, gv, dhv = x_ref[...], g_ref[...], dh_ref[...]
        r = lax.rsqrt(jnp.mean(xv * xv, axis=-1, keepdims=True) + EPS)
        xn = xv * r
        da = dhv * (1.0 + sc_ref[...])
        dxn = da * gv
        dx = gin_ref[...] + r * (dxn - xn * jnp.mean(dxn * xn, axis=-1, keepdims=True))
        dx_ref[...] = dx
        st_ref[0:1, :] += jnp.sum(dhv, axis=0, keepdims=True)
        st_ref[1:2, :] += jnp.sum(dhv * (xn * gv), axis=0, keepdims=True)
        st_ref[2:3, :] += jnp.sum(da * xn, axis=0, keepdims=True)
        if below is not None:
            _gate_bwd_tile(dx, o_ref, gate_ref, dob_ref, dgate_ref)

    out_shape = [_sds((T, D), f32), _sds((8, D), f32)]
    in_specs = [_row_spec(tm, D), _row_spec(tm, D), _row_spec(tm, D), _vec_spec(D), _vec_spec(D)]
    out_specs = [_row_spec(tm, D), _vec_spec(D, 8)]
    args = [x, dh, gin, g, scale]
    if below is not None:
        out_shape += [_sds((T, D), bf16), _sds((1, D), f32)]
        in_specs += [_row_spec(tm, D), _vec_spec(D)]
        out_specs += [_row_spec(tm, D), _vec_spec(D)]
        args += list(below)
    return _pcall(body, name="norm_bwd", out_shape=out_shape, grid=(T // tm,), in_specs=in_specs,
                  out_specs=out_specs)(*args)


STEPS = 4
ADAMW_STEPS = 8


def _cast_place(place, ws, layer, after=None):
    n = len(ws)

    def body(place_ref, *refs):
        for t in range(n):
            refs[-n + t][...] = refs[t][...].astype(bf16)

    def tile(w):
        return w.shape[1] // STEPS, w.shape[2]

    extra = [] if after is None else [after]
    return _pcall(body, name="cast_place", out_shape=[_sds((4,) + w.shape[1:], bf16) for w in ws], grid=(STEPS,),
                  prefetch=1,
                  in_specs=[pl.BlockSpec((None,) + tile(w), lambda i, pr: (layer, i, 0)) for w in ws] + [ANY] * len(extra),
                  out_specs=[pl.BlockSpec((None,) + tile(w), lambda i, pr: (pr[0], i, 0)) for w in ws])(
                      place, *ws, *extra)


def _rs_add(place, grads, theirs):
    n = len(grads)

    def body(place_ref, *refs):
        for t in range(n):
            refs[2 * n + t][...] = (refs[t][...].astype(f32) + refs[n + t][...].astype(f32)).astype(bf16)

    def tile(q):
        return q.shape[1] // 2, q.shape[2]

    mine = [pl.BlockSpec((None, None) + tile(q), lambda s, i, pr: (s, pr[1], i, 0)) for q in theirs]
    shard = [pl.BlockSpec((None,) + tile(q), lambda s, i, pr: (s, i, 0)) for q in theirs]
    return _pcall(body, name="rs_add", out_shape=[_sds(q.shape, bf16) for q in theirs], grid=(4, 2), prefetch=1,
                  in_specs=mine + shard, out_specs=shard)(place, *grads, *theirs)


def _rs_sum(place, pairs, slots):
    n, steps = len(pairs), 4

    def body(place_ref, *refs):
        for t in range(n):
            p_ref, q_ref = refs[t], refs[n + t]
            total = ((p_ref[...].astype(f32) + q_ref[0].astype(f32)) + q_ref[1].astype(f32)) + q_ref[2].astype(f32)
            refs[2 * n + t][...] = total.astype(bf16)

    def tile(q):
        return q.shape[1] // steps, q.shape[2]

    return _pcall(body, name="rs_sum", out_shape=[_sds((2,) + q.shape[1:], bf16) for q in slots], grid=(steps,),
                  prefetch=1,
                  in_specs=[pl.BlockSpec((None,) + tile(q), lambda i, pr: (pr[0], i, 0)) for q in slots]
                  + [pl.BlockSpec((3,) + tile(q), lambda i, pr: (0, i, 0)) for q in slots],
                  out_specs=[pl.BlockSpec((None,) + tile(q), lambda i, pr: (pr[1], i, 0)) for q in slots])(
                      place, *pairs, *slots)


def _adamw_math(w, g, m, v):
    m = ADAM_B1 * m + (1.0 - ADAM_B1) * g
    v = ADAM_B2 * v + (1.0 - ADAM_B2) * jnp.square(g)
    m_hat = m / (1.0 - ADAM_B1 ** ADAM_STEP)
    v_hat = v / (1.0 - ADAM_B2 ** ADAM_STEP)
    delta = -ADAM_LR * (m_hat / (jnp.sqrt(v_hat) + ADAM_EPS) + ADAM_WD * w)
    return delta, m, v


def _adamw_layer(layer, items):
    n = len(items)

    def body(*refs):
        outs = refs[-4 * n:]
        for t in range(n):
            w_ref, g_ref, m_ref, v_ref = refs[4 * t:4 * t + 4]
            g = g_ref[...].astype(f32)
            outs[4 * t][...] = g
            outs[4 * t + 1][...], outs[4 * t + 2][...], outs[4 * t + 3][...] = _adamw_math(
                w_ref[...], g, m_ref[...], v_ref[...])

    args, in_specs, out_specs, out_shape = [], [], [], []
    for w, g, m, v, _ in items:
        tr, cols = w.shape[1] // ADAMW_STEPS, w.shape[2]
        spec = pl.BlockSpec((None, tr, cols), lambda i: (layer, i, 0))
        args += [w, g, m, v]
        in_specs += [spec, pl.BlockSpec((tr, cols), lambda i: (i, 0)), spec, spec]
        out_specs += [spec] * 4
        out_shape += [_sds(w.shape, f32)] * 4
    aliases = {}
    for t, it in enumerate(items):
        if it[4] is not None:
            for k in range(4):
                aliases[len(args)] = 4 * t + k
                args.append(it[4][k])
                in_specs.append(ANY)
    res = _pcall(body, name="adamw", out_shape=out_shape, grid=(ADAMW_STEPS,), in_specs=in_specs, out_specs=out_specs,
                 aliases=aliases)(*args)
    return [tuple(res[4 * t:4 * t + 4]) for t in range(n)]


def _adamw_small(items):
    n = len(items)

    def body(*refs):
        ins, outs = refs[:4 * n], refs[4 * n:]
        for t in range(n):
            w_ref, g_ref, m_ref, v_ref = ins[4 * t:4 * t + 4]
            if len(g_ref.shape) == len(w_ref.shape) + 1:
                g = g_ref[0]
                for b in range(1, g_ref.shape[0]):
                    g = g + g_ref[b]
            else:
                g = g_ref[...]
            d, m, v = _adamw_math(w_ref[...], g, m_ref[...], v_ref[...])
            outs[4 * t][...], outs[4 * t + 1][...], outs[4 * t + 2][...], outs[4 * t + 3][...] = g, d, m, v

    out_shape = [_sds(w.shape, f32) for (w, _, _, _) in items for _ in range(4)]
    flat = [a for it in items for a in it]
    res = _pcall(body, name="adamw_small", out_shape=out_shape, in_specs=[VMEM] * (4 * n),
                 out_specs=[VMEM] * (4 * n))(*flat)
    return [tuple(res[4 * t:4 * t + 4]) for t in range(n)]


NN = ((1,), (0,))
NT = ((1,), (1,))
TN = ((0,), (0,))


def _mm(name, a, b, *, grid, a_spec, b_spec, out_shape, out_spec, dims, vmem_mb=None):
    def body(a_ref, b_ref, o_ref):
        r = lax.dot_general(a_ref[...], b_ref[...], (dims, ((), ())), preferred_element_type=f32)
        o_ref[...] = r.astype(o_ref.dtype)

    return _pcall(body, name=name, out_shape=out_shape, grid=grid, in_specs=[a_spec, b_spec], out_specs=out_spec,
                  vmem_mb=vmem_mb)(a, b)


def _whole(shape):
    return pl.BlockSpec(shape, lambda j: (0,) * len(shape))


def _split_spec(rows, tile, per_split):
    return pl.BlockSpec((None, rows, tile), lambda j: (j // per_split, 0, j % per_split))


class _Proj:
    def __init__(self, n, splits, tile):
        self.n, self.splits, self.tile = n, splits, tile
        self.steps = n // tile
        self.w_per = n // 4 // tile
        self.a_per = n // splits // tile
        assert self.w_per * tile * 4 == n and self.a_per * tile * splits == n

    def fwd(self, hb, wg):
        T = hb.shape[0]
        sub, tile, w_per = FWD_TILES, self.tile, self.w_per
        wide = sub * tile
        a_per = self.n // self.splits // wide
        assert a_per * wide * self.splits == self.n

        def w_tile(q):
            return pl.BlockSpec((None, D, tile), lambda j: ((sub * j + q) // w_per, 0, (sub * j + q) % w_per))

        def body(a_ref, *rest):
            w = jnp.concatenate([rest[q][...] for q in range(sub)], axis=1)
            rest[sub][...] = jnp.dot(a_ref[...], w, preferred_element_type=f32).astype(bf16)

        return _pcall(body, name="proj_fwd", out_shape=_sds((self.splits, T, self.n // self.splits), bf16),
                      grid=(self.n // wide,), in_specs=[_whole((T, D))] + [w_tile(q) for q in range(sub)],
                      out_specs=pl.BlockSpec((None, T, wide), lambda j: (j // a_per, 0, j % a_per)),
                      vmem_mb=40 if wide > 512 else None)(hb, *([wg] * sub))

    def dw(self, hb, dp):
        T = hb.shape[0]
        return _mm("proj_dw", hb, dp, grid=(self.steps,), a_spec=_whole((T, D)),
                   b_spec=_split_spec(T, self.tile, self.a_per), out_shape=_sds((4, D, self.n // 4), bf16),
                   out_spec=_split_spec(D, self.tile, self.w_per), dims=TN)

    def dh(self, dp, wg):
        T = dp.shape[1]
        sub, tile, w_per = DH_WIDE // self.tile, self.tile, self.w_per
        a_per = self.n // self.splits // DH_WIDE
        assert sub * tile == DH_WIDE and a_per * DH_WIDE * self.splits == self.n

        def w_tile(q):
            return pl.BlockSpec((None, D, tile), lambda k: ((sub * k + q) // w_per, 0, (sub * k + q) % w_per))

        def body(a_ref, *rest):
            o_ref = rest[sub]
            w = jnp.concatenate([rest[q][...] for q in range(sub)], axis=1)
            r = lax.dot_general(a_ref[...], w, (NT, ((), ())), preferred_element_type=f32)

            @pl.when(pl.program_id(0) == 0)
            def _():
                o_ref[...] = r

            @pl.when(pl.program_id(0) > 0)
            def _():
                o_ref[...] += r

        return _pcall(body, name="proj_dh", out_shape=_sds((T, D), f32), grid=(self.n // DH_WIDE,),
                      in_specs=[pl.BlockSpec((None, T, DH_WIDE), lambda k: (k // a_per, 0, k % a_per))]
                      + [w_tile(q) for q in range(sub)],
                      out_specs=_whole((T, D)), vmem_mb=40)(dp, *([wg] * sub))


EVEN_PROJ = _Proj(7 * D, 7, 256)
ODD_PROJ = _Proj(4 * D, 2, 512)


def _out_bwd(dob, wo, y2):
    T = dob.shape[0]
    w_spec = pl.BlockSpec((None, 512, D), lambda j: (j, 0, 0))

    def body(dob_ref, w_ref, y_ref, dy_ref, dw_ref):
        dob_v = dob_ref[...]
        dy_ref[...] = lax.dot_general(dob_v, w_ref[...], (NT, ((), ())), preferred_element_type=f32).astype(bf16)
        dw_ref[...] = lax.dot_general(y_ref[...], dob_v, (TN, ((), ())), preferred_element_type=f32).astype(bf16)

    return _pcall(body, name="out_bwd", out_shape=[_sds((2, T, D), bf16), _sds((4, 512, D), bf16)], grid=(4,),
                  in_specs=[_whole((T, D)), w_spec, _split_spec(T, 512, 2)],
                  out_specs=[_split_spec(T, 512, 2), w_spec])(dob, wo, y2)


def _head_spec(lead, T):
    return pl.BlockSpec((lead, T, HEAD), lambda h: (0, 0, h))


def _head_vec(rows):
    return pl.BlockSpec((rows, HEAD), lambda h: (0, h))


_HEAD_MAT = pl.BlockSpec((None, HEAD, HEAD), lambda h: (h, 0, 0))


def _causal():
    return lax.broadcasted_iota(jnp.int32, (HEAD, HEAD), 0) >= lax.broadcasted_iota(jnp.int32, (HEAD, HEAD), 1)


def _layernorm_head(v):
    mu = jnp.mean(v, axis=-1, keepdims=True)
    d = v - mu
    rstd = lax.rsqrt(jnp.mean(d * d, axis=-1, keepdims=True) + EPS)
    return d * rstd, rstd


def _even_fwd(p7, conv_w, ln_g, ln_b, sgu_w, sgu_bias):
    T, C = p7.shape[1], CHUNK_ROWS

    def body(p_ref, cw_ref, lg_ref, lb_ref, w_ref, b_ref, y_ref):
        w0, w1, w2 = cw_ref[0:1, :], cw_ref[1:2, :], cw_ref[2:3, :]
        wm = jnp.where(_causal(), w_ref[...], 0.0).astype(bf16)
        bias, lg, lb = b_ref[...], lg_ref[...], lb_ref[...]

        def step(i, halo):
            rows = pl.ds(pl.multiple_of(i * C, C), C)
            ah, ab, ac, az, u, v, zb = (p_ref[k, rows, :].astype(f32) for k in range(7))
            tt = ac * ah
            ext = jnp.concatenate([halo, tt], axis=0)
            cv = w2 * tt + w1 * pltpu.roll(ext, 1, 0)[HALO_CONV:] + w0 * pltpu.roll(ext, 2, 0)[HALO_CONV:]
            y_ref[0, rows, :] = (ab * cv * _silu(az)).astype(bf16)
            vhat, _ = _layernorm_head(v)
            vn = (vhat * lg + lb).astype(bf16)
            mix = jnp.concatenate([jnp.dot(wm, vn[k * HEAD:(k + 1) * HEAD], preferred_element_type=f32) + bias
                                   for k in range(C // HEAD)], axis=0)
            y_ref[1, rows, :] = (u * mix * _silu(zb)).astype(bf16)
            return tt[C - HALO_CONV:]

        lax.fori_loop(0, T // C, step, jnp.zeros((HALO_CONV, HEAD), f32))

    return _pcall(body, name="even_fwd", out_shape=_sds((2, T, D), bf16), grid=(NH,),
                  in_specs=[_head_spec(7, T), _head_vec(3), _head_vec(1), _head_vec(1), _HEAD_MAT, _HEAD_MAT],
                  out_specs=_head_spec(2, T))(p7, conv_w, ln_g, ln_b, sgu_w, sgu_bias)


def _even_bwd(p7, dy2, conv_w, ln_g, ln_b, sgu_w, sgu_bias):
    T, C = p7.shape[1], CHUNK_ROWS
    n_chunks = T // C

    def body(p_ref, dy_ref, cw_ref, lg_ref, lb_ref, w_ref, b_ref,
             dp_ref, dcw_ref, dlg_ref, dlb_ref, dw_ref, dms_ref, dcv_s):
        w0, w1, w2 = cw_ref[0:1, :], cw_ref[1:2, :], cw_ref[2:3, :]
        tri = _causal()
        wm = jnp.where(tri, w_ref[...], 0.0).astype(bf16)
        bias, lg, lb = b_ref[...], lg_ref[...], lb_ref[...]
        dw_ref[...] = jnp.zeros_like(dw_ref)
        dms_ref[...] = jnp.zeros_like(dms_ref)

        def fwd_step(i, carry):
            halo, a0, a1, a2, alg, alb = carry
            rows = pl.ds(pl.multiple_of(i * C, C), C)
            ah, ab, ac, az = (p_ref[k, rows, :].astype(f32) for k in range(4))
            dya = dy_ref[0, rows, :].astype(f32)
            tt = ac * ah
            ext = jnp.concatenate([halo, tt], axis=0)
            t1, t2 = pltpu.roll(ext, 1, 0)[HALO_CONV:], pltpu.roll(ext, 2, 0)[HALO_CONV:]
            cv = w2 * tt + w1 * t1 + w0 * t2
            sa, dsa = _silu_and_grad(az)
            g1 = dya * sa
            dp_ref[1, rows, :] = (g1 * cv).astype(bf16)
            dp_ref[3, rows, :] = (dya * ab * cv * dsa).astype(bf16)
            dcv = g1 * ab
            dcv_s[rows, :] = dcv
            a2 = a2 + jnp.sum(dcv * tt, axis=0, keepdims=True)
            a1 = a1 + jnp.sum(dcv * t1, axis=0, keepdims=True)
            a0 = a0 + jnp.sum(dcv * t2, axis=0, keepdims=True)

            u, zb, dyb = p_ref[4, rows, :].astype(f32), p_ref[6, rows, :].astype(f32), dy_ref[1, rows, :].astype(f32)
            vhat, rstd = _layernorm_head(p_ref[5, rows, :].astype(f32))
            vn = (vhat * lg + lb).astype(bf16)
            sb, dsb = _silu_and_grad(zb)
            mix = jnp.concatenate([jnp.dot(wm, vn[k * HEAD:(k + 1) * HEAD], preferred_element_type=f32) + bias
                                   for k in range(C // HEAD)], axis=0)
            dp_ref[4, rows, :] = (dyb * mix * sb).astype(bf16)
            dp_ref[6, rows, :] = (dyb * u * mix * dsb).astype(bf16)
            dmix = dyb * u * sb
            dvn_parts = []
            for k in range(C // HEAD):
                dm = dmix[k * HEAD:(k + 1) * HEAD]
                dmb = dm.astype(bf16)
                dvn_parts.append(lax.dot_general(wm, dmb, (TN, ((), ())), preferred_element_type=f32))
                dw_ref[...] += lax.dot_general(dmb, vn[k * HEAD:(k + 1) * HEAD], (NT, ((), ())),
                                               preferred_element_type=f32)
                dms_ref[...] += dm
            dvn = jnp.concatenate(dvn_parts, axis=0)
            alg = alg + jnp.sum(dvn * vhat, axis=0, keepdims=True)
            alb = alb + jnp.sum(dvn, axis=0, keepdims=True)
            dvh = dvn * lg
            dv = rstd * (dvh - jnp.mean(dvh, axis=-1, keepdims=True)
                         - vhat * jnp.mean(dvh * vhat, axis=-1, keepdims=True))
            dp_ref[5, rows, :] = dv.astype(bf16)
            return tt[C - HALO_CONV:], a0, a1, a2, alg, alb

        zrow = jnp.zeros((1, HEAD), f32)
        _, a0, a1, a2, alg, alb = lax.fori_loop(
            0, n_chunks, fwd_step, (jnp.zeros((HALO_CONV, HEAD), f32), zrow, zrow, zrow, zrow, zrow))
        dcw_ref[0:1, :], dcw_ref[1:2, :], dcw_ref[2:3, :] = a0, a1, a2
        dlg_ref[...], dlb_ref[...] = alg, alb
        dw_ref[...] = jnp.where(tri, dw_ref[...], 0.0)

        def bwd_step(k, halo):
            rows = pl.ds(pl.multiple_of((n_chunks - 1 - k) * C, C), C)
            dcv = dcv_s[rows, :]
            ext = jnp.concatenate([dcv, halo], axis=0)
            n1 = pltpu.roll(ext, C + HALO_CONV - 1, 0)[:C]
            n2 = pltpu.roll(ext, C + HALO_CONV - 2, 0)[:C]
            dtt = w2 * dcv + w1 * n1 + w0 * n2
            dp_ref[2, rows, :] = (dtt * p_ref[0, rows, :].astype(f32)).astype(bf16)
            dp_ref[0, rows, :] = (dtt * p_ref[2, rows, :].astype(f32)).astype(bf16)
            return dcv[:HALO_CONV]

        lax.fori_loop(0, n_chunks, bwd_step, jnp.zeros((HALO_CONV, HEAD), f32))

    out_shape = [_sds((7, T, D), bf16), _sds((3, D), f32), _sds((1, D), f32), _sds((1, D), f32),
                 _sds((NH, HEAD, HEAD), f32), _sds((NH, HEAD, HEAD), f32)]
    return _pcall(body, name="even_bwd", out_shape=out_shape, grid=(NH,),
                  in_specs=[_head_spec(7, T), _head_spec(2, T), _head_vec(3), _head_vec(1), _head_vec(1),
                            _HEAD_MAT, _HEAD_MAT],
                  out_specs=[_head_spec(7, T), _head_vec(3), _head_vec(1), _head_vec(1), _HEAD_MAT, _HEAD_MAT],
                  scratch=[pltpu.VMEM((T, HEAD), f32)])(p7, dy2, conv_w, ln_g, ln_b, sgu_w, sgu_bias)


def _window_sum(ext, win, towards_past):
    n, k, s = ext.shape[0], 1, ext
    while k < win:
        s = s + pltpu.roll(s, k if towards_past else n - k, 0)
        k *= 2
    return s


def _pool_count(i, C, win):
    t = i * C + lax.broadcasted_iota(jnp.int32, (C, 1), 0)
    cnt = jnp.minimum(t + 1, win).astype(f32)
    return cnt, 1.0 / cnt


def _group_specs(T):
    p_spec = pl.BlockSpec((None, T, GC), lambda g: (0, 0, g))
    z_spec = pl.BlockSpec((None, T, GC), lambda g: (1, 0, g))
    pw_spec = pl.BlockSpec((4, GC // 4, GC), lambda g: (0, g, 0))
    ps_spec = pl.BlockSpec((1, GC), lambda g: (0, g))
    y_spec = pl.BlockSpec((None, T, GC), lambda g: (g // 2, 0, g % 2))
    return p_spec, z_spec, pw_spec, ps_spec, y_spec


def _odd_fwd(p2, pool_wg, pool_scale):
    T, C = p2.shape[1], CHUNK_ROWS
    p_spec, z_spec, pw_spec, ps_spec, y_spec = _group_specs(T)

    def body(p_ref, z_ref, pw_ref, ps_ref, y_ref):
        pw, ps = pw_ref[...].reshape(GC, GC), ps_ref[...]

        def run(win):
            def step(i, halo):
                rows = pl.ds(pl.multiple_of(i * C, C), C)
                p = p_ref[rows, :].astype(f32)
                s = _window_sum(jnp.concatenate([halo, p], axis=0), win, True)[HALO_POOL:]
                pooled = s * _pool_count(i, C, win)[1] - p
                ypre = jnp.dot(pooled.astype(bf16), pw, preferred_element_type=f32)
                y_ref[rows, :] = (ypre * ps * _silu(z_ref[rows, :].astype(f32))).astype(bf16)
                return p[C - HALO_POOL:]

            lax.fori_loop(0, T // C, step, jnp.zeros((HALO_POOL, GC), f32))

        for gi, win in enumerate(WINDOWS):
            pl.when(pl.program_id(0) == gi)(functools.partial(run, win))

    return _pcall(body, name="odd_fwd", out_shape=_sds((2, T, D), bf16), grid=(len(WINDOWS),),
                  in_specs=[p_spec, z_spec, pw_spec, ps_spec], out_specs=y_spec)(p2, p2, pool_wg, pool_scale)


def _odd_bwd(p2, dy2, pool_wg, pool_scale):
    T, C = p2.shape[1], CHUNK_ROWS
    n_chunks = T // C
    p_spec, z_spec, pw_spec, ps_spec, y_spec = _group_specs(T)

    def body(p_ref, z_ref, dy_ref, pw_ref, ps_ref, dp_ref, dpw_ref, dps_ref, q_s, acc_s):
        pw, ps = pw_ref[...].reshape(GC, GC), ps_ref[...]

        def run(win):
            acc_s[...] = jnp.zeros_like(acc_s)

            def fwd_step(i, carry):
                halo, aps = carry
                rows = pl.ds(pl.multiple_of(i * C, C), C)
                p, z, dy = p_ref[rows, :].astype(f32), z_ref[rows, :].astype(f32), dy_ref[rows, :].astype(f32)
                _, inv_cnt = _pool_count(i, C, win)
                s = _window_sum(jnp.concatenate([halo, p], axis=0), win, True)[HALO_POOL:]
                pb = (s * inv_cnt - p).astype(bf16)
                ypre = jnp.dot(pb, pw, preferred_element_type=f32)
                sz, dsz = _silu_and_grad(z)
                aps = aps + jnp.sum(dy * ypre * sz, axis=0, keepdims=True)
                dp_ref[1, rows, :] = (dy * ypre * ps * dsz).astype(bf16)
                dyp = (dy * ps * sz).astype(bf16)
                acc_s[...] += lax.dot_general(pb, dyp, (TN, ((), ())), preferred_element_type=f32)
                dpool = lax.dot_general(dyp, pw, (NT, ((), ())), preferred_element_type=f32)
                q_s[rows, :] = dpool * inv_cnt
                return p[C - HALO_POOL:], aps

            _, aps = lax.fori_loop(0, n_chunks, fwd_step, (jnp.zeros((HALO_POOL, GC), f32), jnp.zeros((1, GC), f32)))
            dps_ref[...] = aps
            dpw_ref[...] = acc_s[...].reshape(4, GC // 4, GC).astype(bf16)

            def bwd_step(k, halo):
                i = n_chunks - 1 - k
                rows = pl.ds(pl.multiple_of(i * C, C), C)
                q = q_s[rows, :]
                s = _window_sum(jnp.concatenate([q, halo], axis=0), win, False)[:C]
                dp_ref[0, rows, :] = (s - q * _pool_count(i, C, win)[0]).astype(bf16)
                return q[:HALO_POOL]

            lax.fori_loop(0, n_chunks, bwd_step, jnp.zeros((HALO_POOL, GC), f32))

        for gi, win in enumerate(WINDOWS):
            pl.when(pl.program_id(0) == gi)(functools.partial(run, win))

    out_shape = [_sds((2, T, 2 * D), bf16), _sds((4, GC, GC), bf16), _sds((1, 2 * D), f32)]
    return _pcall(body, name="odd_bwd", out_shape=out_shape, grid=(len(WINDOWS),),
                  in_specs=[p_spec, z_spec, y_spec, pw_spec, ps_spec],
                  out_specs=[pl.BlockSpec((2, T, GC), lambda g: (0, 0, g)), pw_spec, ps_spec],
                  scratch=[pltpu.VMEM((T, GC), f32), pltpu.VMEM((GC, GC), f32)], vmem_mb=44)(
                      p2, p2, dy2, pool_wg, pool_scale)


def _ada_fwd(c_all, ada_w):
    cols = ada_w.shape[2]

    def body(c_ref, w_ref, o_ref):
        o_ref[...] = jnp.dot(_silu(c_ref[...]), w_ref[...], preferred_element_type=f32,
                             precision=lax.Precision.HIGHEST)

    return _pcall(body, name="ada_fwd", out_shape=_sds((4, N_DEV, cols), f32), grid=(4,),
                  in_specs=[pl.BlockSpec((N_DEV, D), lambda i: (0, 0)), pl.BlockSpec((None, D, cols), lambda i: (i, 0, 0))],
                  out_specs=pl.BlockSpec((None, N_DEV, cols), lambda i: (i, 0, 0)))(c_all, ada_w)


def _ada_bwd(c_all_t, dmod, w, m, v):
    cols, tr = w.shape[2], 256
    spec = pl.BlockSpec((None, tr, cols), lambda l, i: (l, i, 0))

    def body(c_ref, dm_ref, w_ref, m_ref, v_ref, g_ref, d_ref, mo_ref, vo_ref):
        sc = _silu(c_ref[...])
        g = sc[:, 0:1] * dm_ref[0:1, :]
        for b in range(1, N_DEV):
            g = g + sc[:, b:b + 1] * dm_ref[b:b + 1, :]
        g_ref[...] = g
        d_ref[...], mo_ref[...], vo_ref[...] = _adamw_math(w_ref[...], g, m_ref[...], v_ref[...])

    return _pcall(body, name="ada_bwd", out_shape=[_sds(w.shape, f32)] * 4, grid=(4, D // tr),
                  in_specs=[pl.BlockSpec((tr, N_DEV), lambda l, i: (i, 0)),
                            pl.BlockSpec((None, N_DEV, cols), lambda l, i: (l, 0, 0)), spec, spec, spec],
                  out_specs=[spec] * 4)(c_all_t, dmod, w, m, v)


def _layer_fwd(even, x, hb, gate, w, nxt, before_out=None):
    if even:
        w_in, w_out, conv_w, ln_g, ln_b, sgu_w, sgu_b = w
        bias = jnp.broadcast_to(sgu_b[:, :, None], (NH, HEAD, HEAD))
        p = EVEN_PROJ.fwd(hb, w_in)
        y2 = _even_fwd(p, conv_w, ln_g, ln_b, sgu_w, bias)
    else:
        w_in, pool_w, w_out, pool_scale = w
        p = ODD_PROJ.fwd(hb, w_in)
        y2 = _odd_fwd(p, pool_w, pool_scale)
    if before_out is not None:
        late_w_out, tok = before_out(y2)
        if late_w_out is not None:
            w_out = late_w_out
            w = (w_in, w_out) + tuple(w[2:]) if even else (w_in, pool_w, w_out, pool_scale)
        if tok is not None:
            gate = gate + tok[0:1, 0:1]
    outs = _out_proj(y2, w_out.reshape(2, D, D), x, gate, nxt)
    return outs[0], (None if nxt is None else outs[2]), (x, hb, p, y2, outs[1]), w


def _layer_bwd(even, gin, dob, dgate, saved, scale, g, w, below=None, send=None):
    x_in, hb, p, y2, o = saved
    if even:
        w_in, w_out, conv_w, ln_g, ln_b, sgu_w, sgu_b = w
        bias = jnp.broadcast_to(sgu_b[:, :, None], (NH, HEAD, HEAD))
        dy2, dwo = _out_bwd(dob, w_out, y2)
        dp, dconv, dlg, dlb, dsw, dms = _even_bwd(p, dy2, conv_w, ln_g, ln_b, sgu_w, bias)
        proj = EVEN_PROJ
        small = dict(conv_w=dconv, ln_g=dlg, ln_b=dlb, sgu_w=dsw, sgu_b=jnp.sum(dms, axis=-1))
        big = [proj.dw(hb, dp), dwo]
    else:
        w_in, pool_w, w_out, pool_scale = w
        dy2, dwo = _out_bwd(dob, w_out, y2)
        dp, dpw, dps = _odd_bwd(p, dy2, pool_w, pool_scale)
        proj = ODD_PROJ
        small = dict(pool_scale=dps)
        big = [proj.dw(hb, dp), dpw, dwo]
    if send is not None:
        big, tok = send(big)
        scale = scale + tok[0:1, 0:1]
    dh = proj.dh(dp, w_in)
    res = _norm_bwd(x_in, dh, gin, g, scale, below)
    stats = res[1]
    return (res[0], (None if below is None else (res[2], res[3])), big, small,
            jnp.concatenate([stats[0:2], dgate], axis=0), stats[2:3])


def _pack_rows(parts):
    rows = [p.reshape(-1, LANES) for p in parts]
    total = sum(r.shape[0] for r in rows)
    padded = -(-total // (8 * N_DEV)) * (8 * N_DEV)
    if padded > total:
        rows.append(jnp.zeros((padded - total, LANES), f32))
    return jnp.concatenate(rows, axis=0)


def _unpack_rows(buf, shapes):
    out, r = [], 0
    for shp in shapes:
        n = 1
        for d in shp:
            n *= d
        out.append(buf[r:r + n // LANES].reshape(shp))
        r += n // LANES
    return out


def kernel(x, c, norm_g, ada_w, ada_b, ab_w_in, ab_conv_w, ab_ln_g, ab_ln_b, ab_sgu_w, ab_sgu_b, ab_w_out, c_w_in, c_pool_w, c_pool_scale, c_w_out, final_g, loss_target, m_norm_g, m_ada_w, m_ada_b, m_ab_w_in, m_ab_conv_w, m_ab_ln_g, m_ab_ln_b, m_ab_sgu_w, m_ab_sgu_b, m_ab_w_out, m_c_w_in, m_c_pool_w, m_c_pool_scale, m_c_w_out, m_final_g, v_norm_g, v_ada_w, v_ada_b, v_ab_w_in, v_ab_conv_w, v_ab_ln_g, v_ab_ln_b, v_ab_sgu_w, v_ab_sgu_b, v_ab_w_out, v_c_w_in, v_c_pool_w, v_c_pool_scale, v_c_w_out, v_final_g):
    ix, iy, ic = _place()
    chip, dev = 2 * ix + iy, 4 * ix + 2 * iy + ic
    n_even, n_odd = ab_w_in.shape[0], c_w_in.shape[0]
    depth = n_even + n_odd
    acols = ada_w.shape[2]

    place = jnp.stack([chip, ic]).astype(jnp.int32)
    even_names, odd_names = ["ab_w_in", "ab_w_out"], ["c_w_in", "c_pool_w", "c_w_out"]
    params = {"ab_w_in": (ab_w_in, m_ab_w_in, v_ab_w_in), "ab_w_out": (ab_w_out, m_ab_w_out, v_ab_w_out),
              "c_w_in": (c_w_in, m_c_w_in, v_c_w_in), "c_w_out": (c_w_out, m_c_w_out, v_c_w_out),
              "c_pool_w": tuple(a.reshape(n_odd, GC, GC) for a in (c_pool_w, m_c_pool_w, v_c_pool_w))}

    def placed(names, layer, after=None):
        ws = [params[nm][0] for nm in names]
        return [p.reshape(4, 2, p.shape[1] // 2, p.shape[2]) for p in _cast_place(place, ws, layer, after)]

    def whole(arrays):
        return [g.reshape(4, 2 * g.shape[2], g.shape[3]) for g in arrays]

    first = _gather8(jnp.concatenate([c, ab_conv_w.reshape(1, -1), c_pool_scale.reshape(1, -1)], axis=1), "gather_c")
    c_all, small_all = first[:, 0, :D], first[0::2, 0, D:]
    sems_a, in_a, tok = _ag_start([placed(even_names[:1], 0)], first[0:1, 0, 0:LANES], "ag_start_0a")
    modp = _ada_fwd(c_all, ada_w)
    later = [placed(even_names[1:], 0, tok)]
    later += [placed(even_names if i % 2 == 0 else odd_names, i // 2, tok) for i in range(1, depth)]
    modg = _gather8(modp + tok[0:1, 0:1], "gather_mod", [lay[-1] for lay in later])
    mod_rows = lax.dynamic_index_in_dim(modg[0::2], dev, axis=2, keepdims=False)
    mod = jnp.transpose(mod_rows, (1, 0, 2)).reshape(depth, 3 * D) + ada_b
    mods = [(mod[i:i + 1, 0:D], mod[i:i + 1, D:2 * D], mod[i:i + 1, 2 * D:3 * D]) for i in range(depth)]

    def shard_cols(a, width):
        return lax.dynamic_slice_in_dim(a, chip * width, width, axis=a.ndim - 1)

    n_conv = ab_conv_w.size
    conv_all = small_all[:, :n_conv].reshape(4, n_even, 3, D // 4)
    conv_full = jnp.transpose(conv_all, (1, 2, 0, 3)).reshape(n_even, 3, D)
    scale_all = small_all[:, n_conv:].reshape(4, n_odd, 2 * D // 4)
    scale_full = jnp.transpose(scale_all, (1, 0, 2)).reshape(n_odd, 2 * D)

    gathers_done = mod[0:1, 0:LANES] + scale_full[0:1, 0:LANES]
    sems_b, in_b, tok = _ag_start(later[:1], gathers_done, "ag_start_0b")
    sems_r, in_r, tok = _ag_start(later[1:], tok, "ag_start_rest")

    x_cur, saved, weights, handoff = x[0], [], [], {}
    sems_f, in_f, tok = _agf_start(_ag_wait(in_a[0], sems_a[0], tok, "ag_wait_0a"), "agf_start_0")
    hb = _hnorm(x_cur, norm_g[0:1], mods[0][0] + tok[0:1, 0:1], mods[0][1])
    for i in range(depth):
        j = i // 2
        if i == 0:
            full = whole(_agf_wait(sems_f, in_f, hb, "agf_wait_0")) + [None]
        else:
            full = whole(_agf_wait(*handoff.pop(i), x_cur, f"agf_wait_{i}"))
        if i % 2 == 0:
            w = (full[0], full[1], conv_full[j], ab_ln_g[j:j + 1], ab_ln_b[j:j + 1], ab_sgu_w[j], ab_sgu_b[j])
        else:
            w = (full[0], full[1], full[2], scale_full[j:j + 1])

        def before_out(y2, i=i):
            w_out, tok = None, None
            if i == 0:
                w_out = whole(_ag_forward(_ag_wait(in_b[0], sems_b[0], y2, "ag_wait_0b"), "ag_forward"))[0]
            if i + 1 < depth:
                arrived = _ag_wait(in_r[i], sems_r[i], y2, f"ag_wait_{i + 1}")
                sems_f, inflight, tok = _agf_start(arrived, f"agf_start_{i + 1}")
                handoff[i + 1] = (sems_f, inflight)
            return w_out, tok

        nxt = (norm_g[i + 1:i + 2], mods[i + 1][0], mods[i + 1][1]) if i + 1 < depth else None
        x_cur, hb, sv, w = _layer_fwd(i % 2 == 0, x_cur, hb, mods[i][2], w, nxt, before_out)
        weights.append(w)
        saved.append(sv)
    gin, loss, dfinal_g, dob, dgate = _loss_bwd(x_cur, loss_target[0], final_g.reshape(1, D), saved[-1][4],
                                                mods[-1][2])

    stacked = {}

    def reduce_layer(i, sems, pairs, lands, after):
        pairs, slots = _rs_chip_wait(sems, pairs, lands, after, f"rs_chip_wait_{i}")
        half_sems, halves, _ = _rs_half_start(_rs_sum(place, pairs, slots), f"rs_half_start_{i}")
        return i, half_sems, halves

    def update_layer(i, half_sems, halves, after):
        names = even_names if i % 2 == 0 else odd_names
        grads = _rs_half_wait(half_sems, halves, after, f"rs_half_wait_{i}")
        items = [(params[nm][0], g.reshape(params[nm][0].shape[1:]), params[nm][1], params[nm][2], stacked.get(nm))
                 for nm, g in zip(names, grads)]
        for nm, res in zip(names, _adamw_layer(i // 2, items)):
            stacked[nm] = res

    small_g, dmod, dnorm_g, pending, tok = [None] * depth, [None] * depth, [None] * depth, None, None
    exchanging = []
    for i in reversed(range(depth)):
        w = weights[i]
        if tok is not None:
            w = w[:2] + (w[2] + tok[0:1, 0:1],) + w[3:] if i % 2 == 0 else w[:3] + (w[3] + tok[0:1, 0:1],)
        below = (saved[i - 1][4], mods[i - 1][2]) if i > 0 else None

        def send(big_g, i=i):
            if exchanging:
                update_layer(*exchanging.pop(), big_g[0])
            big_g = [g.reshape(4, 2, g.shape[1] // 2, g.shape[2]) for g in big_g]
            sems, big_g, lands, tok = _rs_pair_start(big_g, f"rs_pair_start_{i}")
            return (sems, big_g, lands), tok

        gin, gate_bwd, sent, small_g[i], dmod[i], dnorm_g[i] = _layer_bwd(
            i % 2 == 0, gin, dob, dgate, saved[i], mods[i][1], norm_g[i:i + 1], w, below, send)
        if below is not None:
            dob, dgate = gate_bwd
        after = gin
        if i == 0:
            dmod_all = _gather8(jnp.stack(dmod).reshape(depth * 3 * D // LANES, LANES), "gather_dmod")
            after = dmod_all = dmod_all.reshape(N_DEV, depth, 3 * D)
        big_g, theirs = _rs_pair_wait(*sent, after, f"rs_pair_wait_{i}")
        pairs = _rs_add(place, big_g, theirs)
        sems, pairs, lands, tok = _rs_chip_start(pairs, f"rs_chip_start_{i}")
        if pending is not None:
            exchanging.append(reduce_layer(*pending, tok))
        pending = (i, sems, pairs, lands)
    grad_x = gin
    dnorm_g = jnp.concatenate(dnorm_g, axis=0)

    dmod_cols = jnp.transpose(shard_cols(dmod_all, acols), (1, 0, 2))
    r_ada_w = _ada_bwd(c_all.T, dmod_cols, ada_w, m_ada_w, v_ada_w)
    update_layer(*exchanging.pop(), r_ada_w[1])
    last = reduce_layer(*pending, r_ada_w[1])

    small_parts = [dnorm_g, dfinal_g,
                   jnp.stack([small_g[2 * j]["conv_w"] for j in range(n_even)]),
                   jnp.concatenate([small_g[2 * j]["ln_g"] for j in range(n_even)], axis=0),
                   jnp.concatenate([small_g[2 * j]["ln_b"] for j in range(n_even)], axis=0),
                   jnp.stack([small_g[2 * j]["sgu_b"] for j in range(n_even)]),
                   jnp.concatenate([small_g[2 * j + 1]["pool_scale"] for j in range(n_odd)], axis=0),
                   jnp.pad(loss, ((0, 7), (0, LANES - 1)))]
    small_shapes = [p.shape for p in small_parts]
    sgu_parts = [small_g[2 * j]["sgu_w"].reshape(NH * HEAD, HEAD) for j in range(n_even)]
    reduced = _allreduce8([_pack_rows(small_parts)] + sgu_parts, "allreduce_small", last[2][0])
    update_layer(*last, reduced[0])
    r_ab_w_in, r_ab_w_out, r_c_w_in, r_c_w_out = (stacked[nm] for nm in ("ab_w_in", "ab_w_out", "c_w_in", "c_w_out"))
    r_c_pool_w = tuple(a.reshape(c_pool_w.shape) for a in stacked["c_pool_w"])
    g_norm_g, g_final_g, g_conv_full, g_ln_g, g_ln_b, g_sgu_b, g_scale_full, loss_row = _unpack_rows(reduced[0],
                                                                                                     small_shapes)
    g_sgu_w = jnp.stack(reduced[1:])
    loss = loss_row[0, 0]
    g_conv = shard_cols(g_conv_full, D // 4)
    g_scale = shard_cols(g_scale_full, 2 * D // 4)

    def two_d(a):
        return a.reshape(-1, a.shape[-1])

    small = [(norm_g, g_norm_g, m_norm_g, v_norm_g),
             (ada_b, dmod_all, m_ada_b, v_ada_b),
             (two_d(ab_conv_w), two_d(g_conv), two_d(m_ab_conv_w), two_d(v_ab_conv_w)),
             (ab_ln_g, g_ln_g, m_ab_ln_g, v_ab_ln_g),
             (ab_ln_b, g_ln_b, m_ab_ln_b, v_ab_ln_b),
             (two_d(ab_sgu_w), two_d(g_sgu_w), two_d(m_ab_sgu_w), two_d(v_ab_sgu_w)),
             (two_d(ab_sgu_b), two_d(g_sgu_b), two_d(m_ab_sgu_b), two_d(v_ab_sgu_b)),
             (c_pool_scale, g_scale, m_c_pool_scale, v_c_pool_scale),
             (final_g.reshape(1, D), g_final_g, m_final_g.reshape(1, D), v_final_g.reshape(1, D))]
    small_res = _adamw_small(small)
    small_shapes_out = [norm_g.shape, ada_b.shape, ab_conv_w.shape, ab_ln_g.shape, ab_ln_b.shape, ab_sgu_w.shape,
                        ab_sgu_b.shape, c_pool_scale.shape, final_g.shape]
    (r_norm_g, r_ada_b, r_conv, r_ln_g, r_ln_b, r_sgu_w, r_sgu_b, r_scale, r_final_g) = [
        tuple(a.reshape(shp) for a in res) for res, shp in zip(small_res, small_shapes_out)]

    order = [r_norm_g, r_ada_w, r_ada_b, r_ab_w_in, r_conv, r_ln_g, r_ln_b, r_sgu_w, r_sgu_b, r_ab_w_out,
             r_c_w_in, r_c_pool_w, r_scale, r_c_w_out, r_final_g]
    outs = [loss, grad_x[None]]
    for field in range(4):
        outs += [r[field] for r in order]
    return tuple(outs)
```

```python
import functools

import jax
import jax.numpy as jnp
from jax import lax
from jax.experimental import pallas as pl
from jax.experimental.pallas import tpu as pltpu

f32, bf16 = jnp.float32, jnp.bfloat16

D = 1024
HEAD = 128
NH = 8
WINDOWS = (2, 4, 8, 16)
GC = 512
EPS = 1e-6
HALO_CONV = 8
HALO_POOL = 16
CHUNK_ROWS = 512
DH_WIDE = 1024
FWD_TILES = 2
N_DEV = 8
LANES = 128

ADAM_LR, ADAM_B1, ADAM_B2, ADAM_EPS, ADAM_WD, ADAM_STEP = 0.001, 0.9, 0.999, 1e-08, 0.01, 10

MESH = pl.DeviceIdType.MESH
ANY = pl.BlockSpec(memory_space=pl.ANY)
VMEM = pl.BlockSpec(memory_space=pltpu.VMEM)
MIB = 2 ** 20


def _pcall(body, *, name, out_shape, grid=None, in_specs=None, out_specs=None, scratch=(), vmem_mb=None,
           aliases=None, prefetch=0):
    kw = {}
    if prefetch:
        kw["grid_spec"] = pltpu.PrefetchScalarGridSpec(num_scalar_prefetch=prefetch, grid=grid, in_specs=in_specs,
                                                       out_specs=out_specs, scratch_shapes=list(scratch))
    else:
        if grid is not None:
            kw["grid"] = grid
        if in_specs is not None:
            kw["in_specs"] = in_specs
        if out_specs is not None:
            kw["out_specs"] = out_specs
        if scratch:
            kw["scratch_shapes"] = list(scratch)
    if aliases:
        kw["input_output_aliases"] = aliases
    params = pltpu.CompilerParams(vmem_limit_bytes=None if vmem_mb is None else vmem_mb * MIB)
    return pl.pallas_call(body, name=name, out_shape=out_shape, compiler_params=params, **kw)


def _sds(shape, dtype):
    return jax.ShapeDtypeStruct(tuple(shape), dtype)


def _sigmoid(z):
    return pl.reciprocal(1.0 + jnp.exp(-z), approx=True)


def _silu(z):
    return z * _sigmoid(z)


def _silu_and_grad(z):
    s = _sigmoid(z)
    return z * s, s * (1.0 + z * (1.0 - s))


def _place():
    return lax.axis_index("x"), lax.axis_index("y"), lax.axis_index("c")


def _gather8(blk, name, after=()):
    def body(x_ref, *rest):
        o_ref, ssem, rsem = rest[len(after):]
        x, y, c = _place()
        me = 4 * x + 2 * y + c
        o_ref[me] = x_ref[...]
        sends = []
        for k in range(1, N_DEV):
            px = 1 - x if k & 4 else x
            py = 1 - y if k & 2 else y
            pc = 1 - c if k & 1 else c
            cp = pltpu.make_async_remote_copy(src_ref=x_ref, dst_ref=o_ref.at[me], send_sem=ssem.at[k - 1],
                                              recv_sem=rsem.at[k - 1], device_id=(px, py, pc), device_id_type=MESH)
            cp.start()
            sends.append((cp, 4 * px + 2 * py + pc))
        for k, (cp, peer) in enumerate(sends):
            pltpu.make_async_remote_copy(src_ref=x_ref, dst_ref=o_ref.at[peer], send_sem=ssem.at[k],
                                         recv_sem=rsem.at[k], device_id=(x, y, c), device_id_type=MESH).wait_recv()
        for cp, _ in sends:
            cp.wait_send()

    return _pcall(body, name=name, out_shape=_sds((N_DEV,) + blk.shape, blk.dtype), in_specs=[VMEM] + [ANY] * len(after),
                  out_specs=VMEM,
                  scratch=[pltpu.SemaphoreType.DMA((N_DEV - 1,)), pltpu.SemaphoreType.DMA((N_DEV - 1,))])(blk, *after)


def _allreduce8(bufs, name, after=None):
    n, n_after = len(bufs), 0 if after is None else 1
    rbs = [b.shape[0] // N_DEV for b in bufs]
    assert all(rb * N_DEV == b.shape[0] and rb % 8 == 0 for rb, b in zip(rbs, bufs))

    def body(*refs):
        refs = refs[:n] + refs[n + n_after:]
        xs, outs, stages = refs[:n], refs[n:2 * n], refs[2 * n:3 * n]
        ssem, rsem = refs[3 * n:]
        x, y, c = _place()
        me = 4 * x + 2 * y + c
        peers = []
        for k in range(1, N_DEV):
            px = 1 - x if k & 4 else x
            py = 1 - y if k & 2 else y
            pc = 1 - c if k & 1 else c
            peers.append(((px, py, pc), 4 * px + 2 * py + pc))

        def blk(t, ref, idx):
            return ref.at[pl.ds(pl.multiple_of(idx * rbs[t], 8), rbs[t]), :]

        def copy(t, phase, k, src, dst, dev):
            return pltpu.make_async_remote_copy(src_ref=src, dst_ref=dst, send_sem=ssem.at[t, phase, k],
                                                recv_sem=rsem.at[t, phase, k], device_id=dev, device_id_type=MESH)

        scatter = [copy(t, 0, k, blk(t, xs[t], pidx), stages[t].at[me], dev)
                   for t in range(n) for k, (dev, pidx) in enumerate(peers)]
        for cp in scatter:
            cp.start()
        gather = []
        for t in range(n):
            stages[t][me] = blk(t, xs[t], me)[...]
            for k, (dev, pidx) in enumerate(peers):
                copy(t, 0, k, blk(t, xs[t], pidx), stages[t].at[pidx], dev).wait_recv()
            total = stages[t][0]
            for j in range(1, N_DEV):
                total = total + stages[t][j]
            blk(t, outs[t], me)[...] = total
            sends = [copy(t, 1, k, blk(t, outs[t], me), blk(t, outs[t], me), dev) for k, (dev, pidx) in enumerate(peers)]
            for cp in sends:
                cp.start()
            gather += sends
        for t in range(n):
            for k, (dev, pidx) in enumerate(peers):
                copy(t, 1, k, blk(t, outs[t], pidx), blk(t, outs[t], pidx), dev).wait_recv()
        for cp in scatter + gather:
            cp.wait_send()

    return _pcall(body, name=name, out_shape=[_sds(b.shape, f32) for b in bufs], in_specs=[VMEM] * n + [ANY] * n_after,
                  out_specs=[VMEM] * n,
                  scratch=[pltpu.VMEM((N_DEV, rb, LANES), f32) for rb in rbs]
                  + [pltpu.SemaphoreType.DMA((n, 2, N_DEV - 1)), pltpu.SemaphoreType.DMA((n, 2, N_DEV - 1))])(
                      *bufs, *([] if after is None else [after]))


def _other_chips(x, y):
    return [((1 - x, y), 2 * (1 - x) + y), ((x, 1 - y), 2 * x + (1 - y)), ((1 - x, 1 - y), 2 * (1 - x) + (1 - y))]


HBM = pl.BlockSpec(memory_space=pltpu.HBM)
SEM = pl.BlockSpec(memory_space=pltpu.SEMAPHORE)
EFFECT = pltpu.SideEffectType.DATAFLOW_SIDE_EFFECTING


def _in_hbm(a):
    return pltpu.with_memory_space_constraint(a, pltpu.HBM)


def _ag_start(layers, after, name):
    flat = [t for lay in layers for t in lay]
    n, nl = len(flat), len(layers)

    def body(*refs):
        src = refs[:n]
        sems = refs[n + 1:n + 1 + 2 * nl]
        token = refs[-1]
        x, y, c = _place()
        s_me = 2 * x + y
        t = 0
        for i, lay in enumerate(layers):
            for k in range(len(lay)):
                for j, ((px, py), _) in enumerate(_other_chips(x, y)):
                    pltpu.make_async_remote_copy(src_ref=src[t].at[s_me, c], dst_ref=src[t].at[s_me, c],
                                                 send_sem=sems[2 * i].at[3 * k + j], recv_sem=sems[2 * i + 1].at[3 * k + j],
                                                 device_id=(px, py, c), device_id_type=MESH).start()
                t += 1
        token[...] = jnp.zeros_like(token)

    sem_shapes = [pltpu.SemaphoreType.DMA((3 * len(lay),)) for lay in layers for _ in range(2)]
    out_shape = sem_shapes + [pltpu.HBM(t.shape, t.dtype) for t in flat] + [_sds((8, LANES), f32)]
    outs = pl.pallas_call(
        body, name=name, out_shape=out_shape, in_specs=[HBM] * n + [ANY],
        out_specs=[SEM] * (2 * nl) + [HBM] * n + [VMEM], input_output_aliases={t: 2 * nl + t for t in range(n)},
        compiler_params=pltpu.CompilerParams(has_side_effects=EFFECT))(*[_in_hbm(t) for t in flat], after)
    sems = [(outs[2 * i], outs[2 * i + 1]) for i in range(nl)]
    thru, t = [], 2 * nl
    for lay in layers:
        thru.append(list(outs[t:t + len(lay)]))
        t += len(lay)
    return sems, thru, outs[-1]


def _ag_wait(inflight, sems, after, name):
    n = len(inflight)

    def body(*refs):
        src, ssem, rsem = refs[:n], refs[n], refs[n + 1]
        x, y, c = _place()
        s_me = 2 * x + y
        for k in range(n):
            for j, (_, s_p) in enumerate(_other_chips(x, y)):
                cp = pltpu.make_async_remote_copy(src_ref=src[k].at[s_me, c], dst_ref=src[k].at[s_p, c],
                                                  send_sem=ssem.at[3 * k + j], recv_sem=rsem.at[3 * k + j],
                                                  device_id=(x, y, c), device_id_type=MESH)
                cp.wait_send()
                cp.wait_recv()

    return pl.pallas_call(
        body, name=name, out_shape=[pltpu.HBM(t.shape, t.dtype) for t in inflight],
        in_specs=[HBM] * n + [SEM, SEM, ANY], out_specs=[HBM] * n, input_output_aliases={t: t for t in range(n)},
        compiler_params=pltpu.CompilerParams(has_side_effects=EFFECT))(*inflight, sems[0], sems[1], after)


def _ag_forward(arrived, name):
    n = len(arrived)

    def body(*refs):
        o = refs[n:2 * n]
        ssem, rsem = refs[2 * n:]
        x, y, c = _place()

        def copy(t, j, s, half, dev):
            return pltpu.make_async_remote_copy(src_ref=o[t].at[s, c], dst_ref=o[t].at[s, half], send_sem=ssem.at[t, j],
                                                recv_sem=rsem.at[t, j], device_id=dev, device_id_type=MESH)

        chips = _other_chips(x, y)
        sends = [copy(t, j, s_p, c, (x, y, 1 - c)) for t in range(n) for j, (_, s_p) in enumerate(chips)]
        for cp in sends:
            cp.start()
        for t in range(n):
            for j, (_, s_p) in enumerate(chips):
                copy(t, j, s_p, 1 - c, (x, y, c)).wait_recv()
        for cp in sends:
            cp.wait_send()

    return _pcall(body, name=name, out_shape=[_sds(p.shape, bf16) for p in arrived], in_specs=[ANY] * n,
                  out_specs=[ANY] * n, aliases={t: t for t in range(n)},
                  scratch=[pltpu.SemaphoreType.DMA((n, 3)), pltpu.SemaphoreType.DMA((n, 3))])(*arrived)


def _agf_start(arrived, name):
    n = len(arrived)

    def body(*refs):
        o = refs[:n]
        ssem, rsem, token = refs[n], refs[n + 1], refs[-1]
        x, y, c = _place()
        for t in range(n):
            for j, (_, s_p) in enumerate(_other_chips(x, y)):
                pltpu.make_async_remote_copy(src_ref=o[t].at[s_p, c], dst_ref=o[t].at[s_p, c],
                                             send_sem=ssem.at[3 * t + j], recv_sem=rsem.at[3 * t + j],
                                             device_id=(x, y, 1 - c), device_id_type=MESH).start()
        token[...] = jnp.zeros_like(token)

    out_shape = ([pltpu.SemaphoreType.DMA((3 * n,))] * 2 + [pltpu.HBM(a.shape, bf16) for a in arrived]
                 + [_sds((8, LANES), f32)])
    outs = pl.pallas_call(
        body, name=name, out_shape=out_shape, in_specs=[HBM] * n, out_specs=[SEM, SEM] + [HBM] * n + [VMEM],
        input_output_aliases={t: 2 + t for t in range(n)},
        compiler_params=pltpu.CompilerParams(has_side_effects=EFFECT))(*[_in_hbm(a) for a in arrived])
    return (outs[0], outs[1]), list(outs[2:2 + n]), outs[-1]


def _agf_wait(sems, inflight, after, name):
    n = len(inflight)

    def body(*refs):
        o, ssem, rsem = refs[:n], refs[n], refs[n + 1]
        x, y, c = _place()
        for t in range(n):
            for j, (_, s_p) in enumerate(_other_chips(x, y)):
                cp = pltpu.make_async_remote_copy(src_ref=o[t].at[s_p, c], dst_ref=o[t].at[s_p, 1 - c],
                                                  send_sem=ssem.at[3 * t + j], recv_sem=rsem.at[3 * t + j],
                                                  device_id=(x, y, c), device_id_type=MESH)
                cp.wait_send()
                cp.wait_recv()

    return pl.pallas_call(
        body, name=name, out_shape=[pltpu.HBM(a.shape, bf16) for a in inflight],
        in_specs=[HBM] * n + [SEM, SEM, ANY], out_specs=[HBM] * n, input_output_aliases={t: t for t in range(n)},
        compiler_params=pltpu.CompilerParams(has_side_effects=EFFECT))(*inflight, sems[0], sems[1], after)


def _rs_pair_start(grads, name):
    n = len(grads)

    def body(*refs):
        g, theirs = refs[:n], refs[n:2 * n]
        ssem, rsem, token = refs[2 * n], refs[2 * n + 1], refs[-1]
        x, y, c = _place()
        for t in range(n):
            pltpu.make_async_remote_copy(src_ref=g[t].at[:, 1 - c], dst_ref=theirs[t], send_sem=ssem.at[t],
                                         recv_sem=rsem.at[t], device_id=(x, y, 1 - c), device_id_type=MESH).start()
        token[...] = jnp.zeros_like(token)

    lands = [lax.empty((4,) + g.shape[2:], bf16) for g in grads]
    out_shape = ([pltpu.SemaphoreType.DMA((n,))] * 2 + [pltpu.HBM(g.shape, bf16) for g in grads]
                 + [pltpu.HBM(q.shape, bf16) for q in lands] + [_sds((8, LANES), f32)])
    outs = pl.pallas_call(
        body, name=name, out_shape=out_shape, in_specs=[HBM] * (2 * n), out_specs=[SEM, SEM] + [HBM] * (2 * n) + [VMEM],
        input_output_aliases={t: 2 + t for t in range(2 * n)},
        compiler_params=pltpu.CompilerParams(has_side_effects=EFFECT))(*[_in_hbm(a) for a in list(grads) + lands])
    return (outs[0], outs[1]), list(outs[2:2 + n]), list(outs[2 + n:2 + 2 * n]), outs[-1]


def _rs_pair_wait(sems, grads, lands, after, name):
    n = len(grads)

    def body(*refs):
        g, theirs = refs[:n], refs[n:2 * n]
        ssem, rsem = refs[2 * n], refs[2 * n + 1]
        x, y, c = _place()
        for t in range(n):
            cp = pltpu.make_async_remote_copy(src_ref=g[t].at[:, 1 - c], dst_ref=theirs[t], send_sem=ssem.at[t],
                                              recv_sem=rsem.at[t], device_id=(x, y, c), device_id_type=MESH)
            cp.wait_send()
            cp.wait_recv()

    outs = pl.pallas_call(
        body, name=name, out_shape=[pltpu.HBM(a.shape, bf16) for a in list(grads) + list(lands)],
        in_specs=[HBM] * (2 * n) + [SEM, SEM] + [ANY] * len(after), out_specs=[HBM] * (2 * n),
        input_output_aliases={t: t for t in range(2 * n)},
        compiler_params=pltpu.CompilerParams(has_side_effects=EFFECT))(*grads, *lands, sems[0], sems[1], *after)
    return list(outs[:n]), list(outs[n:])


def _rs_chip_start(pairs, name):
    n = len(pairs)

    def body(*refs):
        p, q = refs[:n], refs[n:2 * n]
        ssem, rsem, token = refs[2 * n], refs[2 * n + 1], refs[-1]
        x, y, c = _place()
        for t in range(n):
            for j, ((px, py), s_p) in enumerate(_other_chips(x, y)):
                pltpu.make_async_remote_copy(src_ref=p[t].at[s_p], dst_ref=q[t].at[j], send_sem=ssem.at[3 * t + j],
                                             recv_sem=rsem.at[3 * t + j], device_id=(px, py, c), device_id_type=MESH).start()
        token[...] = jnp.zeros_like(token)

    lands = [lax.empty((3,) + p.shape[1:], bf16) for p in pairs]
    out_shape = ([pltpu.SemaphoreType.DMA((3 * n,))] * 2 + [pltpu.HBM(p.shape, bf16) for p in pairs]
                 + [pltpu.HBM(q.shape, bf16) for q in lands] + [_sds((8, LANES), f32)])
    outs = pl.pallas_call(
        body, name=name, out_shape=out_shape, in_specs=[HBM] * (2 * n), out_specs=[SEM, SEM] + [HBM] * (2 * n) + [VMEM],
        input_output_aliases={t: 2 + t for t in range(2 * n)},
        compiler_params=pltpu.CompilerParams(has_side_effects=EFFECT))(*[_in_hbm(a) for a in list(pairs) + lands])
    return (outs[0], outs[1]), list(outs[2:2 + n]), list(outs[2 + n:2 + 2 * n]), outs[-1]


def _rs_chip_wait(sems, pairs, lands, after, name):
    n = len(pairs)

    def body(*refs):
        p, q = refs[:n], refs[n:2 * n]
        ssem, rsem = refs[2 * n], refs[2 * n + 1]
        x, y, c = _place()
        for t in range(n):
            for j, (_, s_p) in enumerate(_other_chips(x, y)):
                cp = pltpu.make_async_remote_copy(src_ref=p[t].at[s_p], dst_ref=q[t].at[j], send_sem=ssem.at[3 * t + j],
                                                  recv_sem=rsem.at[3 * t + j], device_id=(x, y, c), device_id_type=MESH)
                cp.wait_send()
                cp.wait_recv()

    outs = pl.pallas_call(
        body, name=name, out_shape=[pltpu.HBM(a.shape, bf16) for a in list(pairs) + list(lands)],
        in_specs=[HBM] * (2 * n) + [SEM, SEM] + [ANY] * len(after), out_specs=[HBM] * (2 * n),
        input_output_aliases={t: t for t in range(2 * n)},
        compiler_params=pltpu.CompilerParams(has_side_effects=EFFECT))(*pairs, *lands, sems[0], sems[1], *after)
    return list(outs[:n]), list(outs[n:])


def _rs_half_start(halves, name):
    n = len(halves)

    def body(*refs):
        o = refs[:n]
        ssem, rsem, token = refs[n], refs[n + 1], refs[-1]
        x, y, c = _place()
        for t in range(n):
            pltpu.make_async_remote_copy(src_ref=o[t].at[c], dst_ref=o[t].at[c], send_sem=ssem.at[t],
                                         recv_sem=rsem.at[t], device_id=(x, y, 1 - c), device_id_type=MESH).start()
        token[...] = jnp.zeros_like(token)

    out_shape = ([pltpu.SemaphoreType.DMA((n,))] * 2 + [pltpu.HBM(h.shape, h.dtype) for h in halves]
                 + [_sds((8, LANES), f32)])
    outs = pl.pallas_call(
        body, name=name, out_shape=out_shape, in_specs=[HBM] * n, out_specs=[SEM, SEM] + [HBM] * n + [VMEM],
        input_output_aliases={t: 2 + t for t in range(n)},
        compiler_params=pltpu.CompilerParams(has_side_effects=EFFECT))(*[_in_hbm(h) for h in halves])
    return (outs[0], outs[1]), list(outs[2:2 + n]), outs[-1]


def _rs_half_wait(sems, inflight, after, name):
    n = len(inflight)

    def body(*refs):
        o, ssem, rsem = refs[:n], refs[n], refs[n + 1]
        x, y, c = _place()
        for t in range(n):
            cp = pltpu.make_async_remote_copy(src_ref=o[t].at[c], dst_ref=o[t].at[1 - c], send_sem=ssem.at[t],
                                              recv_sem=rsem.at[t], device_id=(x, y, c), device_id_type=MESH)
            cp.wait_send()
            cp.wait_recv()

    return pl.pallas_call(
        body, name=name, out_shape=[pltpu.HBM(h.shape, h.dtype) for h in inflight],
        in_specs=[HBM] * n + [SEM, SEM, ANY], out_specs=[HBM] * n, input_output_aliases={t: t for t in range(n)},
        compiler_params=pltpu.CompilerParams(has_side_effects=EFFECT))(*inflight, sems[0], sems[1], after)


def _row_spec(tm, cols):
    return pl.BlockSpec((tm, cols), lambda i: (i, 0))


def _vec_spec(cols, rows=1):
    return pl.BlockSpec((rows, cols), lambda i: (0, 0))


def _modulated_norm(xv, g, shift, scale):
    r = lax.rsqrt(jnp.mean(xv * xv, axis=-1, keepdims=True) + EPS)
    return (((xv * r) * g) * (1.0 + scale) + shift).astype(bf16)


def _hnorm(x, g, shift, scale):
    T, tm = x.shape[0], 256

    def body(x_ref, g_ref, sh_ref, sc_ref, h_ref):
        h_ref[...] = _modulated_norm(x_ref[...], g_ref[...], sh_ref[...], sc_ref[...])

    return _pcall(body, name="hnorm", out_shape=_sds((T, D), bf16), grid=(T // tm,),
                  in_specs=[_row_spec(tm, D), _vec_spec(D), _vec_spec(D), _vec_spec(D)],
                  out_specs=_row_spec(tm, D))(x, g, shift, scale)


def _out_proj(y2, wo, x, gate, nxt=None):
    T, tm = x.shape[0], 512

    def body(y_ref, w_ref, x_ref, g_ref, *rest):
        o = jnp.dot(y_ref[0], w_ref[0], preferred_element_type=f32)
        o = o + jnp.dot(y_ref[1], w_ref[1], preferred_element_type=f32)
        xo = x_ref[...] + g_ref[...] * o
        if nxt is None:
            xo_ref, o_ref = rest
        else:
            ng_ref, nsh_ref, nsc_ref, xo_ref, o_ref, h_ref = rest
            h_ref[...] = _modulated_norm(xo, ng_ref[...], nsh_ref[...], nsc_ref[...])
        o_ref[...] = o.astype(bf16)
        xo_ref[...] = xo

    extra = [] if nxt is None else list(nxt)
    n_out = 2 if nxt is None else 3
    return _pcall(body, name="out_proj", out_shape=[_sds((T, D), f32), _sds((T, D), bf16), _sds((T, D), bf16)][:n_out],
                  grid=(T // tm,),
                  in_specs=[pl.BlockSpec((2, tm, D), lambda i: (0, i, 0)), pl.BlockSpec((2, D, D), lambda i: (0, 0, 0)),
                            _row_spec(tm, D), _vec_spec(D)] + [_vec_spec(D)] * len(extra),
                  out_specs=[_row_spec(tm, D)] * n_out, vmem_mb=40)(y2, wo, x, gate, *extra)


def _gate_bwd_tile(dx, o_ref, gate_ref, dob_ref, dgate_ref):
    dob_ref[...] = (dx * gate_ref[...]).astype(bf16)
    dgate_ref[...] += jnp.sum(dx * o_ref[...].astype(f32), axis=0, keepdims=True)


def _loss_bwd(x, target, g, o, gate):
    T, tm = x.shape[0], 512

    def body(x_ref, t_ref, g_ref, o_ref, gate_ref, dx_ref, loss_ref, dg_ref, dob_ref, dgate_ref):
        @pl.when(pl.program_id(0) == 0)
        def _():
            loss_ref[...] = jnp.zeros_like(loss_ref)
            dg_ref[...] = jnp.zeros_like(dg_ref)
            dgate_ref[...] = jnp.zeros_like(dgate_ref)

        xv, gv = x_ref[...], g_ref[...]
        r = lax.rsqrt(jnp.mean(xv * xv, axis=-1, keepdims=True) + EPS)
        xn = xv * r
        err = xn * gv - t_ref[...]
        dy = err * (1.0 / D)
        dxn = dy * gv
        dx = r * (dxn - xn * jnp.mean(dxn * xn, axis=-1, keepdims=True))
        dx_ref[...] = dx
        dg_ref[...] += jnp.sum(dy * xn, axis=0, keepdims=True)
        loss_ref[...] += (0.5 / D) * jnp.sum(jnp.sum(err * err, axis=1, keepdims=True), axis=0, keepdims=True)
        _gate_bwd_tile(dx, o_ref, gate_ref, dob_ref, dgate_ref)

    return _pcall(body, name="loss_bwd",
                  out_shape=[_sds((T, D), f32), _sds((1, 1), f32), _sds((1, D), f32), _sds((T, D), bf16), _sds((1, D), f32)],
                  grid=(T // tm,),
                  in_specs=[_row_spec(tm, D), _row_spec(tm, D), _vec_spec(D), _row_spec(tm, D), _vec_spec(D)],
                  out_specs=[_row_spec(tm, D), pl.BlockSpec((1, 1), lambda i: (0, 0)), _vec_spec(D), _row_spec(tm, D),
                             _vec_spec(D)])(x, target, g, o, gate)


def _norm_bwd(x, dh, gin, g, scale, below=None):
    T, tm = x.shape[0], 512

    def body(x_ref, dh_ref, gin_ref, g_ref, sc_ref, *rest):
        if below is None:
            dx_ref, st_ref = rest
        else:
            o_ref, gate_ref, dx_ref, st_ref, dob_ref, dgate_ref = rest

        @pl.when(pl.program_id(0) == 0)
        def _():
            st_ref[...] = jnp.zeros_like(st_ref)
            if below is not None:
                dgate_ref[...] = jnp.zeros_like(dgate_ref)

        xv, gv, dhv = x_ref[...], g_ref[...], dh_ref[...]
        r = lax.rsqrt(jnp.mean(xv * xv, axis=-1, keepdims=True) + EPS)
        xn = xv * r
        da = dhv * (1.0 + sc_ref[...])
        dxn = da * gv
        dx = gin_ref[...] + r * (dxn - xn * jnp.mean(dxn * xn, axis=-1, keepdims=True))
        dx_ref[...] = dx
        st_ref[0:1, :] += jnp.sum(dhv, axis=0, keepdims=True)
        st_ref[1:2, :] += jnp.sum(dhv * (xn * gv), axis=0, keepdims=True)
        st_ref[2:3, :] += jnp.sum(da * xn, axis=0, keepdims=True)
        if below is not None:
            _gate_bwd_tile(dx, o_ref, gate_ref, dob_ref, dgate_ref)

    out_shape = [_sds((T, D), f32), _sds((8, D), f32)]
    in_specs = [_row_spec(tm, D), _row_spec(tm, D), _row_spec(tm, D), _vec_spec(D), _vec_spec(D)]
    out_specs = [_row_spec(tm, D), _vec_spec(D, 8)]
    args = [x, dh, gin, g, scale]
    if below is not None:
        out_shape += [_sds((T, D), bf16), _sds((1, D), f32)]
        in_specs += [_row_spec(tm, D), _vec_spec(D)]
        out_specs += [_row_spec(tm, D), _vec_spec(D)]
        args += list(below)
    return _pcall(body, name="norm_bwd", out_shape=out_shape, grid=(T // tm,), in_specs=in_specs,
                  out_specs=out_specs)(*args)


STEPS = 4
ADAMW_STEPS = 8


def _cast_place(place, ws, layer, after=None):
    n = len(ws)

    def body(place_ref, *refs):
        for t in range(n):
            refs[-n + t][...] = refs[t][...].astype(bf16)

    def tile(w):
        return w.shape[1] // STEPS, w.shape[2]

    extra = [] if after is None else [after]
    return _pcall(body, name="cast_place", out_shape=[_sds((4,) + w.shape[1:], bf16) for w in ws], grid=(STEPS,),
                  prefetch=1,
                  in_specs=[pl.BlockSpec((None,) + tile(w), lambda i, pr: (layer, i, 0)) for w in ws] + [ANY] * len(extra),
                  out_specs=[pl.BlockSpec((None,) + tile(w), lambda i, pr: (pr[0], i, 0)) for w in ws])(
                      place, *ws, *extra)


def _rs_add(place, grads, theirs):
    n = len(grads)

    def body(place_ref, *refs):
        for t in range(n):
            refs[2 * n + t][...] = (refs[t][...].astype(f32) + refs[n + t][...].astype(f32)).astype(bf16)

    def tile(q):
        return q.shape[1] // 2, q.shape[2]

    mine = [pl.BlockSpec((None, None) + tile(q), lambda s, i, pr: (s, pr[1], i, 0)) for q in theirs]
    shard = [pl.BlockSpec((None,) + tile(q), lambda s, i, pr: (s, i, 0)) for q in theirs]
    return _pcall(body, name="rs_add", out_shape=[_sds(q.shape, bf16) for q in theirs], grid=(4, 2), prefetch=1,
                  in_specs=mine + shard, out_specs=shard)(place, *grads, *theirs)


def _rs_sum(place, pairs, slots):
    n, steps = len(pairs), 4

    def body(place_ref, *refs):
        for t in range(n):
            p_ref, q_ref = refs[t], refs[n + t]
            total = ((p_ref[...].astype(f32) + q_ref[0].astype(f32)) + q_ref[1].astype(f32)) + q_ref[2].astype(f32)
            refs[2 * n + t][...] = total.astype(bf16)

    def tile(q):
        return q.shape[1] // steps, q.shape[2]

    return _pcall(body, name="rs_sum", out_shape=[_sds((2,) + q.shape[1:], bf16) for q in slots], grid=(steps,),
                  prefetch=1,
                  in_specs=[pl.BlockSpec((None,) + tile(q), lambda i, pr: (pr[0], i, 0)) for q in slots]
                  + [pl.BlockSpec((3,) + tile(q), lambda i, pr: (0, i, 0)) for q in slots],
                  out_specs=[pl.BlockSpec((None,) + tile(q), lambda i, pr: (pr[1], i, 0)) for q in slots])(
                      place, *pairs, *slots)


def _adamw_math(w, g, m, v):
    m = ADAM_B1 * m + (1.0 - ADAM_B1) * g
    v = ADAM_B2 * v + (1.0 - ADAM_B2) * jnp.square(g)
    m_hat = m / (1.0 - ADAM_B1 ** ADAM_STEP)
    v_hat = v / (1.0 - ADAM_B2 ** ADAM_STEP)
    delta = -ADAM_LR * (m_hat / (jnp.sqrt(v_hat) + ADAM_EPS) + ADAM_WD * w)
    return delta, m, v


def _adamw_layer(layer, items):
    n = len(items)

    def body(*refs):
        outs = refs[-4 * n:]
        for t in range(n):
            w_ref, g_ref, m_ref, v_ref = refs[4 * t:4 * t + 4]
            g = g_ref[...].astype(f32)
            outs[4 * t][...] = g
            outs[4 * t + 1][...], outs[4 * t + 2][...], outs[4 * t + 3][...] = _adamw_math(
                w_ref[...], g, m_ref[...], v_ref[...])

    args, in_specs, out_specs, out_shape = [], [], [], []
    for w, g, m, v, _ in items:
        tr, cols = w.shape[1] // ADAMW_STEPS, w.shape[2]
        spec = pl.BlockSpec((None, tr, cols), lambda i: (layer, i, 0))
        args += [w, g, m, v]
        in_specs += [spec, pl.BlockSpec((tr, cols), lambda i: (i, 0)), spec, spec]
        out_specs += [spec] * 4
        out_shape += [_sds(w.shape, f32)] * 4
    aliases = {}
    for t, it in enumerate(items):
        if it[4] is not None:
            for k in range(4):
                aliases[len(args)] = 4 * t + k
                args.append(it[4][k])
                in_specs.append(ANY)
    res = _pcall(body, name="adamw", out_shape=out_shape, grid=(ADAMW_STEPS,), in_specs=in_specs, out_specs=out_specs,
                 aliases=aliases)(*args)
    return [tuple(res[4 * t:4 * t + 4]) for t in range(n)]


def _adamw_small(items):
    n = len(items)

    def body(*refs):
        ins, outs = refs[:4 * n], refs[4 * n:]
        for t in range(n):
            w_ref, g_ref, m_ref, v_ref = ins[4 * t:4 * t + 4]
            if len(g_ref.shape) == len(w_ref.shape) + 1:
                g = g_ref[0]
                for b in range(1, g_ref.shape[0]):
                    g = g + g_ref[b]
            else:
                g = g_ref[...]
            d, m, v = _adamw_math(w_ref[...], g, m_ref[...], v_ref[...])
            outs[4 * t][...], outs[4 * t + 1][...], outs[4 * t + 2][...], outs[4 * t + 3][...] = g, d, m, v

    out_shape = [_sds(w.shape, f32) for (w, _, _, _) in items for _ in range(4)]
    flat = [a for it in items for a in it]
    res = _pcall(body, name="adamw_small", out_shape=out_shape, in_specs=[VMEM] * (4 * n),
                 out_specs=[VMEM] * (4 * n))(*flat)
    return [tuple(res[4 * t:4 * t + 4]) for t in range(n)]


NN = ((1,), (0,))
NT = ((1,), (1,))
TN = ((0,), (0,))


def _mm(name, a, b, *, grid, a_spec, b_spec, out_shape, out_spec, dims, vmem_mb=None):
    def body(a_ref, b_ref, o_ref):
        r = lax.dot_general(a_ref[...], b_ref[...], (dims, ((), ())), preferred_element_type=f32)
        o_ref[...] = r.astype(o_ref.dtype)

    return _pcall(body, name=name, out_shape=out_shape, grid=grid, in_specs=[a_spec, b_spec], out_specs=out_spec,
                  vmem_mb=vmem_mb)(a, b)


def _whole(shape):
    return pl.BlockSpec(shape, lambda j: (0,) * len(shape))


def _split_spec(rows, tile, per_split):
    return pl.BlockSpec((None, rows, tile), lambda j: (j // per_split, 0, j % per_split))


class _Proj:
    def __init__(self, n, splits, tile):
        self.n, self.splits, self.tile = n, splits, tile
        self.steps = n // tile
        self.w_per = n // 4 // tile
        self.a_per = n // splits // tile
        assert self.w_per * tile * 4 == n and self.a_per * tile * splits == n

    def fwd(self, hb, wg):
        T = hb.shape[0]
        sub, tile, w_per = FWD_TILES, self.tile, self.w_per
        wide = sub * tile
        a_per = self.n // self.splits // wide
        assert a_per * wide * self.splits == self.n

        def w_tile(q):
            return pl.BlockSpec((None, D, tile), lambda j: ((sub * j + q) // w_per, 0, (sub * j + q) % w_per))

        def body(a_ref, *rest):
            w = jnp.concatenate([rest[q][...] for q in range(sub)], axis=1)
            rest[sub][...] = jnp.dot(a_ref[...], w, preferred_element_type=f32).astype(bf16)

        return _pcall(body, name="proj_fwd", out_shape=_sds((self.splits, T, self.n // self.splits), bf16),
                      grid=(self.n // wide,), in_specs=[_whole((T, D))] + [w_tile(q) for q in range(sub)],
                      out_specs=pl.BlockSpec((None, T, wide), lambda j: (j // a_per, 0, j % a_per)),
                      vmem_mb=40 if wide > 512 else None)(hb, *([wg] * sub))

    def dw(self, hb, dp):
        T = hb.shape[0]
        return _mm("proj_dw", hb, dp, grid=(self.steps,), a_spec=_whole((T, D)),
                   b_spec=_split_spec(T, self.tile, self.a_per), out_shape=_sds((4, D, self.n // 4), bf16),
                   out_spec=_split_spec(D, self.tile, self.w_per), dims=TN)

    def dh(self, dp, wg):
        T = dp.shape[1]
        sub, tile, w_per = DH_WIDE // self.tile, self.tile, self.w_per
        a_per = self.n // self.splits // DH_WIDE
        assert sub * tile == DH_WIDE and a_per * DH_WIDE * self.splits == self.n

        def w_tile(q):
            return pl.BlockSpec((None, D, tile), lambda k: ((sub * k + q) // w_per, 0, (sub * k + q) % w_per))

        def body(a_ref, *rest):
            o_ref = rest[sub]
            w = jnp.concatenate([rest[q][...] for q in range(sub)], axis=1)
            r = lax.dot_general(a_ref[...], w, (NT, ((), ())), preferred_element_type=f32)

            @pl.when(pl.program_id(0) == 0)
            def _():
                o_ref[...] = r

            @pl.when(pl.program_id(0) > 0)
            def _():
                o_ref[...] += r

        return _pcall(body, name="proj_dh", out_shape=_sds((T, D), f32), grid=(self.n // DH_WIDE,),
                      in_specs=[pl.BlockSpec((None, T, DH_WIDE), lambda k: (k // a_per, 0, k % a_per))]
                      + [w_tile(q) for q in range(sub)],
                      out_specs=_whole((T, D)), vmem_mb=40)(dp, *([wg] * sub))


EVEN_PROJ = _Proj(7 * D, 7, 256)
ODD_PROJ = _Proj(4 * D, 2, 512)


def _out_bwd(dob, wo, y2):
    T = dob.shape[0]
    w_spec = pl.BlockSpec((None, 512, D), lambda j: (j, 0, 0))

    def body(dob_ref, w_ref, y_ref, dy_ref, dw_ref):
        dob_v = dob_ref[...]
        dy_ref[...] = lax.dot_general(dob_v, w_ref[...], (NT, ((), ())), preferred_element_type=f32).astype(bf16)
        dw_ref[...] = lax.dot_general(y_ref[...], dob_v, (TN, ((), ())), preferred_element_type=f32).astype(bf16)

    return _pcall(body, name="out_bwd", out_shape=[_sds((2, T, D), bf16), _sds((4, 512, D), bf16)], grid=(4,),
                  in_specs=[_whole((T, D)), w_spec, _split_spec(T, 512, 2)],
                  out_specs=[_split_spec(T, 512, 2), w_spec])(dob, wo, y2)


def _head_spec(lead, T):
    return pl.BlockSpec((lead, T, HEAD), lambda h: (0, 0, h))


def _head_vec(rows):
    return pl.BlockSpec((rows, HEAD), lambda h: (0, h))


_HEAD_MAT = pl.BlockSpec((None, HEAD, HEAD), lambda h: (h, 0, 0))


def _causal():
    return lax.broadcasted_iota(jnp.int32, (HEAD, HEAD), 0) >= lax.broadcasted_iota(jnp.int32, (HEAD, HEAD), 1)


def _layernorm_head(v):
    mu = jnp.mean(v, axis=-1, keepdims=True)
    d = v - mu
    rstd = lax.rsqrt(jnp.mean(d * d, axis=-1, keepdims=True) + EPS)
    return d * rstd, rstd


def _even_fwd(p7, conv_w, ln_g, ln_b, sgu_w, sgu_bias):
    T, C = p7.shape[1], CHUNK_ROWS

    def body(p_ref, cw_ref, lg_ref, lb_ref, w_ref, b_ref, y_ref):
        w0, w1, w2 = cw_ref[0:1, :], cw_ref[1:2, :], cw_ref[2:3, :]
        wm = jnp.where(_causal(), w_ref[...], 0.0).astype(bf16)
        bias, lg, lb = b_ref[...], lg_ref[...], lb_ref[...]

        def step(i, halo):
            rows = pl.ds(pl.multiple_of(i * C, C), C)
            ah, ab, ac, az, u, v, zb = (p_ref[k, rows, :].astype(f32) for k in range(7))
            tt = ac * ah
            ext = jnp.concatenate([halo, tt], axis=0)
            cv = w2 * tt + w1 * pltpu.roll(ext, 1, 0)[HALO_CONV:] + w0 * pltpu.roll(ext, 2, 0)[HALO_CONV:]
            y_ref[0, rows, :] = (ab * cv * _silu(az)).astype(bf16)
            vhat, _ = _layernorm_head(v)
            vn = (vhat * lg + lb).astype(bf16)
            mix = jnp.concatenate([jnp.dot(wm, vn[k * HEAD:(k + 1) * HEAD], preferred_element_type=f32) + bias
                                   for k in range(C // HEAD)], axis=0)
            y_ref[1, rows, :] = (u * mix * _silu(zb)).astype(bf16)
            return tt[C - HALO_CONV:]

        lax.fori_loop(0, T // C, step, jnp.zeros((HALO_CONV, HEAD), f32))

    return _pcall(body, name="even_fwd", out_shape=_sds((2, T, D), bf16), grid=(NH,),
                  in_specs=[_head_spec(7, T), _head_vec(3), _head_vec(1), _head_vec(1), _HEAD_MAT, _HEAD_MAT],
                  out_specs=_head_spec(2, T))(p7, conv_w, ln_g, ln_b, sgu_w, sgu_bias)


def _even_bwd(p7, dy2, conv_w, ln_g, ln_b, sgu_w, sgu_bias):
    T, C = p7.shape[1], CHUNK_ROWS
    n_chunks = T // C

    def body(p_ref, dy_ref, cw_ref, lg_ref, lb_ref, w_ref, b_ref,
             dp_ref, dcw_ref, dlg_ref, dlb_ref, dw_ref, dms_ref, dcv_s):
        w0, w1, w2 = cw_ref[0:1, :], cw_ref[1:2, :], cw_ref[2:3, :]
        tri = _causal()
        wm = jnp.where(tri, w_ref[...], 0.0).astype(bf16)
        bias, lg, lb = b_ref[...], lg_ref[...], lb_ref[...]
        dw_ref[...] = jnp.zeros_like(dw_ref)
        dms_ref[...] = jnp.zeros_like(dms_ref)

        def fwd_step(i, carry):
            halo, a0, a1, a2, alg, alb = carry
            rows = pl.ds(pl.multiple_of(i * C, C), C)
            ah, ab, ac, az = (p_ref[k, rows, :].astype(f32) for k in range(4))
            dya = dy_ref[0, rows, :].astype(f32)
            tt = ac * ah
            ext = jnp.concatenate([halo, tt], axis=0)
            t1, t2 = pltpu.roll(ext, 1, 0)[HALO_CONV:], pltpu.roll(ext, 2, 0)[HALO_CONV:]
            cv = w2 * tt + w1 * t1 + w0 * t2
            sa, dsa = _silu_and_grad(az)
            g1 = dya * sa
            dp_ref[1, rows, :] = (g1 * cv).astype(bf16)
            dp_ref[3, rows, :] = (dya * ab * cv * dsa).astype(bf16)
            dcv = g1 * ab
            dcv_s[rows, :] = dcv
            a2 = a2 + jnp.sum(dcv * tt, axis=0, keepdims=True)
            a1 = a1 + jnp.sum(dcv * t1, axis=0, keepdims=True)
            a0 = a0 + jnp.sum(dcv * t2, axis=0, keepdims=True)

            u, zb, dyb = p_ref[4, rows, :].astype(f32), p_ref[6, rows, :].astype(f32), dy_ref[1, rows, :].astype(f32)
            vhat, rstd = _layernorm_head(p_ref[5, rows, :].astype(f32))
            vn = (vhat * lg + lb).astype(bf16)
            sb, dsb = _silu_and_grad(zb)
            mix = jnp.concatenate([jnp.dot(wm, vn[k * HEAD:(k + 1) * HEAD], preferred_element_type=f32) + bias
                                   for k in range(C // HEAD)], axis=0)
            dp_ref[4, rows, :] = (dyb * mix * sb).astype(bf16)
            dp_ref[6, rows, :] = (dyb * u * mix * dsb).astype(bf16)
            dmix = dyb * u * sb
            dvn_parts = []
            for k in range(C // HEAD):
                dm = dmix[k * HEAD:(k + 1) * HEAD]
                dmb = dm.astype(bf16)
                dvn_parts.append(lax.dot_general(wm, dmb, (TN, ((), ())), preferred_element_type=f32))
                dw_ref[...] += lax.dot_general(dmb, vn[k * HEAD:(k + 1) * HEAD], (NT, ((), ())),
                                               preferred_element_type=f32)
                dms_ref[...] += dm
            dvn = jnp.concatenate(dvn_parts, axis=0)
            alg = alg + jnp.sum(dvn * vhat, axis=0, keepdims=True)
            alb = alb + jnp.sum(dvn, axis=0, keepdims=True)
            dvh = dvn * lg
            dv = rstd * (dvh - jnp.mean(dvh, axis=-1, keepdims=True)
                         - vhat * jnp.mean(dvh * vhat, axis=-1, keepdims=True))
            dp_ref[5, rows, :] = dv.astype(bf16)
            return tt[C - HALO_CONV:], a0, a1, a2, alg, alb

        zrow = jnp.zeros((1, HEAD), f32)
        _, a0, a1, a2, alg, alb = lax.fori_loop(
            0, n_chunks, fwd_step, (jnp.zeros((HALO_CONV, HEAD), f32), zrow, zrow, zrow, zrow, zrow))
        dcw_ref[0:1, :], dcw_ref[1:2, :], dcw_ref[2:3, :] = a0, a1, a2
        dlg_ref[...], dlb_ref[...] = alg, alb
        dw_ref[...] = jnp.where(tri, dw_ref[...], 0.0)

        def bwd_step(k, halo):
            rows = pl.ds(pl.multiple_of((n_chunks - 1 - k) * C, C), C)
            dcv = dcv_s[rows, :]
            ext = jnp.concatenate([dcv, halo], axis=0)
            n1 = pltpu.roll(ext, C + HALO_CONV - 1, 0)[:C]
            n2 = pltpu.roll(ext, C + HALO_CONV - 2, 0)[:C]
            dtt = w2 * dcv + w1 * n1 + w0 * n2
            dp_ref[2, rows, :] = (dtt * p_ref[0, rows, :].astype(f32)).astype(bf16)
            dp_ref[0, rows, :] = (dtt * p_ref[2, rows, :].astype(f32)).astype(bf16)
            return dcv[:HALO_CONV]

        lax.fori_loop(0, n_chunks, bwd_step, jnp.zeros((HALO_CONV, HEAD), f32))

    out_shape = [_sds((7, T, D), bf16), _sds((3, D), f32), _sds((1, D), f32), _sds((1, D), f32),
                 _sds((NH, HEAD, HEAD), f32), _sds((NH, HEAD, HEAD), f32)]
    return _pcall(body, name="even_bwd", out_shape=out_shape, grid=(NH,),
                  in_specs=[_head_spec(7, T), _head_spec(2, T), _head_vec(3), _head_vec(1), _head_vec(1),
                            _HEAD_MAT, _HEAD_MAT],
                  out_specs=[_head_spec(7, T), _head_vec(3), _head_vec(1), _head_vec(1), _HEAD_MAT, _HEAD_MAT],
                  scratch=[pltpu.VMEM((T, HEAD), f32)])(p7, dy2, conv_w, ln_g, ln_b, sgu_w, sgu_bias)


def _window_sum(ext, win, towards_past):
    n, k, s = ext.shape[0], 1, ext
    while k < win:
        s = s + pltpu.roll(s, k if towards_past else n - k, 0)
        k *= 2
    return s


def _pool_count(i, C, win):
    t = i * C + lax.broadcasted_iota(jnp.int32, (C, 1), 0)
    cnt = jnp.minimum(t + 1, win).astype(f32)
    return cnt, 1.0 / cnt


def _group_specs(T):
    p_spec = pl.BlockSpec((None, T, GC), lambda g: (0, 0, g))
    z_spec = pl.BlockSpec((None, T, GC), lambda g: (1, 0, g))
    pw_spec = pl.BlockSpec((4, GC // 4, GC), lambda g: (0, g, 0))
    ps_spec = pl.BlockSpec((1, GC), lambda g: (0, g))
    y_spec = pl.BlockSpec((None, T, GC), lambda g: (g // 2, 0, g % 2))
    return p_spec, z_spec, pw_spec, ps_spec, y_spec


def _odd_fwd(p2, pool_wg, pool_scale):
    T, C = p2.shape[1], CHUNK_ROWS
    p_spec, z_spec, pw_spec, ps_spec, y_spec = _group_specs(T)

    def body(p_ref, z_ref, pw_ref, ps_ref, y_ref):
        pw, ps = pw_ref[...].reshape(GC, GC), ps_ref[...]

        def run(win):
            def step(i, halo):
                rows = pl.ds(pl.multiple_of(i * C, C), C)
                p = p_ref[rows, :].astype(f32)
                s = _window_sum(jnp.concatenate([halo, p], axis=0), win, True)[HALO_POOL:]
                pooled = s * _pool_count(i, C, win)[1] - p
                ypre = jnp.dot(pooled.astype(bf16), pw, preferred_element_type=f32)
                y_ref[rows, :] = (ypre * ps * _silu(z_ref[rows, :].astype(f32))).astype(bf16)
                return p[C - HALO_POOL:]

            lax.fori_loop(0, T // C, step, jnp.zeros((HALO_POOL, GC), f32))

        for gi, win in enumerate(WINDOWS):
            pl.when(pl.program_id(0) == gi)(functools.partial(run, win))

    return _pcall(body, name="odd_fwd", out_shape=_sds((2, T, D), bf16), grid=(len(WINDOWS),),
                  in_specs=[p_spec, z_spec, pw_spec, ps_spec], out_specs=y_spec)(p2, p2, pool_wg, pool_scale)


def _odd_bwd(p2, dy2, pool_wg, pool_scale):
    T, C = p2.shape[1], CHUNK_ROWS
    n_chunks = T // C
    p_spec, z_spec, pw_spec, ps_spec, y_spec = _group_specs(T)

    def body(p_ref, z_ref, dy_ref, pw_ref, ps_ref, dp_ref, dpw_ref, dps_ref, q_s, acc_s):
        pw, ps = pw_ref[...].reshape(GC, GC), ps_ref[...]

        def run(win):
            acc_s[...] = jnp.zeros_like(acc_s)

            def fwd_step(i, carry):
                halo, aps = carry
                rows = pl.ds(pl.multiple_of(i * C, C), C)
                p, z, dy = p_ref[rows, :].astype(f32), z_ref[rows, :].astype(f32), dy_ref[rows, :].astype(f32)
                _, inv_cnt = _pool_count(i, C, win)
                s = _window_sum(jnp.concatenate([halo, p], axis=0), win, True)[HALO_POOL:]
                pb = (s * inv_cnt - p).astype(bf16)
                ypre = jnp.dot(pb, pw, preferred_element_type=f32)
                sz, dsz = _silu_and_grad(z)
                aps = aps + jnp.sum(dy * ypre * sz, axis=0, keepdims=True)
                dp_ref[1, rows, :] = (dy * ypre * ps * dsz).astype(bf16)
                dyp = (dy * ps * sz).astype(bf16)
                acc_s[...] += lax.dot_general(pb, dyp, (TN, ((), ())), preferred_element_type=f32)
                dpool = lax.dot_general(dyp, pw, (NT, ((), ())), preferred_element_type=f32)
                q_s[rows, :] = dpool * inv_cnt
                return p[C - HALO_POOL:], aps

            _, aps = lax.fori_loop(0, n_chunks, fwd_step, (jnp.zeros((HALO_POOL, GC), f32), jnp.zeros((1, GC), f32)))
            dps_ref[...] = aps
            dpw_ref[...] = acc_s[...].reshape(4, GC // 4, GC).astype(bf16)

            def bwd_step(k, halo):
                i = n_chunks - 1 - k
                rows = pl.ds(pl.multiple_of(i * C, C), C)
                q = q_s[rows, :]
                s = _window_sum(jnp.concatenate([q, halo], axis=0), win, False)[:C]
                dp_ref[0, rows, :] = (s - q * _pool_count(i, C, win)[0]).astype(bf16)
                return q[:HALO_POOL]

            lax.fori_loop(0, n_chunks, bwd_step, jnp.zeros((HALO_POOL, GC), f32))

        for gi, win in enumerate(WINDOWS):
            pl.when(pl.program_id(0) == gi)(functools.partial(run, win))

    out_shape = [_sds((2, T, 2 * D), bf16), _sds((4, GC, GC), bf16), _sds((1, 2 * D), f32)]
    return _pcall(body, name="odd_bwd", out_shape=out_shape, grid=(len(WINDOWS),),
                  in_specs=[p_spec, z_spec, y_spec, pw_spec, ps_spec],
                  out_specs=[pl.BlockSpec((2, T, GC), lambda g: (0, 0, g)), pw_spec, ps_spec],
                  scratch=[pltpu.VMEM((T, GC), f32), pltpu.VMEM((GC, GC), f32)], vmem_mb=44)(
                      p2, p2, dy2, pool_wg, pool_scale)


def _ada_fwd(c_all, ada_w):
    cols = ada_w.shape[2]

    def body(c_ref, w_ref, o_ref):
        o_ref[...] = jnp.dot(_silu(c_ref[...]), w_ref[...], preferred_element_type=f32,
                             precision=lax.Precision.HIGHEST)

    return _pcall(body, name="ada_fwd", out_shape=_sds((4, N_DEV, cols), f32), grid=(4,),
                  in_specs=[pl.BlockSpec((N_DEV, D), lambda i: (0, 0)), pl.BlockSpec((None, D, cols), lambda i: (i, 0, 0))],
                  out_specs=pl.BlockSpec((None, N_DEV, cols), lambda i: (i, 0, 0)))(c_all, ada_w)


def _ada_bwd(c_all_t, dmod, w, m, v):
    cols, tr = w.shape[2], 256
    spec = pl.BlockSpec((None, tr, cols), lambda l, i: (l, i, 0))

    def body(c_ref, dm_ref, w_ref, m_ref, v_ref, g_ref, d_ref, mo_ref, vo_ref):
        sc = _silu(c_ref[...])
        g = sc[:, 0:1] * dm_ref[0:1, :]
        for b in range(1, N_DEV):
            g = g + sc[:, b:b + 1] * dm_ref[b:b + 1, :]
        g_ref[...] = g
        d_ref[...], mo_ref[...], vo_ref[...] = _adamw_math(w_ref[...], g, m_ref[...], v_ref[...])

    return _pcall(body, name="ada_bwd", out_shape=[_sds(w.shape, f32)] * 4, grid=(4, D // tr),
                  in_specs=[pl.BlockSpec((tr, N_DEV), lambda l, i: (i, 0)),
                            pl.BlockSpec((None, N_DEV, cols), lambda l, i: (l, 0, 0)), spec, spec, spec],
                  out_specs=[spec] * 4)(c_all_t, dmod, w, m, v)


def _layer_fwd(even, x, hb, gate, w, nxt, before_out=None):
    if even:
        w_in, w_out, conv_w, ln_g, ln_b, sgu_w, sgu_b = w
        bias = jnp.broadcast_to(sgu_b[:, :, None], (NH, HEAD, HEAD))
        p = EVEN_PROJ.fwd(hb, w_in)
        y2 = _even_fwd(p, conv_w, ln_g, ln_b, sgu_w, bias)
    else:
        w_in, pool_w, w_out, pool_scale = w
        p = ODD_PROJ.fwd(hb, w_in)
        y2 = _odd_fwd(p, pool_w, pool_scale)
    if before_out is not None:
        late_w_out, tok = before_out(y2)
        if late_w_out is not None:
            w_out = late_w_out
            w = (w_in, w_out) + tuple(w[2:]) if even else (w_in, pool_w, w_out, pool_scale)
        if tok is not None:
            gate = gate + tok[0:1, 0:1]
    outs = _out_proj(y2, w_out.reshape(2, D, D), x, gate, nxt)
    return outs[0], (None if nxt is None else outs[2]), (x, hb, p, y2, outs[1]), w


def _layer_bwd(even, gin, dob, dgate, saved, scale, g, w, below=None, send=None):
    x_in, hb, p, y2, o = saved
    if even:
        w_in, w_out, conv_w, ln_g, ln_b, sgu_w, sgu_b = w
        bias = jnp.broadcast_to(sgu_b[:, :, None], (NH, HEAD, HEAD))
        dy2, dwo = _out_bwd(dob, w_out, y2)
        dp, dconv, dlg, dlb, dsw, dms = _even_bwd(p, dy2, conv_w, ln_g, ln_b, sgu_w, bias)
        proj = EVEN_PROJ
        small = dict(conv_w=dconv, ln_g=dlg, ln_b=dlb, sgu_w=dsw, sgu_b=jnp.sum(dms, axis=-1))
        big = [proj.dw(hb, dp), dwo]
    else:
        w_in, pool_w, w_out, pool_scale = w
        dy2, dwo = _out_bwd(dob, w_out, y2)
        dp, dpw, dps = _odd_bwd(p, dy2, pool_w, pool_scale)
        proj = ODD_PROJ
        small = dict(pool_scale=dps)
        big = [proj.dw(hb, dp), dpw, dwo]
    if send is not None:
        big, tok = send(big)
        scale = scale + tok[0:1, 0:1]
    dh = proj.dh(dp, w_in)
    res = _norm_bwd(x_in, dh, gin, g, scale, below)
    stats = res[1]
    return (res[0], (None if below is None else (res[2], res[3])), big, small,
            jnp.concatenate([stats[0:2], dgate], axis=0), stats[2:3])


def _pack_rows(parts):
    rows = [p.reshape(-1, LANES) for p in parts]
    total = sum(r.shape[0] for r in rows)
    padded = -(-total // (8 * N_DEV)) * (8 * N_DEV)
    if padded > total:
        rows.append(jnp.zeros((padded - total, LANES), f32))
    return jnp.concatenate(rows, axis=0)


def _unpack_rows(buf, shapes):
    out, r = [], 0
    for shp in shapes:
        n = 1
        for d in shp:
            n *= d
        out.append(buf[r:r + n // LANES].reshape(shp))
        r += n // LANES
    return out


def kernel(x, c, norm_g, ada_w, ada_b, ab_w_in, ab_conv_w, ab_ln_g, ab_ln_b, ab_sgu_w, ab_sgu_b, ab_w_out, c_w_in, c_pool_w, c_pool_scale, c_w_out, final_g, loss_target, m_norm_g, m_ada_w, m_ada_b, m_ab_w_in, m_ab_conv_w, m_ab_ln_g, m_ab_ln_b, m_ab_sgu_w, m_ab_sgu_b, m_ab_w_out, m_c_w_in, m_c_pool_w, m_c_pool_scale, m_c_w_out, m_final_g, v_norm_g, v_ada_w, v_ada_b, v_ab_w_in, v_ab_conv_w, v_ab_ln_g, v_ab_ln_b, v_ab_sgu_w, v_ab_sgu_b, v_ab_w_out, v_c_w_in, v_c_pool_w, v_c_pool_scale, v_c_w_out, v_final_g):
    ix, iy, ic = _place()
    chip, dev = 2 * ix + iy, 4 * ix + 2 * iy + ic
    n_even, n_odd = ab_w_in.shape[0], c_w_in.shape[0]
    depth = n_even + n_odd
    acols = ada_w.shape[2]

    place = jnp.stack([chip, ic]).astype(jnp.int32)
    even_names, odd_names = ["ab_w_in", "ab_w_out"], ["c_w_in", "c_pool_w", "c_w_out"]
    params = {"ab_w_in": (ab_w_in, m_ab_w_in, v_ab_w_in), "ab_w_out": (ab_w_out, m_ab_w_out, v_ab_w_out),
              "c_w_in": (c_w_in, m_c_w_in, v_c_w_in), "c_w_out": (c_w_out, m_c_w_out, v_c_w_out),
              "c_pool_w": tuple(a.reshape(n_odd, GC, GC) for a in (c_pool_w, m_c_pool_w, v_c_pool_w))}

    def placed(names, layer, after=None):
        ws = [params[nm][0] for nm in names]
        return [p.reshape(4, 2, p.shape[1] // 2, p.shape[2]) for p in _cast_place(place, ws, layer, after)]

    def whole(arrays):
        return [g.reshape(4, 2 * g.shape[2], g.shape[3]) for g in arrays]

    first = _gather8(jnp.concatenate([c, ab_conv_w.reshape(1, -1), c_pool_scale.reshape(1, -1)], axis=1), "gather_c")
    c_all, small_all = first[:, 0, :D], first[0::2, 0, D:]
    sems_a, in_a, tok = _ag_start([placed(even_names[:1], 0)], first[0:1, 0, 0:LANES], "ag_start_0a")
    modp = _ada_fwd(c_all, ada_w)
    later = [placed(even_names[1:], 0, tok)]
    later += [placed(even_names if i % 2 == 0 else odd_names, i // 2, tok) for i in range(1, depth)]
    modg = _gather8(modp + tok[0:1, 0:1], "gather_mod", [lay[-1] for lay in later])
    mod_rows = lax.dynamic_index_in_dim(modg[0::2], dev, axis=2, keepdims=False)
    mod = jnp.transpose(mod_rows, (1, 0, 2)).reshape(depth, 3 * D) + ada_b
    mods = [(mod[i:i + 1, 0:D], mod[i:i + 1, D:2 * D], mod[i:i + 1, 2 * D:3 * D]) for i in range(depth)]

    def shard_cols(a, width):
        return lax.dynamic_slice_in_dim(a, chip * width, width, axis=a.ndim - 1)

    n_conv = ab_conv_w.size
    conv_all = small_all[:, :n_conv].reshape(4, n_even, 3, D // 4)
    conv_full = jnp.transpose(conv_all, (1, 2, 0, 3)).reshape(n_even, 3, D)
    scale_all = small_all[:, n_conv:].reshape(4, n_odd, 2 * D // 4)
    scale_full = jnp.transpose(scale_all, (1, 0, 2)).reshape(n_odd, 2 * D)

    gathers_done = mod[0:1, 0:LANES] + scale_full[0:1, 0:LANES]
    sems_b, in_b, tok = _ag_start(later[:1], gathers_done, "ag_start_0b")
    sems_r, in_r, tok = _ag_start(later[1:], tok, "ag_start_rest")

    x_cur, saved, weights, handoff = x[0], [], [], {}
    sems_f, in_f, tok = _agf_start(_ag_wait(in_a[0], sems_a[0], tok, "ag_wait_0a"), "agf_start_0")
    hb = _hnorm(x_cur, norm_g[0:1], mods[0][0] + tok[0:1, 0:1], mods[0][1])
    for i in range(depth):
        j = i // 2
        if i == 0:
            full = whole(_agf_wait(sems_f, in_f, hb, "agf_wait_0")) + [None]
        else:
            full = whole(_agf_wait(*handoff.pop(i), x_cur, f"agf_wait_{i}"))
        if i % 2 == 0:
            w = (full[0], full[1], conv_full[j], ab_ln_g[j:j + 1], ab_ln_b[j:j + 1], ab_sgu_w[j], ab_sgu_b[j])
        else:
            w = (full[0], full[1], full[2], scale_full[j:j + 1])

        def before_out(y2, i=i):
            w_out, tok = None, None
            if i == 0:
                w_out = whole(_ag_forward(_ag_wait(in_b[0], sems_b[0], y2, "ag_wait_0b"), "ag_forward"))[0]
            if i + 1 < depth:
                arrived = _ag_wait(in_r[i], sems_r[i], y2, f"ag_wait_{i + 1}")
                sems_f, inflight, tok = _agf_start(arrived, f"agf_start_{i + 1}")
                handoff[i + 1] = (sems_f, inflight)
            return w_out, tok

        nxt = (norm_g[i + 1:i + 2], mods[i + 1][0], mods[i + 1][1]) if i + 1 < depth else None
        x_cur, hb, sv, w = _layer_fwd(i % 2 == 0, x_cur, hb, mods[i][2], w, nxt, before_out)
        weights.append(w)
        saved.append(sv)
    gin, loss, dfinal_g, dob, dgate = _loss_bwd(x_cur, loss_target[0], final_g.reshape(1, D), saved[-1][4],
                                                mods[-1][2])

    stacked = {}

    def reduce_layer(i, sems, pairs, lands, after):
        pairs, slots = _rs_chip_wait(sems, pairs, lands, after, f"rs_chip_wait_{i}")
        half_sems, halves, _ = _rs_half_start(_rs_sum(place, pairs, slots), f"rs_half_start_{i}")
        return i, half_sems, halves

    def update_layer(i, half_sems, halves, after):
        names = even_names if i % 2 == 0 else odd_names
        grads = _rs_half_wait(half_sems, halves, after, f"rs_half_wait_{i}")
        items = [(params[nm][0], g.reshape(params[nm][0].shape[1:]), params[nm][1], params[nm][2], stacked.get(nm))
                 for nm, g in zip(names, grads)]
        for nm, res in zip(names, _adamw_layer(i // 2, items)):
            stacked[nm] = res
            updated.append(res[1])

    updated = []
    small_g, dmod, dnorm_g, pending, tok = [None] * depth, [None] * depth, [None] * depth, None, None
    exchanging = []
    for i in reversed(range(depth)):
        w = weights[i]
        if tok is not None:
            w = w[:2] + (w[2] + tok[0:1, 0:1],) + w[3:] if i % 2 == 0 else w[:3] + (w[3] + tok[0:1, 0:1],)
        below = (saved[i - 1][4], mods[i - 1][2]) if i > 0 else None

        def send(big_g, i=i):
            if exchanging:
                update_layer(*exchanging.pop(), big_g[0])
            big_g = [g.reshape(4, 2, g.shape[1] // 2, g.shape[2]) for g in big_g]
            sems, big_g, lands, tok = _rs_pair_start(big_g, f"rs_pair_start_{i}")
            return (sems, big_g, lands), tok

        gin, gate_bwd, sent, small_g[i], dmod[i], dnorm_g[i] = _layer_bwd(
            i % 2 == 0, gin, dob, dgate, saved[i], mods[i][1], norm_g[i:i + 1], w, below, send)
        if below is not None:
            dob, dgate = gate_bwd
        after = gin
        if i == 0:
            dmod_all = _gather8(jnp.stack(dmod).reshape(depth * 3 * D // LANES, LANES), "gather_dmod")
            after = dmod_all = dmod_all.reshape(N_DEV, depth, 3 * D)
        big_g, theirs = _rs_pair_wait(*sent, [after] + updated, f"rs_pair_wait_{i}")
        updated.clear()
        pairs = _rs_add(place, big_g, theirs)
        sems, pairs, lands, tok = _rs_chip_start(pairs, f"rs_chip_start_{i}")
        if pending is not None:
            exchanging.append(reduce_layer(*pending, [tok]))
        pending = (i, sems, pairs, lands)
    grad_x = gin
    dnorm_g = jnp.concatenate(dnorm_g, axis=0)

    dmod_cols = jnp.transpose(shard_cols(dmod_all, acols), (1, 0, 2))
    r_ada_w = _ada_bwd(c_all.T, dmod_cols, ada_w, m_ada_w, v_ada_w)
    update_layer(*exchanging.pop(), r_ada_w[1])
    last = reduce_layer(*pending, [r_ada_w[1]] + updated)

    small_parts = [dnorm_g, dfinal_g,
                   jnp.stack([small_g[2 * j]["conv_w"] for j in range(n_even)]),
                   jnp.concatenate([small_g[2 * j]["ln_g"] for j in range(n_even)], axis=0),
                   jnp.concatenate([small_g[2 * j]["ln_b"] for j in range(n_even)], axis=0),
                   jnp.stack([small_g[2 * j]["sgu_b"] for j in range(n_even)]),
                   jnp.concatenate([small_g[2 * j + 1]["pool_scale"] for j in range(n_odd)], axis=0),
                   jnp.pad(loss, ((0, 7), (0, LANES - 1)))]
    small_shapes = [p.shape for p in small_parts]
    sgu_parts = [small_g[2 * j]["sgu_w"].reshape(NH * HEAD, HEAD) for j in range(n_even)]
    reduced = _allreduce8([_pack_rows(small_parts)] + sgu_parts, "allreduce_small", last[2][0])
    update_layer(*last, reduced[0])
    r_ab_w_in, r_ab_w_out, r_c_w_in, r_c_w_out = (stacked[nm] for nm in ("ab_w_in", "ab_w_out", "c_w_in", "c_w_out"))
    r_c_pool_w = tuple(a.reshape(c_pool_w.shape) for a in stacked["c_pool_w"])
    g_norm_g, g_final_g, g_conv_full, g_ln_g, g_ln_b, g_sgu_b, g_scale_full, loss_row = _unpack_rows(reduced[0],
                                                                                                     small_shapes)
    g_sgu_w = jnp.stack(reduced[1:])
    loss = loss_row[0, 0]
    g_conv = shard_cols(g_conv_full, D // 4)
    g_scale = shard_cols(g_scale_full, 2 * D // 4)

    def two_d(a):
        return a.reshape(-1, a.shape[-1])

    small = [(norm_g, g_norm_g, m_norm_g, v_norm_g),
             (ada_b, dmod_all, m_ada_b, v_ada_b),
             (two_d(ab_conv_w), two_d(g_conv), two_d(m_ab_conv_w), two_d(v_ab_conv_w)),
             (ab_ln_g, g_ln_g, m_ab_ln_g, v_ab_ln_g),
             (ab_ln_b, g_ln_b, m_ab_ln_b, v_ab_ln_b),
             (two_d(ab_sgu_w), two_d(g_sgu_w), two_d(m_ab_sgu_w), two_d(v_ab_sgu_w)),
             (two_d(ab_sgu_b), two_d(g_sgu_b), two_d(m_ab_sgu_b), two_d(v_ab_sgu_b)),
             (c_pool_scale, g_scale, m_c_pool_scale, v_c_pool_scale),
             (final_g.reshape(1, D), g_final_g, m_final_g.reshape(1, D), v_final_g.reshape(1, D))]
    small_res = _adamw_small(small)
    small_shapes_out = [norm_g.shape, ada_b.shape, ab_conv_w.shape, ab_ln_g.shape, ab_ln_b.shape, ab_sgu_w.shape,
                        ab_sgu_b.shape, c_pool_scale.shape, final_g.shape]
    (r_norm_g, r_ada_b, r_conv, r_ln_g, r_ln_b, r_sgu_w, r_sgu_b, r_scale, r_final_g) = [
        tuple(a.reshape(shp) for a in res) for res, shp in zip(small_res, small_shapes_out)]

    order = [r_norm_g, r_ada_w, r_ada_b, r_ab_w_in, r_conv, r_ln_g, r_ln_b, r_sgu_w, r_sgu_b, r_ab_w_out,
             r_c_w_in, r_c_pool_w, r_scale, r_c_w_out, r_final_g]
    outs = [loss, grad_x[None]]
    for field in range(4):
        outs += [r[field] for r in order]
    return tuple(outs)
```

```python
import functools

import jax
import jax.numpy as jnp
from jax import lax
from jax.experimental import pallas as pl
from jax.experimental.pallas import tpu as pltpu

f32, bf16 = jnp.float32, jnp.bfloat16

D = 1024
HEAD = 128
NH = 8
WINDOWS = (2, 4, 8, 16)
GC = 512
EPS = 1e-6
HALO_CONV = 8
HALO_POOL = 16
CHUNK_ROWS = 512
DH_WIDE = 1024
FWD_TILES = 2
N_DEV = 8
LANES = 128

ADAM_LR, ADAM_B1, ADAM_B2, ADAM_EPS, ADAM_WD, ADAM_STEP = 0.001, 0.9, 0.999, 1e-08, 0.01, 10

MESH = pl.DeviceIdType.MESH
ANY = pl.BlockSpec(memory_space=pl.ANY)
VMEM = pl.BlockSpec(memory_space=pltpu.VMEM)
MIB = 2 ** 20


def _pcall(body, *, name, out_shape, grid=None, in_specs=None, out_specs=None, scratch=(), vmem_mb=None,
           aliases=None, prefetch=0):
    kw = {}
    if prefetch:
        kw["grid_spec"] = pltpu.PrefetchScalarGridSpec(num_scalar_prefetch=prefetch, grid=grid, in_specs=in_specs,
                                                       out_specs=out_specs, scratch_shapes=list(scratch))
    else:
        if grid is not None:
            kw["grid"] = grid
        if in_specs is not None:
            kw["in_specs"] = in_specs
        if out_specs is not None:
            kw["out_specs"] = out_specs
        if scratch:
            kw["scratch_shapes"] = list(scratch)
    if aliases:
        kw["input_output_aliases"] = aliases
    params = pltpu.CompilerParams(vmem_limit_bytes=None if vmem_mb is None else vmem_mb * MIB)
    return pl.pallas_call(body, name=name, out_shape=out_shape, compiler_params=params, **kw)


def _sds(shape, dtype):
    return jax.ShapeDtypeStruct(tuple(shape), dtype)


def _sigmoid(z):
    return pl.reciprocal(1.0 + jnp.exp(-z), approx=True)


def _silu(z):
    return z * _sigmoid(z)


def _silu_and_grad(z):
    s = _sigmoid(z)
    return z * s, s * (1.0 + z * (1.0 - s))


def _place():
    return lax.axis_index("x"), lax.axis_index("y"), lax.axis_index("c")


def _gather8(blk, name, after=()):
    def body(x_ref, *rest):
        o_ref, ssem, rsem = rest[len(after):]
        x, y, c = _place()
        me = 4 * x + 2 * y + c
        o_ref[me] = x_ref[...]
        sends = []
        for k in range(1, N_DEV):
            px = 1 - x if k & 4 else x
            py = 1 - y if k & 2 else y
            pc = 1 - c if k & 1 else c
            cp = pltpu.make_async_remote_copy(src_ref=x_ref, dst_ref=o_ref.at[me], send_sem=ssem.at[k - 1],
                                              recv_sem=rsem.at[k - 1], device_id=(px, py, pc), device_id_type=MESH)
            cp.start()
            sends.append((cp, 4 * px + 2 * py + pc))
        for k, (cp, peer) in enumerate(sends):
            pltpu.make_async_remote_copy(src_ref=x_ref, dst_ref=o_ref.at[peer], send_sem=ssem.at[k],
                                         recv_sem=rsem.at[k], device_id=(x, y, c), device_id_type=MESH).wait_recv()
        for cp, _ in sends:
            cp.wait_send()

    return _pcall(body, name=name, out_shape=_sds((N_DEV,) + blk.shape, blk.dtype), in_specs=[VMEM] + [ANY] * len(after),
                  out_specs=VMEM,
                  scratch=[pltpu.SemaphoreType.DMA((N_DEV - 1,)), pltpu.SemaphoreType.DMA((N_DEV - 1,))])(blk, *after)


def _allreduce8(bufs, name, after=None):
    n, n_after = len(bufs), 0 if after is None else 1
    rbs = [b.shape[0] // N_DEV for b in bufs]
    assert all(rb * N_DEV == b.shape[0] and rb % 8 == 0 for rb, b in zip(rbs, bufs))

    def body(*refs):
        refs = refs[:n] + refs[n + n_after:]
        xs, outs, stages = refs[:n], refs[n:2 * n], refs[2 * n:3 * n]
        ssem, rsem = refs[3 * n:]
        x, y, c = _place()
        me = 4 * x + 2 * y + c
        peers = []
        for k in range(1, N_DEV):
            px = 1 - x if k & 4 else x
            py = 1 - y if k & 2 else y
            pc = 1 - c if k & 1 else c
            peers.append(((px, py, pc), 4 * px + 2 * py + pc))

        def blk(t, ref, idx):
            return ref.at[pl.ds(pl.multiple_of(idx * rbs[t], 8), rbs[t]), :]

        def copy(t, phase, k, src, dst, dev):
            return pltpu.make_async_remote_copy(src_ref=src, dst_ref=dst, send_sem=ssem.at[t, phase, k],
                                                recv_sem=rsem.at[t, phase, k], device_id=dev, device_id_type=MESH)

        scatter = [copy(t, 0, k, blk(t, xs[t], pidx), stages[t].at[me], dev)
                   for t in range(n) for k, (dev, pidx) in enumerate(peers)]
        for cp in scatter:
            cp.start()
        gather = []
        for t in range(n):
            stages[t][me] = blk(t, xs[t], me)[...]
            for k, (dev, pidx) in enumerate(peers):
                copy(t, 0, k, blk(t, xs[t], pidx), stages[t].at[pidx], dev).wait_recv()
            total = stages[t][0]
            for j in range(1, N_DEV):
                total = total + stages[t][j]
            blk(t, outs[t], me)[...] = total
            sends = [copy(t, 1, k, blk(t, outs[t], me), blk(t, outs[t], me), dev) for k, (dev, pidx) in enumerate(peers)]
            for cp in sends:
                cp.start()
            gather += sends
        for t in range(n):
            for k, (dev, pidx) in enumerate(peers):
                copy(t, 1, k, blk(t, outs[t], pidx), blk(t, outs[t], pidx), dev).wait_recv()
        for cp in scatter + gather:
            cp.wait_send()

    return _pcall(body, name=name, out_shape=[_sds(b.shape, f32) for b in bufs], in_specs=[VMEM] * n + [ANY] * n_after,
                  out_specs=[VMEM] * n,
                  scratch=[pltpu.VMEM((N_DEV, rb, LANES), f32) for rb in rbs]
                  + [pltpu.SemaphoreType.DMA((n, 2, N_DEV - 1)), pltpu.SemaphoreType.DMA((n, 2, N_DEV - 1))])(
                      *bufs, *([] if after is None else [after]))


def _other_chips(x, y):
    return [((1 - x, y), 2 * (1 - x) + y), ((x, 1 - y), 2 * x + (1 - y)), ((1 - x, 1 - y), 2 * (1 - x) + (1 - y))]


HBM = pl.BlockSpec(memory_space=pltpu.HBM)
SEM = pl.BlockSpec(memory_space=pltpu.SEMAPHORE)
EFFECT = pltpu.SideEffectType.DATAFLOW_SIDE_EFFECTING


def _in_hbm(a):
    return pltpu.with_memory_space_constraint(a, pltpu.HBM)


def _ag_start(layers, after, name):
    flat = [t for lay in layers for t in lay]
    n, nl = len(flat), len(layers)

    def body(*refs):
        src = refs[:n]
        sems = refs[n + 1:n + 1 + 2 * nl]
        token = refs[-1]
        x, y, c = _place()
        s_me = 2 * x + y
        t = 0
        for i, lay in enumerate(layers):
            for k in range(len(lay)):
                for j, ((px, py), _) in enumerate(_other_chips(x, y)):
                    pltpu.make_async_remote_copy(src_ref=src[t].at[s_me, c], dst_ref=src[t].at[s_me, c],
                                                 send_sem=sems[2 * i].at[3 * k + j], recv_sem=sems[2 * i + 1].at[3 * k + j],
                                                 device_id=(px, py, c), device_id_type=MESH).start()
                t += 1
        token[...] = jnp.zeros_like(token)

    sem_shapes = [pltpu.SemaphoreType.DMA((3 * len(lay),)) for lay in layers for _ in range(2)]
    out_shape = sem_shapes + [pltpu.HBM(t.shape, t.dtype) for t in flat] + [_sds((8, LANES), f32)]
    outs = pl.pallas_call(
        body, name=name, out_shape=out_shape, in_specs=[HBM] * n + [ANY],
        out_specs=[SEM] * (2 * nl) + [HBM] * n + [VMEM], input_output_aliases={t: 2 * nl + t for t in range(n)},
        compiler_params=pltpu.CompilerParams(has_side_effects=EFFECT))(*[_in_hbm(t) for t in flat], after)
    sems = [(outs[2 * i], outs[2 * i + 1]) for i in range(nl)]
    thru, t = [], 2 * nl
    for lay in layers:
        thru.append(list(outs[t:t + len(lay)]))
        t += len(lay)
    return sems, thru, outs[-1]


def _ag_wait(inflight, sems, after, name):
    n = len(inflight)

    def body(*refs):
        src, ssem, rsem = refs[:n], refs[n], refs[n + 1]
        x, y, c = _place()
        s_me = 2 * x + y
        for k in range(n):
            for j, (_, s_p) in enumerate(_other_chips(x, y)):
                cp = pltpu.make_async_remote_copy(src_ref=src[k].at[s_me, c], dst_ref=src[k].at[s_p, c],
                                                  send_sem=ssem.at[3 * k + j], recv_sem=rsem.at[3 * k + j],
                                                  device_id=(x, y, c), device_id_type=MESH)
                cp.wait_send()
                cp.wait_recv()

    return pl.pallas_call(
        body, name=name, out_shape=[pltpu.HBM(t.shape, t.dtype) for t in inflight],
        in_specs=[HBM] * n + [SEM, SEM, ANY], out_specs=[HBM] * n, input_output_aliases={t: t for t in range(n)},
        compiler_params=pltpu.CompilerParams(has_side_effects=EFFECT))(*inflight, sems[0], sems[1], after)


def _ag_forward(arrived, name):
    n = len(arrived)

    def body(*refs):
        o = refs[n:2 * n]
        ssem, rsem = refs[2 * n:]
        x, y, c = _place()

        def copy(t, j, s, half, dev):
            return pltpu.make_async_remote_copy(src_ref=o[t].at[s, c], dst_ref=o[t].at[s, half], send_sem=ssem.at[t, j],
                                                recv_sem=rsem.at[t, j], device_id=dev, device_id_type=MESH)

        chips = _other_chips(x, y)
        sends = [copy(t, j, s_p, c, (x, y, 1 - c)) for t in range(n) for j, (_, s_p) in enumerate(chips)]
        for cp in sends:
            cp.start()
        for t in range(n):
            for j, (_, s_p) in enumerate(chips):
                copy(t, j, s_p, 1 - c, (x, y, c)).wait_recv()
        for cp in sends:
            cp.wait_send()

    return _pcall(body, name=name, out_shape=[_sds(p.shape, bf16) for p in arrived], in_specs=[ANY] * n,
                  out_specs=[ANY] * n, aliases={t: t for t in range(n)},
                  scratch=[pltpu.SemaphoreType.DMA((n, 3)), pltpu.SemaphoreType.DMA((n, 3))])(*arrived)


def _agf_start(arrived, name):
    n = len(arrived)

    def body(*refs):
        o = refs[:n]
        ssem, rsem, token = refs[n], refs[n + 1], refs[-1]
        x, y, c = _place()
        for t in range(n):
            for j, (_, s_p) in enumerate(_other_chips(x, y)):
                pltpu.make_async_remote_copy(src_ref=o[t].at[s_p, c], dst_ref=o[t].at[s_p, c],
                                             send_sem=ssem.at[3 * t + j], recv_sem=rsem.at[3 * t + j],
                                             device_id=(x, y, 1 - c), device_id_type=MESH).start()
        token[...] = jnp.zeros_like(token)

    out_shape = ([pltpu.SemaphoreType.DMA((3 * n,))] * 2 + [pltpu.HBM(a.shape, bf16) for a in arrived]
                 + [_sds((8, LANES), f32)])
    outs = pl.pallas_call(
        body, name=name, out_shape=out_shape, in_specs=[HBM] * n, out_specs=[SEM, SEM] + [HBM] * n + [VMEM],
        input_output_aliases={t: 2 + t for t in range(n)},
        compiler_params=pltpu.CompilerParams(has_side_effects=EFFECT))(*[_in_hbm(a) for a in arrived])
    return (outs[0], outs[1]), list(outs[2:2 + n]), outs[-1]


def _agf_wait(sems, inflight, after, name):
    n = len(inflight)

    def body(*refs):
        o, ssem, rsem = refs[:n], refs[n], refs[n + 1]
        x, y, c = _place()
        for t in range(n):
            for j, (_, s_p) in enumerate(_other_chips(x, y)):
                cp = pltpu.make_async_remote_copy(src_ref=o[t].at[s_p, c], dst_ref=o[t].at[s_p, 1 - c],
                                                  send_sem=ssem.at[3 * t + j], recv_sem=rsem.at[3 * t + j],
                                                  device_id=(x, y, c), device_id_type=MESH)
                cp.wait_send()
                cp.wait_recv()

    return pl.pallas_call(
        body, name=name, out_shape=[pltpu.HBM(a.shape, bf16) for a in inflight],
        in_specs=[HBM] * n + [SEM, SEM, ANY], out_specs=[HBM] * n, input_output_aliases={t: t for t in range(n)},
        compiler_params=pltpu.CompilerParams(has_side_effects=EFFECT))(*inflight, sems[0], sems[1], after)


def _rs_pair_start(grads, name):
    n = len(grads)

    def body(*refs):
        g, theirs = refs[:n], refs[n:2 * n]
        ssem, rsem, token = refs[2 * n], refs[2 * n + 1], refs[-1]
        x, y, c = _place()
        for t in range(n):
            pltpu.make_async_remote_copy(src_ref=g[t].at[:, 1 - c], dst_ref=theirs[t], send_sem=ssem.at[t],
                                         recv_sem=rsem.at[t], device_id=(x, y, 1 - c), device_id_type=MESH).start()
        token[...] = jnp.zeros_like(token)

    lands = [lax.empty((4,) + g.shape[2:], bf16) for g in grads]
    out_shape = ([pltpu.SemaphoreType.DMA((n,))] * 2 + [pltpu.HBM(g.shape, bf16) for g in grads]
                 + [pltpu.HBM(q.shape, bf16) for q in lands] + [_sds((8, LANES), f32)])
    outs = pl.pallas_call(
        body, name=name, out_shape=out_shape, in_specs=[HBM] * (2 * n), out_specs=[SEM, SEM] + [HBM] * (2 * n) + [VMEM],
        input_output_aliases={t: 2 + t for t in range(2 * n)},
        compiler_params=pltpu.CompilerParams(has_side_effects=EFFECT))(*[_in_hbm(a) for a in list(grads) + lands])
    return (outs[0], outs[1]), list(outs[2:2 + n]), list(outs[2 + n:2 + 2 * n]), outs[-1]


def _rs_pair_wait(sems, grads, lands, after, name):
    n = len(grads)

    def body(*refs):
        g, theirs = refs[:n], refs[n:2 * n]
        ssem, rsem = refs[2 * n], refs[2 * n + 1]
        x, y, c = _place()
        for t in range(n):
            cp = pltpu.make_async_remote_copy(src_ref=g[t].at[:, 1 - c], dst_ref=theirs[t], send_sem=ssem.at[t],
                                              recv_sem=rsem.at[t], device_id=(x, y, c), device_id_type=MESH)
            cp.wait_send()
            cp.wait_recv()

    outs = pl.pallas_call(
        body, name=name, out_shape=[pltpu.HBM(a.shape, bf16) for a in list(grads) + list(lands)],
        in_specs=[HBM] * (2 * n) + [SEM, SEM] + [ANY] * len(after), out_specs=[HBM] * (2 * n),
        input_output_aliases={t: t for t in range(2 * n)},
        compiler_params=pltpu.CompilerParams(has_side_effects=EFFECT))(*grads, *lands, sems[0], sems[1], *after)
    return list(outs[:n]), list(outs[n:])


def _rs_chip_start(pairs, name):
    n = len(pairs)

    def body(*refs):
        p, q = refs[:n], refs[n:2 * n]
        ssem, rsem, token = refs[2 * n], refs[2 * n + 1], refs[-1]
        x, y, c = _place()
        for t in range(n):
            for j, ((px, py), s_p) in enumerate(_other_chips(x, y)):
                pltpu.make_async_remote_copy(src_ref=p[t].at[s_p], dst_ref=q[t].at[j], send_sem=ssem.at[3 * t + j],
                                             recv_sem=rsem.at[3 * t + j], device_id=(px, py, c), device_id_type=MESH).start()
        token[...] = jnp.zeros_like(token)

    lands = [lax.empty((3,) + p.shape[1:], bf16) for p in pairs]
    out_shape = ([pltpu.SemaphoreType.DMA((3 * n,))] * 2 + [pltpu.HBM(p.shape, bf16) for p in pairs]
                 + [pltpu.HBM(q.shape, bf16) for q in lands] + [_sds((8, LANES), f32)])
    outs = pl.pallas_call(
        body, name=name, out_shape=out_shape, in_specs=[HBM] * (2 * n), out_specs=[SEM, SEM] + [HBM] * (2 * n) + [VMEM],
        input_output_aliases={t: 2 + t for t in range(2 * n)},
        compiler_params=pltpu.CompilerParams(has_side_effects=EFFECT))(*[_in_hbm(a) for a in list(pairs) + lands])
    return (outs[0], outs[1]), list(outs[2:2 + n]), list(outs[2 + n:2 + 2 * n]), outs[-1]


def _rs_chip_wait(sems, pairs, lands, after, name):
    n = len(pairs)

    def body(*refs):
        p, q = refs[:n], refs[n:2 * n]
        ssem, rsem = refs[2 * n], refs[2 * n + 1]
        x, y, c = _place()
        for t in range(n):
            for j, (_, s_p) in enumerate(_other_chips(x, y)):
                cp = pltpu.make_async_remote_copy(src_ref=p[t].at[s_p], dst_ref=q[t].at[j], send_sem=ssem.at[3 * t + j],
                                                  recv_sem=rsem.at[3 * t + j], device_id=(x, y, c), device_id_type=MESH)
                cp.wait_send()
                cp.wait_recv()

    outs = pl.pallas_call(
        body, name=name, out_shape=[pltpu.HBM(a.shape, bf16) for a in list(pairs) + list(lands)],
        in_specs=[HBM] * (2 * n) + [SEM, SEM] + [ANY] * len(after), out_specs=[HBM] * (2 * n),
        input_output_aliases={t: t for t in range(2 * n)},
        compiler_params=pltpu.CompilerParams(has_side_effects=EFFECT))(*pairs, *lands, sems[0], sems[1], *after)
    return list(outs[:n]), list(outs[n:])


def _rs_half_start(halves, name):
    n = len(halves)

    def body(*refs):
        o = refs[:n]
        ssem, rsem, token = refs[n], refs[n + 1], refs[-1]
        x, y, c = _place()
        for t in range(n):
            pltpu.make_async_remote_copy(src_ref=o[t].at[c], dst_ref=o[t].at[c], send_sem=ssem.at[t],
                                         recv_sem=rsem.at[t], device_id=(x, y, 1 - c), device_id_type=MESH).start()
        token[...] = jnp.zeros_like(token)

    out_shape = ([pltpu.SemaphoreType.DMA((n,))] * 2 + [pltpu.HBM(h.shape, h.dtype) for h in halves]
                 + [_sds((8, LANES), f32)])
    outs = pl.pallas_call(
        body, name=name, out_shape=out_shape, in_specs=[HBM] * n, out_specs=[SEM, SEM] + [HBM] * n + [VMEM],
        input_output_aliases={t: 2 + t for t in range(n)},
        compiler_params=pltpu.CompilerParams(has_side_effects=EFFECT))(*[_in_hbm(h) for h in halves])
    return (outs[0], outs[1]), list(outs[2:2 + n]), outs[-1]


def _rs_half_wait(sems, inflight, after, name):
    n = len(inflight)

    def body(*refs):
        o, ssem, rsem = refs[:n], refs[n], refs[n + 1]
        x, y, c = _place()
        for t in range(n):
            cp = pltpu.make_async_remote_copy(src_ref=o[t].at[c], dst_ref=o[t].at[1 - c], send_sem=ssem.at[t],
                                              recv_sem=rsem.at[t], device_id=(x, y, c), device_id_type=MESH)
            cp.wait_send()
            cp.wait_recv()

    return pl.pallas_call(
        body, name=name, out_shape=[pltpu.HBM(h.shape, h.dtype) for h in inflight],
        in_specs=[HBM] * n + [SEM, SEM, ANY], out_specs=[HBM] * n, input_output_aliases={t: t for t in range(n)},
        compiler_params=pltpu.CompilerParams(has_side_effects=EFFECT))(*inflight, sems[0], sems[1], after)


def _row_spec(tm, cols):
    return pl.BlockSpec((tm, cols), lambda i: (i, 0))


def _vec_spec(cols, rows=1):
    return pl.BlockSpec((rows, cols), lambda i: (0, 0))


def _modulated_norm(xv, g, shift, scale):
    r = lax.rsqrt(jnp.mean(xv * xv, axis=-1, keepdims=True) + EPS)
    return (((xv * r) * g) * (1.0 + scale) + shift).astype(bf16)


def _hnorm(x, g, shift, scale):
    T, tm = x.shape[0], 256

    def body(x_ref, g_ref, sh_ref, sc_ref, h_ref):
        h_ref[...] = _modulated_norm(x_ref[...], g_ref[...], sh_ref[...], sc_ref[...])

    return _pcall(body, name="hnorm", out_shape=_sds((T, D), bf16), grid=(T // tm,),
                  in_specs=[_row_spec(tm, D), _vec_spec(D), _vec_spec(D), _vec_spec(D)],
                  out_specs=_row_spec(tm, D))(x, g, shift, scale)


def _out_proj(y2, wo, x, gate, nxt=None):
    T, tm = x.shape[0], 512

    def body(y_ref, w_ref, x_ref, g_ref, *rest):
        o = jnp.dot(y_ref[0], w_ref[0], preferred_element_type=f32)
        o = o + jnp.dot(y_ref[1], w_ref[1], preferred_element_type=f32)
        xo = x_ref[...] + g_ref[...] * o
        if nxt is None:
            xo_ref, o_ref = rest
        else:
            ng_ref, nsh_ref, nsc_ref, xo_ref, o_ref, h_ref = rest
            h_ref[...] = _modulated_norm(xo, ng_ref[...], nsh_ref[...], nsc_ref[...])
        o_ref[...] = o.astype(bf16)
        xo_ref[...] = xo

    extra = [] if nxt is None else list(nxt)
    n_out = 2 if nxt is None else 3
    return _pcall(body, name="out_proj", out_shape=[_sds((T, D), f32), _sds((T, D), bf16), _sds((T, D), bf16)][:n_out],
                  grid=(T // tm,),
                  in_specs=[pl.BlockSpec((2, tm, D), lambda i: (0, i, 0)), pl.BlockSpec((2, D, D), lambda i: (0, 0, 0)),
                            _row_spec(tm, D), _vec_spec(D)] + [_vec_spec(D)] * len(extra),
                  out_specs=[_row_spec(tm, D)] * n_out, vmem_mb=40)(y2, wo, x, gate, *extra)


def _gate_bwd_tile(dx, o_ref, gate_ref, dob_ref, dgate_ref):
    dob_ref[...] = (dx * gate_ref[...]).astype(bf16)
    dgate_ref[...] += jnp.sum(dx * o_ref[...].astype(f32), axis=0, keepdims=True)


def _loss_bwd(x, target, g, o, gate):
    T, tm = x.shape[0], 512

    def body(x_ref, t_ref, g_ref, o_ref, gate_ref, dx_ref, loss_ref, dg_ref, dob_ref, dgate_ref):
        @pl.when(pl.program_id(0) == 0)
        def _():
            loss_ref[...] = jnp.zeros_like(loss_ref)
            dg_ref[...] = jnp.zeros_like(dg_ref)
            dgate_ref[...] = jnp.zeros_like(dgate_ref)

        xv, gv = x_ref[...], g_ref[...]
        r = lax.rsqrt(jnp.mean(xv * xv, axis=-1, keepdims=True) + EPS)
        xn = xv * r
        err = xn * gv - t_ref[...]
        dy = err * (1.0 / D)
        dxn = dy * gv
        dx = r * (dxn - xn * jnp.mean(dxn * xn, axis=-1, keepdims=True))
        dx_ref[...] = dx
        dg_ref[...] += jnp.sum(dy * xn, axis=0, keepdims=True)
        loss_ref[...] += (0.5 / D) * jnp.sum(jnp.sum(err * err, axis=1, keepdims=True), axis=0, keepdims=True)
        _gate_bwd_tile(dx, o_ref, gate_ref, dob_ref, dgate_ref)

    return _pcall(body, name="loss_bwd",
                  out_shape=[_sds((T, D), f32), _sds((1, 1), f32), _sds((1, D), f32), _sds((T, D), bf16), _sds((1, D), f32)],
                  grid=(T // tm,),
                  in_specs=[_row_spec(tm, D), _row_spec(tm, D), _vec_spec(D), _row_spec(tm, D), _vec_spec(D)],
                  out_specs=[_row_spec(tm, D), pl.BlockSpec((1, 1), lambda i: (0, 0)), _vec_spec(D), _row_spec(tm, D),
                             _vec_spec(D)])(x, target, g, o, gate)


def _norm_bwd(x, dh, gin, g, scale, below=None):
    T, tm = x.shape[0], 512

    def body(x_ref, dh_ref, gin_ref, g_ref, sc_ref, *rest):
        if below is None:
            dx_ref, st_ref = rest
        else:
            o_ref, gate_ref, dx_ref, st_ref, dob_ref, dgate_ref = rest

        @pl.when(pl.program_id(0) == 0)
        def _():
            st_ref[...] = jnp.zeros_like(st_ref)
            if below is not None:
                dgate_ref[...] = jnp.zeros_like(dgate_ref)

        xv, gv, dhv = x_ref[...], g_ref[...], dh_ref[...]
        r = lax.rsqrt(jnp.mean(xv * xv, axis=-1, keepdims=True) + EPS)
        xn = xv * r
        da = dhv * (1.0 + sc_ref[...])
        dxn = da * gv
        dx = gin_ref[...] + r * (dxn - xn * jnp.mean(dxn * xn, axis=-1, keepdims=True))
        dx_ref[...] = dx
        st_ref[0:1, :] += jnp.sum(dhv, axis=0, keepdims=True)
        st_ref[1:2, :] += jnp.sum(dhv * (xn * gv), axis=0, keepdims=True)
        st_ref[2:3, :] += jnp.sum(da * xn, axis=0, keepdims=True)
        if below is not None:
            _gate_bwd_tile(dx, o_ref, gate_ref, dob_ref, dgate_ref)

    out_shape = [_sds((T, D), f32), _sds((8, D), f32)]
    in_specs = [_row_spec(tm, D), _row_spec(tm, D), _row_spec(tm, D), _vec_spec(D), _vec_spec(D)]
    out_specs = [_row_spec(tm, D), _vec_spec(D, 8)]
    args = [x, dh, gin, g, scale]
    if below is not None:
        out_shape += [_sds((T, D), bf16), _sds((1, D), f32)]
        in_specs += [_row_spec(tm, D), _vec_spec(D)]
        out_specs += [_row_spec(tm, D), _vec_spec(D)]
        args += list(below)
    return _pcall(body, name="norm_bwd", out_shape=out_shape, grid=(T // tm,), in_specs=in_specs,
                  out_specs=out_specs)(*args)


STEPS = 4
ADAMW_STEPS = 8


def _cast_place(place, ws, layer, after=None):
    n = len(ws)

    def body(place_ref, *refs):
        for t in range(n):
            refs[-n + t][...] = refs[t][...].astype(bf16)

    def tile(w):
        return w.shape[1] // STEPS, w.shape[2]

    extra = [] if after is None else [after]
    return _pcall(body, name="cast_place", out_shape=[_sds((4,) + w.shape[1:], bf16) for w in ws], grid=(STEPS,),
                  prefetch=1,
                  in_specs=[pl.BlockSpec((None,) + tile(w), lambda i, pr: (layer, i, 0)) for w in ws] + [ANY] * len(extra),
                  out_specs=[pl.BlockSpec((None,) + tile(w), lambda i, pr: (pr[0], i, 0)) for w in ws])(
                      place, *ws, *extra)


def _rs_add(place, grads, theirs):
    n = len(grads)

    def body(place_ref, *refs):
        for t in range(n):
            refs[2 * n + t][...] = (refs[t][...].astype(f32) + refs[n + t][...].astype(f32)).astype(bf16)

    def tile(q):
        return q.shape[1] // 2, q.shape[2]

    mine = [pl.BlockSpec((None, None) + tile(q), lambda s, i, pr: (s, pr[1], i, 0)) for q in theirs]
    shard = [pl.BlockSpec((None,) + tile(q), lambda s, i, pr: (s, i, 0)) for q in theirs]
    return _pcall(body, name="rs_add", out_shape=[_sds(q.shape, bf16) for q in theirs], grid=(4, 2), prefetch=1,
                  in_specs=mine + shard, out_specs=shard)(place, *grads, *theirs)


def _rs_sum(place, pairs, slots):
    n, steps = len(pairs), 4

    def body(place_ref, *refs):
        for t in range(n):
            p_ref, q_ref = refs[t], refs[n + t]
            total = ((p_ref[...].astype(f32) + q_ref[0].astype(f32)) + q_ref[1].astype(f32)) + q_ref[2].astype(f32)
            refs[2 * n + t][...] = total.astype(bf16)

    def tile(q):
        return q.shape[1] // steps, q.shape[2]

    return _pcall(body, name="rs_sum", out_shape=[_sds((2,) + q.shape[1:], bf16) for q in slots], grid=(steps,),
                  prefetch=1,
                  in_specs=[pl.BlockSpec((None,) + tile(q), lambda i, pr: (pr[0], i, 0)) for q in slots]
                  + [pl.BlockSpec((3,) + tile(q), lambda i, pr: (0, i, 0)) for q in slots],
                  out_specs=[pl.BlockSpec((None,) + tile(q), lambda i, pr: (pr[1], i, 0)) for q in slots])(
                      place, *pairs, *slots)


def _adamw_math(w, g, m, v):
    m = ADAM_B1 * m + (1.0 - ADAM_B1) * g
    v = ADAM_B2 * v + (1.0 - ADAM_B2) * jnp.square(g)
    m_hat = m / (1.0 - ADAM_B1 ** ADAM_STEP)
    v_hat = v / (1.0 - ADAM_B2 ** ADAM_STEP)
    delta = -ADAM_LR * (m_hat / (jnp.sqrt(v_hat) + ADAM_EPS) + ADAM_WD * w)
    return delta, m, v


def _adamw_layer(layer, items):
    n = len(items)

    def body(*refs):
        outs = refs[-4 * n:]
        for t in range(n):
            w_ref, g_ref, m_ref, v_ref = refs[4 * t:4 * t + 4]
            g = g_ref[...].astype(f32)
            outs[4 * t][...] = g
            outs[4 * t + 1][...], outs[4 * t + 2][...], outs[4 * t + 3][...] = _adamw_math(
                w_ref[...], g, m_ref[...], v_ref[...])

    args, in_specs, out_specs, out_shape = [], [], [], []
    for w, g, m, v, _ in items:
        tr, cols = w.shape[1] // ADAMW_STEPS, w.shape[2]
        spec = pl.BlockSpec((None, tr, cols), lambda i: (layer, i, 0))
        args += [w, g, m, v]
        in_specs += [spec, pl.BlockSpec((tr, cols), lambda i: (i, 0)), spec, spec]
        out_specs += [spec] * 4
        out_shape += [_sds(w.shape, f32)] * 4
    aliases = {}
    for t, it in enumerate(items):
        if it[4] is not None:
            for k in range(4):
                aliases[len(args)] = 4 * t + k
                args.append(it[4][k])
                in_specs.append(ANY)
    res = _pcall(body, name="adamw", out_shape=out_shape, grid=(ADAMW_STEPS,), in_specs=in_specs, out_specs=out_specs,
                 aliases=aliases)(*args)
    return [tuple(res[4 * t:4 * t + 4]) for t in range(n)]


def _adamw_small(items):
    n = len(items)

    def body(*refs):
        ins, outs = refs[:4 * n], refs[4 * n:]
        for t in range(n):
            w_ref, g_ref, m_ref, v_ref = ins[4 * t:4 * t + 4]
            if len(g_ref.shape) == len(w_ref.shape) + 1:
                g = g_ref[0]
                for b in range(1, g_ref.shape[0]):
                    g = g + g_ref[b]
            else:
                g = g_ref[...]
            d, m, v = _adamw_math(w_ref[...], g, m_ref[...], v_ref[...])
            outs[4 * t][...], outs[4 * t + 1][...], outs[4 * t + 2][...], outs[4 * t + 3][...] = g, d, m, v

    out_shape = [_sds(w.shape, f32) for (w, _, _, _) in items for _ in range(4)]
    flat = [a for it in items for a in it]
    res = _pcall(body, name="adamw_small", out_shape=out_shape, in_specs=[VMEM] * (4 * n),
                 out_specs=[VMEM] * (4 * n))(*flat)
    return [tuple(res[4 * t:4 * t + 4]) for t in range(n)]


NN = ((1,), (0,))
NT = ((1,), (1,))
TN = ((0,), (0,))


def _mm(name, a, b, *, grid, a_spec, b_spec, out_shape, out_spec, dims, vmem_mb=None):
    def body(a_ref, b_ref, o_ref):
        r = lax.dot_general(a_ref[...], b_ref[...], (dims, ((), ())), preferred_element_type=f32)
        o_ref[...] = r.astype(o_ref.dtype)

    return _pcall(body, name=name, out_shape=out_shape, grid=grid, in_specs=[a_spec, b_spec], out_specs=out_spec,
                  vmem_mb=vmem_mb)(a, b)


def _whole(shape):
    return pl.BlockSpec(shape, lambda j: (0,) * len(shape))


def _split_spec(rows, tile, per_split):
    return pl.BlockSpec((None, rows, tile), lambda j: (j // per_split, 0, j % per_split))


class _Proj:
    def __init__(self, n, splits, tile):
        self.n, self.splits, self.tile = n, splits, tile
        self.steps = n // tile
        self.w_per = n // 4 // tile
        self.a_per = n // splits // tile
        assert self.w_per * tile * 4 == n and self.a_per * tile * splits == n

    def fwd(self, hb, wg):
        T = hb.shape[0]
        sub, tile, w_per = FWD_TILES, self.tile, self.w_per
        wide = sub * tile
        a_per = self.n // self.splits // wide
        assert a_per * wide * self.splits == self.n

        def w_tile(q):
            return pl.BlockSpec((None, D, tile), lambda j: ((sub * j + q) // w_per, 0, (sub * j + q) % w_per))

        def body(a_ref, *rest):
            w = jnp.concatenate([rest[q][...] for q in range(sub)], axis=1)
            rest[sub][...] = jnp.dot(a_ref[...], w, preferred_element_type=f32).astype(bf16)

        return _pcall(body, name="proj_fwd", out_shape=_sds((self.splits, T, self.n // self.splits), bf16),
                      grid=(self.n // wide,), in_specs=[_whole((T, D))] + [w_tile(q) for q in range(sub)],
                      out_specs=pl.BlockSpec((None, T, wide), lambda j: (j // a_per, 0, j % a_per)),
                      vmem_mb=40 if wide > 512 else None)(hb, *([wg] * sub))

    def dw(self, hb, dp):
        T = hb.shape[0]
        return _mm("proj_dw", hb, dp, grid=(self.steps,), a_spec=_whole((T, D)),
                   b_spec=_split_spec(T, self.tile, self.a_per), out_shape=_sds((4, D, self.n // 4), bf16),
                   out_spec=_split_spec(D, self.tile, self.w_per), dims=TN)

    def dh(self, dp, wg):
        T = dp.shape[1]
        sub, tile, w_per = DH_WIDE // self.tile, self.tile, self.w_per
        a_per = self.n // self.splits // DH_WIDE
        assert sub * tile == DH_WIDE and a_per * DH_WIDE * self.splits == self.n

        def w_tile(q):
            return pl.BlockSpec((None, D, tile), lambda k: ((sub * k + q) // w_per, 0, (sub * k + q) % w_per))

        def body(a_ref, *rest):
            o_ref = rest[sub]
            w = jnp.concatenate([rest[q][...] for q in range(sub)], axis=1)
            r = lax.dot_general(a_ref[...], w, (NT, ((), ())), preferred_element_type=f32)

            @pl.when(pl.program_id(0) == 0)
            def _():
                o_ref[...] = r

            @pl.when(pl.program_id(0) > 0)
            def _():
                o_ref[...] += r

        return _pcall(body, name="proj_dh", out_shape=_sds((T, D), f32), grid=(self.n // DH_WIDE,),
                      in_specs=[pl.BlockSpec((None, T, DH_WIDE), lambda k: (k // a_per, 0, k % a_per))]
                      + [w_tile(q) for q in range(sub)],
                      out_specs=_whole((T, D)), vmem_mb=40)(dp, *([wg] * sub))


EVEN_PROJ = _Proj(7 * D, 7, 256)
ODD_PROJ = _Proj(4 * D, 2, 512)


def _out_bwd(dob, wo, y2):
    T = dob.shape[0]
    w_spec = pl.BlockSpec((None, 512, D), lambda j: (j, 0, 0))

    def body(dob_ref, w_ref, y_ref, dy_ref, dw_ref):
        dob_v = dob_ref[...]
        dy_ref[...] = lax.dot_general(dob_v, w_ref[...], (NT, ((), ())), preferred_element_type=f32).astype(bf16)
        dw_ref[...] = lax.dot_general(y_ref[...], dob_v, (TN, ((), ())), preferred_element_type=f32).astype(bf16)

    return _pcall(body, name="out_bwd", out_shape=[_sds((2, T, D), bf16), _sds((4, 512, D), bf16)], grid=(4,),
                  in_specs=[_whole((T, D)), w_spec, _split_spec(T, 512, 2)],
                  out_specs=[_split_spec(T, 512, 2), w_spec])(dob, wo, y2)


def _head_spec(lead, T):
    return pl.BlockSpec((lead, T, HEAD), lambda h: (0, 0, h))


def _head_vec(rows):
    return pl.BlockSpec((rows, HEAD), lambda h: (0, h))


_HEAD_MAT = pl.BlockSpec((None, HEAD, HEAD), lambda h: (h, 0, 0))


def _causal():
    return lax.broadcasted_iota(jnp.int32, (HEAD, HEAD), 0) >= lax.broadcasted_iota(jnp.int32, (HEAD, HEAD), 1)


def _layernorm_head(v):
    mu = jnp.mean(v, axis=-1, keepdims=True)
    d = v - mu
    rstd = lax.rsqrt(jnp.mean(d * d, axis=-1, keepdims=True) + EPS)
    return d * rstd, rstd


def _even_fwd(p7, conv_w, ln_g, ln_b, sgu_w, sgu_bias):
    T, C = p7.shape[1], CHUNK_ROWS

    def body(p_ref, cw_ref, lg_ref, lb_ref, w_ref, b_ref, y_ref):
        w0, w1, w2 = cw_ref[0:1, :], cw_ref[1:2, :], cw_ref[2:3, :]
        wm = jnp.where(_causal(), w_ref[...], 0.0).astype(bf16)
        bias, lg, lb = b_ref[...], lg_ref[...], lb_ref[...]

        def step(i, halo):
            rows = pl.ds(pl.multiple_of(i * C, C), C)
            ah, ab, ac, az, u, v, zb = (p_ref[k, rows, :].astype(f32) for k in range(7))
            tt = ac * ah
            ext = jnp.concatenate([halo, tt], axis=0)
            cv = w2 * tt + w1 * pltpu.roll(ext, 1, 0)[HALO_CONV:] + w0 * pltpu.roll(ext, 2, 0)[HALO_CONV:]
            y_ref[0, rows, :] = (ab * cv * _silu(az)).astype(bf16)
            vhat, _ = _layernorm_head(v)
            vn = (vhat * lg + lb).astype(bf16)
            mix = jnp.concatenate([jnp.dot(wm, vn[k * HEAD:(k + 1) * HEAD], preferred_element_type=f32) + bias
                                   for k in range(C // HEAD)], axis=0)
            y_ref[1, rows, :] = (u * mix * _silu(zb)).astype(bf16)
            return tt[C - HALO_CONV:]

        lax.fori_loop(0, T // C, step, jnp.zeros((HALO_CONV, HEAD), f32))

    return _pcall(body, name="even_fwd", out_shape=_sds((2, T, D), bf16), grid=(NH,),
                  in_specs=[_head_spec(7, T), _head_vec(3), _head_vec(1), _head_vec(1), _HEAD_MAT, _HEAD_MAT],
                  out_specs=_head_spec(2, T))(p7, conv_w, ln_g, ln_b, sgu_w, sgu_bias)


def _even_bwd(p7, dy2, conv_w, ln_g, ln_b, sgu_w, sgu_bias):
    T, C = p7.shape[1], CHUNK_ROWS
    n_chunks = T // C

    def body(p_ref, dy_ref, cw_ref, lg_ref, lb_ref, w_ref, b_ref,
             dp_ref, dcw_ref, dlg_ref, dlb_ref, dw_ref, dms_ref, dcv_s):
        w0, w1, w2 = cw_ref[0:1, :], cw_ref[1:2, :], cw_ref[2:3, :]
        tri = _causal()
        wm = jnp.where(tri, w_ref[...], 0.0).astype(bf16)
        bias, lg, lb = b_ref[...], lg_ref[...], lb_ref[...]
        dw_ref[...] = jnp.zeros_like(dw_ref)
        dms_ref[...] = jnp.zeros_like(dms_ref)

        def fwd_step(i, carry):
            halo, a0, a1, a2, alg, alb = carry
            rows = pl.ds(pl.multiple_of(i * C, C), C)
            ah, ab, ac, az = (p_ref[k, rows, :].astype(f32) for k in range(4))
            dya = dy_ref[0, rows, :].astype(f32)
            tt = ac * ah
            ext = jnp.concatenate([halo, tt], axis=0)
            t1, t2 = pltpu.roll(ext, 1, 0)[HALO_CONV:], pltpu.roll(ext, 2, 0)[HALO_CONV:]
            cv = w2 * tt + w1 * t1 + w0 * t2
            sa, dsa = _silu_and_grad(az)
            g1 = dya * sa
            dp_ref[1, rows, :] = (g1 * cv).astype(bf16)
            dp_ref[3, rows, :] = (dya * ab * cv * dsa).astype(bf16)
            dcv = g1 * ab
            dcv_s[rows, :] = dcv
            a2 = a2 + jnp.sum(dcv * tt, axis=0, keepdims=True)
            a1 = a1 + jnp.sum(dcv * t1, axis=0, keepdims=True)
            a0 = a0 + jnp.sum(dcv * t2, axis=0, keepdims=True)

            u, zb, dyb = p_ref[4, rows, :].astype(f32), p_ref[6, rows, :].astype(f32), dy_ref[1, rows, :].astype(f32)
            vhat, rstd = _layernorm_head(p_ref[5, rows, :].astype(f32))
            vn = (vhat * lg + lb).astype(bf16)
            sb, dsb = _silu_and_grad(zb)
            mix = jnp.concatenate([jnp.dot(wm, vn[k * HEAD:(k + 1) * HEAD], preferred_element_type=f32) + bias
                                   for k in range(C // HEAD)], axis=0)
            dp_ref[4, rows, :] = (dyb * mix * sb).astype(bf16)
            dp_ref[6, rows, :] = (dyb * u * mix * dsb).astype(bf16)
            dmix = dyb * u * sb
            dvn_parts = []
            for k in range(C // HEAD):
                dm = dmix[k * HEAD:(k + 1) * HEAD]
                dmb = dm.astype(bf16)
                dvn_parts.append(lax.dot_general(wm, dmb, (TN, ((), ())), preferred_element_type=f32))
                dw_ref[...] += lax.dot_general(dmb, vn[k * HEAD:(k + 1) * HEAD], (NT, ((), ())),
                                               preferred_element_type=f32)
                dms_ref[...] += dm
            dvn = jnp.concatenate(dvn_parts, axis=0)
            alg = alg + jnp.sum(dvn * vhat, axis=0, keepdims=True)
            alb = alb + jnp.sum(dvn, axis=0, keepdims=True)
            dvh = dvn * lg
            dv = rstd * (dvh - jnp.mean(dvh, axis=-1, keepdims=True)
                         - vhat * jnp.mean(dvh * vhat, axis=-1, keepdims=True))
            dp_ref[5, rows, :] = dv.astype(bf16)
            return tt[C - HALO_CONV:], a0, a1, a2, alg, alb

        zrow = jnp.zeros((1, HEAD), f32)
        _, a0, a1, a2, alg, alb = lax.fori_loop(
            0, n_chunks, fwd_step, (jnp.zeros((HALO_CONV, HEAD), f32), zrow, zrow, zrow, zrow, zrow))
        dcw_ref[0:1, :], dcw_ref[1:2, :], dcw_ref[2:3, :] = a0, a1, a2
        dlg_ref[...], dlb_ref[...] = alg, alb
        dw_ref[...] = jnp.where(tri, dw_ref[...], 0.0)

        def bwd_step(k, halo):
            rows = pl.ds(pl.multiple_of((n_chunks - 1 - k) * C, C), C)
            dcv = dcv_s[rows, :]
            ext = jnp.concatenate([dcv, halo], axis=0)
            n1 = pltpu.roll(ext, C + HALO_CONV - 1, 0)[:C]
            n2 = pltpu.roll(ext, C + HALO_CONV - 2, 0)[:C]
            dtt = w2 * dcv + w1 * n1 + w0 * n2
            dp_ref[2, rows, :] = (dtt * p_ref[0, rows, :].astype(f32)).astype(bf16)
            dp_ref[0, rows, :] = (dtt * p_ref[2, rows, :].astype(f32)).astype(bf16)
            return dcv[:HALO_CONV]

        lax.fori_loop(0, n_chunks, bwd_step, jnp.zeros((HALO_CONV, HEAD), f32))

    out_shape = [_sds((7, T, D), bf16), _sds((3, D), f32), _sds((1, D), f32), _sds((1, D), f32),
                 _sds((NH, HEAD, HEAD), f32), _sds((NH, HEAD, HEAD), f32)]
    return _pcall(body, name="even_bwd", out_shape=out_shape, grid=(NH,),
                  in_specs=[_head_spec(7, T), _head_spec(2, T), _head_vec(3), _head_vec(1), _head_vec(1),
                            _HEAD_MAT, _HEAD_MAT],
                  out_specs=[_head_spec(7, T), _head_vec(3), _head_vec(1), _head_vec(1), _HEAD_MAT, _HEAD_MAT],
                  scratch=[pltpu.VMEM((T, HEAD), f32)])(p7, dy2, conv_w, ln_g, ln_b, sgu_w, sgu_bias)


def _window_sum(ext, win, towards_past):
    n, k, s = ext.shape[0], 1, ext
    while k < win:
        s = s + pltpu.roll(s, k if towards_past else n - k, 0)
        k *= 2
    return s


def _pool_count(i, C, win):
    t = i * C + lax.broadcasted_iota(jnp.int32, (C, 1), 0)
    cnt = jnp.minimum(t + 1, win).astype(f32)
    return cnt, 1.0 / cnt


def _group_specs(T):
    p_spec = pl.BlockSpec((None, T, GC), lambda g: (0, 0, g))
    z_spec = pl.BlockSpec((None, T, GC), lambda g: (1, 0, g))
    pw_spec = pl.BlockSpec((4, GC // 4, GC), lambda g: (0, g, 0))
    ps_spec = pl.BlockSpec((1, GC), lambda g: (0, g))
    y_spec = pl.BlockSpec((None, T, GC), lambda g: (g // 2, 0, g % 2))
    return p_spec, z_spec, pw_spec, ps_spec, y_spec


def _odd_fwd(p2, pool_wg, pool_scale):
    T, C = p2.shape[1], CHUNK_ROWS
    p_spec, z_spec, pw_spec, ps_spec, y_spec = _group_specs(T)

    def body(p_ref, z_ref, pw_ref, ps_ref, y_ref):
        pw, ps = pw_ref[...].reshape(GC, GC), ps_ref[...]

        def run(win):
            def step(i, halo):
                rows = pl.ds(pl.multiple_of(i * C, C), C)
                p = p_ref[rows, :].astype(f32)
                s = _window_sum(jnp.concatenate([halo, p], axis=0), win, True)[HALO_POOL:]
                pooled = s * _pool_count(i, C, win)[1] - p
                ypre = jnp.dot(pooled.astype(bf16), pw, preferred_element_type=f32)
                y_ref[rows, :] = (ypre * ps * _silu(z_ref[rows, :].astype(f32))).astype(bf16)
                return p[C - HALO_POOL:]

            lax.fori_loop(0, T // C, step, jnp.zeros((HALO_POOL, GC), f32))

        for gi, win in enumerate(WINDOWS):
            pl.when(pl.program_id(0) == gi)(functools.partial(run, win))

    return _pcall(body, name="odd_fwd", out_shape=_sds((2, T, D), bf16), grid=(len(WINDOWS),),
                  in_specs=[p_spec, z_spec, pw_spec, ps_spec], out_specs=y_spec)(p2, p2, pool_wg, pool_scale)


def _odd_bwd(p2, dy2, pool_wg, pool_scale):
    T, C = p2.shape[1], CHUNK_ROWS
    n_chunks = T // C
    p_spec, z_spec, pw_spec, ps_spec, y_spec = _group_specs(T)

    def body(p_ref, z_ref, dy_ref, pw_ref, ps_ref, dp_ref, dpw_ref, dps_ref, q_s, acc_s):
        pw, ps = pw_ref[...].reshape(GC, GC), ps_ref[...]

        def run(win):
            acc_s[...] = jnp.zeros_like(acc_s)

            def fwd_step(i, carry):
                halo, aps = carry
                rows = pl.ds(pl.multiple_of(i * C, C), C)
                p, z, dy = p_ref[rows, :].astype(f32), z_ref[rows, :].astype(f32), dy_ref[rows, :].astype(f32)
                _, inv_cnt = _pool_count(i, C, win)
                s = _window_sum(jnp.concatenate([halo, p], axis=0), win, True)[HALO_POOL:]
                pb = (s * inv_cnt - p).astype(bf16)
                ypre = jnp.dot(pb, pw, preferred_element_type=f32)
                sz, dsz = _silu_and_grad(z)
                aps = aps + jnp.sum(dy * ypre * sz, axis=0, keepdims=True)
                dp_ref[1, rows, :] = (dy * ypre * ps * dsz).astype(bf16)
                dyp = (dy * ps * sz).astype(bf16)
                acc_s[...] += lax.dot_general(pb, dyp, (TN, ((), ())), preferred_element_type=f32)
                dpool = lax.dot_general(dyp, pw, (NT, ((), ())), preferred_element_type=f32)
                q_s[rows, :] = dpool * inv_cnt
                return p[C - HALO_POOL:], aps

            _, aps = lax.fori_loop(0, n_chunks, fwd_step, (jnp.zeros((HALO_POOL, GC), f32), jnp.zeros((1, GC), f32)))
            dps_ref[...] = aps
            dpw_ref[...] = acc_s[...].reshape(4, GC // 4, GC).astype(bf16)

            def bwd_step(k, halo):
                i = n_chunks - 1 - k
                rows = pl.ds(pl.multiple_of(i * C, C), C)
                q = q_s[rows, :]
                s = _window_sum(jnp.concatenate([q, halo], axis=0), win, False)[:C]
                dp_ref[0, rows, :] = (s - q * _pool_count(i, C, win)[0]).astype(bf16)
                return q[:HALO_POOL]

            lax.fori_loop(0, n_chunks, bwd_step, jnp.zeros((HALO_POOL, GC), f32))

        for gi, win in enumerate(WINDOWS):
            pl.when(pl.program_id(0) == gi)(functools.partial(run, win))

    out_shape = [_sds((2, T, 2 * D), bf16), _sds((4, GC, GC), bf16), _sds((1, 2 * D), f32)]
    return _pcall(body, name="odd_bwd", out_shape=out_shape, grid=(len(WINDOWS),),
                  in_specs=[p_spec, z_spec, y_spec, pw_spec, ps_spec],
                  out_specs=[pl.BlockSpec((2, T, GC), lambda g: (0, 0, g)), pw_spec, ps_spec],
                  scratch=[pltpu.VMEM((T, GC), f32), pltpu.VMEM((GC, GC), f32)], vmem_mb=44)(
                      p2, p2, dy2, pool_wg, pool_scale)


def _ada_fwd(c_all, ada_w):
    cols = ada_w.shape[2]

    def body(c_ref, w_ref, o_ref):
        o_ref[...] = jnp.dot(_silu(c_ref[...]), w_ref[...], preferred_element_type=f32,
                             precision=lax.Precision.HIGHEST)

    return _pcall(body, name="ada_fwd", out_shape=_sds((4, N_DEV, cols), f32), grid=(4,),
                  in_specs=[pl.BlockSpec((N_DEV, D), lambda i: (0, 0)), pl.BlockSpec((None, D, cols), lambda i: (i, 0, 0))],
                  out_specs=pl.BlockSpec((None, N_DEV, cols), lambda i: (i, 0, 0)))(c_all, ada_w)


def _ada_bwd(c_all_t, dmod, w, m, v):
    cols, tr = w.shape[2], 256
    spec = pl.BlockSpec((None, tr, cols), lambda l, i: (l, i, 0))

    def body(c_ref, dm_ref, w_ref, m_ref, v_ref, g_ref, d_ref, mo_ref, vo_ref):
        sc = _silu(c_ref[...])
        g = sc[:, 0:1] * dm_ref[0:1, :]
        for b in range(1, N_DEV):
            g = g + sc[:, b:b + 1] * dm_ref[b:b + 1, :]
        g_ref[...] = g
        d_ref[...], mo_ref[...], vo_ref[...] = _adamw_math(w_ref[...], g, m_ref[...], v_ref[...])

    return _pcall(body, name="ada_bwd", out_shape=[_sds(w.shape, f32)] * 4, grid=(4, D // tr),
                  in_specs=[pl.BlockSpec((tr, N_DEV), lambda l, i: (i, 0)),
                            pl.BlockSpec((None, N_DEV, cols), lambda l, i: (l, 0, 0)), spec, spec, spec],
                  out_specs=[spec] * 4)(c_all_t, dmod, w, m, v)


def _layer_fwd(even, x, hb, gate, w, nxt, before_out=None):
    if even:
        w_in, w_out, conv_w, ln_g, ln_b, sgu_w, sgu_b = w
        bias = jnp.broadcast_to(sgu_b[:, :, None], (NH, HEAD, HEAD))
        p = EVEN_PROJ.fwd(hb, w_in)
        y2 = _even_fwd(p, conv_w, ln_g, ln_b, sgu_w, bias)
    else:
        w_in, pool_w, w_out, pool_scale = w
        p = ODD_PROJ.fwd(hb, w_in)
        y2 = _odd_fwd(p, pool_w, pool_scale)
    if before_out is not None:
        late_w_out, tok = before_out(y2)
        if late_w_out is not None:
            w_out = late_w_out
            w = (w_in, w_out) + tuple(w[2:]) if even else (w_in, pool_w, w_out, pool_scale)
        if tok is not None:
            gate = gate + tok[0:1, 0:1]
    outs = _out_proj(y2, w_out.reshape(2, D, D), x, gate, nxt)
    return outs[0], (None if nxt is None else outs[2]), (x, hb, p, y2, outs[1]), w


def _layer_bwd(even, gin, dob, dgate, saved, scale, g, w, below=None, send=None):
    x_in, hb, p, y2, o = saved
    if even:
        w_in, w_out, conv_w, ln_g, ln_b, sgu_w, sgu_b = w
        bias = jnp.broadcast_to(sgu_b[:, :, None], (NH, HEAD, HEAD))
        dy2, dwo = _out_bwd(dob, w_out, y2)
        dp, dconv, dlg, dlb, dsw, dms = _even_bwd(p, dy2, conv_w, ln_g, ln_b, sgu_w, bias)
        proj = EVEN_PROJ
        small = dict(conv_w=dconv, ln_g=dlg, ln_b=dlb, sgu_w=dsw, sgu_b=jnp.sum(dms, axis=-1))
        big = [proj.dw(hb, dp), dwo]
    else:
        w_in, pool_w, w_out, pool_scale = w
        dy2, dwo = _out_bwd(dob, w_out, y2)
        dp, dpw, dps = _odd_bwd(p, dy2, pool_w, pool_scale)
        proj = ODD_PROJ
        small = dict(pool_scale=dps)
        big = [proj.dw(hb, dp), dpw, dwo]
    if send is not None:
        big, tok = send(big)
        scale = scale + tok[0:1, 0:1]
    dh = proj.dh(dp, w_in)
    res = _norm_bwd(x_in, dh, gin, g, scale, below)
    stats = res[1]
    return (res[0], (None if below is None else (res[2], res[3])), big, small,
            jnp.concatenate([stats[0:2], dgate], axis=0), stats[2:3])


def _pack_rows(parts):
    rows = [p.reshape(-1, LANES) for p in parts]
    total = sum(r.shape[0] for r in rows)
    padded = -(-total // (8 * N_DEV)) * (8 * N_DEV)
    if padded > total:
        rows.append(jnp.zeros((padded - total, LANES), f32))
    return jnp.concatenate(rows, axis=0)


def _unpack_rows(buf, shapes):
    out, r = [], 0
    for shp in shapes:
        n = 1
        for d in shp:
            n *= d
        out.append(buf[r:r + n // LANES].reshape(shp))
        r += n // LANES
    return out


def kernel(x, c, norm_g, ada_w, ada_b, ab_w_in, ab_conv_w, ab_ln_g, ab_ln_b, ab_sgu_w, ab_sgu_b, ab_w_out, c_w_in, c_pool_w, c_pool_scale, c_w_out, final_g, loss_target, m_norm_g, m_ada_w, m_ada_b, m_ab_w_in, m_ab_conv_w, m_ab_ln_g, m_ab_ln_b, m_ab_sgu_w, m_ab_sgu_b, m_ab_w_out, m_c_w_in, m_c_pool_w, m_c_pool_scale, m_c_w_out, m_final_g, v_norm_g, v_ada_w, v_ada_b, v_ab_w_in, v_ab_conv_w, v_ab_ln_g, v_ab_ln_b, v_ab_sgu_w, v_ab_sgu_b, v_ab_w_out, v_c_w_in, v_c_pool_w, v_c_pool_scale, v_c_w_out, v_final_g):
    ix, iy, ic = _place()
    chip, dev = 2 * ix + iy, 4 * ix + 2 * iy + ic
    n_even, n_odd = ab_w_in.shape[0], c_w_in.shape[0]
    depth = n_even + n_odd
    acols = ada_w.shape[2]

    place = jnp.stack([chip, ic]).astype(jnp.int32)
    even_names, odd_names = ["ab_w_in", "ab_w_out"], ["c_w_in", "c_pool_w", "c_w_out"]
    params = {"ab_w_in": (ab_w_in, m_ab_w_in, v_ab_w_in), "ab_w_out": (ab_w_out, m_ab_w_out, v_ab_w_out),
              "c_w_in": (c_w_in, m_c_w_in, v_c_w_in), "c_w_out": (c_w_out, m_c_w_out, v_c_w_out),
              "c_pool_w": tuple(a.reshape(n_odd, GC, GC) for a in (c_pool_w, m_c_pool_w, v_c_pool_w))}

    def placed(names, layer, after=None):
        ws = [params[nm][0] for nm in names]
        return [p.reshape(4, 2, p.shape[1] // 2, p.shape[2]) for p in _cast_place(place, ws, layer, after)]

    def whole(arrays):
        return [g.reshape(4, 2 * g.shape[2], g.shape[3]) for g in arrays]

    first = _gather8(jnp.concatenate([c, ab_conv_w.reshape(1, -1), c_pool_scale.reshape(1, -1)], axis=1), "gather_c")
    c_all, small_all = first[:, 0, :D], first[0::2, 0, D:]
    sems_a, in_a, tok = _ag_start([placed(even_names[:1], 0)], first[0:1, 0, 0:LANES], "ag_start_0a")
    modp = _ada_fwd(c_all, ada_w)
    later = [placed(even_names[1:], 0, tok)]
    later += [placed(even_names if i % 2 == 0 else odd_names, i // 2, tok) for i in range(1, depth)]
    modg = _gather8(modp + tok[0:1, 0:1], "gather_mod", [lay[-1] for lay in later])
    mod_rows = lax.dynamic_index_in_dim(modg[0::2], dev, axis=2, keepdims=False)
    mod = jnp.transpose(mod_rows, (1, 0, 2)).reshape(depth, 3 * D) + ada_b
    mods = [(mod[i:i + 1, 0:D], mod[i:i + 1, D:2 * D], mod[i:i + 1, 2 * D:3 * D]) for i in range(depth)]

    def shard_cols(a, width):
        return lax.dynamic_slice_in_dim(a, chip * width, width, axis=a.ndim - 1)

    n_conv = ab_conv_w.size
    conv_all = small_all[:, :n_conv].reshape(4, n_even, 3, D // 4)
    conv_full = jnp.transpose(conv_all, (1, 2, 0, 3)).reshape(n_even, 3, D)
    scale_all = small_all[:, n_conv:].reshape(4, n_odd, 2 * D // 4)
    scale_full = jnp.transpose(scale_all, (1, 0, 2)).reshape(n_odd, 2 * D)

    gathers_done = mod[0:1, 0:LANES] + scale_full[0:1, 0:LANES]
    sems_b, in_b, tok = _ag_start(later[:1], gathers_done, "ag_start_0b")
    sems_r, in_r, tok = _ag_start(later[1:], tok, "ag_start_rest")

    x_cur, saved, weights, handoff = x[0], [], [], {}
    sems_f, in_f, tok = _agf_start(_ag_wait(in_a[0], sems_a[0], tok, "ag_wait_0a"), "agf_start_0")
    hb = _hnorm(x_cur, norm_g[0:1], mods[0][0] + tok[0:1, 0:1], mods[0][1])
    for i in range(depth):
        j = i // 2
        if i == 0:
            full = whole(_agf_wait(sems_f, in_f, hb, "agf_wait_0")) + [None]
        else:
            full = whole(_agf_wait(*handoff.pop(i), x_cur, f"agf_wait_{i}"))
        if i % 2 == 0:
            w = (full[0], full[1], conv_full[j], ab_ln_g[j:j + 1], ab_ln_b[j:j + 1], ab_sgu_w[j], ab_sgu_b[j])
        else:
            w = (full[0], full[1], full[2], scale_full[j:j + 1])

        def before_out(y2, i=i):
            w_out, tok = None, None
            if i == 0:
                w_out = whole(_ag_forward(_ag_wait(in_b[0], sems_b[0], y2, "ag_wait_0b"), "ag_forward"))[0]
            if i + 1 < depth:
                arrived = _ag_wait(in_r[i], sems_r[i], y2, f"ag_wait_{i + 1}")
                sems_f, inflight, tok = _agf_start(arrived, f"agf_start_{i + 1}")
                handoff[i + 1] = (sems_f, inflight)
            return w_out, tok

        nxt = (norm_g[i + 1:i + 2], mods[i + 1][0], mods[i + 1][1]) if i + 1 < depth else None
        x_cur, hb, sv, w = _layer_fwd(i % 2 == 0, x_cur, hb, mods[i][2], w, nxt, before_out)
        weights.append(w)
        saved.append(sv)
    gin, loss, dfinal_g, dob, dgate = _loss_bwd(x_cur, loss_target[0], final_g.reshape(1, D), saved[-1][4],
                                                mods[-1][2])

    stacked = {}

    def reduce_layer(i, sems, pairs, lands, after):
        pairs, slots = _rs_chip_wait(sems, pairs, lands, after, f"rs_chip_wait_{i}")
        half_sems, halves, _ = _rs_half_start(_rs_sum(place, pairs, slots), f"rs_half_start_{i}")
        return i, half_sems, halves

    def update_layer(i, half_sems, halves, after):
        names = even_names if i % 2 == 0 else odd_names
        grads = _rs_half_wait(half_sems, halves, after, f"rs_half_wait_{i}")
        items = [(params[nm][0], g.reshape(params[nm][0].shape[1:]), params[nm][1], params[nm][2], stacked.get(nm))
                 for nm, g in zip(names, grads)]
        for nm, res in zip(names, _adamw_layer(i // 2, items)):
            stacked[nm] = res
            updated.append(res[1])

    updated = []
    small_g, dmod, dnorm_g, pending, tok = [None] * depth, [None] * depth, [None] * depth, None, None
    exchanging = []
    for i in reversed(range(depth)):
        w = weights[i]
        if tok is not None:
            w = w[:2] + (w[2] + tok[0:1, 0:1],) + w[3:] if i % 2 == 0 else w[:3] + (w[3] + tok[0:1, 0:1],)
        below = (saved[i - 1][4], mods[i - 1][2]) if i > 0 else None

        def send(big_g, i=i):
            if exchanging:
                update_layer(*exchanging.pop(), big_g[0])
            big_g = [g.reshape(4, 2, g.shape[1] // 2, g.shape[2]) for g in big_g]
            sems, big_g, lands, tok = _rs_pair_start(big_g, f"rs_pair_start_{i}")
            return (sems, big_g, lands), tok

        gin, gate_bwd, sent, small_g[i], dmod[i], dnorm_g[i] = _layer_bwd(
            i % 2 == 0, gin, dob, dgate, saved[i], mods[i][1], norm_g[i:i + 1], w, below, send)
        if below is not None:
            dob, dgate = gate_bwd
        after = gin
        if i == 0:
            dmod_all = _gather8(jnp.stack(dmod).reshape(depth * 3 * D // LANES, LANES), "gather_dmod")
            after = dmod_all = dmod_all.reshape(N_DEV, depth, 3 * D)
        if i > 0:
            after, updated = [after] + updated, []
        else:
            after = [after]
        big_g, theirs = _rs_pair_wait(*sent, after, f"rs_pair_wait_{i}")
        pairs = _rs_add(place, big_g, theirs)
        sems, pairs, lands, tok = _rs_chip_start(pairs, f"rs_chip_start_{i}")
        if pending is not None:
            exchanging.append(reduce_layer(*pending, [tok]))
        pending = (i, sems, pairs, lands)
    grad_x = gin
    dnorm_g = jnp.concatenate(dnorm_g, axis=0)

    dmod_cols = jnp.transpose(shard_cols(dmod_all, acols), (1, 0, 2))
    r_ada_w = _ada_bwd(c_all.T, dmod_cols, ada_w, m_ada_w, v_ada_w)
    update_layer(*exchanging.pop(), r_ada_w[1])
    last = reduce_layer(*pending, [r_ada_w[1]] + updated)

    small_parts = [dnorm_g, dfinal_g,
                   jnp.stack([small_g[2 * j]["conv_w"] for j in range(n_even)]),
                   jnp.concatenate([small_g[2 * j]["ln_g"] for j in range(n_even)], axis=0),
                   jnp.concatenate([small_g[2 * j]["ln_b"] for j in range(n_even)], axis=0),
                   jnp.stack([small_g[2 * j]["sgu_b"] for j in range(n_even)]),
                   jnp.concatenate([small_g[2 * j + 1]["pool_scale"] for j in range(n_odd)], axis=0),
                   jnp.pad(loss, ((0, 7), (0, LANES - 1)))]
    small_shapes = [p.shape for p in small_parts]
    sgu_parts = [small_g[2 * j]["sgu_w"].reshape(NH * HEAD, HEAD) for j in range(n_even)]
    reduced = _allreduce8([_pack_rows(small_parts)] + sgu_parts, "allreduce_small", last[2][0])
    update_layer(*last, reduced[0])
    r_ab_w_in, r_ab_w_out, r_c_w_in, r_c_w_out = (stacked[nm] for nm in ("ab_w_in", "ab_w_out", "c_w_in", "c_w_out"))
    r_c_pool_w = tuple(a.reshape(c_pool_w.shape) for a in stacked["c_pool_w"])
    g_norm_g, g_final_g, g_conv_full, g_ln_g, g_ln_b, g_sgu_b, g_scale_full, loss_row = _unpack_rows(reduced[0],
                                                                                                     small_shapes)
    g_sgu_w = jnp.stack(reduced[1:])
    loss = loss_row[0, 0]
    g_conv = shard_cols(g_conv_full, D // 4)
    g_scale = shard_cols(g_scale_full, 2 * D // 4)

    def two_d(a):
        return a.reshape(-1, a.shape[-1])

    small = [(norm_g, g_norm_g, m_norm_g, v_norm_g),
             (ada_b, dmod_all, m_ada_b, v_ada_b),
             (two_d(ab_conv_w), two_d(g_conv), two_d(m_ab_conv_w), two_d(v_ab_conv_w)),
             (ab_ln_g, g_ln_g, m_ab_ln_g, v_ab_ln_g),
             (ab_ln_b, g_ln_b, m_ab_ln_b, v_ab_ln_b),
             (two_d(ab_sgu_w), two_d(g_sgu_w), two_d(m_ab_sgu_w), two_d(v_ab_sgu_w)),
             (two_d(ab_sgu_b), two_d(g_sgu_b), two_d(m_ab_sgu_b), two_d(v_ab_sgu_b)),
             (c_pool_scale, g_scale, m_c_pool_scale, v_c_pool_scale),
             (final_g.reshape(1, D), g_final_g, m_final_g.reshape(1, D), v_final_g.reshape(1, D))]
    small_res = _adamw_small(small)
    small_shapes_out = [norm_g.shape, ada_b.shape, ab_conv_w.shape, ab_ln_g.shape, ab_ln_b.shape, ab_sgu_w.shape,
                        ab_sgu_b.shape, c_pool_scale.shape, final_g.shape]
    (r_norm_g, r_ada_b, r_conv, r_ln_g, r_ln_b, r_sgu_w, r_sgu_b, r_scale, r_final_g) = [
        tuple(a.reshape(shp) for a in res) for res, shp in zip(small_res, small_shapes_out)]

    order = [r_norm_g, r_ada_w, r_ada_b, r_ab_w_in, r_conv, r_ln_g, r_ln_b, r_sgu_w, r_sgu_b, r_ab_w_out,
             r_c_w_in, r_c_pool_w, r_scale, r_c_w_out, r_final_g]
    outs = [loss, grad_x[None]]
    for field in range(4):
        outs += [r[field] for r in order]
    return tuple(outs)
```

```python
import functools

import jax
import jax.numpy as jnp
from jax import lax
from jax.experimental import pallas as pl
from jax.experimental.pallas import tpu as pltpu

f32, bf16 = jnp.float32, jnp.bfloat16

D = 1024
HEAD = 128
NH = 8
WINDOWS = (2, 4, 8, 16)
GC = 512
EPS = 1e-6
HALO_CONV = 8
HALO_POOL = 16
CHUNK_ROWS = 512
DH_WIDE = 1024
FWD_TILES = 2
N_DEV = 8
LANES = 128

ADAM_LR, ADAM_B1, ADAM_B2, ADAM_EPS, ADAM_WD, ADAM_STEP = 0.001, 0.9, 0.999, 1e-08, 0.01, 10

MESH = pl.DeviceIdType.MESH
ANY = pl.BlockSpec(memory_space=pl.ANY)
VMEM = pl.BlockSpec(memory_space=pltpu.VMEM)
MIB = 2 ** 20


def _pcall(body, *, name, out_shape, grid=None, in_specs=None, out_specs=None, scratch=(), vmem_mb=None,
           aliases=None, prefetch=0):
    kw = {}
    if prefetch:
        kw["grid_spec"] = pltpu.PrefetchScalarGridSpec(num_scalar_prefetch=prefetch, grid=grid, in_specs=in_specs,
                                                       out_specs=out_specs, scratch_shapes=list(scratch))
    else:
        if grid is not None:
            kw["grid"] = grid
        if in_specs is not None:
            kw["in_specs"] = in_specs
        if out_specs is not None:
            kw["out_specs"] = out_specs
        if scratch:
            kw["scratch_shapes"] = list(scratch)
    if aliases:
        kw["input_output_aliases"] = aliases
    params = pltpu.CompilerParams(vmem_limit_bytes=None if vmem_mb is None else vmem_mb * MIB)
    return pl.pallas_call(body, name=name, out_shape=out_shape, compiler_params=params, **kw)


def _sds(shape, dtype):
    return jax.ShapeDtypeStruct(tuple(shape), dtype)


def _sigmoid(z):
    return pl.reciprocal(1.0 + jnp.exp(-z), approx=True)


def _silu(z):
    return z * _sigmoid(z)


def _silu_and_grad(z):
    s = _sigmoid(z)
    return z * s, s * (1.0 + z * (1.0 - s))


def _place():
    return lax.axis_index("x"), lax.axis_index("y"), lax.axis_index("c")


def _gather8(blk, name, after=()):
    def body(x_ref, *rest):
        o_ref, ssem, rsem = rest[len(after):]
        x, y, c = _place()
        me = 4 * x + 2 * y + c
        o_ref[me] = x_ref[...]
        sends = []
        for k in range(1, N_DEV):
            px = 1 - x if k & 4 else x
            py = 1 - y if k & 2 else y
            pc = 1 - c if k & 1 else c
            cp = pltpu.make_async_remote_copy(src_ref=x_ref, dst_ref=o_ref.at[me], send_sem=ssem.at[k - 1],
                                              recv_sem=rsem.at[k - 1], device_id=(px, py, pc), device_id_type=MESH)
            cp.start()
            sends.append((cp, 4 * px + 2 * py + pc))
        for k, (cp, peer) in enumerate(sends):
            pltpu.make_async_remote_copy(src_ref=x_ref, dst_ref=o_ref.at[peer], send_sem=ssem.at[k],
                                         recv_sem=rsem.at[k], device_id=(x, y, c), device_id_type=MESH).wait_recv()
        for cp, _ in sends:
            cp.wait_send()

    return _pcall(body, name=name, out_shape=_sds((N_DEV,) + blk.shape, blk.dtype), in_specs=[VMEM] + [ANY] * len(after),
                  out_specs=VMEM,
                  scratch=[pltpu.SemaphoreType.DMA((N_DEV - 1,)), pltpu.SemaphoreType.DMA((N_DEV - 1,))])(blk, *after)


def _allreduce8(bufs, name, after=None):
    n, n_after = len(bufs), 0 if after is None else 1
    rbs = [b.shape[0] // N_DEV for b in bufs]
    assert all(rb * N_DEV == b.shape[0] and rb % 8 == 0 for rb, b in zip(rbs, bufs))

    def body(*refs):
        refs = refs[:n] + refs[n + n_after:]
        xs, outs, stages = refs[:n], refs[n:2 * n], refs[2 * n:3 * n]
        ssem, rsem = refs[3 * n:]
        x, y, c = _place()
        me = 4 * x + 2 * y + c
        peers = []
        for k in range(1, N_DEV):
            px = 1 - x if k & 4 else x
            py = 1 - y if k & 2 else y
            pc = 1 - c if k & 1 else c
            peers.append(((px, py, pc), 4 * px + 2 * py + pc))

        def blk(t, ref, idx):
            return ref.at[pl.ds(pl.multiple_of(idx * rbs[t], 8), rbs[t]), :]

        def copy(t, phase, k, src, dst, dev):
            return pltpu.make_async_remote_copy(src_ref=src, dst_ref=dst, send_sem=ssem.at[t, phase, k],
                                                recv_sem=rsem.at[t, phase, k], device_id=dev, device_id_type=MESH)

        scatter = [copy(t, 0, k, blk(t, xs[t], pidx), stages[t].at[me], dev)
                   for t in range(n) for k, (dev, pidx) in enumerate(peers)]
        for cp in scatter:
            cp.start()
        gather = []
        for t in range(n):
            stages[t][me] = blk(t, xs[t], me)[...]
            for k, (dev, pidx) in enumerate(peers):
                copy(t, 0, k, blk(t, xs[t], pidx), stages[t].at[pidx], dev).wait_recv()
            total = stages[t][0]
            for j in range(1, N_DEV):
                total = total + stages[t][j]
            blk(t, outs[t], me)[...] = total
            sends = [copy(t, 1, k, blk(t, outs[t], me), blk(t, outs[t], me), dev) for k, (dev, pidx) in enumerate(peers)]
            for cp in sends:
                cp.start()
            gather += sends
        for t in range(n):
            for k, (dev, pidx) in enumerate(peers):
                copy(t, 1, k, blk(t, outs[t], pidx), blk(t, outs[t], pidx), dev).wait_recv()
        for cp in scatter + gather:
            cp.wait_send()

    return _pcall(body, name=name, out_shape=[_sds(b.shape, f32) for b in bufs], in_specs=[VMEM] * n + [ANY] * n_after,
                  out_specs=[VMEM] * n,
                  scratch=[pltpu.VMEM((N_DEV, rb, LANES), f32) for rb in rbs]
                  + [pltpu.SemaphoreType.DMA((n, 2, N_DEV - 1)), pltpu.SemaphoreType.DMA((n, 2, N_DEV - 1))])(
                      *bufs, *([] if after is None else [after]))


def _other_chips(x, y):
    return [((1 - x, y), 2 * (1 - x) + y), ((x, 1 - y), 2 * x + (1 - y)), ((1 - x, 1 - y), 2 * (1 - x) + (1 - y))]


HBM = pl.BlockSpec(memory_space=pltpu.HBM)
SEM = pl.BlockSpec(memory_space=pltpu.SEMAPHORE)
EFFECT = pltpu.SideEffectType.DATAFLOW_SIDE_EFFECTING


def _in_hbm(a):
    return pltpu.with_memory_space_constraint(a, pltpu.HBM)


SIBLING_ID = 1


def _sibling_handshake():
    x, y, c = _place()
    barrier = pltpu.get_barrier_semaphore()
    pl.semaphore_signal(barrier, inc=1, device_id=(x, y, 1 - c), device_id_type=MESH)
    pl.semaphore_wait(barrier, 1)
    return x, y, c


def _ag_start(layers, after, name):
    flat = [t for lay in layers for t in lay]
    n, nl = len(flat), len(layers)

    def body(*refs):
        src = refs[:n]
        sems = refs[n + 1:n + 1 + 2 * nl]
        token = refs[-1]
        x, y, c = _place()
        s_me = 2 * x + y
        t = 0
        for i, lay in enumerate(layers):
            for k in range(len(lay)):
                for j, ((px, py), _) in enumerate(_other_chips(x, y)):
                    pltpu.make_async_remote_copy(src_ref=src[t].at[s_me, c], dst_ref=src[t].at[s_me, c],
                                                 send_sem=sems[2 * i].at[3 * k + j], recv_sem=sems[2 * i + 1].at[3 * k + j],
                                                 device_id=(px, py, c), device_id_type=MESH).start()
                t += 1
        token[...] = jnp.zeros_like(token)

    sem_shapes = [pltpu.SemaphoreType.DMA((3 * len(lay),)) for lay in layers for _ in range(2)]
    out_shape = sem_shapes + [pltpu.HBM(t.shape, t.dtype) for t in flat] + [_sds((8, LANES), f32)]
    outs = pl.pallas_call(
        body, name=name, out_shape=out_shape, in_specs=[HBM] * n + [ANY],
        out_specs=[SEM] * (2 * nl) + [HBM] * n + [VMEM], input_output_aliases={t: 2 * nl + t for t in range(n)},
        compiler_params=pltpu.CompilerParams(has_side_effects=EFFECT))(*[_in_hbm(t) for t in flat], after)
    sems = [(outs[2 * i], outs[2 * i + 1]) for i in range(nl)]
    thru, t = [], 2 * nl
    for lay in layers:
        thru.append(list(outs[t:t + len(lay)]))
        t += len(lay)
    return sems, thru, outs[-1]


def _ag_wait(inflight, sems, after, name):
    n = len(inflight)

    def body(*refs):
        src, ssem, rsem = refs[:n], refs[n], refs[n + 1]
        x, y, c = _place()
        s_me = 2 * x + y
        for k in range(n):
            for j, (_, s_p) in enumerate(_other_chips(x, y)):
                cp = pltpu.make_async_remote_copy(src_ref=src[k].at[s_me, c], dst_ref=src[k].at[s_p, c],
                                                  send_sem=ssem.at[3 * k + j], recv_sem=rsem.at[3 * k + j],
                                                  device_id=(x, y, c), device_id_type=MESH)
                cp.wait_send()
                cp.wait_recv()

    return pl.pallas_call(
        body, name=name, out_shape=[pltpu.HBM(t.shape, t.dtype) for t in inflight],
        in_specs=[HBM] * n + [SEM, SEM, ANY], out_specs=[HBM] * n, input_output_aliases={t: t for t in range(n)},
        compiler_params=pltpu.CompilerParams(has_side_effects=EFFECT))(*inflight, sems[0], sems[1], after)


def _ag_forward(arrived, name):
    n = len(arrived)

    def body(*refs):
        o = refs[n:2 * n]
        ssem, rsem = refs[2 * n:]
        x, y, c = _place()

        def copy(t, j, s, half, dev):
            return pltpu.make_async_remote_copy(src_ref=o[t].at[s, c], dst_ref=o[t].at[s, half], send_sem=ssem.at[t, j],
                                                recv_sem=rsem.at[t, j], device_id=dev, device_id_type=MESH)

        chips = _other_chips(x, y)
        sends = [copy(t, j, s_p, c, (x, y, 1 - c)) for t in range(n) for j, (_, s_p) in enumerate(chips)]
        for cp in sends:
            cp.start()
        for t in range(n):
            for j, (_, s_p) in enumerate(chips):
                copy(t, j, s_p, 1 - c, (x, y, c)).wait_recv()
        for cp in sends:
            cp.wait_send()

    return _pcall(body, name=name, out_shape=[_sds(p.shape, bf16) for p in arrived], in_specs=[ANY] * n,
                  out_specs=[ANY] * n, aliases={t: t for t in range(n)},
                  scratch=[pltpu.SemaphoreType.DMA((n, 3)), pltpu.SemaphoreType.DMA((n, 3))])(*arrived)


def _agf_start(arrived, name):
    n = len(arrived)

    def body(*refs):
        o = refs[:n]
        ssem, rsem, token = refs[n], refs[n + 1], refs[-1]
        x, y, c = _sibling_handshake()
        for t in range(n):
            for j, (_, s_p) in enumerate(_other_chips(x, y)):
                pltpu.make_async_remote_copy(src_ref=o[t].at[s_p, c], dst_ref=o[t].at[s_p, c],
                                             send_sem=ssem.at[3 * t + j], recv_sem=rsem.at[3 * t + j],
                                             device_id=(x, y, 1 - c), device_id_type=MESH).start()
        token[...] = jnp.zeros_like(token)

    out_shape = ([pltpu.SemaphoreType.DMA((3 * n,))] * 2 + [pltpu.HBM(a.shape, bf16) for a in arrived]
                 + [_sds((8, LANES), f32)])
    outs = pl.pallas_call(
        body, name=name, out_shape=out_shape, in_specs=[HBM] * n, out_specs=[SEM, SEM] + [HBM] * n + [VMEM],
        input_output_aliases={t: 2 + t for t in range(n)},
        compiler_params=pltpu.CompilerParams(has_side_effects=EFFECT, collective_id=SIBLING_ID))(
            *[_in_hbm(a) for a in arrived])
    return (outs[0], outs[1]), list(outs[2:2 + n]), outs[-1]


def _agf_wait(sems, inflight, after, name):
    n = len(inflight)

    def body(*refs):
        o, ssem, rsem = refs[:n], refs[n], refs[n + 1]
        x, y, c = _place()
        for t in range(n):
            for j, (_, s_p) in enumerate(_other_chips(x, y)):
                cp = pltpu.make_async_remote_copy(src_ref=o[t].at[s_p, c], dst_ref=o[t].at[s_p, 1 - c],
                                                  send_sem=ssem.at[3 * t + j], recv_sem=rsem.at[3 * t + j],
                                                  device_id=(x, y, c), device_id_type=MESH)
                cp.wait_send()
                cp.wait_recv()

    return pl.pallas_call(
        body, name=name, out_shape=[pltpu.HBM(a.shape, bf16) for a in inflight],
        in_specs=[HBM] * n + [SEM, SEM, ANY], out_specs=[HBM] * n, input_output_aliases={t: t for t in range(n)},
        compiler_params=pltpu.CompilerParams(has_side_effects=EFFECT))(*inflight, sems[0], sems[1], after)


def _rs_pair_start(grads, name):
    n = len(grads)

    def body(*refs):
        g, theirs = refs[:n], refs[n:2 * n]
        ssem, rsem, token = refs[2 * n], refs[2 * n + 1], refs[-1]
        x, y, c = _sibling_handshake()
        for t in range(n):
            pltpu.make_async_remote_copy(src_ref=g[t].at[:, 1 - c], dst_ref=theirs[t], send_sem=ssem.at[t],
                                         recv_sem=rsem.at[t], device_id=(x, y, 1 - c), device_id_type=MESH).start()
        token[...] = jnp.zeros_like(token)

    lands = [lax.empty((4,) + g.shape[2:], bf16) for g in grads]
    out_shape = ([pltpu.SemaphoreType.DMA((n,))] * 2 + [pltpu.HBM(g.shape, bf16) for g in grads]
                 + [pltpu.HBM(q.shape, bf16) for q in lands] + [_sds((8, LANES), f32)])
    outs = pl.pallas_call(
        body, name=name, out_shape=out_shape, in_specs=[HBM] * (2 * n), out_specs=[SEM, SEM] + [HBM] * (2 * n) + [VMEM],
        input_output_aliases={t: 2 + t for t in range(2 * n)},
        compiler_params=pltpu.CompilerParams(has_side_effects=EFFECT, collective_id=SIBLING_ID))(
            *[_in_hbm(a) for a in list(grads) + lands])
    return (outs[0], outs[1]), list(outs[2:2 + n]), list(outs[2 + n:2 + 2 * n]), outs[-1]


def _rs_pair_wait(sems, grads, lands, after, name):
    n = len(grads)

    def body(*refs):
        g, theirs = refs[:n], refs[n:2 * n]
        ssem, rsem = refs[2 * n], refs[2 * n + 1]
        x, y, c = _place()
        for t in range(n):
            cp = pltpu.make_async_remote_copy(src_ref=g[t].at[:, 1 - c], dst_ref=theirs[t], send_sem=ssem.at[t],
                                              recv_sem=rsem.at[t], device_id=(x, y, c), device_id_type=MESH)
            cp.wait_send()
            cp.wait_recv()

    outs = pl.pallas_call(
        body, name=name, out_shape=[pltpu.HBM(a.shape, bf16) for a in list(grads) + list(lands)],
        in_specs=[HBM] * (2 * n) + [SEM, SEM] + [ANY] * len(after), out_specs=[HBM] * (2 * n),
        input_output_aliases={t: t for t in range(2 * n)},
        compiler_params=pltpu.CompilerParams(has_side_effects=EFFECT))(*grads, *lands, sems[0], sems[1], *after)
    return list(outs[:n]), list(outs[n:])


def _rs_chip_start(pairs, name):
    n = len(pairs)

    def body(*refs):
        p, q = refs[:n], refs[n:2 * n]
        ssem, rsem, token = refs[2 * n], refs[2 * n + 1], refs[-1]
        x, y, c = _place()
        for t in range(n):
            for j, ((px, py), s_p) in enumerate(_other_chips(x, y)):
                pltpu.make_async_remote_copy(src_ref=p[t].at[s_p], dst_ref=q[t].at[j], send_sem=ssem.at[3 * t + j],
                                             recv_sem=rsem.at[3 * t + j], device_id=(px, py, c), device_id_type=MESH).start()
        token[...] = jnp.zeros_like(token)

    lands = [lax.empty((3,) + p.shape[1:], bf16) for p in pairs]
    out_shape = ([pltpu.SemaphoreType.DMA((3 * n,))] * 2 + [pltpu.HBM(p.shape, bf16) for p in pairs]
                 + [pltpu.HBM(q.shape, bf16) for q in lands] + [_sds((8, LANES), f32)])
    outs = pl.pallas_call(
        body, name=name, out_shape=out_shape, in_specs=[HBM] * (2 * n), out_specs=[SEM, SEM] + [HBM] * (2 * n) + [VMEM],
        input_output_aliases={t: 2 + t for t in range(2 * n)},
        compiler_params=pltpu.CompilerParams(has_side_effects=EFFECT))(*[_in_hbm(a) for a in list(pairs) + lands])
    return (outs[0], outs[1]), list(outs[2:2 + n]), list(outs[2 + n:2 + 2 * n]), outs[-1]


def _rs_chip_wait(sems, pairs, lands, after, name):
    n = len(pairs)

    def body(*refs):
        p, q = refs[:n], refs[n:2 * n]
        ssem, rsem = refs[2 * n], refs[2 * n + 1]
        x, y, c = _place()
        for t in range(n):
            for j, (_, s_p) in enumerate(_other_chips(x, y)):
                cp = pltpu.make_async_remote_copy(src_ref=p[t].at[s_p], dst_ref=q[t].at[j], send_sem=ssem.at[3 * t + j],
                                                  recv_sem=rsem.at[3 * t + j], device_id=(x, y, c), device_id_type=MESH)
                cp.wait_send()
                cp.wait_recv()

    outs = pl.pallas_call(
        body, name=name, out_shape=[pltpu.HBM(a.shape, bf16) for a in list(pairs) + list(lands)],
        in_specs=[HBM] * (2 * n) + [SEM, SEM] + [ANY] * len(after), out_specs=[HBM] * (2 * n),
        input_output_aliases={t: t for t in range(2 * n)},
        compiler_params=pltpu.CompilerParams(has_side_effects=EFFECT))(*pairs, *lands, sems[0], sems[1], *after)
    return list(outs[:n]), list(outs[n:])


def _rs_half_start(halves, name):
    n = len(halves)

    def body(*refs):
        o = refs[:n]
        ssem, rsem, token = refs[n], refs[n + 1], refs[-1]
        x, y, c = _sibling_handshake()
        for t in range(n):
            pltpu.make_async_remote_copy(src_ref=o[t].at[c], dst_ref=o[t].at[c], send_sem=ssem.at[t],
                                         recv_sem=rsem.at[t], device_id=(x, y, 1 - c), device_id_type=MESH).start()
        token[...] = jnp.zeros_like(token)

    out_shape = ([pltpu.SemaphoreType.DMA((n,))] * 2 + [pltpu.HBM(h.shape, h.dtype) for h in halves]
                 + [_sds((8, LANES), f32)])
    outs = pl.pallas_call(
        body, name=name, out_shape=out_shape, in_specs=[HBM] * n, out_specs=[SEM, SEM] + [HBM] * n + [VMEM],
        input_output_aliases={t: 2 + t for t in range(n)},
        compiler_params=pltpu.CompilerParams(has_side_effects=EFFECT, collective_id=SIBLING_ID))(
            *[_in_hbm(h) for h in halves])
    return (outs[0], outs[1]), list(outs[2:2 + n]), outs[-1]


def _rs_half_wait(sems, inflight, after, name):
    n = len(inflight)

    def body(*refs):
        o, ssem, rsem = refs[:n], refs[n], refs[n + 1]
        x, y, c = _place()
        for t in range(n):
            cp = pltpu.make_async_remote_copy(src_ref=o[t].at[c], dst_ref=o[t].at[1 - c], send_sem=ssem.at[t],
                                              recv_sem=rsem.at[t], device_id=(x, y, c), device_id_type=MESH)
            cp.wait_send()
            cp.wait_recv()

    return pl.pallas_call(
        body, name=name, out_shape=[pltpu.HBM(h.shape, h.dtype) for h in inflight],
        in_specs=[HBM] * n + [SEM, SEM, ANY], out_specs=[HBM] * n, input_output_aliases={t: t for t in range(n)},
        compiler_params=pltpu.CompilerParams(has_side_effects=EFFECT))(*inflight, sems[0], sems[1], after)


def _row_spec(tm, cols):
    return pl.BlockSpec((tm, cols), lambda i: (i, 0))


def _vec_spec(cols, rows=1):
    return pl.BlockSpec((rows, cols), lambda i: (0, 0))


def _modulated_norm(xv, g, shift, scale):
    r = lax.rsqrt(jnp.mean(xv * xv, axis=-1, keepdims=True) + EPS)
    return (((xv * r) * g) * (1.0 + scale) + shift).astype(bf16)


def _hnorm(x, g, shift, scale):
    T, tm = x.shape[0], 256

    def body(x_ref, g_ref, sh_ref, sc_ref, h_ref):
        h_ref[...] = _modulated_norm(x_ref[...], g_ref[...], sh_ref[...], sc_ref[...])

    return _pcall(body, name="hnorm", out_shape=_sds((T, D), bf16), grid=(T // tm,),
                  in_specs=[_row_spec(tm, D), _vec_spec(D), _vec_spec(D), _vec_spec(D)],
                  out_specs=_row_spec(tm, D))(x, g, shift, scale)


def _out_proj(y2, wo, x, gate, nxt=None):
    T, tm = x.shape[0], 512

    def body(y_ref, w_ref, x_ref, g_ref, *rest):
        o = jnp.dot(y_ref[0], w_ref[0], preferred_element_type=f32)
        o = o + jnp.dot(y_ref[1], w_ref[1], preferred_element_type=f32)
        xo = x_ref[...] + g_ref[...] * o
        if nxt is None:
            xo_ref, o_ref = rest
        else:
            ng_ref, nsh_ref, nsc_ref, xo_ref, o_ref, h_ref = rest
            h_ref[...] = _modulated_norm(xo, ng_ref[...], nsh_ref[...], nsc_ref[...])
        o_ref[...] = o.astype(bf16)
        xo_ref[...] = xo

    extra = [] if nxt is None else list(nxt)
    n_out = 2 if nxt is None else 3
    return _pcall(body, name="out_proj", out_shape=[_sds((T, D), f32), _sds((T, D), bf16), _sds((T, D), bf16)][:n_out],
                  grid=(T // tm,),
                  in_specs=[pl.BlockSpec((2, tm, D), lambda i: (0, i, 0)), pl.BlockSpec((2, D, D), lambda i: (0, 0, 0)),
                            _row_spec(tm, D), _vec_spec(D)] + [_vec_spec(D)] * len(extra),
                  out_specs=[_row_spec(tm, D)] * n_out, vmem_mb=40)(y2, wo, x, gate, *extra)


def _gate_bwd_tile(dx, o_ref, gate_ref, dob_ref, dgate_ref):
    dob_ref[...] = (dx * gate_ref[...]).astype(bf16)
    dgate_ref[...] += jnp.sum(dx * o_ref[...].astype(f32), axis=0, keepdims=True)


def _loss_bwd(x, target, g, o, gate):
    T, tm = x.shape[0], 512

    def body(x_ref, t_ref, g_ref, o_ref, gate_ref, dx_ref, loss_ref, dg_ref, dob_ref, dgate_ref):
        @pl.when(pl.program_id(0) == 0)
        def _():
            loss_ref[...] = jnp.zeros_like(loss_ref)
            dg_ref[...] = jnp.zeros_like(dg_ref)
            dgate_ref[...] = jnp.zeros_like(dgate_ref)

        xv, gv = x_ref[...], g_ref[...]
        r = lax.rsqrt(jnp.mean(xv * xv, axis=-1, keepdims=True) + EPS)
        xn = xv * r
        err = xn * gv - t_ref[...]
        dy = err * (1.0 / D)
        dxn = dy * gv
        dx = r * (dxn - xn * jnp.mean(dxn * xn, axis=-1, keepdims=True))
        dx_ref[...] = dx
        dg_ref[...] += jnp.sum(dy * xn, axis=0, keepdims=True)
        loss_ref[...] += (0.5 / D) * jnp.sum(jnp.sum(err * err, axis=1, keepdims=True), axis=0, keepdims=True)
        _gate_bwd_tile(dx, o_ref, gate_ref, dob_ref, dgate_ref)

    return _pcall(body, name="loss_bwd",
                  out_shape=[_sds((T, D), f32), _sds((1, 1), f32), _sds((1, D), f32), _sds((T, D), bf16), _sds((1, D), f32)],
                  grid=(T // tm,),
                  in_specs=[_row_spec(tm, D), _row_spec(tm, D), _vec_spec(D), _row_spec(tm, D), _vec_spec(D)],
                  out_specs=[_row_spec(tm, D), pl.BlockSpec((1, 1), lambda i: (0, 0)), _vec_spec(D), _row_spec(tm, D),
                             _vec_spec(D)])(x, target, g, o, gate)


def _norm_bwd(x, dh, gin, g, scale, below=None):
    T, tm = x.shape[0], 256

    def body(x_ref, dh_ref, gin_ref, g_ref, sc_ref, *rest):
        if below is None:
            dx_ref, st_ref = rest
        else:
            o_ref, gate_ref, dx_ref, st_ref, dob_ref, dgate_ref = rest

        @pl.when(pl.program_id(0) == 0)
        def _():
            st_ref[...] = jnp.zeros_like(st_ref)
            if below is not None:
                dgate_ref[...] = jnp.zeros_like(dgate_ref)

        xv, gv, dhv = x_ref[...], g_ref[...], dh_ref[...]
        r = lax.rsqrt(jnp.mean(xv * xv, axis=-1, keepdims=True) + EPS)
        xn = xv * r
        da = dhv * (1.0 + sc_ref[...])
        dxn = da * gv
        dx = gin_ref[...] + r * (dxn - xn * jnp.mean(dxn * xn, axis=-1, keepdims=True))
        dx_ref[...] = dx
        st_ref[0:1, :] += jnp.sum(dhv, axis=0, keepdims=True)
        st_ref[1:2, :] += jnp.sum(dhv * (xn * gv), axis=0, keepdims=True)
        st_ref[2:3, :] += jnp.sum(da * xn, axis=0, keepdims=True)
        if below is not None:
            _gate_bwd_tile(dx, o_ref, gate_ref, dob_ref, dgate_ref)

    out_shape = [_sds((T, D), f32), _sds((8, D), f32)]
    in_specs = [_row_spec(tm, D), _row_spec(tm, D), _row_spec(tm, D), _vec_spec(D), _vec_spec(D)]
    out_specs = [_row_spec(tm, D), _vec_spec(D, 8)]
    args = [x, dh, gin, g, scale]
    if below is not None:
        out_shape += [_sds((T, D), bf16), _sds((1, D), f32)]
        in_specs += [_row_spec(tm, D), _vec_spec(D)]
        out_specs += [_row_spec(tm, D), _vec_spec(D)]
        args += list(below)
    return _pcall(body, name="norm_bwd", out_shape=out_shape, grid=(T // tm,), in_specs=in_specs,
                  out_specs=out_specs)(*args)


STEPS = 4
ADAMW_STEPS = 8


def _cast_place(place, ws, layer, after=None):
    n = len(ws)

    def body(place_ref, *refs):
        for t in range(n):
            refs[-n + t][...] = refs[t][...].astype(bf16)

    def tile(w):
        return w.shape[1] // STEPS, w.shape[2]

    extra = [] if after is None else [after]
    return _pcall(body, name="cast_place", out_shape=[_sds((4,) + w.shape[1:], bf16) for w in ws], grid=(STEPS,),
                  prefetch=1,
                  in_specs=[pl.BlockSpec((None,) + tile(w), lambda i, pr: (layer, i, 0)) for w in ws] + [ANY] * len(extra),
                  out_specs=[pl.BlockSpec((None,) + tile(w), lambda i, pr: (pr[0], i, 0)) for w in ws])(
                      place, *ws, *extra)


def _rs_add(place, grads, theirs):
    n = len(grads)

    def body(place_ref, *refs):
        for t in range(n):
            refs[2 * n + t][...] = (refs[t][...].astype(f32) + refs[n + t][...].astype(f32)).astype(bf16)

    def tile(q):
        return q.shape[1] // 2, q.shape[2]

    mine = [pl.BlockSpec((None, None) + tile(q), lambda s, i, pr: (s, pr[1], i, 0)) for q in theirs]
    shard = [pl.BlockSpec((None,) + tile(q), lambda s, i, pr: (s, i, 0)) for q in theirs]
    return _pcall(body, name="rs_add", out_shape=[_sds(q.shape, bf16) for q in theirs], grid=(4, 2), prefetch=1,
                  in_specs=mine + shard, out_specs=shard)(place, *grads, *theirs)


def _rs_sum(place, pairs, slots):
    n, steps = len(pairs), 4

    def body(place_ref, *refs):
        for t in range(n):
            p_ref, q_ref = refs[t], refs[n + t]
            total = ((p_ref[...].astype(f32) + q_ref[0].astype(f32)) + q_ref[1].astype(f32)) + q_ref[2].astype(f32)
            refs[2 * n + t][...] = total.astype(bf16)

    def tile(q):
        return q.shape[1] // steps, q.shape[2]

    return _pcall(body, name="rs_sum", out_shape=[_sds((2,) + q.shape[1:], bf16) for q in slots], grid=(steps,),
                  prefetch=1,
                  in_specs=[pl.BlockSpec((None,) + tile(q), lambda i, pr: (pr[0], i, 0)) for q in slots]
                  + [pl.BlockSpec((3,) + tile(q), lambda i, pr: (0, i, 0)) for q in slots],
                  out_specs=[pl.BlockSpec((None,) + tile(q), lambda i, pr: (pr[1], i, 0)) for q in slots])(
                      place, *pairs, *slots)


def _adamw_math(w, g, m, v):
    m = ADAM_B1 * m + (1.0 - ADAM_B1) * g
    v = ADAM_B2 * v + (1.0 - ADAM_B2) * jnp.square(g)
    m_hat = m / (1.0 - ADAM_B1 ** ADAM_STEP)
    v_hat = v / (1.0 - ADAM_B2 ** ADAM_STEP)
    delta = -ADAM_LR * (m_hat / (jnp.sqrt(v_hat) + ADAM_EPS) + ADAM_WD * w)
    return delta, m, v


def _adamw_layer(layer, items):
    n = len(items)

    def body(*refs):
        outs = refs[-4 * n:]
        for t in range(n):
            w_ref, g_ref, m_ref, v_ref = refs[4 * t:4 * t + 4]
            g = g_ref[...].astype(f32)
            outs[4 * t][...] = g
            outs[4 * t + 1][...], outs[4 * t + 2][...], outs[4 * t + 3][...] = _adamw_math(
                w_ref[...], g, m_ref[...], v_ref[...])

    args, in_specs, out_specs, out_shape = [], [], [], []
    for w, g, m, v, _ in items:
        tr, cols = w.shape[1] // ADAMW_STEPS, w.shape[2]
        spec = pl.BlockSpec((None, tr, cols), lambda i: (layer, i, 0))
        args += [w, g, m, v]
        in_specs += [spec, pl.BlockSpec((tr, cols), lambda i: (i, 0)), spec, spec]
        out_specs += [spec] * 4
        out_shape += [_sds(w.shape, f32)] * 4
    aliases = {}
    for t, it in enumerate(items):
        if it[4] is not None:
            for k in range(4):
                aliases[len(args)] = 4 * t + k
                args.append(it[4][k])
                in_specs.append(ANY)
    res = _pcall(body, name="adamw", out_shape=out_shape, grid=(ADAMW_STEPS,), in_specs=in_specs, out_specs=out_specs,
                 aliases=aliases)(*args)
    return [tuple(res[4 * t:4 * t + 4]) for t in range(n)]


def _adamw_small(items):
    n = len(items)

    def body(*refs):
        ins, outs = refs[:4 * n], refs[4 * n:]
        for t in range(n):
            w_ref, g_ref, m_ref, v_ref = ins[4 * t:4 * t + 4]
            if len(g_ref.shape) == len(w_ref.shape) + 1:
                g = g_ref[0]
                for b in range(1, g_ref.shape[0]):
                    g = g + g_ref[b]
            else:
                g = g_ref[...]
            d, m, v = _adamw_math(w_ref[...], g, m_ref[...], v_ref[...])
            outs[4 * t][...], outs[4 * t + 1][...], outs[4 * t + 2][...], outs[4 * t + 3][...] = g, d, m, v

    out_shape = [_sds(w.shape, f32) for (w, _, _, _) in items for _ in range(4)]
    flat = [a for it in items for a in it]
    res = _pcall(body, name="adamw_small", out_shape=out_shape, in_specs=[VMEM] * (4 * n),
                 out_specs=[VMEM] * (4 * n))(*flat)
    return [tuple(res[4 * t:4 * t + 4]) for t in range(n)]


NN = ((1,), (0,))
NT = ((1,), (1,))
TN = ((0,), (0,))


def _mm(name, a, b, *, grid, a_spec, b_spec, out_shape, out_spec, dims, vmem_mb=None):
    def body(a_ref, b_ref, o_ref):
        r = lax.dot_general(a_ref[...], b_ref[...], (dims, ((), ())), preferred_element_type=f32)
        o_ref[...] = r.astype(o_ref.dtype)

    return _pcall(body, name=name, out_shape=out_shape, grid=grid, in_specs=[a_spec, b_spec], out_specs=out_spec,
                  vmem_mb=vmem_mb)(a, b)


def _whole(shape):
    return pl.BlockSpec(shape, lambda j: (0,) * len(shape))


def _split_spec(rows, tile, per_split):
    return pl.BlockSpec((None, rows, tile), lambda j: (j // per_split, 0, j % per_split))


class _Proj:
    def __init__(self, n, splits, tile):
        self.n, self.splits, self.tile = n, splits, tile
        self.steps = n // tile
        self.w_per = n // 4 // tile
        self.a_per = n // splits // tile
        assert self.w_per * tile * 4 == n and self.a_per * tile * splits == n

    def fwd(self, hb, wg):
        T = hb.shape[0]
        sub, tile, w_per = FWD_TILES, self.tile, self.w_per
        wide = sub * tile
        a_per = self.n // self.splits // wide
        assert a_per * wide * self.splits == self.n

        def w_tile(q):
            return pl.BlockSpec((None, D, tile), lambda j: ((sub * j + q) // w_per, 0, (sub * j + q) % w_per))

        def body(a_ref, *rest):
            w = jnp.concatenate([rest[q][...] for q in range(sub)], axis=1)
            rest[sub][...] = jnp.dot(a_ref[...], w, preferred_element_type=f32).astype(bf16)

        return _pcall(body, name="proj_fwd", out_shape=_sds((self.splits, T, self.n // self.splits), bf16),
                      grid=(self.n // wide,), in_specs=[_whole((T, D))] + [w_tile(q) for q in range(sub)],
                      out_specs=pl.BlockSpec((None, T, wide), lambda j: (j // a_per, 0, j % a_per)),
                      vmem_mb=40 if wide > 512 else None)(hb, *([wg] * sub))

    def dw(self, hb, dp):
        T = hb.shape[0]
        return _mm("proj_dw", hb, dp, grid=(self.steps,), a_spec=_whole((T, D)),
                   b_spec=_split_spec(T, self.tile, self.a_per), out_shape=_sds((4, D, self.n // 4), bf16),
                   out_spec=_split_spec(D, self.tile, self.w_per), dims=TN)

    def dh(self, dp, wg):
        T = dp.shape[1]
        sub, tile, w_per = DH_WIDE // self.tile, self.tile, self.w_per
        a_per = self.n // self.splits // DH_WIDE
        assert sub * tile == DH_WIDE and a_per * DH_WIDE * self.splits == self.n

        def w_tile(q):
            return pl.BlockSpec((None, D, tile), lambda k: ((sub * k + q) // w_per, 0, (sub * k + q) % w_per))

        def body(a_ref, *rest):
            o_ref = rest[sub]
            w = jnp.concatenate([rest[q][...] for q in range(sub)], axis=1)
            r = lax.dot_general(a_ref[...], w, (NT, ((), ())), preferred_element_type=f32)

            @pl.when(pl.program_id(0) == 0)
            def _():
                o_ref[...] = r

            @pl.when(pl.program_id(0) > 0)
            def _():
                o_ref[...] += r

        return _pcall(body, name="proj_dh", out_shape=_sds((T, D), f32), grid=(self.n // DH_WIDE,),
                      in_specs=[pl.BlockSpec((None, T, DH_WIDE), lambda k: (k // a_per, 0, k % a_per))]
                      + [w_tile(q) for q in range(sub)],
                      out_specs=_whole((T, D)), vmem_mb=40)(dp, *([wg] * sub))


EVEN_PROJ = _Proj(7 * D, 7, 256)
ODD_PROJ = _Proj(4 * D, 2, 512)


def _out_bwd(dob, wo, y2):
    T = dob.shape[0]
    w_spec = pl.BlockSpec((None, 512, D), lambda j: (j, 0, 0))

    def body(dob_ref, w_ref, y_ref, dy_ref, dw_ref):
        dob_v = dob_ref[...]
        dy_ref[...] = lax.dot_general(dob_v, w_ref[...], (NT, ((), ())), preferred_element_type=f32).astype(bf16)
        dw_ref[...] = lax.dot_general(y_ref[...], dob_v, (TN, ((), ())), preferred_element_type=f32).astype(bf16)

    return _pcall(body, name="out_bwd", out_shape=[_sds((2, T, D), bf16), _sds((4, 512, D), bf16)], grid=(4,),
                  in_specs=[_whole((T, D)), w_spec, _split_spec(T, 512, 2)],
                  out_specs=[_split_spec(T, 512, 2), w_spec])(dob, wo, y2)


def _head_spec(lead, T):
    return pl.BlockSpec((lead, T, HEAD), lambda h: (0, 0, h))


def _head_vec(rows):
    return pl.BlockSpec((rows, HEAD), lambda h: (0, h))


_HEAD_MAT = pl.BlockSpec((None, HEAD, HEAD), lambda h: (h, 0, 0))


def _causal():
    return lax.broadcasted_iota(jnp.int32, (HEAD, HEAD), 0) >= lax.broadcasted_iota(jnp.int32, (HEAD, HEAD), 1)


def _layernorm_head(v):
    mu = jnp.mean(v, axis=-1, keepdims=True)
    d = v - mu
    rstd = lax.rsqrt(jnp.mean(d * d, axis=-1, keepdims=True) + EPS)
    return d * rstd, rstd


def _even_fwd(p7, conv_w, ln_g, ln_b, sgu_w, sgu_bias):
    T, C = p7.shape[1], CHUNK_ROWS

    def body(p_ref, cw_ref, lg_ref, lb_ref, w_ref, b_ref, y_ref):
        w0, w1, w2 = cw_ref[0:1, :], cw_ref[1:2, :], cw_ref[2:3, :]
        wm = jnp.where(_causal(), w_ref[...], 0.0).astype(bf16)
        bias, lg, lb = b_ref[...], lg_ref[...], lb_ref[...]

        def step(i, halo):
            rows = pl.ds(pl.multiple_of(i * C, C), C)
            ah, ab, ac, az, u, v, zb = (p_ref[k, rows, :].astype(f32) for k in range(7))
            tt = ac * ah
            ext = jnp.concatenate([halo, tt], axis=0)
            cv = w2 * tt + w1 * pltpu.roll(ext, 1, 0)[HALO_CONV:] + w0 * pltpu.roll(ext, 2, 0)[HALO_CONV:]
            y_ref[0, rows, :] = (ab * cv * _silu(az)).astype(bf16)
            vhat, _ = _layernorm_head(v)
            vn = (vhat * lg + lb).astype(bf16)
            mix = jnp.concatenate([jnp.dot(wm, vn[k * HEAD:(k + 1) * HEAD], preferred_element_type=f32) + bias
                                   for k in range(C // HEAD)], axis=0)
            y_ref[1, rows, :] = (u * mix * _silu(zb)).astype(bf16)
            return tt[C - HALO_CONV:]

        lax.fori_loop(0, T // C, step, jnp.zeros((HALO_CONV, HEAD), f32))

    return _pcall(body, name="even_fwd", out_shape=_sds((2, T, D), bf16), grid=(NH,),
                  in_specs=[_head_spec(7, T), _head_vec(3), _head_vec(1), _head_vec(1), _HEAD_MAT, _HEAD_MAT],
                  out_specs=_head_spec(2, T))(p7, conv_w, ln_g, ln_b, sgu_w, sgu_bias)


def _even_bwd(p7, dy2, conv_w, ln_g, ln_b, sgu_w, sgu_bias):
    T, C = p7.shape[1], CHUNK_ROWS
    n_chunks = T // C

    def body(p_ref, dy_ref, cw_ref, lg_ref, lb_ref, w_ref, b_ref,
             dp_ref, dcw_ref, dlg_ref, dlb_ref, dw_ref, dms_ref, dcv_s):
        w0, w1, w2 = cw_ref[0:1, :], cw_ref[1:2, :], cw_ref[2:3, :]
        tri = _causal()
        wm = jnp.where(tri, w_ref[...], 0.0).astype(bf16)
        bias, lg, lb = b_ref[...], lg_ref[...], lb_ref[...]
        dw_ref[...] = jnp.zeros_like(dw_ref)
        dms_ref[...] = jnp.zeros_like(dms_ref)

        def fwd_step(i, carry):
            halo, a0, a1, a2, alg, alb = carry
            rows = pl.ds(pl.multiple_of(i * C, C), C)
            ah, ab, ac, az = (p_ref[k, rows, :].astype(f32) for k in range(4))
            dya = dy_ref[0, rows, :].astype(f32)
            tt = ac * ah
            ext = jnp.concatenate([halo, tt], axis=0)
            t1, t2 = pltpu.roll(ext, 1, 0)[HALO_CONV:], pltpu.roll(ext, 2, 0)[HALO_CONV:]
            cv = w2 * tt + w1 * t1 + w0 * t2
            sa, dsa = _silu_and_grad(az)
            g1 = dya * sa
            dp_ref[1, rows, :] = (g1 * cv).astype(bf16)
            dp_ref[3, rows, :] = (dya * ab * cv * dsa).astype(bf16)
            dcv = g1 * ab
            dcv_s[rows, :] = dcv
            a2 = a2 + jnp.sum(dcv * tt, axis=0, keepdims=True)
            a1 = a1 + jnp.sum(dcv * t1, axis=0, keepdims=True)
            a0 = a0 + jnp.sum(dcv * t2, axis=0, keepdims=True)

            u, zb, dyb = p_ref[4, rows, :].astype(f32), p_ref[6, rows, :].astype(f32), dy_ref[1, rows, :].astype(f32)
            vhat, rstd = _layernorm_head(p_ref[5, rows, :].astype(f32))
            vn = (vhat * lg + lb).astype(bf16)
            sb, dsb = _silu_and_grad(zb)
            mix = jnp.concatenate([jnp.dot(wm, vn[k * HEAD:(k + 1) * HEAD], preferred_element_type=f32) + bias
                                   for k in range(C // HEAD)], axis=0)
            dp_ref[4, rows, :] = (dyb * mix * sb).astype(bf16)
            dp_ref[6, rows, :] = (dyb * u * mix * dsb).astype(bf16)
            dmix = dyb * u * sb
            dvn_parts = []
            for k in range(C // HEAD):
                dm = dmix[k * HEAD:(k + 1) * HEAD]
                dmb = dm.astype(bf16)
                dvn_parts.append(lax.dot_general(wm, dmb, (TN, ((), ())), preferred_element_type=f32))
                dw_ref[...] += lax.dot_general(dmb, vn[k * HEAD:(k + 1) * HEAD], (NT, ((), ())),
                                               preferred_element_type=f32)
                dms_ref[...] += dm
            dvn = jnp.concatenate(dvn_parts, axis=0)
            alg = alg + jnp.sum(dvn * vhat, axis=0, keepdims=True)
            alb = alb + jnp.sum(dvn, axis=0, keepdims=True)
            dvh = dvn * lg
            dv = rstd * (dvh - jnp.mean(dvh, axis=-1, keepdims=True)
                         - vhat * jnp.mean(dvh * vhat, axis=-1, keepdims=True))
            dp_ref[5, rows, :] = dv.astype(bf16)
            return tt[C - HALO_CONV:], a0, a1, a2, alg, alb

        zrow = jnp.zeros((1, HEAD), f32)
        _, a0, a1, a2, alg, alb = lax.fori_loop(
            0, n_chunks, fwd_step, (jnp.zeros((HALO_CONV, HEAD), f32), zrow, zrow, zrow, zrow, zrow))
        dcw_ref[0:1, :], dcw_ref[1:2, :], dcw_ref[2:3, :] = a0, a1, a2
        dlg_ref[...], dlb_ref[...] = alg, alb
        dw_ref[...] = jnp.where(tri, dw_ref[...], 0.0)

        def bwd_step(k, halo):
            rows = pl.ds(pl.multiple_of((n_chunks - 1 - k) * C, C), C)
            dcv = dcv_s[rows, :]
            ext = jnp.concatenate([dcv, halo], axis=0)
            n1 = pltpu.roll(ext, C + HALO_CONV - 1, 0)[:C]
            n2 = pltpu.roll(ext, C + HALO_CONV - 2, 0)[:C]
            dtt = w2 * dcv + w1 * n1 + w0 * n2
            dp_ref[2, rows, :] = (dtt * p_ref[0, rows, :].astype(f32)).astype(bf16)
            dp_ref[0, rows, :] = (dtt * p_ref[2, rows, :].astype(f32)).astype(bf16)
            return dcv[:HALO_CONV]

        lax.fori_loop(0, n_chunks, bwd_step, jnp.zeros((HALO_CONV, HEAD), f32))

    out_shape = [_sds((7, T, D), bf16), _sds((3, D), f32), _sds((1, D), f32), _sds((1, D), f32),
                 _sds((NH, HEAD, HEAD), f32), _sds((NH, HEAD, HEAD), f32)]
    return _pcall(body, name="even_bwd", out_shape=out_shape, grid=(NH,),
                  in_specs=[_head_spec(7, T), _head_spec(2, T), _head_vec(3), _head_vec(1), _head_vec(1),
                            _HEAD_MAT, _HEAD_MAT],
                  out_specs=[_head_spec(7, T), _head_vec(3), _head_vec(1), _head_vec(1), _HEAD_MAT, _HEAD_MAT],
                  scratch=[pltpu.VMEM((T, HEAD), f32)])(p7, dy2, conv_w, ln_g, ln_b, sgu_w, sgu_bias)


def _window_sum(ext, win, towards_past):
    n, k, s = ext.shape[0], 1, ext
    while k < win:
        s = s + pltpu.roll(s, k if towards_past else n - k, 0)
        k *= 2
    return s


def _pool_count(i, C, win):
    t = i * C + lax.broadcasted_iota(jnp.int32, (C, 1), 0)
    cnt = jnp.minimum(t + 1, win).astype(f32)
    return cnt, 1.0 / cnt


def _group_specs(T):
    p_spec = pl.BlockSpec((None, T, GC), lambda g: (0, 0, g))
    z_spec = pl.BlockSpec((None, T, GC), lambda g: (1, 0, g))
    pw_spec = pl.BlockSpec((4, GC // 4, GC), lambda g: (0, g, 0))
    ps_spec = pl.BlockSpec((1, GC), lambda g: (0, g))
    y_spec = pl.BlockSpec((None, T, GC), lambda g: (g // 2, 0, g % 2))
    return p_spec, z_spec, pw_spec, ps_spec, y_spec


def _odd_fwd(p2, pool_wg, pool_scale):
    T, C = p2.shape[1], CHUNK_ROWS
    p_spec, z_spec, pw_spec, ps_spec, y_spec = _group_specs(T)

    def body(p_ref, z_ref, pw_ref, ps_ref, y_ref):
        pw, ps = pw_ref[...].reshape(GC, GC), ps_ref[...]

        def run(win):
            def step(i, halo):
                rows = pl.ds(pl.multiple_of(i * C, C), C)
                p = p_ref[rows, :].astype(f32)
                s = _window_sum(jnp.concatenate([halo, p], axis=0), win, True)[HALO_POOL:]
                pooled = s * _pool_count(i, C, win)[1] - p
                ypre = jnp.dot(pooled.astype(bf16), pw, preferred_element_type=f32)
                y_ref[rows, :] = (ypre * ps * _silu(z_ref[rows, :].astype(f32))).astype(bf16)
                return p[C - HALO_POOL:]

            lax.fori_loop(0, T // C, step, jnp.zeros((HALO_POOL, GC), f32))

        for gi, win in enumerate(WINDOWS):
            pl.when(pl.program_id(0) == gi)(functools.partial(run, win))

    return _pcall(body, name="odd_fwd", out_shape=_sds((2, T, D), bf16), grid=(len(WINDOWS),),
                  in_specs=[p_spec, z_spec, pw_spec, ps_spec], out_specs=y_spec)(p2, p2, pool_wg, pool_scale)


def _odd_bwd(p2, dy2, pool_wg, pool_scale):
    T, C = p2.shape[1], CHUNK_ROWS
    n_chunks = T // C
    p_spec, z_spec, pw_spec, ps_spec, y_spec = _group_specs(T)

    def body(p_ref, z_ref, dy_ref, pw_ref, ps_ref, dp_ref, dpw_ref, dps_ref, q_s, acc_s):
        pw, ps = pw_ref[...].reshape(GC, GC), ps_ref[...]

        def run(win):
            acc_s[...] = jnp.zeros_like(acc_s)

            def fwd_step(i, carry):
                halo, aps = carry
                rows = pl.ds(pl.multiple_of(i * C, C), C)
                p, z, dy = p_ref[rows, :].astype(f32), z_ref[rows, :].astype(f32), dy_ref[rows, :].astype(f32)
                _, inv_cnt = _pool_count(i, C, win)
                s = _window_sum(jnp.concatenate([halo, p], axis=0), win, True)[HALO_POOL:]
                pb = (s * inv_cnt - p).astype(bf16)
                ypre = jnp.dot(pb, pw, preferred_element_type=f32)
                sz, dsz = _silu_and_grad(z)
                aps = aps + jnp.sum(dy * ypre * sz, axis=0, keepdims=True)
                dp_ref[1, rows, :] = (dy * ypre * ps * dsz).astype(bf16)
                dyp = (dy * ps * sz).astype(bf16)
                acc_s[...] += lax.dot_general(pb, dyp, (TN, ((), ())), preferred_element_type=f32)
                dpool = lax.dot_general(dyp, pw, (NT, ((), ())), preferred_element_type=f32)
                q_s[rows, :] = dpool * inv_cnt
                return p[C - HALO_POOL:], aps

            _, aps = lax.fori_loop(0, n_chunks, fwd_step, (jnp.zeros((HALO_POOL, GC), f32), jnp.zeros((1, GC), f32)))
            dps_ref[...] = aps
            dpw_ref[...] = acc_s[...].reshape(4, GC // 4, GC).astype(bf16)

            def bwd_step(k, halo):
                i = n_chunks - 1 - k
                rows = pl.ds(pl.multiple_of(i * C, C), C)
                q = q_s[rows, :]
                s = _window_sum(jnp.concatenate([q, halo], axis=0), win, False)[:C]
                dp_ref[0, rows, :] = (s - q * _pool_count(i, C, win)[0]).astype(bf16)
                return q[:HALO_POOL]

            lax.fori_loop(0, n_chunks, bwd_step, jnp.zeros((HALO_POOL, GC), f32))

        for gi, win in enumerate(WINDOWS):
            pl.when(pl.program_id(0) == gi)(functools.partial(run, win))

    out_shape = [_sds((2, T, 2 * D), bf16), _sds((4, GC, GC), bf16), _sds((1, 2 * D), f32)]
    return _pcall(body, name="odd_bwd", out_shape=out_shape, grid=(len(WINDOWS),),
                  in_specs=[p_spec, z_spec, y_spec, pw_spec, ps_spec],
                  out_specs=[pl.BlockSpec((2, T, GC), lambda g: (0, 0, g)), pw_spec, ps_spec],
                  scratch=[pltpu.VMEM((T, GC), f32), pltpu.VMEM((GC, GC), f32)], vmem_mb=44)(
                      p2, p2, dy2, pool_wg, pool_scale)


def _ada_fwd(c_all, ada_w):
    cols = ada_w.shape[2]

    def body(c_ref, w_ref, o_ref):
        o_ref[...] = jnp.dot(_silu(c_ref[...]), w_ref[...], preferred_element_type=f32,
                             precision=lax.Precision.HIGHEST)

    return _pcall(body, name="ada_fwd", out_shape=_sds((4, N_DEV, cols), f32), grid=(4,),
                  in_specs=[pl.BlockSpec((N_DEV, D), lambda i: (0, 0)), pl.BlockSpec((None, D, cols), lambda i: (i, 0, 0))],
                  out_specs=pl.BlockSpec((None, N_DEV, cols), lambda i: (i, 0, 0)))(c_all, ada_w)


def _ada_bwd(c_all_t, dmod, w, m, v):
    cols, tr = w.shape[2], 256
    spec = pl.BlockSpec((None, tr, cols), lambda l, i: (l, i, 0))

    def body(c_ref, dm_ref, w_ref, m_ref, v_ref, g_ref, d_ref, mo_ref, vo_ref):
        sc = _silu(c_ref[...])
        g = sc[:, 0:1] * dm_ref[0:1, :]
        for b in range(1, N_DEV):
            g = g + sc[:, b:b + 1] * dm_ref[b:b + 1, :]
        g_ref[...] = g
        d_ref[...], mo_ref[...], vo_ref[...] = _adamw_math(w_ref[...], g, m_ref[...], v_ref[...])

    return _pcall(body, name="ada_bwd", out_shape=[_sds(w.shape, f32)] * 4, grid=(4, D // tr),
                  in_specs=[pl.BlockSpec((tr, N_DEV), lambda l, i: (i, 0)),
                            pl.BlockSpec((None, N_DEV, cols), lambda l, i: (l, 0, 0)), spec, spec, spec],
                  out_specs=[spec] * 4)(c_all_t, dmod, w, m, v)


def _layer_fwd(even, x, hb, gate, w, nxt, before_out=None):
    if even:
        w_in, w_out, conv_w, ln_g, ln_b, sgu_w, sgu_b = w
        bias = jnp.broadcast_to(sgu_b[:, :, None], (NH, HEAD, HEAD))
        p = EVEN_PROJ.fwd(hb, w_in)
        y2 = _even_fwd(p, conv_w, ln_g, ln_b, sgu_w, bias)
    else:
        w_in, pool_w, w_out, pool_scale = w
        p = ODD_PROJ.fwd(hb, w_in)
        y2 = _odd_fwd(p, pool_w, pool_scale)
    if before_out is not None:
        late_w_out, tok = before_out(y2)
        if late_w_out is not None:
            w_out = late_w_out
            w = (w_in, w_out) + tuple(w[2:]) if even else (w_in, pool_w, w_out, pool_scale)
        if tok is not None:
            gate = gate + tok[0:1, 0:1]
    outs = _out_proj(y2, w_out.reshape(2, D, D), x, gate, nxt)
    return outs[0], (None if nxt is None else outs[2]), (x, hb, p, y2, outs[1]), w


def _layer_bwd(even, gin, dob, dgate, saved, scale, g, w, below=None, send=None):
    x_in, hb, p, y2, o = saved
    if even:
        w_in, w_out, conv_w, ln_g, ln_b, sgu_w, sgu_b = w
        bias = jnp.broadcast_to(sgu_b[:, :, None], (NH, HEAD, HEAD))
        dy2, dwo = _out_bwd(dob, w_out, y2)
        dp, dconv, dlg, dlb, dsw, dms = _even_bwd(p, dy2, conv_w, ln_g, ln_b, sgu_w, bias)
        proj = EVEN_PROJ
        small = dict(conv_w=dconv, ln_g=dlg, ln_b=dlb, sgu_w=dsw, sgu_b=jnp.sum(dms, axis=-1))
        big = [proj.dw(hb, dp), dwo]
    else:
        w_in, pool_w, w_out, pool_scale = w
        dy2, dwo = _out_bwd(dob, w_out, y2)
        dp, dpw, dps = _odd_bwd(p, dy2, pool_w, pool_scale)
        proj = ODD_PROJ
        small = dict(pool_scale=dps)
        big = [proj.dw(hb, dp), dpw, dwo]
    if send is not None:
        big, tok = send(big)
        scale = scale + tok[0:1, 0:1]
    dh = proj.dh(dp, w_in)
    res = _norm_bwd(x_in, dh, gin, g, scale, below)
    stats = res[1]
    return (res[0], (None if below is None else (res[2], res[3])), big, small,
            jnp.concatenate([stats[0:2], dgate], axis=0), stats[2:3])


def _pack_rows(parts):
    rows = [p.reshape(-1, LANES) for p in parts]
    total = sum(r.shape[0] for r in rows)
    padded = -(-total // (8 * N_DEV)) * (8 * N_DEV)
    if padded > total:
        rows.append(jnp.zeros((padded - total, LANES), f32))
    return jnp.concatenate(rows, axis=0)


def _unpack_rows(buf, shapes):
    out, r = [], 0
    for shp in shapes:
        n = 1
        for d in shp:
            n *= d
        out.append(buf[r:r + n // LANES].reshape(shp))
        r += n // LANES
    return out


def kernel(x, c, norm_g, ada_w, ada_b, ab_w_in, ab_conv_w, ab_ln_g, ab_ln_b, ab_sgu_w, ab_sgu_b, ab_w_out, c_w_in, c_pool_w, c_pool_scale, c_w_out, final_g, loss_target, m_norm_g, m_ada_w, m_ada_b, m_ab_w_in, m_ab_conv_w, m_ab_ln_g, m_ab_ln_b, m_ab_sgu_w, m_ab_sgu_b, m_ab_w_out, m_c_w_in, m_c_pool_w, m_c_pool_scale, m_c_w_out, m_final_g, v_norm_g, v_ada_w, v_ada_b, v_ab_w_in, v_ab_conv_w, v_ab_ln_g, v_ab_ln_b, v_ab_sgu_w, v_ab_sgu_b, v_ab_w_out, v_c_w_in, v_c_pool_w, v_c_pool_scale, v_c_w_out, v_final_g):
    ix, iy, ic = _place()
    chip, dev = 2 * ix + iy, 4 * ix + 2 * iy + ic
    n_even, n_odd = ab_w_in.shape[0], c_w_in.shape[0]
    depth = n_even + n_odd
    acols = ada_w.shape[2]

    place = jnp.stack([chip, ic]).astype(jnp.int32)
    even_names, odd_names = ["ab_w_in", "ab_w_out"], ["c_w_in", "c_pool_w", "c_w_out"]
    params = {"ab_w_in": (ab_w_in, m_ab_w_in, v_ab_w_in), "ab_w_out": (ab_w_out, m_ab_w_out, v_ab_w_out),
              "c_w_in": (c_w_in, m_c_w_in, v_c_w_in), "c_w_out": (c_w_out, m_c_w_out, v_c_w_out),
              "c_pool_w": tuple(a.reshape(n_odd, GC, GC) for a in (c_pool_w, m_c_pool_w, v_c_pool_w))}

    def placed(names, layer, after=None):
        ws = [params[nm][0] for nm in names]
        return [p.reshape(4, 2, p.shape[1] // 2, p.shape[2]) for p in _cast_place(place, ws, layer, after)]

    def whole(arrays):
        return [g.reshape(4, 2 * g.shape[2], g.shape[3]) for g in arrays]

    first = _gather8(jnp.concatenate([c, ab_conv_w.reshape(1, -1), c_pool_scale.reshape(1, -1)], axis=1), "gather_c")
    c_all, small_all = first[:, 0, :D], first[0::2, 0, D:]
    sems_a, in_a, tok = _ag_start([placed(even_names[:1], 0)], first[0:1, 0, 0:LANES], "ag_start_0a")
    modp = _ada_fwd(c_all, ada_w)
    later = [placed(even_names[1:], 0, tok)]
    later += [placed(even_names if i % 2 == 0 else odd_names, i // 2, tok) for i in range(1, depth)]
    modg = _gather8(modp + tok[0:1, 0:1], "gather_mod", [lay[-1] for lay in later])
    mod_rows = lax.dynamic_index_in_dim(modg[0::2], dev, axis=2, keepdims=False)
    mod = jnp.transpose(mod_rows, (1, 0, 2)).reshape(depth, 3 * D) + ada_b
    mods = [(mod[i:i + 1, 0:D], mod[i:i + 1, D:2 * D], mod[i:i + 1, 2 * D:3 * D]) for i in range(depth)]

    def shard_cols(a, width):
        return lax.dynamic_slice_in_dim(a, chip * width, width, axis=a.ndim - 1)

    n_conv = ab_conv_w.size
    conv_all = small_all[:, :n_conv].reshape(4, n_even, 3, D // 4)
    conv_full = jnp.transpose(conv_all, (1, 2, 0, 3)).reshape(n_even, 3, D)
    scale_all = small_all[:, n_conv:].reshape(4, n_odd, 2 * D // 4)
    scale_full = jnp.transpose(scale_all, (1, 0, 2)).reshape(n_odd, 2 * D)

    gathers_done = mod[0:1, 0:LANES] + scale_full[0:1, 0:LANES]
    sems_b, in_b, tok = _ag_start(later[:1], gathers_done, "ag_start_0b")
    sems_r, in_r, tok = _ag_start(later[1:], tok, "ag_start_rest")

    x_cur, saved, weights, handoff = x[0], [], [], {}
    sems_f, in_f, tok = _agf_start(_ag_wait(in_a[0], sems_a[0], tok, "ag_wait_0a"), "agf_start_0")
    hb = _hnorm(x_cur, norm_g[0:1], mods[0][0] + tok[0:1, 0:1], mods[0][1])
    for i in range(depth):
        j = i // 2
        if i == 0:
            full = whole(_agf_wait(sems_f, in_f, hb, "agf_wait_0")) + [None]
        else:
            full = whole(_agf_wait(*handoff.pop(i), x_cur, f"agf_wait_{i}"))
        if i % 2 == 0:
            w = (full[0], full[1], conv_full[j], ab_ln_g[j:j + 1], ab_ln_b[j:j + 1], ab_sgu_w[j], ab_sgu_b[j])
        else:
            w = (full[0], full[1], full[2], scale_full[j:j + 1])

        def before_out(y2, i=i):
            w_out, tok = None, None
            if i == 0:
                w_out = whole(_ag_forward(_ag_wait(in_b[0], sems_b[0], y2, "ag_wait_0b"), "ag_forward"))[0]
            if i + 1 < depth:
                arrived = _ag_wait(in_r[i], sems_r[i], y2, f"ag_wait_{i + 1}")
                sems_f, inflight, tok = _agf_start(arrived, f"agf_start_{i + 1}")
                handoff[i + 1] = (sems_f, inflight)
            return w_out, tok

        nxt = (norm_g[i + 1:i + 2], mods[i + 1][0], mods[i + 1][1]) if i + 1 < depth else None
        x_cur, hb, sv, w = _layer_fwd(i % 2 == 0, x_cur, hb, mods[i][2], w, nxt, before_out)
        weights.append(w)
        saved.append(sv)
    gin, loss, dfinal_g, dob, dgate = _loss_bwd(x_cur, loss_target[0], final_g.reshape(1, D), saved[-1][4],
                                                mods[-1][2])

    stacked = {}

    def reduce_layer(i, sems, pairs, lands, after):
        pairs, slots = _rs_chip_wait(sems, pairs, lands, after, f"rs_chip_wait_{i}")
        half_sems, halves, _ = _rs_half_start(_rs_sum(place, pairs, slots), f"rs_half_start_{i}")
        return i, half_sems, halves

    def update_layer(i, half_sems, halves, after):
        names = even_names if i % 2 == 0 else odd_names
        grads = _rs_half_wait(half_sems, halves, after, f"rs_half_wait_{i}")
        items = [(params[nm][0], g.reshape(params[nm][0].shape[1:]), params[nm][1], params[nm][2], stacked.get(nm))
                 for nm, g in zip(names, grads)]
        for nm, res in zip(names, _adamw_layer(i // 2, items)):
            stacked[nm] = res
            updated.append(res[1])

    updated = []
    small_g, dmod, dnorm_g, pending, tok = [None] * depth, [None] * depth, [None] * depth, None, None
    exchanging = []
    for i in reversed(range(depth)):
        w = weights[i]
        if tok is not None:
            w = w[:2] + (w[2] + tok[0:1, 0:1],) + w[3:] if i % 2 == 0 else w[:3] + (w[3] + tok[0:1, 0:1],)
        below = (saved[i - 1][4], mods[i - 1][2]) if i > 0 else None

        def send(big_g, i=i):
            if exchanging:
                update_layer(*exchanging.pop(), big_g[0])
            big_g = [g.reshape(4, 2, g.shape[1] // 2, g.shape[2]) for g in big_g]
            sems, big_g, lands, tok = _rs_pair_start(big_g, f"rs_pair_start_{i}")
            return (sems, big_g, lands), tok

        gin, gate_bwd, sent, small_g[i], dmod[i], dnorm_g[i] = _layer_bwd(
            i % 2 == 0, gin, dob, dgate, saved[i], mods[i][1], norm_g[i:i + 1], w, below, send)
        if below is not None:
            dob, dgate = gate_bwd
        after = gin
        if i == 0:
            dmod_all = _gather8(jnp.stack(dmod).reshape(depth * 3 * D // LANES, LANES), "gather_dmod")
            after = dmod_all = dmod_all.reshape(N_DEV, depth, 3 * D)
        if i > 0:
            after, updated = [after] + updated, []
        else:
            after = [after]
        big_g, theirs = _rs_pair_wait(*sent, after, f"rs_pair_wait_{i}")
        pairs = _rs_add(place, big_g, theirs)
        sems, pairs, lands, tok = _rs_chip_start(pairs, f"rs_chip_start_{i}")
        if pending is not None:
            exchanging.append(reduce_layer(*pending, [tok]))
        pending = (i, sems, pairs, lands)
    grad_x = gin
    dnorm_g = jnp.concatenate(dnorm_g, axis=0)

    dmod_cols = jnp.transpose(shard_cols(dmod_all, acols), (1, 0, 2))
    r_ada_w = _ada_bwd(c_all.T, dmod_cols, ada_w, m_ada_w, v_ada_w)
    update_layer(*exchanging.pop(), r_ada_w[1])
    last = reduce_layer(*pending, [r_ada_w[1]] + updated)

    small_parts = [dnorm_g, dfinal_g,
                   jnp.stack([small_g[2 * j]["conv_w"] for j in range(n_even)]),
                   jnp.concatenate([small_g[2 * j]["ln_g"] for j in range(n_even)], axis=0),
                   jnp.concatenate([small_g[2 * j]["ln_b"] for j in range(n_even)], axis=0),
                   jnp.stack([small_g[2 * j]["sgu_b"] for j in range(n_even)]),
                   jnp.concatenate([small_g[2 * j + 1]["pool_scale"] for j in range(n_odd)], axis=0),
                   jnp.pad(loss, ((0, 7), (0, LANES - 1)))]
    small_shapes = [p.shape for p in small_parts]
    sgu_parts = [small_g[2 * j]["sgu_w"].reshape(NH * HEAD, HEAD) for j in range(n_even)]
    reduced = _allreduce8([_pack_rows(small_parts)] + sgu_parts, "allreduce_small", last[2][0])
    update_layer(*last, reduced[0])
    r_ab_w_in, r_ab_w_out, r_c_w_in, r_c_w_out = (stacked[nm] for nm in ("ab_w_in", "ab_w_out", "c_w_in", "c_w_out"))
    r_c_pool_w = tuple(a.reshape(c_pool_w.shape) for a in stacked["c_pool_w"])
    g_norm_g, g_final_g, g_conv_full, g_ln_g, g_ln_b, g_sgu_b, g_scale_full, loss_row = _unpack_rows(reduced[0],
                                                                                                     small_shapes)
    g_sgu_w = jnp.stack(reduced[1:])
    loss = loss_row[0, 0]
    g_conv = shard_cols(g_conv_full, D // 4)
    g_scale = shard_cols(g_scale_full, 2 * D // 4)

    def two_d(a):
        return a.reshape(-1, a.shape[-1])

    small = [(norm_g, g_norm_g, m_norm_g, v_norm_g),
             (ada_b, dmod_all, m_ada_b, v_ada_b),
             (two_d(ab_conv_w), two_d(g_conv), two_d(m_ab_conv_w), two_d(v_ab_conv_w)),
             (ab_ln_g, g_ln_g, m_ab_ln_g, v_ab_ln_g),
             (ab_ln_b, g_ln_b, m_ab_ln_b, v_ab_ln_b),
             (two_d(ab_sgu_w), two_d(g_sgu_w), two_d(m_ab_sgu_w), two_d(v_ab_sgu_w)),
             (two_d(ab_sgu_b), two_d(g_sgu_b), two_d(m_ab_sgu_b), two_d(v_ab_sgu_b)),
             (c_pool_scale, g_scale, m_c_pool_scale, v_c_pool_scale),
             (final_g.reshape(1, D), g_final_g, m_final_g.reshape(1, D), v_final_g.reshape(1, D))]
    small_res = _adamw_small(small)
    small_shapes_out = [norm_g.shape, ada_b.shape, ab_conv_w.shape, ab_ln_g.shape, ab_ln_b.shape, ab_sgu_w.shape,
                        ab_sgu_b.shape, c_pool_scale.shape, final_g.shape]
    (r_norm_g, r_ada_b, r_conv, r_ln_g, r_ln_b, r_sgu_w, r_sgu_b, r_scale, r_final_g) = [
        tuple(a.reshape(shp) for a in res) for res, shp in zip(small_res, small_shapes_out)]

    order = [r_norm_g, r_ada_w, r_ada_b, r_ab_w_in, r_conv, r_ln_g, r_ln_b, r_sgu_w, r_sgu_b, r_ab_w_out,
             r_c_w_in, r_c_pool_w, r_scale, r_c_w_out, r_final_g]
    outs = [loss, grad_x[None]]
    for field in range(4):
        outs += [r[field] for r in order]
    return tuple(outs)
```

```python
import functools

import jax
import jax.numpy as jnp
from jax import lax
from jax.experimental import pallas as pl
from jax.experimental.pallas import tpu as pltpu

f32, bf16 = jnp.float32, jnp.bfloat16

D = 1024
HEAD = 128
NH = 8
WINDOWS = (2, 4, 8, 16)
GC = 512
EPS = 1e-6
HALO_CONV = 8
HALO_POOL = 16
CHUNK_ROWS = 512
DH_WIDE = 1024
FWD_TILES = 2
N_DEV = 8
LANES = 128

ADAM_LR, ADAM_B1, ADAM_B2, ADAM_EPS, ADAM_WD, ADAM_STEP = 0.001, 0.9, 0.999, 1e-08, 0.01, 10

MESH = pl.DeviceIdType.MESH
ANY = pl.BlockSpec(memory_space=pl.ANY)
VMEM = pl.BlockSpec(memory_space=pltpu.VMEM)
MIB = 2 ** 20


def _pcall(body, *, name, out_shape, grid=None, in_specs=None, out_specs=None, scratch=(), vmem_mb=None,
           aliases=None, prefetch=0):
    kw = {}
    if prefetch:
        kw["grid_spec"] = pltpu.PrefetchScalarGridSpec(num_scalar_prefetch=prefetch, grid=grid, in_specs=in_specs,
                                                       out_specs=out_specs, scratch_shapes=list(scratch))
    else:
        if grid is not None:
            kw["grid"] = grid
        if in_specs is not None:
            kw["in_specs"] = in_specs
        if out_specs is not None:
            kw["out_specs"] = out_specs
        if scratch:
            kw["scratch_shapes"] = list(scratch)
    if aliases:
        kw["input_output_aliases"] = aliases
    params = pltpu.CompilerParams(vmem_limit_bytes=None if vmem_mb is None else vmem_mb * MIB)
    return pl.pallas_call(body, name=name, out_shape=out_shape, compiler_params=params, **kw)


def _sds(shape, dtype):
    return jax.ShapeDtypeStruct(tuple(shape), dtype)


def _sigmoid(z):
    return pl.reciprocal(1.0 + jnp.exp(-z), approx=True)


def _silu(z):
    return z * _sigmoid(z)


def _silu_and_grad(z):
    s = _sigmoid(z)
    return z * s, s * (1.0 + z * (1.0 - s))


def _place():
    return lax.axis_index("x"), lax.axis_index("y"), lax.axis_index("c")


def _gather8(blk, name, after=()):
    def body(x_ref, *rest):
        o_ref, ssem, rsem = rest[len(after):]
        x, y, c = _place()
        me = 4 * x + 2 * y + c
        o_ref[me] = x_ref[...]
        sends = []
        for k in range(1, N_DEV):
            px = 1 - x if k & 4 else x
            py = 1 - y if k & 2 else y
            pc = 1 - c if k & 1 else c
            cp = pltpu.make_async_remote_copy(src_ref=x_ref, dst_ref=o_ref.at[me], send_sem=ssem.at[k - 1],
                                              recv_sem=rsem.at[k - 1], device_id=(px, py, pc), device_id_type=MESH)
            cp.start()
            sends.append((cp, 4 * px + 2 * py + pc))
        for k, (cp, peer) in enumerate(sends):
            pltpu.make_async_remote_copy(src_ref=x_ref, dst_ref=o_ref.at[peer], send_sem=ssem.at[k],
                                         recv_sem=rsem.at[k], device_id=(x, y, c), device_id_type=MESH).wait_recv()
        for cp, _ in sends:
            cp.wait_send()

    return _pcall(body, name=name, out_shape=_sds((N_DEV,) + blk.shape, blk.dtype), in_specs=[VMEM] + [ANY] * len(after),
                  out_specs=VMEM,
                  scratch=[pltpu.SemaphoreType.DMA((N_DEV - 1,)), pltpu.SemaphoreType.DMA((N_DEV - 1,))])(blk, *after)


def _allreduce8(bufs, name, after=None):
    n, n_after = len(bufs), 0 if after is None else 1
    rbs = [b.shape[0] // N_DEV for b in bufs]
    assert all(rb * N_DEV == b.shape[0] and rb % 8 == 0 for rb, b in zip(rbs, bufs))

    def body(*refs):
        refs = refs[:n] + refs[n + n_after:]
        xs, outs, stages = refs[:n], refs[n:2 * n], refs[2 * n:3 * n]
        ssem, rsem = refs[3 * n:]
        x, y, c = _place()
        me = 4 * x + 2 * y + c
        peers = []
        for k in range(1, N_DEV):
            px = 1 - x if k & 4 else x
            py = 1 - y if k & 2 else y
            pc = 1 - c if k & 1 else c
            peers.append(((px, py, pc), 4 * px + 2 * py + pc))

        def blk(t, ref, idx):
            return ref.at[pl.ds(pl.multiple_of(idx * rbs[t], 8), rbs[t]), :]

        def copy(t, phase, k, src, dst, dev):
            return pltpu.make_async_remote_copy(src_ref=src, dst_ref=dst, send_sem=ssem.at[t, phase, k],
                                                recv_sem=rsem.at[t, phase, k], device_id=dev, device_id_type=MESH)

        scatter = [copy(t, 0, k, blk(t, xs[t], pidx), stages[t].at[me], dev)
                   for t in range(n) for k, (dev, pidx) in enumerate(peers)]
        for cp in scatter:
            cp.start()
        gather = []
        for t in range(n):
            stages[t][me] = blk(t, xs[t], me)[...]
            for k, (dev, pidx) in enumerate(peers):
                copy(t, 0, k, blk(t, xs[t], pidx), stages[t].at[pidx], dev).wait_recv()
            total = stages[t][0]
            for j in range(1, N_DEV):
                total = total + stages[t][j]
            blk(t, outs[t], me)[...] = total
            sends = [copy(t, 1, k, blk(t, outs[t], me), blk(t, outs[t], me), dev) for k, (dev, pidx) in enumerate(peers)]
            for cp in sends:
                cp.start()
            gather += sends
        for t in range(n):
            for k, (dev, pidx) in enumerate(peers):
                copy(t, 1, k, blk(t, outs[t], pidx), blk(t, outs[t], pidx), dev).wait_recv()
        for cp in scatter + gather:
            cp.wait_send()

    return _pcall(body, name=name, out_shape=[_sds(b.shape, f32) for b in bufs], in_specs=[VMEM] * n + [ANY] * n_after,
                  out_specs=[VMEM] * n,
                  scratch=[pltpu.VMEM((N_DEV, rb, LANES), f32) for rb in rbs]
                  + [pltpu.SemaphoreType.DMA((n, 2, N_DEV - 1)), pltpu.SemaphoreType.DMA((n, 2, N_DEV - 1))])(
                      *bufs, *([] if after is None else [after]))


def _other_chips(x, y):
    return [((1 - x, y), 2 * (1 - x) + y), ((x, 1 - y), 2 * x + (1 - y)), ((1 - x, 1 - y), 2 * (1 - x) + (1 - y))]


HBM = pl.BlockSpec(memory_space=pltpu.HBM)
SEM = pl.BlockSpec(memory_space=pltpu.SEMAPHORE)
EFFECT = pltpu.SideEffectType.DATAFLOW_SIDE_EFFECTING


def _in_hbm(a):
    return pltpu.with_memory_space_constraint(a, pltpu.HBM)


SIBLING_ID = 1


def _sibling_handshake():
    x, y, c = _place()
    barrier = pltpu.get_barrier_semaphore()
    pl.semaphore_signal(barrier, inc=1, device_id=(x, y, 1 - c), device_id_type=MESH)
    pl.semaphore_wait(barrier, 1)
    return x, y, c


def _ag_start(layers, after, name):
    flat = [t for lay in layers for t in lay]
    n, nl = len(flat), len(layers)

    def body(*refs):
        src = refs[:n]
        sems = refs[n + 1:n + 1 + 2 * nl]
        token = refs[-1]
        x, y, c = _place()
        s_me = 2 * x + y
        t = 0
        for i, lay in enumerate(layers):
            for k in range(len(lay)):
                for j, ((px, py), _) in enumerate(_other_chips(x, y)):
                    pltpu.make_async_remote_copy(src_ref=src[t].at[s_me, c], dst_ref=src[t].at[s_me, c],
                                                 send_sem=sems[2 * i].at[3 * k + j], recv_sem=sems[2 * i + 1].at[3 * k + j],
                                                 device_id=(px, py, c), device_id_type=MESH).start()
                t += 1
        token[...] = jnp.zeros_like(token)

    sem_shapes = [pltpu.SemaphoreType.DMA((3 * len(lay),)) for lay in layers for _ in range(2)]
    out_shape = sem_shapes + [pltpu.HBM(t.shape, t.dtype) for t in flat] + [_sds((8, LANES), f32)]
    outs = pl.pallas_call(
        body, name=name, out_shape=out_shape, in_specs=[HBM] * n + [ANY],
        out_specs=[SEM] * (2 * nl) + [HBM] * n + [VMEM], input_output_aliases={t: 2 * nl + t for t in range(n)},
        compiler_params=pltpu.CompilerParams(has_side_effects=EFFECT))(*[_in_hbm(t) for t in flat], after)
    sems = [(outs[2 * i], outs[2 * i + 1]) for i in range(nl)]
    thru, t = [], 2 * nl
    for lay in layers:
        thru.append(list(outs[t:t + len(lay)]))
        t += len(lay)
    return sems, thru, outs[-1]


def _ag_wait(inflight, sems, after, name):
    n = len(inflight)

    def body(*refs):
        src, ssem, rsem = refs[:n], refs[n], refs[n + 1]
        x, y, c = _place()
        s_me = 2 * x + y
        for k in range(n):
            for j, (_, s_p) in enumerate(_other_chips(x, y)):
                cp = pltpu.make_async_remote_copy(src_ref=src[k].at[s_me, c], dst_ref=src[k].at[s_p, c],
                                                  send_sem=ssem.at[3 * k + j], recv_sem=rsem.at[3 * k + j],
                                                  device_id=(x, y, c), device_id_type=MESH)
                cp.wait_send()
                cp.wait_recv()

    return pl.pallas_call(
        body, name=name, out_shape=[pltpu.HBM(t.shape, t.dtype) for t in inflight],
        in_specs=[HBM] * n + [SEM, SEM, ANY], out_specs=[HBM] * n, input_output_aliases={t: t for t in range(n)},
        compiler_params=pltpu.CompilerParams(has_side_effects=EFFECT))(*inflight, sems[0], sems[1], after)


def _ag_forward(arrived, name):
    n = len(arrived)

    def body(*refs):
        o = refs[n:2 * n]
        ssem, rsem = refs[2 * n:]
        x, y, c = _place()

        def copy(t, j, s, half, dev):
            return pltpu.make_async_remote_copy(src_ref=o[t].at[s, c], dst_ref=o[t].at[s, half], send_sem=ssem.at[t, j],
                                                recv_sem=rsem.at[t, j], device_id=dev, device_id_type=MESH)

        chips = _other_chips(x, y)
        sends = [copy(t, j, s_p, c, (x, y, 1 - c)) for t in range(n) for j, (_, s_p) in enumerate(chips)]
        for cp in sends:
            cp.start()
        for t in range(n):
            for j, (_, s_p) in enumerate(chips):
                copy(t, j, s_p, 1 - c, (x, y, c)).wait_recv()
        for cp in sends:
            cp.wait_send()

    return _pcall(body, name=name, out_shape=[_sds(p.shape, bf16) for p in arrived], in_specs=[ANY] * n,
                  out_specs=[ANY] * n, aliases={t: t for t in range(n)},
                  scratch=[pltpu.SemaphoreType.DMA((n, 3)), pltpu.SemaphoreType.DMA((n, 3))])(*arrived)


def _agf_start(arrived, name):
    n = len(arrived)

    def body(*refs):
        o = refs[:n]
        ssem, rsem, token = refs[n], refs[n + 1], refs[-1]
        x, y, c = _sibling_handshake()
        for t in range(n):
            for j, (_, s_p) in enumerate(_other_chips(x, y)):
                pltpu.make_async_remote_copy(src_ref=o[t].at[s_p, c], dst_ref=o[t].at[s_p, c],
                                             send_sem=ssem.at[3 * t + j], recv_sem=rsem.at[3 * t + j],
                                             device_id=(x, y, 1 - c), device_id_type=MESH).start()
        token[...] = jnp.zeros_like(token)

    out_shape = ([pltpu.SemaphoreType.DMA((3 * n,))] * 2 + [pltpu.HBM(a.shape, bf16) for a in arrived]
                 + [_sds((8, LANES), f32)])
    outs = pl.pallas_call(
        body, name=name, out_shape=out_shape, in_specs=[HBM] * n, out_specs=[SEM, SEM] + [HBM] * n + [VMEM],
        input_output_aliases={t: 2 + t for t in range(n)},
        compiler_params=pltpu.CompilerParams(has_side_effects=EFFECT, collective_id=SIBLING_ID))(
            *[_in_hbm(a) for a in arrived])
    return (outs[0], outs[1]), list(outs[2:2 + n]), outs[-1]


def _agf_wait(sems, inflight, after, name):
    n = len(inflight)

    def body(*refs):
        o, ssem, rsem = refs[:n], refs[n], refs[n + 1]
        x, y, c = _place()
        for t in range(n):
            for j, (_, s_p) in enumerate(_other_chips(x, y)):
                cp = pltpu.make_async_remote_copy(src_ref=o[t].at[s_p, c], dst_ref=o[t].at[s_p, 1 - c],
                                                  send_sem=ssem.at[3 * t + j], recv_sem=rsem.at[3 * t + j],
                                                  device_id=(x, y, c), device_id_type=MESH)
                cp.wait_send()
                cp.wait_recv()

    return pl.pallas_call(
        body, name=name, out_shape=[pltpu.HBM(a.shape, bf16) for a in inflight],
        in_specs=[HBM] * n + [SEM, SEM, ANY], out_specs=[HBM] * n, input_output_aliases={t: t for t in range(n)},
        compiler_params=pltpu.CompilerParams(has_side_effects=EFFECT))(*inflight, sems[0], sems[1], after)


def _rs_pair_start(grads, name):
    n = len(grads)

    def body(*refs):
        g, theirs = refs[:n], refs[n:2 * n]
        ssem, rsem, token = refs[2 * n], refs[2 * n + 1], refs[-1]
        x, y, c = _sibling_handshake()
        for t in range(n):
            pltpu.make_async_remote_copy(src_ref=g[t].at[:, 1 - c], dst_ref=theirs[t], send_sem=ssem.at[t],
                                         recv_sem=rsem.at[t], device_id=(x, y, 1 - c), device_id_type=MESH).start()
        token[...] = jnp.zeros_like(token)

    lands = [lax.empty((4,) + g.shape[2:], bf16) for g in grads]
    out_shape = ([pltpu.SemaphoreType.DMA((n,))] * 2 + [pltpu.HBM(g.shape, bf16) for g in grads]
                 + [pltpu.HBM(q.shape, bf16) for q in lands] + [_sds((8, LANES), f32)])
    outs = pl.pallas_call(
        body, name=name, out_shape=out_shape, in_specs=[HBM] * (2 * n), out_specs=[SEM, SEM] + [HBM] * (2 * n) + [VMEM],
        input_output_aliases={t: 2 + t for t in range(2 * n)},
        compiler_params=pltpu.CompilerParams(has_side_effects=EFFECT, collective_id=SIBLING_ID))(
            *[_in_hbm(a) for a in list(grads) + lands])
    return (outs[0], outs[1]), list(outs[2:2 + n]), list(outs[2 + n:2 + 2 * n]), outs[-1]


def _rs_pair_wait(sems, grads, lands, after, name):
    n = len(grads)

    def body(*refs):
        g, theirs = refs[:n], refs[n:2 * n]
        ssem, rsem = refs[2 * n], refs[2 * n + 1]
        x, y, c = _place()
        for t in range(n):
            cp = pltpu.make_async_remote_copy(src_ref=g[t].at[:, 1 - c], dst_ref=theirs[t], send_sem=ssem.at[t],
                                              recv_sem=rsem.at[t], device_id=(x, y, c), device_id_type=MESH)
            cp.wait_send()
            cp.wait_recv()

    outs = pl.pallas_call(
        body, name=name, out_shape=[pltpu.HBM(a.shape, bf16) for a in list(grads) + list(lands)],
        in_specs=[HBM] * (2 * n) + [SEM, SEM] + [ANY] * len(after), out_specs=[HBM] * (2 * n),
        input_output_aliases={t: t for t in range(2 * n)},
        compiler_params=pltpu.CompilerParams(has_side_effects=EFFECT))(*grads, *lands, sems[0], sems[1], *after)
    return list(outs[:n]), list(outs[n:])


def _rs_chip_start(pairs, name):
    n = len(pairs)

    def body(*refs):
        p, q = refs[:n], refs[n:2 * n]
        ssem, rsem, token = refs[2 * n], refs[2 * n + 1], refs[-1]
        x, y, c = _place()
        for t in range(n):
            for j, ((px, py), s_p) in enumerate(_other_chips(x, y)):
                pltpu.make_async_remote_copy(src_ref=p[t].at[s_p], dst_ref=q[t].at[j], send_sem=ssem.at[3 * t + j],
                                             recv_sem=rsem.at[3 * t + j], device_id=(px, py, c), device_id_type=MESH).start()
        token[...] = jnp.zeros_like(token)

    lands = [lax.empty((3,) + p.shape[1:], bf16) for p in pairs]
    out_shape = ([pltpu.SemaphoreType.DMA((3 * n,))] * 2 + [pltpu.HBM(p.shape, bf16) for p in pairs]
                 + [pltpu.HBM(q.shape, bf16) for q in lands] + [_sds((8, LANES), f32)])
    outs = pl.pallas_call(
        body, name=name, out_shape=out_shape, in_specs=[HBM] * (2 * n), out_specs=[SEM, SEM] + [HBM] * (2 * n) + [VMEM],
        input_output_aliases={t: 2 + t for t in range(2 * n)},
        compiler_params=pltpu.CompilerParams(has_side_effects=EFFECT))(*[_in_hbm(a) for a in list(pairs) + lands])
    return (outs[0], outs[1]), list(outs[2:2 + n]), list(outs[2 + n:2 + 2 * n]), outs[-1]


def _rs_chip_wait(sems, pairs, lands, after, name):
    n = len(pairs)

    def body(*refs):
        p, q = refs[:n], refs[n:2 * n]
        ssem, rsem = refs[2 * n], refs[2 * n + 1]
        x, y, c = _place()
        for t in range(n):
            for j, (_, s_p) in enumerate(_other_chips(x, y)):
                cp = pltpu.make_async_remote_copy(src_ref=p[t].at[s_p], dst_ref=q[t].at[j], send_sem=ssem.at[3 * t + j],
                                                  recv_sem=rsem.at[3 * t + j], device_id=(x, y, c), device_id_type=MESH)
                cp.wait_send()
                cp.wait_recv()

    outs = pl.pallas_call(
        body, name=name, out_shape=[pltpu.HBM(a.shape, bf16) for a in list(pairs) + list(lands)],
        in_specs=[HBM] * (2 * n) + [SEM, SEM] + [ANY] * len(after), out_specs=[HBM] * (2 * n),
        input_output_aliases={t: t for t in range(2 * n)},
        compiler_params=pltpu.CompilerParams(has_side_effects=EFFECT))(*pairs, *lands, sems[0], sems[1], *after)
    return list(outs[:n]), list(outs[n:])


def _rs_half_start(halves, name):
    n = len(halves)

    def body(*refs):
        o = refs[:n]
        ssem, rsem, token = refs[n], refs[n + 1], refs[-1]
        x, y, c = _sibling_handshake()
        for t in range(n):
            pltpu.make_async_remote_copy(src_ref=o[t].at[c], dst_ref=o[t].at[c], send_sem=ssem.at[t],
                                         recv_sem=rsem.at[t], device_id=(x, y, 1 - c), device_id_type=MESH).start()
        token[...] = jnp.zeros_like(token)

    out_shape = ([pltpu.SemaphoreType.DMA((n,))] * 2 + [pltpu.HBM(h.shape, h.dtype) for h in halves]
                 + [_sds((8, LANES), f32)])
    outs = pl.pallas_call(
        body, name=name, out_shape=out_shape, in_specs=[HBM] * n, out_specs=[SEM, SEM] + [HBM] * n + [VMEM],
        input_output_aliases={t: 2 + t for t in range(n)},
        compiler_params=pltpu.CompilerParams(has_side_effects=EFFECT, collective_id=SIBLING_ID))(
            *[_in_hbm(h) for h in halves])
    return (outs[0], outs[1]), list(outs[2:2 + n]), outs[-1]


def _rs_half_wait(sems, inflight, after, name):
    n = len(inflight)

    def body(*refs):
        o, ssem, rsem = refs[:n], refs[n], refs[n + 1]
        x, y, c = _place()
        for t in range(n):
            cp = pltpu.make_async_remote_copy(src_ref=o[t].at[c], dst_ref=o[t].at[1 - c], send_sem=ssem.at[t],
                                              recv_sem=rsem.at[t], device_id=(x, y, c), device_id_type=MESH)
            cp.wait_send()
            cp.wait_recv()

    return pl.pallas_call(
        body, name=name, out_shape=[pltpu.HBM(h.shape, h.dtype) for h in inflight],
        in_specs=[HBM] * n + [SEM, SEM, ANY], out_specs=[HBM] * n, input_output_aliases={t: t for t in range(n)},
        compiler_params=pltpu.CompilerParams(has_side_effects=EFFECT))(*inflight, sems[0], sems[1], after)


def _row_spec(tm, cols):
    return pl.BlockSpec((tm, cols), lambda i: (i, 0))


def _vec_spec(cols, rows=1):
    return pl.BlockSpec((rows, cols), lambda i: (0, 0))


def _modulated_norm(xv, g, shift, scale):
    r = lax.rsqrt(jnp.mean(xv * xv, axis=-1, keepdims=True) + EPS)
    return (((xv * r) * g) * (1.0 + scale) + shift).astype(bf16)


def _hnorm(x, g, shift, scale):
    T, tm = x.shape[0], 256

    def body(x_ref, g_ref, sh_ref, sc_ref, h_ref):
        h_ref[...] = _modulated_norm(x_ref[...], g_ref[...], sh_ref[...], sc_ref[...])

    return _pcall(body, name="hnorm", out_shape=_sds((T, D), bf16), grid=(T // tm,),
                  in_specs=[_row_spec(tm, D), _vec_spec(D), _vec_spec(D), _vec_spec(D)],
                  out_specs=_row_spec(tm, D))(x, g, shift, scale)


def _out_proj(y2, wo, x, gate, nxt=None):
    T, tm = x.shape[0], 512

    def body(y_ref, w_ref, x_ref, g_ref, *rest):
        o = jnp.dot(y_ref[0], w_ref[0], preferred_element_type=f32)
        o = o + jnp.dot(y_ref[1], w_ref[1], preferred_element_type=f32)
        xo = x_ref[...] + g_ref[...] * o
        if nxt is None:
            xo_ref, o_ref = rest
        else:
            ng_ref, nsh_ref, nsc_ref, xo_ref, o_ref, h_ref = rest
            h_ref[...] = _modulated_norm(xo, ng_ref[...], nsh_ref[...], nsc_ref[...])
        o_ref[...] = o.astype(bf16)
        xo_ref[...] = xo

    extra = [] if nxt is None else list(nxt)
    n_out = 2 if nxt is None else 3
    return _pcall(body, name="out_proj", out_shape=[_sds((T, D), f32), _sds((T, D), bf16), _sds((T, D), bf16)][:n_out],
                  grid=(T // tm,),
                  in_specs=[pl.BlockSpec((2, tm, D), lambda i: (0, i, 0)), pl.BlockSpec((2, D, D), lambda i: (0, 0, 0)),
                            _row_spec(tm, D), _vec_spec(D)] + [_vec_spec(D)] * len(extra),
                  out_specs=[_row_spec(tm, D)] * n_out, vmem_mb=40)(y2, wo, x, gate, *extra)


def _gate_bwd_tile(dx, o_ref, gate_ref, dob_ref, dgate_ref):
    dob_ref[...] = (dx * gate_ref[...]).astype(bf16)
    dgate_ref[...] += jnp.sum(dx * o_ref[...].astype(f32), axis=0, keepdims=True)


def _loss_bwd(x, target, g, o, gate):
    T, tm = x.shape[0], 512

    def body(x_ref, t_ref, g_ref, o_ref, gate_ref, dx_ref, loss_ref, dg_ref, dob_ref, dgate_ref):
        @pl.when(pl.program_id(0) == 0)
        def _():
            loss_ref[...] = jnp.zeros_like(loss_ref)
            dg_ref[...] = jnp.zeros_like(dg_ref)
            dgate_ref[...] = jnp.zeros_like(dgate_ref)

        xv, gv = x_ref[...], g_ref[...]
        r = lax.rsqrt(jnp.mean(xv * xv, axis=-1, keepdims=True) + EPS)
        xn = xv * r
        err = xn * gv - t_ref[...]
        dy = err * (1.0 / D)
        dxn = dy * gv
        dx = r * (dxn - xn * jnp.mean(dxn * xn, axis=-1, keepdims=True))
        dx_ref[...] = dx
        dg_ref[...] += jnp.sum(dy * xn, axis=0, keepdims=True)
        loss_ref[...] += (0.5 / D) * jnp.sum(jnp.sum(err * err, axis=1, keepdims=True), axis=0, keepdims=True)
        _gate_bwd_tile(dx, o_ref, gate_ref, dob_ref, dgate_ref)

    return _pcall(body, name="loss_bwd",
                  out_shape=[_sds((T, D), f32), _sds((1, 1), f32), _sds((1, D), f32), _sds((T, D), bf16), _sds((1, D), f32)],
                  grid=(T // tm,),
                  in_specs=[_row_spec(tm, D), _row_spec(tm, D), _vec_spec(D), _row_spec(tm, D), _vec_spec(D)],
                  out_specs=[_row_spec(tm, D), pl.BlockSpec((1, 1), lambda i: (0, 0)), _vec_spec(D), _row_spec(tm, D),
                             _vec_spec(D)])(x, target, g, o, gate)


def _norm_bwd(x, dh, gin, g, scale, below=None):
    T, tm = x.shape[0], 256

    def body(x_ref, dh_ref, gin_ref, g_ref, sc_ref, *rest):
        if below is None:
            dx_ref, st_ref = rest
        else:
            o_ref, gate_ref, dx_ref, st_ref, dob_ref, dgate_ref = rest

        @pl.when(pl.program_id(0) == 0)
        def _():
            st_ref[...] = jnp.zeros_like(st_ref)
            if below is not None:
                dgate_ref[...] = jnp.zeros_like(dgate_ref)

        xv, gv, dhv = x_ref[...], g_ref[...], dh_ref[...]
        r = lax.rsqrt(jnp.mean(xv * xv, axis=-1, keepdims=True) + EPS)
        xn = xv * r
        da = dhv * (1.0 + sc_ref[...])
        dxn = da * gv
        dx = gin_ref[...] + r * (dxn - xn * jnp.mean(dxn * xn, axis=-1, keepdims=True))
        dx_ref[...] = dx
        st_ref[0:1, :] += jnp.sum(dhv, axis=0, keepdims=True)
        st_ref[1:2, :] += jnp.sum(dhv * (xn * gv), axis=0, keepdims=True)
        st_ref[2:3, :] += jnp.sum(da * xn, axis=0, keepdims=True)
        if below is not None:
            _gate_bwd_tile(dx, o_ref, gate_ref, dob_ref, dgate_ref)

    out_shape = [_sds((T, D), f32), _sds((8, D), f32)]
    in_specs = [_row_spec(tm, D), _row_spec(tm, D), _row_spec(tm, D), _vec_spec(D), _vec_spec(D)]
    out_specs = [_row_spec(tm, D), _vec_spec(D, 8)]
    args = [x, dh, gin, g, scale]
    if below is not None:
        out_shape += [_sds((T, D), bf16), _sds((1, D), f32)]
        in_specs += [_row_spec(tm, D), _vec_spec(D)]
        out_specs += [_row_spec(tm, D), _vec_spec(D)]
        args += list(below)
    return _pcall(body, name="norm_bwd", out_shape=out_shape, grid=(T // tm,), in_specs=in_specs,
                  out_specs=out_specs)(*args)


STEPS = 4
ADAMW_STEPS = 8


def _cast_place(place, ws, layer, after=None):
    n = len(ws)

    def body(place_ref, *refs):
        for t in range(n):
            refs[-n + t][...] = refs[t][...].astype(bf16)

    def tile(w):
        return w.shape[1] // STEPS, w.shape[2]

    extra = [] if after is None else [after]
    return _pcall(body, name="cast_place", out_shape=[_sds((4,) + w.shape[1:], bf16) for w in ws], grid=(STEPS,),
                  prefetch=1,
                  in_specs=[pl.BlockSpec((None,) + tile(w), lambda i, pr: (layer, i, 0)) for w in ws] + [ANY] * len(extra),
                  out_specs=[pl.BlockSpec((None,) + tile(w), lambda i, pr: (pr[0], i, 0)) for w in ws])(
                      place, *ws, *extra)


def _rs_add(place, grads, theirs):
    n = len(grads)

    def body(place_ref, *refs):
        for t in range(n):
            refs[2 * n + t][...] = (refs[t][...].astype(f32) + refs[n + t][...].astype(f32)).astype(bf16)

    def tile(q):
        return q.shape[1] // 2, q.shape[2]

    mine = [pl.BlockSpec((None, None) + tile(q), lambda s, i, pr: (s, pr[1], i, 0)) for q in theirs]
    shard = [pl.BlockSpec((None,) + tile(q), lambda s, i, pr: (s, i, 0)) for q in theirs]
    return _pcall(body, name="rs_add", out_shape=[_sds(q.shape, bf16) for q in theirs], grid=(4, 2), prefetch=1,
                  in_specs=mine + shard, out_specs=shard)(place, *grads, *theirs)


def _rs_sum(place, pairs, slots):
    n, steps = len(pairs), 4

    def body(place_ref, *refs):
        for t in range(n):
            p_ref, q_ref = refs[t], refs[n + t]
            total = ((p_ref[...].astype(f32) + q_ref[0].astype(f32)) + q_ref[1].astype(f32)) + q_ref[2].astype(f32)
            refs[2 * n + t][...] = total.astype(bf16)

    def tile(q):
        return q.shape[1] // steps, q.shape[2]

    return _pcall(body, name="rs_sum", out_shape=[_sds((2,) + q.shape[1:], bf16) for q in slots], grid=(steps,),
                  prefetch=1,
                  in_specs=[pl.BlockSpec((None,) + tile(q), lambda i, pr: (pr[0], i, 0)) for q in slots]
                  + [pl.BlockSpec((3,) + tile(q), lambda i, pr: (0, i, 0)) for q in slots],
                  out_specs=[pl.BlockSpec((None,) + tile(q), lambda i, pr: (pr[1], i, 0)) for q in slots])(
                      place, *pairs, *slots)


def _adamw_math(w, g, m, v):
    m = ADAM_B1 * m + (1.0 - ADAM_B1) * g
    v = ADAM_B2 * v + (1.0 - ADAM_B2) * jnp.square(g)
    m_hat = m / (1.0 - ADAM_B1 ** ADAM_STEP)
    v_hat = v / (1.0 - ADAM_B2 ** ADAM_STEP)
    delta = -ADAM_LR * (m_hat / (jnp.sqrt(v_hat) + ADAM_EPS) + ADAM_WD * w)
    return delta, m, v


def _adamw_layer(layer, items):
    n = len(items)

    def body(*refs):
        outs = refs[-4 * n:]
        for t in range(n):
            w_ref, g_ref, m_ref, v_ref = refs[4 * t:4 * t + 4]
            g = g_ref[...].astype(f32)
            outs[4 * t][...] = g
            outs[4 * t + 1][...], outs[4 * t + 2][...], outs[4 * t + 3][...] = _adamw_math(
                w_ref[...], g, m_ref[...], v_ref[...])

    args, in_specs, out_specs, out_shape = [], [], [], []
    for w, g, m, v, _ in items:
        tr, cols = w.shape[1] // ADAMW_STEPS, w.shape[2]
        spec = pl.BlockSpec((None, tr, cols), lambda i: (layer, i, 0))
        args += [w, g, m, v]
        in_specs += [spec, pl.BlockSpec((tr, cols), lambda i: (i, 0)), spec, spec]
        out_specs += [spec] * 4
        out_shape += [_sds(w.shape, f32)] * 4
    aliases = {}
    for t, it in enumerate(items):
        if it[4] is not None:
            for k in range(4):
                aliases[len(args)] = 4 * t + k
                args.append(it[4][k])
                in_specs.append(ANY)
    res = _pcall(body, name="adamw", out_shape=out_shape, grid=(ADAMW_STEPS,), in_specs=in_specs, out_specs=out_specs,
                 aliases=aliases)(*args)
    return [tuple(res[4 * t:4 * t + 4]) for t in range(n)]


def _adamw_small(items):
    n = len(items)

    def body(*refs):
        ins, outs = refs[:4 * n], refs[4 * n:]
        for t in range(n):
            w_ref, g_ref, m_ref, v_ref = ins[4 * t:4 * t + 4]
            if len(g_ref.shape) == len(w_ref.shape) + 1:
                g = g_ref[0]
                for b in range(1, g_ref.shape[0]):
                    g = g + g_ref[b]
            else:
                g = g_ref[...]
            d, m, v = _adamw_math(w_ref[...], g, m_ref[...], v_ref[...])
            outs[4 * t][...], outs[4 * t + 1][...], outs[4 * t + 2][...], outs[4 * t + 3][...] = g, d, m, v

    out_shape = [_sds(w.shape, f32) for (w, _, _, _) in items for _ in range(4)]
    flat = [a for it in items for a in it]
    res = _pcall(body, name="adamw_small", out_shape=out_shape, in_specs=[VMEM] * (4 * n),
                 out_specs=[VMEM] * (4 * n))(*flat)
    return [tuple(res[4 * t:4 * t + 4]) for t in range(n)]


NN = ((1,), (0,))
NT = ((1,), (1,))
TN = ((0,), (0,))


def _mm(name, a, b, *, grid, a_spec, b_spec, out_shape, out_spec, dims, vmem_mb=None):
    def body(a_ref, b_ref, o_ref):
        r = lax.dot_general(a_ref[...], b_ref[...], (dims, ((), ())), preferred_element_type=f32)
        o_ref[...] = r.astype(o_ref.dtype)

    return _pcall(body, name=name, out_shape=out_shape, grid=grid, in_specs=[a_spec, b_spec], out_specs=out_spec,
                  vmem_mb=vmem_mb)(a, b)


def _whole(shape):
    return pl.BlockSpec(shape, lambda j: (0,) * len(shape))


def _split_spec(rows, tile, per_split):
    return pl.BlockSpec((None, rows, tile), lambda j: (j // per_split, 0, j % per_split))


class _Proj:
    def __init__(self, n, splits, tile):
        self.n, self.splits, self.tile = n, splits, tile
        self.steps = n // tile
        self.w_per = n // 4 // tile
        self.a_per = n // splits // tile
        assert self.w_per * tile * 4 == n and self.a_per * tile * splits == n

    def fwd(self, hb, wg):
        T = hb.shape[0]
        sub, tile, w_per = FWD_TILES, self.tile, self.w_per
        wide = sub * tile
        a_per = self.n // self.splits // wide
        assert a_per * wide * self.splits == self.n

        def w_tile(q):
            return pl.BlockSpec((None, D, tile), lambda j: ((sub * j + q) // w_per, 0, (sub * j + q) % w_per))

        def body(a_ref, *rest):
            w = jnp.concatenate([rest[q][...] for q in range(sub)], axis=1)
            rest[sub][...] = jnp.dot(a_ref[...], w, preferred_element_type=f32).astype(bf16)

        return _pcall(body, name="proj_fwd", out_shape=_sds((self.splits, T, self.n // self.splits), bf16),
                      grid=(self.n // wide,), in_specs=[_whole((T, D))] + [w_tile(q) for q in range(sub)],
                      out_specs=pl.BlockSpec((None, T, wide), lambda j: (j // a_per, 0, j % a_per)),
                      vmem_mb=40 if wide > 512 else None)(hb, *([wg] * sub))

    def dw(self, hb, dp):
        T = hb.shape[0]
        return _mm("proj_dw", hb, dp, grid=(self.steps,), a_spec=_whole((T, D)),
                   b_spec=_split_spec(T, self.tile, self.a_per), out_shape=_sds((4, D, self.n // 4), bf16),
                   out_spec=_split_spec(D, self.tile, self.w_per), dims=TN)

    def dh(self, dp, wg, after=None):
        T = dp.shape[1]
        extra = [] if after is None else [after]
        sub, tile, w_per = DH_WIDE // self.tile, self.tile, self.w_per
        a_per = self.n // self.splits // DH_WIDE
        assert sub * tile == DH_WIDE and a_per * DH_WIDE * self.splits == self.n

        def w_tile(q):
            return pl.BlockSpec((None, D, tile), lambda k: ((sub * k + q) // w_per, 0, (sub * k + q) % w_per))

        def body(a_ref, *rest):
            o_ref = rest[-1]
            w = jnp.concatenate([rest[q][...] for q in range(sub)], axis=1)
            r = lax.dot_general(a_ref[...], w, (NT, ((), ())), preferred_element_type=f32)

            @pl.when(pl.program_id(0) == 0)
            def _():
                o_ref[...] = r

            @pl.when(pl.program_id(0) > 0)
            def _():
                o_ref[...] += r

        return _pcall(body, name="proj_dh", out_shape=_sds((T, D), f32), grid=(self.n // DH_WIDE,),
                      in_specs=[pl.BlockSpec((None, T, DH_WIDE), lambda k: (k // a_per, 0, k % a_per))]
                      + [w_tile(q) for q in range(sub)] + [ANY] * len(extra),
                      out_specs=_whole((T, D)), vmem_mb=40)(dp, *([wg] * sub), *extra)


EVEN_PROJ = _Proj(7 * D, 7, 256)
ODD_PROJ = _Proj(4 * D, 2, 512)


def _out_bwd(dob, wo, y2):
    T = dob.shape[0]
    w_spec = pl.BlockSpec((None, 512, D), lambda j: (j, 0, 0))

    def body(dob_ref, w_ref, y_ref, dy_ref, dw_ref):
        dob_v = dob_ref[...]
        dy_ref[...] = lax.dot_general(dob_v, w_ref[...], (NT, ((), ())), preferred_element_type=f32).astype(bf16)
        dw_ref[...] = lax.dot_general(y_ref[...], dob_v, (TN, ((), ())), preferred_element_type=f32).astype(bf16)

    return _pcall(body, name="out_bwd", out_shape=[_sds((2, T, D), bf16), _sds((4, 512, D), bf16)], grid=(4,),
                  in_specs=[_whole((T, D)), w_spec, _split_spec(T, 512, 2)],
                  out_specs=[_split_spec(T, 512, 2), w_spec])(dob, wo, y2)


def _head_spec(lead, T):
    return pl.BlockSpec((lead, T, HEAD), lambda h: (0, 0, h))


def _head_vec(rows):
    return pl.BlockSpec((rows, HEAD), lambda h: (0, h))


_HEAD_MAT = pl.BlockSpec((None, HEAD, HEAD), lambda h: (h, 0, 0))


def _causal():
    return lax.broadcasted_iota(jnp.int32, (HEAD, HEAD), 0) >= lax.broadcasted_iota(jnp.int32, (HEAD, HEAD), 1)


def _layernorm_head(v):
    mu = jnp.mean(v, axis=-1, keepdims=True)
    d = v - mu
    rstd = lax.rsqrt(jnp.mean(d * d, axis=-1, keepdims=True) + EPS)
    return d * rstd, rstd


def _even_fwd(p7, conv_w, ln_g, ln_b, sgu_w, sgu_bias):
    T, C = p7.shape[1], CHUNK_ROWS

    def body(p_ref, cw_ref, lg_ref, lb_ref, w_ref, b_ref, y_ref):
        w0, w1, w2 = cw_ref[0:1, :], cw_ref[1:2, :], cw_ref[2:3, :]
        wm = jnp.where(_causal(), w_ref[...], 0.0).astype(bf16)
        bias, lg, lb = b_ref[...], lg_ref[...], lb_ref[...]

        def step(i, halo):
            rows = pl.ds(pl.multiple_of(i * C, C), C)
            ah, ab, ac, az, u, v, zb = (p_ref[k, rows, :].astype(f32) for k in range(7))
            tt = ac * ah
            ext = jnp.concatenate([halo, tt], axis=0)
            cv = w2 * tt + w1 * pltpu.roll(ext, 1, 0)[HALO_CONV:] + w0 * pltpu.roll(ext, 2, 0)[HALO_CONV:]
            y_ref[0, rows, :] = (ab * cv * _silu(az)).astype(bf16)
            vhat, _ = _layernorm_head(v)
            vn = (vhat * lg + lb).astype(bf16)
            mix = jnp.concatenate([jnp.dot(wm, vn[k * HEAD:(k + 1) * HEAD], preferred_element_type=f32) + bias
                                   for k in range(C // HEAD)], axis=0)
            y_ref[1, rows, :] = (u * mix * _silu(zb)).astype(bf16)
            return tt[C - HALO_CONV:]

        lax.fori_loop(0, T // C, step, jnp.zeros((HALO_CONV, HEAD), f32))

    return _pcall(body, name="even_fwd", out_shape=_sds((2, T, D), bf16), grid=(NH,),
                  in_specs=[_head_spec(7, T), _head_vec(3), _head_vec(1), _head_vec(1), _HEAD_MAT, _HEAD_MAT],
                  out_specs=_head_spec(2, T))(p7, conv_w, ln_g, ln_b, sgu_w, sgu_bias)


def _even_bwd(p7, dy2, conv_w, ln_g, ln_b, sgu_w, sgu_bias):
    T, C = p7.shape[1], CHUNK_ROWS
    n_chunks = T // C

    def body(p_ref, dy_ref, cw_ref, lg_ref, lb_ref, w_ref, b_ref,
             dp_ref, dcw_ref, dlg_ref, dlb_ref, dw_ref, dms_ref, dcv_s):
        w0, w1, w2 = cw_ref[0:1, :], cw_ref[1:2, :], cw_ref[2:3, :]
        tri = _causal()
        wm = jnp.where(tri, w_ref[...], 0.0).astype(bf16)
        bias, lg, lb = b_ref[...], lg_ref[...], lb_ref[...]
        dw_ref[...] = jnp.zeros_like(dw_ref)
        dms_ref[...] = jnp.zeros_like(dms_ref)

        def fwd_step(i, carry):
            halo, a0, a1, a2, alg, alb = carry
            rows = pl.ds(pl.multiple_of(i * C, C), C)
            ah, ab, ac, az = (p_ref[k, rows, :].astype(f32) for k in range(4))
            dya = dy_ref[0, rows, :].astype(f32)
            tt = ac * ah
            ext = jnp.concatenate([halo, tt], axis=0)
            t1, t2 = pltpu.roll(ext, 1, 0)[HALO_CONV:], pltpu.roll(ext, 2, 0)[HALO_CONV:]
            cv = w2 * tt + w1 * t1 + w0 * t2
            sa, dsa = _silu_and_grad(az)
            g1 = dya * sa
            dp_ref[1, rows, :] = (g1 * cv).astype(bf16)
            dp_ref[3, rows, :] = (dya * ab * cv * dsa).astype(bf16)
            dcv = g1 * ab
            dcv_s[rows, :] = dcv
            a2 = a2 + jnp.sum(dcv * tt, axis=0, keepdims=True)
            a1 = a1 + jnp.sum(dcv * t1, axis=0, keepdims=True)
            a0 = a0 + jnp.sum(dcv * t2, axis=0, keepdims=True)

            u, zb, dyb = p_ref[4, rows, :].astype(f32), p_ref[6, rows, :].astype(f32), dy_ref[1, rows, :].astype(f32)
            vhat, rstd = _layernorm_head(p_ref[5, rows, :].astype(f32))
            vn = (vhat * lg + lb).astype(bf16)
            sb, dsb = _silu_and_grad(zb)
            mix = jnp.concatenate([jnp.dot(wm, vn[k * HEAD:(k + 1) * HEAD], preferred_element_type=f32) + bias
                                   for k in range(C // HEAD)], axis=0)
            dp_ref[4, rows, :] = (dyb * mix * sb).astype(bf16)
            dp_ref[6, rows, :] = (dyb * u * mix * dsb).astype(bf16)
            dmix = dyb * u * sb
            dvn_parts = []
            for k in range(C // HEAD):
                dm = dmix[k * HEAD:(k + 1) * HEAD]
                dmb = dm.astype(bf16)
                dvn_parts.append(lax.dot_general(wm, dmb, (TN, ((), ())), preferred_element_type=f32))
                dw_ref[...] += lax.dot_general(dmb, vn[k * HEAD:(k + 1) * HEAD], (NT, ((), ())),
                                               preferred_element_type=f32)
                dms_ref[...] += dm
            dvn = jnp.concatenate(dvn_parts, axis=0)
            alg = alg + jnp.sum(dvn * vhat, axis=0, keepdims=True)
            alb = alb + jnp.sum(dvn, axis=0, keepdims=True)
            dvh = dvn * lg
            dv = rstd * (dvh - jnp.mean(dvh, axis=-1, keepdims=True)
                         - vhat * jnp.mean(dvh * vhat, axis=-1, keepdims=True))
            dp_ref[5, rows, :] = dv.astype(bf16)
            return tt[C - HALO_CONV:], a0, a1, a2, alg, alb

        zrow = jnp.zeros((1, HEAD), f32)
        _, a0, a1, a2, alg, alb = lax.fori_loop(
            0, n_chunks, fwd_step, (jnp.zeros((HALO_CONV, HEAD), f32), zrow, zrow, zrow, zrow, zrow))
        dcw_ref[0:1, :], dcw_ref[1:2, :], dcw_ref[2:3, :] = a0, a1, a2
        dlg_ref[...], dlb_ref[...] = alg, alb
        dw_ref[...] = jnp.where(tri, dw_ref[...], 0.0)

        def bwd_step(k, halo):
            rows = pl.ds(pl.multiple_of((n_chunks - 1 - k) * C, C), C)
            dcv = dcv_s[rows, :]
            ext = jnp.concatenate([dcv, halo], axis=0)
            n1 = pltpu.roll(ext, C + HALO_CONV - 1, 0)[:C]
            n2 = pltpu.roll(ext, C + HALO_CONV - 2, 0)[:C]
            dtt = w2 * dcv + w1 * n1 + w0 * n2
            dp_ref[2, rows, :] = (dtt * p_ref[0, rows, :].astype(f32)).astype(bf16)
            dp_ref[0, rows, :] = (dtt * p_ref[2, rows, :].astype(f32)).astype(bf16)
            return dcv[:HALO_CONV]

        lax.fori_loop(0, n_chunks, bwd_step, jnp.zeros((HALO_CONV, HEAD), f32))

    out_shape = [_sds((7, T, D), bf16), _sds((3, D), f32), _sds((1, D), f32), _sds((1, D), f32),
                 _sds((NH, HEAD, HEAD), f32), _sds((NH, HEAD, HEAD), f32)]
    return _pcall(body, name="even_bwd", out_shape=out_shape, grid=(NH,),
                  in_specs=[_head_spec(7, T), _head_spec(2, T), _head_vec(3), _head_vec(1), _head_vec(1),
                            _HEAD_MAT, _HEAD_MAT],
                  out_specs=[_head_spec(7, T), _head_vec(3), _head_vec(1), _head_vec(1), _HEAD_MAT, _HEAD_MAT],
                  scratch=[pltpu.VMEM((T, HEAD), f32)])(p7, dy2, conv_w, ln_g, ln_b, sgu_w, sgu_bias)


def _window_sum(ext, win, towards_past):
    n, k, s = ext.shape[0], 1, ext
    while k < win:
        s = s + pltpu.roll(s, k if towards_past else n - k, 0)
        k *= 2
    return s


def _pool_count(i, C, win):
    t = i * C + lax.broadcasted_iota(jnp.int32, (C, 1), 0)
    cnt = jnp.minimum(t + 1, win).astype(f32)
    return cnt, 1.0 / cnt


def _group_specs(T):
    p_spec = pl.BlockSpec((None, T, GC), lambda g: (0, 0, g))
    z_spec = pl.BlockSpec((None, T, GC), lambda g: (1, 0, g))
    pw_spec = pl.BlockSpec((4, GC // 4, GC), lambda g: (0, g, 0))
    ps_spec = pl.BlockSpec((1, GC), lambda g: (0, g))
    y_spec = pl.BlockSpec((None, T, GC), lambda g: (g // 2, 0, g % 2))
    return p_spec, z_spec, pw_spec, ps_spec, y_spec


def _odd_fwd(p2, pool_wg, pool_scale):
    T, C = p2.shape[1], CHUNK_ROWS
    p_spec, z_spec, pw_spec, ps_spec, y_spec = _group_specs(T)

    def body(p_ref, z_ref, pw_ref, ps_ref, y_ref):
        pw, ps = pw_ref[...].reshape(GC, GC), ps_ref[...]

        def run(win):
            def step(i, halo):
                rows = pl.ds(pl.multiple_of(i * C, C), C)
                p = p_ref[rows, :].astype(f32)
                s = _window_sum(jnp.concatenate([halo, p], axis=0), win, True)[HALO_POOL:]
                pooled = s * _pool_count(i, C, win)[1] - p
                ypre = jnp.dot(pooled.astype(bf16), pw, preferred_element_type=f32)
                y_ref[rows, :] = (ypre * ps * _silu(z_ref[rows, :].astype(f32))).astype(bf16)
                return p[C - HALO_POOL:]

            lax.fori_loop(0, T // C, step, jnp.zeros((HALO_POOL, GC), f32))

        for gi, win in enumerate(WINDOWS):
            pl.when(pl.program_id(0) == gi)(functools.partial(run, win))

    return _pcall(body, name="odd_fwd", out_shape=_sds((2, T, D), bf16), grid=(len(WINDOWS),),
                  in_specs=[p_spec, z_spec, pw_spec, ps_spec], out_specs=y_spec)(p2, p2, pool_wg, pool_scale)


def _odd_bwd(p2, dy2, pool_wg, pool_scale):
    T, C = p2.shape[1], CHUNK_ROWS
    n_chunks = T // C
    p_spec, z_spec, pw_spec, ps_spec, y_spec = _group_specs(T)

    def body(p_ref, z_ref, dy_ref, pw_ref, ps_ref, dp_ref, dpw_ref, dps_ref, q_s, acc_s):
        pw, ps = pw_ref[...].reshape(GC, GC), ps_ref[...]

        def run(win):
            acc_s[...] = jnp.zeros_like(acc_s)

            def fwd_step(i, carry):
                halo, aps = carry
                rows = pl.ds(pl.multiple_of(i * C, C), C)
                p, z, dy = p_ref[rows, :].astype(f32), z_ref[rows, :].astype(f32), dy_ref[rows, :].astype(f32)
                _, inv_cnt = _pool_count(i, C, win)
                s = _window_sum(jnp.concatenate([halo, p], axis=0), win, True)[HALO_POOL:]
                pb = (s * inv_cnt - p).astype(bf16)
                ypre = jnp.dot(pb, pw, preferred_element_type=f32)
                sz, dsz = _silu_and_grad(z)
                aps = aps + jnp.sum(dy * ypre * sz, axis=0, keepdims=True)
                dp_ref[1, rows, :] = (dy * ypre * ps * dsz).astype(bf16)
                dyp = (dy * ps * sz).astype(bf16)
                acc_s[...] += lax.dot_general(pb, dyp, (TN, ((), ())), preferred_element_type=f32)
                dpool = lax.dot_general(dyp, pw, (NT, ((), ())), preferred_element_type=f32)
                q_s[rows, :] = dpool * inv_cnt
                return p[C - HALO_POOL:], aps

            _, aps = lax.fori_loop(0, n_chunks, fwd_step, (jnp.zeros((HALO_POOL, GC), f32), jnp.zeros((1, GC), f32)))
            dps_ref[...] = aps
            dpw_ref[...] = acc_s[...].reshape(4, GC // 4, GC).astype(bf16)

            def bwd_step(k, halo):
                i = n_chunks - 1 - k
                rows = pl.ds(pl.multiple_of(i * C, C), C)
                q = q_s[rows, :]
                s = _window_sum(jnp.concatenate([q, halo], axis=0), win, False)[:C]
                dp_ref[0, rows, :] = (s - q * _pool_count(i, C, win)[0]).astype(bf16)
                return q[:HALO_POOL]

            lax.fori_loop(0, n_chunks, bwd_step, jnp.zeros((HALO_POOL, GC), f32))

        for gi, win in enumerate(WINDOWS):
            pl.when(pl.program_id(0) == gi)(functools.partial(run, win))

    out_shape = [_sds((2, T, 2 * D), bf16), _sds((4, GC, GC), bf16), _sds((1, 2 * D), f32)]
    return _pcall(body, name="odd_bwd", out_shape=out_shape, grid=(len(WINDOWS),),
                  in_specs=[p_spec, z_spec, y_spec, pw_spec, ps_spec],
                  out_specs=[pl.BlockSpec((2, T, GC), lambda g: (0, 0, g)), pw_spec, ps_spec],
                  scratch=[pltpu.VMEM((T, GC), f32), pltpu.VMEM((GC, GC), f32)], vmem_mb=44)(
                      p2, p2, dy2, pool_wg, pool_scale)


def _ada_fwd(c_all, ada_w):
    cols = ada_w.shape[2]

    def body(c_ref, w_ref, o_ref):
        o_ref[...] = jnp.dot(_silu(c_ref[...]), w_ref[...], preferred_element_type=f32,
                             precision=lax.Precision.HIGHEST)

    return _pcall(body, name="ada_fwd", out_shape=_sds((4, N_DEV, cols), f32), grid=(4,),
                  in_specs=[pl.BlockSpec((N_DEV, D), lambda i: (0, 0)), pl.BlockSpec((None, D, cols), lambda i: (i, 0, 0))],
                  out_specs=pl.BlockSpec((None, N_DEV, cols), lambda i: (i, 0, 0)))(c_all, ada_w)


def _ada_bwd(c_all_t, dmod, w, m, v):
    cols, tr = w.shape[2], 256
    spec = pl.BlockSpec((None, tr, cols), lambda l, i: (l, i, 0))

    def body(c_ref, dm_ref, w_ref, m_ref, v_ref, g_ref, d_ref, mo_ref, vo_ref):
        sc = _silu(c_ref[...])
        g = sc[:, 0:1] * dm_ref[0:1, :]
        for b in range(1, N_DEV):
            g = g + sc[:, b:b + 1] * dm_ref[b:b + 1, :]
        g_ref[...] = g
        d_ref[...], mo_ref[...], vo_ref[...] = _adamw_math(w_ref[...], g, m_ref[...], v_ref[...])

    return _pcall(body, name="ada_bwd", out_shape=[_sds(w.shape, f32)] * 4, grid=(4, D // tr),
                  in_specs=[pl.BlockSpec((tr, N_DEV), lambda l, i: (i, 0)),
                            pl.BlockSpec((None, N_DEV, cols), lambda l, i: (l, 0, 0)), spec, spec, spec],
                  out_specs=[spec] * 4)(c_all_t, dmod, w, m, v)


def _layer_fwd(even, x, hb, gate, w, nxt, before_out=None):
    if even:
        w_in, w_out, conv_w, ln_g, ln_b, sgu_w, sgu_b = w
        bias = jnp.broadcast_to(sgu_b[:, :, None], (NH, HEAD, HEAD))
        p = EVEN_PROJ.fwd(hb, w_in)
        y2 = _even_fwd(p, conv_w, ln_g, ln_b, sgu_w, bias)
    else:
        w_in, pool_w, w_out, pool_scale = w
        p = ODD_PROJ.fwd(hb, w_in)
        y2 = _odd_fwd(p, pool_w, pool_scale)
    if before_out is not None:
        late_w_out, tok = before_out(y2)
        if late_w_out is not None:
            w_out = late_w_out
            w = (w_in, w_out) + tuple(w[2:]) if even else (w_in, pool_w, w_out, pool_scale)
        if tok is not None:
            gate = gate + tok[0:1, 0:1]
    outs = _out_proj(y2, w_out.reshape(2, D, D), x, gate, nxt)
    return outs[0], (None if nxt is None else outs[2]), (x, hb, p, y2, outs[1]), w


def _layer_bwd(even, gin, dob, dgate, saved, scale, g, w, below=None, send=None):
    x_in, hb, p, y2, o = saved
    if even:
        w_in, w_out, conv_w, ln_g, ln_b, sgu_w, sgu_b = w
        bias = jnp.broadcast_to(sgu_b[:, :, None], (NH, HEAD, HEAD))
        dy2, dwo = _out_bwd(dob, w_out, y2)
        dp, dconv, dlg, dlb, dsw, dms = _even_bwd(p, dy2, conv_w, ln_g, ln_b, sgu_w, bias)
        proj = EVEN_PROJ
        small = dict(conv_w=dconv, ln_g=dlg, ln_b=dlb, sgu_w=dsw, sgu_b=jnp.sum(dms, axis=-1))
        big = [proj.dw(hb, dp), dwo]
    else:
        w_in, pool_w, w_out, pool_scale = w
        dy2, dwo = _out_bwd(dob, w_out, y2)
        dp, dpw, dps = _odd_bwd(p, dy2, pool_w, pool_scale)
        proj = ODD_PROJ
        small = dict(pool_scale=dps)
        big = [proj.dw(hb, dp), dpw, dwo]
    tok = None
    if send is not None:
        big, tok = send(big)
    dh = proj.dh(dp, w_in, tok)
    res = _norm_bwd(x_in, dh, gin, g, scale, below)
    stats = res[1]
    return (res[0], (None if below is None else (res[2], res[3])), big, small,
            jnp.concatenate([stats[0:2], dgate], axis=0), stats[2:3])


def _pack_rows(parts):
    rows = [p.reshape(-1, LANES) for p in parts]
    total = sum(r.shape[0] for r in rows)
    padded = -(-total // (8 * N_DEV)) * (8 * N_DEV)
    if padded > total:
        rows.append(jnp.zeros((padded - total, LANES), f32))
    return jnp.concatenate(rows, axis=0)


def _unpack_rows(buf, shapes):
    out, r = [], 0
    for shp in shapes:
        n = 1
        for d in shp:
            n *= d
        out.append(buf[r:r + n // LANES].reshape(shp))
        r += n // LANES
    return out


def kernel(x, c, norm_g, ada_w, ada_b, ab_w_in, ab_conv_w, ab_ln_g, ab_ln_b, ab_sgu_w, ab_sgu_b, ab_w_out, c_w_in, c_pool_w, c_pool_scale, c_w_out, final_g, loss_target, m_norm_g, m_ada_w, m_ada_b, m_ab_w_in, m_ab_conv_w, m_ab_ln_g, m_ab_ln_b, m_ab_sgu_w, m_ab_sgu_b, m_ab_w_out, m_c_w_in, m_c_pool_w, m_c_pool_scale, m_c_w_out, m_final_g, v_norm_g, v_ada_w, v_ada_b, v_ab_w_in, v_ab_conv_w, v_ab_ln_g, v_ab_ln_b, v_ab_sgu_w, v_ab_sgu_b, v_ab_w_out, v_c_w_in, v_c_pool_w, v_c_pool_scale, v_c_w_out, v_final_g):
    ix, iy, ic = _place()
    chip, dev = 2 * ix + iy, 4 * ix + 2 * iy + ic
    n_even, n_odd = ab_w_in.shape[0], c_w_in.shape[0]
    depth = n_even + n_odd
    acols = ada_w.shape[2]

    place = jnp.stack([chip, ic]).astype(jnp.int32)
    even_names, odd_names = ["ab_w_in", "ab_w_out"], ["c_w_in", "c_pool_w", "c_w_out"]
    params = {"ab_w_in": (ab_w_in, m_ab_w_in, v_ab_w_in), "ab_w_out": (ab_w_out, m_ab_w_out, v_ab_w_out),
              "c_w_in": (c_w_in, m_c_w_in, v_c_w_in), "c_w_out": (c_w_out, m_c_w_out, v_c_w_out),
              "c_pool_w": tuple(a.reshape(n_odd, GC, GC) for a in (c_pool_w, m_c_pool_w, v_c_pool_w))}

    def placed(names, layer, after=None):
        ws = [params[nm][0] for nm in names]
        return [p.reshape(4, 2, p.shape[1] // 2, p.shape[2]) for p in _cast_place(place, ws, layer, after)]

    def whole(arrays):
        return [g.reshape(4, 2 * g.shape[2], g.shape[3]) for g in arrays]

    first = _gather8(jnp.concatenate([c, ab_conv_w.reshape(1, -1), c_pool_scale.reshape(1, -1)], axis=1), "gather_c")
    c_all, small_all = first[:, 0, :D], first[0::2, 0, D:]
    sems_a, in_a, tok = _ag_start([placed(even_names[:1], 0)], first[0:1, 0, 0:LANES], "ag_start_0a")
    modp = _ada_fwd(c_all, ada_w)
    later = [placed(even_names[1:], 0, tok)]
    later += [placed(even_names if i % 2 == 0 else odd_names, i // 2, tok) for i in range(1, depth)]
    modg = _gather8(modp + tok[0:1, 0:1], "gather_mod", [lay[-1] for lay in later])
    mod_rows = lax.dynamic_index_in_dim(modg[0::2], dev, axis=2, keepdims=False)
    mod = jnp.transpose(mod_rows, (1, 0, 2)).reshape(depth, 3 * D) + ada_b
    mods = [(mod[i:i + 1, 0:D], mod[i:i + 1, D:2 * D], mod[i:i + 1, 2 * D:3 * D]) for i in range(depth)]

    def shard_cols(a, width):
        return lax.dynamic_slice_in_dim(a, chip * width, width, axis=a.ndim - 1)

    n_conv = ab_conv_w.size
    conv_all = small_all[:, :n_conv].reshape(4, n_even, 3, D // 4)
    conv_full = jnp.transpose(conv_all, (1, 2, 0, 3)).reshape(n_even, 3, D)
    scale_all = small_all[:, n_conv:].reshape(4, n_odd, 2 * D // 4)
    scale_full = jnp.transpose(scale_all, (1, 0, 2)).reshape(n_odd, 2 * D)

    gathers_done = mod[0:1, 0:LANES] + scale_full[0:1, 0:LANES]
    sems_b, in_b, tok = _ag_start(later[:1], gathers_done, "ag_start_0b")
    sems_r, in_r, tok = _ag_start(later[1:], tok, "ag_start_rest")

    x_cur, saved, weights, handoff = x[0], [], [], {}
    sems_f, in_f, tok = _agf_start(_ag_wait(in_a[0], sems_a[0], tok, "ag_wait_0a"), "agf_start_0")
    hb = _hnorm(x_cur, norm_g[0:1], mods[0][0] + tok[0:1, 0:1], mods[0][1])
    for i in range(depth):
        j = i // 2
        if i == 0:
            full = whole(_agf_wait(sems_f, in_f, hb, "agf_wait_0")) + [None]
        else:
            full = whole(_agf_wait(*handoff.pop(i), x_cur, f"agf_wait_{i}"))
        if i % 2 == 0:
            w = (full[0], full[1], conv_full[j], ab_ln_g[j:j + 1], ab_ln_b[j:j + 1], ab_sgu_w[j], ab_sgu_b[j])
        else:
            w = (full[0], full[1], full[2], scale_full[j:j + 1])

        def before_out(y2, i=i):
            w_out, tok = None, None
            if i == 0:
                w_out = whole(_ag_forward(_ag_wait(in_b[0], sems_b[0], y2, "ag_wait_0b"), "ag_forward"))[0]
            if i + 1 < depth:
                arrived = _ag_wait(in_r[i], sems_r[i], y2, f"ag_wait_{i + 1}")
                sems_f, inflight, tok = _agf_start(arrived, f"agf_start_{i + 1}")
                handoff[i + 1] = (sems_f, inflight)
            return w_out, tok

        nxt = (norm_g[i + 1:i + 2], mods[i + 1][0], mods[i + 1][1]) if i + 1 < depth else None
        x_cur, hb, sv, w = _layer_fwd(i % 2 == 0, x_cur, hb, mods[i][2], w, nxt, before_out)
        weights.append(w)
        saved.append(sv)
    gin, loss, dfinal_g, dob, dgate = _loss_bwd(x_cur, loss_target[0], final_g.reshape(1, D), saved[-1][4],
                                                mods[-1][2])

    stacked = {}

    def reduce_layer(i, sems, pairs, lands, after):
        pairs, slots = _rs_chip_wait(sems, pairs, lands, after, f"rs_chip_wait_{i}")
        half_sems, halves, _ = _rs_half_start(_rs_sum(place, pairs, slots), f"rs_half_start_{i}")
        return i, half_sems, halves

    def update_layer(i, half_sems, halves, after):
        names = even_names if i % 2 == 0 else odd_names
        grads = _rs_half_wait(half_sems, halves, after, f"rs_half_wait_{i}")
        items = [(params[nm][0], g.reshape(params[nm][0].shape[1:]), params[nm][1], params[nm][2], stacked.get(nm))
                 for nm, g in zip(names, grads)]
        for nm, res in zip(names, _adamw_layer(i // 2, items)):
            stacked[nm] = res
            updated.append(res[1])

    updated = []
    small_g, dmod, dnorm_g, pending, tok = [None] * depth, [None] * depth, [None] * depth, None, None
    exchanging = []
    for i in reversed(range(depth)):
        w = weights[i]
        if tok is not None:
            w = w[:2] + (w[2] + tok[0:1, 0:1],) + w[3:] if i % 2 == 0 else w[:3] + (w[3] + tok[0:1, 0:1],)
        below = (saved[i - 1][4], mods[i - 1][2]) if i > 0 else None

        def send(big_g, i=i):
            if exchanging:
                update_layer(*exchanging.pop(), big_g[0])
            big_g = [g.reshape(4, 2, g.shape[1] // 2, g.shape[2]) for g in big_g]
            sems, big_g, lands, tok = _rs_pair_start(big_g, f"rs_pair_start_{i}")
            return (sems, big_g, lands), tok

        gin, gate_bwd, sent, small_g[i], dmod[i], dnorm_g[i] = _layer_bwd(
            i % 2 == 0, gin, dob, dgate, saved[i], mods[i][1], norm_g[i:i + 1], w, below, send)
        if below is not None:
            dob, dgate = gate_bwd
        after = gin
        if i == 0:
            dmod_all = _gather8(jnp.stack(dmod).reshape(depth * 3 * D // LANES, LANES), "gather_dmod")
            after = dmod_all = dmod_all.reshape(N_DEV, depth, 3 * D)
        if i > 0:
            after, updated = [after] + updated, []
        else:
            after = [after]
        big_g, theirs = _rs_pair_wait(*sent, after, f"rs_pair_wait_{i}")
        pairs = _rs_add(place, big_g, theirs)
        sems, pairs, lands, tok = _rs_chip_start(pairs, f"rs_chip_start_{i}")
        if pending is not None:
            exchanging.append(reduce_layer(*pending, [tok]))
        pending = (i, sems, pairs, lands)
    grad_x = gin
    dnorm_g = jnp.concatenate(dnorm_g, axis=0)

    dmod_cols = jnp.transpose(shard_cols(dmod_all, acols), (1, 0, 2))
    r_ada_w = _ada_bwd(c_all.T, dmod_cols, ada_w, m_ada_w, v_ada_w)
    update_layer(*exchanging.pop(), r_ada_w[1])
    last = reduce_layer(*pending, [r_ada_w[1]] + updated)

    small_parts = [dnorm_g, dfinal_g,
                   jnp.stack([small_g[2 * j]["conv_w"] for j in range(n_even)]),
                   jnp.concatenate([small_g[2 * j]["ln_g"] for j in range(n_even)], axis=0),
                   jnp.concatenate([small_g[2 * j]["ln_b"] for j in range(n_even)], axis=0),
                   jnp.stack([small_g[2 * j]["sgu_b"] for j in range(n_even)]),
                   jnp.concatenate([small_g[2 * j + 1]["pool_scale"] for j in range(n_odd)], axis=0),
                   jnp.pad(loss, ((0, 7), (0, LANES - 1)))]
    small_shapes = [p.shape for p in small_parts]
    sgu_parts = [small_g[2 * j]["sgu_w"].reshape(NH * HEAD, HEAD) for j in range(n_even)]
    reduced = _allreduce8([_pack_rows(small_parts)] + sgu_parts, "allreduce_small", last[2][0])
    update_layer(*last, reduced[0])
    r_ab_w_in, r_ab_w_out, r_c_w_in, r_c_w_out = (stacked[nm] for nm in ("ab_w_in", "ab_w_out", "c_w_in", "c_w_out"))
    r_c_pool_w = tuple(a.reshape(c_pool_w.shape) for a in stacked["c_pool_w"])
    g_norm_g, g_final_g, g_conv_full, g_ln_g, g_ln_b, g_sgu_b, g_scale_full, loss_row = _unpack_rows(reduced[0],
                                                                                                     small_shapes)
    g_sgu_w = jnp.stack(reduced[1:])
    loss = loss_row[0, 0]
    g_conv = shard_cols(g_conv_full, D // 4)
    g_scale = shard_cols(g_scale_full, 2 * D // 4)

    def two_d(a):
        return a.reshape(-1, a.shape[-1])

    small = [(norm_g, g_norm_g, m_norm_g, v_norm_g),
             (ada_b, dmod_all, m_ada_b, v_ada_b),
             (two_d(ab_conv_w), two_d(g_conv), two_d(m_ab_conv_w), two_d(v_ab_conv_w)),
             (ab_ln_g, g_ln_g, m_ab_ln_g, v_ab_ln_g),
             (ab_ln_b, g_ln_b, m_ab_ln_b, v_ab_ln_b),
             (two_d(ab_sgu_w), two_d(g_sgu_w), two_d(m_ab_sgu_w), two_d(v_ab_sgu_w)),
             (two_d(ab_sgu_b), two_d(g_sgu_b), two_d(m_ab_sgu_b), two_d(v_ab_sgu_b)),
             (c_pool_scale, g_scale, m_c_pool_scale, v_c_pool_scale),
             (final_g.reshape(1, D), g_final_g, m_final_g.reshape(1, D), v_final_g.reshape(1, D))]
    small_res = _adamw_small(small)
    small_shapes_out = [norm_g.shape, ada_b.shape, ab_conv_w.shape, ab_ln_g.shape, ab_ln_b.shape, ab_sgu_w.shape,
                        ab_sgu_b.shape, c_pool_scale.shape, final_g.shape]
    (r_norm_g, r_ada_b, r_conv, r_ln_g, r_ln_b, r_sgu_w, r_sgu_b, r_scale, r_final_g) = [
        tuple(a.reshape(shp) for a in res) for res, shp in zip(small_res, small_shapes_out)]

    order = [r_norm_g, r_ada_w, r_ada_b, r_ab_w_in, r_conv, r_ln_g, r_ln_b, r_sgu_w, r_sgu_b, r_ab_w_out,
             r_c_w_in, r_c_pool_w, r_scale, r_c_w_out, r_final_g]
    outs = [loss, grad_x[None]]
    for field in range(4):
        outs += [r[field] for r in order]
    return tuple(outs)
```

```python
import functools

import jax
import jax.numpy as jnp
from jax import lax
from jax.experimental import pallas as pl
from jax.experimental.pallas import tpu as pltpu

f32, bf16 = jnp.float32, jnp.bfloat16

D = 1024
HEAD = 128
NH = 8
WINDOWS = (2, 4, 8, 16)
GC = 512
EPS = 1e-6
HALO_CONV = 8
HALO_POOL = 16
CHUNK_ROWS = 512
DH_WIDE = 1024
FWD_TILES = 2
DH_VMEM_MB = 56
N_DEV = 8
LANES = 128

ADAM_LR, ADAM_B1, ADAM_B2, ADAM_EPS, ADAM_WD, ADAM_STEP = 0.001, 0.9, 0.999, 1e-08, 0.01, 10

MESH = pl.DeviceIdType.MESH
ANY = pl.BlockSpec(memory_space=pl.ANY)
VMEM = pl.BlockSpec(memory_space=pltpu.VMEM)
MIB = 2 ** 20


def _pcall(body, *, name, out_shape, grid=None, in_specs=None, out_specs=None, scratch=(), vmem_mb=None,
           aliases=None, prefetch=0):
    kw = {}
    if prefetch:
        kw["grid_spec"] = pltpu.PrefetchScalarGridSpec(num_scalar_prefetch=prefetch, grid=grid, in_specs=in_specs,
                                                       out_specs=out_specs, scratch_shapes=list(scratch))
    else:
        if grid is not None:
            kw["grid"] = grid
        if in_specs is not None:
            kw["in_specs"] = in_specs
        if out_specs is not None:
            kw["out_specs"] = out_specs
        if scratch:
            kw["scratch_shapes"] = list(scratch)
    if aliases:
        kw["input_output_aliases"] = aliases
    params = pltpu.CompilerParams(vmem_limit_bytes=None if vmem_mb is None else vmem_mb * MIB)
    return pl.pallas_call(body, name=name, out_shape=out_shape, compiler_params=params, **kw)


def _sds(shape, dtype):
    return jax.ShapeDtypeStruct(tuple(shape), dtype)


def _sigmoid(z):
    return pl.reciprocal(1.0 + jnp.exp(-z), approx=True)


def _silu(z):
    return z * _sigmoid(z)


def _silu_and_grad(z):
    s = _sigmoid(z)
    return z * s, s * (1.0 + z * (1.0 - s))


def _place():
    return lax.axis_index("x"), lax.axis_index("y"), lax.axis_index("c")


def _gather8(blk, name, after=()):
    def body(x_ref, *rest):
        o_ref, ssem, rsem = rest[len(after):]
        x, y, c = _place()
        me = 4 * x + 2 * y + c
        o_ref[me] = x_ref[...]
        sends = []
        for k in range(1, N_DEV):
            px = 1 - x if k & 4 else x
            py = 1 - y if k & 2 else y
            pc = 1 - c if k & 1 else c
            cp = pltpu.make_async_remote_copy(src_ref=x_ref, dst_ref=o_ref.at[me], send_sem=ssem.at[k - 1],
                                              recv_sem=rsem.at[k - 1], device_id=(px, py, pc), device_id_type=MESH)
            cp.start()
            sends.append((cp, 4 * px + 2 * py + pc))
        for k, (cp, peer) in enumerate(sends):
            pltpu.make_async_remote_copy(src_ref=x_ref, dst_ref=o_ref.at[peer], send_sem=ssem.at[k],
                                         recv_sem=rsem.at[k], device_id=(x, y, c), device_id_type=MESH).wait_recv()
        for cp, _ in sends:
            cp.wait_send()

    return _pcall(body, name=name, out_shape=_sds((N_DEV,) + blk.shape, blk.dtype), in_specs=[VMEM] + [ANY] * len(after),
                  out_specs=VMEM,
                  scratch=[pltpu.SemaphoreType.DMA((N_DEV - 1,)), pltpu.SemaphoreType.DMA((N_DEV - 1,))])(blk, *after)


def _allreduce8(bufs, name, after=None):
    n, n_after = len(bufs), 0 if after is None else 1
    rbs = [b.shape[0] // N_DEV for b in bufs]
    assert all(rb * N_DEV == b.shape[0] and rb % 8 == 0 for rb, b in zip(rbs, bufs))

    def body(*refs):
        refs = refs[:n] + refs[n + n_after:]
        xs, outs, stages = refs[:n], refs[n:2 * n], refs[2 * n:3 * n]
        ssem, rsem = refs[3 * n:]
        x, y, c = _place()
        me = 4 * x + 2 * y + c
        peers = []
        for k in range(1, N_DEV):
            px = 1 - x if k & 4 else x
            py = 1 - y if k & 2 else y
            pc = 1 - c if k & 1 else c
            peers.append(((px, py, pc), 4 * px + 2 * py + pc))

        def blk(t, ref, idx):
            return ref.at[pl.ds(pl.multiple_of(idx * rbs[t], 8), rbs[t]), :]

        def copy(t, phase, k, src, dst, dev):
            return pltpu.make_async_remote_copy(src_ref=src, dst_ref=dst, send_sem=ssem.at[t, phase, k],
                                                recv_sem=rsem.at[t, phase, k], device_id=dev, device_id_type=MESH)

        scatter = [copy(t, 0, k, blk(t, xs[t], pidx), stages[t].at[me], dev)
                   for t in range(n) for k, (dev, pidx) in enumerate(peers)]
        for cp in scatter:
            cp.start()
        gather = []
        for t in range(n):
            stages[t][me] = blk(t, xs[t], me)[...]
            for k, (dev, pidx) in enumerate(peers):
                copy(t, 0, k, blk(t, xs[t], pidx), stages[t].at[pidx], dev).wait_recv()
            total = stages[t][0]
            for j in range(1, N_DEV):
                total = total + stages[t][j]
            blk(t, outs[t], me)[...] = total
            sends = [copy(t, 1, k, blk(t, outs[t], me), blk(t, outs[t], me), dev) for k, (dev, pidx) in enumerate(peers)]
            for cp in sends:
                cp.start()
            gather += sends
        for t in range(n):
            for k, (dev, pidx) in enumerate(peers):
                copy(t, 1, k, blk(t, outs[t], pidx), blk(t, outs[t], pidx), dev).wait_recv()
        for cp in scatter + gather:
            cp.wait_send()

    return _pcall(body, name=name, out_shape=[_sds(b.shape, f32) for b in bufs], in_specs=[VMEM] * n + [ANY] * n_after,
                  out_specs=[VMEM] * n,
                  scratch=[pltpu.VMEM((N_DEV, rb, LANES), f32) for rb in rbs]
                  + [pltpu.SemaphoreType.DMA((n, 2, N_DEV - 1)), pltpu.SemaphoreType.DMA((n, 2, N_DEV - 1))])(
                      *bufs, *([] if after is None else [after]))


def _other_chips(x, y):
    return [((1 - x, y), 2 * (1 - x) + y), ((x, 1 - y), 2 * x + (1 - y)), ((1 - x, 1 - y), 2 * (1 - x) + (1 - y))]


HBM = pl.BlockSpec(memory_space=pltpu.HBM)
SEM = pl.BlockSpec(memory_space=pltpu.SEMAPHORE)
EFFECT = pltpu.SideEffectType.DATAFLOW_SIDE_EFFECTING


def _in_hbm(a):
    return pltpu.with_memory_space_constraint(a, pltpu.HBM)


SIBLING_ID = 1


def _sibling_handshake():
    x, y, c = _place()
    barrier = pltpu.get_barrier_semaphore()
    pl.semaphore_signal(barrier, inc=1, device_id=(x, y, 1 - c), device_id_type=MESH)
    pl.semaphore_wait(barrier, 1)
    return x, y, c


def _ag_start(layers, after, name):
    flat = [t for lay in layers for t in lay]
    n, nl = len(flat), len(layers)

    def body(*refs):
        src = refs[:n]
        sems = refs[n + 1:n + 1 + 2 * nl]
        token = refs[-1]
        x, y, c = _place()
        s_me = 2 * x + y
        t = 0
        for i, lay in enumerate(layers):
            for k in range(len(lay)):
                for j, ((px, py), _) in enumerate(_other_chips(x, y)):
                    pltpu.make_async_remote_copy(src_ref=src[t].at[s_me, c], dst_ref=src[t].at[s_me, c],
                                                 send_sem=sems[2 * i].at[3 * k + j], recv_sem=sems[2 * i + 1].at[3 * k + j],
                                                 device_id=(px, py, c), device_id_type=MESH).start()
                t += 1
        token[...] = jnp.zeros_like(token)

    sem_shapes = [pltpu.SemaphoreType.DMA((3 * len(lay),)) for lay in layers for _ in range(2)]
    out_shape = sem_shapes + [pltpu.HBM(t.shape, t.dtype) for t in flat] + [_sds((8, LANES), f32)]
    outs = pl.pallas_call(
        body, name=name, out_shape=out_shape, in_specs=[HBM] * n + [ANY],
        out_specs=[SEM] * (2 * nl) + [HBM] * n + [VMEM], input_output_aliases={t: 2 * nl + t for t in range(n)},
        compiler_params=pltpu.CompilerParams(has_side_effects=EFFECT))(*[_in_hbm(t) for t in flat], after)
    sems = [(outs[2 * i], outs[2 * i + 1]) for i in range(nl)]
    thru, t = [], 2 * nl
    for lay in layers:
        thru.append(list(outs[t:t + len(lay)]))
        t += len(lay)
    return sems, thru, outs[-1]


def _ag_wait(inflight, sems, after, name):
    n = len(inflight)

    def body(*refs):
        src, ssem, rsem = refs[:n], refs[n], refs[n + 1]
        x, y, c = _place()
        s_me = 2 * x + y
        for k in range(n):
            for j, (_, s_p) in enumerate(_other_chips(x, y)):
                cp = pltpu.make_async_remote_copy(src_ref=src[k].at[s_me, c], dst_ref=src[k].at[s_p, c],
                                                  send_sem=ssem.at[3 * k + j], recv_sem=rsem.at[3 * k + j],
                                                  device_id=(x, y, c), device_id_type=MESH)
                cp.wait_send()
                cp.wait_recv()

    return pl.pallas_call(
        body, name=name, out_shape=[pltpu.HBM(t.shape, t.dtype) for t in inflight],
        in_specs=[HBM] * n + [SEM, SEM, ANY], out_specs=[HBM] * n, input_output_aliases={t: t for t in range(n)},
        compiler_params=pltpu.CompilerParams(has_side_effects=EFFECT))(*inflight, sems[0], sems[1], after)


def _ag_forward(arrived, name):
    n = len(arrived)

    def body(*refs):
        o = refs[n:2 * n]
        ssem, rsem = refs[2 * n:]
        x, y, c = _place()

        def copy(t, j, s, half, dev):
            return pltpu.make_async_remote_copy(src_ref=o[t].at[s, c], dst_ref=o[t].at[s, half], send_sem=ssem.at[t, j],
                                                recv_sem=rsem.at[t, j], device_id=dev, device_id_type=MESH)

        chips = _other_chips(x, y)
        sends = [copy(t, j, s_p, c, (x, y, 1 - c)) for t in range(n) for j, (_, s_p) in enumerate(chips)]
        for cp in sends:
            cp.start()
        for t in range(n):
            for j, (_, s_p) in enumerate(chips):
                copy(t, j, s_p, 1 - c, (x, y, c)).wait_recv()
        for cp in sends:
            cp.wait_send()

    return _pcall(body, name=name, out_shape=[_sds(p.shape, bf16) for p in arrived], in_specs=[ANY] * n,
                  out_specs=[ANY] * n, aliases={t: t for t in range(n)},
                  scratch=[pltpu.SemaphoreType.DMA((n, 3)), pltpu.SemaphoreType.DMA((n, 3))])(*arrived)


def _agf_start(arrived, name):
    n = len(arrived)

    def body(*refs):
        o = refs[:n]
        ssem, rsem, token = refs[n], refs[n + 1], refs[-1]
        x, y, c = _sibling_handshake()
        for t in range(n):
            for j, (_, s_p) in enumerate(_other_chips(x, y)):
                pltpu.make_async_remote_copy(src_ref=o[t].at[s_p, c], dst_ref=o[t].at[s_p, c],
                                             send_sem=ssem.at[3 * t + j], recv_sem=rsem.at[3 * t + j],
                                             device_id=(x, y, 1 - c), device_id_type=MESH).start()
        token[...] = jnp.zeros_like(token)

    out_shape = ([pltpu.SemaphoreType.DMA((3 * n,))] * 2 + [pltpu.HBM(a.shape, bf16) for a in arrived]
                 + [_sds((8, LANES), f32)])
    outs = pl.pallas_call(
        body, name=name, out_shape=out_shape, in_specs=[HBM] * n, out_specs=[SEM, SEM] + [HBM] * n + [VMEM],
        input_output_aliases={t: 2 + t for t in range(n)},
        compiler_params=pltpu.CompilerParams(has_side_effects=EFFECT, collective_id=SIBLING_ID))(
            *[_in_hbm(a) for a in arrived])
    return (outs[0], outs[1]), list(outs[2:2 + n]), outs[-1]


def _agf_wait(sems, inflight, after, name):
    n = len(inflight)

    def body(*refs):
        o, ssem, rsem = refs[:n], refs[n], refs[n + 1]
        x, y, c = _place()
        for t in range(n):
            for j, (_, s_p) in enumerate(_other_chips(x, y)):
                cp = pltpu.make_async_remote_copy(src_ref=o[t].at[s_p, c], dst_ref=o[t].at[s_p, 1 - c],
                                                  send_sem=ssem.at[3 * t + j], recv_sem=rsem.at[3 * t + j],
                                                  device_id=(x, y, c), device_id_type=MESH)
                cp.wait_send()
                cp.wait_recv()

    return pl.pallas_call(
        body, name=name, out_shape=[pltpu.HBM(a.shape, bf16) for a in inflight],
        in_specs=[HBM] * n + [SEM, SEM, ANY], out_specs=[HBM] * n, input_output_aliases={t: t for t in range(n)},
        compiler_params=pltpu.CompilerParams(has_side_effects=EFFECT))(*inflight, sems[0], sems[1], after)


def _rs_pair_start(grads, name):
    n = len(grads)

    def body(*refs):
        g, theirs = refs[:n], refs[n:2 * n]
        ssem, rsem, token = refs[2 * n], refs[2 * n + 1], refs[-1]
        x, y, c = _sibling_handshake()
        for t in range(n):
            pltpu.make_async_remote_copy(src_ref=g[t].at[:, 1 - c], dst_ref=theirs[t], send_sem=ssem.at[t],
                                         recv_sem=rsem.at[t], device_id=(x, y, 1 - c), device_id_type=MESH).start()
        token[...] = jnp.zeros_like(token)

    lands = [lax.empty((4,) + g.shape[2:], bf16) for g in grads]
    out_shape = ([pltpu.SemaphoreType.DMA((n,))] * 2 + [pltpu.HBM(g.shape, bf16) for g in grads]
                 + [pltpu.HBM(q.shape, bf16) for q in lands] + [_sds((8, LANES), f32)])
    outs = pl.pallas_call(
        body, name=name, out_shape=out_shape, in_specs=[HBM] * (2 * n), out_specs=[SEM, SEM] + [HBM] * (2 * n) + [VMEM],
        input_output_aliases={t: 2 + t for t in range(2 * n)},
        compiler_params=pltpu.CompilerParams(has_side_effects=EFFECT, collective_id=SIBLING_ID))(
            *[_in_hbm(a) for a in list(grads) + lands])
    return (outs[0], outs[1]), list(outs[2:2 + n]), list(outs[2 + n:2 + 2 * n]), outs[-1]


def _rs_pair_wait(sems, grads, lands, after, name):
    n = len(grads)

    def body(*refs):
        g, theirs = refs[:n], refs[n:2 * n]
        ssem, rsem = refs[2 * n], refs[2 * n + 1]
        x, y, c = _place()
        for t in range(n):
            cp = pltpu.make_async_remote_copy(src_ref=g[t].at[:, 1 - c], dst_ref=theirs[t], send_sem=ssem.at[t],
                                              recv_sem=rsem.at[t], device_id=(x, y, c), device_id_type=MESH)
            cp.wait_send()
            cp.wait_recv()

    outs = pl.pallas_call(
        body, name=name, out_shape=[pltpu.HBM(a.shape, bf16) for a in list(grads) + list(lands)],
        in_specs=[HBM] * (2 * n) + [SEM, SEM] + [ANY] * len(after), out_specs=[HBM] * (2 * n),
        input_output_aliases={t: t for t in range(2 * n)},
        compiler_params=pltpu.CompilerParams(has_side_effects=EFFECT))(*grads, *lands, sems[0], sems[1], *after)
    return list(outs[:n]), list(outs[n:])


def _rs_chip_start(pairs, name):
    n = len(pairs)

    def body(*refs):
        p, q = refs[:n], refs[n:2 * n]
        ssem, rsem, token = refs[2 * n], refs[2 * n + 1], refs[-1]
        x, y, c = _place()
        for t in range(n):
            for j, ((px, py), s_p) in enumerate(_other_chips(x, y)):
                pltpu.make_async_remote_copy(src_ref=p[t].at[s_p], dst_ref=q[t].at[j], send_sem=ssem.at[3 * t + j],
                                             recv_sem=rsem.at[3 * t + j], device_id=(px, py, c), device_id_type=MESH).start()
        token[...] = jnp.zeros_like(token)

    lands = [lax.empty((3,) + p.shape[1:], bf16) for p in pairs]
    out_shape = ([pltpu.SemaphoreType.DMA((3 * n,))] * 2 + [pltpu.HBM(p.shape, bf16) for p in pairs]
                 + [pltpu.HBM(q.shape, bf16) for q in lands] + [_sds((8, LANES), f32)])
    outs = pl.pallas_call(
        body, name=name, out_shape=out_shape, in_specs=[HBM] * (2 * n), out_specs=[SEM, SEM] + [HBM] * (2 * n) + [VMEM],
        input_output_aliases={t: 2 + t for t in range(2 * n)},
        compiler_params=pltpu.CompilerParams(has_side_effects=EFFECT))(*[_in_hbm(a) for a in list(pairs) + lands])
    return (outs[0], outs[1]), list(outs[2:2 + n]), list(outs[2 + n:2 + 2 * n]), outs[-1]


def _rs_chip_wait(sems, pairs, lands, after, name):
    n = len(pairs)

    def body(*refs):
        p, q = refs[:n], refs[n:2 * n]
        ssem, rsem = refs[2 * n], refs[2 * n + 1]
        x, y, c = _place()
        for t in range(n):
            for j, (_, s_p) in enumerate(_other_chips(x, y)):
                cp = pltpu.make_async_remote_copy(src_ref=p[t].at[s_p], dst_ref=q[t].at[j], send_sem=ssem.at[3 * t + j],
                                                  recv_sem=rsem.at[3 * t + j], device_id=(x, y, c), device_id_type=MESH)
                cp.wait_send()
                cp.wait_recv()

    outs = pl.pallas_call(
        body, name=name, out_shape=[pltpu.HBM(a.shape, bf16) for a in list(pairs) + list(lands)],
        in_specs=[HBM] * (2 * n) + [SEM, SEM] + [ANY] * len(after), out_specs=[HBM] * (2 * n),
        input_output_aliases={t: t for t in range(2 * n)},
        compiler_params=pltpu.CompilerParams(has_side_effects=EFFECT))(*pairs, *lands, sems[0], sems[1], *after)
    return list(outs[:n]), list(outs[n:])


def _rs_half_start(halves, name):
    n = len(halves)

    def body(*refs):
        o = refs[:n]
        ssem, rsem, token = refs[n], refs[n + 1], refs[-1]
        x, y, c = _sibling_handshake()
        for t in range(n):
            pltpu.make_async_remote_copy(src_ref=o[t].at[c], dst_ref=o[t].at[c], send_sem=ssem.at[t],
                                         recv_sem=rsem.at[t], device_id=(x, y, 1 - c), device_id_type=MESH).start()
        token[...] = jnp.zeros_like(token)

    out_shape = ([pltpu.SemaphoreType.DMA((n,))] * 2 + [pltpu.HBM(h.shape, h.dtype) for h in halves]
                 + [_sds((8, LANES), f32)])
    outs = pl.pallas_call(
        body, name=name, out_shape=out_shape, in_specs=[HBM] * n, out_specs=[SEM, SEM] + [HBM] * n + [VMEM],
        input_output_aliases={t: 2 + t for t in range(n)},
        compiler_params=pltpu.CompilerParams(has_side_effects=EFFECT, collective_id=SIBLING_ID))(
            *[_in_hbm(h) for h in halves])
    return (outs[0], outs[1]), list(outs[2:2 + n]), outs[-1]


def _rs_half_wait(sems, inflight, after, name):
    n = len(inflight)

    def body(*refs):
        o, ssem, rsem = refs[:n], refs[n], refs[n + 1]
        x, y, c = _place()
        for t in range(n):
            cp = pltpu.make_async_remote_copy(src_ref=o[t].at[c], dst_ref=o[t].at[1 - c], send_sem=ssem.at[t],
                                              recv_sem=rsem.at[t], device_id=(x, y, c), device_id_type=MESH)
            cp.wait_send()
            cp.wait_recv()

    return pl.pallas_call(
        body, name=name, out_shape=[pltpu.HBM(h.shape, h.dtype) for h in inflight],
        in_specs=[HBM] * n + [SEM, SEM, ANY], out_specs=[HBM] * n, input_output_aliases={t: t for t in range(n)},
        compiler_params=pltpu.CompilerParams(has_side_effects=EFFECT))(*inflight, sems[0], sems[1], after)


def _row_spec(tm, cols):
    return pl.BlockSpec((tm, cols), lambda i: (i, 0))


def _vec_spec(cols, rows=1):
    return pl.BlockSpec((rows, cols), lambda i: (0, 0))


def _modulated_norm(xv, g, shift, scale):
    r = lax.rsqrt(jnp.mean(xv * xv, axis=-1, keepdims=True) + EPS)
    return (((xv * r) * g) * (1.0 + scale) + shift).astype(bf16)


def _hnorm(x, g, shift, scale):
    T, tm = x.shape[0], 256

    def body(x_ref, g_ref, sh_ref, sc_ref, h_ref):
        h_ref[...] = _modulated_norm(x_ref[...], g_ref[...], sh_ref[...], sc_ref[...])

    return _pcall(body, name="hnorm", out_shape=_sds((T, D), bf16), grid=(T // tm,),
                  in_specs=[_row_spec(tm, D), _vec_spec(D), _vec_spec(D), _vec_spec(D)],
                  out_specs=_row_spec(tm, D))(x, g, shift, scale)


def _out_proj(y2, wo, x, gate, nxt=None):
    T, tm = x.shape[0], 512

    def body(y_ref, w_ref, x_ref, g_ref, *rest):
        o = jnp.dot(y_ref[0], w_ref[0], preferred_element_type=f32)
        o = o + jnp.dot(y_ref[1], w_ref[1], preferred_element_type=f32)
        xo = x_ref[...] + g_ref[...] * o
        if nxt is None:
            xo_ref, o_ref = rest
        else:
            ng_ref, nsh_ref, nsc_ref, xo_ref, o_ref, h_ref = rest
            h_ref[...] = _modulated_norm(xo, ng_ref[...], nsh_ref[...], nsc_ref[...])
        o_ref[...] = o.astype(bf16)
        xo_ref[...] = xo

    extra = [] if nxt is None else list(nxt)
    n_out = 2 if nxt is None else 3
    return _pcall(body, name="out_proj", out_shape=[_sds((T, D), f32), _sds((T, D), bf16), _sds((T, D), bf16)][:n_out],
                  grid=(T // tm,),
                  in_specs=[pl.BlockSpec((2, tm, D), lambda i: (0, i, 0)), pl.BlockSpec((2, D, D), lambda i: (0, 0, 0)),
                            _row_spec(tm, D), _vec_spec(D)] + [_vec_spec(D)] * len(extra),
                  out_specs=[_row_spec(tm, D)] * n_out, vmem_mb=40)(y2, wo, x, gate, *extra)


def _gate_bwd_tile(dx, o_ref, gate_ref, dob_ref, dgate_ref):
    dob_ref[...] = (dx * gate_ref[...]).astype(bf16)
    dgate_ref[...] += jnp.sum(dx * o_ref[...].astype(f32), axis=0, keepdims=True)


def _loss_bwd(x, target, g, o, gate):
    T, tm = x.shape[0], 512

    def body(x_ref, t_ref, g_ref, o_ref, gate_ref, dx_ref, loss_ref, dg_ref, dob_ref, dgate_ref):
        @pl.when(pl.program_id(0) == 0)
        def _():
            loss_ref[...] = jnp.zeros_like(loss_ref)
            dg_ref[...] = jnp.zeros_like(dg_ref)
            dgate_ref[...] = jnp.zeros_like(dgate_ref)

        xv, gv = x_ref[...], g_ref[...]
        r = lax.rsqrt(jnp.mean(xv * xv, axis=-1, keepdims=True) + EPS)
        xn = xv * r
        err = xn * gv - t_ref[...]
        dy = err * (1.0 / D)
        dxn = dy * gv
        dx = r * (dxn - xn * jnp.mean(dxn * xn, axis=-1, keepdims=True))
        dx_ref[...] = dx
        dg_ref[...] += jnp.sum(dy * xn, axis=0, keepdims=True)
        loss_ref[...] += (0.5 / D) * jnp.sum(jnp.sum(err * err, axis=1, keepdims=True), axis=0, keepdims=True)
        _gate_bwd_tile(dx, o_ref, gate_ref, dob_ref, dgate_ref)

    return _pcall(body, name="loss_bwd",
                  out_shape=[_sds((T, D), f32), _sds((1, 1), f32), _sds((1, D), f32), _sds((T, D), bf16), _sds((1, D), f32)],
                  grid=(T // tm,),
                  in_specs=[_row_spec(tm, D), _row_spec(tm, D), _vec_spec(D), _row_spec(tm, D), _vec_spec(D)],
                  out_specs=[_row_spec(tm, D), pl.BlockSpec((1, 1), lambda i: (0, 0)), _vec_spec(D), _row_spec(tm, D),
                             _vec_spec(D)])(x, target, g, o, gate)


def _norm_bwd(x, dh, gin, g, scale, below=None):
    T, tm = x.shape[0], 256

    def body(x_ref, dh_ref, gin_ref, g_ref, sc_ref, *rest):
        if below is None:
            dx_ref, st_ref = rest
        else:
            o_ref, gate_ref, dx_ref, st_ref, dob_ref, dgate_ref = rest

        @pl.when(pl.program_id(0) == 0)
        def _():
            st_ref[...] = jnp.zeros_like(st_ref)
            if below is not None:
                dgate_ref[...] = jnp.zeros_like(dgate_ref)

        xv, gv, dhv = x_ref[...], g_ref[...], dh_ref[...]
        r = lax.rsqrt(jnp.mean(xv * xv, axis=-1, keepdims=True) + EPS)
        xn = xv * r
        da = dhv * (1.0 + sc_ref[...])
        dxn = da * gv
        dx = gin_ref[...] + r * (dxn - xn * jnp.mean(dxn * xn, axis=-1, keepdims=True))
        dx_ref[...] = dx
        st_ref[0:1, :] += jnp.sum(dhv, axis=0, keepdims=True)
        st_ref[1:2, :] += jnp.sum(dhv * (xn * gv), axis=0, keepdims=True)
        st_ref[2:3, :] += jnp.sum(da * xn, axis=0, keepdims=True)
        if below is not None:
            _gate_bwd_tile(dx, o_ref, gate_ref, dob_ref, dgate_ref)

    out_shape = [_sds((T, D), f32), _sds((8, D), f32)]
    in_specs = [_row_spec(tm, D), _row_spec(tm, D), _row_spec(tm, D), _vec_spec(D), _vec_spec(D)]
    out_specs = [_row_spec(tm, D), _vec_spec(D, 8)]
    args = [x, dh, gin, g, scale]
    if below is not None:
        out_shape += [_sds((T, D), bf16), _sds((1, D), f32)]
        in_specs += [_row_spec(tm, D), _vec_spec(D)]
        out_specs += [_row_spec(tm, D), _vec_spec(D)]
        args += list(below)
    return _pcall(body, name="norm_bwd", out_shape=out_shape, grid=(T // tm,), in_specs=in_specs,
                  out_specs=out_specs)(*args)


STEPS = 4
ADAMW_STEPS = 8


def _cast_place(place, ws, layer, after=None):
    n = len(ws)

    def body(place_ref, *refs):
        for t in range(n):
            refs[-n + t][...] = refs[t][...].astype(bf16)

    def tile(w):
        return w.shape[1] // STEPS, w.shape[2]

    extra = [] if after is None else [after]
    return _pcall(body, name="cast_place", out_shape=[_sds((4,) + w.shape[1:], bf16) for w in ws], grid=(STEPS,),
                  prefetch=1,
                  in_specs=[pl.BlockSpec((None,) + tile(w), lambda i, pr: (layer, i, 0)) for w in ws] + [ANY] * len(extra),
                  out_specs=[pl.BlockSpec((None,) + tile(w), lambda i, pr: (pr[0], i, 0)) for w in ws])(
                      place, *ws, *extra)


def _rs_add(place, grads, theirs):
    n = len(grads)

    def body(place_ref, *refs):
        for t in range(n):
            refs[2 * n + t][...] = (refs[t][...].astype(f32) + refs[n + t][...].astype(f32)).astype(bf16)

    def tile(q):
        return q.shape[1] // 2, q.shape[2]

    mine = [pl.BlockSpec((None, None) + tile(q), lambda s, i, pr: (s, pr[1], i, 0)) for q in theirs]
    shard = [pl.BlockSpec((None,) + tile(q), lambda s, i, pr: (s, i, 0)) for q in theirs]
    return _pcall(body, name="rs_add", out_shape=[_sds(q.shape, bf16) for q in theirs], grid=(4, 2), prefetch=1,
                  in_specs=mine + shard, out_specs=shard)(place, *grads, *theirs)


def _rs_sum(place, pairs, slots):
    n, steps = len(pairs), 4

    def body(place_ref, *refs):
        for t in range(n):
            p_ref, q_ref = refs[t], refs[n + t]
            total = ((p_ref[...].astype(f32) + q_ref[0].astype(f32)) + q_ref[1].astype(f32)) + q_ref[2].astype(f32)
            refs[2 * n + t][...] = total.astype(bf16)

    def tile(q):
        return q.shape[1] // steps, q.shape[2]

    return _pcall(body, name="rs_sum", out_shape=[_sds((2,) + q.shape[1:], bf16) for q in slots], grid=(steps,),
                  prefetch=1,
                  in_specs=[pl.BlockSpec((None,) + tile(q), lambda i, pr: (pr[0], i, 0)) for q in slots]
                  + [pl.BlockSpec((3,) + tile(q), lambda i, pr: (0, i, 0)) for q in slots],
                  out_specs=[pl.BlockSpec((None,) + tile(q), lambda i, pr: (pr[1], i, 0)) for q in slots])(
                      place, *pairs, *slots)


def _adamw_math(w, g, m, v):
    m = ADAM_B1 * m + (1.0 - ADAM_B1) * g
    v = ADAM_B2 * v + (1.0 - ADAM_B2) * jnp.square(g)
    m_hat = m / (1.0 - ADAM_B1 ** ADAM_STEP)
    v_hat = v / (1.0 - ADAM_B2 ** ADAM_STEP)
    delta = -ADAM_LR * (m_hat / (jnp.sqrt(v_hat) + ADAM_EPS) + ADAM_WD * w)
    return delta, m, v


def _adamw_layer(layer, items):
    n = len(items)

    def body(*refs):
        outs = refs[-4 * n:]
        for t in range(n):
            w_ref, g_ref, m_ref, v_ref = refs[4 * t:4 * t + 4]
            g = g_ref[...].astype(f32)
            outs[4 * t][...] = g
            outs[4 * t + 1][...], outs[4 * t + 2][...], outs[4 * t + 3][...] = _adamw_math(
                w_ref[...], g, m_ref[...], v_ref[...])

    args, in_specs, out_specs, out_shape = [], [], [], []
    for w, g, m, v, _ in items:
        tr, cols = w.shape[1] // ADAMW_STEPS, w.shape[2]
        spec = pl.BlockSpec((None, tr, cols), lambda i: (layer, i, 0))
        args += [w, g, m, v]
        in_specs += [spec, pl.BlockSpec((tr, cols), lambda i: (i, 0)), spec, spec]
        out_specs += [spec] * 4
        out_shape += [_sds(w.shape, f32)] * 4
    aliases = {}
    for t, it in enumerate(items):
        if it[4] is not None:
            for k in range(4):
                aliases[len(args)] = 4 * t + k
                args.append(it[4][k])
                in_specs.append(ANY)
    res = _pcall(body, name="adamw", out_shape=out_shape, grid=(ADAMW_STEPS,), in_specs=in_specs, out_specs=out_specs,
                 aliases=aliases)(*args)
    return [tuple(res[4 * t:4 * t + 4]) for t in range(n)]


def _adamw_small(items):
    n = len(items)

    def body(*refs):
        ins, outs = refs[:4 * n], refs[4 * n:]
        for t in range(n):
            w_ref, g_ref, m_ref, v_ref = ins[4 * t:4 * t + 4]
            if len(g_ref.shape) == len(w_ref.shape) + 1:
                g = g_ref[0]
                for b in range(1, g_ref.shape[0]):
                    g = g + g_ref[b]
            else:
                g = g_ref[...]
            d, m, v = _adamw_math(w_ref[...], g, m_ref[...], v_ref[...])
            outs[4 * t][...], outs[4 * t + 1][...], outs[4 * t + 2][...], outs[4 * t + 3][...] = g, d, m, v

    out_shape = [_sds(w.shape, f32) for (w, _, _, _) in items for _ in range(4)]
    flat = [a for it in items for a in it]
    res = _pcall(body, name="adamw_small", out_shape=out_shape, in_specs=[VMEM] * (4 * n),
                 out_specs=[VMEM] * (4 * n))(*flat)
    return [tuple(res[4 * t:4 * t + 4]) for t in range(n)]


NN = ((1,), (0,))
NT = ((1,), (1,))
TN = ((0,), (0,))


def _mm(name, a, b, *, grid, a_spec, b_spec, out_shape, out_spec, dims, vmem_mb=None):
    def body(a_ref, b_ref, o_ref):
        r = lax.dot_general(a_ref[...], b_ref[...], (dims, ((), ())), preferred_element_type=f32)
        o_ref[...] = r.astype(o_ref.dtype)

    return _pcall(body, name=name, out_shape=out_shape, grid=grid, in_specs=[a_spec, b_spec], out_specs=out_spec,
                  vmem_mb=vmem_mb)(a, b)


def _whole(shape):
    return pl.BlockSpec(shape, lambda j: (0,) * len(shape))


def _split_spec(rows, tile, per_split):
    return pl.BlockSpec((None, rows, tile), lambda j: (j // per_split, 0, j % per_split))


class _Proj:
    def __init__(self, n, splits, tile):
        self.n, self.splits, self.tile = n, splits, tile
        self.steps = n // tile
        self.w_per = n // 4 // tile
        self.a_per = n // splits // tile
        assert self.w_per * tile * 4 == n and self.a_per * tile * splits == n

    def fwd(self, hb, wg):
        T = hb.shape[0]
        sub, tile, w_per = FWD_TILES, self.tile, self.w_per
        wide = sub * tile
        a_per = self.n // self.splits // wide
        assert a_per * wide * self.splits == self.n

        def w_tile(q):
            return pl.BlockSpec((None, D, tile), lambda j: ((sub * j + q) // w_per, 0, (sub * j + q) % w_per))

        def body(a_ref, *rest):
            w = jnp.concatenate([rest[q][...] for q in range(sub)], axis=1)
            rest[sub][...] = jnp.dot(a_ref[...], w, preferred_element_type=f32).astype(bf16)

        return _pcall(body, name="proj_fwd", out_shape=_sds((self.splits, T, self.n // self.splits), bf16),
                      grid=(self.n // wide,), in_specs=[_whole((T, D))] + [w_tile(q) for q in range(sub)],
                      out_specs=pl.BlockSpec((None, T, wide), lambda j: (j // a_per, 0, j % a_per)),
                      vmem_mb=40 if wide > 512 else None)(hb, *([wg] * sub))

    def dw(self, hb, dp):
        T = hb.shape[0]
        return _mm("proj_dw", hb, dp, grid=(self.steps,), a_spec=_whole((T, D)),
                   b_spec=_split_spec(T, self.tile, self.a_per), out_shape=_sds((4, D, self.n // 4), bf16),
                   out_spec=_split_spec(D, self.tile, self.w_per), dims=TN)

    def dh(self, dp, wg, after=None):
        T = dp.shape[1]
        extra = [] if after is None else [after]
        sub, tile, w_per = DH_WIDE // self.tile, self.tile, self.w_per
        a_per = self.n // self.splits // DH_WIDE
        assert sub * tile == DH_WIDE and a_per * DH_WIDE * self.splits == self.n

        def w_tile(q):
            return pl.BlockSpec((None, D, tile), lambda k: ((sub * k + q) // w_per, 0, (sub * k + q) % w_per))

        def body(a_ref, *rest):
            o_ref = rest[-1]
            w = jnp.concatenate([rest[q][...] for q in range(sub)], axis=1)
            r = lax.dot_general(a_ref[...], w, (NT, ((), ())), preferred_element_type=f32)

            @pl.when(pl.program_id(0) == 0)
            def _():
                o_ref[...] = r

            @pl.when(pl.program_id(0) > 0)
            def _():
                o_ref[...] += r

        return _pcall(body, name="proj_dh", out_shape=_sds((T, D), f32), grid=(self.n // DH_WIDE,),
                      in_specs=[pl.BlockSpec((None, T, DH_WIDE), lambda k: (k // a_per, 0, k % a_per))]
                      + [w_tile(q) for q in range(sub)] + [ANY] * len(extra),
                      out_specs=_whole((T, D)), vmem_mb=DH_VMEM_MB)(dp, *([wg] * sub), *extra)


EVEN_PROJ = _Proj(7 * D, 7, 256)
ODD_PROJ = _Proj(4 * D, 2, 512)


def _out_bwd(dob, wo, y2):
    T = dob.shape[0]
    w_spec = pl.BlockSpec((None, 512, D), lambda j: (j, 0, 0))

    def body(dob_ref, w_ref, y_ref, dy_ref, dw_ref):
        dob_v = dob_ref[...]
        dy_ref[...] = lax.dot_general(dob_v, w_ref[...], (NT, ((), ())), preferred_element_type=f32).astype(bf16)
        dw_ref[...] = lax.dot_general(y_ref[...], dob_v, (TN, ((), ())), preferred_element_type=f32).astype(bf16)

    return _pcall(body, name="out_bwd", out_shape=[_sds((2, T, D), bf16), _sds((4, 512, D), bf16)], grid=(4,),
                  in_specs=[_whole((T, D)), w_spec, _split_spec(T, 512, 2)],
                  out_specs=[_split_spec(T, 512, 2), w_spec])(dob, wo, y2)


def _head_spec(lead, T):
    return pl.BlockSpec((lead, T, HEAD), lambda h: (0, 0, h))


def _head_vec(rows):
    return pl.BlockSpec((rows, HEAD), lambda h: (0, h))


_HEAD_MAT = pl.BlockSpec((None, HEAD, HEAD), lambda h: (h, 0, 0))


def _causal():
    return lax.broadcasted_iota(jnp.int32, (HEAD, HEAD), 0) >= lax.broadcasted_iota(jnp.int32, (HEAD, HEAD), 1)


def _layernorm_head(v):
    mu = jnp.mean(v, axis=-1, keepdims=True)
    d = v - mu
    rstd = lax.rsqrt(jnp.mean(d * d, axis=-1, keepdims=True) + EPS)
    return d * rstd, rstd


def _even_fwd(p7, conv_w, ln_g, ln_b, sgu_w, sgu_bias):
    T, C = p7.shape[1], CHUNK_ROWS

    def body(p_ref, cw_ref, lg_ref, lb_ref, w_ref, b_ref, y_ref):
        w0, w1, w2 = cw_ref[0:1, :], cw_ref[1:2, :], cw_ref[2:3, :]
        wm = jnp.where(_causal(), w_ref[...], 0.0).astype(bf16)
        bias, lg, lb = b_ref[...], lg_ref[...], lb_ref[...]

        def step(i, halo):
            rows = pl.ds(pl.multiple_of(i * C, C), C)
            ah, ab, ac, az, u, v, zb = (p_ref[k, rows, :].astype(f32) for k in range(7))
            tt = ac * ah
            ext = jnp.concatenate([halo, tt], axis=0)
            cv = w2 * tt + w1 * pltpu.roll(ext, 1, 0)[HALO_CONV:] + w0 * pltpu.roll(ext, 2, 0)[HALO_CONV:]
            y_ref[0, rows, :] = (ab * cv * _silu(az)).astype(bf16)
            vhat, _ = _layernorm_head(v)
            vn = (vhat * lg + lb).astype(bf16)
            mix = jnp.concatenate([jnp.dot(wm, vn[k * HEAD:(k + 1) * HEAD], preferred_element_type=f32) + bias
                                   for k in range(C // HEAD)], axis=0)
            y_ref[1, rows, :] = (u * mix * _silu(zb)).astype(bf16)
            return tt[C - HALO_CONV:]

        lax.fori_loop(0, T // C, step, jnp.zeros((HALO_CONV, HEAD), f32))

    return _pcall(body, name="even_fwd", out_shape=_sds((2, T, D), bf16), grid=(NH,),
                  in_specs=[_head_spec(7, T), _head_vec(3), _head_vec(1), _head_vec(1), _HEAD_MAT, _HEAD_MAT],
                  out_specs=_head_spec(2, T))(p7, conv_w, ln_g, ln_b, sgu_w, sgu_bias)


def _even_bwd(p7, dy2, conv_w, ln_g, ln_b, sgu_w, sgu_bias):
    T, C = p7.shape[1], CHUNK_ROWS
    n_chunks = T // C

    def body(p_ref, dy_ref, cw_ref, lg_ref, lb_ref, w_ref, b_ref,
             dp_ref, dcw_ref, dlg_ref, dlb_ref, dw_ref, dms_ref, dcv_s):
        w0, w1, w2 = cw_ref[0:1, :], cw_ref[1:2, :], cw_ref[2:3, :]
        tri = _causal()
        wm = jnp.where(tri, w_ref[...], 0.0).astype(bf16)
        bias, lg, lb = b_ref[...], lg_ref[...], lb_ref[...]
        dw_ref[...] = jnp.zeros_like(dw_ref)
        dms_ref[...] = jnp.zeros_like(dms_ref)

        def fwd_step(i, carry):
            halo, a0, a1, a2, alg, alb = carry
            rows = pl.ds(pl.multiple_of(i * C, C), C)
            ah, ab, ac, az = (p_ref[k, rows, :].astype(f32) for k in range(4))
            dya = dy_ref[0, rows, :].astype(f32)
            tt = ac * ah
            ext = jnp.concatenate([halo, tt], axis=0)
            t1, t2 = pltpu.roll(ext, 1, 0)[HALO_CONV:], pltpu.roll(ext, 2, 0)[HALO_CONV:]
            cv = w2 * tt + w1 * t1 + w0 * t2
            sa, dsa = _silu_and_grad(az)
            g1 = dya * sa
            dp_ref[1, rows, :] = (g1 * cv).astype(bf16)
            dp_ref[3, rows, :] = (dya * ab * cv * dsa).astype(bf16)
            dcv = g1 * ab
            dcv_s[rows, :] = dcv
            a2 = a2 + jnp.sum(dcv * tt, axis=0, keepdims=True)
            a1 = a1 + jnp.sum(dcv * t1, axis=0, keepdims=True)
            a0 = a0 + jnp.sum(dcv * t2, axis=0, keepdims=True)

            u, zb, dyb = p_ref[4, rows, :].astype(f32), p_ref[6, rows, :].astype(f32), dy_ref[1, rows, :].astype(f32)
            vhat, rstd = _layernorm_head(p_ref[5, rows, :].astype(f32))
            vn = (vhat * lg + lb).astype(bf16)
            sb, dsb = _silu_and_grad(zb)
            mix = jnp.concatenate([jnp.dot(wm, vn[k * HEAD:(k + 1) * HEAD], preferred_element_type=f32) + bias
                                   for k in range(C // HEAD)], axis=0)
            dp_ref[4, rows, :] = (dyb * mix * sb).astype(bf16)
            dp_ref[6, rows, :] = (dyb * u * mix * dsb).astype(bf16)
            dmix = dyb * u * sb
            dvn_parts = []
            for k in range(C // HEAD):
                dm = dmix[k * HEAD:(k + 1) * HEAD]
                dmb = dm.astype(bf16)
                dvn_parts.append(lax.dot_general(wm, dmb, (TN, ((), ())), preferred_element_type=f32))
                dw_ref[...] += lax.dot_general(dmb, vn[k * HEAD:(k + 1) * HEAD], (NT, ((), ())),
                                               preferred_element_type=f32)
                dms_ref[...] += dm
            dvn = jnp.concatenate(dvn_parts, axis=0)
            alg = alg + jnp.sum(dvn * vhat, axis=0, keepdims=True)
            alb = alb + jnp.sum(dvn, axis=0, keepdims=True)
            dvh = dvn * lg
            dv = rstd * (dvh - jnp.mean(dvh, axis=-1, keepdims=True)
                         - vhat * jnp.mean(dvh * vhat, axis=-1, keepdims=True))
            dp_ref[5, rows, :] = dv.astype(bf16)
            return tt[C - HALO_CONV:], a0, a1, a2, alg, alb

        zrow = jnp.zeros((1, HEAD), f32)
        _, a0, a1, a2, alg, alb = lax.fori_loop(
            0, n_chunks, fwd_step, (jnp.zeros((HALO_CONV, HEAD), f32), zrow, zrow, zrow, zrow, zrow))
        dcw_ref[0:1, :], dcw_ref[1:2, :], dcw_ref[2:3, :] = a0, a1, a2
        dlg_ref[...], dlb_ref[...] = alg, alb
        dw_ref[...] = jnp.where(tri, dw_ref[...], 0.0)

        def bwd_step(k, halo):
            rows = pl.ds(pl.multiple_of((n_chunks - 1 - k) * C, C), C)
            dcv = dcv_s[rows, :]
            ext = jnp.concatenate([dcv, halo], axis=0)
            n1 = pltpu.roll(ext, C + HALO_CONV - 1, 0)[:C]
            n2 = pltpu.roll(ext, C + HALO_CONV - 2, 0)[:C]
            dtt = w2 * dcv + w1 * n1 + w0 * n2
            dp_ref[2, rows, :] = (dtt * p_ref[0, rows, :].astype(f32)).astype(bf16)
            dp_ref[0, rows, :] = (dtt * p_ref[2, rows, :].astype(f32)).astype(bf16)
            return dcv[:HALO_CONV]

        lax.fori_loop(0, n_chunks, bwd_step, jnp.zeros((HALO_CONV, HEAD), f32))

    out_shape = [_sds((7, T, D), bf16), _sds((3, D), f32), _sds((1, D), f32), _sds((1, D), f32),
                 _sds((NH, HEAD, HEAD), f32), _sds((NH, HEAD, HEAD), f32)]
    return _pcall(body, name="even_bwd", out_shape=out_shape, grid=(NH,),
                  in_specs=[_head_spec(7, T), _head_spec(2, T), _head_vec(3), _head_vec(1), _head_vec(1),
                            _HEAD_MAT, _HEAD_MAT],
                  out_specs=[_head_spec(7, T), _head_vec(3), _head_vec(1), _head_vec(1), _HEAD_MAT, _HEAD_MAT],
                  scratch=[pltpu.VMEM((T, HEAD), f32)])(p7, dy2, conv_w, ln_g, ln_b, sgu_w, sgu_bias)


def _window_sum(ext, win, towards_past):
    n, k, s = ext.shape[0], 1, ext
    while k < win:
        s = s + pltpu.roll(s, k if towards_past else n - k, 0)
        k *= 2
    return s


def _pool_count(i, C, win):
    t = i * C + lax.broadcasted_iota(jnp.int32, (C, 1), 0)
    cnt = jnp.minimum(t + 1, win).astype(f32)
    return cnt, 1.0 / cnt


def _group_specs(T):
    p_spec = pl.BlockSpec((None, T, GC), lambda g: (0, 0, g))
    z_spec = pl.BlockSpec((None, T, GC), lambda g: (1, 0, g))
    pw_spec = pl.BlockSpec((4, GC // 4, GC), lambda g: (0, g, 0))
    ps_spec = pl.BlockSpec((1, GC), lambda g: (0, g))
    y_spec = pl.BlockSpec((None, T, GC), lambda g: (g // 2, 0, g % 2))
    return p_spec, z_spec, pw_spec, ps_spec, y_spec


def _odd_fwd(p2, pool_wg, pool_scale):
    T, C = p2.shape[1], CHUNK_ROWS
    p_spec, z_spec, pw_spec, ps_spec, y_spec = _group_specs(T)

    def body(p_ref, z_ref, pw_ref, ps_ref, y_ref):
        pw, ps = pw_ref[...].reshape(GC, GC), ps_ref[...]

        def run(win):
            def step(i, halo):
                rows = pl.ds(pl.multiple_of(i * C, C), C)
                p = p_ref[rows, :].astype(f32)
                s = _window_sum(jnp.concatenate([halo, p], axis=0), win, True)[HALO_POOL:]
                pooled = s * _pool_count(i, C, win)[1] - p
                ypre = jnp.dot(pooled.astype(bf16), pw, preferred_element_type=f32)
                y_ref[rows, :] = (ypre * ps * _silu(z_ref[rows, :].astype(f32))).astype(bf16)
                return p[C - HALO_POOL:]

            lax.fori_loop(0, T // C, step, jnp.zeros((HALO_POOL, GC), f32))

        for gi, win in enumerate(WINDOWS):
            pl.when(pl.program_id(0) == gi)(functools.partial(run, win))

    return _pcall(body, name="odd_fwd", out_shape=_sds((2, T, D), bf16), grid=(len(WINDOWS),),
                  in_specs=[p_spec, z_spec, pw_spec, ps_spec], out_specs=y_spec)(p2, p2, pool_wg, pool_scale)


def _odd_bwd(p2, dy2, pool_wg, pool_scale):
    T, C = p2.shape[1], CHUNK_ROWS
    n_chunks = T // C
    p_spec, z_spec, pw_spec, ps_spec, y_spec = _group_specs(T)

    def body(p_ref, z_ref, dy_ref, pw_ref, ps_ref, dp_ref, dpw_ref, dps_ref, q_s, acc_s):
        pw, ps = pw_ref[...].reshape(GC, GC), ps_ref[...]

        def run(win):
            acc_s[...] = jnp.zeros_like(acc_s)

            def fwd_step(i, carry):
                halo, aps = carry
                rows = pl.ds(pl.multiple_of(i * C, C), C)
                p, z, dy = p_ref[rows, :].astype(f32), z_ref[rows, :].astype(f32), dy_ref[rows, :].astype(f32)
                _, inv_cnt = _pool_count(i, C, win)
                s = _window_sum(jnp.concatenate([halo, p], axis=0), win, True)[HALO_POOL:]
                pb = (s * inv_cnt - p).astype(bf16)
                ypre = jnp.dot(pb, pw, preferred_element_type=f32)
                sz, dsz = _silu_and_grad(z)
                aps = aps + jnp.sum(dy * ypre * sz, axis=0, keepdims=True)
                dp_ref[1, rows, :] = (dy * ypre * ps * dsz).astype(bf16)
                dyp = (dy * ps * sz).astype(bf16)
                acc_s[...] += lax.dot_general(pb, dyp, (TN, ((), ())), preferred_element_type=f32)
                dpool = lax.dot_general(dyp, pw, (NT, ((), ())), preferred_element_type=f32)
                q_s[rows, :] = dpool * inv_cnt
                return p[C - HALO_POOL:], aps

            _, aps = lax.fori_loop(0, n_chunks, fwd_step, (jnp.zeros((HALO_POOL, GC), f32), jnp.zeros((1, GC), f32)))
            dps_ref[...] = aps
            dpw_ref[...] = acc_s[...].reshape(4, GC // 4, GC).astype(bf16)

            def bwd_step(k, halo):
                i = n_chunks - 1 - k
                rows = pl.ds(pl.multiple_of(i * C, C), C)
                q = q_s[rows, :]
                s = _window_sum(jnp.concatenate([q, halo], axis=0), win, False)[:C]
                dp_ref[0, rows, :] = (s - q * _pool_count(i, C, win)[0]).astype(bf16)
                return q[:HALO_POOL]

            lax.fori_loop(0, n_chunks, bwd_step, jnp.zeros((HALO_POOL, GC), f32))

        for gi, win in enumerate(WINDOWS):
            pl.when(pl.program_id(0) == gi)(functools.partial(run, win))

    out_shape = [_sds((2, T, 2 * D), bf16), _sds((4, GC, GC), bf16), _sds((1, 2 * D), f32)]
    return _pcall(body, name="odd_bwd", out_shape=out_shape, grid=(len(WINDOWS),),
                  in_specs=[p_spec, z_spec, y_spec, pw_spec, ps_spec],
                  out_specs=[pl.BlockSpec((2, T, GC), lambda g: (0, 0, g)), pw_spec, ps_spec],
                  scratch=[pltpu.VMEM((T, GC), f32), pltpu.VMEM((GC, GC), f32)], vmem_mb=44)(
                      p2, p2, dy2, pool_wg, pool_scale)


def _ada_fwd(c_all, ada_w):
    cols = ada_w.shape[2]

    def body(c_ref, w_ref, o_ref):
        o_ref[...] = jnp.dot(_silu(c_ref[...]), w_ref[...], preferred_element_type=f32,
                             precision=lax.Precision.HIGHEST)

    return _pcall(body, name="ada_fwd", out_shape=_sds((4, N_DEV, cols), f32), grid=(4,),
                  in_specs=[pl.BlockSpec((N_DEV, D), lambda i: (0, 0)), pl.BlockSpec((None, D, cols), lambda i: (i, 0, 0))],
                  out_specs=pl.BlockSpec((None, N_DEV, cols), lambda i: (i, 0, 0)))(c_all, ada_w)


def _ada_bwd(c_all_t, dmod, w, m, v):
    cols, tr = w.shape[2], 256
    spec = pl.BlockSpec((None, tr, cols), lambda l, i: (l, i, 0))

    def body(c_ref, dm_ref, w_ref, m_ref, v_ref, g_ref, d_ref, mo_ref, vo_ref):
        sc = _silu(c_ref[...])
        g = sc[:, 0:1] * dm_ref[0:1, :]
        for b in range(1, N_DEV):
            g = g + sc[:, b:b + 1] * dm_ref[b:b + 1, :]
        g_ref[...] = g
        d_ref[...], mo_ref[...], vo_ref[...] = _adamw_math(w_ref[...], g, m_ref[...], v_ref[...])

    return _pcall(body, name="ada_bwd", out_shape=[_sds(w.shape, f32)] * 4, grid=(4, D // tr),
                  in_specs=[pl.BlockSpec((tr, N_DEV), lambda l, i: (i, 0)),
                            pl.BlockSpec((None, N_DEV, cols), lambda l, i: (l, 0, 0)), spec, spec, spec],
                  out_specs=[spec] * 4)(c_all_t, dmod, w, m, v)


def _layer_fwd(even, x, hb, gate, w, nxt, before_out=None):
    if even:
        w_in, w_out, conv_w, ln_g, ln_b, sgu_w, sgu_b = w
        bias = jnp.broadcast_to(sgu_b[:, :, None], (NH, HEAD, HEAD))
        p = EVEN_PROJ.fwd(hb, w_in)
        y2 = _even_fwd(p, conv_w, ln_g, ln_b, sgu_w, bias)
    else:
        w_in, pool_w, w_out, pool_scale = w
        p = ODD_PROJ.fwd(hb, w_in)
        y2 = _odd_fwd(p, pool_w, pool_scale)
    if before_out is not None:
        late_w_out, tok = before_out(y2)
        if late_w_out is not None:
            w_out = late_w_out
            w = (w_in, w_out) + tuple(w[2:]) if even else (w_in, pool_w, w_out, pool_scale)
        if tok is not None:
            gate = gate + tok[0:1, 0:1]
    outs = _out_proj(y2, w_out.reshape(2, D, D), x, gate, nxt)
    return outs[0], (None if nxt is None else outs[2]), (x, hb, p, y2, outs[1]), w


def _layer_bwd(even, gin, dob, dgate, saved, scale, g, w, below=None, send=None):
    x_in, hb, p, y2, o = saved
    if even:
        w_in, w_out, conv_w, ln_g, ln_b, sgu_w, sgu_b = w
        bias = jnp.broadcast_to(sgu_b[:, :, None], (NH, HEAD, HEAD))
        dy2, dwo = _out_bwd(dob, w_out, y2)
        dp, dconv, dlg, dlb, dsw, dms = _even_bwd(p, dy2, conv_w, ln_g, ln_b, sgu_w, bias)
        proj = EVEN_PROJ
        small = dict(conv_w=dconv, ln_g=dlg, ln_b=dlb, sgu_w=dsw, sgu_b=jnp.sum(dms, axis=-1))
        big = [proj.dw(hb, dp), dwo]
    else:
        w_in, pool_w, w_out, pool_scale = w
        dy2, dwo = _out_bwd(dob, w_out, y2)
        dp, dpw, dps = _odd_bwd(p, dy2, pool_w, pool_scale)
        proj = ODD_PROJ
        small = dict(pool_scale=dps)
        big = [proj.dw(hb, dp), dpw, dwo]
    tok = None
    if send is not None:
        big, tok = send(big)
    dh = proj.dh(dp, w_in, tok)
    res = _norm_bwd(x_in, dh, gin, g, scale, below)
    stats = res[1]
    return (res[0], (None if below is None else (res[2], res[3])), big, small,
            jnp.concatenate([stats[0:2], dgate], axis=0), stats[2:3])


def _pack_rows(parts):
    rows = [p.reshape(-1, LANES) for p in parts]
    total = sum(r.shape[0] for r in rows)
    padded = -(-total // (8 * N_DEV)) * (8 * N_DEV)
    if padded > total:
        rows.append(jnp.zeros((padded - total, LANES), f32))
    return jnp.concatenate(rows, axis=0)


def _unpack_rows(buf, shapes):
    out, r = [], 0
    for shp in shapes:
        n = 1
        for d in shp:
            n *= d
        out.append(buf[r:r + n // LANES].reshape(shp))
        r += n // LANES
    return out


def kernel(x, c, norm_g, ada_w, ada_b, ab_w_in, ab_conv_w, ab_ln_g, ab_ln_b, ab_sgu_w, ab_sgu_b, ab_w_out, c_w_in, c_pool_w, c_pool_scale, c_w_out, final_g, loss_target, m_norm_g, m_ada_w, m_ada_b, m_ab_w_in, m_ab_conv_w, m_ab_ln_g, m_ab_ln_b, m_ab_sgu_w, m_ab_sgu_b, m_ab_w_out, m_c_w_in, m_c_pool_w, m_c_pool_scale, m_c_w_out, m_final_g, v_norm_g, v_ada_w, v_ada_b, v_ab_w_in, v_ab_conv_w, v_ab_ln_g, v_ab_ln_b, v_ab_sgu_w, v_ab_sgu_b, v_ab_w_out, v_c_w_in, v_c_pool_w, v_c_pool_scale, v_c_w_out, v_final_g):
    ix, iy, ic = _place()
    chip, dev = 2 * ix + iy, 4 * ix + 2 * iy + ic
    n_even, n_odd = ab_w_in.shape[0], c_w_in.shape[0]
    depth = n_even + n_odd
    acols = ada_w.shape[2]

    place = jnp.stack([chip, ic]).astype(jnp.int32)
    even_names, odd_names = ["ab_w_in", "ab_w_out"], ["c_w_in", "c_pool_w", "c_w_out"]
    params = {"ab_w_in": (ab_w_in, m_ab_w_in, v_ab_w_in), "ab_w_out": (ab_w_out, m_ab_w_out, v_ab_w_out),
              "c_w_in": (c_w_in, m_c_w_in, v_c_w_in), "c_w_out": (c_w_out, m_c_w_out, v_c_w_out),
              "c_pool_w": tuple(a.reshape(n_odd, GC, GC) for a in (c_pool_w, m_c_pool_w, v_c_pool_w))}

    def placed(names, layer, after=None):
        ws = [params[nm][0] for nm in names]
        return [p.reshape(4, 2, p.shape[1] // 2, p.shape[2]) for p in _cast_place(place, ws, layer, after)]

    def whole(arrays):
        return [g.reshape(4, 2 * g.shape[2], g.shape[3]) for g in arrays]

    first = _gather8(jnp.concatenate([c, ab_conv_w.reshape(1, -1), c_pool_scale.reshape(1, -1)], axis=1), "gather_c")
    c_all, small_all = first[:, 0, :D], first[0::2, 0, D:]
    sems_a, in_a, tok = _ag_start([placed(even_names[:1], 0)], first[0:1, 0, 0:LANES], "ag_start_0a")
    modp = _ada_fwd(c_all, ada_w)
    later = [placed(even_names[1:], 0, tok)]
    later += [placed(even_names if i % 2 == 0 else odd_names, i // 2, tok) for i in range(1, depth)]
    modg = _gather8(modp + tok[0:1, 0:1], "gather_mod", [lay[-1] for lay in later])
    mod_rows = lax.dynamic_index_in_dim(modg[0::2], dev, axis=2, keepdims=False)
    mod = jnp.transpose(mod_rows, (1, 0, 2)).reshape(depth, 3 * D) + ada_b
    mods = [(mod[i:i + 1, 0:D], mod[i:i + 1, D:2 * D], mod[i:i + 1, 2 * D:3 * D]) for i in range(depth)]

    def shard_cols(a, width):
        return lax.dynamic_slice_in_dim(a, chip * width, width, axis=a.ndim - 1)

    n_conv = ab_conv_w.size
    conv_all = small_all[:, :n_conv].reshape(4, n_even, 3, D // 4)
    conv_full = jnp.transpose(conv_all, (1, 2, 0, 3)).reshape(n_even, 3, D)
    scale_all = small_all[:, n_conv:].reshape(4, n_odd, 2 * D // 4)
    scale_full = jnp.transpose(scale_all, (1, 0, 2)).reshape(n_odd, 2 * D)

    gathers_done = mod[0:1, 0:LANES] + scale_full[0:1, 0:LANES]
    sems_b, in_b, tok = _ag_start(later[:1], gathers_done, "ag_start_0b")
    sems_r, in_r, tok = _ag_start(later[1:], tok, "ag_start_rest")

    x_cur, saved, weights, handoff = x[0], [], [], {}
    sems_f, in_f, tok = _agf_start(_ag_wait(in_a[0], sems_a[0], tok, "ag_wait_0a"), "agf_start_0")
    hb = _hnorm(x_cur, norm_g[0:1], mods[0][0] + tok[0:1, 0:1], mods[0][1])
    for i in range(depth):
        j = i // 2
        if i == 0:
            full = whole(_agf_wait(sems_f, in_f, hb, "agf_wait_0")) + [None]
        else:
            full = whole(_agf_wait(*handoff.pop(i), x_cur, f"agf_wait_{i}"))
        if i % 2 == 0:
            w = (full[0], full[1], conv_full[j], ab_ln_g[j:j + 1], ab_ln_b[j:j + 1], ab_sgu_w[j], ab_sgu_b[j])
        else:
            w = (full[0], full[1], full[2], scale_full[j:j + 1])

        def before_out(y2, i=i):
            w_out, tok = None, None
            if i == 0:
                w_out = whole(_ag_forward(_ag_wait(in_b[0], sems_b[0], y2, "ag_wait_0b"), "ag_forward"))[0]
            if i + 1 < depth:
                arrived = _ag_wait(in_r[i], sems_r[i], y2, f"ag_wait_{i + 1}")
                sems_f, inflight, tok = _agf_start(arrived, f"agf_start_{i + 1}")
                handoff[i + 1] = (sems_f, inflight)
            return w_out, tok

        nxt = (norm_g[i + 1:i + 2], mods[i + 1][0], mods[i + 1][1]) if i + 1 < depth else None
        x_cur, hb, sv, w = _layer_fwd(i % 2 == 0, x_cur, hb, mods[i][2], w, nxt, before_out)
        weights.append(w)
        saved.append(sv)
    gin, loss, dfinal_g, dob, dgate = _loss_bwd(x_cur, loss_target[0], final_g.reshape(1, D), saved[-1][4],
                                                mods[-1][2])

    stacked = {}

    def reduce_layer(i, sems, pairs, lands, after):
        pairs, slots = _rs_chip_wait(sems, pairs, lands, after, f"rs_chip_wait_{i}")
        half_sems, halves, _ = _rs_half_start(_rs_sum(place, pairs, slots), f"rs_half_start_{i}")
        return i, half_sems, halves

    def update_layer(i, half_sems, halves, after):
        names = even_names if i % 2 == 0 else odd_names
        grads = _rs_half_wait(half_sems, halves, after, f"rs_half_wait_{i}")
        items = [(params[nm][0], g.reshape(params[nm][0].shape[1:]), params[nm][1], params[nm][2], stacked.get(nm))
                 for nm, g in zip(names, grads)]
        for nm, res in zip(names, _adamw_layer(i // 2, items)):
            stacked[nm] = res
            updated.append(res[1])

    updated = []
    small_g, dmod, dnorm_g, pending, tok = [None] * depth, [None] * depth, [None] * depth, None, None
    exchanging = []
    for i in reversed(range(depth)):
        w = weights[i]
        if tok is not None:
            w = w[:2] + (w[2] + tok[0:1, 0:1],) + w[3:] if i % 2 == 0 else w[:3] + (w[3] + tok[0:1, 0:1],)
        below = (saved[i - 1][4], mods[i - 1][2]) if i > 0 else None

        def send(big_g, i=i):
            if exchanging:
                update_layer(*exchanging.pop(), big_g[0])
            big_g = [g.reshape(4, 2, g.shape[1] // 2, g.shape[2]) for g in big_g]
            sems, big_g, lands, tok = _rs_pair_start(big_g, f"rs_pair_start_{i}")
            return (sems, big_g, lands), tok

        gin, gate_bwd, sent, small_g[i], dmod[i], dnorm_g[i] = _layer_bwd(
            i % 2 == 0, gin, dob, dgate, saved[i], mods[i][1], norm_g[i:i + 1], w, below, send)
        if below is not None:
            dob, dgate = gate_bwd
        after = gin
        if i == 0:
            dmod_all = _gather8(jnp.stack(dmod).reshape(depth * 3 * D // LANES, LANES), "gather_dmod")
            after = dmod_all = dmod_all.reshape(N_DEV, depth, 3 * D)
        if i > 0:
            after, updated = [after] + updated, []
        else:
            after = [after]
        big_g, theirs = _rs_pair_wait(*sent, after, f"rs_pair_wait_{i}")
        pairs = _rs_add(place, big_g, theirs)
        sems, pairs, lands, tok = _rs_chip_start(pairs, f"rs_chip_start_{i}")
        if pending is not None:
            exchanging.append(reduce_layer(*pending, [tok]))
        pending = (i, sems, pairs, lands)
    grad_x = gin
    dnorm_g = jnp.concatenate(dnorm_g, axis=0)

    dmod_cols = jnp.transpose(shard_cols(dmod_all, acols), (1, 0, 2))
    r_ada_w = _ada_bwd(c_all.T, dmod_cols, ada_w, m_ada_w, v_ada_w)
    update_layer(*exchanging.pop(), r_ada_w[1])
    last = reduce_layer(*pending, [r_ada_w[1]] + updated)

    small_parts = [dnorm_g, dfinal_g,
                   jnp.stack([small_g[2 * j]["conv_w"] for j in range(n_even)]),
                   jnp.concatenate([small_g[2 * j]["ln_g"] for j in range(n_even)], axis=0),
                   jnp.concatenate([small_g[2 * j]["ln_b"] for j in range(n_even)], axis=0),
                   jnp.stack([small_g[2 * j]["sgu_b"] for j in range(n_even)]),
                   jnp.concatenate([small_g[2 * j + 1]["pool_scale"] for j in range(n_odd)], axis=0),
                   jnp.pad(loss, ((0, 7), (0, LANES - 1)))]
    small_shapes = [p.shape for p in small_parts]
    sgu_parts = [small_g[2 * j]["sgu_w"].reshape(NH * HEAD, HEAD) for j in range(n_even)]
    reduced = _allreduce8([_pack_rows(small_parts)] + sgu_parts, "allreduce_small", last[2][0])
    update_layer(*last, reduced[0])
    r_ab_w_in, r_ab_w_out, r_c_w_in, r_c_w_out = (stacked[nm] for nm in ("ab_w_in", "ab_w_out", "c_w_in", "c_w_out"))
    r_c_pool_w = tuple(a.reshape(c_pool_w.shape) for a in stacked["c_pool_w"])
    g_norm_g, g_final_g, g_conv_full, g_ln_g, g_ln_b, g_sgu_b, g_scale_full, loss_row = _unpack_rows(reduced[0],
                                                                                                     small_shapes)
    g_sgu_w = jnp.stack(reduced[1:])
    loss = loss_row[0, 0]
    g_conv = shard_cols(g_conv_full, D // 4)
    g_scale = shard_cols(g_scale_full, 2 * D // 4)

    def two_d(a):
        return a.reshape(-1, a.shape[-1])

    small = [(norm_g, g_norm_g, m_norm_g, v_norm_g),
             (ada_b, dmod_all, m_ada_b, v_ada_b),
             (two_d(ab_conv_w), two_d(g_conv), two_d(m_ab_conv_w), two_d(v_ab_conv_w)),
             (ab_ln_g, g_ln_g, m_ab_ln_g, v_ab_ln_g),
             (ab_ln_b, g_ln_b, m_ab_ln_b, v_ab_ln_b),
             (two_d(ab_sgu_w), two_d(g_sgu_w), two_d(m_ab_sgu_w), two_d(v_ab_sgu_w)),
             (two_d(ab_sgu_b), two_d(g_sgu_b), two_d(m_ab_sgu_b), two_d(v_ab_sgu_b)),
             (c_pool_scale, g_scale, m_c_pool_scale, v_c_pool_scale),
             (final_g.reshape(1, D), g_final_g, m_final_g.reshape(1, D), v_final_g.reshape(1, D))]
    small_res = _adamw_small(small)
    small_shapes_out = [norm_g.shape, ada_b.shape, ab_conv_w.shape, ab_ln_g.shape, ab_ln_b.shape, ab_sgu_w.shape,
                        ab_sgu_b.shape, c_pool_scale.shape, final_g.shape]
    (r_norm_g, r_ada_b, r_conv, r_ln_g, r_ln_b, r_sgu_w, r_sgu_b, r_scale, r_final_g) = [
        tuple(a.reshape(shp) for a in res) for res, shp in zip(small_res, small_shapes_out)]

    order = [r_norm_g, r_ada_w, r_ada_b, r_ab_w_in, r_conv, r_ln_g, r_ln_b, r_sgu_w, r_sgu_b, r_ab_w_out,
             r_c_w_in, r_c_pool_w, r_scale, r_c_w_out, r_final_g]
    outs = [loss, grad_x[None]]
    for field in range(4):
        outs += [r[field] for r in order]
    return tuple(outs)
```

```python
import functools

import jax
import jax.numpy as jnp
from jax import lax
from jax.experimental import pallas as pl
from jax.experimental.pallas import tpu as pltpu

f32, bf16 = jnp.float32, jnp.bfloat16

D = 1024
HEAD = 128
NH = 8
WINDOWS = (2, 4, 8, 16)
GC = 512
EPS = 1e-6
HALO_CONV = 8
HALO_POOL = 16
CHUNK_ROWS = 512
DH_WIDE = 1024
FWD_TILES = 2
N_DEV = 8
LANES = 128

ADAM_LR, ADAM_B1, ADAM_B2, ADAM_EPS, ADAM_WD, ADAM_STEP = 0.001, 0.9, 0.999, 1e-08, 0.01, 10

MESH = pl.DeviceIdType.MESH
ANY = pl.BlockSpec(memory_space=pl.ANY)
VMEM = pl.BlockSpec(memory_space=pltpu.VMEM)
MIB = 2 ** 20


def _pcall(body, *, name, out_shape, grid=None, in_specs=None, out_specs=None, scratch=(), vmem_mb=None,
           aliases=None, prefetch=0):
    kw = {}
    if prefetch:
        kw["grid_spec"] = pltpu.PrefetchScalarGridSpec(num_scalar_prefetch=prefetch, grid=grid, in_specs=in_specs,
                                                       out_specs=out_specs, scratch_shapes=list(scratch))
    else:
        if grid is not None:
            kw["grid"] = grid
        if in_specs is not None:
            kw["in_specs"] = in_specs
        if out_specs is not None:
            kw["out_specs"] = out_specs
        if scratch:
            kw["scratch_shapes"] = list(scratch)
    if aliases:
        kw["input_output_aliases"] = aliases
    params = pltpu.CompilerParams(vmem_limit_bytes=None if vmem_mb is None else vmem_mb * MIB)
    return pl.pallas_call(body, name=name, out_shape=out_shape, compiler_params=params, **kw)


def _sds(shape, dtype):
    return jax.ShapeDtypeStruct(tuple(shape), dtype)


def _sigmoid(z):
    return pl.reciprocal(1.0 + jnp.exp(-z), approx=True)


def _silu(z):
    return z * _sigmoid(z)


def _silu_and_grad(z):
    s = _sigmoid(z)
    return z * s, s * (1.0 + z * (1.0 - s))


def _place():
    return lax.axis_index("x"), lax.axis_index("y"), lax.axis_index("c")


def _gather8(blk, name, after=()):
    def body(x_ref, *rest):
        o_ref, ssem, rsem = rest[len(after):]
        x, y, c = _place()
        me = 4 * x + 2 * y + c
        o_ref[me] = x_ref[...]
        sends = []
        for k in range(1, N_DEV):
            px = 1 - x if k & 4 else x
            py = 1 - y if k & 2 else y
            pc = 1 - c if k & 1 else c
            cp = pltpu.make_async_remote_copy(src_ref=x_ref, dst_ref=o_ref.at[me], send_sem=ssem.at[k - 1],
                                              recv_sem=rsem.at[k - 1], device_id=(px, py, pc), device_id_type=MESH)
            cp.start()
            sends.append((cp, 4 * px + 2 * py + pc))
        for k, (cp, peer) in enumerate(sends):
            pltpu.make_async_remote_copy(src_ref=x_ref, dst_ref=o_ref.at[peer], send_sem=ssem.at[k],
                                         recv_sem=rsem.at[k], device_id=(x, y, c), device_id_type=MESH).wait_recv()
        for cp, _ in sends:
            cp.wait_send()

    return _pcall(body, name=name, out_shape=_sds((N_DEV,) + blk.shape, blk.dtype), in_specs=[VMEM] + [ANY] * len(after),
                  out_specs=VMEM,
                  scratch=[pltpu.SemaphoreType.DMA((N_DEV - 1,)), pltpu.SemaphoreType.DMA((N_DEV - 1,))])(blk, *after)


def _allreduce8(bufs, name, after=None):
    n, n_after = len(bufs), 0 if after is None else 1
    rbs = [b.shape[0] // N_DEV for b in bufs]
    assert all(rb * N_DEV == b.shape[0] and rb % (16 if b.dtype == bf16 else 8) == 0 for rb, b in zip(rbs, bufs))

    def body(*refs):
        refs = refs[:n] + refs[n + n_after:]
        xs, outs, stages = refs[:n], refs[n:2 * n], refs[2 * n:3 * n]
        ssem, rsem = refs[3 * n:]
        x, y, c = _place()
        me = 4 * x + 2 * y + c
        peers = []
        for k in range(1, N_DEV):
            px = 1 - x if k & 4 else x
            py = 1 - y if k & 2 else y
            pc = 1 - c if k & 1 else c
            peers.append(((px, py, pc), 4 * px + 2 * py + pc))

        def blk(t, ref, idx):
            return ref.at[pl.ds(pl.multiple_of(idx * rbs[t], 8), rbs[t]), :]

        def copy(t, phase, k, src, dst, dev):
            return pltpu.make_async_remote_copy(src_ref=src, dst_ref=dst, send_sem=ssem.at[t, phase, k],
                                                recv_sem=rsem.at[t, phase, k], device_id=dev, device_id_type=MESH)

        scatter = [copy(t, 0, k, blk(t, xs[t], pidx), stages[t].at[me], dev)
                   for t in range(n) for k, (dev, pidx) in enumerate(peers)]
        for cp in scatter:
            cp.start()
        gather = []
        for t in range(n):
            stages[t][me] = blk(t, xs[t], me)[...]
            for k, (dev, pidx) in enumerate(peers):
                copy(t, 0, k, blk(t, xs[t], pidx), stages[t].at[pidx], dev).wait_recv()
            total = stages[t][0].astype(f32)
            for j in range(1, N_DEV):
                total = total + stages[t][j].astype(f32)
            blk(t, outs[t], me)[...] = total.astype(outs[t].dtype)
            sends = [copy(t, 1, k, blk(t, outs[t], me), blk(t, outs[t], me), dev) for k, (dev, pidx) in enumerate(peers)]
            for cp in sends:
                cp.start()
            gather += sends
        for t in range(n):
            for k, (dev, pidx) in enumerate(peers):
                copy(t, 1, k, blk(t, outs[t], pidx), blk(t, outs[t], pidx), dev).wait_recv()
        for cp in scatter + gather:
            cp.wait_send()

    return _pcall(body, name=name, out_shape=[_sds(b.shape, b.dtype) for b in bufs], in_specs=[VMEM] * n + [ANY] * n_after,
                  out_specs=[VMEM] * n,
                  scratch=[pltpu.VMEM((N_DEV, rb, LANES), b.dtype) for rb, b in zip(rbs, bufs)]
                  + [pltpu.SemaphoreType.DMA((n, 2, N_DEV - 1)), pltpu.SemaphoreType.DMA((n, 2, N_DEV - 1))])(
                      *bufs, *([] if after is None else [after]))


def _other_chips(x, y):
    return [((1 - x, y), 2 * (1 - x) + y), ((x, 1 - y), 2 * x + (1 - y)), ((1 - x, 1 - y), 2 * (1 - x) + (1 - y))]


HBM = pl.BlockSpec(memory_space=pltpu.HBM)
SEM = pl.BlockSpec(memory_space=pltpu.SEMAPHORE)
EFFECT = pltpu.SideEffectType.DATAFLOW_SIDE_EFFECTING


def _in_hbm(a):
    return pltpu.with_memory_space_constraint(a, pltpu.HBM)


SIBLING_ID = 1


def _sibling_handshake():
    x, y, c = _place()
    barrier = pltpu.get_barrier_semaphore()
    pl.semaphore_signal(barrier, inc=1, device_id=(x, y, 1 - c), device_id_type=MESH)
    pl.semaphore_wait(barrier, 1)
    return x, y, c


def _ag_start(layers, after, name):
    flat = [t for lay in layers for t in lay]
    n, nl = len(flat), len(layers)

    def body(*refs):
        src = refs[:n]
        sems = refs[n + 1:n + 1 + 2 * nl]
        token = refs[-1]
        x, y, c = _place()
        s_me = 2 * x + y
        t = 0
        for i, lay in enumerate(layers):
            for k in range(len(lay)):
                for j, ((px, py), _) in enumerate(_other_chips(x, y)):
                    pltpu.make_async_remote_copy(src_ref=src[t].at[s_me, c], dst_ref=src[t].at[s_me, c],
                                                 send_sem=sems[2 * i].at[3 * k + j], recv_sem=sems[2 * i + 1].at[3 * k + j],
                                                 device_id=(px, py, c), device_id_type=MESH).start()
                t += 1
        token[...] = jnp.zeros_like(token)

    sem_shapes = [pltpu.SemaphoreType.DMA((3 * len(lay),)) for lay in layers for _ in range(2)]
    out_shape = sem_shapes + [pltpu.HBM(t.shape, t.dtype) for t in flat] + [_sds((8, LANES), f32)]
    outs = pl.pallas_call(
        body, name=name, out_shape=out_shape, in_specs=[HBM] * n + [ANY],
        out_specs=[SEM] * (2 * nl) + [HBM] * n + [VMEM], input_output_aliases={t: 2 * nl + t for t in range(n)},
        compiler_params=pltpu.CompilerParams(has_side_effects=EFFECT))(*[_in_hbm(t) for t in flat], after)
    sems = [(outs[2 * i], outs[2 * i + 1]) for i in range(nl)]
    thru, t = [], 2 * nl
    for lay in layers:
        thru.append(list(outs[t:t + len(lay)]))
        t += len(lay)
    return sems, thru, outs[-1]


def _ag_wait(inflight, sems, after, name):
    n = len(inflight)

    def body(*refs):
        src, ssem, rsem = refs[:n], refs[n], refs[n + 1]
        x, y, c = _place()
        s_me = 2 * x + y
        for k in range(n):
            for j, (_, s_p) in enumerate(_other_chips(x, y)):
                cp = pltpu.make_async_remote_copy(src_ref=src[k].at[s_me, c], dst_ref=src[k].at[s_p, c],
                                                  send_sem=ssem.at[3 * k + j], recv_sem=rsem.at[3 * k + j],
                                                  device_id=(x, y, c), device_id_type=MESH)
                cp.wait_send()
                cp.wait_recv()

    return pl.pallas_call(
        body, name=name, out_shape=[pltpu.HBM(t.shape, t.dtype) for t in inflight],
        in_specs=[HBM] * n + [SEM, SEM, ANY], out_specs=[HBM] * n, input_output_aliases={t: t for t in range(n)},
        compiler_params=pltpu.CompilerParams(has_side_effects=EFFECT))(*inflight, sems[0], sems[1], after)


def _ag_forward(arrived, name):
    n = len(arrived)

    def body(*refs):
        o = refs[n:2 * n]
        ssem, rsem = refs[2 * n:]
        x, y, c = _place()

        def copy(t, j, s, half, dev):
            return pltpu.make_async_remote_copy(src_ref=o[t].at[s, c], dst_ref=o[t].at[s, half], send_sem=ssem.at[t, j],
                                                recv_sem=rsem.at[t, j], device_id=dev, device_id_type=MESH)

        chips = _other_chips(x, y)
        sends = [copy(t, j, s_p, c, (x, y, 1 - c)) for t in range(n) for j, (_, s_p) in enumerate(chips)]
        for cp in sends:
            cp.start()
        for t in range(n):
            for j, (_, s_p) in enumerate(chips):
                copy(t, j, s_p, 1 - c, (x, y, c)).wait_recv()
        for cp in sends:
            cp.wait_send()

    return _pcall(body, name=name, out_shape=[_sds(p.shape, bf16) for p in arrived], in_specs=[ANY] * n,
                  out_specs=[ANY] * n, aliases={t: t for t in range(n)},
                  scratch=[pltpu.SemaphoreType.DMA((n, 3)), pltpu.SemaphoreType.DMA((n, 3))])(*arrived)


def _agf_start(arrived, name):
    n = len(arrived)

    def body(*refs):
        o = refs[:n]
        ssem, rsem, token = refs[n], refs[n + 1], refs[-1]
        x, y, c = _sibling_handshake()
        for t in range(n):
            for j, (_, s_p) in enumerate(_other_chips(x, y)):
                pltpu.make_async_remote_copy(src_ref=o[t].at[s_p, c], dst_ref=o[t].at[s_p, c],
                                             send_sem=ssem.at[3 * t + j], recv_sem=rsem.at[3 * t + j],
                                             device_id=(x, y, 1 - c), device_id_type=MESH).start()
        token[...] = jnp.zeros_like(token)

    out_shape = ([pltpu.SemaphoreType.DMA((3 * n,))] * 2 + [pltpu.HBM(a.shape, bf16) for a in arrived]
                 + [_sds((8, LANES), f32)])
    outs = pl.pallas_call(
        body, name=name, out_shape=out_shape, in_specs=[HBM] * n, out_specs=[SEM, SEM] + [HBM] * n + [VMEM],
        input_output_aliases={t: 2 + t for t in range(n)},
        compiler_params=pltpu.CompilerParams(has_side_effects=EFFECT, collective_id=SIBLING_ID))(
            *[_in_hbm(a) for a in arrived])
    return (outs[0], outs[1]), list(outs[2:2 + n]), outs[-1]


def _agf_wait(sems, inflight, after, name):
    n = len(inflight)

    def body(*refs):
        o, ssem, rsem = refs[:n], refs[n], refs[n + 1]
        x, y, c = _place()
        for t in range(n):
            for j, (_, s_p) in enumerate(_other_chips(x, y)):
                cp = pltpu.make_async_remote_copy(src_ref=o[t].at[s_p, c], dst_ref=o[t].at[s_p, 1 - c],
                                                  send_sem=ssem.at[3 * t + j], recv_sem=rsem.at[3 * t + j],
                                                  device_id=(x, y, c), device_id_type=MESH)
                cp.wait_send()
                cp.wait_recv()

    return pl.pallas_call(
        body, name=name, out_shape=[pltpu.HBM(a.shape, bf16) for a in inflight],
        in_specs=[HBM] * n + [SEM, SEM, ANY], out_specs=[HBM] * n, input_output_aliases={t: t for t in range(n)},
        compiler_params=pltpu.CompilerParams(has_side_effects=EFFECT))(*inflight, sems[0], sems[1], after)


def _rs_pair_start(grads, name):
    n = len(grads)

    def body(*refs):
        g, theirs = refs[:n], refs[n:2 * n]
        ssem, rsem, token = refs[2 * n], refs[2 * n + 1], refs[-1]
        x, y, c = _sibling_handshake()
        for t in range(n):
            pltpu.make_async_remote_copy(src_ref=g[t].at[:, 1 - c], dst_ref=theirs[t], send_sem=ssem.at[t],
                                         recv_sem=rsem.at[t], device_id=(x, y, 1 - c), device_id_type=MESH).start()
        token[...] = jnp.zeros_like(token)

    lands = [lax.empty((4,) + g.shape[2:], bf16) for g in grads]
    out_shape = ([pltpu.SemaphoreType.DMA((n,))] * 2 + [pltpu.HBM(g.shape, bf16) for g in grads]
                 + [pltpu.HBM(q.shape, bf16) for q in lands] + [_sds((8, LANES), f32)])
    outs = pl.pallas_call(
        body, name=name, out_shape=out_shape, in_specs=[HBM] * (2 * n), out_specs=[SEM, SEM] + [HBM] * (2 * n) + [VMEM],
        input_output_aliases={t: 2 + t for t in range(2 * n)},
        compiler_params=pltpu.CompilerParams(has_side_effects=EFFECT, collective_id=SIBLING_ID))(
            *[_in_hbm(a) for a in list(grads) + lands])
    return (outs[0], outs[1]), list(outs[2:2 + n]), list(outs[2 + n:2 + 2 * n]), outs[-1]


def _rs_pair_wait(sems, grads, lands, after, name):
    n = len(grads)

    def body(*refs):
        g, theirs = refs[:n], refs[n:2 * n]
        ssem, rsem = refs[2 * n], refs[2 * n + 1]
        x, y, c = _place()
        for t in range(n):
            cp = pltpu.make_async_remote_copy(src_ref=g[t].at[:, 1 - c], dst_ref=theirs[t], send_sem=ssem.at[t],
                                              recv_sem=rsem.at[t], device_id=(x, y, c), device_id_type=MESH)
            cp.wait_send()
            cp.wait_recv()

    outs = pl.pallas_call(
        body, name=name, out_shape=[pltpu.HBM(a.shape, bf16) for a in list(grads) + list(lands)],
        in_specs=[HBM] * (2 * n) + [SEM, SEM] + [ANY] * len(after), out_specs=[HBM] * (2 * n),
        input_output_aliases={t: t for t in range(2 * n)},
        compiler_params=pltpu.CompilerParams(has_side_effects=EFFECT))(*grads, *lands, sems[0], sems[1], *after)
    return list(outs[:n]), list(outs[n:])


def _rs_chip_start(pairs, name):
    n = len(pairs)

    def body(*refs):
        p, q = refs[:n], refs[n:2 * n]
        ssem, rsem, token = refs[2 * n], refs[2 * n + 1], refs[-1]
        x, y, c = _place()
        for t in range(n):
            for j, ((px, py), s_p) in enumerate(_other_chips(x, y)):
                pltpu.make_async_remote_copy(src_ref=p[t].at[s_p], dst_ref=q[t].at[j], send_sem=ssem.at[3 * t + j],
                                             recv_sem=rsem.at[3 * t + j], device_id=(px, py, c), device_id_type=MESH).start()
        token[...] = jnp.zeros_like(token)

    lands = [lax.empty((3,) + p.shape[1:], bf16) for p in pairs]
    out_shape = ([pltpu.SemaphoreType.DMA((3 * n,))] * 2 + [pltpu.HBM(p.shape, bf16) for p in pairs]
                 + [pltpu.HBM(q.shape, bf16) for q in lands] + [_sds((8, LANES), f32)])
    outs = pl.pallas_call(
        body, name=name, out_shape=out_shape, in_specs=[HBM] * (2 * n), out_specs=[SEM, SEM] + [HBM] * (2 * n) + [VMEM],
        input_output_aliases={t: 2 + t for t in range(2 * n)},
        compiler_params=pltpu.CompilerParams(has_side_effects=EFFECT))(*[_in_hbm(a) for a in list(pairs) + lands])
    return (outs[0], outs[1]), list(outs[2:2 + n]), list(outs[2 + n:2 + 2 * n]), outs[-1]


def _rs_chip_wait(sems, pairs, lands, after, name):
    n = len(pairs)

    def body(*refs):
        p, q = refs[:n], refs[n:2 * n]
        ssem, rsem = refs[2 * n], refs[2 * n + 1]
        x, y, c = _place()
        for t in range(n):
            for j, (_, s_p) in enumerate(_other_chips(x, y)):
                cp = pltpu.make_async_remote_copy(src_ref=p[t].at[s_p], dst_ref=q[t].at[j], send_sem=ssem.at[3 * t + j],
                                                  recv_sem=rsem.at[3 * t + j], device_id=(x, y, c), device_id_type=MESH)
                cp.wait_send()
                cp.wait_recv()

    outs = pl.pallas_call(
        body, name=name, out_shape=[pltpu.HBM(a.shape, bf16) for a in list(pairs) + list(lands)],
        in_specs=[HBM] * (2 * n) + [SEM, SEM] + [ANY] * len(after), out_specs=[HBM] * (2 * n),
        input_output_aliases={t: t for t in range(2 * n)},
        compiler_params=pltpu.CompilerParams(has_side_effects=EFFECT))(*pairs, *lands, sems[0], sems[1], *after)
    return list(outs[:n]), list(outs[n:])


def _rs_half_start(halves, name):
    n = len(halves)

    def body(*refs):
        o = refs[:n]
        ssem, rsem, token = refs[n], refs[n + 1], refs[-1]
        x, y, c = _sibling_handshake()
        for t in range(n):
            pltpu.make_async_remote_copy(src_ref=o[t].at[c], dst_ref=o[t].at[c], send_sem=ssem.at[t],
                                         recv_sem=rsem.at[t], device_id=(x, y, 1 - c), device_id_type=MESH).start()
        token[...] = jnp.zeros_like(token)

    out_shape = ([pltpu.SemaphoreType.DMA((n,))] * 2 + [pltpu.HBM(h.shape, h.dtype) for h in halves]
                 + [_sds((8, LANES), f32)])
    outs = pl.pallas_call(
        body, name=name, out_shape=out_shape, in_specs=[HBM] * n, out_specs=[SEM, SEM] + [HBM] * n + [VMEM],
        input_output_aliases={t: 2 + t for t in range(n)},
        compiler_params=pltpu.CompilerParams(has_side_effects=EFFECT, collective_id=SIBLING_ID))(
            *[_in_hbm(h) for h in halves])
    return (outs[0], outs[1]), list(outs[2:2 + n]), outs[-1]


def _rs_half_wait(sems, inflight, after, name):
    n = len(inflight)

    def body(*refs):
        o, ssem, rsem = refs[:n], refs[n], refs[n + 1]
        x, y, c = _place()
        for t in range(n):
            cp = pltpu.make_async_remote_copy(src_ref=o[t].at[c], dst_ref=o[t].at[1 - c], send_sem=ssem.at[t],
                                              recv_sem=rsem.at[t], device_id=(x, y, c), device_id_type=MESH)
            cp.wait_send()
            cp.wait_recv()

    return pl.pallas_call(
        body, name=name, out_shape=[pltpu.HBM(h.shape, h.dtype) for h in inflight],
        in_specs=[HBM] * n + [SEM, SEM, ANY], out_specs=[HBM] * n, input_output_aliases={t: t for t in range(n)},
        compiler_params=pltpu.CompilerParams(has_side_effects=EFFECT))(*inflight, sems[0], sems[1], after)


def _row_spec(tm, cols):
    return pl.BlockSpec((tm, cols), lambda i: (i, 0))


def _vec_spec(cols, rows=1):
    return pl.BlockSpec((rows, cols), lambda i: (0, 0))


def _modulated_norm(xv, g, shift, scale):
    r = lax.rsqrt(jnp.mean(xv * xv, axis=-1, keepdims=True) + EPS)
    return (((xv * r) * g) * (1.0 + scale) + shift).astype(bf16)


def _hnorm(x, g, shift, scale):
    T, tm = x.shape[0], 256

    def body(x_ref, g_ref, sh_ref, sc_ref, h_ref):
        h_ref[...] = _modulated_norm(x_ref[...], g_ref[...], sh_ref[...], sc_ref[...])

    return _pcall(body, name="hnorm", out_shape=_sds((T, D), bf16), grid=(T // tm,),
                  in_specs=[_row_spec(tm, D), _vec_spec(D), _vec_spec(D), _vec_spec(D)],
                  out_specs=_row_spec(tm, D))(x, g, shift, scale)


def _out_proj(y2, wo, x, gate, nxt=None):
    T, tm = x.shape[0], 512

    def body(y_ref, w_ref, x_ref, g_ref, *rest):
        o = jnp.dot(y_ref[0], w_ref[0], preferred_element_type=f32)
        o = o + jnp.dot(y_ref[1], w_ref[1], preferred_element_type=f32)
        xo = x_ref[...] + g_ref[...] * o
        if nxt is None:
            xo_ref, o_ref = rest
        else:
            ng_ref, nsh_ref, nsc_ref, xo_ref, o_ref, h_ref = rest
            h_ref[...] = _modulated_norm(xo, ng_ref[...], nsh_ref[...], nsc_ref[...])
        o_ref[...] = o.astype(bf16)
        xo_ref[...] = xo

    extra = [] if nxt is None else list(nxt)
    n_out = 2 if nxt is None else 3
    return _pcall(body, name="out_proj", out_shape=[_sds((T, D), f32), _sds((T, D), bf16), _sds((T, D), bf16)][:n_out],
                  grid=(T // tm,),
                  in_specs=[pl.BlockSpec((2, tm, D), lambda i: (0, i, 0)), pl.BlockSpec((2, D, D), lambda i: (0, 0, 0)),
                            _row_spec(tm, D), _vec_spec(D)] + [_vec_spec(D)] * len(extra),
                  out_specs=[_row_spec(tm, D)] * n_out, vmem_mb=40)(y2, wo, x, gate, *extra)


def _gate_bwd_tile(dx, o_ref, gate_ref, dob_ref, dgate_ref):
    dob_ref[...] = (dx * gate_ref[...]).astype(bf16)
    dgate_ref[...] += jnp.sum(dx * o_ref[...].astype(f32), axis=0, keepdims=True)


def _loss_bwd(x, target, g, o, gate):
    T, tm = x.shape[0], 512

    def body(x_ref, t_ref, g_ref, o_ref, gate_ref, dx_ref, loss_ref, dg_ref, dob_ref, dgate_ref):
        @pl.when(pl.program_id(0) == 0)
        def _():
            loss_ref[...] = jnp.zeros_like(loss_ref)
            dg_ref[...] = jnp.zeros_like(dg_ref)
            dgate_ref[...] = jnp.zeros_like(dgate_ref)

        xv, gv = x_ref[...], g_ref[...]
        r = lax.rsqrt(jnp.mean(xv * xv, axis=-1, keepdims=True) + EPS)
        xn = xv * r
        err = xn * gv - t_ref[...]
        dy = err * (1.0 / D)
        dxn = dy * gv
        dx = r * (dxn - xn * jnp.mean(dxn * xn, axis=-1, keepdims=True))
        dx_ref[...] = dx
        dg_ref[...] += jnp.sum(dy * xn, axis=0, keepdims=True)
        loss_ref[...] += (0.5 / D) * jnp.sum(jnp.sum(err * err, axis=1, keepdims=True), axis=0, keepdims=True)
        _gate_bwd_tile(dx, o_ref, gate_ref, dob_ref, dgate_ref)

    return _pcall(body, name="loss_bwd",
                  out_shape=[_sds((T, D), f32), _sds((1, 1), f32), _sds((1, D), f32), _sds((T, D), bf16), _sds((1, D), f32)],
                  grid=(T // tm,),
                  in_specs=[_row_spec(tm, D), _row_spec(tm, D), _vec_spec(D), _row_spec(tm, D), _vec_spec(D)],
                  out_specs=[_row_spec(tm, D), pl.BlockSpec((1, 1), lambda i: (0, 0)), _vec_spec(D), _row_spec(tm, D),
                             _vec_spec(D)])(x, target, g, o, gate)


def _norm_bwd(x, dh, gin, g, scale, below=None):
    T, tm = x.shape[0], 256

    def body(x_ref, dh_ref, gin_ref, g_ref, sc_ref, *rest):
        if below is None:
            dx_ref, st_ref = rest
        else:
            o_ref, gate_ref, dx_ref, st_ref, dob_ref, dgate_ref = rest

        @pl.when(pl.program_id(0) == 0)
        def _():
            st_ref[...] = jnp.zeros_like(st_ref)
            if below is not None:
                dgate_ref[...] = jnp.zeros_like(dgate_ref)

        xv, gv, dhv = x_ref[...], g_ref[...], dh_ref[...]
        r = lax.rsqrt(jnp.mean(xv * xv, axis=-1, keepdims=True) + EPS)
        xn = xv * r
        da = dhv * (1.0 + sc_ref[...])
        dxn = da * gv
        dx = gin_ref[...] + r * (dxn - xn * jnp.mean(dxn * xn, axis=-1, keepdims=True))
        dx_ref[...] = dx
        st_ref[0:1, :] += jnp.sum(dhv, axis=0, keepdims=True)
        st_ref[1:2, :] += jnp.sum(dhv * (xn * gv), axis=0, keepdims=True)
        st_ref[2:3, :] += jnp.sum(da * xn, axis=0, keepdims=True)
        if below is not None:
            _gate_bwd_tile(dx, o_ref, gate_ref, dob_ref, dgate_ref)

    out_shape = [_sds((T, D), f32), _sds((8, D), f32)]
    in_specs = [_row_spec(tm, D), _row_spec(tm, D), _row_spec(tm, D), _vec_spec(D), _vec_spec(D)]
    out_specs = [_row_spec(tm, D), _vec_spec(D, 8)]
    args = [x, dh, gin, g, scale]
    if below is not None:
        out_shape += [_sds((T, D), bf16), _sds((1, D), f32)]
        in_specs += [_row_spec(tm, D), _vec_spec(D)]
        out_specs += [_row_spec(tm, D), _vec_spec(D)]
        args += list(below)
    return _pcall(body, name="norm_bwd", out_shape=out_shape, grid=(T // tm,), in_specs=in_specs,
                  out_specs=out_specs)(*args)


STEPS = 4
ADAMW_STEPS = 8


def _cast_place(place, ws, layer, after=None):
    n = len(ws)

    def body(place_ref, *refs):
        for t in range(n):
            refs[-n + t][...] = refs[t][...].astype(bf16)

    def tile(w):
        return w.shape[1] // STEPS, w.shape[2]

    extra = [] if after is None else [after]
    return _pcall(body, name="cast_place", out_shape=[_sds((4,) + w.shape[1:], bf16) for w in ws], grid=(STEPS,),
                  prefetch=1,
                  in_specs=[pl.BlockSpec((None,) + tile(w), lambda i, pr: (layer, i, 0)) for w in ws] + [ANY] * len(extra),
                  out_specs=[pl.BlockSpec((None,) + tile(w), lambda i, pr: (pr[0], i, 0)) for w in ws])(
                      place, *ws, *extra)


def _rs_add(place, grads, theirs):
    n = len(grads)

    def body(place_ref, *refs):
        for t in range(n):
            refs[2 * n + t][...] = (refs[t][...].astype(f32) + refs[n + t][...].astype(f32)).astype(bf16)

    def tile(q):
        return q.shape[1] // 2, q.shape[2]

    mine = [pl.BlockSpec((None, None) + tile(q), lambda s, i, pr: (s, pr[1], i, 0)) for q in theirs]
    shard = [pl.BlockSpec((None,) + tile(q), lambda s, i, pr: (s, i, 0)) for q in theirs]
    return _pcall(body, name="rs_add", out_shape=[_sds(q.shape, bf16) for q in theirs], grid=(4, 2), prefetch=1,
                  in_specs=mine + shard, out_specs=shard)(place, *grads, *theirs)


def _rs_sum(place, pairs, slots):
    n, steps = len(pairs), 4

    def body(place_ref, *refs):
        for t in range(n):
            p_ref, q_ref = refs[t], refs[n + t]
            total = ((p_ref[...].astype(f32) + q_ref[0].astype(f32)) + q_ref[1].astype(f32)) + q_ref[2].astype(f32)
            refs[2 * n + t][...] = total.astype(bf16)

    def tile(q):
        return q.shape[1] // steps, q.shape[2]

    return _pcall(body, name="rs_sum", out_shape=[_sds((2,) + q.shape[1:], bf16) for q in slots], grid=(steps,),
                  prefetch=1,
                  in_specs=[pl.BlockSpec((None,) + tile(q), lambda i, pr: (pr[0], i, 0)) for q in slots]
                  + [pl.BlockSpec((3,) + tile(q), lambda i, pr: (0, i, 0)) for q in slots],
                  out_specs=[pl.BlockSpec((None,) + tile(q), lambda i, pr: (pr[1], i, 0)) for q in slots])(
                      place, *pairs, *slots)


def _adamw_math(w, g, m, v):
    m = ADAM_B1 * m + (1.0 - ADAM_B1) * g
    v = ADAM_B2 * v + (1.0 - ADAM_B2) * jnp.square(g)
    m_hat = m / (1.0 - ADAM_B1 ** ADAM_STEP)
    v_hat = v / (1.0 - ADAM_B2 ** ADAM_STEP)
    delta = -ADAM_LR * (m_hat / (jnp.sqrt(v_hat) + ADAM_EPS) + ADAM_WD * w)
    return delta, m, v


def _adamw_layer(layer, items):
    n = len(items)

    def body(*refs):
        outs = refs[-4 * n:]
        for t in range(n):
            w_ref, g_ref, m_ref, v_ref = refs[4 * t:4 * t + 4]
            g = g_ref[...].astype(f32)
            outs[4 * t][...] = g
            outs[4 * t + 1][...], outs[4 * t + 2][...], outs[4 * t + 3][...] = _adamw_math(
                w_ref[...], g, m_ref[...], v_ref[...])

    args, in_specs, out_specs, out_shape = [], [], [], []
    for w, g, m, v, _ in items:
        tr, cols = w.shape[1] // ADAMW_STEPS, w.shape[2]
        spec = pl.BlockSpec((None, tr, cols), lambda i: (layer, i, 0))
        args += [w, g, m, v]
        in_specs += [spec, pl.BlockSpec((tr, cols), lambda i: (i, 0)), spec, spec]
        out_specs += [spec] * 4
        out_shape += [_sds(w.shape, f32)] * 4
    aliases = {}
    for t, it in enumerate(items):
        if it[4] is not None:
            for k in range(4):
                aliases[len(args)] = 4 * t + k
                args.append(it[4][k])
                in_specs.append(ANY)
    res = _pcall(body, name="adamw", out_shape=out_shape, grid=(ADAMW_STEPS,), in_specs=in_specs, out_specs=out_specs,
                 aliases=aliases)(*args)
    return [tuple(res[4 * t:4 * t + 4]) for t in range(n)]


def _adamw_small(items):
    n = len(items)

    def body(*refs):
        ins, outs = refs[:4 * n], refs[4 * n:]
        for t in range(n):
            w_ref, g_ref, m_ref, v_ref = ins[4 * t:4 * t + 4]
            if len(g_ref.shape) == len(w_ref.shape) + 1:
                g = g_ref[0]
                for b in range(1, g_ref.shape[0]):
                    g = g + g_ref[b]
            else:
                g = g_ref[...]
            d, m, v = _adamw_math(w_ref[...], g, m_ref[...], v_ref[...])
            outs[4 * t][...], outs[4 * t + 1][...], outs[4 * t + 2][...], outs[4 * t + 3][...] = g, d, m, v

    out_shape = [_sds(w.shape, f32) for (w, _, _, _) in items for _ in range(4)]
    flat = [a for it in items for a in it]
    res = _pcall(body, name="adamw_small", out_shape=out_shape, in_specs=[VMEM] * (4 * n),
                 out_specs=[VMEM] * (4 * n))(*flat)
    return [tuple(res[4 * t:4 * t + 4]) for t in range(n)]


NN = ((1,), (0,))
NT = ((1,), (1,))
TN = ((0,), (0,))


def _mm(name, a, b, *, grid, a_spec, b_spec, out_shape, out_spec, dims, vmem_mb=None):
    def body(a_ref, b_ref, o_ref):
        r = lax.dot_general(a_ref[...], b_ref[...], (dims, ((), ())), preferred_element_type=f32)
        o_ref[...] = r.astype(o_ref.dtype)

    return _pcall(body, name=name, out_shape=out_shape, grid=grid, in_specs=[a_spec, b_spec], out_specs=out_spec,
                  vmem_mb=vmem_mb)(a, b)


def _whole(shape):
    return pl.BlockSpec(shape, lambda j: (0,) * len(shape))


def _split_spec(rows, tile, per_split):
    return pl.BlockSpec((None, rows, tile), lambda j: (j // per_split, 0, j % per_split))


class _Proj:
    def __init__(self, n, splits, tile):
        self.n, self.splits, self.tile = n, splits, tile
        self.steps = n // tile
        self.w_per = n // 4 // tile
        self.a_per = n // splits // tile
        assert self.w_per * tile * 4 == n and self.a_per * tile * splits == n

    def fwd(self, hb, wg):
        T = hb.shape[0]
        sub, tile, w_per = FWD_TILES, self.tile, self.w_per
        wide = sub * tile
        a_per = self.n // self.splits // wide
        assert a_per * wide * self.splits == self.n

        def w_tile(q):
            return pl.BlockSpec((None, D, tile), lambda j: ((sub * j + q) // w_per, 0, (sub * j + q) % w_per))

        def body(a_ref, *rest):
            w = jnp.concatenate([rest[q][...] for q in range(sub)], axis=1)
            rest[sub][...] = jnp.dot(a_ref[...], w, preferred_element_type=f32).astype(bf16)

        return _pcall(body, name="proj_fwd", out_shape=_sds((self.splits, T, self.n // self.splits), bf16),
                      grid=(self.n // wide,), in_specs=[_whole((T, D))] + [w_tile(q) for q in range(sub)],
                      out_specs=pl.BlockSpec((None, T, wide), lambda j: (j // a_per, 0, j % a_per)),
                      vmem_mb=40 if wide > 512 else None)(hb, *([wg] * sub))

    def dw(self, hb, dp):
        T = hb.shape[0]
        return _mm("proj_dw", hb, dp, grid=(self.steps,), a_spec=_whole((T, D)),
                   b_spec=_split_spec(T, self.tile, self.a_per), out_shape=_sds((4, D, self.n // 4), bf16),
                   out_spec=_split_spec(D, self.tile, self.w_per), dims=TN)

    def dh(self, dp, wg, after=None):
        T = dp.shape[1]
        extra = [] if after is None else [after]
        sub, tile, w_per = DH_WIDE // self.tile, self.tile, self.w_per
        a_per = self.n // self.splits // DH_WIDE
        assert sub * tile == DH_WIDE and a_per * DH_WIDE * self.splits == self.n

        def w_tile(q):
            return pl.BlockSpec((None, D, tile), lambda k: ((sub * k + q) // w_per, 0, (sub * k + q) % w_per))

        def body(a_ref, *rest):
            o_ref = rest[-1]
            w = jnp.concatenate([rest[q][...] for q in range(sub)], axis=1)
            r = lax.dot_general(a_ref[...], w, (NT, ((), ())), preferred_element_type=f32)

            @pl.when(pl.program_id(0) == 0)
            def _():
                o_ref[...] = r

            @pl.when(pl.program_id(0) > 0)
            def _():
                o_ref[...] += r

        return _pcall(body, name="proj_dh", out_shape=_sds((T, D), f32), grid=(self.n // DH_WIDE,),
                      in_specs=[pl.BlockSpec((None, T, DH_WIDE), lambda k: (k // a_per, 0, k % a_per))]
                      + [w_tile(q) for q in range(sub)] + [ANY] * len(extra),
                      out_specs=_whole((T, D)), vmem_mb=40)(dp, *([wg] * sub), *extra)


EVEN_PROJ = _Proj(7 * D, 7, 256)
ODD_PROJ = _Proj(4 * D, 2, 512)


def _out_bwd(dob, wo, y2):
    T = dob.shape[0]
    w_spec = pl.BlockSpec((None, 512, D), lambda j: (j, 0, 0))

    def body(dob_ref, w_ref, y_ref, dy_ref, dw_ref):
        dob_v = dob_ref[...]
        dy_ref[...] = lax.dot_general(dob_v, w_ref[...], (NT, ((), ())), preferred_element_type=f32).astype(bf16)
        dw_ref[...] = lax.dot_general(y_ref[...], dob_v, (TN, ((), ())), preferred_element_type=f32).astype(bf16)

    return _pcall(body, name="out_bwd", out_shape=[_sds((2, T, D), bf16), _sds((4, 512, D), bf16)], grid=(4,),
                  in_specs=[_whole((T, D)), w_spec, _split_spec(T, 512, 2)],
                  out_specs=[_split_spec(T, 512, 2), w_spec])(dob, wo, y2)


def _head_spec(lead, T):
    return pl.BlockSpec((lead, T, HEAD), lambda h: (0, 0, h))


def _head_vec(rows):
    return pl.BlockSpec((rows, HEAD), lambda h: (0, h))


_HEAD_MAT = pl.BlockSpec((None, HEAD, HEAD), lambda h: (h, 0, 0))


def _causal():
    return lax.broadcasted_iota(jnp.int32, (HEAD, HEAD), 0) >= lax.broadcasted_iota(jnp.int32, (HEAD, HEAD), 1)


def _layernorm_head(v):
    mu = jnp.mean(v, axis=-1, keepdims=True)
    d = v - mu
    rstd = lax.rsqrt(jnp.mean(d * d, axis=-1, keepdims=True) + EPS)
    return d * rstd, rstd


def _even_fwd(p7, conv_w, ln_g, ln_b, sgu_w, sgu_bias):
    T, C = p7.shape[1], CHUNK_ROWS

    def body(p_ref, cw_ref, lg_ref, lb_ref, w_ref, b_ref, y_ref):
        w0, w1, w2 = cw_ref[0:1, :], cw_ref[1:2, :], cw_ref[2:3, :]
        wm = jnp.where(_causal(), w_ref[...], 0.0).astype(bf16)
        bias, lg, lb = b_ref[...], lg_ref[...], lb_ref[...]

        def step(i, halo):
            rows = pl.ds(pl.multiple_of(i * C, C), C)
            ah, ab, ac, az, u, v, zb = (p_ref[k, rows, :].astype(f32) for k in range(7))
            tt = ac * ah
            ext = jnp.concatenate([halo, tt], axis=0)
            cv = w2 * tt + w1 * pltpu.roll(ext, 1, 0)[HALO_CONV:] + w0 * pltpu.roll(ext, 2, 0)[HALO_CONV:]
            y_ref[0, rows, :] = (ab * cv * _silu(az)).astype(bf16)
            vhat, _ = _layernorm_head(v)
            vn = (vhat * lg + lb).astype(bf16)
            mix = jnp.concatenate([jnp.dot(wm, vn[k * HEAD:(k + 1) * HEAD], preferred_element_type=f32) + bias
                                   for k in range(C // HEAD)], axis=0)
            y_ref[1, rows, :] = (u * mix * _silu(zb)).astype(bf16)
            return tt[C - HALO_CONV:]

        lax.fori_loop(0, T // C, step, jnp.zeros((HALO_CONV, HEAD), f32))

    return _pcall(body, name="even_fwd", out_shape=_sds((2, T, D), bf16), grid=(NH,),
                  in_specs=[_head_spec(7, T), _head_vec(3), _head_vec(1), _head_vec(1), _HEAD_MAT, _HEAD_MAT],
                  out_specs=_head_spec(2, T))(p7, conv_w, ln_g, ln_b, sgu_w, sgu_bias)


def _even_bwd(p7, dy2, conv_w, ln_g, ln_b, sgu_w, sgu_bias):
    T, C = p7.shape[1], CHUNK_ROWS
    n_chunks = T // C

    def body(p_ref, dy_ref, cw_ref, lg_ref, lb_ref, w_ref, b_ref,
             dp_ref, dcw_ref, dlg_ref, dlb_ref, dw_ref, dms_ref, dcv_s):
        w0, w1, w2 = cw_ref[0:1, :], cw_ref[1:2, :], cw_ref[2:3, :]
        tri = _causal()
        wm = jnp.where(tri, w_ref[...], 0.0).astype(bf16)
        bias, lg, lb = b_ref[...], lg_ref[...], lb_ref[...]
        dw_ref[...] = jnp.zeros_like(dw_ref)
        dms_ref[...] = jnp.zeros_like(dms_ref)

        def fwd_step(i, carry):
            halo, a0, a1, a2, alg, alb = carry
            rows = pl.ds(pl.multiple_of(i * C, C), C)
            ah, ab, ac, az = (p_ref[k, rows, :].astype(f32) for k in range(4))
            dya = dy_ref[0, rows, :].astype(f32)
            tt = ac * ah
            ext = jnp.concatenate([halo, tt], axis=0)
            t1, t2 = pltpu.roll(ext, 1, 0)[HALO_CONV:], pltpu.roll(ext, 2, 0)[HALO_CONV:]
            cv = w2 * tt + w1 * t1 + w0 * t2
            sa, dsa = _silu_and_grad(az)
            g1 = dya * sa
            dp_ref[1, rows, :] = (g1 * cv).astype(bf16)
            dp_ref[3, rows, :] = (dya * ab * cv * dsa).astype(bf16)
            dcv = g1 * ab
            dcv_s[rows, :] = dcv
            a2 = a2 + jnp.sum(dcv * tt, axis=0, keepdims=True)
            a1 = a1 + jnp.sum(dcv * t1, axis=0, keepdims=True)
            a0 = a0 + jnp.sum(dcv * t2, axis=0, keepdims=True)

            u, zb, dyb = p_ref[4, rows, :].astype(f32), p_ref[6, rows, :].astype(f32), dy_ref[1, rows, :].astype(f32)
            vhat, rstd = _layernorm_head(p_ref[5, rows, :].astype(f32))
            vn = (vhat * lg + lb).astype(bf16)
            sb, dsb = _silu_and_grad(zb)
            mix = jnp.concatenate([jnp.dot(wm, vn[k * HEAD:(k + 1) * HEAD], preferred_element_type=f32) + bias
                                   for k in range(C // HEAD)], axis=0)
            dp_ref[4, rows, :] = (dyb * mix * sb).astype(bf16)
            dp_ref[6, rows, :] = (dyb * u * mix * dsb).astype(bf16)
            dmix = dyb * u * sb
            dvn_parts = []
            for k in range(C // HEAD):
                dm = dmix[k * HEAD:(k + 1) * HEAD]
                dmb = dm.astype(bf16)
                dvn_parts.append(lax.dot_general(wm, dmb, (TN, ((), ())), preferred_element_type=f32))
                dw_ref[...] += lax.dot_general(dmb, vn[k * HEAD:(k + 1) * HEAD], (NT, ((), ())),
                                               preferred_element_type=f32)
                dms_ref[...] += dm
            dvn = jnp.concatenate(dvn_parts, axis=0)
            alg = alg + jnp.sum(dvn * vhat, axis=0, keepdims=True)
            alb = alb + jnp.sum(dvn, axis=0, keepdims=True)
            dvh = dvn * lg
            dv = rstd * (dvh - jnp.mean(dvh, axis=-1, keepdims=True)
                         - vhat * jnp.mean(dvh * vhat, axis=-1, keepdims=True))
            dp_ref[5, rows, :] = dv.astype(bf16)
            return tt[C - HALO_CONV:], a0, a1, a2, alg, alb

        zrow = jnp.zeros((1, HEAD), f32)
        _, a0, a1, a2, alg, alb = lax.fori_loop(
            0, n_chunks, fwd_step, (jnp.zeros((HALO_CONV, HEAD), f32), zrow, zrow, zrow, zrow, zrow))
        dcw_ref[0:1, :], dcw_ref[1:2, :], dcw_ref[2:3, :] = a0, a1, a2
        dlg_ref[...], dlb_ref[...] = alg, alb
        dw_ref[...] = jnp.where(tri, dw_ref[...], 0.0)

        def bwd_step(k, halo):
            rows = pl.ds(pl.multiple_of((n_chunks - 1 - k) * C, C), C)
            dcv = dcv_s[rows, :]
            ext = jnp.concatenate([dcv, halo], axis=0)
            n1 = pltpu.roll(ext, C + HALO_CONV - 1, 0)[:C]
            n2 = pltpu.roll(ext, C + HALO_CONV - 2, 0)[:C]
            dtt = w2 * dcv + w1 * n1 + w0 * n2
            dp_ref[2, rows, :] = (dtt * p_ref[0, rows, :].astype(f32)).astype(bf16)
            dp_ref[0, rows, :] = (dtt * p_ref[2, rows, :].astype(f32)).astype(bf16)
            return dcv[:HALO_CONV]

        lax.fori_loop(0, n_chunks, bwd_step, jnp.zeros((HALO_CONV, HEAD), f32))

    out_shape = [_sds((7, T, D), bf16), _sds((3, D), f32), _sds((1, D), f32), _sds((1, D), f32),
                 _sds((NH, HEAD, HEAD), f32), _sds((NH, HEAD, HEAD), f32)]
    return _pcall(body, name="even_bwd", out_shape=out_shape, grid=(NH,),
                  in_specs=[_head_spec(7, T), _head_spec(2, T), _head_vec(3), _head_vec(1), _head_vec(1),
                            _HEAD_MAT, _HEAD_MAT],
                  out_specs=[_head_spec(7, T), _head_vec(3), _head_vec(1), _head_vec(1), _HEAD_MAT, _HEAD_MAT],
                  scratch=[pltpu.VMEM((T, HEAD), f32)])(p7, dy2, conv_w, ln_g, ln_b, sgu_w, sgu_bias)


def _window_sum(ext, win, towards_past):
    n, k, s = ext.shape[0], 1, ext
    while k < win:
        s = s + pltpu.roll(s, k if towards_past else n - k, 0)
        k *= 2
    return s


def _pool_count(i, C, win):
    t = i * C + lax.broadcasted_iota(jnp.int32, (C, 1), 0)
    cnt = jnp.minimum(t + 1, win).astype(f32)
    return cnt, 1.0 / cnt


def _group_specs(T):
    p_spec = pl.BlockSpec((None, T, GC), lambda g: (0, 0, g))
    z_spec = pl.BlockSpec((None, T, GC), lambda g: (1, 0, g))
    pw_spec = pl.BlockSpec((4, GC // 4, GC), lambda g: (0, g, 0))
    ps_spec = pl.BlockSpec((1, GC), lambda g: (0, g))
    y_spec = pl.BlockSpec((None, T, GC), lambda g: (g // 2, 0, g % 2))
    return p_spec, z_spec, pw_spec, ps_spec, y_spec


def _odd_fwd(p2, pool_wg, pool_scale):
    T, C = p2.shape[1], CHUNK_ROWS
    p_spec, z_spec, pw_spec, ps_spec, y_spec = _group_specs(T)

    def body(p_ref, z_ref, pw_ref, ps_ref, y_ref):
        pw, ps = pw_ref[...].reshape(GC, GC), ps_ref[...]

        def run(win):
            def step(i, halo):
                rows = pl.ds(pl.multiple_of(i * C, C), C)
                p = p_ref[rows, :].astype(f32)
                s = _window_sum(jnp.concatenate([halo, p], axis=0), win, True)[HALO_POOL:]
                pooled = s * _pool_count(i, C, win)[1] - p
                ypre = jnp.dot(pooled.astype(bf16), pw, preferred_element_type=f32)
                y_ref[rows, :] = (ypre * ps * _silu(z_ref[rows, :].astype(f32))).astype(bf16)
                return p[C - HALO_POOL:]

            lax.fori_loop(0, T // C, step, jnp.zeros((HALO_POOL, GC), f32))

        for gi, win in enumerate(WINDOWS):
            pl.when(pl.program_id(0) == gi)(functools.partial(run, win))

    return _pcall(body, name="odd_fwd", out_shape=_sds((2, T, D), bf16), grid=(len(WINDOWS),),
                  in_specs=[p_spec, z_spec, pw_spec, ps_spec], out_specs=y_spec)(p2, p2, pool_wg, pool_scale)


def _odd_bwd(p2, dy2, pool_wg, pool_scale):
    T, C = p2.shape[1], CHUNK_ROWS
    n_chunks = T // C
    p_spec, z_spec, pw_spec, ps_spec, y_spec = _group_specs(T)

    def body(p_ref, z_ref, dy_ref, pw_ref, ps_ref, dp_ref, dpw_ref, dps_ref, q_s, acc_s):
        pw, ps = pw_ref[...].reshape(GC, GC), ps_ref[...]

        def run(win):
            acc_s[...] = jnp.zeros_like(acc_s)

            def fwd_step(i, carry):
                halo, aps = carry
                rows = pl.ds(pl.multiple_of(i * C, C), C)
                p, z, dy = p_ref[rows, :].astype(f32), z_ref[rows, :].astype(f32), dy_ref[rows, :].astype(f32)
                _, inv_cnt = _pool_count(i, C, win)
                s = _window_sum(jnp.concatenate([halo, p], axis=0), win, True)[HALO_POOL:]
                pb = (s * inv_cnt - p).astype(bf16)
                ypre = jnp.dot(pb, pw, preferred_element_type=f32)
                sz, dsz = _silu_and_grad(z)
                aps = aps + jnp.sum(dy * ypre * sz, axis=0, keepdims=True)
                dp_ref[1, rows, :] = (dy * ypre * ps * dsz).astype(bf16)
                dyp = (dy * ps * sz).astype(bf16)
                acc_s[...] += lax.dot_general(pb, dyp, (TN, ((), ())), preferred_element_type=f32)
                dpool = lax.dot_general(dyp, pw, (NT, ((), ())), preferred_element_type=f32)
                q_s[rows, :] = dpool * inv_cnt
                return p[C - HALO_POOL:], aps

            _, aps = lax.fori_loop(0, n_chunks, fwd_step, (jnp.zeros((HALO_POOL, GC), f32), jnp.zeros((1, GC), f32)))
            dps_ref[...] = aps
            dpw_ref[...] = acc_s[...].reshape(4, GC // 4, GC).astype(bf16)

            def bwd_step(k, halo):
                i = n_chunks - 1 - k
                rows = pl.ds(pl.multiple_of(i * C, C), C)
                q = q_s[rows, :]
                s = _window_sum(jnp.concatenate([q, halo], axis=0), win, False)[:C]
                dp_ref[0, rows, :] = (s - q * _pool_count(i, C, win)[0]).astype(bf16)
                return q[:HALO_POOL]

            lax.fori_loop(0, n_chunks, bwd_step, jnp.zeros((HALO_POOL, GC), f32))

        for gi, win in enumerate(WINDOWS):
            pl.when(pl.program_id(0) == gi)(functools.partial(run, win))

    out_shape = [_sds((2, T, 2 * D), bf16), _sds((4, GC, GC), bf16), _sds((1, 2 * D), f32)]
    return _pcall(body, name="odd_bwd", out_shape=out_shape, grid=(len(WINDOWS),),
                  in_specs=[p_spec, z_spec, y_spec, pw_spec, ps_spec],
                  out_specs=[pl.BlockSpec((2, T, GC), lambda g: (0, 0, g)), pw_spec, ps_spec],
                  scratch=[pltpu.VMEM((T, GC), f32), pltpu.VMEM((GC, GC), f32)], vmem_mb=44)(
                      p2, p2, dy2, pool_wg, pool_scale)


def _ada_fwd(c_all, ada_w):
    cols = ada_w.shape[2]

    def body(c_ref, w_ref, o_ref):
        o_ref[...] = jnp.dot(_silu(c_ref[...]), w_ref[...], preferred_element_type=f32,
                             precision=lax.Precision.HIGHEST)

    return _pcall(body, name="ada_fwd", out_shape=_sds((4, N_DEV, cols), f32), grid=(4,),
                  in_specs=[pl.BlockSpec((N_DEV, D), lambda i: (0, 0)), pl.BlockSpec((None, D, cols), lambda i: (i, 0, 0))],
                  out_specs=pl.BlockSpec((None, N_DEV, cols), lambda i: (i, 0, 0)))(c_all, ada_w)


def _ada_bwd(c_all_t, dmod, w, m, v):
    cols, tr = w.shape[2], 256
    spec = pl.BlockSpec((None, tr, cols), lambda l, i: (l, i, 0))

    def body(c_ref, dm_ref, w_ref, m_ref, v_ref, g_ref, d_ref, mo_ref, vo_ref):
        sc = _silu(c_ref[...])
        g = sc[:, 0:1] * dm_ref[0:1, :]
        for b in range(1, N_DEV):
            g = g + sc[:, b:b + 1] * dm_ref[b:b + 1, :]
        g_ref[...] = g
        d_ref[...], mo_ref[...], vo_ref[...] = _adamw_math(w_ref[...], g, m_ref[...], v_ref[...])

    return _pcall(body, name="ada_bwd", out_shape=[_sds(w.shape, f32)] * 4, grid=(4, D // tr),
                  in_specs=[pl.BlockSpec((tr, N_DEV), lambda l, i: (i, 0)),
                            pl.BlockSpec((None, N_DEV, cols), lambda l, i: (l, 0, 0)), spec, spec, spec],
                  out_specs=[spec] * 4)(c_all_t, dmod, w, m, v)


def _layer_fwd(even, x, hb, gate, w, nxt, before_out=None):
    if even:
        w_in, w_out, conv_w, ln_g, ln_b, sgu_w, sgu_b = w
        bias = jnp.broadcast_to(sgu_b[:, :, None], (NH, HEAD, HEAD))
        p = EVEN_PROJ.fwd(hb, w_in)
        y2 = _even_fwd(p, conv_w, ln_g, ln_b, sgu_w, bias)
    else:
        w_in, pool_w, w_out, pool_scale = w
        p = ODD_PROJ.fwd(hb, w_in)
        y2 = _odd_fwd(p, pool_w, pool_scale)
    if before_out is not None:
        late_w_out, tok = before_out(y2)
        if late_w_out is not None:
            w_out = late_w_out
            w = (w_in, w_out) + tuple(w[2:]) if even else (w_in, pool_w, w_out, pool_scale)
        if tok is not None:
            gate = gate + tok[0:1, 0:1]
    outs = _out_proj(y2, w_out.reshape(2, D, D), x, gate, nxt)
    return outs[0], (None if nxt is None else outs[2]), (x, hb, p, y2, outs[1]), w


def _layer_bwd(even, gin, dob, dgate, saved, scale, g, w, below=None, send=None):
    x_in, hb, p, y2, o = saved
    if even:
        w_in, w_out, conv_w, ln_g, ln_b, sgu_w, sgu_b = w
        bias = jnp.broadcast_to(sgu_b[:, :, None], (NH, HEAD, HEAD))
        dy2, dwo = _out_bwd(dob, w_out, y2)
        dp, dconv, dlg, dlb, dsw, dms = _even_bwd(p, dy2, conv_w, ln_g, ln_b, sgu_w, bias)
        proj = EVEN_PROJ
        small = dict(conv_w=dconv, ln_g=dlg, ln_b=dlb, sgu_w=dsw, sgu_b=jnp.sum(dms, axis=-1))
        big = [proj.dw(hb, dp), dwo]
    else:
        w_in, pool_w, w_out, pool_scale = w
        dy2, dwo = _out_bwd(dob, w_out, y2)
        dp, dpw, dps = _odd_bwd(p, dy2, pool_w, pool_scale)
        proj = ODD_PROJ
        small = dict(pool_scale=dps)
        big = [proj.dw(hb, dp), dpw, dwo]
    tok = None
    if send is not None:
        big, tok = send(big)
    dh = proj.dh(dp, w_in, tok)
    res = _norm_bwd(x_in, dh, gin, g, scale, below)
    stats = res[1]
    return (res[0], (None if below is None else (res[2], res[3])), big, small,
            jnp.concatenate([stats[0:2], dgate], axis=0), stats[2:3])


def _pack_rows(parts):
    rows = [p.reshape(-1, LANES) for p in parts]
    total = sum(r.shape[0] for r in rows)
    padded = -(-total // (8 * N_DEV)) * (8 * N_DEV)
    if padded > total:
        rows.append(jnp.zeros((padded - total, LANES), f32))
    return jnp.concatenate(rows, axis=0)


def _unpack_rows(buf, shapes):
    out, r = [], 0
    for shp in shapes:
        n = 1
        for d in shp:
            n *= d
        out.append(buf[r:r + n // LANES].reshape(shp))
        r += n // LANES
    return out


def kernel(x, c, norm_g, ada_w, ada_b, ab_w_in, ab_conv_w, ab_ln_g, ab_ln_b, ab_sgu_w, ab_sgu_b, ab_w_out, c_w_in, c_pool_w, c_pool_scale, c_w_out, final_g, loss_target, m_norm_g, m_ada_w, m_ada_b, m_ab_w_in, m_ab_conv_w, m_ab_ln_g, m_ab_ln_b, m_ab_sgu_w, m_ab_sgu_b, m_ab_w_out, m_c_w_in, m_c_pool_w, m_c_pool_scale, m_c_w_out, m_final_g, v_norm_g, v_ada_w, v_ada_b, v_ab_w_in, v_ab_conv_w, v_ab_ln_g, v_ab_ln_b, v_ab_sgu_w, v_ab_sgu_b, v_ab_w_out, v_c_w_in, v_c_pool_w, v_c_pool_scale, v_c_w_out, v_final_g):
    ix, iy, ic = _place()
    chip, dev = 2 * ix + iy, 4 * ix + 2 * iy + ic
    n_even, n_odd = ab_w_in.shape[0], c_w_in.shape[0]
    depth = n_even + n_odd
    acols = ada_w.shape[2]

    place = jnp.stack([chip, ic]).astype(jnp.int32)
    even_names, odd_names = ["ab_w_in", "ab_w_out"], ["c_w_in", "c_pool_w", "c_w_out"]
    params = {"ab_w_in": (ab_w_in, m_ab_w_in, v_ab_w_in), "ab_w_out": (ab_w_out, m_ab_w_out, v_ab_w_out),
              "c_w_in": (c_w_in, m_c_w_in, v_c_w_in), "c_w_out": (c_w_out, m_c_w_out, v_c_w_out),
              "c_pool_w": tuple(a.reshape(n_odd, GC, GC) for a in (c_pool_w, m_c_pool_w, v_c_pool_w))}

    def placed(names, layer, after=None):
        ws = [params[nm][0] for nm in names]
        return [p.reshape(4, 2, p.shape[1] // 2, p.shape[2]) for p in _cast_place(place, ws, layer, after)]

    def whole(arrays):
        return [g.reshape(4, 2 * g.shape[2], g.shape[3]) for g in arrays]

    first = _gather8(jnp.concatenate([c, ab_conv_w.reshape(1, -1), c_pool_scale.reshape(1, -1)], axis=1), "gather_c")
    c_all, small_all = first[:, 0, :D], first[0::2, 0, D:]
    sems_a, in_a, tok = _ag_start([placed(even_names[:1], 0)], first[0:1, 0, 0:LANES], "ag_start_0a")
    modp = _ada_fwd(c_all, ada_w)
    later = [placed(even_names[1:], 0, tok)]
    later += [placed(even_names if i % 2 == 0 else odd_names, i // 2, tok) for i in range(1, depth)]
    modg = _gather8(modp + tok[0:1, 0:1], "gather_mod", [lay[-1] for lay in later])
    mod_rows = lax.dynamic_index_in_dim(modg[0::2], dev, axis=2, keepdims=False)
    mod = jnp.transpose(mod_rows, (1, 0, 2)).reshape(depth, 3 * D) + ada_b
    mods = [(mod[i:i + 1, 0:D], mod[i:i + 1, D:2 * D], mod[i:i + 1, 2 * D:3 * D]) for i in range(depth)]

    def shard_cols(a, width):
        return lax.dynamic_slice_in_dim(a, chip * width, width, axis=a.ndim - 1)

    n_conv = ab_conv_w.size
    conv_all = small_all[:, :n_conv].reshape(4, n_even, 3, D // 4)
    conv_full = jnp.transpose(conv_all, (1, 2, 0, 3)).reshape(n_even, 3, D)
    scale_all = small_all[:, n_conv:].reshape(4, n_odd, 2 * D // 4)
    scale_full = jnp.transpose(scale_all, (1, 0, 2)).reshape(n_odd, 2 * D)

    gathers_done = mod[0:1, 0:LANES] + scale_full[0:1, 0:LANES]
    sems_b, in_b, tok = _ag_start(later[:1], gathers_done, "ag_start_0b")
    sems_r, in_r, tok = _ag_start(later[1:], tok, "ag_start_rest")

    x_cur, saved, weights, handoff = x[0], [], [], {}
    sems_f, in_f, tok = _agf_start(_ag_wait(in_a[0], sems_a[0], tok, "ag_wait_0a"), "agf_start_0")
    hb = _hnorm(x_cur, norm_g[0:1], mods[0][0] + tok[0:1, 0:1], mods[0][1])
    for i in range(depth):
        j = i // 2
        if i == 0:
            full = whole(_agf_wait(sems_f, in_f, hb, "agf_wait_0")) + [None]
        else:
            full = whole(_agf_wait(*handoff.pop(i), x_cur, f"agf_wait_{i}"))
        if i % 2 == 0:
            w = (full[0], full[1], conv_full[j], ab_ln_g[j:j + 1], ab_ln_b[j:j + 1], ab_sgu_w[j], ab_sgu_b[j])
        else:
            w = (full[0], full[1], full[2], scale_full[j:j + 1])

        def before_out(y2, i=i):
            w_out, tok = None, None
            if i == 0:
                w_out = whole(_ag_forward(_ag_wait(in_b[0], sems_b[0], y2, "ag_wait_0b"), "ag_forward"))[0]
            if i + 1 < depth:
                arrived = _ag_wait(in_r[i], sems_r[i], y2, f"ag_wait_{i + 1}")
                sems_f, inflight, tok = _agf_start(arrived, f"agf_start_{i + 1}")
                handoff[i + 1] = (sems_f, inflight)
            return w_out, tok

        nxt = (norm_g[i + 1:i + 2], mods[i + 1][0], mods[i + 1][1]) if i + 1 < depth else None
        x_cur, hb, sv, w = _layer_fwd(i % 2 == 0, x_cur, hb, mods[i][2], w, nxt, before_out)
        weights.append(w)
        saved.append(sv)
    gin, loss, dfinal_g, dob, dgate = _loss_bwd(x_cur, loss_target[0], final_g.reshape(1, D), saved[-1][4],
                                                mods[-1][2])

    stacked = {}

    def reduce_layer(i, sems, pairs, lands, after):
        pairs, slots = _rs_chip_wait(sems, pairs, lands, after, f"rs_chip_wait_{i}")
        half_sems, halves, _ = _rs_half_start(_rs_sum(place, pairs, slots), f"rs_half_start_{i}")
        return i, half_sems, halves

    def update_layer(i, half_sems, halves, after):
        names = even_names if i % 2 == 0 else odd_names
        grads = _rs_half_wait(half_sems, halves, after, f"rs_half_wait_{i}")
        items = [(params[nm][0], g.reshape(params[nm][0].shape[1:]), params[nm][1], params[nm][2], stacked.get(nm))
                 for nm, g in zip(names, grads)]
        for nm, res in zip(names, _adamw_layer(i // 2, items)):
            stacked[nm] = res
            updated.append(res[1])

    updated = []
    small_g, dmod, dnorm_g, pending, tok = [None] * depth, [None] * depth, [None] * depth, None, None
    exchanging = []
    for i in reversed(range(depth)):
        w = weights[i]
        if tok is not None:
            w = w[:2] + (w[2] + tok[0:1, 0:1],) + w[3:] if i % 2 == 0 else w[:3] + (w[3] + tok[0:1, 0:1],)
        below = (saved[i - 1][4], mods[i - 1][2]) if i > 0 else None

        def send(big_g, i=i):
            if exchanging:
                update_layer(*exchanging.pop(), big_g[0])
            big_g = [g.reshape(4, 2, g.shape[1] // 2, g.shape[2]) for g in big_g]
            sems, big_g, lands, tok = _rs_pair_start(big_g, f"rs_pair_start_{i}")
            return (sems, big_g, lands), tok

        gin, gate_bwd, sent, small_g[i], dmod[i], dnorm_g[i] = _layer_bwd(
            i % 2 == 0, gin, dob, dgate, saved[i], mods[i][1], norm_g[i:i + 1], w, below, send)
        if below is not None:
            dob, dgate = gate_bwd
        after = gin
        if i == 0:
            dmod_all = _gather8(jnp.stack(dmod).reshape(depth * 3 * D // LANES, LANES), "gather_dmod")
            after = dmod_all = dmod_all.reshape(N_DEV, depth, 3 * D)
        if i > 0:
            after, updated = [after] + updated, []
        else:
            after = [after]
        big_g, theirs = _rs_pair_wait(*sent, after, f"rs_pair_wait_{i}")
        pairs = _rs_add(place, big_g, theirs)
        sems, pairs, lands, tok = _rs_chip_start(pairs, f"rs_chip_start_{i}")
        if pending is not None:
            exchanging.append(reduce_layer(*pending, [tok]))
        pending = (i, sems, pairs, lands)
    grad_x = gin
    dnorm_g = jnp.concatenate(dnorm_g, axis=0)

    dmod_cols = jnp.transpose(shard_cols(dmod_all, acols), (1, 0, 2))
    r_ada_w = _ada_bwd(c_all.T, dmod_cols, ada_w, m_ada_w, v_ada_w)
    update_layer(*exchanging.pop(), r_ada_w[1])
    last = reduce_layer(*pending, [r_ada_w[1]] + updated)

    small_parts = [dnorm_g, dfinal_g,
                   jnp.stack([small_g[2 * j]["conv_w"] for j in range(n_even)]),
                   jnp.concatenate([small_g[2 * j]["ln_g"] for j in range(n_even)], axis=0),
                   jnp.concatenate([small_g[2 * j]["ln_b"] for j in range(n_even)], axis=0),
                   jnp.stack([small_g[2 * j]["sgu_b"] for j in range(n_even)]),
                   jnp.concatenate([small_g[2 * j + 1]["pool_scale"] for j in range(n_odd)], axis=0),
                   jnp.pad(loss, ((0, 7), (0, LANES - 1)))]
    small_shapes = [p.shape for p in small_parts]
    sgu_parts = [small_g[2 * j]["sgu_w"].reshape(NH * HEAD, HEAD).astype(bf16) for j in range(n_even)]
    reduced = _allreduce8([_pack_rows(small_parts)] + sgu_parts, "allreduce_small", last[2][0])
    update_layer(*last, reduced[0])
    r_ab_w_in, r_ab_w_out, r_c_w_in, r_c_w_out = (stacked[nm] for nm in ("ab_w_in", "ab_w_out", "c_w_in", "c_w_out"))
    r_c_pool_w = tuple(a.reshape(c_pool_w.shape) for a in stacked["c_pool_w"])
    g_norm_g, g_final_g, g_conv_full, g_ln_g, g_ln_b, g_sgu_b, g_scale_full, loss_row = _unpack_rows(reduced[0],
                                                                                                     small_shapes)
    g_sgu_w = jnp.stack(reduced[1:]).astype(f32)
    loss = loss_row[0, 0]
    g_conv = shard_cols(g_conv_full, D // 4)
    g_scale = shard_cols(g_scale_full, 2 * D // 4)

    def two_d(a):
        return a.reshape(-1, a.shape[-1])

    small = [(norm_g, g_norm_g, m_norm_g, v_norm_g),
             (ada_b, dmod_all, m_ada_b, v_ada_b),
             (two_d(ab_conv_w), two_d(g_conv), two_d(m_ab_conv_w), two_d(v_ab_conv_w)),
             (ab_ln_g, g_ln_g, m_ab_ln_g, v_ab_ln_g),
             (ab_ln_b, g_ln_b, m_ab_ln_b, v_ab_ln_b),
             (two_d(ab_sgu_w), two_d(g_sgu_w), two_d(m_ab_sgu_w), two_d(v_ab_sgu_w)),
             (two_d(ab_sgu_b), two_d(g_sgu_b), two_d(m_ab_sgu_b), two_d(v_ab_sgu_b)),
             (c_pool_scale, g_scale, m_c_pool_scale, v_c_pool_scale),
             (final_g.reshape(1, D), g_final_g, m_final_g.reshape(1, D), v_final_g.reshape(1, D))]
    small_res = _adamw_small(small)
    small_shapes_out = [norm_g.shape, ada_b.shape, ab_conv_w.shape, ab_ln_g.shape, ab_ln_b.shape, ab_sgu_w.shape,
                        ab_sgu_b.shape, c_pool_scale.shape, final_g.shape]
    (r_norm_g, r_ada_b, r_conv, r_ln_g, r_ln_b, r_sgu_w, r_sgu_b, r_scale, r_final_g) = [
        tuple(a.reshape(shp) for a in res) for res, shp in zip(small_res, small_shapes_out)]

    order = [r_norm_g, r_ada_w, r_ada_b, r_ab_w_in, r_conv, r_ln_g, r_ln_b, r_sgu_w, r_sgu_b, r_ab_w_out,
             r_c_w_in, r_c_pool_w, r_scale, r_c_w_out, r_final_g]
    outs = [loss, grad_x[None]]
    for field in range(4):
        outs += [r[field] for r in order]
    return tuple(outs)
```

```python
import functools

import jax
import jax.numpy as jnp
from jax import lax
from jax.experimental import pallas as pl
from jax.experimental.pallas import tpu as pltpu

f32, bf16 = jnp.float32, jnp.bfloat16

D = 1024
HEAD = 128
NH = 8
WINDOWS = (2, 4, 8, 16)
GC = 512
EPS = 1e-6
HALO_CONV = 8
HALO_POOL = 16
CHUNK_ROWS = 512
DH_WIDE = 1024
FWD_TILES = 2
N_DEV = 8
LANES = 128

ADAM_LR, ADAM_B1, ADAM_B2, ADAM_EPS, ADAM_WD, ADAM_STEP = 0.001, 0.9, 0.999, 1e-08, 0.01, 10

MESH = pl.DeviceIdType.MESH
ANY = pl.BlockSpec(memory_space=pl.ANY)
VMEM = pl.BlockSpec(memory_space=pltpu.VMEM)
MIB = 2 ** 20


def _pcall(body, *, name, out_shape, grid=None, in_specs=None, out_specs=None, scratch=(), vmem_mb=None,
           aliases=None, prefetch=0):
    kw = {}
    if prefetch:
        kw["grid_spec"] = pltpu.PrefetchScalarGridSpec(num_scalar_prefetch=prefetch, grid=grid, in_specs=in_specs,
                                                       out_specs=out_specs, scratch_shapes=list(scratch))
    else:
        if grid is not None:
            kw["grid"] = grid
        if in_specs is not None:
            kw["in_specs"] = in_specs
        if out_specs is not None:
            kw["out_specs"] = out_specs
        if scratch:
            kw["scratch_shapes"] = list(scratch)
    if aliases:
        kw["input_output_aliases"] = aliases
    params = pltpu.CompilerParams(vmem_limit_bytes=None if vmem_mb is None else vmem_mb * MIB)
    return pl.pallas_call(body, name=name, out_shape=out_shape, compiler_params=params, **kw)


def _sds(shape, dtype):
    return jax.ShapeDtypeStruct(tuple(shape), dtype)


def _sigmoid(z):
    return pl.reciprocal(1.0 + jnp.exp(-z), approx=True)


def _silu(z):
    return z * _sigmoid(z)


def _silu_and_grad(z):
    s = _sigmoid(z)
    return z * s, s * (1.0 + z * (1.0 - s))


def _place():
    return lax.axis_index("x"), lax.axis_index("y"), lax.axis_index("c")


def _gather8(blk, name, after=()):
    def body(x_ref, *rest):
        o_ref, ssem, rsem = rest[len(after):]
        x, y, c = _place()
        me = 4 * x + 2 * y + c
        o_ref[me] = x_ref[...]
        sends = []
        for k in range(1, N_DEV):
            px = 1 - x if k & 4 else x
            py = 1 - y if k & 2 else y
            pc = 1 - c if k & 1 else c
            cp = pltpu.make_async_remote_copy(src_ref=x_ref, dst_ref=o_ref.at[me], send_sem=ssem.at[k - 1],
                                              recv_sem=rsem.at[k - 1], device_id=(px, py, pc), device_id_type=MESH)
            cp.start()
            sends.append((cp, 4 * px + 2 * py + pc))
        for k, (cp, peer) in enumerate(sends):
            pltpu.make_async_remote_copy(src_ref=x_ref, dst_ref=o_ref.at[peer], send_sem=ssem.at[k],
                                         recv_sem=rsem.at[k], device_id=(x, y, c), device_id_type=MESH).wait_recv()
        for cp, _ in sends:
            cp.wait_send()

    return _pcall(body, name=name, out_shape=_sds((N_DEV,) + blk.shape, blk.dtype), in_specs=[VMEM] + [ANY] * len(after),
                  out_specs=VMEM,
                  scratch=[pltpu.SemaphoreType.DMA((N_DEV - 1,)), pltpu.SemaphoreType.DMA((N_DEV - 1,))])(blk, *after)


def _allreduce8(bufs, name, after=None):
    n, n_after = len(bufs), 0 if after is None else 1
    rbs = [b.shape[0] // N_DEV for b in bufs]
    assert all(rb * N_DEV == b.shape[0] and rb % (16 if b.dtype == bf16 else 8) == 0 for rb, b in zip(rbs, bufs))

    def body(*refs):
        refs = refs[:n] + refs[n + n_after:]
        xs, outs, stages = refs[:n], refs[n:2 * n], refs[2 * n:3 * n]
        ssem, rsem = refs[3 * n:]
        x, y, c = _place()
        me = 4 * x + 2 * y + c
        peers = []
        for k in range(1, N_DEV):
            px = 1 - x if k & 4 else x
            py = 1 - y if k & 2 else y
            pc = 1 - c if k & 1 else c
            peers.append(((px, py, pc), 4 * px + 2 * py + pc))

        def blk(t, ref, idx):
            return ref.at[pl.ds(pl.multiple_of(idx * rbs[t], 8), rbs[t]), :]

        def copy(t, phase, k, src, dst, dev):
            return pltpu.make_async_remote_copy(src_ref=src, dst_ref=dst, send_sem=ssem.at[t, phase, k],
                                                recv_sem=rsem.at[t, phase, k], device_id=dev, device_id_type=MESH)

        scatter = [copy(t, 0, k, blk(t, xs[t], pidx), stages[t].at[me], dev)
                   for t in range(n) for k, (dev, pidx) in enumerate(peers)]
        for cp in scatter:
            cp.start()
        gather = []
        for t in range(n):
            stages[t][me] = blk(t, xs[t], me)[...]
            for k, (dev, pidx) in enumerate(peers):
                copy(t, 0, k, blk(t, xs[t], pidx), stages[t].at[pidx], dev).wait_recv()
            total = stages[t][0].astype(f32)
            for j in range(1, N_DEV):
                total = total + stages[t][j].astype(f32)
            blk(t, outs[t], me)[...] = total.astype(outs[t].dtype)
            sends = [copy(t, 1, k, blk(t, outs[t], me), blk(t, outs[t], me), dev) for k, (dev, pidx) in enumerate(peers)]
            for cp in sends:
                cp.start()
            gather += sends
        for t in range(n):
            for k, (dev, pidx) in enumerate(peers):
                copy(t, 1, k, blk(t, outs[t], pidx), blk(t, outs[t], pidx), dev).wait_recv()
        for cp in scatter + gather:
            cp.wait_send()

    return _pcall(body, name=name, out_shape=[_sds(b.shape, b.dtype) for b in bufs], in_specs=[VMEM] * n + [ANY] * n_after,
                  out_specs=[VMEM] * n,
                  scratch=[pltpu.VMEM((N_DEV, rb, LANES), b.dtype) for rb, b in zip(rbs, bufs)]
                  + [pltpu.SemaphoreType.DMA((n, 2, N_DEV - 1)), pltpu.SemaphoreType.DMA((n, 2, N_DEV - 1))])(
                      *bufs, *([] if after is None else [after]))


def _other_chips(x, y):
    return [((1 - x, y), 2 * (1 - x) + y), ((x, 1 - y), 2 * x + (1 - y)), ((1 - x, 1 - y), 2 * (1 - x) + (1 - y))]


HBM = pl.BlockSpec(memory_space=pltpu.HBM)
SEM = pl.BlockSpec(memory_space=pltpu.SEMAPHORE)
EFFECT = pltpu.SideEffectType.DATAFLOW_SIDE_EFFECTING


def _in_hbm(a):
    return pltpu.with_memory_space_constraint(a, pltpu.HBM)


SIBLING_ID = 1


def _sibling_handshake():
    x, y, c = _place()
    barrier = pltpu.get_barrier_semaphore()
    pl.semaphore_signal(barrier, inc=1, device_id=(x, y, 1 - c), device_id_type=MESH)
    pl.semaphore_wait(barrier, 1)
    return x, y, c


def _ag_start(layers, after, name):
    flat = [t for lay in layers for t in lay]
    n, nl = len(flat), len(layers)

    def body(*refs):
        src = refs[:n]
        sems = refs[n + 1:n + 1 + 2 * nl]
        token = refs[-1]
        x, y, c = _place()
        s_me = 2 * x + y
        t = 0
        for i, lay in enumerate(layers):
            for k in range(len(lay)):
                for j, ((px, py), _) in enumerate(_other_chips(x, y)):
                    pltpu.make_async_remote_copy(src_ref=src[t].at[s_me, c], dst_ref=src[t].at[s_me, c],
                                                 send_sem=sems[2 * i].at[3 * k + j], recv_sem=sems[2 * i + 1].at[3 * k + j],
                                                 device_id=(px, py, c), device_id_type=MESH).start()
                t += 1
        token[...] = jnp.zeros_like(token)

    sem_shapes = [pltpu.SemaphoreType.DMA((3 * len(lay),)) for lay in layers for _ in range(2)]
    out_shape = sem_shapes + [pltpu.HBM(t.shape, t.dtype) for t in flat] + [_sds((8, LANES), f32)]
    outs = pl.pallas_call(
        body, name=name, out_shape=out_shape, in_specs=[HBM] * n + [ANY],
        out_specs=[SEM] * (2 * nl) + [HBM] * n + [VMEM], input_output_aliases={t: 2 * nl + t for t in range(n)},
        compiler_params=pltpu.CompilerParams(has_side_effects=EFFECT))(*[_in_hbm(t) for t in flat], after)
    sems = [(outs[2 * i], outs[2 * i + 1]) for i in range(nl)]
    thru, t = [], 2 * nl
    for lay in layers:
        thru.append(list(outs[t:t + len(lay)]))
        t += len(lay)
    return sems, thru, outs[-1]


def _ag_wait(inflight, sems, after, name):
    n = len(inflight)

    def body(*refs):
        src, ssem, rsem = refs[:n], refs[n], refs[n + 1]
        x, y, c = _place()
        s_me = 2 * x + y
        for k in range(n):
            for j, (_, s_p) in enumerate(_other_chips(x, y)):
                cp = pltpu.make_async_remote_copy(src_ref=src[k].at[s_me, c], dst_ref=src[k].at[s_p, c],
                                                  send_sem=ssem.at[3 * k + j], recv_sem=rsem.at[3 * k + j],
                                                  device_id=(x, y, c), device_id_type=MESH)
                cp.wait_send()
                cp.wait_recv()

    return pl.pallas_call(
        body, name=name, out_shape=[pltpu.HBM(t.shape, t.dtype) for t in inflight],
        in_specs=[HBM] * n + [SEM, SEM, ANY], out_specs=[HBM] * n, input_output_aliases={t: t for t in range(n)},
        compiler_params=pltpu.CompilerParams(has_side_effects=EFFECT))(*inflight, sems[0], sems[1], after)


def _ag_forward(arrived, name):
    n = len(arrived)

    def body(*refs):
        o = refs[n:2 * n]
        ssem, rsem = refs[2 * n:]
        x, y, c = _place()

        def copy(t, j, s, half, dev):
            return pltpu.make_async_remote_copy(src_ref=o[t].at[s, c], dst_ref=o[t].at[s, half], send_sem=ssem.at[t, j],
                                                recv_sem=rsem.at[t, j], device_id=dev, device_id_type=MESH)

        chips = _other_chips(x, y)
        sends = [copy(t, j, s_p, c, (x, y, 1 - c)) for t in range(n) for j, (_, s_p) in enumerate(chips)]
        for cp in sends:
            cp.start()
        for t in range(n):
            for j, (_, s_p) in enumerate(chips):
                copy(t, j, s_p, 1 - c, (x, y, c)).wait_recv()
        for cp in sends:
            cp.wait_send()

    return _pcall(body, name=name, out_shape=[_sds(p.shape, bf16) for p in arrived], in_specs=[ANY] * n,
                  out_specs=[ANY] * n, aliases={t: t for t in range(n)},
                  scratch=[pltpu.SemaphoreType.DMA((n, 3)), pltpu.SemaphoreType.DMA((n, 3))])(*arrived)


def _agf_start(arrived, name):
    n = len(arrived)

    def body(*refs):
        o = refs[:n]
        ssem, rsem, token = refs[n], refs[n + 1], refs[-1]
        x, y, c = _sibling_handshake()
        for t in range(n):
            for j, (_, s_p) in enumerate(_other_chips(x, y)):
                pltpu.make_async_remote_copy(src_ref=o[t].at[s_p, c], dst_ref=o[t].at[s_p, c],
                                             send_sem=ssem.at[3 * t + j], recv_sem=rsem.at[3 * t + j],
                                             device_id=(x, y, 1 - c), device_id_type=MESH).start()
        token[...] = jnp.zeros_like(token)

    out_shape = ([pltpu.SemaphoreType.DMA((3 * n,))] * 2 + [pltpu.HBM(a.shape, bf16) for a in arrived]
                 + [_sds((8, LANES), f32)])
    outs = pl.pallas_call(
        body, name=name, out_shape=out_shape, in_specs=[HBM] * n, out_specs=[SEM, SEM] + [HBM] * n + [VMEM],
        input_output_aliases={t: 2 + t for t in range(n)},
        compiler_params=pltpu.CompilerParams(has_side_effects=EFFECT, collective_id=SIBLING_ID))(
            *[_in_hbm(a) for a in arrived])
    return (outs[0], outs[1]), list(outs[2:2 + n]), outs[-1]


def _agf_wait(sems, inflight, after, name):
    n = len(inflight)

    def body(*refs):
        o, ssem, rsem = refs[:n], refs[n], refs[n + 1]
        x, y, c = _place()
        for t in range(n):
            for j, (_, s_p) in enumerate(_other_chips(x, y)):
                cp = pltpu.make_async_remote_copy(src_ref=o[t].at[s_p, c], dst_ref=o[t].at[s_p, 1 - c],
                                                  send_sem=ssem.at[3 * t + j], recv_sem=rsem.at[3 * t + j],
                                                  device_id=(x, y, c), device_id_type=MESH)
                cp.wait_send()
                cp.wait_recv()

    return pl.pallas_call(
        body, name=name, out_shape=[pltpu.HBM(a.shape, bf16) for a in inflight],
        in_specs=[HBM] * n + [SEM, SEM, ANY], out_specs=[HBM] * n, input_output_aliases={t: t for t in range(n)},
        compiler_params=pltpu.CompilerParams(has_side_effects=EFFECT))(*inflight, sems[0], sems[1], after)


def _rs_pair_start(grads, name):
    n = len(grads)

    def body(*refs):
        g, theirs = refs[:n], refs[n:2 * n]
        ssem, rsem, token = refs[2 * n], refs[2 * n + 1], refs[-1]
        x, y, c = _sibling_handshake()
        for t in range(n):
            pltpu.make_async_remote_copy(src_ref=g[t].at[:, 1 - c], dst_ref=theirs[t], send_sem=ssem.at[t],
                                         recv_sem=rsem.at[t], device_id=(x, y, 1 - c), device_id_type=MESH).start()
        token[...] = jnp.zeros_like(token)

    lands = [lax.empty((4,) + g.shape[2:], bf16) for g in grads]
    out_shape = ([pltpu.SemaphoreType.DMA((n,))] * 2 + [pltpu.HBM(g.shape, bf16) for g in grads]
                 + [pltpu.HBM(q.shape, bf16) for q in lands] + [_sds((8, LANES), f32)])
    outs = pl.pallas_call(
        body, name=name, out_shape=out_shape, in_specs=[HBM] * (2 * n), out_specs=[SEM, SEM] + [HBM] * (2 * n) + [VMEM],
        input_output_aliases={t: 2 + t for t in range(2 * n)},
        compiler_params=pltpu.CompilerParams(has_side_effects=EFFECT, collective_id=SIBLING_ID))(
            *[_in_hbm(a) for a in list(grads) + lands])
    return (outs[0], outs[1]), list(outs[2:2 + n]), list(outs[2 + n:2 + 2 * n]), outs[-1]


def _rs_pair_wait(sems, grads, lands, after, name):
    n = len(grads)

    def body(*refs):
        g, theirs = refs[:n], refs[n:2 * n]
        ssem, rsem = refs[2 * n], refs[2 * n + 1]
        x, y, c = _place()
        for t in range(n):
            cp = pltpu.make_async_remote_copy(src_ref=g[t].at[:, 1 - c], dst_ref=theirs[t], send_sem=ssem.at[t],
                                              recv_sem=rsem.at[t], device_id=(x, y, c), device_id_type=MESH)
            cp.wait_send()
            cp.wait_recv()

    outs = pl.pallas_call(
        body, name=name, out_shape=[pltpu.HBM(a.shape, bf16) for a in list(grads) + list(lands)],
        in_specs=[HBM] * (2 * n) + [SEM, SEM] + [ANY] * len(after), out_specs=[HBM] * (2 * n),
        input_output_aliases={t: t for t in range(2 * n)},
        compiler_params=pltpu.CompilerParams(has_side_effects=EFFECT))(*grads, *lands, sems[0], sems[1], *after)
    return list(outs[:n]), list(outs[n:])


def _rs_chip_start(pairs, name):
    n = len(pairs)

    def body(*refs):
        p, q = refs[:n], refs[n:2 * n]
        ssem, rsem, token = refs[2 * n], refs[2 * n + 1], refs[-1]
        x, y, c = _place()
        for t in range(n):
            for j, ((px, py), s_p) in enumerate(_other_chips(x, y)):
                pltpu.make_async_remote_copy(src_ref=p[t].at[s_p], dst_ref=q[t].at[j], send_sem=ssem.at[3 * t + j],
                                             recv_sem=rsem.at[3 * t + j], device_id=(px, py, c), device_id_type=MESH).start()
        token[...] = jnp.zeros_like(token)

    lands = [lax.empty((3,) + p.shape[1:], bf16) for p in pairs]
    out_shape = ([pltpu.SemaphoreType.DMA((3 * n,))] * 2 + [pltpu.HBM(p.shape, bf16) for p in pairs]
                 + [pltpu.HBM(q.shape, bf16) for q in lands] + [_sds((8, LANES), f32)])
    outs = pl.pallas_call(
        body, name=name, out_shape=out_shape, in_specs=[HBM] * (2 * n), out_specs=[SEM, SEM] + [HBM] * (2 * n) + [VMEM],
        input_output_aliases={t: 2 + t for t in range(2 * n)},
        compiler_params=pltpu.CompilerParams(has_side_effects=EFFECT))(*[_in_hbm(a) for a in list(pairs) + lands])
    return (outs[0], outs[1]), list(outs[2:2 + n]), list(outs[2 + n:2 + 2 * n]), outs[-1]


def _rs_chip_wait(sems, pairs, lands, after, name):
    n = len(pairs)

    def body(*refs):
        p, q = refs[:n], refs[n:2 * n]
        ssem, rsem = refs[2 * n], refs[2 * n + 1]
        x, y, c = _place()
        for t in range(n):
            for j, (_, s_p) in enumerate(_other_chips(x, y)):
                cp = pltpu.make_async_remote_copy(src_ref=p[t].at[s_p], dst_ref=q[t].at[j], send_sem=ssem.at[3 * t + j],
                                                  recv_sem=rsem.at[3 * t + j], device_id=(x, y, c), device_id_type=MESH)
                cp.wait_send()
                cp.wait_recv()

    outs = pl.pallas_call(
        body, name=name, out_shape=[pltpu.HBM(a.shape, bf16) for a in list(pairs) + list(lands)],
        in_specs=[HBM] * (2 * n) + [SEM, SEM] + [ANY] * len(after), out_specs=[HBM] * (2 * n),
        input_output_aliases={t: t for t in range(2 * n)},
        compiler_params=pltpu.CompilerParams(has_side_effects=EFFECT))(*pairs, *lands, sems[0], sems[1], *after)
    return list(outs[:n]), list(outs[n:])


def _rs_half_start(halves, name):
    n = len(halves)

    def body(*refs):
        o = refs[:n]
        ssem, rsem, token = refs[n], refs[n + 1], refs[-1]
        x, y, c = _sibling_handshake()
        for t in range(n):
            pltpu.make_async_remote_copy(src_ref=o[t].at[c], dst_ref=o[t].at[c], send_sem=ssem.at[t],
                                         recv_sem=rsem.at[t], device_id=(x, y, 1 - c), device_id_type=MESH).start()
        token[...] = jnp.zeros_like(token)

    out_shape = ([pltpu.SemaphoreType.DMA((n,))] * 2 + [pltpu.HBM(h.shape, h.dtype) for h in halves]
                 + [_sds((8, LANES), f32)])
    outs = pl.pallas_call(
        body, name=name, out_shape=out_shape, in_specs=[HBM] * n, out_specs=[SEM, SEM] + [HBM] * n + [VMEM],
        input_output_aliases={t: 2 + t for t in range(n)},
        compiler_params=pltpu.CompilerParams(has_side_effects=EFFECT, collective_id=SIBLING_ID))(
            *[_in_hbm(h) for h in halves])
    return (outs[0], outs[1]), list(outs[2:2 + n]), outs[-1]


def _rs_half_wait(sems, inflight, after, name):
    n = len(inflight)

    def body(*refs):
        o, ssem, rsem = refs[:n], refs[n], refs[n + 1]
        x, y, c = _place()
        for t in range(n):
            cp = pltpu.make_async_remote_copy(src_ref=o[t].at[c], dst_ref=o[t].at[1 - c], send_sem=ssem.at[t],
                                              recv_sem=rsem.at[t], device_id=(x, y, c), device_id_type=MESH)
            cp.wait_send()
            cp.wait_recv()

    return pl.pallas_call(
        body, name=name, out_shape=[pltpu.HBM(h.shape, h.dtype) for h in inflight],
        in_specs=[HBM] * n + [SEM, SEM, ANY], out_specs=[HBM] * n, input_output_aliases={t: t for t in range(n)},
        compiler_params=pltpu.CompilerParams(has_side_effects=EFFECT))(*inflight, sems[0], sems[1], after)


def _row_spec(tm, cols):
    return pl.BlockSpec((tm, cols), lambda i: (i, 0))


def _vec_spec(cols, rows=1):
    return pl.BlockSpec((rows, cols), lambda i: (0, 0))


def _modulated_norm(xv, g, shift, scale):
    r = lax.rsqrt(jnp.mean(xv * xv, axis=-1, keepdims=True) + EPS)
    return (((xv * r) * g) * (1.0 + scale) + shift).astype(bf16)


def _hnorm(x, g, shift, scale):
    T, tm = x.shape[0], 256

    def body(x_ref, g_ref, sh_ref, sc_ref, h_ref):
        h_ref[...] = _modulated_norm(x_ref[...], g_ref[...], sh_ref[...], sc_ref[...])

    return _pcall(body, name="hnorm", out_shape=_sds((T, D), bf16), grid=(T // tm,),
                  in_specs=[_row_spec(tm, D), _vec_spec(D), _vec_spec(D), _vec_spec(D)],
                  out_specs=_row_spec(tm, D))(x, g, shift, scale)


def _out_proj(y2, wo, x, gate, nxt=None):
    T, tm = x.shape[0], 512

    def body(y_ref, w_ref, x_ref, g_ref, *rest):
        o = jnp.dot(y_ref[0], w_ref[0], preferred_element_type=f32)
        o = o + jnp.dot(y_ref[1], w_ref[1], preferred_element_type=f32)
        xo = x_ref[...] + g_ref[...] * o
        if nxt is None:
            xo_ref, o_ref = rest
        else:
            ng_ref, nsh_ref, nsc_ref, xo_ref, o_ref, h_ref = rest
            h_ref[...] = _modulated_norm(xo, ng_ref[...], nsh_ref[...], nsc_ref[...])
        o_ref[...] = o.astype(bf16)
        xo_ref[...] = xo

    extra = [] if nxt is None else list(nxt)
    n_out = 2 if nxt is None else 3
    return _pcall(body, name="out_proj", out_shape=[_sds((T, D), f32), _sds((T, D), bf16), _sds((T, D), bf16)][:n_out],
                  grid=(T // tm,),
                  in_specs=[pl.BlockSpec((2, tm, D), lambda i: (0, i, 0)), pl.BlockSpec((2, D, D), lambda i: (0, 0, 0)),
                            _row_spec(tm, D), _vec_spec(D)] + [_vec_spec(D)] * len(extra),
                  out_specs=[_row_spec(tm, D)] * n_out, vmem_mb=40)(y2, wo, x, gate, *extra)


def _gate_bwd_tile(dx, o_ref, gate_ref, dob_ref, dgate_ref):
    dob_ref[...] = (dx * gate_ref[...]).astype(bf16)
    dgate_ref[...] += jnp.sum(dx * o_ref[...].astype(f32), axis=0, keepdims=True)


def _loss_bwd(x, target, g, o, gate):
    T, tm = x.shape[0], 512

    def body(x_ref, t_ref, g_ref, o_ref, gate_ref, dx_ref, loss_ref, dg_ref, dob_ref, dgate_ref):
        @pl.when(pl.program_id(0) == 0)
        def _():
            loss_ref[...] = jnp.zeros_like(loss_ref)
            dg_ref[...] = jnp.zeros_like(dg_ref)
            dgate_ref[...] = jnp.zeros_like(dgate_ref)

        xv, gv = x_ref[...], g_ref[...]
        r = lax.rsqrt(jnp.mean(xv * xv, axis=-1, keepdims=True) + EPS)
        xn = xv * r
        err = xn * gv - t_ref[...]
        dy = err * (1.0 / D)
        dxn = dy * gv
        dx = r * (dxn - xn * jnp.mean(dxn * xn, axis=-1, keepdims=True))
        dx_ref[...] = dx
        dg_ref[...] += jnp.sum(dy * xn, axis=0, keepdims=True)
        loss_ref[...] += (0.5 / D) * jnp.sum(jnp.sum(err * err, axis=1, keepdims=True), axis=0, keepdims=True)
        _gate_bwd_tile(dx, o_ref, gate_ref, dob_ref, dgate_ref)

    return _pcall(body, name="loss_bwd",
                  out_shape=[_sds((T, D), f32), _sds((1, 1), f32), _sds((1, D), f32), _sds((T, D), bf16), _sds((1, D), f32)],
                  grid=(T // tm,),
                  in_specs=[_row_spec(tm, D), _row_spec(tm, D), _vec_spec(D), _row_spec(tm, D), _vec_spec(D)],
                  out_specs=[_row_spec(tm, D), pl.BlockSpec((1, 1), lambda i: (0, 0)), _vec_spec(D), _row_spec(tm, D),
                             _vec_spec(D)])(x, target, g, o, gate)


def _norm_bwd(x, dh, gin, g, scale, below=None):
    T, tm = x.shape[0], 512

    def body(x_ref, dh_ref, gin_ref, g_ref, sc_ref, *rest):
        if below is None:
            dx_ref, st_ref = rest
        else:
            o_ref, gate_ref, dx_ref, st_ref, dob_ref, dgate_ref = rest

        @pl.when(pl.program_id(0) == 0)
        def _():
            st_ref[...] = jnp.zeros_like(st_ref)
            if below is not None:
                dgate_ref[...] = jnp.zeros_like(dgate_ref)

        xv, gv, dhv = x_ref[...], g_ref[...], dh_ref[...]
        r = lax.rsqrt(jnp.mean(xv * xv, axis=-1, keepdims=True) + EPS)
        xn = xv * r
        da = dhv * (1.0 + sc_ref[...])
        dxn = da * gv
        dx = gin_ref[...] + r * (dxn - xn * jnp.mean(dxn * xn, axis=-1, keepdims=True))
        dx_ref[...] = dx
        st_ref[0:1, :] += jnp.sum(dhv, axis=0, keepdims=True)
        st_ref[1:2, :] += jnp.sum(dhv * (xn * gv), axis=0, keepdims=True)
        st_ref[2:3, :] += jnp.sum(da * xn, axis=0, keepdims=True)
        if below is not None:
            _gate_bwd_tile(dx, o_ref, gate_ref, dob_ref, dgate_ref)

    out_shape = [_sds((T, D), f32), _sds((8, D), f32)]
    in_specs = [_row_spec(tm, D), _row_spec(tm, D), _row_spec(tm, D), _vec_spec(D), _vec_spec(D)]
    out_specs = [_row_spec(tm, D), _vec_spec(D, 8)]
    args = [x, dh, gin, g, scale]
    if below is not None:
        out_shape += [_sds((T, D), bf16), _sds((1, D), f32)]
        in_specs += [_row_spec(tm, D), _vec_spec(D)]
        out_specs += [_row_spec(tm, D), _vec_spec(D)]
        args += list(below)
    return _pcall(body, name="norm_bwd", out_shape=out_shape, grid=(T // tm,), in_specs=in_specs,
                  out_specs=out_specs)(*args)


STEPS = 4
ADAMW_STEPS = 8


def _cast_place(place, ws, layer, after=None):
    n = len(ws)

    def body(place_ref, *refs):
        for t in range(n):
            refs[-n + t][...] = refs[t][...].astype(bf16)

    def tile(w):
        return w.shape[1] // STEPS, w.shape[2]

    extra = [] if after is None else [after]
    return _pcall(body, name="cast_place", out_shape=[_sds((4,) + w.shape[1:], bf16) for w in ws], grid=(STEPS,),
                  prefetch=1,
                  in_specs=[pl.BlockSpec((None,) + tile(w), lambda i, pr: (layer, i, 0)) for w in ws] + [ANY] * len(extra),
                  out_specs=[pl.BlockSpec((None,) + tile(w), lambda i, pr: (pr[0], i, 0)) for w in ws])(
                      place, *ws, *extra)


def _rs_add(place, grads, theirs):
    n = len(grads)

    def body(place_ref, *refs):
        for t in range(n):
            refs[2 * n + t][...] = (refs[t][...].astype(f32) + refs[n + t][...].astype(f32)).astype(bf16)

    mine = [pl.BlockSpec((None, None) + q.shape[1:], lambda s, pr: (s, pr[1], 0, 0)) for q in theirs]
    shard = [pl.BlockSpec((None,) + q.shape[1:], lambda s, pr: (s, 0, 0)) for q in theirs]
    return _pcall(body, name="rs_add", out_shape=[_sds(q.shape, bf16) for q in theirs], grid=(4,), prefetch=1,
                  in_specs=mine + shard, out_specs=shard)(place, *grads, *theirs)


def _rs_sum(place, pairs, slots):
    n, steps = len(pairs), 4

    def body(place_ref, *refs):
        for t in range(n):
            p_ref, q_ref = refs[t], refs[n + t]
            total = ((p_ref[...].astype(f32) + q_ref[0].astype(f32)) + q_ref[1].astype(f32)) + q_ref[2].astype(f32)
            refs[2 * n + t][...] = total.astype(bf16)

    def tile(q):
        return q.shape[1] // steps, q.shape[2]

    return _pcall(body, name="rs_sum", out_shape=[_sds((2,) + q.shape[1:], bf16) for q in slots], grid=(steps,),
                  prefetch=1,
                  in_specs=[pl.BlockSpec((None,) + tile(q), lambda i, pr: (pr[0], i, 0)) for q in slots]
                  + [pl.BlockSpec((3,) + tile(q), lambda i, pr: (0, i, 0)) for q in slots],
                  out_specs=[pl.BlockSpec((None,) + tile(q), lambda i, pr: (pr[1], i, 0)) for q in slots])(
                      place, *pairs, *slots)


def _adamw_math(w, g, m, v):
    m = ADAM_B1 * m + (1.0 - ADAM_B1) * g
    v = ADAM_B2 * v + (1.0 - ADAM_B2) * jnp.square(g)
    m_hat = m / (1.0 - ADAM_B1 ** ADAM_STEP)
    v_hat = v / (1.0 - ADAM_B2 ** ADAM_STEP)
    delta = -ADAM_LR * (m_hat / (jnp.sqrt(v_hat) + ADAM_EPS) + ADAM_WD * w)
    return delta, m, v


def _adamw_layer(layer, items):
    n = len(items)

    def body(*refs):
        outs = refs[-4 * n:]
        for t in range(n):
            w_ref, g_ref, m_ref, v_ref = refs[4 * t:4 * t + 4]
            g = g_ref[...].astype(f32)
            outs[4 * t][...] = g
            outs[4 * t + 1][...], outs[4 * t + 2][...], outs[4 * t + 3][...] = _adamw_math(
                w_ref[...], g, m_ref[...], v_ref[...])

    args, in_specs, out_specs, out_shape = [], [], [], []
    for w, g, m, v, _ in items:
        tr, cols = w.shape[1] // ADAMW_STEPS, w.shape[2]
        spec = pl.BlockSpec((None, tr, cols), lambda i: (layer, i, 0))
        args += [w, g, m, v]
        in_specs += [spec, pl.BlockSpec((tr, cols), lambda i: (i, 0)), spec, spec]
        out_specs += [spec] * 4
        out_shape += [_sds(w.shape, f32)] * 4
    aliases = {}
    for t, it in enumerate(items):
        if it[4] is not None:
            for k in range(4):
                aliases[len(args)] = 4 * t + k
                args.append(it[4][k])
                in_specs.append(ANY)
    res = _pcall(body, name="adamw", out_shape=out_shape, grid=(ADAMW_STEPS,), in_specs=in_specs, out_specs=out_specs,
                 aliases=aliases)(*args)
    return [tuple(res[4 * t:4 * t + 4]) for t in range(n)]


def _adamw_small(items):
    n = len(items)

    def body(*refs):
        ins, outs = refs[:4 * n], refs[4 * n:]
        for t in range(n):
            w_ref, g_ref, m_ref, v_ref = ins[4 * t:4 * t + 4]
            if len(g_ref.shape) == len(w_ref.shape) + 1:
                g = g_ref[0]
                for b in range(1, g_ref.shape[0]):
                    g = g + g_ref[b]
            else:
                g = g_ref[...]
            d, m, v = _adamw_math(w_ref[...], g, m_ref[...], v_ref[...])
            outs[4 * t][...], outs[4 * t + 1][...], outs[4 * t + 2][...], outs[4 * t + 3][...] = g, d, m, v

    out_shape = [_sds(w.shape, f32) for (w, _, _, _) in items for _ in range(4)]
    flat = [a for it in items for a in it]
    res = _pcall(body, name="adamw_small", out_shape=out_shape, in_specs=[VMEM] * (4 * n),
                 out_specs=[VMEM] * (4 * n))(*flat)
    return [tuple(res[4 * t:4 * t + 4]) for t in range(n)]


NN = ((1,), (0,))
NT = ((1,), (1,))
TN = ((0,), (0,))


def _mm(name, a, b, *, grid, a_spec, b_spec, out_shape, out_spec, dims, vmem_mb=None):
    def body(a_ref, b_ref, o_ref):
        r = lax.dot_general(a_ref[...], b_ref[...], (dims, ((), ())), preferred_element_type=f32)
        o_ref[...] = r.astype(o_ref.dtype)

    return _pcall(body, name=name, out_shape=out_shape, grid=grid, in_specs=[a_spec, b_spec], out_specs=out_spec,
                  vmem_mb=vmem_mb)(a, b)


def _whole(shape):
    return pl.BlockSpec(shape, lambda j: (0,) * len(shape))


def _split_spec(rows, tile, per_split):
    return pl.BlockSpec((None, rows, tile), lambda j: (j // per_split, 0, j % per_split))


class _Proj:
    def __init__(self, n, splits, tile):
        self.n, self.splits, self.tile = n, splits, tile
        self.steps = n // tile
        self.w_per = n // 4 // tile
        self.a_per = n // splits // tile
        assert self.w_per * tile * 4 == n and self.a_per * tile * splits == n

    def fwd(self, hb, wg):
        T = hb.shape[0]
        sub, tile, w_per = FWD_TILES, self.tile, self.w_per
        wide = sub * tile
        a_per = self.n // self.splits // wide
        assert a_per * wide * self.splits == self.n

        def w_tile(q):
            return pl.BlockSpec((None, D, tile), lambda j: ((sub * j + q) // w_per, 0, (sub * j + q) % w_per))

        def body(a_ref, *rest):
            w = jnp.concatenate([rest[q][...] for q in range(sub)], axis=1)
            rest[sub][...] = jnp.dot(a_ref[...], w, preferred_element_type=f32).astype(bf16)

        return _pcall(body, name="proj_fwd", out_shape=_sds((self.splits, T, self.n // self.splits), bf16),
                      grid=(self.n // wide,), in_specs=[_whole((T, D))] + [w_tile(q) for q in range(sub)],
                      out_specs=pl.BlockSpec((None, T, wide), lambda j: (j // a_per, 0, j % a_per)),
                      vmem_mb=40 if wide > 512 else None)(hb, *([wg] * sub))

    def dw(self, hb, dp):
        T = hb.shape[0]
        return _mm("proj_dw", hb, dp, grid=(self.steps,), a_spec=_whole((T, D)),
                   b_spec=_split_spec(T, self.tile, self.a_per), out_shape=_sds((4, D, self.n // 4), bf16),
                   out_spec=_split_spec(D, self.tile, self.w_per), dims=TN)

    def dh(self, dp, wg, after=None):
        T = dp.shape[1]
        extra = [] if after is None else [after]
        sub, tile, w_per = DH_WIDE // self.tile, self.tile, self.w_per
        a_per = self.n // self.splits // DH_WIDE
        assert sub * tile == DH_WIDE and a_per * DH_WIDE * self.splits == self.n

        def w_tile(q):
            return pl.BlockSpec((None, D, tile), lambda k: ((sub * k + q) // w_per, 0, (sub * k + q) % w_per))

        def body(a_ref, *rest):
            o_ref = rest[-1]
            w = jnp.concatenate([rest[q][...] for q in range(sub)], axis=1)
            r = lax.dot_general(a_ref[...], w, (NT, ((), ())), preferred_element_type=f32)

            @pl.when(pl.program_id(0) == 0)
            def _():
                o_ref[...] = r

            @pl.when(pl.program_id(0) > 0)
            def _():
                o_ref[...] += r

        return _pcall(body, name="proj_dh", out_shape=_sds((T, D), f32), grid=(self.n // DH_WIDE,),
                      in_specs=[pl.BlockSpec((None, T, DH_WIDE), lambda k: (k // a_per, 0, k % a_per))]
                      + [w_tile(q) for q in range(sub)] + [ANY] * len(extra),
                      out_specs=_whole((T, D)), vmem_mb=40)(dp, *([wg] * sub), *extra)


EVEN_PROJ = _Proj(7 * D, 7, 256)
ODD_PROJ = _Proj(4 * D, 2, 512)


def _out_bwd(dob, wo, y2):
    T = dob.shape[0]
    w_spec = pl.BlockSpec((None, 512, D), lambda j: (j, 0, 0))

    def body(dob_ref, w_ref, y_ref, dy_ref, dw_ref):
        dob_v = dob_ref[...]
        dy_ref[...] = lax.dot_general(dob_v, w_ref[...], (NT, ((), ())), preferred_element_type=f32).astype(bf16)
        dw_ref[...] = lax.dot_general(y_ref[...], dob_v, (TN, ((), ())), preferred_element_type=f32).astype(bf16)

    return _pcall(body, name="out_bwd", out_shape=[_sds((2, T, D), bf16), _sds((4, 512, D), bf16)], grid=(4,),
                  in_specs=[_whole((T, D)), w_spec, _split_spec(T, 512, 2)],
                  out_specs=[_split_spec(T, 512, 2), w_spec])(dob, wo, y2)


def _head_spec(lead, T):
    return pl.BlockSpec((lead, T, HEAD), lambda h: (0, 0, h))


def _head_vec(rows):
    return pl.BlockSpec((rows, HEAD), lambda h: (0, h))


_HEAD_MAT = pl.BlockSpec((None, HEAD, HEAD), lambda h: (h, 0, 0))


def _causal():
    return lax.broadcasted_iota(jnp.int32, (HEAD, HEAD), 0) >= lax.broadcasted_iota(jnp.int32, (HEAD, HEAD), 1)


def _layernorm_head(v):
    mu = jnp.mean(v, axis=-1, keepdims=True)
    d = v - mu
    rstd = lax.rsqrt(jnp.mean(d * d, axis=-1, keepdims=True) + EPS)
    return d * rstd, rstd


def _even_fwd(p7, conv_w, ln_g, ln_b, sgu_w, sgu_bias):
    T, C = p7.shape[1], CHUNK_ROWS

    def body(p_ref, cw_ref, lg_ref, lb_ref, w_ref, b_ref, y_ref):
        w0, w1, w2 = cw_ref[0:1, :], cw_ref[1:2, :], cw_ref[2:3, :]
        wm = jnp.where(_causal(), w_ref[...], 0.0).astype(bf16)
        bias, lg, lb = b_ref[...], lg_ref[...], lb_ref[...]

        def step(i, halo):
            rows = pl.ds(pl.multiple_of(i * C, C), C)
            ah, ab, ac, az, u, v, zb = (p_ref[k, rows, :].astype(f32) for k in range(7))
            tt = ac * ah
            ext = jnp.concatenate([halo, tt], axis=0)
            cv = w2 * tt + w1 * pltpu.roll(ext, 1, 0)[HALO_CONV:] + w0 * pltpu.roll(ext, 2, 0)[HALO_CONV:]
            y_ref[0, rows, :] = (ab * cv * _silu(az)).astype(bf16)
            vhat, _ = _layernorm_head(v)
            vn = (vhat * lg + lb).astype(bf16)
            mix = jnp.concatenate([jnp.dot(wm, vn[k * HEAD:(k + 1) * HEAD], preferred_element_type=f32) + bias
                                   for k in range(C // HEAD)], axis=0)
            y_ref[1, rows, :] = (u * mix * _silu(zb)).astype(bf16)
            return tt[C - HALO_CONV:]

        lax.fori_loop(0, T // C, step, jnp.zeros((HALO_CONV, HEAD), f32))

    return _pcall(body, name="even_fwd", out_shape=_sds((2, T, D), bf16), grid=(NH,),
                  in_specs=[_head_spec(7, T), _head_vec(3), _head_vec(1), _head_vec(1), _HEAD_MAT, _HEAD_MAT],
                  out_specs=_head_spec(2, T))(p7, conv_w, ln_g, ln_b, sgu_w, sgu_bias)


def _even_bwd(p7, dy2, conv_w, ln_g, ln_b, sgu_w, sgu_bias):
    T, C = p7.shape[1], CHUNK_ROWS
    n_chunks = T // C

    def body(p_ref, dy_ref, cw_ref, lg_ref, lb_ref, w_ref, b_ref,
             dp_ref, dcw_ref, dlg_ref, dlb_ref, dw_ref, dms_ref, dcv_s):
        w0, w1, w2 = cw_ref[0:1, :], cw_ref[1:2, :], cw_ref[2:3, :]
        tri = _causal()
        wm = jnp.where(tri, w_ref[...], 0.0).astype(bf16)
        bias, lg, lb = b_ref[...], lg_ref[...], lb_ref[...]
        dw_ref[...] = jnp.zeros_like(dw_ref)
        dms_ref[...] = jnp.zeros_like(dms_ref)

        def fwd_step(i, carry):
            halo, a0, a1, a2, alg, alb = carry
            rows = pl.ds(pl.multiple_of(i * C, C), C)
            ah, ab, ac, az = (p_ref[k, rows, :].astype(f32) for k in range(4))
            dya = dy_ref[0, rows, :].astype(f32)
            tt = ac * ah
            ext = jnp.concatenate([halo, tt], axis=0)
            t1, t2 = pltpu.roll(ext, 1, 0)[HALO_CONV:], pltpu.roll(ext, 2, 0)[HALO_CONV:]
            cv = w2 * tt + w1 * t1 + w0 * t2
            sa, dsa = _silu_and_grad(az)
            g1 = dya * sa
            dp_ref[1, rows, :] = (g1 * cv).astype(bf16)
            dp_ref[3, rows, :] = (dya * ab * cv * dsa).astype(bf16)
            dcv = g1 * ab
            dcv_s[rows, :] = dcv
            a2 = a2 + jnp.sum(dcv * tt, axis=0, keepdims=True)
            a1 = a1 + jnp.sum(dcv * t1, axis=0, keepdims=True)
            a0 = a0 + jnp.sum(dcv * t2, axis=0, keepdims=True)

            u, zb, dyb = p_ref[4, rows, :].astype(f32), p_ref[6, rows, :].astype(f32), dy_ref[1, rows, :].astype(f32)
            vhat, rstd = _layernorm_head(p_ref[5, rows, :].astype(f32))
            vn = (vhat * lg + lb).astype(bf16)
            sb, dsb = _silu_and_grad(zb)
            mix = jnp.concatenate([jnp.dot(wm, vn[k * HEAD:(k + 1) * HEAD], preferred_element_type=f32) + bias
                                   for k in range(C // HEAD)], axis=0)
            dp_ref[4, rows, :] = (dyb * mix * sb).astype(bf16)
            dp_ref[6, rows, :] = (dyb * u * mix * dsb).astype(bf16)
            dmix = dyb * u * sb
            dvn_parts = []
            for k in range(C // HEAD):
                dm = dmix[k * HEAD:(k + 1) * HEAD]
                dmb = dm.astype(bf16)
                dvn_parts.append(lax.dot_general(wm, dmb, (TN, ((), ())), preferred_element_type=f32))
                dw_ref[...] += lax.dot_general(dmb, vn[k * HEAD:(k + 1) * HEAD], (NT, ((), ())),
                                               preferred_element_type=f32)
                dms_ref[...] += dm
            dvn = jnp.concatenate(dvn_parts, axis=0)
            alg = alg + jnp.sum(dvn * vhat, axis=0, keepdims=True)
            alb = alb + jnp.sum(dvn, axis=0, keepdims=True)
            dvh = dvn * lg
            dv = rstd * (dvh - jnp.mean(dvh, axis=-1, keepdims=True)
                         - vhat * jnp.mean(dvh * vhat, axis=-1, keepdims=True))
            dp_ref[5, rows, :] = dv.astype(bf16)
            return tt[C - HALO_CONV:], a0, a1, a2, alg, alb

        zrow = jnp.zeros((1, HEAD), f32)
        _, a0, a1, a2, alg, alb = lax.fori_loop(
            0, n_chunks, fwd_step, (jnp.zeros((HALO_CONV, HEAD), f32), zrow, zrow, zrow, zrow, zrow))
        dcw_ref[0:1, :], dcw_ref[1:2, :], dcw_ref[2:3, :] = a0, a1, a2
        dlg_ref[...], dlb_ref[...] = alg, alb
        dw_ref[...] = jnp.where(tri, dw_ref[...], 0.0)

        def bwd_step(k, halo):
            rows = pl.ds(pl.multiple_of((n_chunks - 1 - k) * C, C), C)
            dcv = dcv_s[rows, :]
            ext = jnp.concatenate([dcv, halo], axis=0)
            n1 = pltpu.roll(ext, C + HALO_CONV - 1, 0)[:C]
            n2 = pltpu.roll(ext, C + HALO_CONV - 2, 0)[:C]
            dtt = w2 * dcv + w1 * n1 + w0 * n2
            dp_ref[2, rows, :] = (dtt * p_ref[0, rows, :].astype(f32)).astype(bf16)
            dp_ref[0, rows, :] = (dtt * p_ref[2, rows, :].astype(f32)).astype(bf16)
            return dcv[:HALO_CONV]

        lax.fori_loop(0, n_chunks, bwd_step, jnp.zeros((HALO_CONV, HEAD), f32))

    out_shape = [_sds((7, T, D), bf16), _sds((3, D), f32), _sds((1, D), f32), _sds((1, D), f32),
                 _sds((NH, HEAD, HEAD), f32), _sds((NH, HEAD, HEAD), f32)]
    return _pcall(body, name="even_bwd", out_shape=out_shape, grid=(NH,),
                  in_specs=[_head_spec(7, T), _head_spec(2, T), _head_vec(3), _head_vec(1), _head_vec(1),
                            _HEAD_MAT, _HEAD_MAT],
                  out_specs=[_head_spec(7, T), _head_vec(3), _head_vec(1), _head_vec(1), _HEAD_MAT, _HEAD_MAT],
                  scratch=[pltpu.VMEM((T, HEAD), f32)])(p7, dy2, conv_w, ln_g, ln_b, sgu_w, sgu_bias)


def _window_sum(ext, win, towards_past):
    n, k, s = ext.shape[0], 1, ext
    while k < win:
        s = s + pltpu.roll(s, k if towards_past else n - k, 0)
        k *= 2
    return s


def _pool_count(i, C, win):
    t = i * C + lax.broadcasted_iota(jnp.int32, (C, 1), 0)
    cnt = jnp.minimum(t + 1, win).astype(f32)
    return cnt, 1.0 / cnt


def _group_specs(T):
    p_spec = pl.BlockSpec((None, T, GC), lambda g: (0, 0, g))
    z_spec = pl.BlockSpec((None, T, GC), lambda g: (1, 0, g))
    pw_spec = pl.BlockSpec((4, GC // 4, GC), lambda g: (0, g, 0))
    ps_spec = pl.BlockSpec((1, GC), lambda g: (0, g))
    y_spec = pl.BlockSpec((None, T, GC), lambda g: (g // 2, 0, g % 2))
    return p_spec, z_spec, pw_spec, ps_spec, y_spec


def _odd_fwd(p2, pool_wg, pool_scale):
    T, C = p2.shape[1], CHUNK_ROWS
    p_spec, z_spec, pw_spec, ps_spec, y_spec = _group_specs(T)

    def body(p_ref, z_ref, pw_ref, ps_ref, y_ref):
        pw, ps = pw_ref[...].reshape(GC, GC), ps_ref[...]

        def run(win):
            def step(i, halo):
                rows = pl.ds(pl.multiple_of(i * C, C), C)
                p = p_ref[rows, :].astype(f32)
                s = _window_sum(jnp.concatenate([halo, p], axis=0), win, True)[HALO_POOL:]
                pooled = s * _pool_count(i, C, win)[1] - p
                ypre = jnp.dot(pooled.astype(bf16), pw, preferred_element_type=f32)
                y_ref[rows, :] = (ypre * ps * _silu(z_ref[rows, :].astype(f32))).astype(bf16)
                return p[C - HALO_POOL:]

            lax.fori_loop(0, T // C, step, jnp.zeros((HALO_POOL, GC), f32))

        for gi, win in enumerate(WINDOWS):
            pl.when(pl.program_id(0) == gi)(functools.partial(run, win))

    return _pcall(body, name="odd_fwd", out_shape=_sds((2, T, D), bf16), grid=(len(WINDOWS),),
                  in_specs=[p_spec, z_spec, pw_spec, ps_spec], out_specs=y_spec)(p2, p2, pool_wg, pool_scale)


def _odd_bwd(p2, dy2, pool_wg, pool_scale):
    T, C = p2.shape[1], CHUNK_ROWS
    n_chunks = T // C
    p_spec, z_spec, pw_spec, ps_spec, y_spec = _group_specs(T)

    def body(p_ref, z_ref, dy_ref, pw_ref, ps_ref, dp_ref, dpw_ref, dps_ref, q_s, acc_s):
        pw, ps = pw_ref[...].reshape(GC, GC), ps_ref[...]

        def run(win):
            acc_s[...] = jnp.zeros_like(acc_s)

            def fwd_step(i, carry):
                halo, aps = carry
                rows = pl.ds(pl.multiple_of(i * C, C), C)
                p, z, dy = p_ref[rows, :].astype(f32), z_ref[rows, :].astype(f32), dy_ref[rows, :].astype(f32)
                _, inv_cnt = _pool_count(i, C, win)
                s = _window_sum(jnp.concatenate([halo, p], axis=0), win, True)[HALO_POOL:]
                pb = (s * inv_cnt - p).astype(bf16)
                ypre = jnp.dot(pb, pw, preferred_element_type=f32)
                sz, dsz = _silu_and_grad(z)
                aps = aps + jnp.sum(dy * ypre * sz, axis=0, keepdims=True)
                dp_ref[1, rows, :] = (dy * ypre * ps * dsz).astype(bf16)
                dyp = (dy * ps * sz).astype(bf16)
                acc_s[...] += lax.dot_general(pb, dyp, (TN, ((), ())), preferred_element_type=f32)
                dpool = lax.dot_general(dyp, pw, (NT, ((), ())), preferred_element_type=f32)
                q_s[rows, :] = dpool * inv_cnt
                return p[C - HALO_POOL:], aps

            _, aps = lax.fori_loop(0, n_chunks, fwd_step, (jnp.zeros((HALO_POOL, GC), f32), jnp.zeros((1, GC), f32)))
            dps_ref[...] = aps
            dpw_ref[...] = acc_s[...].reshape(4, GC // 4, GC).astype(bf16)

            def bwd_step(k, halo):
                i = n_chunks - 1 - k
                rows = pl.ds(pl.multiple_of(i * C, C), C)
                q = q_s[rows, :]
                s = _window_sum(jnp.concatenate([q, halo], axis=0), win, False)[:C]
                dp_ref[0, rows, :] = (s - q * _pool_count(i, C, win)[0]).astype(bf16)
                return q[:HALO_POOL]

            lax.fori_loop(0, n_chunks, bwd_step, jnp.zeros((HALO_POOL, GC), f32))

        for gi, win in enumerate(WINDOWS):
            pl.when(pl.program_id(0) == gi)(functools.partial(run, win))

    out_shape = [_sds((2, T, 2 * D), bf16), _sds((4, GC, GC), bf16), _sds((1, 2 * D), f32)]
    return _pcall(body, name="odd_bwd", out_shape=out_shape, grid=(len(WINDOWS),),
                  in_specs=[p_spec, z_spec, y_spec, pw_spec, ps_spec],
                  out_specs=[pl.BlockSpec((2, T, GC), lambda g: (0, 0, g)), pw_spec, ps_spec],
                  scratch=[pltpu.VMEM((T, GC), f32), pltpu.VMEM((GC, GC), f32)], vmem_mb=44)(
                      p2, p2, dy2, pool_wg, pool_scale)


def _ada_fwd(c_all, ada_w):
    cols = ada_w.shape[2]

    def body(c_ref, w_ref, o_ref):
        o_ref[...] = jnp.dot(_silu(c_ref[...]), w_ref[...], preferred_element_type=f32,
                             precision=lax.Precision.HIGHEST)

    return _pcall(body, name="ada_fwd", out_shape=_sds((4, N_DEV, cols), f32), grid=(4,),
                  in_specs=[pl.BlockSpec((N_DEV, D), lambda i: (0, 0)), pl.BlockSpec((None, D, cols), lambda i: (i, 0, 0))],
                  out_specs=pl.BlockSpec((None, N_DEV, cols), lambda i: (i, 0, 0)))(c_all, ada_w)


def _ada_bwd(c_all_t, dmod, w, m, v):
    cols, tr = w.shape[2], 256
    spec = pl.BlockSpec((None, tr, cols), lambda l, i: (l, i, 0))

    def body(c_ref, dm_ref, w_ref, m_ref, v_ref, g_ref, d_ref, mo_ref, vo_ref):
        sc = _silu(c_ref[...])
        g = sc[:, 0:1] * dm_ref[0:1, :]
        for b in range(1, N_DEV):
            g = g + sc[:, b:b + 1] * dm_ref[b:b + 1, :]
        g_ref[...] = g
        d_ref[...], mo_ref[...], vo_ref[...] = _adamw_math(w_ref[...], g, m_ref[...], v_ref[...])

    return _pcall(body, name="ada_bwd", out_shape=[_sds(w.shape, f32)] * 4, grid=(4, D // tr),
                  in_specs=[pl.BlockSpec((tr, N_DEV), lambda l, i: (i, 0)),
                            pl.BlockSpec((None, N_DEV, cols), lambda l, i: (l, 0, 0)), spec, spec, spec],
                  out_specs=[spec] * 4)(c_all_t, dmod, w, m, v)


def _layer_fwd(even, x, hb, gate, w, nxt, before_out=None):
    if even:
        w_in, w_out, conv_w, ln_g, ln_b, sgu_w, sgu_b = w
        bias = jnp.broadcast_to(sgu_b[:, :, None], (NH, HEAD, HEAD))
        p = EVEN_PROJ.fwd(hb, w_in)
        y2 = _even_fwd(p, conv_w, ln_g, ln_b, sgu_w, bias)
    else:
        w_in, pool_w, w_out, pool_scale = w
        p = ODD_PROJ.fwd(hb, w_in)
        y2 = _odd_fwd(p, pool_w, pool_scale)
    if before_out is not None:
        late_w_out, tok = before_out(y2)
        if late_w_out is not None:
            w_out = late_w_out
            w = (w_in, w_out) + tuple(w[2:]) if even else (w_in, pool_w, w_out, pool_scale)
        if tok is not None:
            gate = gate + tok[0:1, 0:1]
    outs = _out_proj(y2, w_out.reshape(2, D, D), x, gate, nxt)
    return outs[0], (None if nxt is None else outs[2]), (x, hb, p, y2, outs[1]), w


def _layer_bwd(even, gin, dob, dgate, saved, scale, g, w, below=None, send=None):
    x_in, hb, p, y2, o = saved
    if even:
        w_in, w_out, conv_w, ln_g, ln_b, sgu_w, sgu_b = w
        bias = jnp.broadcast_to(sgu_b[:, :, None], (NH, HEAD, HEAD))
        dy2, dwo = _out_bwd(dob, w_out, y2)
        dp, dconv, dlg, dlb, dsw, dms = _even_bwd(p, dy2, conv_w, ln_g, ln_b, sgu_w, bias)
        proj = EVEN_PROJ
        small = dict(conv_w=dconv, ln_g=dlg, ln_b=dlb, sgu_w=dsw, sgu_b=jnp.sum(dms, axis=-1))
        big = [proj.dw(hb, dp), dwo]
    else:
        w_in, pool_w, w_out, pool_scale = w
        dy2, dwo = _out_bwd(dob, w_out, y2)
        dp, dpw, dps = _odd_bwd(p, dy2, pool_w, pool_scale)
        proj = ODD_PROJ
        small = dict(pool_scale=dps)
        big = [proj.dw(hb, dp), dpw, dwo]
    tok = None
    if send is not None:
        big, tok = send(big)
    dh = proj.dh(dp, w_in, tok)
    res = _norm_bwd(x_in, dh, gin, g, scale, below)
    stats = res[1]
    return (res[0], (None if below is None else (res[2], res[3])), big, small,
            jnp.concatenate([stats[0:2], dgate], axis=0), stats[2:3])


def _pack_rows(parts):
    rows = [p.reshape(-1, LANES) for p in parts]
    total = sum(r.shape[0] for r in rows)
    padded = -(-total // (8 * N_DEV)) * (8 * N_DEV)
    if padded > total:
        rows.append(jnp.zeros((padded - total, LANES), f32))
    return jnp.concatenate(rows, axis=0)


def _unpack_rows(buf, shapes):
    out, r = [], 0
    for shp in shapes:
        n = 1
        for d in shp:
            n *= d
        out.append(buf[r:r + n // LANES].reshape(shp))
        r += n // LANES
    return out


def kernel(x, c, norm_g, ada_w, ada_b, ab_w_in, ab_conv_w, ab_ln_g, ab_ln_b, ab_sgu_w, ab_sgu_b, ab_w_out, c_w_in, c_pool_w, c_pool_scale, c_w_out, final_g, loss_target, m_norm_g, m_ada_w, m_ada_b, m_ab_w_in, m_ab_conv_w, m_ab_ln_g, m_ab_ln_b, m_ab_sgu_w, m_ab_sgu_b, m_ab_w_out, m_c_w_in, m_c_pool_w, m_c_pool_scale, m_c_w_out, m_final_g, v_norm_g, v_ada_w, v_ada_b, v_ab_w_in, v_ab_conv_w, v_ab_ln_g, v_ab_ln_b, v_ab_sgu_w, v_ab_sgu_b, v_ab_w_out, v_c_w_in, v_c_pool_w, v_c_pool_scale, v_c_w_out, v_final_g):
    ix, iy, ic = _place()
    chip, dev = 2 * ix + iy, 4 * ix + 2 * iy + ic
    n_even, n_odd = ab_w_in.shape[0], c_w_in.shape[0]
    depth = n_even + n_odd
    acols = ada_w.shape[2]

    place = jnp.stack([chip, ic]).astype(jnp.int32)
    even_names, odd_names = ["ab_w_in", "ab_w_out"], ["c_w_in", "c_pool_w", "c_w_out"]
    params = {"ab_w_in": (ab_w_in, m_ab_w_in, v_ab_w_in), "ab_w_out": (ab_w_out, m_ab_w_out, v_ab_w_out),
              "c_w_in": (c_w_in, m_c_w_in, v_c_w_in), "c_w_out": (c_w_out, m_c_w_out, v_c_w_out),
              "c_pool_w": tuple(a.reshape(n_odd, GC, GC) for a in (c_pool_w, m_c_pool_w, v_c_pool_w))}

    def placed(names, layer, after=None):
        ws = [params[nm][0] for nm in names]
        return [p.reshape(4, 2, p.shape[1] // 2, p.shape[2]) for p in _cast_place(place, ws, layer, after)]

    def whole(arrays):
        return [g.reshape(4, 2 * g.shape[2], g.shape[3]) for g in arrays]

    first = _gather8(jnp.concatenate([c, ab_conv_w.reshape(1, -1), c_pool_scale.reshape(1, -1)], axis=1), "gather_c")
    c_all, small_all = first[:, 0, :D], first[0::2, 0, D:]
    sems_a, in_a, tok = _ag_start([placed(even_names[:1], 0)], first[0:1, 0, 0:LANES], "ag_start_0a")
    modp = _ada_fwd(c_all, ada_w)
    later = [placed(even_names[1:], 0, tok)]
    later += [placed(even_names if i % 2 == 0 else odd_names, i // 2, tok) for i in range(1, depth)]
    modg = _gather8(modp + tok[0:1, 0:1], "gather_mod", [lay[-1] for lay in later])
    mod_rows = lax.dynamic_index_in_dim(modg[0::2], dev, axis=2, keepdims=False)
    mod = jnp.transpose(mod_rows, (1, 0, 2)).reshape(depth, 3 * D) + ada_b
    mods = [(mod[i:i + 1, 0:D], mod[i:i + 1, D:2 * D], mod[i:i + 1, 2 * D:3 * D]) for i in range(depth)]

    def shard_cols(a, width):
        return lax.dynamic_slice_in_dim(a, chip * width, width, axis=a.ndim - 1)

    n_conv = ab_conv_w.size
    conv_all = small_all[:, :n_conv].reshape(4, n_even, 3, D // 4)
    conv_full = jnp.transpose(conv_all, (1, 2, 0, 3)).reshape(n_even, 3, D)
    scale_all = small_all[:, n_conv:].reshape(4, n_odd, 2 * D // 4)
    scale_full = jnp.transpose(scale_all, (1, 0, 2)).reshape(n_odd, 2 * D)

    gathers_done = mod[0:1, 0:LANES] + scale_full[0:1, 0:LANES]
    sems_b, in_b, tok = _ag_start(later[:1], gathers_done, "ag_start_0b")
    sems_r, in_r, tok = _ag_start(later[1:], tok, "ag_start_rest")

    x_cur, saved, weights, handoff = x[0], [], [], {}
    sems_f, in_f, tok = _agf_start(_ag_wait(in_a[0], sems_a[0], tok, "ag_wait_0a"), "agf_start_0")
    hb = _hnorm(x_cur, norm_g[0:1], mods[0][0] + tok[0:1, 0:1], mods[0][1])
    for i in range(depth):
        j = i // 2
        if i == 0:
            full = whole(_agf_wait(sems_f, in_f, hb, "agf_wait_0")) + [None]
        else:
            full = whole(_agf_wait(*handoff.pop(i), x_cur, f"agf_wait_{i}"))
        if i % 2 == 0:
            w = (full[0], full[1], conv_full[j], ab_ln_g[j:j + 1], ab_ln_b[j:j + 1], ab_sgu_w[j], ab_sgu_b[j])
        else:
            w = (full[0], full[1], full[2], scale_full[j:j + 1])

        def before_out(y2, i=i):
            w_out, tok = None, None
            if i == 0:
                w_out = whole(_ag_forward(_ag_wait(in_b[0], sems_b[0], y2, "ag_wait_0b"), "ag_forward"))[0]
            if i + 1 < depth:
                arrived = _ag_wait(in_r[i], sems_r[i], y2, f"ag_wait_{i + 1}")
                sems_f, inflight, tok = _agf_start(arrived, f"agf_start_{i + 1}")
                handoff[i + 1] = (sems_f, inflight)
            return w_out, tok

        nxt = (norm_g[i + 1:i + 2], mods[i + 1][0], mods[i + 1][1]) if i + 1 < depth else None
        x_cur, hb, sv, w = _layer_fwd(i % 2 == 0, x_cur, hb, mods[i][2], w, nxt, before_out)
        weights.append(w)
        saved.append(sv)
    gin, loss, dfinal_g, dob, dgate = _loss_bwd(x_cur, loss_target[0], final_g.reshape(1, D), saved[-1][4],
                                                mods[-1][2])

    stacked = {}

    def reduce_layer(i, sems, pairs, lands, after):
        pairs, slots = _rs_chip_wait(sems, pairs, lands, after, f"rs_chip_wait_{i}")
        half_sems, halves, _ = _rs_half_start(_rs_sum(place, pairs, slots), f"rs_half_start_{i}")
        return i, half_sems, halves

    def update_layer(i, half_sems, halves, after):
        names = even_names if i % 2 == 0 else odd_names
        grads = _rs_half_wait(half_sems, halves, after, f"rs_half_wait_{i}")
        items = [(params[nm][0], g.reshape(params[nm][0].shape[1:]), params[nm][1], params[nm][2], stacked.get(nm))
                 for nm, g in zip(names, grads)]
        for nm, res in zip(names, _adamw_layer(i // 2, items)):
            stacked[nm] = res
            updated.append(res[1])

    updated = []
    small_g, dmod, dnorm_g, pending, tok = [None] * depth, [None] * depth, [None] * depth, None, None
    exchanging = []
    for i in reversed(range(depth)):
        w = weights[i]
        if tok is not None:
            w = w[:2] + (w[2] + tok[0:1, 0:1],) + w[3:] if i % 2 == 0 else w[:3] + (w[3] + tok[0:1, 0:1],)
        below = (saved[i - 1][4], mods[i - 1][2]) if i > 0 else None

        def send(big_g, i=i):
            if exchanging:
                update_layer(*exchanging.pop(), big_g[0])
            big_g = [g.reshape(4, 2, g.shape[1] // 2, g.shape[2]) for g in big_g]
            sems, big_g, lands, tok = _rs_pair_start(big_g, f"rs_pair_start_{i}")
            return (sems, big_g, lands), tok

        gin, gate_bwd, sent, small_g[i], dmod[i], dnorm_g[i] = _layer_bwd(
            i % 2 == 0, gin, dob, dgate, saved[i], mods[i][1], norm_g[i:i + 1], w, below, send)
        if below is not None:
            dob, dgate = gate_bwd
        after = gin
        if i == 0:
            dmod_all = _gather8(jnp.stack(dmod).reshape(depth * 3 * D // LANES, LANES), "gather_dmod")
            after = dmod_all = dmod_all.reshape(N_DEV, depth, 3 * D)
        if i > 0:
            after, updated = [after] + updated, []
        else:
            after = [after]
        big_g, theirs = _rs_pair_wait(*sent, after, f"rs_pair_wait_{i}")
        pairs = _rs_add(place, big_g, theirs)
        sems, pairs, lands, tok = _rs_chip_start(pairs, f"rs_chip_start_{i}")
        if pending is not None:
            exchanging.append(reduce_layer(*pending, [tok]))
        pending = (i, sems, pairs, lands)
    grad_x = gin
    dnorm_g = jnp.concatenate(dnorm_g, axis=0)

    dmod_cols = jnp.transpose(shard_cols(dmod_all, acols), (1, 0, 2))
    r_ada_w = _ada_bwd(c_all.T, dmod_cols, ada_w, m_ada_w, v_ada_w)
    update_layer(*exchanging.pop(), r_ada_w[1])
    last = reduce_layer(*pending, [r_ada_w[1]] + updated)

    small_parts = [dnorm_g, dfinal_g,
                   jnp.stack([small_g[2 * j]["conv_w"] for j in range(n_even)]),
                   jnp.concatenate([small_g[2 * j]["ln_g"] for j in range(n_even)], axis=0),
                   jnp.concatenate([small_g[2 * j]["ln_b"] for j in range(n_even)], axis=0),
                   jnp.stack([small_g[2 * j]["sgu_b"] for j in range(n_even)]),
                   jnp.concatenate([small_g[2 * j + 1]["pool_scale"] for j in range(n_odd)], axis=0),
                   jnp.pad(loss, ((0, 7), (0, LANES - 1)))]
    small_shapes = [p.shape for p in small_parts]
    sgu_parts = [small_g[2 * j]["sgu_w"].reshape(NH * HEAD, HEAD).astype(bf16) for j in range(n_even)]
    reduced = _allreduce8([_pack_rows(small_parts)] + sgu_parts, "allreduce_small", last[2][0])
    update_layer(*last, reduced[0])
    r_ab_w_in, r_ab_w_out, r_c_w_in, r_c_w_out = (stacked[nm] for nm in ("ab_w_in", "ab_w_out", "c_w_in", "c_w_out"))
    r_c_pool_w = tuple(a.reshape(c_pool_w.shape) for a in stacked["c_pool_w"])
    g_norm_g, g_final_g, g_conv_full, g_ln_g, g_ln_b, g_sgu_b, g_scale_full, loss_row = _unpack_rows(reduced[0],
                                                                                                     small_shapes)
    g_sgu_w = jnp.stack(reduced[1:]).astype(f32)
    loss = loss_row[0, 0]
    g_conv = shard_cols(g_conv_full, D // 4)
    g_scale = shard_cols(g_scale_full, 2 * D // 4)

    def two_d(a):
        return a.reshape(-1, a.shape[-1])

    small = [(norm_g, g_norm_g, m_norm_g, v_norm_g),
             (ada_b, dmod_all, m_ada_b, v_ada_b),
             (two_d(ab_conv_w), two_d(g_conv), two_d(m_ab_conv_w), two_d(v_ab_conv_w)),
             (ab_ln_g, g_ln_g, m_ab_ln_g, v_ab_ln_g),
             (ab_ln_b, g_ln_b, m_ab_ln_b, v_ab_ln_b),
             (two_d(ab_sgu_w), two_d(g_sgu_w), two_d(m_ab_sgu_w), two_d(v_ab_sgu_w)),
             (two_d(ab_sgu_b), two_d(g_sgu_b), two_d(m_ab_sgu_b), two_d(v_ab_sgu_b)),
             (c_pool_scale, g_scale, m_c_pool_scale, v_c_pool_scale),
             (final_g.reshape(1, D), g_final_g, m_final_g.reshape(1, D), v_final_g.reshape(1, D))]
    small_res = _adamw_small(small)
    small_shapes_out = [norm_g.shape, ada_b.shape, ab_conv_w.shape, ab_ln_g.shape, ab_ln_b.shape, ab_sgu_w.shape,
                        ab_sgu_b.shape, c_pool_scale.shape, final_g.shape]
    (r_norm_g, r_ada_b, r_conv, r_ln_g, r_ln_b, r_sgu_w, r_sgu_b, r_scale, r_final_g) = [
        tuple(a.reshape(shp) for a in res) for res, shp in zip(small_res, small_shapes_out)]

    order = [r_norm_g, r_ada_w, r_ada_b, r_ab_w_in, r_conv, r_ln_g, r_ln_b, r_sgu_w, r_sgu_b, r_ab_w_out,
             r_c_w_in, r_c_pool_w, r_scale, r_c_w_out, r_final_g]
    outs = [loss, grad_x[None]]
    for field in range(4):
        outs += [r[field] for r in order]
    return tuple(outs)
```

```python
import functools

import jax
import jax.numpy as jnp
from jax import lax
from jax.experimental import pallas as pl
from jax.experimental.pallas import tpu as pltpu

f32, bf16 = jnp.float32, jnp.bfloat16

D = 1024
HEAD = 128
NH = 8
WINDOWS = (2, 4, 8, 16)
GC = 512
EPS = 1e-6
HALO_CONV = 8
HALO_POOL = 16
CHUNK_ROWS = 512
DH_WIDE = 1024
FWD_TILES = 2
N_DEV = 8
LANES = 128

ADAM_LR, ADAM_B1, ADAM_B2, ADAM_EPS, ADAM_WD, ADAM_STEP = 0.001, 0.9, 0.999, 1e-08, 0.01, 10

MESH = pl.DeviceIdType.MESH
ANY = pl.BlockSpec(memory_space=pl.ANY)
VMEM = pl.BlockSpec(memory_space=pltpu.VMEM)
MIB = 2 ** 20


def _pcall(body, *, name, out_shape, grid=None, in_specs=None, out_specs=None, scratch=(), vmem_mb=None,
           aliases=None, prefetch=0):
    kw = {}
    if prefetch:
        kw["grid_spec"] = pltpu.PrefetchScalarGridSpec(num_scalar_prefetch=prefetch, grid=grid, in_specs=in_specs,
                                                       out_specs=out_specs, scratch_shapes=list(scratch))
    else:
        if grid is not None:
            kw["grid"] = grid
        if in_specs is not None:
            kw["in_specs"] = in_specs
        if out_specs is not None:
            kw["out_specs"] = out_specs
        if scratch:
            kw["scratch_shapes"] = list(scratch)
    if aliases:
        kw["input_output_aliases"] = aliases
    params = pltpu.CompilerParams(vmem_limit_bytes=None if vmem_mb is None else vmem_mb * MIB)
    return pl.pallas_call(body, name=name, out_shape=out_shape, compiler_params=params, **kw)


def _sds(shape, dtype):
    return jax.ShapeDtypeStruct(tuple(shape), dtype)


def _sigmoid(z):
    return pl.reciprocal(1.0 + jnp.exp(-z), approx=True)


def _silu(z):
    return z * _sigmoid(z)


def _silu_and_grad(z):
    s = _sigmoid(z)
    return z * s, s * (1.0 + z * (1.0 - s))


def _place():
    return lax.axis_index("x"), lax.axis_index("y"), lax.axis_index("c")


def _gather8(blk, name, after=()):
    def body(x_ref, *rest):
        o_ref, ssem, rsem = rest[len(after):]
        x, y, c = _place()
        me = 4 * x + 2 * y + c
        o_ref[me] = x_ref[...]
        sends = []
        for k in range(1, N_DEV):
            px = 1 - x if k & 4 else x
            py = 1 - y if k & 2 else y
            pc = 1 - c if k & 1 else c
            cp = pltpu.make_async_remote_copy(src_ref=x_ref, dst_ref=o_ref.at[me], send_sem=ssem.at[k - 1],
                                              recv_sem=rsem.at[k - 1], device_id=(px, py, pc), device_id_type=MESH)
            cp.start()
            sends.append((cp, 4 * px + 2 * py + pc))
        for k, (cp, peer) in enumerate(sends):
            pltpu.make_async_remote_copy(src_ref=x_ref, dst_ref=o_ref.at[peer], send_sem=ssem.at[k],
                                         recv_sem=rsem.at[k], device_id=(x, y, c), device_id_type=MESH).wait_recv()
        for cp, _ in sends:
            cp.wait_send()

    return _pcall(body, name=name, out_shape=_sds((N_DEV,) + blk.shape, blk.dtype), in_specs=[VMEM] + [ANY] * len(after),
                  out_specs=VMEM,
                  scratch=[pltpu.SemaphoreType.DMA((N_DEV - 1,)), pltpu.SemaphoreType.DMA((N_DEV - 1,))])(blk, *after)


def _allreduce8(bufs, name, after=None):
    n, n_after = len(bufs), 0 if after is None else 1
    rbs = [b.shape[0] // N_DEV for b in bufs]
    assert all(rb * N_DEV == b.shape[0] and rb % (16 if b.dtype == bf16 else 8) == 0 for rb, b in zip(rbs, bufs))

    def body(*refs):
        refs = refs[:n] + refs[n + n_after:]
        xs, outs, stages = refs[:n], refs[n:2 * n], refs[2 * n:3 * n]
        ssem, rsem = refs[3 * n:]
        x, y, c = _place()
        me = 4 * x + 2 * y + c
        peers = []
        for k in range(1, N_DEV):
            px = 1 - x if k & 4 else x
            py = 1 - y if k & 2 else y
            pc = 1 - c if k & 1 else c
            peers.append(((px, py, pc), 4 * px + 2 * py + pc))

        def blk(t, ref, idx):
            return ref.at[pl.ds(pl.multiple_of(idx * rbs[t], 8), rbs[t]), :]

        def copy(t, phase, k, src, dst, dev):
            return pltpu.make_async_remote_copy(src_ref=src, dst_ref=dst, send_sem=ssem.at[t, phase, k],
                                                recv_sem=rsem.at[t, phase, k], device_id=dev, device_id_type=MESH)

        scatter = [copy(t, 0, k, blk(t, xs[t], pidx), stages[t].at[me], dev)
                   for t in range(n) for k, (dev, pidx) in enumerate(peers)]
        for cp in scatter:
            cp.start()
        gather = []
        for t in range(n):
            stages[t][me] = blk(t, xs[t], me)[...]
            for k, (dev, pidx) in enumerate(peers):
                copy(t, 0, k, blk(t, xs[t], pidx), stages[t].at[pidx], dev).wait_recv()
            total = stages[t][0].astype(f32)
            for j in range(1, N_DEV):
                total = total + stages[t][j].astype(f32)
            blk(t, outs[t], me)[...] = total.astype(outs[t].dtype)
            sends = [copy(t, 1, k, blk(t, outs[t], me), blk(t, outs[t], me), dev) for k, (dev, pidx) in enumerate(peers)]
            for cp in sends:
                cp.start()
            gather += sends
        for t in range(n):
            for k, (dev, pidx) in enumerate(peers):
                copy(t, 1, k, blk(t, outs[t], pidx), blk(t, outs[t], pidx), dev).wait_recv()
        for cp in scatter + gather:
            cp.wait_send()

    return _pcall(body, name=name, out_shape=[_sds(b.shape, b.dtype) for b in bufs], in_specs=[VMEM] * n + [ANY] * n_after,
                  out_specs=[VMEM] * n,
                  scratch=[pltpu.VMEM((N_DEV, rb, LANES), b.dtype) for rb, b in zip(rbs, bufs)]
                  + [pltpu.SemaphoreType.DMA((n, 2, N_DEV - 1)), pltpu.SemaphoreType.DMA((n, 2, N_DEV - 1))])(
                      *bufs, *([] if after is None else [after]))


def _other_chips(x, y):
    return [((1 - x, y), 2 * (1 - x) + y), ((x, 1 - y), 2 * x + (1 - y)), ((1 - x, 1 - y), 2 * (1 - x) + (1 - y))]


HBM = pl.BlockSpec(memory_space=pltpu.HBM)
SEM = pl.BlockSpec(memory_space=pltpu.SEMAPHORE)
EFFECT = pltpu.SideEffectType.DATAFLOW_SIDE_EFFECTING


def _in_hbm(a):
    return pltpu.with_memory_space_constraint(a, pltpu.HBM)


SIBLING_ID = 1


def _sibling_handshake():
    x, y, c = _place()
    barrier = pltpu.get_barrier_semaphore()
    pl.semaphore_signal(barrier, inc=1, device_id=(x, y, 1 - c), device_id_type=MESH)
    pl.semaphore_wait(barrier, 1)
    return x, y, c


def _ag_start(layers, after, name):
    flat = [t for lay in layers for t in lay]
    n, nl = len(flat), len(layers)

    def body(*refs):
        src = refs[:n]
        sems = refs[n + 1:n + 1 + 2 * nl]
        token = refs[-1]
        x, y, c = _place()
        s_me = 2 * x + y
        t = 0
        for i, lay in enumerate(layers):
            for k in range(len(lay)):
                for j, ((px, py), _) in enumerate(_other_chips(x, y)):
                    pltpu.make_async_remote_copy(src_ref=src[t].at[s_me, c], dst_ref=src[t].at[s_me, c],
                                                 send_sem=sems[2 * i].at[3 * k + j], recv_sem=sems[2 * i + 1].at[3 * k + j],
                                                 device_id=(px, py, c), device_id_type=MESH).start()
                t += 1
        token[...] = jnp.zeros_like(token)

    sem_shapes = [pltpu.SemaphoreType.DMA((3 * len(lay),)) for lay in layers for _ in range(2)]
    out_shape = sem_shapes + [pltpu.HBM(t.shape, t.dtype) for t in flat] + [_sds((8, LANES), f32)]
    outs = pl.pallas_call(
        body, name=name, out_shape=out_shape, in_specs=[HBM] * n + [ANY],
        out_specs=[SEM] * (2 * nl) + [HBM] * n + [VMEM], input_output_aliases={t: 2 * nl + t for t in range(n)},
        compiler_params=pltpu.CompilerParams(has_side_effects=EFFECT))(*[_in_hbm(t) for t in flat], after)
    sems = [(outs[2 * i], outs[2 * i + 1]) for i in range(nl)]
    thru, t = [], 2 * nl
    for lay in layers:
        thru.append(list(outs[t:t + len(lay)]))
        t += len(lay)
    return sems, thru, outs[-1]


def _ag_wait(inflight, sems, after, name, first=0):
    n = len(inflight)

    def body(*refs):
        src, ssem, rsem = refs[:n], refs[n], refs[n + 1]
        x, y, c = _place()
        s_me = 2 * x + y
        for k in range(n):
            for j, (_, s_p) in enumerate(_other_chips(x, y)):
                cp = pltpu.make_async_remote_copy(src_ref=src[k].at[s_me, c], dst_ref=src[k].at[s_p, c],
                                                  send_sem=ssem.at[3 * (first + k) + j],
                                                  recv_sem=rsem.at[3 * (first + k) + j],
                                                  device_id=(x, y, c), device_id_type=MESH)
                cp.wait_send()
                cp.wait_recv()

    return pl.pallas_call(
        body, name=name, out_shape=[pltpu.HBM(t.shape, t.dtype) for t in inflight],
        in_specs=[HBM] * n + [SEM, SEM, ANY], out_specs=[HBM] * n, input_output_aliases={t: t for t in range(n)},
        compiler_params=pltpu.CompilerParams(has_side_effects=EFFECT))(*inflight, sems[0], sems[1], after)


def _agf_start(arrived, name):
    n = len(arrived)

    def body(*refs):
        o = refs[:n]
        ssem, rsem, token = refs[n], refs[n + 1], refs[-1]
        x, y, c = _sibling_handshake()
        for t in range(n):
            for j, (_, s_p) in enumerate(_other_chips(x, y)):
                pltpu.make_async_remote_copy(src_ref=o[t].at[s_p, c], dst_ref=o[t].at[s_p, c],
                                             send_sem=ssem.at[3 * t + j], recv_sem=rsem.at[3 * t + j],
                                             device_id=(x, y, 1 - c), device_id_type=MESH).start()
        token[...] = jnp.zeros_like(token)

    out_shape = ([pltpu.SemaphoreType.DMA((3 * n,))] * 2 + [pltpu.HBM(a.shape, bf16) for a in arrived]
                 + [_sds((8, LANES), f32)])
    outs = pl.pallas_call(
        body, name=name, out_shape=out_shape, in_specs=[HBM] * n, out_specs=[SEM, SEM] + [HBM] * n + [VMEM],
        input_output_aliases={t: 2 + t for t in range(n)},
        compiler_params=pltpu.CompilerParams(has_side_effects=EFFECT, collective_id=SIBLING_ID))(
            *[_in_hbm(a) for a in arrived])
    return (outs[0], outs[1]), list(outs[2:2 + n]), outs[-1]


def _agf_wait(sems, inflight, after, name):
    n = len(inflight)

    def body(*refs):
        o, ssem, rsem = refs[:n], refs[n], refs[n + 1]
        x, y, c = _place()
        for t in range(n):
            for j, (_, s_p) in enumerate(_other_chips(x, y)):
                cp = pltpu.make_async_remote_copy(src_ref=o[t].at[s_p, c], dst_ref=o[t].at[s_p, 1 - c],
                                                  send_sem=ssem.at[3 * t + j], recv_sem=rsem.at[3 * t + j],
                                                  device_id=(x, y, c), device_id_type=MESH)
                cp.wait_send()
                cp.wait_recv()

    return pl.pallas_call(
        body, name=name, out_shape=[pltpu.HBM(a.shape, bf16) for a in inflight],
        in_specs=[HBM] * n + [SEM, SEM, ANY], out_specs=[HBM] * n, input_output_aliases={t: t for t in range(n)},
        compiler_params=pltpu.CompilerParams(has_side_effects=EFFECT))(*inflight, sems[0], sems[1], after)


def _rs_pair_start(grads, name):
    n = len(grads)

    def body(*refs):
        g, theirs = refs[:n], refs[n:2 * n]
        ssem, rsem, token = refs[2 * n], refs[2 * n + 1], refs[-1]
        x, y, c = _sibling_handshake()
        for t in range(n):
            pltpu.make_async_remote_copy(src_ref=g[t].at[:, 1 - c], dst_ref=theirs[t], send_sem=ssem.at[t],
                                         recv_sem=rsem.at[t], device_id=(x, y, 1 - c), device_id_type=MESH).start()
        token[...] = jnp.zeros_like(token)

    lands = [lax.empty((4,) + g.shape[2:], bf16) for g in grads]
    out_shape = ([pltpu.SemaphoreType.DMA((n,))] * 2 + [pltpu.HBM(g.shape, bf16) for g in grads]
                 + [pltpu.HBM(q.shape, bf16) for q in lands] + [_sds((8, LANES), f32)])
    outs = pl.pallas_call(
        body, name=name, out_shape=out_shape, in_specs=[HBM] * (2 * n), out_specs=[SEM, SEM] + [HBM] * (2 * n) + [VMEM],
        input_output_aliases={t: 2 + t for t in range(2 * n)},
        compiler_params=pltpu.CompilerParams(has_side_effects=EFFECT, collective_id=SIBLING_ID))(
            *[_in_hbm(a) for a in list(grads) + lands])
    return (outs[0], outs[1]), list(outs[2:2 + n]), list(outs[2 + n:2 + 2 * n]), outs[-1]


def _rs_pair_wait(sems, grads, lands, after, name):
    n = len(grads)

    def body(*refs):
        g, theirs = refs[:n], refs[n:2 * n]
        ssem, rsem = refs[2 * n], refs[2 * n + 1]
        x, y, c = _place()
        for t in range(n):
            cp = pltpu.make_async_remote_copy(src_ref=g[t].at[:, 1 - c], dst_ref=theirs[t], send_sem=ssem.at[t],
                                              recv_sem=rsem.at[t], device_id=(x, y, c), device_id_type=MESH)
            cp.wait_send()
            cp.wait_recv()

    outs = pl.pallas_call(
        body, name=name, out_shape=[pltpu.HBM(a.shape, bf16) for a in list(grads) + list(lands)],
        in_specs=[HBM] * (2 * n) + [SEM, SEM] + [ANY] * len(after), out_specs=[HBM] * (2 * n),
        input_output_aliases={t: t for t in range(2 * n)},
        compiler_params=pltpu.CompilerParams(has_side_effects=EFFECT))(*grads, *lands, sems[0], sems[1], *after)
    return list(outs[:n]), list(outs[n:])


def _rs_chip_start(pairs, name):
    n = len(pairs)

    def body(*refs):
        p, q = refs[:n], refs[n:2 * n]
        ssem, rsem, token = refs[2 * n], refs[2 * n + 1], refs[-1]
        x, y, c = _place()
        for t in range(n):
            for j, ((px, py), s_p) in enumerate(_other_chips(x, y)):
                pltpu.make_async_remote_copy(src_ref=p[t].at[s_p], dst_ref=q[t].at[j], send_sem=ssem.at[3 * t + j],
                                             recv_sem=rsem.at[3 * t + j], device_id=(px, py, c), device_id_type=MESH).start()
        token[...] = jnp.zeros_like(token)

    lands = [lax.empty((3,) + p.shape[1:], bf16) for p in pairs]
    out_shape = ([pltpu.SemaphoreType.DMA((3 * n,))] * 2 + [pltpu.HBM(p.shape, bf16) for p in pairs]
                 + [pltpu.HBM(q.shape, bf16) for q in lands] + [_sds((8, LANES), f32)])
    outs = pl.pallas_call(
        body, name=name, out_shape=out_shape, in_specs=[HBM] * (2 * n), out_specs=[SEM, SEM] + [HBM] * (2 * n) + [VMEM],
        input_output_aliases={t: 2 + t for t in range(2 * n)},
        compiler_params=pltpu.CompilerParams(has_side_effects=EFFECT))(*[_in_hbm(a) for a in list(pairs) + lands])
    return (outs[0], outs[1]), list(outs[2:2 + n]), list(outs[2 + n:2 + 2 * n]), outs[-1]


def _rs_chip_wait(sems, pairs, lands, after, name):
    n = len(pairs)

    def body(*refs):
        p, q = refs[:n], refs[n:2 * n]
        ssem, rsem = refs[2 * n], refs[2 * n + 1]
        x, y, c = _place()
        for t in range(n):
            for j, (_, s_p) in enumerate(_other_chips(x, y)):
                cp = pltpu.make_async_remote_copy(src_ref=p[t].at[s_p], dst_ref=q[t].at[j], send_sem=ssem.at[3 * t + j],
                                                  recv_sem=rsem.at[3 * t + j], device_id=(x, y, c), device_id_type=MESH)
                cp.wait_send()
                cp.wait_recv()

    outs = pl.pallas_call(
        body, name=name, out_shape=[pltpu.HBM(a.shape, bf16) for a in list(pairs) + list(lands)],
        in_specs=[HBM] * (2 * n) + [SEM, SEM] + [ANY] * len(after), out_specs=[HBM] * (2 * n),
        input_output_aliases={t: t for t in range(2 * n)},
        compiler_params=pltpu.CompilerParams(has_side_effects=EFFECT))(*pairs, *lands, sems[0], sems[1], *after)
    return list(outs[:n]), list(outs[n:])


def _rs_half_start(halves, name):
    n = len(halves)

    def body(*refs):
        o = refs[:n]
        ssem, rsem, token = refs[n], refs[n + 1], refs[-1]
        x, y, c = _sibling_handshake()
        for t in range(n):
            pltpu.make_async_remote_copy(src_ref=o[t].at[c], dst_ref=o[t].at[c], send_sem=ssem.at[t],
                                         recv_sem=rsem.at[t], device_id=(x, y, 1 - c), device_id_type=MESH).start()
        token[...] = jnp.zeros_like(token)

    out_shape = ([pltpu.SemaphoreType.DMA((n,))] * 2 + [pltpu.HBM(h.shape, h.dtype) for h in halves]
                 + [_sds((8, LANES), f32)])
    outs = pl.pallas_call(
        body, name=name, out_shape=out_shape, in_specs=[HBM] * n, out_specs=[SEM, SEM] + [HBM] * n + [VMEM],
        input_output_aliases={t: 2 + t for t in range(n)},
        compiler_params=pltpu.CompilerParams(has_side_effects=EFFECT, collective_id=SIBLING_ID))(
            *[_in_hbm(h) for h in halves])
    return (outs[0], outs[1]), list(outs[2:2 + n]), outs[-1]


def _rs_half_wait(sems, inflight, after, name):
    n = len(inflight)

    def body(*refs):
        o, ssem, rsem = refs[:n], refs[n], refs[n + 1]
        x, y, c = _place()
        for t in range(n):
            cp = pltpu.make_async_remote_copy(src_ref=o[t].at[c], dst_ref=o[t].at[1 - c], send_sem=ssem.at[t],
                                              recv_sem=rsem.at[t], device_id=(x, y, c), device_id_type=MESH)
            cp.wait_send()
            cp.wait_recv()

    return pl.pallas_call(
        body, name=name, out_shape=[pltpu.HBM(h.shape, h.dtype) for h in inflight],
        in_specs=[HBM] * n + [SEM, SEM, ANY], out_specs=[HBM] * n, input_output_aliases={t: t for t in range(n)},
        compiler_params=pltpu.CompilerParams(has_side_effects=EFFECT))(*inflight, sems[0], sems[1], after)


def _row_spec(tm, cols):
    return pl.BlockSpec((tm, cols), lambda i: (i, 0))


def _vec_spec(cols, rows=1):
    return pl.BlockSpec((rows, cols), lambda i: (0, 0))


def _modulated_norm(xv, g, shift, scale):
    r = lax.rsqrt(jnp.mean(xv * xv, axis=-1, keepdims=True) + EPS)
    return (((xv * r) * g) * (1.0 + scale) + shift).astype(bf16)


def _hnorm(x, g, shift, scale):
    T, tm = x.shape[0], 256

    def body(x_ref, g_ref, sh_ref, sc_ref, h_ref):
        h_ref[...] = _modulated_norm(x_ref[...], g_ref[...], sh_ref[...], sc_ref[...])

    return _pcall(body, name="hnorm", out_shape=_sds((T, D), bf16), grid=(T // tm,),
                  in_specs=[_row_spec(tm, D), _vec_spec(D), _vec_spec(D), _vec_spec(D)],
                  out_specs=_row_spec(tm, D))(x, g, shift, scale)


def _out_proj(y2, wo, x, gate, nxt=None):
    T, tm = x.shape[0], 512

    def body(y_ref, w_ref, x_ref, g_ref, *rest):
        o = jnp.dot(y_ref[0], w_ref[0], preferred_element_type=f32)
        o = o + jnp.dot(y_ref[1], w_ref[1], preferred_element_type=f32)
        xo = x_ref[...] + g_ref[...] * o
        if nxt is None:
            xo_ref, o_ref = rest
        else:
            ng_ref, nsh_ref, nsc_ref, xo_ref, o_ref, h_ref = rest
            h_ref[...] = _modulated_norm(xo, ng_ref[...], nsh_ref[...], nsc_ref[...])
        o_ref[...] = o.astype(bf16)
        xo_ref[...] = xo

    extra = [] if nxt is None else list(nxt)
    n_out = 2 if nxt is None else 3
    return _pcall(body, name="out_proj", out_shape=[_sds((T, D), f32), _sds((T, D), bf16), _sds((T, D), bf16)][:n_out],
                  grid=(T // tm,),
                  in_specs=[pl.BlockSpec((2, tm, D), lambda i: (0, i, 0)), pl.BlockSpec((2, D, D), lambda i: (0, 0, 0)),
                            _row_spec(tm, D), _vec_spec(D)] + [_vec_spec(D)] * len(extra),
                  out_specs=[_row_spec(tm, D)] * n_out, vmem_mb=40)(y2, wo, x, gate, *extra)


def _gate_bwd_tile(dx, o_ref, gate_ref, dob_ref, dgate_ref):
    dob_ref[...] = (dx * gate_ref[...]).astype(bf16)
    dgate_ref[...] += jnp.sum(dx * o_ref[...].astype(f32), axis=0, keepdims=True)


def _loss_bwd(x, target, g, o, gate):
    T, tm = x.shape[0], 512

    def body(x_ref, t_ref, g_ref, o_ref, gate_ref, dx_ref, loss_ref, dg_ref, dob_ref, dgate_ref):
        @pl.when(pl.program_id(0) == 0)
        def _():
            loss_ref[...] = jnp.zeros_like(loss_ref)
            dg_ref[...] = jnp.zeros_like(dg_ref)
            dgate_ref[...] = jnp.zeros_like(dgate_ref)

        xv, gv = x_ref[...], g_ref[...]
        r = lax.rsqrt(jnp.mean(xv * xv, axis=-1, keepdims=True) + EPS)
        xn = xv * r
        err = xn * gv - t_ref[...]
        dy = err * (1.0 / D)
        dxn = dy * gv
        dx = r * (dxn - xn * jnp.mean(dxn * xn, axis=-1, keepdims=True))
        dx_ref[...] = dx
        dg_ref[...] += jnp.sum(dy * xn, axis=0, keepdims=True)
        loss_ref[...] += (0.5 / D) * jnp.sum(jnp.sum(err * err, axis=1, keepdims=True), axis=0, keepdims=True)
        _gate_bwd_tile(dx, o_ref, gate_ref, dob_ref, dgate_ref)

    return _pcall(body, name="loss_bwd",
                  out_shape=[_sds((T, D), f32), _sds((1, 1), f32), _sds((1, D), f32), _sds((T, D), bf16), _sds((1, D), f32)],
                  grid=(T // tm,),
                  in_specs=[_row_spec(tm, D), _row_spec(tm, D), _vec_spec(D), _row_spec(tm, D), _vec_spec(D)],
                  out_specs=[_row_spec(tm, D), pl.BlockSpec((1, 1), lambda i: (0, 0)), _vec_spec(D), _row_spec(tm, D),
                             _vec_spec(D)])(x, target, g, o, gate)


def _norm_bwd(x, dh, gin, g, scale, below=None):
    T, tm = x.shape[0], 512

    def body(x_ref, dh_ref, gin_ref, g_ref, sc_ref, *rest):
        if below is None:
            dx_ref, st_ref = rest
        else:
            o_ref, gate_ref, dx_ref, st_ref, dob_ref, dgate_ref = rest

        @pl.when(pl.program_id(0) == 0)
        def _():
            st_ref[...] = jnp.zeros_like(st_ref)
            if below is not None:
                dgate_ref[...] = jnp.zeros_like(dgate_ref)

        xv, gv, dhv = x_ref[...], g_ref[...], dh_ref[...]
        r = lax.rsqrt(jnp.mean(xv * xv, axis=-1, keepdims=True) + EPS)
        xn = xv * r
        da = dhv * (1.0 + sc_ref[...])
        dxn = da * gv
        dx = gin_ref[...] + r * (dxn - xn * jnp.mean(dxn * xn, axis=-1, keepdims=True))
        dx_ref[...] = dx
        st_ref[0:1, :] += jnp.sum(dhv, axis=0, keepdims=True)
        st_ref[1:2, :] += jnp.sum(dhv * (xn * gv), axis=0, keepdims=True)
        st_ref[2:3, :] += jnp.sum(da * xn, axis=0, keepdims=True)
        if below is not None:
            _gate_bwd_tile(dx, o_ref, gate_ref, dob_ref, dgate_ref)

    out_shape = [_sds((T, D), f32), _sds((8, D), f32)]
    in_specs = [_row_spec(tm, D), _row_spec(tm, D), _row_spec(tm, D), _vec_spec(D), _vec_spec(D)]
    out_specs = [_row_spec(tm, D), _vec_spec(D, 8)]
    args = [x, dh, gin, g, scale]
    if below is not None:
        out_shape += [_sds((T, D), bf16), _sds((1, D), f32)]
        in_specs += [_row_spec(tm, D), _vec_spec(D)]
        out_specs += [_row_spec(tm, D), _vec_spec(D)]
        args += list(below)
    return _pcall(body, name="norm_bwd", out_shape=out_shape, grid=(T // tm,), in_specs=in_specs,
                  out_specs=out_specs)(*args)


STEPS = 4
ADAMW_STEPS = 8


def _cast_place(place, ws, layer, after=None):
    n = len(ws)

    def body(place_ref, *refs):
        for t in range(n):
            refs[-n + t][...] = refs[t][...].astype(bf16)

    def tile(w):
        return w.shape[1] // STEPS, w.shape[2]

    extra = [] if after is None else [after]
    return _pcall(body, name="cast_place", out_shape=[_sds((4,) + w.shape[1:], bf16) for w in ws], grid=(STEPS,),
                  prefetch=1,
                  in_specs=[pl.BlockSpec((None,) + tile(w), lambda i, pr: (layer, i, 0)) for w in ws] + [ANY] * len(extra),
                  out_specs=[pl.BlockSpec((None,) + tile(w), lambda i, pr: (pr[0], i, 0)) for w in ws])(
                      place, *ws, *extra)


def _rs_add(place, grads, theirs):
    n = len(grads)

    def body(place_ref, *refs):
        for t in range(n):
            refs[2 * n + t][...] = (refs[t][...].astype(f32) + refs[n + t][...].astype(f32)).astype(bf16)

    mine = [pl.BlockSpec((None, None) + q.shape[1:], lambda s, pr: (s, pr[1], 0, 0)) for q in theirs]
    shard = [pl.BlockSpec((None,) + q.shape[1:], lambda s, pr: (s, 0, 0)) for q in theirs]
    return _pcall(body, name="rs_add", out_shape=[_sds(q.shape, bf16) for q in theirs], grid=(4,), prefetch=1,
                  in_specs=mine + shard, out_specs=shard)(place, *grads, *theirs)


def _rs_sum(place, pairs, slots):
    n, steps = len(pairs), 4

    def body(place_ref, *refs):
        for t in range(n):
            p_ref, q_ref = refs[t], refs[n + t]
            total = ((p_ref[...].astype(f32) + q_ref[0].astype(f32)) + q_ref[1].astype(f32)) + q_ref[2].astype(f32)
            refs[2 * n + t][...] = total.astype(bf16)

    def tile(q):
        return q.shape[1] // steps, q.shape[2]

    return _pcall(body, name="rs_sum", out_shape=[_sds((2,) + q.shape[1:], bf16) for q in slots], grid=(steps,),
                  prefetch=1,
                  in_specs=[pl.BlockSpec((None,) + tile(q), lambda i, pr: (pr[0], i, 0)) for q in slots]
                  + [pl.BlockSpec((3,) + tile(q), lambda i, pr: (0, i, 0)) for q in slots],
                  out_specs=[pl.BlockSpec((None,) + tile(q), lambda i, pr: (pr[1], i, 0)) for q in slots])(
                      place, *pairs, *slots)


def _adamw_math(w, g, m, v):
    m = ADAM_B1 * m + (1.0 - ADAM_B1) * g
    v = ADAM_B2 * v + (1.0 - ADAM_B2) * jnp.square(g)
    m_hat = m / (1.0 - ADAM_B1 ** ADAM_STEP)
    v_hat = v / (1.0 - ADAM_B2 ** ADAM_STEP)
    delta = -ADAM_LR * (m_hat / (jnp.sqrt(v_hat) + ADAM_EPS) + ADAM_WD * w)
    return delta, m, v


def _adamw_layer(layer, items):
    n = len(items)

    def body(*refs):
        outs = refs[-4 * n:]
        for t in range(n):
            w_ref, g_ref, m_ref, v_ref = refs[4 * t:4 * t + 4]
            g = g_ref[...].astype(f32)
            outs[4 * t][...] = g
            outs[4 * t + 1][...], outs[4 * t + 2][...], outs[4 * t + 3][...] = _adamw_math(
                w_ref[...], g, m_ref[...], v_ref[...])

    args, in_specs, out_specs, out_shape = [], [], [], []
    for w, g, m, v, _ in items:
        tr, cols = w.shape[1] // ADAMW_STEPS, w.shape[2]
        spec = pl.BlockSpec((None, tr, cols), lambda i: (layer, i, 0))
        args += [w, g, m, v]
        in_specs += [spec, pl.BlockSpec((tr, cols), lambda i: (i, 0)), spec, spec]
        out_specs += [spec] * 4
        out_shape += [_sds(w.shape, f32)] * 4
    aliases = {}
    for t, it in enumerate(items):
        if it[4] is not None:
            for k in range(4):
                aliases[len(args)] = 4 * t + k
                args.append(it[4][k])
                in_specs.append(ANY)
    res = _pcall(body, name="adamw", out_shape=out_shape, grid=(ADAMW_STEPS,), in_specs=in_specs, out_specs=out_specs,
                 aliases=aliases)(*args)
    return [tuple(res[4 * t:4 * t + 4]) for t in range(n)]


def _adamw_small(items):
    n = len(items)

    def body(*refs):
        ins, outs = refs[:4 * n], refs[4 * n:]
        for t in range(n):
            w_ref, g_ref, m_ref, v_ref = ins[4 * t:4 * t + 4]
            if len(g_ref.shape) == len(w_ref.shape) + 1:
                g = g_ref[0]
                for b in range(1, g_ref.shape[0]):
                    g = g + g_ref[b]
            else:
                g = g_ref[...]
            d, m, v = _adamw_math(w_ref[...], g, m_ref[...], v_ref[...])
            outs[4 * t][...], outs[4 * t + 1][...], outs[4 * t + 2][...], outs[4 * t + 3][...] = g, d, m, v

    out_shape = [_sds(w.shape, f32) for (w, _, _, _) in items for _ in range(4)]
    flat = [a for it in items for a in it]
    res = _pcall(body, name="adamw_small", out_shape=out_shape, in_specs=[VMEM] * (4 * n),
                 out_specs=[VMEM] * (4 * n))(*flat)
    return [tuple(res[4 * t:4 * t + 4]) for t in range(n)]


NN = ((1,), (0,))
NT = ((1,), (1,))
TN = ((0,), (0,))


def _mm(name, a, b, *, grid, a_spec, b_spec, out_shape, out_spec, dims, vmem_mb=None):
    def body(a_ref, b_ref, o_ref):
        r = lax.dot_general(a_ref[...], b_ref[...], (dims, ((), ())), preferred_element_type=f32)
        o_ref[...] = r.astype(o_ref.dtype)

    return _pcall(body, name=name, out_shape=out_shape, grid=grid, in_specs=[a_spec, b_spec], out_specs=out_spec,
                  vmem_mb=vmem_mb)(a, b)


def _whole(shape):
    return pl.BlockSpec(shape, lambda j: (0,) * len(shape))


def _split_spec(rows, tile, per_split):
    return pl.BlockSpec((None, rows, tile), lambda j: (j // per_split, 0, j % per_split))


class _Proj:
    def __init__(self, n, splits, tile):
        self.n, self.splits, self.tile = n, splits, tile
        self.steps = n // tile
        self.w_per = n // 4 // tile
        self.a_per = n // splits // tile
        assert self.w_per * tile * 4 == n and self.a_per * tile * splits == n

    def fwd(self, hb, wg):
        T = hb.shape[0]
        sub, tile, w_per = FWD_TILES, self.tile, self.w_per
        wide = sub * tile
        a_per = self.n // self.splits // wide
        assert a_per * wide * self.splits == self.n

        def w_tile(q):
            return pl.BlockSpec((None, D, tile), lambda j: ((sub * j + q) // w_per, 0, (sub * j + q) % w_per))

        def body(a_ref, *rest):
            w = jnp.concatenate([rest[q][...] for q in range(sub)], axis=1)
            rest[sub][...] = jnp.dot(a_ref[...], w, preferred_element_type=f32).astype(bf16)

        return _pcall(body, name="proj_fwd", out_shape=_sds((self.splits, T, self.n // self.splits), bf16),
                      grid=(self.n // wide,), in_specs=[_whole((T, D))] + [w_tile(q) for q in range(sub)],
                      out_specs=pl.BlockSpec((None, T, wide), lambda j: (j // a_per, 0, j % a_per)),
                      vmem_mb=40 if wide > 512 else None)(hb, *([wg] * sub))

    def dw(self, hb, dp):
        T = hb.shape[0]
        return _mm("proj_dw", hb, dp, grid=(self.steps,), a_spec=_whole((T, D)),
                   b_spec=_split_spec(T, self.tile, self.a_per), out_shape=_sds((4, D, self.n // 4), bf16),
                   out_spec=_split_spec(D, self.tile, self.w_per), dims=TN)

    def dh(self, dp, wg, after=None):
        T = dp.shape[1]
        extra = [] if after is None else [after]
        sub, tile, w_per = DH_WIDE // self.tile, self.tile, self.w_per
        a_per = self.n // self.splits // DH_WIDE
        assert sub * tile == DH_WIDE and a_per * DH_WIDE * self.splits == self.n

        def w_tile(q):
            return pl.BlockSpec((None, D, tile), lambda k: ((sub * k + q) // w_per, 0, (sub * k + q) % w_per))

        def body(a_ref, *rest):
            o_ref = rest[-1]
            w = jnp.concatenate([rest[q][...] for q in range(sub)], axis=1)
            r = lax.dot_general(a_ref[...], w, (NT, ((), ())), preferred_element_type=f32)

            @pl.when(pl.program_id(0) == 0)
            def _():
                o_ref[...] = r

            @pl.when(pl.program_id(0) > 0)
            def _():
                o_ref[...] += r

        return _pcall(body, name="proj_dh", out_shape=_sds((T, D), f32), grid=(self.n // DH_WIDE,),
                      in_specs=[pl.BlockSpec((None, T, DH_WIDE), lambda k: (k // a_per, 0, k % a_per))]
                      + [w_tile(q) for q in range(sub)] + [ANY] * len(extra),
                      out_specs=_whole((T, D)), vmem_mb=40)(dp, *([wg] * sub), *extra)


EVEN_PROJ = _Proj(7 * D, 7, 256)
ODD_PROJ = _Proj(4 * D, 2, 512)


def _out_bwd(dob, wo, y2):
    T = dob.shape[0]
    w_spec = pl.BlockSpec((None, 512, D), lambda j: (j, 0, 0))

    def body(dob_ref, w_ref, y_ref, dy_ref, dw_ref):
        dob_v = dob_ref[...]
        dy_ref[...] = lax.dot_general(dob_v, w_ref[...], (NT, ((), ())), preferred_element_type=f32).astype(bf16)
        dw_ref[...] = lax.dot_general(y_ref[...], dob_v, (TN, ((), ())), preferred_element_type=f32).astype(bf16)

    return _pcall(body, name="out_bwd", out_shape=[_sds((2, T, D), bf16), _sds((4, 512, D), bf16)], grid=(4,),
                  in_specs=[_whole((T, D)), w_spec, _split_spec(T, 512, 2)],
                  out_specs=[_split_spec(T, 512, 2), w_spec])(dob, wo, y2)


def _head_spec(lead, T):
    return pl.BlockSpec((lead, T, HEAD), lambda h: (0, 0, h))


def _head_vec(rows):
    return pl.BlockSpec((rows, HEAD), lambda h: (0, h))


_HEAD_MAT = pl.BlockSpec((None, HEAD, HEAD), lambda h: (h, 0, 0))


def _causal():
    return lax.broadcasted_iota(jnp.int32, (HEAD, HEAD), 0) >= lax.broadcasted_iota(jnp.int32, (HEAD, HEAD), 1)


def _layernorm_head(v):
    mu = jnp.mean(v, axis=-1, keepdims=True)
    d = v - mu
    rstd = lax.rsqrt(jnp.mean(d * d, axis=-1, keepdims=True) + EPS)
    return d * rstd, rstd


def _even_fwd(p7, conv_w, ln_g, ln_b, sgu_w, sgu_bias):
    T, C = p7.shape[1], CHUNK_ROWS

    def body(p_ref, cw_ref, lg_ref, lb_ref, w_ref, b_ref, y_ref):
        w0, w1, w2 = cw_ref[0:1, :], cw_ref[1:2, :], cw_ref[2:3, :]
        wm = jnp.where(_causal(), w_ref[...], 0.0).astype(bf16)
        bias, lg, lb = b_ref[...], lg_ref[...], lb_ref[...]

        def step(i, halo):
            rows = pl.ds(pl.multiple_of(i * C, C), C)
            ah, ab, ac, az, u, v, zb = (p_ref[k, rows, :].astype(f32) for k in range(7))
            tt = ac * ah
            ext = jnp.concatenate([halo, tt], axis=0)
            cv = w2 * tt + w1 * pltpu.roll(ext, 1, 0)[HALO_CONV:] + w0 * pltpu.roll(ext, 2, 0)[HALO_CONV:]
            y_ref[0, rows, :] = (ab * cv * _silu(az)).astype(bf16)
            vhat, _ = _layernorm_head(v)
            vn = (vhat * lg + lb).astype(bf16)
            mix = jnp.concatenate([jnp.dot(wm, vn[k * HEAD:(k + 1) * HEAD], preferred_element_type=f32) + bias
                                   for k in range(C // HEAD)], axis=0)
            y_ref[1, rows, :] = (u * mix * _silu(zb)).astype(bf16)
            return tt[C - HALO_CONV:]

        lax.fori_loop(0, T // C, step, jnp.zeros((HALO_CONV, HEAD), f32))

    return _pcall(body, name="even_fwd", out_shape=_sds((2, T, D), bf16), grid=(NH,),
                  in_specs=[_head_spec(7, T), _head_vec(3), _head_vec(1), _head_vec(1), _HEAD_MAT, _HEAD_MAT],
                  out_specs=_head_spec(2, T))(p7, conv_w, ln_g, ln_b, sgu_w, sgu_bias)


def _even_bwd(p7, dy2, conv_w, ln_g, ln_b, sgu_w, sgu_bias):
    T, C = p7.shape[1], CHUNK_ROWS
    n_chunks = T // C

    def body(p_ref, dy_ref, cw_ref, lg_ref, lb_ref, w_ref, b_ref,
             dp_ref, dcw_ref, dlg_ref, dlb_ref, dw_ref, dms_ref, dcv_s):
        w0, w1, w2 = cw_ref[0:1, :], cw_ref[1:2, :], cw_ref[2:3, :]
        tri = _causal()
        wm = jnp.where(tri, w_ref[...], 0.0).astype(bf16)
        bias, lg, lb = b_ref[...], lg_ref[...], lb_ref[...]
        dw_ref[...] = jnp.zeros_like(dw_ref)
        dms_ref[...] = jnp.zeros_like(dms_ref)

        def fwd_step(i, carry):
            halo, a0, a1, a2, alg, alb = carry
            rows = pl.ds(pl.multiple_of(i * C, C), C)
            ah, ab, ac, az = (p_ref[k, rows, :].astype(f32) for k in range(4))
            dya = dy_ref[0, rows, :].astype(f32)
            tt = ac * ah
            ext = jnp.concatenate([halo, tt], axis=0)
            t1, t2 = pltpu.roll(ext, 1, 0)[HALO_CONV:], pltpu.roll(ext, 2, 0)[HALO_CONV:]
            cv = w2 * tt + w1 * t1 + w0 * t2
            sa, dsa = _silu_and_grad(az)
            g1 = dya * sa
            dp_ref[1, rows, :] = (g1 * cv).astype(bf16)
            dp_ref[3, rows, :] = (dya * ab * cv * dsa).astype(bf16)
            dcv = g1 * ab
            dcv_s[rows, :] = dcv
            a2 = a2 + jnp.sum(dcv * tt, axis=0, keepdims=True)
            a1 = a1 + jnp.sum(dcv * t1, axis=0, keepdims=True)
            a0 = a0 + jnp.sum(dcv * t2, axis=0, keepdims=True)

            u, zb, dyb = p_ref[4, rows, :].astype(f32), p_ref[6, rows, :].astype(f32), dy_ref[1, rows, :].astype(f32)
            vhat, rstd = _layernorm_head(p_ref[5, rows, :].astype(f32))
            vn = (vhat * lg + lb).astype(bf16)
            sb, dsb = _silu_and_grad(zb)
            mix = jnp.concatenate([jnp.dot(wm, vn[k * HEAD:(k + 1) * HEAD], preferred_element_type=f32) + bias
                                   for k in range(C // HEAD)], axis=0)
            dp_ref[4, rows, :] = (dyb * mix * sb).astype(bf16)
            dp_ref[6, rows, :] = (dyb * u * mix * dsb).astype(bf16)
            dmix = dyb * u * sb
            dvn_parts = []
            for k in range(C // HEAD):
                dm = dmix[k * HEAD:(k + 1) * HEAD]
                dmb = dm.astype(bf16)
                dvn_parts.append(lax.dot_general(wm, dmb, (TN, ((), ())), preferred_element_type=f32))
                dw_ref[...] += lax.dot_general(dmb, vn[k * HEAD:(k + 1) * HEAD], (NT, ((), ())),
                                               preferred_element_type=f32)
                dms_ref[...] += dm
            dvn = jnp.concatenate(dvn_parts, axis=0)
            alg = alg + jnp.sum(dvn * vhat, axis=0, keepdims=True)
            alb = alb + jnp.sum(dvn, axis=0, keepdims=True)
            dvh = dvn * lg
            dv = rstd * (dvh - jnp.mean(dvh, axis=-1, keepdims=True)
                         - vhat * jnp.mean(dvh * vhat, axis=-1, keepdims=True))
            dp_ref[5, rows, :] = dv.astype(bf16)
            return tt[C - HALO_CONV:], a0, a1, a2, alg, alb

        zrow = jnp.zeros((1, HEAD), f32)
        _, a0, a1, a2, alg, alb = lax.fori_loop(
            0, n_chunks, fwd_step, (jnp.zeros((HALO_CONV, HEAD), f32), zrow, zrow, zrow, zrow, zrow))
        dcw_ref[0:1, :], dcw_ref[1:2, :], dcw_ref[2:3, :] = a0, a1, a2
        dlg_ref[...], dlb_ref[...] = alg, alb
        dw_ref[...] = jnp.where(tri, dw_ref[...], 0.0)

        def bwd_step(k, halo):
            rows = pl.ds(pl.multiple_of((n_chunks - 1 - k) * C, C), C)
            dcv = dcv_s[rows, :]
            ext = jnp.concatenate([dcv, halo], axis=0)
            n1 = pltpu.roll(ext, C + HALO_CONV - 1, 0)[:C]
            n2 = pltpu.roll(ext, C + HALO_CONV - 2, 0)[:C]
            dtt = w2 * dcv + w1 * n1 + w0 * n2
            dp_ref[2, rows, :] = (dtt * p_ref[0, rows, :].astype(f32)).astype(bf16)
            dp_ref[0, rows, :] = (dtt * p_ref[2, rows, :].astype(f32)).astype(bf16)
            return dcv[:HALO_CONV]

        lax.fori_loop(0, n_chunks, bwd_step, jnp.zeros((HALO_CONV, HEAD), f32))

    out_shape = [_sds((7, T, D), bf16), _sds((3, D), f32), _sds((1, D), f32), _sds((1, D), f32),
                 _sds((NH, HEAD, HEAD), f32), _sds((NH, HEAD, HEAD), f32)]
    return _pcall(body, name="even_bwd", out_shape=out_shape, grid=(NH,),
                  in_specs=[_head_spec(7, T), _head_spec(2, T), _head_vec(3), _head_vec(1), _head_vec(1),
                            _HEAD_MAT, _HEAD_MAT],
                  out_specs=[_head_spec(7, T), _head_vec(3), _head_vec(1), _head_vec(1), _HEAD_MAT, _HEAD_MAT],
                  scratch=[pltpu.VMEM((T, HEAD), f32)])(p7, dy2, conv_w, ln_g, ln_b, sgu_w, sgu_bias)


def _window_sum(ext, win, towards_past):
    n, k, s = ext.shape[0], 1, ext
    while k < win:
        s = s + pltpu.roll(s, k if towards_past else n - k, 0)
        k *= 2
    return s


def _pool_count(i, C, win):
    t = i * C + lax.broadcasted_iota(jnp.int32, (C, 1), 0)
    cnt = jnp.minimum(t + 1, win).astype(f32)
    return cnt, 1.0 / cnt


def _group_specs(T):
    p_spec = pl.BlockSpec((None, T, GC), lambda g: (0, 0, g))
    z_spec = pl.BlockSpec((None, T, GC), lambda g: (1, 0, g))
    pw_spec = pl.BlockSpec((4, GC // 4, GC), lambda g: (0, g, 0))
    ps_spec = pl.BlockSpec((1, GC), lambda g: (0, g))
    y_spec = pl.BlockSpec((None, T, GC), lambda g: (g // 2, 0, g % 2))
    return p_spec, z_spec, pw_spec, ps_spec, y_spec


def _odd_fwd(p2, pool_wg, pool_scale):
    T, C = p2.shape[1], CHUNK_ROWS
    p_spec, z_spec, pw_spec, ps_spec, y_spec = _group_specs(T)

    def body(p_ref, z_ref, pw_ref, ps_ref, y_ref):
        pw, ps = pw_ref[...].reshape(GC, GC), ps_ref[...]

        def run(win):
            def step(i, halo):
                rows = pl.ds(pl.multiple_of(i * C, C), C)
                p = p_ref[rows, :].astype(f32)
                s = _window_sum(jnp.concatenate([halo, p], axis=0), win, True)[HALO_POOL:]
                pooled = s * _pool_count(i, C, win)[1] - p
                ypre = jnp.dot(pooled.astype(bf16), pw, preferred_element_type=f32)
                y_ref[rows, :] = (ypre * ps * _silu(z_ref[rows, :].astype(f32))).astype(bf16)
                return p[C - HALO_POOL:]

            lax.fori_loop(0, T // C, step, jnp.zeros((HALO_POOL, GC), f32))

        for gi, win in enumerate(WINDOWS):
            pl.when(pl.program_id(0) == gi)(functools.partial(run, win))

    return _pcall(body, name="odd_fwd", out_shape=_sds((2, T, D), bf16), grid=(len(WINDOWS),),
                  in_specs=[p_spec, z_spec, pw_spec, ps_spec], out_specs=y_spec)(p2, p2, pool_wg, pool_scale)


def _odd_bwd(p2, dy2, pool_wg, pool_scale):
    T, C = p2.shape[1], CHUNK_ROWS
    n_chunks = T // C
    p_spec, z_spec, pw_spec, ps_spec, y_spec = _group_specs(T)

    def body(p_ref, z_ref, dy_ref, pw_ref, ps_ref, dp_ref, dpw_ref, dps_ref, q_s, acc_s):
        pw, ps = pw_ref[...].reshape(GC, GC), ps_ref[...]

        def run(win):
            acc_s[...] = jnp.zeros_like(acc_s)

            def fwd_step(i, carry):
                halo, aps = carry
                rows = pl.ds(pl.multiple_of(i * C, C), C)
                p, z, dy = p_ref[rows, :].astype(f32), z_ref[rows, :].astype(f32), dy_ref[rows, :].astype(f32)
                _, inv_cnt = _pool_count(i, C, win)
                s = _window_sum(jnp.concatenate([halo, p], axis=0), win, True)[HALO_POOL:]
                pb = (s * inv_cnt - p).astype(bf16)
                ypre = jnp.dot(pb, pw, preferred_element_type=f32)
                sz, dsz = _silu_and_grad(z)
                aps = aps + jnp.sum(dy * ypre * sz, axis=0, keepdims=True)
                dp_ref[1, rows, :] = (dy * ypre * ps * dsz).astype(bf16)
                dyp = (dy * ps * sz).astype(bf16)
                acc_s[...] += lax.dot_general(pb, dyp, (TN, ((), ())), preferred_element_type=f32)
                dpool = lax.dot_general(dyp, pw, (NT, ((), ())), preferred_element_type=f32)
                q_s[rows, :] = dpool * inv_cnt
                return p[C - HALO_POOL:], aps

            _, aps = lax.fori_loop(0, n_chunks, fwd_step, (jnp.zeros((HALO_POOL, GC), f32), jnp.zeros((1, GC), f32)))
            dps_ref[...] = aps
            dpw_ref[...] = acc_s[...].reshape(4, GC // 4, GC).astype(bf16)

            def bwd_step(k, halo):
                i = n_chunks - 1 - k
                rows = pl.ds(pl.multiple_of(i * C, C), C)
                q = q_s[rows, :]
                s = _window_sum(jnp.concatenate([q, halo], axis=0), win, False)[:C]
                dp_ref[0, rows, :] = (s - q * _pool_count(i, C, win)[0]).astype(bf16)
                return q[:HALO_POOL]

            lax.fori_loop(0, n_chunks, bwd_step, jnp.zeros((HALO_POOL, GC), f32))

        for gi, win in enumerate(WINDOWS):
            pl.when(pl.program_id(0) == gi)(functools.partial(run, win))

    out_shape = [_sds((2, T, 2 * D), bf16), _sds((4, GC, GC), bf16), _sds((1, 2 * D), f32)]
    return _pcall(body, name="odd_bwd", out_shape=out_shape, grid=(len(WINDOWS),),
                  in_specs=[p_spec, z_spec, y_spec, pw_spec, ps_spec],
                  out_specs=[pl.BlockSpec((2, T, GC), lambda g: (0, 0, g)), pw_spec, ps_spec],
                  scratch=[pltpu.VMEM((T, GC), f32), pltpu.VMEM((GC, GC), f32)], vmem_mb=44)(
                      p2, p2, dy2, pool_wg, pool_scale)


def _ada_fwd(c_all, ada_w):
    cols = ada_w.shape[2]

    def body(c_ref, w_ref, o_ref):
        o_ref[...] = jnp.dot(_silu(c_ref[...]), w_ref[...], preferred_element_type=f32,
                             precision=lax.Precision.HIGHEST)

    return _pcall(body, name="ada_fwd", out_shape=_sds((4, N_DEV, cols), f32), grid=(4,),
                  in_specs=[pl.BlockSpec((N_DEV, D), lambda i: (0, 0)), pl.BlockSpec((None, D, cols), lambda i: (i, 0, 0))],
                  out_specs=pl.BlockSpec((None, N_DEV, cols), lambda i: (i, 0, 0)))(c_all, ada_w)


def _ada_bwd(c_all_t, dmod, w, m, v):
    cols, tr = w.shape[2], 256
    spec = pl.BlockSpec((None, tr, cols), lambda l, i: (l, i, 0))

    def body(c_ref, dm_ref, w_ref, m_ref, v_ref, g_ref, d_ref, mo_ref, vo_ref):
        sc = _silu(c_ref[...])
        g = sc[:, 0:1] * dm_ref[0:1, :]
        for b in range(1, N_DEV):
            g = g + sc[:, b:b + 1] * dm_ref[b:b + 1, :]
        g_ref[...] = g
        d_ref[...], mo_ref[...], vo_ref[...] = _adamw_math(w_ref[...], g, m_ref[...], v_ref[...])

    return _pcall(body, name="ada_bwd", out_shape=[_sds(w.shape, f32)] * 4, grid=(4, D // tr),
                  in_specs=[pl.BlockSpec((tr, N_DEV), lambda l, i: (i, 0)),
                            pl.BlockSpec((None, N_DEV, cols), lambda l, i: (l, 0, 0)), spec, spec, spec],
                  out_specs=[spec] * 4)(c_all_t, dmod, w, m, v)


def _layer_fwd(even, x, hb, gate, w, nxt, before_out=None, after_proj=None):
    if even:
        w_in, w_out, conv_w, ln_g, ln_b, sgu_w, sgu_b = w
        bias = jnp.broadcast_to(sgu_b[:, :, None], (NH, HEAD, HEAD))
        p = EVEN_PROJ.fwd(hb, w_in)
        if after_proj is not None:
            conv_w = conv_w + after_proj(p)[0:1, 0:1]
        y2 = _even_fwd(p, conv_w, ln_g, ln_b, sgu_w, bias)
    else:
        w_in, pool_w, w_out, pool_scale = w
        p = ODD_PROJ.fwd(hb, w_in)
        y2 = _odd_fwd(p, pool_w, pool_scale if after_proj is None else pool_scale + after_proj(p)[0:1, 0:1])
    if before_out is not None:
        late_w_out, tok = before_out(y2)
        if late_w_out is not None:
            w_out = late_w_out
            w = (w_in, w_out) + tuple(w[2:]) if even else (w_in, pool_w, w_out, pool_scale)
        if tok is not None:
            gate = gate + tok[0:1, 0:1]
    outs = _out_proj(y2, w_out.reshape(2, D, D), x, gate, nxt)
    return outs[0], (None if nxt is None else outs[2]), (x, hb, p, y2, outs[1]), w


def _layer_bwd(even, gin, dob, dgate, saved, scale, g, w, below=None, send=None):
    x_in, hb, p, y2, o = saved
    if even:
        w_in, w_out, conv_w, ln_g, ln_b, sgu_w, sgu_b = w
        bias = jnp.broadcast_to(sgu_b[:, :, None], (NH, HEAD, HEAD))
        dy2, dwo = _out_bwd(dob, w_out, y2)
        dp, dconv, dlg, dlb, dsw, dms = _even_bwd(p, dy2, conv_w, ln_g, ln_b, sgu_w, bias)
        proj = EVEN_PROJ
        small = dict(conv_w=dconv, ln_g=dlg, ln_b=dlb, sgu_w=dsw, sgu_b=jnp.sum(dms, axis=-1))
        big = [proj.dw(hb, dp), dwo]
    else:
        w_in, pool_w, w_out, pool_scale = w
        dy2, dwo = _out_bwd(dob, w_out, y2)
        dp, dpw, dps = _odd_bwd(p, dy2, pool_w, pool_scale)
        proj = ODD_PROJ
        small = dict(pool_scale=dps)
        big = [proj.dw(hb, dp), dpw, dwo]
    tok = None
    if send is not None:
        big, tok = send(big)
    dh = proj.dh(dp, w_in, tok)
    res = _norm_bwd(x_in, dh, gin, g, scale, below)
    stats = res[1]
    return (res[0], (None if below is None else (res[2], res[3])), big, small,
            jnp.concatenate([stats[0:2], dgate], axis=0), stats[2:3])


def _pack_rows(parts):
    rows = [p.reshape(-1, LANES) for p in parts]
    total = sum(r.shape[0] for r in rows)
    padded = -(-total // (8 * N_DEV)) * (8 * N_DEV)
    if padded > total:
        rows.append(jnp.zeros((padded - total, LANES), f32))
    return jnp.concatenate(rows, axis=0)


def _unpack_rows(buf, shapes):
    out, r = [], 0
    for shp in shapes:
        n = 1
        for d in shp:
            n *= d
        out.append(buf[r:r + n // LANES].reshape(shp))
        r += n // LANES
    return out


def kernel(x, c, norm_g, ada_w, ada_b, ab_w_in, ab_conv_w, ab_ln_g, ab_ln_b, ab_sgu_w, ab_sgu_b, ab_w_out, c_w_in, c_pool_w, c_pool_scale, c_w_out, final_g, loss_target, m_norm_g, m_ada_w, m_ada_b, m_ab_w_in, m_ab_conv_w, m_ab_ln_g, m_ab_ln_b, m_ab_sgu_w, m_ab_sgu_b, m_ab_w_out, m_c_w_in, m_c_pool_w, m_c_pool_scale, m_c_w_out, m_final_g, v_norm_g, v_ada_w, v_ada_b, v_ab_w_in, v_ab_conv_w, v_ab_ln_g, v_ab_ln_b, v_ab_sgu_w, v_ab_sgu_b, v_ab_w_out, v_c_w_in, v_c_pool_w, v_c_pool_scale, v_c_w_out, v_final_g):
    ix, iy, ic = _place()
    chip, dev = 2 * ix + iy, 4 * ix + 2 * iy + ic
    n_even, n_odd = ab_w_in.shape[0], c_w_in.shape[0]
    depth = n_even + n_odd
    acols = ada_w.shape[2]

    place = jnp.stack([chip, ic]).astype(jnp.int32)
    even_names, odd_names = ["ab_w_in", "ab_w_out"], ["c_w_in", "c_pool_w", "c_w_out"]
    params = {"ab_w_in": (ab_w_in, m_ab_w_in, v_ab_w_in), "ab_w_out": (ab_w_out, m_ab_w_out, v_ab_w_out),
              "c_w_in": (c_w_in, m_c_w_in, v_c_w_in), "c_w_out": (c_w_out, m_c_w_out, v_c_w_out),
              "c_pool_w": tuple(a.reshape(n_odd, GC, GC) for a in (c_pool_w, m_c_pool_w, v_c_pool_w))}

    def placed(names, layer, after=None):
        ws = [params[nm][0] for nm in names]
        return [p.reshape(4, 2, p.shape[1] // 2, p.shape[2]) for p in _cast_place(place, ws, layer, after)]

    def whole(arrays):
        return [g.reshape(4, 2 * g.shape[2], g.shape[3]) for g in arrays]

    first = _gather8(jnp.concatenate([c, ab_conv_w.reshape(1, -1), c_pool_scale.reshape(1, -1)], axis=1), "gather_c")
    c_all, small_all = first[:, 0, :D], first[0::2, 0, D:]
    sems_a, in_a, tok = _ag_start([placed(even_names[:1], 0)], first[0:1, 0, 0:LANES], "ag_start_0a")
    modp = _ada_fwd(c_all, ada_w)
    later = [placed(even_names[1:], 0, tok)]
    later += [placed(even_names if i % 2 == 0 else odd_names, i // 2, tok) for i in range(1, depth)]
    modg = _gather8(modp + tok[0:1, 0:1], "gather_mod", [lay[-1] for lay in later])
    mod_rows = lax.dynamic_index_in_dim(modg[0::2], dev, axis=2, keepdims=False)
    mod = jnp.transpose(mod_rows, (1, 0, 2)).reshape(depth, 3 * D) + ada_b
    mods = [(mod[i:i + 1, 0:D], mod[i:i + 1, D:2 * D], mod[i:i + 1, 2 * D:3 * D]) for i in range(depth)]

    def shard_cols(a, width):
        return lax.dynamic_slice_in_dim(a, chip * width, width, axis=a.ndim - 1)

    n_conv = ab_conv_w.size
    conv_all = small_all[:, :n_conv].reshape(4, n_even, 3, D // 4)
    conv_full = jnp.transpose(conv_all, (1, 2, 0, 3)).reshape(n_even, 3, D)
    scale_all = small_all[:, n_conv:].reshape(4, n_odd, 2 * D // 4)
    scale_full = jnp.transpose(scale_all, (1, 0, 2)).reshape(n_odd, 2 * D)

    gathers_done = mod[0:1, 0:LANES] + scale_full[0:1, 0:LANES]
    sems_b, in_b, tok = _ag_start(later[:1], gathers_done, "ag_start_0b")
    sems_r, in_r, tok = _ag_start(later[1:], tok, "ag_start_rest")

    x_cur, saved, weights, first_part, last_part = x[0], [], [], {}, {}

    def hand_off(arrays, sems, after, first, tag):
        arrived = _ag_wait(arrays, sems, after, f"ag_wait_{tag}", first)
        sems_f, inflight, tok = _agf_start(arrived, f"agf_start_{tag}")
        return (sems_f, inflight), tok

    first_part[0], tok = hand_off(in_a[0], sems_a[0], tok, 0, "0a")
    hb = _hnorm(x_cur, norm_g[0:1], mods[0][0] + tok[0:1, 0:1], mods[0][1])
    for i in range(depth):
        j = i // 2
        full = whole(_agf_wait(*first_part.pop(i), hb if i == 0 else x_cur, f"agf_wait_{i}a")) + [None]
        if i % 2 == 0:
            w = (full[0], None, conv_full[j], ab_ln_g[j:j + 1], ab_ln_b[j:j + 1], ab_sgu_w[j], ab_sgu_b[j])
        else:
            w = (full[0], full[1], None, scale_full[j:j + 1])

        def after_proj(p, i=i):
            arrays, sems = (in_b[0], sems_b[0]) if i == 0 else (in_r[i - 1][-1:], sems_r[i - 1])
            last_part[i], tok = hand_off(arrays, sems, p, 0 if i == 0 else len(in_r[i - 1]) - 1, f"{i}b")
            return tok

        def before_out(y2, i=i):
            w_out, tok = whole(_agf_wait(*last_part.pop(i), y2, f"agf_wait_{i}b"))[0], None
            if i + 1 < depth:
                first_part[i + 1], tok = hand_off(in_r[i][:-1], sems_r[i], y2, 0, f"{i + 1}a")
            return w_out, tok

        nxt = (norm_g[i + 1:i + 2], mods[i + 1][0], mods[i + 1][1]) if i + 1 < depth else None
        x_cur, hb, sv, w = _layer_fwd(i % 2 == 0, x_cur, hb, mods[i][2], w, nxt, before_out, after_proj)
        weights.append(w)
        saved.append(sv)
    gin, loss, dfinal_g, dob, dgate = _loss_bwd(x_cur, loss_target[0], final_g.reshape(1, D), saved[-1][4],
                                                mods[-1][2])

    stacked = {}

    def reduce_layer(i, sems, pairs, lands, after):
        pairs, slots = _rs_chip_wait(sems, pairs, lands, after, f"rs_chip_wait_{i}")
        half_sems, halves, _ = _rs_half_start(_rs_sum(place, pairs, slots), f"rs_half_start_{i}")
        return i, half_sems, halves

    def update_layer(i, half_sems, halves, after):
        names = even_names if i % 2 == 0 else odd_names
        grads = _rs_half_wait(half_sems, halves, after, f"rs_half_wait_{i}")
        items = [(params[nm][0], g.reshape(params[nm][0].shape[1:]), params[nm][1], params[nm][2], stacked.get(nm))
                 for nm, g in zip(names, grads)]
        for nm, res in zip(names, _adamw_layer(i // 2, items)):
            stacked[nm] = res
            updated.append(res[1])

    updated = []
    small_g, dmod, dnorm_g, pending, tok = [None] * depth, [None] * depth, [None] * depth, None, None
    exchanging = []
    for i in reversed(range(depth)):
        w = weights[i]
        if tok is not None:
            w = w[:2] + (w[2] + tok[0:1, 0:1],) + w[3:] if i % 2 == 0 else w[:3] + (w[3] + tok[0:1, 0:1],)
        below = (saved[i - 1][4], mods[i - 1][2]) if i > 0 else None

        def send(big_g, i=i):
            if exchanging:
                update_layer(*exchanging.pop(), big_g[0])
            big_g = [g.reshape(4, 2, g.shape[1] // 2, g.shape[2]) for g in big_g]
            sems, big_g, lands, tok = _rs_pair_start(big_g, f"rs_pair_start_{i}")
            return (sems, big_g, lands), tok

        gin, gate_bwd, sent, small_g[i], dmod[i], dnorm_g[i] = _layer_bwd(
            i % 2 == 0, gin, dob, dgate, saved[i], mods[i][1], norm_g[i:i + 1], w, below, send)
        if below is not None:
            dob, dgate = gate_bwd
        after = gin
        if i == 0:
            dmod_all = _gather8(jnp.stack(dmod).reshape(depth * 3 * D // LANES, LANES), "gather_dmod")
            after = dmod_all = dmod_all.reshape(N_DEV, depth, 3 * D)
        if i > 0:
            after, updated = [after] + updated, []
        else:
            after = [after]
        big_g, theirs = _rs_pair_wait(*sent, after, f"rs_pair_wait_{i}")
        pairs = _rs_add(place, big_g, theirs)
        sems, pairs, lands, tok = _rs_chip_start(pairs, f"rs_chip_start_{i}")
        if pending is not None:
            exchanging.append(reduce_layer(*pending, [tok]))
        pending = (i, sems, pairs, lands)
    grad_x = gin
    dnorm_g = jnp.concatenate(dnorm_g, axis=0)

    dmod_cols = jnp.transpose(shard_cols(dmod_all, acols), (1, 0, 2))
    r_ada_w = _ada_bwd(c_all.T, dmod_cols, ada_w, m_ada_w, v_ada_w)
    update_layer(*exchanging.pop(), r_ada_w[1])
    last = reduce_layer(*pending, [r_ada_w[1]] + updated)

    small_parts = [dnorm_g, dfinal_g,
                   jnp.stack([small_g[2 * j]["conv_w"] for j in range(n_even)]),
                   jnp.concatenate([small_g[2 * j]["ln_g"] for j in range(n_even)], axis=0),
                   jnp.concatenate([small_g[2 * j]["ln_b"] for j in range(n_even)], axis=0),
                   jnp.stack([small_g[2 * j]["sgu_b"] for j in range(n_even)]),
                   jnp.concatenate([small_g[2 * j + 1]["pool_scale"] for j in range(n_odd)], axis=0),
                   jnp.pad(loss, ((0, 7), (0, LANES - 1)))]
    small_shapes = [p.shape for p in small_parts]
    sgu_parts = [small_g[2 * j]["sgu_w"].reshape(NH * HEAD, HEAD).astype(bf16) for j in range(n_even)]
    reduced = _allreduce8([_pack_rows(small_parts)] + sgu_parts, "allreduce_small", last[2][0])
    update_layer(*last, reduced[0])
    r_ab_w_in, r_ab_w_out, r_c_w_in, r_c_w_out = (stacked[nm] for nm in ("ab_w_in", "ab_w_out", "c_w_in", "c_w_out"))
    r_c_pool_w = tuple(a.reshape(c_pool_w.shape) for a in stacked["c_pool_w"])
    g_norm_g, g_final_g, g_conv_full, g_ln_g, g_ln_b, g_sgu_b, g_scale_full, loss_row = _unpack_rows(reduced[0],
                                                                                                     small_shapes)
    g_sgu_w = jnp.stack(reduced[1:]).astype(f32)
    loss = loss_row[0, 0]
    g_conv = shard_cols(g_conv_full, D // 4)
    g_scale = shard_cols(g_scale_full, 2 * D // 4)

    def two_d(a):
        return a.reshape(-1, a.shape[-1])

    small = [(norm_g, g_norm_g, m_norm_g, v_norm_g),
             (ada_b, dmod_all, m_ada_b, v_ada_b),
             (two_d(ab_conv_w), two_d(g_conv), two_d(m_ab_conv_w), two_d(v_ab_conv_w)),
             (ab_ln_g, g_ln_g, m_ab_ln_g, v_ab_ln_g),
             (ab_ln_b, g_ln_b, m_ab_ln_b, v_ab_ln_b),
             (two_d(ab_sgu_w), two_d(g_sgu_w), two_d(m_ab_sgu_w), two_d(v_ab_sgu_w)),
             (two_d(ab_sgu_b), two_d(g_sgu_b), two_d(m_ab_sgu_b), two_d(v_ab_sgu_b)),
             (c_pool_scale, g_scale, m_c_pool_scale, v_c_pool_scale),
             (final_g.reshape(1, D), g_final_g, m_final_g.reshape(1, D), v_final_g.reshape(1, D))]
    small_res = _adamw_small(small)
    small_shapes_out = [norm_g.shape, ada_b.shape, ab_conv_w.shape, ab_ln_g.shape, ab_ln_b.shape, ab_sgu_w.shape,
                        ab_sgu_b.shape, c_pool_scale.shape, final_g.shape]
    (r_norm_g, r_ada_b, r_conv, r_ln_g, r_ln_b, r_sgu_w, r_sgu_b, r_scale, r_final_g) = [
        tuple(a.reshape(shp) for a in res) for res, shp in zip(small_res, small_shapes_out)]

    order = [r_norm_g, r_ada_w, r_ada_b, r_ab_w_in, r_conv, r_ln_g, r_ln_b, r_sgu_w, r_sgu_b, r_ab_w_out,
             r_c_w_in, r_c_pool_w, r_scale, r_c_w_out, r_final_g]
    outs = [loss, grad_x[None]]
    for field in range(4):
        outs += [r[field] for r in order]
    return tuple(outs)
```

```python
import functools

import jax
import jax.numpy as jnp
from jax import lax
from jax.experimental import pallas as pl
from jax.experimental.pallas import tpu as pltpu

f32, bf16 = jnp.float32, jnp.bfloat16

D = 1024
HEAD = 128
NH = 8
WINDOWS = (2, 4, 8, 16)
GC = 512
EPS = 1e-6
HALO_CONV = 8
HALO_POOL = 16
CHUNK_ROWS = 512
DH_WIDE = 1024
FWD_TILES = 2
LATE_LAYER = 2
N_DEV = 8
LANES = 128

ADAM_LR, ADAM_B1, ADAM_B2, ADAM_EPS, ADAM_WD, ADAM_STEP = 0.001, 0.9, 0.999, 1e-08, 0.01, 10

MESH = pl.DeviceIdType.MESH
ANY = pl.BlockSpec(memory_space=pl.ANY)
VMEM = pl.BlockSpec(memory_space=pltpu.VMEM)
MIB = 2 ** 20


def _pcall(body, *, name, out_shape, grid=None, in_specs=None, out_specs=None, scratch=(), vmem_mb=None,
           aliases=None, prefetch=0):
    kw = {}
    if prefetch:
        kw["grid_spec"] = pltpu.PrefetchScalarGridSpec(num_scalar_prefetch=prefetch, grid=grid, in_specs=in_specs,
                                                       out_specs=out_specs, scratch_shapes=list(scratch))
    else:
        if grid is not None:
            kw["grid"] = grid
        if in_specs is not None:
            kw["in_specs"] = in_specs
        if out_specs is not None:
            kw["out_specs"] = out_specs
        if scratch:
            kw["scratch_shapes"] = list(scratch)
    if aliases:
        kw["input_output_aliases"] = aliases
    params = pltpu.CompilerParams(vmem_limit_bytes=None if vmem_mb is None else vmem_mb * MIB)
    return pl.pallas_call(body, name=name, out_shape=out_shape, compiler_params=params, **kw)


def _sds(shape, dtype):
    return jax.ShapeDtypeStruct(tuple(shape), dtype)


def _sigmoid(z):
    return pl.reciprocal(1.0 + jnp.exp(-z), approx=True)


def _silu(z):
    return z * _sigmoid(z)


def _silu_and_grad(z):
    s = _sigmoid(z)
    return z * s, s * (1.0 + z * (1.0 - s))


def _place():
    return lax.axis_index("x"), lax.axis_index("y"), lax.axis_index("c")


def _gather8(blk, name, after=()):
    def body(x_ref, *rest):
        o_ref, ssem, rsem = rest[len(after):]
        x, y, c = _place()
        me = 4 * x + 2 * y + c
        o_ref[me] = x_ref[...]
        sends = []
        for k in range(1, N_DEV):
            px = 1 - x if k & 4 else x
            py = 1 - y if k & 2 else y
            pc = 1 - c if k & 1 else c
            cp = pltpu.make_async_remote_copy(src_ref=x_ref, dst_ref=o_ref.at[me], send_sem=ssem.at[k - 1],
                                              recv_sem=rsem.at[k - 1], device_id=(px, py, pc), device_id_type=MESH)
            cp.start()
            sends.append((cp, 4 * px + 2 * py + pc))
        for k, (cp, peer) in enumerate(sends):
            pltpu.make_async_remote_copy(src_ref=x_ref, dst_ref=o_ref.at[peer], send_sem=ssem.at[k],
                                         recv_sem=rsem.at[k], device_id=(x, y, c), device_id_type=MESH).wait_recv()
        for cp, _ in sends:
            cp.wait_send()

    return _pcall(body, name=name, out_shape=_sds((N_DEV,) + blk.shape, blk.dtype), in_specs=[VMEM] + [ANY] * len(after),
                  out_specs=VMEM,
                  scratch=[pltpu.SemaphoreType.DMA((N_DEV - 1,)), pltpu.SemaphoreType.DMA((N_DEV - 1,))])(blk, *after)


def _allreduce8(bufs, name, after=None):
    n, n_after = len(bufs), 0 if after is None else 1
    rbs = [b.shape[0] // N_DEV for b in bufs]
    assert all(rb * N_DEV == b.shape[0] and rb % (16 if b.dtype == bf16 else 8) == 0 for rb, b in zip(rbs, bufs))

    def body(*refs):
        refs = refs[:n] + refs[n + n_after:]
        xs, outs, stages = refs[:n], refs[n:2 * n], refs[2 * n:3 * n]
        ssem, rsem = refs[3 * n:]
        x, y, c = _place()
        me = 4 * x + 2 * y + c
        peers = []
        for k in range(1, N_DEV):
            px = 1 - x if k & 4 else x
            py = 1 - y if k & 2 else y
            pc = 1 - c if k & 1 else c
            peers.append(((px, py, pc), 4 * px + 2 * py + pc))

        def blk(t, ref, idx):
            return ref.at[pl.ds(pl.multiple_of(idx * rbs[t], 8), rbs[t]), :]

        def copy(t, phase, k, src, dst, dev):
            return pltpu.make_async_remote_copy(src_ref=src, dst_ref=dst, send_sem=ssem.at[t, phase, k],
                                                recv_sem=rsem.at[t, phase, k], device_id=dev, device_id_type=MESH)

        scatter = [copy(t, 0, k, blk(t, xs[t], pidx), stages[t].at[me], dev)
                   for t in range(n) for k, (dev, pidx) in enumerate(peers)]
        for cp in scatter:
            cp.start()
        gather = []
        for t in range(n):
            stages[t][me] = blk(t, xs[t], me)[...]
            for k, (dev, pidx) in enumerate(peers):
                copy(t, 0, k, blk(t, xs[t], pidx), stages[t].at[pidx], dev).wait_recv()
            total = stages[t][0].astype(f32)
            for j in range(1, N_DEV):
                total = total + stages[t][j].astype(f32)
            blk(t, outs[t], me)[...] = total.astype(outs[t].dtype)
            sends = [copy(t, 1, k, blk(t, outs[t], me), blk(t, outs[t], me), dev) for k, (dev, pidx) in enumerate(peers)]
            for cp in sends:
                cp.start()
            gather += sends
        for t in range(n):
            for k, (dev, pidx) in enumerate(peers):
                copy(t, 1, k, blk(t, outs[t], pidx), blk(t, outs[t], pidx), dev).wait_recv()
        for cp in scatter + gather:
            cp.wait_send()

    return _pcall(body, name=name, out_shape=[_sds(b.shape, b.dtype) for b in bufs], in_specs=[VMEM] * n + [ANY] * n_after,
                  out_specs=[VMEM] * n,
                  scratch=[pltpu.VMEM((N_DEV, rb, LANES), b.dtype) for rb, b in zip(rbs, bufs)]
                  + [pltpu.SemaphoreType.DMA((n, 2, N_DEV - 1)), pltpu.SemaphoreType.DMA((n, 2, N_DEV - 1))])(
                      *bufs, *([] if after is None else [after]))


def _other_chips(x, y):
    return [((1 - x, y), 2 * (1 - x) + y), ((x, 1 - y), 2 * x + (1 - y)), ((1 - x, 1 - y), 2 * (1 - x) + (1 - y))]


HBM = pl.BlockSpec(memory_space=pltpu.HBM)
SEM = pl.BlockSpec(memory_space=pltpu.SEMAPHORE)
EFFECT = pltpu.SideEffectType.DATAFLOW_SIDE_EFFECTING


def _in_hbm(a):
    return pltpu.with_memory_space_constraint(a, pltpu.HBM)


SIBLING_ID = 1


def _sibling_handshake():
    x, y, c = _place()
    barrier = pltpu.get_barrier_semaphore()
    pl.semaphore_signal(barrier, inc=1, device_id=(x, y, 1 - c), device_id_type=MESH)
    pl.semaphore_wait(barrier, 1)
    return x, y, c


def _ag_start(layers, after, name):
    flat = [t for lay in layers for t in lay]
    n, nl = len(flat), len(layers)

    def body(*refs):
        src = refs[:n]
        sems = refs[n + 1:n + 1 + 2 * nl]
        token = refs[-1]
        x, y, c = _place()
        s_me = 2 * x + y
        t = 0
        for i, lay in enumerate(layers):
            for k in range(len(lay)):
                for j, ((px, py), _) in enumerate(_other_chips(x, y)):
                    pltpu.make_async_remote_copy(src_ref=src[t].at[s_me, c], dst_ref=src[t].at[s_me, c],
                                                 send_sem=sems[2 * i].at[3 * k + j], recv_sem=sems[2 * i + 1].at[3 * k + j],
                                                 device_id=(px, py, c), device_id_type=MESH).start()
                t += 1
        token[...] = jnp.zeros_like(token)

    sem_shapes = [pltpu.SemaphoreType.DMA((3 * len(lay),)) for lay in layers for _ in range(2)]
    out_shape = sem_shapes + [pltpu.HBM(t.shape, t.dtype) for t in flat] + [_sds((8, LANES), f32)]
    outs = pl.pallas_call(
        body, name=name, out_shape=out_shape, in_specs=[HBM] * n + [ANY],
        out_specs=[SEM] * (2 * nl) + [HBM] * n + [VMEM], input_output_aliases={t: 2 * nl + t for t in range(n)},
        compiler_params=pltpu.CompilerParams(has_side_effects=EFFECT))(*[_in_hbm(t) for t in flat], after)
    sems = [(outs[2 * i], outs[2 * i + 1]) for i in range(nl)]
    thru, t = [], 2 * nl
    for lay in layers:
        thru.append(list(outs[t:t + len(lay)]))
        t += len(lay)
    return sems, thru, outs[-1]


def _ag_wait(inflight, sems, after, name, first=0):
    n = len(inflight)

    def body(*refs):
        src, ssem, rsem = refs[:n], refs[n], refs[n + 1]
        x, y, c = _place()
        s_me = 2 * x + y
        for k in range(n):
            for j, (_, s_p) in enumerate(_other_chips(x, y)):
                cp = pltpu.make_async_remote_copy(src_ref=src[k].at[s_me, c], dst_ref=src[k].at[s_p, c],
                                                  send_sem=ssem.at[3 * (first + k) + j],
                                                  recv_sem=rsem.at[3 * (first + k) + j],
                                                  device_id=(x, y, c), device_id_type=MESH)
                cp.wait_send()
                cp.wait_recv()

    return pl.pallas_call(
        body, name=name, out_shape=[pltpu.HBM(t.shape, t.dtype) for t in inflight],
        in_specs=[HBM] * n + [SEM, SEM, ANY], out_specs=[HBM] * n, input_output_aliases={t: t for t in range(n)},
        compiler_params=pltpu.CompilerParams(has_side_effects=EFFECT))(*inflight, sems[0], sems[1], after)


def _agf_start(arrived, name):
    n = len(arrived)

    def body(*refs):
        o = refs[:n]
        ssem, rsem, token = refs[n], refs[n + 1], refs[-1]
        x, y, c = _sibling_handshake()
        for t in range(n):
            for j, (_, s_p) in enumerate(_other_chips(x, y)):
                pltpu.make_async_remote_copy(src_ref=o[t].at[s_p, c], dst_ref=o[t].at[s_p, c],
                                             send_sem=ssem.at[3 * t + j], recv_sem=rsem.at[3 * t + j],
                                             device_id=(x, y, 1 - c), device_id_type=MESH).start()
        token[...] = jnp.zeros_like(token)

    out_shape = ([pltpu.SemaphoreType.DMA((3 * n,))] * 2 + [pltpu.HBM(a.shape, bf16) for a in arrived]
                 + [_sds((8, LANES), f32)])
    outs = pl.pallas_call(
        body, name=name, out_shape=out_shape, in_specs=[HBM] * n, out_specs=[SEM, SEM] + [HBM] * n + [VMEM],
        input_output_aliases={t: 2 + t for t in range(n)},
        compiler_params=pltpu.CompilerParams(has_side_effects=EFFECT, collective_id=SIBLING_ID))(
            *[_in_hbm(a) for a in arrived])
    return (outs[0], outs[1]), list(outs[2:2 + n]), outs[-1]


def _agf_wait(sems, inflight, after, name):
    n = len(inflight)

    def body(*refs):
        o, ssem, rsem = refs[:n], refs[n], refs[n + 1]
        x, y, c = _place()
        for t in range(n):
            for j, (_, s_p) in enumerate(_other_chips(x, y)):
                cp = pltpu.make_async_remote_copy(src_ref=o[t].at[s_p, c], dst_ref=o[t].at[s_p, 1 - c],
                                                  send_sem=ssem.at[3 * t + j], recv_sem=rsem.at[3 * t + j],
                                                  device_id=(x, y, c), device_id_type=MESH)
                cp.wait_send()
                cp.wait_recv()

    return pl.pallas_call(
        body, name=name, out_shape=[pltpu.HBM(a.shape, bf16) for a in inflight],
        in_specs=[HBM] * n + [SEM, SEM, ANY], out_specs=[HBM] * n, input_output_aliases={t: t for t in range(n)},
        compiler_params=pltpu.CompilerParams(has_side_effects=EFFECT))(*inflight, sems[0], sems[1], after)


def _rs_pair_start(grads, name):
    n = len(grads)

    def body(*refs):
        g, theirs = refs[:n], refs[n:2 * n]
        ssem, rsem, token = refs[2 * n], refs[2 * n + 1], refs[-1]
        x, y, c = _sibling_handshake()
        for t in range(n):
            pltpu.make_async_remote_copy(src_ref=g[t].at[:, 1 - c], dst_ref=theirs[t], send_sem=ssem.at[t],
                                         recv_sem=rsem.at[t], device_id=(x, y, 1 - c), device_id_type=MESH).start()
        token[...] = jnp.zeros_like(token)

    lands = [lax.empty((4,) + g.shape[2:], bf16) for g in grads]
    out_shape = ([pltpu.SemaphoreType.DMA((n,))] * 2 + [pltpu.HBM(g.shape, bf16) for g in grads]
                 + [pltpu.HBM(q.shape, bf16) for q in lands] + [_sds((8, LANES), f32)])
    outs = pl.pallas_call(
        body, name=name, out_shape=out_shape, in_specs=[HBM] * (2 * n), out_specs=[SEM, SEM] + [HBM] * (2 * n) + [VMEM],
        input_output_aliases={t: 2 + t for t in range(2 * n)},
        compiler_params=pltpu.CompilerParams(has_side_effects=EFFECT, collective_id=SIBLING_ID))(
            *[_in_hbm(a) for a in list(grads) + lands])
    return (outs[0], outs[1]), list(outs[2:2 + n]), list(outs[2 + n:2 + 2 * n]), outs[-1]


def _rs_pair_wait(sems, grads, lands, after, name):
    n = len(grads)

    def body(*refs):
        g, theirs = refs[:n], refs[n:2 * n]
        ssem, rsem = refs[2 * n], refs[2 * n + 1]
        x, y, c = _place()
        for t in range(n):
            cp = pltpu.make_async_remote_copy(src_ref=g[t].at[:, 1 - c], dst_ref=theirs[t], send_sem=ssem.at[t],
                                              recv_sem=rsem.at[t], device_id=(x, y, c), device_id_type=MESH)
            cp.wait_send()
            cp.wait_recv()

    outs = pl.pallas_call(
        body, name=name, out_shape=[pltpu.HBM(a.shape, bf16) for a in list(grads) + list(lands)],
        in_specs=[HBM] * (2 * n) + [SEM, SEM] + [ANY] * len(after), out_specs=[HBM] * (2 * n),
        input_output_aliases={t: t for t in range(2 * n)},
        compiler_params=pltpu.CompilerParams(has_side_effects=EFFECT))(*grads, *lands, sems[0], sems[1], *after)
    return list(outs[:n]), list(outs[n:])


def _rs_chip_start(pairs, name):
    n = len(pairs)

    def body(*refs):
        p, q = refs[:n], refs[n:2 * n]
        ssem, rsem, token = refs[2 * n], refs[2 * n + 1], refs[-1]
        x, y, c = _place()
        for t in range(n):
            for j, ((px, py), s_p) in enumerate(_other_chips(x, y)):
                pltpu.make_async_remote_copy(src_ref=p[t].at[s_p], dst_ref=q[t].at[j], send_sem=ssem.at[3 * t + j],
                                             recv_sem=rsem.at[3 * t + j], device_id=(px, py, c), device_id_type=MESH).start()
        token[...] = jnp.zeros_like(token)

    lands = [lax.empty((3,) + p.shape[1:], bf16) for p in pairs]
    out_shape = ([pltpu.SemaphoreType.DMA((3 * n,))] * 2 + [pltpu.HBM(p.shape, bf16) for p in pairs]
                 + [pltpu.HBM(q.shape, bf16) for q in lands] + [_sds((8, LANES), f32)])
    outs = pl.pallas_call(
        body, name=name, out_shape=out_shape, in_specs=[HBM] * (2 * n), out_specs=[SEM, SEM] + [HBM] * (2 * n) + [VMEM],
        input_output_aliases={t: 2 + t for t in range(2 * n)},
        compiler_params=pltpu.CompilerParams(has_side_effects=EFFECT))(*[_in_hbm(a) for a in list(pairs) + lands])
    return (outs[0], outs[1]), list(outs[2:2 + n]), list(outs[2 + n:2 + 2 * n]), outs[-1]


def _rs_chip_wait(sems, pairs, lands, after, name):
    n = len(pairs)

    def body(*refs):
        p, q = refs[:n], refs[n:2 * n]
        ssem, rsem = refs[2 * n], refs[2 * n + 1]
        x, y, c = _place()
        for t in range(n):
            for j, (_, s_p) in enumerate(_other_chips(x, y)):
                cp = pltpu.make_async_remote_copy(src_ref=p[t].at[s_p], dst_ref=q[t].at[j], send_sem=ssem.at[3 * t + j],
                                                  recv_sem=rsem.at[3 * t + j], device_id=(x, y, c), device_id_type=MESH)
                cp.wait_send()
                cp.wait_recv()

    outs = pl.pallas_call(
        body, name=name, out_shape=[pltpu.HBM(a.shape, bf16) for a in list(pairs) + list(lands)],
        in_specs=[HBM] * (2 * n) + [SEM, SEM] + [ANY] * len(after), out_specs=[HBM] * (2 * n),
        input_output_aliases={t: t for t in range(2 * n)},
        compiler_params=pltpu.CompilerParams(has_side_effects=EFFECT))(*pairs, *lands, sems[0], sems[1], *after)
    return list(outs[:n]), list(outs[n:])


def _rs_half_start(halves, name):
    n = len(halves)

    def body(*refs):
        o = refs[:n]
        ssem, rsem, token = refs[n], refs[n + 1], refs[-1]
        x, y, c = _sibling_handshake()
        for t in range(n):
            pltpu.make_async_remote_copy(src_ref=o[t].at[c], dst_ref=o[t].at[c], send_sem=ssem.at[t],
                                         recv_sem=rsem.at[t], device_id=(x, y, 1 - c), device_id_type=MESH).start()
        token[...] = jnp.zeros_like(token)

    out_shape = ([pltpu.SemaphoreType.DMA((n,))] * 2 + [pltpu.HBM(h.shape, h.dtype) for h in halves]
                 + [_sds((8, LANES), f32)])
    outs = pl.pallas_call(
        body, name=name, out_shape=out_shape, in_specs=[HBM] * n, out_specs=[SEM, SEM] + [HBM] * n + [VMEM],
        input_output_aliases={t: 2 + t for t in range(n)},
        compiler_params=pltpu.CompilerParams(has_side_effects=EFFECT, collective_id=SIBLING_ID))(
            *[_in_hbm(h) for h in halves])
    return (outs[0], outs[1]), list(outs[2:2 + n]), outs[-1]


def _rs_half_wait(sems, inflight, after, name):
    n = len(inflight)

    def body(*refs):
        o, ssem, rsem = refs[:n], refs[n], refs[n + 1]
        x, y, c = _place()
        for t in range(n):
            cp = pltpu.make_async_remote_copy(src_ref=o[t].at[c], dst_ref=o[t].at[1 - c], send_sem=ssem.at[t],
                                              recv_sem=rsem.at[t], device_id=(x, y, c), device_id_type=MESH)
            cp.wait_send()
            cp.wait_recv()

    return pl.pallas_call(
        body, name=name, out_shape=[pltpu.HBM(h.shape, h.dtype) for h in inflight],
        in_specs=[HBM] * n + [SEM, SEM, ANY], out_specs=[HBM] * n, input_output_aliases={t: t for t in range(n)},
        compiler_params=pltpu.CompilerParams(has_side_effects=EFFECT))(*inflight, sems[0], sems[1], after)


def _row_spec(tm, cols):
    return pl.BlockSpec((tm, cols), lambda i: (i, 0))


def _vec_spec(cols, rows=1):
    return pl.BlockSpec((rows, cols), lambda i: (0, 0))


def _modulated_norm(xv, g, shift, scale):
    r = lax.rsqrt(jnp.mean(xv * xv, axis=-1, keepdims=True) + EPS)
    return (((xv * r) * g) * (1.0 + scale) + shift).astype(bf16)


def _hnorm(x, g, shift, scale):
    T, tm = x.shape[0], 256

    def body(x_ref, g_ref, sh_ref, sc_ref, h_ref):
        h_ref[...] = _modulated_norm(x_ref[...], g_ref[...], sh_ref[...], sc_ref[...])

    return _pcall(body, name="hnorm", out_shape=_sds((T, D), bf16), grid=(T // tm,),
                  in_specs=[_row_spec(tm, D), _vec_spec(D), _vec_spec(D), _vec_spec(D)],
                  out_specs=_row_spec(tm, D))(x, g, shift, scale)


def _out_proj(y2, wo, x, gate, nxt=None):
    T, tm = x.shape[0], 512

    def body(y_ref, w_ref, x_ref, g_ref, *rest):
        o = jnp.dot(y_ref[0], w_ref[0], preferred_element_type=f32)
        o = o + jnp.dot(y_ref[1], w_ref[1], preferred_element_type=f32)
        xo = x_ref[...] + g_ref[...] * o
        if nxt is None:
            xo_ref, o_ref = rest
        else:
            ng_ref, nsh_ref, nsc_ref, xo_ref, o_ref, h_ref = rest
            h_ref[...] = _modulated_norm(xo, ng_ref[...], nsh_ref[...], nsc_ref[...])
        o_ref[...] = o.astype(bf16)
        xo_ref[...] = xo

    extra = [] if nxt is None else list(nxt)
    n_out = 2 if nxt is None else 3
    return _pcall(body, name="out_proj", out_shape=[_sds((T, D), f32), _sds((T, D), bf16), _sds((T, D), bf16)][:n_out],
                  grid=(T // tm,),
                  in_specs=[pl.BlockSpec((2, tm, D), lambda i: (0, i, 0)), pl.BlockSpec((2, D, D), lambda i: (0, 0, 0)),
                            _row_spec(tm, D), _vec_spec(D)] + [_vec_spec(D)] * len(extra),
                  out_specs=[_row_spec(tm, D)] * n_out, vmem_mb=40)(y2, wo, x, gate, *extra)


def _gate_bwd_tile(dx, o_ref, gate_ref, dob_ref, dgate_ref):
    dob_ref[...] = (dx * gate_ref[...]).astype(bf16)
    dgate_ref[...] += jnp.sum(dx * o_ref[...].astype(f32), axis=0, keepdims=True)


def _loss_bwd(x, target, g, o, gate):
    T, tm = x.shape[0], 512

    def body(x_ref, t_ref, g_ref, o_ref, gate_ref, dx_ref, loss_ref, dg_ref, dob_ref, dgate_ref):
        @pl.when(pl.program_id(0) == 0)
        def _():
            loss_ref[...] = jnp.zeros_like(loss_ref)
            dg_ref[...] = jnp.zeros_like(dg_ref)
            dgate_ref[...] = jnp.zeros_like(dgate_ref)

        xv, gv = x_ref[...], g_ref[...]
        r = lax.rsqrt(jnp.mean(xv * xv, axis=-1, keepdims=True) + EPS)
        xn = xv * r
        err = xn * gv - t_ref[...]
        dy = err * (1.0 / D)
        dxn = dy * gv
        dx = r * (dxn - xn * jnp.mean(dxn * xn, axis=-1, keepdims=True))
        dx_ref[...] = dx
        dg_ref[...] += jnp.sum(dy * xn, axis=0, keepdims=True)
        loss_ref[...] += (0.5 / D) * jnp.sum(jnp.sum(err * err, axis=1, keepdims=True), axis=0, keepdims=True)
        _gate_bwd_tile(dx, o_ref, gate_ref, dob_ref, dgate_ref)

    return _pcall(body, name="loss_bwd",
                  out_shape=[_sds((T, D), f32), _sds((1, 1), f32), _sds((1, D), f32), _sds((T, D), bf16), _sds((1, D), f32)],
                  grid=(T // tm,),
                  in_specs=[_row_spec(tm, D), _row_spec(tm, D), _vec_spec(D), _row_spec(tm, D), _vec_spec(D)],
                  out_specs=[_row_spec(tm, D), pl.BlockSpec((1, 1), lambda i: (0, 0)), _vec_spec(D), _row_spec(tm, D),
                             _vec_spec(D)])(x, target, g, o, gate)


def _norm_bwd(x, dh, gin, g, scale, below=None):
    T, tm = x.shape[0], 512

    def body(x_ref, dh_ref, gin_ref, g_ref, sc_ref, *rest):
        if below is None:
            dx_ref, st_ref = rest
        else:
            o_ref, gate_ref, dx_ref, st_ref, dob_ref, dgate_ref = rest

        @pl.when(pl.program_id(0) == 0)
        def _():
            st_ref[...] = jnp.zeros_like(st_ref)
            if below is not None:
                dgate_ref[...] = jnp.zeros_like(dgate_ref)

        xv, gv, dhv = x_ref[...], g_ref[...], dh_ref[...]
        r = lax.rsqrt(jnp.mean(xv * xv, axis=-1, keepdims=True) + EPS)
        xn = xv * r
        da = dhv * (1.0 + sc_ref[...])
        dxn = da * gv
        dx = gin_ref[...] + r * (dxn - xn * jnp.mean(dxn * xn, axis=-1, keepdims=True))
        dx_ref[...] = dx
        st_ref[0:1, :] += jnp.sum(dhv, axis=0, keepdims=True)
        st_ref[1:2, :] += jnp.sum(dhv * (xn * gv), axis=0, keepdims=True)
        st_ref[2:3, :] += jnp.sum(da * xn, axis=0, keepdims=True)
        if below is not None:
            _gate_bwd_tile(dx, o_ref, gate_ref, dob_ref, dgate_ref)

    out_shape = [_sds((T, D), f32), _sds((8, D), f32)]
    in_specs = [_row_spec(tm, D), _row_spec(tm, D), _row_spec(tm, D), _vec_spec(D), _vec_spec(D)]
    out_specs = [_row_spec(tm, D), _vec_spec(D, 8)]
    args = [x, dh, gin, g, scale]
    if below is not None:
        out_shape += [_sds((T, D), bf16), _sds((1, D), f32)]
        in_specs += [_row_spec(tm, D), _vec_spec(D)]
        out_specs += [_row_spec(tm, D), _vec_spec(D)]
        args += list(below)
    return _pcall(body, name="norm_bwd", out_shape=out_shape, grid=(T // tm,), in_specs=in_specs,
                  out_specs=out_specs)(*args)


STEPS = 4
ADAMW_STEPS = 8


def _cast_place(place, ws, layer, after=None):
    n = len(ws)

    def body(place_ref, *refs):
        for t in range(n):
            refs[-n + t][...] = refs[t][...].astype(bf16)

    def tile(w):
        return w.shape[1] // STEPS, w.shape[2]

    extra = [] if after is None else [after]
    return _pcall(body, name="cast_place", out_shape=[_sds((4,) + w.shape[1:], bf16) for w in ws], grid=(STEPS,),
                  prefetch=1,
                  in_specs=[pl.BlockSpec((None,) + tile(w), lambda i, pr: (layer, i, 0)) for w in ws] + [ANY] * len(extra),
                  out_specs=[pl.BlockSpec((None,) + tile(w), lambda i, pr: (pr[0], i, 0)) for w in ws])(
                      place, *ws, *extra)


def _rs_add(place, grads, theirs):
    n = len(grads)

    def body(place_ref, *refs):
        for t in range(n):
            refs[2 * n + t][...] = (refs[t][...].astype(f32) + refs[n + t][...].astype(f32)).astype(bf16)

    mine = [pl.BlockSpec((None, None) + q.shape[1:], lambda s, pr: (s, pr[1], 0, 0)) for q in theirs]
    shard = [pl.BlockSpec((None,) + q.shape[1:], lambda s, pr: (s, 0, 0)) for q in theirs]
    return _pcall(body, name="rs_add", out_shape=[_sds(q.shape, bf16) for q in theirs], grid=(4,), prefetch=1,
                  in_specs=mine + shard, out_specs=shard)(place, *grads, *theirs)


def _rs_sum(place, pairs, slots):
    n, steps = len(pairs), 4

    def body(place_ref, *refs):
        for t in range(n):
            p_ref, q_ref = refs[t], refs[n + t]
            total = ((p_ref[...].astype(f32) + q_ref[0].astype(f32)) + q_ref[1].astype(f32)) + q_ref[2].astype(f32)
            refs[2 * n + t][...] = total.astype(bf16)

    def tile(q):
        return q.shape[1] // steps, q.shape[2]

    return _pcall(body, name="rs_sum", out_shape=[_sds((2,) + q.shape[1:], bf16) for q in slots], grid=(steps,),
                  prefetch=1,
                  in_specs=[pl.BlockSpec((None,) + tile(q), lambda i, pr: (pr[0], i, 0)) for q in slots]
                  + [pl.BlockSpec((3,) + tile(q), lambda i, pr: (0, i, 0)) for q in slots],
                  out_specs=[pl.BlockSpec((None,) + tile(q), lambda i, pr: (pr[1], i, 0)) for q in slots])(
                      place, *pairs, *slots)


def _adamw_math(w, g, m, v):
    m = ADAM_B1 * m + (1.0 - ADAM_B1) * g
    v = ADAM_B2 * v + (1.0 - ADAM_B2) * jnp.square(g)
    m_hat = m / (1.0 - ADAM_B1 ** ADAM_STEP)
    v_hat = v / (1.0 - ADAM_B2 ** ADAM_STEP)
    delta = -ADAM_LR * (m_hat / (jnp.sqrt(v_hat) + ADAM_EPS) + ADAM_WD * w)
    return delta, m, v


def _adamw_layer(layer, items):
    n = len(items)

    def body(*refs):
        outs = refs[-4 * n:]
        for t in range(n):
            w_ref, g_ref, m_ref, v_ref = refs[4 * t:4 * t + 4]
            g = g_ref[...].astype(f32)
            outs[4 * t][...] = g
            outs[4 * t + 1][...], outs[4 * t + 2][...], outs[4 * t + 3][...] = _adamw_math(
                w_ref[...], g, m_ref[...], v_ref[...])

    args, in_specs, out_specs, out_shape = [], [], [], []
    for w, g, m, v, _ in items:
        tr, cols = w.shape[1] // ADAMW_STEPS, w.shape[2]
        spec = pl.BlockSpec((None, tr, cols), lambda i: (layer, i, 0))
        args += [w, g, m, v]
        in_specs += [spec, pl.BlockSpec((tr, cols), lambda i: (i, 0)), spec, spec]
        out_specs += [spec] * 4
        out_shape += [_sds(w.shape, f32)] * 4
    aliases = {}
    for t, it in enumerate(items):
        if it[4] is not None:
            for k in range(4):
                aliases[len(args)] = 4 * t + k
                args.append(it[4][k])
                in_specs.append(ANY)
    res = _pcall(body, name="adamw", out_shape=out_shape, grid=(ADAMW_STEPS,), in_specs=in_specs, out_specs=out_specs,
                 aliases=aliases)(*args)
    return [tuple(res[4 * t:4 * t + 4]) for t in range(n)]


def _adamw_small(items):
    n = len(items)

    def body(*refs):
        ins, outs = refs[:4 * n], refs[4 * n:]
        for t in range(n):
            w_ref, g_ref, m_ref, v_ref = ins[4 * t:4 * t + 4]
            if len(g_ref.shape) == len(w_ref.shape) + 1:
                g = g_ref[0]
                for b in range(1, g_ref.shape[0]):
                    g = g + g_ref[b]
            else:
                g = g_ref[...]
            d, m, v = _adamw_math(w_ref[...], g, m_ref[...], v_ref[...])
            outs[4 * t][...], outs[4 * t + 1][...], outs[4 * t + 2][...], outs[4 * t + 3][...] = g, d, m, v

    out_shape = [_sds(w.shape, f32) for (w, _, _, _) in items for _ in range(4)]
    flat = [a for it in items for a in it]
    res = _pcall(body, name="adamw_small", out_shape=out_shape, in_specs=[VMEM] * (4 * n),
                 out_specs=[VMEM] * (4 * n))(*flat)
    return [tuple(res[4 * t:4 * t + 4]) for t in range(n)]


NN = ((1,), (0,))
NT = ((1,), (1,))
TN = ((0,), (0,))


def _mm(name, a, b, *, grid, a_spec, b_spec, out_shape, out_spec, dims, vmem_mb=None):
    def body(a_ref, b_ref, o_ref):
        r = lax.dot_general(a_ref[...], b_ref[...], (dims, ((), ())), preferred_element_type=f32)
        o_ref[...] = r.astype(o_ref.dtype)

    return _pcall(body, name=name, out_shape=out_shape, grid=grid, in_specs=[a_spec, b_spec], out_specs=out_spec,
                  vmem_mb=vmem_mb)(a, b)


def _whole(shape):
    return pl.BlockSpec(shape, lambda j: (0,) * len(shape))


def _split_spec(rows, tile, per_split):
    return pl.BlockSpec((None, rows, tile), lambda j: (j // per_split, 0, j % per_split))


class _Proj:
    def __init__(self, n, splits, tile):
        self.n, self.splits, self.tile = n, splits, tile
        self.steps = n // tile
        self.w_per = n // 4 // tile
        self.a_per = n // splits // tile
        assert self.w_per * tile * 4 == n and self.a_per * tile * splits == n

    def fwd(self, hb, wg):
        T = hb.shape[0]
        sub, tile, w_per = FWD_TILES, self.tile, self.w_per
        wide = sub * tile
        a_per = self.n // self.splits // wide
        assert a_per * wide * self.splits == self.n

        def w_tile(q):
            return pl.BlockSpec((None, D, tile), lambda j: ((sub * j + q) // w_per, 0, (sub * j + q) % w_per))

        def body(a_ref, *rest):
            w = jnp.concatenate([rest[q][...] for q in range(sub)], axis=1)
            rest[sub][...] = jnp.dot(a_ref[...], w, preferred_element_type=f32).astype(bf16)

        return _pcall(body, name="proj_fwd", out_shape=_sds((self.splits, T, self.n // self.splits), bf16),
                      grid=(self.n // wide,), in_specs=[_whole((T, D))] + [w_tile(q) for q in range(sub)],
                      out_specs=pl.BlockSpec((None, T, wide), lambda j: (j // a_per, 0, j % a_per)),
                      vmem_mb=40 if wide > 512 else None)(hb, *([wg] * sub))

    def dw(self, hb, dp):
        T = hb.shape[0]
        return _mm("proj_dw", hb, dp, grid=(self.steps,), a_spec=_whole((T, D)),
                   b_spec=_split_spec(T, self.tile, self.a_per), out_shape=_sds((4, D, self.n // 4), bf16),
                   out_spec=_split_spec(D, self.tile, self.w_per), dims=TN)

    def dh(self, dp, wg, after=None):
        T = dp.shape[1]
        extra = [] if after is None else [after]
        sub, tile, w_per = DH_WIDE // self.tile, self.tile, self.w_per
        a_per = self.n // self.splits // DH_WIDE
        assert sub * tile == DH_WIDE and a_per * DH_WIDE * self.splits == self.n

        def w_tile(q):
            return pl.BlockSpec((None, D, tile), lambda k: ((sub * k + q) // w_per, 0, (sub * k + q) % w_per))

        def body(a_ref, *rest):
            o_ref = rest[-1]
            w = jnp.concatenate([rest[q][...] for q in range(sub)], axis=1)
            r = lax.dot_general(a_ref[...], w, (NT, ((), ())), preferred_element_type=f32)

            @pl.when(pl.program_id(0) == 0)
            def _():
                o_ref[...] = r

            @pl.when(pl.program_id(0) > 0)
            def _():
                o_ref[...] += r

        return _pcall(body, name="proj_dh", out_shape=_sds((T, D), f32), grid=(self.n // DH_WIDE,),
                      in_specs=[pl.BlockSpec((None, T, DH_WIDE), lambda k: (k // a_per, 0, k % a_per))]
                      + [w_tile(q) for q in range(sub)] + [ANY] * len(extra),
                      out_specs=_whole((T, D)), vmem_mb=40)(dp, *([wg] * sub), *extra)


EVEN_PROJ = _Proj(7 * D, 7, 256)
ODD_PROJ = _Proj(4 * D, 2, 512)


def _out_bwd(dob, wo, y2):
    T = dob.shape[0]
    w_spec = pl.BlockSpec((None, 512, D), lambda j: (j, 0, 0))

    def body(dob_ref, w_ref, y_ref, dy_ref, dw_ref):
        dob_v = dob_ref[...]
        dy_ref[...] = lax.dot_general(dob_v, w_ref[...], (NT, ((), ())), preferred_element_type=f32).astype(bf16)
        dw_ref[...] = lax.dot_general(y_ref[...], dob_v, (TN, ((), ())), preferred_element_type=f32).astype(bf16)

    return _pcall(body, name="out_bwd", out_shape=[_sds((2, T, D), bf16), _sds((4, 512, D), bf16)], grid=(4,),
                  in_specs=[_whole((T, D)), w_spec, _split_spec(T, 512, 2)],
                  out_specs=[_split_spec(T, 512, 2), w_spec])(dob, wo, y2)


def _head_spec(lead, T):
    return pl.BlockSpec((lead, T, HEAD), lambda h: (0, 0, h))


def _head_vec(rows):
    return pl.BlockSpec((rows, HEAD), lambda h: (0, h))


_HEAD_MAT = pl.BlockSpec((None, HEAD, HEAD), lambda h: (h, 0, 0))


def _causal():
    return lax.broadcasted_iota(jnp.int32, (HEAD, HEAD), 0) >= lax.broadcasted_iota(jnp.int32, (HEAD, HEAD), 1)


def _layernorm_head(v):
    mu = jnp.mean(v, axis=-1, keepdims=True)
    d = v - mu
    rstd = lax.rsqrt(jnp.mean(d * d, axis=-1, keepdims=True) + EPS)
    return d * rstd, rstd


def _even_fwd(p7, conv_w, ln_g, ln_b, sgu_w, sgu_bias):
    T, C = p7.shape[1], CHUNK_ROWS

    def body(p_ref, cw_ref, lg_ref, lb_ref, w_ref, b_ref, y_ref):
        w0, w1, w2 = cw_ref[0:1, :], cw_ref[1:2, :], cw_ref[2:3, :]
        wm = jnp.where(_causal(), w_ref[...], 0.0).astype(bf16)
        bias, lg, lb = b_ref[...], lg_ref[...], lb_ref[...]

        def step(i, halo):
            rows = pl.ds(pl.multiple_of(i * C, C), C)
            ah, ab, ac, az, u, v, zb = (p_ref[k, rows, :].astype(f32) for k in range(7))
            tt = ac * ah
            ext = jnp.concatenate([halo, tt], axis=0)
            cv = w2 * tt + w1 * pltpu.roll(ext, 1, 0)[HALO_CONV:] + w0 * pltpu.roll(ext, 2, 0)[HALO_CONV:]
            y_ref[0, rows, :] = (ab * cv * _silu(az)).astype(bf16)
            vhat, _ = _layernorm_head(v)
            vn = (vhat * lg + lb).astype(bf16)
            mix = jnp.concatenate([jnp.dot(wm, vn[k * HEAD:(k + 1) * HEAD], preferred_element_type=f32) + bias
                                   for k in range(C // HEAD)], axis=0)
            y_ref[1, rows, :] = (u * mix * _silu(zb)).astype(bf16)
            return tt[C - HALO_CONV:]

        lax.fori_loop(0, T // C, step, jnp.zeros((HALO_CONV, HEAD), f32))

    return _pcall(body, name="even_fwd", out_shape=_sds((2, T, D), bf16), grid=(NH,),
                  in_specs=[_head_spec(7, T), _head_vec(3), _head_vec(1), _head_vec(1), _HEAD_MAT, _HEAD_MAT],
                  out_specs=_head_spec(2, T))(p7, conv_w, ln_g, ln_b, sgu_w, sgu_bias)


def _even_bwd(p7, dy2, conv_w, ln_g, ln_b, sgu_w, sgu_bias):
    T, C = p7.shape[1], CHUNK_ROWS
    n_chunks = T // C

    def body(p_ref, dy_ref, cw_ref, lg_ref, lb_ref, w_ref, b_ref,
             dp_ref, dcw_ref, dlg_ref, dlb_ref, dw_ref, dms_ref, dcv_s):
        w0, w1, w2 = cw_ref[0:1, :], cw_ref[1:2, :], cw_ref[2:3, :]
        tri = _causal()
        wm = jnp.where(tri, w_ref[...], 0.0).astype(bf16)
        bias, lg, lb = b_ref[...], lg_ref[...], lb_ref[...]
        dw_ref[...] = jnp.zeros_like(dw_ref)
        dms_ref[...] = jnp.zeros_like(dms_ref)

        def fwd_step(i, carry):
            halo, a0, a1, a2, alg, alb = carry
            rows = pl.ds(pl.multiple_of(i * C, C), C)
            ah, ab, ac, az = (p_ref[k, rows, :].astype(f32) for k in range(4))
            dya = dy_ref[0, rows, :].astype(f32)
            tt = ac * ah
            ext = jnp.concatenate([halo, tt], axis=0)
            t1, t2 = pltpu.roll(ext, 1, 0)[HALO_CONV:], pltpu.roll(ext, 2, 0)[HALO_CONV:]
            cv = w2 * tt + w1 * t1 + w0 * t2
            sa, dsa = _silu_and_grad(az)
            g1 = dya * sa
            dp_ref[1, rows, :] = (g1 * cv).astype(bf16)
            dp_ref[3, rows, :] = (dya * ab * cv * dsa).astype(bf16)
            dcv = g1 * ab
            dcv_s[rows, :] = dcv
            a2 = a2 + jnp.sum(dcv * tt, axis=0, keepdims=True)
            a1 = a1 + jnp.sum(dcv * t1, axis=0, keepdims=True)
            a0 = a0 + jnp.sum(dcv * t2, axis=0, keepdims=True)

            u, zb, dyb = p_ref[4, rows, :].astype(f32), p_ref[6, rows, :].astype(f32), dy_ref[1, rows, :].astype(f32)
            vhat, rstd = _layernorm_head(p_ref[5, rows, :].astype(f32))
            vn = (vhat * lg + lb).astype(bf16)
            sb, dsb = _silu_and_grad(zb)
            mix = jnp.concatenate([jnp.dot(wm, vn[k * HEAD:(k + 1) * HEAD], preferred_element_type=f32) + bias
                                   for k in range(C // HEAD)], axis=0)
            dp_ref[4, rows, :] = (dyb * mix * sb).astype(bf16)
            dp_ref[6, rows, :] = (dyb * u * mix * dsb).astype(bf16)
            dmix = dyb * u * sb
            dvn_parts = []
            for k in range(C // HEAD):
                dm = dmix[k * HEAD:(k + 1) * HEAD]
                dmb = dm.astype(bf16)
                dvn_parts.append(lax.dot_general(wm, dmb, (TN, ((), ())), preferred_element_type=f32))
                dw_ref[...] += lax.dot_general(dmb, vn[k * HEAD:(k + 1) * HEAD], (NT, ((), ())),
                                               preferred_element_type=f32)
                dms_ref[...] += dm
            dvn = jnp.concatenate(dvn_parts, axis=0)
            alg = alg + jnp.sum(dvn * vhat, axis=0, keepdims=True)
            alb = alb + jnp.sum(dvn, axis=0, keepdims=True)
            dvh = dvn * lg
            dv = rstd * (dvh - jnp.mean(dvh, axis=-1, keepdims=True)
                         - vhat * jnp.mean(dvh * vhat, axis=-1, keepdims=True))
            dp_ref[5, rows, :] = dv.astype(bf16)
            return tt[C - HALO_CONV:], a0, a1, a2, alg, alb

        zrow = jnp.zeros((1, HEAD), f32)
        _, a0, a1, a2, alg, alb = lax.fori_loop(
            0, n_chunks, fwd_step, (jnp.zeros((HALO_CONV, HEAD), f32), zrow, zrow, zrow, zrow, zrow))
        dcw_ref[0:1, :], dcw_ref[1:2, :], dcw_ref[2:3, :] = a0, a1, a2
        dlg_ref[...], dlb_ref[...] = alg, alb
        dw_ref[...] = jnp.where(tri, dw_ref[...], 0.0)

        def bwd_step(k, halo):
            rows = pl.ds(pl.multiple_of((n_chunks - 1 - k) * C, C), C)
            dcv = dcv_s[rows, :]
            ext = jnp.concatenate([dcv, halo], axis=0)
            n1 = pltpu.roll(ext, C + HALO_CONV - 1, 0)[:C]
            n2 = pltpu.roll(ext, C + HALO_CONV - 2, 0)[:C]
            dtt = w2 * dcv + w1 * n1 + w0 * n2
            dp_ref[2, rows, :] = (dtt * p_ref[0, rows, :].astype(f32)).astype(bf16)
            dp_ref[0, rows, :] = (dtt * p_ref[2, rows, :].astype(f32)).astype(bf16)
            return dcv[:HALO_CONV]

        lax.fori_loop(0, n_chunks, bwd_step, jnp.zeros((HALO_CONV, HEAD), f32))

    out_shape = [_sds((7, T, D), bf16), _sds((3, D), f32), _sds((1, D), f32), _sds((1, D), f32),
                 _sds((NH, HEAD, HEAD), f32), _sds((NH, HEAD, HEAD), f32)]
    return _pcall(body, name="even_bwd", out_shape=out_shape, grid=(NH,),
                  in_specs=[_head_spec(7, T), _head_spec(2, T), _head_vec(3), _head_vec(1), _head_vec(1),
                            _HEAD_MAT, _HEAD_MAT],
                  out_specs=[_head_spec(7, T), _head_vec(3), _head_vec(1), _head_vec(1), _HEAD_MAT, _HEAD_MAT],
                  scratch=[pltpu.VMEM((T, HEAD), f32)])(p7, dy2, conv_w, ln_g, ln_b, sgu_w, sgu_bias)


def _window_sum(ext, win, towards_past):
    n, k, s = ext.shape[0], 1, ext
    while k < win:
        s = s + pltpu.roll(s, k if towards_past else n - k, 0)
        k *= 2
    return s


def _pool_count(i, C, win):
    t = i * C + lax.broadcasted_iota(jnp.int32, (C, 1), 0)
    cnt = jnp.minimum(t + 1, win).astype(f32)
    return cnt, 1.0 / cnt


def _group_specs(T):
    p_spec = pl.BlockSpec((None, T, GC), lambda g: (0, 0, g))
    z_spec = pl.BlockSpec((None, T, GC), lambda g: (1, 0, g))
    pw_spec = pl.BlockSpec((4, GC // 4, GC), lambda g: (0, g, 0))
    ps_spec = pl.BlockSpec((1, GC), lambda g: (0, g))
    y_spec = pl.BlockSpec((None, T, GC), lambda g: (g // 2, 0, g % 2))
    return p_spec, z_spec, pw_spec, ps_spec, y_spec


def _odd_fwd(p2, pool_wg, pool_scale):
    T, C = p2.shape[1], CHUNK_ROWS
    p_spec, z_spec, pw_spec, ps_spec, y_spec = _group_specs(T)

    def body(p_ref, z_ref, pw_ref, ps_ref, y_ref):
        pw, ps = pw_ref[...].reshape(GC, GC), ps_ref[...]

        def run(win):
            def step(i, halo):
                rows = pl.ds(pl.multiple_of(i * C, C), C)
                p = p_ref[rows, :].astype(f32)
                s = _window_sum(jnp.concatenate([halo, p], axis=0), win, True)[HALO_POOL:]
                pooled = s * _pool_count(i, C, win)[1] - p
                ypre = jnp.dot(pooled.astype(bf16), pw, preferred_element_type=f32)
                y_ref[rows, :] = (ypre * ps * _silu(z_ref[rows, :].astype(f32))).astype(bf16)
                return p[C - HALO_POOL:]

            lax.fori_loop(0, T // C, step, jnp.zeros((HALO_POOL, GC), f32))

        for gi, win in enumerate(WINDOWS):
            pl.when(pl.program_id(0) == gi)(functools.partial(run, win))

    return _pcall(body, name="odd_fwd", out_shape=_sds((2, T, D), bf16), grid=(len(WINDOWS),),
                  in_specs=[p_spec, z_spec, pw_spec, ps_spec], out_specs=y_spec)(p2, p2, pool_wg, pool_scale)


def _odd_bwd(p2, dy2, pool_wg, pool_scale):
    T, C = p2.shape[1], CHUNK_ROWS
    n_chunks = T // C
    p_spec, z_spec, pw_spec, ps_spec, y_spec = _group_specs(T)

    def body(p_ref, z_ref, dy_ref, pw_ref, ps_ref, dp_ref, dpw_ref, dps_ref, q_s, acc_s):
        pw, ps = pw_ref[...].reshape(GC, GC), ps_ref[...]

        def run(win):
            acc_s[...] = jnp.zeros_like(acc_s)

            def fwd_step(i, carry):
                halo, aps = carry
                rows = pl.ds(pl.multiple_of(i * C, C), C)
                p, z, dy = p_ref[rows, :].astype(f32), z_ref[rows, :].astype(f32), dy_ref[rows, :].astype(f32)
                _, inv_cnt = _pool_count(i, C, win)
                s = _window_sum(jnp.concatenate([halo, p], axis=0), win, True)[HALO_POOL:]
                pb = (s * inv_cnt - p).astype(bf16)
                ypre = jnp.dot(pb, pw, preferred_element_type=f32)
                sz, dsz = _silu_and_grad(z)
                aps = aps + jnp.sum(dy * ypre * sz, axis=0, keepdims=True)
                dp_ref[1, rows, :] = (dy * ypre * ps * dsz).astype(bf16)
                dyp = (dy * ps * sz).astype(bf16)
                acc_s[...] += lax.dot_general(pb, dyp, (TN, ((), ())), preferred_element_type=f32)
                dpool = lax.dot_general(dyp, pw, (NT, ((), ())), preferred_element_type=f32)
                q_s[rows, :] = dpool * inv_cnt
                return p[C - HALO_POOL:], aps

            _, aps = lax.fori_loop(0, n_chunks, fwd_step, (jnp.zeros((HALO_POOL, GC), f32), jnp.zeros((1, GC), f32)))
            dps_ref[...] = aps
            dpw_ref[...] = acc_s[...].reshape(4, GC // 4, GC).astype(bf16)

            def bwd_step(k, halo):
                i = n_chunks - 1 - k
                rows = pl.ds(pl.multiple_of(i * C, C), C)
                q = q_s[rows, :]
                s = _window_sum(jnp.concatenate([q, halo], axis=0), win, False)[:C]
                dp_ref[0, rows, :] = (s - q * _pool_count(i, C, win)[0]).astype(bf16)
                return q[:HALO_POOL]

            lax.fori_loop(0, n_chunks, bwd_step, jnp.zeros((HALO_POOL, GC), f32))

        for gi, win in enumerate(WINDOWS):
            pl.when(pl.program_id(0) == gi)(functools.partial(run, win))

    out_shape = [_sds((2, T, 2 * D), bf16), _sds((4, GC, GC), bf16), _sds((1, 2 * D), f32)]
    return _pcall(body, name="odd_bwd", out_shape=out_shape, grid=(len(WINDOWS),),
                  in_specs=[p_spec, z_spec, y_spec, pw_spec, ps_spec],
                  out_specs=[pl.BlockSpec((2, T, GC), lambda g: (0, 0, g)), pw_spec, ps_spec],
                  scratch=[pltpu.VMEM((T, GC), f32), pltpu.VMEM((GC, GC), f32)], vmem_mb=44)(
                      p2, p2, dy2, pool_wg, pool_scale)


def _ada_fwd(c_all, ada_w):
    cols = ada_w.shape[2]

    def body(c_ref, w_ref, o_ref):
        o_ref[...] = jnp.dot(_silu(c_ref[...]), w_ref[...], preferred_element_type=f32,
                             precision=lax.Precision.HIGHEST)

    return _pcall(body, name="ada_fwd", out_shape=_sds((4, N_DEV, cols), f32), grid=(4,),
                  in_specs=[pl.BlockSpec((N_DEV, D), lambda i: (0, 0)), pl.BlockSpec((None, D, cols), lambda i: (i, 0, 0))],
                  out_specs=pl.BlockSpec((None, N_DEV, cols), lambda i: (i, 0, 0)))(c_all, ada_w)


def _ada_bwd(c_all_t, dmod, w, m, v):
    cols, tr = w.shape[2], 256
    spec = pl.BlockSpec((None, tr, cols), lambda l, i: (l, i, 0))

    def body(c_ref, dm_ref, w_ref, m_ref, v_ref, g_ref, d_ref, mo_ref, vo_ref):
        sc = _silu(c_ref[...])
        g = sc[:, 0:1] * dm_ref[0:1, :]
        for b in range(1, N_DEV):
            g = g + sc[:, b:b + 1] * dm_ref[b:b + 1, :]
        g_ref[...] = g
        d_ref[...], mo_ref[...], vo_ref[...] = _adamw_math(w_ref[...], g, m_ref[...], v_ref[...])

    return _pcall(body, name="ada_bwd", out_shape=[_sds(w.shape, f32)] * 4, grid=(4, D // tr),
                  in_specs=[pl.BlockSpec((tr, N_DEV), lambda l, i: (i, 0)),
                            pl.BlockSpec((None, N_DEV, cols), lambda l, i: (l, 0, 0)), spec, spec, spec],
                  out_specs=[spec] * 4)(c_all_t, dmod, w, m, v)


def _layer_fwd(even, x, hb, gate, w, nxt, before_out=None, after_proj=None):
    if even:
        w_in, w_out, conv_w, ln_g, ln_b, sgu_w, sgu_b = w
        bias = jnp.broadcast_to(sgu_b[:, :, None], (NH, HEAD, HEAD))
        p = EVEN_PROJ.fwd(hb, w_in)
        if after_proj is not None:
            conv_w = conv_w + after_proj(p)[0:1, 0:1]
        y2 = _even_fwd(p, conv_w, ln_g, ln_b, sgu_w, bias)
    else:
        w_in, pool_w, w_out, pool_scale = w
        p = ODD_PROJ.fwd(hb, w_in)
        y2 = _odd_fwd(p, pool_w, pool_scale if after_proj is None else pool_scale + after_proj(p)[0:1, 0:1])
    if before_out is not None:
        late_w_out, tok = before_out(y2)
        if late_w_out is not None:
            w_out = late_w_out
            w = (w_in, w_out) + tuple(w[2:]) if even else (w_in, pool_w, w_out, pool_scale)
        if tok is not None:
            gate = gate + tok[0:1, 0:1]
    outs = _out_proj(y2, w_out.reshape(2, D, D), x, gate, nxt)
    return outs[0], (None if nxt is None else outs[2]), (x, hb, p, y2, outs[1]), w


def _layer_bwd(even, gin, dob, dgate, saved, scale, g, w, below=None, send=None):
    x_in, hb, p, y2, o = saved
    if even:
        w_in, w_out, conv_w, ln_g, ln_b, sgu_w, sgu_b = w
        bias = jnp.broadcast_to(sgu_b[:, :, None], (NH, HEAD, HEAD))
        dy2, dwo = _out_bwd(dob, w_out, y2)
        dp, dconv, dlg, dlb, dsw, dms = _even_bwd(p, dy2, conv_w, ln_g, ln_b, sgu_w, bias)
        proj = EVEN_PROJ
        small = dict(conv_w=dconv, ln_g=dlg, ln_b=dlb, sgu_w=dsw, sgu_b=jnp.sum(dms, axis=-1))
        big = [proj.dw(hb, dp), dwo]
    else:
        w_in, pool_w, w_out, pool_scale = w
        dy2, dwo = _out_bwd(dob, w_out, y2)
        dp, dpw, dps = _odd_bwd(p, dy2, pool_w, pool_scale)
        proj = ODD_PROJ
        small = dict(pool_scale=dps)
        big = [proj.dw(hb, dp), dpw, dwo]
    tok = None
    if send is not None:
        big, tok = send(big)
    dh = proj.dh(dp, w_in, tok)
    res = _norm_bwd(x_in, dh, gin, g, scale, below)
    stats = res[1]
    return (res[0], (None if below is None else (res[2], res[3])), big, small,
            jnp.concatenate([stats[0:2], dgate], axis=0), stats[2:3])


def _pack_rows(parts):
    rows = [p.reshape(-1, LANES) for p in parts]
    total = sum(r.shape[0] for r in rows)
    padded = -(-total // (8 * N_DEV)) * (8 * N_DEV)
    if padded > total:
        rows.append(jnp.zeros((padded - total, LANES), f32))
    return jnp.concatenate(rows, axis=0)


def _unpack_rows(buf, shapes):
    out, r = [], 0
    for shp in shapes:
        n = 1
        for d in shp:
            n *= d
        out.append(buf[r:r + n // LANES].reshape(shp))
        r += n // LANES
    return out


def kernel(x, c, norm_g, ada_w, ada_b, ab_w_in, ab_conv_w, ab_ln_g, ab_ln_b, ab_sgu_w, ab_sgu_b, ab_w_out, c_w_in, c_pool_w, c_pool_scale, c_w_out, final_g, loss_target, m_norm_g, m_ada_w, m_ada_b, m_ab_w_in, m_ab_conv_w, m_ab_ln_g, m_ab_ln_b, m_ab_sgu_w, m_ab_sgu_b, m_ab_w_out, m_c_w_in, m_c_pool_w, m_c_pool_scale, m_c_w_out, m_final_g, v_norm_g, v_ada_w, v_ada_b, v_ab_w_in, v_ab_conv_w, v_ab_ln_g, v_ab_ln_b, v_ab_sgu_w, v_ab_sgu_b, v_ab_w_out, v_c_w_in, v_c_pool_w, v_c_pool_scale, v_c_w_out, v_final_g):
    ix, iy, ic = _place()
    chip, dev = 2 * ix + iy, 4 * ix + 2 * iy + ic
    n_even, n_odd = ab_w_in.shape[0], c_w_in.shape[0]
    depth = n_even + n_odd
    acols = ada_w.shape[2]

    place = jnp.stack([chip, ic]).astype(jnp.int32)
    even_names, odd_names = ["ab_w_in", "ab_w_out"], ["c_w_in", "c_pool_w", "c_w_out"]
    params = {"ab_w_in": (ab_w_in, m_ab_w_in, v_ab_w_in), "ab_w_out": (ab_w_out, m_ab_w_out, v_ab_w_out),
              "c_w_in": (c_w_in, m_c_w_in, v_c_w_in), "c_w_out": (c_w_out, m_c_w_out, v_c_w_out),
              "c_pool_w": tuple(a.reshape(n_odd, GC, GC) for a in (c_pool_w, m_c_pool_w, v_c_pool_w))}

    def placed(names, layer, after=None):
        ws = [params[nm][0] for nm in names]
        return [p.reshape(4, 2, p.shape[1] // 2, p.shape[2]) for p in _cast_place(place, ws, layer, after)]

    def whole(arrays):
        return [g.reshape(4, 2 * g.shape[2], g.shape[3]) for g in arrays]

    first = _gather8(jnp.concatenate([c, ab_conv_w.reshape(1, -1), c_pool_scale.reshape(1, -1)], axis=1), "gather_c")
    c_all, small_all = first[:, 0, :D], first[0::2, 0, D:]
    sems_a, in_a, tok = _ag_start([placed(even_names[:1], 0)], first[0:1, 0, 0:LANES], "ag_start_0a")
    modp = _ada_fwd(c_all, ada_w)
    later = [placed(even_names[1:], 0, tok)]
    later += [placed(even_names if i % 2 == 0 else odd_names, i // 2, tok) for i in range(1, depth)]
    modg = _gather8(modp + tok[0:1, 0:1], "gather_mod", [lay[-1] for lay in later])
    mod_rows = lax.dynamic_index_in_dim(modg[0::2], dev, axis=2, keepdims=False)
    mod = jnp.transpose(mod_rows, (1, 0, 2)).reshape(depth, 3 * D) + ada_b
    mods = [(mod[i:i + 1, 0:D], mod[i:i + 1, D:2 * D], mod[i:i + 1, 2 * D:3 * D]) for i in range(depth)]

    def shard_cols(a, width):
        return lax.dynamic_slice_in_dim(a, chip * width, width, axis=a.ndim - 1)

    n_conv = ab_conv_w.size
    conv_all = small_all[:, :n_conv].reshape(4, n_even, 3, D // 4)
    conv_full = jnp.transpose(conv_all, (1, 2, 0, 3)).reshape(n_even, 3, D)
    scale_all = small_all[:, n_conv:].reshape(4, n_odd, 2 * D // 4)
    scale_full = jnp.transpose(scale_all, (1, 0, 2)).reshape(n_odd, 2 * D)

    gathers_done = mod[0:1, 0:LANES] + scale_full[0:1, 0:LANES]
    sems_b, in_b, tok = _ag_start(later[:1], gathers_done, "ag_start_0b")
    sems_r, in_r, tok = _ag_start(later[1:], tok, "ag_start_rest")

    x_cur, saved, weights, first_part, last_part = x[0], [], [], {}, {}

    def hand_off(arrays, sems, after, first, tag):
        arrived = _ag_wait(arrays, sems, after, f"ag_wait_{tag}", first)
        sems_f, inflight, tok = _agf_start(arrived, f"agf_start_{tag}")
        return (sems_f, inflight), tok

    first_part[0], tok = hand_off(in_a[0], sems_a[0], tok, 0, "0a")
    hb = _hnorm(x_cur, norm_g[0:1], mods[0][0] + tok[0:1, 0:1], mods[0][1])
    for i in range(depth):
        j = i // 2
        if i not in first_part:
            first_part[i], _ = hand_off(in_r[i - 1][:-1], sems_r[i - 1], x_cur, 0, f"{i}a")
        full = whole(_agf_wait(*first_part.pop(i), hb if i == 0 else x_cur, f"agf_wait_{i}a")) + [None]
        if i % 2 == 0:
            w = (full[0], None, conv_full[j], ab_ln_g[j:j + 1], ab_ln_b[j:j + 1], ab_sgu_w[j], ab_sgu_b[j])
        else:
            w = (full[0], full[1], None, scale_full[j:j + 1])

        def after_proj(p, i=i):
            arrays, sems = (in_b[0], sems_b[0]) if i == 0 else (in_r[i - 1][-1:], sems_r[i - 1])
            last_part[i], tok = hand_off(arrays, sems, p, 0 if i == 0 else len(in_r[i - 1]) - 1, f"{i}b")
            return tok

        def before_out(y2, i=i):
            w_out, tok = whole(_agf_wait(*last_part.pop(i), y2, f"agf_wait_{i}b"))[0], None
            if i + 1 < depth and i + 1 != LATE_LAYER:
                first_part[i + 1], tok = hand_off(in_r[i][:-1], sems_r[i], y2, 0, f"{i + 1}a")
            return w_out, tok

        nxt = (norm_g[i + 1:i + 2], mods[i + 1][0], mods[i + 1][1]) if i + 1 < depth else None
        x_cur, hb, sv, w = _layer_fwd(i % 2 == 0, x_cur, hb, mods[i][2], w, nxt, before_out, after_proj)
        weights.append(w)
        saved.append(sv)
    gin, loss, dfinal_g, dob, dgate = _loss_bwd(x_cur, loss_target[0], final_g.reshape(1, D), saved[-1][4],
                                                mods[-1][2])

    stacked = {}

    def reduce_layer(i, sems, pairs, lands, after):
        pairs, slots = _rs_chip_wait(sems, pairs, lands, after, f"rs_chip_wait_{i}")
        half_sems, halves, _ = _rs_half_start(_rs_sum(place, pairs, slots), f"rs_half_start_{i}")
        return i, half_sems, halves

    def update_layer(i, half_sems, halves, after):
        names = even_names if i % 2 == 0 else odd_names
        grads = _rs_half_wait(half_sems, halves, after, f"rs_half_wait_{i}")
        items = [(params[nm][0], g.reshape(params[nm][0].shape[1:]), params[nm][1], params[nm][2], stacked.get(nm))
                 for nm, g in zip(names, grads)]
        for nm, res in zip(names, _adamw_layer(i // 2, items)):
            stacked[nm] = res
            updated.append(res[1])

    updated = []
    small_g, dmod, dnorm_g, pending, tok = [None] * depth, [None] * depth, [None] * depth, None, None
    exchanging = []
    for i in reversed(range(depth)):
        w = weights[i]
        if tok is not None:
            w = w[:2] + (w[2] + tok[0:1, 0:1],) + w[3:] if i % 2 == 0 else w[:3] + (w[3] + tok[0:1, 0:1],)
        below = (saved[i - 1][4], mods[i - 1][2]) if i > 0 else None

        def send(big_g, i=i):
            if exchanging:
                update_layer(*exchanging.pop(), big_g[0])
            big_g = [g.reshape(4, 2, g.shape[1] // 2, g.shape[2]) for g in big_g]
            sems, big_g, lands, tok = _rs_pair_start(big_g, f"rs_pair_start_{i}")
            return (sems, big_g, lands), tok

        gin, gate_bwd, sent, small_g[i], dmod[i], dnorm_g[i] = _layer_bwd(
            i % 2 == 0, gin, dob, dgate, saved[i], mods[i][1], norm_g[i:i + 1], w, below, send)
        if below is not None:
            dob, dgate = gate_bwd
        after = gin
        if i == 0:
            dmod_all = _gather8(jnp.stack(dmod).reshape(depth * 3 * D // LANES, LANES), "gather_dmod")
            after = dmod_all = dmod_all.reshape(N_DEV, depth, 3 * D)
        if i > 0:
            after, updated = [after] + updated, []
        else:
            after = [after]
        big_g, theirs = _rs_pair_wait(*sent, after, f"rs_pair_wait_{i}")
        pairs = _rs_add(place, big_g, theirs)
        sems, pairs, lands, tok = _rs_chip_start(pairs, f"rs_chip_start_{i}")
        if pending is not None:
            exchanging.append(reduce_layer(*pending, [tok]))
        pending = (i, sems, pairs, lands)
    grad_x = gin
    dnorm_g = jnp.concatenate(dnorm_g, axis=0)

    dmod_cols = jnp.transpose(shard_cols(dmod_all, acols), (1, 0, 2))
    r_ada_w = _ada_bwd(c_all.T, dmod_cols, ada_w, m_ada_w, v_ada_w)
    update_layer(*exchanging.pop(), r_ada_w[1])
    last = reduce_layer(*pending, [r_ada_w[1]] + updated)

    small_parts = [dnorm_g, dfinal_g,
                   jnp.stack([small_g[2 * j]["conv_w"] for j in range(n_even)]),
                   jnp.concatenate([small_g[2 * j]["ln_g"] for j in range(n_even)], axis=0),
                   jnp.concatenate([small_g[2 * j]["ln_b"] for j in range(n_even)], axis=0),
                   jnp.stack([small_g[2 * j]["sgu_b"] for j in range(n_even)]),
                   jnp.concatenate([small_g[2 * j + 1]["pool_scale"] for j in range(n_odd)], axis=0),
                   jnp.pad(loss, ((0, 7), (0, LANES - 1)))]
    small_shapes = [p.shape for p in small_parts]
    sgu_parts = [small_g[2 * j]["sgu_w"].reshape(NH * HEAD, HEAD).astype(bf16) for j in range(n_even)]
    reduced = _allreduce8([_pack_rows(small_parts)] + sgu_parts, "allreduce_small", last[2][0])
    update_layer(*last, reduced[0])
    r_ab_w_in, r_ab_w_out, r_c_w_in, r_c_w_out = (stacked[nm] for nm in ("ab_w_in", "ab_w_out", "c_w_in", "c_w_out"))
    r_c_pool_w = tuple(a.reshape(c_pool_w.shape) for a in stacked["c_pool_w"])
    g_norm_g, g_final_g, g_conv_full, g_ln_g, g_ln_b, g_sgu_b, g_scale_full, loss_row = _unpack_rows(reduced[0],
                                                                                                     small_shapes)
    g_sgu_w = jnp.stack(reduced[1:]).astype(f32)
    loss = loss_row[0, 0]
    g_conv = shard_cols(g_conv_full, D // 4)
    g_scale = shard_cols(g_scale_full, 2 * D // 4)

    def two_d(a):
        return a.reshape(-1, a.shape[-1])

    small = [(norm_g, g_norm_g, m_norm_g, v_norm_g),
             (ada_b, dmod_all, m_ada_b, v_ada_b),
             (two_d(ab_conv_w), two_d(g_conv), two_d(m_ab_conv_w), two_d(v_ab_conv_w)),
             (ab_ln_g, g_ln_g, m_ab_ln_g, v_ab_ln_g),
             (ab_ln_b, g_ln_b, m_ab_ln_b, v_ab_ln_b),
             (two_d(ab_sgu_w), two_d(g_sgu_w), two_d(m_ab_sgu_w), two_d(v_ab_sgu_w)),
             (two_d(ab_sgu_b), two_d(g_sgu_b), two_d(m_ab_sgu_b), two_d(v_ab_sgu_b)),
             (c_pool_scale, g_scale, m_c_pool_scale, v_c_pool_scale),
             (final_g.reshape(1, D), g_final_g, m_final_g.reshape(1, D), v_final_g.reshape(1, D))]
    small_res = _adamw_small(small)
    small_shapes_out = [norm_g.shape, ada_b.shape, ab_conv_w.shape, ab_ln_g.shape, ab_ln_b.shape, ab_sgu_w.shape,
                        ab_sgu_b.shape, c_pool_scale.shape, final_g.shape]
    (r_norm_g, r_ada_b, r_conv, r_ln_g, r_ln_b, r_sgu_w, r_sgu_b, r_scale, r_final_g) = [
        tuple(a.reshape(shp) for a in res) for res, shp in zip(small_res, small_shapes_out)]

    order = [r_norm_g, r_ada_w, r_ada_b, r_ab_w_in, r_conv, r_ln_g, r_ln_b, r_sgu_w, r_sgu_b, r_ab_w_out,
             r_c_w_in, r_c_pool_w, r_scale, r_c_w_out, r_final_g]
    outs = [loss, grad_x[None]]
    for field in range(4):
        outs += [r[field] for r in order]
    return tuple(outs)
```

```python
import functools

import jax
import jax.numpy as jnp
from jax import lax
from jax.experimental import pallas as pl
from jax.experimental.pallas import tpu as pltpu

f32, bf16 = jnp.float32, jnp.bfloat16

D = 1024
HEAD = 128
NH = 8
WINDOWS = (2, 4, 8, 16)
GC = 512
EPS = 1e-6
HALO_CONV = 8
HALO_POOL = 16
CHUNK_ROWS = 512
DH_WIDE = 1024
FWD_TILES = 2
LATE_LAYER = 2
N_DEV = 8
LANES = 128

ADAM_LR, ADAM_B1, ADAM_B2, ADAM_EPS, ADAM_WD, ADAM_STEP = 0.001, 0.9, 0.999, 1e-08, 0.01, 10

MESH = pl.DeviceIdType.MESH
ANY = pl.BlockSpec(memory_space=pl.ANY)
VMEM = pl.BlockSpec(memory_space=pltpu.VMEM)
MIB = 2 ** 20


def _pcall(body, *, name, out_shape, grid=None, in_specs=None, out_specs=None, scratch=(), vmem_mb=None,
           aliases=None, prefetch=0):
    kw = {}
    if prefetch:
        kw["grid_spec"] = pltpu.PrefetchScalarGridSpec(num_scalar_prefetch=prefetch, grid=grid, in_specs=in_specs,
                                                       out_specs=out_specs, scratch_shapes=list(scratch))
    else:
        if grid is not None:
            kw["grid"] = grid
        if in_specs is not None:
            kw["in_specs"] = in_specs
        if out_specs is not None:
            kw["out_specs"] = out_specs
        if scratch:
            kw["scratch_shapes"] = list(scratch)
    if aliases:
        kw["input_output_aliases"] = aliases
    params = pltpu.CompilerParams(vmem_limit_bytes=None if vmem_mb is None else vmem_mb * MIB)
    return pl.pallas_call(body, name=name, out_shape=out_shape, compiler_params=params, **kw)


def _sds(shape, dtype):
    return jax.ShapeDtypeStruct(tuple(shape), dtype)


def _sigmoid(z):
    return pl.reciprocal(1.0 + jnp.exp(-z), approx=True)


def _silu(z):
    return z * _sigmoid(z)


def _silu_and_grad(z):
    s = _sigmoid(z)
    return z * s, s * (1.0 + z * (1.0 - s))


def _place():
    return lax.axis_index("x"), lax.axis_index("y"), lax.axis_index("c")


def _gather8(blk, name, after=()):
    def body(x_ref, *rest):
        o_ref, ssem, rsem = rest[len(after):]
        x, y, c = _place()
        me = 4 * x + 2 * y + c
        o_ref[me] = x_ref[...]
        sends = []
        for k in range(1, N_DEV):
            px = 1 - x if k & 4 else x
            py = 1 - y if k & 2 else y
            pc = 1 - c if k & 1 else c
            cp = pltpu.make_async_remote_copy(src_ref=x_ref, dst_ref=o_ref.at[me], send_sem=ssem.at[k - 1],
                                              recv_sem=rsem.at[k - 1], device_id=(px, py, pc), device_id_type=MESH)
            cp.start()
            sends.append((cp, 4 * px + 2 * py + pc))
        for k, (cp, peer) in enumerate(sends):
            pltpu.make_async_remote_copy(src_ref=x_ref, dst_ref=o_ref.at[peer], send_sem=ssem.at[k],
                                         recv_sem=rsem.at[k], device_id=(x, y, c), device_id_type=MESH).wait_recv()
        for cp, _ in sends:
            cp.wait_send()

    return _pcall(body, name=name, out_shape=_sds((N_DEV,) + blk.shape, blk.dtype), in_specs=[VMEM] + [ANY] * len(after),
                  out_specs=VMEM,
                  scratch=[pltpu.SemaphoreType.DMA((N_DEV - 1,)), pltpu.SemaphoreType.DMA((N_DEV - 1,))])(blk, *after)


def _allreduce8(bufs, name, after=None):
    n, n_after = len(bufs), 0 if after is None else 1
    rbs = [b.shape[0] // N_DEV for b in bufs]
    assert all(rb * N_DEV == b.shape[0] and rb % (16 if b.dtype == bf16 else 8) == 0 for rb, b in zip(rbs, bufs))

    def body(*refs):
        refs = refs[:n] + refs[n + n_after:]
        xs, outs, stages = refs[:n], refs[n:2 * n], refs[2 * n:3 * n]
        ssem, rsem = refs[3 * n:]
        x, y, c = _place()
        me = 4 * x + 2 * y + c
        peers = []
        for k in range(1, N_DEV):
            px = 1 - x if k & 4 else x
            py = 1 - y if k & 2 else y
            pc = 1 - c if k & 1 else c
            peers.append(((px, py, pc), 4 * px + 2 * py + pc))

        def blk(t, ref, idx):
            return ref.at[pl.ds(pl.multiple_of(idx * rbs[t], 8), rbs[t]), :]

        def copy(t, phase, k, src, dst, dev):
            return pltpu.make_async_remote_copy(src_ref=src, dst_ref=dst, send_sem=ssem.at[t, phase, k],
                                                recv_sem=rsem.at[t, phase, k], device_id=dev, device_id_type=MESH)

        scatter = [copy(t, 0, k, blk(t, xs[t], pidx), stages[t].at[me], dev)
                   for t in range(n) for k, (dev, pidx) in enumerate(peers)]
        for cp in scatter:
            cp.start()
        gather = []
        for t in range(n):
            stages[t][me] = blk(t, xs[t], me)[...]
            for k, (dev, pidx) in enumerate(peers):
                copy(t, 0, k, blk(t, xs[t], pidx), stages[t].at[pidx], dev).wait_recv()
            total = stages[t][0].astype(f32)
            for j in range(1, N_DEV):
                total = total + stages[t][j].astype(f32)
            blk(t, outs[t], me)[...] = total.astype(outs[t].dtype)
            sends = [copy(t, 1, k, blk(t, outs[t], me), blk(t, outs[t], me), dev) for k, (dev, pidx) in enumerate(peers)]
            for cp in sends:
                cp.start()
            gather += sends
        for t in range(n):
            for k, (dev, pidx) in enumerate(peers):
                copy(t, 1, k, blk(t, outs[t], pidx), blk(t, outs[t], pidx), dev).wait_recv()
        for cp in scatter + gather:
            cp.wait_send()

    return _pcall(body, name=name, out_shape=[_sds(b.shape, b.dtype) for b in bufs], in_specs=[VMEM] * n + [ANY] * n_after,
                  out_specs=[VMEM] * n,
                  scratch=[pltpu.VMEM((N_DEV, rb, LANES), b.dtype) for rb, b in zip(rbs, bufs)]
                  + [pltpu.SemaphoreType.DMA((n, 2, N_DEV - 1)), pltpu.SemaphoreType.DMA((n, 2, N_DEV - 1))])(
                      *bufs, *([] if after is None else [after]))


def _other_chips(x, y):
    return [((1 - x, y), 2 * (1 - x) + y), ((x, 1 - y), 2 * x + (1 - y)), ((1 - x, 1 - y), 2 * (1 - x) + (1 - y))]


HBM = pl.BlockSpec(memory_space=pltpu.HBM)
SEM = pl.BlockSpec(memory_space=pltpu.SEMAPHORE)
EFFECT = pltpu.SideEffectType.DATAFLOW_SIDE_EFFECTING


def _in_hbm(a):
    return pltpu.with_memory_space_constraint(a, pltpu.HBM)


SIBLING_ID = 1


def _sibling_handshake():
    x, y, c = _place()
    barrier = pltpu.get_barrier_semaphore()
    pl.semaphore_signal(barrier, inc=1, device_id=(x, y, 1 - c), device_id_type=MESH)
    pl.semaphore_wait(barrier, 1)
    return x, y, c


def _ag_start(layers, after, name):
    flat = [t for lay in layers for t in lay]
    n, nl = len(flat), len(layers)

    def body(*refs):
        src = refs[:n]
        sems = refs[n + 1:n + 1 + 2 * nl]
        token = refs[-1]
        x, y, c = _place()
        s_me = 2 * x + y
        t = 0
        for i, lay in enumerate(layers):
            for k in range(len(lay)):
                for j, ((px, py), _) in enumerate(_other_chips(x, y)):
                    pltpu.make_async_remote_copy(src_ref=src[t].at[s_me, c], dst_ref=src[t].at[s_me, c],
                                                 send_sem=sems[2 * i].at[3 * k + j], recv_sem=sems[2 * i + 1].at[3 * k + j],
                                                 device_id=(px, py, c), device_id_type=MESH).start()
                t += 1
        token[...] = jnp.zeros_like(token)

    sem_shapes = [pltpu.SemaphoreType.DMA((3 * len(lay),)) for lay in layers for _ in range(2)]
    out_shape = sem_shapes + [pltpu.HBM(t.shape, t.dtype) for t in flat] + [_sds((8, LANES), f32)]
    outs = pl.pallas_call(
        body, name=name, out_shape=out_shape, in_specs=[HBM] * n + [ANY],
        out_specs=[SEM] * (2 * nl) + [HBM] * n + [VMEM], input_output_aliases={t: 2 * nl + t for t in range(n)},
        compiler_params=pltpu.CompilerParams(has_side_effects=EFFECT))(*[_in_hbm(t) for t in flat], after)
    sems = [(outs[2 * i], outs[2 * i + 1]) for i in range(nl)]
    thru, t = [], 2 * nl
    for lay in layers:
        thru.append(list(outs[t:t + len(lay)]))
        t += len(lay)
    return sems, thru, outs[-1]


def _ag_wait(inflight, sems, after, name, first=0):
    n = len(inflight)

    def body(*refs):
        src, ssem, rsem = refs[:n], refs[n], refs[n + 1]
        x, y, c = _place()
        s_me = 2 * x + y
        for k in range(n):
            for j, (_, s_p) in enumerate(_other_chips(x, y)):
                cp = pltpu.make_async_remote_copy(src_ref=src[k].at[s_me, c], dst_ref=src[k].at[s_p, c],
                                                  send_sem=ssem.at[3 * (first + k) + j],
                                                  recv_sem=rsem.at[3 * (first + k) + j],
                                                  device_id=(x, y, c), device_id_type=MESH)
                cp.wait_send()
                cp.wait_recv()

    return pl.pallas_call(
        body, name=name, out_shape=[pltpu.HBM(t.shape, t.dtype) for t in inflight],
        in_specs=[HBM] * n + [SEM, SEM, ANY], out_specs=[HBM] * n, input_output_aliases={t: t for t in range(n)},
        compiler_params=pltpu.CompilerParams(has_side_effects=EFFECT))(*inflight, sems[0], sems[1], after)


def _agf_start(arrived, name):
    n = len(arrived)

    def body(*refs):
        o = refs[:n]
        ssem, rsem, token = refs[n], refs[n + 1], refs[-1]
        x, y, c = _sibling_handshake()
        for t in range(n):
            for j, (_, s_p) in enumerate(_other_chips(x, y)):
                pltpu.make_async_remote_copy(src_ref=o[t].at[s_p, c], dst_ref=o[t].at[s_p, c],
                                             send_sem=ssem.at[3 * t + j], recv_sem=rsem.at[3 * t + j],
                                             device_id=(x, y, 1 - c), device_id_type=MESH).start()
        token[...] = jnp.zeros_like(token)

    out_shape = ([pltpu.SemaphoreType.DMA((3 * n,))] * 2 + [pltpu.HBM(a.shape, bf16) for a in arrived]
                 + [_sds((8, LANES), f32)])
    outs = pl.pallas_call(
        body, name=name, out_shape=out_shape, in_specs=[HBM] * n, out_specs=[SEM, SEM] + [HBM] * n + [VMEM],
        input_output_aliases={t: 2 + t for t in range(n)},
        compiler_params=pltpu.CompilerParams(has_side_effects=EFFECT, collective_id=SIBLING_ID))(
            *[_in_hbm(a) for a in arrived])
    return (outs[0], outs[1]), list(outs[2:2 + n]), outs[-1]


def _agf_wait(sems, inflight, after, name):
    n = len(inflight)

    def body(*refs):
        o, ssem, rsem = refs[:n], refs[n], refs[n + 1]
        x, y, c = _place()
        for t in range(n):
            for j, (_, s_p) in enumerate(_other_chips(x, y)):
                cp = pltpu.make_async_remote_copy(src_ref=o[t].at[s_p, c], dst_ref=o[t].at[s_p, 1 - c],
                                                  send_sem=ssem.at[3 * t + j], recv_sem=rsem.at[3 * t + j],
                                                  device_id=(x, y, c), device_id_type=MESH)
                cp.wait_send()
                cp.wait_recv()

    return pl.pallas_call(
        body, name=name, out_shape=[pltpu.HBM(a.shape, bf16) for a in inflight],
        in_specs=[HBM] * n + [SEM, SEM, ANY], out_specs=[HBM] * n, input_output_aliases={t: t for t in range(n)},
        compiler_params=pltpu.CompilerParams(has_side_effects=EFFECT))(*inflight, sems[0], sems[1], after)


def _rs_pair_start(grads, name):
    n = len(grads)

    def body(*refs):
        g, theirs = refs[:n], refs[n:2 * n]
        ssem, rsem, token = refs[2 * n], refs[2 * n + 1], refs[-1]
        x, y, c = _sibling_handshake()
        for t in range(n):
            pltpu.make_async_remote_copy(src_ref=g[t].at[:, 1 - c], dst_ref=theirs[t], send_sem=ssem.at[t],
                                         recv_sem=rsem.at[t], device_id=(x, y, 1 - c), device_id_type=MESH).start()
        token[...] = jnp.zeros_like(token)

    lands = [lax.empty((4,) + g.shape[2:], bf16) for g in grads]
    out_shape = ([pltpu.SemaphoreType.DMA((n,))] * 2 + [pltpu.HBM(g.shape, bf16) for g in grads]
                 + [pltpu.HBM(q.shape, bf16) for q in lands] + [_sds((8, LANES), f32)])
    outs = pl.pallas_call(
        body, name=name, out_shape=out_shape, in_specs=[HBM] * (2 * n), out_specs=[SEM, SEM] + [HBM] * (2 * n) + [VMEM],
        input_output_aliases={t: 2 + t for t in range(2 * n)},
        compiler_params=pltpu.CompilerParams(has_side_effects=EFFECT, collective_id=SIBLING_ID))(
            *[_in_hbm(a) for a in list(grads) + lands])
    return (outs[0], outs[1]), list(outs[2:2 + n]), list(outs[2 + n:2 + 2 * n]), outs[-1]


def _rs_pair_wait(sems, grads, lands, after, name):
    n = len(grads)

    def body(*refs):
        g, theirs = refs[:n], refs[n:2 * n]
        ssem, rsem = refs[2 * n], refs[2 * n + 1]
        x, y, c = _place()
        for t in range(n):
            cp = pltpu.make_async_remote_copy(src_ref=g[t].at[:, 1 - c], dst_ref=theirs[t], send_sem=ssem.at[t],
                                              recv_sem=rsem.at[t], device_id=(x, y, c), device_id_type=MESH)
            cp.wait_send()
            cp.wait_recv()

    outs = pl.pallas_call(
        body, name=name, out_shape=[pltpu.HBM(a.shape, bf16) for a in list(grads) + list(lands)],
        in_specs=[HBM] * (2 * n) + [SEM, SEM] + [ANY] * len(after), out_specs=[HBM] * (2 * n),
        input_output_aliases={t: t for t in range(2 * n)},
        compiler_params=pltpu.CompilerParams(has_side_effects=EFFECT))(*grads, *lands, sems[0], sems[1], *after)
    return list(outs[:n]), list(outs[n:])


def _rs_chip_start(pairs, name):
    n = len(pairs)

    def body(*refs):
        p, q = refs[:n], refs[n:2 * n]
        ssem, rsem, token = refs[2 * n], refs[2 * n + 1], refs[-1]
        x, y, c = _place()
        for t in range(n):
            for j, ((px, py), s_p) in enumerate(_other_chips(x, y)):
                pltpu.make_async_remote_copy(src_ref=p[t].at[s_p], dst_ref=q[t].at[j], send_sem=ssem.at[3 * t + j],
                                             recv_sem=rsem.at[3 * t + j], device_id=(px, py, c), device_id_type=MESH).start()
        token[...] = jnp.zeros_like(token)

    lands = [lax.empty((3,) + p.shape[1:], bf16) for p in pairs]
    out_shape = ([pltpu.SemaphoreType.DMA((3 * n,))] * 2 + [pltpu.HBM(p.shape, bf16) for p in pairs]
                 + [pltpu.HBM(q.shape, bf16) for q in lands] + [_sds((8, LANES), f32)])
    outs = pl.pallas_call(
        body, name=name, out_shape=out_shape, in_specs=[HBM] * (2 * n), out_specs=[SEM, SEM] + [HBM] * (2 * n) + [VMEM],
        input_output_aliases={t: 2 + t for t in range(2 * n)},
        compiler_params=pltpu.CompilerParams(has_side_effects=EFFECT))(*[_in_hbm(a) for a in list(pairs) + lands])
    return (outs[0], outs[1]), list(outs[2:2 + n]), list(outs[2 + n:2 + 2 * n]), outs[-1]


def _rs_chip_wait(sems, pairs, lands, after, name):
    n = len(pairs)

    def body(*refs):
        p, q = refs[:n], refs[n:2 * n]
        ssem, rsem = refs[2 * n], refs[2 * n + 1]
        x, y, c = _place()
        for t in range(n):
            for j, (_, s_p) in enumerate(_other_chips(x, y)):
                cp = pltpu.make_async_remote_copy(src_ref=p[t].at[s_p], dst_ref=q[t].at[j], send_sem=ssem.at[3 * t + j],
                                                  recv_sem=rsem.at[3 * t + j], device_id=(x, y, c), device_id_type=MESH)
                cp.wait_send()
                cp.wait_recv()

    outs = pl.pallas_call(
        body, name=name, out_shape=[pltpu.HBM(a.shape, bf16) for a in list(pairs) + list(lands)],
        in_specs=[HBM] * (2 * n) + [SEM, SEM] + [ANY] * len(after), out_specs=[HBM] * (2 * n),
        input_output_aliases={t: t for t in range(2 * n)},
        compiler_params=pltpu.CompilerParams(has_side_effects=EFFECT))(*pairs, *lands, sems[0], sems[1], *after)
    return list(outs[:n]), list(outs[n:])


def _rs_half_start(halves, name):
    n = len(halves)

    def body(*refs):
        o = refs[:n]
        ssem, rsem, token = refs[n], refs[n + 1], refs[-1]
        x, y, c = _sibling_handshake()
        for t in range(n):
            pltpu.make_async_remote_copy(src_ref=o[t].at[c], dst_ref=o[t].at[c], send_sem=ssem.at[t],
                                         recv_sem=rsem.at[t], device_id=(x, y, 1 - c), device_id_type=MESH).start()
        token[...] = jnp.zeros_like(token)

    out_shape = ([pltpu.SemaphoreType.DMA((n,))] * 2 + [pltpu.HBM(h.shape, h.dtype) for h in halves]
                 + [_sds((8, LANES), f32)])
    outs = pl.pallas_call(
        body, name=name, out_shape=out_shape, in_specs=[HBM] * n, out_specs=[SEM, SEM] + [HBM] * n + [VMEM],
        input_output_aliases={t: 2 + t for t in range(n)},
        compiler_params=pltpu.CompilerParams(has_side_effects=EFFECT, collective_id=SIBLING_ID))(
            *[_in_hbm(h) for h in halves])
    return (outs[0], outs[1]), list(outs[2:2 + n]), outs[-1]


def _rs_half_wait(sems, inflight, after, name):
    n = len(inflight)

    def body(*refs):
        o, ssem, rsem = refs[:n], refs[n], refs[n + 1]
        x, y, c = _place()
        for t in range(n):
            cp = pltpu.make_async_remote_copy(src_ref=o[t].at[c], dst_ref=o[t].at[1 - c], send_sem=ssem.at[t],
                                              recv_sem=rsem.at[t], device_id=(x, y, c), device_id_type=MESH)
            cp.wait_send()
            cp.wait_recv()

    return pl.pallas_call(
        body, name=name, out_shape=[pltpu.HBM(h.shape, h.dtype) for h in inflight],
        in_specs=[HBM] * n + [SEM, SEM, ANY], out_specs=[HBM] * n, input_output_aliases={t: t for t in range(n)},
        compiler_params=pltpu.CompilerParams(has_side_effects=EFFECT))(*inflight, sems[0], sems[1], after)


def _row_spec(tm, cols):
    return pl.BlockSpec((tm, cols), lambda i: (i, 0))


def _vec_spec(cols, rows=1):
    return pl.BlockSpec((rows, cols), lambda i: (0, 0))


def _modulated_norm(xv, g, shift, scale):
    r = lax.rsqrt(jnp.mean(xv * xv, axis=-1, keepdims=True) + EPS)
    return (((xv * r) * g) * (1.0 + scale) + shift).astype(bf16)


def _hnorm(x, g, shift, scale):
    T, tm = x.shape[0], 256

    def body(x_ref, g_ref, sh_ref, sc_ref, h_ref):
        h_ref[...] = _modulated_norm(x_ref[...], g_ref[...], sh_ref[...], sc_ref[...])

    return _pcall(body, name="hnorm", out_shape=_sds((T, D), bf16), grid=(T // tm,),
                  in_specs=[_row_spec(tm, D), _vec_spec(D), _vec_spec(D), _vec_spec(D)],
                  out_specs=_row_spec(tm, D))(x, g, shift, scale)


def _out_proj(y2, wo, x, gate, nxt=None):
    T, tm = x.shape[0], 512

    def body(y_ref, w_ref, x_ref, g_ref, *rest):
        o = jnp.dot(y_ref[0], w_ref[0], preferred_element_type=f32)
        o = o + jnp.dot(y_ref[1], w_ref[1], preferred_element_type=f32)
        xo = x_ref[...] + g_ref[...] * o
        if nxt is None:
            xo_ref, o_ref = rest
        else:
            ng_ref, nsh_ref, nsc_ref, xo_ref, o_ref, h_ref = rest
            h_ref[...] = _modulated_norm(xo, ng_ref[...], nsh_ref[...], nsc_ref[...])
        o_ref[...] = o.astype(bf16)
        xo_ref[...] = xo

    extra = [] if nxt is None else list(nxt)
    n_out = 2 if nxt is None else 3
    return _pcall(body, name="out_proj", out_shape=[_sds((T, D), f32), _sds((T, D), bf16), _sds((T, D), bf16)][:n_out],
                  grid=(T // tm,),
                  in_specs=[pl.BlockSpec((2, tm, D), lambda i: (0, i, 0)), pl.BlockSpec((2, D, D), lambda i: (0, 0, 0)),
                            _row_spec(tm, D), _vec_spec(D)] + [_vec_spec(D)] * len(extra),
                  out_specs=[_row_spec(tm, D)] * n_out, vmem_mb=40)(y2, wo, x, gate, *extra)


def _gate_bwd_tile(dx, o_ref, gate_ref, dob_ref, dgate_ref):
    dob_ref[...] = (dx * gate_ref[...]).astype(bf16)
    dgate_ref[...] += jnp.sum(dx * o_ref[...].astype(f32), axis=0, keepdims=True)


def _loss_bwd(x, target, g, o, gate):
    T, tm = x.shape[0], 512

    def body(x_ref, t_ref, g_ref, o_ref, gate_ref, dx_ref, loss_ref, dg_ref, dob_ref, dgate_ref):
        @pl.when(pl.program_id(0) == 0)
        def _():
            loss_ref[...] = jnp.zeros_like(loss_ref)
            dg_ref[...] = jnp.zeros_like(dg_ref)
            dgate_ref[...] = jnp.zeros_like(dgate_ref)

        xv, gv = x_ref[...], g_ref[...]
        r = lax.rsqrt(jnp.mean(xv * xv, axis=-1, keepdims=True) + EPS)
        xn = xv * r
        err = xn * gv - t_ref[...]
        dy = err * (1.0 / D)
        dxn = dy * gv
        dx = r * (dxn - xn * jnp.mean(dxn * xn, axis=-1, keepdims=True))
        dx_ref[...] = dx
        dg_ref[...] += jnp.sum(dy * xn, axis=0, keepdims=True)
        loss_ref[...] += (0.5 / D) * jnp.sum(jnp.sum(err * err, axis=1, keepdims=True), axis=0, keepdims=True)
        _gate_bwd_tile(dx, o_ref, gate_ref, dob_ref, dgate_ref)

    return _pcall(body, name="loss_bwd",
                  out_shape=[_sds((T, D), f32), _sds((1, 1), f32), _sds((1, D), f32), _sds((T, D), bf16), _sds((1, D), f32)],
                  grid=(T // tm,),
                  in_specs=[_row_spec(tm, D), _row_spec(tm, D), _vec_spec(D), _row_spec(tm, D), _vec_spec(D)],
                  out_specs=[_row_spec(tm, D), pl.BlockSpec((1, 1), lambda i: (0, 0)), _vec_spec(D), _row_spec(tm, D),
                             _vec_spec(D)])(x, target, g, o, gate)


def _norm_bwd(x, dh, gin, g, scale, below=None):
    T, tm = x.shape[0], 512

    def body(x_ref, dh_ref, gin_ref, g_ref, sc_ref, *rest):
        if below is None:
            dx_ref, st_ref = rest
        else:
            o_ref, gate_ref, dx_ref, st_ref, dob_ref, dgate_ref = rest

        @pl.when(pl.program_id(0) == 0)
        def _():
            st_ref[...] = jnp.zeros_like(st_ref)
            if below is not None:
                dgate_ref[...] = jnp.zeros_like(dgate_ref)

        xv, gv, dhv = x_ref[...], g_ref[...], dh_ref[...]
        r = lax.rsqrt(jnp.mean(xv * xv, axis=-1, keepdims=True) + EPS)
        xn = xv * r
        da = dhv * (1.0 + sc_ref[...])
        dxn = da * gv
        dx = gin_ref[...] + r * (dxn - xn * jnp.mean(dxn * xn, axis=-1, keepdims=True))
        dx_ref[...] = dx
        st_ref[0:1, :] += jnp.sum(dhv, axis=0, keepdims=True)
        st_ref[1:2, :] += jnp.sum(dhv * (xn * gv), axis=0, keepdims=True)
        st_ref[2:3, :] += jnp.sum(da * xn, axis=0, keepdims=True)
        if below is not None:
            _gate_bwd_tile(dx, o_ref, gate_ref, dob_ref, dgate_ref)

    out_shape = [_sds((T, D), f32), _sds((8, D), f32)]
    in_specs = [_row_spec(tm, D), _row_spec(tm, D), _row_spec(tm, D), _vec_spec(D), _vec_spec(D)]
    out_specs = [_row_spec(tm, D), _vec_spec(D, 8)]
    args = [x, dh, gin, g, scale]
    if below is not None:
        out_shape += [_sds((T, D), bf16), _sds((1, D), f32)]
        in_specs += [_row_spec(tm, D), _vec_spec(D)]
        out_specs += [_row_spec(tm, D), _vec_spec(D)]
        args += list(below)
    return _pcall(body, name="norm_bwd", out_shape=out_shape, grid=(T // tm,), in_specs=in_specs,
                  out_specs=out_specs)(*args)


STEPS = 4
ADAMW_STEPS = 8


def _cast_place(place, ws, layer, after=None):
    n = len(ws)

    def body(place_ref, *refs):
        for t in range(n):
            refs[-n + t][...] = refs[t][...].astype(bf16)

    def tile(w):
        return w.shape[1] // STEPS, w.shape[2]

    extra = [] if after is None else [after]
    return _pcall(body, name="cast_place", out_shape=[_sds((4,) + w.shape[1:], bf16) for w in ws], grid=(STEPS,),
                  prefetch=1,
                  in_specs=[pl.BlockSpec((None,) + tile(w), lambda i, pr: (layer, i, 0)) for w in ws] + [ANY] * len(extra),
                  out_specs=[pl.BlockSpec((None,) + tile(w), lambda i, pr: (pr[0], i, 0)) for w in ws])(
                      place, *ws, *extra)


def _rs_add(place, grads, theirs):
    n = len(grads)

    def body(place_ref, *refs):
        for t in range(n):
            refs[2 * n + t][...] = (refs[t][...].astype(f32) + refs[n + t][...].astype(f32)).astype(bf16)

    mine = [pl.BlockSpec((None, None) + q.shape[1:], lambda s, pr: (s, pr[1], 0, 0)) for q in theirs]
    shard = [pl.BlockSpec((None,) + q.shape[1:], lambda s, pr: (s, 0, 0)) for q in theirs]
    return _pcall(body, name="rs_add", out_shape=[_sds(q.shape, bf16) for q in theirs], grid=(4,), prefetch=1,
                  in_specs=mine + shard, out_specs=shard)(place, *grads, *theirs)


def _rs_sum(place, pairs, slots):
    n, steps = len(pairs), 4

    def body(place_ref, *refs):
        for t in range(n):
            p_ref, q_ref = refs[t], refs[n + t]
            total = ((p_ref[...].astype(f32) + q_ref[0].astype(f32)) + q_ref[1].astype(f32)) + q_ref[2].astype(f32)
            refs[2 * n + t][...] = total.astype(bf16)

    def tile(q):
        return q.shape[1] // steps, q.shape[2]

    return _pcall(body, name="rs_sum", out_shape=[_sds((2,) + q.shape[1:], bf16) for q in slots], grid=(steps,),
                  prefetch=1,
                  in_specs=[pl.BlockSpec((None,) + tile(q), lambda i, pr: (pr[0], i, 0)) for q in slots]
                  + [pl.BlockSpec((3,) + tile(q), lambda i, pr: (0, i, 0)) for q in slots],
                  out_specs=[pl.BlockSpec((None,) + tile(q), lambda i, pr: (pr[1], i, 0)) for q in slots])(
                      place, *pairs, *slots)


def _adamw_math(w, g, m, v):
    m = ADAM_B1 * m + (1.0 - ADAM_B1) * g
    v = ADAM_B2 * v + (1.0 - ADAM_B2) * jnp.square(g)
    m_hat = m / (1.0 - ADAM_B1 ** ADAM_STEP)
    v_hat = v / (1.0 - ADAM_B2 ** ADAM_STEP)
    delta = -ADAM_LR * (m_hat / (jnp.sqrt(v_hat) + ADAM_EPS) + ADAM_WD * w)
    return delta, m, v


def _adamw_layer(layer, items):
    n = len(items)

    def body(*refs):
        outs = refs[-4 * n:]
        for t in range(n):
            w_ref, g_ref, m_ref, v_ref = refs[4 * t:4 * t + 4]
            g = g_ref[...].astype(f32)
            outs[4 * t][...] = g
            outs[4 * t + 1][...], outs[4 * t + 2][...], outs[4 * t + 3][...] = _adamw_math(
                w_ref[...], g, m_ref[...], v_ref[...])

    args, in_specs, out_specs, out_shape = [], [], [], []
    for w, g, m, v, _ in items:
        tr, cols = w.shape[1] // ADAMW_STEPS, w.shape[2]
        spec = pl.BlockSpec((None, tr, cols), lambda i: (layer, i, 0))
        args += [w, g, m, v]
        in_specs += [spec, pl.BlockSpec((tr, cols), lambda i: (i, 0)), spec, spec]
        out_specs += [spec] * 4
        out_shape += [_sds(w.shape, f32)] * 4
    aliases = {}
    for t, it in enumerate(items):
        if it[4] is not None:
            for k in range(4):
                aliases[len(args)] = 4 * t + k
                args.append(it[4][k])
                in_specs.append(ANY)
    res = _pcall(body, name="adamw", out_shape=out_shape, grid=(ADAMW_STEPS,), in_specs=in_specs, out_specs=out_specs,
                 aliases=aliases)(*args)
    return [tuple(res[4 * t:4 * t + 4]) for t in range(n)]


def _adamw_small(items):
    n = len(items)

    def body(*refs):
        ins, outs = refs[:4 * n], refs[4 * n:]
        for t in range(n):
            w_ref, g_ref, m_ref, v_ref = ins[4 * t:4 * t + 4]
            if len(g_ref.shape) == len(w_ref.shape) + 1:
                g = g_ref[0]
                for b in range(1, g_ref.shape[0]):
                    g = g + g_ref[b]
            else:
                g = g_ref[...]
            d, m, v = _adamw_math(w_ref[...], g, m_ref[...], v_ref[...])
            outs[4 * t][...], outs[4 * t + 1][...], outs[4 * t + 2][...], outs[4 * t + 3][...] = g, d, m, v

    out_shape = [_sds(w.shape, f32) for (w, _, _, _) in items for _ in range(4)]
    flat = [a for it in items for a in it]
    res = _pcall(body, name="adamw_small", out_shape=out_shape, in_specs=[VMEM] * (4 * n),
                 out_specs=[VMEM] * (4 * n))(*flat)
    return [tuple(res[4 * t:4 * t + 4]) for t in range(n)]


NN = ((1,), (0,))
NT = ((1,), (1,))
TN = ((0,), (0,))


def _mm(name, a, b, *, grid, a_spec, b_spec, out_shape, out_spec, dims, vmem_mb=None):
    def body(a_ref, b_ref, o_ref):
        r = lax.dot_general(a_ref[...], b_ref[...], (dims, ((), ())), preferred_element_type=f32)
        o_ref[...] = r.astype(o_ref.dtype)

    return _pcall(body, name=name, out_shape=out_shape, grid=grid, in_specs=[a_spec, b_spec], out_specs=out_spec,
                  vmem_mb=vmem_mb)(a, b)


def _whole(shape):
    return pl.BlockSpec(shape, lambda j: (0,) * len(shape))


def _split_spec(rows, tile, per_split):
    return pl.BlockSpec((None, rows, tile), lambda j: (j // per_split, 0, j % per_split))


class _Proj:
    def __init__(self, n, splits, tile):
        self.n, self.splits, self.tile = n, splits, tile
        self.steps = n // tile
        self.w_per = n // 4 // tile
        self.a_per = n // splits // tile
        assert self.w_per * tile * 4 == n and self.a_per * tile * splits == n

    def fwd(self, hb, wg):
        T = hb.shape[0]
        sub, tile, w_per = FWD_TILES, self.tile, self.w_per
        wide = sub * tile
        a_per = self.n // self.splits // wide
        assert a_per * wide * self.splits == self.n

        def w_tile(q):
            return pl.BlockSpec((None, D, tile), lambda j: ((sub * j + q) // w_per, 0, (sub * j + q) % w_per))

        def body(a_ref, *rest):
            w = jnp.concatenate([rest[q][...] for q in range(sub)], axis=1)
            rest[sub][...] = jnp.dot(a_ref[...], w, preferred_element_type=f32).astype(bf16)

        return _pcall(body, name="proj_fwd", out_shape=_sds((self.splits, T, self.n // self.splits), bf16),
                      grid=(self.n // wide,), in_specs=[_whole((T, D))] + [w_tile(q) for q in range(sub)],
                      out_specs=pl.BlockSpec((None, T, wide), lambda j: (j // a_per, 0, j % a_per)),
                      vmem_mb=40 if wide > 512 else None)(hb, *([wg] * sub))

    def dw(self, hb, dp):
        T = hb.shape[0]
        return _mm("proj_dw", hb, dp, grid=(self.steps,), a_spec=_whole((T, D)),
                   b_spec=_split_spec(T, self.tile, self.a_per), out_shape=_sds((4, D, self.n // 4), bf16),
                   out_spec=_split_spec(D, self.tile, self.w_per), dims=TN)

    def dh(self, dp, wg, after=None):
        T = dp.shape[1]
        extra = [] if after is None else [after]
        sub, tile, w_per = DH_WIDE // self.tile, self.tile, self.w_per
        a_per = self.n // self.splits // DH_WIDE
        assert sub * tile == DH_WIDE and a_per * DH_WIDE * self.splits == self.n

        def w_tile(q):
            return pl.BlockSpec((None, D, tile), lambda k: ((sub * k + q) // w_per, 0, (sub * k + q) % w_per))

        def body(a_ref, *rest):
            o_ref = rest[-1]
            w = jnp.concatenate([rest[q][...] for q in range(sub)], axis=1)
            r = lax.dot_general(a_ref[...], w, (NT, ((), ())), preferred_element_type=f32)

            @pl.when(pl.program_id(0) == 0)
            def _():
                o_ref[...] = r

            @pl.when(pl.program_id(0) > 0)
            def _():
                o_ref[...] += r

        return _pcall(body, name="proj_dh", out_shape=_sds((T, D), f32), grid=(self.n // DH_WIDE,),
                      in_specs=[pl.BlockSpec((None, T, DH_WIDE), lambda k: (k // a_per, 0, k % a_per))]
                      + [w_tile(q) for q in range(sub)] + [ANY] * len(extra),
                      out_specs=_whole((T, D)), vmem_mb=40)(dp, *([wg] * sub), *extra)


EVEN_PROJ = _Proj(7 * D, 7, 256)
ODD_PROJ = _Proj(4 * D, 2, 512)


def _out_bwd(dob, wo, y2):
    T = dob.shape[0]
    w_spec = pl.BlockSpec((None, 512, D), lambda j: (j, 0, 0))

    def body(dob_ref, w_ref, y_ref, dy_ref, dw_ref):
        dob_v = dob_ref[...]
        dy_ref[...] = lax.dot_general(dob_v, w_ref[...], (NT, ((), ())), preferred_element_type=f32).astype(bf16)
        dw_ref[...] = lax.dot_general(y_ref[...], dob_v, (TN, ((), ())), preferred_element_type=f32).astype(bf16)

    return _pcall(body, name="out_bwd", out_shape=[_sds((2, T, D), bf16), _sds((4, 512, D), bf16)], grid=(4,),
                  in_specs=[_whole((T, D)), w_spec, _split_spec(T, 512, 2)],
                  out_specs=[_split_spec(T, 512, 2), w_spec])(dob, wo, y2)


SUB_ROWS = 128
SUBS = CHUNK_ROWS // SUB_ROWS


def _several(step):
    def body(k, carry):
        for b in range(SUBS):
            carry = step(k * SUBS + b, carry)
        return carry
    return body


def _head_spec(lead, T):
    return pl.BlockSpec((lead, T, HEAD), lambda h: (0, 0, h))


def _head_vec(rows):
    return pl.BlockSpec((rows, HEAD), lambda h: (0, h))


_HEAD_MAT = pl.BlockSpec((None, HEAD, HEAD), lambda h: (h, 0, 0))


def _causal():
    return lax.broadcasted_iota(jnp.int32, (HEAD, HEAD), 0) >= lax.broadcasted_iota(jnp.int32, (HEAD, HEAD), 1)


def _layernorm_head(v):
    mu = jnp.mean(v, axis=-1, keepdims=True)
    d = v - mu
    rstd = lax.rsqrt(jnp.mean(d * d, axis=-1, keepdims=True) + EPS)
    return d * rstd, rstd


def _even_fwd(p7, conv_w, ln_g, ln_b, sgu_w, sgu_bias):
    T, C = p7.shape[1], CHUNK_ROWS

    def body(p_ref, cw_ref, lg_ref, lb_ref, w_ref, b_ref, y_ref):
        w0, w1, w2 = cw_ref[0:1, :], cw_ref[1:2, :], cw_ref[2:3, :]
        wm = jnp.where(_causal(), w_ref[...], 0.0).astype(bf16)
        bias, lg, lb = b_ref[...], lg_ref[...], lb_ref[...]

        def step(i, halo):
            rows = pl.ds(pl.multiple_of(i * C, C), C)
            ah, ab, ac, az, u, v, zb = (p_ref[k, rows, :].astype(f32) for k in range(7))
            tt = ac * ah
            ext = jnp.concatenate([halo, tt], axis=0)
            cv = w2 * tt + w1 * pltpu.roll(ext, 1, 0)[HALO_CONV:] + w0 * pltpu.roll(ext, 2, 0)[HALO_CONV:]
            y_ref[0, rows, :] = (ab * cv * _silu(az)).astype(bf16)
            vhat, _ = _layernorm_head(v)
            vn = (vhat * lg + lb).astype(bf16)
            mix = jnp.concatenate([jnp.dot(wm, vn[k * HEAD:(k + 1) * HEAD], preferred_element_type=f32) + bias
                                   for k in range(C // HEAD)], axis=0)
            y_ref[1, rows, :] = (u * mix * _silu(zb)).astype(bf16)
            return tt[C - HALO_CONV:]

        lax.fori_loop(0, T // C, step, jnp.zeros((HALO_CONV, HEAD), f32))

    return _pcall(body, name="even_fwd", out_shape=_sds((2, T, D), bf16), grid=(NH,),
                  in_specs=[_head_spec(7, T), _head_vec(3), _head_vec(1), _head_vec(1), _HEAD_MAT, _HEAD_MAT],
                  out_specs=_head_spec(2, T))(p7, conv_w, ln_g, ln_b, sgu_w, sgu_bias)


def _even_bwd(p7, dy2, conv_w, ln_g, ln_b, sgu_w, sgu_bias):
    T, C = p7.shape[1], SUB_ROWS
    n_chunks = T // C

    def body(p_ref, dy_ref, cw_ref, lg_ref, lb_ref, w_ref, b_ref,
             dp_ref, dcw_ref, dlg_ref, dlb_ref, dw_ref, dms_ref, dcv_s):
        w0, w1, w2 = cw_ref[0:1, :], cw_ref[1:2, :], cw_ref[2:3, :]
        tri = _causal()
        wm = jnp.where(tri, w_ref[...], 0.0).astype(bf16)
        bias, lg, lb = b_ref[...], lg_ref[...], lb_ref[...]
        dw_ref[...] = jnp.zeros_like(dw_ref)
        dms_ref[...] = jnp.zeros_like(dms_ref)

        def fwd_step(i, carry):
            halo, a0, a1, a2, alg, alb = carry
            rows = pl.ds(pl.multiple_of(i * C, C), C)
            ah, ab, ac, az = (p_ref[k, rows, :].astype(f32) for k in range(4))
            dya = dy_ref[0, rows, :].astype(f32)
            tt = ac * ah
            ext = jnp.concatenate([halo, tt], axis=0)
            t1, t2 = pltpu.roll(ext, 1, 0)[HALO_CONV:], pltpu.roll(ext, 2, 0)[HALO_CONV:]
            cv = w2 * tt + w1 * t1 + w0 * t2
            sa, dsa = _silu_and_grad(az)
            g1 = dya * sa
            dp_ref[1, rows, :] = (g1 * cv).astype(bf16)
            dp_ref[3, rows, :] = (dya * ab * cv * dsa).astype(bf16)
            dcv = g1 * ab
            dcv_s[rows, :] = dcv
            a2 = a2 + jnp.sum(dcv * tt, axis=0, keepdims=True)
            a1 = a1 + jnp.sum(dcv * t1, axis=0, keepdims=True)
            a0 = a0 + jnp.sum(dcv * t2, axis=0, keepdims=True)

            u, zb, dyb = p_ref[4, rows, :].astype(f32), p_ref[6, rows, :].astype(f32), dy_ref[1, rows, :].astype(f32)
            vhat, rstd = _layernorm_head(p_ref[5, rows, :].astype(f32))
            vn = (vhat * lg + lb).astype(bf16)
            sb, dsb = _silu_and_grad(zb)
            mix = jnp.concatenate([jnp.dot(wm, vn[k * HEAD:(k + 1) * HEAD], preferred_element_type=f32) + bias
                                   for k in range(C // HEAD)], axis=0)
            dp_ref[4, rows, :] = (dyb * mix * sb).astype(bf16)
            dp_ref[6, rows, :] = (dyb * u * mix * dsb).astype(bf16)
            dmix = dyb * u * sb
            dvn_parts = []
            for k in range(C // HEAD):
                dm = dmix[k * HEAD:(k + 1) * HEAD]
                dmb = dm.astype(bf16)
                dvn_parts.append(lax.dot_general(wm, dmb, (TN, ((), ())), preferred_element_type=f32))
                dw_ref[...] += lax.dot_general(dmb, vn[k * HEAD:(k + 1) * HEAD], (NT, ((), ())),
                                               preferred_element_type=f32)
                dms_ref[...] += dm
            dvn = jnp.concatenate(dvn_parts, axis=0)
            alg = alg + jnp.sum(dvn * vhat, axis=0, keepdims=True)
            alb = alb + jnp.sum(dvn, axis=0, keepdims=True)
            dvh = dvn * lg
            dv = rstd * (dvh - jnp.mean(dvh, axis=-1, keepdims=True)
                         - vhat * jnp.mean(dvh * vhat, axis=-1, keepdims=True))
            dp_ref[5, rows, :] = dv.astype(bf16)
            return tt[C - HALO_CONV:], a0, a1, a2, alg, alb

        zrow = jnp.zeros((1, HEAD), f32)
        _, a0, a1, a2, alg, alb = lax.fori_loop(
            0, n_chunks // SUBS, _several(fwd_step), (jnp.zeros((HALO_CONV, HEAD), f32), zrow, zrow, zrow, zrow, zrow))
        dcw_ref[0:1, :], dcw_ref[1:2, :], dcw_ref[2:3, :] = a0, a1, a2
        dlg_ref[...], dlb_ref[...] = alg, alb
        dw_ref[...] = jnp.where(tri, dw_ref[...], 0.0)

        def bwd_step(k, halo):
            rows = pl.ds(pl.multiple_of((n_chunks - 1 - k) * C, C), C)
            dcv = dcv_s[rows, :]
            ext = jnp.concatenate([dcv, halo], axis=0)
            n1 = pltpu.roll(ext, C + HALO_CONV - 1, 0)[:C]
            n2 = pltpu.roll(ext, C + HALO_CONV - 2, 0)[:C]
            dtt = w2 * dcv + w1 * n1 + w0 * n2
            dp_ref[2, rows, :] = (dtt * p_ref[0, rows, :].astype(f32)).astype(bf16)
            dp_ref[0, rows, :] = (dtt * p_ref[2, rows, :].astype(f32)).astype(bf16)
            return dcv[:HALO_CONV]

        lax.fori_loop(0, n_chunks // SUBS, _several(bwd_step), jnp.zeros((HALO_CONV, HEAD), f32))

    out_shape = [_sds((7, T, D), bf16), _sds((3, D), f32), _sds((1, D), f32), _sds((1, D), f32),
                 _sds((NH, HEAD, HEAD), f32), _sds((NH, HEAD, HEAD), f32)]
    return _pcall(body, name="even_bwd", out_shape=out_shape, grid=(NH,),
                  in_specs=[_head_spec(7, T), _head_spec(2, T), _head_vec(3), _head_vec(1), _head_vec(1),
                            _HEAD_MAT, _HEAD_MAT],
                  out_specs=[_head_spec(7, T), _head_vec(3), _head_vec(1), _head_vec(1), _HEAD_MAT, _HEAD_MAT],
                  scratch=[pltpu.VMEM((T, HEAD), f32)])(p7, dy2, conv_w, ln_g, ln_b, sgu_w, sgu_bias)


def _window_sum(ext, win, towards_past):
    n, k, s = ext.shape[0], 1, ext
    while k < win:
        s = s + pltpu.roll(s, k if towards_past else n - k, 0)
        k *= 2
    return s


def _pool_count(i, C, win):
    t = i * C + lax.broadcasted_iota(jnp.int32, (C, 1), 0)
    cnt = jnp.minimum(t + 1, win).astype(f32)
    return cnt, 1.0 / cnt


def _group_specs(T):
    p_spec = pl.BlockSpec((None, T, GC), lambda g: (0, 0, g))
    z_spec = pl.BlockSpec((None, T, GC), lambda g: (1, 0, g))
    pw_spec = pl.BlockSpec((4, GC // 4, GC), lambda g: (0, g, 0))
    ps_spec = pl.BlockSpec((1, GC), lambda g: (0, g))
    y_spec = pl.BlockSpec((None, T, GC), lambda g: (g // 2, 0, g % 2))
    return p_spec, z_spec, pw_spec, ps_spec, y_spec


def _odd_fwd(p2, pool_wg, pool_scale):
    T, C = p2.shape[1], CHUNK_ROWS
    p_spec, z_spec, pw_spec, ps_spec, y_spec = _group_specs(T)

    def body(p_ref, z_ref, pw_ref, ps_ref, y_ref):
        pw, ps = pw_ref[...].reshape(GC, GC), ps_ref[...]

        def run(win):
            def step(i, halo):
                rows = pl.ds(pl.multiple_of(i * C, C), C)
                p = p_ref[rows, :].astype(f32)
                s = _window_sum(jnp.concatenate([halo, p], axis=0), win, True)[HALO_POOL:]
                pooled = s * _pool_count(i, C, win)[1] - p
                ypre = jnp.dot(pooled.astype(bf16), pw, preferred_element_type=f32)
                y_ref[rows, :] = (ypre * ps * _silu(z_ref[rows, :].astype(f32))).astype(bf16)
                return p[C - HALO_POOL:]

            lax.fori_loop(0, T // C, step, jnp.zeros((HALO_POOL, GC), f32))

        for gi, win in enumerate(WINDOWS):
            pl.when(pl.program_id(0) == gi)(functools.partial(run, win))

    return _pcall(body, name="odd_fwd", out_shape=_sds((2, T, D), bf16), grid=(len(WINDOWS),),
                  in_specs=[p_spec, z_spec, pw_spec, ps_spec], out_specs=y_spec)(p2, p2, pool_wg, pool_scale)


def _odd_bwd(p2, dy2, pool_wg, pool_scale):
    T, C = p2.shape[1], CHUNK_ROWS
    n_chunks = T // C
    p_spec, z_spec, pw_spec, ps_spec, y_spec = _group_specs(T)

    def body(p_ref, z_ref, dy_ref, pw_ref, ps_ref, dp_ref, dpw_ref, dps_ref, q_s, acc_s):
        pw, ps = pw_ref[...].reshape(GC, GC), ps_ref[...]

        def run(win):
            acc_s[...] = jnp.zeros_like(acc_s)

            def fwd_step(i, carry):
                halo, aps = carry
                rows = pl.ds(pl.multiple_of(i * C, C), C)
                p, z, dy = p_ref[rows, :].astype(f32), z_ref[rows, :].astype(f32), dy_ref[rows, :].astype(f32)
                _, inv_cnt = _pool_count(i, C, win)
                s = _window_sum(jnp.concatenate([halo, p], axis=0), win, True)[HALO_POOL:]
                pb = (s * inv_cnt - p).astype(bf16)
                ypre = jnp.dot(pb, pw, preferred_element_type=f32)
                sz, dsz = _silu_and_grad(z)
                aps = aps + jnp.sum(dy * ypre * sz, axis=0, keepdims=True)
                dp_ref[1, rows, :] = (dy * ypre * ps * dsz).astype(bf16)
                dyp = (dy * ps * sz).astype(bf16)
                acc_s[...] += lax.dot_general(pb, dyp, (TN, ((), ())), preferred_element_type=f32)
                dpool = lax.dot_general(dyp, pw, (NT, ((), ())), preferred_element_type=f32)
                q_s[rows, :] = dpool * inv_cnt
                return p[C - HALO_POOL:], aps

            _, aps = lax.fori_loop(0, n_chunks, fwd_step, (jnp.zeros((HALO_POOL, GC), f32), jnp.zeros((1, GC), f32)))
            dps_ref[...] = aps
            dpw_ref[...] = acc_s[...].reshape(4, GC // 4, GC).astype(bf16)

            def bwd_step(k, halo):
                i = n_chunks - 1 - k
                rows = pl.ds(pl.multiple_of(i * C, C), C)
                q = q_s[rows, :]
                s = _window_sum(jnp.concatenate([q, halo], axis=0), win, False)[:C]
                dp_ref[0, rows, :] = (s - q * _pool_count(i, C, win)[0]).astype(bf16)
                return q[:HALO_POOL]

            lax.fori_loop(0, n_chunks, bwd_step, jnp.zeros((HALO_POOL, GC), f32))

        for gi, win in enumerate(WINDOWS):
            pl.when(pl.program_id(0) == gi)(functools.partial(run, win))

    out_shape = [_sds((2, T, 2 * D), bf16), _sds((4, GC, GC), bf16), _sds((1, 2 * D), f32)]
    return _pcall(body, name="odd_bwd", out_shape=out_shape, grid=(len(WINDOWS),),
                  in_specs=[p_spec, z_spec, y_spec, pw_spec, ps_spec],
                  out_specs=[pl.BlockSpec((2, T, GC), lambda g: (0, 0, g)), pw_spec, ps_spec],
                  scratch=[pltpu.VMEM((T, GC), f32), pltpu.VMEM((GC, GC), f32)], vmem_mb=44)(
                      p2, p2, dy2, pool_wg, pool_scale)


def _ada_fwd(c_all, ada_w):
    cols = ada_w.shape[2]

    def body(c_ref, w_ref, o_ref):
        o_ref[...] = jnp.dot(_silu(c_ref[...]), w_ref[...], preferred_element_type=f32,
                             precision=lax.Precision.HIGHEST)

    return _pcall(body, name="ada_fwd", out_shape=_sds((4, N_DEV, cols), f32), grid=(4,),
                  in_specs=[pl.BlockSpec((N_DEV, D), lambda i: (0, 0)), pl.BlockSpec((None, D, cols), lambda i: (i, 0, 0))],
                  out_specs=pl.BlockSpec((None, N_DEV, cols), lambda i: (i, 0, 0)))(c_all, ada_w)


def _ada_bwd(c_all_t, dmod, w, m, v):
    cols, tr = w.shape[2], 256
    spec = pl.BlockSpec((None, tr, cols), lambda l, i: (l, i, 0))

    def body(c_ref, dm_ref, w_ref, m_ref, v_ref, g_ref, d_ref, mo_ref, vo_ref):
        sc = _silu(c_ref[...])
        g = sc[:, 0:1] * dm_ref[0:1, :]
        for b in range(1, N_DEV):
            g = g + sc[:, b:b + 1] * dm_ref[b:b + 1, :]
        g_ref[...] = g
        d_ref[...], mo_ref[...], vo_ref[...] = _adamw_math(w_ref[...], g, m_ref[...], v_ref[...])

    return _pcall(body, name="ada_bwd", out_shape=[_sds(w.shape, f32)] * 4, grid=(4, D // tr),
                  in_specs=[pl.BlockSpec((tr, N_DEV), lambda l, i: (i, 0)),
                            pl.BlockSpec((None, N_DEV, cols), lambda l, i: (l, 0, 0)), spec, spec, spec],
                  out_specs=[spec] * 4)(c_all_t, dmod, w, m, v)


def _layer_fwd(even, x, hb, gate, w, nxt, before_out=None, after_proj=None):
    if even:
        w_in, w_out, conv_w, ln_g, ln_b, sgu_w, sgu_b = w
        bias = jnp.broadcast_to(sgu_b[:, :, None], (NH, HEAD, HEAD))
        p = EVEN_PROJ.fwd(hb, w_in)
        if after_proj is not None:
            conv_w = conv_w + after_proj(p)[0:1, 0:1]
        y2 = _even_fwd(p, conv_w, ln_g, ln_b, sgu_w, bias)
    else:
        w_in, pool_w, w_out, pool_scale = w
        p = ODD_PROJ.fwd(hb, w_in)
        y2 = _odd_fwd(p, pool_w, pool_scale if after_proj is None else pool_scale + after_proj(p)[0:1, 0:1])
    if before_out is not None:
        late_w_out, tok = before_out(y2)
        if late_w_out is not None:
            w_out = late_w_out
            w = (w_in, w_out) + tuple(w[2:]) if even else (w_in, pool_w, w_out, pool_scale)
        if tok is not None:
            gate = gate + tok[0:1, 0:1]
    outs = _out_proj(y2, w_out.reshape(2, D, D), x, gate, nxt)
    return outs[0], (None if nxt is None else outs[2]), (x, hb, p, y2, outs[1]), w


def _layer_bwd(even, gin, dob, dgate, saved, scale, g, w, below=None, send=None):
    x_in, hb, p, y2, o = saved
    if even:
        w_in, w_out, conv_w, ln_g, ln_b, sgu_w, sgu_b = w
        bias = jnp.broadcast_to(sgu_b[:, :, None], (NH, HEAD, HEAD))
        dy2, dwo = _out_bwd(dob, w_out, y2)
        dp, dconv, dlg, dlb, dsw, dms = _even_bwd(p, dy2, conv_w, ln_g, ln_b, sgu_w, bias)
        proj = EVEN_PROJ
        small = dict(conv_w=dconv, ln_g=dlg, ln_b=dlb, sgu_w=dsw, sgu_b=jnp.sum(dms, axis=-1))
        big = [proj.dw(hb, dp), dwo]
    else:
        w_in, pool_w, w_out, pool_scale = w
        dy2, dwo = _out_bwd(dob, w_out, y2)
        dp, dpw, dps = _odd_bwd(p, dy2, pool_w, pool_scale)
        proj = ODD_PROJ
        small = dict(pool_scale=dps)
        big = [proj.dw(hb, dp), dpw, dwo]
    tok = None
    if send is not None:
        big, tok = send(big)
    dh = proj.dh(dp, w_in, tok)
    res = _norm_bwd(x_in, dh, gin, g, scale, below)
    stats = res[1]
    return (res[0], (None if below is None else (res[2], res[3])), big, small,
            jnp.concatenate([stats[0:2], dgate], axis=0), stats[2:3])


def _pack_rows(parts):
    rows = [p.reshape(-1, LANES) for p in parts]
    total = sum(r.shape[0] for r in rows)
    padded = -(-total // (8 * N_DEV)) * (8 * N_DEV)
    if padded > total:
        rows.append(jnp.zeros((padded - total, LANES), f32))
    return jnp.concatenate(rows, axis=0)


def _unpack_rows(buf, shapes):
    out, r = [], 0
    for shp in shapes:
        n = 1
        for d in shp:
            n *= d
        out.append(buf[r:r + n // LANES].reshape(shp))
        r += n // LANES
    return out


def kernel(x, c, norm_g, ada_w, ada_b, ab_w_in, ab_conv_w, ab_ln_g, ab_ln_b, ab_sgu_w, ab_sgu_b, ab_w_out, c_w_in, c_pool_w, c_pool_scale, c_w_out, final_g, loss_target, m_norm_g, m_ada_w, m_ada_b, m_ab_w_in, m_ab_conv_w, m_ab_ln_g, m_ab_ln_b, m_ab_sgu_w, m_ab_sgu_b, m_ab_w_out, m_c_w_in, m_c_pool_w, m_c_pool_scale, m_c_w_out, m_final_g, v_norm_g, v_ada_w, v_ada_b, v_ab_w_in, v_ab_conv_w, v_ab_ln_g, v_ab_ln_b, v_ab_sgu_w, v_ab_sgu_b, v_ab_w_out, v_c_w_in, v_c_pool_w, v_c_pool_scale, v_c_w_out, v_final_g):
    ix, iy, ic = _place()
    chip, dev = 2 * ix + iy, 4 * ix + 2 * iy + ic
    n_even, n_odd = ab_w_in.shape[0], c_w_in.shape[0]
    depth = n_even + n_odd
    acols = ada_w.shape[2]

    place = jnp.stack([chip, ic]).astype(jnp.int32)
    even_names, odd_names = ["ab_w_in", "ab_w_out"], ["c_w_in", "c_pool_w", "c_w_out"]
    params = {"ab_w_in": (ab_w_in, m_ab_w_in, v_ab_w_in), "ab_w_out": (ab_w_out, m_ab_w_out, v_ab_w_out),
              "c_w_in": (c_w_in, m_c_w_in, v_c_w_in), "c_w_out": (c_w_out, m_c_w_out, v_c_w_out),
              "c_pool_w": tuple(a.reshape(n_odd, GC, GC) for a in (c_pool_w, m_c_pool_w, v_c_pool_w))}

    def placed(names, layer, after=None):
        ws = [params[nm][0] for nm in names]
        return [p.reshape(4, 2, p.shape[1] // 2, p.shape[2]) for p in _cast_place(place, ws, layer, after)]

    def whole(arrays):
        return [g.reshape(4, 2 * g.shape[2], g.shape[3]) for g in arrays]

    first = _gather8(jnp.concatenate([c, ab_conv_w.reshape(1, -1), c_pool_scale.reshape(1, -1)], axis=1), "gather_c")
    c_all, small_all = first[:, 0, :D], first[0::2, 0, D:]
    sems_a, in_a, tok = _ag_start([placed(even_names[:1], 0)], first[0:1, 0, 0:LANES], "ag_start_0a")
    modp = _ada_fwd(c_all, ada_w)
    later = [placed(even_names[1:], 0, tok)]
    later += [placed(even_names if i % 2 == 0 else odd_names, i // 2, tok) for i in range(1, depth)]
    modg = _gather8(modp + tok[0:1, 0:1], "gather_mod", [lay[-1] for lay in later])
    mod_rows = lax.dynamic_index_in_dim(modg[0::2], dev, axis=2, keepdims=False)
    mod = jnp.transpose(mod_rows, (1, 0, 2)).reshape(depth, 3 * D) + ada_b
    mods = [(mod[i:i + 1, 0:D], mod[i:i + 1, D:2 * D], mod[i:i + 1, 2 * D:3 * D]) for i in range(depth)]

    def shard_cols(a, width):
        return lax.dynamic_slice_in_dim(a, chip * width, width, axis=a.ndim - 1)

    n_conv = ab_conv_w.size
    conv_all = small_all[:, :n_conv].reshape(4, n_even, 3, D // 4)
    conv_full = jnp.transpose(conv_all, (1, 2, 0, 3)).reshape(n_even, 3, D)
    scale_all = small_all[:, n_conv:].reshape(4, n_odd, 2 * D // 4)
    scale_full = jnp.transpose(scale_all, (1, 0, 2)).reshape(n_odd, 2 * D)

    gathers_done = mod[0:1, 0:LANES] + scale_full[0:1, 0:LANES]
    sems_b, in_b, tok = _ag_start(later[:1], gathers_done, "ag_start_0b")
    sems_r, in_r, tok = _ag_start(later[1:], tok, "ag_start_rest")

    x_cur, saved, weights, first_part, last_part = x[0], [], [], {}, {}

    def hand_off(arrays, sems, after, first, tag):
        arrived = _ag_wait(arrays, sems, after, f"ag_wait_{tag}", first)
        sems_f, inflight, tok = _agf_start(arrived, f"agf_start_{tag}")
        return (sems_f, inflight), tok

    first_part[0], tok = hand_off(in_a[0], sems_a[0], tok, 0, "0a")
    hb = _hnorm(x_cur, norm_g[0:1], mods[0][0] + tok[0:1, 0:1], mods[0][1])
    for i in range(depth):
        j = i // 2
        if i not in first_part:
            first_part[i], _ = hand_off(in_r[i - 1][:-1], sems_r[i - 1], x_cur, 0, f"{i}a")
        full = whole(_agf_wait(*first_part.pop(i), hb if i == 0 else x_cur, f"agf_wait_{i}a")) + [None]
        if i % 2 == 0:
            w = (full[0], None, conv_full[j], ab_ln_g[j:j + 1], ab_ln_b[j:j + 1], ab_sgu_w[j], ab_sgu_b[j])
        else:
            w = (full[0], full[1], None, scale_full[j:j + 1])

        def after_proj(p, i=i):
            arrays, sems = (in_b[0], sems_b[0]) if i == 0 else (in_r[i - 1][-1:], sems_r[i - 1])
            last_part[i], tok = hand_off(arrays, sems, p, 0 if i == 0 else len(in_r[i - 1]) - 1, f"{i}b")
            return tok

        def before_out(y2, i=i):
            w_out, tok = whole(_agf_wait(*last_part.pop(i), y2, f"agf_wait_{i}b"))[0], None
            if i + 1 < depth and i + 1 != LATE_LAYER:
                first_part[i + 1], tok = hand_off(in_r[i][:-1], sems_r[i], y2, 0, f"{i + 1}a")
            return w_out, tok

        nxt = (norm_g[i + 1:i + 2], mods[i + 1][0], mods[i + 1][1]) if i + 1 < depth else None
        x_cur, hb, sv, w = _layer_fwd(i % 2 == 0, x_cur, hb, mods[i][2], w, nxt, before_out, after_proj)
        weights.append(w)
        saved.append(sv)
    gin, loss, dfinal_g, dob, dgate = _loss_bwd(x_cur, loss_target[0], final_g.reshape(1, D), saved[-1][4],
                                                mods[-1][2])

    stacked = {}

    def reduce_layer(i, sems, pairs, lands, after):
        pairs, slots = _rs_chip_wait(sems, pairs, lands, after, f"rs_chip_wait_{i}")
        half_sems, halves, _ = _rs_half_start(_rs_sum(place, pairs, slots), f"rs_half_start_{i}")
        return i, half_sems, halves

    def update_layer(i, half_sems, halves, after):
        names = even_names if i % 2 == 0 else odd_names
        grads = _rs_half_wait(half_sems, halves, after, f"rs_half_wait_{i}")
        items = [(params[nm][0], g.reshape(params[nm][0].shape[1:]), params[nm][1], params[nm][2], stacked.get(nm))
                 for nm, g in zip(names, grads)]
        for nm, res in zip(names, _adamw_layer(i // 2, items)):
            stacked[nm] = res
            updated.append(res[1])

    updated = []
    small_g, dmod, dnorm_g, pending, tok = [None] * depth, [None] * depth, [None] * depth, None, None
    exchanging = []
    for i in reversed(range(depth)):
        w = weights[i]
        if tok is not None:
            w = w[:2] + (w[2] + tok[0:1, 0:1],) + w[3:] if i % 2 == 0 else w[:3] + (w[3] + tok[0:1, 0:1],)
        below = (saved[i - 1][4], mods[i - 1][2]) if i > 0 else None

        def send(big_g, i=i):
            if exchanging:
                update_layer(*exchanging.pop(), big_g[0])
            big_g = [g.reshape(4, 2, g.shape[1] // 2, g.shape[2]) for g in big_g]
            sems, big_g, lands, tok = _rs_pair_start(big_g, f"rs_pair_start_{i}")
            return (sems, big_g, lands), tok

        gin, gate_bwd, sent, small_g[i], dmod[i], dnorm_g[i] = _layer_bwd(
            i % 2 == 0, gin, dob, dgate, saved[i], mods[i][1], norm_g[i:i + 1], w, below, send)
        if below is not None:
            dob, dgate = gate_bwd
        after = gin
        if i == 0:
            dmod_all = _gather8(jnp.stack(dmod).reshape(depth * 3 * D // LANES, LANES), "gather_dmod")
            after = dmod_all = dmod_all.reshape(N_DEV, depth, 3 * D)
        if i > 0:
            after, updated = [after] + updated, []
        else:
            after = [after]
        big_g, theirs = _rs_pair_wait(*sent, after, f"rs_pair_wait_{i}")
        pairs = _rs_add(place, big_g, theirs)
        sems, pairs, lands, tok = _rs_chip_start(pairs, f"rs_chip_start_{i}")
        if pending is not None:
            exchanging.append(reduce_layer(*pending, [tok]))
        pending = (i, sems, pairs, lands)
    grad_x = gin
    dnorm_g = jnp.concatenate(dnorm_g, axis=0)

    dmod_cols = jnp.transpose(shard_cols(dmod_all, acols), (1, 0, 2))
    r_ada_w = _ada_bwd(c_all.T, dmod_cols, ada_w, m_ada_w, v_ada_w)
    update_layer(*exchanging.pop(), r_ada_w[1])
    last = reduce_layer(*pending, [r_ada_w[1]] + updated)

    small_parts = [dnorm_g, dfinal_g,
                   jnp.stack([small_g[2 * j]["conv_w"] for j in range(n_even)]),
                   jnp.concatenate([small_g[2 * j]["ln_g"] for j in range(n_even)], axis=0),
                   jnp.concatenate([small_g[2 * j]["ln_b"] for j in range(n_even)], axis=0),
                   jnp.stack([small_g[2 * j]["sgu_b"] for j in range(n_even)]),
                   jnp.concatenate([small_g[2 * j + 1]["pool_scale"] for j in range(n_odd)], axis=0),
                   jnp.pad(loss, ((0, 7), (0, LANES - 1)))]
    small_shapes = [p.shape for p in small_parts]
    sgu_parts = [small_g[2 * j]["sgu_w"].reshape(NH * HEAD, HEAD).astype(bf16) for j in range(n_even)]
    reduced = _allreduce8([_pack_rows(small_parts)] + sgu_parts, "allreduce_small", last[2][0])
    update_layer(*last, reduced[0])
    r_ab_w_in, r_ab_w_out, r_c_w_in, r_c_w_out = (stacked[nm] for nm in ("ab_w_in", "ab_w_out", "c_w_in", "c_w_out"))
    r_c_pool_w = tuple(a.reshape(c_pool_w.shape) for a in stacked["c_pool_w"])
    g_norm_g, g_final_g, g_conv_full, g_ln_g, g_ln_b, g_sgu_b, g_scale_full, loss_row = _unpack_rows(reduced[0],
                                                                                                     small_shapes)
    g_sgu_w = jnp.stack(reduced[1:]).astype(f32)
    loss = loss_row[0, 0]
    g_conv = shard_cols(g_conv_full, D // 4)
    g_scale = shard_cols(g_scale_full, 2 * D // 4)

    def two_d(a):
        return a.reshape(-1, a.shape[-1])

    small = [(norm_g, g_norm_g, m_norm_g, v_norm_g),
             (ada_b, dmod_all, m_ada_b, v_ada_b),
             (two_d(ab_conv_w), two_d(g_conv), two_d(m_ab_conv_w), two_d(v_ab_conv_w)),
             (ab_ln_g, g_ln_g, m_ab_ln_g, v_ab_ln_g),
             (ab_ln_b, g_ln_b, m_ab_ln_b, v_ab_ln_b),
             (two_d(ab_sgu_w), two_d(g_sgu_w), two_d(m_ab_sgu_w), two_d(v_ab_sgu_w)),
             (two_d(ab_sgu_b), two_d(g_sgu_b), two_d(m_ab_sgu_b), two_d(v_ab_sgu_b)),
             (c_pool_scale, g_scale, m_c_pool_scale, v_c_pool_scale),
             (final_g.reshape(1, D), g_final_g, m_final_g.reshape(1, D), v_final_g.reshape(1, D))]
    small_res = _adamw_small(small)
    small_shapes_out = [norm_g.shape, ada_b.shape, ab_conv_w.shape, ab_ln_g.shape, ab_ln_b.shape, ab_sgu_w.shape,
                        ab_sgu_b.shape, c_pool_scale.shape, final_g.shape]
    (r_norm_g, r_ada_b, r_conv, r_ln_g, r_ln_b, r_sgu_w, r_sgu_b, r_scale, r_final_g) = [
        tuple(a.reshape(shp) for a in res) for res, shp in zip(small_res, small_shapes_out)]

    order = [r_norm_g, r_ada_w, r_ada_b, r_ab_w_in, r_conv, r_ln_g, r_ln_b, r_sgu_w, r_sgu_b, r_ab_w_out,
             r_c_w_in, r_c_pool_w, r_scale, r_c_w_out, r_final_g]
    outs = [loss, grad_x[None]]
    for field in range(4):
        outs += [r[field] for r in order]
    return tuple(outs)
```
